```python
import jax, jax.numpy as jnp
from jax import lax
import numpy as np

D_MODEL = 1024
BATCH = 8
SEQ = 4096
DEPTH = 1

MEM_TOKENS = 256
EPS = 1e-6
GLA_HEADS = 4
GLA_DK = 64
GLA_DV = 128
GLA_GATE_RANK = 16
GLA_GATE_NORM = 16.0
GLA_CHUNK = 64
MLA_HEADS = 8
MLA_Q_RANK = 256
MLA_KV_RANK = 128
MLA_NOPE = 64
MLA_ROPE = 32
MLA_QK = MLA_NOPE + MLA_ROPE
MLA_V = 64
ROPE_THETA = 10000.0
Q_BLOCK = 128
XA_HEADS = 4
XA_HEAD_DIM = 128
D_FF = 2816
CONV_W = 3

GLA_QK_W = GLA_HEADS * GLA_DK
GLA_V_W = GLA_HEADS * GLA_DV
MLA_OUT_W = MLA_HEADS * MLA_V
MIX_WIDTH = GLA_V_W + MLA_OUT_W
IN_SPLITS = (GLA_QK_W, GLA_QK_W, GLA_V_W, GLA_GATE_RANK, GLA_V_W, MLA_Q_RANK, MLA_KV_RANK, MLA_ROPE)
IN_WIDTH = int(sum(IN_SPLITS))
IN_SPLIT_POINTS = tuple(int(p) for p in np.cumsum(IN_SPLITS)[:-1])

kernel_name = "hymba_gla_mla_memxattn_convffn"


def rms_norm(x, g):
    xf = x.astype(jnp.float32)
    y = xf * lax.rsqrt(jnp.mean(xf * xf, axis=-1, keepdims=True) + EPS)
    return (y * g.astype(jnp.float32)).astype(x.dtype)


def to_heads(t, n_heads):
    b, s, _ = t.shape
    return t.reshape(b, s, n_heads, -1).transpose(0, 2, 1, 3)


def from_heads(t):
    b, h, s, d = t.shape
    return t.transpose(0, 2, 1, 3).reshape(b, s, h * d)


def rope(x, pos):
    half = x.shape[-1] // 2
    inv = ROPE_THETA ** (-jnp.arange(half, dtype=jnp.float32) / half)
    ang = pos.astype(jnp.float32)[:, None, :, None] * inv
    cos, sin = jnp.cos(ang), jnp.sin(ang)
    xf = x.astype(jnp.float32)
    x1, x2 = xf[..., :half], xf[..., half:]
    return jnp.concatenate([x1 * cos - x2 * sin, x2 * cos + x1 * sin], axis=-1).astype(x.dtype)


def gla_chunked(q, k, v, log_a):
    b, h, s, dk = q.shape
    dv = v.shape[-1]
    c = GLA_CHUNK
    n = s // c
    rs = lambda t: t.reshape(b, h, n, c, t.shape[-1]).astype(jnp.float32)
    qf, kf, vf, la = rs(q) * (GLA_DK ** -0.5), rs(k), rs(v), rs(log_a)
    cum = jnp.cumsum(la, axis=3)
    cum_last = cum[:, :, :, -1:]
    q_dec = qf * jnp.exp(cum)
    k_inv = kf * jnp.exp(-cum)
    k_end = kf * jnp.exp(cum_last - cum)
    att = jnp.einsum('bhnik,bhnjk->bhnij', q_dec, k_inv)
    causal = jnp.tril(jnp.ones((c, c), dtype=bool))
    att = jnp.where(causal, att, 0.0)
    o_intra = jnp.einsum('bhnij,bhnjv->bhniv', att, vf)
    d_state = jnp.einsum('bhnjk,bhnjv->nbhkv', k_end, vf)
    decay = jnp.exp(cum_last[:, :, :, 0]).transpose(2, 0, 1, 3)

    def step(state, inp):
        d, ds = inp
        return d[..., None] * state + ds, state

    _, s_prev = lax.scan(step, jnp.zeros((b, h, dk, dv), jnp.float32), (decay, d_state))
    o_inter = jnp.einsum('bhnik,nbhkv->bhniv', q_dec, s_prev)
    return (o_intra + o_inter).reshape(b, h, s, dv)


def causal_block_attention(q, k, v):
    b, h, s, d = q.shape
    dv = v.shape[-1]
    nb = s // Q_BLOCK
    scale = d ** -0.5
    qb = q.reshape(b, h, nb, Q_BLOCK, d).transpose(2, 0, 1, 3, 4)
    starts = jnp.arange(nb, dtype=jnp.int32) * Q_BLOCK
    key_idx = jnp.arange(s, dtype=jnp.int32)
    kf = k.astype(jnp.float32)
    vf = v.astype(jnp.float32)

    def one_block(args):
        qi, s0 = args
        sc = jnp.einsum('bhqd,bhkd->bhqk', qi.astype(jnp.float32), kf) * scale
        q_idx = s0 + jnp.arange(Q_BLOCK, dtype=jnp.int32)
        sc = jnp.where(key_idx[None, :] <= q_idx[:, None], sc, -jnp.inf)
        p = jax.nn.softmax(sc, axis=-1)
        return jnp.einsum('bhqk,bhkd->bhqd', p, vf)

    out = lax.map(one_block, (qb, starts))
    return out.transpose(1, 2, 0, 3, 4).reshape(b, h, s, dv).astype(v.dtype)


def _fwd_setup_inputs(seed: int = 0) -> dict:
    key = jax.random.key(seed)
    ks = jax.random.split(key, 32)
    L, D, F = DEPTH, D_MODEL, D_FF

    def w(k, shape, fan_in):
        return jax.random.normal(k, shape, jnp.float32) * (fan_in ** -0.5)

    def gain(k, shape):
        return 1.0 + 0.02 * jax.random.normal(k, shape, jnp.float32)

    x = jax.random.normal(ks[0], (BATCH, SEQ, D), jnp.float32)
    mem = jax.random.normal(ks[1], (BATCH, MEM_TOKENS, D), jnp.float32)
    positions = (jnp.arange(SEQ, dtype=jnp.int32)[None, :]
                 + jax.random.randint(ks[2], (BATCH, 1), 0, 1024, dtype=jnp.int32))
    return {
        "x": x,
        "mem": mem,
        "positions": positions,
        "norm_mix": gain(ks[3], (L, D)),
        "w_in": w(ks[4], (L, D, IN_WIDTH), D),
        "gla_gate_w2": w(ks[5], (L, GLA_GATE_RANK, GLA_QK_W), GLA_GATE_RANK),
        "gla_gate_b": 0.1 * jax.random.normal(ks[6], (L, GLA_QK_W), jnp.float32),
        "gla_out_norm": gain(ks[7], (L, GLA_DV)),
        "mla_q_a_norm": gain(ks[8], (L, MLA_Q_RANK)),
        "mla_w_uq": w(ks[9], (L, MLA_Q_RANK, MLA_HEADS * MLA_QK), MLA_Q_RANK),
        "mla_kv_a_norm": gain(ks[10], (L, MLA_KV_RANK)),
        "mla_w_ukv": w(ks[11], (L, MLA_KV_RANK, MLA_HEADS * (MLA_NOPE + MLA_V)), MLA_KV_RANK),
        "mla_q_norm": gain(ks[12], (L, MLA_QK)),
        "mla_k_norm": gain(ks[13], (L, MLA_QK)),
        "w_out": w(ks[14], (L, MIX_WIDTH, D), MIX_WIDTH),
        "norm_xa": gain(ks[15], (L, D)),
        "norm_mem": gain(ks[16], (L, D)),
        "xa_w_q": w(ks[17], (L, D, XA_HEADS * XA_HEAD_DIM), D),
        "xa_w_kv": w(ks[18], (L, D, 2 * XA_HEADS * XA_HEAD_DIM), D),
        "xa_q_norm": gain(ks[19], (L, XA_HEAD_DIM)),
        "xa_k_norm": gain(ks[20], (L, XA_HEAD_DIM)),
        "xa_w_o": w(ks[21], (L, XA_HEADS * XA_HEAD_DIM, D), XA_HEADS * XA_HEAD_DIM),
        "norm_ffn": gain(ks[22], (L, D)),
        "ffn_w_gate": w(ks[23], (L, D, F), D),
        "ffn_w_up": w(ks[24], (L, D, F), D),
        "ffn_conv_w": w(ks[25], (L, CONV_W, F), CONV_W),
        "ffn_conv_b": 0.02 * jax.random.normal(ks[26], (L, F), jnp.float32),
        "ffn_w_down": w(ks[27], (L, F, D), F),
    }


def _fwd_reference(x, mem, positions, norm_mix, w_in, gla_gate_w2, gla_gate_b, gla_out_norm,
              mla_q_a_norm, mla_w_uq, mla_kv_a_norm, mla_w_ukv, mla_q_norm, mla_k_norm, w_out,
              norm_xa, norm_mem, xa_w_q, xa_w_kv, xa_q_norm, xa_k_norm, xa_w_o,
              norm_ffn, ffn_w_gate, ffn_w_up, ffn_conv_w, ffn_conv_b, ffn_w_down):
    b, s, _ = x.shape
    h = x
    for l in range(DEPTH):
        xn = rms_norm(h, norm_mix[l])
        proj = xn @ w_in[l]
        g_q, g_k, g_v, g_alr, g_og, c_q, c_kv, k_pe = jnp.split(proj, IN_SPLIT_POINTS, axis=-1)

        gate_logit = (g_alr @ gla_gate_w2[l] + gla_gate_b[l]).astype(jnp.float32)
        log_a = jax.nn.log_sigmoid(gate_logit) / GLA_GATE_NORM
        o_gla = gla_chunked(to_heads(g_q, GLA_HEADS), to_heads(g_k, GLA_HEADS),
                            to_heads(g_v, GLA_HEADS), to_heads(log_a, GLA_HEADS))
        o_gla = rms_norm(o_gla, gla_out_norm[l]).astype(h.dtype)
        o_gla = from_heads(o_gla) * jax.nn.silu(g_og)

        q = to_heads(rms_norm(c_q, mla_q_a_norm[l]) @ mla_w_uq[l], MLA_HEADS)
        kv = to_heads(rms_norm(c_kv, mla_kv_a_norm[l]) @ mla_w_ukv[l], MLA_HEADS)
        k_nope, v = kv[..., :MLA_NOPE], kv[..., MLA_NOPE:]
        k_rot = jnp.broadcast_to(k_pe[:, None], (b, MLA_HEADS, s, MLA_ROPE))
        k = jnp.concatenate([k_nope, k_rot], axis=-1)
        q = rms_norm(q, mla_q_norm[l])
        k = rms_norm(k, mla_k_norm[l])
        q = jnp.concatenate([q[..., :MLA_NOPE], rope(q[..., MLA_NOPE:], positions)], axis=-1)
        k = jnp.concatenate([k[..., :MLA_NOPE], rope(k[..., MLA_NOPE:], positions)], axis=-1)
        o_mla = from_heads(causal_block_attention(q, k, v))

        h = h + jnp.concatenate([o_gla, o_mla], axis=-1) @ w_out[l]

        hn = rms_norm(h, norm_xa[l])
        mn = rms_norm(mem, norm_mem[l])
        xq = rms_norm((hn @ xa_w_q[l]).reshape(b, s, XA_HEADS, XA_HEAD_DIM), xa_q_norm[l])
        xkv = (mn @ xa_w_kv[l]).reshape(b, mem.shape[1], 2, XA_HEADS, XA_HEAD_DIM)
        xk = rms_norm(xkv[:, :, 0], xa_k_norm[l])
        xv = xkv[:, :, 1]
        sc = jnp.einsum('bqhd,bkhd->bhqk', xq.astype(jnp.float32), xk.astype(jnp.float32)) * (XA_HEAD_DIM ** -0.5)
        p = jax.nn.softmax(sc, axis=-1)
        xo = jnp.einsum('bhqk,bkhd->bqhd', p, xv.astype(jnp.float32)).astype(h.dtype)
        h = h + xo.reshape(b, s, XA_HEADS * XA_HEAD_DIM) @ xa_w_o[l]

        fn = rms_norm(h, norm_ffn[l])
        g = fn @ ffn_w_gate[l]
        g_pad = jnp.pad(g, ((0, 0), (CONV_W - 1, 0), (0, 0)))
        cw = ffn_conv_w[l]
        g_conv = ffn_conv_b[l] + sum(cw[i] * g_pad[:, i:i + s] for i in range(CONV_W))
        h = h + (jax.nn.silu(g_conv) * (fn @ ffn_w_up[l])) @ ffn_w_down[l]
    return h


import jax as _jax
import jax.numpy as _jnp

TWIN_FORMAT = 'train_step'
FWD_PARAMS = ['x', 'mem', 'positions', 'norm_mix', 'w_in', 'gla_gate_w2', 'gla_gate_b', 'gla_out_norm', 'mla_q_a_norm', 'mla_w_uq', 'mla_kv_a_norm', 'mla_w_ukv', 'mla_q_norm', 'mla_k_norm', 'w_out', 'norm_xa', 'norm_mem', 'xa_w_q', 'xa_w_kv', 'xa_q_norm', 'xa_k_norm', 'xa_w_o', 'norm_ffn', 'ffn_w_gate', 'ffn_w_up', 'ffn_conv_w', 'ffn_conv_b', 'ffn_w_down']
TWIN_WEIGHTS = ['norm_mix', 'w_in', 'gla_gate_w2', 'gla_gate_b', 'gla_out_norm', 'mla_q_a_norm', 'mla_w_uq', 'mla_kv_a_norm', 'mla_w_ukv', 'mla_q_norm', 'mla_k_norm', 'w_out', 'norm_xa', 'norm_mem', 'xa_w_q', 'xa_w_kv', 'xa_q_norm', 'xa_k_norm', 'xa_w_o', 'norm_ffn', 'ffn_w_gate', 'ffn_w_up', 'ffn_conv_w', 'ffn_conv_b', 'ffn_w_down']
TWIN_DIFF_INPUT = 'x'
TWIN_INPUTS = ['x', 'mem', 'positions', 'norm_mix', 'w_in', 'gla_gate_w2', 'gla_gate_b', 'gla_out_norm', 'mla_q_a_norm', 'mla_w_uq', 'mla_kv_a_norm', 'mla_w_ukv', 'mla_q_norm', 'mla_k_norm', 'w_out', 'norm_xa', 'norm_mem', 'xa_w_q', 'xa_w_kv', 'xa_q_norm', 'xa_k_norm', 'xa_w_o', 'norm_ffn', 'ffn_w_gate', 'ffn_w_up', 'ffn_conv_w', 'ffn_conv_b', 'ffn_w_down', 'loss_target', 'm_norm_mix', 'm_w_in', 'm_gla_gate_w2', 'm_gla_gate_b', 'm_gla_out_norm', 'm_mla_q_a_norm', 'm_mla_w_uq', 'm_mla_kv_a_norm', 'm_mla_w_ukv', 'm_mla_q_norm', 'm_mla_k_norm', 'm_w_out', 'm_norm_xa', 'm_norm_mem', 'm_xa_w_q', 'm_xa_w_kv', 'm_xa_q_norm', 'm_xa_k_norm', 'm_xa_w_o', 'm_norm_ffn', 'm_ffn_w_gate', 'm_ffn_w_up', 'm_ffn_conv_w', 'm_ffn_conv_b', 'm_ffn_w_down', 'v_norm_mix', 'v_w_in', 'v_gla_gate_w2', 'v_gla_gate_b', 'v_gla_out_norm', 'v_mla_q_a_norm', 'v_mla_w_uq', 'v_mla_kv_a_norm', 'v_mla_w_ukv', 'v_mla_q_norm', 'v_mla_k_norm', 'v_w_out', 'v_norm_xa', 'v_norm_mem', 'v_xa_w_q', 'v_xa_w_kv', 'v_xa_q_norm', 'v_xa_k_norm', 'v_xa_w_o', 'v_norm_ffn', 'v_ffn_w_gate', 'v_ffn_w_up', 'v_ffn_conv_w', 'v_ffn_conv_b', 'v_ffn_w_down']
TWIN_OUTPUTS = ['loss', 'grad_x', 'grad_norm_mix', 'grad_w_in', 'grad_gla_gate_w2', 'grad_gla_gate_b', 'grad_gla_out_norm', 'grad_mla_q_a_norm', 'grad_mla_w_uq', 'grad_mla_kv_a_norm', 'grad_mla_w_ukv', 'grad_mla_q_norm', 'grad_mla_k_norm', 'grad_w_out', 'grad_norm_xa', 'grad_norm_mem', 'grad_xa_w_q', 'grad_xa_w_kv', 'grad_xa_q_norm', 'grad_xa_k_norm', 'grad_xa_w_o', 'grad_norm_ffn', 'grad_ffn_w_gate', 'grad_ffn_w_up', 'grad_ffn_conv_w', 'grad_ffn_conv_b', 'grad_ffn_w_down', 'delta_norm_mix', 'delta_w_in', 'delta_gla_gate_w2', 'delta_gla_gate_b', 'delta_gla_out_norm', 'delta_mla_q_a_norm', 'delta_mla_w_uq', 'delta_mla_kv_a_norm', 'delta_mla_w_ukv', 'delta_mla_q_norm', 'delta_mla_k_norm', 'delta_w_out', 'delta_norm_xa', 'delta_norm_mem', 'delta_xa_w_q', 'delta_xa_w_kv', 'delta_xa_q_norm', 'delta_xa_k_norm', 'delta_xa_w_o', 'delta_norm_ffn', 'delta_ffn_w_gate', 'delta_ffn_w_up', 'delta_ffn_conv_w', 'delta_ffn_conv_b', 'delta_ffn_w_down', 'new_m_norm_mix', 'new_m_w_in', 'new_m_gla_gate_w2', 'new_m_gla_gate_b', 'new_m_gla_out_norm', 'new_m_mla_q_a_norm', 'new_m_mla_w_uq', 'new_m_mla_kv_a_norm', 'new_m_mla_w_ukv', 'new_m_mla_q_norm', 'new_m_mla_k_norm', 'new_m_w_out', 'new_m_norm_xa', 'new_m_norm_mem', 'new_m_xa_w_q', 'new_m_xa_w_kv', 'new_m_xa_q_norm', 'new_m_xa_k_norm', 'new_m_xa_w_o', 'new_m_norm_ffn', 'new_m_ffn_w_gate', 'new_m_ffn_w_up', 'new_m_ffn_conv_w', 'new_m_ffn_conv_b', 'new_m_ffn_w_down', 'new_v_norm_mix', 'new_v_w_in', 'new_v_gla_gate_w2', 'new_v_gla_gate_b', 'new_v_gla_out_norm', 'new_v_mla_q_a_norm', 'new_v_mla_w_uq', 'new_v_mla_kv_a_norm', 'new_v_mla_w_ukv', 'new_v_mla_q_norm', 'new_v_mla_k_norm', 'new_v_w_out', 'new_v_norm_xa', 'new_v_norm_mem', 'new_v_xa_w_q', 'new_v_xa_w_kv', 'new_v_xa_q_norm', 'new_v_xa_k_norm', 'new_v_xa_w_o', 'new_v_norm_ffn', 'new_v_ffn_w_gate', 'new_v_ffn_w_up', 'new_v_ffn_conv_w', 'new_v_ffn_conv_b', 'new_v_ffn_w_down']
TWIN_LEAF_KINDS = {'loss': 'loss', 'grad_x': 'grad_x', 'grad_norm_mix': 'grad_w', 'grad_w_in': 'grad_w', 'grad_gla_gate_w2': 'grad_w', 'grad_gla_gate_b': 'grad_w', 'grad_gla_out_norm': 'grad_w', 'grad_mla_q_a_norm': 'grad_w', 'grad_mla_w_uq': 'grad_w', 'grad_mla_kv_a_norm': 'grad_w', 'grad_mla_w_ukv': 'grad_w', 'grad_mla_q_norm': 'grad_w', 'grad_mla_k_norm': 'grad_w', 'grad_w_out': 'grad_w', 'grad_norm_xa': 'grad_w', 'grad_norm_mem': 'grad_w', 'grad_xa_w_q': 'grad_w', 'grad_xa_w_kv': 'grad_w', 'grad_xa_q_norm': 'grad_w', 'grad_xa_k_norm': 'grad_w', 'grad_xa_w_o': 'grad_w', 'grad_norm_ffn': 'grad_w', 'grad_ffn_w_gate': 'grad_w', 'grad_ffn_w_up': 'grad_w', 'grad_ffn_conv_w': 'grad_w', 'grad_ffn_conv_b': 'grad_w', 'grad_ffn_w_down': 'grad_w', 'delta_norm_mix': 'delta_w', 'delta_w_in': 'delta_w', 'delta_gla_gate_w2': 'delta_w', 'delta_gla_gate_b': 'delta_w', 'delta_gla_out_norm': 'delta_w', 'delta_mla_q_a_norm': 'delta_w', 'delta_mla_w_uq': 'delta_w', 'delta_mla_kv_a_norm': 'delta_w', 'delta_mla_w_ukv': 'delta_w', 'delta_mla_q_norm': 'delta_w', 'delta_mla_k_norm': 'delta_w', 'delta_w_out': 'delta_w', 'delta_norm_xa': 'delta_w', 'delta_norm_mem': 'delta_w', 'delta_xa_w_q': 'delta_w', 'delta_xa_w_kv': 'delta_w', 'delta_xa_q_norm': 'delta_w', 'delta_xa_k_norm': 'delta_w', 'delta_xa_w_o': 'delta_w', 'delta_norm_ffn': 'delta_w', 'delta_ffn_w_gate': 'delta_w', 'delta_ffn_w_up': 'delta_w', 'delta_ffn_conv_w': 'delta_w', 'delta_ffn_conv_b': 'delta_w', 'delta_ffn_w_down': 'delta_w', 'new_m_norm_mix': 'new_m', 'new_m_w_in': 'new_m', 'new_m_gla_gate_w2': 'new_m', 'new_m_gla_gate_b': 'new_m', 'new_m_gla_out_norm': 'new_m', 'new_m_mla_q_a_norm': 'new_m', 'new_m_mla_w_uq': 'new_m', 'new_m_mla_kv_a_norm': 'new_m', 'new_m_mla_w_ukv': 'new_m', 'new_m_mla_q_norm': 'new_m', 'new_m_mla_k_norm': 'new_m', 'new_m_w_out': 'new_m', 'new_m_norm_xa': 'new_m', 'new_m_norm_mem': 'new_m', 'new_m_xa_w_q': 'new_m', 'new_m_xa_w_kv': 'new_m', 'new_m_xa_q_norm': 'new_m', 'new_m_xa_k_norm': 'new_m', 'new_m_xa_w_o': 'new_m', 'new_m_norm_ffn': 'new_m', 'new_m_ffn_w_gate': 'new_m', 'new_m_ffn_w_up': 'new_m', 'new_m_ffn_conv_w': 'new_m', 'new_m_ffn_conv_b': 'new_m', 'new_m_ffn_w_down': 'new_m', 'new_v_norm_mix': 'new_v', 'new_v_w_in': 'new_v', 'new_v_gla_gate_w2': 'new_v', 'new_v_gla_gate_b': 'new_v', 'new_v_gla_out_norm': 'new_v', 'new_v_mla_q_a_norm': 'new_v', 'new_v_mla_w_uq': 'new_v', 'new_v_mla_kv_a_norm': 'new_v', 'new_v_mla_w_ukv': 'new_v', 'new_v_mla_q_norm': 'new_v', 'new_v_mla_k_norm': 'new_v', 'new_v_w_out': 'new_v', 'new_v_norm_xa': 'new_v', 'new_v_norm_mem': 'new_v', 'new_v_xa_w_q': 'new_v', 'new_v_xa_w_kv': 'new_v', 'new_v_xa_q_norm': 'new_v', 'new_v_xa_k_norm': 'new_v', 'new_v_xa_w_o': 'new_v', 'new_v_norm_ffn': 'new_v', 'new_v_ffn_w_gate': 'new_v', 'new_v_ffn_w_up': 'new_v', 'new_v_ffn_conv_w': 'new_v', 'new_v_ffn_conv_b': 'new_v', 'new_v_ffn_w_down': 'new_v'}


def _forward(args):
    return _fwd_reference(*[args[k] for k in FWD_PARAMS])


def _output_shape():
    out = _jax.eval_shape(lambda: _forward(_fwd_setup_inputs(0)))
    return out.shape, out.dtype

N_MICROBATCH = 1
ADAM_LR = 0.001
ADAM_B1 = 0.9
ADAM_B2 = 0.999
ADAM_EPS = 1e-08
ADAM_WD = 0.01
ADAM_STEP = 10
PER_EXAMPLE_BATCH_AXIS = {'x': 0, 'mem': 0, 'positions': 0, 'loss_target': 0}
SHARED_INPUTS = []
_WEIGHT_DTYPES = {'norm_mix': _jnp.float32, 'w_in': _jnp.float32, 'gla_gate_w2': _jnp.float32, 'gla_gate_b': _jnp.float32, 'gla_out_norm': _jnp.float32, 'mla_q_a_norm': _jnp.float32, 'mla_w_uq': _jnp.float32, 'mla_kv_a_norm': _jnp.float32, 'mla_w_ukv': _jnp.float32, 'mla_q_norm': _jnp.float32, 'mla_k_norm': _jnp.float32, 'w_out': _jnp.float32, 'norm_xa': _jnp.float32, 'norm_mem': _jnp.float32, 'xa_w_q': _jnp.float32, 'xa_w_kv': _jnp.float32, 'xa_q_norm': _jnp.float32, 'xa_k_norm': _jnp.float32, 'xa_w_o': _jnp.float32, 'norm_ffn': _jnp.float32, 'ffn_w_gate': _jnp.float32, 'ffn_w_up': _jnp.float32, 'ffn_conv_w': _jnp.float32, 'ffn_conv_b': _jnp.float32, 'ffn_w_down': _jnp.float32}
MOMENT_SCALE = {'norm_mix': 6.249684e+00, 'w_in': 3.276892e-01, 'gla_gate_w2': 5.295150e-02, 'gla_gate_b': 2.457582e-01, 'gla_out_norm': 4.539926e+01, 'mla_q_a_norm': 1.170125e-01, 'mla_w_uq': 7.182835e-02, 'mla_kv_a_norm': 1.256195e+00, 'mla_w_ukv': 1.080142e-01, 'mla_q_norm': 9.109360e-01, 'mla_k_norm': 9.079946e-01, 'w_out': 2.323238e-01, 'norm_xa': 4.753804e-02, 'norm_mem': 3.730529e-01, 'xa_w_q': 7.497588e-02, 'xa_w_kv': 1.980632e-01, 'xa_q_norm': 2.546203e+00, 'xa_k_norm': 2.542336e+00, 'xa_w_o': 1.747734e-01, 'norm_ffn': 2.633461e+01, 'ffn_w_gate': 2.074231e-01, 'ffn_w_up': 2.004621e-01, 'ffn_conv_w': 2.891375e+00, 'ffn_conv_b': 3.521111e+00, 'ffn_w_down': 3.031939e-01}


def _to_microbatches(a, axis):
    t = _jnp.moveaxis(a, axis, 0)
    t = t.reshape((N_MICROBATCH, t.shape[0] // N_MICROBATCH) + t.shape[1:])
    return _jnp.moveaxis(t, 1, axis + 1)


def setup_inputs(seed: int = 0) -> dict:
    inp = _fwd_setup_inputs(seed)
    key = _jax.random.fold_in(_jax.random.key(seed), 7919)
    shape, _ = _output_shape()
    out = dict(inp)
    out["loss_target"] = _jax.random.normal(_jax.random.fold_in(key, 0), shape, _jnp.float32)
    for i, name in enumerate(TWIN_WEIGHTS):
        w = inp[name].astype(_jnp.float32)
        if MOMENT_SCALE is None:
            s = _jnp.sqrt(_jnp.mean(_jnp.square(w)) + 1e-30)
        else:
            s = MOMENT_SCALE[name]
        km, kv = _jax.random.split(_jax.random.fold_in(key, i + 1))
        out[name] = w
        out["m_" + name] = s * _jax.random.normal(km, w.shape, _jnp.float32)
        out["v_" + name] = (s * s) * _jax.random.uniform(kv, w.shape, _jnp.float32, 0.5, 1.5)
    if N_MICROBATCH > 1:
        for name, axis in PER_EXAMPLE_BATCH_AXIS.items():
            out[name] = _to_microbatches(out[name], axis)
    return {'x': out['x'], 'mem': out['mem'], 'positions': out['positions'], 'norm_mix': out['norm_mix'], 'w_in': out['w_in'], 'gla_gate_w2': out['gla_gate_w2'], 'gla_gate_b': out['gla_gate_b'], 'gla_out_norm': out['gla_out_norm'], 'mla_q_a_norm': out['mla_q_a_norm'], 'mla_w_uq': out['mla_w_uq'], 'mla_kv_a_norm': out['mla_kv_a_norm'], 'mla_w_ukv': out['mla_w_ukv'], 'mla_q_norm': out['mla_q_norm'], 'mla_k_norm': out['mla_k_norm'], 'w_out': out['w_out'], 'norm_xa': out['norm_xa'], 'norm_mem': out['norm_mem'], 'xa_w_q': out['xa_w_q'], 'xa_w_kv': out['xa_w_kv'], 'xa_q_norm': out['xa_q_norm'], 'xa_k_norm': out['xa_k_norm'], 'xa_w_o': out['xa_w_o'], 'norm_ffn': out['norm_ffn'], 'ffn_w_gate': out['ffn_w_gate'], 'ffn_w_up': out['ffn_w_up'], 'ffn_conv_w': out['ffn_conv_w'], 'ffn_conv_b': out['ffn_conv_b'], 'ffn_w_down': out['ffn_w_down'], 'loss_target': out['loss_target'], 'm_norm_mix': out['m_norm_mix'], 'm_w_in': out['m_w_in'], 'm_gla_gate_w2': out['m_gla_gate_w2'], 'm_gla_gate_b': out['m_gla_gate_b'], 'm_gla_out_norm': out['m_gla_out_norm'], 'm_mla_q_a_norm': out['m_mla_q_a_norm'], 'm_mla_w_uq': out['m_mla_w_uq'], 'm_mla_kv_a_norm': out['m_mla_kv_a_norm'], 'm_mla_w_ukv': out['m_mla_w_ukv'], 'm_mla_q_norm': out['m_mla_q_norm'], 'm_mla_k_norm': out['m_mla_k_norm'], 'm_w_out': out['m_w_out'], 'm_norm_xa': out['m_norm_xa'], 'm_norm_mem': out['m_norm_mem'], 'm_xa_w_q': out['m_xa_w_q'], 'm_xa_w_kv': out['m_xa_w_kv'], 'm_xa_q_norm': out['m_xa_q_norm'], 'm_xa_k_norm': out['m_xa_k_norm'], 'm_xa_w_o': out['m_xa_w_o'], 'm_norm_ffn': out['m_norm_ffn'], 'm_ffn_w_gate': out['m_ffn_w_gate'], 'm_ffn_w_up': out['m_ffn_w_up'], 'm_ffn_conv_w': out['m_ffn_conv_w'], 'm_ffn_conv_b': out['m_ffn_conv_b'], 'm_ffn_w_down': out['m_ffn_w_down'], 'v_norm_mix': out['v_norm_mix'], 'v_w_in': out['v_w_in'], 'v_gla_gate_w2': out['v_gla_gate_w2'], 'v_gla_gate_b': out['v_gla_gate_b'], 'v_gla_out_norm': out['v_gla_out_norm'], 'v_mla_q_a_norm': out['v_mla_q_a_norm'], 'v_mla_w_uq': out['v_mla_w_uq'], 'v_mla_kv_a_norm': out['v_mla_kv_a_norm'], 'v_mla_w_ukv': out['v_mla_w_ukv'], 'v_mla_q_norm': out['v_mla_q_norm'], 'v_mla_k_norm': out['v_mla_k_norm'], 'v_w_out': out['v_w_out'], 'v_norm_xa': out['v_norm_xa'], 'v_norm_mem': out['v_norm_mem'], 'v_xa_w_q': out['v_xa_w_q'], 'v_xa_w_kv': out['v_xa_w_kv'], 'v_xa_q_norm': out['v_xa_q_norm'], 'v_xa_k_norm': out['v_xa_k_norm'], 'v_xa_w_o': out['v_xa_w_o'], 'v_norm_ffn': out['v_norm_ffn'], 'v_ffn_w_gate': out['v_ffn_w_gate'], 'v_ffn_w_up': out['v_ffn_w_up'], 'v_ffn_conv_w': out['v_ffn_conv_w'], 'v_ffn_conv_b': out['v_ffn_conv_b'], 'v_ffn_w_down': out['v_ffn_w_down']}


def _loss(weights, diff, rest, loss_target):
    with _jax.named_scope("forward"):
        args = {**rest, TWIN_DIFF_INPUT: diff, **{k: w.astype(_WEIGHT_DTYPES[k]) for k, w in weights.items()}}
        y = _forward(args)
    with _jax.named_scope("loss_head"):
        err = _jnp.square(y.astype(_jnp.float32) - loss_target)
        return 0.5 * _jnp.sum(_jnp.mean(err, axis=-1)) if err.ndim else 0.5 * err


def _adamw(w, g, m, v):
    m = ADAM_B1 * m + (1.0 - ADAM_B1) * g
    v = ADAM_B2 * v + (1.0 - ADAM_B2) * _jnp.square(g)
    m_hat = m / (1.0 - ADAM_B1 ** ADAM_STEP)
    v_hat = v / (1.0 - ADAM_B2 ** ADAM_STEP)
    delta = -ADAM_LR * (m_hat / (_jnp.sqrt(v_hat) + ADAM_EPS) + ADAM_WD * w)
    return delta, m, v


def reference(x, mem, positions, norm_mix, w_in, gla_gate_w2, gla_gate_b, gla_out_norm, mla_q_a_norm, mla_w_uq, mla_kv_a_norm, mla_w_ukv, mla_q_norm, mla_k_norm, w_out, norm_xa, norm_mem, xa_w_q, xa_w_kv, xa_q_norm, xa_k_norm, xa_w_o, norm_ffn, ffn_w_gate, ffn_w_up, ffn_conv_w, ffn_conv_b, ffn_w_down, loss_target, m_norm_mix, m_w_in, m_gla_gate_w2, m_gla_gate_b, m_gla_out_norm, m_mla_q_a_norm, m_mla_w_uq, m_mla_kv_a_norm, m_mla_w_ukv, m_mla_q_norm, m_mla_k_norm, m_w_out, m_norm_xa, m_norm_mem, m_xa_w_q, m_xa_w_kv, m_xa_q_norm, m_xa_k_norm, m_xa_w_o, m_norm_ffn, m_ffn_w_gate, m_ffn_w_up, m_ffn_conv_w, m_ffn_conv_b, m_ffn_w_down, v_norm_mix, v_w_in, v_gla_gate_w2, v_gla_gate_b, v_gla_out_norm, v_mla_q_a_norm, v_mla_w_uq, v_mla_kv_a_norm, v_mla_w_ukv, v_mla_q_norm, v_mla_k_norm, v_w_out, v_norm_xa, v_norm_mem, v_xa_w_q, v_xa_w_kv, v_xa_q_norm, v_xa_k_norm, v_xa_w_o, v_norm_ffn, v_ffn_w_gate, v_ffn_w_up, v_ffn_conv_w, v_ffn_conv_b, v_ffn_w_down):
    given = dict(x=x, mem=mem, positions=positions, norm_mix=norm_mix, w_in=w_in, gla_gate_w2=gla_gate_w2, gla_gate_b=gla_gate_b, gla_out_norm=gla_out_norm, mla_q_a_norm=mla_q_a_norm, mla_w_uq=mla_w_uq, mla_kv_a_norm=mla_kv_a_norm, mla_w_ukv=mla_w_ukv, mla_q_norm=mla_q_norm, mla_k_norm=mla_k_norm, w_out=w_out, norm_xa=norm_xa, norm_mem=norm_mem, xa_w_q=xa_w_q, xa_w_kv=xa_w_kv, xa_q_norm=xa_q_norm, xa_k_norm=xa_k_norm, xa_w_o=xa_w_o, norm_ffn=norm_ffn, ffn_w_gate=ffn_w_gate, ffn_w_up=ffn_w_up, ffn_conv_w=ffn_conv_w, ffn_conv_b=ffn_conv_b, ffn_w_down=ffn_w_down, loss_target=loss_target, m_norm_mix=m_norm_mix, m_w_in=m_w_in, m_gla_gate_w2=m_gla_gate_w2, m_gla_gate_b=m_gla_gate_b, m_gla_out_norm=m_gla_out_norm, m_mla_q_a_norm=m_mla_q_a_norm, m_mla_w_uq=m_mla_w_uq, m_mla_kv_a_norm=m_mla_kv_a_norm, m_mla_w_ukv=m_mla_w_ukv, m_mla_q_norm=m_mla_q_norm, m_mla_k_norm=m_mla_k_norm, m_w_out=m_w_out, m_norm_xa=m_norm_xa, m_norm_mem=m_norm_mem, m_xa_w_q=m_xa_w_q, m_xa_w_kv=m_xa_w_kv, m_xa_q_norm=m_xa_q_norm, m_xa_k_norm=m_xa_k_norm, m_xa_w_o=m_xa_w_o, m_norm_ffn=m_norm_ffn, m_ffn_w_gate=m_ffn_w_gate, m_ffn_w_up=m_ffn_w_up, m_ffn_conv_w=m_ffn_conv_w, m_ffn_conv_b=m_ffn_conv_b, m_ffn_w_down=m_ffn_w_down, v_norm_mix=v_norm_mix, v_w_in=v_w_in, v_gla_gate_w2=v_gla_gate_w2, v_gla_gate_b=v_gla_gate_b, v_gla_out_norm=v_gla_out_norm, v_mla_q_a_norm=v_mla_q_a_norm, v_mla_w_uq=v_mla_w_uq, v_mla_kv_a_norm=v_mla_kv_a_norm, v_mla_w_ukv=v_mla_w_ukv, v_mla_q_norm=v_mla_q_norm, v_mla_k_norm=v_mla_k_norm, v_w_out=v_w_out, v_norm_xa=v_norm_xa, v_norm_mem=v_norm_mem, v_xa_w_q=v_xa_w_q, v_xa_w_kv=v_xa_w_kv, v_xa_q_norm=v_xa_q_norm, v_xa_k_norm=v_xa_k_norm, v_xa_w_o=v_xa_w_o, v_norm_ffn=v_norm_ffn, v_ffn_w_gate=v_ffn_w_gate, v_ffn_w_up=v_ffn_w_up, v_ffn_conv_w=v_ffn_conv_w, v_ffn_conv_b=v_ffn_conv_b, v_ffn_w_down=v_ffn_w_down)
    weights = {n: given[n] for n in TWIN_WEIGHTS}
    shared = {n: given[n] for n in SHARED_INPUTS}
    per_example = {n: given[n] for n in ['x', 'mem', 'positions']}
    grad_fn = _jax.value_and_grad(_loss, argnums=(0, 1))

    def one_microbatch(ex, loss_target):
        ex = dict(ex)
        diff = ex.pop(TWIN_DIFF_INPUT)
        return grad_fn(weights, diff, {**shared, **ex}, loss_target)

    if N_MICROBATCH == 1:
        loss, (grad_w, grad_x) = one_microbatch(per_example, given["loss_target"])
    else:
        def body(carry, xs):
            loss_sum, grad_sum = carry
            l_k, (gw_k, gx_k) = one_microbatch(xs[0], xs[1])
            with _jax.named_scope("update"):
                return (loss_sum + l_k, _jax.tree.map(_jnp.add, grad_sum, gw_k)), gx_k

        init = (_jnp.zeros((), _jnp.float32), _jax.tree.map(_jnp.zeros_like, weights))
        (loss, grad_w), grad_x = _jax.lax.scan(body, init, (per_example, given["loss_target"]))
    with _jax.named_scope("update"):
        delta_w, new_m, new_v = {}, {}, {}
        for n in TWIN_WEIGHTS:
            delta_w[n], new_m[n], new_v[n] = _adamw(weights[n], grad_w[n], given["m_" + n], given["v_" + n])
    return (loss, grad_x, *[grad_w[n] for n in TWIN_WEIGHTS], *[delta_w[n] for n in TWIN_WEIGHTS],
            *[new_m[n] for n in TWIN_WEIGHTS], *[new_v[n] for n in TWIN_WEIGHTS])
```

```python
import functools

import jax
import jax.numpy as jnp
from jax import lax
from jax.experimental import pallas as pl
from jax.experimental.pallas import tpu as pltpu

F32, BF16 = jnp.float32, jnp.bfloat16
MESH = pl.DeviceIdType.MESH

D_MODEL = 1024
EPS = 1e-6
GLA_HEADS, GLA_DK, GLA_DV, GLA_RANK, GLA_CHUNK = 4, 64, 128, 16, 64
GLA_GATE_NORM = 16.0
MLA_HEADS, MLA_Q_RANK, MLA_KV_RANK, MLA_NOPE, MLA_ROPE, MLA_V = 8, 256, 128, 64, 32, 64
MLA_QK = MLA_NOPE + MLA_ROPE
ROPE_THETA = 10000.0
XA_HEADS, XA_DIM = 4, 128
D_FF = 2816
ADAM_LR, ADAM_B1, ADAM_B2, ADAM_EPS, ADAM_WD, ADAM_STEP = 0.001, 0.9, 0.999, 1e-08, 0.01, 10

LANES = 128
VMEM_LIMIT = 56 * 1024 * 1024

P_GQ, P_GK, P_GV, P_OG, P_CQ, P_CKV, P_KPE, P_ALR, P_WIDTH = 0, 256, 512, 1024, 1536, 1792, 1920, 2048, 2176
N_GQ, N_GK, N_GV, N_ALR, N_OG, N_CQ, N_CKV, N_KPE, N_WIDTH = 0, 256, 512, 1024, 1040, 1552, 1808, 1936, 1968

SHARDED = (("w_in", 1), ("gla_gate_w2", 1), ("mla_w_uq", 1), ("mla_w_ukv", 1), ("w_out", 0), ("xa_w_q", 0),
           ("xa_w_kv", 0), ("xa_w_o", 1), ("ffn_w_gate", 1), ("ffn_w_up", 1), ("ffn_conv_w", 1), ("ffn_w_down", 0))
REPLICATED = ("norm_mix", "gla_gate_b", "gla_out_norm", "mla_q_a_norm", "mla_kv_a_norm", "mla_q_norm", "mla_k_norm",
              "norm_xa", "norm_mem", "xa_q_norm", "xa_k_norm", "norm_ffn", "ffn_conv_b")
EXACT_GATHER = ("gla_gate_w2", "ffn_conv_w")
WEIGHTS = ("norm_mix", "w_in", "gla_gate_w2", "gla_gate_b", "gla_out_norm", "mla_q_a_norm", "mla_w_uq",
           "mla_kv_a_norm", "mla_w_ukv", "mla_q_norm", "mla_k_norm", "w_out", "norm_xa", "norm_mem", "xa_w_q",
           "xa_w_kv", "xa_q_norm", "xa_k_norm", "xa_w_o", "norm_ffn", "ffn_w_gate", "ffn_w_up", "ffn_conv_w",
           "ffn_conv_b", "ffn_w_down")
PACK_ROWS = 1024


_NN = ((1,), (0,))
_NT = ((1,), (1,))
_TN = ((0,), (0,))


def _dg(a, b, dims):
    return lax.dot_general(a.astype(BF16), b.astype(BF16), (dims, ((), ())), preferred_element_type=F32)


@jax.custom_vjp
def _dot_nn(a, b):
    return _dg(a, b, _NN)


_dot_nn.defvjp(lambda a, b: (_dg(a, b, _NN), (a, b)),
               lambda r, g: (_dg(g, r[1], _NT).astype(r[0].dtype), _dg(r[0], g, _TN).astype(r[1].dtype)))


@jax.custom_vjp
def _dot_nt(a, b):
    return _dg(a, b, _NT)


_dot_nt.defvjp(lambda a, b: (_dg(a, b, _NT), (a, b)),
               lambda r, g: (_dg(g, r[1], _NN).astype(r[0].dtype), _dg(g, r[0], _TN).astype(r[1].dtype)))


@jax.custom_vjp
def _dot_tn(a, b):
    return _dg(a, b, _TN)


_dot_tn.defvjp(lambda a, b: (_dg(a, b, _TN), (a, b)),
               lambda r, g: (_dg(r[1], g, _NT).astype(r[0].dtype), _dg(r[0], g, _NN).astype(r[1].dtype)))


def _rms(x, w, n=None):
    n = x.shape[-1] if n is None else n
    ms = jnp.sum(x * x, axis=-1, keepdims=True) * (1.0 / n)
    return x * lax.rsqrt(ms + EPS) * w


def _silu(x):
    return x * jax.nn.sigmoid(x)


def _log_sigmoid(x):
    return jnp.minimum(x, 0.0) - jnp.log(1.0 + jnp.exp(-jnp.abs(x)))


@jax.custom_vjp
def _rope(y, c, sa, sb):
    return y * c + pltpu.roll(y, LANES - 16, 1) * sa + pltpu.roll(y, 16, 1) * sb


def _rope_bwd(res, g):
    c, sa, sb = res
    gy = g * c + pltpu.roll(g * sa, 16, 1) + pltpu.roll(g * sb, LANES - 16, 1)
    return gy, jnp.zeros_like(c), jnp.zeros_like(sa), jnp.zeros_like(sb)


_rope.defvjp(lambda y, c, sa, sb: (_rope(y, c, sa, sb), (c, sa, sb)), _rope_bwd)


def _make_shift(k):
    def up(g):
        n = g.shape[0]
        r = lax.broadcasted_iota(jnp.int32, g.shape, 0)
        return jnp.where(r < n - k, pltpu.roll(g, n - k, 0), 0.0)

    @jax.custom_vjp
    def down(x):
        r = lax.broadcasted_iota(jnp.int32, x.shape, 0)
        return jnp.where(r >= k, pltpu.roll(x, k, 0), 0.0)

    down.defvjp(lambda x: (down(x), None), lambda _, g: (up(g),))
    return down


_shift1 = _make_shift(1)
_shift2 = _make_shift(2)


def _lane_mask(lo, hi):
    lane = lax.broadcasted_iota(jnp.int32, (1, LANES), 1)
    return ((lane >= lo) & (lane < hi)).astype(F32)


def _tile(n, t):
    t = min(n, t)
    assert n % t == 0, (n, t)
    return t


def _matmul(a, b, mode, out_dtype, name, residual=None):
    if mode == "nn":
        (m, k), (k2, n) = a.shape, b.shape
    elif mode == "nt":
        (m, k), (n, k2) = a.shape, b.shape
    else:
        (k, m), (k2, n) = a.shape, b.shape
    assert k == k2, (a.shape, b.shape, mode)
    if mode == "tn":
        tm = m if m <= 1408 else m // 2
        tn = n if tm * n <= 1024 * 2304 else n // 2
        tk = _tile(k, 512)
    else:
        tm, tn, tk = _tile(m, 512), n, k
    assert m % tm == 0 and n % tn == 0 and k % tk == 0
    nk = k // tk
    dims = {"nn": _NN, "nt": _NT, "tn": _TN}[mode]

    def body(*refs):
        if residual is None:
            a_ref, b_ref, o_ref, acc = refs
        else:
            a_ref, b_ref, r_ref, o_ref, acc = refs
        kk = pl.program_id(2)

        @pl.when(kk == 0)
        def _():
            acc[...] = jnp.zeros_like(acc)

        acc[...] += _dg(a_ref[...], b_ref[...], dims)

        @pl.when(kk == nk - 1)
        def _():
            r = acc[...]
            if residual is not None:
                r = r + r_ref[...]
            o_ref[...] = r.astype(o_ref.dtype)

    if mode == "nn":
        in_specs = [pl.BlockSpec((tm, tk), lambda i, j, kk: (i, kk)), pl.BlockSpec((tk, tn), lambda i, j, kk: (kk, j))]
    elif mode == "nt":
        in_specs = [pl.BlockSpec((tm, tk), lambda i, j, kk: (i, kk)), pl.BlockSpec((tn, tk), lambda i, j, kk: (j, kk))]
    else:
        in_specs = [pl.BlockSpec((tk, tm), lambda i, j, kk: (kk, i)), pl.BlockSpec((tk, tn), lambda i, j, kk: (kk, j))]
    args = [a, b]
    if residual is not None:
        in_specs.append(pl.BlockSpec((tm, tn), lambda i, j, kk: (i, j)))
        args.append(residual)
    return pl.pallas_call(
        body, grid=(m // tm, n // tn, nk), in_specs=in_specs,
        out_specs=pl.BlockSpec((tm, tn), lambda i, j, kk: (i, j)),
        out_shape=jax.ShapeDtypeStruct((m, n), out_dtype),
        scratch_shapes=[pltpu.VMEM((tm, tn), F32)],
        compiler_params=pltpu.CompilerParams(dimension_semantics=("parallel", "parallel", "arbitrary"),
                                             vmem_limit_bytes=VMEM_LIMIT),
        name=name)(*args)


def _row(a, width=None, col_block=0):
    return (a, a.shape[1] if width is None else width, col_block)


def _rows_call(body, rows, consts, outs, accs=(), *, name, tile=512):
    s = rows[0][0].shape[0]
    t = _tile(s, tile)
    nr, nc, no = len(rows), len(consts), len(outs)

    def kern(*refs):
        r = [x[...] for x in refs[:nr]]
        c = [x[...] for x in refs[nr:nr + nc]]
        o_refs = refs[nr + nc:nr + nc + no]
        a_refs = refs[nr + nc + no:]
        ro, ao = body(r, c)
        for ref, val in zip(o_refs, ro, strict=True):
            ref[...] = val.astype(ref.dtype)
        if a_refs:
            @pl.when(pl.program_id(0) == 0)
            def _():
                for ref in a_refs:
                    ref[...] = jnp.zeros_like(ref)

            for ref, val in zip(a_refs, ao, strict=True):
                ref[...] += val

    in_specs = [pl.BlockSpec((t, w), functools.partial(lambda cb, i: (i, cb), cb)) for (_, w, cb) in rows]
    in_specs += [pl.BlockSpec(c.shape, lambda i: (0, 0)) for c in consts]
    out_specs = [pl.BlockSpec((t, w), lambda i: (i, 0)) for (w, _) in outs]
    out_specs += [pl.BlockSpec(shape, lambda i: (0, 0)) for shape in accs]
    out_shape = [jax.ShapeDtypeStruct((s, w), dt) for (w, dt) in outs]
    out_shape += [jax.ShapeDtypeStruct(shape, F32) for shape in accs]
    return pl.pallas_call(
        kern, grid=(s // t,), in_specs=in_specs, out_specs=out_specs, out_shape=out_shape,
        compiler_params=pltpu.CompilerParams(dimension_semantics=("arbitrary" if accs else "parallel",),
                                             vmem_limit_bytes=VMEM_LIMIT),
        name=name)(*[r[0] for r in rows], *consts)


def _gla_chunk(q, k, la, v0, v1, s0, s1):
    c = q.shape[0]
    r = lax.broadcasted_iota(jnp.int32, (c, c), 0)
    cc = lax.broadcasted_iota(jnp.int32, (c, c), 1)
    tril = cc <= r
    cum = lax.dot_general(tril.astype(F32), la, (_NN, ((), ())), precision=lax.Precision.HIGHEST,
                          preferred_element_type=F32)
    cl = jnp.sum(la, axis=0, keepdims=True)
    qd = q * (GLA_DK ** -0.5) * jnp.exp(cum)
    ki = k * jnp.exp(-cum)
    ke = k * jnp.exp(cl - cum)
    dec = jnp.exp(cl)
    outs, news = [], []
    for h, (v, s) in enumerate(((v0, s0), (v1, s1))):
        mk = _lane_mask(GLA_DK * h, GLA_DK * (h + 1))
        qh = qd * mk
        att = jnp.where(tril, _dot_nt(qh, ki), 0.0)
        outs.append(_dot_nn(att, v) + _dot_nt(qh, s))
        news.append(s * dec + _dot_tn(v, ke * mk))
    return outs[0], outs[1], news[0], news[1]


def _gla_specs(tb, rev_nb=None):
    blk = (lambda b: b) if rev_nb is None else (lambda b: rev_nb - 1 - b)
    q = pl.BlockSpec((tb, 128), lambda p, b: (blk(b), P_GQ // 128 + p))
    k = pl.BlockSpec((tb, 128), lambda p, b: (blk(b), P_GK // 128 + p))
    la = pl.BlockSpec((tb, 128), lambda p, b: (blk(b), p))
    v = pl.BlockSpec((tb, 256), lambda p, b: (blk(b), P_GV // 256 + p))
    o = pl.BlockSpec((tb, 256), lambda p, b: (blk(b), p))
    st = pl.BlockSpec((tb // GLA_CHUNK, 2, 128, 128), lambda p, b: (blk(b), p, 0, 0))
    return q, k, la, v, o, st


def _gla_fwd(proj, la):
    s = proj.shape[0]
    tb = _tile(s, 512)
    nb, nch = s // tb, tb // GLA_CHUNK

    def kern(q_ref, k_ref, la_ref, v_ref, o_ref, st_ref, s_sc):
        @pl.when(pl.program_id(1) == 0)
        def _():
            s_sc[...] = jnp.zeros_like(s_sc)

        s0, s1 = s_sc[0], s_sc[1]
        for ci in range(nch):
            sl = slice(ci * GLA_CHUNK, (ci + 1) * GLA_CHUNK)
            st_ref[ci, 0] = s0
            st_ref[ci, 1] = s1
            o0, o1, s0, s1 = _gla_chunk(q_ref[sl, :], k_ref[sl, :], la_ref[sl, :], v_ref[sl, 0:128],
                                        v_ref[sl, 128:256], s0, s1)
            o_ref[sl, 0:128] = o0
            o_ref[sl, 128:256] = o1
        s_sc[0] = s0
        s_sc[1] = s1

    q, k, lasp, v, o, st = _gla_specs(tb)
    return pl.pallas_call(
        kern, grid=(2, nb), in_specs=[q, k, lasp, v], out_specs=[o, st],
        out_shape=[jax.ShapeDtypeStruct((s, 512), F32),
                   jax.ShapeDtypeStruct((s // GLA_CHUNK, GLA_HEADS, 128, 128), F32)],
        scratch_shapes=[pltpu.VMEM((2, 128, 128), F32)],
        compiler_params=pltpu.CompilerParams(dimension_semantics=("parallel", "arbitrary"),
                                             vmem_limit_bytes=VMEM_LIMIT),
        name="gla_fwd")(proj, proj, la, proj)


def _gla_bwd(proj, la, states, d_o):
    s = proj.shape[0]
    tb = _tile(s, 512)
    nb, nch = s // tb, tb // GLA_CHUNK

    def kern(q_ref, k_ref, la_ref, v_ref, do_ref, st_ref, dq_ref, dk_ref, dla_ref, dv_ref, ds_sc):
        @pl.when(pl.program_id(1) == 0)
        def _():
            ds_sc[...] = jnp.zeros_like(ds_sc)

        d0, d1 = ds_sc[0], ds_sc[1]
        for ci in reversed(range(nch)):
            sl = slice(ci * GLA_CHUNK, (ci + 1) * GLA_CHUNK)
            _, vjp = jax.vjp(_gla_chunk, q_ref[sl, :], k_ref[sl, :], la_ref[sl, :], v_ref[sl, 0:128],
                             v_ref[sl, 128:256], st_ref[ci, 0], st_ref[ci, 1])
            gq, gk, gla, gv0, gv1, d0, d1 = vjp((do_ref[sl, 0:128], do_ref[sl, 128:256], d0, d1))
            dq_ref[sl, :] = gq
            dk_ref[sl, :] = gk
            dla_ref[sl, :] = gla
            dv_ref[sl, 0:128] = gv0
            dv_ref[sl, 128:256] = gv1
        ds_sc[0] = d0
        ds_sc[1] = d1

    q, k, lasp, v, o, st = _gla_specs(tb, rev_nb=nb)
    return pl.pallas_call(
        kern, grid=(2, nb), in_specs=[q, k, lasp, v, o, st], out_specs=[lasp, lasp, lasp, o],
        out_shape=[jax.ShapeDtypeStruct((s, 256), F32), jax.ShapeDtypeStruct((s, 256), F32),
                   jax.ShapeDtypeStruct((s, 256), F32), jax.ShapeDtypeStruct((s, 512), F32)],
        scratch_shapes=[pltpu.VMEM((2, 128, 128), F32)],
        compiler_params=pltpu.CompilerParams(dimension_semantics=("parallel", "arbitrary"),
                                             vmem_limit_bytes=VMEM_LIMIT),
        name="gla_bwd")(proj, proj, la, proj, d_o, states)


def _causal_keep(t, qi, ki):
    row = lax.broadcasted_iota(jnp.int32, (t, t), 0) + qi * t
    col = lax.broadcasted_iota(jnp.int32, (t, t), 1) + ki * t
    return col <= row


def _attn_fwd(q, k, v, tile=512):
    s = q.shape[0]
    t = _tile(s, tile)
    n = s // t

    def kern(q_ref, k_ref, v_ref, o_ref, lse_ref, m_sc, l_sc, acc_sc):
        qi, ki = pl.program_id(1), pl.program_id(2)
        first = lax.broadcasted_iota(jnp.int32, (t, LANES), 1) < MLA_V

        @pl.when(ki == 0)
        def _():
            m_sc[...] = jnp.full_like(m_sc, -jnp.inf)
            l_sc[...] = jnp.zeros_like(l_sc)
            acc_sc[...] = jnp.zeros_like(acc_sc)

        @pl.when(ki <= qi)
        def _():
            keep = _causal_keep(t, qi, ki)
            alphas, pvs = [], []
            for h in range(2):
                sc = _dg(q_ref[:, 128 * h:128 * (h + 1)], k_ref[:, 128 * h:128 * (h + 1)], _NT)
                sc = jnp.where(keep, sc, -jnp.inf)
                m_prev = m_sc[h]
                m_new = jnp.maximum(m_prev, jnp.max(sc, axis=1, keepdims=True))
                alpha = jnp.exp(m_prev - m_new)
                p = jnp.exp(sc - m_new[:, 0:1])
                l_sc[h] = alpha * l_sc[h] + jnp.sum(p, axis=1, keepdims=True)
                m_sc[h] = m_new
                alphas.append(alpha)
                pvs.append(_dg(p, v_ref[...], _NN))
            acc_sc[...] = acc_sc[...] * jnp.where(first, alphas[0], alphas[1]) + jnp.where(first, pvs[0], pvs[1])

        @pl.when(ki == qi)
        def _():
            l = jnp.where(first, l_sc[0], l_sc[1])
            m = jnp.where(first, m_sc[0], m_sc[1])
            o_ref[...] = acc_sc[...] / l
            lse_ref[...] = m + jnp.log(l)

    kv_idx = lambda p, qi, ki: (jnp.minimum(ki, qi), p)
    return pl.pallas_call(
        kern, grid=(MLA_HEADS // 2, n, n),
        in_specs=[pl.BlockSpec((t, 256), lambda p, qi, ki: (qi, p)), pl.BlockSpec((t, 256), kv_idx),
                  pl.BlockSpec((t, 128), kv_idx)],
        out_specs=[pl.BlockSpec((t, 128), lambda p, qi, ki: (qi, p)), pl.BlockSpec((t, 128), lambda p, qi, ki: (qi, p))],
        out_shape=[jax.ShapeDtypeStruct((s, 512), F32), jax.ShapeDtypeStruct((s, 512), F32)],
        scratch_shapes=[pltpu.VMEM((2, t, LANES), F32), pltpu.VMEM((2, t, LANES), F32), pltpu.VMEM((t, LANES), F32)],
        compiler_params=pltpu.CompilerParams(dimension_semantics=("parallel", "parallel", "arbitrary"),
                                             vmem_limit_bytes=VMEM_LIMIT),
        name="mla_attn_fwd")(q, k, v)


def _attn_bwd(q, k, v, o, lse, dcat, tile=512):
    s = q.shape[0]
    t = _tile(s, tile)
    n = s // t

    def kern(q_ref, k_ref, v_ref, o_ref, lse_ref, do_ref, dq_ref, dk_ref, dv_ref, dk_sc, dv_sc):
        ki, qi = pl.program_id(1), pl.program_id(2)

        @pl.when((ki == 0) & (qi == 0))
        def _():
            dq_ref[...] = jnp.zeros_like(dq_ref)

        @pl.when(qi == ki)
        def _():
            dk_sc[...] = jnp.zeros_like(dk_sc)
            dv_sc[...] = jnp.zeros_like(dv_sc)

        @pl.when(qi >= ki)
        def _():
            keep = _causal_keep(t, qi, ki)
            d_o = do_ref[...]
            prod = d_o * o_ref[...]
            rows = pl.ds(pl.multiple_of(qi * t, t), t)
            for h in range(2):
                hs = slice(128 * h, 128 * (h + 1))
                mk = _lane_mask(MLA_V * h, MLA_V * (h + 1))
                qh, kh = q_ref[:, hs], k_ref[:, hs]
                sc = jnp.where(keep, _dg(qh, kh, _NT), -jnp.inf)
                p = jnp.exp(sc - lse_ref[:, MLA_V * h:MLA_V * h + 1])
                doh = d_o * mk
                dp = _dg(doh, v_ref[...], _NT)
                delta = jnp.sum(prod * mk, axis=1, keepdims=True)
                ds = p * (dp - delta)
                dv_sc[...] += _dg(p, doh, _TN)
                dk_sc[:, hs] += _dg(ds, qh, _TN)
                dq_ref[rows, hs] += _dg(ds, kh, _NN)

        @pl.when(qi == n - 1)
        def _():
            dk_ref[...] = dk_sc[...]
            dv_ref[...] = dv_sc[...].astype(dv_ref.dtype)

    q_idx = lambda p, ki, qi: (jnp.maximum(qi, ki), p)
    return pl.pallas_call(
        kern, grid=(MLA_HEADS // 2, n, n),
        in_specs=[pl.BlockSpec((t, 256), q_idx), pl.BlockSpec((t, 256), lambda p, ki, qi: (ki, p)),
                  pl.BlockSpec((t, 128), lambda p, ki, qi: (ki, p)), pl.BlockSpec((t, 128), q_idx),
                  pl.BlockSpec((t, 128), q_idx),
                  pl.BlockSpec((t, 128), lambda p, ki, qi: (jnp.maximum(qi, ki), 4 + p))],
        out_specs=[pl.BlockSpec((s, 256), lambda p, ki, qi: (0, p)), pl.BlockSpec((t, 256), lambda p, ki, qi: (ki, p)),
                   pl.BlockSpec((t, 128), lambda p, ki, qi: (ki, p))],
        out_shape=[jax.ShapeDtypeStruct((s, 1024), F32), jax.ShapeDtypeStruct((s, 1024), F32),
                   jax.ShapeDtypeStruct((s, 512), BF16)],
        scratch_shapes=[pltpu.VMEM((t, 256), F32), pltpu.VMEM((t, 128), F32)],
        compiler_params=pltpu.CompilerParams(dimension_semantics=("parallel", "arbitrary", "arbitrary"),
                                             vmem_limit_bytes=VMEM_LIMIT),
        name="mla_attn_bwd")(q, k, v, o, lse, dcat)


def _gate_fn(alr, w2, b):
    return _log_sigmoid(_dot_nn(alr, w2) + b) * (1.0 / GLA_GATE_NORM)


def _qk_head(qh, kh, kpe, c, sa, sb, qn, kn):
    kfull = kh + kpe * _lane_mask(MLA_NOPE, MLA_QK)
    q_r = _rope(_rms(qh, qn, MLA_QK), c, sa, sb) * (MLA_QK ** -0.5)
    k_r = _rope(_rms(kfull, kn, MLA_QK), c, sa, sb)
    return q_r, k_r


def _mix_head(o, og, gn):
    return _rms(o, gn) * _silu(og)


def _xa_head(xq, xk, xv, qn, kn):
    sc = _dot_nt(_rms(xq, qn), _rms(xk, kn)) * (XA_DIM ** -0.5)
    e = jnp.exp(sc - lax.stop_gradient(jnp.max(sc, axis=1, keepdims=True)))
    p = e / jnp.sum(e, axis=1, keepdims=True)
    return _dot_nn(p, xv)


def _convgate(g, u, cw0, cw1, cw2, cb):
    gc = cb + cw0 * _shift2(g) + cw1 * _shift1(g) + cw2 * g
    return _silu(gc) * u


def _heads(x, n):
    return [x[:, 128 * h:128 * (h + 1)] for h in range(n)]


def _cat(xs):
    return jnp.concatenate(xs, axis=1)


def _norm_fwd(x, w, name):
    return _rows_call(lambda r, c: ([_rms(r[0], c[0])], []), [_row(x)], [w], [(x.shape[1], BF16)], name=name)[0]


def _norm_bwd(x, w, d_out, add, name):
    def body(r, c):
        _, vjp = jax.vjp(_rms, r[0], c[0])
        dx, dw = vjp(r[1])
        return [dx + r[2]], [dw]

    return _rows_call(body, [_row(x), _row(d_out), _row(add)], [w], [(x.shape[1], F32)], [w.shape], name=name)


def _conv_call(body, cols, consts, outs, accs, name):
    s, f = cols[0].shape
    nco, nc, no = len(cols), len(consts), len(outs)

    def kern(*refs):
        cv = [x[...] for x in refs[:nco]]
        kv = [x[...] for x in refs[nco:nco + nc]]
        ro, ao = body(cv, kv)
        for ref, val in zip(refs[nco + nc:], list(ro) + list(ao), strict=True):
            ref[...] = val.astype(ref.dtype)

    big = pl.BlockSpec((s, LANES), lambda j: (0, j))
    small = pl.BlockSpec((1, LANES), lambda j: (0, j))
    return pl.pallas_call(
        kern, grid=(f // LANES,), in_specs=[big] * nco + [small] * nc, out_specs=[big] * no + [small] * accs,
        out_shape=[jax.ShapeDtypeStruct((s, f), dt) for dt in outs] + [jax.ShapeDtypeStruct((1, f), F32)] * accs,
        compiler_params=pltpu.CompilerParams(dimension_semantics=("parallel",), vmem_limit_bytes=VMEM_LIMIT),
        name=name)(*cols, *consts)


def _rope_tables(pos):
    half = MLA_ROPE // 2
    inv = ROPE_THETA ** (-jnp.arange(half, dtype=F32) / half)
    ang = pos.astype(F32)[:, None] * inv
    cos, sin = jnp.cos(ang), jnp.sin(ang)
    s = pos.shape[0]
    z = lambda w: jnp.zeros((s, w), F32)
    c = jnp.concatenate([jnp.ones((s, MLA_NOPE), F32), cos, cos, jnp.ones((s, LANES - MLA_QK), F32)], axis=1)
    sa = jnp.concatenate([z(MLA_NOPE), -sin, z(half), z(LANES - MLA_QK)], axis=1)
    sb = jnp.concatenate([z(MLA_NOPE), z(half), sin, z(LANES - MLA_QK)], axis=1)
    return c, sa, sb


def _local_step(x, mem, pos, target, w):
    g = {}
    c, sa, sb = _rope_tables(pos)

    xn = _norm_fwd(x, w["norm_mix"], "norm_mix_fwd")
    proj = _matmul(xn, w["in"], "nn", F32, "proj_fwd")
    alr = _row(proj, 128, P_ALR // 128)
    kpe = _row(proj, 128, P_KPE // 128)
    og = _row(proj, 512, P_OG // 512)
    cq = _row(proj, 256, P_CQ // 256)
    ckv = _row(proj, 128, P_CKV // 128)

    la = _rows_call(lambda r, k: ([_gate_fn(r[0], k[0], k[1])], []), [alr], [w["w2"], w["gate_b"]],
                    [(256, F32)], name="gla_gate_fwd")[0]
    o_gla, states = _gla_fwd(proj, la)

    q_lat, kv_lat = _rows_call(lambda r, k: ([_rms(r[0], k[0]), _rms(r[1], k[1])], []), [cq, ckv],
                               [w["q_a_norm"], w["kv_a_norm"]], [(256, BF16), (128, BF16)], name="mla_lat_fwd")
    q_up = _matmul(q_lat, w["uq"], "nn", F32, "mla_q_fwd")
    k_up = _matmul(kv_lat, w["k"], "nn", F32, "mla_k_fwd")
    v_mla = _matmul(kv_lat, w["v"], "nn", BF16, "mla_v_fwd")

    def qk_body(r, k):
        qs, ks = [], []
        for qh, kh in zip(_heads(r[0], MLA_HEADS), _heads(r[1], MLA_HEADS)):
            a, b = _qk_head(qh, kh, r[2], r[3], r[4], r[5], k[0], k[1])
            qs.append(a)
            ks.append(b)
        return [_cat(qs), _cat(ks)], []

    tabs = [_row(c), _row(sa), _row(sb)]
    q_r, k_r = _rows_call(qk_body, [_row(q_up), _row(k_up), kpe] + tabs, [w["q_norm"], w["k_norm"]],
                          [(1024, BF16), (1024, BF16)], name="mla_qk_fwd")
    o_mla, lse = _attn_fwd(q_r, k_r, v_mla)

    def mix_body(r, k):
        ys = [_mix_head(o, g_, k[0]) for o, g_ in zip(_heads(r[0], GLA_HEADS), _heads(r[1], GLA_HEADS))]
        return [_cat(ys + [r[2]])], []

    cat = _rows_call(mix_body, [_row(o_gla), og, _row(o_mla)], [w["gla_out_norm"]], [(1024, BF16)],
                     name="mix_fwd")[0]
    h1 = _matmul(cat, w["out"], "nn", F32, "out_fwd", residual=x)

    hn = _norm_fwd(h1, w["norm_xa"], "norm_xa_fwd")
    mn = _norm_fwd(mem, w["norm_mem"], "norm_mem_fwd")
    xq = _matmul(hn, w["xq"], "nn", F32, "xa_q_fwd")
    xkv = _matmul(mn, w["xkv"], "nn", F32, "xa_kv_fwd")

    def xa_body(r, k):
        ks, vs = _heads(k[0], 2 * XA_HEADS)[:XA_HEADS], _heads(k[0], 2 * XA_HEADS)[XA_HEADS:]
        return [_cat([_xa_head(a, b, v_, k[1], k[2]) for a, b, v_ in zip(_heads(r[0], XA_HEADS), ks, vs)])], []

    xo = _rows_call(xa_body, [_row(xq)], [xkv, w["xa_q_norm"], w["xa_k_norm"]], [(512, BF16)], name="xa_fwd")[0]
    h2 = _matmul(xo, w["xo"], "nn", F32, "xa_o_fwd", residual=h1)

    fn = _norm_fwd(h2, w["norm_ffn"], "norm_ffn_fwd")
    gg = _matmul(fn, w["wg"], "nn", F32, "ffn_gate_fwd")
    uu = _matmul(fn, w["wu"], "nn", F32, "ffn_up_fwd")
    conv_consts = [w["cw0"], w["cw1"], w["cw2"], w["cb"]]
    act = _conv_call(lambda cv, kv: ([_convgate(cv[0], cv[1], *kv)], []), [gg, uu], conv_consts, [BF16], 0,
                     "ffn_conv_fwd")[0]
    y = _matmul(act, w["wd"], "nn", F32, "ffn_down_fwd", residual=h2)

    def loss_body(r, k):
        err = r[0] - r[1]
        part = 0.5 * jnp.sum(jnp.sum(err * err, axis=1, keepdims=True) * (1.0 / D_MODEL), axis=0, keepdims=True)
        return [err * (1.0 / D_MODEL)], [jnp.broadcast_to(part, (1, LANES))]

    dy, loss = _rows_call(loss_body, [_row(y), _row(target)], [], [(D_MODEL, F32)], [(1, LANES)], name="loss")

    g["ffn_w_down"] = _matmul(act, dy, "tn", F32, "ffn_down_dw")
    dact = _matmul(dy, w["wd"], "nt", F32, "ffn_down_dx")

    def conv_bwd(cv, kv):
        _, vjp = jax.vjp(_convgate, cv[0], cv[1], *kv)
        dg_, du_, d0, d1, d2, db = vjp(cv[2])
        return [dg_, du_], [d0, d1, d2, db]

    dgg, duu, g["cw0"], g["cw1"], g["cw2"], g["ffn_conv_b"] = _conv_call(
        conv_bwd, [gg, uu, dact], conv_consts, [BF16, BF16], 4, "ffn_conv_bwd")
    g["ffn_w_gate"] = _matmul(fn, dgg, "tn", F32, "ffn_gate_dw")
    g["ffn_w_up"] = _matmul(fn, duu, "tn", F32, "ffn_up_dw")
    dfn = _matmul(dgg, w["wg"], "nt", F32, "ffn_gate_dx")
    dfn = _matmul(duu, w["wu"], "nt", F32, "ffn_up_dx", residual=dfn)
    dh2, g["norm_ffn"] = _norm_bwd(h2, w["norm_ffn"], dfn, dy, "norm_ffn_bwd")

    g["xa_w_o"] = _matmul(xo, dh2, "tn", F32, "xa_o_dw")
    dxo = _matmul(dh2, w["xo"], "nt", F32, "xa_o_dx")

    def xa_bwd(r, k):
        kvh = _heads(k[0], 2 * XA_HEADS)
        dq_, dk_, dv_ = [], [], []
        dqn, dkn = 0.0, 0.0
        for h, (a, d_) in enumerate(zip(_heads(r[0], XA_HEADS), _heads(r[1], XA_HEADS))):
            _, vjp = jax.vjp(_xa_head, a, kvh[h], kvh[XA_HEADS + h], k[1], k[2])
            ga, gk, gv, gqn, gkn = vjp(d_)
            dq_.append(ga)
            dk_.append(gk)
            dv_.append(gv)
            dqn, dkn = dqn + gqn, dkn + gkn
        return [_cat(dq_)], [_cat(dk_ + dv_), dqn, dkn]

    dxq, dxkv, g["xa_q_norm"], g["xa_k_norm"] = _rows_call(
        xa_bwd, [_row(xq), _row(dxo)], [xkv, w["xa_q_norm"], w["xa_k_norm"]], [(512, BF16)],
        [xkv.shape, (1, 128), (1, 128)], name="xa_bwd")
    g["xa_w_q"] = _matmul(hn, dxq, "tn", F32, "xa_q_dw")
    dhn = _matmul(dxq, w["xq"], "nt", F32, "xa_q_dx")
    g["xa_w_kv"] = _matmul(mn, dxkv, "tn", F32, "xa_kv_dw")
    dmn = _matmul(dxkv, w["xkv"], "nt", F32, "xa_kv_dx")
    _, g["norm_mem"] = _norm_bwd(mem, w["norm_mem"], dmn, dmn, "norm_mem_bwd")
    dh1, g["norm_xa"] = _norm_bwd(h1, w["norm_xa"], dhn, dh2, "norm_xa_bwd")

    g["w_out"] = _matmul(cat, dh1, "tn", F32, "out_dw")
    dcat = _matmul(dh1, w["out"], "nt", F32, "out_dx")

    def mix_bwd(r, k):
        do_, dog_ = [], []
        dgn = 0.0
        for o, g_, d_ in zip(_heads(r[0], GLA_HEADS), _heads(r[1], GLA_HEADS), _heads(r[2], GLA_HEADS)):
            _, vjp = jax.vjp(_mix_head, o, g_, k[0])
            a, b, gn_ = vjp(d_)
            do_.append(a)
            dog_.append(b)
            dgn = dgn + gn_
        return [_cat(do_), _cat(dog_)], [dgn]

    do_gla, d_og, g["gla_out_norm"] = _rows_call(mix_bwd, [_row(o_gla), og, _row(dcat, 512, 0)], [w["gla_out_norm"]],
                                                 [(512, F32), (512, BF16)], [(1, 128)], name="mix_bwd")

    dq_r, dk_r, dv_mla = _attn_bwd(q_r, k_r, v_mla, o_mla, lse, dcat)

    def qk_bwd(r, k):
        dqs, dks = [], []
        dkpe, dqn, dkn = 0.0, 0.0, 0.0
        for qh, kh, dqh, dkh in zip(_heads(r[0], MLA_HEADS), _heads(r[1], MLA_HEADS), _heads(r[6], MLA_HEADS),
                                    _heads(r[7], MLA_HEADS)):
            _, vjp = jax.vjp(lambda a, b, e, f, h_: _qk_head(a, b, e, r[3], r[4], r[5], f, h_), qh, kh, r[2], k[0], k[1])
            ga, gb, ge, gf, gh = vjp((dqh, dkh))
            dqs.append(ga)
            dks.append(gb)
            dkpe, dqn, dkn = dkpe + ge, dqn + gf, dkn + gh
        return [_cat(dqs), _cat(dks), dkpe], [dqn, dkn]

    dq_up, dk_up, d_kpe, g["q_norm"], g["k_norm"] = _rows_call(
        qk_bwd, [_row(q_up), _row(k_up), kpe] + tabs + [_row(dq_r), _row(dk_r)], [w["q_norm"], w["k_norm"]],
        [(1024, BF16), (1024, BF16), (128, BF16)], [(1, 128), (1, 128)], name="mla_qk_bwd")
    g["uq"] = _matmul(q_lat, dq_up, "tn", F32, "mla_q_dw")
    dq_lat = _matmul(dq_up, w["uq"], "nt", F32, "mla_q_dx")
    g["k"] = _matmul(kv_lat, dk_up, "tn", F32, "mla_k_dw")
    g["v"] = _matmul(kv_lat, dv_mla, "tn", F32, "mla_v_dw")
    dkv_lat = _matmul(dk_up, w["k"], "nt", F32, "mla_k_dx")
    dkv_lat = _matmul(dv_mla, w["v"], "nt", F32, "mla_v_dx", residual=dkv_lat)

    def lat_bwd(r, k):
        _, vjp1 = jax.vjp(_rms, r[0], k[0])
        _, vjp2 = jax.vjp(_rms, r[1], k[1])
        a, ga = vjp1(r[2])
        b, gb = vjp2(r[3])
        return [a, b], [ga, gb]

    d_cq, d_ckv, g["mla_q_a_norm"], g["mla_kv_a_norm"] = _rows_call(
        lat_bwd, [cq, ckv, _row(dq_lat), _row(dkv_lat)], [w["q_a_norm"], w["kv_a_norm"]],
        [(256, BF16), (128, BF16)], [(1, 256), (1, 128)], name="mla_lat_bwd")

    dgq, dgk, dla, dgv = _gla_bwd(proj, la, states, do_gla)

    def gate_bwd(r, k):
        _, vjp = jax.vjp(_gate_fn, r[0], k[0], k[1])
        a, gw, gb = vjp(r[1])
        return [a], [gw, gb]

    d_alr, g["w2"], g["gla_gate_b"] = _rows_call(gate_bwd, [alr, _row(dla)], [w["w2"], w["gate_b"]], [(128, BF16)],
                                                 [(128, 256), (1, 256)], name="gla_gate_bwd")

    dproj = jnp.concatenate([dgq.astype(BF16), dgk.astype(BF16), dgv.astype(BF16), d_og, d_cq, d_ckv, d_kpe, d_alr],
                            axis=1)
    g["in"] = _matmul(xn, dproj, "tn", F32, "proj_dw")
    dxn = _matmul(dproj, w["in"], "nt", F32, "proj_dx")
    dx, g["norm_mix"] = _norm_bwd(x, w["norm_mix"], dxn, dh1, "norm_mix_bwd")
    return loss[0, 0], dx, g


def _to_kernel_layout(full):
    w_in = full["w_in"]
    z = lambda n: jnp.zeros((D_MODEL, n), w_in.dtype)
    seg = lambda lo, n: w_in[:, lo:lo + n]
    w = {
        "in": jnp.concatenate([seg(N_GQ, 256), seg(N_GK, 256), seg(N_GV, 512), seg(N_OG, 512), seg(N_CQ, 256),
                               seg(N_CKV, 128), z(64), seg(N_KPE, 32), z(32), seg(N_ALR, 16), z(112)], axis=1),
        "uq": jnp.pad(full["mla_w_uq"].reshape(MLA_Q_RANK, MLA_HEADS, MLA_QK),
                      ((0, 0), (0, 0), (0, LANES - MLA_QK))).reshape(MLA_Q_RANK, MLA_HEADS * LANES),
        "out": full["w_out"], "xq": full["xa_w_q"], "xkv": full["xa_w_kv"], "xo": full["xa_w_o"],
        "wg": full["ffn_w_gate"], "wu": full["ffn_w_up"], "wd": full["ffn_w_down"],
        "w2": jnp.pad(full["gla_gate_w2"], ((0, LANES - GLA_RANK), (0, 0))),
        "q_norm": jnp.pad(full["mla_q_norm"], ((0, 0), (0, LANES - MLA_QK))),
        "k_norm": jnp.pad(full["mla_k_norm"], ((0, 0), (0, LANES - MLA_QK))),
        "q_a_norm": full["mla_q_a_norm"], "kv_a_norm": full["mla_kv_a_norm"], "gate_b": full["gla_gate_b"],
        "cw0": full["ffn_conv_w"][0:1], "cw1": full["ffn_conv_w"][1:2], "cw2": full["ffn_conv_w"][2:3],
        "cb": full["ffn_conv_b"],
    }
    ukv = full["mla_w_ukv"].reshape(MLA_KV_RANK, MLA_HEADS, MLA_NOPE + MLA_V)
    w["k"] = jnp.pad(ukv[:, :, :MLA_NOPE], ((0, 0), (0, 0), (0, LANES - MLA_NOPE))).reshape(MLA_KV_RANK, -1)
    w["v"] = ukv[:, :, MLA_NOPE:].reshape(MLA_KV_RANK, MLA_HEADS * MLA_V)
    for n in ("norm_mix", "gla_out_norm", "norm_xa", "norm_mem", "xa_q_norm", "xa_k_norm", "norm_ffn"):
        w[n] = full[n]
    return w


def _to_reference_layout(g):
    gi = g["in"]
    seg = lambda lo, n: gi[:, lo:lo + n]
    out = {
        "w_in": jnp.concatenate([seg(P_GQ, 256), seg(P_GK, 256), seg(P_GV, 512), seg(P_ALR, 16), seg(P_OG, 512),
                                 seg(P_CQ, 256), seg(P_CKV, 128), seg(P_KPE + 64, 32)], axis=1),
        "gla_gate_w2": g["w2"][:GLA_RANK],
        "mla_w_uq": g["uq"].reshape(MLA_Q_RANK, MLA_HEADS, LANES)[:, :, :MLA_QK].reshape(MLA_Q_RANK, -1),
        "mla_w_ukv": jnp.concatenate([g["k"].reshape(MLA_KV_RANK, MLA_HEADS, LANES)[:, :, :MLA_NOPE],
                                      g["v"].reshape(MLA_KV_RANK, MLA_HEADS, MLA_V)], axis=2).reshape(MLA_KV_RANK, -1),
        "mla_q_norm": g["q_norm"][:, :MLA_QK], "mla_k_norm": g["k_norm"][:, :MLA_QK],
        "ffn_conv_w": jnp.concatenate([g["cw0"], g["cw1"], g["cw2"]], axis=0),
    }
    for n in WEIGHTS:
        if n not in out:
            out[n] = g[n]
    return out


def _pack(arrays, dtype, lead=()):
    flat = jnp.concatenate([a.astype(dtype).reshape(lead + (-1,)) for a in arrays], axis=-1)
    n = flat.shape[-1]
    unit = PACK_ROWS * LANES
    total = -(-n // unit) * unit
    flat = jnp.pad(flat, [(0, 0)] * len(lead) + [(0, total - n)])
    return flat.reshape(lead + (total // LANES, LANES))


def _unpack(buf, shapes, lead=()):
    flat = buf.reshape(lead + (-1,))
    out, off = [], 0
    for shp in shapes:
        n = 1
        for d in shp:
            n *= d
        out.append(flat[..., off:off + n].reshape(lead + tuple(shp)))
        off += n
    return out


def _join_shards(pieces, axis):
    if axis == 0:
        return pieces.reshape(-1, pieces.shape[2])
    return jnp.transpose(pieces, (1, 0, 2)).reshape(pieces.shape[1], -1)


def _split_shards(full, axis):
    r, c = full.shape
    if axis == 0:
        return full.reshape(4, r // 4, c)
    return jnp.transpose(full.reshape(r, 4, c // 4), (1, 0, 2))


ANY = pl.BlockSpec(memory_space=pl.ANY)


def _place():
    x, y, c = lax.axis_index("x"), lax.axis_index("y"), lax.axis_index("c")
    chips = [(1 - x, y), (x, 1 - y), (1 - x, 1 - y)]
    return x, y, c, chips


def _gather_chips(big, small):
    def body(b_ref, s_ref, bo_ref, so_ref, send_b, recv_b, send_s, recv_s, local):
        x, y, c, chips = _place()
        me = 2 * x + y
        own = [pltpu.make_async_copy(b_ref, bo_ref.at[me], local.at[0]),
               pltpu.make_async_copy(s_ref, so_ref.at[me], local.at[1])]
        for cp in own:
            cp.start()
        sends = []
        for j, (px, py) in enumerate(chips):
            for src, dst, ss, rs in ((b_ref, bo_ref, send_b, recv_b), (s_ref, so_ref, send_s, recv_s)):
                cp = pltpu.make_async_remote_copy(src_ref=src, dst_ref=dst.at[me], send_sem=ss.at[j], recv_sem=rs.at[j],
                                                  device_id=(px, py, c), device_id_type=MESH)
                cp.start()
                sends.append(cp)
        for j, (px, py) in enumerate(chips):
            for src, dst, ss, rs in ((b_ref, bo_ref, send_b, recv_b), (s_ref, so_ref, send_s, recv_s)):
                pltpu.make_async_remote_copy(src_ref=src, dst_ref=dst.at[2 * px + py], send_sem=ss.at[j],
                                             recv_sem=rs.at[j], device_id=(px, py, c), device_id_type=MESH).wait_recv()
        for cp in sends:
            cp.wait_send()
        for cp in own:
            cp.wait()

    return pl.pallas_call(
        body, in_specs=[ANY, ANY], out_specs=[ANY, ANY],
        out_shape=[jax.ShapeDtypeStruct((4,) + big.shape, big.dtype), jax.ShapeDtypeStruct((4,) + small.shape, small.dtype)],
        scratch_shapes=[pltpu.SemaphoreType.DMA((3,))] * 4 + [pltpu.SemaphoreType.DMA((2,))],
        name="gather_weights")(big, small)


def _scatter_partials(big, small):
    def body(b_ref, s_ref, bo_ref, so_ref, send_b, recv_b, send_s, recv_s, local):
        x, y, c, chips = _place()
        me = 2 * x + y
        dev = 4 * x + 2 * y + c
        own = [pltpu.make_async_copy(b_ref.at[me], bo_ref.at[3], local.at[0]),
               pltpu.make_async_copy(s_ref, so_ref.at[dev], local.at[1])]
        for cp in own:
            cp.start()
        sends = []
        for j, (px, py) in enumerate(chips):
            cp = pltpu.make_async_remote_copy(src_ref=b_ref.at[2 * px + py], dst_ref=bo_ref.at[j], send_sem=send_b.at[j],
                                              recv_sem=recv_b.at[j], device_id=(px, py, c), device_id_type=MESH)
            cp.start()
            sends.append(cp)
        peers = []
        for k in range(1, 8):
            bx, by, bc = (k >> 2) & 1, (k >> 1) & 1, k & 1
            px, py, pc = (1 - x) if bx else x, (1 - y) if by else y, (1 - c) if bc else c
            peers.append((px, py, pc))
            cp = pltpu.make_async_remote_copy(src_ref=s_ref, dst_ref=so_ref.at[dev], send_sem=send_s.at[k - 1],
                                              recv_sem=recv_s.at[k - 1], device_id=(px, py, pc), device_id_type=MESH)
            cp.start()
            sends.append(cp)
        for j, (px, py) in enumerate(chips):
            pltpu.make_async_remote_copy(src_ref=b_ref.at[me], dst_ref=bo_ref.at[j], send_sem=send_b.at[j],
                                         recv_sem=recv_b.at[j], device_id=(px, py, c), device_id_type=MESH).wait_recv()
        for k, (px, py, pc) in enumerate(peers):
            pltpu.make_async_remote_copy(src_ref=s_ref, dst_ref=so_ref.at[4 * px + 2 * py + pc], send_sem=send_s.at[k],
                                         recv_sem=recv_s.at[k], device_id=(px, py, pc), device_id_type=MESH).wait_recv()
        for cp in sends:
            cp.wait_send()
        for cp in own:
            cp.wait()

    return pl.pallas_call(
        body, in_specs=[ANY, ANY], out_specs=[ANY, ANY],
        out_shape=[jax.ShapeDtypeStruct(big.shape, big.dtype), jax.ShapeDtypeStruct((8,) + small.shape, small.dtype)],
        scratch_shapes=[pltpu.SemaphoreType.DMA((3,))] * 2 + [pltpu.SemaphoreType.DMA((7,))] * 2
        + [pltpu.SemaphoreType.DMA((2,))],
        name="scatter_partials")(big, small)


def _swap_cores(half):
    def body(h_ref, o_ref, send, recv):
        x, y, c, _ = _place()
        cp = pltpu.make_async_remote_copy(src_ref=h_ref, dst_ref=o_ref, send_sem=send, recv_sem=recv,
                                          device_id=(x, y, 1 - c), device_id_type=MESH)
        cp.start()
        cp.wait()

    return pl.pallas_call(body, in_specs=[ANY], out_specs=ANY, out_shape=jax.ShapeDtypeStruct(half.shape, half.dtype),
                          scratch_shapes=[pltpu.SemaphoreType.DMA(())] * 2, name="swap_cores")(half)


def _sum_slots(buf, name):
    n, r, _ = buf.shape
    t = _tile(r, PACK_ROWS)

    def kern(b_ref, o_ref):
        acc = b_ref[0].astype(F32)
        for i in range(1, n):
            acc = acc + b_ref[i].astype(F32)
        o_ref[...] = acc

    return pl.pallas_call(
        kern, grid=(r // t,), in_specs=[pl.BlockSpec((n, t, LANES), lambda i: (0, i, 0))],
        out_specs=pl.BlockSpec((t, LANES), lambda i: (i, 0)), out_shape=jax.ShapeDtypeStruct((r, LANES), F32),
        compiler_params=pltpu.CompilerParams(dimension_semantics=("parallel",)), name=name)(buf)


def _adamw(w, m, v, g_parts, name):
    r = w.shape[0]
    t = _tile(r, PACK_ROWS)
    npart = len(g_parts)

    def kern(*refs):
        w_ref, m_ref, v_ref = refs[:3]
        g_out, d_out, m_out, v_out = refs[3 + npart:]
        g = refs[3][...]
        for p_ref in refs[4:3 + npart]:
            g = g + p_ref[...]
        m_new = ADAM_B1 * m_ref[...] + (1.0 - ADAM_B1) * g
        v_new = ADAM_B2 * v_ref[...] + (1.0 - ADAM_B2) * (g * g)
        m_hat = m_new / (1.0 - ADAM_B1 ** ADAM_STEP)
        v_hat = v_new / (1.0 - ADAM_B2 ** ADAM_STEP)
        g_out[...] = g
        d_out[...] = -ADAM_LR * (m_hat / (jnp.sqrt(v_hat) + ADAM_EPS) + ADAM_WD * w_ref[...])
        m_out[...] = m_new
        v_out[...] = v_new

    spec = pl.BlockSpec((t, LANES), lambda i: (i, 0))
    return pl.pallas_call(
        kern, grid=(r // t,), in_specs=[spec] * (3 + npart), out_specs=[spec] * 4,
        out_shape=[jax.ShapeDtypeStruct((r, LANES), F32)] * 4,
        compiler_params=pltpu.CompilerParams(dimension_semantics=("parallel",)), name=name)(w, m, v, *g_parts)


def _step(a):
    sq = lambda n: a[n][0] if a[n].ndim == 3 else a[n]
    sh_names = [n for n, _ in SHARDED]
    sh_axis = dict(SHARDED)

    big = _pack([sq(n) for n in sh_names], BF16)
    small = _pack([sq(n) for n in EXACT_GATHER], F32)
    big_all, small_all = _gather_chips(big, small)
    full = {}
    for n, pieces in zip(sh_names, _unpack(big_all, [sq(n).shape for n in sh_names], lead=(4,)), strict=True):
        full[n] = _join_shards(pieces, sh_axis[n])
    for n, pieces in zip(EXACT_GATHER, _unpack(small_all, [sq(n).shape for n in EXACT_GATHER], lead=(4,)), strict=True):
        full[n] = _join_shards(pieces, sh_axis[n])
    for n in REPLICATED:
        full[n] = sq(n)

    loss, dx, g = _local_step(sq("x"), sq("mem"), a["positions"][0], sq("loss_target"), _to_kernel_layout(full))
    g = _to_reference_layout(g)

    part_big = _pack([_split_shards(g[n], sh_axis[n]) for n in sh_names], BF16, lead=(4,))
    part_small = _pack([g[n] for n in REPLICATED], F32)
    land_big, land_small = _scatter_partials(part_big, part_small)
    half = _sum_slots(land_big, "sum_core_group")
    other = _swap_cores(half)
    g_small = _sum_slots(land_small, "sum_replicated")

    outs = {}
    for names, parts, tag in ((sh_names, [half, other], "adamw_sharded"), (list(REPLICATED), [g_small], "adamw_replicated")):
        shapes = [sq(n).shape for n in names]
        packed = [_pack([sq(p + n) for n in names], F32) for p in ("", "m_", "v_")]
        res = _adamw(*packed, parts, tag)
        for kind, buf in zip(("grad_", "delta_", "new_m_", "new_v_"), res, strict=True):
            for n, val in zip(names, _unpack(buf, shapes), strict=True):
                outs[kind + n] = val.reshape(a[n].shape)

    loss = lax.psum(loss, ("x", "y", "c"))
    ordered = [outs[kind + n] for kind in ("grad_", "delta_", "new_m_", "new_v_") for n in WEIGHTS]
    return (loss, dx[None], *ordered)


def kernel(x, mem, positions, norm_mix, w_in, gla_gate_w2, gla_gate_b, gla_out_norm, mla_q_a_norm, mla_w_uq, mla_kv_a_norm, mla_w_ukv, mla_q_norm, mla_k_norm, w_out, norm_xa, norm_mem, xa_w_q, xa_w_kv, xa_q_norm, xa_k_norm, xa_w_o, norm_ffn, ffn_w_gate, ffn_w_up, ffn_conv_w, ffn_conv_b, ffn_w_down, loss_target, m_norm_mix, m_w_in, m_gla_gate_w2, m_gla_gate_b, m_gla_out_norm, m_mla_q_a_norm, m_mla_w_uq, m_mla_kv_a_norm, m_mla_w_ukv, m_mla_q_norm, m_mla_k_norm, m_w_out, m_norm_xa, m_norm_mem, m_xa_w_q, m_xa_w_kv, m_xa_q_norm, m_xa_k_norm, m_xa_w_o, m_norm_ffn, m_ffn_w_gate, m_ffn_w_up, m_ffn_conv_w, m_ffn_conv_b, m_ffn_w_down, v_norm_mix, v_w_in, v_gla_gate_w2, v_gla_gate_b, v_gla_out_norm, v_mla_q_a_norm, v_mla_w_uq, v_mla_kv_a_norm, v_mla_w_ukv, v_mla_q_norm, v_mla_k_norm, v_w_out, v_norm_xa, v_norm_mem, v_xa_w_q, v_xa_w_kv, v_xa_q_norm, v_xa_k_norm, v_xa_w_o, v_norm_ffn, v_ffn_w_gate, v_ffn_w_up, v_ffn_conv_w, v_ffn_conv_b, v_ffn_w_down):
    return _step(dict(locals()))
```

```python
import functools

import jax
import jax.numpy as jnp
from jax import lax
from jax.experimental import pallas as pl
from jax.experimental.pallas import tpu as pltpu

F32, BF16 = jnp.float32, jnp.bfloat16
MESH = pl.DeviceIdType.MESH

D_MODEL = 1024
EPS = 1e-6
GLA_HEADS, GLA_DK, GLA_DV, GLA_RANK, GLA_CHUNK = 4, 64, 128, 16, 64
GLA_GATE_NORM = 16.0
MLA_HEADS, MLA_Q_RANK, MLA_KV_RANK, MLA_NOPE, MLA_ROPE, MLA_V = 8, 256, 128, 64, 32, 64
MLA_QK = MLA_NOPE + MLA_ROPE
ROPE_THETA = 10000.0
XA_HEADS, XA_DIM = 4, 128
D_FF = 2816
ADAM_LR, ADAM_B1, ADAM_B2, ADAM_EPS, ADAM_WD, ADAM_STEP = 0.001, 0.9, 0.999, 1e-08, 0.01, 10

LANES = 128
VMEM_LIMIT = 56 * 1024 * 1024

P_GQ, P_GK, P_GV, P_OG, P_CQ, P_CKV, P_KPE, P_ALR, P_WIDTH = 0, 256, 512, 1024, 1536, 1792, 1920, 2048, 2176
N_GQ, N_GK, N_GV, N_ALR, N_OG, N_CQ, N_CKV, N_KPE, N_WIDTH = 0, 256, 512, 1024, 1040, 1552, 1808, 1936, 1968

SHARDED = (("w_in", 1), ("gla_gate_w2", 1), ("mla_w_uq", 1), ("mla_w_ukv", 1), ("w_out", 0), ("xa_w_q", 0),
           ("xa_w_kv", 0), ("xa_w_o", 1), ("ffn_w_gate", 1), ("ffn_w_up", 1), ("ffn_conv_w", 1), ("ffn_w_down", 0))
REPLICATED = ("norm_mix", "gla_gate_b", "gla_out_norm", "mla_q_a_norm", "mla_kv_a_norm", "mla_q_norm", "mla_k_norm",
              "norm_xa", "norm_mem", "xa_q_norm", "xa_k_norm", "norm_ffn", "ffn_conv_b")
EXACT_GATHER = ("gla_gate_w2", "ffn_conv_w")
WEIGHTS = ("norm_mix", "w_in", "gla_gate_w2", "gla_gate_b", "gla_out_norm", "mla_q_a_norm", "mla_w_uq",
           "mla_kv_a_norm", "mla_w_ukv", "mla_q_norm", "mla_k_norm", "w_out", "norm_xa", "norm_mem", "xa_w_q",
           "xa_w_kv", "xa_q_norm", "xa_k_norm", "xa_w_o", "norm_ffn", "ffn_w_gate", "ffn_w_up", "ffn_conv_w",
           "ffn_conv_b", "ffn_w_down")


_NN = ((1,), (0,))
_NT = ((1,), (1,))
_TN = ((0,), (0,))


def _dg(a, b, dims):
    return lax.dot_general(a.astype(BF16), b.astype(BF16), (dims, ((), ())), preferred_element_type=F32)


@jax.custom_vjp
def _dot_nn(a, b):
    return _dg(a, b, _NN)


_dot_nn.defvjp(lambda a, b: (_dg(a, b, _NN), (a, b)),
               lambda r, g: (_dg(g, r[1], _NT).astype(r[0].dtype), _dg(r[0], g, _TN).astype(r[1].dtype)))


@jax.custom_vjp
def _dot_nt(a, b):
    return _dg(a, b, _NT)


_dot_nt.defvjp(lambda a, b: (_dg(a, b, _NT), (a, b)),
               lambda r, g: (_dg(g, r[1], _NN).astype(r[0].dtype), _dg(g, r[0], _TN).astype(r[1].dtype)))


@jax.custom_vjp
def _dot_tn(a, b):
    return _dg(a, b, _TN)


_dot_tn.defvjp(lambda a, b: (_dg(a, b, _TN), (a, b)),
               lambda r, g: (_dg(r[1], g, _NT).astype(r[0].dtype), _dg(r[0], g, _NN).astype(r[1].dtype)))


def _rms(x, w, n=None):
    n = x.shape[-1] if n is None else n
    ms = jnp.sum(x * x, axis=-1, keepdims=True) * (1.0 / n)
    return x * lax.rsqrt(ms + EPS) * w


def _silu(x):
    return x * jax.nn.sigmoid(x)


def _log_sigmoid(x):
    return jnp.minimum(x, 0.0) - jnp.log(1.0 + jnp.exp(-jnp.abs(x)))


@jax.custom_vjp
def _rope(y, c, sa, sb):
    return y * c + pltpu.roll(y, LANES - 16, 1) * sa + pltpu.roll(y, 16, 1) * sb


def _rope_bwd(res, g):
    c, sa, sb = res
    gy = g * c + pltpu.roll(g * sa, 16, 1) + pltpu.roll(g * sb, LANES - 16, 1)
    return gy, jnp.zeros_like(c), jnp.zeros_like(sa), jnp.zeros_like(sb)


_rope.defvjp(lambda y, c, sa, sb: (_rope(y, c, sa, sb), (c, sa, sb)), _rope_bwd)


def _lane_mask(lo, hi):
    lane = lax.broadcasted_iota(jnp.int32, (1, LANES), 1)
    return ((lane >= lo) & (lane < hi)).astype(F32)


def _tile(n, t):
    t = min(n, t)
    assert n % t == 0, (n, t)
    return t


def _matmul(a, b, mode, out_dtype, name, residual=None, a_lead=None, b_lead=None):
    (a0, a1), (b0, b1) = a.shape[-2:], b.shape[-2:]
    if mode == "nn":
        m, k, k2, n = a0, a1, b0, b1
    elif mode == "nt":
        m, k, n, k2 = a0, a1, b0, b1
    else:
        k, m, k2, n = a0, a1, b0, b1
    assert k == k2, (a.shape, b.shape, mode)
    npar = 4 if "p" in (a_lead, b_lead) else 1
    nsum = 4 if "k" in (a_lead, b_lead) else 1
    if mode == "tn":
        tm = m if m <= 1408 else m // 2
        tn = n if tm * n <= 1024 * 2304 else n // 2
        tk = _tile(k, 512)
    else:
        tm, tn, tk = _tile(m, 512), n, k
    assert m % tm == 0 and n % tn == 0 and k % tk == 0
    nk = k // tk
    dims = {"nn": _NN, "nt": _NT, "tn": _TN}[mode]

    def body(*refs):
        if residual is None:
            a_ref, b_ref, o_ref, acc = refs
        else:
            a_ref, b_ref, r_ref, o_ref, acc = refs
        ks, kk = pl.program_id(3), pl.program_id(4)

        @pl.when((ks == 0) & (kk == 0))
        def _():
            acc[...] = jnp.zeros_like(acc)

        acc[...] += _dg(a_ref[...], b_ref[...], dims)

        @pl.when((ks == nsum - 1) & (kk == nk - 1))
        def _():
            r = acc[...]
            if residual is not None:
                r = r + r_ref[...]
            o_ref[...] = r.astype(o_ref.dtype)

    def spec(lead, blk, idx):
        if lead is None:
            return pl.BlockSpec(blk, lambda p, i, j, ks, kk: idx(i, j, kk))
        if lead == "p":
            return pl.BlockSpec((None,) + blk, lambda p, i, j, ks, kk: (p,) + idx(i, j, kk))
        return pl.BlockSpec((None,) + blk, lambda p, i, j, ks, kk: (ks,) + idx(i, j, kk))

    if mode == "nn":
        in_specs = [spec(a_lead, (tm, tk), lambda i, j, kk: (i, kk)), spec(b_lead, (tk, tn), lambda i, j, kk: (kk, j))]
    elif mode == "nt":
        in_specs = [spec(a_lead, (tm, tk), lambda i, j, kk: (i, kk)), spec(b_lead, (tn, tk), lambda i, j, kk: (j, kk))]
    else:
        in_specs = [spec(a_lead, (tk, tm), lambda i, j, kk: (kk, i)), spec(b_lead, (tk, tn), lambda i, j, kk: (kk, j))]
    args = [a, b]
    if residual is not None:
        assert npar == 1
        in_specs.append(spec(None, (tm, tn), lambda i, j, kk: (i, j)))
        args.append(residual)
    out_lead = "p" if npar > 1 else None
    return pl.pallas_call(
        body, grid=(npar, m // tm, n // tn, nsum, nk), in_specs=in_specs,
        out_specs=spec(out_lead, (tm, tn), lambda i, j, kk: (i, j)),
        out_shape=jax.ShapeDtypeStruct(((4,) if npar > 1 else ()) + (m, n), out_dtype),
        scratch_shapes=[pltpu.VMEM((tm, tn), F32)],
        compiler_params=pltpu.CompilerParams(
            dimension_semantics=("parallel", "parallel", "parallel", "arbitrary", "arbitrary"),
            vmem_limit_bytes=VMEM_LIMIT),
        name=name)(*args)


def _row(a, width=None, col_block=0):
    return (a, a.shape[1] if width is None else width, col_block)


def _rows_call(body, rows, consts, outs, accs=(), *, name, tile=512):
    s = rows[0][0].shape[0]
    t = _tile(s, tile)
    nr, nc, no = len(rows), len(consts), len(outs)

    def kern(*refs):
        r = [x[...] for x in refs[:nr]]
        c = [x[...] for x in refs[nr:nr + nc]]
        o_refs = refs[nr + nc:nr + nc + no]
        a_refs = refs[nr + nc + no:]
        ro, ao = body(r, c)
        for ref, val in zip(o_refs, ro, strict=True):
            ref[...] = val.astype(ref.dtype)
        if a_refs:
            @pl.when(pl.program_id(0) == 0)
            def _():
                for ref in a_refs:
                    ref[...] = jnp.zeros_like(ref)

            for ref, val in zip(a_refs, ao, strict=True):
                ref[...] += val

    in_specs = [pl.BlockSpec((t, w), functools.partial(lambda cb, i: (i, cb), cb)) for (_, w, cb) in rows]
    in_specs += [pl.BlockSpec(c.shape, lambda i: (0, 0)) for c in consts]
    out_specs = [pl.BlockSpec((t, w), lambda i: (i, 0)) for (w, _) in outs]
    out_specs += [pl.BlockSpec(shape, lambda i: (0, 0)) for shape in accs]
    out_shape = [jax.ShapeDtypeStruct((s, w), dt) for (w, dt) in outs]
    out_shape += [jax.ShapeDtypeStruct(shape, F32) for shape in accs]
    return pl.pallas_call(
        kern, grid=(s // t,), in_specs=in_specs, out_specs=out_specs, out_shape=out_shape,
        compiler_params=pltpu.CompilerParams(dimension_semantics=("arbitrary" if accs else "parallel",),
                                             vmem_limit_bytes=VMEM_LIMIT),
        name=name)(*[r[0] for r in rows], *consts)


def _gla_chunk(q, k, la, v0, v1, s0, s1):
    c = q.shape[0]
    r = lax.broadcasted_iota(jnp.int32, (c, c), 0)
    cc = lax.broadcasted_iota(jnp.int32, (c, c), 1)
    tril = cc <= r
    cum = lax.dot_general(tril.astype(F32), la, (_NN, ((), ())), precision=lax.Precision.HIGHEST,
                          preferred_element_type=F32)
    cl = jnp.sum(la, axis=0, keepdims=True)
    qd = q * (GLA_DK ** -0.5) * jnp.exp(cum)
    ki = k * jnp.exp(-cum)
    ke = k * jnp.exp(cl - cum)
    dec = jnp.exp(cl)
    outs, news = [], []
    for h, (v, s) in enumerate(((v0, s0), (v1, s1))):
        mk = _lane_mask(GLA_DK * h, GLA_DK * (h + 1))
        qh = qd * mk
        att = jnp.where(tril, _dot_nt(qh, ki), 0.0)
        outs.append(_dot_nn(att, v) + _dot_nt(qh, s))
        news.append(s * dec + _dot_tn(v, ke * mk))
    return outs[0], outs[1], news[0], news[1]


def _gla_specs(tb, rev_nb=None):
    blk = (lambda b: b) if rev_nb is None else (lambda b: rev_nb - 1 - b)
    q = pl.BlockSpec((tb, 128), lambda p, b: (blk(b), P_GQ // 128 + p))
    k = pl.BlockSpec((tb, 128), lambda p, b: (blk(b), P_GK // 128 + p))
    la = pl.BlockSpec((tb, 128), lambda p, b: (blk(b), p))
    v = pl.BlockSpec((tb, 256), lambda p, b: (blk(b), P_GV // 256 + p))
    o = pl.BlockSpec((tb, 256), lambda p, b: (blk(b), p))
    st = pl.BlockSpec((tb // GLA_CHUNK, 2, 128, 128), lambda p, b: (blk(b), p, 0, 0))
    return q, k, la, v, o, st


def _gla_fwd(proj, la):
    s = proj.shape[0]
    tb = _tile(s, 512)
    nb, nch = s // tb, tb // GLA_CHUNK

    def kern(q_ref, k_ref, la_ref, v_ref, o_ref, st_ref, s_sc):
        @pl.when(pl.program_id(1) == 0)
        def _():
            s_sc[...] = jnp.zeros_like(s_sc)

        s0, s1 = s_sc[0], s_sc[1]
        for ci in range(nch):
            sl = slice(ci * GLA_CHUNK, (ci + 1) * GLA_CHUNK)
            st_ref[ci, 0] = s0
            st_ref[ci, 1] = s1
            o0, o1, s0, s1 = _gla_chunk(q_ref[sl, :], k_ref[sl, :], la_ref[sl, :], v_ref[sl, 0:128],
                                        v_ref[sl, 128:256], s0, s1)
            o_ref[sl, 0:128] = o0
            o_ref[sl, 128:256] = o1
        s_sc[0] = s0
        s_sc[1] = s1

    q, k, lasp, v, o, st = _gla_specs(tb)
    return pl.pallas_call(
        kern, grid=(2, nb), in_specs=[q, k, lasp, v], out_specs=[o, st],
        out_shape=[jax.ShapeDtypeStruct((s, 512), F32),
                   jax.ShapeDtypeStruct((s // GLA_CHUNK, GLA_HEADS, 128, 128), F32)],
        scratch_shapes=[pltpu.VMEM((2, 128, 128), F32)],
        compiler_params=pltpu.CompilerParams(dimension_semantics=("parallel", "arbitrary"),
                                             vmem_limit_bytes=VMEM_LIMIT),
        name="gla_fwd")(proj, proj, la, proj)


def _gla_bwd(proj, la, states, d_o):
    s = proj.shape[0]
    tb = _tile(s, 512)
    nb, nch = s // tb, tb // GLA_CHUNK

    def kern(q_ref, k_ref, la_ref, v_ref, do_ref, st_ref, dq_ref, dk_ref, dla_ref, dv_ref, ds_sc):
        @pl.when(pl.program_id(1) == 0)
        def _():
            ds_sc[...] = jnp.zeros_like(ds_sc)

        d0, d1 = ds_sc[0], ds_sc[1]
        for ci in reversed(range(nch)):
            sl = slice(ci * GLA_CHUNK, (ci + 1) * GLA_CHUNK)
            _, vjp = jax.vjp(_gla_chunk, q_ref[sl, :], k_ref[sl, :], la_ref[sl, :], v_ref[sl, 0:128],
                             v_ref[sl, 128:256], st_ref[ci, 0], st_ref[ci, 1])
            gq, gk, gla, gv0, gv1, d0, d1 = vjp((do_ref[sl, 0:128], do_ref[sl, 128:256], d0, d1))
            dq_ref[sl, :] = gq
            dk_ref[sl, :] = gk
            dla_ref[sl, :] = gla
            dv_ref[sl, 0:128] = gv0
            dv_ref[sl, 128:256] = gv1
        ds_sc[0] = d0
        ds_sc[1] = d1

    q, k, lasp, v, o, st = _gla_specs(tb, rev_nb=nb)
    return pl.pallas_call(
        kern, grid=(2, nb), in_specs=[q, k, lasp, v, o, st], out_specs=[lasp, lasp, lasp, o],
        out_shape=[jax.ShapeDtypeStruct((s, 256), F32), jax.ShapeDtypeStruct((s, 256), F32),
                   jax.ShapeDtypeStruct((s, 256), F32), jax.ShapeDtypeStruct((s, 512), F32)],
        scratch_shapes=[pltpu.VMEM((2, 128, 128), F32)],
        compiler_params=pltpu.CompilerParams(dimension_semantics=("parallel", "arbitrary"),
                                             vmem_limit_bytes=VMEM_LIMIT),
        name="gla_bwd")(proj, proj, la, proj, d_o, states)


def _causal_keep(t, qi, ki):
    row = lax.broadcasted_iota(jnp.int32, (t, t), 0) + qi * t
    col = lax.broadcasted_iota(jnp.int32, (t, t), 1) + ki * t
    return col <= row


def _attn_fwd(q, k, v, tile=512):
    s = q.shape[0]
    t = _tile(s, tile)
    n = s // t

    def kern(q_ref, k_ref, v_ref, o_ref, lse_ref, m_sc, l_sc, acc_sc):
        qi, ki = pl.program_id(1), pl.program_id(2)
        first = lax.broadcasted_iota(jnp.int32, (t, LANES), 1) < MLA_V

        @pl.when(ki == 0)
        def _():
            m_sc[...] = jnp.full_like(m_sc, -jnp.inf)
            l_sc[...] = jnp.zeros_like(l_sc)
            acc_sc[...] = jnp.zeros_like(acc_sc)

        @pl.when(ki <= qi)
        def _():
            keep = _causal_keep(t, qi, ki)
            alphas, pvs = [], []
            for h in range(2):
                sc = _dg(q_ref[:, 128 * h:128 * (h + 1)], k_ref[:, 128 * h:128 * (h + 1)], _NT)
                sc = jnp.where(keep, sc, -jnp.inf)
                m_prev = m_sc[h]
                m_new = jnp.maximum(m_prev, jnp.max(sc, axis=1, keepdims=True))
                alpha = jnp.exp(m_prev - m_new)
                p = jnp.exp(sc - m_new[:, 0:1])
                l_sc[h] = alpha * l_sc[h] + jnp.sum(p, axis=1, keepdims=True)
                m_sc[h] = m_new
                alphas.append(alpha)
                pvs.append(_dg(p, v_ref[...], _NN))
            acc_sc[...] = acc_sc[...] * jnp.where(first, alphas[0], alphas[1]) + jnp.where(first, pvs[0], pvs[1])

        @pl.when(ki == qi)
        def _():
            l = jnp.where(first, l_sc[0], l_sc[1])
            m = jnp.where(first, m_sc[0], m_sc[1])
            o_ref[...] = acc_sc[...] / l
            lse_ref[...] = m + jnp.log(l)

    kv_idx = lambda p, qi, ki: (jnp.minimum(ki, qi), p)
    return pl.pallas_call(
        kern, grid=(MLA_HEADS // 2, n, n),
        in_specs=[pl.BlockSpec((t, 256), lambda p, qi, ki: (qi, p)), pl.BlockSpec((t, 256), kv_idx),
                  pl.BlockSpec((t, 128), kv_idx)],
        out_specs=[pl.BlockSpec((t, 128), lambda p, qi, ki: (qi, p)), pl.BlockSpec((t, 128), lambda p, qi, ki: (qi, p))],
        out_shape=[jax.ShapeDtypeStruct((s, 512), F32), jax.ShapeDtypeStruct((s, 512), F32)],
        scratch_shapes=[pltpu.VMEM((2, t, LANES), F32), pltpu.VMEM((2, t, LANES), F32), pltpu.VMEM((t, LANES), F32)],
        compiler_params=pltpu.CompilerParams(dimension_semantics=("parallel", "parallel", "arbitrary"),
                                             vmem_limit_bytes=VMEM_LIMIT),
        name="mla_attn_fwd")(q, k, v)


def _attn_bwd(q, k, v, o, lse, dcat, tile=512):
    s = q.shape[0]
    t = _tile(s, tile)
    n = s // t

    def kern(q_ref, k_ref, v_ref, o_ref, lse_ref, do_ref, dq_ref, dk_ref, dv_ref, dk_sc, dv_sc):
        ki, qi = pl.program_id(1), pl.program_id(2)

        @pl.when((ki == 0) & (qi == 0))
        def _():
            dq_ref[...] = jnp.zeros_like(dq_ref)

        @pl.when(qi == ki)
        def _():
            dk_sc[...] = jnp.zeros_like(dk_sc)
            dv_sc[...] = jnp.zeros_like(dv_sc)

        @pl.when(qi >= ki)
        def _():
            keep = _causal_keep(t, qi, ki)
            d_o = do_ref[...]
            prod = d_o * o_ref[...]
            rows = pl.ds(pl.multiple_of(qi * t, t), t)
            for h in range(2):
                hs = slice(128 * h, 128 * (h + 1))
                mk = _lane_mask(MLA_V * h, MLA_V * (h + 1))
                qh, kh = q_ref[:, hs], k_ref[:, hs]
                sc = jnp.where(keep, _dg(qh, kh, _NT), -jnp.inf)
                p = jnp.exp(sc - lse_ref[:, MLA_V * h:MLA_V * h + 1])
                doh = d_o * mk
                dp = _dg(doh, v_ref[...], _NT)
                delta = jnp.sum(prod * mk, axis=1, keepdims=True)
                ds = p * (dp - delta)
                dv_sc[...] += _dg(p, doh, _TN)
                dk_sc[:, hs] += _dg(ds, qh, _TN)
                dq_ref[rows, hs] += _dg(ds, kh, _NN)

        @pl.when(qi == n - 1)
        def _():
            dk_ref[...] = dk_sc[...]
            dv_ref[...] = dv_sc[...].astype(dv_ref.dtype)

    q_idx = lambda p, ki, qi: (jnp.maximum(qi, ki), p)
    return pl.pallas_call(
        kern, grid=(MLA_HEADS // 2, n, n),
        in_specs=[pl.BlockSpec((t, 256), q_idx), pl.BlockSpec((t, 256), lambda p, ki, qi: (ki, p)),
                  pl.BlockSpec((t, 128), lambda p, ki, qi: (ki, p)), pl.BlockSpec((t, 128), q_idx),
                  pl.BlockSpec((t, 128), q_idx),
                  pl.BlockSpec((t, 128), lambda p, ki, qi: (jnp.maximum(qi, ki), 4 + p))],
        out_specs=[pl.BlockSpec((s, 256), lambda p, ki, qi: (0, p)), pl.BlockSpec((t, 256), lambda p, ki, qi: (ki, p)),
                   pl.BlockSpec((t, 128), lambda p, ki, qi: (ki, p))],
        out_shape=[jax.ShapeDtypeStruct((s, 1024), F32), jax.ShapeDtypeStruct((s, 1024), F32),
                   jax.ShapeDtypeStruct((s, 512), BF16)],
        scratch_shapes=[pltpu.VMEM((t, 256), F32), pltpu.VMEM((t, 128), F32)],
        compiler_params=pltpu.CompilerParams(dimension_semantics=("parallel", "arbitrary", "arbitrary"),
                                             vmem_limit_bytes=VMEM_LIMIT),
        name="mla_attn_bwd")(q, k, v, o, lse, dcat)


def _gate_fn(alr, w2, b):
    return _log_sigmoid(_dot_nn(alr, w2) + b) * (1.0 / GLA_GATE_NORM)


def _qk_head(qh, kh, kpe, c, sa, sb, qn, kn):
    kfull = kh + kpe * _lane_mask(MLA_NOPE, MLA_QK)
    q_r = _rope(_rms(qh, qn, MLA_QK), c, sa, sb) * (MLA_QK ** -0.5)
    k_r = _rope(_rms(kfull, kn, MLA_QK), c, sa, sb)
    return q_r, k_r


def _mix_head(o, og, gn):
    return _rms(o, gn) * _silu(og)


def _xa_head(xq, xk, xv, qn, kn):
    sc = _dot_nt(_rms(xq, qn), _rms(xk, kn)) * (XA_DIM ** -0.5)
    e = jnp.exp(sc - lax.stop_gradient(jnp.max(sc, axis=1, keepdims=True)))
    p = e / jnp.sum(e, axis=1, keepdims=True)
    return _dot_nn(p, xv)


def _heads(x, n):
    return [x[:, 128 * h:128 * (h + 1)] for h in range(n)]


def _cat(xs):
    return jnp.concatenate(xs, axis=1)


def _norm_fwd(x, w, name):
    return _rows_call(lambda r, c: ([_rms(r[0], c[0])], []), [_row(x)], [w], [(x.shape[1], BF16)], name=name)[0]


def _norm_bwd(x, w, d_out, add, name):
    def body(r, c):
        _, vjp = jax.vjp(_rms, r[0], c[0])
        dx, dw = vjp(r[1])
        return [dx + r[2]], [dw]

    return _rows_call(body, [_row(x), _row(d_out), _row(add)], [w], [(x.shape[1], F32)], [w.shape], name=name)


CONV_HALO = 8


def _conv_specs(s, f, t):
    n8 = t // CONV_HALO
    cur = pl.BlockSpec((None, t, f), lambda j, i: (j, i, 0))
    prev = pl.BlockSpec((None, CONV_HALO, f), lambda j, i: (j, jnp.maximum(i * n8 - 1, 0), 0))
    nxt = pl.BlockSpec((None, CONV_HALO, f), lambda j, i: (j, jnp.minimum((i + 1) * n8, s // CONV_HALO - 1), 0))
    cw = pl.BlockSpec((None, 3, f), lambda j, i: (j, 0, 0))
    cb = pl.BlockSpec((None, 1, f), lambda j, i: (j, 0, 0))
    return cur, prev, nxt, cw, cb


def _conv_taps(g, prev, first):
    ext = jnp.concatenate([jnp.where(first, 0.0, prev), g], axis=0)
    return pltpu.roll(ext, 1, 0)[CONV_HALO:], pltpu.roll(ext, 2, 0)[CONV_HALO:]


def _conv_fwd(gg, uu, cw, cb):
    _, s, f = gg.shape
    t = _tile(s, 512)

    def kern(g_ref, gp_ref, u_ref, cw_ref, cb_ref, o_ref):
        g = g_ref[...]
        g1, g2 = _conv_taps(g, gp_ref[...], pl.program_id(1) == 0)
        w = cw_ref[...]
        gc = cb_ref[...] + w[0:1] * g2 + w[1:2] * g1 + w[2:3] * g
        o_ref[...] = (_silu(gc) * u_ref[...]).astype(o_ref.dtype)

    cur, prev, _, cws, cbs = _conv_specs(s, f, t)
    return pl.pallas_call(
        kern, grid=(4, s // t), in_specs=[cur, prev, cur, cws, cbs], out_specs=cur,
        out_shape=jax.ShapeDtypeStruct(gg.shape, BF16),
        compiler_params=pltpu.CompilerParams(dimension_semantics=("parallel", "parallel"), vmem_limit_bytes=VMEM_LIMIT),
        name="ffn_conv_fwd")(gg, gg, uu, cw, cb)


def _conv_bwd_gate(gg, uu, dact, cw, cb):
    _, s, f = gg.shape
    t = _tile(s, 512)

    def kern(g_ref, gp_ref, u_ref, da_ref, cw_ref, cb_ref, du_ref, dgc_ref, dcw_ref, dcb_ref):
        i = pl.program_id(1)
        g, u, da = g_ref[...], u_ref[...], da_ref[...]
        g1, g2 = _conv_taps(g, gp_ref[...], i == 0)
        w = cw_ref[...]
        gc = cb_ref[...] + w[0:1] * g2 + w[1:2] * g1 + w[2:3] * g
        sg = jax.nn.sigmoid(gc)
        du_ref[...] = (da * (gc * sg)).astype(du_ref.dtype)
        dgc = da * u * (sg * (1.0 + gc * (1.0 - sg)))
        dgc_ref[...] = dgc

        @pl.when(i == 0)
        def _():
            dcw_ref[...] = jnp.zeros_like(dcw_ref)
            dcb_ref[...] = jnp.zeros_like(dcb_ref)

        dcw_ref[0:1, :] += jnp.sum(dgc * g2, axis=0, keepdims=True)
        dcw_ref[1:2, :] += jnp.sum(dgc * g1, axis=0, keepdims=True)
        dcw_ref[2:3, :] += jnp.sum(dgc * g, axis=0, keepdims=True)
        dcb_ref[...] += jnp.sum(dgc, axis=0, keepdims=True)

    cur, prev, _, cws, cbs = _conv_specs(s, f, t)
    return pl.pallas_call(
        kern, grid=(4, s // t), in_specs=[cur, prev, cur, cur, cws, cbs], out_specs=[cur, cur, cws, cbs],
        out_shape=[jax.ShapeDtypeStruct(gg.shape, BF16), jax.ShapeDtypeStruct(gg.shape, F32),
                   jax.ShapeDtypeStruct(cw.shape, F32), jax.ShapeDtypeStruct(cb.shape, F32)],
        compiler_params=pltpu.CompilerParams(dimension_semantics=("parallel", "arbitrary"), vmem_limit_bytes=VMEM_LIMIT),
        name="ffn_conv_bwd_gate")(gg, gg, uu, dact, cw, cb)


def _conv_bwd_taps(dgc, cw):
    _, s, f = dgc.shape
    t = _tile(s, 512)
    nt = s // t

    def kern(d_ref, dn_ref, cw_ref, o_ref):
        d = d_ref[...]
        ext = jnp.concatenate([d, jnp.where(pl.program_id(1) == nt - 1, 0.0, dn_ref[...])], axis=0)
        up1 = pltpu.roll(ext, t + CONV_HALO - 1, 0)[:t]
        up2 = pltpu.roll(ext, t + CONV_HALO - 2, 0)[:t]
        w = cw_ref[...]
        o_ref[...] = (w[2:3] * d + w[1:2] * up1 + w[0:1] * up2).astype(o_ref.dtype)

    cur, _, nxt, cws, _ = _conv_specs(s, f, t)
    return pl.pallas_call(
        kern, grid=(4, nt), in_specs=[cur, nxt, cws], out_specs=cur, out_shape=jax.ShapeDtypeStruct(dgc.shape, BF16),
        compiler_params=pltpu.CompilerParams(dimension_semantics=("parallel", "parallel"), vmem_limit_bytes=VMEM_LIMIT),
        name="ffn_conv_bwd_taps")(dgc, dgc, cw)


def _rope_tables(pos):
    half = MLA_ROPE // 2
    inv = ROPE_THETA ** (-jnp.arange(half, dtype=F32) / half)
    ang = pos.astype(F32)[:, None] * inv
    cos, sin = jnp.cos(ang), jnp.sin(ang)
    s = pos.shape[0]
    z = lambda w: jnp.zeros((s, w), F32)
    c = jnp.concatenate([jnp.ones((s, MLA_NOPE), F32), cos, cos, jnp.ones((s, LANES - MLA_QK), F32)], axis=1)
    sa = jnp.concatenate([z(MLA_NOPE), -sin, z(half), z(LANES - MLA_QK)], axis=1)
    sb = jnp.concatenate([z(MLA_NOPE), z(half), sin, z(LANES - MLA_QK)], axis=1)
    return c, sa, sb


def _local_step(x, mem, pos, target, w):
    g = {}
    c, sa, sb = _rope_tables(pos)

    xn = _norm_fwd(x, w["norm_mix"], "norm_mix_fwd")
    proj = _matmul(xn, w["in"], "nn", F32, "proj_fwd")
    alr = _row(proj, 128, P_ALR // 128)
    kpe = _row(proj, 128, P_KPE // 128)
    og = _row(proj, 512, P_OG // 512)
    cq = _row(proj, 256, P_CQ // 256)
    ckv = _row(proj, 128, P_CKV // 128)

    la = _rows_call(lambda r, k: ([_gate_fn(r[0], k[0], k[1])], []), [alr], [w["w2"], w["gate_b"]],
                    [(256, F32)], name="gla_gate_fwd")[0]
    o_gla, states = _gla_fwd(proj, la)

    q_lat, kv_lat = _rows_call(lambda r, k: ([_rms(r[0], k[0]), _rms(r[1], k[1])], []), [cq, ckv],
                               [w["q_a_norm"], w["kv_a_norm"]], [(256, BF16), (128, BF16)], name="mla_lat_fwd")
    q_up = _matmul(q_lat, w["uq"], "nn", F32, "mla_q_fwd")
    k_up = _matmul(kv_lat, w["k"], "nn", F32, "mla_k_fwd")
    v_mla = _matmul(kv_lat, w["v"], "nn", BF16, "mla_v_fwd")

    def qk_body(r, k):
        qs, ks = [], []
        for qh, kh in zip(_heads(r[0], MLA_HEADS), _heads(r[1], MLA_HEADS)):
            a, b = _qk_head(qh, kh, r[2], r[3], r[4], r[5], k[0], k[1])
            qs.append(a)
            ks.append(b)
        return [_cat(qs), _cat(ks)], []

    tabs = [_row(c), _row(sa), _row(sb)]
    q_r, k_r = _rows_call(qk_body, [_row(q_up), _row(k_up), kpe] + tabs, [w["q_norm"], w["k_norm"]],
                          [(1024, BF16), (1024, BF16)], name="mla_qk_fwd")
    o_mla, lse = _attn_fwd(q_r, k_r, v_mla)

    def mix_body(r, k):
        ys = [_mix_head(o, g_, k[0]) for o, g_ in zip(_heads(r[0], GLA_HEADS), _heads(r[1], GLA_HEADS))]
        return [_cat(ys + [r[2]])], []

    cat = _rows_call(mix_body, [_row(o_gla), og, _row(o_mla)], [w["gla_out_norm"]], [(1024, BF16)],
                     name="mix_fwd")[0]
    h1 = _matmul(cat, w["out"], "nn", F32, "out_fwd", residual=x)

    hn = _norm_fwd(h1, w["norm_xa"], "norm_xa_fwd")
    mn = _norm_fwd(mem, w["norm_mem"], "norm_mem_fwd")
    xq = _matmul(hn, w["xq"], "nn", F32, "xa_q_fwd")
    xkv = _matmul(mn, w["xkv"], "nn", F32, "xa_kv_fwd")

    def xa_body(r, k):
        ks, vs = _heads(k[0], 2 * XA_HEADS)[:XA_HEADS], _heads(k[0], 2 * XA_HEADS)[XA_HEADS:]
        return [_cat([_xa_head(a, b, v_, k[1], k[2]) for a, b, v_ in zip(_heads(r[0], XA_HEADS), ks, vs)])], []

    xo = _rows_call(xa_body, [_row(xq)], [xkv, w["xa_q_norm"], w["xa_k_norm"]], [(512, BF16)], name="xa_fwd")[0]
    h2 = _matmul(xo, w["xo"], "nn", F32, "xa_o_fwd", residual=h1)

    fn = _norm_fwd(h2, w["norm_ffn"], "norm_ffn_fwd")
    gg = _matmul(fn, w["wg"], "nn", F32, "ffn_gate_fwd", b_lead="p")
    uu = _matmul(fn, w["wu"], "nn", F32, "ffn_up_fwd", b_lead="p")
    act = _conv_fwd(gg, uu, w["cw"], w["cb"])
    y = _matmul(act, w["wd"], "nn", F32, "ffn_down_fwd", residual=h2, a_lead="k", b_lead="k")

    def loss_body(r, k):
        err = r[0] - r[1]
        part = 0.5 * jnp.sum(jnp.sum(err * err, axis=1, keepdims=True) * (1.0 / D_MODEL), axis=0, keepdims=True)
        return [err * (1.0 / D_MODEL)], [jnp.broadcast_to(part, (1, LANES))]

    dy, loss = _rows_call(loss_body, [_row(y), _row(target)], [], [(D_MODEL, F32)], [(1, LANES)], name="loss")

    g["ffn_w_down"] = _matmul(act, dy, "tn", BF16, "ffn_down_dw", a_lead="p")
    dact = _matmul(dy, w["wd"], "nt", F32, "ffn_down_dx", b_lead="p")
    duu, dgc, g["ffn_conv_w"], g["ffn_conv_b"] = _conv_bwd_gate(gg, uu, dact, w["cw"], w["cb"])
    dgg = _conv_bwd_taps(dgc, w["cw"])
    g["ffn_w_gate"] = _matmul(fn, dgg, "tn", BF16, "ffn_gate_dw", b_lead="p")
    g["ffn_w_up"] = _matmul(fn, duu, "tn", BF16, "ffn_up_dw", b_lead="p")
    dfn = _matmul(dgg, w["wg"], "nt", F32, "ffn_gate_dx", a_lead="k", b_lead="k")
    dfn = _matmul(duu, w["wu"], "nt", F32, "ffn_up_dx", residual=dfn, a_lead="k", b_lead="k")
    dh2, g["norm_ffn"] = _norm_bwd(h2, w["norm_ffn"], dfn, dy, "norm_ffn_bwd")

    g["xa_w_o"] = _matmul(xo, dh2, "tn", BF16, "xa_o_dw")
    dxo = _matmul(dh2, w["xo"], "nt", F32, "xa_o_dx")

    def xa_bwd(r, k):
        kvh = _heads(k[0], 2 * XA_HEADS)
        dq_, dk_, dv_ = [], [], []
        dqn, dkn = 0.0, 0.0
        for h, (a, d_) in enumerate(zip(_heads(r[0], XA_HEADS), _heads(r[1], XA_HEADS))):
            _, vjp = jax.vjp(_xa_head, a, kvh[h], kvh[XA_HEADS + h], k[1], k[2])
            ga, gk, gv, gqn, gkn = vjp(d_)
            dq_.append(ga)
            dk_.append(gk)
            dv_.append(gv)
            dqn, dkn = dqn + gqn, dkn + gkn
        return [_cat(dq_)], [_cat(dk_ + dv_), dqn, dkn]

    dxq, dxkv, g["xa_q_norm"], g["xa_k_norm"] = _rows_call(
        xa_bwd, [_row(xq), _row(dxo)], [xkv, w["xa_q_norm"], w["xa_k_norm"]], [(512, BF16)],
        [xkv.shape, (1, 128), (1, 128)], name="xa_bwd")
    g["xa_w_q"] = _matmul(hn, dxq, "tn", BF16, "xa_q_dw")
    dhn = _matmul(dxq, w["xq"], "nt", F32, "xa_q_dx")
    g["xa_w_kv"] = _matmul(mn, dxkv, "tn", BF16, "xa_kv_dw")
    dmn = _matmul(dxkv, w["xkv"], "nt", F32, "xa_kv_dx")
    _, g["norm_mem"] = _norm_bwd(mem, w["norm_mem"], dmn, dmn, "norm_mem_bwd")
    dh1, g["norm_xa"] = _norm_bwd(h1, w["norm_xa"], dhn, dh2, "norm_xa_bwd")

    g["w_out"] = _matmul(cat, dh1, "tn", BF16, "out_dw")
    dcat = _matmul(dh1, w["out"], "nt", F32, "out_dx")

    def mix_bwd(r, k):
        do_, dog_ = [], []
        dgn = 0.0
        for o, g_, d_ in zip(_heads(r[0], GLA_HEADS), _heads(r[1], GLA_HEADS), _heads(r[2], GLA_HEADS)):
            _, vjp = jax.vjp(_mix_head, o, g_, k[0])
            a, b, gn_ = vjp(d_)
            do_.append(a)
            dog_.append(b)
            dgn = dgn + gn_
        return [_cat(do_), _cat(dog_)], [dgn]

    do_gla, d_og, g["gla_out_norm"] = _rows_call(mix_bwd, [_row(o_gla), og, _row(dcat, 512, 0)], [w["gla_out_norm"]],
                                                 [(512, F32), (512, BF16)], [(1, 128)], name="mix_bwd")

    dq_r, dk_r, dv_mla = _attn_bwd(q_r, k_r, v_mla, o_mla, lse, dcat)

    def qk_bwd(r, k):
        dqs, dks = [], []
        dkpe, dqn, dkn = 0.0, 0.0, 0.0
        for qh, kh, dqh, dkh in zip(_heads(r[0], MLA_HEADS), _heads(r[1], MLA_HEADS), _heads(r[6], MLA_HEADS),
                                    _heads(r[7], MLA_HEADS)):
            _, vjp = jax.vjp(lambda a, b, e, f, h_: _qk_head(a, b, e, r[3], r[4], r[5], f, h_), qh, kh, r[2], k[0], k[1])
            ga, gb, ge, gf, gh = vjp((dqh, dkh))
            dqs.append(ga)
            dks.append(gb)
            dkpe, dqn, dkn = dkpe + ge, dqn + gf, dkn + gh
        return [_cat(dqs), _cat(dks), dkpe], [dqn, dkn]

    dq_up, dk_up, d_kpe, g["q_norm"], g["k_norm"] = _rows_call(
        qk_bwd, [_row(q_up), _row(k_up), kpe] + tabs + [_row(dq_r), _row(dk_r)], [w["q_norm"], w["k_norm"]],
        [(1024, BF16), (1024, BF16), (128, BF16)], [(1, 128), (1, 128)], name="mla_qk_bwd")
    g["uq"] = _matmul(q_lat, dq_up, "tn", BF16, "mla_q_dw")
    dq_lat = _matmul(dq_up, w["uq"], "nt", F32, "mla_q_dx")
    g["k"] = _matmul(kv_lat, dk_up, "tn", BF16, "mla_k_dw")
    g["v"] = _matmul(kv_lat, dv_mla, "tn", BF16, "mla_v_dw")
    dkv_lat = _matmul(dk_up, w["k"], "nt", F32, "mla_k_dx")
    dkv_lat = _matmul(dv_mla, w["v"], "nt", F32, "mla_v_dx", residual=dkv_lat)

    def lat_bwd(r, k):
        _, vjp1 = jax.vjp(_rms, r[0], k[0])
        _, vjp2 = jax.vjp(_rms, r[1], k[1])
        a, ga = vjp1(r[2])
        b, gb = vjp2(r[3])
        return [a, b], [ga, gb]

    d_cq, d_ckv, g["mla_q_a_norm"], g["mla_kv_a_norm"] = _rows_call(
        lat_bwd, [cq, ckv, _row(dq_lat), _row(dkv_lat)], [w["q_a_norm"], w["kv_a_norm"]],
        [(256, BF16), (128, BF16)], [(1, 256), (1, 128)], name="mla_lat_bwd")

    dgq, dgk, dla, dgv = _gla_bwd(proj, la, states, do_gla)

    def gate_bwd(r, k):
        _, vjp = jax.vjp(_gate_fn, r[0], k[0], k[1])
        a, gw, gb = vjp(r[1])
        return [a], [gw, gb]

    d_alr, g["w2"], g["gla_gate_b"] = _rows_call(gate_bwd, [alr, _row(dla)], [w["w2"], w["gate_b"]], [(128, BF16)],
                                                 [(128, 256), (1, 256)], name="gla_gate_bwd")

    dproj = jnp.concatenate([dgq.astype(BF16), dgk.astype(BF16), dgv.astype(BF16), d_og, d_cq, d_ckv, d_kpe, d_alr],
                            axis=1)
    g["in"] = _matmul(xn, dproj, "tn", BF16, "proj_dw")
    dxn = _matmul(dproj, w["in"], "nt", F32, "proj_dx")
    dx, g["norm_mix"] = _norm_bwd(x, w["norm_mix"], dxn, dh1, "norm_mix_bwd")
    return loss[0, 0], dx, g


def _join_shards(pieces, axis):
    if axis == 0:
        return pieces.reshape(-1, pieces.shape[2])
    return jnp.transpose(pieces, (1, 0, 2)).reshape(pieces.shape[1], -1)


def _split_shards(full, axis):
    r, c = full.shape
    if axis == 0:
        return full.reshape(4, r // 4, c)
    return jnp.transpose(full.reshape(r, 4, c // 4), (1, 0, 2))


def _to_kernel_layout(gath, rep):
    w_in = _join_shards(gath["w_in"], 1)
    z = lambda n: jnp.zeros((D_MODEL, n), w_in.dtype)
    seg = lambda lo, n: w_in[:, lo:lo + n]
    ukv = _join_shards(gath["mla_w_ukv"], 1).reshape(MLA_KV_RANK, MLA_HEADS, MLA_NOPE + MLA_V)
    w = {
        "in": jnp.concatenate([seg(N_GQ, 256), seg(N_GK, 256), seg(N_GV, 512), seg(N_OG, 512), seg(N_CQ, 256),
                               seg(N_CKV, 128), z(64), seg(N_KPE, 32), z(32), seg(N_ALR, 16), z(112)], axis=1),
        "uq": jnp.pad(_join_shards(gath["mla_w_uq"], 1).reshape(MLA_Q_RANK, MLA_HEADS, MLA_QK),
                      ((0, 0), (0, 0), (0, LANES - MLA_QK))).reshape(MLA_Q_RANK, MLA_HEADS * LANES),
        "k": jnp.pad(ukv[:, :, :MLA_NOPE], ((0, 0), (0, 0), (0, LANES - MLA_NOPE))).reshape(MLA_KV_RANK, -1),
        "v": ukv[:, :, MLA_NOPE:].reshape(MLA_KV_RANK, MLA_HEADS * MLA_V),
        "out": _join_shards(gath["w_out"], 0), "xq": _join_shards(gath["xa_w_q"], 0),
        "xkv": _join_shards(gath["xa_w_kv"], 0), "xo": _join_shards(gath["xa_w_o"], 1),
        "wg": gath["ffn_w_gate"], "wu": gath["ffn_w_up"], "wd": gath["ffn_w_down"],
        "w2": jnp.pad(_join_shards(gath["gla_gate_w2"], 1), ((0, LANES - GLA_RANK), (0, 0))),
        "cw": gath["ffn_conv_w"], "cb": rep["ffn_conv_b"].reshape(4, 1, D_FF // 4),
        "q_norm": jnp.pad(rep["mla_q_norm"], ((0, 0), (0, LANES - MLA_QK))),
        "k_norm": jnp.pad(rep["mla_k_norm"], ((0, 0), (0, LANES - MLA_QK))),
        "q_a_norm": rep["mla_q_a_norm"], "kv_a_norm": rep["mla_kv_a_norm"], "gate_b": rep["gla_gate_b"],
    }
    for n in ("norm_mix", "gla_out_norm", "norm_xa", "norm_mem", "xa_q_norm", "xa_k_norm", "norm_ffn"):
        w[n] = rep[n]
    return w


def _grad_shards(g):
    gi = g["in"]
    seg = lambda lo, n: gi[:, lo:lo + n]
    w_in = jnp.concatenate([seg(P_GQ, 256), seg(P_GK, 256), seg(P_GV, 512), seg(P_ALR, 16), seg(P_OG, 512),
                            seg(P_CQ, 256), seg(P_CKV, 128), seg(P_KPE + 64, 32)], axis=1)
    uq = g["uq"].reshape(MLA_Q_RANK, MLA_HEADS, LANES)[:, :, :MLA_QK].reshape(MLA_Q_RANK, -1)
    ukv = jnp.concatenate([g["k"].reshape(MLA_KV_RANK, MLA_HEADS, LANES)[:, :, :MLA_NOPE],
                           g["v"].reshape(MLA_KV_RANK, MLA_HEADS, MLA_V)], axis=2).reshape(MLA_KV_RANK, -1)
    sh = {
        "w_in": _split_shards(w_in, 1), "gla_gate_w2": _split_shards(g["w2"][:GLA_RANK], 1),
        "mla_w_uq": _split_shards(uq, 1), "mla_w_ukv": _split_shards(ukv, 1),
        "w_out": _split_shards(g["w_out"], 0), "xa_w_q": _split_shards(g["xa_w_q"], 0),
        "xa_w_kv": _split_shards(g["xa_w_kv"], 0), "xa_w_o": _split_shards(g["xa_w_o"], 1),
        "ffn_w_gate": g["ffn_w_gate"], "ffn_w_up": g["ffn_w_up"], "ffn_conv_w": g["ffn_conv_w"],
        "ffn_w_down": g["ffn_w_down"],
    }
    sh = {n: v.astype(BF16) for n, v in sh.items()}
    rep = {n: g[n] for n in REPLICATED if n in g}
    rep["mla_q_norm"] = g["q_norm"][:, :MLA_QK]
    rep["mla_k_norm"] = g["k_norm"][:, :MLA_QK]
    rep["ffn_conv_b"] = g["ffn_conv_b"].reshape(1, D_FF)
    return sh, rep


SMALL_SHAPE = (8, 1024)


def _pack_small(vectors):
    flat = jnp.concatenate(vectors, axis=1)
    return jnp.pad(flat, ((0, 0), (0, SMALL_SHAPE[0] * SMALL_SHAPE[1] - flat.shape[1]))).reshape(SMALL_SHAPE)


def _unpack_small(buf, widths):
    flat = buf.reshape(1, -1)
    out, off = [], 0
    for wd in widths:
        out.append(flat[:, off:off + wd])
        off += wd
    return out


ANY = pl.BlockSpec(memory_space=pl.ANY)


def _place():
    x, y, c = lax.axis_index("x"), lax.axis_index("y"), lax.axis_index("c")
    chips = [(1 - x, y), (x, 1 - y), (1 - x, 1 - y)]
    return x, y, c, chips


def _gather_weights(shards):
    n = len(shards)

    def body(*refs):
        ins, outs = refs[:n], refs[n:2 * n]
        send, recv, local = refs[2 * n:]
        x, y, c, chips = _place()
        me = 2 * x + y
        started = []
        for t in range(n):
            cp = pltpu.make_async_copy(ins[t], outs[t].at[me], local.at[t])
            cp.start()
            started.append(cp.wait)
            for j, (px, py) in enumerate(chips):
                cp = pltpu.make_async_remote_copy(src_ref=ins[t], dst_ref=outs[t].at[me], send_sem=send.at[3 * t + j],
                                                  recv_sem=recv.at[3 * t + j], device_id=(px, py, c), device_id_type=MESH)
                cp.start()
                started.append(cp.wait_send)
        for t in range(n):
            for j, (px, py) in enumerate(chips):
                pltpu.make_async_remote_copy(src_ref=ins[t], dst_ref=outs[t].at[2 * px + py], send_sem=send.at[3 * t + j],
                                             recv_sem=recv.at[3 * t + j], device_id=(px, py, c),
                                             device_id_type=MESH).wait_recv()
        for wait in started:
            wait()

    return pl.pallas_call(
        body, in_specs=[ANY] * n, out_specs=[ANY] * n,
        out_shape=[jax.ShapeDtypeStruct((4,) + s.shape, s.dtype) for s in shards],
        scratch_shapes=[pltpu.SemaphoreType.DMA((3 * n,)), pltpu.SemaphoreType.DMA((3 * n,)),
                        pltpu.SemaphoreType.DMA((n,))],
        name="gather_weights")(*shards)


def _scatter_partials(parts, small):
    n = len(parts)

    def body(*refs):
        ins, s_ref = refs[:n], refs[n]
        outs, so_ref = refs[n + 1:2 * n + 1], refs[2 * n + 1]
        ici_s, ici_r, d2d_s, d2d_r, sm_s, sm_r, local = refs[2 * n + 2:]
        x, y, c, chips = _place()
        me = 2 * x + y
        dev = 4 * x + 2 * y + c
        sib = (x, y, 1 - c)
        waits = []

        def push(src, dst, ss, rs, to):
            cp = pltpu.make_async_remote_copy(src_ref=src, dst_ref=dst, send_sem=ss, recv_sem=rs, device_id=to,
                                              device_id_type=MESH)
            cp.start()
            waits.append(cp.wait_send)

        def landed(dst, rs):
            pltpu.make_async_remote_copy(src_ref=dst, dst_ref=dst, send_sem=local.at[0], recv_sem=rs, device_id=sib,
                                         device_id_type=MESH).wait_recv()

        cp = pltpu.make_async_copy(s_ref, so_ref.at[dev], local.at[n])
        cp.start()
        waits.append(cp.wait)
        peers = []
        for k in range(1, 8):
            px = (1 - x) if (k >> 2) & 1 else x
            py = (1 - y) if (k >> 1) & 1 else y
            pc = (1 - c) if k & 1 else c
            peers.append((px, py, pc))
            push(s_ref, so_ref.at[dev], sm_s.at[k - 1], sm_r.at[k - 1], (px, py, pc))
        for t in range(n):
            cp = pltpu.make_async_copy(ins[t].at[me], outs[t].at[dev], local.at[t])
            cp.start()
            waits.append(cp.wait)
            push(ins[t].at[me], outs[t].at[dev], d2d_s.at[4 * t], d2d_r.at[4 * t], sib)
            for j, (px, py) in enumerate(chips):
                push(ins[t].at[2 * px + py], outs[t].at[dev], ici_s.at[3 * t + j], ici_r.at[3 * t + j], (px, py, c))
        for t in range(n):
            for j, (px, py) in enumerate(chips):
                slot = outs[t].at[4 * px + 2 * py + c]
                landed(slot, ici_r.at[3 * t + j])
                push(slot, slot, d2d_s.at[4 * t + 1 + j], d2d_r.at[4 * t + 1 + j], sib)
        for t in range(n):
            landed(outs[t].at[4 * x + 2 * y + (1 - c)], d2d_r.at[4 * t])
            for j, (px, py) in enumerate(chips):
                landed(outs[t].at[4 * px + 2 * py + (1 - c)], d2d_r.at[4 * t + 1 + j])
        for k, (px, py, pc) in enumerate(peers):
            landed(so_ref.at[4 * px + 2 * py + pc], sm_r.at[k])
        for wait in waits:
            wait()

    dma = pltpu.SemaphoreType.DMA
    return pl.pallas_call(
        body, in_specs=[ANY] * (n + 1), out_specs=[ANY] * (n + 1),
        out_shape=[jax.ShapeDtypeStruct((8,) + p.shape[1:], p.dtype) for p in parts]
        + [jax.ShapeDtypeStruct((8,) + small.shape, small.dtype)],
        scratch_shapes=[dma((3 * n,)), dma((3 * n,)), dma((4 * n,)), dma((4 * n,)), dma((7,)), dma((7,)), dma((n + 1,))],
        name="scatter_partials")(*parts, small)


def _row_tile(r, cap=256):
    if r <= cap:
        return r
    return max(t for t in range(8, cap + 1, 8) if r % t == 0)


def _adamw(w, m, v, land, name):
    r, c = w.shape
    t = _row_tile(r)

    def kern(w_ref, m_ref, v_ref, l_ref, g_out, d_out, m_out, v_out):
        g = l_ref[0].astype(F32)
        for i in range(1, 8):
            g = g + l_ref[i].astype(F32)
        m_new = ADAM_B1 * m_ref[...] + (1.0 - ADAM_B1) * g
        v_new = ADAM_B2 * v_ref[...] + (1.0 - ADAM_B2) * (g * g)
        m_hat = m_new / (1.0 - ADAM_B1 ** ADAM_STEP)
        v_hat = v_new / (1.0 - ADAM_B2 ** ADAM_STEP)
        g_out[...] = g
        d_out[...] = -ADAM_LR * (m_hat / (jnp.sqrt(v_hat) + ADAM_EPS) + ADAM_WD * w_ref[...])
        m_out[...] = m_new
        v_out[...] = v_new

    spec = pl.BlockSpec((t, c), lambda i: (i, 0))
    return pl.pallas_call(
        kern, grid=(r // t,), in_specs=[spec] * 3 + [pl.BlockSpec((8, t, c), lambda i: (0, i, 0))], out_specs=[spec] * 4,
        out_shape=[jax.ShapeDtypeStruct((r, c), F32)] * 4,
        compiler_params=pltpu.CompilerParams(dimension_semantics=("parallel",), vmem_limit_bytes=VMEM_LIMIT),
        name=name)(w, m, v, land)


def _step(a):
    sq = lambda n: a[n][0] if a[n].ndim == 3 else a[n]
    sh_names = [n for n, _ in SHARDED]

    mine = [sq(n) if n in EXACT_GATHER else sq(n).astype(BF16) for n in sh_names]
    gath = dict(zip(sh_names, _gather_weights(mine), strict=True))
    w = _to_kernel_layout(gath, {n: a[n] for n in REPLICATED})

    loss, dx, g = _local_step(sq("x"), sq("mem"), a["positions"][0], sq("loss_target"), w)

    sh, rep = _grad_shards(g)
    *lands, land_small = _scatter_partials([sh[n] for n in sh_names], _pack_small([rep[n] for n in REPLICATED]))

    outs = {}
    kinds = ("grad_", "delta_", "new_m_", "new_v_")
    for n, land in zip(sh_names, lands, strict=True):
        res = _adamw(sq(n), sq("m_" + n), sq("v_" + n), land, "adamw_" + n)
        for kind, val in zip(kinds, res, strict=True):
            outs[kind + n] = val.reshape(a[n].shape)
    packed = [_pack_small([a[p + n] for n in REPLICATED]) for p in ("", "m_", "v_")]
    res = _adamw(*packed, land_small, "adamw_replicated")
    widths = [a[n].shape[1] for n in REPLICATED]
    for kind, buf in zip(kinds, res, strict=True):
        for n, val in zip(REPLICATED, _unpack_small(buf, widths), strict=True):
            outs[kind + n] = val

    loss = lax.psum(loss, ("x", "y", "c"))
    ordered = [outs[kind + n] for kind in kinds for n in WEIGHTS]
    return (loss, dx[None], *ordered)


def kernel(x, mem, positions, norm_mix, w_in, gla_gate_w2, gla_gate_b, gla_out_norm, mla_q_a_norm, mla_w_uq, mla_kv_a_norm, mla_w_ukv, mla_q_norm, mla_k_norm, w_out, norm_xa, norm_mem, xa_w_q, xa_w_kv, xa_q_norm, xa_k_norm, xa_w_o, norm_ffn, ffn_w_gate, ffn_w_up, ffn_conv_w, ffn_conv_b, ffn_w_down, loss_target, m_norm_mix, m_w_in, m_gla_gate_w2, m_gla_gate_b, m_gla_out_norm, m_mla_q_a_norm, m_mla_w_uq, m_mla_kv_a_norm, m_mla_w_ukv, m_mla_q_norm, m_mla_k_norm, m_w_out, m_norm_xa, m_norm_mem, m_xa_w_q, m_xa_w_kv, m_xa_q_norm, m_xa_k_norm, m_xa_w_o, m_norm_ffn, m_ffn_w_gate, m_ffn_w_up, m_ffn_conv_w, m_ffn_conv_b, m_ffn_w_down, v_norm_mix, v_w_in, v_gla_gate_w2, v_gla_gate_b, v_gla_out_norm, v_mla_q_a_norm, v_mla_w_uq, v_mla_kv_a_norm, v_mla_w_ukv, v_mla_q_norm, v_mla_k_norm, v_w_out, v_norm_xa, v_norm_mem, v_xa_w_q, v_xa_w_kv, v_xa_q_norm, v_xa_k_norm, v_xa_w_o, v_norm_ffn, v_ffn_w_gate, v_ffn_w_up, v_ffn_conv_w, v_ffn_conv_b, v_ffn_w_down):
    return _step(dict(locals()))
```

```python
import functools

import jax
import jax.numpy as jnp
from jax import lax
from jax.experimental import pallas as pl
from jax.experimental.pallas import tpu as pltpu

F32, BF16 = jnp.float32, jnp.bfloat16
MESH = pl.DeviceIdType.MESH

D_MODEL = 1024
EPS = 1e-6
GLA_HEADS, GLA_DK, GLA_DV, GLA_RANK, GLA_CHUNK = 4, 64, 128, 16, 64
GLA_GATE_NORM = 16.0
MLA_HEADS, MLA_Q_RANK, MLA_KV_RANK, MLA_NOPE, MLA_ROPE, MLA_V = 8, 256, 128, 64, 32, 64
MLA_QK = MLA_NOPE + MLA_ROPE
ROPE_THETA = 10000.0
XA_HEADS, XA_DIM = 4, 128
D_FF = 2816
ADAM_LR, ADAM_B1, ADAM_B2, ADAM_EPS, ADAM_WD, ADAM_STEP = 0.001, 0.9, 0.999, 1e-08, 0.01, 10

LANES = 128
VMEM_LIMIT = 56 * 1024 * 1024

P_GQ, P_GK, P_GV, P_OG, P_CQ, P_CKV, P_KPE, P_ALR, P_WIDTH = 0, 256, 512, 1024, 1536, 1792, 1920, 2048, 2176
N_GQ, N_GK, N_GV, N_ALR, N_OG, N_CQ, N_CKV, N_KPE, N_WIDTH = 0, 256, 512, 1024, 1040, 1552, 1808, 1936, 1968

SHARDED = (("w_in", 1), ("gla_gate_w2", 1), ("mla_w_uq", 1), ("mla_w_ukv", 1), ("w_out", 0), ("xa_w_q", 0),
           ("xa_w_kv", 0), ("xa_w_o", 1), ("ffn_w_gate", 1), ("ffn_w_up", 1), ("ffn_conv_w", 1), ("ffn_w_down", 0))
REPLICATED = ("norm_mix", "gla_gate_b", "gla_out_norm", "mla_q_a_norm", "mla_kv_a_norm", "mla_q_norm", "mla_k_norm",
              "norm_xa", "norm_mem", "xa_q_norm", "xa_k_norm", "norm_ffn", "ffn_conv_b")
EXACT_GATHER = ("gla_gate_w2", "ffn_conv_w")
EARLY = ("w_in", "gla_gate_w2", "mla_w_uq", "mla_w_ukv")
LATE = tuple(n for n, _ in SHARDED if n not in EARLY)
WEIGHTS = ("norm_mix", "w_in", "gla_gate_w2", "gla_gate_b", "gla_out_norm", "mla_q_a_norm", "mla_w_uq",
           "mla_kv_a_norm", "mla_w_ukv", "mla_q_norm", "mla_k_norm", "w_out", "norm_xa", "norm_mem", "xa_w_q",
           "xa_w_kv", "xa_q_norm", "xa_k_norm", "xa_w_o", "norm_ffn", "ffn_w_gate", "ffn_w_up", "ffn_conv_w",
           "ffn_conv_b", "ffn_w_down")


_NN = ((1,), (0,))
_NT = ((1,), (1,))
_TN = ((0,), (0,))


def _dg(a, b, dims):
    return lax.dot_general(a.astype(BF16), b.astype(BF16), (dims, ((), ())), preferred_element_type=F32)


@jax.custom_vjp
def _dot_nn(a, b):
    return _dg(a, b, _NN)


_dot_nn.defvjp(lambda a, b: (_dg(a, b, _NN), (a, b)),
               lambda r, g: (_dg(g, r[1], _NT).astype(r[0].dtype), _dg(r[0], g, _TN).astype(r[1].dtype)))


@jax.custom_vjp
def _dot_nt(a, b):
    return _dg(a, b, _NT)


_dot_nt.defvjp(lambda a, b: (_dg(a, b, _NT), (a, b)),
               lambda r, g: (_dg(g, r[1], _NN).astype(r[0].dtype), _dg(g, r[0], _TN).astype(r[1].dtype)))


@jax.custom_vjp
def _dot_tn(a, b):
    return _dg(a, b, _TN)


_dot_tn.defvjp(lambda a, b: (_dg(a, b, _TN), (a, b)),
               lambda r, g: (_dg(r[1], g, _NT).astype(r[0].dtype), _dg(r[0], g, _NN).astype(r[1].dtype)))


def _rms(x, w, n=None):
    n = x.shape[-1] if n is None else n
    ms = jnp.sum(x * x, axis=-1, keepdims=True) * (1.0 / n)
    return x * lax.rsqrt(ms + EPS) * w


def _silu(x):
    return x * jax.nn.sigmoid(x)


def _log_sigmoid(x):
    return jnp.minimum(x, 0.0) - jnp.log(1.0 + jnp.exp(-jnp.abs(x)))


@jax.custom_vjp
def _rope(y, c, sa, sb):
    return y * c + pltpu.roll(y, LANES - 16, 1) * sa + pltpu.roll(y, 16, 1) * sb


def _rope_bwd(res, g):
    c, sa, sb = res
    gy = g * c + pltpu.roll(g * sa, 16, 1) + pltpu.roll(g * sb, LANES - 16, 1)
    return gy, jnp.zeros_like(c), jnp.zeros_like(sa), jnp.zeros_like(sb)


_rope.defvjp(lambda y, c, sa, sb: (_rope(y, c, sa, sb), (c, sa, sb)), _rope_bwd)


def _lane_mask(lo, hi):
    lane = lax.broadcasted_iota(jnp.int32, (1, LANES), 1)
    return ((lane >= lo) & (lane < hi)).astype(F32)


def _tile(n, t):
    t = min(n, t)
    assert n % t == 0, (n, t)
    return t


def _matmul(a, b, mode, out_dtype, name, residual=None, a_lead=None, b_lead=None):
    (a0, a1), (b0, b1) = a.shape[-2:], b.shape[-2:]
    if mode == "nn":
        m, k, k2, n = a0, a1, b0, b1
    elif mode == "nt":
        m, k, n, k2 = a0, a1, b0, b1
    else:
        k, m, k2, n = a0, a1, b0, b1
    assert k == k2, (a.shape, b.shape, mode)
    npar = 4 if "p" in (a_lead, b_lead) else 1
    nsum = 4 if "k" in (a_lead, b_lead) else 1
    if mode == "tn":
        tm = m if m <= 1408 else m // 2
        tn = n if tm * n <= 1024 * 2304 else n // 2
        tk = _tile(k, 512)
    else:
        tm, tn, tk = _tile(m, 512), n, k
    assert m % tm == 0 and n % tn == 0 and k % tk == 0
    nk = k // tk
    dims = {"nn": _NN, "nt": _NT, "tn": _TN}[mode]

    def body(*refs):
        if residual is None:
            a_ref, b_ref, o_ref, acc = refs
        else:
            a_ref, b_ref, r_ref, o_ref, acc = refs
        ks, kk = pl.program_id(3), pl.program_id(4)

        @pl.when((ks == 0) & (kk == 0))
        def _():
            acc[...] = jnp.zeros_like(acc)

        acc[...] += _dg(a_ref[...], b_ref[...], dims)

        @pl.when((ks == nsum - 1) & (kk == nk - 1))
        def _():
            r = acc[...]
            if residual is not None:
                r = r + r_ref[...]
            o_ref[...] = r.astype(o_ref.dtype)

    def spec(lead, blk, idx):
        if lead is None:
            return pl.BlockSpec(blk, lambda p, i, j, ks, kk: idx(i, j, kk))
        if lead == "p":
            return pl.BlockSpec((None,) + blk, lambda p, i, j, ks, kk: (p,) + idx(i, j, kk))
        return pl.BlockSpec((None,) + blk, lambda p, i, j, ks, kk: (ks,) + idx(i, j, kk))

    if mode == "nn":
        in_specs = [spec(a_lead, (tm, tk), lambda i, j, kk: (i, kk)), spec(b_lead, (tk, tn), lambda i, j, kk: (kk, j))]
    elif mode == "nt":
        in_specs = [spec(a_lead, (tm, tk), lambda i, j, kk: (i, kk)), spec(b_lead, (tn, tk), lambda i, j, kk: (j, kk))]
    else:
        in_specs = [spec(a_lead, (tk, tm), lambda i, j, kk: (kk, i)), spec(b_lead, (tk, tn), lambda i, j, kk: (kk, j))]
    args = [a, b]
    if residual is not None:
        assert npar == 1
        in_specs.append(spec(None, (tm, tn), lambda i, j, kk: (i, j)))
        args.append(residual)
    out_lead = "p" if npar > 1 else None
    return pl.pallas_call(
        body, grid=(npar, m // tm, n // tn, nsum, nk), in_specs=in_specs,
        out_specs=spec(out_lead, (tm, tn), lambda i, j, kk: (i, j)),
        out_shape=jax.ShapeDtypeStruct(((4,) if npar > 1 else ()) + (m, n), out_dtype),
        scratch_shapes=[pltpu.VMEM((tm, tn), F32)],
        compiler_params=pltpu.CompilerParams(
            dimension_semantics=("parallel", "parallel", "parallel", "arbitrary", "arbitrary"),
            vmem_limit_bytes=VMEM_LIMIT),
        name=name)(*args)


def _row(a, width=None, col_block=0):
    return (a, a.shape[1] if width is None else width, col_block)


def _rows_call(body, rows, consts, outs, accs=(), *, name, tile=512):
    s = rows[0][0].shape[0]
    t = _tile(s, tile)
    nr, nc, no = len(rows), len(consts), len(outs)

    def kern(*refs):
        r = [x[...] for x in refs[:nr]]
        c = [x[...] for x in refs[nr:nr + nc]]
        o_refs = refs[nr + nc:nr + nc + no]
        a_refs = refs[nr + nc + no:]
        ro, ao = body(r, c)
        for ref, val in zip(o_refs, ro, strict=True):
            ref[...] = val.astype(ref.dtype)
        if a_refs:
            @pl.when(pl.program_id(0) == 0)
            def _():
                for ref in a_refs:
                    ref[...] = jnp.zeros_like(ref)

            for ref, val in zip(a_refs, ao, strict=True):
                ref[...] += val

    in_specs = [pl.BlockSpec((t, w), functools.partial(lambda cb, i: (i, cb), cb)) for (_, w, cb) in rows]
    in_specs += [pl.BlockSpec(c.shape, lambda i: (0, 0)) for c in consts]
    out_specs = [pl.BlockSpec((t, w), lambda i: (i, 0)) for (w, _) in outs]
    out_specs += [pl.BlockSpec(shape, lambda i: (0, 0)) for shape in accs]
    out_shape = [jax.ShapeDtypeStruct((s, w), dt) for (w, dt) in outs]
    out_shape += [jax.ShapeDtypeStruct(shape, F32) for shape in accs]
    return pl.pallas_call(
        kern, grid=(s // t,), in_specs=in_specs, out_specs=out_specs, out_shape=out_shape,
        compiler_params=pltpu.CompilerParams(dimension_semantics=("arbitrary" if accs else "parallel",),
                                             vmem_limit_bytes=VMEM_LIMIT),
        name=name)(*[r[0] for r in rows], *consts)


def _gla_chunk(q, k, la, v0, v1, s0, s1):
    c = q.shape[0]
    r = lax.broadcasted_iota(jnp.int32, (c, c), 0)
    cc = lax.broadcasted_iota(jnp.int32, (c, c), 1)
    tril = cc <= r
    cum = lax.dot_general(tril.astype(F32), la, (_NN, ((), ())), precision=lax.Precision.HIGHEST,
                          preferred_element_type=F32)
    cl = jnp.sum(la, axis=0, keepdims=True)
    qd = q * (GLA_DK ** -0.5) * jnp.exp(cum)
    ki = k * jnp.exp(-cum)
    ke = k * jnp.exp(cl - cum)
    dec = jnp.exp(cl)
    outs, news = [], []
    for h, (v, s) in enumerate(((v0, s0), (v1, s1))):
        mk = _lane_mask(GLA_DK * h, GLA_DK * (h + 1))
        qh = qd * mk
        att = jnp.where(tril, _dot_nt(qh, ki), 0.0)
        outs.append(_dot_nn(att, v) + _dot_nt(qh, s))
        news.append(s * dec + _dot_tn(v, ke * mk))
    return outs[0], outs[1], news[0], news[1]


def _gla_specs(tb, rev_nb=None):
    blk = (lambda b: b) if rev_nb is None else (lambda b: rev_nb - 1 - b)
    q = pl.BlockSpec((tb, 128), lambda p, b: (blk(b), P_GQ // 128 + p))
    k = pl.BlockSpec((tb, 128), lambda p, b: (blk(b), P_GK // 128 + p))
    la = pl.BlockSpec((tb, 128), lambda p, b: (blk(b), p))
    v = pl.BlockSpec((tb, 256), lambda p, b: (blk(b), P_GV // 256 + p))
    o = pl.BlockSpec((tb, 256), lambda p, b: (blk(b), p))
    st = pl.BlockSpec((tb // GLA_CHUNK, 2, 128, 128), lambda p, b: (blk(b), p, 0, 0))
    return q, k, la, v, o, st


def _gla_fwd(proj, la):
    s = proj.shape[0]
    tb = _tile(s, 512)
    nb, nch = s // tb, tb // GLA_CHUNK

    def kern(q_ref, k_ref, la_ref, v_ref, o_ref, st_ref, s_sc):
        @pl.when(pl.program_id(1) == 0)
        def _():
            s_sc[...] = jnp.zeros_like(s_sc)

        s0, s1 = s_sc[0], s_sc[1]
        for ci in range(nch):
            sl = slice(ci * GLA_CHUNK, (ci + 1) * GLA_CHUNK)
            st_ref[ci, 0] = s0
            st_ref[ci, 1] = s1
            o0, o1, s0, s1 = _gla_chunk(q_ref[sl, :], k_ref[sl, :], la_ref[sl, :], v_ref[sl, 0:128],
                                        v_ref[sl, 128:256], s0, s1)
            o_ref[sl, 0:128] = o0
            o_ref[sl, 128:256] = o1
        s_sc[0] = s0
        s_sc[1] = s1

    q, k, lasp, v, o, st = _gla_specs(tb)
    return pl.pallas_call(
        kern, grid=(2, nb), in_specs=[q, k, lasp, v], out_specs=[o, st],
        out_shape=[jax.ShapeDtypeStruct((s, 512), F32),
                   jax.ShapeDtypeStruct((s // GLA_CHUNK, GLA_HEADS, 128, 128), F32)],
        scratch_shapes=[pltpu.VMEM((2, 128, 128), F32)],
        compiler_params=pltpu.CompilerParams(dimension_semantics=("parallel", "arbitrary"),
                                             vmem_limit_bytes=VMEM_LIMIT),
        name="gla_fwd")(proj, proj, la, proj)


def _gla_bwd(proj, la, states, d_o):
    s = proj.shape[0]
    tb = _tile(s, 512)
    nb, nch = s // tb, tb // GLA_CHUNK

    def kern(q_ref, k_ref, la_ref, v_ref, do_ref, st_ref, dq_ref, dk_ref, dla_ref, dv_ref, ds_sc):
        @pl.when(pl.program_id(1) == 0)
        def _():
            ds_sc[...] = jnp.zeros_like(ds_sc)

        d0, d1 = ds_sc[0], ds_sc[1]
        for ci in reversed(range(nch)):
            sl = slice(ci * GLA_CHUNK, (ci + 1) * GLA_CHUNK)
            _, vjp = jax.vjp(_gla_chunk, q_ref[sl, :], k_ref[sl, :], la_ref[sl, :], v_ref[sl, 0:128],
                             v_ref[sl, 128:256], st_ref[ci, 0], st_ref[ci, 1])
            gq, gk, gla, gv0, gv1, d0, d1 = vjp((do_ref[sl, 0:128], do_ref[sl, 128:256], d0, d1))
            dq_ref[sl, :] = gq
            dk_ref[sl, :] = gk
            dla_ref[sl, :] = gla
            dv_ref[sl, 0:128] = gv0
            dv_ref[sl, 128:256] = gv1
        ds_sc[0] = d0
        ds_sc[1] = d1

    q, k, lasp, v, o, st = _gla_specs(tb, rev_nb=nb)
    return pl.pallas_call(
        kern, grid=(2, nb), in_specs=[q, k, lasp, v, o, st], out_specs=[lasp, lasp, lasp, o],
        out_shape=[jax.ShapeDtypeStruct((s, 256), F32), jax.ShapeDtypeStruct((s, 256), F32),
                   jax.ShapeDtypeStruct((s, 256), F32), jax.ShapeDtypeStruct((s, 512), F32)],
        scratch_shapes=[pltpu.VMEM((2, 128, 128), F32)],
        compiler_params=pltpu.CompilerParams(dimension_semantics=("parallel", "arbitrary"),
                                             vmem_limit_bytes=VMEM_LIMIT),
        name="gla_bwd")(proj, proj, la, proj, d_o, states)


def _causal_keep(t, qi, ki):
    row = lax.broadcasted_iota(jnp.int32, (t, t), 0) + qi * t
    col = lax.broadcasted_iota(jnp.int32, (t, t), 1) + ki * t
    return col <= row


def _split_refs(refs, counts):
    out, off = [], 0
    for cnt in counts:
        out.append(refs[off:off + cnt])
        off += cnt
    return out


def _attn_fwd(q, k, v, comm, tile=512):
    s = q.shape[0]
    t = _tile(s, tile)
    n = s // t
    nci, nco = len(comm.ins), len(comm.out_shape)

    def kern(*refs):
        (q_ref, k_ref, v_ref), cins, (o_ref, lse_ref), couts, (m_sc, l_sc, acc_sc), csems = _split_refs(
            refs, (3, nci, 2, nco, 3, len(comm.sems)))
        qi, ki = pl.program_id(1), pl.program_id(2)
        place = _place()

        @pl.when((pl.program_id(0) == 0) & (qi == 0) & (ki == 0))
        def _():
            comm.start(place, cins, couts, csems)

        first = lax.broadcasted_iota(jnp.int32, (t, LANES), 1) < MLA_V

        @pl.when(ki == 0)
        def _():
            m_sc[...] = jnp.full_like(m_sc, -jnp.inf)
            l_sc[...] = jnp.zeros_like(l_sc)
            acc_sc[...] = jnp.zeros_like(acc_sc)

        @pl.when(ki <= qi)
        def _():
            keep = _causal_keep(t, qi, ki)
            alphas, pvs = [], []
            for h in range(2):
                sc = _dg(q_ref[:, 128 * h:128 * (h + 1)], k_ref[:, 128 * h:128 * (h + 1)], _NT)
                sc = jnp.where(keep, sc, -jnp.inf)
                m_prev = m_sc[h]
                m_new = jnp.maximum(m_prev, jnp.max(sc, axis=1, keepdims=True))
                alpha = jnp.exp(m_prev - m_new)
                p = jnp.exp(sc - m_new[:, 0:1])
                l_sc[h] = alpha * l_sc[h] + jnp.sum(p, axis=1, keepdims=True)
                m_sc[h] = m_new
                alphas.append(alpha)
                pvs.append(_dg(p, v_ref[...], _NN))
            acc_sc[...] = acc_sc[...] * jnp.where(first, alphas[0], alphas[1]) + jnp.where(first, pvs[0], pvs[1])

        @pl.when(ki == qi)
        def _():
            l = jnp.where(first, l_sc[0], l_sc[1])
            m = jnp.where(first, m_sc[0], m_sc[1])
            o_ref[...] = acc_sc[...] / l
            lse_ref[...] = m + jnp.log(l)

        @pl.when((pl.program_id(0) == MLA_HEADS // 2 - 1) & (qi == n - 1) & (ki == n - 1))
        def _():
            comm.finish(place, cins, couts, csems)

    kv_idx = lambda p, qi, ki: (jnp.minimum(ki, qi), p)
    res = pl.pallas_call(
        kern, grid=(MLA_HEADS // 2, n, n),
        in_specs=[pl.BlockSpec((t, 256), lambda p, qi, ki: (qi, p)), pl.BlockSpec((t, 256), kv_idx),
                  pl.BlockSpec((t, 128), kv_idx)] + [ANY] * nci,
        out_specs=[pl.BlockSpec((t, 128), lambda p, qi, ki: (qi, p)), pl.BlockSpec((t, 128), lambda p, qi, ki: (qi, p))]
        + [ANY] * nco,
        out_shape=[jax.ShapeDtypeStruct((s, 512), F32), jax.ShapeDtypeStruct((s, 512), F32)] + comm.out_shape,
        scratch_shapes=[pltpu.VMEM((2, t, LANES), F32), pltpu.VMEM((2, t, LANES), F32), pltpu.VMEM((t, LANES), F32)]
        + comm.sems,
        compiler_params=pltpu.CompilerParams(dimension_semantics=("arbitrary", "arbitrary", "arbitrary"),
                                             vmem_limit_bytes=VMEM_LIMIT),
        name="mla_attn_fwd")(q, k, v, *comm.ins)
    return res[0], res[1], res[2:]


def _attn_bwd(q, k, v, o, lse, dcat, comm, tile=512):
    s = q.shape[0]
    t = _tile(s, tile)
    n = s // t
    nci, nco = len(comm.ins), len(comm.out_shape)

    def kern(*refs):
        (q_ref, k_ref, v_ref, o_ref, lse_ref, do_ref), cins, (dq_ref, dk_ref, dv_ref), couts, (dk_sc, dv_sc), csems = \
            _split_refs(refs, (6, nci, 3, nco, 2, len(comm.sems)))
        ki, qi = pl.program_id(1), pl.program_id(2)
        place = _place()

        @pl.when((pl.program_id(0) == 0) & (qi == 0) & (ki == 0))
        def _():
            comm.start(place, cins, couts, csems)

        @pl.when((ki == 0) & (qi == 0))
        def _():
            dq_ref[...] = jnp.zeros_like(dq_ref)

        @pl.when(qi == ki)
        def _():
            dk_sc[...] = jnp.zeros_like(dk_sc)
            dv_sc[...] = jnp.zeros_like(dv_sc)

        @pl.when(qi >= ki)
        def _():
            keep = _causal_keep(t, qi, ki)
            d_o = do_ref[...]
            prod = d_o * o_ref[...]
            rows = pl.ds(pl.multiple_of(qi * t, t), t)
            for h in range(2):
                hs = slice(128 * h, 128 * (h + 1))
                mk = _lane_mask(MLA_V * h, MLA_V * (h + 1))
                qh, kh = q_ref[:, hs], k_ref[:, hs]
                sc = jnp.where(keep, _dg(qh, kh, _NT), -jnp.inf)
                p = jnp.exp(sc - lse_ref[:, MLA_V * h:MLA_V * h + 1])
                doh = d_o * mk
                dp = _dg(doh, v_ref[...], _NT)
                delta = jnp.sum(prod * mk, axis=1, keepdims=True)
                ds = p * (dp - delta)
                dv_sc[...] += _dg(p, doh, _TN)
                dk_sc[:, hs] += _dg(ds, qh, _TN)
                dq_ref[rows, hs] += _dg(ds, kh, _NN)

        @pl.when(qi == n - 1)
        def _():
            dk_ref[...] = dk_sc[...]
            dv_ref[...] = dv_sc[...].astype(dv_ref.dtype)

        @pl.when((pl.program_id(0) == MLA_HEADS // 2 - 1) & (qi == n - 1) & (ki == n - 1))
        def _():
            comm.finish(place, cins, couts, csems)

    q_idx = lambda p, ki, qi: (jnp.maximum(qi, ki), p)
    res = pl.pallas_call(
        kern, grid=(MLA_HEADS // 2, n, n),
        in_specs=[pl.BlockSpec((t, 256), q_idx), pl.BlockSpec((t, 256), lambda p, ki, qi: (ki, p)),
                  pl.BlockSpec((t, 128), lambda p, ki, qi: (ki, p)), pl.BlockSpec((t, 128), q_idx),
                  pl.BlockSpec((t, 128), q_idx),
                  pl.BlockSpec((t, 128), lambda p, ki, qi: (jnp.maximum(qi, ki), 4 + p))] + [ANY] * nci,
        out_specs=[pl.BlockSpec((s, 256), lambda p, ki, qi: (0, p)), pl.BlockSpec((t, 256), lambda p, ki, qi: (ki, p)),
                   pl.BlockSpec((t, 128), lambda p, ki, qi: (ki, p))] + [ANY] * nco,
        out_shape=[jax.ShapeDtypeStruct((s, 1024), F32), jax.ShapeDtypeStruct((s, 1024), F32),
                   jax.ShapeDtypeStruct((s, 512), BF16)] + comm.out_shape,
        scratch_shapes=[pltpu.VMEM((t, 256), F32), pltpu.VMEM((t, 128), F32)] + comm.sems,
        compiler_params=pltpu.CompilerParams(dimension_semantics=("arbitrary", "arbitrary", "arbitrary"),
                                             vmem_limit_bytes=VMEM_LIMIT),
        name="mla_attn_bwd")(q, k, v, o, lse, dcat, *comm.ins)
    return res[0], res[1], res[2], res[3:]


def _gate_fn(alr, w2, b):
    return _log_sigmoid(_dot_nn(alr, w2) + b) * (1.0 / GLA_GATE_NORM)


def _qk_head(qh, kh, kpe, c, sa, sb, qn, kn):
    kfull = kh + kpe * _lane_mask(MLA_NOPE, MLA_QK)
    q_r = _rope(_rms(qh, qn, MLA_QK), c, sa, sb) * (MLA_QK ** -0.5)
    k_r = _rope(_rms(kfull, kn, MLA_QK), c, sa, sb)
    return q_r, k_r


def _mix_head(o, og, gn):
    return _rms(o, gn) * _silu(og)


def _xa_head(xq, xk, xv, qn, kn):
    sc = _dot_nt(_rms(xq, qn), _rms(xk, kn)) * (XA_DIM ** -0.5)
    e = jnp.exp(sc - lax.stop_gradient(jnp.max(sc, axis=1, keepdims=True)))
    p = e / jnp.sum(e, axis=1, keepdims=True)
    return _dot_nn(p, xv)


def _heads(x, n):
    return [x[:, 128 * h:128 * (h + 1)] for h in range(n)]


def _cat(xs):
    return jnp.concatenate(xs, axis=1)


def _norm_fwd(x, w, name):
    return _rows_call(lambda r, c: ([_rms(r[0], c[0])], []), [_row(x)], [w], [(x.shape[1], BF16)], name=name)[0]


def _norm_bwd(x, w, d_out, add, name):
    def body(r, c):
        _, vjp = jax.vjp(_rms, r[0], c[0])
        dx, dw = vjp(r[1])
        return [dx + r[2]], [dw]

    return _rows_call(body, [_row(x), _row(d_out), _row(add)], [w], [(x.shape[1], F32)], [w.shape], name=name)


CONV_HALO = 8


def _conv_specs(s, f, t):
    n8 = t // CONV_HALO
    cur = pl.BlockSpec((None, t, f), lambda j, i: (j, i, 0))
    prev = pl.BlockSpec((None, CONV_HALO, f), lambda j, i: (j, jnp.maximum(i * n8 - 1, 0), 0))
    nxt = pl.BlockSpec((None, CONV_HALO, f), lambda j, i: (j, jnp.minimum((i + 1) * n8, s // CONV_HALO - 1), 0))
    cw = pl.BlockSpec((None, 3, f), lambda j, i: (j, 0, 0))
    cb = pl.BlockSpec((None, 1, f), lambda j, i: (j, 0, 0))
    return cur, prev, nxt, cw, cb


def _conv_taps(g, prev, first):
    ext = jnp.concatenate([jnp.where(first, 0.0, prev), g], axis=0)
    return pltpu.roll(ext, 1, 0)[CONV_HALO:], pltpu.roll(ext, 2, 0)[CONV_HALO:]


def _conv_fwd(gg, uu, cw, cb):
    _, s, f = gg.shape
    t = _tile(s, 512)

    def kern(g_ref, gp_ref, u_ref, cw_ref, cb_ref, o_ref):
        g = g_ref[...]
        g1, g2 = _conv_taps(g, gp_ref[...], pl.program_id(1) == 0)
        w = cw_ref[...]
        gc = cb_ref[...] + w[0:1] * g2 + w[1:2] * g1 + w[2:3] * g
        o_ref[...] = (_silu(gc) * u_ref[...]).astype(o_ref.dtype)

    cur, prev, _, cws, cbs = _conv_specs(s, f, t)
    return pl.pallas_call(
        kern, grid=(4, s // t), in_specs=[cur, prev, cur, cws, cbs], out_specs=cur,
        out_shape=jax.ShapeDtypeStruct(gg.shape, BF16),
        compiler_params=pltpu.CompilerParams(dimension_semantics=("parallel", "parallel"), vmem_limit_bytes=VMEM_LIMIT),
        name="ffn_conv_fwd")(gg, gg, uu, cw, cb)


def _conv_bwd_gate(gg, uu, dact, cw, cb):
    _, s, f = gg.shape
    t = _tile(s, 512)

    def kern(g_ref, gp_ref, u_ref, da_ref, cw_ref, cb_ref, du_ref, dgc_ref, dcw_ref, dcb_ref):
        i = pl.program_id(1)
        g, u, da = g_ref[...], u_ref[...], da_ref[...]
        g1, g2 = _conv_taps(g, gp_ref[...], i == 0)
        w = cw_ref[...]
        gc = cb_ref[...] + w[0:1] * g2 + w[1:2] * g1 + w[2:3] * g
        sg = jax.nn.sigmoid(gc)
        du_ref[...] = (da * (gc * sg)).astype(du_ref.dtype)
        dgc = da * u * (sg * (1.0 + gc * (1.0 - sg)))
        dgc_ref[...] = dgc

        @pl.when(i == 0)
        def _():
            dcw_ref[...] = jnp.zeros_like(dcw_ref)
            dcb_ref[...] = jnp.zeros_like(dcb_ref)

        dcw_ref[0:1, :] += jnp.sum(dgc * g2, axis=0, keepdims=True)
        dcw_ref[1:2, :] += jnp.sum(dgc * g1, axis=0, keepdims=True)
        dcw_ref[2:3, :] += jnp.sum(dgc * g, axis=0, keepdims=True)
        dcb_ref[...] += jnp.sum(dgc, axis=0, keepdims=True)

    cur, prev, _, cws, cbs = _conv_specs(s, f, t)
    return pl.pallas_call(
        kern, grid=(4, s // t), in_specs=[cur, prev, cur, cur, cws, cbs], out_specs=[cur, cur, cws, cbs],
        out_shape=[jax.ShapeDtypeStruct(gg.shape, BF16), jax.ShapeDtypeStruct(gg.shape, F32),
                   jax.ShapeDtypeStruct(cw.shape, F32), jax.ShapeDtypeStruct(cb.shape, F32)],
        compiler_params=pltpu.CompilerParams(dimension_semantics=("parallel", "arbitrary"), vmem_limit_bytes=VMEM_LIMIT),
        name="ffn_conv_bwd_gate")(gg, gg, uu, dact, cw, cb)


def _conv_bwd_taps(dgc, cw):
    _, s, f = dgc.shape
    t = _tile(s, 512)
    nt = s // t

    def kern(d_ref, dn_ref, cw_ref, o_ref):
        d = d_ref[...]
        ext = jnp.concatenate([d, jnp.where(pl.program_id(1) == nt - 1, 0.0, dn_ref[...])], axis=0)
        up1 = pltpu.roll(ext, t + CONV_HALO - 1, 0)[:t]
        up2 = pltpu.roll(ext, t + CONV_HALO - 2, 0)[:t]
        w = cw_ref[...]
        o_ref[...] = (w[2:3] * d + w[1:2] * up1 + w[0:1] * up2).astype(o_ref.dtype)

    cur, _, nxt, cws, _ = _conv_specs(s, f, t)
    return pl.pallas_call(
        kern, grid=(4, nt), in_specs=[cur, nxt, cws], out_specs=cur, out_shape=jax.ShapeDtypeStruct(dgc.shape, BF16),
        compiler_params=pltpu.CompilerParams(dimension_semantics=("parallel", "parallel"), vmem_limit_bytes=VMEM_LIMIT),
        name="ffn_conv_bwd_taps")(dgc, dgc, cw)


def _rope_tables(pos):
    half = MLA_ROPE // 2
    inv = ROPE_THETA ** (-jnp.arange(half, dtype=F32) / half)
    ang = pos.astype(F32)[:, None] * inv
    cos, sin = jnp.cos(ang), jnp.sin(ang)
    s = pos.shape[0]
    z = lambda w: jnp.zeros((s, w), F32)
    c = jnp.concatenate([jnp.ones((s, MLA_NOPE), F32), cos, cos, jnp.ones((s, LANES - MLA_QK), F32)], axis=1)
    sa = jnp.concatenate([z(MLA_NOPE), -sin, z(half), z(LANES - MLA_QK)], axis=1)
    sb = jnp.concatenate([z(MLA_NOPE), z(half), sin, z(LANES - MLA_QK)], axis=1)
    return c, sa, sb


def _local_step(x, mem, pos, target, w, late_shards):
    g = {}
    w = dict(w)
    c, sa, sb = _rope_tables(pos)

    xn = _norm_fwd(x, w["norm_mix"], "norm_mix_fwd")
    proj = _matmul(xn, w["in"], "nn", F32, "proj_fwd")
    alr = _row(proj, 128, P_ALR // 128)
    kpe = _row(proj, 128, P_KPE // 128)
    og = _row(proj, 512, P_OG // 512)
    cq = _row(proj, 256, P_CQ // 256)
    ckv = _row(proj, 128, P_CKV // 128)

    la = _rows_call(lambda r, k: ([_gate_fn(r[0], k[0], k[1])], []), [alr], [w["w2"], w["gate_b"]],
                    [(256, F32)], name="gla_gate_fwd")[0]
    o_gla, states = _gla_fwd(proj, la)

    q_lat, kv_lat = _rows_call(lambda r, k: ([_rms(r[0], k[0]), _rms(r[1], k[1])], []), [cq, ckv],
                               [w["q_a_norm"], w["kv_a_norm"]], [(256, BF16), (128, BF16)], name="mla_lat_fwd")
    q_up = _matmul(q_lat, w["uq"], "nn", F32, "mla_q_fwd")
    k_up = _matmul(kv_lat, w["k"], "nn", F32, "mla_k_fwd")
    v_mla = _matmul(kv_lat, w["v"], "nn", BF16, "mla_v_fwd")

    def qk_body(r, k):
        qs, ks = [], []
        for qh, kh in zip(_heads(r[0], MLA_HEADS), _heads(r[1], MLA_HEADS)):
            a, b = _qk_head(qh, kh, r[2], r[3], r[4], r[5], k[0], k[1])
            qs.append(a)
            ks.append(b)
        return [_cat(qs), _cat(ks)], []

    tabs = [_row(c), _row(sa), _row(sb)]
    q_r, k_r = _rows_call(qk_body, [_row(q_up), _row(k_up), kpe] + tabs, [w["q_norm"], w["k_norm"]],
                          [(1024, BF16), (1024, BF16)], name="mla_qk_fwd")
    o_mla, lse, gathered = _attn_fwd(q_r, k_r, v_mla, _gather_plan(late_shards))
    w.update(_late_layout(dict(zip(LATE, gathered, strict=True))))

    def mix_body(r, k):
        ys = [_mix_head(o, g_, k[0]) for o, g_ in zip(_heads(r[0], GLA_HEADS), _heads(r[1], GLA_HEADS))]
        return [_cat(ys + [r[2]])], []

    cat = _rows_call(mix_body, [_row(o_gla), og, _row(o_mla)], [w["gla_out_norm"]], [(1024, BF16)],
                     name="mix_fwd")[0]
    h1 = _matmul(cat, w["out"], "nn", F32, "out_fwd", residual=x)

    hn = _norm_fwd(h1, w["norm_xa"], "norm_xa_fwd")
    mn = _norm_fwd(mem, w["norm_mem"], "norm_mem_fwd")
    xq = _matmul(hn, w["xq"], "nn", F32, "xa_q_fwd")
    xkv = _matmul(mn, w["xkv"], "nn", F32, "xa_kv_fwd")

    def xa_body(r, k):
        ks, vs = _heads(k[0], 2 * XA_HEADS)[:XA_HEADS], _heads(k[0], 2 * XA_HEADS)[XA_HEADS:]
        return [_cat([_xa_head(a, b, v_, k[1], k[2]) for a, b, v_ in zip(_heads(r[0], XA_HEADS), ks, vs)])], []

    xo = _rows_call(xa_body, [_row(xq)], [xkv, w["xa_q_norm"], w["xa_k_norm"]], [(512, BF16)], name="xa_fwd")[0]
    h2 = _matmul(xo, w["xo"], "nn", F32, "xa_o_fwd", residual=h1)

    fn = _norm_fwd(h2, w["norm_ffn"], "norm_ffn_fwd")
    gg = _matmul(fn, w["wg"], "nn", F32, "ffn_gate_fwd", b_lead="p")
    uu = _matmul(fn, w["wu"], "nn", F32, "ffn_up_fwd", b_lead="p")
    act = _conv_fwd(gg, uu, w["cw"], w["cb"])
    y = _matmul(act, w["wd"], "nn", F32, "ffn_down_fwd", residual=h2, a_lead="k", b_lead="k")

    def loss_body(r, k):
        err = r[0] - r[1]
        part = 0.5 * jnp.sum(jnp.sum(err * err, axis=1, keepdims=True) * (1.0 / D_MODEL), axis=0, keepdims=True)
        return [err * (1.0 / D_MODEL)], [jnp.broadcast_to(part, (1, LANES))]

    dy, loss = _rows_call(loss_body, [_row(y), _row(target)], [], [(D_MODEL, F32)], [(1, LANES)], name="loss")

    g["ffn_w_down"] = _matmul(act, dy, "tn", BF16, "ffn_down_dw", a_lead="p")
    dact = _matmul(dy, w["wd"], "nt", F32, "ffn_down_dx", b_lead="p")
    duu, dgc, g["ffn_conv_w"], g["ffn_conv_b"] = _conv_bwd_gate(gg, uu, dact, w["cw"], w["cb"])
    dgg = _conv_bwd_taps(dgc, w["cw"])
    g["ffn_w_gate"] = _matmul(fn, dgg, "tn", BF16, "ffn_gate_dw", b_lead="p")
    g["ffn_w_up"] = _matmul(fn, duu, "tn", BF16, "ffn_up_dw", b_lead="p")
    dfn = _matmul(dgg, w["wg"], "nt", F32, "ffn_gate_dx", a_lead="k", b_lead="k")
    dfn = _matmul(duu, w["wu"], "nt", F32, "ffn_up_dx", residual=dfn, a_lead="k", b_lead="k")
    dh2, g["norm_ffn"] = _norm_bwd(h2, w["norm_ffn"], dfn, dy, "norm_ffn_bwd")

    g["xa_w_o"] = _matmul(xo, dh2, "tn", BF16, "xa_o_dw")
    dxo = _matmul(dh2, w["xo"], "nt", F32, "xa_o_dx")

    def xa_bwd(r, k):
        kvh = _heads(k[0], 2 * XA_HEADS)
        dq_, dk_, dv_ = [], [], []
        dqn, dkn = 0.0, 0.0
        for h, (a, d_) in enumerate(zip(_heads(r[0], XA_HEADS), _heads(r[1], XA_HEADS))):
            _, vjp = jax.vjp(_xa_head, a, kvh[h], kvh[XA_HEADS + h], k[1], k[2])
            ga, gk, gv, gqn, gkn = vjp(d_)
            dq_.append(ga)
            dk_.append(gk)
            dv_.append(gv)
            dqn, dkn = dqn + gqn, dkn + gkn
        return [_cat(dq_)], [_cat(dk_ + dv_), dqn, dkn]

    dxq, dxkv, g["xa_q_norm"], g["xa_k_norm"] = _rows_call(
        xa_bwd, [_row(xq), _row(dxo)], [xkv, w["xa_q_norm"], w["xa_k_norm"]], [(512, BF16)],
        [xkv.shape, (1, 128), (1, 128)], name="xa_bwd")
    g["xa_w_q"] = _matmul(hn, dxq, "tn", BF16, "xa_q_dw")
    dhn = _matmul(dxq, w["xq"], "nt", F32, "xa_q_dx")
    g["xa_w_kv"] = _matmul(mn, dxkv, "tn", BF16, "xa_kv_dw")
    dmn = _matmul(dxkv, w["xkv"], "nt", F32, "xa_kv_dx")
    _, g["norm_mem"] = _norm_bwd(mem, w["norm_mem"], dmn, dmn, "norm_mem_bwd")
    dh1, g["norm_xa"] = _norm_bwd(h1, w["norm_xa"], dhn, dh2, "norm_xa_bwd")

    g["w_out"] = _matmul(cat, dh1, "tn", BF16, "out_dw")
    dcat = _matmul(dh1, w["out"], "nt", F32, "out_dx")

    def mix_bwd(r, k):
        do_, dog_ = [], []
        dgn = 0.0
        for o, g_, d_ in zip(_heads(r[0], GLA_HEADS), _heads(r[1], GLA_HEADS), _heads(r[2], GLA_HEADS)):
            _, vjp = jax.vjp(_mix_head, o, g_, k[0])
            a, b, gn_ = vjp(d_)
            do_.append(a)
            dog_.append(b)
            dgn = dgn + gn_
        return [_cat(do_), _cat(dog_)], [dgn]

    do_gla, d_og, g["gla_out_norm"] = _rows_call(mix_bwd, [_row(o_gla), og, _row(dcat, 512, 0)], [w["gla_out_norm"]],
                                                 [(512, F32), (512, BF16)], [(1, 128)], name="mix_bwd")

    late_parts = _late_grad_shards(g)
    dq_r, dk_r, dv_mla, lands_late = _attn_bwd(q_r, k_r, v_mla, o_mla, lse, dcat,
                                               _scatter_plan([late_parts[n] for n in LATE]))

    def qk_bwd(r, k):
        dqs, dks = [], []
        dkpe, dqn, dkn = 0.0, 0.0, 0.0
        for qh, kh, dqh, dkh in zip(_heads(r[0], MLA_HEADS), _heads(r[1], MLA_HEADS), _heads(r[6], MLA_HEADS),
                                    _heads(r[7], MLA_HEADS)):
            _, vjp = jax.vjp(lambda a, b, e, f, h_: _qk_head(a, b, e, r[3], r[4], r[5], f, h_), qh, kh, r[2], k[0], k[1])
            ga, gb, ge, gf, gh = vjp((dqh, dkh))
            dqs.append(ga)
            dks.append(gb)
            dkpe, dqn, dkn = dkpe + ge, dqn + gf, dkn + gh
        return [_cat(dqs), _cat(dks), dkpe], [dqn, dkn]

    dq_up, dk_up, d_kpe, g["q_norm"], g["k_norm"] = _rows_call(
        qk_bwd, [_row(q_up), _row(k_up), kpe] + tabs + [_row(dq_r), _row(dk_r)], [w["q_norm"], w["k_norm"]],
        [(1024, BF16), (1024, BF16), (128, BF16)], [(1, 128), (1, 128)], name="mla_qk_bwd")
    g["uq"] = _matmul(q_lat, dq_up, "tn", BF16, "mla_q_dw")
    dq_lat = _matmul(dq_up, w["uq"], "nt", F32, "mla_q_dx")
    g["k"] = _matmul(kv_lat, dk_up, "tn", BF16, "mla_k_dw")
    g["v"] = _matmul(kv_lat, dv_mla, "tn", BF16, "mla_v_dw")
    dkv_lat = _matmul(dk_up, w["k"], "nt", F32, "mla_k_dx")
    dkv_lat = _matmul(dv_mla, w["v"], "nt", F32, "mla_v_dx", residual=dkv_lat)

    def lat_bwd(r, k):
        _, vjp1 = jax.vjp(_rms, r[0], k[0])
        _, vjp2 = jax.vjp(_rms, r[1], k[1])
        a, ga = vjp1(r[2])
        b, gb = vjp2(r[3])
        return [a, b], [ga, gb]

    d_cq, d_ckv, g["mla_q_a_norm"], g["mla_kv_a_norm"] = _rows_call(
        lat_bwd, [cq, ckv, _row(dq_lat), _row(dkv_lat)], [w["q_a_norm"], w["kv_a_norm"]],
        [(256, BF16), (128, BF16)], [(1, 256), (1, 128)], name="mla_lat_bwd")

    dgq, dgk, dla, dgv = _gla_bwd(proj, la, states, do_gla)

    def gate_bwd(r, k):
        _, vjp = jax.vjp(_gate_fn, r[0], k[0], k[1])
        a, gw, gb = vjp(r[1])
        return [a], [gw, gb]

    d_alr, g["w2"], g["gla_gate_b"] = _rows_call(gate_bwd, [alr, _row(dla)], [w["w2"], w["gate_b"]], [(128, BF16)],
                                                 [(128, 256), (1, 256)], name="gla_gate_bwd")

    dproj = jnp.concatenate([dgq.astype(BF16), dgk.astype(BF16), dgv.astype(BF16), d_og, d_cq, d_ckv, d_kpe, d_alr],
                            axis=1)
    g["in"] = _matmul(xn, dproj, "tn", BF16, "proj_dw")
    dxn = _matmul(dproj, w["in"], "nt", F32, "proj_dx")
    dx, g["norm_mix"] = _norm_bwd(x, w["norm_mix"], dxn, dh1, "norm_mix_bwd")
    return loss[0, 0], dx, g, lands_late


def _join_shards(pieces, axis):
    if axis == 0:
        return pieces.reshape(-1, pieces.shape[2])
    return jnp.transpose(pieces, (1, 0, 2)).reshape(pieces.shape[1], -1)


def _split_shards(full, axis):
    r, c = full.shape
    if axis == 0:
        return full.reshape(4, r // 4, c)
    return jnp.transpose(full.reshape(r, 4, c // 4), (1, 0, 2))


def _early_layout(gath, rep):
    w_in = _join_shards(gath["w_in"], 1)
    z = lambda n: jnp.zeros((D_MODEL, n), w_in.dtype)
    seg = lambda lo, n: w_in[:, lo:lo + n]
    ukv = _join_shards(gath["mla_w_ukv"], 1).reshape(MLA_KV_RANK, MLA_HEADS, MLA_NOPE + MLA_V)
    w = {
        "in": jnp.concatenate([seg(N_GQ, 256), seg(N_GK, 256), seg(N_GV, 512), seg(N_OG, 512), seg(N_CQ, 256),
                               seg(N_CKV, 128), z(64), seg(N_KPE, 32), z(32), seg(N_ALR, 16), z(112)], axis=1),
        "uq": jnp.pad(_join_shards(gath["mla_w_uq"], 1).reshape(MLA_Q_RANK, MLA_HEADS, MLA_QK),
                      ((0, 0), (0, 0), (0, LANES - MLA_QK))).reshape(MLA_Q_RANK, MLA_HEADS * LANES),
        "k": jnp.pad(ukv[:, :, :MLA_NOPE], ((0, 0), (0, 0), (0, LANES - MLA_NOPE))).reshape(MLA_KV_RANK, -1),
        "v": ukv[:, :, MLA_NOPE:].reshape(MLA_KV_RANK, MLA_HEADS * MLA_V),
        "w2": jnp.pad(_join_shards(gath["gla_gate_w2"], 1), ((0, LANES - GLA_RANK), (0, 0))),
        "cb": rep["ffn_conv_b"].reshape(4, 1, D_FF // 4),
        "q_norm": jnp.pad(rep["mla_q_norm"], ((0, 0), (0, LANES - MLA_QK))),
        "k_norm": jnp.pad(rep["mla_k_norm"], ((0, 0), (0, LANES - MLA_QK))),
        "q_a_norm": rep["mla_q_a_norm"], "kv_a_norm": rep["mla_kv_a_norm"], "gate_b": rep["gla_gate_b"],
    }
    for n in ("norm_mix", "gla_out_norm", "norm_xa", "norm_mem", "xa_q_norm", "xa_k_norm", "norm_ffn"):
        w[n] = rep[n]
    return w


def _late_layout(gath):
    return {"out": _join_shards(gath["w_out"], 0), "xq": _join_shards(gath["xa_w_q"], 0),
            "xkv": _join_shards(gath["xa_w_kv"], 0), "xo": _join_shards(gath["xa_w_o"], 1),
            "wg": gath["ffn_w_gate"], "wu": gath["ffn_w_up"], "wd": gath["ffn_w_down"], "cw": gath["ffn_conv_w"]}


def _late_grad_shards(g):
    sh = {"w_out": _split_shards(g["w_out"], 0), "xa_w_q": _split_shards(g["xa_w_q"], 0),
          "xa_w_kv": _split_shards(g["xa_w_kv"], 0), "xa_w_o": _split_shards(g["xa_w_o"], 1),
          "ffn_w_gate": g["ffn_w_gate"], "ffn_w_up": g["ffn_w_up"], "ffn_conv_w": g["ffn_conv_w"],
          "ffn_w_down": g["ffn_w_down"]}
    return {n: v.astype(BF16) for n, v in sh.items()}


def _early_grad_shards(g):
    gi = g["in"]
    seg = lambda lo, n: gi[:, lo:lo + n]
    w_in = jnp.concatenate([seg(P_GQ, 256), seg(P_GK, 256), seg(P_GV, 512), seg(P_ALR, 16), seg(P_OG, 512),
                            seg(P_CQ, 256), seg(P_CKV, 128), seg(P_KPE + 64, 32)], axis=1)
    uq = g["uq"].reshape(MLA_Q_RANK, MLA_HEADS, LANES)[:, :, :MLA_QK].reshape(MLA_Q_RANK, -1)
    ukv = jnp.concatenate([g["k"].reshape(MLA_KV_RANK, MLA_HEADS, LANES)[:, :, :MLA_NOPE],
                           g["v"].reshape(MLA_KV_RANK, MLA_HEADS, MLA_V)], axis=2).reshape(MLA_KV_RANK, -1)
    sh = {"w_in": _split_shards(w_in, 1), "gla_gate_w2": _split_shards(g["w2"][:GLA_RANK], 1),
          "mla_w_uq": _split_shards(uq, 1), "mla_w_ukv": _split_shards(ukv, 1)}
    sh = {n: v.astype(BF16) for n, v in sh.items()}
    rep = {n: g[n] for n in REPLICATED if n in g}
    rep["mla_q_norm"] = g["q_norm"][:, :MLA_QK]
    rep["mla_k_norm"] = g["k_norm"][:, :MLA_QK]
    rep["ffn_conv_b"] = g["ffn_conv_b"].reshape(1, D_FF)
    return sh, rep


SMALL_SHAPE = (8, 1024)


def _pack_small(vectors):
    flat = jnp.concatenate(vectors, axis=1)
    return jnp.pad(flat, ((0, 0), (0, SMALL_SHAPE[0] * SMALL_SHAPE[1] - flat.shape[1]))).reshape(SMALL_SHAPE)


def _unpack_small(buf, widths):
    flat = buf.reshape(1, -1)
    out, off = [], 0
    for wd in widths:
        out.append(flat[:, off:off + wd])
        off += wd
    return out


ANY = pl.BlockSpec(memory_space=pl.ANY)


def _place():
    x, y, c = lax.axis_index("x"), lax.axis_index("y"), lax.axis_index("c")
    chips = [(1 - x, y), (x, 1 - y), (1 - x, 1 - y)]
    return x, y, c, chips


class _Comm:
    def __init__(self, ins, out_shape, sems, start, finish):
        self.ins, self.out_shape, self.sems, self.start, self.finish = list(ins), list(out_shape), list(sems), start, finish


def _run_comm(plan, name):
    ni, no = len(plan.ins), len(plan.out_shape)

    def body(*refs):
        ins, outs, sems = refs[:ni], refs[ni:ni + no], refs[ni + no:]
        place = _place()
        plan.start(place, ins, outs, sems)
        plan.finish(place, ins, outs, sems)

    return pl.pallas_call(body, in_specs=[ANY] * ni, out_specs=[ANY] * no, out_shape=plan.out_shape,
                          scratch_shapes=plan.sems, name=name)(*plan.ins)


def _gather_plan(shards):
    n = len(shards)

    def copies(place, ins, outs, sems, landing):
        x, y, c, chips = place
        send, recv, local = sems
        me = 2 * x + y
        own = [pltpu.make_async_copy(ins[t], outs[t].at[me], local.at[t]) for t in range(n)]
        remote = []
        for t in range(n):
            for j, (px, py) in enumerate(chips):
                remote.append(pltpu.make_async_remote_copy(
                    src_ref=ins[t], dst_ref=outs[t].at[2 * px + py if landing else me], send_sem=send.at[3 * t + j],
                    recv_sem=recv.at[3 * t + j], device_id=(px, py, c), device_id_type=MESH))
        return own, remote

    def start(place, ins, outs, sems):
        own, push = copies(place, ins, outs, sems, False)
        for cp in own + push:
            cp.start()

    def finish(place, ins, outs, sems):
        own, land = copies(place, ins, outs, sems, True)
        for cp in land:
            cp.wait_recv()
        for cp in land:
            cp.wait_send()
        for cp in own:
            cp.wait()

    dma = pltpu.SemaphoreType.DMA
    return _Comm(shards, [jax.ShapeDtypeStruct((4,) + s.shape, s.dtype) for s in shards],
                 [dma((3 * n,)), dma((3 * n,)), dma((n,))], start, finish)


def _scatter_plan(parts, small=None):
    n = len(parts)
    ns = 0 if small is None else 1

    def unpack(place, ins, outs, sems):
        x, y, c, chips = place
        return x, y, c, chips, 2 * x + y, 4 * x + 2 * y + c, (x, y, 1 - c)

    def remote(src, dst, ss, rs, to):
        return pltpu.make_async_remote_copy(src_ref=src, dst_ref=dst, send_sem=ss, recv_sem=rs, device_id=to,
                                            device_id_type=MESH)

    def first_wave(place, ins, outs, sems):
        x, y, c, chips, me, dev, sib = unpack(place, ins, outs, sems)
        ici_s, ici_r, d2d_s, d2d_r, sm_s, sm_r, local = sems
        own, push = [], []
        if ns:
            own.append(pltpu.make_async_copy(ins[n], outs[n].at[dev], local.at[n]))
            for k in range(1, 8):
                px = (1 - x) if (k >> 2) & 1 else x
                py = (1 - y) if (k >> 1) & 1 else y
                pc = (1 - c) if k & 1 else c
                push.append(remote(ins[n], outs[n].at[dev], sm_s.at[k - 1], sm_r.at[k - 1], (px, py, pc)))
        for t in range(n):
            own.append(pltpu.make_async_copy(ins[t].at[me], outs[t].at[dev], local.at[t]))
            push.append(remote(ins[t].at[me], outs[t].at[dev], d2d_s.at[4 * t], d2d_r.at[4 * t], sib))
            for j, (px, py) in enumerate(chips):
                push.append(remote(ins[t].at[2 * px + py], outs[t].at[dev], ici_s.at[3 * t + j], ici_r.at[3 * t + j],
                                   (px, py, c)))
        return own, push

    def start(place, ins, outs, sems):
        own, push = first_wave(place, ins, outs, sems)
        for cp in own + push:
            cp.start()

    def finish(place, ins, outs, sems):
        x, y, c, chips, me, dev, sib = unpack(place, ins, outs, sems)
        ici_s, ici_r, d2d_s, d2d_r, sm_s, sm_r, local = sems
        own, push = first_wave(place, ins, outs, sems)

        def landed(dst, rs):
            remote(dst, dst, local.at[0], rs, sib).wait_recv()

        for t in range(n):
            for j, (px, py) in enumerate(chips):
                slot = outs[t].at[4 * px + 2 * py + c]
                landed(slot, ici_r.at[3 * t + j])
                cp = remote(slot, slot, d2d_s.at[4 * t + 1 + j], d2d_r.at[4 * t + 1 + j], sib)
                cp.start()
                push.append(cp)
        for t in range(n):
            landed(outs[t].at[4 * x + 2 * y + (1 - c)], d2d_r.at[4 * t])
            for j, (px, py) in enumerate(chips):
                landed(outs[t].at[4 * px + 2 * py + (1 - c)], d2d_r.at[4 * t + 1 + j])
        if ns:
            for k in range(1, 8):
                px = (1 - x) if (k >> 2) & 1 else x
                py = (1 - y) if (k >> 1) & 1 else y
                pc = (1 - c) if k & 1 else c
                landed(outs[n].at[4 * px + 2 * py + pc], sm_r.at[k - 1])
        for cp in push:
            cp.wait_send()
        for cp in own:
            cp.wait()

    dma = pltpu.SemaphoreType.DMA
    ins = list(parts) + ([small] if ns else [])
    out_shape = [jax.ShapeDtypeStruct((8,) + p.shape[1:], p.dtype) for p in parts]
    if ns:
        out_shape.append(jax.ShapeDtypeStruct((8,) + small.shape, small.dtype))
    return _Comm(ins, out_shape, [dma((3 * n,)), dma((3 * n,)), dma((4 * n,)), dma((4 * n,)), dma((7,)), dma((7,)),
                                  dma((n + 1,))], start, finish)


def _row_tile(r, cap=256):
    if r <= cap:
        return r
    return max(t for t in range(8, cap + 1, 8) if r % t == 0)


def _adamw(w, m, v, land, name):
    r, c = w.shape
    t = _row_tile(r)

    def kern(w_ref, m_ref, v_ref, l_ref, g_out, d_out, m_out, v_out):
        g = l_ref[0].astype(F32)
        for i in range(1, 8):
            g = g + l_ref[i].astype(F32)
        m_new = ADAM_B1 * m_ref[...] + (1.0 - ADAM_B1) * g
        v_new = ADAM_B2 * v_ref[...] + (1.0 - ADAM_B2) * (g * g)
        m_hat = m_new / (1.0 - ADAM_B1 ** ADAM_STEP)
        v_hat = v_new / (1.0 - ADAM_B2 ** ADAM_STEP)
        g_out[...] = g
        d_out[...] = -ADAM_LR * (m_hat / (jnp.sqrt(v_hat) + ADAM_EPS) + ADAM_WD * w_ref[...])
        m_out[...] = m_new
        v_out[...] = v_new

    spec = pl.BlockSpec((t, c), lambda i: (i, 0))
    return pl.pallas_call(
        kern, grid=(r // t,), in_specs=[spec] * 3 + [pl.BlockSpec((8, t, c), lambda i: (0, i, 0))], out_specs=[spec] * 4,
        out_shape=[jax.ShapeDtypeStruct((r, c), F32)] * 4,
        compiler_params=pltpu.CompilerParams(dimension_semantics=("parallel",), vmem_limit_bytes=VMEM_LIMIT),
        name=name)(w, m, v, land)


def _step(a):
    sq = lambda n: a[n][0] if a[n].ndim == 3 else a[n]
    payload = lambda n: sq(n) if n in EXACT_GATHER else sq(n).astype(BF16)

    gathered = _run_comm(_gather_plan([payload(n) for n in EARLY]), "gather_early")
    w = _early_layout(dict(zip(EARLY, gathered, strict=True)), {n: a[n] for n in REPLICATED})

    loss, dx, g, lands_late = _local_step(sq("x"), sq("mem"), a["positions"][0], sq("loss_target"), w,
                                          [payload(n) for n in LATE])

    sh, rep = _early_grad_shards(g)
    *lands_early, land_small = _run_comm(
        _scatter_plan([sh[n] for n in EARLY], _pack_small([rep[n] for n in REPLICATED])), "scatter_last")
    lands = dict(zip(EARLY + LATE, list(lands_early) + list(lands_late), strict=True))

    outs = {}
    kinds = ("grad_", "delta_", "new_m_", "new_v_")
    for n, _ in SHARDED:
        res = _adamw(sq(n), sq("m_" + n), sq("v_" + n), lands[n], "adamw_" + n)
        for kind, val in zip(kinds, res, strict=True):
            outs[kind + n] = val.reshape(a[n].shape)
    packed = [_pack_small([a[p + n] for n in REPLICATED]) for p in ("", "m_", "v_")]
    res = _adamw(*packed, land_small, "adamw_replicated")
    widths = [a[n].shape[1] for n in REPLICATED]
    for kind, buf in zip(kinds, res, strict=True):
        for n, val in zip(REPLICATED, _unpack_small(buf, widths), strict=True):
            outs[kind + n] = val

    loss = lax.psum(loss, ("x", "y", "c"))
    ordered = [outs[kind + n] for kind in kinds for n in WEIGHTS]
    return (loss, dx[None], *ordered)


def kernel(x, mem, positions, norm_mix, w_in, gla_gate_w2, gla_gate_b, gla_out_norm, mla_q_a_norm, mla_w_uq, mla_kv_a_norm, mla_w_ukv, mla_q_norm, mla_k_norm, w_out, norm_xa, norm_mem, xa_w_q, xa_w_kv, xa_q_norm, xa_k_norm, xa_w_o, norm_ffn, ffn_w_gate, ffn_w_up, ffn_conv_w, ffn_conv_b, ffn_w_down, loss_target, m_norm_mix, m_w_in, m_gla_gate_w2, m_gla_gate_b, m_gla_out_norm, m_mla_q_a_norm, m_mla_w_uq, m_mla_kv_a_norm, m_mla_w_ukv, m_mla_q_norm, m_mla_k_norm, m_w_out, m_norm_xa, m_norm_mem, m_xa_w_q, m_xa_w_kv, m_xa_q_norm, m_xa_k_norm, m_xa_w_o, m_norm_ffn, m_ffn_w_gate, m_ffn_w_up, m_ffn_conv_w, m_ffn_conv_b, m_ffn_w_down, v_norm_mix, v_w_in, v_gla_gate_w2, v_gla_gate_b, v_gla_out_norm, v_mla_q_a_norm, v_mla_w_uq, v_mla_kv_a_norm, v_mla_w_ukv, v_mla_q_norm, v_mla_k_norm, v_w_out, v_norm_xa, v_norm_mem, v_xa_w_q, v_xa_w_kv, v_xa_q_norm, v_xa_k_norm, v_xa_w_o, v_norm_ffn, v_ffn_w_gate, v_ffn_w_up, v_ffn_conv_w, v_ffn_conv_b, v_ffn_w_down):
    return _step(dict(locals()))
```

```python
import functools

import jax
import jax.numpy as jnp
from jax import lax
from jax.experimental import pallas as pl
from jax.experimental.pallas import tpu as pltpu

F32, BF16 = jnp.float32, jnp.bfloat16
MESH = pl.DeviceIdType.MESH

D_MODEL = 1024
EPS = 1e-6
GLA_HEADS, GLA_DK, GLA_DV, GLA_RANK, GLA_CHUNK = 4, 64, 128, 16, 64
GLA_GATE_NORM = 16.0
MLA_HEADS, MLA_Q_RANK, MLA_KV_RANK, MLA_NOPE, MLA_ROPE, MLA_V = 8, 256, 128, 64, 32, 64
MLA_QK = MLA_NOPE + MLA_ROPE
ROPE_THETA = 10000.0
LOG2E, LN2 = 1.4426950408889634, 0.6931471805599453
XA_HEADS, XA_DIM = 4, 128
D_FF = 2816
ADAM_LR, ADAM_B1, ADAM_B2, ADAM_EPS, ADAM_WD, ADAM_STEP = 0.001, 0.9, 0.999, 1e-08, 0.01, 10

LANES = 128
VMEM_LIMIT = 56 * 1024 * 1024
MATMUL_VMEM = 40 * 1024 * 1024

P_GQ, P_GK, P_GV, P_OG, P_CQ, P_CKV, P_KPE, P_ALR, P_WIDTH = 0, 256, 512, 1024, 1536, 1792, 1920, 2048, 2176
N_GQ, N_GK, N_GV, N_ALR, N_OG, N_CQ, N_CKV, N_KPE, N_WIDTH = 0, 256, 512, 1024, 1040, 1552, 1808, 1936, 1968

SHARDED = (("w_in", 1), ("gla_gate_w2", 1), ("mla_w_uq", 1), ("mla_w_ukv", 1), ("w_out", 0), ("xa_w_q", 0),
           ("xa_w_kv", 0), ("xa_w_o", 1), ("ffn_w_gate", 1), ("ffn_w_up", 1), ("ffn_conv_w", 1), ("ffn_w_down", 0))
REPLICATED = ("norm_mix", "gla_gate_b", "gla_out_norm", "mla_q_a_norm", "mla_kv_a_norm", "mla_q_norm", "mla_k_norm",
              "norm_xa", "norm_mem", "xa_q_norm", "xa_k_norm", "norm_ffn", "ffn_conv_b")
EXACT_GATHER = ("gla_gate_w2", "ffn_conv_w")
EARLY = ("w_in", "gla_gate_w2", "mla_w_uq", "mla_w_ukv")
LATE = tuple(n for n, _ in SHARDED if n not in EARLY)
WEIGHTS = ("norm_mix", "w_in", "gla_gate_w2", "gla_gate_b", "gla_out_norm", "mla_q_a_norm", "mla_w_uq",
           "mla_kv_a_norm", "mla_w_ukv", "mla_q_norm", "mla_k_norm", "w_out", "norm_xa", "norm_mem", "xa_w_q",
           "xa_w_kv", "xa_q_norm", "xa_k_norm", "xa_w_o", "norm_ffn", "ffn_w_gate", "ffn_w_up", "ffn_conv_w",
           "ffn_conv_b", "ffn_w_down")


_NN = ((1,), (0,))
_NT = ((1,), (1,))
_TN = ((0,), (0,))


def _dg(a, b, dims):
    return lax.dot_general(a.astype(BF16), b.astype(BF16), (dims, ((), ())), preferred_element_type=F32)


@jax.custom_vjp
def _dot_nn(a, b):
    return _dg(a, b, _NN)


_dot_nn.defvjp(lambda a, b: (_dg(a, b, _NN), (a, b)),
               lambda r, g: (_dg(g, r[1], _NT).astype(r[0].dtype), _dg(r[0], g, _TN).astype(r[1].dtype)))


@jax.custom_vjp
def _dot_nt(a, b):
    return _dg(a, b, _NT)


_dot_nt.defvjp(lambda a, b: (_dg(a, b, _NT), (a, b)),
               lambda r, g: (_dg(g, r[1], _NN).astype(r[0].dtype), _dg(g, r[0], _TN).astype(r[1].dtype)))


@jax.custom_vjp
def _dot_tn(a, b):
    return _dg(a, b, _TN)


_dot_tn.defvjp(lambda a, b: (_dg(a, b, _TN), (a, b)),
               lambda r, g: (_dg(r[1], g, _NT).astype(r[0].dtype), _dg(r[0], g, _NN).astype(r[1].dtype)))


def _rms(x, w, n=None):
    n = x.shape[-1] if n is None else n
    ms = jnp.sum(x * x, axis=-1, keepdims=True) * (1.0 / n)
    return x * lax.rsqrt(ms + EPS) * w


def _silu(x):
    return x * jax.nn.sigmoid(x)


def _log_sigmoid(x):
    return jnp.minimum(x, 0.0) - jnp.log(1.0 + jnp.exp(-jnp.abs(x)))


@jax.custom_vjp
def _rope(y, c, sa, sb):
    return y * c + pltpu.roll(y, LANES - 16, 1) * sa + pltpu.roll(y, 16, 1) * sb


def _rope_bwd(res, g):
    c, sa, sb = res
    gy = g * c + pltpu.roll(g * sa, 16, 1) + pltpu.roll(g * sb, LANES - 16, 1)
    return gy, jnp.zeros_like(c), jnp.zeros_like(sa), jnp.zeros_like(sb)


_rope.defvjp(lambda y, c, sa, sb: (_rope(y, c, sa, sb), (c, sa, sb)), _rope_bwd)


def _lane_mask(lo, hi):
    lane = lax.broadcasted_iota(jnp.int32, (1, LANES), 1)
    return ((lane >= lo) & (lane < hi)).astype(F32)


def _tile(n, t):
    t = min(n, t)
    assert n % t == 0, (n, t)
    return t


def _matmul(a, b, mode, out_dtype, name, residual=None, a_lead=None, b_lead=None):
    (a0, a1), (b0, b1) = a.shape[-2:], b.shape[-2:]
    if mode == "nn":
        m, k, k2, n = a0, a1, b0, b1
    elif mode == "nt":
        m, k, n, k2 = a0, a1, b0, b1
    else:
        k, m, k2, n = a0, a1, b0, b1
    assert k == k2, (a.shape, b.shape, mode)
    npar = 4 if "p" in (a_lead, b_lead) else 1
    nsum = 4 if "k" in (a_lead, b_lead) else 1
    if mode == "tn":
        tm = m if m <= 1408 else m // 2
        tn = n if tm * n <= 1024 * 2304 else n // 2
        tk = _tile(k, 512)
    else:
        tn, tk = n, k
        out_bytes = jnp.dtype(out_dtype).itemsize

        def vmem_need(rows):
            need = 2 * rows * tk * a.dtype.itemsize + 2 * tk * tn * b.dtype.itemsize + 2 * rows * tn * out_bytes
            need += rows * tn * 4 * (2 if nsum > 1 else 1) + rows * tk * 2
            return need + (2 * rows * tn * 4 if residual is not None else 0)

        tm = next((r for r in (2048, 1024, 512, 256, 128, 64, 32, 16, 8) if m % r == 0 and vmem_need(r) <= MATMUL_VMEM),
                  m)
    assert m % tm == 0 and n % tn == 0 and k % tk == 0
    nk = k // tk
    nred = nsum * nk
    dims = {"nn": _NN, "nt": _NT, "tn": _TN}[mode]

    def body(*refs):
        a_ref, b_ref = refs[0], refs[1]
        r_ref = refs[2] if residual is not None else None
        o_ref = refs[3 if residual is not None else 2]
        prod = _dg(a_ref[...], b_ref[...], dims)
        if nred == 1:
            o_ref[...] = (prod if r_ref is None else prod + r_ref[...]).astype(o_ref.dtype)
            return
        acc = refs[-1]
        ks, kk = pl.program_id(3), pl.program_id(4)

        @pl.when((ks == 0) & (kk == 0))
        def _():
            acc[...] = prod

        @pl.when((ks > 0) | (kk > 0))
        def _():
            acc[...] += prod

        @pl.when((ks == nsum - 1) & (kk == nk - 1))
        def _():
            r = acc[...]
            if r_ref is not None:
                r = r + r_ref[...]
            o_ref[...] = r.astype(o_ref.dtype)

    def spec(lead, blk, idx):
        if lead is None:
            return pl.BlockSpec(blk, lambda p, i, j, ks, kk: idx(i, j, kk))
        if lead == "p":
            return pl.BlockSpec((None,) + blk, lambda p, i, j, ks, kk: (p,) + idx(i, j, kk))
        return pl.BlockSpec((None,) + blk, lambda p, i, j, ks, kk: (ks,) + idx(i, j, kk))

    if mode == "nn":
        in_specs = [spec(a_lead, (tm, tk), lambda i, j, kk: (i, kk)), spec(b_lead, (tk, tn), lambda i, j, kk: (kk, j))]
    elif mode == "nt":
        in_specs = [spec(a_lead, (tm, tk), lambda i, j, kk: (i, kk)), spec(b_lead, (tn, tk), lambda i, j, kk: (j, kk))]
    else:
        in_specs = [spec(a_lead, (tk, tm), lambda i, j, kk: (kk, i)), spec(b_lead, (tk, tn), lambda i, j, kk: (kk, j))]
    args = [a, b]
    if residual is not None:
        assert npar == 1
        in_specs.append(spec(None, (tm, tn), lambda i, j, kk: (i, j)))
        args.append(residual)
    out_lead = "p" if npar > 1 else None
    return pl.pallas_call(
        body, grid=(npar, m // tm, n // tn, nsum, nk), in_specs=in_specs,
        out_specs=spec(out_lead, (tm, tn), lambda i, j, kk: (i, j)),
        out_shape=jax.ShapeDtypeStruct(((4,) if npar > 1 else ()) + (m, n), out_dtype),
        scratch_shapes=[pltpu.VMEM((tm, tn), F32)] if nred > 1 else [],
        compiler_params=pltpu.CompilerParams(
            dimension_semantics=("parallel", "parallel", "parallel", "arbitrary", "arbitrary"),
            vmem_limit_bytes=VMEM_LIMIT),
        name=name)(*args)


def _row(a, width=None, col_block=0):
    return (a, a.shape[1] if width is None else width, col_block)


def _rows_call(body, rows, consts, outs, accs=(), *, name, tile=512):
    s = rows[0][0].shape[0]
    t = _tile(s, tile)
    nr, nc, no = len(rows), len(consts), len(outs)

    def kern(*refs):
        r = [x[...] for x in refs[:nr]]
        c = [x[...] for x in refs[nr:nr + nc]]
        o_refs = refs[nr + nc:nr + nc + no]
        a_refs = refs[nr + nc + no:]
        ro, ao = body(r, c)
        for ref, val in zip(o_refs, ro, strict=True):
            ref[...] = val.astype(ref.dtype)
        if a_refs:
            @pl.when(pl.program_id(0) == 0)
            def _():
                for ref in a_refs:
                    ref[...] = jnp.zeros_like(ref)

            for ref, val in zip(a_refs, ao, strict=True):
                ref[...] += val

    in_specs = [pl.BlockSpec((t, w), functools.partial(lambda cb, i: (i, cb), cb)) for (_, w, cb) in rows]
    in_specs += [pl.BlockSpec(c.shape, lambda i: (0, 0)) for c in consts]
    out_specs = [pl.BlockSpec((t, w), lambda i: (i, 0)) for (w, _) in outs]
    out_specs += [pl.BlockSpec(shape, lambda i: (0, 0)) for shape in accs]
    out_shape = [jax.ShapeDtypeStruct((s, w), dt) for (w, dt) in outs]
    out_shape += [jax.ShapeDtypeStruct(shape, F32) for shape in accs]
    return pl.pallas_call(
        kern, grid=(s // t,), in_specs=in_specs, out_specs=out_specs, out_shape=out_shape,
        compiler_params=pltpu.CompilerParams(dimension_semantics=("arbitrary" if accs else "parallel",),
                                             vmem_limit_bytes=VMEM_LIMIT),
        name=name)(*[r[0] for r in rows], *consts)


def _gla_chunk(q, k, la, v0, v1, s0, s1):
    c = q.shape[0]
    r = lax.broadcasted_iota(jnp.int32, (c, c), 0)
    cc = lax.broadcasted_iota(jnp.int32, (c, c), 1)
    tril = cc <= r
    cum = lax.dot_general(tril.astype(F32), la, (_NN, ((), ())), precision=lax.Precision.HIGHEST,
                          preferred_element_type=F32)
    cl = jnp.sum(la, axis=0, keepdims=True)
    qd = q * (GLA_DK ** -0.5) * jnp.exp(cum)
    ki = k * jnp.exp(-cum)
    ke = k * jnp.exp(cl - cum)
    dec = jnp.exp(cl)
    outs, news = [], []
    for h, (v, s) in enumerate(((v0, s0), (v1, s1))):
        mk = _lane_mask(GLA_DK * h, GLA_DK * (h + 1))
        qh = qd * mk
        att = jnp.where(tril, _dot_nt(qh, ki), 0.0)
        outs.append(_dot_nn(att, v) + _dot_nt(qh, s))
        news.append(s * dec + _dot_tn(v, ke * mk))
    return outs[0], outs[1], news[0], news[1]


def _gla_specs(tb, rev_nb=None):
    blk = (lambda b: b) if rev_nb is None else (lambda b: rev_nb - 1 - b)
    q = pl.BlockSpec((tb, 128), lambda p, b: (blk(b), P_GQ // 128 + p))
    k = pl.BlockSpec((tb, 128), lambda p, b: (blk(b), P_GK // 128 + p))
    la = pl.BlockSpec((tb, 128), lambda p, b: (blk(b), p))
    v = pl.BlockSpec((tb, 256), lambda p, b: (blk(b), P_GV // 256 + p))
    o = pl.BlockSpec((tb, 256), lambda p, b: (blk(b), p))
    st = pl.BlockSpec((tb // GLA_CHUNK, 2, 128, 128), lambda p, b: (blk(b), p, 0, 0))
    return q, k, la, v, o, st


def _gla_fwd(proj, la):
    s = proj.shape[0]
    tb = _tile(s, 512)
    nb, nch = s // tb, tb // GLA_CHUNK

    def kern(q_ref, k_ref, la_ref, v_ref, o_ref, st_ref, s_sc):
        @pl.when(pl.program_id(1) == 0)
        def _():
            s_sc[...] = jnp.zeros_like(s_sc)

        s0, s1 = s_sc[0], s_sc[1]
        for ci in range(nch):
            sl = slice(ci * GLA_CHUNK, (ci + 1) * GLA_CHUNK)
            st_ref[ci, 0] = s0
            st_ref[ci, 1] = s1
            o0, o1, s0, s1 = _gla_chunk(q_ref[sl, :], k_ref[sl, :], la_ref[sl, :], v_ref[sl, 0:128],
                                        v_ref[sl, 128:256], s0, s1)
            o_ref[sl, 0:128] = o0
            o_ref[sl, 128:256] = o1
        s_sc[0] = s0
        s_sc[1] = s1

    q, k, lasp, v, o, st = _gla_specs(tb)
    return pl.pallas_call(
        kern, grid=(2, nb), in_specs=[q, k, lasp, v], out_specs=[o, st],
        out_shape=[jax.ShapeDtypeStruct((s, 512), F32),
                   jax.ShapeDtypeStruct((s // GLA_CHUNK, GLA_HEADS, 128, 128), F32)],
        scratch_shapes=[pltpu.VMEM((2, 128, 128), F32)],
        compiler_params=pltpu.CompilerParams(dimension_semantics=("parallel", "arbitrary"),
                                             vmem_limit_bytes=VMEM_LIMIT),
        name="gla_fwd")(proj, proj, la, proj)


def _gla_bwd(proj, la, states, d_o):
    s = proj.shape[0]
    tb = _tile(s, 512)
    nb, nch = s // tb, tb // GLA_CHUNK

    def kern(q_ref, k_ref, la_ref, v_ref, do_ref, st_ref, dq_ref, dk_ref, dla_ref, dv_ref, ds_sc):
        @pl.when(pl.program_id(1) == 0)
        def _():
            ds_sc[...] = jnp.zeros_like(ds_sc)

        d0, d1 = ds_sc[0], ds_sc[1]
        for ci in reversed(range(nch)):
            sl = slice(ci * GLA_CHUNK, (ci + 1) * GLA_CHUNK)
            _, vjp = jax.vjp(_gla_chunk, q_ref[sl, :], k_ref[sl, :], la_ref[sl, :], v_ref[sl, 0:128],
                             v_ref[sl, 128:256], st_ref[ci, 0], st_ref[ci, 1])
            gq, gk, gla, gv0, gv1, d0, d1 = vjp((do_ref[sl, 0:128], do_ref[sl, 128:256], d0, d1))
            dq_ref[sl, :] = gq
            dk_ref[sl, :] = gk
            dla_ref[sl, :] = gla
            dv_ref[sl, 0:128] = gv0
            dv_ref[sl, 128:256] = gv1
        ds_sc[0] = d0
        ds_sc[1] = d1

    q, k, lasp, v, o, st = _gla_specs(tb, rev_nb=nb)
    return pl.pallas_call(
        kern, grid=(2, nb), in_specs=[q, k, lasp, v, o, st], out_specs=[lasp, lasp, lasp, o],
        out_shape=[jax.ShapeDtypeStruct((s, 256), F32), jax.ShapeDtypeStruct((s, 256), F32),
                   jax.ShapeDtypeStruct((s, 256), F32), jax.ShapeDtypeStruct((s, 512), F32)],
        scratch_shapes=[pltpu.VMEM((2, 128, 128), F32)],
        compiler_params=pltpu.CompilerParams(dimension_semantics=("parallel", "arbitrary"),
                                             vmem_limit_bytes=VMEM_LIMIT),
        name="gla_bwd")(proj, proj, la, proj, d_o, states)


def _causal_keep(t, qi, ki):
    row = lax.broadcasted_iota(jnp.int32, (t, t), 0) + qi * t
    col = lax.broadcasted_iota(jnp.int32, (t, t), 1) + ki * t
    return col <= row


def _split_refs(refs, counts):
    out, off = [], 0
    for cnt in counts:
        out.append(refs[off:off + cnt])
        off += cnt
    return out


def _attn_fwd(q, k, v, comm, tile=1024):
    s = q.shape[0]
    t = _tile(s, tile)
    n = s // t
    nci, nco = len(comm.ins), len(comm.out_shape)

    def kern(*refs):
        (q_ref, k_ref, v_ref), cins, (o_ref, lse_ref), couts, (m_sc, l_sc, acc_sc), csems = _split_refs(
            refs, (3, nci, 2, nco, 3, len(comm.sems)))
        qi, ki = pl.program_id(1), pl.program_id(2)
        place = _place()

        @pl.when((pl.program_id(0) == 0) & (qi == 0) & (ki == 0))
        def _():
            comm.start(place, cins, couts, csems)

        first = lax.broadcasted_iota(jnp.int32, (t, LANES), 1) < MLA_V

        @pl.when(ki == 0)
        def _():
            m_sc[...] = jnp.full_like(m_sc, -jnp.inf)
            l_sc[...] = jnp.zeros_like(l_sc)
            acc_sc[...] = jnp.zeros_like(acc_sc)

        def update(diagonal):
            keep = _causal_keep(t, 0, 0)
            alphas, pvs = [], []
            for h in range(2):
                sc = _dg(q_ref[:, 128 * h:128 * (h + 1)], k_ref[:, 128 * h:128 * (h + 1)], _NT)
                if diagonal:
                    sc = jnp.where(keep, sc, -jnp.inf)
                m_prev = m_sc[h]
                m_new = jnp.maximum(m_prev, jnp.max(sc, axis=1, keepdims=True))
                alpha = jnp.exp2(m_prev - m_new)
                p = jnp.exp2(sc - m_new[:, 0:1])
                l_sc[h] = alpha * l_sc[h] + jnp.sum(p, axis=1, keepdims=True)
                m_sc[h] = m_new
                alphas.append(alpha)
                pvs.append(_dg(p, v_ref[...], _NN))
            acc_sc[...] = acc_sc[...] * jnp.where(first, alphas[0], alphas[1]) + jnp.where(first, pvs[0], pvs[1])

        @pl.when(ki < qi)
        def _():
            update(False)

        @pl.when(ki == qi)
        def _():
            update(True)

        @pl.when(ki == qi)
        def _():
            l = jnp.where(first, l_sc[0], l_sc[1])
            m = jnp.where(first, m_sc[0], m_sc[1])
            o_ref[...] = acc_sc[...] / l
            lse_ref[...] = m + jnp.log2(l)

        @pl.when((pl.program_id(0) == MLA_HEADS // 2 - 1) & (qi == n - 1) & (ki == n - 1))
        def _():
            comm.finish(place, cins, couts, csems)

    kv_idx = lambda p, qi, ki: (jnp.minimum(ki, qi), p)
    res = pl.pallas_call(
        kern, grid=(MLA_HEADS // 2, n, n),
        in_specs=[pl.BlockSpec((t, 256), lambda p, qi, ki: (qi, p)), pl.BlockSpec((t, 256), kv_idx),
                  pl.BlockSpec((t, 128), kv_idx)] + [ANY] * nci,
        out_specs=[pl.BlockSpec((t, 128), lambda p, qi, ki: (qi, p)), pl.BlockSpec((t, 128), lambda p, qi, ki: (qi, p))]
        + [ANY] * nco,
        out_shape=[jax.ShapeDtypeStruct((s, 512), F32), jax.ShapeDtypeStruct((s, 512), F32)] + comm.out_shape,
        scratch_shapes=[pltpu.VMEM((2, t, LANES), F32), pltpu.VMEM((2, t, LANES), F32), pltpu.VMEM((t, LANES), F32)]
        + comm.sems,
        compiler_params=pltpu.CompilerParams(dimension_semantics=("arbitrary", "arbitrary", "arbitrary"),
                                             vmem_limit_bytes=VMEM_LIMIT),
        name="mla_attn_fwd")(q, k, v, *comm.ins)
    return res[0], res[1], res[2:]


def _attn_bwd(q, k, v, o, lse, dcat, comm, tile=512):
    s = q.shape[0]
    t = _tile(s, tile)
    n = s // t
    nci, nco = len(comm.ins), len(comm.out_shape)

    def kern(*refs):
        (q_ref, k_ref, v_ref, o_ref, lse_ref, do_ref), cins, (dq_ref, dk_ref, dv_ref), couts, (dk_sc, dv_sc), csems = \
            _split_refs(refs, (6, nci, 3, nco, 2, len(comm.sems)))
        ki, qi = pl.program_id(1), pl.program_id(2)
        place = _place()

        @pl.when((pl.program_id(0) == 0) & (qi == 0) & (ki == 0))
        def _():
            comm.start(place, cins, couts, csems)

        @pl.when((ki == 0) & (qi == 0))
        def _():
            dq_ref[...] = jnp.zeros_like(dq_ref)

        @pl.when(qi == ki)
        def _():
            dk_sc[...] = jnp.zeros_like(dk_sc)
            dv_sc[...] = jnp.zeros_like(dv_sc)

        def update(diagonal):
            keep = _causal_keep(t, 0, 0)
            d_o = do_ref[...]
            prod = d_o * o_ref[...]
            rows = pl.ds(pl.multiple_of(qi * t, t), t)
            for h in range(2):
                hs = slice(128 * h, 128 * (h + 1))
                mk = _lane_mask(MLA_V * h, MLA_V * (h + 1))
                qh, kh = q_ref[:, hs], k_ref[:, hs]
                sc = _dg(qh, kh, _NT)
                if diagonal:
                    sc = jnp.where(keep, sc, -jnp.inf)
                p = jnp.exp2(sc - lse_ref[:, MLA_V * h:MLA_V * h + 1])
                doh = d_o * mk
                dp = _dg(doh * LN2, v_ref[...], _NT)
                delta = jnp.sum(prod * mk, axis=1, keepdims=True) * LN2
                ds = p * (dp - delta)
                dv_sc[...] += _dg(p, doh, _TN)
                dk_sc[:, hs] += _dg(ds, qh, _TN)
                dq_ref[rows, hs] += _dg(ds, kh, _NN)

        @pl.when(qi > ki)
        def _():
            update(False)

        @pl.when(qi == ki)
        def _():
            update(True)

        @pl.when(qi == n - 1)
        def _():
            dk_ref[...] = dk_sc[...]
            dv_ref[...] = dv_sc[...].astype(dv_ref.dtype)

        @pl.when((pl.program_id(0) == MLA_HEADS // 2 - 1) & (qi == n - 1) & (ki == n - 1))
        def _():
            comm.finish(place, cins, couts, csems)

    q_idx = lambda p, ki, qi: (jnp.maximum(qi, ki), p)
    res = pl.pallas_call(
        kern, grid=(MLA_HEADS // 2, n, n),
        in_specs=[pl.BlockSpec((t, 256), q_idx), pl.BlockSpec((t, 256), lambda p, ki, qi: (ki, p)),
                  pl.BlockSpec((t, 128), lambda p, ki, qi: (ki, p)), pl.BlockSpec((t, 128), q_idx),
                  pl.BlockSpec((t, 128), q_idx),
                  pl.BlockSpec((t, 128), lambda p, ki, qi: (jnp.maximum(qi, ki), 4 + p))] + [ANY] * nci,
        out_specs=[pl.BlockSpec((s, 256), lambda p, ki, qi: (0, p)), pl.BlockSpec((t, 256), lambda p, ki, qi: (ki, p)),
                   pl.BlockSpec((t, 128), lambda p, ki, qi: (ki, p))] + [ANY] * nco,
        out_shape=[jax.ShapeDtypeStruct((s, 1024), F32), jax.ShapeDtypeStruct((s, 1024), F32),
                   jax.ShapeDtypeStruct((s, 512), BF16)] + comm.out_shape,
        scratch_shapes=[pltpu.VMEM((t, 256), F32), pltpu.VMEM((t, 128), F32)] + comm.sems,
        compiler_params=pltpu.CompilerParams(dimension_semantics=("arbitrary", "arbitrary", "arbitrary"),
                                             vmem_limit_bytes=VMEM_LIMIT),
        name="mla_attn_bwd")(q, k, v, o, lse, dcat, *comm.ins)
    return res[0], res[1], res[2], res[3:]


def _gate_fn(alr, w2, b):
    return _log_sigmoid(_dot_nn(alr, w2) + b) * (1.0 / GLA_GATE_NORM)


def _qk_head(qh, kh, kpe, c, sa, sb, qn, kn):
    kfull = kh + kpe * _lane_mask(MLA_NOPE, MLA_QK)
    q_r = _rope(_rms(qh, qn, MLA_QK), c, sa, sb) * (MLA_QK ** -0.5 * LOG2E)
    k_r = _rope(_rms(kfull, kn, MLA_QK), c, sa, sb)
    return q_r, k_r


def _mix_head(o, og, gn):
    return _rms(o, gn) * _silu(og)


def _xa_head(xq, xk, xv, qn, kn):
    sc = _dot_nt(_rms(xq, qn), _rms(xk, kn)) * (XA_DIM ** -0.5)
    e = jnp.exp(sc - lax.stop_gradient(jnp.max(sc, axis=1, keepdims=True)))
    p = e / jnp.sum(e, axis=1, keepdims=True)
    return _dot_nn(p, xv)


def _heads(x, n):
    return [x[:, 128 * h:128 * (h + 1)] for h in range(n)]


def _cat(xs):
    return jnp.concatenate(xs, axis=1)


def _norm_fwd(x, w, name):
    return _rows_call(lambda r, c: ([_rms(r[0], c[0])], []), [_row(x)], [w], [(x.shape[1], BF16)], name=name)[0]


def _norm_bwd(x, w, d_out, add, name):
    def body(r, c):
        _, vjp = jax.vjp(_rms, r[0], c[0])
        dx, dw = vjp(r[1])
        return [dx + r[2]], [dw]

    return _rows_call(body, [_row(x), _row(d_out), _row(add)], [w], [(x.shape[1], F32)], [w.shape], name=name)


CONV_HALO = 8


def _conv_specs(s, f, t):
    n8 = t // CONV_HALO
    cur = pl.BlockSpec((None, t, f), lambda j, i: (j, i, 0))
    prev = pl.BlockSpec((None, CONV_HALO, f), lambda j, i: (j, jnp.maximum(i * n8 - 1, 0), 0))
    nxt = pl.BlockSpec((None, CONV_HALO, f), lambda j, i: (j, jnp.minimum((i + 1) * n8, s // CONV_HALO - 1), 0))
    cw = pl.BlockSpec((None, 3, f), lambda j, i: (j, 0, 0))
    cb = pl.BlockSpec((None, 1, f), lambda j, i: (j, 0, 0))
    return cur, prev, nxt, cw, cb


def _conv_taps(g, prev, first):
    ext = jnp.concatenate([jnp.where(first, 0.0, prev), g], axis=0)
    return pltpu.roll(ext, 1, 0)[CONV_HALO:], pltpu.roll(ext, 2, 0)[CONV_HALO:]


def _conv_fwd(gg, uu, cw, cb):
    _, s, f = gg.shape
    t = _tile(s, 512)

    def kern(g_ref, gp_ref, u_ref, cw_ref, cb_ref, o_ref):
        g = g_ref[...]
        g1, g2 = _conv_taps(g, gp_ref[...], pl.program_id(1) == 0)
        w = cw_ref[...]
        gc = cb_ref[...] + w[0:1] * g2 + w[1:2] * g1 + w[2:3] * g
        o_ref[...] = (_silu(gc) * u_ref[...]).astype(o_ref.dtype)

    cur, prev, _, cws, cbs = _conv_specs(s, f, t)
    return pl.pallas_call(
        kern, grid=(4, s // t), in_specs=[cur, prev, cur, cws, cbs], out_specs=cur,
        out_shape=jax.ShapeDtypeStruct(gg.shape, BF16),
        compiler_params=pltpu.CompilerParams(dimension_semantics=("parallel", "parallel"), vmem_limit_bytes=VMEM_LIMIT),
        name="ffn_conv_fwd")(gg, gg, uu, cw, cb)


def _conv_bwd_gate(gg, uu, dact, cw, cb):
    _, s, f = gg.shape
    t = _tile(s, 512)

    def kern(g_ref, gp_ref, u_ref, da_ref, cw_ref, cb_ref, du_ref, dgc_ref, dcw_ref, dcb_ref):
        i = pl.program_id(1)
        g, u, da = g_ref[...], u_ref[...], da_ref[...]
        g1, g2 = _conv_taps(g, gp_ref[...], i == 0)
        w = cw_ref[...]
        gc = cb_ref[...] + w[0:1] * g2 + w[1:2] * g1 + w[2:3] * g
        sg = jax.nn.sigmoid(gc)
        du_ref[...] = (da * (gc * sg)).astype(du_ref.dtype)
        dgc = da * u * (sg * (1.0 + gc * (1.0 - sg)))
        dgc_ref[...] = dgc

        @pl.when(i == 0)
        def _():
            dcw_ref[...] = jnp.zeros_like(dcw_ref)
            dcb_ref[...] = jnp.zeros_like(dcb_ref)

        dcw_ref[0:1, :] += jnp.sum(dgc * g2, axis=0, keepdims=True)
        dcw_ref[1:2, :] += jnp.sum(dgc * g1, axis=0, keepdims=True)
        dcw_ref[2:3, :] += jnp.sum(dgc * g, axis=0, keepdims=True)
        dcb_ref[...] += jnp.sum(dgc, axis=0, keepdims=True)

    cur, prev, _, cws, cbs = _conv_specs(s, f, t)
    return pl.pallas_call(
        kern, grid=(4, s // t), in_specs=[cur, prev, cur, cur, cws, cbs], out_specs=[cur, cur, cws, cbs],
        out_shape=[jax.ShapeDtypeStruct(gg.shape, BF16), jax.ShapeDtypeStruct(gg.shape, F32),
                   jax.ShapeDtypeStruct(cw.shape, F32), jax.ShapeDtypeStruct(cb.shape, F32)],
        compiler_params=pltpu.CompilerParams(dimension_semantics=("parallel", "arbitrary"), vmem_limit_bytes=VMEM_LIMIT),
        name="ffn_conv_bwd_gate")(gg, gg, uu, dact, cw, cb)


def _conv_bwd_taps(dgc, cw):
    _, s, f = dgc.shape
    t = _tile(s, 512)
    nt = s // t

    def kern(d_ref, dn_ref, cw_ref, o_ref):
        d = d_ref[...]
        ext = jnp.concatenate([d, jnp.where(pl.program_id(1) == nt - 1, 0.0, dn_ref[...])], axis=0)
        up1 = pltpu.roll(ext, t + CONV_HALO - 1, 0)[:t]
        up2 = pltpu.roll(ext, t + CONV_HALO - 2, 0)[:t]
        w = cw_ref[...]
        o_ref[...] = (w[2:3] * d + w[1:2] * up1 + w[0:1] * up2).astype(o_ref.dtype)

    cur, _, nxt, cws, _ = _conv_specs(s, f, t)
    return pl.pallas_call(
        kern, grid=(4, nt), in_specs=[cur, nxt, cws], out_specs=cur, out_shape=jax.ShapeDtypeStruct(dgc.shape, BF16),
        compiler_params=pltpu.CompilerParams(dimension_semantics=("parallel", "parallel"), vmem_limit_bytes=VMEM_LIMIT),
        name="ffn_conv_bwd_taps")(dgc, dgc, cw)


def _rope_tables(pos):
    half = MLA_ROPE // 2
    inv = ROPE_THETA ** (-jnp.arange(half, dtype=F32) / half)
    ang = pos.astype(F32)[:, None] * inv
    cos, sin = jnp.cos(ang), jnp.sin(ang)
    s = pos.shape[0]
    z = lambda w: jnp.zeros((s, w), F32)
    c = jnp.concatenate([jnp.ones((s, MLA_NOPE), F32), cos, cos, jnp.ones((s, LANES - MLA_QK), F32)], axis=1)
    sa = jnp.concatenate([z(MLA_NOPE), -sin, z(half), z(LANES - MLA_QK)], axis=1)
    sb = jnp.concatenate([z(MLA_NOPE), z(half), sin, z(LANES - MLA_QK)], axis=1)
    return c, sa, sb


def _local_step(x, mem, pos, target, w, late_shards):
    g = {}
    w = dict(w)
    c, sa, sb = _rope_tables(pos)

    xn = _norm_fwd(x, w["norm_mix"], "norm_mix_fwd")
    proj = _matmul(xn, w["in"], "nn", F32, "proj_fwd")
    alr = _row(proj, 128, P_ALR // 128)
    kpe = _row(proj, 128, P_KPE // 128)
    og = _row(proj, 512, P_OG // 512)
    cq = _row(proj, 256, P_CQ // 256)
    ckv = _row(proj, 128, P_CKV // 128)

    la = _rows_call(lambda r, k: ([_gate_fn(r[0], k[0], k[1])], []), [alr], [w["w2"], w["gate_b"]],
                    [(256, F32)], name="gla_gate_fwd")[0]
    o_gla, states = _gla_fwd(proj, la)

    q_lat, kv_lat = _rows_call(lambda r, k: ([_rms(r[0], k[0]), _rms(r[1], k[1])], []), [cq, ckv],
                               [w["q_a_norm"], w["kv_a_norm"]], [(256, BF16), (128, BF16)], name="mla_lat_fwd")
    q_up = _matmul(q_lat, w["uq"], "nn", F32, "mla_q_fwd")
    k_up = _matmul(kv_lat, w["k"], "nn", F32, "mla_k_fwd")
    v_mla = _matmul(kv_lat, w["v"], "nn", BF16, "mla_v_fwd")

    def qk_body(r, k):
        qs, ks = [], []
        for qh, kh in zip(_heads(r[0], MLA_HEADS), _heads(r[1], MLA_HEADS)):
            a, b = _qk_head(qh, kh, r[2], r[3], r[4], r[5], k[0], k[1])
            qs.append(a)
            ks.append(b)
        return [_cat(qs), _cat(ks)], []

    tabs = [_row(c), _row(sa), _row(sb)]
    q_r, k_r = _rows_call(qk_body, [_row(q_up), _row(k_up), kpe] + tabs, [w["q_norm"], w["k_norm"]],
                          [(1024, BF16), (1024, BF16)], name="mla_qk_fwd")
    o_mla, lse, gathered = _attn_fwd(q_r, k_r, v_mla, _gather_plan(late_shards))
    w.update(_late_layout(dict(zip(LATE, gathered, strict=True))))

    def mix_body(r, k):
        ys = [_mix_head(o, g_, k[0]) for o, g_ in zip(_heads(r[0], GLA_HEADS), _heads(r[1], GLA_HEADS))]
        return [_cat(ys + [r[2]])], []

    cat = _rows_call(mix_body, [_row(o_gla), og, _row(o_mla)], [w["gla_out_norm"]], [(1024, BF16)],
                     name="mix_fwd")[0]
    h1 = _matmul(cat, w["out"], "nn", F32, "out_fwd", residual=x)

    hn = _norm_fwd(h1, w["norm_xa"], "norm_xa_fwd")
    mn = _norm_fwd(mem, w["norm_mem"], "norm_mem_fwd")
    xq = _matmul(hn, w["xq"], "nn", F32, "xa_q_fwd")
    xkv = _matmul(mn, w["xkv"], "nn", F32, "xa_kv_fwd")

    def xa_body(r, k):
        ks, vs = _heads(k[0], 2 * XA_HEADS)[:XA_HEADS], _heads(k[0], 2 * XA_HEADS)[XA_HEADS:]
        return [_cat([_xa_head(a, b, v_, k[1], k[2]) for a, b, v_ in zip(_heads(r[0], XA_HEADS), ks, vs)])], []

    xo = _rows_call(xa_body, [_row(xq)], [xkv, w["xa_q_norm"], w["xa_k_norm"]], [(512, BF16)], name="xa_fwd")[0]
    h2 = _matmul(xo, w["xo"], "nn", F32, "xa_o_fwd", residual=h1)

    fn = _norm_fwd(h2, w["norm_ffn"], "norm_ffn_fwd")
    gg = _matmul(fn, w["wg"], "nn", F32, "ffn_gate_fwd", b_lead="p")
    uu = _matmul(fn, w["wu"], "nn", F32, "ffn_up_fwd", b_lead="p")
    act = _conv_fwd(gg, uu, w["cw"], w["cb"])
    y = _matmul(act, w["wd"], "nn", F32, "ffn_down_fwd", residual=h2, a_lead="k", b_lead="k")

    def loss_body(r, k):
        err = r[0] - r[1]
        part = 0.5 * jnp.sum(jnp.sum(err * err, axis=1, keepdims=True) * (1.0 / D_MODEL), axis=0, keepdims=True)
        return [err * (1.0 / D_MODEL)], [jnp.broadcast_to(part, (1, LANES))]

    dy, loss = _rows_call(loss_body, [_row(y), _row(target)], [], [(D_MODEL, F32)], [(1, LANES)], name="loss")

    g["ffn_w_down"] = _matmul(act, dy, "tn", BF16, "ffn_down_dw", a_lead="p")
    dact = _matmul(dy, w["wd"], "nt", F32, "ffn_down_dx", b_lead="p")
    duu, dgc, g["ffn_conv_w"], g["ffn_conv_b"] = _conv_bwd_gate(gg, uu, dact, w["cw"], w["cb"])
    dgg = _conv_bwd_taps(dgc, w["cw"])
    g["ffn_w_gate"] = _matmul(fn, dgg, "tn", BF16, "ffn_gate_dw", b_lead="p")
    g["ffn_w_up"] = _matmul(fn, duu, "tn", BF16, "ffn_up_dw", b_lead="p")
    dfn = _matmul(dgg, w["wg"], "nt", F32, "ffn_gate_dx", a_lead="k", b_lead="k")
    dfn = _matmul(duu, w["wu"], "nt", F32, "ffn_up_dx", residual=dfn, a_lead="k", b_lead="k")
    dh2, g["norm_ffn"] = _norm_bwd(h2, w["norm_ffn"], dfn, dy, "norm_ffn_bwd")

    g["xa_w_o"] = _matmul(xo, dh2, "tn", BF16, "xa_o_dw")
    dxo = _matmul(dh2, w["xo"], "nt", F32, "xa_o_dx")

    def xa_bwd(r, k):
        kvh = _heads(k[0], 2 * XA_HEADS)
        dq_, dk_, dv_ = [], [], []
        dqn, dkn = 0.0, 0.0
        for h, (a, d_) in enumerate(zip(_heads(r[0], XA_HEADS), _heads(r[1], XA_HEADS))):
            _, vjp = jax.vjp(_xa_head, a, kvh[h], kvh[XA_HEADS + h], k[1], k[2])
            ga, gk, gv, gqn, gkn = vjp(d_)
            dq_.append(ga)
            dk_.append(gk)
            dv_.append(gv)
            dqn, dkn = dqn + gqn, dkn + gkn
        return [_cat(dq_)], [_cat(dk_ + dv_), dqn, dkn]

    dxq, dxkv, g["xa_q_norm"], g["xa_k_norm"] = _rows_call(
        xa_bwd, [_row(xq), _row(dxo)], [xkv, w["xa_q_norm"], w["xa_k_norm"]], [(512, BF16)],
        [xkv.shape, (1, 128), (1, 128)], name="xa_bwd")
    g["xa_w_q"] = _matmul(hn, dxq, "tn", BF16, "xa_q_dw")
    dhn = _matmul(dxq, w["xq"], "nt", F32, "xa_q_dx")
    g["xa_w_kv"] = _matmul(mn, dxkv, "tn", BF16, "xa_kv_dw")
    dmn = _matmul(dxkv, w["xkv"], "nt", F32, "xa_kv_dx")
    _, g["norm_mem"] = _norm_bwd(mem, w["norm_mem"], dmn, dmn, "norm_mem_bwd")
    dh1, g["norm_xa"] = _norm_bwd(h1, w["norm_xa"], dhn, dh2, "norm_xa_bwd")

    g["w_out"] = _matmul(cat, dh1, "tn", BF16, "out_dw")
    dcat = _matmul(dh1, w["out"], "nt", F32, "out_dx")

    def mix_bwd(r, k):
        do_, dog_ = [], []
        dgn = 0.0
        for o, g_, d_ in zip(_heads(r[0], GLA_HEADS), _heads(r[1], GLA_HEADS), _heads(r[2], GLA_HEADS)):
            _, vjp = jax.vjp(_mix_head, o, g_, k[0])
            a, b, gn_ = vjp(d_)
            do_.append(a)
            dog_.append(b)
            dgn = dgn + gn_
        return [_cat(do_), _cat(dog_)], [dgn]

    do_gla, d_og, g["gla_out_norm"] = _rows_call(mix_bwd, [_row(o_gla), og, _row(dcat, 512, 0)], [w["gla_out_norm"]],
                                                 [(512, F32), (512, BF16)], [(1, 128)], name="mix_bwd")

    late_parts = _late_grad_shards(g)
    dq_r, dk_r, dv_mla, lands_late = _attn_bwd(q_r, k_r, v_mla, o_mla, lse, dcat,
                                               _scatter_plan([late_parts[n] for n in LATE]))

    def qk_bwd(r, k):
        dqs, dks = [], []
        dkpe, dqn, dkn = 0.0, 0.0, 0.0
        for qh, kh, dqh, dkh in zip(_heads(r[0], MLA_HEADS), _heads(r[1], MLA_HEADS), _heads(r[6], MLA_HEADS),
                                    _heads(r[7], MLA_HEADS)):
            _, vjp = jax.vjp(lambda a, b, e, f, h_: _qk_head(a, b, e, r[3], r[4], r[5], f, h_), qh, kh, r[2], k[0], k[1])
            ga, gb, ge, gf, gh = vjp((dqh, dkh))
            dqs.append(ga)
            dks.append(gb)
            dkpe, dqn, dkn = dkpe + ge, dqn + gf, dkn + gh
        return [_cat(dqs), _cat(dks), dkpe], [dqn, dkn]

    dq_up, dk_up, d_kpe, g["q_norm"], g["k_norm"] = _rows_call(
        qk_bwd, [_row(q_up), _row(k_up), kpe] + tabs + [_row(dq_r), _row(dk_r)], [w["q_norm"], w["k_norm"]],
        [(1024, BF16), (1024, BF16), (128, BF16)], [(1, 128), (1, 128)], name="mla_qk_bwd")
    g["uq"] = _matmul(q_lat, dq_up, "tn", BF16, "mla_q_dw")
    dq_lat = _matmul(dq_up, w["uq"], "nt", F32, "mla_q_dx")
    g["k"] = _matmul(kv_lat, dk_up, "tn", BF16, "mla_k_dw")
    g["v"] = _matmul(kv_lat, dv_mla, "tn", BF16, "mla_v_dw")
    dkv_lat = _matmul(dk_up, w["k"], "nt", F32, "mla_k_dx")
    dkv_lat = _matmul(dv_mla, w["v"], "nt", F32, "mla_v_dx", residual=dkv_lat)

    def lat_bwd(r, k):
        _, vjp1 = jax.vjp(_rms, r[0], k[0])
        _, vjp2 = jax.vjp(_rms, r[1], k[1])
        a, ga = vjp1(r[2])
        b, gb = vjp2(r[3])
        return [a, b], [ga, gb]

    d_cq, d_ckv, g["mla_q_a_norm"], g["mla_kv_a_norm"] = _rows_call(
        lat_bwd, [cq, ckv, _row(dq_lat), _row(dkv_lat)], [w["q_a_norm"], w["kv_a_norm"]],
        [(256, BF16), (128, BF16)], [(1, 256), (1, 128)], name="mla_lat_bwd")

    dgq, dgk, dla, dgv = _gla_bwd(proj, la, states, do_gla)

    def gate_bwd(r, k):
        _, vjp = jax.vjp(_gate_fn, r[0], k[0], k[1])
        a, gw, gb = vjp(r[1])
        return [a], [gw, gb]

    d_alr, g["w2"], g["gla_gate_b"] = _rows_call(gate_bwd, [alr, _row(dla)], [w["w2"], w["gate_b"]], [(128, BF16)],
                                                 [(128, 256), (1, 256)], name="gla_gate_bwd")

    dproj = jnp.concatenate([dgq.astype(BF16), dgk.astype(BF16), dgv.astype(BF16), d_og, d_cq, d_ckv, d_kpe, d_alr],
                            axis=1)
    g["in"] = _matmul(xn, dproj, "tn", BF16, "proj_dw")
    dxn = _matmul(dproj, w["in"], "nt", F32, "proj_dx")
    dx, g["norm_mix"] = _norm_bwd(x, w["norm_mix"], dxn, dh1, "norm_mix_bwd")
    return loss[0, 0], dx, g, lands_late


def _join_shards(pieces, axis):
    if axis == 0:
        return pieces.reshape(-1, pieces.shape[2])
    return jnp.transpose(pieces, (1, 0, 2)).reshape(pieces.shape[1], -1)


def _split_shards(full, axis):
    r, c = full.shape
    if axis == 0:
        return full.reshape(4, r // 4, c)
    return jnp.transpose(full.reshape(r, 4, c // 4), (1, 0, 2))


def _early_layout(gath, rep):
    w_in = _join_shards(gath["w_in"], 1)
    z = lambda n: jnp.zeros((D_MODEL, n), w_in.dtype)
    seg = lambda lo, n: w_in[:, lo:lo + n]
    ukv = _join_shards(gath["mla_w_ukv"], 1).reshape(MLA_KV_RANK, MLA_HEADS, MLA_NOPE + MLA_V)
    w = {
        "in": jnp.concatenate([seg(N_GQ, 256), seg(N_GK, 256), seg(N_GV, 512), seg(N_OG, 512), seg(N_CQ, 256),
                               seg(N_CKV, 128), z(64), seg(N_KPE, 32), z(32), seg(N_ALR, 16), z(112)], axis=1),
        "uq": jnp.pad(_join_shards(gath["mla_w_uq"], 1).reshape(MLA_Q_RANK, MLA_HEADS, MLA_QK),
                      ((0, 0), (0, 0), (0, LANES - MLA_QK))).reshape(MLA_Q_RANK, MLA_HEADS * LANES),
        "k": jnp.pad(ukv[:, :, :MLA_NOPE], ((0, 0), (0, 0), (0, LANES - MLA_NOPE))).reshape(MLA_KV_RANK, -1),
        "v": ukv[:, :, MLA_NOPE:].reshape(MLA_KV_RANK, MLA_HEADS * MLA_V),
        "w2": jnp.pad(_join_shards(gath["gla_gate_w2"], 1), ((0, LANES - GLA_RANK), (0, 0))),
        "cb": rep["ffn_conv_b"].reshape(4, 1, D_FF // 4),
        "q_norm": jnp.pad(rep["mla_q_norm"], ((0, 0), (0, LANES - MLA_QK))),
        "k_norm": jnp.pad(rep["mla_k_norm"], ((0, 0), (0, LANES - MLA_QK))),
        "q_a_norm": rep["mla_q_a_norm"], "kv_a_norm": rep["mla_kv_a_norm"], "gate_b": rep["gla_gate_b"],
    }
    for n in ("norm_mix", "gla_out_norm", "norm_xa", "norm_mem", "xa_q_norm", "xa_k_norm", "norm_ffn"):
        w[n] = rep[n]
    return w


def _late_layout(gath):
    return {"out": _join_shards(gath["w_out"], 0), "xq": _join_shards(gath["xa_w_q"], 0),
            "xkv": _join_shards(gath["xa_w_kv"], 0), "xo": _join_shards(gath["xa_w_o"], 1),
            "wg": gath["ffn_w_gate"], "wu": gath["ffn_w_up"], "wd": gath["ffn_w_down"], "cw": gath["ffn_conv_w"]}


def _late_grad_shards(g):
    sh = {"w_out": _split_shards(g["w_out"], 0), "xa_w_q": _split_shards(g["xa_w_q"], 0),
          "xa_w_kv": _split_shards(g["xa_w_kv"], 0), "xa_w_o": _split_shards(g["xa_w_o"], 1),
          "ffn_w_gate": g["ffn_w_gate"], "ffn_w_up": g["ffn_w_up"], "ffn_conv_w": g["ffn_conv_w"],
          "ffn_w_down": g["ffn_w_down"]}
    return {n: v.astype(BF16) for n, v in sh.items()}


def _early_grad_shards(g):
    gi = g["in"]
    seg = lambda lo, n: gi[:, lo:lo + n]
    w_in = jnp.concatenate([seg(P_GQ, 256), seg(P_GK, 256), seg(P_GV, 512), seg(P_ALR, 16), seg(P_OG, 512),
                            seg(P_CQ, 256), seg(P_CKV, 128), seg(P_KPE + 64, 32)], axis=1)
    uq = g["uq"].reshape(MLA_Q_RANK, MLA_HEADS, LANES)[:, :, :MLA_QK].reshape(MLA_Q_RANK, -1)
    ukv = jnp.concatenate([g["k"].reshape(MLA_KV_RANK, MLA_HEADS, LANES)[:, :, :MLA_NOPE],
                           g["v"].reshape(MLA_KV_RANK, MLA_HEADS, MLA_V)], axis=2).reshape(MLA_KV_RANK, -1)
    sh = {"w_in": _split_shards(w_in, 1), "gla_gate_w2": _split_shards(g["w2"][:GLA_RANK], 1),
          "mla_w_uq": _split_shards(uq, 1), "mla_w_ukv": _split_shards(ukv, 1)}
    sh = {n: v.astype(BF16) for n, v in sh.items()}
    rep = {n: g[n] for n in REPLICATED if n in g}
    rep["mla_q_norm"] = g["q_norm"][:, :MLA_QK]
    rep["mla_k_norm"] = g["k_norm"][:, :MLA_QK]
    rep["ffn_conv_b"] = g["ffn_conv_b"].reshape(1, D_FF)
    return sh, rep


SMALL_SHAPE = (8, 1024)


def _pack_small(vectors):
    flat = jnp.concatenate(vectors, axis=1)
    return jnp.pad(flat, ((0, 0), (0, SMALL_SHAPE[0] * SMALL_SHAPE[1] - flat.shape[1]))).reshape(SMALL_SHAPE)


def _unpack_small(buf, widths):
    flat = buf.reshape(1, -1)
    out, off = [], 0
    for wd in widths:
        out.append(flat[:, off:off + wd])
        off += wd
    return out


ANY = pl.BlockSpec(memory_space=pl.ANY)


def _place():
    x, y, c = lax.axis_index("x"), lax.axis_index("y"), lax.axis_index("c")
    chips = [(1 - x, y), (x, 1 - y), (1 - x, 1 - y)]
    return x, y, c, chips


class _Comm:
    def __init__(self, ins, out_shape, sems, start, finish):
        self.ins, self.out_shape, self.sems, self.start, self.finish = list(ins), list(out_shape), list(sems), start, finish


def _run_comm(plan, name):
    ni, no = len(plan.ins), len(plan.out_shape)

    def body(*refs):
        ins, outs, sems = refs[:ni], refs[ni:ni + no], refs[ni + no:]
        place = _place()
        plan.start(place, ins, outs, sems)
        plan.finish(place, ins, outs, sems)

    return pl.pallas_call(body, in_specs=[ANY] * ni, out_specs=[ANY] * no, out_shape=plan.out_shape,
                          scratch_shapes=plan.sems, name=name)(*plan.ins)


def _gather_plan(shards):
    n = len(shards)

    def copies(place, ins, outs, sems, landing):
        x, y, c, chips = place
        send, recv, local = sems
        me = 2 * x + y
        own = [pltpu.make_async_copy(ins[t], outs[t].at[me], local.at[t]) for t in range(n)]
        remote = []
        for t in range(n):
            for j, (px, py) in enumerate(chips):
                remote.append(pltpu.make_async_remote_copy(
                    src_ref=ins[t], dst_ref=outs[t].at[2 * px + py if landing else me], send_sem=send.at[3 * t + j],
                    recv_sem=recv.at[3 * t + j], device_id=(px, py, c), device_id_type=MESH))
        return own, remote

    def start(place, ins, outs, sems):
        own, push = copies(place, ins, outs, sems, False)
        for cp in own + push:
            cp.start()

    def finish(place, ins, outs, sems):
        own, land = copies(place, ins, outs, sems, True)
        for cp in land:
            cp.wait_recv()
        for cp in land:
            cp.wait_send()
        for cp in own:
            cp.wait()

    dma = pltpu.SemaphoreType.DMA
    return _Comm(shards, [jax.ShapeDtypeStruct((4,) + s.shape, s.dtype) for s in shards],
                 [dma((3 * n,)), dma((3 * n,)), dma((n,))], start, finish)


def _scatter_plan(parts, small=None):
    n = len(parts)
    ns = 0 if small is None else 1

    def unpack(place, ins, outs, sems):
        x, y, c, chips = place
        return x, y, c, chips, 2 * x + y, 4 * x + 2 * y + c, (x, y, 1 - c)

    def remote(src, dst, ss, rs, to):
        return pltpu.make_async_remote_copy(src_ref=src, dst_ref=dst, send_sem=ss, recv_sem=rs, device_id=to,
                                            device_id_type=MESH)

    def first_wave(place, ins, outs, sems):
        x, y, c, chips, me, dev, sib = unpack(place, ins, outs, sems)
        ici_s, ici_r, d2d_s, d2d_r, sm_s, sm_r, local = sems
        own, push = [], []
        if ns:
            own.append(pltpu.make_async_copy(ins[n], outs[n].at[dev], local.at[n]))
            for k in range(1, 8):
                px = (1 - x) if (k >> 2) & 1 else x
                py = (1 - y) if (k >> 1) & 1 else y
                pc = (1 - c) if k & 1 else c
                push.append(remote(ins[n], outs[n].at[dev], sm_s.at[k - 1], sm_r.at[k - 1], (px, py, pc)))
        for t in range(n):
            own.append(pltpu.make_async_copy(ins[t].at[me], outs[t].at[dev], local.at[t]))
            push.append(remote(ins[t].at[me], outs[t].at[dev], d2d_s.at[4 * t], d2d_r.at[4 * t], sib))
            for j, (px, py) in enumerate(chips):
                push.append(remote(ins[t].at[2 * px + py], outs[t].at[dev], ici_s.at[3 * t + j], ici_r.at[3 * t + j],
                                   (px, py, c)))
        return own, push

    def start(place, ins, outs, sems):
        own, push = first_wave(place, ins, outs, sems)
        for cp in own + push:
            cp.start()

    def finish(place, ins, outs, sems):
        x, y, c, chips, me, dev, sib = unpack(place, ins, outs, sems)
        ici_s, ici_r, d2d_s, d2d_r, sm_s, sm_r, local = sems
        own, push = first_wave(place, ins, outs, sems)

        def landed(dst, rs):
            remote(dst, dst, local.at[0], rs, sib).wait_recv()

        for t in range(n):
            for j, (px, py) in enumerate(chips):
                slot = outs[t].at[4 * px + 2 * py + c]
                landed(slot, ici_r.at[3 * t + j])
                cp = remote(slot, slot, d2d_s.at[4 * t + 1 + j], d2d_r.at[4 * t + 1 + j], sib)
                cp.start()
                push.append(cp)
        for t in range(n):
            landed(outs[t].at[4 * x + 2 * y + (1 - c)], d2d_r.at[4 * t])
            for j, (px, py) in enumerate(chips):
                landed(outs[t].at[4 * px + 2 * py + (1 - c)], d2d_r.at[4 * t + 1 + j])
        if ns:
            for k in range(1, 8):
                px = (1 - x) if (k >> 2) & 1 else x
                py = (1 - y) if (k >> 1) & 1 else y
                pc = (1 - c) if k & 1 else c
                landed(outs[n].at[4 * px + 2 * py + pc], sm_r.at[k - 1])
        for cp in push:
            cp.wait_send()
        for cp in own:
            cp.wait()

    dma = pltpu.SemaphoreType.DMA
    ins = list(parts) + ([small] if ns else [])
    out_shape = [jax.ShapeDtypeStruct((8,) + p.shape[1:], p.dtype) for p in parts]
    if ns:
        out_shape.append(jax.ShapeDtypeStruct((8,) + small.shape, small.dtype))
    return _Comm(ins, out_shape, [dma((3 * n,)), dma((3 * n,)), dma((4 * n,)), dma((4 * n,)), dma((7,)), dma((7,)),
                                  dma((n + 1,))], start, finish)


def _row_tile(r, cap=256):
    if r <= cap:
        return r
    return max(t for t in range(8, cap + 1, 8) if r % t == 0)


def _adamw(w, m, v, land, name):
    r, c = w.shape
    t = _row_tile(r)

    def kern(w_ref, m_ref, v_ref, l_ref, g_out, d_out, m_out, v_out):
        g = l_ref[0].astype(F32)
        for i in range(1, 8):
            g = g + l_ref[i].astype(F32)
        m_new = ADAM_B1 * m_ref[...] + (1.0 - ADAM_B1) * g
        v_new = ADAM_B2 * v_ref[...] + (1.0 - ADAM_B2) * (g * g)
        m_hat = m_new / (1.0 - ADAM_B1 ** ADAM_STEP)
        v_hat = v_new / (1.0 - ADAM_B2 ** ADAM_STEP)
        g_out[...] = g
        d_out[...] = -ADAM_LR * (m_hat / (jnp.sqrt(v_hat) + ADAM_EPS) + ADAM_WD * w_ref[...])
        m_out[...] = m_new
        v_out[...] = v_new

    spec = pl.BlockSpec((t, c), lambda i: (i, 0))
    return pl.pallas_call(
        kern, grid=(r // t,), in_specs=[spec] * 3 + [pl.BlockSpec((8, t, c), lambda i: (0, i, 0))], out_specs=[spec] * 4,
        out_shape=[jax.ShapeDtypeStruct((r, c), F32)] * 4,
        compiler_params=pltpu.CompilerParams(dimension_semantics=("parallel",), vmem_limit_bytes=VMEM_LIMIT),
        name=name)(w, m, v, land)


def _step(a):
    sq = lambda n: a[n][0] if a[n].ndim == 3 else a[n]
    payload = lambda n: sq(n) if n in EXACT_GATHER else sq(n).astype(BF16)

    gathered = _run_comm(_gather_plan([payload(n) for n in EARLY]), "gather_early")
    w = _early_layout(dict(zip(EARLY, gathered, strict=True)), {n: a[n] for n in REPLICATED})

    loss, dx, g, lands_late = _local_step(sq("x"), sq("mem"), a["positions"][0], sq("loss_target"), w,
                                          [payload(n) for n in LATE])

    sh, rep = _early_grad_shards(g)
    *lands_early, land_small = _run_comm(
        _scatter_plan([sh[n] for n in EARLY], _pack_small([rep[n] for n in REPLICATED])), "scatter_last")
    lands = dict(zip(EARLY + LATE, list(lands_early) + list(lands_late), strict=True))

    outs = {}
    kinds = ("grad_", "delta_", "new_m_", "new_v_")
    for n, _ in SHARDED:
        res = _adamw(sq(n), sq("m_" + n), sq("v_" + n), lands[n], "adamw_" + n)
        for kind, val in zip(kinds, res, strict=True):
            outs[kind + n] = val.reshape(a[n].shape)
    packed = [_pack_small([a[p + n] for n in REPLICATED]) for p in ("", "m_", "v_")]
    res = _adamw(*packed, land_small, "adamw_replicated")
    widths = [a[n].shape[1] for n in REPLICATED]
    for kind, buf in zip(kinds, res, strict=True):
        for n, val in zip(REPLICATED, _unpack_small(buf, widths), strict=True):
            outs[kind + n] = val

    loss = lax.psum(loss, ("x", "y", "c"))
    ordered = [outs[kind + n] for kind in kinds for n in WEIGHTS]
    return (loss, dx[None], *ordered)


def kernel(x, mem, positions, norm_mix, w_in, gla_gate_w2, gla_gate_b, gla_out_norm, mla_q_a_norm, mla_w_uq, mla_kv_a_norm, mla_w_ukv, mla_q_norm, mla_k_norm, w_out, norm_xa, norm_mem, xa_w_q, xa_w_kv, xa_q_norm, xa_k_norm, xa_w_o, norm_ffn, ffn_w_gate, ffn_w_up, ffn_conv_w, ffn_conv_b, ffn_w_down, loss_target, m_norm_mix, m_w_in, m_gla_gate_w2, m_gla_gate_b, m_gla_out_norm, m_mla_q_a_norm, m_mla_w_uq, m_mla_kv_a_norm, m_mla_w_ukv, m_mla_q_norm, m_mla_k_norm, m_w_out, m_norm_xa, m_norm_mem, m_xa_w_q, m_xa_w_kv, m_xa_q_norm, m_xa_k_norm, m_xa_w_o, m_norm_ffn, m_ffn_w_gate, m_ffn_w_up, m_ffn_conv_w, m_ffn_conv_b, m_ffn_w_down, v_norm_mix, v_w_in, v_gla_gate_w2, v_gla_gate_b, v_gla_out_norm, v_mla_q_a_norm, v_mla_w_uq, v_mla_kv_a_norm, v_mla_w_ukv, v_mla_q_norm, v_mla_k_norm, v_w_out, v_norm_xa, v_norm_mem, v_xa_w_q, v_xa_w_kv, v_xa_q_norm, v_xa_k_norm, v_xa_w_o, v_norm_ffn, v_ffn_w_gate, v_ffn_w_up, v_ffn_conv_w, v_ffn_conv_b, v_ffn_w_down):
    return _step(dict(locals()))
```

```python
import functools

import jax
import jax.numpy as jnp
from jax import lax
from jax.experimental import pallas as pl
from jax.experimental.pallas import tpu as pltpu

F32, BF16 = jnp.float32, jnp.bfloat16
MESH = pl.DeviceIdType.MESH

D_MODEL = 1024
EPS = 1e-6
GLA_HEADS, GLA_DK, GLA_DV, GLA_RANK, GLA_CHUNK = 4, 64, 128, 16, 64
GLA_GATE_NORM = 16.0
MLA_HEADS, MLA_Q_RANK, MLA_KV_RANK, MLA_NOPE, MLA_ROPE, MLA_V = 8, 256, 128, 64, 32, 64
MLA_QK = MLA_NOPE + MLA_ROPE
ROPE_THETA = 10000.0
LOG2E, LN2 = 1.4426950408889634, 0.6931471805599453
XA_HEADS, XA_DIM = 4, 128
D_FF = 2816
ADAM_LR, ADAM_B1, ADAM_B2, ADAM_EPS, ADAM_WD, ADAM_STEP = 0.001, 0.9, 0.999, 1e-08, 0.01, 10

LANES = 128
VMEM_LIMIT = 56 * 1024 * 1024
MATMUL_VMEM = 44 * 1024 * 1024

P_GQ, P_GK, P_GV, P_OG, P_CQ, P_CKV, P_KPE, P_ALR, P_WIDTH = 0, 256, 512, 1024, 1536, 1792, 1920, 2048, 2176
N_GQ, N_GK, N_GV, N_ALR, N_OG, N_CQ, N_CKV, N_KPE, N_WIDTH = 0, 256, 512, 1024, 1040, 1552, 1808, 1936, 1968

SHARDED = (("w_in", 1), ("gla_gate_w2", 1), ("mla_w_uq", 1), ("mla_w_ukv", 1), ("w_out", 0), ("xa_w_q", 0),
           ("xa_w_kv", 0), ("xa_w_o", 1), ("ffn_w_gate", 1), ("ffn_w_up", 1), ("ffn_conv_w", 1), ("ffn_w_down", 0))
REPLICATED = ("norm_mix", "gla_gate_b", "gla_out_norm", "mla_q_a_norm", "mla_kv_a_norm", "mla_q_norm", "mla_k_norm",
              "norm_xa", "norm_mem", "xa_q_norm", "xa_k_norm", "norm_ffn", "ffn_conv_b")
EXACT_GATHER = ("gla_gate_w2", "ffn_conv_w")
EARLY = ("w_in", "gla_gate_w2", "mla_w_uq", "mla_w_ukv")
LATE = tuple(n for n, _ in SHARDED if n not in EARLY)
WEIGHTS = ("norm_mix", "w_in", "gla_gate_w2", "gla_gate_b", "gla_out_norm", "mla_q_a_norm", "mla_w_uq",
           "mla_kv_a_norm", "mla_w_ukv", "mla_q_norm", "mla_k_norm", "w_out", "norm_xa", "norm_mem", "xa_w_q",
           "xa_w_kv", "xa_q_norm", "xa_k_norm", "xa_w_o", "norm_ffn", "ffn_w_gate", "ffn_w_up", "ffn_conv_w",
           "ffn_conv_b", "ffn_w_down")


_NN = ((1,), (0,))
_NT = ((1,), (1,))
_TN = ((0,), (0,))


def _dg(a, b, dims):
    return lax.dot_general(a.astype(BF16), b.astype(BF16), (dims, ((), ())), preferred_element_type=F32)


@jax.custom_vjp
def _dot_nn(a, b):
    return _dg(a, b, _NN)


_dot_nn.defvjp(lambda a, b: (_dg(a, b, _NN), (a, b)),
               lambda r, g: (_dg(g, r[1], _NT).astype(r[0].dtype), _dg(r[0], g, _TN).astype(r[1].dtype)))


@jax.custom_vjp
def _dot_nt(a, b):
    return _dg(a, b, _NT)


_dot_nt.defvjp(lambda a, b: (_dg(a, b, _NT), (a, b)),
               lambda r, g: (_dg(g, r[1], _NN).astype(r[0].dtype), _dg(g, r[0], _TN).astype(r[1].dtype)))


@jax.custom_vjp
def _dot_tn(a, b):
    return _dg(a, b, _TN)


_dot_tn.defvjp(lambda a, b: (_dg(a, b, _TN), (a, b)),
               lambda r, g: (_dg(r[1], g, _NT).astype(r[0].dtype), _dg(r[0], g, _NN).astype(r[1].dtype)))


def _rms(x, w, n=None):
    n = x.shape[-1] if n is None else n
    ms = jnp.sum(x * x, axis=-1, keepdims=True) * (1.0 / n)
    return x * lax.rsqrt(ms + EPS) * w


def _silu(x):
    return x * jax.nn.sigmoid(x)


def _log_sigmoid(x):
    return jnp.minimum(x, 0.0) - jnp.log(1.0 + jnp.exp(-jnp.abs(x)))


@jax.custom_vjp
def _rope(y, c, sa, sb):
    return y * c + pltpu.roll(y, LANES - 16, 1) * sa + pltpu.roll(y, 16, 1) * sb


def _rope_bwd(res, g):
    c, sa, sb = res
    gy = g * c + pltpu.roll(g * sa, 16, 1) + pltpu.roll(g * sb, LANES - 16, 1)
    return gy, jnp.zeros_like(c), jnp.zeros_like(sa), jnp.zeros_like(sb)


_rope.defvjp(lambda y, c, sa, sb: (_rope(y, c, sa, sb), (c, sa, sb)), _rope_bwd)


def _lane_mask(lo, hi):
    lane = lax.broadcasted_iota(jnp.int32, (1, LANES), 1)
    return ((lane >= lo) & (lane < hi)).astype(F32)


def _tile(n, t):
    t = min(n, t)
    assert n % t == 0, (n, t)
    return t


def _matmul(a, b, mode, out_dtype, name, residual=None, a_lead=None, b_lead=None):
    (a0, a1), (b0, b1) = a.shape[-2:], b.shape[-2:]
    if mode == "nn":
        m, k, k2, n = a0, a1, b0, b1
    elif mode == "nt":
        m, k, n, k2 = a0, a1, b0, b1
    else:
        k, m, k2, n = a0, a1, b0, b1
    assert k == k2, (a.shape, b.shape, mode)
    npar = 4 if "p" in (a_lead, b_lead) else 1
    nsum = 4 if "k" in (a_lead, b_lead) else 1
    a_item, b_item, o_item = a.dtype.itemsize, b.dtype.itemsize, jnp.dtype(out_dtype).itemsize

    def vmem_need(tm, tn, tk):
        need = 2 * (nsum if a_lead == "k" else 1) * tm * tk * a_item + 2 * (nsum if b_lead == "k" else 1) * tk * tn * b_item
        need += 2 * tm * tn * o_item + tm * tn * 4 * (2 if tk < k else 1)
        need += tm * tk * 2 * (a_item == 4 or mode == "tn") + tk * tn * 2 * (b_item == 4)
        return need + (2 * tm * tn * 4 if residual is not None else 0)

    halvings = (4096, 2048, 1024, 512, 256, 128, 64, 32, 16, 8)
    if mode == "tn":
        tm = m if m <= 1408 else m // 2
        tn = n if tm * n <= 1024 * 2304 else n // 2
        tk = next((r for r in halvings if k % r == 0 and vmem_need(tm, tn, r) <= MATMUL_VMEM), k)
    else:
        tn, tk = n, k
        tm = next((r for r in halvings if m % r == 0 and vmem_need(r, tn, tk) <= MATMUL_VMEM), m)
    assert m % tm == 0 and n % tn == 0 and k % tk == 0
    nk = k // tk
    dims = {"nn": _NN, "nt": _NT, "tn": _TN}[mode]

    def body(*refs):
        a_ref, b_ref = refs[0], refs[1]
        r_ref = refs[2] if residual is not None else None
        o_ref = refs[3 if residual is not None else 2]
        prod = None
        for sh in range(nsum):
            term = _dg(a_ref[sh] if a_lead == "k" else a_ref[...], b_ref[sh] if b_lead == "k" else b_ref[...], dims)
            prod = term if prod is None else prod + term
        if nk == 1:
            o_ref[...] = (prod if r_ref is None else prod + r_ref[...]).astype(o_ref.dtype)
            return
        acc = refs[-1]
        kk = pl.program_id(3)

        @pl.when(kk == 0)
        def _():
            acc[...] = prod

        @pl.when(kk > 0)
        def _():
            acc[...] += prod

        @pl.when(kk == nk - 1)
        def _():
            r = acc[...]
            if r_ref is not None:
                r = r + r_ref[...]
            o_ref[...] = r.astype(o_ref.dtype)

    def spec(lead, blk, idx):
        if lead is None:
            return pl.BlockSpec(blk, lambda p, i, j, kk: idx(i, j, kk))
        if lead == "p":
            return pl.BlockSpec((None,) + blk, lambda p, i, j, kk: (p,) + idx(i, j, kk))
        return pl.BlockSpec((nsum,) + blk, lambda p, i, j, kk: (0,) + idx(i, j, kk))

    if mode == "nn":
        in_specs = [spec(a_lead, (tm, tk), lambda i, j, kk: (i, kk)), spec(b_lead, (tk, tn), lambda i, j, kk: (kk, j))]
    elif mode == "nt":
        in_specs = [spec(a_lead, (tm, tk), lambda i, j, kk: (i, kk)), spec(b_lead, (tn, tk), lambda i, j, kk: (j, kk))]
    else:
        in_specs = [spec(a_lead, (tk, tm), lambda i, j, kk: (kk, i)), spec(b_lead, (tk, tn), lambda i, j, kk: (kk, j))]
    args = [a, b]
    if residual is not None:
        assert npar == 1
        in_specs.append(spec(None, (tm, tn), lambda i, j, kk: (i, j)))
        args.append(residual)
    return pl.pallas_call(
        body, grid=(npar, m // tm, n // tn, nk), in_specs=in_specs,
        out_specs=spec("p" if npar > 1 else None, (tm, tn), lambda i, j, kk: (i, j)),
        out_shape=jax.ShapeDtypeStruct(((4,) if npar > 1 else ()) + (m, n), out_dtype),
        scratch_shapes=[pltpu.VMEM((tm, tn), F32)] if nk > 1 else [],
        compiler_params=pltpu.CompilerParams(dimension_semantics=("parallel", "parallel", "parallel", "arbitrary"),
                                             vmem_limit_bytes=VMEM_LIMIT),
        name=name)(*args)


def _row(a, width=None, col_block=0):
    return (a, a.shape[1] if width is None else width, col_block)


def _rows_call(body, rows, consts, outs, accs=(), *, name, tile=512):
    s = rows[0][0].shape[0]
    t = _tile(s, tile)
    nr, nc, no = len(rows), len(consts), len(outs)

    def kern(*refs):
        r = [x[...] for x in refs[:nr]]
        c = [x[...] for x in refs[nr:nr + nc]]
        o_refs = refs[nr + nc:nr + nc + no]
        a_refs = refs[nr + nc + no:]
        ro, ao = body(r, c)
        for ref, val in zip(o_refs, ro, strict=True):
            ref[...] = val.astype(ref.dtype)
        if a_refs:
            @pl.when(pl.program_id(0) == 0)
            def _():
                for ref in a_refs:
                    ref[...] = jnp.zeros_like(ref)

            for ref, val in zip(a_refs, ao, strict=True):
                ref[...] += val

    in_specs = [pl.BlockSpec((t, w), functools.partial(lambda cb, i: (i, cb), cb)) for (_, w, cb) in rows]
    in_specs += [pl.BlockSpec(c.shape, lambda i: (0, 0)) for c in consts]
    out_specs = [pl.BlockSpec((t, w), lambda i: (i, 0)) for (w, _) in outs]
    out_specs += [pl.BlockSpec(shape, lambda i: (0, 0)) for shape in accs]
    out_shape = [jax.ShapeDtypeStruct((s, w), dt) for (w, dt) in outs]
    out_shape += [jax.ShapeDtypeStruct(shape, F32) for shape in accs]
    return pl.pallas_call(
        kern, grid=(s // t,), in_specs=in_specs, out_specs=out_specs, out_shape=out_shape,
        compiler_params=pltpu.CompilerParams(dimension_semantics=("arbitrary" if accs else "parallel",),
                                             vmem_limit_bytes=VMEM_LIMIT),
        name=name)(*[r[0] for r in rows], *consts)


def _gla_chunk(q, k, la, v0, v1, s0, s1):
    c = q.shape[0]
    r = lax.broadcasted_iota(jnp.int32, (c, c), 0)
    cc = lax.broadcasted_iota(jnp.int32, (c, c), 1)
    tril = cc <= r
    cum = lax.dot_general(tril.astype(F32), la, (_NN, ((), ())), precision=lax.Precision.HIGHEST,
                          preferred_element_type=F32)
    cl = jnp.sum(la, axis=0, keepdims=True)
    qd = q * (GLA_DK ** -0.5) * jnp.exp(cum)
    ki = k * jnp.exp(-cum)
    ke = k * jnp.exp(cl - cum)
    dec = jnp.exp(cl)
    outs, news = [], []
    for h, (v, s) in enumerate(((v0, s0), (v1, s1))):
        mk = _lane_mask(GLA_DK * h, GLA_DK * (h + 1))
        qh = qd * mk
        att = jnp.where(tril, _dot_nt(qh, ki), 0.0)
        outs.append(_dot_nn(att, v) + _dot_nt(qh, s))
        news.append(s * dec + _dot_tn(v, ke * mk))
    return outs[0], outs[1], news[0], news[1]


def _gla_specs(tb, rev_nb=None):
    blk = (lambda b: b) if rev_nb is None else (lambda b: rev_nb - 1 - b)
    q = pl.BlockSpec((tb, 128), lambda p, b: (blk(b), P_GQ // 128 + p))
    k = pl.BlockSpec((tb, 128), lambda p, b: (blk(b), P_GK // 128 + p))
    la = pl.BlockSpec((tb, 128), lambda p, b: (blk(b), p))
    v = pl.BlockSpec((tb, 256), lambda p, b: (blk(b), P_GV // 256 + p))
    o = pl.BlockSpec((tb, 256), lambda p, b: (blk(b), p))
    st = pl.BlockSpec((tb // GLA_CHUNK, 2, 128, 128), lambda p, b: (blk(b), p, 0, 0))
    return q, k, la, v, o, st


def _gla_fwd(proj, la):
    s = proj.shape[0]
    tb = _tile(s, 512)
    nb, nch = s // tb, tb // GLA_CHUNK

    def kern(q_ref, k_ref, la_ref, v_ref, o_ref, st_ref, s_sc):
        @pl.when(pl.program_id(1) == 0)
        def _():
            s_sc[...] = jnp.zeros_like(s_sc)

        s0, s1 = s_sc[0], s_sc[1]
        for ci in range(nch):
            sl = slice(ci * GLA_CHUNK, (ci + 1) * GLA_CHUNK)
            st_ref[ci, 0] = s0
            st_ref[ci, 1] = s1
            o0, o1, s0, s1 = _gla_chunk(q_ref[sl, :], k_ref[sl, :], la_ref[sl, :], v_ref[sl, 0:128],
                                        v_ref[sl, 128:256], s0, s1)
            o_ref[sl, 0:128] = o0
            o_ref[sl, 128:256] = o1
        s_sc[0] = s0
        s_sc[1] = s1

    q, k, lasp, v, o, st = _gla_specs(tb)
    return pl.pallas_call(
        kern, grid=(2, nb), in_specs=[q, k, lasp, v], out_specs=[o, st],
        out_shape=[jax.ShapeDtypeStruct((s, 512), F32),
                   jax.ShapeDtypeStruct((s // GLA_CHUNK, GLA_HEADS, 128, 128), F32)],
        scratch_shapes=[pltpu.VMEM((2, 128, 128), F32)],
        compiler_params=pltpu.CompilerParams(dimension_semantics=("parallel", "arbitrary"),
                                             vmem_limit_bytes=VMEM_LIMIT),
        name="gla_fwd")(proj, proj, la, proj)


def _gla_bwd(proj, la, states, d_o):
    s = proj.shape[0]
    tb = _tile(s, 512)
    nb, nch = s // tb, tb // GLA_CHUNK

    def kern(q_ref, k_ref, la_ref, v_ref, do_ref, st_ref, dq_ref, dk_ref, dla_ref, dv_ref, ds_sc):
        @pl.when(pl.program_id(1) == 0)
        def _():
            ds_sc[...] = jnp.zeros_like(ds_sc)

        d0, d1 = ds_sc[0], ds_sc[1]
        for ci in reversed(range(nch)):
            sl = slice(ci * GLA_CHUNK, (ci + 1) * GLA_CHUNK)
            _, vjp = jax.vjp(_gla_chunk, q_ref[sl, :], k_ref[sl, :], la_ref[sl, :], v_ref[sl, 0:128],
                             v_ref[sl, 128:256], st_ref[ci, 0], st_ref[ci, 1])
            gq, gk, gla, gv0, gv1, d0, d1 = vjp((do_ref[sl, 0:128], do_ref[sl, 128:256], d0, d1))
            dq_ref[sl, :] = gq
            dk_ref[sl, :] = gk
            dla_ref[sl, :] = gla
            dv_ref[sl, 0:128] = gv0
            dv_ref[sl, 128:256] = gv1
        ds_sc[0] = d0
        ds_sc[1] = d1

    q, k, lasp, v, o, st = _gla_specs(tb, rev_nb=nb)
    return pl.pallas_call(
        kern, grid=(2, nb), in_specs=[q, k, lasp, v, o, st], out_specs=[lasp, lasp, lasp, o],
        out_shape=[jax.ShapeDtypeStruct((s, 256), F32), jax.ShapeDtypeStruct((s, 256), F32),
                   jax.ShapeDtypeStruct((s, 256), F32), jax.ShapeDtypeStruct((s, 512), F32)],
        scratch_shapes=[pltpu.VMEM((2, 128, 128), F32)],
        compiler_params=pltpu.CompilerParams(dimension_semantics=("parallel", "arbitrary"),
                                             vmem_limit_bytes=VMEM_LIMIT),
        name="gla_bwd")(proj, proj, la, proj, d_o, states)


def _causal_keep(t, qi, ki):
    row = lax.broadcasted_iota(jnp.int32, (t, t), 0) + qi * t
    col = lax.broadcasted_iota(jnp.int32, (t, t), 1) + ki * t
    return col <= row


def _split_refs(refs, counts):
    out, off = [], 0
    for cnt in counts:
        out.append(refs[off:off + cnt])
        off += cnt
    return out


def _attn_fwd(q, k, v, comm, tile=1024):
    s = q.shape[0]
    t = _tile(s, tile)
    n = s // t
    nci, nco = len(comm.ins), len(comm.out_shape)

    def kern(*refs):
        (q_ref, k_ref, v_ref), cins, (o_ref, lse_ref), couts, (m_sc, l_sc, acc_sc), csems = _split_refs(
            refs, (3, nci, 2, nco, 3, len(comm.sems)))
        qi, ki = pl.program_id(1), pl.program_id(2)
        place = _place()

        @pl.when((pl.program_id(0) == 0) & (qi == 0) & (ki == 0))
        def _():
            comm.start(place, cins, couts, csems)

        @pl.when((pl.program_id(0) == MLA_HEADS // 2 - 1) & (qi == 0) & (ki == 0))
        def _():
            comm.mid(place, cins, couts, csems)

        first = lax.broadcasted_iota(jnp.int32, (t, LANES), 1) < MLA_V

        @pl.when(ki == 0)
        def _():
            m_sc[...] = jnp.full_like(m_sc, -jnp.inf)
            l_sc[...] = jnp.zeros_like(l_sc)
            acc_sc[...] = jnp.zeros_like(acc_sc)

        def update(diagonal):
            keep = _causal_keep(t, 0, 0)
            alphas, pvs = [], []
            for h in range(2):
                sc = _dg(q_ref[:, 128 * h:128 * (h + 1)], k_ref[:, 128 * h:128 * (h + 1)], _NT)
                if diagonal:
                    sc = jnp.where(keep, sc, -jnp.inf)
                m_prev = m_sc[h]
                m_new = jnp.maximum(m_prev, jnp.max(sc, axis=1, keepdims=True))
                alpha = jnp.exp2(m_prev - m_new)
                p = jnp.exp2(sc - m_new[:, 0:1])
                l_sc[h] = alpha * l_sc[h] + jnp.sum(p, axis=1, keepdims=True)
                m_sc[h] = m_new
                alphas.append(alpha)
                pvs.append(_dg(p, v_ref[...], _NN))
            acc_sc[...] = acc_sc[...] * jnp.where(first, alphas[0], alphas[1]) + jnp.where(first, pvs[0], pvs[1])

        @pl.when(ki < qi)
        def _():
            update(False)

        @pl.when(ki == qi)
        def _():
            update(True)

        @pl.when(ki == qi)
        def _():
            l = jnp.where(first, l_sc[0], l_sc[1])
            m = jnp.where(first, m_sc[0], m_sc[1])
            o_ref[...] = acc_sc[...] / l
            lse_ref[...] = m + jnp.log2(l)

        @pl.when((pl.program_id(0) == MLA_HEADS // 2 - 1) & (qi == n - 1) & (ki == n - 1))
        def _():
            comm.finish(place, cins, couts, csems)

    kv_idx = lambda p, qi, ki: (jnp.minimum(ki, qi), p)
    res = pl.pallas_call(
        kern, grid=(MLA_HEADS // 2, n, n),
        in_specs=[pl.BlockSpec((t, 256), lambda p, qi, ki: (qi, p)), pl.BlockSpec((t, 256), kv_idx),
                  pl.BlockSpec((t, 128), kv_idx)] + [ANY] * nci,
        out_specs=[pl.BlockSpec((t, 128), lambda p, qi, ki: (qi, p)), pl.BlockSpec((t, 128), lambda p, qi, ki: (qi, p))]
        + [ANY] * nco,
        out_shape=[jax.ShapeDtypeStruct((s, 512), F32), jax.ShapeDtypeStruct((s, 512), F32)] + comm.out_shape,
        scratch_shapes=[pltpu.VMEM((2, t, LANES), F32), pltpu.VMEM((2, t, LANES), F32), pltpu.VMEM((t, LANES), F32)]
        + comm.sems,
        compiler_params=pltpu.CompilerParams(dimension_semantics=("arbitrary", "arbitrary", "arbitrary"),
                                             vmem_limit_bytes=VMEM_LIMIT),
        name="mla_attn_fwd")(q, k, v, *comm.ins)
    return res[0], res[1], res[2:]


def _attn_bwd(q, k, v, o, lse, dcat, comm, tile=512):
    s = q.shape[0]
    t = _tile(s, tile)
    n = s // t
    nci, nco = len(comm.ins), len(comm.out_shape)

    def kern(*refs):
        (q_ref, k_ref, v_ref, o_ref, lse_ref, do_ref), cins, (dq_ref, dk_ref, dv_ref), couts, (dk_sc, dv_sc), csems = \
            _split_refs(refs, (6, nci, 3, nco, 2, len(comm.sems)))
        ki, qi = pl.program_id(1), pl.program_id(2)
        place = _place()

        @pl.when((pl.program_id(0) == 0) & (qi == 0) & (ki == 0))
        def _():
            comm.start(place, cins, couts, csems)

        @pl.when((pl.program_id(0) == MLA_HEADS // 2 - 1) & (qi == 0) & (ki == 0))
        def _():
            comm.mid(place, cins, couts, csems)

        @pl.when((ki == 0) & (qi == 0))
        def _():
            dq_ref[...] = jnp.zeros_like(dq_ref)

        @pl.when(qi == ki)
        def _():
            dk_sc[...] = jnp.zeros_like(dk_sc)
            dv_sc[...] = jnp.zeros_like(dv_sc)

        def update(diagonal):
            keep = _causal_keep(t, 0, 0)
            d_o = do_ref[...]
            prod = d_o * o_ref[...]
            rows = pl.ds(pl.multiple_of(qi * t, t), t)
            for h in range(2):
                hs = slice(128 * h, 128 * (h + 1))
                mk = _lane_mask(MLA_V * h, MLA_V * (h + 1))
                qh, kh = q_ref[:, hs], k_ref[:, hs]
                sc = _dg(qh, kh, _NT)
                if diagonal:
                    sc = jnp.where(keep, sc, -jnp.inf)
                p = jnp.exp2(sc - lse_ref[:, MLA_V * h:MLA_V * h + 1])
                doh = d_o * mk
                dp = _dg(doh * LN2, v_ref[...], _NT)
                delta = jnp.sum(prod * mk, axis=1, keepdims=True) * LN2
                ds = p * (dp - delta)
                dv_sc[...] += _dg(p, doh, _TN)
                dk_sc[:, hs] += _dg(ds, qh, _TN)
                dq_ref[rows, hs] += _dg(ds, kh, _NN)

        @pl.when(qi > ki)
        def _():
            update(False)

        @pl.when(qi == ki)
        def _():
            update(True)

        @pl.when(qi == n - 1)
        def _():
            dk_ref[...] = dk_sc[...]
            dv_ref[...] = dv_sc[...].astype(dv_ref.dtype)

        @pl.when((pl.program_id(0) == MLA_HEADS // 2 - 1) & (qi == n - 1) & (ki == n - 1))
        def _():
            comm.finish(place, cins, couts, csems)

    q_idx = lambda p, ki, qi: (jnp.maximum(qi, ki), p)
    res = pl.pallas_call(
        kern, grid=(MLA_HEADS // 2, n, n),
        in_specs=[pl.BlockSpec((t, 256), q_idx), pl.BlockSpec((t, 256), lambda p, ki, qi: (ki, p)),
                  pl.BlockSpec((t, 128), lambda p, ki, qi: (ki, p)), pl.BlockSpec((t, 128), q_idx),
                  pl.BlockSpec((t, 128), q_idx),
                  pl.BlockSpec((t, 128), lambda p, ki, qi: (jnp.maximum(qi, ki), 4 + p))] + [ANY] * nci,
        out_specs=[pl.BlockSpec((s, 256), lambda p, ki, qi: (0, p)), pl.BlockSpec((t, 256), lambda p, ki, qi: (ki, p)),
                   pl.BlockSpec((t, 128), lambda p, ki, qi: (ki, p))] + [ANY] * nco,
        out_shape=[jax.ShapeDtypeStruct((s, 1024), F32), jax.ShapeDtypeStruct((s, 1024), F32),
                   jax.ShapeDtypeStruct((s, 512), BF16)] + comm.out_shape,
        scratch_shapes=[pltpu.VMEM((t, 256), F32), pltpu.VMEM((t, 128), F32)] + comm.sems,
        compiler_params=pltpu.CompilerParams(dimension_semantics=("arbitrary", "arbitrary", "arbitrary"),
                                             vmem_limit_bytes=VMEM_LIMIT),
        name="mla_attn_bwd")(q, k, v, o, lse, dcat, *comm.ins)
    return res[0], res[1], res[2], res[3:]


def _gate_fn(alr, w2, b):
    return _log_sigmoid(_dot_nn(alr, w2) + b) * (1.0 / GLA_GATE_NORM)


def _qk_head(qh, kh, kpe, c, sa, sb, qn, kn):
    kfull = kh + kpe * _lane_mask(MLA_NOPE, MLA_QK)
    q_r = _rope(_rms(qh, qn, MLA_QK), c, sa, sb) * (MLA_QK ** -0.5 * LOG2E)
    k_r = _rope(_rms(kfull, kn, MLA_QK), c, sa, sb)
    return q_r, k_r


def _mix_head(o, og, gn):
    return _rms(o, gn) * _silu(og)


def _xa_head(xq, xk, xv, qn, kn):
    sc = _dot_nt(_rms(xq, qn), _rms(xk, kn)) * (XA_DIM ** -0.5)
    e = jnp.exp(sc - lax.stop_gradient(jnp.max(sc, axis=1, keepdims=True)))
    p = e / jnp.sum(e, axis=1, keepdims=True)
    return _dot_nn(p, xv)


def _heads(x, n):
    return [x[:, 128 * h:128 * (h + 1)] for h in range(n)]


def _cat(xs):
    return jnp.concatenate(xs, axis=1)


def _norm_fwd(x, w, name):
    return _rows_call(lambda r, c: ([_rms(r[0], c[0])], []), [_row(x)], [w], [(x.shape[1], BF16)], name=name)[0]


def _norm_bwd(x, w, d_out, add, name):
    def body(r, c):
        _, vjp = jax.vjp(_rms, r[0], c[0])
        dx, dw = vjp(r[1])
        return [dx + r[2]], [dw]

    return _rows_call(body, [_row(x), _row(d_out), _row(add)], [w], [(x.shape[1], F32)], [w.shape], name=name)


CONV_HALO = 8


def _conv_specs(s, f, t):
    n8 = t // CONV_HALO
    cur = pl.BlockSpec((None, t, f), lambda j, i: (j, i, 0))
    prev = pl.BlockSpec((None, CONV_HALO, f), lambda j, i: (j, jnp.maximum(i * n8 - 1, 0), 0))
    nxt = pl.BlockSpec((None, CONV_HALO, f), lambda j, i: (j, jnp.minimum((i + 1) * n8, s // CONV_HALO - 1), 0))
    cw = pl.BlockSpec((None, 3, f), lambda j, i: (j, 0, 0))
    cb = pl.BlockSpec((None, 1, f), lambda j, i: (j, 0, 0))
    return cur, prev, nxt, cw, cb


def _conv_taps(g, prev, first):
    ext = jnp.concatenate([jnp.where(first, 0.0, prev), g], axis=0)
    return pltpu.roll(ext, 1, 0)[CONV_HALO:], pltpu.roll(ext, 2, 0)[CONV_HALO:]


def _conv_fwd(gg, uu, cw, cb):
    _, s, f = gg.shape
    t = _tile(s, 512)

    def kern(g_ref, gp_ref, u_ref, cw_ref, cb_ref, o_ref):
        g = g_ref[...]
        g1, g2 = _conv_taps(g, gp_ref[...], pl.program_id(1) == 0)
        w = cw_ref[...]
        gc = cb_ref[...] + w[0:1] * g2 + w[1:2] * g1 + w[2:3] * g
        o_ref[...] = (_silu(gc) * u_ref[...]).astype(o_ref.dtype)

    cur, prev, _, cws, cbs = _conv_specs(s, f, t)
    return pl.pallas_call(
        kern, grid=(4, s // t), in_specs=[cur, prev, cur, cws, cbs], out_specs=cur,
        out_shape=jax.ShapeDtypeStruct(gg.shape, BF16),
        compiler_params=pltpu.CompilerParams(dimension_semantics=("parallel", "parallel"), vmem_limit_bytes=VMEM_LIMIT),
        name="ffn_conv_fwd")(gg, gg, uu, cw, cb)


def _conv_bwd_gate(gg, uu, dact, cw, cb):
    _, s, f = gg.shape
    t = _tile(s, 512)

    def kern(g_ref, gp_ref, u_ref, da_ref, cw_ref, cb_ref, du_ref, dgc_ref, dcw_ref, dcb_ref):
        i = pl.program_id(1)
        g, u, da = g_ref[...], u_ref[...], da_ref[...]
        g1, g2 = _conv_taps(g, gp_ref[...], i == 0)
        w = cw_ref[...]
        gc = cb_ref[...] + w[0:1] * g2 + w[1:2] * g1 + w[2:3] * g
        sg = jax.nn.sigmoid(gc)
        du_ref[...] = (da * (gc * sg)).astype(du_ref.dtype)
        dgc = da * u * (sg * (1.0 + gc * (1.0 - sg)))
        dgc_ref[...] = dgc

        @pl.when(i == 0)
        def _():
            dcw_ref[...] = jnp.zeros_like(dcw_ref)
            dcb_ref[...] = jnp.zeros_like(dcb_ref)

        dcw_ref[0:1, :] += jnp.sum(dgc * g2, axis=0, keepdims=True)
        dcw_ref[1:2, :] += jnp.sum(dgc * g1, axis=0, keepdims=True)
        dcw_ref[2:3, :] += jnp.sum(dgc * g, axis=0, keepdims=True)
        dcb_ref[...] += jnp.sum(dgc, axis=0, keepdims=True)

    cur, prev, _, cws, cbs = _conv_specs(s, f, t)
    return pl.pallas_call(
        kern, grid=(4, s // t), in_specs=[cur, prev, cur, cur, cws, cbs], out_specs=[cur, cur, cws, cbs],
        out_shape=[jax.ShapeDtypeStruct(gg.shape, BF16), jax.ShapeDtypeStruct(gg.shape, F32),
                   jax.ShapeDtypeStruct(cw.shape, F32), jax.ShapeDtypeStruct(cb.shape, F32)],
        compiler_params=pltpu.CompilerParams(dimension_semantics=("parallel", "arbitrary"), vmem_limit_bytes=VMEM_LIMIT),
        name="ffn_conv_bwd_gate")(gg, gg, uu, dact, cw, cb)


def _conv_bwd_taps(dgc, cw):
    _, s, f = dgc.shape
    t = _tile(s, 512)
    nt = s // t

    def kern(d_ref, dn_ref, cw_ref, o_ref):
        d = d_ref[...]
        ext = jnp.concatenate([d, jnp.where(pl.program_id(1) == nt - 1, 0.0, dn_ref[...])], axis=0)
        up1 = pltpu.roll(ext, t + CONV_HALO - 1, 0)[:t]
        up2 = pltpu.roll(ext, t + CONV_HALO - 2, 0)[:t]
        w = cw_ref[...]
        o_ref[...] = (w[2:3] * d + w[1:2] * up1 + w[0:1] * up2).astype(o_ref.dtype)

    cur, _, nxt, cws, _ = _conv_specs(s, f, t)
    return pl.pallas_call(
        kern, grid=(4, nt), in_specs=[cur, nxt, cws], out_specs=cur, out_shape=jax.ShapeDtypeStruct(dgc.shape, BF16),
        compiler_params=pltpu.CompilerParams(dimension_semantics=("parallel", "parallel"), vmem_limit_bytes=VMEM_LIMIT),
        name="ffn_conv_bwd_taps")(dgc, dgc, cw)


def _rope_tables(pos):
    half = MLA_ROPE // 2
    inv = ROPE_THETA ** (-jnp.arange(half, dtype=F32) / half)
    ang = pos.astype(F32)[:, None] * inv
    cos, sin = jnp.cos(ang), jnp.sin(ang)
    s = pos.shape[0]
    z = lambda w: jnp.zeros((s, w), F32)
    c = jnp.concatenate([jnp.ones((s, MLA_NOPE), F32), cos, cos, jnp.ones((s, LANES - MLA_QK), F32)], axis=1)
    sa = jnp.concatenate([z(MLA_NOPE), -sin, z(half), z(LANES - MLA_QK)], axis=1)
    sb = jnp.concatenate([z(MLA_NOPE), z(half), sin, z(LANES - MLA_QK)], axis=1)
    return c, sa, sb


def _local_step(x, mem, pos, target, w, late_shards):
    g = {}
    w = dict(w)
    c, sa, sb = _rope_tables(pos)

    xn = _norm_fwd(x, w["norm_mix"], "norm_mix_fwd")
    proj = _matmul(xn, w["in"], "nn", F32, "proj_fwd")
    alr = _row(proj, 128, P_ALR // 128)
    kpe = _row(proj, 128, P_KPE // 128)
    og = _row(proj, 512, P_OG // 512)
    cq = _row(proj, 256, P_CQ // 256)
    ckv = _row(proj, 128, P_CKV // 128)

    la = _rows_call(lambda r, k: ([_gate_fn(r[0], k[0], k[1])], []), [alr], [w["w2"], w["gate_b"]],
                    [(256, F32)], name="gla_gate_fwd")[0]
    o_gla, states = _gla_fwd(proj, la)

    q_lat, kv_lat = _rows_call(lambda r, k: ([_rms(r[0], k[0]), _rms(r[1], k[1])], []), [cq, ckv],
                               [w["q_a_norm"], w["kv_a_norm"]], [(256, BF16), (128, BF16)], name="mla_lat_fwd")
    q_up = _matmul(q_lat, w["uq"], "nn", F32, "mla_q_fwd")
    k_up = _matmul(kv_lat, w["k"], "nn", F32, "mla_k_fwd")
    v_mla = _matmul(kv_lat, w["v"], "nn", BF16, "mla_v_fwd")

    def qk_body(r, k):
        qs, ks = [], []
        for qh, kh in zip(_heads(r[0], MLA_HEADS), _heads(r[1], MLA_HEADS)):
            a, b = _qk_head(qh, kh, r[2], r[3], r[4], r[5], k[0], k[1])
            qs.append(a)
            ks.append(b)
        return [_cat(qs), _cat(ks)], []

    tabs = [_row(c), _row(sa), _row(sb)]
    q_r, k_r = _rows_call(qk_body, [_row(q_up), _row(k_up), kpe] + tabs, [w["q_norm"], w["k_norm"]],
                          [(1024, BF16), (1024, BF16)], name="mla_qk_fwd")
    o_mla, lse, gathered = _attn_fwd(q_r, k_r, v_mla, _gather_plan(late_shards))
    w.update(_late_layout(dict(zip(LATE, gathered, strict=True))))

    def mix_body(r, k):
        ys = [_mix_head(o, g_, k[0]) for o, g_ in zip(_heads(r[0], GLA_HEADS), _heads(r[1], GLA_HEADS))]
        return [_cat(ys + [r[2]])], []

    cat = _rows_call(mix_body, [_row(o_gla), og, _row(o_mla)], [w["gla_out_norm"]], [(1024, BF16)],
                     name="mix_fwd")[0]
    h1 = _matmul(cat, w["out"], "nn", F32, "out_fwd", residual=x)

    hn = _norm_fwd(h1, w["norm_xa"], "norm_xa_fwd")
    mn = _norm_fwd(mem, w["norm_mem"], "norm_mem_fwd")
    xq = _matmul(hn, w["xq"], "nn", F32, "xa_q_fwd")
    xkv = _matmul(mn, w["xkv"], "nn", F32, "xa_kv_fwd")

    def xa_body(r, k):
        ks, vs = _heads(k[0], 2 * XA_HEADS)[:XA_HEADS], _heads(k[0], 2 * XA_HEADS)[XA_HEADS:]
        return [_cat([_xa_head(a, b, v_, k[1], k[2]) for a, b, v_ in zip(_heads(r[0], XA_HEADS), ks, vs)])], []

    xo = _rows_call(xa_body, [_row(xq)], [xkv, w["xa_q_norm"], w["xa_k_norm"]], [(512, BF16)], name="xa_fwd")[0]
    h2 = _matmul(xo, w["xo"], "nn", F32, "xa_o_fwd", residual=h1)

    fn = _norm_fwd(h2, w["norm_ffn"], "norm_ffn_fwd")
    gg = _matmul(fn, w["wg"], "nn", F32, "ffn_gate_fwd", b_lead="p")
    uu = _matmul(fn, w["wu"], "nn", F32, "ffn_up_fwd", b_lead="p")
    act = _conv_fwd(gg, uu, w["cw"], w["cb"])
    y = _matmul(act, w["wd"], "nn", F32, "ffn_down_fwd", residual=h2, a_lead="k", b_lead="k")

    def loss_body(r, k):
        err = r[0] - r[1]
        part = 0.5 * jnp.sum(jnp.sum(err * err, axis=1, keepdims=True) * (1.0 / D_MODEL), axis=0, keepdims=True)
        return [err * (1.0 / D_MODEL)], [jnp.broadcast_to(part, (1, LANES))]

    dy, loss = _rows_call(loss_body, [_row(y), _row(target)], [], [(D_MODEL, F32)], [(1, LANES)], name="loss")

    g["ffn_w_down"] = _matmul(act, dy, "tn", BF16, "ffn_down_dw", a_lead="p")
    dact = _matmul(dy, w["wd"], "nt", F32, "ffn_down_dx", b_lead="p")
    duu, dgc, g["ffn_conv_w"], g["ffn_conv_b"] = _conv_bwd_gate(gg, uu, dact, w["cw"], w["cb"])
    dgg = _conv_bwd_taps(dgc, w["cw"])
    g["ffn_w_gate"] = _matmul(fn, dgg, "tn", BF16, "ffn_gate_dw", b_lead="p")
    g["ffn_w_up"] = _matmul(fn, duu, "tn", BF16, "ffn_up_dw", b_lead="p")
    dfn = _matmul(dgg, w["wg"], "nt", F32, "ffn_gate_dx", a_lead="k", b_lead="k")
    dfn = _matmul(duu, w["wu"], "nt", F32, "ffn_up_dx", residual=dfn, a_lead="k", b_lead="k")
    dh2, g["norm_ffn"] = _norm_bwd(h2, w["norm_ffn"], dfn, dy, "norm_ffn_bwd")

    g["xa_w_o"] = _matmul(xo, dh2, "tn", BF16, "xa_o_dw")
    dxo = _matmul(dh2, w["xo"], "nt", F32, "xa_o_dx")

    def xa_bwd(r, k):
        kvh = _heads(k[0], 2 * XA_HEADS)
        dq_, dk_, dv_ = [], [], []
        dqn, dkn = 0.0, 0.0
        for h, (a, d_) in enumerate(zip(_heads(r[0], XA_HEADS), _heads(r[1], XA_HEADS))):
            _, vjp = jax.vjp(_xa_head, a, kvh[h], kvh[XA_HEADS + h], k[1], k[2])
            ga, gk, gv, gqn, gkn = vjp(d_)
            dq_.append(ga)
            dk_.append(gk)
            dv_.append(gv)
            dqn, dkn = dqn + gqn, dkn + gkn
        return [_cat(dq_)], [_cat(dk_ + dv_), dqn, dkn]

    dxq, dxkv, g["xa_q_norm"], g["xa_k_norm"] = _rows_call(
        xa_bwd, [_row(xq), _row(dxo)], [xkv, w["xa_q_norm"], w["xa_k_norm"]], [(512, BF16)],
        [xkv.shape, (1, 128), (1, 128)], name="xa_bwd")
    g["xa_w_q"] = _matmul(hn, dxq, "tn", BF16, "xa_q_dw")
    dhn = _matmul(dxq, w["xq"], "nt", F32, "xa_q_dx")
    g["xa_w_kv"] = _matmul(mn, dxkv, "tn", BF16, "xa_kv_dw")
    dmn = _matmul(dxkv, w["xkv"], "nt", F32, "xa_kv_dx")
    _, g["norm_mem"] = _norm_bwd(mem, w["norm_mem"], dmn, dmn, "norm_mem_bwd")
    dh1, g["norm_xa"] = _norm_bwd(h1, w["norm_xa"], dhn, dh2, "norm_xa_bwd")

    g["w_out"] = _matmul(cat, dh1, "tn", BF16, "out_dw")
    dcat = _matmul(dh1, w["out"], "nt", F32, "out_dx")

    def mix_bwd(r, k):
        do_, dog_ = [], []
        dgn = 0.0
        for o, g_, d_ in zip(_heads(r[0], GLA_HEADS), _heads(r[1], GLA_HEADS), _heads(r[2], GLA_HEADS)):
            _, vjp = jax.vjp(_mix_head, o, g_, k[0])
            a, b, gn_ = vjp(d_)
            do_.append(a)
            dog_.append(b)
            dgn = dgn + gn_
        return [_cat(do_), _cat(dog_)], [dgn]

    do_gla, d_og, g["gla_out_norm"] = _rows_call(mix_bwd, [_row(o_gla), og, _row(dcat, 512, 0)], [w["gla_out_norm"]],
                                                 [(512, F32), (512, BF16)], [(1, 128)], name="mix_bwd")

    late_parts = _late_grad_shards(g)
    dq_r, dk_r, dv_mla, lands_late = _attn_bwd(q_r, k_r, v_mla, o_mla, lse, dcat,
                                               _scatter_plan([late_parts[n] for n in LATE]))

    def qk_bwd(r, k):
        dqs, dks = [], []
        dkpe, dqn, dkn = 0.0, 0.0, 0.0
        for qh, kh, dqh, dkh in zip(_heads(r[0], MLA_HEADS), _heads(r[1], MLA_HEADS), _heads(r[6], MLA_HEADS),
                                    _heads(r[7], MLA_HEADS)):
            _, vjp = jax.vjp(lambda a, b, e, f, h_: _qk_head(a, b, e, r[3], r[4], r[5], f, h_), qh, kh, r[2], k[0], k[1])
            ga, gb, ge, gf, gh = vjp((dqh, dkh))
            dqs.append(ga)
            dks.append(gb)
            dkpe, dqn, dkn = dkpe + ge, dqn + gf, dkn + gh
        return [_cat(dqs), _cat(dks), dkpe], [dqn, dkn]

    dq_up, dk_up, d_kpe, g["q_norm"], g["k_norm"] = _rows_call(
        qk_bwd, [_row(q_up), _row(k_up), kpe] + tabs + [_row(dq_r), _row(dk_r)], [w["q_norm"], w["k_norm"]],
        [(1024, BF16), (1024, BF16), (128, BF16)], [(1, 128), (1, 128)], name="mla_qk_bwd")
    g["uq"] = _matmul(q_lat, dq_up, "tn", BF16, "mla_q_dw")
    dq_lat = _matmul(dq_up, w["uq"], "nt", F32, "mla_q_dx")
    g["k"] = _matmul(kv_lat, dk_up, "tn", BF16, "mla_k_dw")
    g["v"] = _matmul(kv_lat, dv_mla, "tn", BF16, "mla_v_dw")
    dkv_lat = _matmul(dk_up, w["k"], "nt", F32, "mla_k_dx")
    dkv_lat = _matmul(dv_mla, w["v"], "nt", F32, "mla_v_dx", residual=dkv_lat)

    def lat_bwd(r, k):
        _, vjp1 = jax.vjp(_rms, r[0], k[0])
        _, vjp2 = jax.vjp(_rms, r[1], k[1])
        a, ga = vjp1(r[2])
        b, gb = vjp2(r[3])
        return [a, b], [ga, gb]

    d_cq, d_ckv, g["mla_q_a_norm"], g["mla_kv_a_norm"] = _rows_call(
        lat_bwd, [cq, ckv, _row(dq_lat), _row(dkv_lat)], [w["q_a_norm"], w["kv_a_norm"]],
        [(256, BF16), (128, BF16)], [(1, 256), (1, 128)], name="mla_lat_bwd")

    dgq, dgk, dla, dgv = _gla_bwd(proj, la, states, do_gla)

    def gate_bwd(r, k):
        _, vjp = jax.vjp(_gate_fn, r[0], k[0], k[1])
        a, gw, gb = vjp(r[1])
        return [a], [gw, gb]

    d_alr, g["w2"], g["gla_gate_b"] = _rows_call(gate_bwd, [alr, _row(dla)], [w["w2"], w["gate_b"]], [(128, BF16)],
                                                 [(128, 256), (1, 256)], name="gla_gate_bwd")

    dproj = jnp.concatenate([dgq.astype(BF16), dgk.astype(BF16), dgv.astype(BF16), d_og, d_cq, d_ckv, d_kpe, d_alr],
                            axis=1)
    g["in"] = _matmul(xn, dproj, "tn", BF16, "proj_dw")
    dxn = _matmul(dproj, w["in"], "nt", F32, "proj_dx")
    dx, g["norm_mix"] = _norm_bwd(x, w["norm_mix"], dxn, dh1, "norm_mix_bwd")
    return loss[0, 0], dx, g, lands_late


def _join_shards(pieces, axis):
    if axis == 0:
        return pieces.reshape(-1, pieces.shape[2])
    return jnp.transpose(pieces, (1, 0, 2)).reshape(pieces.shape[1], -1)


def _split_shards(full, axis):
    r, c = full.shape
    if axis == 0:
        return full.reshape(4, r // 4, c)
    return jnp.transpose(full.reshape(r, 4, c // 4), (1, 0, 2))


def _early_layout(gath, rep):
    w_in = _join_shards(gath["w_in"], 1)
    z = lambda n: jnp.zeros((D_MODEL, n), w_in.dtype)
    seg = lambda lo, n: w_in[:, lo:lo + n]
    ukv = _join_shards(gath["mla_w_ukv"], 1).reshape(MLA_KV_RANK, MLA_HEADS, MLA_NOPE + MLA_V)
    w = {
        "in": jnp.concatenate([seg(N_GQ, 256), seg(N_GK, 256), seg(N_GV, 512), seg(N_OG, 512), seg(N_CQ, 256),
                               seg(N_CKV, 128), z(64), seg(N_KPE, 32), z(32), seg(N_ALR, 16), z(112)], axis=1),
        "uq": jnp.pad(_join_shards(gath["mla_w_uq"], 1).reshape(MLA_Q_RANK, MLA_HEADS, MLA_QK),
                      ((0, 0), (0, 0), (0, LANES - MLA_QK))).reshape(MLA_Q_RANK, MLA_HEADS * LANES),
        "k": jnp.pad(ukv[:, :, :MLA_NOPE], ((0, 0), (0, 0), (0, LANES - MLA_NOPE))).reshape(MLA_KV_RANK, -1),
        "v": ukv[:, :, MLA_NOPE:].reshape(MLA_KV_RANK, MLA_HEADS * MLA_V),
        "w2": jnp.pad(_join_shards(gath["gla_gate_w2"], 1), ((0, LANES - GLA_RANK), (0, 0))),
        "cb": rep["ffn_conv_b"].reshape(4, 1, D_FF // 4),
        "q_norm": jnp.pad(rep["mla_q_norm"], ((0, 0), (0, LANES - MLA_QK))),
        "k_norm": jnp.pad(rep["mla_k_norm"], ((0, 0), (0, LANES - MLA_QK))),
        "q_a_norm": rep["mla_q_a_norm"], "kv_a_norm": rep["mla_kv_a_norm"], "gate_b": rep["gla_gate_b"],
    }
    for n in ("norm_mix", "gla_out_norm", "norm_xa", "norm_mem", "xa_q_norm", "xa_k_norm", "norm_ffn"):
        w[n] = rep[n]
    return w


def _late_layout(gath):
    return {"out": _join_shards(gath["w_out"], 0), "xq": _join_shards(gath["xa_w_q"], 0),
            "xkv": _join_shards(gath["xa_w_kv"], 0), "xo": _join_shards(gath["xa_w_o"], 1),
            "wg": gath["ffn_w_gate"], "wu": gath["ffn_w_up"], "wd": gath["ffn_w_down"], "cw": gath["ffn_conv_w"]}


def _late_grad_shards(g):
    sh = {"w_out": _split_shards(g["w_out"], 0), "xa_w_q": _split_shards(g["xa_w_q"], 0),
          "xa_w_kv": _split_shards(g["xa_w_kv"], 0), "xa_w_o": _split_shards(g["xa_w_o"], 1),
          "ffn_w_gate": g["ffn_w_gate"], "ffn_w_up": g["ffn_w_up"], "ffn_conv_w": g["ffn_conv_w"],
          "ffn_w_down": g["ffn_w_down"]}
    return {n: v.astype(BF16) for n, v in sh.items()}


def _early_grad_shards(g):
    gi = g["in"]
    seg = lambda lo, n: gi[:, lo:lo + n]
    w_in = jnp.concatenate([seg(P_GQ, 256), seg(P_GK, 256), seg(P_GV, 512), seg(P_ALR, 16), seg(P_OG, 512),
                            seg(P_CQ, 256), seg(P_CKV, 128), seg(P_KPE + 64, 32)], axis=1)
    uq = g["uq"].reshape(MLA_Q_RANK, MLA_HEADS, LANES)[:, :, :MLA_QK].reshape(MLA_Q_RANK, -1)
    ukv = jnp.concatenate([g["k"].reshape(MLA_KV_RANK, MLA_HEADS, LANES)[:, :, :MLA_NOPE],
                           g["v"].reshape(MLA_KV_RANK, MLA_HEADS, MLA_V)], axis=2).reshape(MLA_KV_RANK, -1)
    sh = {"w_in": _split_shards(w_in, 1), "gla_gate_w2": _split_shards(g["w2"][:GLA_RANK], 1),
          "mla_w_uq": _split_shards(uq, 1), "mla_w_ukv": _split_shards(ukv, 1)}
    sh = {n: v.astype(BF16) for n, v in sh.items()}
    rep = {n: g[n] for n in REPLICATED if n in g}
    rep["mla_q_norm"] = g["q_norm"][:, :MLA_QK]
    rep["mla_k_norm"] = g["k_norm"][:, :MLA_QK]
    rep["ffn_conv_b"] = g["ffn_conv_b"].reshape(1, D_FF)
    return sh, rep


SMALL_SHAPE = (8, 1024)


def _pack_small(vectors):
    flat = jnp.concatenate(vectors, axis=1)
    return jnp.pad(flat, ((0, 0), (0, SMALL_SHAPE[0] * SMALL_SHAPE[1] - flat.shape[1]))).reshape(SMALL_SHAPE)


def _unpack_small(buf, widths):
    flat = buf.reshape(1, -1)
    out, off = [], 0
    for wd in widths:
        out.append(flat[:, off:off + wd])
        off += wd
    return out


ANY = pl.BlockSpec(memory_space=pl.ANY)


def _place():
    x, y, c = lax.axis_index("x"), lax.axis_index("y"), lax.axis_index("c")
    chips = [(1 - x, y), (x, 1 - y), (1 - x, 1 - y)]
    return x, y, c, chips


class _Comm:
    def __init__(self, ins, out_shape, sems, start, finish, mid=None):
        self.ins, self.out_shape, self.sems = list(ins), list(out_shape), list(sems)
        self.start, self.finish, self.mid = start, finish, mid or (lambda *args: None)


def _run_comm(plan, name):
    ni, no = len(plan.ins), len(plan.out_shape)

    def body(*refs):
        ins, outs, sems = refs[:ni], refs[ni:ni + no], refs[ni + no:]
        place = _place()
        plan.start(place, ins, outs, sems)
        plan.mid(place, ins, outs, sems)
        plan.finish(place, ins, outs, sems)

    return pl.pallas_call(body, in_specs=[ANY] * ni, out_specs=[ANY] * no, out_shape=plan.out_shape,
                          scratch_shapes=plan.sems, name=name)(*plan.ins)


def _gather_plan(shards):
    n = len(shards)

    def copies(place, ins, outs, sems, landing):
        x, y, c, chips = place
        send, recv, local = sems
        me = 2 * x + y
        own = [pltpu.make_async_copy(ins[t], outs[t].at[me], local.at[t]) for t in range(n)]
        remote = []
        for t in range(n):
            for j, (px, py) in enumerate(chips):
                remote.append(pltpu.make_async_remote_copy(
                    src_ref=ins[t], dst_ref=outs[t].at[2 * px + py if landing else me], send_sem=send.at[3 * t + j],
                    recv_sem=recv.at[3 * t + j], device_id=(px, py, c), device_id_type=MESH))
        return own, remote

    def start(place, ins, outs, sems):
        own, push = copies(place, ins, outs, sems, False)
        for cp in own + push:
            cp.start()

    def finish(place, ins, outs, sems):
        own, land = copies(place, ins, outs, sems, True)
        for cp in land:
            cp.wait_recv()
        for cp in land:
            cp.wait_send()
        for cp in own:
            cp.wait()

    dma = pltpu.SemaphoreType.DMA
    return _Comm(shards, [jax.ShapeDtypeStruct((4,) + s.shape, s.dtype) for s in shards],
                 [dma((3 * n,)), dma((3 * n,)), dma((n,))], start, finish)


def _scatter_plan(parts, small=None):
    n = len(parts)
    ns = 0 if small is None else 1

    def unpack(place, ins, outs, sems):
        x, y, c, chips = place
        return x, y, c, chips, 2 * x + y, 4 * x + 2 * y + c, (x, y, 1 - c)

    def remote(src, dst, ss, rs, to):
        return pltpu.make_async_remote_copy(src_ref=src, dst_ref=dst, send_sem=ss, recv_sem=rs, device_id=to,
                                            device_id_type=MESH)

    def first_wave(place, ins, outs, sems):
        x, y, c, chips, me, dev, sib = unpack(place, ins, outs, sems)
        ici_s, ici_r, d2d_s, d2d_r, sm_s, sm_r, local = sems
        own, push = [], []
        if ns:
            own.append(pltpu.make_async_copy(ins[n], outs[n].at[dev], local.at[n]))
            for k in range(1, 8):
                px = (1 - x) if (k >> 2) & 1 else x
                py = (1 - y) if (k >> 1) & 1 else y
                pc = (1 - c) if k & 1 else c
                push.append(remote(ins[n], outs[n].at[dev], sm_s.at[k - 1], sm_r.at[k - 1], (px, py, pc)))
        for t in range(n):
            own.append(pltpu.make_async_copy(ins[t].at[me], outs[t].at[dev], local.at[t]))
            push.append(remote(ins[t].at[me], outs[t].at[dev], d2d_s.at[4 * t], d2d_r.at[4 * t], sib))
            for j, (px, py) in enumerate(chips):
                push.append(remote(ins[t].at[2 * px + py], outs[t].at[dev], ici_s.at[3 * t + j], ici_r.at[3 * t + j],
                                   (px, py, c)))
        return own, push

    def start(place, ins, outs, sems):
        own, push = first_wave(place, ins, outs, sems)
        for cp in own + push:
            cp.start()

    def landed(dst, rs, sems, sib):
        remote(dst, dst, sems[-1].at[0], rs, sib).wait_recv()

    def forwards(place, ins, outs, sems):
        x, y, c, chips, me, dev, sib = unpack(place, ins, outs, sems)
        d2d_s, d2d_r = sems[2], sems[3]
        slots = [(t, j, outs[t].at[4 * px + 2 * py + c]) for t in range(n) for j, (px, py) in enumerate(chips)]
        return [(t, j, slot, remote(slot, slot, d2d_s.at[4 * t + 1 + j], d2d_r.at[4 * t + 1 + j], sib))
                for t, j, slot in slots]

    def mid(place, ins, outs, sems):
        sib = unpack(place, ins, outs, sems)[-1]
        for t, j, slot, cp in forwards(place, ins, outs, sems):
            landed(slot, sems[1].at[3 * t + j], sems, sib)
            cp.start()

    def finish(place, ins, outs, sems):
        x, y, c, chips, me, dev, sib = unpack(place, ins, outs, sems)
        d2d_r, sm_r = sems[3], sems[5]
        own, push = first_wave(place, ins, outs, sems)
        push += [cp for _, _, _, cp in forwards(place, ins, outs, sems)]
        for t in range(n):
            landed(outs[t].at[4 * x + 2 * y + (1 - c)], d2d_r.at[4 * t], sems, sib)
            for j, (px, py) in enumerate(chips):
                landed(outs[t].at[4 * px + 2 * py + (1 - c)], d2d_r.at[4 * t + 1 + j], sems, sib)
        if ns:
            for k in range(1, 8):
                px = (1 - x) if (k >> 2) & 1 else x
                py = (1 - y) if (k >> 1) & 1 else y
                pc = (1 - c) if k & 1 else c
                landed(outs[n].at[4 * px + 2 * py + pc], sm_r.at[k - 1], sems, sib)
        for cp in push:
            cp.wait_send()
        for cp in own:
            cp.wait()

    dma = pltpu.SemaphoreType.DMA
    ins = list(parts) + ([small] if ns else [])
    out_shape = [jax.ShapeDtypeStruct((8,) + p.shape[1:], p.dtype) for p in parts]
    if ns:
        out_shape.append(jax.ShapeDtypeStruct((8,) + small.shape, small.dtype))
    return _Comm(ins, out_shape, [dma((3 * n,)), dma((3 * n,)), dma((4 * n,)), dma((4 * n,)), dma((7,)), dma((7,)),
                                  dma((n + 1,))], start, finish, mid)


def _row_tile(r, cap=256):
    if r <= cap:
        return r
    return max(t for t in range(8, cap + 1, 8) if r % t == 0)


def _adamw(w, m, v, land, name):
    r, c = w.shape
    t = _row_tile(r)

    def kern(w_ref, m_ref, v_ref, l_ref, g_out, d_out, m_out, v_out):
        g = l_ref[0].astype(F32)
        for i in range(1, 8):
            g = g + l_ref[i].astype(F32)
        m_new = ADAM_B1 * m_ref[...] + (1.0 - ADAM_B1) * g
        v_new = ADAM_B2 * v_ref[...] + (1.0 - ADAM_B2) * (g * g)
        m_hat = m_new / (1.0 - ADAM_B1 ** ADAM_STEP)
        v_hat = v_new / (1.0 - ADAM_B2 ** ADAM_STEP)
        g_out[...] = g
        d_out[...] = -ADAM_LR * (m_hat / (jnp.sqrt(v_hat) + ADAM_EPS) + ADAM_WD * w_ref[...])
        m_out[...] = m_new
        v_out[...] = v_new

    spec = pl.BlockSpec((t, c), lambda i: (i, 0))
    return pl.pallas_call(
        kern, grid=(r // t,), in_specs=[spec] * 3 + [pl.BlockSpec((8, t, c), lambda i: (0, i, 0))], out_specs=[spec] * 4,
        out_shape=[jax.ShapeDtypeStruct((r, c), F32)] * 4,
        compiler_params=pltpu.CompilerParams(dimension_semantics=("parallel",), vmem_limit_bytes=VMEM_LIMIT),
        name=name)(w, m, v, land)


def _step(a):
    sq = lambda n: a[n][0] if a[n].ndim == 3 else a[n]
    payload = lambda n: sq(n) if n in EXACT_GATHER else sq(n).astype(BF16)

    gathered = _run_comm(_gather_plan([payload(n) for n in EARLY]), "gather_early")
    w = _early_layout(dict(zip(EARLY, gathered, strict=True)), {n: a[n] for n in REPLICATED})

    loss, dx, g, lands_late = _local_step(sq("x"), sq("mem"), a["positions"][0], sq("loss_target"), w,
                                          [payload(n) for n in LATE])

    sh, rep = _early_grad_shards(g)
    *lands_early, land_small = _run_comm(
        _scatter_plan([sh[n] for n in EARLY], _pack_small([rep[n] for n in REPLICATED])), "scatter_last")
    lands = dict(zip(EARLY + LATE, list(lands_early) + list(lands_late), strict=True))

    outs = {}
    kinds = ("grad_", "delta_", "new_m_", "new_v_")
    for n, _ in SHARDED:
        res = _adamw(sq(n), sq("m_" + n), sq("v_" + n), lands[n], "adamw_" + n)
        for kind, val in zip(kinds, res, strict=True):
            outs[kind + n] = val.reshape(a[n].shape)
    packed = [_pack_small([a[p + n] for n in REPLICATED]) for p in ("", "m_", "v_")]
    res = _adamw(*packed, land_small, "adamw_replicated")
    widths = [a[n].shape[1] for n in REPLICATED]
    for kind, buf in zip(kinds, res, strict=True):
        for n, val in zip(REPLICATED, _unpack_small(buf, widths), strict=True):
            outs[kind + n] = val

    loss = lax.psum(loss, ("x", "y", "c"))
    ordered = [outs[kind + n] for kind in kinds for n in WEIGHTS]
    return (loss, dx[None], *ordered)


def kernel(x, mem, positions, norm_mix, w_in, gla_gate_w2, gla_gate_b, gla_out_norm, mla_q_a_norm, mla_w_uq, mla_kv_a_norm, mla_w_ukv, mla_q_norm, mla_k_norm, w_out, norm_xa, norm_mem, xa_w_q, xa_w_kv, xa_q_norm, xa_k_norm, xa_w_o, norm_ffn, ffn_w_gate, ffn_w_up, ffn_conv_w, ffn_conv_b, ffn_w_down, loss_target, m_norm_mix, m_w_in, m_gla_gate_w2, m_gla_gate_b, m_gla_out_norm, m_mla_q_a_norm, m_mla_w_uq, m_mla_kv_a_norm, m_mla_w_ukv, m_mla_q_norm, m_mla_k_norm, m_w_out, m_norm_xa, m_norm_mem, m_xa_w_q, m_xa_w_kv, m_xa_q_norm, m_xa_k_norm, m_xa_w_o, m_norm_ffn, m_ffn_w_gate, m_ffn_w_up, m_ffn_conv_w, m_ffn_conv_b, m_ffn_w_down, v_norm_mix, v_w_in, v_gla_gate_w2, v_gla_gate_b, v_gla_out_norm, v_mla_q_a_norm, v_mla_w_uq, v_mla_kv_a_norm, v_mla_w_ukv, v_mla_q_norm, v_mla_k_norm, v_w_out, v_norm_xa, v_norm_mem, v_xa_w_q, v_xa_w_kv, v_xa_q_norm, v_xa_k_norm, v_xa_w_o, v_norm_ffn, v_ffn_w_gate, v_ffn_w_up, v_ffn_conv_w, v_ffn_conv_b, v_ffn_w_down):
    return _step(dict(locals()))
```

```python
import functools

import jax
import jax.numpy as jnp
from jax import lax
from jax.experimental import pallas as pl
from jax.experimental.pallas import tpu as pltpu

F32, BF16 = jnp.float32, jnp.bfloat16
MESH = pl.DeviceIdType.MESH

D_MODEL = 1024
EPS = 1e-6
GLA_HEADS, GLA_DK, GLA_DV, GLA_RANK, GLA_CHUNK = 4, 64, 128, 16, 64
GLA_GATE_NORM = 16.0
MLA_HEADS, MLA_Q_RANK, MLA_KV_RANK, MLA_NOPE, MLA_ROPE, MLA_V = 8, 256, 128, 64, 32, 64
MLA_QK = MLA_NOPE + MLA_ROPE
ROPE_THETA = 10000.0
LOG2E, LN2 = 1.4426950408889634, 0.6931471805599453
XA_HEADS, XA_DIM = 4, 128
D_FF = 2816
ADAM_LR, ADAM_B1, ADAM_B2, ADAM_EPS, ADAM_WD, ADAM_STEP = 0.001, 0.9, 0.999, 1e-08, 0.01, 10

LANES = 128
BF16_ROWS = 16
VMEM_LIMIT = 56 * 1024 * 1024
MATMUL_VMEM = 44 * 1024 * 1024

P_GQ, P_GK, P_GV, P_OG, P_CQ, P_CKV, P_KPE, P_ALR, P_WIDTH = 0, 256, 512, 1024, 1536, 1792, 1920, 2048, 2176
N_GQ, N_GK, N_GV, N_ALR, N_OG, N_CQ, N_CKV, N_KPE, N_WIDTH = 0, 256, 512, 1024, 1040, 1552, 1808, 1936, 1968

SHARDED = (("w_in", 1), ("gla_gate_w2", 1), ("mla_w_uq", 1), ("mla_w_ukv", 1), ("w_out", 0), ("xa_w_q", 0),
           ("xa_w_kv", 0), ("xa_w_o", 1), ("ffn_w_gate", 1), ("ffn_w_up", 1), ("ffn_conv_w", 1), ("ffn_w_down", 0))
REPLICATED = ("norm_mix", "gla_gate_b", "gla_out_norm", "mla_q_a_norm", "mla_kv_a_norm", "mla_q_norm", "mla_k_norm",
              "norm_xa", "norm_mem", "xa_q_norm", "xa_k_norm", "norm_ffn", "ffn_conv_b")
EXACT_GATHER = ("gla_gate_w2", "ffn_conv_w")
TRANSPOSED = ("ffn_w_gate", "ffn_w_up")
EARLY = ("w_in", "gla_gate_w2", "mla_w_uq", "mla_w_ukv")
LATE = tuple(n for n, _ in SHARDED if n not in EARLY)
WEIGHTS = ("norm_mix", "w_in", "gla_gate_w2", "gla_gate_b", "gla_out_norm", "mla_q_a_norm", "mla_w_uq",
           "mla_kv_a_norm", "mla_w_ukv", "mla_q_norm", "mla_k_norm", "w_out", "norm_xa", "norm_mem", "xa_w_q",
           "xa_w_kv", "xa_q_norm", "xa_k_norm", "xa_w_o", "norm_ffn", "ffn_w_gate", "ffn_w_up", "ffn_conv_w",
           "ffn_conv_b", "ffn_w_down")


_NN = ((1,), (0,))
_NT = ((1,), (1,))
_TN = ((0,), (0,))


def _dg(a, b, dims):
    return lax.dot_general(a.astype(BF16), b.astype(BF16), (dims, ((), ())), preferred_element_type=F32)


@jax.custom_vjp
def _dot_nn(a, b):
    return _dg(a, b, _NN)


_dot_nn.defvjp(lambda a, b: (_dg(a, b, _NN), (a, b)),
               lambda r, g: (_dg(g, r[1], _NT).astype(r[0].dtype), _dg(r[0], g, _TN).astype(r[1].dtype)))


@jax.custom_vjp
def _dot_nt(a, b):
    return _dg(a, b, _NT)


_dot_nt.defvjp(lambda a, b: (_dg(a, b, _NT), (a, b)),
               lambda r, g: (_dg(g, r[1], _NN).astype(r[0].dtype), _dg(g, r[0], _TN).astype(r[1].dtype)))


@jax.custom_vjp
def _dot_tn(a, b):
    return _dg(a, b, _TN)


_dot_tn.defvjp(lambda a, b: (_dg(a, b, _TN), (a, b)),
               lambda r, g: (_dg(r[1], g, _NT).astype(r[0].dtype), _dg(r[0], g, _NN).astype(r[1].dtype)))


def _rms(x, w, n=None):
    n = x.shape[-1] if n is None else n
    ms = jnp.sum(x * x, axis=-1, keepdims=True) * (1.0 / n)
    return x * lax.rsqrt(ms + EPS) * w


def _silu(x):
    return x * jax.nn.sigmoid(x)


def _log_sigmoid(x):
    return jnp.minimum(x, 0.0) - jnp.log(1.0 + jnp.exp(-jnp.abs(x)))


@jax.custom_vjp
def _rope(y, c, sa, sb):
    return y * c + pltpu.roll(y, LANES - 16, 1) * sa + pltpu.roll(y, 16, 1) * sb


def _rope_bwd(res, g):
    c, sa, sb = res
    gy = g * c + pltpu.roll(g * sa, 16, 1) + pltpu.roll(g * sb, LANES - 16, 1)
    return gy, jnp.zeros_like(c), jnp.zeros_like(sa), jnp.zeros_like(sb)


_rope.defvjp(lambda y, c, sa, sb: (_rope(y, c, sa, sb), (c, sa, sb)), _rope_bwd)


def _lane_mask(lo, hi):
    lane = lax.broadcasted_iota(jnp.int32, (1, LANES), 1)
    return ((lane >= lo) & (lane < hi)).astype(F32)


def _tile(n, t):
    t = min(n, t)
    assert n % t == 0, (n, t)
    return t


def _matmul(a, b, mode, out_dtype, name, residual=None, a_lead=None, b_lead=None):
    (a0, a1), (b0, b1) = a.shape[-2:], b.shape[-2:]
    if mode == "nn":
        m, k, k2, n = a0, a1, b0, b1
    elif mode == "nt":
        m, k, n, k2 = a0, a1, b0, b1
    else:
        k, m, k2, n = a0, a1, b0, b1
    assert k == k2, (a.shape, b.shape, mode)
    npar = 4 if "p" in (a_lead, b_lead) else 1
    nsum = 4 if "k" in (a_lead, b_lead) else 1
    a_item, b_item, o_item = a.dtype.itemsize, b.dtype.itemsize, jnp.dtype(out_dtype).itemsize

    def vmem_need(tm, tn, tk):
        need = 2 * (nsum if a_lead == "k" else 1) * tm * tk * a_item + 2 * (nsum if b_lead == "k" else 1) * tk * tn * b_item
        need += 2 * tm * tn * o_item + tm * tn * 4 * (2 if tk < k else 1)
        need += tm * tk * 2 * (a_item == 4 or mode == "tn") + tk * tn * 2 * (b_item == 4)
        return need + (2 * tm * tn * 4 if residual is not None else 0)

    halvings = (4096, 2048, 1024, 512, 256, 128, 64, 32, 16, 8)
    if mode == "tn":
        tm = m if m <= 1408 else m // 2
        tn = n if tm * n <= 1024 * 2304 else n // 2
        tk = next((r for r in halvings if k % r == 0 and vmem_need(tm, tn, r) <= MATMUL_VMEM), k)
    else:
        tn, tk = n, k
        tm = next((r for r in halvings if m % r == 0 and vmem_need(r, tn, tk) <= MATMUL_VMEM), m)
    assert m % tm == 0 and n % tn == 0 and k % tk == 0
    nk = k // tk
    dims = {"nn": _NN, "nt": _NT, "tn": _TN}[mode]

    def body(*refs):
        a_ref, b_ref = refs[0], refs[1]
        r_ref = refs[2] if residual is not None else None
        o_ref = refs[3 if residual is not None else 2]
        prod = None
        for sh in range(nsum):
            term = _dg(a_ref[sh] if a_lead == "k" else a_ref[...], b_ref[sh] if b_lead == "k" else b_ref[...], dims)
            prod = term if prod is None else prod + term
        if nk == 1:
            o_ref[...] = (prod if r_ref is None else prod + r_ref[...]).astype(o_ref.dtype)
            return
        acc = refs[-1]
        kk = pl.program_id(3)

        @pl.when(kk == 0)
        def _():
            acc[...] = prod

        @pl.when(kk > 0)
        def _():
            acc[...] += prod

        @pl.when(kk == nk - 1)
        def _():
            r = acc[...]
            if r_ref is not None:
                r = r + r_ref[...]
            o_ref[...] = r.astype(o_ref.dtype)

    def spec(lead, blk, idx):
        if lead is None:
            return pl.BlockSpec(blk, lambda p, i, j, kk: idx(i, j, kk))
        if lead == "p":
            return pl.BlockSpec((None,) + blk, lambda p, i, j, kk: (p,) + idx(i, j, kk))
        return pl.BlockSpec((nsum,) + blk, lambda p, i, j, kk: (0,) + idx(i, j, kk))

    if mode == "nn":
        in_specs = [spec(a_lead, (tm, tk), lambda i, j, kk: (i, kk)), spec(b_lead, (tk, tn), lambda i, j, kk: (kk, j))]
    elif mode == "nt":
        in_specs = [spec(a_lead, (tm, tk), lambda i, j, kk: (i, kk)), spec(b_lead, (tn, tk), lambda i, j, kk: (j, kk))]
    else:
        in_specs = [spec(a_lead, (tk, tm), lambda i, j, kk: (kk, i)), spec(b_lead, (tk, tn), lambda i, j, kk: (kk, j))]
    args = [a, b]
    if residual is not None:
        assert npar == 1
        in_specs.append(spec(None, (tm, tn), lambda i, j, kk: (i, j)))
        args.append(residual)
    return pl.pallas_call(
        body, grid=(npar, m // tm, n // tn, nk), in_specs=in_specs,
        out_specs=spec("p" if npar > 1 else None, (tm, tn), lambda i, j, kk: (i, j)),
        out_shape=jax.ShapeDtypeStruct(((4,) if npar > 1 else ()) + (m, n), out_dtype),
        scratch_shapes=[pltpu.VMEM((tm, tn), F32)] if nk > 1 else [],
        compiler_params=pltpu.CompilerParams(dimension_semantics=("parallel", "parallel", "parallel", "arbitrary"),
                                             vmem_limit_bytes=VMEM_LIMIT),
        name=name)(*args)


def _row(a, width=None, col_block=0):
    return (a, a.shape[1] if width is None else width, col_block)


def _rows_call(body, rows, consts, outs, accs=(), *, name, tile=512):
    s = rows[0][0].shape[0]
    t = _tile(s, tile)
    nr, nc, no = len(rows), len(consts), len(outs)

    def kern(*refs):
        r = [x[...] for x in refs[:nr]]
        c = [x[...] for x in refs[nr:nr + nc]]
        o_refs = refs[nr + nc:nr + nc + no]
        a_refs = refs[nr + nc + no:]
        ro, ao = body(r, c)
        for ref, val in zip(o_refs, ro, strict=True):
            ref[...] = val.astype(ref.dtype)
        if a_refs:
            @pl.when(pl.program_id(0) == 0)
            def _():
                for ref in a_refs:
                    ref[...] = jnp.zeros_like(ref)

            for ref, val in zip(a_refs, ao, strict=True):
                ref[...] += val

    in_specs = [pl.BlockSpec((t, w), functools.partial(lambda cb, i: (i, cb), cb)) for (_, w, cb) in rows]
    in_specs += [pl.BlockSpec(c.shape, lambda i: (0, 0)) for c in consts]
    out_specs = [pl.BlockSpec((t, w), lambda i: (i, 0)) for (w, _) in outs]
    out_specs += [pl.BlockSpec(shape, lambda i: (0, 0)) for shape in accs]
    out_shape = [jax.ShapeDtypeStruct((s, w), dt) for (w, dt) in outs]
    out_shape += [jax.ShapeDtypeStruct(shape, F32) for shape in accs]
    return pl.pallas_call(
        kern, grid=(s // t,), in_specs=in_specs, out_specs=out_specs, out_shape=out_shape,
        compiler_params=pltpu.CompilerParams(dimension_semantics=("arbitrary" if accs else "parallel",),
                                             vmem_limit_bytes=VMEM_LIMIT),
        name=name)(*[r[0] for r in rows], *consts)


def _gla_chunk(q, k, la, v0, v1, s0, s1):
    c = q.shape[0]
    r = lax.broadcasted_iota(jnp.int32, (c, c), 0)
    cc = lax.broadcasted_iota(jnp.int32, (c, c), 1)
    tril = cc <= r
    cum = lax.dot_general(tril.astype(F32), la, (_NN, ((), ())), precision=lax.Precision.HIGHEST,
                          preferred_element_type=F32)
    cl = jnp.sum(la, axis=0, keepdims=True)
    qd = q * (GLA_DK ** -0.5) * jnp.exp(cum)
    ki = k * jnp.exp(-cum)
    ke = k * jnp.exp(cl - cum)
    dec = jnp.exp(cl)
    outs, news = [], []
    for h, (v, s) in enumerate(((v0, s0), (v1, s1))):
        mk = _lane_mask(GLA_DK * h, GLA_DK * (h + 1))
        qh = qd * mk
        att = jnp.where(tril, _dot_nt(qh, ki), 0.0)
        outs.append(_dot_nn(att, v) + _dot_nt(qh, s))
        news.append(s * dec + _dot_tn(v, ke * mk))
    return outs[0], outs[1], news[0], news[1]


def _gla_specs(tb, rev_nb=None):
    blk = (lambda b: b) if rev_nb is None else (lambda b: rev_nb - 1 - b)
    q = pl.BlockSpec((tb, 128), lambda p, b: (blk(b), P_GQ // 128 + p))
    k = pl.BlockSpec((tb, 128), lambda p, b: (blk(b), P_GK // 128 + p))
    la = pl.BlockSpec((tb, 128), lambda p, b: (blk(b), p))
    v = pl.BlockSpec((tb, 256), lambda p, b: (blk(b), P_GV // 256 + p))
    o = pl.BlockSpec((tb, 256), lambda p, b: (blk(b), p))
    st = pl.BlockSpec((tb // GLA_CHUNK, 2, 128, 128), lambda p, b: (blk(b), p, 0, 0))
    return q, k, la, v, o, st


def _gla_fwd(proj, la):
    s = proj.shape[0]
    tb = _tile(s, 512)
    nb, nch = s // tb, tb // GLA_CHUNK

    def kern(q_ref, k_ref, la_ref, v_ref, o_ref, st_ref, s_sc):
        @pl.when(pl.program_id(1) == 0)
        def _():
            s_sc[...] = jnp.zeros_like(s_sc)

        s0, s1 = s_sc[0], s_sc[1]
        for ci in range(nch):
            sl = slice(ci * GLA_CHUNK, (ci + 1) * GLA_CHUNK)
            st_ref[ci, 0] = s0
            st_ref[ci, 1] = s1
            o0, o1, s0, s1 = _gla_chunk(q_ref[sl, :], k_ref[sl, :], la_ref[sl, :], v_ref[sl, 0:128],
                                        v_ref[sl, 128:256], s0, s1)
            o_ref[sl, 0:128] = o0
            o_ref[sl, 128:256] = o1
        s_sc[0] = s0
        s_sc[1] = s1

    q, k, lasp, v, o, st = _gla_specs(tb)
    return pl.pallas_call(
        kern, grid=(2, nb), in_specs=[q, k, lasp, v], out_specs=[o, st],
        out_shape=[jax.ShapeDtypeStruct((s, 512), F32),
                   jax.ShapeDtypeStruct((s // GLA_CHUNK, GLA_HEADS, 128, 128), F32)],
        scratch_shapes=[pltpu.VMEM((2, 128, 128), F32)],
        compiler_params=pltpu.CompilerParams(dimension_semantics=("parallel", "arbitrary"),
                                             vmem_limit_bytes=VMEM_LIMIT),
        name="gla_fwd")(proj, proj, la, proj)


def _gla_bwd(proj, la, states, d_o):
    s = proj.shape[0]
    tb = _tile(s, 512)
    nb, nch = s // tb, tb // GLA_CHUNK

    def kern(q_ref, k_ref, la_ref, v_ref, do_ref, st_ref, dq_ref, dk_ref, dla_ref, dv_ref, ds_sc):
        @pl.when(pl.program_id(1) == 0)
        def _():
            ds_sc[...] = jnp.zeros_like(ds_sc)

        d0, d1 = ds_sc[0], ds_sc[1]
        for ci in reversed(range(nch)):
            sl = slice(ci * GLA_CHUNK, (ci + 1) * GLA_CHUNK)
            _, vjp = jax.vjp(_gla_chunk, q_ref[sl, :], k_ref[sl, :], la_ref[sl, :], v_ref[sl, 0:128],
                             v_ref[sl, 128:256], st_ref[ci, 0], st_ref[ci, 1])
            gq, gk, gla, gv0, gv1, d0, d1 = vjp((do_ref[sl, 0:128], do_ref[sl, 128:256], d0, d1))
            dq_ref[sl, :] = gq
            dk_ref[sl, :] = gk
            dla_ref[sl, :] = gla
            dv_ref[sl, 0:128] = gv0
            dv_ref[sl, 128:256] = gv1
        ds_sc[0] = d0
        ds_sc[1] = d1

    q, k, lasp, v, o, st = _gla_specs(tb, rev_nb=nb)
    return pl.pallas_call(
        kern, grid=(2, nb), in_specs=[q, k, lasp, v, o, st], out_specs=[lasp, lasp, lasp, o],
        out_shape=[jax.ShapeDtypeStruct((s, 256), F32), jax.ShapeDtypeStruct((s, 256), F32),
                   jax.ShapeDtypeStruct((s, 256), F32), jax.ShapeDtypeStruct((s, 512), F32)],
        scratch_shapes=[pltpu.VMEM((2, 128, 128), F32)],
        compiler_params=pltpu.CompilerParams(dimension_semantics=("parallel", "arbitrary"),
                                             vmem_limit_bytes=VMEM_LIMIT),
        name="gla_bwd")(proj, proj, la, proj, d_o, states)


def _causal_keep(t, qi, ki):
    row = lax.broadcasted_iota(jnp.int32, (t, t), 0) + qi * t
    col = lax.broadcasted_iota(jnp.int32, (t, t), 1) + ki * t
    return col <= row


def _split_refs(refs, counts):
    out, off = [], 0
    for cnt in counts:
        out.append(refs[off:off + cnt])
        off += cnt
    return out


def _attn_fwd(q, k, v, comm, tile=1024):
    s = q.shape[0]
    t = _tile(s, tile)
    n = s // t
    nci, nco = len(comm.ins), len(comm.out_shape)

    def kern(*refs):
        (q_ref, k_ref, v_ref), cins, (o_ref, lse_ref), couts, (m_sc, l_sc, acc_sc), csems = _split_refs(
            refs, (3, nci, 2, nco, 3, len(comm.sems)))
        qi, ki = pl.program_id(1), pl.program_id(2)
        place = _place()

        @pl.when((pl.program_id(0) == 0) & (qi == 0) & (ki == 0))
        def _():
            comm.start(place, cins, couts, csems)

        @pl.when((pl.program_id(0) == MLA_HEADS // 2 - 1) & (qi == 0) & (ki == 0))
        def _():
            comm.mid(place, cins, couts, csems)

        first = lax.broadcasted_iota(jnp.int32, (t, LANES), 1) < MLA_V

        @pl.when(ki == 0)
        def _():
            m_sc[...] = jnp.full_like(m_sc, -jnp.inf)
            l_sc[...] = jnp.zeros_like(l_sc)
            acc_sc[...] = jnp.zeros_like(acc_sc)

        def update(diagonal):
            keep = _causal_keep(t, 0, 0)
            alphas, pvs = [], []
            for h in range(2):
                sc = _dg(q_ref[:, 128 * h:128 * (h + 1)], k_ref[:, 128 * h:128 * (h + 1)], _NT)
                if diagonal:
                    sc = jnp.where(keep, sc, -jnp.inf)
                m_prev = m_sc[h]
                m_new = jnp.maximum(m_prev, jnp.max(sc, axis=1, keepdims=True))
                alpha = jnp.exp2(m_prev - m_new)
                p = jnp.exp2(sc - m_new[:, 0:1])
                l_sc[h] = alpha * l_sc[h] + jnp.sum(p, axis=1, keepdims=True)
                m_sc[h] = m_new
                alphas.append(alpha)
                pvs.append(_dg(p, v_ref[...], _NN))
            acc_sc[...] = acc_sc[...] * jnp.where(first, alphas[0], alphas[1]) + jnp.where(first, pvs[0], pvs[1])

        @pl.when(ki < qi)
        def _():
            update(False)

        @pl.when(ki == qi)
        def _():
            update(True)

        @pl.when(ki == qi)
        def _():
            l = jnp.where(first, l_sc[0], l_sc[1])
            m = jnp.where(first, m_sc[0], m_sc[1])
            o_ref[...] = acc_sc[...] / l
            lse_ref[...] = m + jnp.log2(l)

        @pl.when((pl.program_id(0) == MLA_HEADS // 2 - 1) & (qi == n - 1) & (ki == n - 1))
        def _():
            comm.finish(place, cins, couts, csems)

    kv_idx = lambda p, qi, ki: (jnp.minimum(ki, qi), p)
    res = pl.pallas_call(
        kern, grid=(MLA_HEADS // 2, n, n),
        in_specs=[pl.BlockSpec((t, 256), lambda p, qi, ki: (qi, p)), pl.BlockSpec((t, 256), kv_idx),
                  pl.BlockSpec((t, 128), kv_idx)] + [ANY] * nci,
        out_specs=[pl.BlockSpec((t, 128), lambda p, qi, ki: (qi, p)), pl.BlockSpec((t, 128), lambda p, qi, ki: (qi, p))]
        + [ANY] * nco,
        out_shape=[jax.ShapeDtypeStruct((s, 512), F32), jax.ShapeDtypeStruct((s, 512), F32)] + comm.out_shape,
        scratch_shapes=[pltpu.VMEM((2, t, LANES), F32), pltpu.VMEM((2, t, LANES), F32), pltpu.VMEM((t, LANES), F32)]
        + comm.sems,
        compiler_params=pltpu.CompilerParams(dimension_semantics=("arbitrary", "arbitrary", "arbitrary"),
                                             vmem_limit_bytes=VMEM_LIMIT),
        name="mla_attn_fwd")(q, k, v, *comm.ins)
    return res[0], res[1], res[2:]


def _attn_bwd(q, k, v, o, lse, dcat, comm, tile=512):
    s = q.shape[0]
    t = _tile(s, tile)
    n = s // t
    nci, nco = len(comm.ins), len(comm.out_shape)

    def kern(*refs):
        (q_ref, k_ref, v_ref, o_ref, lse_ref, do_ref), cins, (dq_ref, dk_ref, dv_ref), couts, (dk_sc, dv_sc), csems = \
            _split_refs(refs, (6, nci, 3, nco, 2, len(comm.sems)))
        ki, qi = pl.program_id(1), pl.program_id(2)
        place = _place()

        @pl.when((pl.program_id(0) == 0) & (qi == 0) & (ki == 0))
        def _():
            comm.start(place, cins, couts, csems)

        @pl.when((pl.program_id(0) == MLA_HEADS // 2 - 1) & (qi == 0) & (ki == 0))
        def _():
            comm.mid(place, cins, couts, csems)

        @pl.when((ki == 0) & (qi == 0))
        def _():
            dq_ref[...] = jnp.zeros_like(dq_ref)

        @pl.when(qi == ki)
        def _():
            dk_sc[...] = jnp.zeros_like(dk_sc)
            dv_sc[...] = jnp.zeros_like(dv_sc)

        def update(diagonal):
            keep = _causal_keep(t, 0, 0)
            d_o = do_ref[...]
            prod = d_o * o_ref[...]
            rows = pl.ds(pl.multiple_of(qi * t, t), t)
            for h in range(2):
                hs = slice(128 * h, 128 * (h + 1))
                mk = _lane_mask(MLA_V * h, MLA_V * (h + 1))
                qh, kh = q_ref[:, hs], k_ref[:, hs]
                sc = _dg(qh, kh, _NT)
                if diagonal:
                    sc = jnp.where(keep, sc, -jnp.inf)
                p = jnp.exp2(sc - lse_ref[:, MLA_V * h:MLA_V * h + 1])
                doh = d_o * mk
                dp = _dg(doh * LN2, v_ref[...], _NT)
                delta = jnp.sum(prod * mk, axis=1, keepdims=True) * LN2
                ds = p * (dp - delta)
                dv_sc[...] += _dg(p, doh, _TN)
                dk_sc[:, hs] += _dg(ds, qh, _TN)
                dq_ref[rows, hs] += _dg(ds, kh, _NN)

        @pl.when(qi > ki)
        def _():
            update(False)

        @pl.when(qi == ki)
        def _():
            update(True)

        @pl.when(qi == n - 1)
        def _():
            dk_ref[...] = dk_sc[...]
            dv_ref[...] = dv_sc[...].astype(dv_ref.dtype)

        @pl.when((pl.program_id(0) == MLA_HEADS // 2 - 1) & (qi == n - 1) & (ki == n - 1))
        def _():
            comm.finish(place, cins, couts, csems)

    q_idx = lambda p, ki, qi: (jnp.maximum(qi, ki), p)
    res = pl.pallas_call(
        kern, grid=(MLA_HEADS // 2, n, n),
        in_specs=[pl.BlockSpec((t, 256), q_idx), pl.BlockSpec((t, 256), lambda p, ki, qi: (ki, p)),
                  pl.BlockSpec((t, 128), lambda p, ki, qi: (ki, p)), pl.BlockSpec((t, 128), q_idx),
                  pl.BlockSpec((t, 128), q_idx),
                  pl.BlockSpec((t, 128), lambda p, ki, qi: (jnp.maximum(qi, ki), 4 + p))] + [ANY] * nci,
        out_specs=[pl.BlockSpec((s, 256), lambda p, ki, qi: (0, p)), pl.BlockSpec((t, 256), lambda p, ki, qi: (ki, p)),
                   pl.BlockSpec((t, 128), lambda p, ki, qi: (ki, p))] + [ANY] * nco,
        out_shape=[jax.ShapeDtypeStruct((s, 1024), F32), jax.ShapeDtypeStruct((s, 1024), F32),
                   jax.ShapeDtypeStruct((s, 512), BF16)] + comm.out_shape,
        scratch_shapes=[pltpu.VMEM((t, 256), F32), pltpu.VMEM((t, 128), F32)] + comm.sems,
        compiler_params=pltpu.CompilerParams(dimension_semantics=("arbitrary", "arbitrary", "arbitrary"),
                                             vmem_limit_bytes=VMEM_LIMIT),
        name="mla_attn_bwd")(q, k, v, o, lse, dcat, *comm.ins)
    return res[0], res[1], res[2], res[3:]


def _gate_fn(alr, w2, b):
    return _log_sigmoid(_dot_nn(alr, w2) + b) * (1.0 / GLA_GATE_NORM)


def _qk_head(qh, kh, kpe, c, sa, sb, qn, kn):
    kfull = kh + kpe * _lane_mask(MLA_NOPE, MLA_QK)
    q_r = _rope(_rms(qh, qn, MLA_QK), c, sa, sb) * (MLA_QK ** -0.5 * LOG2E)
    k_r = _rope(_rms(kfull, kn, MLA_QK), c, sa, sb)
    return q_r, k_r


def _mix_head(o, og, gn):
    return _rms(o, gn) * _silu(og)


def _xa_head(xq, xk, xv, qn, kn):
    sc = _dot_nt(_rms(xq, qn), _rms(xk, kn)) * (XA_DIM ** -0.5)
    e = jnp.exp(sc - lax.stop_gradient(jnp.max(sc, axis=1, keepdims=True)))
    p = e / jnp.sum(e, axis=1, keepdims=True)
    return _dot_nn(p, xv)


def _heads(x, n):
    return [x[:, 128 * h:128 * (h + 1)] for h in range(n)]


def _cat(xs):
    return jnp.concatenate(xs, axis=1)


def _norm_fwd(x, w, name):
    return _rows_call(lambda r, c: ([_rms(r[0], c[0])], []), [_row(x)], [w], [(x.shape[1], BF16)], name=name)[0]


def _norm_bwd(x, w, d_out, add, name):
    def body(r, c):
        _, vjp = jax.vjp(_rms, r[0], c[0])
        dx, dw = vjp(r[1])
        return [dx + r[2]], [dw]

    return _rows_call(body, [_row(x), _row(d_out), _row(add)], [w], [(x.shape[1], F32)], [w.shape], name=name)


CONV_HALO = 8


def _conv_specs(s, f, t):
    n8 = t // CONV_HALO
    cur = pl.BlockSpec((None, t, f), lambda j, i: (j, i, 0))
    prev = pl.BlockSpec((None, CONV_HALO, f), lambda j, i: (j, jnp.maximum(i * n8 - 1, 0), 0))
    nxt = pl.BlockSpec((None, CONV_HALO, f), lambda j, i: (j, jnp.minimum((i + 1) * n8, s // CONV_HALO - 1), 0))
    cw = pl.BlockSpec((None, 3, f), lambda j, i: (j, 0, 0))
    cb = pl.BlockSpec((None, 1, f), lambda j, i: (j, 0, 0))
    return cur, prev, nxt, cw, cb


def _conv_taps(g, prev, first):
    ext = jnp.concatenate([jnp.where(first, 0.0, prev), g], axis=0)
    return pltpu.roll(ext, 1, 0)[CONV_HALO:], pltpu.roll(ext, 2, 0)[CONV_HALO:]


def _conv_fwd(gg, uu, cw, cb):
    _, s, f = gg.shape
    t = _tile(s, 512)

    def kern(g_ref, gp_ref, u_ref, cw_ref, cb_ref, o_ref):
        g = g_ref[...]
        g1, g2 = _conv_taps(g, gp_ref[...], pl.program_id(1) == 0)
        w = cw_ref[...]
        gc = cb_ref[...] + w[0:1] * g2 + w[1:2] * g1 + w[2:3] * g
        o_ref[...] = (_silu(gc) * u_ref[...]).astype(o_ref.dtype)

    cur, prev, _, cws, cbs = _conv_specs(s, f, t)
    return pl.pallas_call(
        kern, grid=(4, s // t), in_specs=[cur, prev, cur, cws, cbs], out_specs=cur,
        out_shape=jax.ShapeDtypeStruct(gg.shape, BF16),
        compiler_params=pltpu.CompilerParams(dimension_semantics=("parallel", "parallel"), vmem_limit_bytes=VMEM_LIMIT),
        name="ffn_conv_fwd")(gg, gg, uu, cw, cb)


def _conv_bwd_gate(gg, uu, dact, cw, cb):
    _, s, f = gg.shape
    t = _tile(s, 512)

    def kern(g_ref, gp_ref, u_ref, da_ref, cw_ref, cb_ref, du_ref, dgc_ref, dcw_ref, dcb_ref):
        i = pl.program_id(1)
        g, u, da = g_ref[...], u_ref[...], da_ref[...]
        g1, g2 = _conv_taps(g, gp_ref[...], i == 0)
        w = cw_ref[...]
        gc = cb_ref[...] + w[0:1] * g2 + w[1:2] * g1 + w[2:3] * g
        sg = jax.nn.sigmoid(gc)
        du_ref[...] = (da * (gc * sg)).astype(du_ref.dtype)
        dgc = da * u * (sg * (1.0 + gc * (1.0 - sg)))
        dgc_ref[...] = dgc

        @pl.when(i == 0)
        def _():
            dcw_ref[...] = jnp.zeros_like(dcw_ref)
            dcb_ref[...] = jnp.zeros_like(dcb_ref)

        dcw_ref[0:1, :] += jnp.sum(dgc * g2, axis=0, keepdims=True)
        dcw_ref[1:2, :] += jnp.sum(dgc * g1, axis=0, keepdims=True)
        dcw_ref[2:3, :] += jnp.sum(dgc * g, axis=0, keepdims=True)
        dcb_ref[...] += jnp.sum(dgc, axis=0, keepdims=True)

    cur, prev, _, cws, cbs = _conv_specs(s, f, t)
    return pl.pallas_call(
        kern, grid=(4, s // t), in_specs=[cur, prev, cur, cur, cws, cbs], out_specs=[cur, cur, cws, cbs],
        out_shape=[jax.ShapeDtypeStruct(gg.shape, BF16), jax.ShapeDtypeStruct(gg.shape, F32),
                   jax.ShapeDtypeStruct(cw.shape, F32), jax.ShapeDtypeStruct(cb.shape, F32)],
        compiler_params=pltpu.CompilerParams(dimension_semantics=("parallel", "arbitrary"), vmem_limit_bytes=VMEM_LIMIT),
        name="ffn_conv_bwd_gate")(gg, gg, uu, dact, cw, cb)


def _conv_bwd_taps(dgc, cw):
    _, s, f = dgc.shape
    t = _tile(s, 512)
    nt = s // t

    def kern(d_ref, dn_ref, cw_ref, o_ref):
        d = d_ref[...]
        ext = jnp.concatenate([d, jnp.where(pl.program_id(1) == nt - 1, 0.0, dn_ref[...])], axis=0)
        up1 = pltpu.roll(ext, t + CONV_HALO - 1, 0)[:t]
        up2 = pltpu.roll(ext, t + CONV_HALO - 2, 0)[:t]
        w = cw_ref[...]
        o_ref[...] = (w[2:3] * d + w[1:2] * up1 + w[0:1] * up2).astype(o_ref.dtype)

    cur, _, nxt, cws, _ = _conv_specs(s, f, t)
    return pl.pallas_call(
        kern, grid=(4, nt), in_specs=[cur, nxt, cws], out_specs=cur, out_shape=jax.ShapeDtypeStruct(dgc.shape, BF16),
        compiler_params=pltpu.CompilerParams(dimension_semantics=("parallel", "parallel"), vmem_limit_bytes=VMEM_LIMIT),
        name="ffn_conv_bwd_taps")(dgc, dgc, cw)


def _rope_tables(pos):
    half = MLA_ROPE // 2
    inv = ROPE_THETA ** (-jnp.arange(half, dtype=F32) / half)
    ang = pos.astype(F32)[:, None] * inv
    cos, sin = jnp.cos(ang), jnp.sin(ang)
    s = pos.shape[0]
    z = lambda w: jnp.zeros((s, w), F32)
    c = jnp.concatenate([jnp.ones((s, MLA_NOPE), F32), cos, cos, jnp.ones((s, LANES - MLA_QK), F32)], axis=1)
    sa = jnp.concatenate([z(MLA_NOPE), -sin, z(half), z(LANES - MLA_QK)], axis=1)
    sb = jnp.concatenate([z(MLA_NOPE), z(half), sin, z(LANES - MLA_QK)], axis=1)
    return c, sa, sb


def _local_step(x, mem, pos, target, w, late_shards):
    g = {}
    w = dict(w)
    c, sa, sb = _rope_tables(pos)

    xn = _norm_fwd(x, w["norm_mix"], "norm_mix_fwd")
    proj = _matmul(xn, w["in"], "nn", F32, "proj_fwd")
    alr = _row(proj, 128, P_ALR // 128)
    kpe = _row(proj, 128, P_KPE // 128)
    og = _row(proj, 512, P_OG // 512)
    cq = _row(proj, 256, P_CQ // 256)
    ckv = _row(proj, 128, P_CKV // 128)

    la = _rows_call(lambda r, k: ([_gate_fn(r[0], k[0], k[1])], []), [alr], [w["w2"], w["gate_b"]],
                    [(256, F32)], name="gla_gate_fwd")[0]
    o_gla, states = _gla_fwd(proj, la)

    q_lat, kv_lat = _rows_call(lambda r, k: ([_rms(r[0], k[0]), _rms(r[1], k[1])], []), [cq, ckv],
                               [w["q_a_norm"], w["kv_a_norm"]], [(256, BF16), (128, BF16)], name="mla_lat_fwd")
    q_up = _matmul(q_lat, w["uq"], "nn", F32, "mla_q_fwd")
    k_up = _matmul(kv_lat, w["k"], "nn", F32, "mla_k_fwd")
    v_mla = _matmul(kv_lat, w["v"], "nn", BF16, "mla_v_fwd")

    def qk_body(r, k):
        qs, ks = [], []
        for qh, kh in zip(_heads(r[0], MLA_HEADS), _heads(r[1], MLA_HEADS)):
            a, b = _qk_head(qh, kh, r[2], r[3], r[4], r[5], k[0], k[1])
            qs.append(a)
            ks.append(b)
        return [_cat(qs), _cat(ks)], []

    tabs = [_row(c), _row(sa), _row(sb)]
    q_r, k_r = _rows_call(qk_body, [_row(q_up), _row(k_up), kpe] + tabs, [w["q_norm"], w["k_norm"]],
                          [(1024, BF16), (1024, BF16)], name="mla_qk_fwd")
    o_mla, lse, gathered = _attn_fwd(q_r, k_r, v_mla, _gather_plan(late_shards))
    w.update(_late_layout(dict(zip(LATE, gathered, strict=True))))

    def mix_body(r, k):
        ys = [_mix_head(o, g_, k[0]) for o, g_ in zip(_heads(r[0], GLA_HEADS), _heads(r[1], GLA_HEADS))]
        return [_cat(ys + [r[2]])], []

    cat = _rows_call(mix_body, [_row(o_gla), og, _row(o_mla)], [w["gla_out_norm"]], [(1024, BF16)],
                     name="mix_fwd")[0]
    h1 = _matmul(cat, w["out"], "nn", F32, "out_fwd", residual=x)

    hn = _norm_fwd(h1, w["norm_xa"], "norm_xa_fwd")
    mn = _norm_fwd(mem, w["norm_mem"], "norm_mem_fwd")
    xq = _matmul(hn, w["xq"], "nn", F32, "xa_q_fwd")
    xkv = _matmul(mn, w["xkv"], "nn", F32, "xa_kv_fwd")

    def xa_body(r, k):
        ks, vs = _heads(k[0], 2 * XA_HEADS)[:XA_HEADS], _heads(k[0], 2 * XA_HEADS)[XA_HEADS:]
        return [_cat([_xa_head(a, b, v_, k[1], k[2]) for a, b, v_ in zip(_heads(r[0], XA_HEADS), ks, vs)])], []

    xo = _rows_call(xa_body, [_row(xq)], [xkv, w["xa_q_norm"], w["xa_k_norm"]], [(512, BF16)], name="xa_fwd")[0]
    h2 = _matmul(xo, w["xo"], "nn", F32, "xa_o_fwd", residual=h1)

    fn = _norm_fwd(h2, w["norm_ffn"], "norm_ffn_fwd")
    gg = _matmul(fn, w["wg"], "nt", F32, "ffn_gate_fwd", b_lead="p")
    uu = _matmul(fn, w["wu"], "nt", F32, "ffn_up_fwd", b_lead="p")
    act = _conv_fwd(gg, uu, w["cw"], w["cb"])
    y = _matmul(act, w["wd"], "nn", F32, "ffn_down_fwd", residual=h2, a_lead="k", b_lead="k")

    def loss_body(r, k):
        err = r[0] - r[1]
        part = 0.5 * jnp.sum(jnp.sum(err * err, axis=1, keepdims=True) * (1.0 / D_MODEL), axis=0, keepdims=True)
        return [err * (1.0 / D_MODEL)], [jnp.broadcast_to(part, (1, LANES))]

    dy, loss = _rows_call(loss_body, [_row(y), _row(target)], [], [(D_MODEL, F32)], [(1, LANES)], name="loss")

    g["ffn_w_down"] = _matmul(act, dy, "tn", BF16, "ffn_down_dw", a_lead="p")
    dact = _matmul(dy, w["wd"], "nt", F32, "ffn_down_dx", b_lead="p")
    duu, dgc, g["ffn_conv_w"], g["ffn_conv_b"] = _conv_bwd_gate(gg, uu, dact, w["cw"], w["cb"])
    dgg = _conv_bwd_taps(dgc, w["cw"])
    g["ffn_w_gate"] = _matmul(dgg, fn, "tn", BF16, "ffn_gate_dw", a_lead="p")
    g["ffn_w_up"] = _matmul(duu, fn, "tn", BF16, "ffn_up_dw", a_lead="p")
    dfn = _matmul(dgg, w["wg"], "nn", F32, "ffn_gate_dx", a_lead="k", b_lead="k")
    dfn = _matmul(duu, w["wu"], "nn", F32, "ffn_up_dx", residual=dfn, a_lead="k", b_lead="k")
    dh2, g["norm_ffn"] = _norm_bwd(h2, w["norm_ffn"], dfn, dy, "norm_ffn_bwd")

    g["xa_w_o"] = _matmul(xo, dh2, "tn", BF16, "xa_o_dw")
    dxo = _matmul(dh2, w["xo"], "nt", F32, "xa_o_dx")

    def xa_bwd(r, k):
        kvh = _heads(k[0], 2 * XA_HEADS)
        dq_, dk_, dv_ = [], [], []
        dqn, dkn = 0.0, 0.0
        for h, (a, d_) in enumerate(zip(_heads(r[0], XA_HEADS), _heads(r[1], XA_HEADS))):
            _, vjp = jax.vjp(_xa_head, a, kvh[h], kvh[XA_HEADS + h], k[1], k[2])
            ga, gk, gv, gqn, gkn = vjp(d_)
            dq_.append(ga)
            dk_.append(gk)
            dv_.append(gv)
            dqn, dkn = dqn + gqn, dkn + gkn
        return [_cat(dq_)], [_cat(dk_ + dv_), dqn, dkn]

    dxq, dxkv, g["xa_q_norm"], g["xa_k_norm"] = _rows_call(
        xa_bwd, [_row(xq), _row(dxo)], [xkv, w["xa_q_norm"], w["xa_k_norm"]], [(512, BF16)],
        [xkv.shape, (1, 128), (1, 128)], name="xa_bwd")
    g["xa_w_q"] = _matmul(hn, dxq, "tn", BF16, "xa_q_dw")
    dhn = _matmul(dxq, w["xq"], "nt", F32, "xa_q_dx")
    g["xa_w_kv"] = _matmul(mn, dxkv, "tn", BF16, "xa_kv_dw")
    dmn = _matmul(dxkv, w["xkv"], "nt", F32, "xa_kv_dx")
    _, g["norm_mem"] = _norm_bwd(mem, w["norm_mem"], dmn, dmn, "norm_mem_bwd")
    dh1, g["norm_xa"] = _norm_bwd(h1, w["norm_xa"], dhn, dh2, "norm_xa_bwd")

    g["w_out"] = _matmul(cat, dh1, "tn", BF16, "out_dw")
    dcat = _matmul(dh1, w["out"], "nt", F32, "out_dx")

    def mix_bwd(r, k):
        do_, dog_ = [], []
        dgn = 0.0
        for o, g_, d_ in zip(_heads(r[0], GLA_HEADS), _heads(r[1], GLA_HEADS), _heads(r[2], GLA_HEADS)):
            _, vjp = jax.vjp(_mix_head, o, g_, k[0])
            a, b, gn_ = vjp(d_)
            do_.append(a)
            dog_.append(b)
            dgn = dgn + gn_
        return [_cat(do_), _cat(dog_)], [dgn]

    do_gla, d_og, g["gla_out_norm"] = _rows_call(mix_bwd, [_row(o_gla), og, _row(dcat, 512, 0)], [w["gla_out_norm"]],
                                                 [(512, F32), (512, BF16)], [(1, 128)], name="mix_bwd")

    late_parts = _late_grad_shards(g)
    dq_r, dk_r, dv_mla, lands_late = _attn_bwd(q_r, k_r, v_mla, o_mla, lse, dcat,
                                               _scatter_plan([late_parts[n] for n in LATE]))

    def qk_bwd(r, k):
        dqs, dks = [], []
        dkpe, dqn, dkn = 0.0, 0.0, 0.0
        for qh, kh, dqh, dkh in zip(_heads(r[0], MLA_HEADS), _heads(r[1], MLA_HEADS), _heads(r[6], MLA_HEADS),
                                    _heads(r[7], MLA_HEADS)):
            _, vjp = jax.vjp(lambda a, b, e, f, h_: _qk_head(a, b, e, r[3], r[4], r[5], f, h_), qh, kh, r[2], k[0], k[1])
            ga, gb, ge, gf, gh = vjp((dqh, dkh))
            dqs.append(ga)
            dks.append(gb)
            dkpe, dqn, dkn = dkpe + ge, dqn + gf, dkn + gh
        return [_cat(dqs), _cat(dks), dkpe], [dqn, dkn]

    dq_up, dk_up, d_kpe, g["q_norm"], g["k_norm"] = _rows_call(
        qk_bwd, [_row(q_up), _row(k_up), kpe] + tabs + [_row(dq_r), _row(dk_r)], [w["q_norm"], w["k_norm"]],
        [(1024, BF16), (1024, BF16), (128, BF16)], [(1, 128), (1, 128)], name="mla_qk_bwd")
    g["uq"] = _matmul(q_lat, dq_up, "tn", BF16, "mla_q_dw")
    dq_lat = _matmul(dq_up, w["uq"], "nt", F32, "mla_q_dx")
    g["k"] = _matmul(kv_lat, dk_up, "tn", BF16, "mla_k_dw")
    g["v"] = _matmul(kv_lat, dv_mla, "tn", BF16, "mla_v_dw")
    dkv_lat = _matmul(dk_up, w["k"], "nt", F32, "mla_k_dx")
    dkv_lat = _matmul(dv_mla, w["v"], "nt", F32, "mla_v_dx", residual=dkv_lat)

    def lat_bwd(r, k):
        _, vjp1 = jax.vjp(_rms, r[0], k[0])
        _, vjp2 = jax.vjp(_rms, r[1], k[1])
        a, ga = vjp1(r[2])
        b, gb = vjp2(r[3])
        return [a, b], [ga, gb]

    d_cq, d_ckv, g["mla_q_a_norm"], g["mla_kv_a_norm"] = _rows_call(
        lat_bwd, [cq, ckv, _row(dq_lat), _row(dkv_lat)], [w["q_a_norm"], w["kv_a_norm"]],
        [(256, BF16), (128, BF16)], [(1, 256), (1, 128)], name="mla_lat_bwd")

    dgq, dgk, dla, dgv = _gla_bwd(proj, la, states, do_gla)

    def gate_bwd(r, k):
        _, vjp = jax.vjp(_gate_fn, r[0], k[0], k[1])
        a, gw, gb = vjp(r[1])
        return [a], [gw, gb]

    d_alr, g["w2"], g["gla_gate_b"] = _rows_call(gate_bwd, [alr, _row(dla)], [w["w2"], w["gate_b"]], [(128, BF16)],
                                                 [(128, 256), (1, 256)], name="gla_gate_bwd")

    dproj = jnp.concatenate([dgq.astype(BF16), dgk.astype(BF16), dgv.astype(BF16), d_og, d_cq, d_ckv, d_kpe, d_alr],
                            axis=1)
    g["in"] = _matmul(xn, dproj, "tn", BF16, "proj_dw")
    dxn = _matmul(dproj, w["in"], "nt", F32, "proj_dx")
    dx, g["norm_mix"] = _norm_bwd(x, w["norm_mix"], dxn, dh1, "norm_mix_bwd")
    return loss[0, 0], dx, g, lands_late


def _join_shards(pieces, axis):
    if axis == 0:
        return pieces.reshape(-1, pieces.shape[2])
    return jnp.transpose(pieces, (1, 0, 2)).reshape(pieces.shape[1], -1)


def _split_shards(full, axis):
    r, c = full.shape
    if axis == 0:
        return full.reshape(4, r // 4, c)
    return jnp.transpose(full.reshape(r, 4, c // 4), (1, 0, 2))


def _early_layout(gath, rep):
    w_in = _join_shards(gath["w_in"], 1)
    z = lambda n: jnp.zeros((D_MODEL, n), w_in.dtype)
    seg = lambda lo, n: w_in[:, lo:lo + n]
    ukv = _join_shards(gath["mla_w_ukv"], 1).reshape(MLA_KV_RANK, MLA_HEADS, MLA_NOPE + MLA_V)
    w = {
        "in": jnp.concatenate([seg(N_GQ, 256), seg(N_GK, 256), seg(N_GV, 512), seg(N_OG, 512), seg(N_CQ, 256),
                               seg(N_CKV, 128), z(64), seg(N_KPE, 32), z(32), seg(N_ALR, 16), z(112)], axis=1),
        "uq": jnp.pad(_join_shards(gath["mla_w_uq"], 1).reshape(MLA_Q_RANK, MLA_HEADS, MLA_QK),
                      ((0, 0), (0, 0), (0, LANES - MLA_QK))).reshape(MLA_Q_RANK, MLA_HEADS * LANES),
        "k": jnp.pad(ukv[:, :, :MLA_NOPE], ((0, 0), (0, 0), (0, LANES - MLA_NOPE))).reshape(MLA_KV_RANK, -1),
        "v": ukv[:, :, MLA_NOPE:].reshape(MLA_KV_RANK, MLA_HEADS * MLA_V),
        "w2": jnp.pad(_join_shards(gath["gla_gate_w2"], 1), ((0, LANES - GLA_RANK), (0, 0))),
        "cb": rep["ffn_conv_b"].reshape(4, 1, D_FF // 4),
        "q_norm": jnp.pad(rep["mla_q_norm"], ((0, 0), (0, LANES - MLA_QK))),
        "k_norm": jnp.pad(rep["mla_k_norm"], ((0, 0), (0, LANES - MLA_QK))),
        "q_a_norm": rep["mla_q_a_norm"], "kv_a_norm": rep["mla_kv_a_norm"], "gate_b": rep["gla_gate_b"],
    }
    for n in ("norm_mix", "gla_out_norm", "norm_xa", "norm_mem", "xa_q_norm", "xa_k_norm", "norm_ffn"):
        w[n] = rep[n]
    return w


def _late_layout(gath):
    return {"out": _join_shards(gath["w_out"], 0), "xq": _join_shards(gath["xa_w_q"], 0),
            "xkv": _join_shards(gath["xa_w_kv"], 0), "xo": _join_shards(gath["xa_w_o"], 1),
            "wg": gath["ffn_w_gate"], "wu": gath["ffn_w_up"], "wd": gath["ffn_w_down"], "cw": gath["ffn_conv_w"]}


def _late_grad_shards(g):
    sh = {"w_out": _split_shards(g["w_out"], 0), "xa_w_q": _split_shards(g["xa_w_q"], 0),
          "xa_w_kv": _split_shards(g["xa_w_kv"], 0), "xa_w_o": _split_shards(g["xa_w_o"], 1),
          "ffn_w_gate": g["ffn_w_gate"], "ffn_w_up": g["ffn_w_up"], "ffn_conv_w": g["ffn_conv_w"],
          "ffn_w_down": g["ffn_w_down"]}
    return {n: v.astype(BF16) for n, v in sh.items()}


def _early_grad_shards(g):
    gi = g["in"]
    seg = lambda lo, n: gi[:, lo:lo + n]
    w_in = jnp.concatenate([seg(P_GQ, 256), seg(P_GK, 256), seg(P_GV, 512), seg(P_ALR, 16), seg(P_OG, 512),
                            seg(P_CQ, 256), seg(P_CKV, 128), seg(P_KPE + 64, 32)], axis=1)
    uq = g["uq"].reshape(MLA_Q_RANK, MLA_HEADS, LANES)[:, :, :MLA_QK].reshape(MLA_Q_RANK, -1)
    ukv = jnp.concatenate([g["k"].reshape(MLA_KV_RANK, MLA_HEADS, LANES)[:, :, :MLA_NOPE],
                           g["v"].reshape(MLA_KV_RANK, MLA_HEADS, MLA_V)], axis=2).reshape(MLA_KV_RANK, -1)
    sh = {"w_in": _split_shards(w_in, 1), "gla_gate_w2": _split_shards(g["w2"][:GLA_RANK], 1),
          "mla_w_uq": _split_shards(uq, 1), "mla_w_ukv": _split_shards(ukv, 1)}
    sh = {n: v.astype(BF16) for n, v in sh.items()}
    rep = {n: g[n] for n in REPLICATED if n in g}
    rep["mla_q_norm"] = g["q_norm"][:, :MLA_QK]
    rep["mla_k_norm"] = g["k_norm"][:, :MLA_QK]
    rep["ffn_conv_b"] = g["ffn_conv_b"].reshape(1, D_FF)
    return sh, rep


SMALL_SHAPE = (8, 1024)


def _pack_small(vectors):
    flat = jnp.concatenate(vectors, axis=1)
    return jnp.pad(flat, ((0, 0), (0, SMALL_SHAPE[0] * SMALL_SHAPE[1] - flat.shape[1]))).reshape(SMALL_SHAPE)


def _unpack_small(buf, widths):
    flat = buf.reshape(1, -1)
    out, off = [], 0
    for wd in widths:
        out.append(flat[:, off:off + wd])
        off += wd
    return out


ANY = pl.BlockSpec(memory_space=pl.ANY)


def _place():
    x, y, c = lax.axis_index("x"), lax.axis_index("y"), lax.axis_index("c")
    chips = [(1 - x, y), (x, 1 - y), (1 - x, 1 - y)]
    return x, y, c, chips


class _Comm:
    def __init__(self, ins, out_shape, sems, start, finish, mid=None):
        self.ins, self.out_shape, self.sems = list(ins), list(out_shape), list(sems)
        self.start, self.finish, self.mid = start, finish, mid or (lambda *args: None)


def _run_comm(plan, name):
    ni, no = len(plan.ins), len(plan.out_shape)

    def body(*refs):
        ins, outs, sems = refs[:ni], refs[ni:ni + no], refs[ni + no:]
        place = _place()
        plan.start(place, ins, outs, sems)
        plan.mid(place, ins, outs, sems)
        plan.finish(place, ins, outs, sems)

    return pl.pallas_call(body, in_specs=[ANY] * ni, out_specs=[ANY] * no, out_shape=plan.out_shape,
                          scratch_shapes=plan.sems, name=name)(*plan.ins)


def _gather_plan(shards):
    n = len(shards)
    split = [s.shape[0] % (2 * BF16_ROWS) == 0 for s in shards]

    def rows(ref, t, c):
        if not split[t]:
            return ref
        half = shards[t].shape[0] // 2
        return ref.at[pl.ds(pl.multiple_of(c * half, BF16_ROWS), half)]

    def remote(src, dst, ss, rs, to):
        return pltpu.make_async_remote_copy(src_ref=src, dst_ref=dst, send_sem=ss, recv_sem=rs, device_id=to,
                                            device_id_type=MESH)

    def first_wave(place, ins, outs, sems):
        x, y, c, chips = place
        ici_s, ici_r, _, _, local = sems
        me = 2 * x + y
        own = [pltpu.make_async_copy(ins[t], outs[t].at[me], local.at[t]) for t in range(n)]
        push = [remote(rows(ins[t], t, c), rows(outs[t].at[me], t, c), ici_s.at[3 * t + j], ici_r.at[3 * t + j], (px, py, c))
                for t in range(n) for j, (px, py) in enumerate(chips)]
        return own, push

    def second_wave(place, ins, outs, sems, last):
        x, y, c, chips = place
        ici_s, ici_r, d2d_s, d2d_r, local = sems
        sib = (x, y, 1 - c)
        out = []
        for t in range(n):
            for j, (px, py) in enumerate(chips):
                block = outs[t].at[2 * px + py]
                got = rows(block, t, c)
                if split[t]:
                    hand = remote(got, got, d2d_s.at[3 * t + j], d2d_r.at[3 * t + j], sib)
                    theirs = rows(block, t, 1 - c)
                    other = (remote(theirs, theirs, local.at[0], d2d_r.at[3 * t + j], sib) if last else
                             remote(got, got, local.at[0], ici_r.at[3 * t + j], sib))
                    out.append((other, hand))
                elif last:
                    out.append((remote(got, got, local.at[0], ici_r.at[3 * t + j], sib), None))
        return out

    def start(place, ins, outs, sems):
        own, push = first_wave(place, ins, outs, sems)
        for cp in own + push:
            cp.start()

    def mid(place, ins, outs, sems):
        for arrival, hand in second_wave(place, ins, outs, sems, False):
            arrival.wait_recv()
            hand.start()

    def finish(place, ins, outs, sems):
        own, push = first_wave(place, ins, outs, sems)
        for arrival, hand in second_wave(place, ins, outs, sems, True):
            arrival.wait_recv()
            if hand is not None:
                hand.wait_send()
        for cp in push:
            cp.wait_send()
        for cp in own:
            cp.wait()

    dma = pltpu.SemaphoreType.DMA
    return _Comm(shards, [jax.ShapeDtypeStruct((4,) + s.shape, s.dtype) for s in shards],
                 [dma((3 * n,)), dma((3 * n,)), dma((3 * n,)), dma((3 * n,)), dma((n,))], start, finish, mid)


def _scatter_plan(parts, small=None):
    n = len(parts)
    ns = 0 if small is None else 1

    def unpack(place, ins, outs, sems):
        x, y, c, chips = place
        return x, y, c, chips, 2 * x + y, 4 * x + 2 * y + c, (x, y, 1 - c)

    def remote(src, dst, ss, rs, to):
        return pltpu.make_async_remote_copy(src_ref=src, dst_ref=dst, send_sem=ss, recv_sem=rs, device_id=to,
                                            device_id_type=MESH)

    def first_wave(place, ins, outs, sems):
        x, y, c, chips, me, dev, sib = unpack(place, ins, outs, sems)
        ici_s, ici_r, d2d_s, d2d_r, sm_s, sm_r, local = sems
        own, push = [], []
        if ns:
            own.append(pltpu.make_async_copy(ins[n], outs[n].at[dev], local.at[n]))
            for k in range(1, 8):
                px = (1 - x) if (k >> 2) & 1 else x
                py = (1 - y) if (k >> 1) & 1 else y
                pc = (1 - c) if k & 1 else c
                push.append(remote(ins[n], outs[n].at[dev], sm_s.at[k - 1], sm_r.at[k - 1], (px, py, pc)))
        for t in range(n):
            own.append(pltpu.make_async_copy(ins[t].at[me], outs[t].at[dev], local.at[t]))
            push.append(remote(ins[t].at[me], outs[t].at[dev], d2d_s.at[4 * t], d2d_r.at[4 * t], sib))
            for j, (px, py) in enumerate(chips):
                push.append(remote(ins[t].at[2 * px + py], outs[t].at[dev], ici_s.at[3 * t + j], ici_r.at[3 * t + j],
                                   (px, py, c)))
        return own, push

    def start(place, ins, outs, sems):
        own, push = first_wave(place, ins, outs, sems)
        for cp in own + push:
            cp.start()

    def landed(dst, rs, sems, sib):
        remote(dst, dst, sems[-1].at[0], rs, sib).wait_recv()

    def forwards(place, ins, outs, sems):
        x, y, c, chips, me, dev, sib = unpack(place, ins, outs, sems)
        d2d_s, d2d_r = sems[2], sems[3]
        slots = [(t, j, outs[t].at[4 * px + 2 * py + c]) for t in range(n) for j, (px, py) in enumerate(chips)]
        return [(t, j, slot, remote(slot, slot, d2d_s.at[4 * t + 1 + j], d2d_r.at[4 * t + 1 + j], sib))
                for t, j, slot in slots]

    def mid(place, ins, outs, sems):
        sib = unpack(place, ins, outs, sems)[-1]
        for t, j, slot, cp in forwards(place, ins, outs, sems):
            landed(slot, sems[1].at[3 * t + j], sems, sib)
            cp.start()

    def finish(place, ins, outs, sems):
        x, y, c, chips, me, dev, sib = unpack(place, ins, outs, sems)
        d2d_r, sm_r = sems[3], sems[5]
        own, push = first_wave(place, ins, outs, sems)
        push += [cp for _, _, _, cp in forwards(place, ins, outs, sems)]
        for t in range(n):
            landed(outs[t].at[4 * x + 2 * y + (1 - c)], d2d_r.at[4 * t], sems, sib)
            for j, (px, py) in enumerate(chips):
                landed(outs[t].at[4 * px + 2 * py + (1 - c)], d2d_r.at[4 * t + 1 + j], sems, sib)
        if ns:
            for k in range(1, 8):
                px = (1 - x) if (k >> 2) & 1 else x
                py = (1 - y) if (k >> 1) & 1 else y
                pc = (1 - c) if k & 1 else c
                landed(outs[n].at[4 * px + 2 * py + pc], sm_r.at[k - 1], sems, sib)
        for cp in push:
            cp.wait_send()
        for cp in own:
            cp.wait()

    dma = pltpu.SemaphoreType.DMA
    ins = list(parts) + ([small] if ns else [])
    out_shape = [jax.ShapeDtypeStruct((8,) + p.shape[1:], p.dtype) for p in parts]
    if ns:
        out_shape.append(jax.ShapeDtypeStruct((8,) + small.shape, small.dtype))
    return _Comm(ins, out_shape, [dma((3 * n,)), dma((3 * n,)), dma((4 * n,)), dma((4 * n,)), dma((7,)), dma((7,)),
                                  dma((n + 1,))], start, finish, mid)


def _row_tile(r, cap=256):
    if r <= cap:
        return r
    return max(t for t in range(8, cap + 1, 8) if r % t == 0)


def _adamw(w, m, v, land, name):
    r, c = w.shape
    t = _row_tile(r)

    def kern(w_ref, m_ref, v_ref, l_ref, g_out, d_out, m_out, v_out):
        g = l_ref[0].astype(F32)
        for i in range(1, 8):
            g = g + l_ref[i].astype(F32)
        m_new = ADAM_B1 * m_ref[...] + (1.0 - ADAM_B1) * g
        v_new = ADAM_B2 * v_ref[...] + (1.0 - ADAM_B2) * (g * g)
        m_hat = m_new / (1.0 - ADAM_B1 ** ADAM_STEP)
        v_hat = v_new / (1.0 - ADAM_B2 ** ADAM_STEP)
        g_out[...] = g
        d_out[...] = -ADAM_LR * (m_hat / (jnp.sqrt(v_hat) + ADAM_EPS) + ADAM_WD * w_ref[...])
        m_out[...] = m_new
        v_out[...] = v_new

    spec = pl.BlockSpec((t, c), lambda i: (i, 0))
    return pl.pallas_call(
        kern, grid=(r // t,), in_specs=[spec] * 3 + [pl.BlockSpec((8, t, c), lambda i: (0, i, 0))], out_specs=[spec] * 4,
        out_shape=[jax.ShapeDtypeStruct((r, c), F32)] * 4,
        compiler_params=pltpu.CompilerParams(dimension_semantics=("parallel",), vmem_limit_bytes=VMEM_LIMIT),
        name=name)(w, m, v, land)


def _step(a):
    def sq(n):
        v = a[n][0] if a[n].ndim == 3 else a[n]
        return v.T if n.removeprefix("m_").removeprefix("v_") in TRANSPOSED else v

    payload = lambda n: sq(n) if n in EXACT_GATHER else sq(n).astype(BF16)

    gathered = _run_comm(_gather_plan([payload(n) for n in EARLY]), "gather_early")
    w = _early_layout(dict(zip(EARLY, gathered, strict=True)), {n: a[n] for n in REPLICATED})

    loss, dx, g, lands_late = _local_step(sq("x"), sq("mem"), a["positions"][0], sq("loss_target"), w,
                                          [payload(n) for n in LATE])

    sh, rep = _early_grad_shards(g)
    small = _pack_small([rep[n] for n in REPLICATED] + [loss.reshape(1, 1)])
    *lands_early, land_small = _run_comm(_scatter_plan([sh[n] for n in EARLY], small), "scatter_last")
    lands = dict(zip(EARLY + LATE, list(lands_early) + list(lands_late), strict=True))

    outs = {}
    kinds = ("grad_", "delta_", "new_m_", "new_v_")
    for n, _ in SHARDED:
        res = _adamw(sq(n), sq("m_" + n), sq("v_" + n), lands[n], "adamw_" + n)
        for kind, val in zip(kinds, res, strict=True):
            outs[kind + n] = (val.T if n in TRANSPOSED else val).reshape(a[n].shape)
    zero = jnp.zeros((1, 1), F32)
    packed = [_pack_small([a[p + n] for n in REPLICATED] + [zero]) for p in ("", "m_", "v_")]
    res = _adamw(*packed, land_small, "adamw_replicated")
    widths = [a[n].shape[1] for n in REPLICATED] + [1]
    for kind, buf in zip(kinds, res, strict=True):
        *vals, total = _unpack_small(buf, widths)
        for n, val in zip(REPLICATED, vals, strict=True):
            outs[kind + n] = val
        if kind == "grad_":
            loss = total[0, 0]

    ordered = [outs[kind + n] for kind in kinds for n in WEIGHTS]
    return (loss, dx[None], *ordered)


def kernel(x, mem, positions, norm_mix, w_in, gla_gate_w2, gla_gate_b, gla_out_norm, mla_q_a_norm, mla_w_uq, mla_kv_a_norm, mla_w_ukv, mla_q_norm, mla_k_norm, w_out, norm_xa, norm_mem, xa_w_q, xa_w_kv, xa_q_norm, xa_k_norm, xa_w_o, norm_ffn, ffn_w_gate, ffn_w_up, ffn_conv_w, ffn_conv_b, ffn_w_down, loss_target, m_norm_mix, m_w_in, m_gla_gate_w2, m_gla_gate_b, m_gla_out_norm, m_mla_q_a_norm, m_mla_w_uq, m_mla_kv_a_norm, m_mla_w_ukv, m_mla_q_norm, m_mla_k_norm, m_w_out, m_norm_xa, m_norm_mem, m_xa_w_q, m_xa_w_kv, m_xa_q_norm, m_xa_k_norm, m_xa_w_o, m_norm_ffn, m_ffn_w_gate, m_ffn_w_up, m_ffn_conv_w, m_ffn_conv_b, m_ffn_w_down, v_norm_mix, v_w_in, v_gla_gate_w2, v_gla_gate_b, v_gla_out_norm, v_mla_q_a_norm, v_mla_w_uq, v_mla_kv_a_norm, v_mla_w_ukv, v_mla_q_norm, v_mla_k_norm, v_w_out, v_norm_xa, v_norm_mem, v_xa_w_q, v_xa_w_kv, v_xa_q_norm, v_xa_k_norm, v_xa_w_o, v_norm_ffn, v_ffn_w_gate, v_ffn_w_up, v_ffn_conv_w, v_ffn_conv_b, v_ffn_w_down):
    return _step(dict(locals()))
```

```python
import functools

import jax
import jax.numpy as jnp
from jax import lax
from jax.experimental import pallas as pl
from jax.experimental.pallas import tpu as pltpu

F32, BF16 = jnp.float32, jnp.bfloat16
MESH = pl.DeviceIdType.MESH

D_MODEL = 1024
EPS = 1e-6
GLA_HEADS, GLA_DK, GLA_DV, GLA_RANK, GLA_CHUNK = 4, 64, 128, 16, 64
GLA_GATE_NORM = 16.0
MLA_HEADS, MLA_Q_RANK, MLA_KV_RANK, MLA_NOPE, MLA_ROPE, MLA_V = 8, 256, 128, 64, 32, 64
MLA_QK = MLA_NOPE + MLA_ROPE
ROPE_THETA = 10000.0
LOG2E, LN2 = 1.4426950408889634, 0.6931471805599453
XA_HEADS, XA_DIM = 4, 128
D_FF = 2816
ADAM_LR, ADAM_B1, ADAM_B2, ADAM_EPS, ADAM_WD, ADAM_STEP = 0.001, 0.9, 0.999, 1e-08, 0.01, 10

LANES = 128
BF16_ROWS = 16
VMEM_LIMIT = 56 * 1024 * 1024
MATMUL_VMEM = 44 * 1024 * 1024

P_GQ, P_GK, P_GV, P_OG, P_CQ, P_CKV, P_KPE, P_ALR, P_WIDTH = 0, 256, 512, 1024, 1536, 1792, 1920, 2048, 2176
N_GQ, N_GK, N_GV, N_ALR, N_OG, N_CQ, N_CKV, N_KPE, N_WIDTH = 0, 256, 512, 1024, 1040, 1552, 1808, 1936, 1968

SHARDED = (("w_in", 1), ("gla_gate_w2", 1), ("mla_w_uq", 1), ("mla_w_ukv", 1), ("w_out", 0), ("xa_w_q", 0),
           ("xa_w_kv", 0), ("xa_w_o", 1), ("ffn_w_gate", 1), ("ffn_w_up", 1), ("ffn_conv_w", 1), ("ffn_w_down", 0))
REPLICATED = ("norm_mix", "gla_gate_b", "gla_out_norm", "mla_q_a_norm", "mla_kv_a_norm", "mla_q_norm", "mla_k_norm",
              "norm_xa", "norm_mem", "xa_q_norm", "xa_k_norm", "norm_ffn", "ffn_conv_b")
EXACT_GATHER = ("gla_gate_w2", "ffn_conv_w")
TRANSPOSED = ("ffn_w_gate", "ffn_w_up")
EARLY = ("w_in", "gla_gate_w2", "mla_w_uq", "mla_w_ukv")
LATE = tuple(n for n, _ in SHARDED if n not in EARLY)
LATE_MLP = tuple(n for n in LATE if n.startswith("ffn_"))
LATE_MIX = tuple(n for n in LATE if not n.startswith("ffn_"))
WEIGHTS = ("norm_mix", "w_in", "gla_gate_w2", "gla_gate_b", "gla_out_norm", "mla_q_a_norm", "mla_w_uq",
           "mla_kv_a_norm", "mla_w_ukv", "mla_q_norm", "mla_k_norm", "w_out", "norm_xa", "norm_mem", "xa_w_q",
           "xa_w_kv", "xa_q_norm", "xa_k_norm", "xa_w_o", "norm_ffn", "ffn_w_gate", "ffn_w_up", "ffn_conv_w",
           "ffn_conv_b", "ffn_w_down")


_NN = ((1,), (0,))
_NT = ((1,), (1,))
_TN = ((0,), (0,))


def _dg(a, b, dims):
    return lax.dot_general(a.astype(BF16), b.astype(BF16), (dims, ((), ())), preferred_element_type=F32)


@jax.custom_vjp
def _dot_nn(a, b):
    return _dg(a, b, _NN)


_dot_nn.defvjp(lambda a, b: (_dg(a, b, _NN), (a, b)),
               lambda r, g: (_dg(g, r[1], _NT).astype(r[0].dtype), _dg(r[0], g, _TN).astype(r[1].dtype)))


@jax.custom_vjp
def _dot_nt(a, b):
    return _dg(a, b, _NT)


_dot_nt.defvjp(lambda a, b: (_dg(a, b, _NT), (a, b)),
               lambda r, g: (_dg(g, r[1], _NN).astype(r[0].dtype), _dg(g, r[0], _TN).astype(r[1].dtype)))


@jax.custom_vjp
def _dot_tn(a, b):
    return _dg(a, b, _TN)


_dot_tn.defvjp(lambda a, b: (_dg(a, b, _TN), (a, b)),
               lambda r, g: (_dg(r[1], g, _NT).astype(r[0].dtype), _dg(r[0], g, _NN).astype(r[1].dtype)))


def _rms(x, w, n=None):
    n = x.shape[-1] if n is None else n
    ms = jnp.sum(x * x, axis=-1, keepdims=True) * (1.0 / n)
    return x * lax.rsqrt(ms + EPS) * w


def _silu(x):
    return x * jax.nn.sigmoid(x)


def _log_sigmoid(x):
    return jnp.minimum(x, 0.0) - jnp.log(1.0 + jnp.exp(-jnp.abs(x)))


@jax.custom_vjp
def _rope(y, c, sa, sb):
    return y * c + pltpu.roll(y, LANES - 16, 1) * sa + pltpu.roll(y, 16, 1) * sb


def _rope_bwd(res, g):
    c, sa, sb = res
    gy = g * c + pltpu.roll(g * sa, 16, 1) + pltpu.roll(g * sb, LANES - 16, 1)
    return gy, jnp.zeros_like(c), jnp.zeros_like(sa), jnp.zeros_like(sb)


_rope.defvjp(lambda y, c, sa, sb: (_rope(y, c, sa, sb), (c, sa, sb)), _rope_bwd)


def _lane_mask(lo, hi):
    lane = lax.broadcasted_iota(jnp.int32, (1, LANES), 1)
    return ((lane >= lo) & (lane < hi)).astype(F32)


def _tile(n, t):
    t = min(n, t)
    assert n % t == 0, (n, t)
    return t


def _matmul(a, b, mode, out_dtype, name, residual=None, a_lead=None, b_lead=None):
    (a0, a1), (b0, b1) = a.shape[-2:], b.shape[-2:]
    if mode == "nn":
        m, k, k2, n = a0, a1, b0, b1
    elif mode == "nt":
        m, k, n, k2 = a0, a1, b0, b1
    else:
        k, m, k2, n = a0, a1, b0, b1
    assert k == k2, (a.shape, b.shape, mode)
    npar = 4 if "p" in (a_lead, b_lead) else 1
    nsum = 4 if "k" in (a_lead, b_lead) else 1
    a_item, b_item, o_item = a.dtype.itemsize, b.dtype.itemsize, jnp.dtype(out_dtype).itemsize

    def vmem_need(tm, tn, tk):
        need = 2 * (nsum if a_lead == "k" else 1) * tm * tk * a_item + 2 * (nsum if b_lead == "k" else 1) * tk * tn * b_item
        need += 2 * tm * tn * o_item + tm * tn * 4 * (2 if tk < k else 1)
        need += tm * tk * 2 * (a_item == 4 or mode == "tn") + tk * tn * 2 * (b_item == 4)
        return need + (2 * tm * tn * 4 if residual is not None else 0)

    halvings = (4096, 2048, 1024, 512, 256, 128, 64, 32, 16, 8)
    if mode == "tn":
        tm = m if m <= 1408 else m // 2
        tn = n if tm * n <= 1024 * 2304 else n // 2
        tk = next((r for r in halvings if k % r == 0 and vmem_need(tm, tn, r) <= MATMUL_VMEM), k)
    else:
        tn, tk = n, k
        tm = next((r for r in halvings if m % r == 0 and vmem_need(r, tn, tk) <= MATMUL_VMEM), m)
    assert m % tm == 0 and n % tn == 0 and k % tk == 0
    nk = k // tk
    dims = {"nn": _NN, "nt": _NT, "tn": _TN}[mode]

    def body(*refs):
        a_ref, b_ref = refs[0], refs[1]
        r_ref = refs[2] if residual is not None else None
        o_ref = refs[3 if residual is not None else 2]
        prod = None
        for sh in range(nsum):
            term = _dg(a_ref[sh] if a_lead == "k" else a_ref[...], b_ref[sh] if b_lead == "k" else b_ref[...], dims)
            prod = term if prod is None else prod + term
        if nk == 1:
            o_ref[...] = (prod if r_ref is None else prod + r_ref[...]).astype(o_ref.dtype)
            return
        acc = refs[-1]
        kk = pl.program_id(3)

        @pl.when(kk == 0)
        def _():
            acc[...] = prod

        @pl.when(kk > 0)
        def _():
            acc[...] += prod

        @pl.when(kk == nk - 1)
        def _():
            r = acc[...]
            if r_ref is not None:
                r = r + r_ref[...]
            o_ref[...] = r.astype(o_ref.dtype)

    def spec(lead, blk, idx):
        if lead is None:
            return pl.BlockSpec(blk, lambda p, i, j, kk: idx(i, j, kk))
        if lead == "p":
            return pl.BlockSpec((None,) + blk, lambda p, i, j, kk: (p,) + idx(i, j, kk))
        return pl.BlockSpec((nsum,) + blk, lambda p, i, j, kk: (0,) + idx(i, j, kk))

    if mode == "nn":
        in_specs = [spec(a_lead, (tm, tk), lambda i, j, kk: (i, kk)), spec(b_lead, (tk, tn), lambda i, j, kk: (kk, j))]
    elif mode == "nt":
        in_specs = [spec(a_lead, (tm, tk), lambda i, j, kk: (i, kk)), spec(b_lead, (tn, tk), lambda i, j, kk: (j, kk))]
    else:
        in_specs = [spec(a_lead, (tk, tm), lambda i, j, kk: (kk, i)), spec(b_lead, (tk, tn), lambda i, j, kk: (kk, j))]
    args = [a, b]
    if residual is not None:
        assert npar == 1
        in_specs.append(spec(None, (tm, tn), lambda i, j, kk: (i, j)))
        args.append(residual)
    return pl.pallas_call(
        body, grid=(npar, m // tm, n // tn, nk), in_specs=in_specs,
        out_specs=spec("p" if npar > 1 else None, (tm, tn), lambda i, j, kk: (i, j)),
        out_shape=jax.ShapeDtypeStruct(((4,) if npar > 1 else ()) + (m, n), out_dtype),
        scratch_shapes=[pltpu.VMEM((tm, tn), F32)] if nk > 1 else [],
        compiler_params=pltpu.CompilerParams(dimension_semantics=("parallel", "parallel", "parallel", "arbitrary"),
                                             vmem_limit_bytes=VMEM_LIMIT),
        name=name)(*args)


def _row(a, width=None, col_block=0):
    return (a, a.shape[1] if width is None else width, col_block)


def _rows_call(body, rows, consts, outs, accs=(), *, name, tile=512):
    s = rows[0][0].shape[0]
    t = _tile(s, tile)
    nr, nc, no = len(rows), len(consts), len(outs)

    def kern(*refs):
        r = [x[...] for x in refs[:nr]]
        c = [x[...] for x in refs[nr:nr + nc]]
        o_refs = refs[nr + nc:nr + nc + no]
        a_refs = refs[nr + nc + no:]
        ro, ao = body(r, c)
        for ref, val in zip(o_refs, ro, strict=True):
            ref[...] = val.astype(ref.dtype)
        if a_refs:
            @pl.when(pl.program_id(0) == 0)
            def _():
                for ref in a_refs:
                    ref[...] = jnp.zeros_like(ref)

            for ref, val in zip(a_refs, ao, strict=True):
                ref[...] += val

    in_specs = [pl.BlockSpec((t, w), functools.partial(lambda cb, i: (i, cb), cb)) for (_, w, cb) in rows]
    in_specs += [pl.BlockSpec(c.shape, lambda i: (0, 0)) for c in consts]
    out_specs = [pl.BlockSpec((t, w), lambda i: (i, 0)) for (w, _) in outs]
    out_specs += [pl.BlockSpec(shape, lambda i: (0, 0)) for shape in accs]
    out_shape = [jax.ShapeDtypeStruct((s, w), dt) for (w, dt) in outs]
    out_shape += [jax.ShapeDtypeStruct(shape, F32) for shape in accs]
    return pl.pallas_call(
        kern, grid=(s // t,), in_specs=in_specs, out_specs=out_specs, out_shape=out_shape,
        compiler_params=pltpu.CompilerParams(dimension_semantics=("arbitrary" if accs else "parallel",),
                                             vmem_limit_bytes=VMEM_LIMIT),
        name=name)(*[r[0] for r in rows], *consts)


def _gla_chunk(q, k, la, v0, v1, s0, s1):
    c = q.shape[0]
    r = lax.broadcasted_iota(jnp.int32, (c, c), 0)
    cc = lax.broadcasted_iota(jnp.int32, (c, c), 1)
    tril = cc <= r
    cum = lax.dot_general(tril.astype(F32), la, (_NN, ((), ())), precision=lax.Precision.HIGHEST,
                          preferred_element_type=F32)
    cl = jnp.sum(la, axis=0, keepdims=True)
    qd = q * (GLA_DK ** -0.5) * jnp.exp(cum)
    ki = k * jnp.exp(-cum)
    ke = k * jnp.exp(cl - cum)
    dec = jnp.exp(cl)
    outs, news = [], []
    for h, (v, s) in enumerate(((v0, s0), (v1, s1))):
        mk = _lane_mask(GLA_DK * h, GLA_DK * (h + 1))
        qh = qd * mk
        att = jnp.where(tril, _dot_nt(qh, ki), 0.0)
        outs.append(_dot_nn(att, v) + _dot_nt(qh, s))
        news.append(s * dec + _dot_tn(v, ke * mk))
    return outs[0], outs[1], news[0], news[1]


def _gla_specs(tb, rev_nb=None):
    blk = (lambda b: b) if rev_nb is None else (lambda b: rev_nb - 1 - b)
    q = pl.BlockSpec((tb, 128), lambda p, b: (blk(b), P_GQ // 128 + p))
    k = pl.BlockSpec((tb, 128), lambda p, b: (blk(b), P_GK // 128 + p))
    la = pl.BlockSpec((tb, 128), lambda p, b: (blk(b), p))
    v = pl.BlockSpec((tb, 256), lambda p, b: (blk(b), P_GV // 256 + p))
    o = pl.BlockSpec((tb, 256), lambda p, b: (blk(b), p))
    st = pl.BlockSpec((tb // GLA_CHUNK, 2, 128, 128), lambda p, b: (blk(b), p, 0, 0))
    return q, k, la, v, o, st


def _gla_fwd(proj, la):
    s = proj.shape[0]
    tb = _tile(s, 512)
    nb, nch = s // tb, tb // GLA_CHUNK

    def kern(q_ref, k_ref, la_ref, v_ref, o_ref, st_ref, s_sc):
        @pl.when(pl.program_id(1) == 0)
        def _():
            s_sc[...] = jnp.zeros_like(s_sc)

        s0, s1 = s_sc[0], s_sc[1]
        for ci in range(nch):
            sl = slice(ci * GLA_CHUNK, (ci + 1) * GLA_CHUNK)
            st_ref[ci, 0] = s0
            st_ref[ci, 1] = s1
            o0, o1, s0, s1 = _gla_chunk(q_ref[sl, :], k_ref[sl, :], la_ref[sl, :], v_ref[sl, 0:128],
                                        v_ref[sl, 128:256], s0, s1)
            o_ref[sl, 0:128] = o0
            o_ref[sl, 128:256] = o1
        s_sc[0] = s0
        s_sc[1] = s1

    q, k, lasp, v, o, st = _gla_specs(tb)
    return pl.pallas_call(
        kern, grid=(2, nb), in_specs=[q, k, lasp, v], out_specs=[o, st],
        out_shape=[jax.ShapeDtypeStruct((s, 512), F32),
                   jax.ShapeDtypeStruct((s // GLA_CHUNK, GLA_HEADS, 128, 128), F32)],
        scratch_shapes=[pltpu.VMEM((2, 128, 128), F32)],
        compiler_params=pltpu.CompilerParams(dimension_semantics=("parallel", "arbitrary"),
                                             vmem_limit_bytes=VMEM_LIMIT),
        name="gla_fwd")(proj, proj, la, proj)


def _gla_bwd(proj, la, states, d_o, comm):
    s = proj.shape[0]
    tb = _tile(s, 512)
    nb, nch = s // tb, tb // GLA_CHUNK
    nci, nco = len(comm.ins), len(comm.out_shape)

    def kern(*refs):
        (q_ref, k_ref, la_ref, v_ref, do_ref, st_ref), cins, (dq_ref, dk_ref, dla_ref, dv_ref), couts, (ds_sc,), csems = \
            _split_refs(refs, (6, nci, 4, nco, 1, len(comm.sems)))
        place = _place()
        pair, blk = pl.program_id(0), pl.program_id(1)

        @pl.when((pair == 0) & (blk == 0))
        def _():
            comm.start(place, cins, couts, csems)

        @pl.when((pair == 1) & (blk == nb // 2))
        def _():
            comm.mid(place, cins, couts, csems)

        @pl.when(blk == 0)
        def _():
            ds_sc[...] = jnp.zeros_like(ds_sc)

        d0, d1 = ds_sc[0], ds_sc[1]
        for ci in reversed(range(nch)):
            sl = slice(ci * GLA_CHUNK, (ci + 1) * GLA_CHUNK)
            _, vjp = jax.vjp(_gla_chunk, q_ref[sl, :], k_ref[sl, :], la_ref[sl, :], v_ref[sl, 0:128],
                             v_ref[sl, 128:256], st_ref[ci, 0], st_ref[ci, 1])
            gq, gk, gla, gv0, gv1, d0, d1 = vjp((do_ref[sl, 0:128], do_ref[sl, 128:256], d0, d1))
            dq_ref[sl, :] = gq
            dk_ref[sl, :] = gk
            dla_ref[sl, :] = gla
            dv_ref[sl, 0:128] = gv0
            dv_ref[sl, 128:256] = gv1
        ds_sc[0] = d0
        ds_sc[1] = d1

        @pl.when((pair == 1) & (blk == nb - 1))
        def _():
            comm.finish(place, cins, couts, csems)

    q, k, lasp, v, o, st = _gla_specs(tb, rev_nb=nb)
    res = pl.pallas_call(
        kern, grid=(2, nb), in_specs=[q, k, lasp, v, o, st] + [ANY] * nci, out_specs=[lasp, lasp, lasp, o] + [ANY] * nco,
        out_shape=[jax.ShapeDtypeStruct((s, 256), F32), jax.ShapeDtypeStruct((s, 256), F32),
                   jax.ShapeDtypeStruct((s, 256), F32), jax.ShapeDtypeStruct((s, 512), F32)] + comm.out_shape,
        scratch_shapes=[pltpu.VMEM((2, 128, 128), F32)] + comm.sems,
        compiler_params=pltpu.CompilerParams(dimension_semantics=("arbitrary", "arbitrary"),
                                             vmem_limit_bytes=VMEM_LIMIT),
        name="gla_bwd")(proj, proj, la, proj, d_o, states, *comm.ins)
    return res[0], res[1], res[2], res[3], res[4:]


def _causal_keep(t, qi, ki):
    row = lax.broadcasted_iota(jnp.int32, (t, t), 0) + qi * t
    col = lax.broadcasted_iota(jnp.int32, (t, t), 1) + ki * t
    return col <= row


def _split_refs(refs, counts):
    out, off = [], 0
    for cnt in counts:
        out.append(refs[off:off + cnt])
        off += cnt
    return out


def _attn_fwd(q, k, v, comm, tile=1024):
    s = q.shape[0]
    t = _tile(s, tile)
    n = s // t
    nci, nco = len(comm.ins), len(comm.out_shape)

    def kern(*refs):
        (q_ref, k_ref, v_ref), cins, (o_ref, lse_ref), couts, (m_sc, l_sc, acc_sc), csems = _split_refs(
            refs, (3, nci, 2, nco, 3, len(comm.sems)))
        qi, ki = pl.program_id(1), pl.program_id(2)
        place = _place()

        @pl.when((pl.program_id(0) == 0) & (qi == 0) & (ki == 0))
        def _():
            comm.start(place, cins, couts, csems)

        @pl.when((pl.program_id(0) == MLA_HEADS // 2 - 1) & (qi == 0) & (ki == 0))
        def _():
            comm.mid(place, cins, couts, csems)

        first = lax.broadcasted_iota(jnp.int32, (t, LANES), 1) < MLA_V

        @pl.when(ki == 0)
        def _():
            m_sc[...] = jnp.full_like(m_sc, -jnp.inf)
            l_sc[...] = jnp.zeros_like(l_sc)
            acc_sc[...] = jnp.zeros_like(acc_sc)

        def update(diagonal):
            keep = _causal_keep(t, 0, 0)
            alphas, pvs = [], []
            for h in range(2):
                sc = _dg(q_ref[:, 128 * h:128 * (h + 1)], k_ref[:, 128 * h:128 * (h + 1)], _NT)
                if diagonal:
                    sc = jnp.where(keep, sc, -jnp.inf)
                m_prev = m_sc[h]
                m_new = jnp.maximum(m_prev, jnp.max(sc, axis=1, keepdims=True))
                alpha = jnp.exp2(m_prev - m_new)
                p = jnp.exp2(sc - m_new[:, 0:1])
                l_sc[h] = alpha * l_sc[h] + jnp.sum(p, axis=1, keepdims=True)
                m_sc[h] = m_new
                alphas.append(alpha)
                pvs.append(_dg(p, v_ref[...], _NN))
            acc_sc[...] = acc_sc[...] * jnp.where(first, alphas[0], alphas[1]) + jnp.where(first, pvs[0], pvs[1])

        @pl.when(ki < qi)
        def _():
            update(False)

        @pl.when(ki == qi)
        def _():
            update(True)

        @pl.when(ki == qi)
        def _():
            l = jnp.where(first, l_sc[0], l_sc[1])
            m = jnp.where(first, m_sc[0], m_sc[1])
            o_ref[...] = acc_sc[...] / l
            lse_ref[...] = m + jnp.log2(l)

        @pl.when((pl.program_id(0) == MLA_HEADS // 2 - 1) & (qi == n - 1) & (ki == n - 1))
        def _():
            comm.finish(place, cins, couts, csems)

    kv_idx = lambda p, qi, ki: (jnp.minimum(ki, qi), p)
    res = pl.pallas_call(
        kern, grid=(MLA_HEADS // 2, n, n),
        in_specs=[pl.BlockSpec((t, 256), lambda p, qi, ki: (qi, p)), pl.BlockSpec((t, 256), kv_idx),
                  pl.BlockSpec((t, 128), kv_idx)] + [ANY] * nci,
        out_specs=[pl.BlockSpec((t, 128), lambda p, qi, ki: (qi, p)), pl.BlockSpec((t, 128), lambda p, qi, ki: (qi, p))]
        + [ANY] * nco,
        out_shape=[jax.ShapeDtypeStruct((s, 512), F32), jax.ShapeDtypeStruct((s, 512), F32)] + comm.out_shape,
        scratch_shapes=[pltpu.VMEM((2, t, LANES), F32), pltpu.VMEM((2, t, LANES), F32), pltpu.VMEM((t, LANES), F32)]
        + comm.sems,
        compiler_params=pltpu.CompilerParams(dimension_semantics=("arbitrary", "arbitrary", "arbitrary"),
                                             vmem_limit_bytes=VMEM_LIMIT),
        name="mla_attn_fwd")(q, k, v, *comm.ins)
    return res[0], res[1], res[2:]


def _attn_bwd(q, k, v, o, lse, dcat, comm, tile=512):
    s = q.shape[0]
    t = _tile(s, tile)
    n = s // t
    nci, nco = len(comm.ins), len(comm.out_shape)

    def kern(*refs):
        (q_ref, k_ref, v_ref, o_ref, lse_ref, do_ref), cins, (dq_ref, dk_ref, dv_ref), couts, (dk_sc, dv_sc), csems = \
            _split_refs(refs, (6, nci, 3, nco, 2, len(comm.sems)))
        ki, qi = pl.program_id(1), pl.program_id(2)
        place = _place()

        @pl.when((pl.program_id(0) == 0) & (qi == 0) & (ki == 0))
        def _():
            comm.start(place, cins, couts, csems)

        @pl.when((pl.program_id(0) == MLA_HEADS // 2 - 1) & (qi == 0) & (ki == 0))
        def _():
            comm.mid(place, cins, couts, csems)

        @pl.when((ki == 0) & (qi == 0))
        def _():
            dq_ref[...] = jnp.zeros_like(dq_ref)

        @pl.when(qi == ki)
        def _():
            dk_sc[...] = jnp.zeros_like(dk_sc)
            dv_sc[...] = jnp.zeros_like(dv_sc)

        def update(diagonal):
            keep = _causal_keep(t, 0, 0)
            d_o = do_ref[...]
            prod = d_o * o_ref[...]
            rows = pl.ds(pl.multiple_of(qi * t, t), t)
            for h in range(2):
                hs = slice(128 * h, 128 * (h + 1))
                mk = _lane_mask(MLA_V * h, MLA_V * (h + 1))
                qh, kh = q_ref[:, hs], k_ref[:, hs]
                sc = _dg(qh, kh, _NT)
                if diagonal:
                    sc = jnp.where(keep, sc, -jnp.inf)
                p = jnp.exp2(sc - lse_ref[:, MLA_V * h:MLA_V * h + 1])
                doh = d_o * mk
                dp = _dg(doh * LN2, v_ref[...], _NT)
                delta = jnp.sum(prod * mk, axis=1, keepdims=True) * LN2
                ds = p * (dp - delta)
                dv_sc[...] += _dg(p, doh, _TN)
                dk_sc[:, hs] += _dg(ds, qh, _TN)
                dq_ref[rows, hs] += _dg(ds, kh, _NN)

        @pl.when(qi > ki)
        def _():
            update(False)

        @pl.when(qi == ki)
        def _():
            update(True)

        @pl.when(qi == n - 1)
        def _():
            dk_ref[...] = dk_sc[...]
            dv_ref[...] = dv_sc[...].astype(dv_ref.dtype)

        @pl.when((pl.program_id(0) == MLA_HEADS // 2 - 1) & (qi == n - 1) & (ki == n - 1))
        def _():
            comm.finish(place, cins, couts, csems)

    q_idx = lambda p, ki, qi: (jnp.maximum(qi, ki), p)
    res = pl.pallas_call(
        kern, grid=(MLA_HEADS // 2, n, n),
        in_specs=[pl.BlockSpec((t, 256), q_idx), pl.BlockSpec((t, 256), lambda p, ki, qi: (ki, p)),
                  pl.BlockSpec((t, 128), lambda p, ki, qi: (ki, p)), pl.BlockSpec((t, 128), q_idx),
                  pl.BlockSpec((t, 128), q_idx),
                  pl.BlockSpec((t, 128), lambda p, ki, qi: (jnp.maximum(qi, ki), 4 + p))] + [ANY] * nci,
        out_specs=[pl.BlockSpec((s, 256), lambda p, ki, qi: (0, p)), pl.BlockSpec((t, 256), lambda p, ki, qi: (ki, p)),
                   pl.BlockSpec((t, 128), lambda p, ki, qi: (ki, p))] + [ANY] * nco,
        out_shape=[jax.ShapeDtypeStruct((s, 1024), F32), jax.ShapeDtypeStruct((s, 1024), F32),
                   jax.ShapeDtypeStruct((s, 512), BF16)] + comm.out_shape,
        scratch_shapes=[pltpu.VMEM((t, 256), F32), pltpu.VMEM((t, 128), F32)] + comm.sems,
        compiler_params=pltpu.CompilerParams(dimension_semantics=("arbitrary", "arbitrary", "arbitrary"),
                                             vmem_limit_bytes=VMEM_LIMIT),
        name="mla_attn_bwd")(q, k, v, o, lse, dcat, *comm.ins)
    return res[0], res[1], res[2], res[3:]


def _gate_fn(alr, w2, b):
    return _log_sigmoid(_dot_nn(alr, w2) + b) * (1.0 / GLA_GATE_NORM)


def _qk_head(qh, kh, kpe, c, sa, sb, qn, kn):
    kfull = kh + kpe * _lane_mask(MLA_NOPE, MLA_QK)
    q_r = _rope(_rms(qh, qn, MLA_QK), c, sa, sb) * (MLA_QK ** -0.5 * LOG2E)
    k_r = _rope(_rms(kfull, kn, MLA_QK), c, sa, sb)
    return q_r, k_r


def _mix_head(o, og, gn):
    return _rms(o, gn) * _silu(og)


def _xa_head(xq, xk, xv, qn, kn):
    sc = _dot_nt(_rms(xq, qn), _rms(xk, kn)) * (XA_DIM ** -0.5)
    e = jnp.exp(sc - lax.stop_gradient(jnp.max(sc, axis=1, keepdims=True)))
    p = e / jnp.sum(e, axis=1, keepdims=True)
    return _dot_nn(p, xv)


def _heads(x, n):
    return [x[:, 128 * h:128 * (h + 1)] for h in range(n)]


def _cat(xs):
    return jnp.concatenate(xs, axis=1)


def _norm_fwd(x, w, name):
    return _rows_call(lambda r, c: ([_rms(r[0], c[0])], []), [_row(x)], [w], [(x.shape[1], BF16)], name=name)[0]


def _norm_bwd(x, w, d_out, add, name):
    def body(r, c):
        _, vjp = jax.vjp(_rms, r[0], c[0])
        dx, dw = vjp(r[1])
        return [dx + r[2]], [dw]

    return _rows_call(body, [_row(x), _row(d_out), _row(add)], [w], [(x.shape[1], F32)], [w.shape], name=name)


CONV_HALO = 8


def _conv_specs(s, f, t):
    n8 = t // CONV_HALO
    cur = pl.BlockSpec((None, t, f), lambda j, i: (j, i, 0))
    prev = pl.BlockSpec((None, CONV_HALO, f), lambda j, i: (j, jnp.maximum(i * n8 - 1, 0), 0))
    nxt = pl.BlockSpec((None, CONV_HALO, f), lambda j, i: (j, jnp.minimum((i + 1) * n8, s // CONV_HALO - 1), 0))
    cw = pl.BlockSpec((None, 3, f), lambda j, i: (j, 0, 0))
    cb = pl.BlockSpec((None, 1, f), lambda j, i: (j, 0, 0))
    return cur, prev, nxt, cw, cb


def _conv_taps(g, prev, first):
    ext = jnp.concatenate([jnp.where(first, 0.0, prev), g], axis=0)
    return pltpu.roll(ext, 1, 0)[CONV_HALO:], pltpu.roll(ext, 2, 0)[CONV_HALO:]


def _conv_fwd(gg, uu, cw, cb):
    _, s, f = gg.shape
    t = _tile(s, 512)

    def kern(g_ref, gp_ref, u_ref, cw_ref, cb_ref, o_ref):
        g = g_ref[...]
        g1, g2 = _conv_taps(g, gp_ref[...], pl.program_id(1) == 0)
        w = cw_ref[...]
        gc = cb_ref[...] + w[0:1] * g2 + w[1:2] * g1 + w[2:3] * g
        o_ref[...] = (_silu(gc) * u_ref[...]).astype(o_ref.dtype)

    cur, prev, _, cws, cbs = _conv_specs(s, f, t)
    return pl.pallas_call(
        kern, grid=(4, s // t), in_specs=[cur, prev, cur, cws, cbs], out_specs=cur,
        out_shape=jax.ShapeDtypeStruct(gg.shape, BF16),
        compiler_params=pltpu.CompilerParams(dimension_semantics=("parallel", "parallel"), vmem_limit_bytes=VMEM_LIMIT),
        name="ffn_conv_fwd")(gg, gg, uu, cw, cb)


def _conv_bwd_gate(gg, uu, dact, cw, cb):
    _, s, f = gg.shape
    t = _tile(s, 512)

    def kern(g_ref, gp_ref, u_ref, da_ref, cw_ref, cb_ref, du_ref, dgc_ref, dcw_ref, dcb_ref):
        i = pl.program_id(1)
        g, u, da = g_ref[...], u_ref[...], da_ref[...]
        g1, g2 = _conv_taps(g, gp_ref[...], i == 0)
        w = cw_ref[...]
        gc = cb_ref[...] + w[0:1] * g2 + w[1:2] * g1 + w[2:3] * g
        sg = jax.nn.sigmoid(gc)
        du_ref[...] = (da * (gc * sg)).astype(du_ref.dtype)
        dgc = da * u * (sg * (1.0 + gc * (1.0 - sg)))
        dgc_ref[...] = dgc

        @pl.when(i == 0)
        def _():
            dcw_ref[...] = jnp.zeros_like(dcw_ref)
            dcb_ref[...] = jnp.zeros_like(dcb_ref)

        dcw_ref[0:1, :] += jnp.sum(dgc * g2, axis=0, keepdims=True)
        dcw_ref[1:2, :] += jnp.sum(dgc * g1, axis=0, keepdims=True)
        dcw_ref[2:3, :] += jnp.sum(dgc * g, axis=0, keepdims=True)
        dcb_ref[...] += jnp.sum(dgc, axis=0, keepdims=True)

    cur, prev, _, cws, cbs = _conv_specs(s, f, t)
    return pl.pallas_call(
        kern, grid=(4, s // t), in_specs=[cur, prev, cur, cur, cws, cbs], out_specs=[cur, cur, cws, cbs],
        out_shape=[jax.ShapeDtypeStruct(gg.shape, BF16), jax.ShapeDtypeStruct(gg.shape, F32),
                   jax.ShapeDtypeStruct(cw.shape, F32), jax.ShapeDtypeStruct(cb.shape, F32)],
        compiler_params=pltpu.CompilerParams(dimension_semantics=("parallel", "arbitrary"), vmem_limit_bytes=VMEM_LIMIT),
        name="ffn_conv_bwd_gate")(gg, gg, uu, dact, cw, cb)


def _conv_bwd_taps(dgc, cw):
    _, s, f = dgc.shape
    t = _tile(s, 512)
    nt = s // t

    def kern(d_ref, dn_ref, cw_ref, o_ref):
        d = d_ref[...]
        ext = jnp.concatenate([d, jnp.where(pl.program_id(1) == nt - 1, 0.0, dn_ref[...])], axis=0)
        up1 = pltpu.roll(ext, t + CONV_HALO - 1, 0)[:t]
        up2 = pltpu.roll(ext, t + CONV_HALO - 2, 0)[:t]
        w = cw_ref[...]
        o_ref[...] = (w[2:3] * d + w[1:2] * up1 + w[0:1] * up2).astype(o_ref.dtype)

    cur, _, nxt, cws, _ = _conv_specs(s, f, t)
    return pl.pallas_call(
        kern, grid=(4, nt), in_specs=[cur, nxt, cws], out_specs=cur, out_shape=jax.ShapeDtypeStruct(dgc.shape, BF16),
        compiler_params=pltpu.CompilerParams(dimension_semantics=("parallel", "parallel"), vmem_limit_bytes=VMEM_LIMIT),
        name="ffn_conv_bwd_taps")(dgc, dgc, cw)


def _rope_tables(pos):
    half = MLA_ROPE // 2
    inv = ROPE_THETA ** (-jnp.arange(half, dtype=F32) / half)
    ang = pos.astype(F32)[:, None] * inv
    cos, sin = jnp.cos(ang), jnp.sin(ang)
    s = pos.shape[0]
    z = lambda w: jnp.zeros((s, w), F32)
    c = jnp.concatenate([jnp.ones((s, MLA_NOPE), F32), cos, cos, jnp.ones((s, LANES - MLA_QK), F32)], axis=1)
    sa = jnp.concatenate([z(MLA_NOPE), -sin, z(half), z(LANES - MLA_QK)], axis=1)
    sb = jnp.concatenate([z(MLA_NOPE), z(half), sin, z(LANES - MLA_QK)], axis=1)
    return c, sa, sb


def _local_step(x, mem, pos, target, w, late_shards):
    g = {}
    w = dict(w)
    c, sa, sb = _rope_tables(pos)

    xn = _norm_fwd(x, w["norm_mix"], "norm_mix_fwd")
    proj = _matmul(xn, w["in"], "nn", F32, "proj_fwd")
    alr = _row(proj, 128, P_ALR // 128)
    kpe = _row(proj, 128, P_KPE // 128)
    og = _row(proj, 512, P_OG // 512)
    cq = _row(proj, 256, P_CQ // 256)
    ckv = _row(proj, 128, P_CKV // 128)

    la = _rows_call(lambda r, k: ([_gate_fn(r[0], k[0], k[1])], []), [alr], [w["w2"], w["gate_b"]],
                    [(256, F32)], name="gla_gate_fwd")[0]
    o_gla, states = _gla_fwd(proj, la)

    q_lat, kv_lat = _rows_call(lambda r, k: ([_rms(r[0], k[0]), _rms(r[1], k[1])], []), [cq, ckv],
                               [w["q_a_norm"], w["kv_a_norm"]], [(256, BF16), (128, BF16)], name="mla_lat_fwd")
    q_up = _matmul(q_lat, w["uq"], "nn", F32, "mla_q_fwd")
    k_up = _matmul(kv_lat, w["k"], "nn", F32, "mla_k_fwd")
    v_mla = _matmul(kv_lat, w["v"], "nn", BF16, "mla_v_fwd")

    def qk_body(r, k):
        qs, ks = [], []
        for qh, kh in zip(_heads(r[0], MLA_HEADS), _heads(r[1], MLA_HEADS)):
            a, b = _qk_head(qh, kh, r[2], r[3], r[4], r[5], k[0], k[1])
            qs.append(a)
            ks.append(b)
        return [_cat(qs), _cat(ks)], []

    tabs = [_row(c), _row(sa), _row(sb)]
    q_r, k_r = _rows_call(qk_body, [_row(q_up), _row(k_up), kpe] + tabs, [w["q_norm"], w["k_norm"]],
                          [(1024, BF16), (1024, BF16)], name="mla_qk_fwd")
    o_mla, lse, gathered = _attn_fwd(q_r, k_r, v_mla, _gather_plan(late_shards))
    w.update(_late_layout(dict(zip(LATE, gathered, strict=True))))

    def mix_body(r, k):
        ys = [_mix_head(o, g_, k[0]) for o, g_ in zip(_heads(r[0], GLA_HEADS), _heads(r[1], GLA_HEADS))]
        return [_cat(ys + [r[2]])], []

    cat = _rows_call(mix_body, [_row(o_gla), og, _row(o_mla)], [w["gla_out_norm"]], [(1024, BF16)],
                     name="mix_fwd")[0]
    h1 = _matmul(cat, w["out"], "nn", F32, "out_fwd", residual=x)

    hn = _norm_fwd(h1, w["norm_xa"], "norm_xa_fwd")
    mn = _norm_fwd(mem, w["norm_mem"], "norm_mem_fwd")
    xq = _matmul(hn, w["xq"], "nn", F32, "xa_q_fwd")
    xkv = _matmul(mn, w["xkv"], "nn", F32, "xa_kv_fwd")

    def xa_body(r, k):
        ks, vs = _heads(k[0], 2 * XA_HEADS)[:XA_HEADS], _heads(k[0], 2 * XA_HEADS)[XA_HEADS:]
        return [_cat([_xa_head(a, b, v_, k[1], k[2]) for a, b, v_ in zip(_heads(r[0], XA_HEADS), ks, vs)])], []

    xo = _rows_call(xa_body, [_row(xq)], [xkv, w["xa_q_norm"], w["xa_k_norm"]], [(512, BF16)], name="xa_fwd")[0]
    h2 = _matmul(xo, w["xo"], "nn", F32, "xa_o_fwd", residual=h1)

    fn = _norm_fwd(h2, w["norm_ffn"], "norm_ffn_fwd")
    gg = _matmul(fn, w["wg"], "nt", F32, "ffn_gate_fwd", b_lead="p")
    uu = _matmul(fn, w["wu"], "nt", F32, "ffn_up_fwd", b_lead="p")
    act = _conv_fwd(gg, uu, w["cw"], w["cb"])
    y = _matmul(act, w["wd"], "nn", F32, "ffn_down_fwd", residual=h2, a_lead="k", b_lead="k")

    def loss_body(r, k):
        err = r[0] - r[1]
        part = 0.5 * jnp.sum(jnp.sum(err * err, axis=1, keepdims=True) * (1.0 / D_MODEL), axis=0, keepdims=True)
        return [err * (1.0 / D_MODEL)], [jnp.broadcast_to(part, (1, LANES))]

    dy, loss = _rows_call(loss_body, [_row(y), _row(target)], [], [(D_MODEL, F32)], [(1, LANES)], name="loss")

    g["ffn_w_down"] = _matmul(act, dy, "tn", BF16, "ffn_down_dw", a_lead="p")
    dact = _matmul(dy, w["wd"], "nt", F32, "ffn_down_dx", b_lead="p")
    duu, dgc, g["ffn_conv_w"], g["ffn_conv_b"] = _conv_bwd_gate(gg, uu, dact, w["cw"], w["cb"])
    dgg = _conv_bwd_taps(dgc, w["cw"])
    g["ffn_w_gate"] = _matmul(dgg, fn, "tn", BF16, "ffn_gate_dw", a_lead="p")
    g["ffn_w_up"] = _matmul(duu, fn, "tn", BF16, "ffn_up_dw", a_lead="p")
    dfn = _matmul(dgg, w["wg"], "nn", F32, "ffn_gate_dx", a_lead="k", b_lead="k")
    dfn = _matmul(duu, w["wu"], "nn", F32, "ffn_up_dx", residual=dfn, a_lead="k", b_lead="k")
    dh2, g["norm_ffn"] = _norm_bwd(h2, w["norm_ffn"], dfn, dy, "norm_ffn_bwd")

    g["xa_w_o"] = _matmul(xo, dh2, "tn", BF16, "xa_o_dw")
    dxo = _matmul(dh2, w["xo"], "nt", F32, "xa_o_dx")

    def xa_bwd(r, k):
        kvh = _heads(k[0], 2 * XA_HEADS)
        dq_, dk_, dv_ = [], [], []
        dqn, dkn = 0.0, 0.0
        for h, (a, d_) in enumerate(zip(_heads(r[0], XA_HEADS), _heads(r[1], XA_HEADS))):
            _, vjp = jax.vjp(_xa_head, a, kvh[h], kvh[XA_HEADS + h], k[1], k[2])
            ga, gk, gv, gqn, gkn = vjp(d_)
            dq_.append(ga)
            dk_.append(gk)
            dv_.append(gv)
            dqn, dkn = dqn + gqn, dkn + gkn
        return [_cat(dq_)], [_cat(dk_ + dv_), dqn, dkn]

    dxq, dxkv, g["xa_q_norm"], g["xa_k_norm"] = _rows_call(
        xa_bwd, [_row(xq), _row(dxo)], [xkv, w["xa_q_norm"], w["xa_k_norm"]], [(512, BF16)],
        [xkv.shape, (1, 128), (1, 128)], name="xa_bwd")
    g["xa_w_q"] = _matmul(hn, dxq, "tn", BF16, "xa_q_dw")
    dhn = _matmul(dxq, w["xq"], "nt", F32, "xa_q_dx")
    g["xa_w_kv"] = _matmul(mn, dxkv, "tn", BF16, "xa_kv_dw")
    dmn = _matmul(dxkv, w["xkv"], "nt", F32, "xa_kv_dx")
    _, g["norm_mem"] = _norm_bwd(mem, w["norm_mem"], dmn, dmn, "norm_mem_bwd")
    dh1, g["norm_xa"] = _norm_bwd(h1, w["norm_xa"], dhn, dh2, "norm_xa_bwd")

    g["w_out"] = _matmul(cat, dh1, "tn", BF16, "out_dw")
    dcat = _matmul(dh1, w["out"], "nt", F32, "out_dx")

    def mix_bwd(r, k):
        do_, dog_ = [], []
        dgn = 0.0
        for o, g_, d_ in zip(_heads(r[0], GLA_HEADS), _heads(r[1], GLA_HEADS), _heads(r[2], GLA_HEADS)):
            _, vjp = jax.vjp(_mix_head, o, g_, k[0])
            a, b, gn_ = vjp(d_)
            do_.append(a)
            dog_.append(b)
            dgn = dgn + gn_
        return [_cat(do_), _cat(dog_)], [dgn]

    do_gla, d_og, g["gla_out_norm"] = _rows_call(mix_bwd, [_row(o_gla), og, _row(dcat, 512, 0)], [w["gla_out_norm"]],
                                                 [(512, F32), (512, BF16)], [(1, 128)], name="mix_bwd")

    late_parts = _late_grad_shards(g)
    dq_r, dk_r, dv_mla, lands_mlp = _attn_bwd(q_r, k_r, v_mla, o_mla, lse, dcat,
                                              _scatter_plan([late_parts[n] for n in LATE_MLP]))

    def qk_bwd(r, k):
        dqs, dks = [], []
        dkpe, dqn, dkn = 0.0, 0.0, 0.0
        for qh, kh, dqh, dkh in zip(_heads(r[0], MLA_HEADS), _heads(r[1], MLA_HEADS), _heads(r[6], MLA_HEADS),
                                    _heads(r[7], MLA_HEADS)):
            _, vjp = jax.vjp(lambda a, b, e, f, h_: _qk_head(a, b, e, r[3], r[4], r[5], f, h_), qh, kh, r[2], k[0], k[1])
            ga, gb, ge, gf, gh = vjp((dqh, dkh))
            dqs.append(ga)
            dks.append(gb)
            dkpe, dqn, dkn = dkpe + ge, dqn + gf, dkn + gh
        return [_cat(dqs), _cat(dks), dkpe], [dqn, dkn]

    dq_up, dk_up, d_kpe, g["q_norm"], g["k_norm"] = _rows_call(
        qk_bwd, [_row(q_up), _row(k_up), kpe] + tabs + [_row(dq_r), _row(dk_r)], [w["q_norm"], w["k_norm"]],
        [(1024, BF16), (1024, BF16), (128, BF16)], [(1, 128), (1, 128)], name="mla_qk_bwd")
    g["uq"] = _matmul(q_lat, dq_up, "tn", BF16, "mla_q_dw")
    dq_lat = _matmul(dq_up, w["uq"], "nt", F32, "mla_q_dx")
    g["k"] = _matmul(kv_lat, dk_up, "tn", BF16, "mla_k_dw")
    g["v"] = _matmul(kv_lat, dv_mla, "tn", BF16, "mla_v_dw")
    dkv_lat = _matmul(dk_up, w["k"], "nt", F32, "mla_k_dx")
    dkv_lat = _matmul(dv_mla, w["v"], "nt", F32, "mla_v_dx", residual=dkv_lat)

    def lat_bwd(r, k):
        _, vjp1 = jax.vjp(_rms, r[0], k[0])
        _, vjp2 = jax.vjp(_rms, r[1], k[1])
        a, ga = vjp1(r[2])
        b, gb = vjp2(r[3])
        return [a, b], [ga, gb]

    d_cq, d_ckv, g["mla_q_a_norm"], g["mla_kv_a_norm"] = _rows_call(
        lat_bwd, [cq, ckv, _row(dq_lat), _row(dkv_lat)], [w["q_a_norm"], w["kv_a_norm"]],
        [(256, BF16), (128, BF16)], [(1, 256), (1, 128)], name="mla_lat_bwd")

    dgq, dgk, dla, dgv, lands_mix = _gla_bwd(proj, la, states, do_gla, _scatter_plan([late_parts[n] for n in LATE_MIX]))
    lands_late = dict(zip(LATE_MLP + LATE_MIX, list(lands_mlp) + list(lands_mix), strict=True))

    def gate_bwd(r, k):
        _, vjp = jax.vjp(_gate_fn, r[0], k[0], k[1])
        a, gw, gb = vjp(r[1])
        return [a], [gw, gb]

    d_alr, g["w2"], g["gla_gate_b"] = _rows_call(gate_bwd, [alr, _row(dla)], [w["w2"], w["gate_b"]], [(128, BF16)],
                                                 [(128, 256), (1, 256)], name="gla_gate_bwd")

    dproj = jnp.concatenate([dgq.astype(BF16), dgk.astype(BF16), dgv.astype(BF16), d_og, d_cq, d_ckv, d_kpe, d_alr],
                            axis=1)
    g["in"] = _matmul(xn, dproj, "tn", BF16, "proj_dw")
    dxn = _matmul(dproj, w["in"], "nt", F32, "proj_dx")
    dx, g["norm_mix"] = _norm_bwd(x, w["norm_mix"], dxn, dh1, "norm_mix_bwd")
    return loss[0, 0], dx, g, lands_late


def _join_shards(pieces, axis):
    if axis == 0:
        return pieces.reshape(-1, pieces.shape[2])
    return jnp.transpose(pieces, (1, 0, 2)).reshape(pieces.shape[1], -1)


def _split_shards(full, axis):
    r, c = full.shape
    if axis == 0:
        return full.reshape(4, r // 4, c)
    return jnp.transpose(full.reshape(r, 4, c // 4), (1, 0, 2))


def _early_layout(gath, rep):
    w_in = _join_shards(gath["w_in"], 1)
    z = lambda n: jnp.zeros((D_MODEL, n), w_in.dtype)
    seg = lambda lo, n: w_in[:, lo:lo + n]
    ukv = _join_shards(gath["mla_w_ukv"], 1).reshape(MLA_KV_RANK, MLA_HEADS, MLA_NOPE + MLA_V)
    w = {
        "in": jnp.concatenate([seg(N_GQ, 256), seg(N_GK, 256), seg(N_GV, 512), seg(N_OG, 512), seg(N_CQ, 256),
                               seg(N_CKV, 128), z(64), seg(N_KPE, 32), z(32), seg(N_ALR, 16), z(112)], axis=1),
        "uq": jnp.pad(_join_shards(gath["mla_w_uq"], 1).reshape(MLA_Q_RANK, MLA_HEADS, MLA_QK),
                      ((0, 0), (0, 0), (0, LANES - MLA_QK))).reshape(MLA_Q_RANK, MLA_HEADS * LANES),
        "k": jnp.pad(ukv[:, :, :MLA_NOPE], ((0, 0), (0, 0), (0, LANES - MLA_NOPE))).reshape(MLA_KV_RANK, -1),
        "v": ukv[:, :, MLA_NOPE:].reshape(MLA_KV_RANK, MLA_HEADS * MLA_V),
        "w2": jnp.pad(_join_shards(gath["gla_gate_w2"], 1), ((0, LANES - GLA_RANK), (0, 0))),
        "cb": rep["ffn_conv_b"].reshape(4, 1, D_FF // 4),
        "q_norm": jnp.pad(rep["mla_q_norm"], ((0, 0), (0, LANES - MLA_QK))),
        "k_norm": jnp.pad(rep["mla_k_norm"], ((0, 0), (0, LANES - MLA_QK))),
        "q_a_norm": rep["mla_q_a_norm"], "kv_a_norm": rep["mla_kv_a_norm"], "gate_b": rep["gla_gate_b"],
    }
    for n in ("norm_mix", "gla_out_norm", "norm_xa", "norm_mem", "xa_q_norm", "xa_k_norm", "norm_ffn"):
        w[n] = rep[n]
    return w


def _late_layout(gath):
    return {"out": _join_shards(gath["w_out"], 0), "xq": _join_shards(gath["xa_w_q"], 0),
            "xkv": _join_shards(gath["xa_w_kv"], 0), "xo": _join_shards(gath["xa_w_o"], 1),
            "wg": gath["ffn_w_gate"], "wu": gath["ffn_w_up"], "wd": gath["ffn_w_down"], "cw": gath["ffn_conv_w"]}


def _late_grad_shards(g):
    sh = {"w_out": _split_shards(g["w_out"], 0), "xa_w_q": _split_shards(g["xa_w_q"], 0),
          "xa_w_kv": _split_shards(g["xa_w_kv"], 0), "xa_w_o": _split_shards(g["xa_w_o"], 1),
          "ffn_w_gate": g["ffn_w_gate"], "ffn_w_up": g["ffn_w_up"], "ffn_conv_w": g["ffn_conv_w"],
          "ffn_w_down": g["ffn_w_down"]}
    return {n: v.astype(BF16) for n, v in sh.items()}


def _early_grad_shards(g):
    gi = g["in"]
    seg = lambda lo, n: gi[:, lo:lo + n]
    w_in = jnp.concatenate([seg(P_GQ, 256), seg(P_GK, 256), seg(P_GV, 512), seg(P_ALR, 16), seg(P_OG, 512),
                            seg(P_CQ, 256), seg(P_CKV, 128), seg(P_KPE + 64, 32)], axis=1)
    uq = g["uq"].reshape(MLA_Q_RANK, MLA_HEADS, LANES)[:, :, :MLA_QK].reshape(MLA_Q_RANK, -1)
    ukv = jnp.concatenate([g["k"].reshape(MLA_KV_RANK, MLA_HEADS, LANES)[:, :, :MLA_NOPE],
                           g["v"].reshape(MLA_KV_RANK, MLA_HEADS, MLA_V)], axis=2).reshape(MLA_KV_RANK, -1)
    sh = {"w_in": _split_shards(w_in, 1), "gla_gate_w2": _split_shards(g["w2"][:GLA_RANK], 1),
          "mla_w_uq": _split_shards(uq, 1), "mla_w_ukv": _split_shards(ukv, 1)}
    sh = {n: v.astype(BF16) for n, v in sh.items()}
    rep = {n: g[n] for n in REPLICATED if n in g}
    rep["mla_q_norm"] = g["q_norm"][:, :MLA_QK]
    rep["mla_k_norm"] = g["k_norm"][:, :MLA_QK]
    rep["ffn_conv_b"] = g["ffn_conv_b"].reshape(1, D_FF)
    return sh, rep


SMALL_SHAPE = (8, 1024)


def _pack_small(vectors):
    flat = jnp.concatenate(vectors, axis=1)
    return jnp.pad(flat, ((0, 0), (0, SMALL_SHAPE[0] * SMALL_SHAPE[1] - flat.shape[1]))).reshape(SMALL_SHAPE)


def _unpack_small(buf, widths):
    flat = buf.reshape(1, -1)
    out, off = [], 0
    for wd in widths:
        out.append(flat[:, off:off + wd])
        off += wd
    return out


ANY = pl.BlockSpec(memory_space=pl.ANY)


def _place():
    x, y, c = lax.axis_index("x"), lax.axis_index("y"), lax.axis_index("c")
    chips = [(1 - x, y), (x, 1 - y), (1 - x, 1 - y)]
    return x, y, c, chips


class _Comm:
    def __init__(self, ins, out_shape, sems, start, finish, mid=None):
        self.ins, self.out_shape, self.sems = list(ins), list(out_shape), list(sems)
        self.start, self.finish, self.mid = start, finish, mid or (lambda *args: None)


def _run_comm(plan, name):
    ni, no = len(plan.ins), len(plan.out_shape)

    def body(*refs):
        ins, outs, sems = refs[:ni], refs[ni:ni + no], refs[ni + no:]
        place = _place()
        plan.start(place, ins, outs, sems)
        plan.mid(place, ins, outs, sems)
        plan.finish(place, ins, outs, sems)

    return pl.pallas_call(body, in_specs=[ANY] * ni, out_specs=[ANY] * no, out_shape=plan.out_shape,
                          scratch_shapes=plan.sems, name=name)(*plan.ins)


def _gather_plan(shards):
    n = len(shards)
    split = [s.shape[0] % (2 * BF16_ROWS) == 0 for s in shards]

    def rows(ref, t, c):
        if not split[t]:
            return ref
        half = shards[t].shape[0] // 2
        return ref.at[pl.ds(pl.multiple_of(c * half, BF16_ROWS), half)]

    def remote(src, dst, ss, rs, to):
        return pltpu.make_async_remote_copy(src_ref=src, dst_ref=dst, send_sem=ss, recv_sem=rs, device_id=to,
                                            device_id_type=MESH)

    def first_wave(place, ins, outs, sems):
        x, y, c, chips = place
        ici_s, ici_r, _, _, local = sems
        me = 2 * x + y
        own = [pltpu.make_async_copy(ins[t], outs[t].at[me], local.at[t]) for t in range(n)]
        push = [remote(rows(ins[t], t, c), rows(outs[t].at[me], t, c), ici_s.at[3 * t + j], ici_r.at[3 * t + j], (px, py, c))
                for t in range(n) for j, (px, py) in enumerate(chips)]
        return own, push

    def second_wave(place, ins, outs, sems, last):
        x, y, c, chips = place
        ici_s, ici_r, d2d_s, d2d_r, local = sems
        sib = (x, y, 1 - c)
        out = []
        for t in range(n):
            for j, (px, py) in enumerate(chips):
                block = outs[t].at[2 * px + py]
                got = rows(block, t, c)
                if split[t]:
                    hand = remote(got, got, d2d_s.at[3 * t + j], d2d_r.at[3 * t + j], sib)
                    theirs = rows(block, t, 1 - c)
                    other = (remote(theirs, theirs, local.at[0], d2d_r.at[3 * t + j], sib) if last else
                             remote(got, got, local.at[0], ici_r.at[3 * t + j], sib))
                    out.append((other, hand))
                elif last:
                    out.append((remote(got, got, local.at[0], ici_r.at[3 * t + j], sib), None))
        return out

    def start(place, ins, outs, sems):
        own, push = first_wave(place, ins, outs, sems)
        for cp in own + push:
            cp.start()

    def mid(place, ins, outs, sems):
        for arrival, hand in second_wave(place, ins, outs, sems, False):
            arrival.wait_recv()
            hand.start()

    def finish(place, ins, outs, sems):
        own, push = first_wave(place, ins, outs, sems)
        for arrival, hand in second_wave(place, ins, outs, sems, True):
            arrival.wait_recv()
            if hand is not None:
                hand.wait_send()
        for cp in push:
            cp.wait_send()
        for cp in own:
            cp.wait()

    dma = pltpu.SemaphoreType.DMA
    return _Comm(shards, [jax.ShapeDtypeStruct((4,) + s.shape, s.dtype) for s in shards],
                 [dma((3 * n,)), dma((3 * n,)), dma((3 * n,)), dma((3 * n,)), dma((n,))], start, finish, mid)


def _scatter_plan(parts, small=None):
    n = len(parts)
    ns = 0 if small is None else 1

    def unpack(place, ins, outs, sems):
        x, y, c, chips = place
        return x, y, c, chips, 2 * x + y, 4 * x + 2 * y + c, (x, y, 1 - c)

    def remote(src, dst, ss, rs, to):
        return pltpu.make_async_remote_copy(src_ref=src, dst_ref=dst, send_sem=ss, recv_sem=rs, device_id=to,
                                            device_id_type=MESH)

    def first_wave(place, ins, outs, sems):
        x, y, c, chips, me, dev, sib = unpack(place, ins, outs, sems)
        ici_s, ici_r, d2d_s, d2d_r, sm_s, sm_r, local = sems
        own, push = [], []
        if ns:
            own.append(pltpu.make_async_copy(ins[n], outs[n].at[dev], local.at[n]))
            for k in range(1, 8):
                px = (1 - x) if (k >> 2) & 1 else x
                py = (1 - y) if (k >> 1) & 1 else y
                pc = (1 - c) if k & 1 else c
                push.append(remote(ins[n], outs[n].at[dev], sm_s.at[k - 1], sm_r.at[k - 1], (px, py, pc)))
        for t in range(n):
            own.append(pltpu.make_async_copy(ins[t].at[me], outs[t].at[dev], local.at[t]))
            push.append(remote(ins[t].at[me], outs[t].at[dev], d2d_s.at[4 * t], d2d_r.at[4 * t], sib))
            for j, (px, py) in enumerate(chips):
                push.append(remote(ins[t].at[2 * px + py], outs[t].at[dev], ici_s.at[3 * t + j], ici_r.at[3 * t + j],
                                   (px, py, c)))
        return own, push

    def start(place, ins, outs, sems):
        own, push = first_wave(place, ins, outs, sems)
        for cp in own + push:
            cp.start()

    def landed(dst, rs, sems, sib):
        remote(dst, dst, sems[-1].at[0], rs, sib).wait_recv()

    def forwards(place, ins, outs, sems):
        x, y, c, chips, me, dev, sib = unpack(place, ins, outs, sems)
        d2d_s, d2d_r = sems[2], sems[3]
        slots = [(t, j, outs[t].at[4 * px + 2 * py + c]) for t in range(n) for j, (px, py) in enumerate(chips)]
        return [(t, j, slot, remote(slot, slot, d2d_s.at[4 * t + 1 + j], d2d_r.at[4 * t + 1 + j], sib))
                for t, j, slot in slots]

    def mid(place, ins, outs, sems):
        sib = unpack(place, ins, outs, sems)[-1]
        for t, j, slot, cp in forwards(place, ins, outs, sems):
            landed(slot, sems[1].at[3 * t + j], sems, sib)
            cp.start()

    def finish(place, ins, outs, sems):
        x, y, c, chips, me, dev, sib = unpack(place, ins, outs, sems)
        d2d_r, sm_r = sems[3], sems[5]
        own, push = first_wave(place, ins, outs, sems)
        push += [cp for _, _, _, cp in forwards(place, ins, outs, sems)]
        for t in range(n):
            landed(outs[t].at[4 * x + 2 * y + (1 - c)], d2d_r.at[4 * t], sems, sib)
            for j, (px, py) in enumerate(chips):
                landed(outs[t].at[4 * px + 2 * py + (1 - c)], d2d_r.at[4 * t + 1 + j], sems, sib)
        if ns:
            for k in range(1, 8):
                px = (1 - x) if (k >> 2) & 1 else x
                py = (1 - y) if (k >> 1) & 1 else y
                pc = (1 - c) if k & 1 else c
                landed(outs[n].at[4 * px + 2 * py + pc], sm_r.at[k - 1], sems, sib)
        for cp in push:
            cp.wait_send()
        for cp in own:
            cp.wait()

    dma = pltpu.SemaphoreType.DMA
    ins = list(parts) + ([small] if ns else [])
    out_shape = [jax.ShapeDtypeStruct((8,) + p.shape[1:], p.dtype) for p in parts]
    if ns:
        out_shape.append(jax.ShapeDtypeStruct((8,) + small.shape, small.dtype))
    return _Comm(ins, out_shape, [dma((3 * n,)), dma((3 * n,)), dma((4 * n,)), dma((4 * n,)), dma((7,)), dma((7,)),
                                  dma((n + 1,))], start, finish, mid)


def _row_tile(r, cap=256):
    if r <= cap:
        return r
    return max(t for t in range(8, cap + 1, 8) if r % t == 0)


def _adamw(w, m, v, land, name):
    r, c = w.shape
    t = _row_tile(r)

    def kern(w_ref, m_ref, v_ref, l_ref, g_out, d_out, m_out, v_out):
        g = l_ref[0].astype(F32)
        for i in range(1, 8):
            g = g + l_ref[i].astype(F32)
        m_new = ADAM_B1 * m_ref[...] + (1.0 - ADAM_B1) * g
        v_new = ADAM_B2 * v_ref[...] + (1.0 - ADAM_B2) * (g * g)
        m_hat = m_new / (1.0 - ADAM_B1 ** ADAM_STEP)
        v_hat = v_new / (1.0 - ADAM_B2 ** ADAM_STEP)
        g_out[...] = g
        d_out[...] = -ADAM_LR * (m_hat / (jnp.sqrt(v_hat) + ADAM_EPS) + ADAM_WD * w_ref[...])
        m_out[...] = m_new
        v_out[...] = v_new

    spec = pl.BlockSpec((t, c), lambda i: (i, 0))
    return pl.pallas_call(
        kern, grid=(r // t,), in_specs=[spec] * 3 + [pl.BlockSpec((8, t, c), lambda i: (0, i, 0))], out_specs=[spec] * 4,
        out_shape=[jax.ShapeDtypeStruct((r, c), F32)] * 4,
        compiler_params=pltpu.CompilerParams(dimension_semantics=("parallel",), vmem_limit_bytes=VMEM_LIMIT),
        name=name)(w, m, v, land)


def _step(a):
    def sq(n):
        v = a[n][0] if a[n].ndim == 3 else a[n]
        return v.T if n.removeprefix("m_").removeprefix("v_") in TRANSPOSED else v

    payload = lambda n: sq(n) if n in EXACT_GATHER else sq(n).astype(BF16)

    gathered = _run_comm(_gather_plan([payload(n) for n in EARLY]), "gather_early")
    w = _early_layout(dict(zip(EARLY, gathered, strict=True)), {n: a[n] for n in REPLICATED})

    loss, dx, g, lands_late = _local_step(sq("x"), sq("mem"), a["positions"][0], sq("loss_target"), w,
                                          [payload(n) for n in LATE])

    sh, rep = _early_grad_shards(g)
    small = _pack_small([rep[n] for n in REPLICATED] + [loss.reshape(1, 1)])
    *lands_early, land_small = _run_comm(_scatter_plan([sh[n] for n in EARLY], small), "scatter_last")
    lands = dict(zip(EARLY, lands_early, strict=True)) | lands_late

    outs = {}
    kinds = ("grad_", "delta_", "new_m_", "new_v_")
    for n, _ in SHARDED:
        res = _adamw(sq(n), sq("m_" + n), sq("v_" + n), lands[n], "adamw_" + n)
        for kind, val in zip(kinds, res, strict=True):
            outs[kind + n] = (val.T if n in TRANSPOSED else val).reshape(a[n].shape)
    zero = jnp.zeros((1, 1), F32)
    packed = [_pack_small([a[p + n] for n in REPLICATED] + [zero]) for p in ("", "m_", "v_")]
    res = _adamw(*packed, land_small, "adamw_replicated")
    widths = [a[n].shape[1] for n in REPLICATED] + [1]
    for kind, buf in zip(kinds, res, strict=True):
        *vals, total = _unpack_small(buf, widths)
        for n, val in zip(REPLICATED, vals, strict=True):
            outs[kind + n] = val
        if kind == "grad_":
            loss = total[0, 0]

    ordered = [outs[kind + n] for kind in kinds for n in WEIGHTS]
    return (loss, dx[None], *ordered)


def kernel(x, mem, positions, norm_mix, w_in, gla_gate_w2, gla_gate_b, gla_out_norm, mla_q_a_norm, mla_w_uq, mla_kv_a_norm, mla_w_ukv, mla_q_norm, mla_k_norm, w_out, norm_xa, norm_mem, xa_w_q, xa_w_kv, xa_q_norm, xa_k_norm, xa_w_o, norm_ffn, ffn_w_gate, ffn_w_up, ffn_conv_w, ffn_conv_b, ffn_w_down, loss_target, m_norm_mix, m_w_in, m_gla_gate_w2, m_gla_gate_b, m_gla_out_norm, m_mla_q_a_norm, m_mla_w_uq, m_mla_kv_a_norm, m_mla_w_ukv, m_mla_q_norm, m_mla_k_norm, m_w_out, m_norm_xa, m_norm_mem, m_xa_w_q, m_xa_w_kv, m_xa_q_norm, m_xa_k_norm, m_xa_w_o, m_norm_ffn, m_ffn_w_gate, m_ffn_w_up, m_ffn_conv_w, m_ffn_conv_b, m_ffn_w_down, v_norm_mix, v_w_in, v_gla_gate_w2, v_gla_gate_b, v_gla_out_norm, v_mla_q_a_norm, v_mla_w_uq, v_mla_kv_a_norm, v_mla_w_ukv, v_mla_q_norm, v_mla_k_norm, v_w_out, v_norm_xa, v_norm_mem, v_xa_w_q, v_xa_w_kv, v_xa_q_norm, v_xa_k_norm, v_xa_w_o, v_norm_ffn, v_ffn_w_gate, v_ffn_w_up, v_ffn_conv_w, v_ffn_conv_b, v_ffn_w_down):
    return _step(dict(locals()))
```

```python
import functools

import jax
import jax.numpy as jnp
from jax import lax
from jax.experimental import pallas as pl
from jax.experimental.pallas import tpu as pltpu

F32, BF16 = jnp.float32, jnp.bfloat16
MESH = pl.DeviceIdType.MESH

D_MODEL = 1024
EPS = 1e-6
GLA_HEADS, GLA_DK, GLA_DV, GLA_RANK, GLA_CHUNK = 4, 64, 128, 16, 64
GLA_GATE_NORM = 16.0
MLA_HEADS, MLA_Q_RANK, MLA_KV_RANK, MLA_NOPE, MLA_ROPE, MLA_V = 8, 256, 128, 64, 32, 64
MLA_QK = MLA_NOPE + MLA_ROPE
ROPE_THETA = 10000.0
LOG2E, LN2 = 1.4426950408889634, 0.6931471805599453
XA_HEADS, XA_DIM = 4, 128
D_FF = 2816
ADAM_LR, ADAM_B1, ADAM_B2, ADAM_EPS, ADAM_WD, ADAM_STEP = 0.001, 0.9, 0.999, 1e-08, 0.01, 10

LANES = 128
BF16_ROWS = 16
VMEM_LIMIT = 56 * 1024 * 1024
MATMUL_VMEM = 44 * 1024 * 1024

P_GQ, P_GK, P_GV, P_OG, P_CQ, P_CKV, P_KPE, P_ALR, P_WIDTH = 0, 256, 512, 1024, 1536, 1792, 1920, 2048, 2176
N_GQ, N_GK, N_GV, N_ALR, N_OG, N_CQ, N_CKV, N_KPE, N_WIDTH = 0, 256, 512, 1024, 1040, 1552, 1808, 1936, 1968

SHARDED = (("w_in", 1), ("gla_gate_w2", 1), ("mla_w_uq", 1), ("mla_w_ukv", 1), ("w_out", 0), ("xa_w_q", 0),
           ("xa_w_kv", 0), ("xa_w_o", 1), ("ffn_w_gate", 1), ("ffn_w_up", 1), ("ffn_conv_w", 1), ("ffn_w_down", 0))
REPLICATED = ("norm_mix", "gla_gate_b", "gla_out_norm", "mla_q_a_norm", "mla_kv_a_norm", "mla_q_norm", "mla_k_norm",
              "norm_xa", "norm_mem", "xa_q_norm", "xa_k_norm", "norm_ffn", "ffn_conv_b")
EXACT_GATHER = ("gla_gate_w2", "ffn_conv_w")
TRANSPOSED = ("ffn_w_gate", "ffn_w_up")
EARLY = ("w_in", "gla_gate_w2", "mla_w_uq", "mla_w_ukv")
LATE = tuple(n for n, _ in SHARDED if n not in EARLY)
LATE_MLP = tuple(n for n in LATE if n.startswith("ffn_"))
LATE_MIX = tuple(n for n in LATE if not n.startswith("ffn_"))
WEIGHTS = ("norm_mix", "w_in", "gla_gate_w2", "gla_gate_b", "gla_out_norm", "mla_q_a_norm", "mla_w_uq",
           "mla_kv_a_norm", "mla_w_ukv", "mla_q_norm", "mla_k_norm", "w_out", "norm_xa", "norm_mem", "xa_w_q",
           "xa_w_kv", "xa_q_norm", "xa_k_norm", "xa_w_o", "norm_ffn", "ffn_w_gate", "ffn_w_up", "ffn_conv_w",
           "ffn_conv_b", "ffn_w_down")


_NN = ((1,), (0,))
_NT = ((1,), (1,))
_TN = ((0,), (0,))


def _dg(a, b, dims):
    return lax.dot_general(a.astype(BF16), b.astype(BF16), (dims, ((), ())), preferred_element_type=F32)


@jax.custom_vjp
def _dot_nn(a, b):
    return _dg(a, b, _NN)


_dot_nn.defvjp(lambda a, b: (_dg(a, b, _NN), (a, b)),
               lambda r, g: (_dg(g, r[1], _NT).astype(r[0].dtype), _dg(r[0], g, _TN).astype(r[1].dtype)))


@jax.custom_vjp
def _dot_nt(a, b):
    return _dg(a, b, _NT)


_dot_nt.defvjp(lambda a, b: (_dg(a, b, _NT), (a, b)),
               lambda r, g: (_dg(g, r[1], _NN).astype(r[0].dtype), _dg(g, r[0], _TN).astype(r[1].dtype)))


@jax.custom_vjp
def _dot_tn(a, b):
    return _dg(a, b, _TN)


_dot_tn.defvjp(lambda a, b: (_dg(a, b, _TN), (a, b)),
               lambda r, g: (_dg(r[1], g, _NT).astype(r[0].dtype), _dg(r[0], g, _NN).astype(r[1].dtype)))


def _rms(x, w, n=None):
    n = x.shape[-1] if n is None else n
    ms = jnp.sum(x * x, axis=-1, keepdims=True) * (1.0 / n)
    return x * lax.rsqrt(ms + EPS) * w


def _silu(x):
    return x * jax.nn.sigmoid(x)


def _log_sigmoid(x):
    return jnp.minimum(x, 0.0) - jnp.log(1.0 + jnp.exp(-jnp.abs(x)))


@jax.custom_vjp
def _rope(y, c, sa, sb):
    return y * c + pltpu.roll(y, LANES - 16, 1) * sa + pltpu.roll(y, 16, 1) * sb


def _rope_bwd(res, g):
    c, sa, sb = res
    gy = g * c + pltpu.roll(g * sa, 16, 1) + pltpu.roll(g * sb, LANES - 16, 1)
    return gy, jnp.zeros_like(c), jnp.zeros_like(sa), jnp.zeros_like(sb)


_rope.defvjp(lambda y, c, sa, sb: (_rope(y, c, sa, sb), (c, sa, sb)), _rope_bwd)


def _lane_mask(lo, hi):
    lane = lax.broadcasted_iota(jnp.int32, (1, LANES), 1)
    return ((lane >= lo) & (lane < hi)).astype(F32)


def _tile(n, t):
    t = min(n, t)
    assert n % t == 0, (n, t)
    return t


def _matmul(a, b, mode, out_dtype, name, residual=None, a_lead=None, b_lead=None):
    (a0, a1), (b0, b1) = a.shape[-2:], b.shape[-2:]
    if mode == "nn":
        m, k, k2, n = a0, a1, b0, b1
    elif mode == "nt":
        m, k, n, k2 = a0, a1, b0, b1
    else:
        k, m, k2, n = a0, a1, b0, b1
    assert k == k2, (a.shape, b.shape, mode)
    npar = 4 if "p" in (a_lead, b_lead) else 1
    nsum = 4 if "k" in (a_lead, b_lead) else 1
    a_item, b_item, o_item = a.dtype.itemsize, b.dtype.itemsize, jnp.dtype(out_dtype).itemsize

    def vmem_need(tm, tn, tk):
        need = 2 * (nsum if a_lead == "k" else 1) * tm * tk * a_item + 2 * (nsum if b_lead == "k" else 1) * tk * tn * b_item
        need += 2 * tm * tn * o_item + tm * tn * 4 * (2 if tk < k else 1)
        need += tm * tk * 2 * (a_item == 4 or mode == "tn") + tk * tn * 2 * (b_item == 4)
        return need + (2 * tm * tn * 4 if residual is not None else 0)

    halvings = (4096, 2048, 1024, 512, 256, 128, 64, 32, 16, 8)
    if mode == "tn":
        tm = m if m <= 1408 else m // 2
        tn = n if tm * n <= 1024 * 2304 else n // 2
        tk = next((r for r in halvings if k % r == 0 and vmem_need(tm, tn, r) <= MATMUL_VMEM), k)
    else:
        tn, tk = n, k
        tm = next((r for r in halvings if m % r == 0 and vmem_need(r, tn, tk) <= MATMUL_VMEM), m)
    assert m % tm == 0 and n % tn == 0 and k % tk == 0
    nk = k // tk
    dims = {"nn": _NN, "nt": _NT, "tn": _TN}[mode]

    def body(*refs):
        a_ref, b_ref = refs[0], refs[1]
        r_ref = refs[2] if residual is not None else None
        o_ref = refs[3 if residual is not None else 2]
        prod = None
        for sh in range(nsum):
            term = _dg(a_ref[sh] if a_lead == "k" else a_ref[...], b_ref[sh] if b_lead == "k" else b_ref[...], dims)
            prod = term if prod is None else prod + term
        if nk == 1:
            o_ref[...] = (prod if r_ref is None else prod + r_ref[...]).astype(o_ref.dtype)
            return
        acc = refs[-1]
        kk = pl.program_id(3)

        @pl.when(kk == 0)
        def _():
            acc[...] = prod

        @pl.when(kk > 0)
        def _():
            acc[...] += prod

        @pl.when(kk == nk - 1)
        def _():
            r = acc[...]
            if r_ref is not None:
                r = r + r_ref[...]
            o_ref[...] = r.astype(o_ref.dtype)

    def spec(lead, blk, idx):
        if lead is None:
            return pl.BlockSpec(blk, lambda i, j, p, kk: idx(i, j, kk))
        if lead == "p":
            return pl.BlockSpec((None,) + blk, lambda i, j, p, kk: (p,) + idx(i, j, kk))
        return pl.BlockSpec((nsum,) + blk, lambda i, j, p, kk: (0,) + idx(i, j, kk))

    if mode == "nn":
        in_specs = [spec(a_lead, (tm, tk), lambda i, j, kk: (i, kk)), spec(b_lead, (tk, tn), lambda i, j, kk: (kk, j))]
    elif mode == "nt":
        in_specs = [spec(a_lead, (tm, tk), lambda i, j, kk: (i, kk)), spec(b_lead, (tn, tk), lambda i, j, kk: (j, kk))]
    else:
        in_specs = [spec(a_lead, (tk, tm), lambda i, j, kk: (kk, i)), spec(b_lead, (tk, tn), lambda i, j, kk: (kk, j))]
    args = [a, b]
    if residual is not None:
        assert npar == 1
        in_specs.append(spec(None, (tm, tn), lambda i, j, kk: (i, j)))
        args.append(residual)
    return pl.pallas_call(
        body, grid=(m // tm, n // tn, npar, nk), in_specs=in_specs,
        out_specs=spec("p" if npar > 1 else None, (tm, tn), lambda i, j, kk: (i, j)),
        out_shape=jax.ShapeDtypeStruct(((4,) if npar > 1 else ()) + (m, n), out_dtype),
        scratch_shapes=[pltpu.VMEM((tm, tn), F32)] if nk > 1 else [],
        compiler_params=pltpu.CompilerParams(dimension_semantics=("parallel", "parallel", "parallel", "arbitrary"),
                                             vmem_limit_bytes=VMEM_LIMIT),
        name=name)(*args)


def _row(a, width=None, col_block=0):
    return (a, a.shape[1] if width is None else width, col_block)


def _rows_call(body, rows, consts, outs, accs=(), *, name, tile=512):
    s = rows[0][0].shape[0]
    t = _tile(s, tile)
    nr, nc, no = len(rows), len(consts), len(outs)

    def kern(*refs):
        r = [x[...] for x in refs[:nr]]
        c = [x[...] for x in refs[nr:nr + nc]]
        o_refs = refs[nr + nc:nr + nc + no]
        a_refs = refs[nr + nc + no:]
        ro, ao = body(r, c)
        for ref, val in zip(o_refs, ro, strict=True):
            ref[...] = val.astype(ref.dtype)
        if a_refs:
            @pl.when(pl.program_id(0) == 0)
            def _():
                for ref in a_refs:
                    ref[...] = jnp.zeros_like(ref)

            for ref, val in zip(a_refs, ao, strict=True):
                ref[...] += val

    in_specs = [pl.BlockSpec((t, w), functools.partial(lambda cb, i: (i, cb), cb)) for (_, w, cb) in rows]
    in_specs += [pl.BlockSpec(c.shape, lambda i: (0, 0)) for c in consts]
    out_specs = [pl.BlockSpec((t, w), lambda i: (i, 0)) for (w, _) in outs]
    out_specs += [pl.BlockSpec(shape, lambda i: (0, 0)) for shape in accs]
    out_shape = [jax.ShapeDtypeStruct((s, w), dt) for (w, dt) in outs]
    out_shape += [jax.ShapeDtypeStruct(shape, F32) for shape in accs]
    return pl.pallas_call(
        kern, grid=(s // t,), in_specs=in_specs, out_specs=out_specs, out_shape=out_shape,
        compiler_params=pltpu.CompilerParams(dimension_semantics=("arbitrary" if accs else "parallel",),
                                             vmem_limit_bytes=VMEM_LIMIT),
        name=name)(*[r[0] for r in rows], *consts)


def _gla_chunk(q, k, la, v0, v1, s0, s1):
    c = q.shape[0]
    r = lax.broadcasted_iota(jnp.int32, (c, c), 0)
    cc = lax.broadcasted_iota(jnp.int32, (c, c), 1)
    tril = cc <= r
    cum = lax.dot_general(tril.astype(F32), la, (_NN, ((), ())), precision=lax.Precision.HIGHEST,
                          preferred_element_type=F32)
    cl = jnp.sum(la, axis=0, keepdims=True)
    qd = q * (GLA_DK ** -0.5) * jnp.exp(cum)
    ki = k * jnp.exp(-cum)
    ke = k * jnp.exp(cl - cum)
    dec = jnp.exp(cl)
    outs, news = [], []
    for h, (v, s) in enumerate(((v0, s0), (v1, s1))):
        mk = _lane_mask(GLA_DK * h, GLA_DK * (h + 1))
        qh = qd * mk
        att = jnp.where(tril, _dot_nt(qh, ki), 0.0)
        outs.append(_dot_nn(att, v) + _dot_nt(qh, s))
        news.append(s * dec + _dot_tn(v, ke * mk))
    return outs[0], outs[1], news[0], news[1]


def _gla_specs(tb, rev_nb=None):
    blk = (lambda b: b) if rev_nb is None else (lambda b: rev_nb - 1 - b)
    q = pl.BlockSpec((tb, 128), lambda p, b: (blk(b), P_GQ // 128 + p))
    k = pl.BlockSpec((tb, 128), lambda p, b: (blk(b), P_GK // 128 + p))
    la = pl.BlockSpec((tb, 128), lambda p, b: (blk(b), p))
    v = pl.BlockSpec((tb, 256), lambda p, b: (blk(b), P_GV // 256 + p))
    o = pl.BlockSpec((tb, 256), lambda p, b: (blk(b), p))
    st = pl.BlockSpec((tb // GLA_CHUNK, 2, 128, 128), lambda p, b: (blk(b), p, 0, 0))
    return q, k, la, v, o, st


def _gla_fwd(proj, la):
    s = proj.shape[0]
    tb = _tile(s, 512)
    nb, nch = s // tb, tb // GLA_CHUNK

    def kern(q_ref, k_ref, la_ref, v_ref, o_ref, st_ref, s_sc):
        @pl.when(pl.program_id(1) == 0)
        def _():
            s_sc[...] = jnp.zeros_like(s_sc)

        s0, s1 = s_sc[0], s_sc[1]
        for ci in range(nch):
            sl = slice(ci * GLA_CHUNK, (ci + 1) * GLA_CHUNK)
            st_ref[ci, 0] = s0
            st_ref[ci, 1] = s1
            o0, o1, s0, s1 = _gla_chunk(q_ref[sl, :], k_ref[sl, :], la_ref[sl, :], v_ref[sl, 0:128],
                                        v_ref[sl, 128:256], s0, s1)
            o_ref[sl, 0:128] = o0
            o_ref[sl, 128:256] = o1
        s_sc[0] = s0
        s_sc[1] = s1

    q, k, lasp, v, o, st = _gla_specs(tb)
    return pl.pallas_call(
        kern, grid=(2, nb), in_specs=[q, k, lasp, v], out_specs=[o, st],
        out_shape=[jax.ShapeDtypeStruct((s, 512), F32),
                   jax.ShapeDtypeStruct((s // GLA_CHUNK, GLA_HEADS, 128, 128), F32)],
        scratch_shapes=[pltpu.VMEM((2, 128, 128), F32)],
        compiler_params=pltpu.CompilerParams(dimension_semantics=("parallel", "arbitrary"),
                                             vmem_limit_bytes=VMEM_LIMIT),
        name="gla_fwd")(proj, proj, la, proj)


def _gla_bwd(proj, la, states, d_o, comm):
    s = proj.shape[0]
    tb = _tile(s, 512)
    nb, nch = s // tb, tb // GLA_CHUNK
    nci, nco = len(comm.ins), len(comm.out_shape)

    def kern(*refs):
        (q_ref, k_ref, la_ref, v_ref, do_ref, st_ref), cins, (dq_ref, dk_ref, dla_ref, dv_ref), couts, (ds_sc,), csems = \
            _split_refs(refs, (6, nci, 4, nco, 1, len(comm.sems)))
        place = _place()
        pair, blk = pl.program_id(0), pl.program_id(1)

        @pl.when((pair == 0) & (blk == 0))
        def _():
            comm.start(place, cins, couts, csems)

        @pl.when((pair == 1) & (blk == nb // 2))
        def _():
            comm.mid(place, cins, couts, csems)

        @pl.when(blk == 0)
        def _():
            ds_sc[...] = jnp.zeros_like(ds_sc)

        d0, d1 = ds_sc[0], ds_sc[1]
        for ci in reversed(range(nch)):
            sl = slice(ci * GLA_CHUNK, (ci + 1) * GLA_CHUNK)
            _, vjp = jax.vjp(_gla_chunk, q_ref[sl, :], k_ref[sl, :], la_ref[sl, :], v_ref[sl, 0:128],
                             v_ref[sl, 128:256], st_ref[ci, 0], st_ref[ci, 1])
            gq, gk, gla, gv0, gv1, d0, d1 = vjp((do_ref[sl, 0:128], do_ref[sl, 128:256], d0, d1))
            dq_ref[sl, :] = gq
            dk_ref[sl, :] = gk
            dla_ref[sl, :] = gla
            dv_ref[sl, 0:128] = gv0
            dv_ref[sl, 128:256] = gv1
        ds_sc[0] = d0
        ds_sc[1] = d1

        @pl.when((pair == 1) & (blk == nb - 1))
        def _():
            comm.finish(place, cins, couts, csems)

    q, k, lasp, v, o, st = _gla_specs(tb, rev_nb=nb)
    res = pl.pallas_call(
        kern, grid=(2, nb), in_specs=[q, k, lasp, v, o, st] + [ANY] * nci, out_specs=[lasp, lasp, lasp, o] + [ANY] * nco,
        out_shape=[jax.ShapeDtypeStruct((s, 256), F32), jax.ShapeDtypeStruct((s, 256), F32),
                   jax.ShapeDtypeStruct((s, 256), F32), jax.ShapeDtypeStruct((s, 512), F32)] + comm.out_shape,
        scratch_shapes=[pltpu.VMEM((2, 128, 128), F32)] + comm.sems,
        compiler_params=pltpu.CompilerParams(dimension_semantics=("arbitrary", "arbitrary"),
                                             vmem_limit_bytes=VMEM_LIMIT),
        name="gla_bwd")(proj, proj, la, proj, d_o, states, *comm.ins)
    return res[0], res[1], res[2], res[3], res[4:]


def _causal_keep(t, qi, ki):
    row = lax.broadcasted_iota(jnp.int32, (t, t), 0) + qi * t
    col = lax.broadcasted_iota(jnp.int32, (t, t), 1) + ki * t
    return col <= row


def _split_refs(refs, counts):
    out, off = [], 0
    for cnt in counts:
        out.append(refs[off:off + cnt])
        off += cnt
    return out


def _attn_fwd(q, k, v, comm, tile=1024):
    s = q.shape[0]
    t = _tile(s, tile)
    n = s // t
    nci, nco = len(comm.ins), len(comm.out_shape)

    def kern(*refs):
        (q_ref, k_ref, v_ref), cins, (o_ref, lse_ref), couts, (m_sc, l_sc, acc_sc), csems = _split_refs(
            refs, (3, nci, 2, nco, 3, len(comm.sems)))
        qi, ki = pl.program_id(1), pl.program_id(2)
        place = _place()

        @pl.when((pl.program_id(0) == 0) & (qi == 0) & (ki == 0))
        def _():
            comm.start(place, cins, couts, csems)

        @pl.when((pl.program_id(0) == MLA_HEADS // 2 - 1) & (qi == 0) & (ki == 0))
        def _():
            comm.mid(place, cins, couts, csems)

        first = lax.broadcasted_iota(jnp.int32, (t, LANES), 1) < MLA_V

        @pl.when(ki == 0)
        def _():
            m_sc[...] = jnp.full_like(m_sc, -jnp.inf)
            l_sc[...] = jnp.zeros_like(l_sc)
            acc_sc[...] = jnp.zeros_like(acc_sc)

        def update(diagonal):
            keep = _causal_keep(t, 0, 0)
            alphas, pvs = [], []
            for h in range(2):
                sc = _dg(q_ref[:, 128 * h:128 * (h + 1)], k_ref[:, 128 * h:128 * (h + 1)], _NT)
                if diagonal:
                    sc = jnp.where(keep, sc, -jnp.inf)
                m_prev = m_sc[h]
                m_new = jnp.maximum(m_prev, jnp.max(sc, axis=1, keepdims=True))
                alpha = jnp.exp2(m_prev - m_new)
                p = jnp.exp2(sc - m_new[:, 0:1])
                l_sc[h] = alpha * l_sc[h] + jnp.sum(p, axis=1, keepdims=True)
                m_sc[h] = m_new
                alphas.append(alpha)
                pvs.append(_dg(p, v_ref[...], _NN))
            acc_sc[...] = acc_sc[...] * jnp.where(first, alphas[0], alphas[1]) + jnp.where(first, pvs[0], pvs[1])

        @pl.when(ki < qi)
        def _():
            update(False)

        @pl.when(ki == qi)
        def _():
            update(True)

        @pl.when(ki == qi)
        def _():
            l = jnp.where(first, l_sc[0], l_sc[1])
            m = jnp.where(first, m_sc[0], m_sc[1])
            o_ref[...] = acc_sc[...] / l
            lse_ref[...] = m + jnp.log2(l)

        @pl.when((pl.program_id(0) == MLA_HEADS // 2 - 1) & (qi == n - 1) & (ki == n - 1))
        def _():
            comm.finish(place, cins, couts, csems)

    kv_idx = lambda p, qi, ki: (jnp.minimum(ki, qi), p)
    res = pl.pallas_call(
        kern, grid=(MLA_HEADS // 2, n, n),
        in_specs=[pl.BlockSpec((t, 256), lambda p, qi, ki: (qi, p)), pl.BlockSpec((t, 256), kv_idx),
                  pl.BlockSpec((t, 128), kv_idx)] + [ANY] * nci,
        out_specs=[pl.BlockSpec((t, 128), lambda p, qi, ki: (qi, p)), pl.BlockSpec((t, 128), lambda p, qi, ki: (qi, p))]
        + [ANY] * nco,
        out_shape=[jax.ShapeDtypeStruct((s, 512), F32), jax.ShapeDtypeStruct((s, 512), F32)] + comm.out_shape,
        scratch_shapes=[pltpu.VMEM((2, t, LANES), F32), pltpu.VMEM((2, t, LANES), F32), pltpu.VMEM((t, LANES), F32)]
        + comm.sems,
        compiler_params=pltpu.CompilerParams(dimension_semantics=("arbitrary", "arbitrary", "arbitrary"),
                                             vmem_limit_bytes=VMEM_LIMIT),
        name="mla_attn_fwd")(q, k, v, *comm.ins)
    return res[0], res[1], res[2:]


def _attn_bwd(q, k, v, o, lse, dcat, comm, tile=512):
    s = q.shape[0]
    t = _tile(s, tile)
    n = s // t
    nci, nco = len(comm.ins), len(comm.out_shape)

    def kern(*refs):
        (q_ref, k_ref, v_ref, o_ref, lse_ref, do_ref), cins, (dq_ref, dk_ref, dv_ref), couts, (dk_sc, dv_sc), csems = \
            _split_refs(refs, (6, nci, 3, nco, 2, len(comm.sems)))
        ki, qi = pl.program_id(1), pl.program_id(2)
        place = _place()

        @pl.when((pl.program_id(0) == 0) & (qi == 0) & (ki == 0))
        def _():
            comm.start(place, cins, couts, csems)

        @pl.when((pl.program_id(0) == MLA_HEADS // 2 - 1) & (qi == 0) & (ki == 0))
        def _():
            comm.mid(place, cins, couts, csems)

        @pl.when((ki == 0) & (qi == 0))
        def _():
            dq_ref[...] = jnp.zeros_like(dq_ref)

        @pl.when(qi == ki)
        def _():
            dk_sc[...] = jnp.zeros_like(dk_sc)
            dv_sc[...] = jnp.zeros_like(dv_sc)

        def update(diagonal):
            keep = _causal_keep(t, 0, 0)
            d_o = do_ref[...]
            prod = d_o * o_ref[...]
            rows = pl.ds(pl.multiple_of(qi * t, t), t)
            for h in range(2):
                hs = slice(128 * h, 128 * (h + 1))
                mk = _lane_mask(MLA_V * h, MLA_V * (h + 1))
                qh, kh = q_ref[:, hs], k_ref[:, hs]
                sc = _dg(qh, kh, _NT)
                if diagonal:
                    sc = jnp.where(keep, sc, -jnp.inf)
                p = jnp.exp2(sc - lse_ref[:, MLA_V * h:MLA_V * h + 1])
                doh = d_o * mk
                dp = _dg(doh * LN2, v_ref[...], _NT)
                delta = jnp.sum(prod * mk, axis=1, keepdims=True) * LN2
                ds = p * (dp - delta)
                dv_sc[...] += _dg(p, doh, _TN)
                dk_sc[:, hs] += _dg(ds, qh, _TN)
                dq_ref[rows, hs] += _dg(ds, kh, _NN)

        @pl.when(qi > ki)
        def _():
            update(False)

        @pl.when(qi == ki)
        def _():
            update(True)

        @pl.when(qi == n - 1)
        def _():
            dk_ref[...] = dk_sc[...]
            dv_ref[...] = dv_sc[...].astype(dv_ref.dtype)

        @pl.when((pl.program_id(0) == MLA_HEADS // 2 - 1) & (qi == n - 1) & (ki == n - 1))
        def _():
            comm.finish(place, cins, couts, csems)

    q_idx = lambda p, ki, qi: (jnp.maximum(qi, ki), p)
    res = pl.pallas_call(
        kern, grid=(MLA_HEADS // 2, n, n),
        in_specs=[pl.BlockSpec((t, 256), q_idx), pl.BlockSpec((t, 256), lambda p, ki, qi: (ki, p)),
                  pl.BlockSpec((t, 128), lambda p, ki, qi: (ki, p)), pl.BlockSpec((t, 128), q_idx),
                  pl.BlockSpec((t, 128), q_idx),
                  pl.BlockSpec((t, 128), lambda p, ki, qi: (jnp.maximum(qi, ki), 4 + p))] + [ANY] * nci,
        out_specs=[pl.BlockSpec((s, 256), lambda p, ki, qi: (0, p)), pl.BlockSpec((t, 256), lambda p, ki, qi: (ki, p)),
                   pl.BlockSpec((t, 128), lambda p, ki, qi: (ki, p))] + [ANY] * nco,
        out_shape=[jax.ShapeDtypeStruct((s, 1024), F32), jax.ShapeDtypeStruct((s, 1024), F32),
                   jax.ShapeDtypeStruct((s, 512), BF16)] + comm.out_shape,
        scratch_shapes=[pltpu.VMEM((t, 256), F32), pltpu.VMEM((t, 128), F32)] + comm.sems,
        compiler_params=pltpu.CompilerParams(dimension_semantics=("arbitrary", "arbitrary", "arbitrary"),
                                             vmem_limit_bytes=VMEM_LIMIT),
        name="mla_attn_bwd")(q, k, v, o, lse, dcat, *comm.ins)
    return res[0], res[1], res[2], res[3:]


def _gate_fn(alr, w2, b):
    return _log_sigmoid(_dot_nn(alr, w2) + b) * (1.0 / GLA_GATE_NORM)


def _qk_head(qh, kh, kpe, c, sa, sb, qn, kn):
    kfull = kh + kpe * _lane_mask(MLA_NOPE, MLA_QK)
    q_r = _rope(_rms(qh, qn, MLA_QK), c, sa, sb) * (MLA_QK ** -0.5 * LOG2E)
    k_r = _rope(_rms(kfull, kn, MLA_QK), c, sa, sb)
    return q_r, k_r


def _mix_head(o, og, gn):
    return _rms(o, gn) * _silu(og)


def _xa_head(xq, xk, xv, qn, kn):
    sc = _dot_nt(_rms(xq, qn), _rms(xk, kn)) * (XA_DIM ** -0.5)
    e = jnp.exp(sc - lax.stop_gradient(jnp.max(sc, axis=1, keepdims=True)))
    p = e / jnp.sum(e, axis=1, keepdims=True)
    return _dot_nn(p, xv)


def _heads(x, n):
    return [x[:, 128 * h:128 * (h + 1)] for h in range(n)]


def _cat(xs):
    return jnp.concatenate(xs, axis=1)


def _norm_fwd(x, w, name):
    return _rows_call(lambda r, c: ([_rms(r[0], c[0])], []), [_row(x)], [w], [(x.shape[1], BF16)], name=name)[0]


def _norm_bwd(x, w, d_out, add, name):
    def body(r, c):
        _, vjp = jax.vjp(_rms, r[0], c[0])
        dx, dw = vjp(r[1])
        return [dx + r[2]], [dw]

    return _rows_call(body, [_row(x), _row(d_out), _row(add)], [w], [(x.shape[1], F32)], [w.shape], name=name)


CONV_HALO = BF16_ROWS


def _conv_specs(s, f, t):
    n8 = t // CONV_HALO
    cur = pl.BlockSpec((None, t, f), lambda j, i: (j, i, 0))
    prev = pl.BlockSpec((None, CONV_HALO, f), lambda j, i: (j, jnp.maximum(i * n8 - 1, 0), 0))
    nxt = pl.BlockSpec((None, CONV_HALO, f), lambda j, i: (j, jnp.minimum((i + 1) * n8, s // CONV_HALO - 1), 0))
    cw = pl.BlockSpec((None, 3, f), lambda j, i: (j, 0, 0))
    cb = pl.BlockSpec((None, 1, f), lambda j, i: (j, 0, 0))
    return cur, prev, nxt, cw, cb


def _conv_taps(g, prev, first):
    ext = jnp.concatenate([jnp.where(first, 0.0, prev.astype(F32)), g], axis=0)
    return pltpu.roll(ext, 1, 0)[CONV_HALO:], pltpu.roll(ext, 2, 0)[CONV_HALO:]


def _conv_fwd(gg, uu, cw, cb):
    _, s, f = gg.shape
    t = _tile(s, 512)

    def kern(g_ref, gp_ref, u_ref, cw_ref, cb_ref, o_ref):
        g = g_ref[...].astype(F32)
        g1, g2 = _conv_taps(g, gp_ref[...], pl.program_id(1) == 0)
        w = cw_ref[...]
        gc = cb_ref[...] + w[0:1] * g2 + w[1:2] * g1 + w[2:3] * g
        o_ref[...] = (_silu(gc) * u_ref[...].astype(F32)).astype(o_ref.dtype)

    cur, prev, _, cws, cbs = _conv_specs(s, f, t)
    return pl.pallas_call(
        kern, grid=(4, s // t), in_specs=[cur, prev, cur, cws, cbs], out_specs=cur,
        out_shape=jax.ShapeDtypeStruct(gg.shape, BF16),
        compiler_params=pltpu.CompilerParams(dimension_semantics=("parallel", "parallel"), vmem_limit_bytes=VMEM_LIMIT),
        name="ffn_conv_fwd")(gg, gg, uu, cw, cb)


def _conv_bwd_gate(gg, uu, dact, cw, cb):
    _, s, f = gg.shape
    t = _tile(s, 512)

    def kern(g_ref, gp_ref, u_ref, da_ref, cw_ref, cb_ref, du_ref, dgc_ref, dcw_ref, dcb_ref):
        i = pl.program_id(1)
        g, u, da = g_ref[...].astype(F32), u_ref[...].astype(F32), da_ref[...].astype(F32)
        g1, g2 = _conv_taps(g, gp_ref[...], i == 0)
        w = cw_ref[...]
        gc = cb_ref[...] + w[0:1] * g2 + w[1:2] * g1 + w[2:3] * g
        sg = jax.nn.sigmoid(gc)
        du_ref[...] = (da * (gc * sg)).astype(du_ref.dtype)
        dgc = da * u * (sg * (1.0 + gc * (1.0 - sg)))
        dgc_ref[...] = dgc.astype(dgc_ref.dtype)

        @pl.when(i == 0)
        def _():
            dcw_ref[...] = jnp.zeros_like(dcw_ref)
            dcb_ref[...] = jnp.zeros_like(dcb_ref)

        dcw_ref[0:1, :] += jnp.sum(dgc * g2, axis=0, keepdims=True)
        dcw_ref[1:2, :] += jnp.sum(dgc * g1, axis=0, keepdims=True)
        dcw_ref[2:3, :] += jnp.sum(dgc * g, axis=0, keepdims=True)
        dcb_ref[...] += jnp.sum(dgc, axis=0, keepdims=True)

    cur, prev, _, cws, cbs = _conv_specs(s, f, t)
    return pl.pallas_call(
        kern, grid=(4, s // t), in_specs=[cur, prev, cur, cur, cws, cbs], out_specs=[cur, cur, cws, cbs],
        out_shape=[jax.ShapeDtypeStruct(gg.shape, BF16), jax.ShapeDtypeStruct(gg.shape, BF16),
                   jax.ShapeDtypeStruct(cw.shape, F32), jax.ShapeDtypeStruct(cb.shape, F32)],
        compiler_params=pltpu.CompilerParams(dimension_semantics=("parallel", "arbitrary"), vmem_limit_bytes=VMEM_LIMIT),
        name="ffn_conv_bwd_gate")(gg, gg, uu, dact, cw, cb)


def _conv_bwd_taps(dgc, cw):
    _, s, f = dgc.shape
    t = _tile(s, 512)
    nt = s // t

    def kern(d_ref, dn_ref, cw_ref, o_ref):
        d = d_ref[...].astype(F32)
        ext = jnp.concatenate([d, jnp.where(pl.program_id(1) == nt - 1, 0.0, dn_ref[...].astype(F32))], axis=0)
        up1 = pltpu.roll(ext, t + CONV_HALO - 1, 0)[:t]
        up2 = pltpu.roll(ext, t + CONV_HALO - 2, 0)[:t]
        w = cw_ref[...]
        o_ref[...] = (w[2:3] * d + w[1:2] * up1 + w[0:1] * up2).astype(o_ref.dtype)

    cur, _, nxt, cws, _ = _conv_specs(s, f, t)
    return pl.pallas_call(
        kern, grid=(4, nt), in_specs=[cur, nxt, cws], out_specs=cur, out_shape=jax.ShapeDtypeStruct(dgc.shape, BF16),
        compiler_params=pltpu.CompilerParams(dimension_semantics=("parallel", "parallel"), vmem_limit_bytes=VMEM_LIMIT),
        name="ffn_conv_bwd_taps")(dgc, dgc, cw)


def _rope_tables(pos):
    half = MLA_ROPE // 2
    inv = ROPE_THETA ** (-jnp.arange(half, dtype=F32) / half)
    ang = pos.astype(F32)[:, None] * inv
    cos, sin = jnp.cos(ang), jnp.sin(ang)
    s = pos.shape[0]
    z = lambda w: jnp.zeros((s, w), F32)
    c = jnp.concatenate([jnp.ones((s, MLA_NOPE), F32), cos, cos, jnp.ones((s, LANES - MLA_QK), F32)], axis=1)
    sa = jnp.concatenate([z(MLA_NOPE), -sin, z(half), z(LANES - MLA_QK)], axis=1)
    sb = jnp.concatenate([z(MLA_NOPE), z(half), sin, z(LANES - MLA_QK)], axis=1)
    return c, sa, sb


def _local_step(x, mem, pos, target, w, late_shards):
    g = {}
    w = dict(w)
    c, sa, sb = _rope_tables(pos)

    xn = _norm_fwd(x, w["norm_mix"], "norm_mix_fwd")
    proj = _matmul(xn, w["in"], "nn", F32, "proj_fwd")
    alr = _row(proj, 128, P_ALR // 128)
    kpe = _row(proj, 128, P_KPE // 128)
    og = _row(proj, 512, P_OG // 512)
    cq = _row(proj, 256, P_CQ // 256)
    ckv = _row(proj, 128, P_CKV // 128)

    la = _rows_call(lambda r, k: ([_gate_fn(r[0], k[0], k[1])], []), [alr], [w["w2"], w["gate_b"]],
                    [(256, F32)], name="gla_gate_fwd")[0]
    o_gla, states = _gla_fwd(proj, la)

    q_lat, kv_lat = _rows_call(lambda r, k: ([_rms(r[0], k[0]), _rms(r[1], k[1])], []), [cq, ckv],
                               [w["q_a_norm"], w["kv_a_norm"]], [(256, BF16), (128, BF16)], name="mla_lat_fwd")
    q_up = _matmul(q_lat, w["uq"], "nn", F32, "mla_q_fwd")
    k_up = _matmul(kv_lat, w["k"], "nn", F32, "mla_k_fwd")
    v_mla = _matmul(kv_lat, w["v"], "nn", BF16, "mla_v_fwd")

    def qk_body(r, k):
        qs, ks = [], []
        for qh, kh in zip(_heads(r[0], MLA_HEADS), _heads(r[1], MLA_HEADS)):
            a, b = _qk_head(qh, kh, r[2], r[3], r[4], r[5], k[0], k[1])
            qs.append(a)
            ks.append(b)
        return [_cat(qs), _cat(ks)], []

    tabs = [_row(c), _row(sa), _row(sb)]
    q_r, k_r = _rows_call(qk_body, [_row(q_up), _row(k_up), kpe] + tabs, [w["q_norm"], w["k_norm"]],
                          [(1024, BF16), (1024, BF16)], name="mla_qk_fwd")
    o_mla, lse, gathered = _attn_fwd(q_r, k_r, v_mla, _gather_plan(late_shards))
    w.update(_late_layout(dict(zip(LATE, gathered, strict=True))))

    def mix_body(r, k):
        ys = [_mix_head(o, g_, k[0]) for o, g_ in zip(_heads(r[0], GLA_HEADS), _heads(r[1], GLA_HEADS))]
        return [_cat(ys + [r[2]])], []

    cat = _rows_call(mix_body, [_row(o_gla), og, _row(o_mla)], [w["gla_out_norm"]], [(1024, BF16)],
                     name="mix_fwd")[0]
    h1 = _matmul(cat, w["out"], "nn", F32, "out_fwd", residual=x)

    hn = _norm_fwd(h1, w["norm_xa"], "norm_xa_fwd")
    mn = _norm_fwd(mem, w["norm_mem"], "norm_mem_fwd")
    xq = _matmul(hn, w["xq"], "nn", F32, "xa_q_fwd")
    xkv = _matmul(mn, w["xkv"], "nn", F32, "xa_kv_fwd")

    def xa_body(r, k):
        ks, vs = _heads(k[0], 2 * XA_HEADS)[:XA_HEADS], _heads(k[0], 2 * XA_HEADS)[XA_HEADS:]
        return [_cat([_xa_head(a, b, v_, k[1], k[2]) for a, b, v_ in zip(_heads(r[0], XA_HEADS), ks, vs)])], []

    xo = _rows_call(xa_body, [_row(xq)], [xkv, w["xa_q_norm"], w["xa_k_norm"]], [(512, BF16)], name="xa_fwd")[0]
    h2 = _matmul(xo, w["xo"], "nn", F32, "xa_o_fwd", residual=h1)

    fn = _norm_fwd(h2, w["norm_ffn"], "norm_ffn_fwd")
    gg = _matmul(fn, w["wg"], "nt", BF16, "ffn_gate_fwd", b_lead="p")
    uu = _matmul(fn, w["wu"], "nt", BF16, "ffn_up_fwd", b_lead="p")
    act = _conv_fwd(gg, uu, w["cw"], w["cb"])
    y = _matmul(act, w["wd"], "nn", F32, "ffn_down_fwd", residual=h2, a_lead="k", b_lead="k")

    def loss_body(r, k):
        err = r[0] - r[1]
        part = 0.5 * jnp.sum(jnp.sum(err * err, axis=1, keepdims=True) * (1.0 / D_MODEL), axis=0, keepdims=True)
        return [err * (1.0 / D_MODEL)], [jnp.broadcast_to(part, (1, LANES))]

    dy, loss = _rows_call(loss_body, [_row(y), _row(target)], [], [(D_MODEL, F32)], [(1, LANES)], name="loss")

    g["ffn_w_down"] = _matmul(act, dy, "tn", BF16, "ffn_down_dw", a_lead="p")
    dact = _matmul(dy, w["wd"], "nt", BF16, "ffn_down_dx", b_lead="p")
    duu, dgc, g["ffn_conv_w"], g["ffn_conv_b"] = _conv_bwd_gate(gg, uu, dact, w["cw"], w["cb"])
    dgg = _conv_bwd_taps(dgc, w["cw"])
    g["ffn_w_gate"] = _matmul(dgg, fn, "tn", BF16, "ffn_gate_dw", a_lead="p")
    g["ffn_w_up"] = _matmul(duu, fn, "tn", BF16, "ffn_up_dw", a_lead="p")
    dfn = _matmul(dgg, w["wg"], "nn", F32, "ffn_gate_dx", a_lead="k", b_lead="k")
    dfn = _matmul(duu, w["wu"], "nn", F32, "ffn_up_dx", residual=dfn, a_lead="k", b_lead="k")
    dh2, g["norm_ffn"] = _norm_bwd(h2, w["norm_ffn"], dfn, dy, "norm_ffn_bwd")

    g["xa_w_o"] = _matmul(xo, dh2, "tn", BF16, "xa_o_dw")
    dxo = _matmul(dh2, w["xo"], "nt", F32, "xa_o_dx")

    def xa_bwd(r, k):
        kvh = _heads(k[0], 2 * XA_HEADS)
        dq_, dk_, dv_ = [], [], []
        dqn, dkn = 0.0, 0.0
        for h, (a, d_) in enumerate(zip(_heads(r[0], XA_HEADS), _heads(r[1], XA_HEADS))):
            _, vjp = jax.vjp(_xa_head, a, kvh[h], kvh[XA_HEADS + h], k[1], k[2])
            ga, gk, gv, gqn, gkn = vjp(d_)
            dq_.append(ga)
            dk_.append(gk)
            dv_.append(gv)
            dqn, dkn = dqn + gqn, dkn + gkn
        return [_cat(dq_)], [_cat(dk_ + dv_), dqn, dkn]

    dxq, dxkv, g["xa_q_norm"], g["xa_k_norm"] = _rows_call(
        xa_bwd, [_row(xq), _row(dxo)], [xkv, w["xa_q_norm"], w["xa_k_norm"]], [(512, BF16)],
        [xkv.shape, (1, 128), (1, 128)], name="xa_bwd")
    g["xa_w_q"] = _matmul(hn, dxq, "tn", BF16, "xa_q_dw")
    dhn = _matmul(dxq, w["xq"], "nt", F32, "xa_q_dx")
    g["xa_w_kv"] = _matmul(mn, dxkv, "tn", BF16, "xa_kv_dw")
    dmn = _matmul(dxkv, w["xkv"], "nt", F32, "xa_kv_dx")
    _, g["norm_mem"] = _norm_bwd(mem, w["norm_mem"], dmn, dmn, "norm_mem_bwd")
    dh1, g["norm_xa"] = _norm_bwd(h1, w["norm_xa"], dhn, dh2, "norm_xa_bwd")

    g["w_out"] = _matmul(cat, dh1, "tn", BF16, "out_dw")
    dcat = _matmul(dh1, w["out"], "nt", F32, "out_dx")

    def mix_bwd(r, k):
        do_, dog_ = [], []
        dgn = 0.0
        for o, g_, d_ in zip(_heads(r[0], GLA_HEADS), _heads(r[1], GLA_HEADS), _heads(r[2], GLA_HEADS)):
            _, vjp = jax.vjp(_mix_head, o, g_, k[0])
            a, b, gn_ = vjp(d_)
            do_.append(a)
            dog_.append(b)
            dgn = dgn + gn_
        return [_cat(do_), _cat(dog_)], [dgn]

    do_gla, d_og, g["gla_out_norm"] = _rows_call(mix_bwd, [_row(o_gla), og, _row(dcat, 512, 0)], [w["gla_out_norm"]],
                                                 [(512, F32), (512, BF16)], [(1, 128)], name="mix_bwd")

    late_parts = _late_grad_shards(g)
    dq_r, dk_r, dv_mla, lands_mlp = _attn_bwd(q_r, k_r, v_mla, o_mla, lse, dcat,
                                              _scatter_plan([late_parts[n] for n in LATE_MLP]))

    def qk_bwd(r, k):
        dqs, dks = [], []
        dkpe, dqn, dkn = 0.0, 0.0, 0.0
        for qh, kh, dqh, dkh in zip(_heads(r[0], MLA_HEADS), _heads(r[1], MLA_HEADS), _heads(r[6], MLA_HEADS),
                                    _heads(r[7], MLA_HEADS)):
            _, vjp = jax.vjp(lambda a, b, e, f, h_: _qk_head(a, b, e, r[3], r[4], r[5], f, h_), qh, kh, r[2], k[0], k[1])
            ga, gb, ge, gf, gh = vjp((dqh, dkh))
            dqs.append(ga)
            dks.append(gb)
            dkpe, dqn, dkn = dkpe + ge, dqn + gf, dkn + gh
        return [_cat(dqs), _cat(dks), dkpe], [dqn, dkn]

    dq_up, dk_up, d_kpe, g["q_norm"], g["k_norm"] = _rows_call(
        qk_bwd, [_row(q_up), _row(k_up), kpe] + tabs + [_row(dq_r), _row(dk_r)], [w["q_norm"], w["k_norm"]],
        [(1024, BF16), (1024, BF16), (128, BF16)], [(1, 128), (1, 128)], name="mla_qk_bwd")
    g["uq"] = _matmul(q_lat, dq_up, "tn", BF16, "mla_q_dw")
    dq_lat = _matmul(dq_up, w["uq"], "nt", F32, "mla_q_dx")
    g["k"] = _matmul(kv_lat, dk_up, "tn", BF16, "mla_k_dw")
    g["v"] = _matmul(kv_lat, dv_mla, "tn", BF16, "mla_v_dw")
    dkv_lat = _matmul(dk_up, w["k"], "nt", F32, "mla_k_dx")
    dkv_lat = _matmul(dv_mla, w["v"], "nt", F32, "mla_v_dx", residual=dkv_lat)

    def lat_bwd(r, k):
        _, vjp1 = jax.vjp(_rms, r[0], k[0])
        _, vjp2 = jax.vjp(_rms, r[1], k[1])
        a, ga = vjp1(r[2])
        b, gb = vjp2(r[3])
        return [a, b], [ga, gb]

    d_cq, d_ckv, g["mla_q_a_norm"], g["mla_kv_a_norm"] = _rows_call(
        lat_bwd, [cq, ckv, _row(dq_lat), _row(dkv_lat)], [w["q_a_norm"], w["kv_a_norm"]],
        [(256, BF16), (128, BF16)], [(1, 256), (1, 128)], name="mla_lat_bwd")

    dgq, dgk, dla, dgv, lands_mix = _gla_bwd(proj, la, states, do_gla, _scatter_plan([late_parts[n] for n in LATE_MIX]))
    lands_late = dict(zip(LATE_MLP + LATE_MIX, list(lands_mlp) + list(lands_mix), strict=True))

    def gate_bwd(r, k):
        _, vjp = jax.vjp(_gate_fn, r[0], k[0], k[1])
        a, gw, gb = vjp(r[1])
        return [a], [gw, gb]

    d_alr, g["w2"], g["gla_gate_b"] = _rows_call(gate_bwd, [alr, _row(dla)], [w["w2"], w["gate_b"]], [(128, BF16)],
                                                 [(128, 256), (1, 256)], name="gla_gate_bwd")

    dproj = jnp.concatenate([dgq.astype(BF16), dgk.astype(BF16), dgv.astype(BF16), d_og, d_cq, d_ckv, d_kpe, d_alr],
                            axis=1)
    g["in"] = _matmul(xn, dproj, "tn", BF16, "proj_dw")
    dxn = _matmul(dproj, w["in"], "nt", F32, "proj_dx")
    dx, g["norm_mix"] = _norm_bwd(x, w["norm_mix"], dxn, dh1, "norm_mix_bwd")
    return loss[0, 0], dx, g, lands_late


def _join_shards(pieces, axis):
    if axis == 0:
        return pieces.reshape(-1, pieces.shape[2])
    return jnp.transpose(pieces, (1, 0, 2)).reshape(pieces.shape[1], -1)


def _split_shards(full, axis):
    r, c = full.shape
    if axis == 0:
        return full.reshape(4, r // 4, c)
    return jnp.transpose(full.reshape(r, 4, c // 4), (1, 0, 2))


def _early_layout(gath, rep):
    w_in = _join_shards(gath["w_in"], 1)
    z = lambda n: jnp.zeros((D_MODEL, n), w_in.dtype)
    seg = lambda lo, n: w_in[:, lo:lo + n]
    ukv = _join_shards(gath["mla_w_ukv"], 1).reshape(MLA_KV_RANK, MLA_HEADS, MLA_NOPE + MLA_V)
    w = {
        "in": jnp.concatenate([seg(N_GQ, 256), seg(N_GK, 256), seg(N_GV, 512), seg(N_OG, 512), seg(N_CQ, 256),
                               seg(N_CKV, 128), z(64), seg(N_KPE, 32), z(32), seg(N_ALR, 16), z(112)], axis=1),
        "uq": jnp.pad(_join_shards(gath["mla_w_uq"], 1).reshape(MLA_Q_RANK, MLA_HEADS, MLA_QK),
                      ((0, 0), (0, 0), (0, LANES - MLA_QK))).reshape(MLA_Q_RANK, MLA_HEADS * LANES),
        "k": jnp.pad(ukv[:, :, :MLA_NOPE], ((0, 0), (0, 0), (0, LANES - MLA_NOPE))).reshape(MLA_KV_RANK, -1),
        "v": ukv[:, :, MLA_NOPE:].reshape(MLA_KV_RANK, MLA_HEADS * MLA_V),
        "w2": jnp.pad(_join_shards(gath["gla_gate_w2"], 1), ((0, LANES - GLA_RANK), (0, 0))),
        "cb": rep["ffn_conv_b"].reshape(4, 1, D_FF // 4),
        "q_norm": jnp.pad(rep["mla_q_norm"], ((0, 0), (0, LANES - MLA_QK))),
        "k_norm": jnp.pad(rep["mla_k_norm"], ((0, 0), (0, LANES - MLA_QK))),
        "q_a_norm": rep["mla_q_a_norm"], "kv_a_norm": rep["mla_kv_a_norm"], "gate_b": rep["gla_gate_b"],
    }
    for n in ("norm_mix", "gla_out_norm", "norm_xa", "norm_mem", "xa_q_norm", "xa_k_norm", "norm_ffn"):
        w[n] = rep[n]
    return w


def _late_layout(gath):
    return {"out": _join_shards(gath["w_out"], 0), "xq": _join_shards(gath["xa_w_q"], 0),
            "xkv": _join_shards(gath["xa_w_kv"], 0), "xo": _join_shards(gath["xa_w_o"], 1),
            "wg": gath["ffn_w_gate"], "wu": gath["ffn_w_up"], "wd": gath["ffn_w_down"], "cw": gath["ffn_conv_w"]}


def _late_grad_shards(g):
    sh = {"w_out": _split_shards(g["w_out"], 0), "xa_w_q": _split_shards(g["xa_w_q"], 0),
          "xa_w_kv": _split_shards(g["xa_w_kv"], 0), "xa_w_o": _split_shards(g["xa_w_o"], 1),
          "ffn_w_gate": g["ffn_w_gate"], "ffn_w_up": g["ffn_w_up"], "ffn_conv_w": g["ffn_conv_w"],
          "ffn_w_down": g["ffn_w_down"]}
    return {n: v.astype(BF16) for n, v in sh.items()}


def _early_grad_shards(g):
    gi = g["in"]
    seg = lambda lo, n: gi[:, lo:lo + n]
    w_in = jnp.concatenate([seg(P_GQ, 256), seg(P_GK, 256), seg(P_GV, 512), seg(P_ALR, 16), seg(P_OG, 512),
                            seg(P_CQ, 256), seg(P_CKV, 128), seg(P_KPE + 64, 32)], axis=1)
    uq = g["uq"].reshape(MLA_Q_RANK, MLA_HEADS, LANES)[:, :, :MLA_QK].reshape(MLA_Q_RANK, -1)
    ukv = jnp.concatenate([g["k"].reshape(MLA_KV_RANK, MLA_HEADS, LANES)[:, :, :MLA_NOPE],
                           g["v"].reshape(MLA_KV_RANK, MLA_HEADS, MLA_V)], axis=2).reshape(MLA_KV_RANK, -1)
    sh = {"w_in": _split_shards(w_in, 1), "gla_gate_w2": _split_shards(g["w2"][:GLA_RANK], 1),
          "mla_w_uq": _split_shards(uq, 1), "mla_w_ukv": _split_shards(ukv, 1)}
    sh = {n: v.astype(BF16) for n, v in sh.items()}
    rep = {n: g[n] for n in REPLICATED if n in g}
    rep["mla_q_norm"] = g["q_norm"][:, :MLA_QK]
    rep["mla_k_norm"] = g["k_norm"][:, :MLA_QK]
    rep["ffn_conv_b"] = g["ffn_conv_b"].reshape(1, D_FF)
    return sh, rep


SMALL_SHAPE = (8, 1024)


def _pack_small(vectors):
    flat = jnp.concatenate(vectors, axis=1)
    return jnp.pad(flat, ((0, 0), (0, SMALL_SHAPE[0] * SMALL_SHAPE[1] - flat.shape[1]))).reshape(SMALL_SHAPE)


def _unpack_small(buf, widths):
    flat = buf.reshape(1, -1)
    out, off = [], 0
    for wd in widths:
        out.append(flat[:, off:off + wd])
        off += wd
    return out


ANY = pl.BlockSpec(memory_space=pl.ANY)


def _place():
    x, y, c = lax.axis_index("x"), lax.axis_index("y"), lax.axis_index("c")
    chips = [(1 - x, y), (x, 1 - y), (1 - x, 1 - y)]
    return x, y, c, chips


class _Comm:
    def __init__(self, ins, out_shape, sems, start, finish, mid=None):
        self.ins, self.out_shape, self.sems = list(ins), list(out_shape), list(sems)
        self.start, self.finish, self.mid = start, finish, mid or (lambda *args: None)


def _run_comm(plan, name):
    ni, no = len(plan.ins), len(plan.out_shape)

    def body(*refs):
        ins, outs, sems = refs[:ni], refs[ni:ni + no], refs[ni + no:]
        place = _place()
        plan.start(place, ins, outs, sems)
        plan.mid(place, ins, outs, sems)
        plan.finish(place, ins, outs, sems)

    return pl.pallas_call(body, in_specs=[ANY] * ni, out_specs=[ANY] * no, out_shape=plan.out_shape,
                          scratch_shapes=plan.sems, name=name)(*plan.ins)


def _gather_plan(shards):
    n = len(shards)
    split = [s.shape[0] % (2 * BF16_ROWS) == 0 for s in shards]

    def rows(ref, t, c):
        if not split[t]:
            return ref
        half = shards[t].shape[0] // 2
        return ref.at[pl.ds(pl.multiple_of(c * half, BF16_ROWS), half)]

    def remote(src, dst, ss, rs, to):
        return pltpu.make_async_remote_copy(src_ref=src, dst_ref=dst, send_sem=ss, recv_sem=rs, device_id=to,
                                            device_id_type=MESH)

    def first_wave(place, ins, outs, sems):
        x, y, c, chips = place
        ici_s, ici_r, _, _, local = sems
        me = 2 * x + y
        own = [pltpu.make_async_copy(ins[t], outs[t].at[me], local.at[t]) for t in range(n)]
        push = [remote(rows(ins[t], t, c), rows(outs[t].at[me], t, c), ici_s.at[3 * t + j], ici_r.at[3 * t + j], (px, py, c))
                for t in range(n) for j, (px, py) in enumerate(chips)]
        return own, push

    def second_wave(place, ins, outs, sems, last):
        x, y, c, chips = place
        ici_s, ici_r, d2d_s, d2d_r, local = sems
        sib = (x, y, 1 - c)
        out = []
        for t in range(n):
            for j, (px, py) in enumerate(chips):
                block = outs[t].at[2 * px + py]
                got = rows(block, t, c)
                if split[t]:
                    hand = remote(got, got, d2d_s.at[3 * t + j], d2d_r.at[3 * t + j], sib)
                    theirs = rows(block, t, 1 - c)
                    other = (remote(theirs, theirs, local.at[0], d2d_r.at[3 * t + j], sib) if last else
                             remote(got, got, local.at[0], ici_r.at[3 * t + j], sib))
                    out.append((other, hand))
                elif last:
                    out.append((remote(got, got, local.at[0], ici_r.at[3 * t + j], sib), None))
        return out

    def start(place, ins, outs, sems):
        own, push = first_wave(place, ins, outs, sems)
        for cp in own + push:
            cp.start()

    def mid(place, ins, outs, sems):
        for arrival, hand in second_wave(place, ins, outs, sems, False):
            arrival.wait_recv()
            hand.start()

    def finish(place, ins, outs, sems):
        own, push = first_wave(place, ins, outs, sems)
        for arrival, hand in second_wave(place, ins, outs, sems, True):
            arrival.wait_recv()
            if hand is not None:
                hand.wait_send()
        for cp in push:
            cp.wait_send()
        for cp in own:
            cp.wait()

    dma = pltpu.SemaphoreType.DMA
    return _Comm(shards, [jax.ShapeDtypeStruct((4,) + s.shape, s.dtype) for s in shards],
                 [dma((3 * n,)), dma((3 * n,)), dma((3 * n,)), dma((3 * n,)), dma((n,))], start, finish, mid)


def _scatter_plan(parts, small=None):
    n = len(parts)
    ns = 0 if small is None else 1

    def unpack(place, ins, outs, sems):
        x, y, c, chips = place
        return x, y, c, chips, 2 * x + y, 4 * x + 2 * y + c, (x, y, 1 - c)

    def remote(src, dst, ss, rs, to):
        return pltpu.make_async_remote_copy(src_ref=src, dst_ref=dst, send_sem=ss, recv_sem=rs, device_id=to,
                                            device_id_type=MESH)

    def first_wave(place, ins, outs, sems):
        x, y, c, chips, me, dev, sib = unpack(place, ins, outs, sems)
        ici_s, ici_r, d2d_s, d2d_r, sm_s, sm_r, local = sems
        own, push = [], []
        if ns:
            own.append(pltpu.make_async_copy(ins[n], outs[n].at[dev], local.at[n]))
            for k in range(1, 8):
                px = (1 - x) if (k >> 2) & 1 else x
                py = (1 - y) if (k >> 1) & 1 else y
                pc = (1 - c) if k & 1 else c
                push.append(remote(ins[n], outs[n].at[dev], sm_s.at[k - 1], sm_r.at[k - 1], (px, py, pc)))
        for t in range(n):
            own.append(pltpu.make_async_copy(ins[t].at[me], outs[t].at[dev], local.at[t]))
            push.append(remote(ins[t].at[me], outs[t].at[dev], d2d_s.at[4 * t], d2d_r.at[4 * t], sib))
            for j, (px, py) in enumerate(chips):
                push.append(remote(ins[t].at[2 * px + py], outs[t].at[dev], ici_s.at[3 * t + j], ici_r.at[3 * t + j],
                                   (px, py, c)))
        return own, push

    def start(place, ins, outs, sems):
        own, push = first_wave(place, ins, outs, sems)
        for cp in own + push:
            cp.start()

    def landed(dst, rs, sems, sib):
        remote(dst, dst, sems[-1].at[0], rs, sib).wait_recv()

    def forwards(place, ins, outs, sems):
        x, y, c, chips, me, dev, sib = unpack(place, ins, outs, sems)
        d2d_s, d2d_r = sems[2], sems[3]
        slots = [(t, j, outs[t].at[4 * px + 2 * py + c]) for t in range(n) for j, (px, py) in enumerate(chips)]
        return [(t, j, slot, remote(slot, slot, d2d_s.at[4 * t + 1 + j], d2d_r.at[4 * t + 1 + j], sib))
                for t, j, slot in slots]

    def mid(place, ins, outs, sems):
        sib = unpack(place, ins, outs, sems)[-1]
        for t, j, slot, cp in forwards(place, ins, outs, sems):
            landed(slot, sems[1].at[3 * t + j], sems, sib)
            cp.start()

    def finish(place, ins, outs, sems):
        x, y, c, chips, me, dev, sib = unpack(place, ins, outs, sems)
        d2d_r, sm_r = sems[3], sems[5]
        own, push = first_wave(place, ins, outs, sems)
        push += [cp for _, _, _, cp in forwards(place, ins, outs, sems)]
        for t in range(n):
            landed(outs[t].at[4 * x + 2 * y + (1 - c)], d2d_r.at[4 * t], sems, sib)
            for j, (px, py) in enumerate(chips):
                landed(outs[t].at[4 * px + 2 * py + (1 - c)], d2d_r.at[4 * t + 1 + j], sems, sib)
        if ns:
            for k in range(1, 8):
                px = (1 - x) if (k >> 2) & 1 else x
                py = (1 - y) if (k >> 1) & 1 else y
                pc = (1 - c) if k & 1 else c
                landed(outs[n].at[4 * px + 2 * py + pc], sm_r.at[k - 1], sems, sib)
        for cp in push:
            cp.wait_send()
        for cp in own:
            cp.wait()

    dma = pltpu.SemaphoreType.DMA
    ins = list(parts) + ([small] if ns else [])
    out_shape = [jax.ShapeDtypeStruct((8,) + p.shape[1:], p.dtype) for p in parts]
    if ns:
        out_shape.append(jax.ShapeDtypeStruct((8,) + small.shape, small.dtype))
    return _Comm(ins, out_shape, [dma((3 * n,)), dma((3 * n,)), dma((4 * n,)), dma((4 * n,)), dma((7,)), dma((7,)),
                                  dma((n + 1,))], start, finish, mid)


def _row_tile(r, cap=256):
    if r <= cap:
        return r
    return max(t for t in range(8, cap + 1, 8) if r % t == 0)


def _adamw(w, m, v, land, name):
    r, c = w.shape
    t = _row_tile(r)

    def kern(w_ref, m_ref, v_ref, l_ref, g_out, d_out, m_out, v_out):
        g = l_ref[0].astype(F32)
        for i in range(1, 8):
            g = g + l_ref[i].astype(F32)
        m_new = ADAM_B1 * m_ref[...] + (1.0 - ADAM_B1) * g
        v_new = ADAM_B2 * v_ref[...] + (1.0 - ADAM_B2) * (g * g)
        m_hat = m_new / (1.0 - ADAM_B1 ** ADAM_STEP)
        v_hat = v_new / (1.0 - ADAM_B2 ** ADAM_STEP)
        g_out[...] = g
        d_out[...] = -ADAM_LR * (m_hat / (jnp.sqrt(v_hat) + ADAM_EPS) + ADAM_WD * w_ref[...])
        m_out[...] = m_new
        v_out[...] = v_new

    spec = pl.BlockSpec((t, c), lambda i: (i, 0))
    return pl.pallas_call(
        kern, grid=(r // t,), in_specs=[spec] * 3 + [pl.BlockSpec((8, t, c), lambda i: (0, i, 0))], out_specs=[spec] * 4,
        out_shape=[jax.ShapeDtypeStruct((r, c), F32)] * 4,
        compiler_params=pltpu.CompilerParams(dimension_semantics=("parallel",), vmem_limit_bytes=VMEM_LIMIT),
        name=name)(w, m, v, land)


def _step(a):
    def sq(n):
        v = a[n][0] if a[n].ndim == 3 else a[n]
        return v.T if n.removeprefix("m_").removeprefix("v_") in TRANSPOSED else v

    payload = lambda n: sq(n) if n in EXACT_GATHER else sq(n).astype(BF16)

    gathered = _run_comm(_gather_plan([payload(n) for n in EARLY]), "gather_early")
    w = _early_layout(dict(zip(EARLY, gathered, strict=True)), {n: a[n] for n in REPLICATED})

    loss, dx, g, lands_late = _local_step(sq("x"), sq("mem"), a["positions"][0], sq("loss_target"), w,
                                          [payload(n) for n in LATE])

    sh, rep = _early_grad_shards(g)
    small = _pack_small([rep[n] for n in REPLICATED] + [loss.reshape(1, 1)])
    *lands_early, land_small = _run_comm(_scatter_plan([sh[n] for n in EARLY], small), "scatter_last")
    lands = dict(zip(EARLY, lands_early, strict=True)) | lands_late

    outs = {}
    kinds = ("grad_", "delta_", "new_m_", "new_v_")
    for n, _ in SHARDED:
        res = _adamw(sq(n), sq("m_" + n), sq("v_" + n), lands[n], "adamw_" + n)
        for kind, val in zip(kinds, res, strict=True):
            outs[kind + n] = (val.T if n in TRANSPOSED else val).reshape(a[n].shape)
    zero = jnp.zeros((1, 1), F32)
    packed = [_pack_small([a[p + n] for n in REPLICATED] + [zero]) for p in ("", "m_", "v_")]
    res = _adamw(*packed, land_small, "adamw_replicated")
    widths = [a[n].shape[1] for n in REPLICATED] + [1]
    for kind, buf in zip(kinds, res, strict=True):
        *vals, total = _unpack_small(buf, widths)
        for n, val in zip(REPLICATED, vals, strict=True):
            outs[kind + n] = val
        if kind == "grad_":
            loss = total[0, 0]

    ordered = [outs[kind + n] for kind in kinds for n in WEIGHTS]
    return (loss, dx[None], *ordered)


def kernel(x, mem, positions, norm_mix, w_in, gla_gate_w2, gla_gate_b, gla_out_norm, mla_q_a_norm, mla_w_uq, mla_kv_a_norm, mla_w_ukv, mla_q_norm, mla_k_norm, w_out, norm_xa, norm_mem, xa_w_q, xa_w_kv, xa_q_norm, xa_k_norm, xa_w_o, norm_ffn, ffn_w_gate, ffn_w_up, ffn_conv_w, ffn_conv_b, ffn_w_down, loss_target, m_norm_mix, m_w_in, m_gla_gate_w2, m_gla_gate_b, m_gla_out_norm, m_mla_q_a_norm, m_mla_w_uq, m_mla_kv_a_norm, m_mla_w_ukv, m_mla_q_norm, m_mla_k_norm, m_w_out, m_norm_xa, m_norm_mem, m_xa_w_q, m_xa_w_kv, m_xa_q_norm, m_xa_k_norm, m_xa_w_o, m_norm_ffn, m_ffn_w_gate, m_ffn_w_up, m_ffn_conv_w, m_ffn_conv_b, m_ffn_w_down, v_norm_mix, v_w_in, v_gla_gate_w2, v_gla_gate_b, v_gla_out_norm, v_mla_q_a_norm, v_mla_w_uq, v_mla_kv_a_norm, v_mla_w_ukv, v_mla_q_norm, v_mla_k_norm, v_w_out, v_norm_xa, v_norm_mem, v_xa_w_q, v_xa_w_kv, v_xa_q_norm, v_xa_k_norm, v_xa_w_o, v_norm_ffn, v_ffn_w_gate, v_ffn_w_up, v_ffn_conv_w, v_ffn_conv_b, v_ffn_w_down):
    return _step(dict(locals()))
```

```python
import functools

import jax
import jax.numpy as jnp
from jax import lax
from jax.experimental import pallas as pl
from jax.experimental.pallas import tpu as pltpu

F32, BF16 = jnp.float32, jnp.bfloat16
MESH = pl.DeviceIdType.MESH

D_MODEL = 1024
EPS = 1e-6
GLA_HEADS, GLA_DK, GLA_DV, GLA_RANK, GLA_CHUNK = 4, 64, 128, 16, 64
GLA_GATE_NORM = 16.0
MLA_HEADS, MLA_Q_RANK, MLA_KV_RANK, MLA_NOPE, MLA_ROPE, MLA_V = 8, 256, 128, 64, 32, 64
MLA_QK = MLA_NOPE + MLA_ROPE
ROPE_THETA = 10000.0
LOG2E, LN2 = 1.4426950408889634, 0.6931471805599453
XA_HEADS, XA_DIM = 4, 128
D_FF = 2816
ADAM_LR, ADAM_B1, ADAM_B2, ADAM_EPS, ADAM_WD, ADAM_STEP = 0.001, 0.9, 0.999, 1e-08, 0.01, 10

LANES = 128
BF16_ROWS = 16
VMEM_LIMIT = 56 * 1024 * 1024
MATMUL_VMEM = 44 * 1024 * 1024

P_GQ, P_GK, P_GV, P_OG, P_CQ, P_CKV, P_KPE, P_ALR, P_WIDTH = 0, 256, 512, 1024, 1536, 1792, 1920, 2048, 2176
N_GQ, N_GK, N_GV, N_ALR, N_OG, N_CQ, N_CKV, N_KPE, N_WIDTH = 0, 256, 512, 1024, 1040, 1552, 1808, 1936, 1968

SHARDED = (("w_in", 1), ("gla_gate_w2", 1), ("mla_w_uq", 1), ("mla_w_ukv", 1), ("w_out", 0), ("xa_w_q", 0),
           ("xa_w_kv", 0), ("xa_w_o", 1), ("ffn_w_gate", 1), ("ffn_w_up", 1), ("ffn_conv_w", 1), ("ffn_w_down", 0))
REPLICATED = ("norm_mix", "gla_gate_b", "gla_out_norm", "mla_q_a_norm", "mla_kv_a_norm", "mla_q_norm", "mla_k_norm",
              "norm_xa", "norm_mem", "xa_q_norm", "xa_k_norm", "norm_ffn", "ffn_conv_b")
EXACT_GATHER = ("gla_gate_w2", "ffn_conv_w")
TRANSPOSED = ("ffn_w_gate", "ffn_w_up")
EARLY = ("w_in", "gla_gate_w2", "mla_w_uq", "mla_w_ukv")
LATE = tuple(n for n, _ in SHARDED if n not in EARLY)
LATE_MLP = tuple(n for n in LATE if n.startswith("ffn_"))
LATE_MIX = tuple(n for n in LATE if not n.startswith("ffn_"))
WEIGHTS = ("norm_mix", "w_in", "gla_gate_w2", "gla_gate_b", "gla_out_norm", "mla_q_a_norm", "mla_w_uq",
           "mla_kv_a_norm", "mla_w_ukv", "mla_q_norm", "mla_k_norm", "w_out", "norm_xa", "norm_mem", "xa_w_q",
           "xa_w_kv", "xa_q_norm", "xa_k_norm", "xa_w_o", "norm_ffn", "ffn_w_gate", "ffn_w_up", "ffn_conv_w",
           "ffn_conv_b", "ffn_w_down")


_NN = ((1,), (0,))
_NT = ((1,), (1,))
_TN = ((0,), (0,))


def _dg(a, b, dims):
    return lax.dot_general(a.astype(BF16), b.astype(BF16), (dims, ((), ())), preferred_element_type=F32)


@jax.custom_vjp
def _dot_nn(a, b):
    return _dg(a, b, _NN)


_dot_nn.defvjp(lambda a, b: (_dg(a, b, _NN), (a, b)),
               lambda r, g: (_dg(g, r[1], _NT).astype(r[0].dtype), _dg(r[0], g, _TN).astype(r[1].dtype)))


@jax.custom_vjp
def _dot_nt(a, b):
    return _dg(a, b, _NT)


_dot_nt.defvjp(lambda a, b: (_dg(a, b, _NT), (a, b)),
               lambda r, g: (_dg(g, r[1], _NN).astype(r[0].dtype), _dg(g, r[0], _TN).astype(r[1].dtype)))


@jax.custom_vjp
def _dot_tn(a, b):
    return _dg(a, b, _TN)


_dot_tn.defvjp(lambda a, b: (_dg(a, b, _TN), (a, b)),
               lambda r, g: (_dg(r[1], g, _NT).astype(r[0].dtype), _dg(r[0], g, _NN).astype(r[1].dtype)))


def _rms(x, w, n=None):
    n = x.shape[-1] if n is None else n
    ms = jnp.sum(x * x, axis=-1, keepdims=True) * (1.0 / n)
    return x * lax.rsqrt(ms + EPS) * w


def _silu(x):
    return x * jax.nn.sigmoid(x)


def _log_sigmoid(x):
    return jnp.minimum(x, 0.0) - jnp.log(1.0 + jnp.exp(-jnp.abs(x)))


@jax.custom_vjp
def _rope(y, c, sa, sb):
    return y * c + pltpu.roll(y, LANES - 16, 1) * sa + pltpu.roll(y, 16, 1) * sb


def _rope_bwd(res, g):
    c, sa, sb = res
    gy = g * c + pltpu.roll(g * sa, 16, 1) + pltpu.roll(g * sb, LANES - 16, 1)
    return gy, jnp.zeros_like(c), jnp.zeros_like(sa), jnp.zeros_like(sb)


_rope.defvjp(lambda y, c, sa, sb: (_rope(y, c, sa, sb), (c, sa, sb)), _rope_bwd)


def _lane_mask(lo, hi):
    lane = lax.broadcasted_iota(jnp.int32, (1, LANES), 1)
    return ((lane >= lo) & (lane < hi)).astype(F32)


def _tile(n, t):
    t = min(n, t)
    assert n % t == 0, (n, t)
    return t


def _matmul(a, b, mode, out_dtype, name, residual=None, a_lead=None, b_lead=None):
    (a0, a1), (b0, b1) = a.shape[-2:], b.shape[-2:]
    if mode == "nn":
        m, k, k2, n = a0, a1, b0, b1
    elif mode == "nt":
        m, k, n, k2 = a0, a1, b0, b1
    else:
        k, m, k2, n = a0, a1, b0, b1
    assert k == k2, (a.shape, b.shape, mode)
    npar = 4 if "p" in (a_lead, b_lead) else 1
    nsum = 4 if "k" in (a_lead, b_lead) else 1
    a_item, b_item, o_item = a.dtype.itemsize, b.dtype.itemsize, jnp.dtype(out_dtype).itemsize

    def vmem_need(tm, tn, tk):
        need = 2 * (nsum if a_lead == "k" else 1) * tm * tk * a_item + 2 * (nsum if b_lead == "k" else 1) * tk * tn * b_item
        need += 2 * tm * tn * o_item + tm * tn * 4 * (2 if tk < k else 1)
        need += tm * tk * 2 * (a_item == 4 or mode == "tn") + tk * tn * 2 * (b_item == 4)
        return need + (2 * tm * tn * 4 if residual is not None else 0)

    halvings = (4096, 2048, 1024, 512, 256, 128, 64, 32, 16, 8)
    if mode == "tn":
        tm = m if m <= 1408 else m // 2
        tn = n if tm * n <= 1024 * 2304 else n // 2
        tk = next((r for r in halvings if k % r == 0 and vmem_need(tm, tn, r) <= MATMUL_VMEM), k)
    else:
        tn, tk = n, k
        tm = next((r for r in halvings if m % r == 0 and vmem_need(r, tn, tk) <= MATMUL_VMEM), m)
    assert m % tm == 0 and n % tn == 0 and k % tk == 0
    nk = k // tk
    dims = {"nn": _NN, "nt": _NT, "tn": _TN}[mode]

    def body(*refs):
        a_ref, b_ref = refs[0], refs[1]
        r_ref = refs[2] if residual is not None else None
        o_ref = refs[3 if residual is not None else 2]
        prod = None
        for sh in range(nsum):
            term = _dg(a_ref[sh] if a_lead == "k" else a_ref[...], b_ref[sh] if b_lead == "k" else b_ref[...], dims)
            prod = term if prod is None else prod + term
        if nk == 1:
            o_ref[...] = (prod if r_ref is None else prod + r_ref[...]).astype(o_ref.dtype)
            return
        acc = refs[-1]
        kk = pl.program_id(3)

        @pl.when(kk == 0)
        def _():
            acc[...] = prod

        @pl.when(kk > 0)
        def _():
            acc[...] += prod

        @pl.when(kk == nk - 1)
        def _():
            r = acc[...]
            if r_ref is not None:
                r = r + r_ref[...]
            o_ref[...] = r.astype(o_ref.dtype)

    def spec(lead, blk, idx):
        if lead is None:
            return pl.BlockSpec(blk, lambda i, j, p, kk: idx(i, j, kk))
        if lead == "p":
            return pl.BlockSpec((None,) + blk, lambda i, j, p, kk: (p,) + idx(i, j, kk))
        return pl.BlockSpec((nsum,) + blk, lambda i, j, p, kk: (0,) + idx(i, j, kk))

    if mode == "nn":
        in_specs = [spec(a_lead, (tm, tk), lambda i, j, kk: (i, kk)), spec(b_lead, (tk, tn), lambda i, j, kk: (kk, j))]
    elif mode == "nt":
        in_specs = [spec(a_lead, (tm, tk), lambda i, j, kk: (i, kk)), spec(b_lead, (tn, tk), lambda i, j, kk: (j, kk))]
    else:
        in_specs = [spec(a_lead, (tk, tm), lambda i, j, kk: (kk, i)), spec(b_lead, (tk, tn), lambda i, j, kk: (kk, j))]
    args = [a, b]
    if residual is not None:
        assert npar == 1
        in_specs.append(spec(None, (tm, tn), lambda i, j, kk: (i, j)))
        args.append(residual)
    return pl.pallas_call(
        body, grid=(m // tm, n // tn, npar, nk), in_specs=in_specs,
        out_specs=spec("p" if npar > 1 else None, (tm, tn), lambda i, j, kk: (i, j)),
        out_shape=jax.ShapeDtypeStruct(((4,) if npar > 1 else ()) + (m, n), out_dtype),
        scratch_shapes=[pltpu.VMEM((tm, tn), F32)] if nk > 1 else [],
        compiler_params=pltpu.CompilerParams(dimension_semantics=("parallel", "parallel", "parallel", "arbitrary"),
                                             vmem_limit_bytes=VMEM_LIMIT),
        name=name)(*args)


def _row(a, width=None, col_block=0):
    return (a, a.shape[1] if width is None else width, col_block)


def _rows_call(body, rows, consts, outs, accs=(), *, name, tile=512):
    s = rows[0][0].shape[0]
    t = _tile(s, tile)
    nr, nc, no = len(rows), len(consts), len(outs)

    def kern(*refs):
        r = [x[...] for x in refs[:nr]]
        c = [x[...] for x in refs[nr:nr + nc]]
        o_refs = refs[nr + nc:nr + nc + no]
        a_refs = refs[nr + nc + no:]
        ro, ao = body(r, c)
        for ref, val in zip(o_refs, ro, strict=True):
            ref[...] = val.astype(ref.dtype)
        if a_refs:
            @pl.when(pl.program_id(0) == 0)
            def _():
                for ref in a_refs:
                    ref[...] = jnp.zeros_like(ref)

            for ref, val in zip(a_refs, ao, strict=True):
                ref[...] += val

    in_specs = [pl.BlockSpec((t, w), functools.partial(lambda cb, i: (i, cb), cb)) for (_, w, cb) in rows]
    in_specs += [pl.BlockSpec(c.shape, lambda i: (0, 0)) for c in consts]
    out_specs = [pl.BlockSpec((t, w), lambda i: (i, 0)) for (w, _) in outs]
    out_specs += [pl.BlockSpec(shape, lambda i: (0, 0)) for shape in accs]
    out_shape = [jax.ShapeDtypeStruct((s, w), dt) for (w, dt) in outs]
    out_shape += [jax.ShapeDtypeStruct(shape, F32) for shape in accs]
    return pl.pallas_call(
        kern, grid=(s // t,), in_specs=in_specs, out_specs=out_specs, out_shape=out_shape,
        compiler_params=pltpu.CompilerParams(dimension_semantics=("arbitrary" if accs else "parallel",),
                                             vmem_limit_bytes=VMEM_LIMIT),
        name=name)(*[r[0] for r in rows], *consts)


def _gla_chunk(q, k, la, v0, v1, s0, s1):
    c = q.shape[0]
    r = lax.broadcasted_iota(jnp.int32, (c, c), 0)
    cc = lax.broadcasted_iota(jnp.int32, (c, c), 1)
    tril = cc <= r
    cum = lax.dot_general(tril.astype(F32), la, (_NN, ((), ())), precision=lax.Precision.HIGHEST,
                          preferred_element_type=F32)
    cl = jnp.sum(la, axis=0, keepdims=True)
    qd = q * (GLA_DK ** -0.5) * jnp.exp(cum)
    ki = k * jnp.exp(-cum)
    ke = k * jnp.exp(cl - cum)
    dec = jnp.exp(cl)
    outs, news = [], []
    for h, (v, s) in enumerate(((v0, s0), (v1, s1))):
        mk = _lane_mask(GLA_DK * h, GLA_DK * (h + 1))
        qh = qd * mk
        att = jnp.where(tril, _dot_nt(qh, ki), 0.0)
        outs.append(_dot_nn(att, v) + _dot_nt(qh, s))
        news.append(s * dec + _dot_tn(v, ke * mk))
    return outs[0], outs[1], news[0], news[1]


def _gla_specs(tb, rev_nb=None):
    blk = (lambda b: b) if rev_nb is None else (lambda b: rev_nb - 1 - b)
    q = pl.BlockSpec((tb, 128), lambda p, b: (blk(b), P_GQ // 128 + p))
    k = pl.BlockSpec((tb, 128), lambda p, b: (blk(b), P_GK // 128 + p))
    la = pl.BlockSpec((tb, 128), lambda p, b: (blk(b), p))
    v = pl.BlockSpec((tb, 256), lambda p, b: (blk(b), P_GV // 256 + p))
    o = pl.BlockSpec((tb, 256), lambda p, b: (blk(b), p))
    st = pl.BlockSpec((tb // GLA_CHUNK, 2, 128, 128), lambda p, b: (blk(b), p, 0, 0))
    return q, k, la, v, o, st


def _gla_fwd(proj, la):
    s = proj.shape[0]
    tb = _tile(s, 512)
    nb, nch = s // tb, tb // GLA_CHUNK

    def kern(q_ref, k_ref, la_ref, v_ref, o_ref, st_ref, s_sc):
        @pl.when(pl.program_id(1) == 0)
        def _():
            s_sc[...] = jnp.zeros_like(s_sc)

        s0, s1 = s_sc[0], s_sc[1]
        for ci in range(nch):
            sl = slice(ci * GLA_CHUNK, (ci + 1) * GLA_CHUNK)
            st_ref[ci, 0] = s0
            st_ref[ci, 1] = s1
            o0, o1, s0, s1 = _gla_chunk(q_ref[sl, :], k_ref[sl, :], la_ref[sl, :], v_ref[sl, 0:128],
                                        v_ref[sl, 128:256], s0, s1)
            o_ref[sl, 0:128] = o0
            o_ref[sl, 128:256] = o1
        s_sc[0] = s0
        s_sc[1] = s1

    q, k, lasp, v, o, st = _gla_specs(tb)
    return pl.pallas_call(
        kern, grid=(2, nb), in_specs=[q, k, lasp, v], out_specs=[o, st],
        out_shape=[jax.ShapeDtypeStruct((s, 512), F32),
                   jax.ShapeDtypeStruct((s // GLA_CHUNK, GLA_HEADS, 128, 128), F32)],
        scratch_shapes=[pltpu.VMEM((2, 128, 128), F32)],
        compiler_params=pltpu.CompilerParams(dimension_semantics=("parallel", "arbitrary"),
                                             vmem_limit_bytes=VMEM_LIMIT),
        name="gla_fwd")(proj, proj, la, proj)


def _gla_bwd(proj, la, states, d_o, comm):
    s = proj.shape[0]
    tb = _tile(s, 512)
    nb, nch = s // tb, tb // GLA_CHUNK
    nci, nco = len(comm.ins), len(comm.out_shape)

    def kern(*refs):
        (q_ref, k_ref, la_ref, v_ref, do_ref, st_ref), cins, (dq_ref, dk_ref, dla_ref, dv_ref), couts, (ds_sc,), csems = \
            _split_refs(refs, (6, nci, 4, nco, 1, len(comm.sems)))
        place = _place()
        pair, blk = pl.program_id(0), pl.program_id(1)

        @pl.when((pair == 0) & (blk == 0))
        def _():
            comm.start(place, cins, couts, csems)

        @pl.when((pair == 1) & (blk == nb // 2))
        def _():
            comm.mid(place, cins, couts, csems)

        @pl.when(blk == 0)
        def _():
            ds_sc[...] = jnp.zeros_like(ds_sc)

        d0, d1 = ds_sc[0], ds_sc[1]
        for ci in reversed(range(nch)):
            sl = slice(ci * GLA_CHUNK, (ci + 1) * GLA_CHUNK)
            _, vjp = jax.vjp(_gla_chunk, q_ref[sl, :], k_ref[sl, :], la_ref[sl, :], v_ref[sl, 0:128],
                             v_ref[sl, 128:256], st_ref[ci, 0], st_ref[ci, 1])
            gq, gk, gla, gv0, gv1, d0, d1 = vjp((do_ref[sl, 0:128], do_ref[sl, 128:256], d0, d1))
            dq_ref[sl, :] = gq
            dk_ref[sl, :] = gk
            dla_ref[sl, :] = gla
            dv_ref[sl, 0:128] = gv0
            dv_ref[sl, 128:256] = gv1
        ds_sc[0] = d0
        ds_sc[1] = d1

        @pl.when((pair == 1) & (blk == nb - 1))
        def _():
            comm.finish(place, cins, couts, csems)

    q, k, lasp, v, o, st = _gla_specs(tb, rev_nb=nb)
    res = pl.pallas_call(
        kern, grid=(2, nb), in_specs=[q, k, lasp, v, o, st] + [ANY] * nci, out_specs=[lasp, lasp, lasp, o] + [ANY] * nco,
        out_shape=[jax.ShapeDtypeStruct((s, 256), F32), jax.ShapeDtypeStruct((s, 256), F32),
                   jax.ShapeDtypeStruct((s, 256), F32), jax.ShapeDtypeStruct((s, 512), F32)] + comm.out_shape,
        scratch_shapes=[pltpu.VMEM((2, 128, 128), F32)] + comm.sems,
        compiler_params=pltpu.CompilerParams(dimension_semantics=("arbitrary", "arbitrary"),
                                             vmem_limit_bytes=VMEM_LIMIT),
        name="gla_bwd")(proj, proj, la, proj, d_o, states, *comm.ins)
    return res[0], res[1], res[2], res[3], res[4:]


def _causal_keep(t, qi, ki):
    row = lax.broadcasted_iota(jnp.int32, (t, t), 0) + qi * t
    col = lax.broadcasted_iota(jnp.int32, (t, t), 1) + ki * t
    return col <= row


def _split_refs(refs, counts):
    out, off = [], 0
    for cnt in counts:
        out.append(refs[off:off + cnt])
        off += cnt
    return out


def _attn_fwd(q, k, v, comm, tile=1024):
    s = q.shape[0]
    t = _tile(s, tile)
    n = s // t
    nci, nco = len(comm.ins), len(comm.out_shape)

    def kern(*refs):
        (q_ref, k_ref, v_ref), cins, (o_ref, lse_ref), couts, (m_sc, l_sc, acc_sc), csems = _split_refs(
            refs, (3, nci, 2, nco, 3, len(comm.sems)))
        qi, ki = pl.program_id(1), pl.program_id(2)
        place = _place()

        @pl.when((pl.program_id(0) == 0) & (qi == 0) & (ki == 0))
        def _():
            comm.start(place, cins, couts, csems)

        @pl.when((pl.program_id(0) == MLA_HEADS // 2 - 1) & (qi == 0) & (ki == 0))
        def _():
            comm.mid(place, cins, couts, csems)

        first = lax.broadcasted_iota(jnp.int32, (t, LANES), 1) < MLA_V

        @pl.when(ki == 0)
        def _():
            m_sc[...] = jnp.full_like(m_sc, -jnp.inf)
            l_sc[...] = jnp.zeros_like(l_sc)
            acc_sc[...] = jnp.zeros_like(acc_sc)

        def update(diagonal):
            keep = _causal_keep(t, 0, 0)
            alphas, pvs = [], []
            for h in range(2):
                sc = _dg(q_ref[:, 128 * h:128 * (h + 1)], k_ref[:, 128 * h:128 * (h + 1)], _NT)
                if diagonal:
                    sc = jnp.where(keep, sc, -jnp.inf)
                m_prev = m_sc[h]
                m_new = jnp.maximum(m_prev, jnp.max(sc, axis=1, keepdims=True))
                alpha = jnp.exp2(m_prev - m_new)
                p = jnp.exp2(sc - m_new[:, 0:1])
                l_sc[h] = alpha * l_sc[h] + jnp.sum(p, axis=1, keepdims=True)
                m_sc[h] = m_new
                alphas.append(alpha)
                pvs.append(_dg(p, v_ref[...], _NN))
            acc_sc[...] = acc_sc[...] * jnp.where(first, alphas[0], alphas[1]) + jnp.where(first, pvs[0], pvs[1])

        @pl.when(ki < qi)
        def _():
            update(False)

        @pl.when(ki == qi)
        def _():
            update(True)

        @pl.when(ki == qi)
        def _():
            l = jnp.where(first, l_sc[0], l_sc[1])
            m = jnp.where(first, m_sc[0], m_sc[1])
            o_ref[...] = acc_sc[...] / l
            lse_ref[...] = m + jnp.log2(l)

        @pl.when((pl.program_id(0) == MLA_HEADS // 2 - 1) & (qi == n - 1) & (ki == n - 1))
        def _():
            comm.finish(place, cins, couts, csems)

    kv_idx = lambda p, qi, ki: (jnp.minimum(ki, qi), p)
    res = pl.pallas_call(
        kern, grid=(MLA_HEADS // 2, n, n),
        in_specs=[pl.BlockSpec((t, 256), lambda p, qi, ki: (qi, p)), pl.BlockSpec((t, 256), kv_idx),
                  pl.BlockSpec((t, 128), kv_idx)] + [ANY] * nci,
        out_specs=[pl.BlockSpec((t, 128), lambda p, qi, ki: (qi, p)), pl.BlockSpec((t, 128), lambda p, qi, ki: (qi, p))]
        + [ANY] * nco,
        out_shape=[jax.ShapeDtypeStruct((s, 512), F32), jax.ShapeDtypeStruct((s, 512), F32)] + comm.out_shape,
        scratch_shapes=[pltpu.VMEM((2, t, LANES), F32), pltpu.VMEM((2, t, LANES), F32), pltpu.VMEM((t, LANES), F32)]
        + comm.sems,
        compiler_params=pltpu.CompilerParams(dimension_semantics=("arbitrary", "arbitrary", "arbitrary"),
                                             vmem_limit_bytes=VMEM_LIMIT),
        name="mla_attn_fwd")(q, k, v, *comm.ins)
    return res[0], res[1], res[2:]


def _attn_bwd(q, k, v, o, lse, dcat, comm, tile=512):
    s = q.shape[0]
    t = _tile(s, tile)
    n = s // t
    nci, nco = len(comm.ins), len(comm.out_shape)

    def kern(*refs):
        (q_ref, k_ref, v_ref, o_ref, lse_ref, do_ref), cins, (dq_ref, dk_ref, dv_ref), couts, (dk_sc, dv_sc), csems = \
            _split_refs(refs, (6, nci, 3, nco, 2, len(comm.sems)))
        ki, qi = pl.program_id(1), pl.program_id(2)
        place = _place()

        @pl.when((pl.program_id(0) == 0) & (qi == 0) & (ki == 0))
        def _():
            comm.start(place, cins, couts, csems)

        @pl.when((pl.program_id(0) == MLA_HEADS // 2 - 1) & (qi == 0) & (ki == 0))
        def _():
            comm.mid(place, cins, couts, csems)

        @pl.when((ki == 0) & (qi == 0))
        def _():
            dq_ref[...] = jnp.zeros_like(dq_ref)

        @pl.when(qi == ki)
        def _():
            dk_sc[...] = jnp.zeros_like(dk_sc)
            dv_sc[...] = jnp.zeros_like(dv_sc)

        def update(diagonal):
            keep = _causal_keep(t, 0, 0)
            d_o = do_ref[...]
            prod = d_o * o_ref[...]
            rows = pl.ds(pl.multiple_of(qi * t, t), t)
            for h in range(2):
                hs = slice(128 * h, 128 * (h + 1))
                mk = _lane_mask(MLA_V * h, MLA_V * (h + 1))
                qh, kh = q_ref[:, hs], k_ref[:, hs]
                sc = _dg(qh, kh, _NT)
                if diagonal:
                    sc = jnp.where(keep, sc, -jnp.inf)
                p = jnp.exp2(sc - lse_ref[:, MLA_V * h:MLA_V * h + 1])
                doh = d_o * mk
                dp = _dg(doh * LN2, v_ref[...], _NT)
                delta = jnp.sum(prod * mk, axis=1, keepdims=True) * LN2
                ds = p * (dp - delta)
                dv_sc[...] += _dg(p, doh, _TN)
                dk_sc[:, hs] += _dg(ds, qh, _TN)
                dq_ref[rows, hs] += _dg(ds, kh, _NN)

        @pl.when(qi > ki)
        def _():
            update(False)

        @pl.when(qi == ki)
        def _():
            update(True)

        @pl.when(qi == n - 1)
        def _():
            dk_ref[...] = dk_sc[...]
            dv_ref[...] = dv_sc[...].astype(dv_ref.dtype)

        @pl.when((pl.program_id(0) == MLA_HEADS // 2 - 1) & (qi == n - 1) & (ki == n - 1))
        def _():
            comm.finish(place, cins, couts, csems)

    q_idx = lambda p, ki, qi: (jnp.maximum(qi, ki), p)
    res = pl.pallas_call(
        kern, grid=(MLA_HEADS // 2, n, n),
        in_specs=[pl.BlockSpec((t, 256), q_idx), pl.BlockSpec((t, 256), lambda p, ki, qi: (ki, p)),
                  pl.BlockSpec((t, 128), lambda p, ki, qi: (ki, p)), pl.BlockSpec((t, 128), q_idx),
                  pl.BlockSpec((t, 128), q_idx),
                  pl.BlockSpec((t, 128), lambda p, ki, qi: (jnp.maximum(qi, ki), 4 + p))] + [ANY] * nci,
        out_specs=[pl.BlockSpec((s, 256), lambda p, ki, qi: (0, p)), pl.BlockSpec((t, 256), lambda p, ki, qi: (ki, p)),
                   pl.BlockSpec((t, 128), lambda p, ki, qi: (ki, p))] + [ANY] * nco,
        out_shape=[jax.ShapeDtypeStruct((s, 1024), F32), jax.ShapeDtypeStruct((s, 1024), F32),
                   jax.ShapeDtypeStruct((s, 512), BF16)] + comm.out_shape,
        scratch_shapes=[pltpu.VMEM((t, 256), F32), pltpu.VMEM((t, 128), F32)] + comm.sems,
        compiler_params=pltpu.CompilerParams(dimension_semantics=("arbitrary", "arbitrary", "arbitrary"),
                                             vmem_limit_bytes=VMEM_LIMIT),
        name="mla_attn_bwd")(q, k, v, o, lse, dcat, *comm.ins)
    return res[0], res[1], res[2], res[3:]


def _gate_fn(alr, w2, b):
    return _log_sigmoid(_dot_nn(alr, w2) + b) * (1.0 / GLA_GATE_NORM)


def _qk_head(qh, kh, kpe, c, sa, sb, qn, kn):
    kfull = kh + kpe * _lane_mask(MLA_NOPE, MLA_QK)
    q_r = _rope(_rms(qh, qn, MLA_QK), c, sa, sb) * (MLA_QK ** -0.5 * LOG2E)
    k_r = _rope(_rms(kfull, kn, MLA_QK), c, sa, sb)
    return q_r, k_r


def _mix_head(o, og, gn):
    return _rms(o, gn) * _silu(og)


def _xa_head(xq, xk, xv, qn, kn):
    sc = _dot_nt(_rms(xq, qn), _rms(xk, kn)) * (XA_DIM ** -0.5)
    e = jnp.exp(sc - lax.stop_gradient(jnp.max(sc, axis=1, keepdims=True)))
    p = e / jnp.sum(e, axis=1, keepdims=True)
    return _dot_nn(p, xv)


def _heads(x, n):
    return [x[:, 128 * h:128 * (h + 1)] for h in range(n)]


def _cat(xs):
    return jnp.concatenate(xs, axis=1)


def _norm_fwd(x, w, name):
    return _rows_call(lambda r, c: ([_rms(r[0], c[0])], []), [_row(x)], [w], [(x.shape[1], BF16)], name=name)[0]


def _norm_bwd(x, w, d_out, add, name):
    def body(r, c):
        _, vjp = jax.vjp(_rms, r[0], c[0])
        dx, dw = vjp(r[1])
        return [dx + r[2]], [dw]

    return _rows_call(body, [_row(x), _row(d_out), _row(add)], [w], [(x.shape[1], F32)], [w.shape], name=name)


CONV_HALO = BF16_ROWS


def _conv_specs(s, f, t):
    n8 = t // CONV_HALO
    cur = pl.BlockSpec((None, t, f), lambda j, i: (j, i, 0))
    prev = pl.BlockSpec((None, CONV_HALO, f), lambda j, i: (j, jnp.maximum(i * n8 - 1, 0), 0))
    nxt = pl.BlockSpec((None, CONV_HALO, f), lambda j, i: (j, jnp.minimum((i + 1) * n8, s // CONV_HALO - 1), 0))
    cw = pl.BlockSpec((None, 3, f), lambda j, i: (j, 0, 0))
    cb = pl.BlockSpec((None, 1, f), lambda j, i: (j, 0, 0))
    return cur, prev, nxt, cw, cb


def _conv_taps(g, prev, first):
    ext = jnp.concatenate([jnp.where(first, 0.0, prev.astype(F32)), g], axis=0)
    return pltpu.roll(ext, 1, 0)[CONV_HALO:], pltpu.roll(ext, 2, 0)[CONV_HALO:]


def _conv_fwd(gg, uu, cw, cb):
    _, s, f = gg.shape
    t = _tile(s, 512)

    def kern(g_ref, gp_ref, u_ref, cw_ref, cb_ref, o_ref):
        g = g_ref[...].astype(F32)
        g1, g2 = _conv_taps(g, gp_ref[...], pl.program_id(1) == 0)
        w = cw_ref[...]
        gc = cb_ref[...] + w[0:1] * g2 + w[1:2] * g1 + w[2:3] * g
        o_ref[...] = (_silu(gc) * u_ref[...].astype(F32)).astype(o_ref.dtype)

    cur, prev, _, cws, cbs = _conv_specs(s, f, t)
    return pl.pallas_call(
        kern, grid=(4, s // t), in_specs=[cur, prev, cur, cws, cbs], out_specs=cur,
        out_shape=jax.ShapeDtypeStruct(gg.shape, BF16),
        compiler_params=pltpu.CompilerParams(dimension_semantics=("parallel", "parallel"), vmem_limit_bytes=VMEM_LIMIT),
        name="ffn_conv_fwd")(gg, gg, uu, cw, cb)


def _conv_bwd_gate(gg, uu, dact, cw, cb):
    _, s, f = gg.shape
    t = _tile(s, 512)

    def kern(g_ref, gp_ref, u_ref, da_ref, cw_ref, cb_ref, du_ref, dgc_ref, dcw_ref, dcb_ref):
        i = pl.program_id(1)
        g, u, da = g_ref[...].astype(F32), u_ref[...].astype(F32), da_ref[...].astype(F32)
        g1, g2 = _conv_taps(g, gp_ref[...], i == 0)
        w = cw_ref[...]
        gc = cb_ref[...] + w[0:1] * g2 + w[1:2] * g1 + w[2:3] * g
        sg = jax.nn.sigmoid(gc)
        du_ref[...] = (da * (gc * sg)).astype(du_ref.dtype)
        dgc = da * u * (sg * (1.0 + gc * (1.0 - sg)))
        dgc_ref[...] = dgc.astype(dgc_ref.dtype)

        @pl.when(i == 0)
        def _():
            dcw_ref[...] = jnp.zeros_like(dcw_ref)
            dcb_ref[...] = jnp.zeros_like(dcb_ref)

        dcw_ref[0:1, :] += jnp.sum(dgc * g2, axis=0, keepdims=True)
        dcw_ref[1:2, :] += jnp.sum(dgc * g1, axis=0, keepdims=True)
        dcw_ref[2:3, :] += jnp.sum(dgc * g, axis=0, keepdims=True)
        dcb_ref[...] += jnp.sum(dgc, axis=0, keepdims=True)

    cur, prev, _, cws, cbs = _conv_specs(s, f, t)
    return pl.pallas_call(
        kern, grid=(4, s // t), in_specs=[cur, prev, cur, cur, cws, cbs], out_specs=[cur, cur, cws, cbs],
        out_shape=[jax.ShapeDtypeStruct(gg.shape, BF16), jax.ShapeDtypeStruct(gg.shape, BF16),
                   jax.ShapeDtypeStruct(cw.shape, F32), jax.ShapeDtypeStruct(cb.shape, F32)],
        compiler_params=pltpu.CompilerParams(dimension_semantics=("parallel", "arbitrary"), vmem_limit_bytes=VMEM_LIMIT),
        name="ffn_conv_bwd_gate")(gg, gg, uu, dact, cw, cb)


def _conv_bwd_taps(dgc, cw):
    _, s, f = dgc.shape
    t = _tile(s, 512)
    nt = s // t

    def kern(d_ref, dn_ref, cw_ref, o_ref):
        d = d_ref[...].astype(F32)
        ext = jnp.concatenate([d, jnp.where(pl.program_id(1) == nt - 1, 0.0, dn_ref[...].astype(F32))], axis=0)
        up1 = pltpu.roll(ext, t + CONV_HALO - 1, 0)[:t]
        up2 = pltpu.roll(ext, t + CONV_HALO - 2, 0)[:t]
        w = cw_ref[...]
        o_ref[...] = (w[2:3] * d + w[1:2] * up1 + w[0:1] * up2).astype(o_ref.dtype)

    cur, _, nxt, cws, _ = _conv_specs(s, f, t)
    return pl.pallas_call(
        kern, grid=(4, nt), in_specs=[cur, nxt, cws], out_specs=cur, out_shape=jax.ShapeDtypeStruct(dgc.shape, BF16),
        compiler_params=pltpu.CompilerParams(dimension_semantics=("parallel", "parallel"), vmem_limit_bytes=VMEM_LIMIT),
        name="ffn_conv_bwd_taps")(dgc, dgc, cw)


def _rope_tables(pos):
    half = MLA_ROPE // 2
    inv = ROPE_THETA ** (-jnp.arange(half, dtype=F32) / half)
    ang = pos.astype(F32)[:, None] * inv
    cos, sin = jnp.cos(ang), jnp.sin(ang)
    s = pos.shape[0]
    z = lambda w: jnp.zeros((s, w), F32)
    c = jnp.concatenate([jnp.ones((s, MLA_NOPE), F32), cos, cos, jnp.ones((s, LANES - MLA_QK), F32)], axis=1)
    sa = jnp.concatenate([z(MLA_NOPE), -sin, z(half), z(LANES - MLA_QK)], axis=1)
    sb = jnp.concatenate([z(MLA_NOPE), z(half), sin, z(LANES - MLA_QK)], axis=1)
    return c, sa, sb


def _local_step(x, mem, pos, target, w, late_shards):
    g = {}
    w = dict(w)
    c, sa, sb = _rope_tables(pos)

    xn = _norm_fwd(x, w["norm_mix"], "norm_mix_fwd")
    proj = _matmul(xn, w["in"], "nn", F32, "proj_fwd")
    alr = _row(proj, 128, P_ALR // 128)
    kpe = _row(proj, 128, P_KPE // 128)
    og = _row(proj, 512, P_OG // 512)
    cq = _row(proj, 256, P_CQ // 256)
    ckv = _row(proj, 128, P_CKV // 128)

    la = _rows_call(lambda r, k: ([_gate_fn(r[0], k[0], k[1])], []), [alr], [w["w2"], w["gate_b"]],
                    [(256, F32)], name="gla_gate_fwd")[0]
    o_gla, states = _gla_fwd(proj, la)

    q_lat, kv_lat = _rows_call(lambda r, k: ([_rms(r[0], k[0]), _rms(r[1], k[1])], []), [cq, ckv],
                               [w["q_a_norm"], w["kv_a_norm"]], [(256, BF16), (128, BF16)], name="mla_lat_fwd")
    q_up = _matmul(q_lat, w["uq"], "nn", F32, "mla_q_fwd")
    k_up = _matmul(kv_lat, w["k"], "nn", F32, "mla_k_fwd")
    v_mla = _matmul(kv_lat, w["v"], "nn", BF16, "mla_v_fwd")

    def qk_body(r, k):
        qs, ks = [], []
        for qh, kh in zip(_heads(r[0], MLA_HEADS), _heads(r[1], MLA_HEADS)):
            a, b = _qk_head(qh, kh, r[2], r[3], r[4], r[5], k[0], k[1])
            qs.append(a)
            ks.append(b)
        return [_cat(qs), _cat(ks)], []

    tabs = [_row(c), _row(sa), _row(sb)]
    q_r, k_r = _rows_call(qk_body, [_row(q_up), _row(k_up), kpe] + tabs, [w["q_norm"], w["k_norm"]],
                          [(1024, BF16), (1024, BF16)], name="mla_qk_fwd")
    o_mla, lse, gathered = _attn_fwd(q_r, k_r, v_mla, _gather_plan(late_shards))
    w.update(_late_layout(dict(zip(LATE, gathered, strict=True))))

    def mix_body(r, k):
        ys = [_mix_head(o, g_, k[0]) for o, g_ in zip(_heads(r[0], GLA_HEADS), _heads(r[1], GLA_HEADS))]
        return [_cat(ys + [r[2]])], []

    cat = _rows_call(mix_body, [_row(o_gla), og, _row(o_mla)], [w["gla_out_norm"]], [(1024, BF16)],
                     name="mix_fwd")[0]
    h1 = _matmul(cat, w["out"], "nn", F32, "out_fwd", residual=x)

    hn = _norm_fwd(h1, w["norm_xa"], "norm_xa_fwd")
    mn = _norm_fwd(mem, w["norm_mem"], "norm_mem_fwd")
    xq = _matmul(hn, w["xq"], "nn", F32, "xa_q_fwd")
    xkv = _matmul(mn, w["xkv"], "nn", F32, "xa_kv_fwd")

    def xa_body(r, k):
        ks, vs = _heads(k[0], 2 * XA_HEADS)[:XA_HEADS], _heads(k[0], 2 * XA_HEADS)[XA_HEADS:]
        return [_cat([_xa_head(a, b, v_, k[1], k[2]) for a, b, v_ in zip(_heads(r[0], XA_HEADS), ks, vs)])], []

    xo = _rows_call(xa_body, [_row(xq)], [xkv, w["xa_q_norm"], w["xa_k_norm"]], [(512, BF16)], name="xa_fwd")[0]
    h2 = _matmul(xo, w["xo"], "nn", F32, "xa_o_fwd", residual=h1)

    fn = _norm_fwd(h2, w["norm_ffn"], "norm_ffn_fwd")
    gg = _matmul(fn, w["wg"], "nt", BF16, "ffn_gate_fwd", b_lead="p")
    uu = _matmul(fn, w["wu"], "nt", BF16, "ffn_up_fwd", b_lead="p")
    act = _conv_fwd(gg, uu, w["cw"], w["cb"])
    y = _matmul(act, w["wd"], "nn", F32, "ffn_down_fwd", residual=h2, a_lead="k", b_lead="k")

    def loss_body(r, k):
        err = r[0] - r[1]
        part = 0.5 * jnp.sum(jnp.sum(err * err, axis=1, keepdims=True) * (1.0 / D_MODEL), axis=0, keepdims=True)
        return [err * (1.0 / D_MODEL)], [jnp.broadcast_to(part, (1, LANES))]

    dy, loss = _rows_call(loss_body, [_row(y), _row(target)], [], [(D_MODEL, F32)], [(1, LANES)], name="loss")

    g["ffn_w_down"] = _matmul(act, dy, "tn", BF16, "ffn_down_dw", a_lead="p")
    dact = _matmul(dy, w["wd"], "nt", BF16, "ffn_down_dx", b_lead="p")
    duu, dgc, g["ffn_conv_w"], g["ffn_conv_b"] = _conv_bwd_gate(gg, uu, dact, w["cw"], w["cb"])
    dgg = _conv_bwd_taps(dgc, w["cw"])
    g["ffn_w_gate"] = _matmul(dgg, fn, "tn", BF16, "ffn_gate_dw", a_lead="p")
    g["ffn_w_up"] = _matmul(duu, fn, "tn", BF16, "ffn_up_dw", a_lead="p")
    dfn = _matmul(dgg, w["wg"], "nn", F32, "ffn_gate_dx", a_lead="k", b_lead="k")
    dfn = _matmul(duu, w["wu"], "nn", F32, "ffn_up_dx", residual=dfn, a_lead="k", b_lead="k")
    dh2, g["norm_ffn"] = _norm_bwd(h2, w["norm_ffn"], dfn, dy, "norm_ffn_bwd")

    g["xa_w_o"] = _matmul(xo, dh2, "tn", BF16, "xa_o_dw")
    dxo = _matmul(dh2, w["xo"], "nt", F32, "xa_o_dx")

    def xa_bwd(r, k):
        kvh = _heads(k[0], 2 * XA_HEADS)
        dq_, dk_, dv_ = [], [], []
        dqn, dkn = 0.0, 0.0
        for h, (a, d_) in enumerate(zip(_heads(r[0], XA_HEADS), _heads(r[1], XA_HEADS))):
            _, vjp = jax.vjp(_xa_head, a, kvh[h], kvh[XA_HEADS + h], k[1], k[2])
            ga, gk, gv, gqn, gkn = vjp(d_)
            dq_.append(ga)
            dk_.append(gk)
            dv_.append(gv)
            dqn, dkn = dqn + gqn, dkn + gkn
        return [_cat(dq_)], [_cat(dk_ + dv_), dqn, dkn]

    dxq, dxkv, g["xa_q_norm"], g["xa_k_norm"] = _rows_call(
        xa_bwd, [_row(xq), _row(dxo)], [xkv, w["xa_q_norm"], w["xa_k_norm"]], [(512, BF16)],
        [xkv.shape, (1, 128), (1, 128)], name="xa_bwd")
    g["xa_w_q"] = _matmul(hn, dxq, "tn", BF16, "xa_q_dw")
    dhn = _matmul(dxq, w["xq"], "nt", F32, "xa_q_dx")
    g["xa_w_kv"] = _matmul(mn, dxkv, "tn", BF16, "xa_kv_dw")
    dmn = _matmul(dxkv, w["xkv"], "nt", F32, "xa_kv_dx")
    _, g["norm_mem"] = _norm_bwd(mem, w["norm_mem"], dmn, dmn, "norm_mem_bwd")
    dh1, g["norm_xa"] = _norm_bwd(h1, w["norm_xa"], dhn, dh2, "norm_xa_bwd")

    g["w_out"] = _matmul(cat, dh1, "tn", BF16, "out_dw")
    dcat = _matmul(dh1, w["out"], "nt", F32, "out_dx")

    def mix_bwd(r, k):
        do_, dog_ = [], []
        dgn = 0.0
        for o, g_, d_ in zip(_heads(r[0], GLA_HEADS), _heads(r[1], GLA_HEADS), _heads(r[2], GLA_HEADS)):
            _, vjp = jax.vjp(_mix_head, o, g_, k[0])
            a, b, gn_ = vjp(d_)
            do_.append(a)
            dog_.append(b)
            dgn = dgn + gn_
        return [_cat(do_), _cat(dog_)], [dgn]

    do_gla, d_og, g["gla_out_norm"] = _rows_call(mix_bwd, [_row(o_gla), og, _row(dcat, 512, 0)], [w["gla_out_norm"]],
                                                 [(512, F32), (512, BF16)], [(1, 128)], name="mix_bwd")

    late_parts = _late_grad_shards(g)
    dq_r, dk_r, dv_mla, lands_mlp = _attn_bwd(q_r, k_r, v_mla, o_mla, lse, dcat,
                                              _scatter_plan([late_parts[n] for n in LATE_MLP]))

    def qk_bwd(r, k):
        dqs, dks = [], []
        dkpe, dqn, dkn = 0.0, 0.0, 0.0
        for qh, kh, dqh, dkh in zip(_heads(r[0], MLA_HEADS), _heads(r[1], MLA_HEADS), _heads(r[6], MLA_HEADS),
                                    _heads(r[7], MLA_HEADS)):
            _, vjp = jax.vjp(lambda a, b, e, f, h_: _qk_head(a, b, e, r[3], r[4], r[5], f, h_), qh, kh, r[2], k[0], k[1])
            ga, gb, ge, gf, gh = vjp((dqh, dkh))
            dqs.append(ga)
            dks.append(gb)
            dkpe, dqn, dkn = dkpe + ge, dqn + gf, dkn + gh
        return [_cat(dqs), _cat(dks), dkpe], [dqn, dkn]

    dq_up, dk_up, d_kpe, g["q_norm"], g["k_norm"] = _rows_call(
        qk_bwd, [_row(q_up), _row(k_up), kpe] + tabs + [_row(dq_r), _row(dk_r)], [w["q_norm"], w["k_norm"]],
        [(1024, BF16), (1024, BF16), (128, BF16)], [(1, 128), (1, 128)], name="mla_qk_bwd")
    g["uq"] = _matmul(q_lat, dq_up, "tn", BF16, "mla_q_dw")
    dq_lat = _matmul(dq_up, w["uq"], "nt", F32, "mla_q_dx")
    g["k"] = _matmul(kv_lat, dk_up, "tn", BF16, "mla_k_dw")
    g["v"] = _matmul(kv_lat, dv_mla, "tn", BF16, "mla_v_dw")
    dkv_lat = _matmul(dk_up, w["k"], "nt", F32, "mla_k_dx")
    dkv_lat = _matmul(dv_mla, w["v"], "nt", F32, "mla_v_dx", residual=dkv_lat)

    def lat_bwd(r, k):
        _, vjp1 = jax.vjp(_rms, r[0], k[0])
        _, vjp2 = jax.vjp(_rms, r[1], k[1])
        a, ga = vjp1(r[2])
        b, gb = vjp2(r[3])
        return [a, b], [ga, gb]

    d_cq, d_ckv, g["mla_q_a_norm"], g["mla_kv_a_norm"] = _rows_call(
        lat_bwd, [cq, ckv, _row(dq_lat), _row(dkv_lat)], [w["q_a_norm"], w["kv_a_norm"]],
        [(256, BF16), (128, BF16)], [(1, 256), (1, 128)], name="mla_lat_bwd")

    dgq, dgk, dla, dgv, lands_mix = _gla_bwd(proj, la, states, do_gla, _scatter_plan([late_parts[n] for n in LATE_MIX]))
    lands_late = dict(zip(LATE_MLP + LATE_MIX, list(lands_mlp) + list(lands_mix), strict=True))

    def gate_bwd(r, k):
        _, vjp = jax.vjp(_gate_fn, r[0], k[0], k[1])
        a, gw, gb = vjp(r[1])
        return [a], [gw, gb]

    d_alr, g["w2"], g["gla_gate_b"] = _rows_call(gate_bwd, [alr, _row(dla)], [w["w2"], w["gate_b"]], [(128, BF16)],
                                                 [(128, 256), (1, 256)], name="gla_gate_bwd")

    dproj = jnp.concatenate([dgq.astype(BF16), dgk.astype(BF16), dgv.astype(BF16), d_og, d_cq, d_ckv, d_kpe, d_alr],
                            axis=1)
    g["in"] = _matmul(xn, dproj, "tn", BF16, "proj_dw")
    dxn = _matmul(dproj, w["in"], "nt", F32, "proj_dx")
    dx, g["norm_mix"] = _norm_bwd(x, w["norm_mix"], dxn, dh1, "norm_mix_bwd")
    return loss[0, 0], dx, g, lands_late


def _join_shards(pieces, axis):
    if axis == 0:
        return pieces.reshape(-1, pieces.shape[2])
    return jnp.transpose(pieces, (1, 0, 2)).reshape(pieces.shape[1], -1)


def _split_shards(full, axis):
    r, c = full.shape
    if axis == 0:
        return full.reshape(4, r // 4, c)
    return jnp.transpose(full.reshape(r, 4, c // 4), (1, 0, 2))


def _early_layout(gath, rep):
    w_in = _join_shards(gath["w_in"], 1)
    z = lambda n: jnp.zeros((D_MODEL, n), w_in.dtype)
    seg = lambda lo, n: w_in[:, lo:lo + n]
    ukv = _join_shards(gath["mla_w_ukv"], 1).reshape(MLA_KV_RANK, MLA_HEADS, MLA_NOPE + MLA_V)
    w = {
        "in": jnp.concatenate([seg(N_GQ, 256), seg(N_GK, 256), seg(N_GV, 512), seg(N_OG, 512), seg(N_CQ, 256),
                               seg(N_CKV, 128), z(64), seg(N_KPE, 32), z(32), seg(N_ALR, 16), z(112)], axis=1),
        "uq": jnp.pad(_join_shards(gath["mla_w_uq"], 1).reshape(MLA_Q_RANK, MLA_HEADS, MLA_QK),
                      ((0, 0), (0, 0), (0, LANES - MLA_QK))).reshape(MLA_Q_RANK, MLA_HEADS * LANES),
        "k": jnp.pad(ukv[:, :, :MLA_NOPE], ((0, 0), (0, 0), (0, LANES - MLA_NOPE))).reshape(MLA_KV_RANK, -1),
        "v": ukv[:, :, MLA_NOPE:].reshape(MLA_KV_RANK, MLA_HEADS * MLA_V),
        "w2": jnp.pad(_join_shards(gath["gla_gate_w2"], 1), ((0, LANES - GLA_RANK), (0, 0))),
        "cb": rep["ffn_conv_b"].reshape(4, 1, D_FF // 4),
        "q_norm": jnp.pad(rep["mla_q_norm"], ((0, 0), (0, LANES - MLA_QK))),
        "k_norm": jnp.pad(rep["mla_k_norm"], ((0, 0), (0, LANES - MLA_QK))),
        "q_a_norm": rep["mla_q_a_norm"], "kv_a_norm": rep["mla_kv_a_norm"], "gate_b": rep["gla_gate_b"],
    }
    for n in ("norm_mix", "gla_out_norm", "norm_xa", "norm_mem", "xa_q_norm", "xa_k_norm", "norm_ffn"):
        w[n] = rep[n]
    return w


def _late_layout(gath):
    return {"out": _join_shards(gath["w_out"], 0), "xq": _join_shards(gath["xa_w_q"], 0),
            "xkv": _join_shards(gath["xa_w_kv"], 0), "xo": _join_shards(gath["xa_w_o"], 1),
            "wg": gath["ffn_w_gate"], "wu": gath["ffn_w_up"], "wd": gath["ffn_w_down"], "cw": gath["ffn_conv_w"]}


def _late_grad_shards(g):
    sh = {"w_out": _split_shards(g["w_out"], 0), "xa_w_q": _split_shards(g["xa_w_q"], 0),
          "xa_w_kv": _split_shards(g["xa_w_kv"], 0), "xa_w_o": _split_shards(g["xa_w_o"], 1),
          "ffn_w_gate": g["ffn_w_gate"], "ffn_w_up": g["ffn_w_up"], "ffn_conv_w": g["ffn_conv_w"],
          "ffn_w_down": g["ffn_w_down"]}
    return {n: v.astype(BF16) for n, v in sh.items()}


def _early_grad_shards(g):
    gi = g["in"]
    seg = lambda lo, n: gi[:, lo:lo + n]
    w_in = jnp.concatenate([seg(P_GQ, 256), seg(P_GK, 256), seg(P_GV, 512), seg(P_ALR, 16), seg(P_OG, 512),
                            seg(P_CQ, 256), seg(P_CKV, 128), seg(P_KPE + 64, 32)], axis=1)
    uq = g["uq"].reshape(MLA_Q_RANK, MLA_HEADS, LANES)[:, :, :MLA_QK].reshape(MLA_Q_RANK, -1)
    ukv = jnp.concatenate([g["k"].reshape(MLA_KV_RANK, MLA_HEADS, LANES)[:, :, :MLA_NOPE],
                           g["v"].reshape(MLA_KV_RANK, MLA_HEADS, MLA_V)], axis=2).reshape(MLA_KV_RANK, -1)
    sh = {"w_in": _split_shards(w_in, 1), "gla_gate_w2": _split_shards(g["w2"][:GLA_RANK], 1),
          "mla_w_uq": _split_shards(uq, 1), "mla_w_ukv": _split_shards(ukv, 1)}
    sh = {n: v.astype(BF16) for n, v in sh.items()}
    rep = {n: g[n] for n in REPLICATED if n in g}
    rep["mla_q_norm"] = g["q_norm"][:, :MLA_QK]
    rep["mla_k_norm"] = g["k_norm"][:, :MLA_QK]
    rep["ffn_conv_b"] = g["ffn_conv_b"].reshape(1, D_FF)
    return sh, rep


SMALL_SHAPE = (8, 1024)


def _pack_small(vectors):
    flat = jnp.concatenate(vectors, axis=1)
    return jnp.pad(flat, ((0, 0), (0, SMALL_SHAPE[0] * SMALL_SHAPE[1] - flat.shape[1]))).reshape(SMALL_SHAPE)


def _unpack_small(buf, widths):
    flat = buf.reshape(1, -1)
    out, off = [], 0
    for wd in widths:
        out.append(flat[:, off:off + wd])
        off += wd
    return out


ANY = pl.BlockSpec(memory_space=pl.ANY)


def _place():
    x, y, c = lax.axis_index("x"), lax.axis_index("y"), lax.axis_index("c")
    chips = [(1 - x, y), (x, 1 - y), (1 - x, 1 - y)]
    return x, y, c, chips


class _Comm:
    def __init__(self, ins, out_shape, sems, start, finish, mid=None):
        self.ins, self.out_shape, self.sems = list(ins), list(out_shape), list(sems)
        self.start, self.finish, self.mid = start, finish, mid or (lambda *args: None)


def _run_comm(plan, name):
    ni, no = len(plan.ins), len(plan.out_shape)

    def body(*refs):
        ins, outs, sems = refs[:ni], refs[ni:ni + no], refs[ni + no:]
        place = _place()
        plan.start(place, ins, outs, sems)
        plan.mid(place, ins, outs, sems)
        plan.finish(place, ins, outs, sems)

    return pl.pallas_call(body, in_specs=[ANY] * ni, out_specs=[ANY] * no, out_shape=plan.out_shape,
                          scratch_shapes=plan.sems, name=name)(*plan.ins)


def _gather_plan(shards):
    n = len(shards)
    split = [s.shape[0] % (2 * BF16_ROWS) == 0 for s in shards]

    def rows(ref, t, c):
        if not split[t]:
            return ref
        half = shards[t].shape[0] // 2
        return ref.at[pl.ds(pl.multiple_of(c * half, BF16_ROWS), half)]

    def remote(src, dst, ss, rs, to):
        return pltpu.make_async_remote_copy(src_ref=src, dst_ref=dst, send_sem=ss, recv_sem=rs, device_id=to,
                                            device_id_type=MESH)

    def first_wave(place, ins, outs, sems):
        x, y, c, chips = place
        ici_s, ici_r, _, _, local = sems
        me = 2 * x + y
        own = [pltpu.make_async_copy(ins[t], outs[t].at[me], local.at[t]) for t in range(n)]
        push = [remote(rows(ins[t], t, c), rows(outs[t].at[me], t, c), ici_s.at[3 * t + j], ici_r.at[3 * t + j], (px, py, c))
                for t in range(n) for j, (px, py) in enumerate(chips)]
        return own, push

    def second_wave(place, ins, outs, sems, last):
        x, y, c, chips = place
        ici_s, ici_r, d2d_s, d2d_r, local = sems
        sib = (x, y, 1 - c)
        out = []
        for t in range(n):
            for j, (px, py) in enumerate(chips):
                block = outs[t].at[2 * px + py]
                got = rows(block, t, c)
                if split[t]:
                    hand = remote(got, got, d2d_s.at[3 * t + j], d2d_r.at[3 * t + j], sib)
                    theirs = rows(block, t, 1 - c)
                    other = (remote(theirs, theirs, local.at[0], d2d_r.at[3 * t + j], sib) if last else
                             remote(got, got, local.at[0], ici_r.at[3 * t + j], sib))
                    out.append((other, hand))
                elif last:
                    out.append((remote(got, got, local.at[0], ici_r.at[3 * t + j], sib), None))
        return out

    def start(place, ins, outs, sems):
        own, push = first_wave(place, ins, outs, sems)
        for cp in own + push:
            cp.start()

    def mid(place, ins, outs, sems):
        for arrival, hand in second_wave(place, ins, outs, sems, False):
            arrival.wait_recv()
            hand.start()

    def finish(place, ins, outs, sems):
        own, push = first_wave(place, ins, outs, sems)
        for arrival, hand in second_wave(place, ins, outs, sems, True):
            arrival.wait_recv()
            if hand is not None:
                hand.wait_send()
        for cp in push:
            cp.wait_send()
        for cp in own:
            cp.wait()

    dma = pltpu.SemaphoreType.DMA
    return _Comm(shards, [jax.ShapeDtypeStruct((4,) + s.shape, s.dtype) for s in shards],
                 [dma((3 * n,)), dma((3 * n,)), dma((3 * n,)), dma((3 * n,)), dma((n,))], start, finish, mid)


def _scatter_plan(parts, small=None):
    n = len(parts)
    ns = 0 if small is None else 1

    def unpack(place, ins, outs, sems):
        x, y, c, chips = place
        return x, y, c, chips, 2 * x + y, 4 * x + 2 * y + c, (x, y, 1 - c)

    def remote(src, dst, ss, rs, to):
        return pltpu.make_async_remote_copy(src_ref=src, dst_ref=dst, send_sem=ss, recv_sem=rs, device_id=to,
                                            device_id_type=MESH)

    def first_wave(place, ins, outs, sems):
        x, y, c, chips, me, dev, sib = unpack(place, ins, outs, sems)
        ici_s, ici_r, d2d_s, d2d_r, sm_s, sm_r, local = sems
        own, push = [], []
        if ns:
            own.append(pltpu.make_async_copy(ins[n], outs[n].at[dev], local.at[n]))
            for k in range(1, 8):
                px = (1 - x) if (k >> 2) & 1 else x
                py = (1 - y) if (k >> 1) & 1 else y
                pc = (1 - c) if k & 1 else c
                push.append(remote(ins[n], outs[n].at[dev], sm_s.at[k - 1], sm_r.at[k - 1], (px, py, pc)))
        for t in range(n):
            own.append(pltpu.make_async_copy(ins[t].at[me], outs[t].at[dev], local.at[t]))
            push.append(remote(ins[t].at[me], outs[t].at[dev], d2d_s.at[4 * t], d2d_r.at[4 * t], sib))
            for j, (px, py) in enumerate(chips):
                push.append(remote(ins[t].at[2 * px + py], outs[t].at[dev], ici_s.at[3 * t + j], ici_r.at[3 * t + j],
                                   (px, py, c)))
        return own, push

    def start(place, ins, outs, sems):
        own, push = first_wave(place, ins, outs, sems)
        for cp in own + push:
            cp.start()

    def landed(dst, rs, sems, sib):
        remote(dst, dst, sems[-1].at[0], rs, sib).wait_recv()

    def forwards(place, ins, outs, sems):
        x, y, c, chips, me, dev, sib = unpack(place, ins, outs, sems)
        d2d_s, d2d_r = sems[2], sems[3]
        slots = [(t, j, outs[t].at[4 * px + 2 * py + c]) for t in range(n) for j, (px, py) in enumerate(chips)]
        return [(t, j, slot, remote(slot, slot, d2d_s.at[4 * t + 1 + j], d2d_r.at[4 * t + 1 + j], sib))
                for t, j, slot in slots]

    def mid(place, ins, outs, sems):
        sib = unpack(place, ins, outs, sems)[-1]
        for t, j, slot, cp in forwards(place, ins, outs, sems):
            landed(slot, sems[1].at[3 * t + j], sems, sib)
            cp.start()

    def finish(place, ins, outs, sems):
        x, y, c, chips, me, dev, sib = unpack(place, ins, outs, sems)
        d2d_r, sm_r = sems[3], sems[5]
        own, push = first_wave(place, ins, outs, sems)
        push += [cp for _, _, _, cp in forwards(place, ins, outs, sems)]
        for t in range(n):
            landed(outs[t].at[4 * x + 2 * y + (1 - c)], d2d_r.at[4 * t], sems, sib)
            for j, (px, py) in enumerate(chips):
                landed(outs[t].at[4 * px + 2 * py + (1 - c)], d2d_r.at[4 * t + 1 + j], sems, sib)
        if ns:
            for k in range(1, 8):
                px = (1 - x) if (k >> 2) & 1 else x
                py = (1 - y) if (k >> 1) & 1 else y
                pc = (1 - c) if k & 1 else c
                landed(outs[n].at[4 * px + 2 * py + pc], sm_r.at[k - 1], sems, sib)
        for cp in push:
            cp.wait_send()
        for cp in own:
            cp.wait()

    dma = pltpu.SemaphoreType.DMA
    ins = list(parts) + ([small] if ns else [])
    out_shape = [jax.ShapeDtypeStruct((8,) + p.shape[1:], p.dtype) for p in parts]
    if ns:
        out_shape.append(jax.ShapeDtypeStruct((8,) + small.shape, small.dtype))
    return _Comm(ins, out_shape, [dma((3 * n,)), dma((3 * n,)), dma((4 * n,)), dma((4 * n,)), dma((7,)), dma((7,)),
                                  dma((n + 1,))], start, finish, mid)


ADAM_ROWS = 288


def _row_tile(r, cap):
    if r <= cap:
        return r
    return max(t for t in range(8, cap + 1, 8) if r % t == 0)


def _adamw_update(w, m, v, land):
    g = land[0].astype(F32)
    for i in range(1, 8):
        g = g + land[i].astype(F32)
    m_new = ADAM_B1 * m + (1.0 - ADAM_B1) * g
    v_new = ADAM_B2 * v + (1.0 - ADAM_B2) * (g * g)
    m_hat = m_new / (1.0 - ADAM_B1 ** ADAM_STEP)
    v_hat = v_new / (1.0 - ADAM_B2 ** ADAM_STEP)
    return g, -ADAM_LR * (m_hat / (jnp.sqrt(v_hat) + ADAM_EPS) + ADAM_WD * w), m_new, v_new


def _adamw(tensors, name, comm=None):
    k = len(tensors)
    r, c = tensors[0][0].shape
    t = _row_tile(r, ADAM_ROWS // k)
    n = r // t
    nci, nco, nsem = (len(comm.ins), len(comm.out_shape), len(comm.sems)) if comm else (0, 0, 0)

    def kern(*refs):
        ins, cins, outs, couts, csems = _split_refs(refs, (4 * k, nci, 4 * k, nco, nsem))
        if comm:
            place = _place()

            @pl.when(pl.program_id(0) == 0)
            def _():
                comm.start(place, cins, couts, csems)

        for i in range(k):
            w_ref, m_ref, v_ref, l_ref = ins[4 * i:4 * i + 4]
            res = _adamw_update(w_ref[...], m_ref[...], v_ref[...], l_ref)
            for ref, val in zip(outs[4 * i:4 * i + 4], res, strict=True):
                ref[...] = val
        if comm:
            @pl.when(pl.program_id(0) == n - 1)
            def _():
                comm.mid(place, cins, couts, csems)
                comm.finish(place, cins, couts, csems)

    spec = pl.BlockSpec((t, c), lambda i: (i, 0))
    lspec = pl.BlockSpec((8, t, c), lambda i: (0, i, 0))
    res = pl.pallas_call(
        kern, grid=(n,), in_specs=[spec, spec, spec, lspec] * k + [ANY] * nci, out_specs=[spec] * (4 * k) + [ANY] * nco,
        out_shape=[jax.ShapeDtypeStruct((r, c), F32)] * (4 * k) + (comm.out_shape if comm else []),
        scratch_shapes=comm.sems if comm else [],
        compiler_params=pltpu.CompilerParams(dimension_semantics=("arbitrary" if comm else "parallel",),
                                             vmem_limit_bytes=VMEM_LIMIT),
        name=name)(*[x for tens in tensors for x in tens], *(comm.ins if comm else []))
    return [res[4 * i:4 * i + 4] for i in range(k)], res[4 * k:]


def _step(a):
    def sq(n):
        v = a[n][0] if a[n].ndim == 3 else a[n]
        return v.T if n.removeprefix("m_").removeprefix("v_") in TRANSPOSED else v

    payload = lambda n: sq(n) if n in EXACT_GATHER else sq(n).astype(BF16)

    gathered = _run_comm(_gather_plan([payload(n) for n in EARLY]), "gather_early")
    w = _early_layout(dict(zip(EARLY, gathered, strict=True)), {n: a[n] for n in REPLICATED})

    loss, dx, g, lands_late = _local_step(sq("x"), sq("mem"), a["positions"][0], sq("loss_target"), w,
                                          [payload(n) for n in LATE])

    sh, rep = _early_grad_shards(g)
    small = _pack_small([rep[n] for n in REPLICATED] + [loss.reshape(1, 1)])
    last = _scatter_plan([sh[n] for n in EARLY], small)
    quad = lambda n, land: (sq(n), sq("m_" + n), sq("v_" + n), land)
    mlp = [n for n in LATE_MLP if sq(n).shape == sq("ffn_w_down").shape]
    updates, (*lands_early, land_small) = _adamw([quad(n, lands_late[n]) for n in mlp], "adamw_mlp", last)
    lands = dict(zip(EARLY, lands_early, strict=True)) | lands_late

    outs = {}
    kinds = ("grad_", "delta_", "new_m_", "new_v_")
    for n, _ in SHARDED:
        res = updates[mlp.index(n)] if n in mlp else _adamw([quad(n, lands[n])], "adamw_" + n)[0][0]
        for kind, val in zip(kinds, res, strict=True):
            outs[kind + n] = (val.T if n in TRANSPOSED else val).reshape(a[n].shape)
    zero = jnp.zeros((1, 1), F32)
    packed = [_pack_small([a[p + n] for n in REPLICATED] + [zero]) for p in ("", "m_", "v_")]
    res = _adamw([(*packed, land_small)], "adamw_replicated")[0][0]
    widths = [a[n].shape[1] for n in REPLICATED] + [1]
    for kind, buf in zip(kinds, res, strict=True):
        *vals, total = _unpack_small(buf, widths)
        for n, val in zip(REPLICATED, vals, strict=True):
            outs[kind + n] = val
        if kind == "grad_":
            loss = total[0, 0]

    ordered = [outs[kind + n] for kind in kinds for n in WEIGHTS]
    return (loss, dx[None], *ordered)


def kernel(x, mem, positions, norm_mix, w_in, gla_gate_w2, gla_gate_b, gla_out_norm, mla_q_a_norm, mla_w_uq, mla_kv_a_norm, mla_w_ukv, mla_q_norm, mla_k_norm, w_out, norm_xa, norm_mem, xa_w_q, xa_w_kv, xa_q_norm, xa_k_norm, xa_w_o, norm_ffn, ffn_w_gate, ffn_w_up, ffn_conv_w, ffn_conv_b, ffn_w_down, loss_target, m_norm_mix, m_w_in, m_gla_gate_w2, m_gla_gate_b, m_gla_out_norm, m_mla_q_a_norm, m_mla_w_uq, m_mla_kv_a_norm, m_mla_w_ukv, m_mla_q_norm, m_mla_k_norm, m_w_out, m_norm_xa, m_norm_mem, m_xa_w_q, m_xa_w_kv, m_xa_q_norm, m_xa_k_norm, m_xa_w_o, m_norm_ffn, m_ffn_w_gate, m_ffn_w_up, m_ffn_conv_w, m_ffn_conv_b, m_ffn_w_down, v_norm_mix, v_w_in, v_gla_gate_w2, v_gla_gate_b, v_gla_out_norm, v_mla_q_a_norm, v_mla_w_uq, v_mla_kv_a_norm, v_mla_w_ukv, v_mla_q_norm, v_mla_k_norm, v_w_out, v_norm_xa, v_norm_mem, v_xa_w_q, v_xa_w_kv, v_xa_q_norm, v_xa_k_norm, v_xa_w_o, v_norm_ffn, v_ffn_w_gate, v_ffn_w_up, v_ffn_conv_w, v_ffn_conv_b, v_ffn_w_down):
    return _step(dict(locals()))
```

```python
import functools

import jax
import jax.numpy as jnp
from jax import lax
from jax.experimental import pallas as pl
from jax.experimental.pallas import tpu as pltpu

F32, BF16 = jnp.float32, jnp.bfloat16
MESH = pl.DeviceIdType.MESH

D_MODEL = 1024
EPS = 1e-6
GLA_HEADS, GLA_DK, GLA_DV, GLA_RANK, GLA_CHUNK = 4, 64, 128, 16, 64
GLA_GATE_NORM = 16.0
MLA_HEADS, MLA_Q_RANK, MLA_KV_RANK, MLA_NOPE, MLA_ROPE, MLA_V = 8, 256, 128, 64, 32, 64
MLA_QK = MLA_NOPE + MLA_ROPE
ROPE_THETA = 10000.0
LOG2E, LN2 = 1.4426950408889634, 0.6931471805599453
XA_HEADS, XA_DIM = 4, 128
D_FF = 2816
ADAM_LR, ADAM_B1, ADAM_B2, ADAM_EPS, ADAM_WD, ADAM_STEP = 0.001, 0.9, 0.999, 1e-08, 0.01, 10

LANES = 128
BF16_ROWS = 16
VMEM_LIMIT = 56 * 1024 * 1024
MATMUL_VMEM = 44 * 1024 * 1024

P_GQ, P_GK, P_GV, P_OG, P_CQ, P_CKV, P_KPE, P_ALR, P_WIDTH = 0, 256, 512, 1024, 1536, 1792, 1920, 2048, 2176
N_GQ, N_GK, N_GV, N_ALR, N_OG, N_CQ, N_CKV, N_KPE, N_WIDTH = 0, 256, 512, 1024, 1040, 1552, 1808, 1936, 1968

SHARDED = (("w_in", 1), ("gla_gate_w2", 1), ("mla_w_uq", 1), ("mla_w_ukv", 1), ("w_out", 0), ("xa_w_q", 0),
           ("xa_w_kv", 0), ("xa_w_o", 1), ("ffn_w_gate", 1), ("ffn_w_up", 1), ("ffn_conv_w", 1), ("ffn_w_down", 0))
REPLICATED = ("norm_mix", "gla_gate_b", "gla_out_norm", "mla_q_a_norm", "mla_kv_a_norm", "mla_q_norm", "mla_k_norm",
              "norm_xa", "norm_mem", "xa_q_norm", "xa_k_norm", "norm_ffn", "ffn_conv_b")
EXACT_GATHER = ("gla_gate_w2", "ffn_conv_w")
TRANSPOSED = ("ffn_w_gate", "ffn_w_up")
EARLY = ("w_in", "gla_gate_w2", "mla_w_uq", "mla_w_ukv")
LATE = tuple(n for n, _ in SHARDED if n not in EARLY)
LATE_MLP = tuple(n for n in LATE if n.startswith("ffn_"))
LATE_MIX = tuple(n for n in LATE if not n.startswith("ffn_"))
WEIGHTS = ("norm_mix", "w_in", "gla_gate_w2", "gla_gate_b", "gla_out_norm", "mla_q_a_norm", "mla_w_uq",
           "mla_kv_a_norm", "mla_w_ukv", "mla_q_norm", "mla_k_norm", "w_out", "norm_xa", "norm_mem", "xa_w_q",
           "xa_w_kv", "xa_q_norm", "xa_k_norm", "xa_w_o", "norm_ffn", "ffn_w_gate", "ffn_w_up", "ffn_conv_w",
           "ffn_conv_b", "ffn_w_down")


_NN = ((1,), (0,))
_NT = ((1,), (1,))
_TN = ((0,), (0,))


def _dg(a, b, dims):
    return lax.dot_general(a.astype(BF16), b.astype(BF16), (dims, ((), ())), preferred_element_type=F32)


@jax.custom_vjp
def _dot_nn(a, b):
    return _dg(a, b, _NN)


_dot_nn.defvjp(lambda a, b: (_dg(a, b, _NN), (a, b)),
               lambda r, g: (_dg(g, r[1], _NT).astype(r[0].dtype), _dg(r[0], g, _TN).astype(r[1].dtype)))


@jax.custom_vjp
def _dot_nt(a, b):
    return _dg(a, b, _NT)


_dot_nt.defvjp(lambda a, b: (_dg(a, b, _NT), (a, b)),
               lambda r, g: (_dg(g, r[1], _NN).astype(r[0].dtype), _dg(g, r[0], _TN).astype(r[1].dtype)))


@jax.custom_vjp
def _dot_tn(a, b):
    return _dg(a, b, _TN)


_dot_tn.defvjp(lambda a, b: (_dg(a, b, _TN), (a, b)),
               lambda r, g: (_dg(r[1], g, _NT).astype(r[0].dtype), _dg(r[0], g, _NN).astype(r[1].dtype)))


def _rms(x, w, n=None):
    n = x.shape[-1] if n is None else n
    ms = jnp.sum(x * x, axis=-1, keepdims=True) * (1.0 / n)
    return x * lax.rsqrt(ms + EPS) * w


def _silu(x):
    return x * jax.nn.sigmoid(x)


def _log_sigmoid(x):
    return jnp.minimum(x, 0.0) - jnp.log(1.0 + jnp.exp(-jnp.abs(x)))


@jax.custom_vjp
def _rope(y, c, sa, sb):
    return y * c + pltpu.roll(y, LANES - 16, 1) * sa + pltpu.roll(y, 16, 1) * sb


def _rope_bwd(res, g):
    c, sa, sb = res
    gy = g * c + pltpu.roll(g * sa, 16, 1) + pltpu.roll(g * sb, LANES - 16, 1)
    return gy, jnp.zeros_like(c), jnp.zeros_like(sa), jnp.zeros_like(sb)


_rope.defvjp(lambda y, c, sa, sb: (_rope(y, c, sa, sb), (c, sa, sb)), _rope_bwd)


def _lane_mask(lo, hi):
    lane = lax.broadcasted_iota(jnp.int32, (1, LANES), 1)
    return ((lane >= lo) & (lane < hi)).astype(F32)


def _tile(n, t):
    t = min(n, t)
    assert n % t == 0, (n, t)
    return t


class _Epilogue:
    def __init__(self, fn, rows=(), consts=(), outs=(), accs=()):
        self.fn, self.rows, self.consts, self.outs, self.accs = fn, list(rows), list(consts), list(outs), list(accs)


def _matmul(a, b, mode, out_dtype, name, residual=None, a_lead=None, b_lead=None, more=None, epilogue=None):
    (a0, a1), (b0, b1) = a.shape[-2:], b.shape[-2:]
    if mode == "nn":
        m, k, k2, n = a0, a1, b0, b1
    elif mode == "nt":
        m, k, n, k2 = a0, a1, b0, b1
    else:
        k, m, k2, n = a0, a1, b0, b1
    assert k == k2, (a.shape, b.shape, mode)
    npar = 4 if "p" in (a_lead, b_lead) else 1
    nsum = 4 if "k" in (a_lead, b_lead) else 1
    pairs = [(a, b)] + ([more] if more else [])
    a_item, b_item, o_item = a.dtype.itemsize, b.dtype.itemsize, jnp.dtype(out_dtype).itemsize
    ep = epilogue
    row_extra = (4 if residual is not None else 0) + (sum(r.dtype.itemsize for r in ep.rows) +
                                                       sum(jnp.dtype(d).itemsize for d in ep.outs) if ep else 0)

    def vmem_need(tm, tn, tk):
        need = 2 * (nsum if a_lead == "k" else 1) * tm * tk * a_item + 2 * (nsum if b_lead == "k" else 1) * tk * tn * b_item
        need *= len(pairs)
        need += (0 if ep else 2 * tm * tn * o_item) + tm * tn * 4 * (2 if tk < k else 1)
        need += tm * tk * 2 * (a_item == 4 or mode == "tn") + tk * tn * 2 * (b_item == 4)
        return need + 2 * tm * tn * row_extra + (3 * tm * tn * 4 if ep else 0)

    halvings = (4096, 2048, 1024, 512, 256, 128, 64, 32, 16, 8)
    if mode == "tn":
        tm = m if m <= 1408 else m // 2
        tn = n if tm * n <= 1024 * 2304 else n // 2
        tk = next((r for r in halvings if k % r == 0 and vmem_need(tm, tn, r) <= MATMUL_VMEM), k)
    else:
        tn, tk = n, k
        tm = next((r for r in halvings if m % r == 0 and vmem_need(r, tn, tk) <= MATMUL_VMEM), m)
    assert m % tm == 0 and n % tn == 0 and k % tk == 0
    assert ep is None or (tn == n and tk == k and npar == 1)
    nk = k // tk
    dims = {"nn": _NN, "nt": _NT, "tn": _TN}[mode]
    n_in = 2 * len(pairs) + (residual is not None)
    n_ep_in = len(ep.rows) + len(ep.consts) if ep else 0
    n_out = len(ep.outs) + len(ep.accs) if ep else 1

    def body(*refs):
        ab, rs, ep_in, outs, scratch = _split_refs(refs, (2 * len(pairs), n_in - 2 * len(pairs), n_ep_in, n_out, nk > 1))
        prod = None
        for a_ref, b_ref in zip(ab[0::2], ab[1::2]):
            for sh in range(nsum):
                term = _dg(a_ref[sh] if a_lead == "k" else a_ref[...], b_ref[sh] if b_lead == "k" else b_ref[...], dims)
                prod = term if prod is None else prod + term

        def finish(r):
            if rs:
                r = r + rs[0][...]
            if ep is None:
                outs[0][...] = r.astype(outs[0].dtype)
                return
            vals = [x[...] for x in ep_in]
            ro, ao = ep.fn(r, vals[:len(ep.rows)], vals[len(ep.rows):])
            for ref, val in zip(outs[:len(ep.outs)], ro, strict=True):
                ref[...] = val.astype(ref.dtype)
            if ep.accs:
                @pl.when(pl.program_id(0) == 0)
                def _():
                    for ref in outs[len(ep.outs):]:
                        ref[...] = jnp.zeros_like(ref)

                for ref, val in zip(outs[len(ep.outs):], ao, strict=True):
                    ref[...] += val

        if nk == 1:
            finish(prod)
            return
        acc = scratch[0]
        kk = pl.program_id(3)

        @pl.when(kk == 0)
        def _():
            acc[...] = prod

        @pl.when(kk > 0)
        def _():
            acc[...] += prod

        @pl.when(kk == nk - 1)
        def _():
            finish(acc[...])

    def spec(lead, blk, idx):
        if lead is None:
            return pl.BlockSpec(blk, lambda i, j, p, kk: idx(i, j, kk))
        if lead == "p":
            return pl.BlockSpec((None,) + blk, lambda i, j, p, kk: (p,) + idx(i, j, kk))
        return pl.BlockSpec((nsum,) + blk, lambda i, j, p, kk: (0,) + idx(i, j, kk))

    if mode == "nn":
        pair_specs = [spec(a_lead, (tm, tk), lambda i, j, kk: (i, kk)), spec(b_lead, (tk, tn), lambda i, j, kk: (kk, j))]
    elif mode == "nt":
        pair_specs = [spec(a_lead, (tm, tk), lambda i, j, kk: (i, kk)), spec(b_lead, (tn, tk), lambda i, j, kk: (j, kk))]
    else:
        pair_specs = [spec(a_lead, (tk, tm), lambda i, j, kk: (kk, i)), spec(b_lead, (tk, tn), lambda i, j, kk: (kk, j))]
    tile = spec(None, (tm, tn), lambda i, j, kk: (i, j))
    in_specs = pair_specs * len(pairs)
    args = [x for pair in pairs for x in pair]
    if residual is not None:
        assert npar == 1
        in_specs.append(tile)
        args.append(residual)
    if ep:
        in_specs += [tile] * len(ep.rows) + [pl.BlockSpec(c.shape, lambda i, j, p, kk: (0, 0)) for c in ep.consts]
        args += ep.rows + ep.consts
        out_specs = [tile] * len(ep.outs) + [pl.BlockSpec(shape, lambda i, j, p, kk: (0, 0)) for shape in ep.accs]
        out_shape = [jax.ShapeDtypeStruct((m, n), d) for d in ep.outs] + [jax.ShapeDtypeStruct(sh, F32) for sh in ep.accs]
    else:
        out_specs = spec("p" if npar > 1 else None, (tm, tn), lambda i, j, kk: (i, j))
        out_shape = jax.ShapeDtypeStruct(((4,) if npar > 1 else ()) + (m, n), out_dtype)
    outer = "arbitrary" if ep and ep.accs else "parallel"
    return pl.pallas_call(
        body, grid=(m // tm, n // tn, npar, nk), in_specs=in_specs, out_specs=out_specs, out_shape=out_shape,
        scratch_shapes=[pltpu.VMEM((tm, tn), F32)] if nk > 1 else [],
        compiler_params=pltpu.CompilerParams(dimension_semantics=(outer, outer, outer, "arbitrary"),
                                             vmem_limit_bytes=VMEM_LIMIT),
        name=name)(*args)


def _row(a, width=None, col_block=0):
    return (a, a.shape[1] if width is None else width, col_block)


def _rows_call(body, rows, consts, outs, accs=(), *, name, tile=512):
    s = rows[0][0].shape[0]
    t = _tile(s, tile)
    nr, nc, no = len(rows), len(consts), len(outs)

    def kern(*refs):
        r = [x[...] for x in refs[:nr]]
        c = [x[...] for x in refs[nr:nr + nc]]
        o_refs = refs[nr + nc:nr + nc + no]
        a_refs = refs[nr + nc + no:]
        ro, ao = body(r, c)
        for ref, val in zip(o_refs, ro, strict=True):
            ref[...] = val.astype(ref.dtype)
        if a_refs:
            @pl.when(pl.program_id(0) == 0)
            def _():
                for ref in a_refs:
                    ref[...] = jnp.zeros_like(ref)

            for ref, val in zip(a_refs, ao, strict=True):
                ref[...] += val

    in_specs = [pl.BlockSpec((t, w), functools.partial(lambda cb, i: (i, cb), cb)) for (_, w, cb) in rows]
    in_specs += [pl.BlockSpec(c.shape, lambda i: (0, 0)) for c in consts]
    out_specs = [pl.BlockSpec((t, w), lambda i: (i, 0)) for (w, _) in outs]
    out_specs += [pl.BlockSpec(shape, lambda i: (0, 0)) for shape in accs]
    out_shape = [jax.ShapeDtypeStruct((s, w), dt) for (w, dt) in outs]
    out_shape += [jax.ShapeDtypeStruct(shape, F32) for shape in accs]
    return pl.pallas_call(
        kern, grid=(s // t,), in_specs=in_specs, out_specs=out_specs, out_shape=out_shape,
        compiler_params=pltpu.CompilerParams(dimension_semantics=("arbitrary" if accs else "parallel",),
                                             vmem_limit_bytes=VMEM_LIMIT),
        name=name)(*[r[0] for r in rows], *consts)


def _gla_chunk(q, k, la, v0, v1, s0, s1):
    c = q.shape[0]
    r = lax.broadcasted_iota(jnp.int32, (c, c), 0)
    cc = lax.broadcasted_iota(jnp.int32, (c, c), 1)
    tril = cc <= r
    cum = lax.dot_general(tril.astype(F32), la, (_NN, ((), ())), precision=lax.Precision.HIGHEST,
                          preferred_element_type=F32)
    cl = jnp.sum(la, axis=0, keepdims=True)
    qd = q * (GLA_DK ** -0.5) * jnp.exp(cum)
    ki = k * jnp.exp(-cum)
    ke = k * jnp.exp(cl - cum)
    dec = jnp.exp(cl)
    outs, news = [], []
    for h, (v, s) in enumerate(((v0, s0), (v1, s1))):
        mk = _lane_mask(GLA_DK * h, GLA_DK * (h + 1))
        qh = qd * mk
        att = jnp.where(tril, _dot_nt(qh, ki), 0.0)
        outs.append(_dot_nn(att, v) + _dot_nt(qh, s))
        news.append(s * dec + _dot_tn(v, ke * mk))
    return outs[0], outs[1], news[0], news[1]


def _gla_specs(tb, rev_nb=None):
    blk = (lambda b: b) if rev_nb is None else (lambda b: rev_nb - 1 - b)
    q = pl.BlockSpec((tb, 128), lambda p, b: (blk(b), P_GQ // 128 + p))
    k = pl.BlockSpec((tb, 128), lambda p, b: (blk(b), P_GK // 128 + p))
    la = pl.BlockSpec((tb, 128), lambda p, b: (blk(b), p))
    v = pl.BlockSpec((tb, 256), lambda p, b: (blk(b), P_GV // 256 + p))
    o = pl.BlockSpec((tb, 256), lambda p, b: (blk(b), p))
    st = pl.BlockSpec((tb // GLA_CHUNK, 2, 128, 128), lambda p, b: (blk(b), p, 0, 0))
    return q, k, la, v, o, st


def _gla_fwd(proj, la):
    s = proj.shape[0]
    tb = _tile(s, 512)
    nb, nch = s // tb, tb // GLA_CHUNK

    def kern(q_ref, k_ref, la_ref, v_ref, o_ref, st_ref, s_sc):
        @pl.when(pl.program_id(1) == 0)
        def _():
            s_sc[...] = jnp.zeros_like(s_sc)

        s0, s1 = s_sc[0], s_sc[1]
        for ci in range(nch):
            sl = slice(ci * GLA_CHUNK, (ci + 1) * GLA_CHUNK)
            st_ref[ci, 0] = s0
            st_ref[ci, 1] = s1
            o0, o1, s0, s1 = _gla_chunk(q_ref[sl, :], k_ref[sl, :], la_ref[sl, :], v_ref[sl, 0:128],
                                        v_ref[sl, 128:256], s0, s1)
            o_ref[sl, 0:128] = o0
            o_ref[sl, 128:256] = o1
        s_sc[0] = s0
        s_sc[1] = s1

    q, k, lasp, v, o, st = _gla_specs(tb)
    return pl.pallas_call(
        kern, grid=(2, nb), in_specs=[q, k, lasp, v], out_specs=[o, st],
        out_shape=[jax.ShapeDtypeStruct((s, 512), F32),
                   jax.ShapeDtypeStruct((s // GLA_CHUNK, GLA_HEADS, 128, 128), F32)],
        scratch_shapes=[pltpu.VMEM((2, 128, 128), F32)],
        compiler_params=pltpu.CompilerParams(dimension_semantics=("parallel", "arbitrary"),
                                             vmem_limit_bytes=VMEM_LIMIT),
        name="gla_fwd")(proj, proj, la, proj)


def _gla_bwd(proj, la, states, d_o, comm):
    s = proj.shape[0]
    tb = _tile(s, 512)
    nb, nch = s // tb, tb // GLA_CHUNK
    nci, nco = len(comm.ins), len(comm.out_shape)

    def kern(*refs):
        (q_ref, k_ref, la_ref, v_ref, do_ref, st_ref), cins, (dq_ref, dk_ref, dla_ref, dv_ref), couts, (ds_sc,), csems = \
            _split_refs(refs, (6, nci, 4, nco, 1, len(comm.sems)))
        place = _place()
        pair, blk = pl.program_id(0), pl.program_id(1)

        @pl.when((pair == 0) & (blk == 0))
        def _():
            comm.start(place, cins, couts, csems)

        @pl.when((pair == 1) & (blk == nb // 2))
        def _():
            comm.mid(place, cins, couts, csems)

        @pl.when(blk == 0)
        def _():
            ds_sc[...] = jnp.zeros_like(ds_sc)

        d0, d1 = ds_sc[0], ds_sc[1]
        for ci in reversed(range(nch)):
            sl = slice(ci * GLA_CHUNK, (ci + 1) * GLA_CHUNK)
            _, vjp = jax.vjp(_gla_chunk, q_ref[sl, :], k_ref[sl, :], la_ref[sl, :], v_ref[sl, 0:128],
                             v_ref[sl, 128:256], st_ref[ci, 0], st_ref[ci, 1])
            gq, gk, gla, gv0, gv1, d0, d1 = vjp((do_ref[sl, 0:128], do_ref[sl, 128:256], d0, d1))
            dq_ref[sl, :] = gq
            dk_ref[sl, :] = gk
            dla_ref[sl, :] = gla
            dv_ref[sl, 0:128] = gv0
            dv_ref[sl, 128:256] = gv1
        ds_sc[0] = d0
        ds_sc[1] = d1

        @pl.when((pair == 1) & (blk == nb - 1))
        def _():
            comm.finish(place, cins, couts, csems)

    q, k, lasp, v, o, st = _gla_specs(tb, rev_nb=nb)
    res = pl.pallas_call(
        kern, grid=(2, nb), in_specs=[q, k, lasp, v, o, st] + [ANY] * nci, out_specs=[lasp, lasp, lasp, o] + [ANY] * nco,
        out_shape=[jax.ShapeDtypeStruct((s, 256), F32), jax.ShapeDtypeStruct((s, 256), F32),
                   jax.ShapeDtypeStruct((s, 256), F32), jax.ShapeDtypeStruct((s, 512), F32)] + comm.out_shape,
        scratch_shapes=[pltpu.VMEM((2, 128, 128), F32)] + comm.sems,
        compiler_params=pltpu.CompilerParams(dimension_semantics=("arbitrary", "arbitrary"),
                                             vmem_limit_bytes=VMEM_LIMIT),
        name="gla_bwd")(proj, proj, la, proj, d_o, states, *comm.ins)
    return res[0], res[1], res[2], res[3], res[4:]


def _causal_keep(t, qi, ki):
    row = lax.broadcasted_iota(jnp.int32, (t, t), 0) + qi * t
    col = lax.broadcasted_iota(jnp.int32, (t, t), 1) + ki * t
    return col <= row


def _split_refs(refs, counts):
    out, off = [], 0
    for cnt in counts:
        out.append(refs[off:off + cnt])
        off += cnt
    return out


def _attn_fwd(q, k, v, comm, tile=1024):
    s = q.shape[0]
    t = _tile(s, tile)
    n = s // t
    nci, nco = len(comm.ins), len(comm.out_shape)

    def kern(*refs):
        (q_ref, k_ref, v_ref), cins, (o_ref, lse_ref), couts, (m_sc, l_sc, acc_sc), csems = _split_refs(
            refs, (3, nci, 2, nco, 3, len(comm.sems)))
        qi, ki = pl.program_id(1), pl.program_id(2)
        place = _place()

        @pl.when((pl.program_id(0) == 0) & (qi == 0) & (ki == 0))
        def _():
            comm.start(place, cins, couts, csems)

        @pl.when((pl.program_id(0) == MLA_HEADS // 2 - 1) & (qi == 0) & (ki == 0))
        def _():
            comm.mid(place, cins, couts, csems)

        first = lax.broadcasted_iota(jnp.int32, (t, LANES), 1) < MLA_V

        @pl.when(ki == 0)
        def _():
            m_sc[...] = jnp.full_like(m_sc, -jnp.inf)
            l_sc[...] = jnp.zeros_like(l_sc)
            acc_sc[...] = jnp.zeros_like(acc_sc)

        def update(diagonal):
            keep = _causal_keep(t, 0, 0)
            alphas, pvs = [], []
            for h in range(2):
                sc = _dg(q_ref[:, 128 * h:128 * (h + 1)], k_ref[:, 128 * h:128 * (h + 1)], _NT)
                if diagonal:
                    sc = jnp.where(keep, sc, -jnp.inf)
                m_prev = m_sc[h]
                m_new = jnp.maximum(m_prev, jnp.max(sc, axis=1, keepdims=True))
                alpha = jnp.exp2(m_prev - m_new)
                p = jnp.exp2(sc - m_new[:, 0:1])
                l_sc[h] = alpha * l_sc[h] + jnp.sum(p, axis=1, keepdims=True)
                m_sc[h] = m_new
                alphas.append(alpha)
                pvs.append(_dg(p, v_ref[...], _NN))
            acc_sc[...] = acc_sc[...] * jnp.where(first, alphas[0], alphas[1]) + jnp.where(first, pvs[0], pvs[1])

        @pl.when(ki < qi)
        def _():
            update(False)

        @pl.when(ki == qi)
        def _():
            update(True)

        @pl.when(ki == qi)
        def _():
            l = jnp.where(first, l_sc[0], l_sc[1])
            m = jnp.where(first, m_sc[0], m_sc[1])
            o_ref[...] = acc_sc[...] / l
            lse_ref[...] = m + jnp.log2(l)

        @pl.when((pl.program_id(0) == MLA_HEADS // 2 - 1) & (qi == n - 1) & (ki == n - 1))
        def _():
            comm.finish(place, cins, couts, csems)

    kv_idx = lambda p, qi, ki: (jnp.minimum(ki, qi), p)
    res = pl.pallas_call(
        kern, grid=(MLA_HEADS // 2, n, n),
        in_specs=[pl.BlockSpec((t, 256), lambda p, qi, ki: (qi, p)), pl.BlockSpec((t, 256), kv_idx),
                  pl.BlockSpec((t, 128), kv_idx)] + [ANY] * nci,
        out_specs=[pl.BlockSpec((t, 128), lambda p, qi, ki: (qi, p)), pl.BlockSpec((t, 128), lambda p, qi, ki: (qi, p))]
        + [ANY] * nco,
        out_shape=[jax.ShapeDtypeStruct((s, 512), F32), jax.ShapeDtypeStruct((s, 512), F32)] + comm.out_shape,
        scratch_shapes=[pltpu.VMEM((2, t, LANES), F32), pltpu.VMEM((2, t, LANES), F32), pltpu.VMEM((t, LANES), F32)]
        + comm.sems,
        compiler_params=pltpu.CompilerParams(dimension_semantics=("arbitrary", "arbitrary", "arbitrary"),
                                             vmem_limit_bytes=VMEM_LIMIT),
        name="mla_attn_fwd")(q, k, v, *comm.ins)
    return res[0], res[1], res[2:]


def _attn_bwd(q, k, v, o, lse, dcat, comm, tile=512):
    s = q.shape[0]
    t = _tile(s, tile)
    n = s // t
    nci, nco = len(comm.ins), len(comm.out_shape)

    def kern(*refs):
        (q_ref, k_ref, v_ref, o_ref, lse_ref, do_ref), cins, (dq_ref, dk_ref, dv_ref), couts, (dk_sc, dv_sc), csems = \
            _split_refs(refs, (6, nci, 3, nco, 2, len(comm.sems)))
        ki, qi = pl.program_id(1), pl.program_id(2)
        place = _place()

        @pl.when((pl.program_id(0) == 0) & (qi == 0) & (ki == 0))
        def _():
            comm.start(place, cins, couts, csems)

        @pl.when((pl.program_id(0) == MLA_HEADS // 2 - 1) & (qi == 0) & (ki == 0))
        def _():
            comm.mid(place, cins, couts, csems)

        @pl.when((ki == 0) & (qi == 0))
        def _():
            dq_ref[...] = jnp.zeros_like(dq_ref)

        @pl.when(qi == ki)
        def _():
            dk_sc[...] = jnp.zeros_like(dk_sc)
            dv_sc[...] = jnp.zeros_like(dv_sc)

        def update(diagonal):
            keep = _causal_keep(t, 0, 0)
            d_o = do_ref[...]
            prod = d_o * o_ref[...]
            rows = pl.ds(pl.multiple_of(qi * t, t), t)
            for h in range(2):
                hs = slice(128 * h, 128 * (h + 1))
                mk = _lane_mask(MLA_V * h, MLA_V * (h + 1))
                qh, kh = q_ref[:, hs], k_ref[:, hs]
                sc = _dg(qh, kh, _NT)
                if diagonal:
                    sc = jnp.where(keep, sc, -jnp.inf)
                p = jnp.exp2(sc - lse_ref[:, MLA_V * h:MLA_V * h + 1])
                doh = d_o * mk
                dp = _dg(doh * LN2, v_ref[...], _NT)
                delta = jnp.sum(prod * mk, axis=1, keepdims=True) * LN2
                ds = p * (dp - delta)
                dv_sc[...] += _dg(p, doh, _TN)
                dk_sc[:, hs] += _dg(ds, qh, _TN)
                dq_ref[rows, hs] += _dg(ds, kh, _NN)

        @pl.when(qi > ki)
        def _():
            update(False)

        @pl.when(qi == ki)
        def _():
            update(True)

        @pl.when(qi == n - 1)
        def _():
            dk_ref[...] = dk_sc[...]
            dv_ref[...] = dv_sc[...].astype(dv_ref.dtype)

        @pl.when((pl.program_id(0) == MLA_HEADS // 2 - 1) & (qi == n - 1) & (ki == n - 1))
        def _():
            comm.finish(place, cins, couts, csems)

    q_idx = lambda p, ki, qi: (jnp.maximum(qi, ki), p)
    res = pl.pallas_call(
        kern, grid=(MLA_HEADS // 2, n, n),
        in_specs=[pl.BlockSpec((t, 256), q_idx), pl.BlockSpec((t, 256), lambda p, ki, qi: (ki, p)),
                  pl.BlockSpec((t, 128), lambda p, ki, qi: (ki, p)), pl.BlockSpec((t, 128), q_idx),
                  pl.BlockSpec((t, 128), q_idx),
                  pl.BlockSpec((t, 128), lambda p, ki, qi: (jnp.maximum(qi, ki), 4 + p))] + [ANY] * nci,
        out_specs=[pl.BlockSpec((s, 256), lambda p, ki, qi: (0, p)), pl.BlockSpec((t, 256), lambda p, ki, qi: (ki, p)),
                   pl.BlockSpec((t, 128), lambda p, ki, qi: (ki, p))] + [ANY] * nco,
        out_shape=[jax.ShapeDtypeStruct((s, 1024), F32), jax.ShapeDtypeStruct((s, 1024), F32),
                   jax.ShapeDtypeStruct((s, 512), BF16)] + comm.out_shape,
        scratch_shapes=[pltpu.VMEM((t, 256), F32), pltpu.VMEM((t, 128), F32)] + comm.sems,
        compiler_params=pltpu.CompilerParams(dimension_semantics=("arbitrary", "arbitrary", "arbitrary"),
                                             vmem_limit_bytes=VMEM_LIMIT),
        name="mla_attn_bwd")(q, k, v, o, lse, dcat, *comm.ins)
    return res[0], res[1], res[2], res[3:]


def _gate_fn(alr, w2, b):
    return _log_sigmoid(_dot_nn(alr, w2) + b) * (1.0 / GLA_GATE_NORM)


def _qk_head(qh, kh, kpe, c, sa, sb, qn, kn):
    kfull = kh + kpe * _lane_mask(MLA_NOPE, MLA_QK)
    q_r = _rope(_rms(qh, qn, MLA_QK), c, sa, sb) * (MLA_QK ** -0.5 * LOG2E)
    k_r = _rope(_rms(kfull, kn, MLA_QK), c, sa, sb)
    return q_r, k_r


def _mix_head(o, og, gn):
    return _rms(o, gn) * _silu(og)


def _xa_head(xq, xk, xv, qn, kn):
    sc = _dot_nt(_rms(xq, qn), _rms(xk, kn)) * (XA_DIM ** -0.5)
    e = jnp.exp(sc - lax.stop_gradient(jnp.max(sc, axis=1, keepdims=True)))
    p = e / jnp.sum(e, axis=1, keepdims=True)
    return _dot_nn(p, xv)


def _heads(x, n):
    return [x[:, 128 * h:128 * (h + 1)] for h in range(n)]


def _cat(xs):
    return jnp.concatenate(xs, axis=1)


def _norm_fwd(x, w, name):
    return _rows_call(lambda r, c: ([_rms(r[0], c[0])], []), [_row(x)], [w], [(x.shape[1], BF16)], name=name)[0]


def _norm_bwd_epilogue(x, w, add):
    def fn(d_out, rows, consts):
        _, vjp = jax.vjp(_rms, rows[0], consts[0])
        dx, dw = vjp(d_out)
        return [dx + rows[1]], [dw]

    return _Epilogue(fn, [x, add], [w], [F32], [w.shape])


def _norm_bwd(x, w, d_out, add, name):
    def body(r, c):
        _, vjp = jax.vjp(_rms, r[0], c[0])
        dx, dw = vjp(r[1])
        return [dx + r[2]], [dw]

    return _rows_call(body, [_row(x), _row(d_out), _row(add)], [w], [(x.shape[1], F32)], [w.shape], name=name)


CONV_HALO = BF16_ROWS


def _conv_specs(s, f, t):
    n8 = t // CONV_HALO
    cur = pl.BlockSpec((None, t, f), lambda j, i: (j, i, 0))
    prev = pl.BlockSpec((None, CONV_HALO, f), lambda j, i: (j, jnp.maximum(i * n8 - 1, 0), 0))
    nxt = pl.BlockSpec((None, CONV_HALO, f), lambda j, i: (j, jnp.minimum((i + 1) * n8, s // CONV_HALO - 1), 0))
    cw = pl.BlockSpec((None, 3, f), lambda j, i: (j, 0, 0))
    cb = pl.BlockSpec((None, 1, f), lambda j, i: (j, 0, 0))
    return cur, prev, nxt, cw, cb


def _conv_taps(g, prev, first):
    ext = jnp.concatenate([jnp.where(first, 0.0, prev.astype(F32)), g], axis=0)
    return pltpu.roll(ext, 1, 0)[CONV_HALO:], pltpu.roll(ext, 2, 0)[CONV_HALO:]


def _conv_fwd(gg, uu, cw, cb):
    _, s, f = gg.shape
    t = _tile(s, 512)

    def kern(g_ref, gp_ref, u_ref, cw_ref, cb_ref, o_ref):
        g = g_ref[...].astype(F32)
        g1, g2 = _conv_taps(g, gp_ref[...], pl.program_id(1) == 0)
        w = cw_ref[...]
        gc = cb_ref[...] + w[0:1] * g2 + w[1:2] * g1 + w[2:3] * g
        o_ref[...] = (_silu(gc) * u_ref[...].astype(F32)).astype(o_ref.dtype)

    cur, prev, _, cws, cbs = _conv_specs(s, f, t)
    return pl.pallas_call(
        kern, grid=(4, s // t), in_specs=[cur, prev, cur, cws, cbs], out_specs=cur,
        out_shape=jax.ShapeDtypeStruct(gg.shape, BF16),
        compiler_params=pltpu.CompilerParams(dimension_semantics=("parallel", "parallel"), vmem_limit_bytes=VMEM_LIMIT),
        name="ffn_conv_fwd")(gg, gg, uu, cw, cb)


def _conv_bwd_gate(gg, uu, dact, cw, cb):
    _, s, f = gg.shape
    t = _tile(s, 512)

    def kern(g_ref, gp_ref, u_ref, da_ref, cw_ref, cb_ref, du_ref, dgc_ref, dcw_ref, dcb_ref):
        i = pl.program_id(1)
        g, u, da = g_ref[...].astype(F32), u_ref[...].astype(F32), da_ref[...].astype(F32)
        g1, g2 = _conv_taps(g, gp_ref[...], i == 0)
        w = cw_ref[...]
        gc = cb_ref[...] + w[0:1] * g2 + w[1:2] * g1 + w[2:3] * g
        sg = jax.nn.sigmoid(gc)
        du_ref[...] = (da * (gc * sg)).astype(du_ref.dtype)
        dgc = da * u * (sg * (1.0 + gc * (1.0 - sg)))
        dgc_ref[...] = dgc.astype(dgc_ref.dtype)

        @pl.when(i == 0)
        def _():
            dcw_ref[...] = jnp.zeros_like(dcw_ref)
            dcb_ref[...] = jnp.zeros_like(dcb_ref)

        dcw_ref[0:1, :] += jnp.sum(dgc * g2, axis=0, keepdims=True)
        dcw_ref[1:2, :] += jnp.sum(dgc * g1, axis=0, keepdims=True)
        dcw_ref[2:3, :] += jnp.sum(dgc * g, axis=0, keepdims=True)
        dcb_ref[...] += jnp.sum(dgc, axis=0, keepdims=True)

    cur, prev, _, cws, cbs = _conv_specs(s, f, t)
    return pl.pallas_call(
        kern, grid=(4, s // t), in_specs=[cur, prev, cur, cur, cws, cbs], out_specs=[cur, cur, cws, cbs],
        out_shape=[jax.ShapeDtypeStruct(gg.shape, BF16), jax.ShapeDtypeStruct(gg.shape, BF16),
                   jax.ShapeDtypeStruct(cw.shape, F32), jax.ShapeDtypeStruct(cb.shape, F32)],
        compiler_params=pltpu.CompilerParams(dimension_semantics=("parallel", "arbitrary"), vmem_limit_bytes=VMEM_LIMIT),
        name="ffn_conv_bwd_gate")(gg, gg, uu, dact, cw, cb)


def _conv_bwd_taps(dgc, cw):
    _, s, f = dgc.shape
    t = _tile(s, 512)
    nt = s // t

    def kern(d_ref, dn_ref, cw_ref, o_ref):
        d = d_ref[...].astype(F32)
        ext = jnp.concatenate([d, jnp.where(pl.program_id(1) == nt - 1, 0.0, dn_ref[...].astype(F32))], axis=0)
        up1 = pltpu.roll(ext, t + CONV_HALO - 1, 0)[:t]
        up2 = pltpu.roll(ext, t + CONV_HALO - 2, 0)[:t]
        w = cw_ref[...]
        o_ref[...] = (w[2:3] * d + w[1:2] * up1 + w[0:1] * up2).astype(o_ref.dtype)

    cur, _, nxt, cws, _ = _conv_specs(s, f, t)
    return pl.pallas_call(
        kern, grid=(4, nt), in_specs=[cur, nxt, cws], out_specs=cur, out_shape=jax.ShapeDtypeStruct(dgc.shape, BF16),
        compiler_params=pltpu.CompilerParams(dimension_semantics=("parallel", "parallel"), vmem_limit_bytes=VMEM_LIMIT),
        name="ffn_conv_bwd_taps")(dgc, dgc, cw)


def _rope_tables(pos):
    half = MLA_ROPE // 2
    inv = ROPE_THETA ** (-jnp.arange(half, dtype=F32) / half)
    ang = pos.astype(F32)[:, None] * inv
    cos, sin = jnp.cos(ang), jnp.sin(ang)
    s = pos.shape[0]
    z = lambda w: jnp.zeros((s, w), F32)
    c = jnp.concatenate([jnp.ones((s, MLA_NOPE), F32), cos, cos, jnp.ones((s, LANES - MLA_QK), F32)], axis=1)
    sa = jnp.concatenate([z(MLA_NOPE), -sin, z(half), z(LANES - MLA_QK)], axis=1)
    sb = jnp.concatenate([z(MLA_NOPE), z(half), sin, z(LANES - MLA_QK)], axis=1)
    return c, sa, sb


def _local_step(x, mem, pos, target, w, late_shards):
    g = {}
    w = dict(w)
    c, sa, sb = _rope_tables(pos)

    xn = _norm_fwd(x, w["norm_mix"], "norm_mix_fwd")
    proj = _matmul(xn, w["in"], "nn", F32, "proj_fwd")
    alr = _row(proj, 128, P_ALR // 128)
    kpe = _row(proj, 128, P_KPE // 128)
    og = _row(proj, 512, P_OG // 512)
    cq = _row(proj, 256, P_CQ // 256)
    ckv = _row(proj, 128, P_CKV // 128)

    la = _rows_call(lambda r, k: ([_gate_fn(r[0], k[0], k[1])], []), [alr], [w["w2"], w["gate_b"]],
                    [(256, F32)], name="gla_gate_fwd")[0]
    o_gla, states = _gla_fwd(proj, la)

    q_lat, kv_lat = _rows_call(lambda r, k: ([_rms(r[0], k[0]), _rms(r[1], k[1])], []), [cq, ckv],
                               [w["q_a_norm"], w["kv_a_norm"]], [(256, BF16), (128, BF16)], name="mla_lat_fwd")
    q_up = _matmul(q_lat, w["uq"], "nn", F32, "mla_q_fwd")
    k_up = _matmul(kv_lat, w["k"], "nn", F32, "mla_k_fwd")
    v_mla = _matmul(kv_lat, w["v"], "nn", BF16, "mla_v_fwd")

    def qk_body(r, k):
        qs, ks = [], []
        for qh, kh in zip(_heads(r[0], MLA_HEADS), _heads(r[1], MLA_HEADS)):
            a, b = _qk_head(qh, kh, r[2], r[3], r[4], r[5], k[0], k[1])
            qs.append(a)
            ks.append(b)
        return [_cat(qs), _cat(ks)], []

    tabs = [_row(c), _row(sa), _row(sb)]
    q_r, k_r = _rows_call(qk_body, [_row(q_up), _row(k_up), kpe] + tabs, [w["q_norm"], w["k_norm"]],
                          [(1024, BF16), (1024, BF16)], name="mla_qk_fwd")
    o_mla, lse, gathered = _attn_fwd(q_r, k_r, v_mla, _gather_plan(late_shards))
    w.update(_late_layout(dict(zip(LATE, gathered, strict=True))))

    def mix_body(r, k):
        ys = [_mix_head(o, g_, k[0]) for o, g_ in zip(_heads(r[0], GLA_HEADS), _heads(r[1], GLA_HEADS))]
        return [_cat(ys + [r[2]])], []

    cat = _rows_call(mix_body, [_row(o_gla), og, _row(o_mla)], [w["gla_out_norm"]], [(1024, BF16)],
                     name="mix_fwd")[0]
    h1 = _matmul(cat, w["out"], "nn", F32, "out_fwd", residual=x)

    hn = _norm_fwd(h1, w["norm_xa"], "norm_xa_fwd")
    mn = _norm_fwd(mem, w["norm_mem"], "norm_mem_fwd")
    xq = _matmul(hn, w["xq"], "nn", F32, "xa_q_fwd")
    xkv = _matmul(mn, w["xkv"], "nn", F32, "xa_kv_fwd")

    def xa_body(r, k):
        ks, vs = _heads(k[0], 2 * XA_HEADS)[:XA_HEADS], _heads(k[0], 2 * XA_HEADS)[XA_HEADS:]
        return [_cat([_xa_head(a, b, v_, k[1], k[2]) for a, b, v_ in zip(_heads(r[0], XA_HEADS), ks, vs)])], []

    xo = _rows_call(xa_body, [_row(xq)], [xkv, w["xa_q_norm"], w["xa_k_norm"]], [(512, BF16)], name="xa_fwd")[0]
    h2 = _matmul(xo, w["xo"], "nn", F32, "xa_o_fwd", residual=h1)

    fn = _norm_fwd(h2, w["norm_ffn"], "norm_ffn_fwd")
    gg = _matmul(fn, w["wg"], "nt", BF16, "ffn_gate_fwd", b_lead="p")
    uu = _matmul(fn, w["wu"], "nt", BF16, "ffn_up_fwd", b_lead="p")
    act = _conv_fwd(gg, uu, w["cw"], w["cb"])
    def loss_fn(y, rows, consts):
        err = y - rows[0]
        part = 0.5 * jnp.sum(jnp.sum(err * err, axis=1, keepdims=True) * (1.0 / D_MODEL), axis=0, keepdims=True)
        return [err * (1.0 / D_MODEL)], [jnp.broadcast_to(part, (1, LANES))]

    dy, loss = _matmul(act, w["wd"], "nn", F32, "ffn_down_fwd_loss", residual=h2, a_lead="k", b_lead="k",
                       epilogue=_Epilogue(loss_fn, [target], [], [F32], [(1, LANES)]))

    g["ffn_w_down"] = _matmul(act, dy, "tn", BF16, "ffn_down_dw", a_lead="p")
    dact = _matmul(dy, w["wd"], "nt", BF16, "ffn_down_dx", b_lead="p")
    duu, dgc, g["ffn_conv_w"], g["ffn_conv_b"] = _conv_bwd_gate(gg, uu, dact, w["cw"], w["cb"])
    dgg = _conv_bwd_taps(dgc, w["cw"])
    g["ffn_w_gate"] = _matmul(dgg, fn, "tn", BF16, "ffn_gate_dw", a_lead="p")
    g["ffn_w_up"] = _matmul(duu, fn, "tn", BF16, "ffn_up_dw", a_lead="p")
    dh2, g["norm_ffn"] = _matmul(dgg, w["wg"], "nn", F32, "ffn_dx_norm_bwd", a_lead="k", b_lead="k", more=(duu, w["wu"]),
                                 epilogue=_norm_bwd_epilogue(h2, w["norm_ffn"], dy))

    g["xa_w_o"] = _matmul(xo, dh2, "tn", BF16, "xa_o_dw")
    dxo = _matmul(dh2, w["xo"], "nt", F32, "xa_o_dx")

    def xa_bwd(r, k):
        kvh = _heads(k[0], 2 * XA_HEADS)
        dq_, dk_, dv_ = [], [], []
        dqn, dkn = 0.0, 0.0
        for h, (a, d_) in enumerate(zip(_heads(r[0], XA_HEADS), _heads(r[1], XA_HEADS))):
            _, vjp = jax.vjp(_xa_head, a, kvh[h], kvh[XA_HEADS + h], k[1], k[2])
            ga, gk, gv, gqn, gkn = vjp(d_)
            dq_.append(ga)
            dk_.append(gk)
            dv_.append(gv)
            dqn, dkn = dqn + gqn, dkn + gkn
        return [_cat(dq_)], [_cat(dk_ + dv_), dqn, dkn]

    dxq, dxkv, g["xa_q_norm"], g["xa_k_norm"] = _rows_call(
        xa_bwd, [_row(xq), _row(dxo)], [xkv, w["xa_q_norm"], w["xa_k_norm"]], [(512, BF16)],
        [xkv.shape, (1, 128), (1, 128)], name="xa_bwd")
    g["xa_w_q"] = _matmul(hn, dxq, "tn", BF16, "xa_q_dw")
    dh1, g["norm_xa"] = _matmul(dxq, w["xq"], "nt", F32, "xa_q_dx_norm_bwd",
                                epilogue=_norm_bwd_epilogue(h1, w["norm_xa"], dh2))
    g["xa_w_kv"] = _matmul(mn, dxkv, "tn", BF16, "xa_kv_dw")
    dmn = _matmul(dxkv, w["xkv"], "nt", F32, "xa_kv_dx")
    _, g["norm_mem"] = _norm_bwd(mem, w["norm_mem"], dmn, dmn, "norm_mem_bwd")

    g["w_out"] = _matmul(cat, dh1, "tn", BF16, "out_dw")
    dcat = _matmul(dh1, w["out"], "nt", F32, "out_dx")

    def mix_bwd(r, k):
        do_, dog_ = [], []
        dgn = 0.0
        for o, g_, d_ in zip(_heads(r[0], GLA_HEADS), _heads(r[1], GLA_HEADS), _heads(r[2], GLA_HEADS)):
            _, vjp = jax.vjp(_mix_head, o, g_, k[0])
            a, b, gn_ = vjp(d_)
            do_.append(a)
            dog_.append(b)
            dgn = dgn + gn_
        return [_cat(do_), _cat(dog_)], [dgn]

    do_gla, d_og, g["gla_out_norm"] = _rows_call(mix_bwd, [_row(o_gla), og, _row(dcat, 512, 0)], [w["gla_out_norm"]],
                                                 [(512, F32), (512, BF16)], [(1, 128)], name="mix_bwd")

    late_parts = _late_grad_shards(g)
    dq_r, dk_r, dv_mla, lands_mlp = _attn_bwd(q_r, k_r, v_mla, o_mla, lse, dcat,
                                              _scatter_plan([late_parts[n] for n in LATE_MLP]))

    def qk_bwd(r, k):
        dqs, dks = [], []
        dkpe, dqn, dkn = 0.0, 0.0, 0.0
        for qh, kh, dqh, dkh in zip(_heads(r[0], MLA_HEADS), _heads(r[1], MLA_HEADS), _heads(r[6], MLA_HEADS),
                                    _heads(r[7], MLA_HEADS)):
            _, vjp = jax.vjp(lambda a, b, e, f, h_: _qk_head(a, b, e, r[3], r[4], r[5], f, h_), qh, kh, r[2], k[0], k[1])
            ga, gb, ge, gf, gh = vjp((dqh, dkh))
            dqs.append(ga)
            dks.append(gb)
            dkpe, dqn, dkn = dkpe + ge, dqn + gf, dkn + gh
        return [_cat(dqs), _cat(dks), dkpe], [dqn, dkn]

    dq_up, dk_up, d_kpe, g["q_norm"], g["k_norm"] = _rows_call(
        qk_bwd, [_row(q_up), _row(k_up), kpe] + tabs + [_row(dq_r), _row(dk_r)], [w["q_norm"], w["k_norm"]],
        [(1024, BF16), (1024, BF16), (128, BF16)], [(1, 128), (1, 128)], name="mla_qk_bwd")
    g["uq"] = _matmul(q_lat, dq_up, "tn", BF16, "mla_q_dw")
    dq_lat = _matmul(dq_up, w["uq"], "nt", F32, "mla_q_dx")
    g["k"] = _matmul(kv_lat, dk_up, "tn", BF16, "mla_k_dw")
    g["v"] = _matmul(kv_lat, dv_mla, "tn", BF16, "mla_v_dw")
    dkv_lat = _matmul(dk_up, w["k"], "nt", F32, "mla_k_dx")
    dkv_lat = _matmul(dv_mla, w["v"], "nt", F32, "mla_v_dx", residual=dkv_lat)

    def lat_bwd(r, k):
        _, vjp1 = jax.vjp(_rms, r[0], k[0])
        _, vjp2 = jax.vjp(_rms, r[1], k[1])
        a, ga = vjp1(r[2])
        b, gb = vjp2(r[3])
        return [a, b], [ga, gb]

    d_cq, d_ckv, g["mla_q_a_norm"], g["mla_kv_a_norm"] = _rows_call(
        lat_bwd, [cq, ckv, _row(dq_lat), _row(dkv_lat)], [w["q_a_norm"], w["kv_a_norm"]],
        [(256, BF16), (128, BF16)], [(1, 256), (1, 128)], name="mla_lat_bwd")

    dgq, dgk, dla, dgv, lands_mix = _gla_bwd(proj, la, states, do_gla, _scatter_plan([late_parts[n] for n in LATE_MIX]))
    lands_late = dict(zip(LATE_MLP + LATE_MIX, list(lands_mlp) + list(lands_mix), strict=True))

    def gate_bwd(r, k):
        _, vjp = jax.vjp(_gate_fn, r[0], k[0], k[1])
        a, gw, gb = vjp(r[1])
        return [a], [gw, gb]

    d_alr, g["w2"], g["gla_gate_b"] = _rows_call(gate_bwd, [alr, _row(dla)], [w["w2"], w["gate_b"]], [(128, BF16)],
                                                 [(128, 256), (1, 256)], name="gla_gate_bwd")

    dproj = jnp.concatenate([dgq.astype(BF16), dgk.astype(BF16), dgv.astype(BF16), d_og, d_cq, d_ckv, d_kpe, d_alr],
                            axis=1)
    g["in"] = _matmul(xn, dproj, "tn", BF16, "proj_dw")
    dx, g["norm_mix"] = _matmul(dproj, w["in"], "nt", F32, "proj_dx_norm_bwd",
                                epilogue=_norm_bwd_epilogue(x, w["norm_mix"], dh1))
    return loss[0, 0], dx, g, lands_late


def _join_shards(pieces, axis):
    if axis == 0:
        return pieces.reshape(-1, pieces.shape[2])
    return jnp.transpose(pieces, (1, 0, 2)).reshape(pieces.shape[1], -1)


def _split_shards(full, axis):
    r, c = full.shape
    if axis == 0:
        return full.reshape(4, r // 4, c)
    return jnp.transpose(full.reshape(r, 4, c // 4), (1, 0, 2))


def _early_layout(gath, rep):
    w_in = _join_shards(gath["w_in"], 1)
    z = lambda n: jnp.zeros((D_MODEL, n), w_in.dtype)
    seg = lambda lo, n: w_in[:, lo:lo + n]
    ukv = _join_shards(gath["mla_w_ukv"], 1).reshape(MLA_KV_RANK, MLA_HEADS, MLA_NOPE + MLA_V)
    w = {
        "in": jnp.concatenate([seg(N_GQ, 256), seg(N_GK, 256), seg(N_GV, 512), seg(N_OG, 512), seg(N_CQ, 256),
                               seg(N_CKV, 128), z(64), seg(N_KPE, 32), z(32), seg(N_ALR, 16), z(112)], axis=1),
        "uq": jnp.pad(_join_shards(gath["mla_w_uq"], 1).reshape(MLA_Q_RANK, MLA_HEADS, MLA_QK),
                      ((0, 0), (0, 0), (0, LANES - MLA_QK))).reshape(MLA_Q_RANK, MLA_HEADS * LANES),
        "k": jnp.pad(ukv[:, :, :MLA_NOPE], ((0, 0), (0, 0), (0, LANES - MLA_NOPE))).reshape(MLA_KV_RANK, -1),
        "v": ukv[:, :, MLA_NOPE:].reshape(MLA_KV_RANK, MLA_HEADS * MLA_V),
        "w2": jnp.pad(_join_shards(gath["gla_gate_w2"], 1), ((0, LANES - GLA_RANK), (0, 0))),
        "cb": rep["ffn_conv_b"].reshape(4, 1, D_FF // 4),
        "q_norm": jnp.pad(rep["mla_q_norm"], ((0, 0), (0, LANES - MLA_QK))),
        "k_norm": jnp.pad(rep["mla_k_norm"], ((0, 0), (0, LANES - MLA_QK))),
        "q_a_norm": rep["mla_q_a_norm"], "kv_a_norm": rep["mla_kv_a_norm"], "gate_b": rep["gla_gate_b"],
    }
    for n in ("norm_mix", "gla_out_norm", "norm_xa", "norm_mem", "xa_q_norm", "xa_k_norm", "norm_ffn"):
        w[n] = rep[n]
    return w


def _late_layout(gath):
    return {"out": _join_shards(gath["w_out"], 0), "xq": _join_shards(gath["xa_w_q"], 0),
            "xkv": _join_shards(gath["xa_w_kv"], 0), "xo": _join_shards(gath["xa_w_o"], 1),
            "wg": gath["ffn_w_gate"], "wu": gath["ffn_w_up"], "wd": gath["ffn_w_down"], "cw": gath["ffn_conv_w"]}


def _late_grad_shards(g):
    sh = {"w_out": _split_shards(g["w_out"], 0), "xa_w_q": _split_shards(g["xa_w_q"], 0),
          "xa_w_kv": _split_shards(g["xa_w_kv"], 0), "xa_w_o": _split_shards(g["xa_w_o"], 1),
          "ffn_w_gate": g["ffn_w_gate"], "ffn_w_up": g["ffn_w_up"], "ffn_conv_w": g["ffn_conv_w"],
          "ffn_w_down": g["ffn_w_down"]}
    return {n: v.astype(BF16) for n, v in sh.items()}


def _early_grad_shards(g):
    gi = g["in"]
    seg = lambda lo, n: gi[:, lo:lo + n]
    w_in = jnp.concatenate([seg(P_GQ, 256), seg(P_GK, 256), seg(P_GV, 512), seg(P_ALR, 16), seg(P_OG, 512),
                            seg(P_CQ, 256), seg(P_CKV, 128), seg(P_KPE + 64, 32)], axis=1)
    uq = g["uq"].reshape(MLA_Q_RANK, MLA_HEADS, LANES)[:, :, :MLA_QK].reshape(MLA_Q_RANK, -1)
    ukv = jnp.concatenate([g["k"].reshape(MLA_KV_RANK, MLA_HEADS, LANES)[:, :, :MLA_NOPE],
                           g["v"].reshape(MLA_KV_RANK, MLA_HEADS, MLA_V)], axis=2).reshape(MLA_KV_RANK, -1)
    sh = {"w_in": _split_shards(w_in, 1), "gla_gate_w2": _split_shards(g["w2"][:GLA_RANK], 1),
          "mla_w_uq": _split_shards(uq, 1), "mla_w_ukv": _split_shards(ukv, 1)}
    sh = {n: v.astype(BF16) for n, v in sh.items()}
    rep = {n: g[n] for n in REPLICATED if n in g}
    rep["mla_q_norm"] = g["q_norm"][:, :MLA_QK]
    rep["mla_k_norm"] = g["k_norm"][:, :MLA_QK]
    rep["ffn_conv_b"] = g["ffn_conv_b"].reshape(1, D_FF)
    return sh, rep


SMALL_SHAPE = (8, 1024)


def _pack_small(vectors):
    flat = jnp.concatenate(vectors, axis=1)
    return jnp.pad(flat, ((0, 0), (0, SMALL_SHAPE[0] * SMALL_SHAPE[1] - flat.shape[1]))).reshape(SMALL_SHAPE)


def _unpack_small(buf, widths):
    flat = buf.reshape(1, -1)
    out, off = [], 0
    for wd in widths:
        out.append(flat[:, off:off + wd])
        off += wd
    return out


ANY = pl.BlockSpec(memory_space=pl.ANY)


def _place():
    x, y, c = lax.axis_index("x"), lax.axis_index("y"), lax.axis_index("c")
    chips = [(1 - x, y), (x, 1 - y), (1 - x, 1 - y)]
    return x, y, c, chips


class _Comm:
    def __init__(self, ins, out_shape, sems, start, finish, mid=None):
        self.ins, self.out_shape, self.sems = list(ins), list(out_shape), list(sems)
        self.start, self.finish, self.mid = start, finish, mid or (lambda *args: None)


def _run_comm(plan, name):
    ni, no = len(plan.ins), len(plan.out_shape)

    def body(*refs):
        ins, outs, sems = refs[:ni], refs[ni:ni + no], refs[ni + no:]
        place = _place()
        plan.start(place, ins, outs, sems)
        plan.mid(place, ins, outs, sems)
        plan.finish(place, ins, outs, sems)

    return pl.pallas_call(body, in_specs=[ANY] * ni, out_specs=[ANY] * no, out_shape=plan.out_shape,
                          scratch_shapes=plan.sems, name=name)(*plan.ins)


def _gather_plan(shards):
    n = len(shards)
    split = [s.shape[0] % (2 * BF16_ROWS) == 0 for s in shards]

    def rows(ref, t, c):
        if not split[t]:
            return ref
        half = shards[t].shape[0] // 2
        return ref.at[pl.ds(pl.multiple_of(c * half, BF16_ROWS), half)]

    def remote(src, dst, ss, rs, to):
        return pltpu.make_async_remote_copy(src_ref=src, dst_ref=dst, send_sem=ss, recv_sem=rs, device_id=to,
                                            device_id_type=MESH)

    def first_wave(place, ins, outs, sems):
        x, y, c, chips = place
        ici_s, ici_r, _, _, local = sems
        me = 2 * x + y
        own = [pltpu.make_async_copy(ins[t], outs[t].at[me], local.at[t]) for t in range(n)]
        push = [remote(rows(ins[t], t, c), rows(outs[t].at[me], t, c), ici_s.at[3 * t + j], ici_r.at[3 * t + j], (px, py, c))
                for t in range(n) for j, (px, py) in enumerate(chips)]
        return own, push

    def second_wave(place, ins, outs, sems, last):
        x, y, c, chips = place
        ici_s, ici_r, d2d_s, d2d_r, local = sems
        sib = (x, y, 1 - c)
        out = []
        for t in range(n):
            for j, (px, py) in enumerate(chips):
                block = outs[t].at[2 * px + py]
                got = rows(block, t, c)
                if split[t]:
                    hand = remote(got, got, d2d_s.at[3 * t + j], d2d_r.at[3 * t + j], sib)
                    theirs = rows(block, t, 1 - c)
                    other = (remote(theirs, theirs, local.at[0], d2d_r.at[3 * t + j], sib) if last else
                             remote(got, got, local.at[0], ici_r.at[3 * t + j], sib))
                    out.append((other, hand))
                elif last:
                    out.append((remote(got, got, local.at[0], ici_r.at[3 * t + j], sib), None))
        return out

    def start(place, ins, outs, sems):
        own, push = first_wave(place, ins, outs, sems)
        for cp in own + push:
            cp.start()

    def mid(place, ins, outs, sems):
        for arrival, hand in second_wave(place, ins, outs, sems, False):
            arrival.wait_recv()
            hand.start()

    def finish(place, ins, outs, sems):
        own, push = first_wave(place, ins, outs, sems)
        for arrival, hand in second_wave(place, ins, outs, sems, True):
            arrival.wait_recv()
            if hand is not None:
                hand.wait_send()
        for cp in push:
            cp.wait_send()
        for cp in own:
            cp.wait()

    dma = pltpu.SemaphoreType.DMA
    return _Comm(shards, [jax.ShapeDtypeStruct((4,) + s.shape, s.dtype) for s in shards],
                 [dma((3 * n,)), dma((3 * n,)), dma((3 * n,)), dma((3 * n,)), dma((n,))], start, finish, mid)


def _scatter_plan(parts, small=None):
    n = len(parts)
    ns = 0 if small is None else 1

    def unpack(place, ins, outs, sems):
        x, y, c, chips = place
        return x, y, c, chips, 2 * x + y, 4 * x + 2 * y + c, (x, y, 1 - c)

    def remote(src, dst, ss, rs, to):
        return pltpu.make_async_remote_copy(src_ref=src, dst_ref=dst, send_sem=ss, recv_sem=rs, device_id=to,
                                            device_id_type=MESH)

    def first_wave(place, ins, outs, sems):
        x, y, c, chips, me, dev, sib = unpack(place, ins, outs, sems)
        ici_s, ici_r, d2d_s, d2d_r, sm_s, sm_r, local = sems
        own, push = [], []
        if ns:
            own.append(pltpu.make_async_copy(ins[n], outs[n].at[dev], local.at[n]))
            for k in range(1, 8):
                px = (1 - x) if (k >> 2) & 1 else x
                py = (1 - y) if (k >> 1) & 1 else y
                pc = (1 - c) if k & 1 else c
                push.append(remote(ins[n], outs[n].at[dev], sm_s.at[k - 1], sm_r.at[k - 1], (px, py, pc)))
        for t in range(n):
            own.append(pltpu.make_async_copy(ins[t].at[me], outs[t].at[dev], local.at[t]))
            push.append(remote(ins[t].at[me], outs[t].at[dev], d2d_s.at[4 * t], d2d_r.at[4 * t], sib))
            for j, (px, py) in enumerate(chips):
                push.append(remote(ins[t].at[2 * px + py], outs[t].at[dev], ici_s.at[3 * t + j], ici_r.at[3 * t + j],
                                   (px, py, c)))
        return own, push

    def start(place, ins, outs, sems):
        own, push = first_wave(place, ins, outs, sems)
        for cp in own + push:
            cp.start()

    def landed(dst, rs, sems, sib):
        remote(dst, dst, sems[-1].at[0], rs, sib).wait_recv()

    def forwards(place, ins, outs, sems):
        x, y, c, chips, me, dev, sib = unpack(place, ins, outs, sems)
        d2d_s, d2d_r = sems[2], sems[3]
        slots = [(t, j, outs[t].at[4 * px + 2 * py + c]) for t in range(n) for j, (px, py) in enumerate(chips)]
        return [(t, j, slot, remote(slot, slot, d2d_s.at[4 * t + 1 + j], d2d_r.at[4 * t + 1 + j], sib))
                for t, j, slot in slots]

    def mid(place, ins, outs, sems):
        sib = unpack(place, ins, outs, sems)[-1]
        for t, j, slot, cp in forwards(place, ins, outs, sems):
            landed(slot, sems[1].at[3 * t + j], sems, sib)
            cp.start()

    def finish(place, ins, outs, sems):
        x, y, c, chips, me, dev, sib = unpack(place, ins, outs, sems)
        d2d_r, sm_r = sems[3], sems[5]
        own, push = first_wave(place, ins, outs, sems)
        push += [cp for _, _, _, cp in forwards(place, ins, outs, sems)]
        for t in range(n):
            landed(outs[t].at[4 * x + 2 * y + (1 - c)], d2d_r.at[4 * t], sems, sib)
            for j, (px, py) in enumerate(chips):
                landed(outs[t].at[4 * px + 2 * py + (1 - c)], d2d_r.at[4 * t + 1 + j], sems, sib)
        if ns:
            for k in range(1, 8):
                px = (1 - x) if (k >> 2) & 1 else x
                py = (1 - y) if (k >> 1) & 1 else y
                pc = (1 - c) if k & 1 else c
                landed(outs[n].at[4 * px + 2 * py + pc], sm_r.at[k - 1], sems, sib)
        for cp in push:
            cp.wait_send()
        for cp in own:
            cp.wait()

    dma = pltpu.SemaphoreType.DMA
    ins = list(parts) + ([small] if ns else [])
    out_shape = [jax.ShapeDtypeStruct((8,) + p.shape[1:], p.dtype) for p in parts]
    if ns:
        out_shape.append(jax.ShapeDtypeStruct((8,) + small.shape, small.dtype))
    return _Comm(ins, out_shape, [dma((3 * n,)), dma((3 * n,)), dma((4 * n,)), dma((4 * n,)), dma((7,)), dma((7,)),
                                  dma((n + 1,))], start, finish, mid)


ADAM_ROWS = 288


def _row_tile(r, cap):
    if r <= cap:
        return r
    return max(t for t in range(8, cap + 1, 8) if r % t == 0)


def _adamw_update(w, m, v, land):
    g = land[0].astype(F32)
    for i in range(1, 8):
        g = g + land[i].astype(F32)
    m_new = ADAM_B1 * m + (1.0 - ADAM_B1) * g
    v_new = ADAM_B2 * v + (1.0 - ADAM_B2) * (g * g)
    m_hat = m_new / (1.0 - ADAM_B1 ** ADAM_STEP)
    v_hat = v_new / (1.0 - ADAM_B2 ** ADAM_STEP)
    return g, -ADAM_LR * (m_hat / (jnp.sqrt(v_hat) + ADAM_EPS) + ADAM_WD * w), m_new, v_new


def _adamw(tensors, name, comm=None):
    k = len(tensors)
    r, c = tensors[0][0].shape
    t = _row_tile(r, ADAM_ROWS // k)
    n = r // t
    nci, nco, nsem = (len(comm.ins), len(comm.out_shape), len(comm.sems)) if comm else (0, 0, 0)

    def kern(*refs):
        ins, cins, outs, couts, csems = _split_refs(refs, (4 * k, nci, 4 * k, nco, nsem))
        if comm:
            place = _place()

            @pl.when(pl.program_id(0) == 0)
            def _():
                comm.start(place, cins, couts, csems)

        for i in range(k):
            w_ref, m_ref, v_ref, l_ref = ins[4 * i:4 * i + 4]
            res = _adamw_update(w_ref[...], m_ref[...], v_ref[...], l_ref)
            for ref, val in zip(outs[4 * i:4 * i + 4], res, strict=True):
                ref[...] = val
        if comm:
            @pl.when(pl.program_id(0) == n - 1)
            def _():
                comm.mid(place, cins, couts, csems)
                comm.finish(place, cins, couts, csems)

    spec = pl.BlockSpec((t, c), lambda i: (i, 0))
    lspec = pl.BlockSpec((8, t, c), lambda i: (0, i, 0))
    res = pl.pallas_call(
        kern, grid=(n,), in_specs=[spec, spec, spec, lspec] * k + [ANY] * nci, out_specs=[spec] * (4 * k) + [ANY] * nco,
        out_shape=[jax.ShapeDtypeStruct((r, c), F32)] * (4 * k) + (comm.out_shape if comm else []),
        scratch_shapes=comm.sems if comm else [],
        compiler_params=pltpu.CompilerParams(dimension_semantics=("arbitrary" if comm else "parallel",),
                                             vmem_limit_bytes=VMEM_LIMIT),
        name=name)(*[x for tens in tensors for x in tens], *(comm.ins if comm else []))
    return [res[4 * i:4 * i + 4] for i in range(k)], res[4 * k:]


def _step(a):
    def sq(n):
        v = a[n][0] if a[n].ndim == 3 else a[n]
        return v.T if n.removeprefix("m_").removeprefix("v_") in TRANSPOSED else v

    payload = lambda n: sq(n) if n in EXACT_GATHER else sq(n).astype(BF16)

    gathered = _run_comm(_gather_plan([payload(n) for n in EARLY]), "gather_early")
    w = _early_layout(dict(zip(EARLY, gathered, strict=True)), {n: a[n] for n in REPLICATED})

    loss, dx, g, lands_late = _local_step(sq("x"), sq("mem"), a["positions"][0], sq("loss_target"), w,
                                          [payload(n) for n in LATE])

    sh, rep = _early_grad_shards(g)
    small = _pack_small([rep[n] for n in REPLICATED] + [loss.reshape(1, 1)])
    *lands_early, land_small = _run_comm(_scatter_plan([sh[n] for n in EARLY], small), "scatter_last")
    quad = lambda n, land: (sq(n), sq("m_" + n), sq("v_" + n), land)
    lands = dict(zip(EARLY, lands_early, strict=True)) | lands_late

    outs = {}
    kinds = ("grad_", "delta_", "new_m_", "new_v_")
    for n, _ in SHARDED:
        res = _adamw([quad(n, lands[n])], "adamw_" + n)[0][0]
        for kind, val in zip(kinds, res, strict=True):
            outs[kind + n] = (val.T if n in TRANSPOSED else val).reshape(a[n].shape)
    zero = jnp.zeros((1, 1), F32)
    packed = [_pack_small([a[p + n] for n in REPLICATED] + [zero]) for p in ("", "m_", "v_")]
    res = _adamw([(*packed, land_small)], "adamw_replicated")[0][0]
    widths = [a[n].shape[1] for n in REPLICATED] + [1]
    for kind, buf in zip(kinds, res, strict=True):
        *vals, total = _unpack_small(buf, widths)
        for n, val in zip(REPLICATED, vals, strict=True):
            outs[kind + n] = val
        if kind == "grad_":
            loss = total[0, 0]

    ordered = [outs[kind + n] for kind in kinds for n in WEIGHTS]
    return (loss, dx[None], *ordered)


def kernel(x, mem, positions, norm_mix, w_in, gla_gate_w2, gla_gate_b, gla_out_norm, mla_q_a_norm, mla_w_uq, mla_kv_a_norm, mla_w_ukv, mla_q_norm, mla_k_norm, w_out, norm_xa, norm_mem, xa_w_q, xa_w_kv, xa_q_norm, xa_k_norm, xa_w_o, norm_ffn, ffn_w_gate, ffn_w_up, ffn_conv_w, ffn_conv_b, ffn_w_down, loss_target, m_norm_mix, m_w_in, m_gla_gate_w2, m_gla_gate_b, m_gla_out_norm, m_mla_q_a_norm, m_mla_w_uq, m_mla_kv_a_norm, m_mla_w_ukv, m_mla_q_norm, m_mla_k_norm, m_w_out, m_norm_xa, m_norm_mem, m_xa_w_q, m_xa_w_kv, m_xa_q_norm, m_xa_k_norm, m_xa_w_o, m_norm_ffn, m_ffn_w_gate, m_ffn_w_up, m_ffn_conv_w, m_ffn_conv_b, m_ffn_w_down, v_norm_mix, v_w_in, v_gla_gate_w2, v_gla_gate_b, v_gla_out_norm, v_mla_q_a_norm, v_mla_w_uq, v_mla_kv_a_norm, v_mla_w_ukv, v_mla_q_norm, v_mla_k_norm, v_w_out, v_norm_xa, v_norm_mem, v_xa_w_q, v_xa_w_kv, v_xa_q_norm, v_xa_k_norm, v_xa_w_o, v_norm_ffn, v_ffn_w_gate, v_ffn_w_up, v_ffn_conv_w, v_ffn_conv_b, v_ffn_w_down):
    return _step(dict(locals()))
```

```python
import functools

import jax
import jax.numpy as jnp
from jax import lax
from jax.experimental import pallas as pl
from jax.experimental.pallas import tpu as pltpu

F32, BF16 = jnp.float32, jnp.bfloat16
MESH = pl.DeviceIdType.MESH

D_MODEL = 1024
EPS = 1e-6
GLA_HEADS, GLA_DK, GLA_DV, GLA_RANK, GLA_CHUNK = 4, 64, 128, 16, 64
GLA_GATE_NORM = 16.0
MLA_HEADS, MLA_Q_RANK, MLA_KV_RANK, MLA_NOPE, MLA_ROPE, MLA_V = 8, 256, 128, 64, 32, 64
MLA_QK = MLA_NOPE + MLA_ROPE
ROPE_THETA = 10000.0
LOG2E, LN2 = 1.4426950408889634, 0.6931471805599453
XA_HEADS, XA_DIM = 4, 128
D_FF = 2816
ADAM_LR, ADAM_B1, ADAM_B2, ADAM_EPS, ADAM_WD, ADAM_STEP = 0.001, 0.9, 0.999, 1e-08, 0.01, 10

LANES = 128
BF16_ROWS = 16
VMEM_LIMIT = 56 * 1024 * 1024
MATMUL_VMEM = 44 * 1024 * 1024

P_GQ, P_GK, P_GV, P_OG, P_CQ, P_CKV, P_KPE, P_ALR, P_WIDTH = 0, 256, 512, 1024, 1536, 1792, 1920, 2048, 2176
N_GQ, N_GK, N_GV, N_ALR, N_OG, N_CQ, N_CKV, N_KPE, N_WIDTH = 0, 256, 512, 1024, 1040, 1552, 1808, 1936, 1968

SHARDED = (("w_in", 1), ("gla_gate_w2", 1), ("mla_w_uq", 1), ("mla_w_ukv", 1), ("w_out", 0), ("xa_w_q", 0),
           ("xa_w_kv", 0), ("xa_w_o", 1), ("ffn_w_gate", 1), ("ffn_w_up", 1), ("ffn_conv_w", 1), ("ffn_w_down", 0))
REPLICATED = ("norm_mix", "gla_gate_b", "gla_out_norm", "mla_q_a_norm", "mla_kv_a_norm", "mla_q_norm", "mla_k_norm",
              "norm_xa", "norm_mem", "xa_q_norm", "xa_k_norm", "norm_ffn", "ffn_conv_b")
EXACT_GATHER = ("gla_gate_w2", "ffn_conv_w")
TRANSPOSED = ("ffn_w_gate", "ffn_w_up")
EARLY = ("w_in", "gla_gate_w2", "mla_w_uq", "mla_w_ukv")
LATE = tuple(n for n, _ in SHARDED if n not in EARLY)
LATE_MLP = tuple(n for n in LATE if n.startswith("ffn_"))
LATE_MIX = tuple(n for n in LATE if not n.startswith("ffn_"))
WEIGHTS = ("norm_mix", "w_in", "gla_gate_w2", "gla_gate_b", "gla_out_norm", "mla_q_a_norm", "mla_w_uq",
           "mla_kv_a_norm", "mla_w_ukv", "mla_q_norm", "mla_k_norm", "w_out", "norm_xa", "norm_mem", "xa_w_q",
           "xa_w_kv", "xa_q_norm", "xa_k_norm", "xa_w_o", "norm_ffn", "ffn_w_gate", "ffn_w_up", "ffn_conv_w",
           "ffn_conv_b", "ffn_w_down")


_NN = ((1,), (0,))
_NT = ((1,), (1,))
_TN = ((0,), (0,))


def _dg(a, b, dims):
    return lax.dot_general(a.astype(BF16), b.astype(BF16), (dims, ((), ())), preferred_element_type=F32)


@jax.custom_vjp
def _dot_nn(a, b):
    return _dg(a, b, _NN)


_dot_nn.defvjp(lambda a, b: (_dg(a, b, _NN), (a, b)),
               lambda r, g: (_dg(g, r[1], _NT).astype(r[0].dtype), _dg(r[0], g, _TN).astype(r[1].dtype)))


@jax.custom_vjp
def _dot_nt(a, b):
    return _dg(a, b, _NT)


_dot_nt.defvjp(lambda a, b: (_dg(a, b, _NT), (a, b)),
               lambda r, g: (_dg(g, r[1], _NN).astype(r[0].dtype), _dg(g, r[0], _TN).astype(r[1].dtype)))


@jax.custom_vjp
def _dot_tn(a, b):
    return _dg(a, b, _TN)


_dot_tn.defvjp(lambda a, b: (_dg(a, b, _TN), (a, b)),
               lambda r, g: (_dg(r[1], g, _NT).astype(r[0].dtype), _dg(r[0], g, _NN).astype(r[1].dtype)))


def _rms(x, w, n=None):
    n = x.shape[-1] if n is None else n
    ms = jnp.sum(x * x, axis=-1, keepdims=True) * (1.0 / n)
    return x * lax.rsqrt(ms + EPS) * w


def _silu(x):
    return x * jax.nn.sigmoid(x)


def _log_sigmoid(x):
    return jnp.minimum(x, 0.0) - jnp.log(1.0 + jnp.exp(-jnp.abs(x)))


@jax.custom_vjp
def _rope(y, c, sa, sb):
    return y * c + pltpu.roll(y, LANES - 16, 1) * sa + pltpu.roll(y, 16, 1) * sb


def _rope_bwd(res, g):
    c, sa, sb = res
    gy = g * c + pltpu.roll(g * sa, 16, 1) + pltpu.roll(g * sb, LANES - 16, 1)
    return gy, jnp.zeros_like(c), jnp.zeros_like(sa), jnp.zeros_like(sb)


_rope.defvjp(lambda y, c, sa, sb: (_rope(y, c, sa, sb), (c, sa, sb)), _rope_bwd)


def _lane_mask(lo, hi):
    lane = lax.broadcasted_iota(jnp.int32, (1, LANES), 1)
    return ((lane >= lo) & (lane < hi)).astype(F32)


def _tile(n, t):
    t = min(n, t)
    assert n % t == 0, (n, t)
    return t


class _Epilogue:
    def __init__(self, fn, rows=(), consts=(), outs=(), accs=()):
        self.fn, self.rows, self.consts, self.outs, self.accs = fn, list(rows), list(consts), list(outs), list(accs)


def _matmul(a, b, mode, out_dtype, name, residual=None, a_lead=None, b_lead=None, more=None, epilogue=None):
    (a0, a1), (b0, b1) = a.shape[-2:], b.shape[-2:]
    if mode == "nn":
        m, k, k2, n = a0, a1, b0, b1
    elif mode == "nt":
        m, k, n, k2 = a0, a1, b0, b1
    else:
        k, m, k2, n = a0, a1, b0, b1
    assert k == k2, (a.shape, b.shape, mode)
    npar = 4 if "p" in (a_lead, b_lead) else 1
    nsum = 4 if "k" in (a_lead, b_lead) else 1
    pairs = [(a, b)] + ([more] if more else [])
    a_item, b_item, o_item = a.dtype.itemsize, b.dtype.itemsize, jnp.dtype(out_dtype).itemsize
    ep = epilogue
    row_extra = (4 if residual is not None else 0) + (sum(r.dtype.itemsize for r in ep.rows) +
                                                       sum(jnp.dtype(d).itemsize for d in ep.outs) if ep else 0)

    def vmem_need(tm, tn, tk):
        need = 2 * (nsum if a_lead == "k" else 1) * tm * tk * a_item + 2 * (nsum if b_lead == "k" else 1) * tk * tn * b_item
        need *= len(pairs)
        need += (0 if ep else 2 * tm * tn * o_item) + tm * tn * 4 * (2 if tk < k else 1)
        need += tm * tk * 2 * (a_item == 4 or mode == "tn") + tk * tn * 2 * (b_item == 4)
        return need + 2 * tm * tn * row_extra + (3 * tm * tn * 4 if ep else 0)

    halvings = (4096, 2048, 1024, 512, 256, 128, 64, 32, 16, 8)
    if mode == "tn":
        tm = m if m <= 1408 else m // 2
        tn = n if tm * n <= 1024 * 2304 else n // 2
        tk = next((r for r in halvings if k % r == 0 and vmem_need(tm, tn, r) <= MATMUL_VMEM), k)
    else:
        tn, tk = n, k
        tm = next((r for r in halvings if m % r == 0 and vmem_need(r, tn, tk) <= MATMUL_VMEM), m)
    assert m % tm == 0 and n % tn == 0 and k % tk == 0
    assert ep is None or (tn == n and tk == k and npar == 1)
    nk = k // tk
    dims = {"nn": _NN, "nt": _NT, "tn": _TN}[mode]
    n_in = 2 * len(pairs) + (residual is not None)
    n_ep_in = len(ep.rows) + len(ep.consts) if ep else 0
    n_out = len(ep.outs) + len(ep.accs) if ep else 1

    def body(*refs):
        ab, rs, ep_in, outs, scratch = _split_refs(refs, (2 * len(pairs), n_in - 2 * len(pairs), n_ep_in, n_out, nk > 1))
        prod = None
        for a_ref, b_ref in zip(ab[0::2], ab[1::2]):
            for sh in range(nsum):
                term = _dg(a_ref[sh] if a_lead == "k" else a_ref[...], b_ref[sh] if b_lead == "k" else b_ref[...], dims)
                prod = term if prod is None else prod + term

        def finish(r):
            if rs:
                r = r + rs[0][...]
            if ep is None:
                outs[0][...] = r.astype(outs[0].dtype)
                return
            vals = [x[...] for x in ep_in]
            ro, ao = ep.fn(r, vals[:len(ep.rows)], vals[len(ep.rows):])
            for ref, val in zip(outs[:len(ep.outs)], ro, strict=True):
                ref[...] = val.astype(ref.dtype)
            if ep.accs:
                @pl.when(pl.program_id(0) == 0)
                def _():
                    for ref in outs[len(ep.outs):]:
                        ref[...] = jnp.zeros_like(ref)

                for ref, val in zip(outs[len(ep.outs):], ao, strict=True):
                    ref[...] += val

        if nk == 1:
            finish(prod)
            return
        acc = scratch[0]
        kk = pl.program_id(3)

        @pl.when(kk == 0)
        def _():
            acc[...] = prod

        @pl.when(kk > 0)
        def _():
            acc[...] += prod

        @pl.when(kk == nk - 1)
        def _():
            finish(acc[...])

    def spec(lead, blk, idx):
        if lead is None:
            return pl.BlockSpec(blk, lambda i, j, p, kk: idx(i, j, kk))
        if lead == "p":
            return pl.BlockSpec((None,) + blk, lambda i, j, p, kk: (p,) + idx(i, j, kk))
        return pl.BlockSpec((nsum,) + blk, lambda i, j, p, kk: (0,) + idx(i, j, kk))

    if mode == "nn":
        pair_specs = [spec(a_lead, (tm, tk), lambda i, j, kk: (i, kk)), spec(b_lead, (tk, tn), lambda i, j, kk: (kk, j))]
    elif mode == "nt":
        pair_specs = [spec(a_lead, (tm, tk), lambda i, j, kk: (i, kk)), spec(b_lead, (tn, tk), lambda i, j, kk: (j, kk))]
    else:
        pair_specs = [spec(a_lead, (tk, tm), lambda i, j, kk: (kk, i)), spec(b_lead, (tk, tn), lambda i, j, kk: (kk, j))]
    tile = spec(None, (tm, tn), lambda i, j, kk: (i, j))
    in_specs = pair_specs * len(pairs)
    args = [x for pair in pairs for x in pair]
    if residual is not None:
        assert npar == 1
        in_specs.append(tile)
        args.append(residual)
    if ep:
        in_specs += [tile] * len(ep.rows) + [pl.BlockSpec(c.shape, lambda i, j, p, kk: (0, 0)) for c in ep.consts]
        args += ep.rows + ep.consts
        out_specs = [tile] * len(ep.outs) + [pl.BlockSpec(shape, lambda i, j, p, kk: (0, 0)) for shape in ep.accs]
        out_shape = [jax.ShapeDtypeStruct((m, n), d) for d in ep.outs] + [jax.ShapeDtypeStruct(sh, F32) for sh in ep.accs]
    else:
        out_specs = spec("p" if npar > 1 else None, (tm, tn), lambda i, j, kk: (i, j))
        out_shape = jax.ShapeDtypeStruct(((4,) if npar > 1 else ()) + (m, n), out_dtype)
    outer = "arbitrary" if ep and ep.accs else "parallel"
    return pl.pallas_call(
        body, grid=(m // tm, n // tn, npar, nk), in_specs=in_specs, out_specs=out_specs, out_shape=out_shape,
        scratch_shapes=[pltpu.VMEM((tm, tn), F32)] if nk > 1 else [],
        compiler_params=pltpu.CompilerParams(dimension_semantics=(outer, outer, outer, "arbitrary"),
                                             vmem_limit_bytes=VMEM_LIMIT),
        name=name)(*args)


def _row(a, width=None, col_block=0):
    return (a, a.shape[1] if width is None else width, col_block)


def _rows_call(body, rows, consts, outs, accs=(), *, name, tile=512):
    s = rows[0][0].shape[0]
    t = _tile(s, tile)
    nr, nc, no = len(rows), len(consts), len(outs)

    def kern(*refs):
        r = [x[...] for x in refs[:nr]]
        c = [x[...] for x in refs[nr:nr + nc]]
        o_refs = refs[nr + nc:nr + nc + no]
        a_refs = refs[nr + nc + no:]
        ro, ao = body(r, c)
        for ref, val in zip(o_refs, ro, strict=True):
            ref[...] = val.astype(ref.dtype)
        if a_refs:
            @pl.when(pl.program_id(0) == 0)
            def _():
                for ref in a_refs:
                    ref[...] = jnp.zeros_like(ref)

            for ref, val in zip(a_refs, ao, strict=True):
                ref[...] += val

    in_specs = [pl.BlockSpec((t, w), functools.partial(lambda cb, i: (i, cb), cb)) for (_, w, cb) in rows]
    in_specs += [pl.BlockSpec(c.shape, lambda i: (0, 0)) for c in consts]
    out_specs = [pl.BlockSpec((t, w), lambda i: (i, 0)) for (w, _) in outs]
    out_specs += [pl.BlockSpec(shape, lambda i: (0, 0)) for shape in accs]
    out_shape = [jax.ShapeDtypeStruct((s, w), dt) for (w, dt) in outs]
    out_shape += [jax.ShapeDtypeStruct(shape, F32) for shape in accs]
    return pl.pallas_call(
        kern, grid=(s // t,), in_specs=in_specs, out_specs=out_specs, out_shape=out_shape,
        compiler_params=pltpu.CompilerParams(dimension_semantics=("arbitrary" if accs else "parallel",),
                                             vmem_limit_bytes=VMEM_LIMIT),
        name=name)(*[r[0] for r in rows], *consts)


def _gla_chunk(q, k, la, v0, v1, s0, s1):
    c = q.shape[0]
    r = lax.broadcasted_iota(jnp.int32, (c, c), 0)
    cc = lax.broadcasted_iota(jnp.int32, (c, c), 1)
    tril = cc <= r
    cum = lax.dot_general(tril.astype(F32), la, (_NN, ((), ())), precision=lax.Precision.HIGHEST,
                          preferred_element_type=F32)
    cl = jnp.sum(la, axis=0, keepdims=True)
    qd = q * (GLA_DK ** -0.5) * jnp.exp(cum)
    ki = k * jnp.exp(-cum)
    ke = k * jnp.exp(cl - cum)
    dec = jnp.exp(cl)
    outs, news = [], []
    for h, (v, s) in enumerate(((v0, s0), (v1, s1))):
        mk = _lane_mask(GLA_DK * h, GLA_DK * (h + 1))
        qh = qd * mk
        att = jnp.where(tril, _dot_nt(qh, ki), 0.0)
        outs.append(_dot_nn(att, v) + _dot_nt(qh, s))
        news.append(s * dec + _dot_tn(v, ke * mk))
    return outs[0], outs[1], news[0], news[1]


def _gla_specs(tb, rev_nb=None):
    blk = (lambda b: b) if rev_nb is None else (lambda b: rev_nb - 1 - b)
    q = pl.BlockSpec((tb, 128), lambda p, b: (blk(b), P_GQ // 128 + p))
    k = pl.BlockSpec((tb, 128), lambda p, b: (blk(b), P_GK // 128 + p))
    la = pl.BlockSpec((tb, 128), lambda p, b: (blk(b), p))
    v = pl.BlockSpec((tb, 256), lambda p, b: (blk(b), P_GV // 256 + p))
    o = pl.BlockSpec((tb, 256), lambda p, b: (blk(b), p))
    st = pl.BlockSpec((tb // GLA_CHUNK, 2, 128, 128), lambda p, b: (blk(b), p, 0, 0))
    return q, k, la, v, o, st


def _gla_fwd(proj, la):
    s = proj.shape[0]
    tb = _tile(s, 512)
    nb, nch = s // tb, tb // GLA_CHUNK

    def kern(q_ref, k_ref, la_ref, v_ref, o_ref, st_ref, s_sc):
        @pl.when(pl.program_id(1) == 0)
        def _():
            s_sc[...] = jnp.zeros_like(s_sc)

        s0, s1 = s_sc[0], s_sc[1]
        for ci in range(nch):
            sl = slice(ci * GLA_CHUNK, (ci + 1) * GLA_CHUNK)
            st_ref[ci, 0] = s0
            st_ref[ci, 1] = s1
            o0, o1, s0, s1 = _gla_chunk(q_ref[sl, :], k_ref[sl, :], la_ref[sl, :], v_ref[sl, 0:128],
                                        v_ref[sl, 128:256], s0, s1)
            o_ref[sl, 0:128] = o0
            o_ref[sl, 128:256] = o1
        s_sc[0] = s0
        s_sc[1] = s1

    q, k, lasp, v, o, st = _gla_specs(tb)
    return pl.pallas_call(
        kern, grid=(2, nb), in_specs=[q, k, lasp, v], out_specs=[o, st],
        out_shape=[jax.ShapeDtypeStruct((s, 512), F32),
                   jax.ShapeDtypeStruct((s // GLA_CHUNK, GLA_HEADS, 128, 128), F32)],
        scratch_shapes=[pltpu.VMEM((2, 128, 128), F32)],
        compiler_params=pltpu.CompilerParams(dimension_semantics=("parallel", "arbitrary"),
                                             vmem_limit_bytes=VMEM_LIMIT),
        name="gla_fwd")(proj, proj, la, proj)


def _gla_bwd(proj, la, states, d_o, comm):
    s = proj.shape[0]
    tb = _tile(s, 512)
    nb, nch = s // tb, tb // GLA_CHUNK
    nci, nco = len(comm.ins), len(comm.out_shape)

    def kern(*refs):
        (q_ref, k_ref, la_ref, v_ref, do_ref, st_ref), cins, (dq_ref, dk_ref, dla_ref, dv_ref), couts, (ds_sc,), csems = \
            _split_refs(refs, (6, nci, 4, nco, 1, len(comm.sems)))
        place = _place()
        pair, blk = pl.program_id(0), pl.program_id(1)

        @pl.when((pair == 0) & (blk == 0))
        def _():
            comm.start(place, cins, couts, csems)

        @pl.when((pair == 1) & (blk == nb // 2))
        def _():
            comm.mid(place, cins, couts, csems)

        @pl.when(blk == 0)
        def _():
            ds_sc[...] = jnp.zeros_like(ds_sc)

        d0, d1 = ds_sc[0], ds_sc[1]
        for ci in reversed(range(nch)):
            sl = slice(ci * GLA_CHUNK, (ci + 1) * GLA_CHUNK)
            _, vjp = jax.vjp(_gla_chunk, q_ref[sl, :], k_ref[sl, :], la_ref[sl, :], v_ref[sl, 0:128],
                             v_ref[sl, 128:256], st_ref[ci, 0], st_ref[ci, 1])
            gq, gk, gla, gv0, gv1, d0, d1 = vjp((do_ref[sl, 0:128], do_ref[sl, 128:256], d0, d1))
            dq_ref[sl, :] = gq
            dk_ref[sl, :] = gk
            dla_ref[sl, :] = gla
            dv_ref[sl, 0:128] = gv0
            dv_ref[sl, 128:256] = gv1
        ds_sc[0] = d0
        ds_sc[1] = d1

        @pl.when((pair == 1) & (blk == nb - 1))
        def _():
            comm.finish(place, cins, couts, csems)

    q, k, lasp, v, o, st = _gla_specs(tb, rev_nb=nb)
    res = pl.pallas_call(
        kern, grid=(2, nb), in_specs=[q, k, lasp, v, o, st] + [ANY] * nci, out_specs=[lasp, lasp, lasp, o] + [ANY] * nco,
        out_shape=[jax.ShapeDtypeStruct((s, 256), F32), jax.ShapeDtypeStruct((s, 256), F32),
                   jax.ShapeDtypeStruct((s, 256), F32), jax.ShapeDtypeStruct((s, 512), F32)] + comm.out_shape,
        scratch_shapes=[pltpu.VMEM((2, 128, 128), F32)] + comm.sems,
        compiler_params=pltpu.CompilerParams(dimension_semantics=("arbitrary", "arbitrary"),
                                             vmem_limit_bytes=VMEM_LIMIT),
        name="gla_bwd")(proj, proj, la, proj, d_o, states, *comm.ins)
    return res[0], res[1], res[2], res[3], res[4:]


def _causal_keep(t, qi, ki):
    row = lax.broadcasted_iota(jnp.int32, (t, t), 0) + qi * t
    col = lax.broadcasted_iota(jnp.int32, (t, t), 1) + ki * t
    return col <= row


def _split_refs(refs, counts):
    out, off = [], 0
    for cnt in counts:
        out.append(refs[off:off + cnt])
        off += cnt
    return out


def _attn_fwd(q, k, v, comm, tile=1024):
    s = q.shape[0]
    t = _tile(s, tile)
    n = s // t
    nci, nco = len(comm.ins), len(comm.out_shape)

    def kern(*refs):
        (q_ref, k_ref, v_ref), cins, (o_ref, lse_ref), couts, (m_sc, l_sc, acc_sc), csems = _split_refs(
            refs, (3, nci, 2, nco, 3, len(comm.sems)))
        qi, ki = pl.program_id(1), pl.program_id(2)
        place = _place()

        @pl.when((pl.program_id(0) == 0) & (qi == 0) & (ki == 0))
        def _():
            comm.start(place, cins, couts, csems)

        @pl.when((pl.program_id(0) == MLA_HEADS // 2 - 1) & (qi == 0) & (ki == 0))
        def _():
            comm.mid(place, cins, couts, csems)

        first = lax.broadcasted_iota(jnp.int32, (t, LANES), 1) < MLA_V

        @pl.when(ki == 0)
        def _():
            m_sc[...] = jnp.full_like(m_sc, -jnp.inf)
            l_sc[...] = jnp.zeros_like(l_sc)
            acc_sc[...] = jnp.zeros_like(acc_sc)

        def update(diagonal):
            keep = _causal_keep(t, 0, 0)
            alphas, pvs = [], []
            for h in range(2):
                sc = _dg(q_ref[:, 128 * h:128 * (h + 1)], k_ref[:, 128 * h:128 * (h + 1)], _NT)
                if diagonal:
                    sc = jnp.where(keep, sc, -jnp.inf)
                m_prev = m_sc[h]
                m_new = jnp.maximum(m_prev, jnp.max(sc, axis=1, keepdims=True))
                alpha = jnp.exp2(m_prev - m_new)
                p = jnp.exp2(sc - m_new[:, 0:1])
                l_sc[h] = alpha * l_sc[h] + jnp.sum(p, axis=1, keepdims=True)
                m_sc[h] = m_new
                alphas.append(alpha)
                pvs.append(_dg(p, v_ref[...], _NN))
            acc_sc[...] = acc_sc[...] * jnp.where(first, alphas[0], alphas[1]) + jnp.where(first, pvs[0], pvs[1])

        @pl.when(ki < qi)
        def _():
            update(False)

        @pl.when(ki == qi)
        def _():
            update(True)

        @pl.when(ki == qi)
        def _():
            l = jnp.where(first, l_sc[0], l_sc[1])
            m = jnp.where(first, m_sc[0], m_sc[1])
            o_ref[...] = acc_sc[...] / l
            lse_ref[...] = m + jnp.log2(l)

        @pl.when((pl.program_id(0) == MLA_HEADS // 2 - 1) & (qi == n - 1) & (ki == n - 1))
        def _():
            comm.finish(place, cins, couts, csems)

    kv_idx = lambda p, qi, ki: (jnp.minimum(ki, qi), p)
    res = pl.pallas_call(
        kern, grid=(MLA_HEADS // 2, n, n),
        in_specs=[pl.BlockSpec((t, 256), lambda p, qi, ki: (qi, p)), pl.BlockSpec((t, 256), kv_idx),
                  pl.BlockSpec((t, 128), kv_idx)] + [ANY] * nci,
        out_specs=[pl.BlockSpec((t, 128), lambda p, qi, ki: (qi, p)), pl.BlockSpec((t, 128), lambda p, qi, ki: (qi, p))]
        + [ANY] * nco,
        out_shape=[jax.ShapeDtypeStruct((s, 512), F32), jax.ShapeDtypeStruct((s, 512), F32)] + comm.out_shape,
        scratch_shapes=[pltpu.VMEM((2, t, LANES), F32), pltpu.VMEM((2, t, LANES), F32), pltpu.VMEM((t, LANES), F32)]
        + comm.sems,
        compiler_params=pltpu.CompilerParams(dimension_semantics=("arbitrary", "arbitrary", "arbitrary"),
                                             vmem_limit_bytes=VMEM_LIMIT),
        name="mla_attn_fwd")(q, k, v, *comm.ins)
    return res[0], res[1], res[2:]


def _attn_bwd(q, k, v, o, lse, dcat, comm, tile=512):
    s = q.shape[0]
    t = _tile(s, tile)
    n = s // t
    nci, nco = len(comm.ins), len(comm.out_shape)

    def kern(*refs):
        (q_ref, k_ref, v_ref, o_ref, lse_ref, do_ref), cins, (dq_ref, dk_ref, dv_ref), couts, (dk_sc, dv_sc), csems = \
            _split_refs(refs, (6, nci, 3, nco, 2, len(comm.sems)))
        ki, qi = pl.program_id(1), pl.program_id(2)
        place = _place()

        @pl.when((pl.program_id(0) == 0) & (qi == 0) & (ki == 0))
        def _():
            comm.start(place, cins, couts, csems)

        @pl.when((pl.program_id(0) == MLA_HEADS // 2 - 1) & (qi == 0) & (ki == 0))
        def _():
            comm.mid(place, cins, couts, csems)

        @pl.when((ki == 0) & (qi == 0))
        def _():
            dq_ref[...] = jnp.zeros_like(dq_ref)

        @pl.when(qi == ki)
        def _():
            dk_sc[...] = jnp.zeros_like(dk_sc)
            dv_sc[...] = jnp.zeros_like(dv_sc)

        def update(diagonal):
            keep = _causal_keep(t, 0, 0)
            d_o = do_ref[...]
            prod = d_o * o_ref[...]
            rows = pl.ds(pl.multiple_of(qi * t, t), t)
            for h in range(2):
                hs = slice(128 * h, 128 * (h + 1))
                mk = _lane_mask(MLA_V * h, MLA_V * (h + 1))
                qh, kh = q_ref[:, hs], k_ref[:, hs]
                sc = _dg(qh, kh, _NT)
                if diagonal:
                    sc = jnp.where(keep, sc, -jnp.inf)
                p = jnp.exp2(sc - lse_ref[:, MLA_V * h:MLA_V * h + 1])
                doh = d_o * mk
                dp = _dg(doh * LN2, v_ref[...], _NT)
                delta = jnp.sum(prod * mk, axis=1, keepdims=True) * LN2
                ds = p * (dp - delta)
                dv_sc[...] += _dg(p, doh, _TN)
                dk_sc[:, hs] += _dg(ds, qh, _TN)
                dq_ref[rows, hs] += _dg(ds, kh, _NN)

        @pl.when(qi > ki)
        def _():
            update(False)

        @pl.when(qi == ki)
        def _():
            update(True)

        @pl.when(qi == n - 1)
        def _():
            dk_ref[...] = dk_sc[...]
            dv_ref[...] = dv_sc[...].astype(dv_ref.dtype)

        @pl.when((pl.program_id(0) == MLA_HEADS // 2 - 1) & (qi == n - 1) & (ki == n - 1))
        def _():
            comm.finish(place, cins, couts, csems)

    q_idx = lambda p, ki, qi: (jnp.maximum(qi, ki), p)
    res = pl.pallas_call(
        kern, grid=(MLA_HEADS // 2, n, n),
        in_specs=[pl.BlockSpec((t, 256), q_idx), pl.BlockSpec((t, 256), lambda p, ki, qi: (ki, p)),
                  pl.BlockSpec((t, 128), lambda p, ki, qi: (ki, p)), pl.BlockSpec((t, 128), q_idx),
                  pl.BlockSpec((t, 128), q_idx),
                  pl.BlockSpec((t, 128), lambda p, ki, qi: (jnp.maximum(qi, ki), 4 + p))] + [ANY] * nci,
        out_specs=[pl.BlockSpec((s, 256), lambda p, ki, qi: (0, p)), pl.BlockSpec((t, 256), lambda p, ki, qi: (ki, p)),
                   pl.BlockSpec((t, 128), lambda p, ki, qi: (ki, p))] + [ANY] * nco,
        out_shape=[jax.ShapeDtypeStruct((s, 1024), F32), jax.ShapeDtypeStruct((s, 1024), F32),
                   jax.ShapeDtypeStruct((s, 512), BF16)] + comm.out_shape,
        scratch_shapes=[pltpu.VMEM((t, 256), F32), pltpu.VMEM((t, 128), F32)] + comm.sems,
        compiler_params=pltpu.CompilerParams(dimension_semantics=("arbitrary", "arbitrary", "arbitrary"),
                                             vmem_limit_bytes=VMEM_LIMIT),
        name="mla_attn_bwd")(q, k, v, o, lse, dcat, *comm.ins)
    return res[0], res[1], res[2], res[3:]


def _gate_fn(alr, w2, b):
    return _log_sigmoid(_dot_nn(alr, w2) + b) * (1.0 / GLA_GATE_NORM)


def _qk_head(qh, kh, kpe, c, sa, sb, qn, kn):
    kfull = kh + kpe * _lane_mask(MLA_NOPE, MLA_QK)
    q_r = _rope(_rms(qh, qn, MLA_QK), c, sa, sb) * (MLA_QK ** -0.5 * LOG2E)
    k_r = _rope(_rms(kfull, kn, MLA_QK), c, sa, sb)
    return q_r, k_r


def _mix_head(o, og, gn):
    return _rms(o, gn) * _silu(og)


def _xa_head(xq, xk, xv, qn, kn):
    sc = _dot_nt(_rms(xq, qn), _rms(xk, kn)) * (XA_DIM ** -0.5)
    e = jnp.exp(sc - lax.stop_gradient(jnp.max(sc, axis=1, keepdims=True)))
    p = e / jnp.sum(e, axis=1, keepdims=True)
    return _dot_nn(p, xv)


def _heads(x, n):
    return [x[:, 128 * h:128 * (h + 1)] for h in range(n)]


def _cat(xs):
    return jnp.concatenate(xs, axis=1)


def _norm_fwd(x, w, name):
    return _rows_call(lambda r, c: ([_rms(r[0], c[0])], []), [_row(x)], [w], [(x.shape[1], BF16)], name=name)[0]


def _norm_fwd_epilogue(w):
    return _Epilogue(lambda h, rows, consts: ([h, _rms(h, consts[0])], []), [], [w], [F32, BF16], [])


def _norm_bwd_epilogue(x, w, add):
    def fn(d_out, rows, consts):
        _, vjp = jax.vjp(_rms, rows[0], consts[0])
        dx, dw = vjp(d_out)
        return [dx + rows[1]], [dw]

    return _Epilogue(fn, [x, add], [w], [F32], [w.shape])


def _norm_bwd(x, w, d_out, add, name):
    def body(r, c):
        _, vjp = jax.vjp(_rms, r[0], c[0])
        dx, dw = vjp(r[1])
        return [dx + r[2]], [dw]

    return _rows_call(body, [_row(x), _row(d_out), _row(add)], [w], [(x.shape[1], F32)], [w.shape], name=name)


CONV_HALO = BF16_ROWS


def _conv_specs(s, f, t):
    n8 = t // CONV_HALO
    cur = pl.BlockSpec((None, t, f), lambda j, i: (j, i, 0))
    prev = pl.BlockSpec((None, CONV_HALO, f), lambda j, i: (j, jnp.maximum(i * n8 - 1, 0), 0))
    nxt = pl.BlockSpec((None, CONV_HALO, f), lambda j, i: (j, jnp.minimum((i + 1) * n8, s // CONV_HALO - 1), 0))
    cw = pl.BlockSpec((None, 3, f), lambda j, i: (j, 0, 0))
    cb = pl.BlockSpec((None, 1, f), lambda j, i: (j, 0, 0))
    return cur, prev, nxt, cw, cb


def _conv_taps(g, prev, first):
    ext = jnp.concatenate([jnp.where(first, 0.0, prev.astype(F32)), g], axis=0)
    return pltpu.roll(ext, 1, 0)[CONV_HALO:], pltpu.roll(ext, 2, 0)[CONV_HALO:]


def _conv_fwd(gg, uu, cw, cb):
    _, s, f = gg.shape
    t = _tile(s, 512)

    def kern(g_ref, gp_ref, u_ref, cw_ref, cb_ref, o_ref):
        g = g_ref[...].astype(F32)
        g1, g2 = _conv_taps(g, gp_ref[...], pl.program_id(1) == 0)
        w = cw_ref[...]
        gc = cb_ref[...] + w[0:1] * g2 + w[1:2] * g1 + w[2:3] * g
        o_ref[...] = (_silu(gc) * u_ref[...].astype(F32)).astype(o_ref.dtype)

    cur, prev, _, cws, cbs = _conv_specs(s, f, t)
    return pl.pallas_call(
        kern, grid=(4, s // t), in_specs=[cur, prev, cur, cws, cbs], out_specs=cur,
        out_shape=jax.ShapeDtypeStruct(gg.shape, BF16),
        compiler_params=pltpu.CompilerParams(dimension_semantics=("parallel", "parallel"), vmem_limit_bytes=VMEM_LIMIT),
        name="ffn_conv_fwd")(gg, gg, uu, cw, cb)


def _conv_bwd(gg, uu, dact, cw, cb):
    _, s, f = gg.shape
    t = _tile(s, 512)
    nt = s // t

    def kern(g_ref, gp_ref, gn_ref, u_ref, un_ref, da_ref, dan_ref, cw_ref, cb_ref, du_ref, dg_ref, dcw_ref, dcb_ref):
        i = pl.program_id(1)
        cat = lambda a_ref, b_ref: jnp.concatenate([a_ref[...].astype(F32), b_ref[...].astype(F32)], axis=0)
        g, u, da = cat(g_ref, gn_ref), cat(u_ref, un_ref), cat(da_ref, dan_ref)
        g1, g2 = _conv_taps(g, gp_ref[...], i == 0)
        w = cw_ref[...]
        gc = cb_ref[...] + w[0:1] * g2 + w[1:2] * g1 + w[2:3] * g
        sg = jax.nn.sigmoid(gc)
        du_ref[...] = (da[:t] * (gc[:t] * sg[:t])).astype(du_ref.dtype)
        row = lax.broadcasted_iota(jnp.int32, (t + CONV_HALO, 1), 0)
        dgc = jnp.where((row < t) | (i < nt - 1), da * u * (sg * (1.0 + gc * (1.0 - sg))), 0.0)
        up1 = pltpu.roll(dgc, t + CONV_HALO - 1, 0)[:t]
        up2 = pltpu.roll(dgc, t + CONV_HALO - 2, 0)[:t]
        dgc = dgc[:t]
        dg_ref[...] = (w[2:3] * dgc + w[1:2] * up1 + w[0:1] * up2).astype(dg_ref.dtype)

        @pl.when(i == 0)
        def _():
            dcw_ref[...] = jnp.zeros_like(dcw_ref)
            dcb_ref[...] = jnp.zeros_like(dcb_ref)

        dcw_ref[0:1, :] += jnp.sum(dgc * g2[:t], axis=0, keepdims=True)
        dcw_ref[1:2, :] += jnp.sum(dgc * g1[:t], axis=0, keepdims=True)
        dcw_ref[2:3, :] += jnp.sum(dgc * g[:t], axis=0, keepdims=True)
        dcb_ref[...] += jnp.sum(dgc, axis=0, keepdims=True)

    cur, prev, nxt, cws, cbs = _conv_specs(s, f, t)
    return pl.pallas_call(
        kern, grid=(4, nt), in_specs=[cur, prev, nxt, cur, nxt, cur, nxt, cws, cbs], out_specs=[cur, cur, cws, cbs],
        out_shape=[jax.ShapeDtypeStruct(gg.shape, BF16), jax.ShapeDtypeStruct(gg.shape, BF16),
                   jax.ShapeDtypeStruct(cw.shape, F32), jax.ShapeDtypeStruct(cb.shape, F32)],
        compiler_params=pltpu.CompilerParams(dimension_semantics=("parallel", "arbitrary"), vmem_limit_bytes=VMEM_LIMIT),
        name="ffn_conv_bwd")(gg, gg, gg, uu, uu, dact, dact, cw, cb)


def _rope_tables(pos):
    half = MLA_ROPE // 2
    inv = ROPE_THETA ** (-jnp.arange(half, dtype=F32) / half)
    ang = pos.astype(F32)[:, None] * inv
    cos, sin = jnp.cos(ang), jnp.sin(ang)
    s = pos.shape[0]
    z = lambda w: jnp.zeros((s, w), F32)
    c = jnp.concatenate([jnp.ones((s, MLA_NOPE), F32), cos, cos, jnp.ones((s, LANES - MLA_QK), F32)], axis=1)
    sa = jnp.concatenate([z(MLA_NOPE), -sin, z(half), z(LANES - MLA_QK)], axis=1)
    sb = jnp.concatenate([z(MLA_NOPE), z(half), sin, z(LANES - MLA_QK)], axis=1)
    return c, sa, sb


def _local_step(x, mem, pos, target, w, late_shards):
    g = {}
    w = dict(w)
    c, sa, sb = _rope_tables(pos)

    xn = _norm_fwd(x, w["norm_mix"], "norm_mix_fwd")
    proj = _matmul(xn, w["in"], "nn", F32, "proj_fwd")
    alr = _row(proj, 128, P_ALR // 128)
    kpe = _row(proj, 128, P_KPE // 128)
    og = _row(proj, 512, P_OG // 512)
    cq = _row(proj, 256, P_CQ // 256)
    ckv = _row(proj, 128, P_CKV // 128)

    la = _rows_call(lambda r, k: ([_gate_fn(r[0], k[0], k[1])], []), [alr], [w["w2"], w["gate_b"]],
                    [(256, F32)], name="gla_gate_fwd")[0]
    o_gla, states = _gla_fwd(proj, la)

    q_lat, kv_lat = _rows_call(lambda r, k: ([_rms(r[0], k[0]), _rms(r[1], k[1])], []), [cq, ckv],
                               [w["q_a_norm"], w["kv_a_norm"]], [(256, BF16), (128, BF16)], name="mla_lat_fwd")
    q_up = _matmul(q_lat, w["uq"], "nn", F32, "mla_q_fwd")
    k_up = _matmul(kv_lat, w["k"], "nn", F32, "mla_k_fwd")
    v_mla = _matmul(kv_lat, w["v"], "nn", BF16, "mla_v_fwd")

    def qk_body(r, k):
        qs, ks = [], []
        for qh, kh in zip(_heads(r[0], MLA_HEADS), _heads(r[1], MLA_HEADS)):
            a, b = _qk_head(qh, kh, r[2], r[3], r[4], r[5], k[0], k[1])
            qs.append(a)
            ks.append(b)
        return [_cat(qs), _cat(ks)], []

    tabs = [_row(c), _row(sa), _row(sb)]
    q_r, k_r = _rows_call(qk_body, [_row(q_up), _row(k_up), kpe] + tabs, [w["q_norm"], w["k_norm"]],
                          [(1024, BF16), (1024, BF16)], name="mla_qk_fwd")
    o_mla, lse, gathered = _attn_fwd(q_r, k_r, v_mla, _gather_plan(late_shards))
    w.update(_late_layout(dict(zip(LATE, gathered, strict=True))))

    def mix_body(r, k):
        ys = [_mix_head(o, g_, k[0]) for o, g_ in zip(_heads(r[0], GLA_HEADS), _heads(r[1], GLA_HEADS))]
        return [_cat(ys + [r[2]])], []

    cat = _rows_call(mix_body, [_row(o_gla), og, _row(o_mla)], [w["gla_out_norm"]], [(1024, BF16)],
                     name="mix_fwd")[0]
    h1, hn = _matmul(cat, w["out"], "nn", F32, "out_fwd_norm", residual=x, epilogue=_norm_fwd_epilogue(w["norm_xa"]))
    mn = _norm_fwd(mem, w["norm_mem"], "norm_mem_fwd")
    xq = _matmul(hn, w["xq"], "nn", F32, "xa_q_fwd")
    xkv = _matmul(mn, w["xkv"], "nn", F32, "xa_kv_fwd")

    def xa_body(r, k):
        ks, vs = _heads(k[0], 2 * XA_HEADS)[:XA_HEADS], _heads(k[0], 2 * XA_HEADS)[XA_HEADS:]
        return [_cat([_xa_head(a, b, v_, k[1], k[2]) for a, b, v_ in zip(_heads(r[0], XA_HEADS), ks, vs)])], []

    xo = _rows_call(xa_body, [_row(xq)], [xkv, w["xa_q_norm"], w["xa_k_norm"]], [(512, BF16)], name="xa_fwd")[0]
    h2, fn = _matmul(xo, w["xo"], "nn", F32, "xa_o_fwd_norm", residual=h1, epilogue=_norm_fwd_epilogue(w["norm_ffn"]))
    gg = _matmul(fn, w["wg"], "nt", BF16, "ffn_gate_fwd", b_lead="p")
    uu = _matmul(fn, w["wu"], "nt", BF16, "ffn_up_fwd", b_lead="p")
    act = _conv_fwd(gg, uu, w["cw"], w["cb"])
    def loss_fn(y, rows, consts):
        err = y - rows[0]
        part = 0.5 * jnp.sum(jnp.sum(err * err, axis=1, keepdims=True) * (1.0 / D_MODEL), axis=0, keepdims=True)
        return [err * (1.0 / D_MODEL)], [jnp.broadcast_to(part, (1, LANES))]

    dy, loss = _matmul(act, w["wd"], "nn", F32, "ffn_down_fwd_loss", residual=h2, a_lead="k", b_lead="k",
                       epilogue=_Epilogue(loss_fn, [target], [], [F32], [(1, LANES)]))

    g["ffn_w_down"] = _matmul(act, dy, "tn", BF16, "ffn_down_dw", a_lead="p")
    dact = _matmul(dy, w["wd"], "nt", BF16, "ffn_down_dx", b_lead="p")
    duu, dgg, g["ffn_conv_w"], g["ffn_conv_b"] = _conv_bwd(gg, uu, dact, w["cw"], w["cb"])
    g["ffn_w_gate"] = _matmul(dgg, fn, "tn", BF16, "ffn_gate_dw", a_lead="p")
    g["ffn_w_up"] = _matmul(duu, fn, "tn", BF16, "ffn_up_dw", a_lead="p")
    dh2, g["norm_ffn"] = _matmul(dgg, w["wg"], "nn", F32, "ffn_dx_norm_bwd", a_lead="k", b_lead="k", more=(duu, w["wu"]),
                                 epilogue=_norm_bwd_epilogue(h2, w["norm_ffn"], dy))

    g["xa_w_o"] = _matmul(xo, dh2, "tn", BF16, "xa_o_dw")
    dxo = _matmul(dh2, w["xo"], "nt", F32, "xa_o_dx")

    def xa_bwd(r, k):
        kvh = _heads(k[0], 2 * XA_HEADS)
        dq_, dk_, dv_ = [], [], []
        dqn, dkn = 0.0, 0.0
        for h, (a, d_) in enumerate(zip(_heads(r[0], XA_HEADS), _heads(r[1], XA_HEADS))):
            _, vjp = jax.vjp(_xa_head, a, kvh[h], kvh[XA_HEADS + h], k[1], k[2])
            ga, gk, gv, gqn, gkn = vjp(d_)
            dq_.append(ga)
            dk_.append(gk)
            dv_.append(gv)
            dqn, dkn = dqn + gqn, dkn + gkn
        return [_cat(dq_)], [_cat(dk_ + dv_), dqn, dkn]

    dxq, dxkv, g["xa_q_norm"], g["xa_k_norm"] = _rows_call(
        xa_bwd, [_row(xq), _row(dxo)], [xkv, w["xa_q_norm"], w["xa_k_norm"]], [(512, BF16)],
        [xkv.shape, (1, 128), (1, 128)], name="xa_bwd")
    g["xa_w_q"] = _matmul(hn, dxq, "tn", BF16, "xa_q_dw")
    dh1, g["norm_xa"] = _matmul(dxq, w["xq"], "nt", F32, "xa_q_dx_norm_bwd",
                                epilogue=_norm_bwd_epilogue(h1, w["norm_xa"], dh2))
    g["xa_w_kv"] = _matmul(mn, dxkv, "tn", BF16, "xa_kv_dw")
    dmn = _matmul(dxkv, w["xkv"], "nt", F32, "xa_kv_dx")
    _, g["norm_mem"] = _norm_bwd(mem, w["norm_mem"], dmn, dmn, "norm_mem_bwd")

    g["w_out"] = _matmul(cat, dh1, "tn", BF16, "out_dw")
    dcat = _matmul(dh1, w["out"], "nt", F32, "out_dx")

    def mix_bwd(r, k):
        do_, dog_ = [], []
        dgn = 0.0
        for o, g_, d_ in zip(_heads(r[0], GLA_HEADS), _heads(r[1], GLA_HEADS), _heads(r[2], GLA_HEADS)):
            _, vjp = jax.vjp(_mix_head, o, g_, k[0])
            a, b, gn_ = vjp(d_)
            do_.append(a)
            dog_.append(b)
            dgn = dgn + gn_
        return [_cat(do_), _cat(dog_)], [dgn]

    do_gla, d_og, g["gla_out_norm"] = _rows_call(mix_bwd, [_row(o_gla), og, _row(dcat, 512, 0)], [w["gla_out_norm"]],
                                                 [(512, F32), (512, BF16)], [(1, 128)], name="mix_bwd")

    late_parts = _late_grad_shards(g)
    dq_r, dk_r, dv_mla, lands_mlp = _attn_bwd(q_r, k_r, v_mla, o_mla, lse, dcat,
                                              _scatter_plan([late_parts[n] for n in LATE_MLP]))

    def qk_bwd(r, k):
        dqs, dks = [], []
        dkpe, dqn, dkn = 0.0, 0.0, 0.0
        for qh, kh, dqh, dkh in zip(_heads(r[0], MLA_HEADS), _heads(r[1], MLA_HEADS), _heads(r[6], MLA_HEADS),
                                    _heads(r[7], MLA_HEADS)):
            _, vjp = jax.vjp(lambda a, b, e, f, h_: _qk_head(a, b, e, r[3], r[4], r[5], f, h_), qh, kh, r[2], k[0], k[1])
            ga, gb, ge, gf, gh = vjp((dqh, dkh))
            dqs.append(ga)
            dks.append(gb)
            dkpe, dqn, dkn = dkpe + ge, dqn + gf, dkn + gh
        return [_cat(dqs), _cat(dks), dkpe], [dqn, dkn]

    dq_up, dk_up, d_kpe, g["q_norm"], g["k_norm"] = _rows_call(
        qk_bwd, [_row(q_up), _row(k_up), kpe] + tabs + [_row(dq_r), _row(dk_r)], [w["q_norm"], w["k_norm"]],
        [(1024, BF16), (1024, BF16), (128, BF16)], [(1, 128), (1, 128)], name="mla_qk_bwd")
    g["uq"] = _matmul(q_lat, dq_up, "tn", BF16, "mla_q_dw")
    dq_lat = _matmul(dq_up, w["uq"], "nt", F32, "mla_q_dx")
    g["k"] = _matmul(kv_lat, dk_up, "tn", BF16, "mla_k_dw")
    g["v"] = _matmul(kv_lat, dv_mla, "tn", BF16, "mla_v_dw")
    dkv_lat = _matmul(dk_up, w["k"], "nt", F32, "mla_k_dx")
    dkv_lat = _matmul(dv_mla, w["v"], "nt", F32, "mla_v_dx", residual=dkv_lat)

    def lat_bwd(r, k):
        _, vjp1 = jax.vjp(_rms, r[0], k[0])
        _, vjp2 = jax.vjp(_rms, r[1], k[1])
        a, ga = vjp1(r[2])
        b, gb = vjp2(r[3])
        return [a, b], [ga, gb]

    d_cq, d_ckv, g["mla_q_a_norm"], g["mla_kv_a_norm"] = _rows_call(
        lat_bwd, [cq, ckv, _row(dq_lat), _row(dkv_lat)], [w["q_a_norm"], w["kv_a_norm"]],
        [(256, BF16), (128, BF16)], [(1, 256), (1, 128)], name="mla_lat_bwd")

    dgq, dgk, dla, dgv, lands_mix = _gla_bwd(proj, la, states, do_gla, _scatter_plan([late_parts[n] for n in LATE_MIX]))
    lands_late = dict(zip(LATE_MLP + LATE_MIX, list(lands_mlp) + list(lands_mix), strict=True))

    def gate_bwd(r, k):
        _, vjp = jax.vjp(_gate_fn, r[0], k[0], k[1])
        a, gw, gb = vjp(r[1])
        return [a], [gw, gb]

    d_alr, g["w2"], g["gla_gate_b"] = _rows_call(gate_bwd, [alr, _row(dla)], [w["w2"], w["gate_b"]], [(128, BF16)],
                                                 [(128, 256), (1, 256)], name="gla_gate_bwd")

    dproj = jnp.concatenate([dgq.astype(BF16), dgk.astype(BF16), dgv.astype(BF16), d_og, d_cq, d_ckv, d_kpe, d_alr],
                            axis=1)
    g["in"] = _matmul(xn, dproj, "tn", BF16, "proj_dw")
    dx, g["norm_mix"] = _matmul(dproj, w["in"], "nt", F32, "proj_dx_norm_bwd",
                                epilogue=_norm_bwd_epilogue(x, w["norm_mix"], dh1))
    return loss[0, 0], dx, g, lands_late


def _join_shards(pieces, axis):
    if axis == 0:
        return pieces.reshape(-1, pieces.shape[2])
    return jnp.transpose(pieces, (1, 0, 2)).reshape(pieces.shape[1], -1)


def _split_shards(full, axis):
    r, c = full.shape
    if axis == 0:
        return full.reshape(4, r // 4, c)
    return jnp.transpose(full.reshape(r, 4, c // 4), (1, 0, 2))


def _early_layout(gath, rep):
    w_in = _join_shards(gath["w_in"], 1)
    z = lambda n: jnp.zeros((D_MODEL, n), w_in.dtype)
    seg = lambda lo, n: w_in[:, lo:lo + n]
    ukv = _join_shards(gath["mla_w_ukv"], 1).reshape(MLA_KV_RANK, MLA_HEADS, MLA_NOPE + MLA_V)
    w = {
        "in": jnp.concatenate([seg(N_GQ, 256), seg(N_GK, 256), seg(N_GV, 512), seg(N_OG, 512), seg(N_CQ, 256),
                               seg(N_CKV, 128), z(64), seg(N_KPE, 32), z(32), seg(N_ALR, 16), z(112)], axis=1),
        "uq": jnp.pad(_join_shards(gath["mla_w_uq"], 1).reshape(MLA_Q_RANK, MLA_HEADS, MLA_QK),
                      ((0, 0), (0, 0), (0, LANES - MLA_QK))).reshape(MLA_Q_RANK, MLA_HEADS * LANES),
        "k": jnp.pad(ukv[:, :, :MLA_NOPE], ((0, 0), (0, 0), (0, LANES - MLA_NOPE))).reshape(MLA_KV_RANK, -1),
        "v": ukv[:, :, MLA_NOPE:].reshape(MLA_KV_RANK, MLA_HEADS * MLA_V),
        "w2": jnp.pad(_join_shards(gath["gla_gate_w2"], 1), ((0, LANES - GLA_RANK), (0, 0))),
        "cb": rep["ffn_conv_b"].reshape(4, 1, D_FF // 4),
        "q_norm": jnp.pad(rep["mla_q_norm"], ((0, 0), (0, LANES - MLA_QK))),
        "k_norm": jnp.pad(rep["mla_k_norm"], ((0, 0), (0, LANES - MLA_QK))),
        "q_a_norm": rep["mla_q_a_norm"], "kv_a_norm": rep["mla_kv_a_norm"], "gate_b": rep["gla_gate_b"],
    }
    for n in ("norm_mix", "gla_out_norm", "norm_xa", "norm_mem", "xa_q_norm", "xa_k_norm", "norm_ffn"):
        w[n] = rep[n]
    return w


def _late_layout(gath):
    return {"out": _join_shards(gath["w_out"], 0), "xq": _join_shards(gath["xa_w_q"], 0),
            "xkv": _join_shards(gath["xa_w_kv"], 0), "xo": _join_shards(gath["xa_w_o"], 1),
            "wg": gath["ffn_w_gate"], "wu": gath["ffn_w_up"], "wd": gath["ffn_w_down"], "cw": gath["ffn_conv_w"]}


def _late_grad_shards(g):
    sh = {"w_out": _split_shards(g["w_out"], 0), "xa_w_q": _split_shards(g["xa_w_q"], 0),
          "xa_w_kv": _split_shards(g["xa_w_kv"], 0), "xa_w_o": _split_shards(g["xa_w_o"], 1),
          "ffn_w_gate": g["ffn_w_gate"], "ffn_w_up": g["ffn_w_up"], "ffn_conv_w": g["ffn_conv_w"],
          "ffn_w_down": g["ffn_w_down"]}
    return {n: v.astype(BF16) for n, v in sh.items()}


def _early_grad_shards(g):
    gi = g["in"]
    seg = lambda lo, n: gi[:, lo:lo + n]
    w_in = jnp.concatenate([seg(P_GQ, 256), seg(P_GK, 256), seg(P_GV, 512), seg(P_ALR, 16), seg(P_OG, 512),
                            seg(P_CQ, 256), seg(P_CKV, 128), seg(P_KPE + 64, 32)], axis=1)
    uq = g["uq"].reshape(MLA_Q_RANK, MLA_HEADS, LANES)[:, :, :MLA_QK].reshape(MLA_Q_RANK, -1)
    ukv = jnp.concatenate([g["k"].reshape(MLA_KV_RANK, MLA_HEADS, LANES)[:, :, :MLA_NOPE],
                           g["v"].reshape(MLA_KV_RANK, MLA_HEADS, MLA_V)], axis=2).reshape(MLA_KV_RANK, -1)
    sh = {"w_in": _split_shards(w_in, 1), "gla_gate_w2": _split_shards(g["w2"][:GLA_RANK], 1),
          "mla_w_uq": _split_shards(uq, 1), "mla_w_ukv": _split_shards(ukv, 1)}
    sh = {n: v.astype(BF16) for n, v in sh.items()}
    rep = {n: g[n] for n in REPLICATED if n in g}
    rep["mla_q_norm"] = g["q_norm"][:, :MLA_QK]
    rep["mla_k_norm"] = g["k_norm"][:, :MLA_QK]
    rep["ffn_conv_b"] = g["ffn_conv_b"].reshape(1, D_FF)
    return sh, rep


SMALL_SHAPE = (8, 1024)


def _pack_small(vectors):
    flat = jnp.concatenate(vectors, axis=1)
    return jnp.pad(flat, ((0, 0), (0, SMALL_SHAPE[0] * SMALL_SHAPE[1] - flat.shape[1]))).reshape(SMALL_SHAPE)


def _unpack_small(buf, widths):
    flat = buf.reshape(1, -1)
    out, off = [], 0
    for wd in widths:
        out.append(flat[:, off:off + wd])
        off += wd
    return out


ANY = pl.BlockSpec(memory_space=pl.ANY)


def _place():
    x, y, c = lax.axis_index("x"), lax.axis_index("y"), lax.axis_index("c")
    chips = [(1 - x, y), (x, 1 - y), (1 - x, 1 - y)]
    return x, y, c, chips


class _Comm:
    def __init__(self, ins, out_shape, sems, start, finish, mid=None):
        self.ins, self.out_shape, self.sems = list(ins), list(out_shape), list(sems)
        self.start, self.finish, self.mid = start, finish, mid or (lambda *args: None)


def _run_comm(plan, name):
    ni, no = len(plan.ins), len(plan.out_shape)

    def body(*refs):
        ins, outs, sems = refs[:ni], refs[ni:ni + no], refs[ni + no:]
        place = _place()
        plan.start(place, ins, outs, sems)
        plan.mid(place, ins, outs, sems)
        plan.finish(place, ins, outs, sems)

    return pl.pallas_call(body, in_specs=[ANY] * ni, out_specs=[ANY] * no, out_shape=plan.out_shape,
                          scratch_shapes=plan.sems, name=name)(*plan.ins)


def _gather_plan(shards):
    n = len(shards)
    split = [s.shape[0] % (2 * BF16_ROWS) == 0 for s in shards]

    def rows(ref, t, c):
        if not split[t]:
            return ref
        half = shards[t].shape[0] // 2
        return ref.at[pl.ds(pl.multiple_of(c * half, BF16_ROWS), half)]

    def remote(src, dst, ss, rs, to):
        return pltpu.make_async_remote_copy(src_ref=src, dst_ref=dst, send_sem=ss, recv_sem=rs, device_id=to,
                                            device_id_type=MESH)

    def first_wave(place, ins, outs, sems):
        x, y, c, chips = place
        ici_s, ici_r, _, _, local = sems
        me = 2 * x + y
        own = [pltpu.make_async_copy(ins[t], outs[t].at[me], local.at[t]) for t in range(n)]
        push = [remote(rows(ins[t], t, c), rows(outs[t].at[me], t, c), ici_s.at[3 * t + j], ici_r.at[3 * t + j], (px, py, c))
                for t in range(n) for j, (px, py) in enumerate(chips)]
        return own, push

    def second_wave(place, ins, outs, sems, last):
        x, y, c, chips = place
        ici_s, ici_r, d2d_s, d2d_r, local = sems
        sib = (x, y, 1 - c)
        out = []
        for t in range(n):
            for j, (px, py) in enumerate(chips):
                block = outs[t].at[2 * px + py]
                got = rows(block, t, c)
                if split[t]:
                    hand = remote(got, got, d2d_s.at[3 * t + j], d2d_r.at[3 * t + j], sib)
                    theirs = rows(block, t, 1 - c)
                    other = (remote(theirs, theirs, local.at[0], d2d_r.at[3 * t + j], sib) if last else
                             remote(got, got, local.at[0], ici_r.at[3 * t + j], sib))
                    out.append((other, hand))
                elif last:
                    out.append((remote(got, got, local.at[0], ici_r.at[3 * t + j], sib), None))
        return out

    def start(place, ins, outs, sems):
        own, push = first_wave(place, ins, outs, sems)
        for cp in own + push:
            cp.start()

    def mid(place, ins, outs, sems):
        for arrival, hand in second_wave(place, ins, outs, sems, False):
            arrival.wait_recv()
            hand.start()

    def finish(place, ins, outs, sems):
        own, push = first_wave(place, ins, outs, sems)
        for arrival, hand in second_wave(place, ins, outs, sems, True):
            arrival.wait_recv()
            if hand is not None:
                hand.wait_send()
        for cp in push:
            cp.wait_send()
        for cp in own:
            cp.wait()

    dma = pltpu.SemaphoreType.DMA
    return _Comm(shards, [jax.ShapeDtypeStruct((4,) + s.shape, s.dtype) for s in shards],
                 [dma((3 * n,)), dma((3 * n,)), dma((3 * n,)), dma((3 * n,)), dma((n,))], start, finish, mid)


def _scatter_plan(parts, small=None):
    n = len(parts)
    ns = 0 if small is None else 1

    def unpack(place, ins, outs, sems):
        x, y, c, chips = place
        return x, y, c, chips, 2 * x + y, 4 * x + 2 * y + c, (x, y, 1 - c)

    def remote(src, dst, ss, rs, to):
        return pltpu.make_async_remote_copy(src_ref=src, dst_ref=dst, send_sem=ss, recv_sem=rs, device_id=to,
                                            device_id_type=MESH)

    def first_wave(place, ins, outs, sems):
        x, y, c, chips, me, dev, sib = unpack(place, ins, outs, sems)
        ici_s, ici_r, d2d_s, d2d_r, sm_s, sm_r, local = sems
        own, push = [], []
        if ns:
            own.append(pltpu.make_async_copy(ins[n], outs[n].at[dev], local.at[n]))
            for k in range(1, 8):
                px = (1 - x) if (k >> 2) & 1 else x
                py = (1 - y) if (k >> 1) & 1 else y
                pc = (1 - c) if k & 1 else c
                push.append(remote(ins[n], outs[n].at[dev], sm_s.at[k - 1], sm_r.at[k - 1], (px, py, pc)))
        for t in range(n):
            own.append(pltpu.make_async_copy(ins[t].at[me], outs[t].at[dev], local.at[t]))
            push.append(remote(ins[t].at[me], outs[t].at[dev], d2d_s.at[4 * t], d2d_r.at[4 * t], sib))
            for j, (px, py) in enumerate(chips):
                push.append(remote(ins[t].at[2 * px + py], outs[t].at[dev], ici_s.at[3 * t + j], ici_r.at[3 * t + j],
                                   (px, py, c)))
        return own, push

    def start(place, ins, outs, sems):
        own, push = first_wave(place, ins, outs, sems)
        for cp in own + push:
            cp.start()

    def landed(dst, rs, sems, sib):
        remote(dst, dst, sems[-1].at[0], rs, sib).wait_recv()

    def forwards(place, ins, outs, sems):
        x, y, c, chips, me, dev, sib = unpack(place, ins, outs, sems)
        d2d_s, d2d_r = sems[2], sems[3]
        slots = [(t, j, outs[t].at[4 * px + 2 * py + c]) for t in range(n) for j, (px, py) in enumerate(chips)]
        return [(t, j, slot, remote(slot, slot, d2d_s.at[4 * t + 1 + j], d2d_r.at[4 * t + 1 + j], sib))
                for t, j, slot in slots]

    def mid(place, ins, outs, sems):
        sib = unpack(place, ins, outs, sems)[-1]
        for t, j, slot, cp in forwards(place, ins, outs, sems):
            landed(slot, sems[1].at[3 * t + j], sems, sib)
            cp.start()

    def finish(place, ins, outs, sems):
        x, y, c, chips, me, dev, sib = unpack(place, ins, outs, sems)
        d2d_r, sm_r = sems[3], sems[5]
        own, push = first_wave(place, ins, outs, sems)
        push += [cp for _, _, _, cp in forwards(place, ins, outs, sems)]
        for t in range(n):
            landed(outs[t].at[4 * x + 2 * y + (1 - c)], d2d_r.at[4 * t], sems, sib)
            for j, (px, py) in enumerate(chips):
                landed(outs[t].at[4 * px + 2 * py + (1 - c)], d2d_r.at[4 * t + 1 + j], sems, sib)
        if ns:
            for k in range(1, 8):
                px = (1 - x) if (k >> 2) & 1 else x
                py = (1 - y) if (k >> 1) & 1 else y
                pc = (1 - c) if k & 1 else c
                landed(outs[n].at[4 * px + 2 * py + pc], sm_r.at[k - 1], sems, sib)
        for cp in push:
            cp.wait_send()
        for cp in own:
            cp.wait()

    dma = pltpu.SemaphoreType.DMA
    ins = list(parts) + ([small] if ns else [])
    out_shape = [jax.ShapeDtypeStruct((8,) + p.shape[1:], p.dtype) for p in parts]
    if ns:
        out_shape.append(jax.ShapeDtypeStruct((8,) + small.shape, small.dtype))
    return _Comm(ins, out_shape, [dma((3 * n,)), dma((3 * n,)), dma((4 * n,)), dma((4 * n,)), dma((7,)), dma((7,)),
                                  dma((n + 1,))], start, finish, mid)


ADAM_ROWS = 288


def _row_tile(r, cap):
    if r <= cap:
        return r
    return max(t for t in range(8, cap + 1, 8) if r % t == 0)


def _adamw_update(w, m, v, land):
    g = land[0].astype(F32)
    for i in range(1, 8):
        g = g + land[i].astype(F32)
    m_new = ADAM_B1 * m + (1.0 - ADAM_B1) * g
    v_new = ADAM_B2 * v + (1.0 - ADAM_B2) * (g * g)
    m_hat = m_new / (1.0 - ADAM_B1 ** ADAM_STEP)
    v_hat = v_new / (1.0 - ADAM_B2 ** ADAM_STEP)
    return g, -ADAM_LR * (m_hat / (jnp.sqrt(v_hat) + ADAM_EPS) + ADAM_WD * w), m_new, v_new


def _adamw(tensors, name, comm=None):
    k = len(tensors)
    r, c = tensors[0][0].shape
    t = _row_tile(r, ADAM_ROWS // k)
    n = r // t
    nci, nco, nsem = (len(comm.ins), len(comm.out_shape), len(comm.sems)) if comm else (0, 0, 0)

    def kern(*refs):
        ins, cins, outs, couts, csems = _split_refs(refs, (4 * k, nci, 4 * k, nco, nsem))
        if comm:
            place = _place()

            @pl.when(pl.program_id(0) == 0)
            def _():
                comm.start(place, cins, couts, csems)

        for i in range(k):
            w_ref, m_ref, v_ref, l_ref = ins[4 * i:4 * i + 4]
            res = _adamw_update(w_ref[...], m_ref[...], v_ref[...], l_ref)
            for ref, val in zip(outs[4 * i:4 * i + 4], res, strict=True):
                ref[...] = val
        if comm:
            @pl.when(pl.program_id(0) == n - 1)
            def _():
                comm.mid(place, cins, couts, csems)
                comm.finish(place, cins, couts, csems)

    spec = pl.BlockSpec((t, c), lambda i: (i, 0))
    lspec = pl.BlockSpec((8, t, c), lambda i: (0, i, 0))
    res = pl.pallas_call(
        kern, grid=(n,), in_specs=[spec, spec, spec, lspec] * k + [ANY] * nci, out_specs=[spec] * (4 * k) + [ANY] * nco,
        out_shape=[jax.ShapeDtypeStruct((r, c), F32)] * (4 * k) + (comm.out_shape if comm else []),
        scratch_shapes=comm.sems if comm else [],
        compiler_params=pltpu.CompilerParams(dimension_semantics=("arbitrary" if comm else "parallel",),
                                             vmem_limit_bytes=VMEM_LIMIT),
        name=name)(*[x for tens in tensors for x in tens], *(comm.ins if comm else []))
    return [res[4 * i:4 * i + 4] for i in range(k)], res[4 * k:]


def _step(a):
    def sq(n):
        v = a[n][0] if a[n].ndim == 3 else a[n]
        return v.T if n.removeprefix("m_").removeprefix("v_") in TRANSPOSED else v

    payload = lambda n: sq(n) if n in EXACT_GATHER else sq(n).astype(BF16)

    gathered = _run_comm(_gather_plan([payload(n) for n in EARLY]), "gather_early")
    w = _early_layout(dict(zip(EARLY, gathered, strict=True)), {n: a[n] for n in REPLICATED})

    loss, dx, g, lands_late = _local_step(sq("x"), sq("mem"), a["positions"][0], sq("loss_target"), w,
                                          [payload(n) for n in LATE])

    sh, rep = _early_grad_shards(g)
    small = _pack_small([rep[n] for n in REPLICATED] + [loss.reshape(1, 1)])
    *lands_early, land_small = _run_comm(_scatter_plan([sh[n] for n in EARLY], small), "scatter_last")
    quad = lambda n, land: (sq(n), sq("m_" + n), sq("v_" + n), land)
    lands = dict(zip(EARLY, lands_early, strict=True)) | lands_late

    outs = {}
    kinds = ("grad_", "delta_", "new_m_", "new_v_")
    for n, _ in SHARDED:
        res = _adamw([quad(n, lands[n])], "adamw_" + n)[0][0]
        for kind, val in zip(kinds, res, strict=True):
            outs[kind + n] = (val.T if n in TRANSPOSED else val).reshape(a[n].shape)
    zero = jnp.zeros((1, 1), F32)
    packed = [_pack_small([a[p + n] for n in REPLICATED] + [zero]) for p in ("", "m_", "v_")]
    res = _adamw([(*packed, land_small)], "adamw_replicated")[0][0]
    widths = [a[n].shape[1] for n in REPLICATED] + [1]
    for kind, buf in zip(kinds, res, strict=True):
        *vals, total = _unpack_small(buf, widths)
        for n, val in zip(REPLICATED, vals, strict=True):
            outs[kind + n] = val
        if kind == "grad_":
            loss = total[0, 0]

    ordered = [outs[kind + n] for kind in kinds for n in WEIGHTS]
    return (loss, dx[None], *ordered)


def kernel(x, mem, positions, norm_mix, w_in, gla_gate_w2, gla_gate_b, gla_out_norm, mla_q_a_norm, mla_w_uq, mla_kv_a_norm, mla_w_ukv, mla_q_norm, mla_k_norm, w_out, norm_xa, norm_mem, xa_w_q, xa_w_kv, xa_q_norm, xa_k_norm, xa_w_o, norm_ffn, ffn_w_gate, ffn_w_up, ffn_conv_w, ffn_conv_b, ffn_w_down, loss_target, m_norm_mix, m_w_in, m_gla_gate_w2, m_gla_gate_b, m_gla_out_norm, m_mla_q_a_norm, m_mla_w_uq, m_mla_kv_a_norm, m_mla_w_ukv, m_mla_q_norm, m_mla_k_norm, m_w_out, m_norm_xa, m_norm_mem, m_xa_w_q, m_xa_w_kv, m_xa_q_norm, m_xa_k_norm, m_xa_w_o, m_norm_ffn, m_ffn_w_gate, m_ffn_w_up, m_ffn_conv_w, m_ffn_conv_b, m_ffn_w_down, v_norm_mix, v_w_in, v_gla_gate_w2, v_gla_gate_b, v_gla_out_norm, v_mla_q_a_norm, v_mla_w_uq, v_mla_kv_a_norm, v_mla_w_ukv, v_mla_q_norm, v_mla_k_norm, v_w_out, v_norm_xa, v_norm_mem, v_xa_w_q, v_xa_w_kv, v_xa_q_norm, v_xa_k_norm, v_xa_w_o, v_norm_ffn, v_ffn_w_gate, v_ffn_w_up, v_ffn_conv_w, v_ffn_conv_b, v_ffn_w_down):
    return _step(dict(locals()))
```

```python
import functools

import jax
import jax.numpy as jnp
from jax import lax
from jax.experimental import pallas as pl
from jax.experimental.pallas import tpu as pltpu

F32, BF16 = jnp.float32, jnp.bfloat16
MESH = pl.DeviceIdType.MESH

D_MODEL = 1024
EPS = 1e-6
GLA_HEADS, GLA_DK, GLA_DV, GLA_RANK, GLA_CHUNK = 4, 64, 128, 16, 64
GLA_GATE_NORM = 16.0
MLA_HEADS, MLA_Q_RANK, MLA_KV_RANK, MLA_NOPE, MLA_ROPE, MLA_V = 8, 256, 128, 64, 32, 64
MLA_QK = MLA_NOPE + MLA_ROPE
ROPE_THETA = 10000.0
LOG2E, LN2 = 1.4426950408889634, 0.6931471805599453
XA_HEADS, XA_DIM = 4, 128
D_FF = 2816
ADAM_LR, ADAM_B1, ADAM_B2, ADAM_EPS, ADAM_WD, ADAM_STEP = 0.001, 0.9, 0.999, 1e-08, 0.01, 10

LANES = 128
BF16_ROWS = 16
VMEM_LIMIT = 56 * 1024 * 1024
MATMUL_VMEM = 44 * 1024 * 1024

P_GQ, P_GK, P_GV, P_OG, P_CQ, P_CKV, P_KPE, P_ALR, P_WIDTH = 0, 256, 512, 1024, 1536, 1792, 1920, 2048, 2176
N_GQ, N_GK, N_GV, N_ALR, N_OG, N_CQ, N_CKV, N_KPE, N_WIDTH = 0, 256, 512, 1024, 1040, 1552, 1808, 1936, 1968

SHARDED = (("w_in", 1), ("gla_gate_w2", 1), ("mla_w_uq", 1), ("mla_w_ukv", 1), ("w_out", 0), ("xa_w_q", 0),
           ("xa_w_kv", 0), ("xa_w_o", 1), ("ffn_w_gate", 1), ("ffn_w_up", 1), ("ffn_conv_w", 1), ("ffn_w_down", 0))
REPLICATED = ("norm_mix", "gla_gate_b", "gla_out_norm", "mla_q_a_norm", "mla_kv_a_norm", "mla_q_norm", "mla_k_norm",
              "norm_xa", "norm_mem", "xa_q_norm", "xa_k_norm", "norm_ffn", "ffn_conv_b")
EXACT_GATHER = ("gla_gate_w2", "ffn_conv_w")
TRANSPOSED = ("ffn_w_gate", "ffn_w_up")
EARLY = ("w_in", "gla_gate_w2", "mla_w_uq", "mla_w_ukv")
LATE = tuple(n for n, _ in SHARDED if n not in EARLY)
LATE_MLP = tuple(n for n in LATE if n.startswith("ffn_"))
LATE_MIX = tuple(n for n in LATE if not n.startswith("ffn_"))
WEIGHTS = ("norm_mix", "w_in", "gla_gate_w2", "gla_gate_b", "gla_out_norm", "mla_q_a_norm", "mla_w_uq",
           "mla_kv_a_norm", "mla_w_ukv", "mla_q_norm", "mla_k_norm", "w_out", "norm_xa", "norm_mem", "xa_w_q",
           "xa_w_kv", "xa_q_norm", "xa_k_norm", "xa_w_o", "norm_ffn", "ffn_w_gate", "ffn_w_up", "ffn_conv_w",
           "ffn_conv_b", "ffn_w_down")


_NN = ((1,), (0,))
_NT = ((1,), (1,))
_TN = ((0,), (0,))


def _dg(a, b, dims):
    return lax.dot_general(a.astype(BF16), b.astype(BF16), (dims, ((), ())), preferred_element_type=F32)


@jax.custom_vjp
def _dot_nn(a, b):
    return _dg(a, b, _NN)


_dot_nn.defvjp(lambda a, b: (_dg(a, b, _NN), (a, b)),
               lambda r, g: (_dg(g, r[1], _NT).astype(r[0].dtype), _dg(r[0], g, _TN).astype(r[1].dtype)))


@jax.custom_vjp
def _dot_nt(a, b):
    return _dg(a, b, _NT)


_dot_nt.defvjp(lambda a, b: (_dg(a, b, _NT), (a, b)),
               lambda r, g: (_dg(g, r[1], _NN).astype(r[0].dtype), _dg(g, r[0], _TN).astype(r[1].dtype)))


@jax.custom_vjp
def _dot_tn(a, b):
    return _dg(a, b, _TN)


_dot_tn.defvjp(lambda a, b: (_dg(a, b, _TN), (a, b)),
               lambda r, g: (_dg(r[1], g, _NT).astype(r[0].dtype), _dg(r[0], g, _NN).astype(r[1].dtype)))


def _rms(x, w, n=None):
    n = x.shape[-1] if n is None else n
    ms = jnp.sum(x * x, axis=-1, keepdims=True) * (1.0 / n)
    return x * lax.rsqrt(ms + EPS) * w


def _silu(x):
    return x * jax.nn.sigmoid(x)


def _log_sigmoid(x):
    return jnp.minimum(x, 0.0) - jnp.log(1.0 + jnp.exp(-jnp.abs(x)))


@jax.custom_vjp
def _rope(y, c, sa, sb):
    return y * c + pltpu.roll(y, LANES - 16, 1) * sa + pltpu.roll(y, 16, 1) * sb


def _rope_bwd(res, g):
    c, sa, sb = res
    gy = g * c + pltpu.roll(g * sa, 16, 1) + pltpu.roll(g * sb, LANES - 16, 1)
    return gy, jnp.zeros_like(c), jnp.zeros_like(sa), jnp.zeros_like(sb)


_rope.defvjp(lambda y, c, sa, sb: (_rope(y, c, sa, sb), (c, sa, sb)), _rope_bwd)


def _lane_mask(lo, hi):
    lane = lax.broadcasted_iota(jnp.int32, (1, LANES), 1)
    return ((lane >= lo) & (lane < hi)).astype(F32)


def _tile(n, t):
    t = min(n, t)
    assert n % t == 0, (n, t)
    return t


class _Epilogue:
    def __init__(self, fn, rows=(), consts=(), outs=(), accs=()):
        self.fn, self.rows, self.consts, self.outs, self.accs = fn, list(rows), list(consts), list(outs), list(accs)


def _matmul(a, b, mode, out_dtype, name, residual=None, a_lead=None, b_lead=None, more=None, epilogue=None):
    (a0, a1), (b0, b1) = a.shape[-2:], b.shape[-2:]
    if mode == "nn":
        m, k, k2, n = a0, a1, b0, b1
    elif mode == "nt":
        m, k, n, k2 = a0, a1, b0, b1
    else:
        k, m, k2, n = a0, a1, b0, b1
    assert k == k2, (a.shape, b.shape, mode)
    npar = 4 if "p" in (a_lead, b_lead) else 1
    nsum = 4 if "k" in (a_lead, b_lead) else 1
    pairs = [(a, b)] + ([more] if more else [])
    a_item, b_item, o_item = a.dtype.itemsize, b.dtype.itemsize, jnp.dtype(out_dtype).itemsize
    ep = epilogue
    row_extra = 4 if residual is not None else 0
    if ep:
        row_extra += (sum(r.dtype.itemsize * wd for r, wd, _ in ep.rows) + sum(jnp.dtype(d).itemsize * wd for wd, d in ep.outs)) / n

    def vmem_need(tm, tn, tk):
        need = 2 * (nsum if a_lead == "k" else 1) * tm * tk * a_item + 2 * (nsum if b_lead == "k" else 1) * tk * tn * b_item
        need *= len(pairs)
        need += (0 if ep else 2 * tm * tn * o_item) + tm * tn * 4 * (2 if tk < k else 1)
        need += tm * tk * 2 * (a_item == 4 or mode == "tn") + tk * tn * 2 * (b_item == 4)
        return need + int(2 * tm * tn * row_extra) + (3 * tm * tn * 4 if ep else 0)

    halvings = (4096, 2048, 1024, 512, 256, 128, 64, 32, 16, 8)
    if mode == "tn":
        tm = m if m <= 1408 else m // 2
        tn = n if tm * n <= 1024 * 2304 else n // 2
        tk = next((r for r in halvings if k % r == 0 and vmem_need(tm, tn, r) <= MATMUL_VMEM), k)
    else:
        tn, tk = n, k
        tm = next((r for r in halvings if m % r == 0 and vmem_need(r, tn, tk) <= MATMUL_VMEM), m)
    assert m % tm == 0 and n % tn == 0 and k % tk == 0
    assert ep is None or (tn == n and tk == k and npar == 1)
    nk = k // tk
    dims = {"nn": _NN, "nt": _NT, "tn": _TN}[mode]
    n_in = 2 * len(pairs) + (residual is not None)
    n_ep_in = len(ep.rows) + len(ep.consts) if ep else 0
    n_out = len(ep.outs) + len(ep.accs) if ep else 1

    def body(*refs):
        ab, rs, ep_in, outs, scratch = _split_refs(refs, (2 * len(pairs), n_in - 2 * len(pairs), n_ep_in, n_out, nk > 1))
        prod = None
        for a_ref, b_ref in zip(ab[0::2], ab[1::2]):
            for sh in range(nsum):
                term = _dg(a_ref[sh] if a_lead == "k" else a_ref[...], b_ref[sh] if b_lead == "k" else b_ref[...], dims)
                prod = term if prod is None else prod + term

        def finish(r):
            if rs:
                r = r + rs[0][...]
            if ep is None:
                outs[0][...] = r.astype(outs[0].dtype)
                return
            vals = [x[...] for x in ep_in]
            ro, ao = ep.fn(r, vals[:len(ep.rows)], vals[len(ep.rows):])
            for ref, val in zip(outs[:len(ep.outs)], ro, strict=True):
                ref[...] = val.astype(ref.dtype)
            if ep.accs:
                @pl.when(pl.program_id(0) == 0)
                def _():
                    for ref in outs[len(ep.outs):]:
                        ref[...] = jnp.zeros_like(ref)

                for ref, val in zip(outs[len(ep.outs):], ao, strict=True):
                    ref[...] += val

        if nk == 1:
            finish(prod)
            return
        acc = scratch[0]
        kk = pl.program_id(3)

        @pl.when(kk == 0)
        def _():
            acc[...] = prod

        @pl.when(kk > 0)
        def _():
            acc[...] += prod

        @pl.when(kk == nk - 1)
        def _():
            finish(acc[...])

    def spec(lead, blk, idx):
        if lead is None:
            return pl.BlockSpec(blk, lambda i, j, p, kk: idx(i, j, kk))
        if lead == "p":
            return pl.BlockSpec((None,) + blk, lambda i, j, p, kk: (p,) + idx(i, j, kk))
        return pl.BlockSpec((nsum,) + blk, lambda i, j, p, kk: (0,) + idx(i, j, kk))

    if mode == "nn":
        pair_specs = [spec(a_lead, (tm, tk), lambda i, j, kk: (i, kk)), spec(b_lead, (tk, tn), lambda i, j, kk: (kk, j))]
    elif mode == "nt":
        pair_specs = [spec(a_lead, (tm, tk), lambda i, j, kk: (i, kk)), spec(b_lead, (tn, tk), lambda i, j, kk: (j, kk))]
    else:
        pair_specs = [spec(a_lead, (tk, tm), lambda i, j, kk: (kk, i)), spec(b_lead, (tk, tn), lambda i, j, kk: (kk, j))]
    tile = spec(None, (tm, tn), lambda i, j, kk: (i, j))
    in_specs = pair_specs * len(pairs)
    args = [x for pair in pairs for x in pair]
    if residual is not None:
        assert npar == 1
        in_specs.append(tile)
        args.append(residual)
    if ep:
        in_specs += [pl.BlockSpec((tm, wd), functools.partial(lambda cb, i, j, p, kk: (i, cb), cb)) for _, wd, cb in ep.rows]
        in_specs += [pl.BlockSpec(c.shape, lambda i, j, p, kk: (0, 0)) for c in ep.consts]
        args += [r for r, _, _ in ep.rows] + ep.consts
        out_specs = [pl.BlockSpec((tm, wd), lambda i, j, p, kk: (i, 0)) for wd, _ in ep.outs]
        out_specs += [pl.BlockSpec(shape, lambda i, j, p, kk: (0, 0)) for shape in ep.accs]
        out_shape = [jax.ShapeDtypeStruct((m, wd), d) for wd, d in ep.outs] + [jax.ShapeDtypeStruct(sh, F32) for sh in ep.accs]
    else:
        out_specs = spec("p" if npar > 1 else None, (tm, tn), lambda i, j, kk: (i, j))
        out_shape = jax.ShapeDtypeStruct(((4,) if npar > 1 else ()) + (m, n), out_dtype)
    outer = "arbitrary" if ep and ep.accs else "parallel"
    return pl.pallas_call(
        body, grid=(m // tm, n // tn, npar, nk), in_specs=in_specs, out_specs=out_specs, out_shape=out_shape,
        scratch_shapes=[pltpu.VMEM((tm, tn), F32)] if nk > 1 else [],
        compiler_params=pltpu.CompilerParams(dimension_semantics=(outer, outer, outer, "arbitrary"),
                                             vmem_limit_bytes=VMEM_LIMIT),
        name=name)(*args)


def _row(a, width=None, col_block=0):
    return (a, a.shape[1] if width is None else width, col_block)


def _rows_call(body, rows, consts, outs, accs=(), *, name, tile=512):
    s = rows[0][0].shape[0]
    t = _tile(s, tile)
    nr, nc, no = len(rows), len(consts), len(outs)

    def kern(*refs):
        r = [x[...] for x in refs[:nr]]
        c = [x[...] for x in refs[nr:nr + nc]]
        o_refs = refs[nr + nc:nr + nc + no]
        a_refs = refs[nr + nc + no:]
        ro, ao = body(r, c)
        for ref, val in zip(o_refs, ro, strict=True):
            ref[...] = val.astype(ref.dtype)
        if a_refs:
            @pl.when(pl.program_id(0) == 0)
            def _():
                for ref in a_refs:
                    ref[...] = jnp.zeros_like(ref)

            for ref, val in zip(a_refs, ao, strict=True):
                ref[...] += val

    in_specs = [pl.BlockSpec((t, w), functools.partial(lambda cb, i: (i, cb), cb)) for (_, w, cb) in rows]
    in_specs += [pl.BlockSpec(c.shape, lambda i: (0, 0)) for c in consts]
    out_specs = [pl.BlockSpec((t, w), lambda i: (i, 0)) for (w, _) in outs]
    out_specs += [pl.BlockSpec(shape, lambda i: (0, 0)) for shape in accs]
    out_shape = [jax.ShapeDtypeStruct((s, w), dt) for (w, dt) in outs]
    out_shape += [jax.ShapeDtypeStruct(shape, F32) for shape in accs]
    return pl.pallas_call(
        kern, grid=(s // t,), in_specs=in_specs, out_specs=out_specs, out_shape=out_shape,
        compiler_params=pltpu.CompilerParams(dimension_semantics=("arbitrary" if accs else "parallel",),
                                             vmem_limit_bytes=VMEM_LIMIT),
        name=name)(*[r[0] for r in rows], *consts)


def _gla_chunk(q, k, la, v0, v1, s0, s1):
    c = q.shape[0]
    r = lax.broadcasted_iota(jnp.int32, (c, c), 0)
    cc = lax.broadcasted_iota(jnp.int32, (c, c), 1)
    tril = cc <= r
    cum = lax.dot_general(tril.astype(F32), la, (_NN, ((), ())), precision=lax.Precision.HIGHEST,
                          preferred_element_type=F32)
    cl = jnp.sum(la, axis=0, keepdims=True)
    qd = q * (GLA_DK ** -0.5) * jnp.exp(cum)
    ki = k * jnp.exp(-cum)
    ke = k * jnp.exp(cl - cum)
    dec = jnp.exp(cl)
    outs, news = [], []
    for h, (v, s) in enumerate(((v0, s0), (v1, s1))):
        mk = _lane_mask(GLA_DK * h, GLA_DK * (h + 1))
        qh = qd * mk
        att = jnp.where(tril, _dot_nt(qh, ki), 0.0)
        outs.append(_dot_nn(att, v) + _dot_nt(qh, s))
        news.append(s * dec + _dot_tn(v, ke * mk))
    return outs[0], outs[1], news[0], news[1]


def _gla_specs(tb, rev_nb=None):
    blk = (lambda b: b) if rev_nb is None else (lambda b: rev_nb - 1 - b)
    q = pl.BlockSpec((tb, 128), lambda p, b: (blk(b), P_GQ // 128 + p))
    k = pl.BlockSpec((tb, 128), lambda p, b: (blk(b), P_GK // 128 + p))
    la = pl.BlockSpec((tb, 128), lambda p, b: (blk(b), p))
    v = pl.BlockSpec((tb, 256), lambda p, b: (blk(b), P_GV // 256 + p))
    o = pl.BlockSpec((tb, 256), lambda p, b: (blk(b), p))
    st = pl.BlockSpec((tb // GLA_CHUNK, 2, 128, 128), lambda p, b: (blk(b), p, 0, 0))
    return q, k, la, v, o, st


def _gla_fwd(proj, la):
    s = proj.shape[0]
    tb = _tile(s, 512)
    nb, nch = s // tb, tb // GLA_CHUNK

    def kern(q_ref, k_ref, la_ref, v_ref, o_ref, st_ref, s_sc):
        @pl.when(pl.program_id(1) == 0)
        def _():
            s_sc[...] = jnp.zeros_like(s_sc)

        s0, s1 = s_sc[0], s_sc[1]
        for ci in range(nch):
            sl = slice(ci * GLA_CHUNK, (ci + 1) * GLA_CHUNK)
            st_ref[ci, 0] = s0
            st_ref[ci, 1] = s1
            o0, o1, s0, s1 = _gla_chunk(q_ref[sl, :], k_ref[sl, :], la_ref[sl, :], v_ref[sl, 0:128],
                                        v_ref[sl, 128:256], s0, s1)
            o_ref[sl, 0:128] = o0
            o_ref[sl, 128:256] = o1
        s_sc[0] = s0
        s_sc[1] = s1

    q, k, lasp, v, o, st = _gla_specs(tb)
    return pl.pallas_call(
        kern, grid=(2, nb), in_specs=[q, k, lasp, v], out_specs=[o, st],
        out_shape=[jax.ShapeDtypeStruct((s, 512), F32),
                   jax.ShapeDtypeStruct((s // GLA_CHUNK, GLA_HEADS, 128, 128), F32)],
        scratch_shapes=[pltpu.VMEM((2, 128, 128), F32)],
        compiler_params=pltpu.CompilerParams(dimension_semantics=("parallel", "arbitrary"),
                                             vmem_limit_bytes=VMEM_LIMIT),
        name="gla_fwd")(proj, proj, la, proj)


def _gla_bwd(proj, la, states, d_o, comm):
    s = proj.shape[0]
    tb = _tile(s, 512)
    nb, nch = s // tb, tb // GLA_CHUNK
    nci, nco = len(comm.ins), len(comm.out_shape)

    def kern(*refs):
        (q_ref, k_ref, la_ref, v_ref, do_ref, st_ref), cins, (dq_ref, dk_ref, dla_ref, dv_ref), couts, (ds_sc,), csems = \
            _split_refs(refs, (6, nci, 4, nco, 1, len(comm.sems)))
        place = _place()
        pair, blk = pl.program_id(0), pl.program_id(1)

        @pl.when((pair == 0) & (blk == 0))
        def _():
            comm.start(place, cins, couts, csems)

        @pl.when((pair == 1) & (blk == nb // 2))
        def _():
            comm.mid(place, cins, couts, csems)

        @pl.when(blk == 0)
        def _():
            ds_sc[...] = jnp.zeros_like(ds_sc)

        d0, d1 = ds_sc[0], ds_sc[1]
        for ci in reversed(range(nch)):
            sl = slice(ci * GLA_CHUNK, (ci + 1) * GLA_CHUNK)
            _, vjp = jax.vjp(_gla_chunk, q_ref[sl, :], k_ref[sl, :], la_ref[sl, :], v_ref[sl, 0:128],
                             v_ref[sl, 128:256], st_ref[ci, 0], st_ref[ci, 1])
            gq, gk, gla, gv0, gv1, d0, d1 = vjp((do_ref[sl, 0:128], do_ref[sl, 128:256], d0, d1))
            dq_ref[sl, :] = gq
            dk_ref[sl, :] = gk
            dla_ref[sl, :] = gla
            dv_ref[sl, 0:128] = gv0
            dv_ref[sl, 128:256] = gv1
        ds_sc[0] = d0
        ds_sc[1] = d1

        @pl.when((pair == 1) & (blk == nb - 1))
        def _():
            comm.finish(place, cins, couts, csems)

    q, k, lasp, v, o, st = _gla_specs(tb, rev_nb=nb)
    res = pl.pallas_call(
        kern, grid=(2, nb), in_specs=[q, k, lasp, v, o, st] + [ANY] * nci, out_specs=[lasp, lasp, lasp, o] + [ANY] * nco,
        out_shape=[jax.ShapeDtypeStruct((s, 256), F32), jax.ShapeDtypeStruct((s, 256), F32),
                   jax.ShapeDtypeStruct((s, 256), F32), jax.ShapeDtypeStruct((s, 512), F32)] + comm.out_shape,
        scratch_shapes=[pltpu.VMEM((2, 128, 128), F32)] + comm.sems,
        compiler_params=pltpu.CompilerParams(dimension_semantics=("arbitrary", "arbitrary"),
                                             vmem_limit_bytes=VMEM_LIMIT),
        name="gla_bwd")(proj, proj, la, proj, d_o, states, *comm.ins)
    return res[0], res[1], res[2], res[3], res[4:]


def _causal_keep(t, qi, ki):
    row = lax.broadcasted_iota(jnp.int32, (t, t), 0) + qi * t
    col = lax.broadcasted_iota(jnp.int32, (t, t), 1) + ki * t
    return col <= row


def _split_refs(refs, counts):
    out, off = [], 0
    for cnt in counts:
        out.append(refs[off:off + cnt])
        off += cnt
    return out


def _attn_fwd(q, k, v, comm, tile=1024):
    s = q.shape[0]
    t = _tile(s, tile)
    n = s // t
    nci, nco = len(comm.ins), len(comm.out_shape)

    def kern(*refs):
        (q_ref, k_ref, v_ref), cins, (o_ref, lse_ref), couts, (m_sc, l_sc, acc_sc), csems = _split_refs(
            refs, (3, nci, 2, nco, 3, len(comm.sems)))
        qi, ki = pl.program_id(1), pl.program_id(2)
        place = _place()

        @pl.when((pl.program_id(0) == 0) & (qi == 0) & (ki == 0))
        def _():
            comm.start(place, cins, couts, csems)

        @pl.when((pl.program_id(0) == MLA_HEADS // 2 - 1) & (qi == 0) & (ki == 0))
        def _():
            comm.mid(place, cins, couts, csems)

        first = lax.broadcasted_iota(jnp.int32, (t, LANES), 1) < MLA_V

        @pl.when(ki == 0)
        def _():
            m_sc[...] = jnp.full_like(m_sc, -jnp.inf)
            l_sc[...] = jnp.zeros_like(l_sc)
            acc_sc[...] = jnp.zeros_like(acc_sc)

        def update(diagonal):
            keep = _causal_keep(t, 0, 0)
            alphas, pvs = [], []
            for h in range(2):
                sc = _dg(q_ref[:, 128 * h:128 * (h + 1)], k_ref[:, 128 * h:128 * (h + 1)], _NT)
                if diagonal:
                    sc = jnp.where(keep, sc, -jnp.inf)
                m_prev = m_sc[h]
                m_new = jnp.maximum(m_prev, jnp.max(sc, axis=1, keepdims=True))
                alpha = jnp.exp2(m_prev - m_new)
                p = jnp.exp2(sc - m_new[:, 0:1])
                l_sc[h] = alpha * l_sc[h] + jnp.sum(p, axis=1, keepdims=True)
                m_sc[h] = m_new
                alphas.append(alpha)
                pvs.append(_dg(p, v_ref[...], _NN))
            acc_sc[...] = acc_sc[...] * jnp.where(first, alphas[0], alphas[1]) + jnp.where(first, pvs[0], pvs[1])

        @pl.when(ki < qi)
        def _():
            update(False)

        @pl.when(ki == qi)
        def _():
            update(True)

        @pl.when(ki == qi)
        def _():
            l = jnp.where(first, l_sc[0], l_sc[1])
            m = jnp.where(first, m_sc[0], m_sc[1])
            o_ref[...] = acc_sc[...] / l
            lse_ref[...] = m + jnp.log2(l)

        @pl.when((pl.program_id(0) == MLA_HEADS // 2 - 1) & (qi == n - 1) & (ki == n - 1))
        def _():
            comm.finish(place, cins, couts, csems)

    kv_idx = lambda p, qi, ki: (jnp.minimum(ki, qi), p)
    res = pl.pallas_call(
        kern, grid=(MLA_HEADS // 2, n, n),
        in_specs=[pl.BlockSpec((t, 256), lambda p, qi, ki: (qi, p)), pl.BlockSpec((t, 256), kv_idx),
                  pl.BlockSpec((t, 128), kv_idx)] + [ANY] * nci,
        out_specs=[pl.BlockSpec((t, 128), lambda p, qi, ki: (qi, p)), pl.BlockSpec((t, 128), lambda p, qi, ki: (qi, p))]
        + [ANY] * nco,
        out_shape=[jax.ShapeDtypeStruct((s, 512), F32), jax.ShapeDtypeStruct((s, 512), F32)] + comm.out_shape,
        scratch_shapes=[pltpu.VMEM((2, t, LANES), F32), pltpu.VMEM((2, t, LANES), F32), pltpu.VMEM((t, LANES), F32)]
        + comm.sems,
        compiler_params=pltpu.CompilerParams(dimension_semantics=("arbitrary", "arbitrary", "arbitrary"),
                                             vmem_limit_bytes=VMEM_LIMIT),
        name="mla_attn_fwd")(q, k, v, *comm.ins)
    return res[0], res[1], res[2:]


def _attn_bwd(q, k, v, o, lse, d_o, comm, tile=512):
    s = q.shape[0]
    t = _tile(s, tile)
    n = s // t
    nci, nco = len(comm.ins), len(comm.out_shape)

    def kern(*refs):
        (q_ref, k_ref, v_ref, o_ref, lse_ref, do_ref), cins, (dq_ref, dk_ref, dv_ref), couts, (dk_sc, dv_sc), csems = \
            _split_refs(refs, (6, nci, 3, nco, 2, len(comm.sems)))
        ki, qi = pl.program_id(1), pl.program_id(2)
        place = _place()

        @pl.when((pl.program_id(0) == 0) & (qi == 0) & (ki == 0))
        def _():
            comm.start(place, cins, couts, csems)

        @pl.when((pl.program_id(0) == MLA_HEADS // 2 - 1) & (qi == 0) & (ki == 0))
        def _():
            comm.mid(place, cins, couts, csems)

        @pl.when((ki == 0) & (qi == 0))
        def _():
            dq_ref[...] = jnp.zeros_like(dq_ref)

        @pl.when(qi == ki)
        def _():
            dk_sc[...] = jnp.zeros_like(dk_sc)
            dv_sc[...] = jnp.zeros_like(dv_sc)

        def update(diagonal):
            keep = _causal_keep(t, 0, 0)
            d_o = do_ref[...]
            prod = d_o * o_ref[...]
            rows = pl.ds(pl.multiple_of(qi * t, t), t)
            for h in range(2):
                hs = slice(128 * h, 128 * (h + 1))
                mk = _lane_mask(MLA_V * h, MLA_V * (h + 1))
                qh, kh = q_ref[:, hs], k_ref[:, hs]
                sc = _dg(qh, kh, _NT)
                if diagonal:
                    sc = jnp.where(keep, sc, -jnp.inf)
                p = jnp.exp2(sc - lse_ref[:, MLA_V * h:MLA_V * h + 1])
                doh = d_o * mk
                dp = _dg(doh * LN2, v_ref[...], _NT)
                delta = jnp.sum(prod * mk, axis=1, keepdims=True) * LN2
                ds = p * (dp - delta)
                dv_sc[...] += _dg(p, doh, _TN)
                dk_sc[:, hs] += _dg(ds, qh, _TN)
                dq_ref[rows, hs] += _dg(ds, kh, _NN)

        @pl.when(qi > ki)
        def _():
            update(False)

        @pl.when(qi == ki)
        def _():
            update(True)

        @pl.when(qi == n - 1)
        def _():
            dk_ref[...] = dk_sc[...]
            dv_ref[...] = dv_sc[...].astype(dv_ref.dtype)

        @pl.when((pl.program_id(0) == MLA_HEADS // 2 - 1) & (qi == n - 1) & (ki == n - 1))
        def _():
            comm.finish(place, cins, couts, csems)

    q_idx = lambda p, ki, qi: (jnp.maximum(qi, ki), p)
    res = pl.pallas_call(
        kern, grid=(MLA_HEADS // 2, n, n),
        in_specs=[pl.BlockSpec((t, 256), q_idx), pl.BlockSpec((t, 256), lambda p, ki, qi: (ki, p)),
                  pl.BlockSpec((t, 128), lambda p, ki, qi: (ki, p)), pl.BlockSpec((t, 128), q_idx),
                  pl.BlockSpec((t, 128), q_idx),
                  pl.BlockSpec((t, 128), q_idx)] + [ANY] * nci,
        out_specs=[pl.BlockSpec((s, 256), lambda p, ki, qi: (0, p)), pl.BlockSpec((t, 256), lambda p, ki, qi: (ki, p)),
                   pl.BlockSpec((t, 128), lambda p, ki, qi: (ki, p))] + [ANY] * nco,
        out_shape=[jax.ShapeDtypeStruct((s, 1024), F32), jax.ShapeDtypeStruct((s, 1024), F32),
                   jax.ShapeDtypeStruct((s, 512), BF16)] + comm.out_shape,
        scratch_shapes=[pltpu.VMEM((t, 256), F32), pltpu.VMEM((t, 128), F32)] + comm.sems,
        compiler_params=pltpu.CompilerParams(dimension_semantics=("arbitrary", "arbitrary", "arbitrary"),
                                             vmem_limit_bytes=VMEM_LIMIT),
        name="mla_attn_bwd")(q, k, v, o, lse, d_o, *comm.ins)
    return res[0], res[1], res[2], res[3:]


def _gate_fn(alr, w2, b):
    return _log_sigmoid(_dot_nn(alr, w2) + b) * (1.0 / GLA_GATE_NORM)


def _qk_head(qh, kh, kpe, c, sa, sb, qn, kn):
    kfull = kh + kpe * _lane_mask(MLA_NOPE, MLA_QK)
    q_r = _rope(_rms(qh, qn, MLA_QK), c, sa, sb) * (MLA_QK ** -0.5 * LOG2E)
    k_r = _rope(_rms(kfull, kn, MLA_QK), c, sa, sb)
    return q_r, k_r


def _mix_head(o, og, gn):
    return _rms(o, gn) * _silu(og)


def _xa_head(xq, xk, xv, qn, kn):
    sc = _dot_nt(_rms(xq, qn), _rms(xk, kn)) * (XA_DIM ** -0.5)
    e = jnp.exp(sc - lax.stop_gradient(jnp.max(sc, axis=1, keepdims=True)))
    p = e / jnp.sum(e, axis=1, keepdims=True)
    return _dot_nn(p, xv)


def _heads(x, n):
    return [x[:, 128 * h:128 * (h + 1)] for h in range(n)]


def _cat(xs):
    return jnp.concatenate(xs, axis=1)


def _norm_fwd(x, w, name):
    return _rows_call(lambda r, c: ([_rms(r[0], c[0])], []), [_row(x)], [w], [(x.shape[1], BF16)], name=name)[0]


def _norm_fwd_epilogue(w):
    return _Epilogue(lambda h, rows, consts: ([h, _rms(h, consts[0])], []), [], [w], [(D_MODEL, F32), (D_MODEL, BF16)], [])


def _norm_bwd_epilogue(x, w, add):
    def fn(d_out, rows, consts):
        _, vjp = jax.vjp(_rms, rows[0], consts[0])
        dx, dw = vjp(d_out)
        return [dx + rows[1]], [dw]

    return _Epilogue(fn, [_row(x), _row(add)], [w], [(D_MODEL, F32)], [w.shape])


def _norm_bwd(x, w, d_out, add, name):
    def body(r, c):
        _, vjp = jax.vjp(_rms, r[0], c[0])
        dx, dw = vjp(r[1])
        return [dx + r[2]], [dw]

    return _rows_call(body, [_row(x), _row(d_out), _row(add)], [w], [(x.shape[1], F32)], [w.shape], name=name)


CONV_HALO = BF16_ROWS


def _conv_specs(s, f, t):
    n8 = t // CONV_HALO
    cur = pl.BlockSpec((None, t, f), lambda j, i: (j, i, 0))
    prev = pl.BlockSpec((None, CONV_HALO, f), lambda j, i: (j, jnp.maximum(i * n8 - 1, 0), 0))
    nxt = pl.BlockSpec((None, CONV_HALO, f), lambda j, i: (j, jnp.minimum((i + 1) * n8, s // CONV_HALO - 1), 0))
    cw = pl.BlockSpec((None, 3, f), lambda j, i: (j, 0, 0))
    cb = pl.BlockSpec((None, 1, f), lambda j, i: (j, 0, 0))
    return cur, prev, nxt, cw, cb


def _conv_taps(g, prev, first):
    ext = jnp.concatenate([jnp.where(first, 0.0, prev.astype(F32)), g], axis=0)
    return pltpu.roll(ext, 1, 0)[CONV_HALO:], pltpu.roll(ext, 2, 0)[CONV_HALO:]


def _conv_fwd(gg, uu, cw, cb):
    _, s, f = gg.shape
    t = _tile(s, 512)

    def kern(g_ref, gp_ref, u_ref, cw_ref, cb_ref, o_ref):
        g = g_ref[...].astype(F32)
        g1, g2 = _conv_taps(g, gp_ref[...], pl.program_id(1) == 0)
        w = cw_ref[...]
        gc = cb_ref[...] + w[0:1] * g2 + w[1:2] * g1 + w[2:3] * g
        o_ref[...] = (_silu(gc) * u_ref[...].astype(F32)).astype(o_ref.dtype)

    cur, prev, _, cws, cbs = _conv_specs(s, f, t)
    return pl.pallas_call(
        kern, grid=(4, s // t), in_specs=[cur, prev, cur, cws, cbs], out_specs=cur,
        out_shape=jax.ShapeDtypeStruct(gg.shape, BF16),
        compiler_params=pltpu.CompilerParams(dimension_semantics=("parallel", "parallel"), vmem_limit_bytes=VMEM_LIMIT),
        name="ffn_conv_fwd")(gg, gg, uu, cw, cb)


def _conv_bwd(gg, uu, dact, cw, cb):
    _, s, f = gg.shape
    t = _tile(s, 512)
    nt = s // t

    def kern(g_ref, gp_ref, gn_ref, u_ref, un_ref, da_ref, dan_ref, cw_ref, cb_ref, du_ref, dg_ref, dcw_ref, dcb_ref):
        i = pl.program_id(1)
        cat = lambda a_ref, b_ref: jnp.concatenate([a_ref[...].astype(F32), b_ref[...].astype(F32)], axis=0)
        g, u, da = cat(g_ref, gn_ref), cat(u_ref, un_ref), cat(da_ref, dan_ref)
        g1, g2 = _conv_taps(g, gp_ref[...], i == 0)
        w = cw_ref[...]
        gc = cb_ref[...] + w[0:1] * g2 + w[1:2] * g1 + w[2:3] * g
        sg = jax.nn.sigmoid(gc)
        du_ref[...] = (da[:t] * (gc[:t] * sg[:t])).astype(du_ref.dtype)
        row = lax.broadcasted_iota(jnp.int32, (t + CONV_HALO, 1), 0)
        dgc = jnp.where((row < t) | (i < nt - 1), da * u * (sg * (1.0 + gc * (1.0 - sg))), 0.0)
        up1 = pltpu.roll(dgc, t + CONV_HALO - 1, 0)[:t]
        up2 = pltpu.roll(dgc, t + CONV_HALO - 2, 0)[:t]
        dgc = dgc[:t]
        dg_ref[...] = (w[2:3] * dgc + w[1:2] * up1 + w[0:1] * up2).astype(dg_ref.dtype)

        @pl.when(i == 0)
        def _():
            dcw_ref[...] = jnp.zeros_like(dcw_ref)
            dcb_ref[...] = jnp.zeros_like(dcb_ref)

        dcw_ref[0:1, :] += jnp.sum(dgc * g2[:t], axis=0, keepdims=True)
        dcw_ref[1:2, :] += jnp.sum(dgc * g1[:t], axis=0, keepdims=True)
        dcw_ref[2:3, :] += jnp.sum(dgc * g[:t], axis=0, keepdims=True)
        dcb_ref[...] += jnp.sum(dgc, axis=0, keepdims=True)

    cur, prev, nxt, cws, cbs = _conv_specs(s, f, t)
    return pl.pallas_call(
        kern, grid=(4, nt), in_specs=[cur, prev, nxt, cur, nxt, cur, nxt, cws, cbs], out_specs=[cur, cur, cws, cbs],
        out_shape=[jax.ShapeDtypeStruct(gg.shape, BF16), jax.ShapeDtypeStruct(gg.shape, BF16),
                   jax.ShapeDtypeStruct(cw.shape, F32), jax.ShapeDtypeStruct(cb.shape, F32)],
        compiler_params=pltpu.CompilerParams(dimension_semantics=("parallel", "arbitrary"), vmem_limit_bytes=VMEM_LIMIT),
        name="ffn_conv_bwd")(gg, gg, gg, uu, uu, dact, dact, cw, cb)


def _rope_tables(pos):
    half = MLA_ROPE // 2
    inv = ROPE_THETA ** (-jnp.arange(half, dtype=F32) / half)
    ang = pos.astype(F32)[:, None] * inv
    cos, sin = jnp.cos(ang), jnp.sin(ang)
    s = pos.shape[0]
    z = lambda w: jnp.zeros((s, w), F32)
    c = jnp.concatenate([jnp.ones((s, MLA_NOPE), F32), cos, cos, jnp.ones((s, LANES - MLA_QK), F32)], axis=1)
    sa = jnp.concatenate([z(MLA_NOPE), -sin, z(half), z(LANES - MLA_QK)], axis=1)
    sb = jnp.concatenate([z(MLA_NOPE), z(half), sin, z(LANES - MLA_QK)], axis=1)
    return c, sa, sb


def _local_step(x, mem, pos, target, w, late_shards):
    g = {}
    w = dict(w)
    c, sa, sb = _rope_tables(pos)

    xn = _norm_fwd(x, w["norm_mix"], "norm_mix_fwd")

    def proj_fn(r, rows, k):
        la_ = _gate_fn(r[:, P_ALR:P_ALR + 128], k[0], k[1])
        return [r, la_, _rms(r[:, P_CQ:P_CQ + MLA_Q_RANK], k[2]), _rms(r[:, P_CKV:P_CKV + MLA_KV_RANK], k[3])], []

    proj, la, q_lat, kv_lat = _matmul(
        xn, w["in"], "nn", F32, "proj_fwd", epilogue=_Epilogue(
            proj_fn, [], [w["w2"], w["gate_b"], w["q_a_norm"], w["kv_a_norm"]],
            [(P_WIDTH, F32), (256, F32), (MLA_Q_RANK, BF16), (MLA_KV_RANK, BF16)], []))
    alr = _row(proj, 128, P_ALR // 128)
    kpe = _row(proj, 128, P_KPE // 128)
    og = _row(proj, 512, P_OG // 512)
    cq = _row(proj, 256, P_CQ // 256)
    ckv = _row(proj, 128, P_CKV // 128)

    o_gla, states = _gla_fwd(proj, la)

    q_up = _matmul(q_lat, w["uq"], "nn", F32, "mla_q_fwd")
    k_up = _matmul(kv_lat, w["k"], "nn", F32, "mla_k_fwd")
    v_mla = _matmul(kv_lat, w["v"], "nn", BF16, "mla_v_fwd")

    def qk_body(r, k):
        qs, ks = [], []
        for qh, kh in zip(_heads(r[0], MLA_HEADS), _heads(r[1], MLA_HEADS)):
            a, b = _qk_head(qh, kh, r[2], r[3], r[4], r[5], k[0], k[1])
            qs.append(a)
            ks.append(b)
        return [_cat(qs), _cat(ks)], []

    tabs = [_row(c), _row(sa), _row(sb)]
    q_r, k_r = _rows_call(qk_body, [_row(q_up), _row(k_up), kpe] + tabs, [w["q_norm"], w["k_norm"]],
                          [(1024, BF16), (1024, BF16)], name="mla_qk_fwd")
    o_mla, lse, gathered = _attn_fwd(q_r, k_r, v_mla, _gather_plan(late_shards))
    w.update(_late_layout(dict(zip(LATE, gathered, strict=True))))

    def mix_body(r, k):
        ys = [_mix_head(o, g_, k[0]) for o, g_ in zip(_heads(r[0], GLA_HEADS), _heads(r[1], GLA_HEADS))]
        return [_cat(ys + [r[2]])], []

    cat = _rows_call(mix_body, [_row(o_gla), og, _row(o_mla)], [w["gla_out_norm"]], [(1024, BF16)],
                     name="mix_fwd")[0]
    h1, hn = _matmul(cat, w["out"], "nn", F32, "out_fwd_norm", residual=x, epilogue=_norm_fwd_epilogue(w["norm_xa"]))
    mn = _norm_fwd(mem, w["norm_mem"], "norm_mem_fwd")
    xkv = _matmul(mn, w["xkv"], "nn", F32, "xa_kv_fwd")

    def xa_fn(r, rows, k):
        ks, vs = _heads(k[0], 2 * XA_HEADS)[:XA_HEADS], _heads(k[0], 2 * XA_HEADS)[XA_HEADS:]
        return [r, _cat([_xa_head(a, b, v_, k[1], k[2]) for a, b, v_ in zip(_heads(r, XA_HEADS), ks, vs)])], []

    xq, xo = _matmul(hn, w["xq"], "nn", F32, "xa_q_fwd_attn", epilogue=_Epilogue(
        xa_fn, [], [xkv, w["xa_q_norm"], w["xa_k_norm"]], [(512, F32), (512, BF16)], []))
    h2, fn = _matmul(xo, w["xo"], "nn", F32, "xa_o_fwd_norm", residual=h1, epilogue=_norm_fwd_epilogue(w["norm_ffn"]))
    gg = _matmul(fn, w["wg"], "nt", BF16, "ffn_gate_fwd", b_lead="p")
    uu = _matmul(fn, w["wu"], "nt", BF16, "ffn_up_fwd", b_lead="p")
    act = _conv_fwd(gg, uu, w["cw"], w["cb"])
    def loss_fn(y, rows, consts):
        err = y - rows[0]
        part = 0.5 * jnp.sum(jnp.sum(err * err, axis=1, keepdims=True) * (1.0 / D_MODEL), axis=0, keepdims=True)
        return [err * (1.0 / D_MODEL)], [jnp.broadcast_to(part, (1, LANES))]

    dy, loss = _matmul(act, w["wd"], "nn", F32, "ffn_down_fwd_loss", residual=h2, a_lead="k", b_lead="k",
                       epilogue=_Epilogue(loss_fn, [_row(target)], [], [(D_MODEL, F32)], [(1, LANES)]))

    g["ffn_w_down"] = _matmul(act, dy, "tn", BF16, "ffn_down_dw", a_lead="p")
    dact = _matmul(dy, w["wd"], "nt", BF16, "ffn_down_dx", b_lead="p")
    duu, dgg, g["ffn_conv_w"], g["ffn_conv_b"] = _conv_bwd(gg, uu, dact, w["cw"], w["cb"])
    g["ffn_w_gate"] = _matmul(dgg, fn, "tn", BF16, "ffn_gate_dw", a_lead="p")
    g["ffn_w_up"] = _matmul(duu, fn, "tn", BF16, "ffn_up_dw", a_lead="p")
    dh2, g["norm_ffn"] = _matmul(dgg, w["wg"], "nn", F32, "ffn_dx_norm_bwd", a_lead="k", b_lead="k", more=(duu, w["wu"]),
                                 epilogue=_norm_bwd_epilogue(h2, w["norm_ffn"], dy))

    g["xa_w_o"] = _matmul(xo, dh2, "tn", BF16, "xa_o_dw")
    def xa_bwd(dxo_, rows, k):
        kvh = _heads(k[0], 2 * XA_HEADS)
        dq_, dk_, dv_ = [], [], []
        dqn, dkn = 0.0, 0.0
        for h, (a, d_) in enumerate(zip(_heads(rows[0], XA_HEADS), _heads(dxo_, XA_HEADS))):
            _, vjp = jax.vjp(_xa_head, a, kvh[h], kvh[XA_HEADS + h], k[1], k[2])
            ga, gk, gv, gqn, gkn = vjp(d_)
            dq_.append(ga)
            dk_.append(gk)
            dv_.append(gv)
            dqn, dkn = dqn + gqn, dkn + gkn
        return [_cat(dq_)], [_cat(dk_ + dv_), dqn, dkn]

    dxq, dxkv, g["xa_q_norm"], g["xa_k_norm"] = _matmul(dh2, w["xo"], "nt", F32, "xa_o_dx_attn_bwd", epilogue=_Epilogue(
        xa_bwd, [_row(xq)], [xkv, w["xa_q_norm"], w["xa_k_norm"]], [(512, BF16)], [xkv.shape, (1, 128), (1, 128)]))
    g["xa_w_q"] = _matmul(hn, dxq, "tn", BF16, "xa_q_dw")
    dh1, g["norm_xa"] = _matmul(dxq, w["xq"], "nt", F32, "xa_q_dx_norm_bwd",
                                epilogue=_norm_bwd_epilogue(h1, w["norm_xa"], dh2))
    g["xa_w_kv"] = _matmul(mn, dxkv, "tn", BF16, "xa_kv_dw")
    dmn = _matmul(dxkv, w["xkv"], "nt", F32, "xa_kv_dx")
    _, g["norm_mem"] = _norm_bwd(mem, w["norm_mem"], dmn, dmn, "norm_mem_bwd")

    g["w_out"] = _matmul(cat, dh1, "tn", BF16, "out_dw")
    def mix_bwd(dcat_, rows, k):
        do_, dog_ = [], []
        dgn = 0.0
        for o, g_, d_ in zip(_heads(rows[0], GLA_HEADS), _heads(rows[1], GLA_HEADS), _heads(dcat_, GLA_HEADS)):
            _, vjp = jax.vjp(_mix_head, o, g_, k[0])
            a, b, gn_ = vjp(d_)
            do_.append(a)
            dog_.append(b)
            dgn = dgn + gn_
        return [_cat(do_), _cat(dog_), dcat_[:, 512:]], [dgn]

    do_gla, d_og, do_mla, g["gla_out_norm"] = _matmul(dh1, w["out"], "nt", F32, "out_dx_mix_bwd", epilogue=_Epilogue(
        mix_bwd, [_row(o_gla), og], [w["gla_out_norm"]], [(512, F32), (512, BF16), (512, F32)], [(1, 128)]))

    late_parts = _late_grad_shards(g)
    dq_r, dk_r, dv_mla, lands_mlp = _attn_bwd(q_r, k_r, v_mla, o_mla, lse, do_mla,
                                              _scatter_plan([late_parts[n] for n in LATE_MLP]))

    def qk_bwd(r, k):
        dqs, dks = [], []
        dkpe, dqn, dkn = 0.0, 0.0, 0.0
        for qh, kh, dqh, dkh in zip(_heads(r[0], MLA_HEADS), _heads(r[1], MLA_HEADS), _heads(r[6], MLA_HEADS),
                                    _heads(r[7], MLA_HEADS)):
            _, vjp = jax.vjp(lambda a, b, e, f, h_: _qk_head(a, b, e, r[3], r[4], r[5], f, h_), qh, kh, r[2], k[0], k[1])
            ga, gb, ge, gf, gh = vjp((dqh, dkh))
            dqs.append(ga)
            dks.append(gb)
            dkpe, dqn, dkn = dkpe + ge, dqn + gf, dkn + gh
        return [_cat(dqs), _cat(dks), dkpe], [dqn, dkn]

    dq_up, dk_up, d_kpe, g["q_norm"], g["k_norm"] = _rows_call(
        qk_bwd, [_row(q_up), _row(k_up), kpe] + tabs + [_row(dq_r), _row(dk_r)], [w["q_norm"], w["k_norm"]],
        [(1024, BF16), (1024, BF16), (128, BF16)], [(1, 128), (1, 128)], name="mla_qk_bwd")
    g["uq"] = _matmul(q_lat, dq_up, "tn", BF16, "mla_q_dw")
    dq_lat = _matmul(dq_up, w["uq"], "nt", F32, "mla_q_dx")
    g["k"] = _matmul(kv_lat, dk_up, "tn", BF16, "mla_k_dw")
    g["v"] = _matmul(kv_lat, dv_mla, "tn", BF16, "mla_v_dw")
    dkv_lat = _matmul(dk_up, w["k"], "nt", F32, "mla_k_dx")
    dkv_lat = _matmul(dv_mla, w["v"], "nt", F32, "mla_v_dx", residual=dkv_lat)

    def lat_bwd(r, k):
        _, vjp1 = jax.vjp(_rms, r[0], k[0])
        _, vjp2 = jax.vjp(_rms, r[1], k[1])
        a, ga = vjp1(r[2])
        b, gb = vjp2(r[3])
        return [a, b], [ga, gb]

    d_cq, d_ckv, g["mla_q_a_norm"], g["mla_kv_a_norm"] = _rows_call(
        lat_bwd, [cq, ckv, _row(dq_lat), _row(dkv_lat)], [w["q_a_norm"], w["kv_a_norm"]],
        [(256, BF16), (128, BF16)], [(1, 256), (1, 128)], name="mla_lat_bwd")

    dgq, dgk, dla, dgv, lands_mix = _gla_bwd(proj, la, states, do_gla, _scatter_plan([late_parts[n] for n in LATE_MIX]))
    lands_late = dict(zip(LATE_MLP + LATE_MIX, list(lands_mlp) + list(lands_mix), strict=True))

    def gate_bwd(r, k):
        _, vjp = jax.vjp(_gate_fn, r[0], k[0], k[1])
        a, gw, gb = vjp(r[1])
        return [a], [gw, gb]

    d_alr, g["w2"], g["gla_gate_b"] = _rows_call(gate_bwd, [alr, _row(dla)], [w["w2"], w["gate_b"]], [(128, BF16)],
                                                 [(128, 256), (1, 256)], name="gla_gate_bwd")

    dproj = jnp.concatenate([dgq.astype(BF16), dgk.astype(BF16), dgv.astype(BF16), d_og, d_cq, d_ckv, d_kpe, d_alr],
                            axis=1)
    g["in"] = _matmul(xn, dproj, "tn", BF16, "proj_dw")
    dx, g["norm_mix"] = _matmul(dproj, w["in"], "nt", F32, "proj_dx_norm_bwd",
                                epilogue=_norm_bwd_epilogue(x, w["norm_mix"], dh1))
    return loss[0, 0], dx, g, lands_late


def _join_shards(pieces, axis):
    if axis == 0:
        return pieces.reshape(-1, pieces.shape[2])
    return jnp.transpose(pieces, (1, 0, 2)).reshape(pieces.shape[1], -1)


def _split_shards(full, axis):
    r, c = full.shape
    if axis == 0:
        return full.reshape(4, r // 4, c)
    return jnp.transpose(full.reshape(r, 4, c // 4), (1, 0, 2))


def _early_layout(gath, rep):
    w_in = _join_shards(gath["w_in"], 1)
    z = lambda n: jnp.zeros((D_MODEL, n), w_in.dtype)
    seg = lambda lo, n: w_in[:, lo:lo + n]
    ukv = _join_shards(gath["mla_w_ukv"], 1).reshape(MLA_KV_RANK, MLA_HEADS, MLA_NOPE + MLA_V)
    w = {
        "in": jnp.concatenate([seg(N_GQ, 256), seg(N_GK, 256), seg(N_GV, 512), seg(N_OG, 512), seg(N_CQ, 256),
                               seg(N_CKV, 128), z(64), seg(N_KPE, 32), z(32), seg(N_ALR, 16), z(112)], axis=1),
        "uq": jnp.pad(_join_shards(gath["mla_w_uq"], 1).reshape(MLA_Q_RANK, MLA_HEADS, MLA_QK),
                      ((0, 0), (0, 0), (0, LANES - MLA_QK))).reshape(MLA_Q_RANK, MLA_HEADS * LANES),
        "k": jnp.pad(ukv[:, :, :MLA_NOPE], ((0, 0), (0, 0), (0, LANES - MLA_NOPE))).reshape(MLA_KV_RANK, -1),
        "v": ukv[:, :, MLA_NOPE:].reshape(MLA_KV_RANK, MLA_HEADS * MLA_V),
        "w2": jnp.pad(_join_shards(gath["gla_gate_w2"], 1), ((0, LANES - GLA_RANK), (0, 0))),
        "cb": rep["ffn_conv_b"].reshape(4, 1, D_FF // 4),
        "q_norm": jnp.pad(rep["mla_q_norm"], ((0, 0), (0, LANES - MLA_QK))),
        "k_norm": jnp.pad(rep["mla_k_norm"], ((0, 0), (0, LANES - MLA_QK))),
        "q_a_norm": rep["mla_q_a_norm"], "kv_a_norm": rep["mla_kv_a_norm"], "gate_b": rep["gla_gate_b"],
    }
    for n in ("norm_mix", "gla_out_norm", "norm_xa", "norm_mem", "xa_q_norm", "xa_k_norm", "norm_ffn"):
        w[n] = rep[n]
    return w


def _late_layout(gath):
    return {"out": _join_shards(gath["w_out"], 0), "xq": _join_shards(gath["xa_w_q"], 0),
            "xkv": _join_shards(gath["xa_w_kv"], 0), "xo": _join_shards(gath["xa_w_o"], 1),
            "wg": gath["ffn_w_gate"], "wu": gath["ffn_w_up"], "wd": gath["ffn_w_down"], "cw": gath["ffn_conv_w"]}


def _late_grad_shards(g):
    sh = {"w_out": _split_shards(g["w_out"], 0), "xa_w_q": _split_shards(g["xa_w_q"], 0),
          "xa_w_kv": _split_shards(g["xa_w_kv"], 0), "xa_w_o": _split_shards(g["xa_w_o"], 1),
          "ffn_w_gate": g["ffn_w_gate"], "ffn_w_up": g["ffn_w_up"], "ffn_conv_w": g["ffn_conv_w"],
          "ffn_w_down": g["ffn_w_down"]}
    return {n: v.astype(BF16) for n, v in sh.items()}


def _early_grad_shards(g):
    gi = g["in"]
    seg = lambda lo, n: gi[:, lo:lo + n]
    w_in = jnp.concatenate([seg(P_GQ, 256), seg(P_GK, 256), seg(P_GV, 512), seg(P_ALR, 16), seg(P_OG, 512),
                            seg(P_CQ, 256), seg(P_CKV, 128), seg(P_KPE + 64, 32)], axis=1)
    uq = g["uq"].reshape(MLA_Q_RANK, MLA_HEADS, LANES)[:, :, :MLA_QK].reshape(MLA_Q_RANK, -1)
    ukv = jnp.concatenate([g["k"].reshape(MLA_KV_RANK, MLA_HEADS, LANES)[:, :, :MLA_NOPE],
                           g["v"].reshape(MLA_KV_RANK, MLA_HEADS, MLA_V)], axis=2).reshape(MLA_KV_RANK, -1)
    sh = {"w_in": _split_shards(w_in, 1), "gla_gate_w2": _split_shards(g["w2"][:GLA_RANK], 1),
          "mla_w_uq": _split_shards(uq, 1), "mla_w_ukv": _split_shards(ukv, 1)}
    sh = {n: v.astype(BF16) for n, v in sh.items()}
    rep = {n: g[n] for n in REPLICATED if n in g}
    rep["mla_q_norm"] = g["q_norm"][:, :MLA_QK]
    rep["mla_k_norm"] = g["k_norm"][:, :MLA_QK]
    rep["ffn_conv_b"] = g["ffn_conv_b"].reshape(1, D_FF)
    return sh, rep


SMALL_SHAPE = (8, 1024)


def _pack_small(vectors):
    flat = jnp.concatenate(vectors, axis=1)
    return jnp.pad(flat, ((0, 0), (0, SMALL_SHAPE[0] * SMALL_SHAPE[1] - flat.shape[1]))).reshape(SMALL_SHAPE)


def _unpack_small(buf, widths):
    flat = buf.reshape(1, -1)
    out, off = [], 0
    for wd in widths:
        out.append(flat[:, off:off + wd])
        off += wd
    return out


ANY = pl.BlockSpec(memory_space=pl.ANY)


def _place():
    x, y, c = lax.axis_index("x"), lax.axis_index("y"), lax.axis_index("c")
    chips = [(1 - x, y), (x, 1 - y), (1 - x, 1 - y)]
    return x, y, c, chips


class _Comm:
    def __init__(self, ins, out_shape, sems, start, finish, mid=None):
        self.ins, self.out_shape, self.sems = list(ins), list(out_shape), list(sems)
        self.start, self.finish, self.mid = start, finish, mid or (lambda *args: None)


def _run_comm(plan, name):
    ni, no = len(plan.ins), len(plan.out_shape)

    def body(*refs):
        ins, outs, sems = refs[:ni], refs[ni:ni + no], refs[ni + no:]
        place = _place()
        plan.start(place, ins, outs, sems)
        plan.mid(place, ins, outs, sems)
        plan.finish(place, ins, outs, sems)

    return pl.pallas_call(body, in_specs=[ANY] * ni, out_specs=[ANY] * no, out_shape=plan.out_shape,
                          scratch_shapes=plan.sems, name=name)(*plan.ins)


def _gather_plan(shards):
    n = len(shards)
    split = [s.shape[0] % (2 * BF16_ROWS) == 0 for s in shards]

    def rows(ref, t, c):
        if not split[t]:
            return ref
        half = shards[t].shape[0] // 2
        return ref.at[pl.ds(pl.multiple_of(c * half, BF16_ROWS), half)]

    def remote(src, dst, ss, rs, to):
        return pltpu.make_async_remote_copy(src_ref=src, dst_ref=dst, send_sem=ss, recv_sem=rs, device_id=to,
                                            device_id_type=MESH)

    def first_wave(place, ins, outs, sems):
        x, y, c, chips = place
        ici_s, ici_r, _, _, local = sems
        me = 2 * x + y
        own = [pltpu.make_async_copy(ins[t], outs[t].at[me], local.at[t]) for t in range(n)]
        push = [remote(rows(ins[t], t, c), rows(outs[t].at[me], t, c), ici_s.at[3 * t + j], ici_r.at[3 * t + j], (px, py, c))
                for t in range(n) for j, (px, py) in enumerate(chips)]
        return own, push

    def second_wave(place, ins, outs, sems, last):
        x, y, c, chips = place
        ici_s, ici_r, d2d_s, d2d_r, local = sems
        sib = (x, y, 1 - c)
        out = []
        for t in range(n):
            for j, (px, py) in enumerate(chips):
                block = outs[t].at[2 * px + py]
                got = rows(block, t, c)
                if split[t]:
                    hand = remote(got, got, d2d_s.at[3 * t + j], d2d_r.at[3 * t + j], sib)
                    theirs = rows(block, t, 1 - c)
                    other = (remote(theirs, theirs, local.at[0], d2d_r.at[3 * t + j], sib) if last else
                             remote(got, got, local.at[0], ici_r.at[3 * t + j], sib))
                    out.append((other, hand))
                elif last:
                    out.append((remote(got, got, local.at[0], ici_r.at[3 * t + j], sib), None))
        return out

    def start(place, ins, outs, sems):
        own, push = first_wave(place, ins, outs, sems)
        for cp in own + push:
            cp.start()

    def mid(place, ins, outs, sems):
        for arrival, hand in second_wave(place, ins, outs, sems, False):
            arrival.wait_recv()
            hand.start()

    def finish(place, ins, outs, sems):
        own, push = first_wave(place, ins, outs, sems)
        for arrival, hand in second_wave(place, ins, outs, sems, True):
            arrival.wait_recv()
            if hand is not None:
                hand.wait_send()
        for cp in push:
            cp.wait_send()
        for cp in own:
            cp.wait()

    dma = pltpu.SemaphoreType.DMA
    return _Comm(shards, [jax.ShapeDtypeStruct((4,) + s.shape, s.dtype) for s in shards],
                 [dma((3 * n,)), dma((3 * n,)), dma((3 * n,)), dma((3 * n,)), dma((n,))], start, finish, mid)


def _scatter_plan(parts, small=None):
    n = len(parts)
    ns = 0 if small is None else 1

    def unpack(place, ins, outs, sems):
        x, y, c, chips = place
        return x, y, c, chips, 2 * x + y, 4 * x + 2 * y + c, (x, y, 1 - c)

    def remote(src, dst, ss, rs, to):
        return pltpu.make_async_remote_copy(src_ref=src, dst_ref=dst, send_sem=ss, recv_sem=rs, device_id=to,
                                            device_id_type=MESH)

    def first_wave(place, ins, outs, sems):
        x, y, c, chips, me, dev, sib = unpack(place, ins, outs, sems)
        ici_s, ici_r, d2d_s, d2d_r, sm_s, sm_r, local = sems
        own, push = [], []
        if ns:
            own.append(pltpu.make_async_copy(ins[n], outs[n].at[dev], local.at[n]))
            for k in range(1, 8):
                px = (1 - x) if (k >> 2) & 1 else x
                py = (1 - y) if (k >> 1) & 1 else y
                pc = (1 - c) if k & 1 else c
                push.append(remote(ins[n], outs[n].at[dev], sm_s.at[k - 1], sm_r.at[k - 1], (px, py, pc)))
        for t in range(n):
            own.append(pltpu.make_async_copy(ins[t].at[me], outs[t].at[dev], local.at[t]))
            push.append(remote(ins[t].at[me], outs[t].at[dev], d2d_s.at[4 * t], d2d_r.at[4 * t], sib))
            for j, (px, py) in enumerate(chips):
                push.append(remote(ins[t].at[2 * px + py], outs[t].at[dev], ici_s.at[3 * t + j], ici_r.at[3 * t + j],
                                   (px, py, c)))
        return own, push

    def start(place, ins, outs, sems):
        own, push = first_wave(place, ins, outs, sems)
        for cp in own + push:
            cp.start()

    def landed(dst, rs, sems, sib):
        remote(dst, dst, sems[-1].at[0], rs, sib).wait_recv()

    def forwards(place, ins, outs, sems):
        x, y, c, chips, me, dev, sib = unpack(place, ins, outs, sems)
        d2d_s, d2d_r = sems[2], sems[3]
        slots = [(t, j, outs[t].at[4 * px + 2 * py + c]) for t in range(n) for j, (px, py) in enumerate(chips)]
        return [(t, j, slot, remote(slot, slot, d2d_s.at[4 * t + 1 + j], d2d_r.at[4 * t + 1 + j], sib))
                for t, j, slot in slots]

    def mid(place, ins, outs, sems):
        sib = unpack(place, ins, outs, sems)[-1]
        for t, j, slot, cp in forwards(place, ins, outs, sems):
            landed(slot, sems[1].at[3 * t + j], sems, sib)
            cp.start()

    def finish(place, ins, outs, sems):
        x, y, c, chips, me, dev, sib = unpack(place, ins, outs, sems)
        d2d_r, sm_r = sems[3], sems[5]
        own, push = first_wave(place, ins, outs, sems)
        push += [cp for _, _, _, cp in forwards(place, ins, outs, sems)]
        for t in range(n):
            landed(outs[t].at[4 * x + 2 * y + (1 - c)], d2d_r.at[4 * t], sems, sib)
            for j, (px, py) in enumerate(chips):
                landed(outs[t].at[4 * px + 2 * py + (1 - c)], d2d_r.at[4 * t + 1 + j], sems, sib)
        if ns:
            for k in range(1, 8):
                px = (1 - x) if (k >> 2) & 1 else x
                py = (1 - y) if (k >> 1) & 1 else y
                pc = (1 - c) if k & 1 else c
                landed(outs[n].at[4 * px + 2 * py + pc], sm_r.at[k - 1], sems, sib)
        for cp in push:
            cp.wait_send()
        for cp in own:
            cp.wait()

    dma = pltpu.SemaphoreType.DMA
    ins = list(parts) + ([small] if ns else [])
    out_shape = [jax.ShapeDtypeStruct((8,) + p.shape[1:], p.dtype) for p in parts]
    if ns:
        out_shape.append(jax.ShapeDtypeStruct((8,) + small.shape, small.dtype))
    return _Comm(ins, out_shape, [dma((3 * n,)), dma((3 * n,)), dma((4 * n,)), dma((4 * n,)), dma((7,)), dma((7,)),
                                  dma((n + 1,))], start, finish, mid)


ADAM_ROWS = 288


def _row_tile(r, cap):
    if r <= cap:
        return r
    return max(t for t in range(8, cap + 1, 8) if r % t == 0)


def _adamw_update(w, m, v, land):
    g = land[0].astype(F32)
    for i in range(1, 8):
        g = g + land[i].astype(F32)
    m_new = ADAM_B1 * m + (1.0 - ADAM_B1) * g
    v_new = ADAM_B2 * v + (1.0 - ADAM_B2) * (g * g)
    m_hat = m_new / (1.0 - ADAM_B1 ** ADAM_STEP)
    v_hat = v_new / (1.0 - ADAM_B2 ** ADAM_STEP)
    return g, -ADAM_LR * (m_hat / (jnp.sqrt(v_hat) + ADAM_EPS) + ADAM_WD * w), m_new, v_new


def _adamw(tensors, name, comm=None):
    k = len(tensors)
    r, c = tensors[0][0].shape
    t = _row_tile(r, ADAM_ROWS // k)
    n = r // t
    nci, nco, nsem = (len(comm.ins), len(comm.out_shape), len(comm.sems)) if comm else (0, 0, 0)

    def kern(*refs):
        ins, cins, outs, couts, csems = _split_refs(refs, (4 * k, nci, 4 * k, nco, nsem))
        if comm:
            place = _place()

            @pl.when(pl.program_id(0) == 0)
            def _():
                comm.start(place, cins, couts, csems)

        for i in range(k):
            w_ref, m_ref, v_ref, l_ref = ins[4 * i:4 * i + 4]
            res = _adamw_update(w_ref[...], m_ref[...], v_ref[...], l_ref)
            for ref, val in zip(outs[4 * i:4 * i + 4], res, strict=True):
                ref[...] = val
        if comm:
            @pl.when(pl.program_id(0) == n - 1)
            def _():
                comm.mid(place, cins, couts, csems)
                comm.finish(place, cins, couts, csems)

    spec = pl.BlockSpec((t, c), lambda i: (i, 0))
    lspec = pl.BlockSpec((8, t, c), lambda i: (0, i, 0))
    res = pl.pallas_call(
        kern, grid=(n,), in_specs=[spec, spec, spec, lspec] * k + [ANY] * nci, out_specs=[spec] * (4 * k) + [ANY] * nco,
        out_shape=[jax.ShapeDtypeStruct((r, c), F32)] * (4 * k) + (comm.out_shape if comm else []),
        scratch_shapes=comm.sems if comm else [],
        compiler_params=pltpu.CompilerParams(dimension_semantics=("arbitrary" if comm else "parallel",),
                                             vmem_limit_bytes=VMEM_LIMIT),
        name=name)(*[x for tens in tensors for x in tens], *(comm.ins if comm else []))
    return [res[4 * i:4 * i + 4] for i in range(k)], res[4 * k:]


def _step(a):
    def sq(n):
        v = a[n][0] if a[n].ndim == 3 else a[n]
        return v.T if n.removeprefix("m_").removeprefix("v_") in TRANSPOSED else v

    payload = lambda n: sq(n) if n in EXACT_GATHER else sq(n).astype(BF16)

    gathered = _run_comm(_gather_plan([payload(n) for n in EARLY]), "gather_early")
    w = _early_layout(dict(zip(EARLY, gathered, strict=True)), {n: a[n] for n in REPLICATED})

    loss, dx, g, lands_late = _local_step(sq("x"), sq("mem"), a["positions"][0], sq("loss_target"), w,
                                          [payload(n) for n in LATE])

    sh, rep = _early_grad_shards(g)
    small = _pack_small([rep[n] for n in REPLICATED] + [loss.reshape(1, 1)])
    *lands_early, land_small = _run_comm(_scatter_plan([sh[n] for n in EARLY], small), "scatter_last")
    quad = lambda n, land: (sq(n), sq("m_" + n), sq("v_" + n), land)
    lands = dict(zip(EARLY, lands_early, strict=True)) | lands_late

    outs = {}
    kinds = ("grad_", "delta_", "new_m_", "new_v_")
    for n, _ in SHARDED:
        res = _adamw([quad(n, lands[n])], "adamw_" + n)[0][0]
        for kind, val in zip(kinds, res, strict=True):
            outs[kind + n] = (val.T if n in TRANSPOSED else val).reshape(a[n].shape)
    zero = jnp.zeros((1, 1), F32)
    packed = [_pack_small([a[p + n] for n in REPLICATED] + [zero]) for p in ("", "m_", "v_")]
    res = _adamw([(*packed, land_small)], "adamw_replicated")[0][0]
    widths = [a[n].shape[1] for n in REPLICATED] + [1]
    for kind, buf in zip(kinds, res, strict=True):
        *vals, total = _unpack_small(buf, widths)
        for n, val in zip(REPLICATED, vals, strict=True):
            outs[kind + n] = val
        if kind == "grad_":
            loss = total[0, 0]

    ordered = [outs[kind + n] for kind in kinds for n in WEIGHTS]
    return (loss, dx[None], *ordered)


def kernel(x, mem, positions, norm_mix, w_in, gla_gate_w2, gla_gate_b, gla_out_norm, mla_q_a_norm, mla_w_uq, mla_kv_a_norm, mla_w_ukv, mla_q_norm, mla_k_norm, w_out, norm_xa, norm_mem, xa_w_q, xa_w_kv, xa_q_norm, xa_k_norm, xa_w_o, norm_ffn, ffn_w_gate, ffn_w_up, ffn_conv_w, ffn_conv_b, ffn_w_down, loss_target, m_norm_mix, m_w_in, m_gla_gate_w2, m_gla_gate_b, m_gla_out_norm, m_mla_q_a_norm, m_mla_w_uq, m_mla_kv_a_norm, m_mla_w_ukv, m_mla_q_norm, m_mla_k_norm, m_w_out, m_norm_xa, m_norm_mem, m_xa_w_q, m_xa_w_kv, m_xa_q_norm, m_xa_k_norm, m_xa_w_o, m_norm_ffn, m_ffn_w_gate, m_ffn_w_up, m_ffn_conv_w, m_ffn_conv_b, m_ffn_w_down, v_norm_mix, v_w_in, v_gla_gate_w2, v_gla_gate_b, v_gla_out_norm, v_mla_q_a_norm, v_mla_w_uq, v_mla_kv_a_norm, v_mla_w_ukv, v_mla_q_norm, v_mla_k_norm, v_w_out, v_norm_xa, v_norm_mem, v_xa_w_q, v_xa_w_kv, v_xa_q_norm, v_xa_k_norm, v_xa_w_o, v_norm_ffn, v_ffn_w_gate, v_ffn_w_up, v_ffn_conv_w, v_ffn_conv_b, v_ffn_w_down):
    return _step(dict(locals()))
```

```python
import functools

import jax
import jax.numpy as jnp
from jax import lax
from jax.experimental import pallas as pl
from jax.experimental.pallas import tpu as pltpu

F32, BF16 = jnp.float32, jnp.bfloat16
MESH = pl.DeviceIdType.MESH

D_MODEL = 1024
EPS = 1e-6
GLA_HEADS, GLA_DK, GLA_DV, GLA_RANK, GLA_CHUNK = 4, 64, 128, 16, 64
GLA_GATE_NORM = 16.0
MLA_HEADS, MLA_Q_RANK, MLA_KV_RANK, MLA_NOPE, MLA_ROPE, MLA_V = 8, 256, 128, 64, 32, 64
MLA_QK = MLA_NOPE + MLA_ROPE
ROPE_THETA = 10000.0
LOG2E, LN2 = 1.4426950408889634, 0.6931471805599453
XA_HEADS, XA_DIM = 4, 128
D_FF = 2816
ADAM_LR, ADAM_B1, ADAM_B2, ADAM_EPS, ADAM_WD, ADAM_STEP = 0.001, 0.9, 0.999, 1e-08, 0.01, 10

LANES = 128
BF16_ROWS = 16
VMEM_LIMIT = 56 * 1024 * 1024
MATMUL_VMEM = 44 * 1024 * 1024

P_GQ, P_GK, P_GV, P_OG, P_CQ, P_CKV, P_KPE, P_ALR, P_WIDTH = 0, 256, 512, 1024, 1536, 1792, 1920, 2048, 2176
N_GQ, N_GK, N_GV, N_ALR, N_OG, N_CQ, N_CKV, N_KPE, N_WIDTH = 0, 256, 512, 1024, 1040, 1552, 1808, 1936, 1968

SHARDED = (("w_in", 1), ("gla_gate_w2", 1), ("mla_w_uq", 1), ("mla_w_ukv", 1), ("w_out", 0), ("xa_w_q", 0),
           ("xa_w_kv", 0), ("xa_w_o", 1), ("ffn_w_gate", 1), ("ffn_w_up", 1), ("ffn_conv_w", 1), ("ffn_w_down", 0))
REPLICATED = ("norm_mix", "gla_gate_b", "gla_out_norm", "mla_q_a_norm", "mla_kv_a_norm", "mla_q_norm", "mla_k_norm",
              "norm_xa", "norm_mem", "xa_q_norm", "xa_k_norm", "norm_ffn", "ffn_conv_b")
EXACT_GATHER = ("gla_gate_w2", "ffn_conv_w")
TRANSPOSED = ("ffn_w_gate", "ffn_w_up")
EARLY = ("w_in", "gla_gate_w2", "mla_w_uq", "mla_w_ukv")
LATE = tuple(n for n, _ in SHARDED if n not in EARLY)
LATE_MLP = tuple(n for n in LATE if n.startswith("ffn_"))
LATE_MIX = tuple(n for n in LATE if not n.startswith("ffn_"))
WEIGHTS = ("norm_mix", "w_in", "gla_gate_w2", "gla_gate_b", "gla_out_norm", "mla_q_a_norm", "mla_w_uq",
           "mla_kv_a_norm", "mla_w_ukv", "mla_q_norm", "mla_k_norm", "w_out", "norm_xa", "norm_mem", "xa_w_q",
           "xa_w_kv", "xa_q_norm", "xa_k_norm", "xa_w_o", "norm_ffn", "ffn_w_gate", "ffn_w_up", "ffn_conv_w",
           "ffn_conv_b", "ffn_w_down")


_NN = ((1,), (0,))
_NT = ((1,), (1,))
_TN = ((0,), (0,))


def _dg(a, b, dims):
    return lax.dot_general(a.astype(BF16), b.astype(BF16), (dims, ((), ())), preferred_element_type=F32)


@jax.custom_vjp
def _dot_nn(a, b):
    return _dg(a, b, _NN)


_dot_nn.defvjp(lambda a, b: (_dg(a, b, _NN), (a, b)),
               lambda r, g: (_dg(g, r[1], _NT).astype(r[0].dtype), _dg(r[0], g, _TN).astype(r[1].dtype)))


@jax.custom_vjp
def _dot_nt(a, b):
    return _dg(a, b, _NT)


_dot_nt.defvjp(lambda a, b: (_dg(a, b, _NT), (a, b)),
               lambda r, g: (_dg(g, r[1], _NN).astype(r[0].dtype), _dg(g, r[0], _TN).astype(r[1].dtype)))


@jax.custom_vjp
def _dot_tn(a, b):
    return _dg(a, b, _TN)


_dot_tn.defvjp(lambda a, b: (_dg(a, b, _TN), (a, b)),
               lambda r, g: (_dg(r[1], g, _NT).astype(r[0].dtype), _dg(r[0], g, _NN).astype(r[1].dtype)))


def _rms(x, w, n=None):
    n = x.shape[-1] if n is None else n
    ms = jnp.sum(x * x, axis=-1, keepdims=True) * (1.0 / n)
    return x * lax.rsqrt(ms + EPS) * w


def _silu(x):
    return x * jax.nn.sigmoid(x)


def _log_sigmoid(x):
    return jnp.minimum(x, 0.0) - jnp.log(1.0 + jnp.exp(-jnp.abs(x)))


@jax.custom_vjp
def _rope(y, c, sa, sb):
    return y * c + pltpu.roll(y, LANES - 16, 1) * sa + pltpu.roll(y, 16, 1) * sb


def _rope_bwd(res, g):
    c, sa, sb = res
    gy = g * c + pltpu.roll(g * sa, 16, 1) + pltpu.roll(g * sb, LANES - 16, 1)
    return gy, jnp.zeros_like(c), jnp.zeros_like(sa), jnp.zeros_like(sb)


_rope.defvjp(lambda y, c, sa, sb: (_rope(y, c, sa, sb), (c, sa, sb)), _rope_bwd)


@jax.custom_vjp
def _cumsum_rows(x):
    n = x.shape[0]
    row = lax.broadcasted_iota(jnp.int32, x.shape, 0)
    k = 1
    while k < n:
        x = x + jnp.where(row >= k, pltpu.roll(x, k, 0), 0.0)
        k *= 2
    return x


def _cumsum_rows_bwd(_, g):
    n = g.shape[0]
    row = lax.broadcasted_iota(jnp.int32, g.shape, 0)
    k = 1
    while k < n:
        g = g + jnp.where(row < n - k, pltpu.roll(g, n - k, 0), 0.0)
        k *= 2
    return (g,)


_cumsum_rows.defvjp(lambda x: (_cumsum_rows(x), None), _cumsum_rows_bwd)


def _lane_mask(lo, hi):
    lane = lax.broadcasted_iota(jnp.int32, (1, LANES), 1)
    return ((lane >= lo) & (lane < hi)).astype(F32)


def _tile(n, t):
    t = min(n, t)
    assert n % t == 0, (n, t)
    return t


class _Epilogue:
    def __init__(self, fn, rows=(), consts=(), outs=(), accs=()):
        self.fn, self.rows, self.consts, self.outs, self.accs = fn, list(rows), list(consts), list(outs), list(accs)


def _matmul(a, b, mode, out_dtype, name, residual=None, a_lead=None, b_lead=None, more=None, epilogue=None):
    (a0, a1), (b0, b1) = a.shape[-2:], b.shape[-2:]
    if mode == "nn":
        m, k, k2, n = a0, a1, b0, b1
    elif mode == "nt":
        m, k, n, k2 = a0, a1, b0, b1
    else:
        k, m, k2, n = a0, a1, b0, b1
    assert k == k2, (a.shape, b.shape, mode)
    npar = 4 if "p" in (a_lead, b_lead) else 1
    nsum = 4 if "k" in (a_lead, b_lead) else 1
    pairs = [(a, b)] + ([more] if more else [])
    a_item, b_item, o_item = a.dtype.itemsize, b.dtype.itemsize, jnp.dtype(out_dtype).itemsize
    ep = epilogue
    row_extra = 4 if residual is not None else 0
    if ep:
        row_extra += (sum(r.dtype.itemsize * wd for r, wd, _ in ep.rows) + sum(jnp.dtype(d).itemsize * wd for wd, d in ep.outs)) / n

    def vmem_need(tm, tn, tk):
        need = 2 * (nsum if a_lead == "k" else 1) * tm * tk * a_item + 2 * (nsum if b_lead == "k" else 1) * tk * tn * b_item
        need *= len(pairs)
        need += (0 if ep else 2 * tm * tn * o_item) + tm * tn * 4 * (2 if tk < k else 1)
        need += tm * tk * 2 * (a_item == 4 or mode == "tn") + tk * tn * 2 * (b_item == 4)
        return need + int(2 * tm * tn * row_extra) + (3 * tm * tn * 4 if ep else 0)

    halvings = (4096, 2048, 1024, 512, 256, 128, 64, 32, 16, 8)
    if mode == "tn":
        tm = m if m <= 1408 else m // 2
        tn = n if tm * n <= 1024 * 2304 else n // 2
        tk = next((r for r in halvings if k % r == 0 and vmem_need(tm, tn, r) <= MATMUL_VMEM), k)
    else:
        tn, tk = n, k
        tm = next((r for r in halvings if m % r == 0 and vmem_need(r, tn, tk) <= MATMUL_VMEM), m)
    assert m % tm == 0 and n % tn == 0 and k % tk == 0
    assert ep is None or (tn == n and tk == k and npar == 1)
    nk = k // tk
    dims = {"nn": _NN, "nt": _NT, "tn": _TN}[mode]
    n_in = 2 * len(pairs) + (residual is not None)
    n_ep_in = len(ep.rows) + len(ep.consts) if ep else 0
    n_out = len(ep.outs) + len(ep.accs) if ep else 1

    def body(*refs):
        ab, rs, ep_in, outs, scratch = _split_refs(refs, (2 * len(pairs), n_in - 2 * len(pairs), n_ep_in, n_out, nk > 1))
        prod = None
        for a_ref, b_ref in zip(ab[0::2], ab[1::2]):
            for sh in range(nsum):
                term = _dg(a_ref[sh] if a_lead == "k" else a_ref[...], b_ref[sh] if b_lead == "k" else b_ref[...], dims)
                prod = term if prod is None else prod + term

        def finish(r):
            if rs:
                r = r + rs[0][...]
            if ep is None:
                outs[0][...] = r.astype(outs[0].dtype)
                return
            vals = [x[...] for x in ep_in]
            ro, ao = ep.fn(r, vals[:len(ep.rows)], vals[len(ep.rows):])
            for ref, val in zip(outs[:len(ep.outs)], ro, strict=True):
                ref[...] = val.astype(ref.dtype)
            if ep.accs:
                @pl.when(pl.program_id(0) == 0)
                def _():
                    for ref in outs[len(ep.outs):]:
                        ref[...] = jnp.zeros_like(ref)

                for ref, val in zip(outs[len(ep.outs):], ao, strict=True):
                    ref[...] += val

        if nk == 1:
            finish(prod)
            return
        acc = scratch[0]
        kk = pl.program_id(3)

        @pl.when(kk == 0)
        def _():
            acc[...] = prod

        @pl.when(kk > 0)
        def _():
            acc[...] += prod

        @pl.when(kk == nk - 1)
        def _():
            finish(acc[...])

    def spec(lead, blk, idx):
        if lead is None:
            return pl.BlockSpec(blk, lambda i, j, p, kk: idx(i, j, kk))
        if lead == "p":
            return pl.BlockSpec((None,) + blk, lambda i, j, p, kk: (p,) + idx(i, j, kk))
        return pl.BlockSpec((nsum,) + blk, lambda i, j, p, kk: (0,) + idx(i, j, kk))

    if mode == "nn":
        pair_specs = [spec(a_lead, (tm, tk), lambda i, j, kk: (i, kk)), spec(b_lead, (tk, tn), lambda i, j, kk: (kk, j))]
    elif mode == "nt":
        pair_specs = [spec(a_lead, (tm, tk), lambda i, j, kk: (i, kk)), spec(b_lead, (tn, tk), lambda i, j, kk: (j, kk))]
    else:
        pair_specs = [spec(a_lead, (tk, tm), lambda i, j, kk: (kk, i)), spec(b_lead, (tk, tn), lambda i, j, kk: (kk, j))]
    tile = spec(None, (tm, tn), lambda i, j, kk: (i, j))
    in_specs = pair_specs * len(pairs)
    args = [x for pair in pairs for x in pair]
    if residual is not None:
        assert npar == 1
        in_specs.append(tile)
        args.append(residual)
    if ep:
        in_specs += [pl.BlockSpec((tm, wd), functools.partial(lambda cb, i, j, p, kk: (i, cb), cb)) for _, wd, cb in ep.rows]
        in_specs += [pl.BlockSpec(c.shape, lambda i, j, p, kk: (0, 0)) for c in ep.consts]
        args += [r for r, _, _ in ep.rows] + ep.consts
        out_specs = [pl.BlockSpec((tm, wd), lambda i, j, p, kk: (i, 0)) for wd, _ in ep.outs]
        out_specs += [pl.BlockSpec(shape, lambda i, j, p, kk: (0, 0)) for shape in ep.accs]
        out_shape = [jax.ShapeDtypeStruct((m, wd), d) for wd, d in ep.outs] + [jax.ShapeDtypeStruct(sh, F32) for sh in ep.accs]
    else:
        out_specs = spec("p" if npar > 1 else None, (tm, tn), lambda i, j, kk: (i, j))
        out_shape = jax.ShapeDtypeStruct(((4,) if npar > 1 else ()) + (m, n), out_dtype)
    outer = "arbitrary" if ep and ep.accs else "parallel"
    return pl.pallas_call(
        body, grid=(m // tm, n // tn, npar, nk), in_specs=in_specs, out_specs=out_specs, out_shape=out_shape,
        scratch_shapes=[pltpu.VMEM((tm, tn), F32)] if nk > 1 else [],
        compiler_params=pltpu.CompilerParams(dimension_semantics=(outer, outer, outer, "arbitrary"),
                                             vmem_limit_bytes=VMEM_LIMIT),
        name=name)(*args)


def _row(a, width=None, col_block=0):
    return (a, a.shape[1] if width is None else width, col_block)


def _rows_call(body, rows, consts, outs, accs=(), *, name, tile=512):
    s = rows[0][0].shape[0]
    t = _tile(s, tile)
    nr, nc, no = len(rows), len(consts), len(outs)

    def kern(*refs):
        r = [x[...] for x in refs[:nr]]
        c = [x[...] for x in refs[nr:nr + nc]]
        o_refs = refs[nr + nc:nr + nc + no]
        a_refs = refs[nr + nc + no:]
        ro, ao = body(r, c)
        for ref, val in zip(o_refs, ro, strict=True):
            ref[...] = val.astype(ref.dtype)
        if a_refs:
            @pl.when(pl.program_id(0) == 0)
            def _():
                for ref in a_refs:
                    ref[...] = jnp.zeros_like(ref)

            for ref, val in zip(a_refs, ao, strict=True):
                ref[...] += val

    in_specs = [pl.BlockSpec((t, w), functools.partial(lambda cb, i: (i, cb), cb)) for (_, w, cb) in rows]
    in_specs += [pl.BlockSpec(c.shape, lambda i: (0, 0)) for c in consts]
    out_specs = [pl.BlockSpec((t, w), lambda i: (i, 0)) for (w, _) in outs]
    out_specs += [pl.BlockSpec(shape, lambda i: (0, 0)) for shape in accs]
    out_shape = [jax.ShapeDtypeStruct((s, w), dt) for (w, dt) in outs]
    out_shape += [jax.ShapeDtypeStruct(shape, F32) for shape in accs]
    return pl.pallas_call(
        kern, grid=(s // t,), in_specs=in_specs, out_specs=out_specs, out_shape=out_shape,
        compiler_params=pltpu.CompilerParams(dimension_semantics=("arbitrary" if accs else "parallel",),
                                             vmem_limit_bytes=VMEM_LIMIT),
        name=name)(*[r[0] for r in rows], *consts)


def _gla_chunk(q, k, la, v0, v1, s0, s1):
    c = q.shape[0]
    r = lax.broadcasted_iota(jnp.int32, (c, c), 0)
    cc = lax.broadcasted_iota(jnp.int32, (c, c), 1)
    tril = cc <= r
    cum = _cumsum_rows(la)
    cl = jnp.sum(la, axis=0, keepdims=True)
    qd = q * (GLA_DK ** -0.5) * jnp.exp(cum)
    ki = k * jnp.exp(-cum)
    ke = k * jnp.exp(cl - cum)
    dec = jnp.exp(cl)
    outs, news = [], []
    for h, (v, s) in enumerate(((v0, s0), (v1, s1))):
        mk = _lane_mask(GLA_DK * h, GLA_DK * (h + 1))
        qh = qd * mk
        att = jnp.where(tril, _dot_nt(qh, ki), 0.0)
        outs.append(_dot_nn(att, v) + _dot_nt(qh, s))
        news.append(s * dec + _dot_tn(v, ke * mk))
    return outs[0], outs[1], news[0], news[1]


def _gla_specs(tb, rev_nb=None):
    blk = (lambda b: b) if rev_nb is None else (lambda b: rev_nb - 1 - b)
    q = pl.BlockSpec((tb, 128), lambda p, b: (blk(b), P_GQ // 128 + p))
    k = pl.BlockSpec((tb, 128), lambda p, b: (blk(b), P_GK // 128 + p))
    la = pl.BlockSpec((tb, 128), lambda p, b: (blk(b), p))
    v = pl.BlockSpec((tb, 256), lambda p, b: (blk(b), P_GV // 256 + p))
    o = pl.BlockSpec((tb, 256), lambda p, b: (blk(b), p))
    st = pl.BlockSpec((tb // GLA_CHUNK, 2, 128, 128), lambda p, b: (blk(b), p, 0, 0))
    return q, k, la, v, o, st


def _gla_fwd(proj, la):
    s = proj.shape[0]
    tb = _tile(s, 512)
    nb, nch = s // tb, tb // GLA_CHUNK

    def kern(q_ref, k_ref, la_ref, v_ref, o_ref, st_ref, s_sc):
        @pl.when(pl.program_id(1) == 0)
        def _():
            s_sc[...] = jnp.zeros_like(s_sc)

        s0, s1 = s_sc[0], s_sc[1]
        for ci in range(nch):
            sl = slice(ci * GLA_CHUNK, (ci + 1) * GLA_CHUNK)
            st_ref[ci, 0] = s0
            st_ref[ci, 1] = s1
            o0, o1, s0, s1 = _gla_chunk(q_ref[sl, :], k_ref[sl, :], la_ref[sl, :], v_ref[sl, 0:128],
                                        v_ref[sl, 128:256], s0, s1)
            o_ref[sl, 0:128] = o0
            o_ref[sl, 128:256] = o1
        s_sc[0] = s0
        s_sc[1] = s1

    q, k, lasp, v, o, st = _gla_specs(tb)
    return pl.pallas_call(
        kern, grid=(2, nb), in_specs=[q, k, lasp, v], out_specs=[o, st],
        out_shape=[jax.ShapeDtypeStruct((s, 512), F32),
                   jax.ShapeDtypeStruct((s // GLA_CHUNK, GLA_HEADS, 128, 128), F32)],
        scratch_shapes=[pltpu.VMEM((2, 128, 128), F32)],
        compiler_params=pltpu.CompilerParams(dimension_semantics=("parallel", "arbitrary"),
                                             vmem_limit_bytes=VMEM_LIMIT),
        name="gla_fwd")(proj, proj, la, proj)


def _gla_bwd(proj, la, states, d_o, comm):
    s = proj.shape[0]
    tb = _tile(s, 512)
    nb, nch = s // tb, tb // GLA_CHUNK
    nci, nco = len(comm.ins), len(comm.out_shape)

    def kern(*refs):
        (q_ref, k_ref, la_ref, v_ref, do_ref, st_ref), cins, (dq_ref, dk_ref, dla_ref, dv_ref), couts, (ds_sc,), csems = \
            _split_refs(refs, (6, nci, 4, nco, 1, len(comm.sems)))
        place = _place()
        pair, blk = pl.program_id(0), pl.program_id(1)

        @pl.when((pair == 0) & (blk == 0))
        def _():
            comm.start(place, cins, couts, csems)

        @pl.when((pair == 1) & (blk == nb // 2))
        def _():
            comm.mid(place, cins, couts, csems)

        @pl.when(blk == 0)
        def _():
            ds_sc[...] = jnp.zeros_like(ds_sc)

        d0, d1 = ds_sc[0], ds_sc[1]
        for ci in reversed(range(nch)):
            sl = slice(ci * GLA_CHUNK, (ci + 1) * GLA_CHUNK)
            _, vjp = jax.vjp(_gla_chunk, q_ref[sl, :], k_ref[sl, :], la_ref[sl, :], v_ref[sl, 0:128],
                             v_ref[sl, 128:256], st_ref[ci, 0], st_ref[ci, 1])
            gq, gk, gla, gv0, gv1, d0, d1 = vjp((do_ref[sl, 0:128], do_ref[sl, 128:256], d0, d1))
            dq_ref[sl, :] = gq
            dk_ref[sl, :] = gk
            dla_ref[sl, :] = gla
            dv_ref[sl, 0:128] = gv0
            dv_ref[sl, 128:256] = gv1
        ds_sc[0] = d0
        ds_sc[1] = d1

        @pl.when((pair == 1) & (blk == nb - 1))
        def _():
            comm.finish(place, cins, couts, csems)

    q, k, lasp, v, o, st = _gla_specs(tb, rev_nb=nb)
    res = pl.pallas_call(
        kern, grid=(2, nb), in_specs=[q, k, lasp, v, o, st] + [ANY] * nci, out_specs=[lasp, lasp, lasp, o] + [ANY] * nco,
        out_shape=[jax.ShapeDtypeStruct((s, 256), F32), jax.ShapeDtypeStruct((s, 256), F32),
                   jax.ShapeDtypeStruct((s, 256), F32), jax.ShapeDtypeStruct((s, 512), F32)] + comm.out_shape,
        scratch_shapes=[pltpu.VMEM((2, 128, 128), F32)] + comm.sems,
        compiler_params=pltpu.CompilerParams(dimension_semantics=("arbitrary", "arbitrary"),
                                             vmem_limit_bytes=VMEM_LIMIT),
        name="gla_bwd")(proj, proj, la, proj, d_o, states, *comm.ins)
    return res[0], res[1], res[2], res[3], res[4:]


def _causal_keep(t, qi, ki):
    row = lax.broadcasted_iota(jnp.int32, (t, t), 0) + qi * t
    col = lax.broadcasted_iota(jnp.int32, (t, t), 1) + ki * t
    return col <= row


def _split_refs(refs, counts):
    out, off = [], 0
    for cnt in counts:
        out.append(refs[off:off + cnt])
        off += cnt
    return out


def _attn_fwd(q, k, v, comm, tile=1024):
    s = q.shape[0]
    t = _tile(s, tile)
    n = s // t
    nci, nco = len(comm.ins), len(comm.out_shape)

    def kern(*refs):
        (q_ref, k_ref, v_ref), cins, (o_ref, lse_ref), couts, (m_sc, l_sc, acc_sc), csems = _split_refs(
            refs, (3, nci, 2, nco, 3, len(comm.sems)))
        qi, ki = pl.program_id(1), pl.program_id(2)
        place = _place()

        @pl.when((pl.program_id(0) == 0) & (qi == 0) & (ki == 0))
        def _():
            comm.start(place, cins, couts, csems)

        @pl.when((pl.program_id(0) == MLA_HEADS // 2 - 1) & (qi == 0) & (ki == 0))
        def _():
            comm.mid(place, cins, couts, csems)

        first = lax.broadcasted_iota(jnp.int32, (t, LANES), 1) < MLA_V

        @pl.when(ki == 0)
        def _():
            m_sc[...] = jnp.full_like(m_sc, -jnp.inf)
            l_sc[...] = jnp.zeros_like(l_sc)
            acc_sc[...] = jnp.zeros_like(acc_sc)

        def update(diagonal):
            keep = _causal_keep(t, 0, 0)
            alphas, pvs = [], []
            for h in range(2):
                sc = _dg(q_ref[:, 128 * h:128 * (h + 1)], k_ref[:, 128 * h:128 * (h + 1)], _NT)
                if diagonal:
                    sc = jnp.where(keep, sc, -jnp.inf)
                m_prev = m_sc[h]
                m_new = jnp.maximum(m_prev, jnp.max(sc, axis=1, keepdims=True))
                alpha = jnp.exp2(m_prev - m_new)
                p = jnp.exp2(sc - m_new[:, 0:1])
                l_sc[h] = alpha * l_sc[h] + jnp.sum(p, axis=1, keepdims=True)
                m_sc[h] = m_new
                alphas.append(alpha)
                pvs.append(_dg(p, v_ref[...], _NN))
            acc_sc[...] = acc_sc[...] * jnp.where(first, alphas[0], alphas[1]) + jnp.where(first, pvs[0], pvs[1])

        @pl.when(ki < qi)
        def _():
            update(False)

        @pl.when(ki == qi)
        def _():
            update(True)

        @pl.when(ki == qi)
        def _():
            l = jnp.where(first, l_sc[0], l_sc[1])
            m = jnp.where(first, m_sc[0], m_sc[1])
            o_ref[...] = acc_sc[...] / l
            lse_ref[...] = m + jnp.log2(l)

        @pl.when((pl.program_id(0) == MLA_HEADS // 2 - 1) & (qi == n - 1) & (ki == n - 1))
        def _():
            comm.finish(place, cins, couts, csems)

    kv_idx = lambda p, qi, ki: (jnp.minimum(ki, qi), p)
    res = pl.pallas_call(
        kern, grid=(MLA_HEADS // 2, n, n),
        in_specs=[pl.BlockSpec((t, 256), lambda p, qi, ki: (qi, p)), pl.BlockSpec((t, 256), kv_idx),
                  pl.BlockSpec((t, 128), kv_idx)] + [ANY] * nci,
        out_specs=[pl.BlockSpec((t, 128), lambda p, qi, ki: (qi, p)), pl.BlockSpec((t, 128), lambda p, qi, ki: (qi, p))]
        + [ANY] * nco,
        out_shape=[jax.ShapeDtypeStruct((s, 512), F32), jax.ShapeDtypeStruct((s, 512), F32)] + comm.out_shape,
        scratch_shapes=[pltpu.VMEM((2, t, LANES), F32), pltpu.VMEM((2, t, LANES), F32), pltpu.VMEM((t, LANES), F32)]
        + comm.sems,
        compiler_params=pltpu.CompilerParams(dimension_semantics=("arbitrary", "arbitrary", "arbitrary"),
                                             vmem_limit_bytes=VMEM_LIMIT),
        name="mla_attn_fwd")(q, k, v, *comm.ins)
    return res[0], res[1], res[2:]


def _attn_bwd(q, k, v, o, lse, d_o, comm, tile=512):
    s = q.shape[0]
    t = _tile(s, tile)
    n = s // t
    nci, nco = len(comm.ins), len(comm.out_shape)

    def kern(*refs):
        (q_ref, k_ref, v_ref, o_ref, lse_ref, do_ref), cins, (dq_ref, dk_ref, dv_ref), couts, (dk_sc, dv_sc), csems = \
            _split_refs(refs, (6, nci, 3, nco, 2, len(comm.sems)))
        ki, qi = pl.program_id(1), pl.program_id(2)
        place = _place()

        @pl.when((pl.program_id(0) == 0) & (qi == 0) & (ki == 0))
        def _():
            comm.start(place, cins, couts, csems)

        @pl.when((pl.program_id(0) == MLA_HEADS // 2 - 1) & (qi == 0) & (ki == 0))
        def _():
            comm.mid(place, cins, couts, csems)

        @pl.when((ki == 0) & (qi == 0))
        def _():
            dq_ref[...] = jnp.zeros_like(dq_ref)

        @pl.when(qi == ki)
        def _():
            dk_sc[...] = jnp.zeros_like(dk_sc)
            dv_sc[...] = jnp.zeros_like(dv_sc)

        def update(diagonal):
            keep = _causal_keep(t, 0, 0)
            d_o = do_ref[...]
            prod = d_o * o_ref[...]
            rows = pl.ds(pl.multiple_of(qi * t, t), t)
            for h in range(2):
                hs = slice(128 * h, 128 * (h + 1))
                mk = _lane_mask(MLA_V * h, MLA_V * (h + 1))
                qh, kh = q_ref[:, hs], k_ref[:, hs]
                sc = _dg(qh, kh, _NT)
                if diagonal:
                    sc = jnp.where(keep, sc, -jnp.inf)
                p = jnp.exp2(sc - lse_ref[:, MLA_V * h:MLA_V * h + 1])
                doh = d_o * mk
                dp = _dg(doh * LN2, v_ref[...], _NT)
                delta = jnp.sum(prod * mk, axis=1, keepdims=True) * LN2
                ds = p * (dp - delta)
                dv_sc[...] += _dg(p, doh, _TN)
                dk_sc[:, hs] += _dg(ds, qh, _TN)
                dq_ref[rows, hs] += _dg(ds, kh, _NN)

        @pl.when(qi > ki)
        def _():
            update(False)

        @pl.when(qi == ki)
        def _():
            update(True)

        @pl.when(qi == n - 1)
        def _():
            dk_ref[...] = dk_sc[...]
            dv_ref[...] = dv_sc[...].astype(dv_ref.dtype)

        @pl.when((pl.program_id(0) == MLA_HEADS // 2 - 1) & (qi == n - 1) & (ki == n - 1))
        def _():
            comm.finish(place, cins, couts, csems)

    q_idx = lambda p, ki, qi: (jnp.maximum(qi, ki), p)
    res = pl.pallas_call(
        kern, grid=(MLA_HEADS // 2, n, n),
        in_specs=[pl.BlockSpec((t, 256), q_idx), pl.BlockSpec((t, 256), lambda p, ki, qi: (ki, p)),
                  pl.BlockSpec((t, 128), lambda p, ki, qi: (ki, p)), pl.BlockSpec((t, 128), q_idx),
                  pl.BlockSpec((t, 128), q_idx),
                  pl.BlockSpec((t, 128), q_idx)] + [ANY] * nci,
        out_specs=[pl.BlockSpec((s, 256), lambda p, ki, qi: (0, p)), pl.BlockSpec((t, 256), lambda p, ki, qi: (ki, p)),
                   pl.BlockSpec((t, 128), lambda p, ki, qi: (ki, p))] + [ANY] * nco,
        out_shape=[jax.ShapeDtypeStruct((s, 1024), F32), jax.ShapeDtypeStruct((s, 1024), F32),
                   jax.ShapeDtypeStruct((s, 512), BF16)] + comm.out_shape,
        scratch_shapes=[pltpu.VMEM((t, 256), F32), pltpu.VMEM((t, 128), F32)] + comm.sems,
        compiler_params=pltpu.CompilerParams(dimension_semantics=("arbitrary", "arbitrary", "arbitrary"),
                                             vmem_limit_bytes=VMEM_LIMIT),
        name="mla_attn_bwd")(q, k, v, o, lse, d_o, *comm.ins)
    return res[0], res[1], res[2], res[3:]


def _gate_fn(alr, w2, b):
    return _log_sigmoid(_dot_nn(alr, w2) + b) * (1.0 / GLA_GATE_NORM)


def _qk_head(qh, kh, kpe, c, sa, sb, qn, kn):
    kfull = kh + kpe * _lane_mask(MLA_NOPE, MLA_QK)
    q_r = _rope(_rms(qh, qn, MLA_QK), c, sa, sb) * (MLA_QK ** -0.5 * LOG2E)
    k_r = _rope(_rms(kfull, kn, MLA_QK), c, sa, sb)
    return q_r, k_r


def _mix_head(o, og, gn):
    return _rms(o, gn) * _silu(og)


def _xa_head(xq, xk, xv, qn, kn):
    sc = _dot_nt(_rms(xq, qn), _rms(xk, kn)) * (XA_DIM ** -0.5)
    e = jnp.exp(sc - lax.stop_gradient(jnp.max(sc, axis=1, keepdims=True)))
    p = e / jnp.sum(e, axis=1, keepdims=True)
    return _dot_nn(p, xv)


def _heads(x, n):
    return [x[:, 128 * h:128 * (h + 1)] for h in range(n)]


def _cat(xs):
    return jnp.concatenate(xs, axis=1)


def _norm_fwd(x, w, name):
    return _rows_call(lambda r, c: ([_rms(r[0], c[0])], []), [_row(x)], [w], [(x.shape[1], BF16)], name=name)[0]


def _norm_fwd_epilogue(w):
    return _Epilogue(lambda h, rows, consts: ([h, _rms(h, consts[0])], []), [], [w], [(D_MODEL, F32), (D_MODEL, BF16)], [])


def _norm_bwd_epilogue(x, w, add):
    def fn(d_out, rows, consts):
        _, vjp = jax.vjp(_rms, rows[0], consts[0])
        dx, dw = vjp(d_out)
        return [dx + rows[1]], [dw]

    return _Epilogue(fn, [_row(x), _row(add)], [w], [(D_MODEL, F32)], [w.shape])


def _norm_bwd(x, w, d_out, add, name):
    def body(r, c):
        _, vjp = jax.vjp(_rms, r[0], c[0])
        dx, dw = vjp(r[1])
        return [dx + r[2]], [dw]

    return _rows_call(body, [_row(x), _row(d_out), _row(add)], [w], [(x.shape[1], F32)], [w.shape], name=name)


CONV_HALO = BF16_ROWS


def _conv_specs(s, f, t):
    n8 = t // CONV_HALO
    cur = pl.BlockSpec((None, t, f), lambda j, i: (j, i, 0))
    prev = pl.BlockSpec((None, CONV_HALO, f), lambda j, i: (j, jnp.maximum(i * n8 - 1, 0), 0))
    nxt = pl.BlockSpec((None, CONV_HALO, f), lambda j, i: (j, jnp.minimum((i + 1) * n8, s // CONV_HALO - 1), 0))
    cw = pl.BlockSpec((None, 3, f), lambda j, i: (j, 0, 0))
    cb = pl.BlockSpec((None, 1, f), lambda j, i: (j, 0, 0))
    return cur, prev, nxt, cw, cb


def _conv_taps(g, prev, first):
    ext = jnp.concatenate([jnp.where(first, 0.0, prev.astype(F32)), g], axis=0)
    return pltpu.roll(ext, 1, 0)[CONV_HALO:], pltpu.roll(ext, 2, 0)[CONV_HALO:]


def _conv_fwd(gg, uu, cw, cb):
    _, s, f = gg.shape
    t = _tile(s, 512)

    def kern(g_ref, gp_ref, u_ref, cw_ref, cb_ref, o_ref):
        g = g_ref[...].astype(F32)
        g1, g2 = _conv_taps(g, gp_ref[...], pl.program_id(1) == 0)
        w = cw_ref[...]
        gc = cb_ref[...] + w[0:1] * g2 + w[1:2] * g1 + w[2:3] * g
        o_ref[...] = (_silu(gc) * u_ref[...].astype(F32)).astype(o_ref.dtype)

    cur, prev, _, cws, cbs = _conv_specs(s, f, t)
    return pl.pallas_call(
        kern, grid=(4, s // t), in_specs=[cur, prev, cur, cws, cbs], out_specs=cur,
        out_shape=jax.ShapeDtypeStruct(gg.shape, BF16),
        compiler_params=pltpu.CompilerParams(dimension_semantics=("parallel", "parallel"), vmem_limit_bytes=VMEM_LIMIT),
        name="ffn_conv_fwd")(gg, gg, uu, cw, cb)


def _conv_bwd(gg, uu, dact, cw, cb):
    _, s, f = gg.shape
    t = _tile(s, 512)
    nt = s // t

    def kern(g_ref, gp_ref, gn_ref, u_ref, un_ref, da_ref, dan_ref, cw_ref, cb_ref, du_ref, dg_ref, dcw_ref, dcb_ref):
        i = pl.program_id(1)
        cat = lambda a_ref, b_ref: jnp.concatenate([a_ref[...].astype(F32), b_ref[...].astype(F32)], axis=0)
        g, u, da = cat(g_ref, gn_ref), cat(u_ref, un_ref), cat(da_ref, dan_ref)
        g1, g2 = _conv_taps(g, gp_ref[...], i == 0)
        w = cw_ref[...]
        gc = cb_ref[...] + w[0:1] * g2 + w[1:2] * g1 + w[2:3] * g
        sg = jax.nn.sigmoid(gc)
        du_ref[...] = (da[:t] * (gc[:t] * sg[:t])).astype(du_ref.dtype)
        row = lax.broadcasted_iota(jnp.int32, (t + CONV_HALO, 1), 0)
        dgc = jnp.where((row < t) | (i < nt - 1), da * u * (sg * (1.0 + gc * (1.0 - sg))), 0.0)
        up1 = pltpu.roll(dgc, t + CONV_HALO - 1, 0)[:t]
        up2 = pltpu.roll(dgc, t + CONV_HALO - 2, 0)[:t]
        dgc = dgc[:t]
        dg_ref[...] = (w[2:3] * dgc + w[1:2] * up1 + w[0:1] * up2).astype(dg_ref.dtype)

        @pl.when(i == 0)
        def _():
            dcw_ref[...] = jnp.zeros_like(dcw_ref)
            dcb_ref[...] = jnp.zeros_like(dcb_ref)

        dcw_ref[0:1, :] += jnp.sum(dgc * g2[:t], axis=0, keepdims=True)
        dcw_ref[1:2, :] += jnp.sum(dgc * g1[:t], axis=0, keepdims=True)
        dcw_ref[2:3, :] += jnp.sum(dgc * g[:t], axis=0, keepdims=True)
        dcb_ref[...] += jnp.sum(dgc, axis=0, keepdims=True)

    cur, prev, nxt, cws, cbs = _conv_specs(s, f, t)
    return pl.pallas_call(
        kern, grid=(4, nt), in_specs=[cur, prev, nxt, cur, nxt, cur, nxt, cws, cbs], out_specs=[cur, cur, cws, cbs],
        out_shape=[jax.ShapeDtypeStruct(gg.shape, BF16), jax.ShapeDtypeStruct(gg.shape, BF16),
                   jax.ShapeDtypeStruct(cw.shape, F32), jax.ShapeDtypeStruct(cb.shape, F32)],
        compiler_params=pltpu.CompilerParams(dimension_semantics=("parallel", "arbitrary"), vmem_limit_bytes=VMEM_LIMIT),
        name="ffn_conv_bwd")(gg, gg, gg, uu, uu, dact, dact, cw, cb)


def _rope_tables(pos):
    half = MLA_ROPE // 2
    inv = ROPE_THETA ** (-jnp.arange(half, dtype=F32) / half)
    ang = pos.astype(F32)[:, None] * inv
    cos, sin = jnp.cos(ang), jnp.sin(ang)
    s = pos.shape[0]
    z = lambda w: jnp.zeros((s, w), F32)
    c = jnp.concatenate([jnp.ones((s, MLA_NOPE), F32), cos, cos, jnp.ones((s, LANES - MLA_QK), F32)], axis=1)
    sa = jnp.concatenate([z(MLA_NOPE), -sin, z(half), z(LANES - MLA_QK)], axis=1)
    sb = jnp.concatenate([z(MLA_NOPE), z(half), sin, z(LANES - MLA_QK)], axis=1)
    return c, sa, sb


def _local_step(x, mem, pos, target, w, late_shards):
    g = {}
    w = dict(w)
    c, sa, sb = _rope_tables(pos)

    xn = _norm_fwd(x, w["norm_mix"], "norm_mix_fwd")

    def proj_fn(r, rows, k):
        la_ = _gate_fn(r[:, P_ALR:P_ALR + 128], k[0], k[1])
        return [r, la_, _rms(r[:, P_CQ:P_CQ + MLA_Q_RANK], k[2]), _rms(r[:, P_CKV:P_CKV + MLA_KV_RANK], k[3])], []

    proj, la, q_lat, kv_lat = _matmul(
        xn, w["in"], "nn", F32, "proj_fwd", epilogue=_Epilogue(
            proj_fn, [], [w["w2"], w["gate_b"], w["q_a_norm"], w["kv_a_norm"]],
            [(P_WIDTH, F32), (256, F32), (MLA_Q_RANK, BF16), (MLA_KV_RANK, BF16)], []))
    alr = _row(proj, 128, P_ALR // 128)
    kpe = _row(proj, 128, P_KPE // 128)
    og = _row(proj, 512, P_OG // 512)
    cq = _row(proj, 256, P_CQ // 256)
    ckv = _row(proj, 128, P_CKV // 128)

    o_gla, states = _gla_fwd(proj, la)

    q_up = _matmul(q_lat, w["uq"], "nn", F32, "mla_q_fwd")
    k_up = _matmul(kv_lat, w["k"], "nn", F32, "mla_k_fwd")
    v_mla = _matmul(kv_lat, w["v"], "nn", BF16, "mla_v_fwd")

    def qk_body(r, k):
        qs, ks = [], []
        for qh, kh in zip(_heads(r[0], MLA_HEADS), _heads(r[1], MLA_HEADS)):
            a, b = _qk_head(qh, kh, r[2], r[3], r[4], r[5], k[0], k[1])
            qs.append(a)
            ks.append(b)
        return [_cat(qs), _cat(ks)], []

    tabs = [_row(c), _row(sa), _row(sb)]
    q_r, k_r = _rows_call(qk_body, [_row(q_up), _row(k_up), kpe] + tabs, [w["q_norm"], w["k_norm"]],
                          [(1024, BF16), (1024, BF16)], name="mla_qk_fwd")
    o_mla, lse, gathered = _attn_fwd(q_r, k_r, v_mla, _gather_plan(late_shards))
    w.update(_late_layout(dict(zip(LATE, gathered, strict=True))))

    def mix_body(r, k):
        ys = [_mix_head(o, g_, k[0]) for o, g_ in zip(_heads(r[0], GLA_HEADS), _heads(r[1], GLA_HEADS))]
        return [_cat(ys + [r[2]])], []

    cat = _rows_call(mix_body, [_row(o_gla), og, _row(o_mla)], [w["gla_out_norm"]], [(1024, BF16)],
                     name="mix_fwd")[0]
    h1, hn = _matmul(cat, w["out"], "nn", F32, "out_fwd_norm", residual=x, epilogue=_norm_fwd_epilogue(w["norm_xa"]))
    mn = _norm_fwd(mem, w["norm_mem"], "norm_mem_fwd")
    xkv = _matmul(mn, w["xkv"], "nn", F32, "xa_kv_fwd")

    def xa_fn(r, rows, k):
        ks, vs = _heads(k[0], 2 * XA_HEADS)[:XA_HEADS], _heads(k[0], 2 * XA_HEADS)[XA_HEADS:]
        return [r, _cat([_xa_head(a, b, v_, k[1], k[2]) for a, b, v_ in zip(_heads(r, XA_HEADS), ks, vs)])], []

    xq, xo = _matmul(hn, w["xq"], "nn", F32, "xa_q_fwd_attn", epilogue=_Epilogue(
        xa_fn, [], [xkv, w["xa_q_norm"], w["xa_k_norm"]], [(512, F32), (512, BF16)], []))
    h2, fn = _matmul(xo, w["xo"], "nn", F32, "xa_o_fwd_norm", residual=h1, epilogue=_norm_fwd_epilogue(w["norm_ffn"]))
    gg = _matmul(fn, w["wg"], "nt", BF16, "ffn_gate_fwd", b_lead="p")
    uu = _matmul(fn, w["wu"], "nt", BF16, "ffn_up_fwd", b_lead="p")
    act = _conv_fwd(gg, uu, w["cw"], w["cb"])
    def loss_fn(y, rows, consts):
        err = y - rows[0]
        part = 0.5 * jnp.sum(jnp.sum(err * err, axis=1, keepdims=True) * (1.0 / D_MODEL), axis=0, keepdims=True)
        return [err * (1.0 / D_MODEL)], [jnp.broadcast_to(part, (1, LANES))]

    dy, loss = _matmul(act, w["wd"], "nn", F32, "ffn_down_fwd_loss", residual=h2, a_lead="k", b_lead="k",
                       epilogue=_Epilogue(loss_fn, [_row(target)], [], [(D_MODEL, F32)], [(1, LANES)]))

    g["ffn_w_down"] = _matmul(act, dy, "tn", BF16, "ffn_down_dw", a_lead="p")
    dact = _matmul(dy, w["wd"], "nt", BF16, "ffn_down_dx", b_lead="p")
    duu, dgg, g["ffn_conv_w"], g["ffn_conv_b"] = _conv_bwd(gg, uu, dact, w["cw"], w["cb"])
    g["ffn_w_gate"] = _matmul(dgg, fn, "tn", BF16, "ffn_gate_dw", a_lead="p")
    g["ffn_w_up"] = _matmul(duu, fn, "tn", BF16, "ffn_up_dw", a_lead="p")
    dh2, g["norm_ffn"] = _matmul(dgg, w["wg"], "nn", F32, "ffn_dx_norm_bwd", a_lead="k", b_lead="k", more=(duu, w["wu"]),
                                 epilogue=_norm_bwd_epilogue(h2, w["norm_ffn"], dy))

    g["xa_w_o"] = _matmul(xo, dh2, "tn", BF16, "xa_o_dw")
    def xa_bwd(dxo_, rows, k):
        kvh = _heads(k[0], 2 * XA_HEADS)
        dq_, dk_, dv_ = [], [], []
        dqn, dkn = 0.0, 0.0
        for h, (a, d_) in enumerate(zip(_heads(rows[0], XA_HEADS), _heads(dxo_, XA_HEADS))):
            _, vjp = jax.vjp(_xa_head, a, kvh[h], kvh[XA_HEADS + h], k[1], k[2])
            ga, gk, gv, gqn, gkn = vjp(d_)
            dq_.append(ga)
            dk_.append(gk)
            dv_.append(gv)
            dqn, dkn = dqn + gqn, dkn + gkn
        return [_cat(dq_)], [_cat(dk_ + dv_), dqn, dkn]

    dxq, dxkv, g["xa_q_norm"], g["xa_k_norm"] = _matmul(dh2, w["xo"], "nt", F32, "xa_o_dx_attn_bwd", epilogue=_Epilogue(
        xa_bwd, [_row(xq)], [xkv, w["xa_q_norm"], w["xa_k_norm"]], [(512, BF16)], [xkv.shape, (1, 128), (1, 128)]))
    g["xa_w_q"] = _matmul(hn, dxq, "tn", BF16, "xa_q_dw")
    dh1, g["norm_xa"] = _matmul(dxq, w["xq"], "nt", F32, "xa_q_dx_norm_bwd",
                                epilogue=_norm_bwd_epilogue(h1, w["norm_xa"], dh2))
    g["xa_w_kv"] = _matmul(mn, dxkv, "tn", BF16, "xa_kv_dw")
    dmn = _matmul(dxkv, w["xkv"], "nt", F32, "xa_kv_dx")
    _, g["norm_mem"] = _norm_bwd(mem, w["norm_mem"], dmn, dmn, "norm_mem_bwd")

    g["w_out"] = _matmul(cat, dh1, "tn", BF16, "out_dw")
    def mix_bwd(dcat_, rows, k):
        do_, dog_ = [], []
        dgn = 0.0
        for o, g_, d_ in zip(_heads(rows[0], GLA_HEADS), _heads(rows[1], GLA_HEADS), _heads(dcat_, GLA_HEADS)):
            _, vjp = jax.vjp(_mix_head, o, g_, k[0])
            a, b, gn_ = vjp(d_)
            do_.append(a)
            dog_.append(b)
            dgn = dgn + gn_
        return [_cat(do_), _cat(dog_), dcat_[:, 512:]], [dgn]

    do_gla, d_og, do_mla, g["gla_out_norm"] = _matmul(dh1, w["out"], "nt", F32, "out_dx_mix_bwd", epilogue=_Epilogue(
        mix_bwd, [_row(o_gla), og], [w["gla_out_norm"]], [(512, F32), (512, BF16), (512, F32)], [(1, 128)]))

    late_parts = _late_grad_shards(g)
    dq_r, dk_r, dv_mla, lands_mlp = _attn_bwd(q_r, k_r, v_mla, o_mla, lse, do_mla,
                                              _scatter_plan([late_parts[n] for n in LATE_MLP]))

    def qk_bwd(r, k):
        dqs, dks = [], []
        dkpe, dqn, dkn = 0.0, 0.0, 0.0
        for qh, kh, dqh, dkh in zip(_heads(r[0], MLA_HEADS), _heads(r[1], MLA_HEADS), _heads(r[6], MLA_HEADS),
                                    _heads(r[7], MLA_HEADS)):
            _, vjp = jax.vjp(lambda a, b, e, f, h_: _qk_head(a, b, e, r[3], r[4], r[5], f, h_), qh, kh, r[2], k[0], k[1])
            ga, gb, ge, gf, gh = vjp((dqh, dkh))
            dqs.append(ga)
            dks.append(gb)
            dkpe, dqn, dkn = dkpe + ge, dqn + gf, dkn + gh
        return [_cat(dqs), _cat(dks), dkpe], [dqn, dkn]

    dq_up, dk_up, d_kpe, g["q_norm"], g["k_norm"] = _rows_call(
        qk_bwd, [_row(q_up), _row(k_up), kpe] + tabs + [_row(dq_r), _row(dk_r)], [w["q_norm"], w["k_norm"]],
        [(1024, BF16), (1024, BF16), (128, BF16)], [(1, 128), (1, 128)], name="mla_qk_bwd")
    g["uq"] = _matmul(q_lat, dq_up, "tn", BF16, "mla_q_dw")
    dq_lat = _matmul(dq_up, w["uq"], "nt", F32, "mla_q_dx")
    g["k"] = _matmul(kv_lat, dk_up, "tn", BF16, "mla_k_dw")
    g["v"] = _matmul(kv_lat, dv_mla, "tn", BF16, "mla_v_dw")
    dkv_lat = _matmul(dk_up, w["k"], "nt", F32, "mla_k_dx")
    dkv_lat = _matmul(dv_mla, w["v"], "nt", F32, "mla_v_dx", residual=dkv_lat)

    def lat_bwd(r, k):
        _, vjp1 = jax.vjp(_rms, r[0], k[0])
        _, vjp2 = jax.vjp(_rms, r[1], k[1])
        a, ga = vjp1(r[2])
        b, gb = vjp2(r[3])
        return [a, b], [ga, gb]

    d_cq, d_ckv, g["mla_q_a_norm"], g["mla_kv_a_norm"] = _rows_call(
        lat_bwd, [cq, ckv, _row(dq_lat), _row(dkv_lat)], [w["q_a_norm"], w["kv_a_norm"]],
        [(256, BF16), (128, BF16)], [(1, 256), (1, 128)], name="mla_lat_bwd")

    dgq, dgk, dla, dgv, lands_mix = _gla_bwd(proj, la, states, do_gla, _scatter_plan([late_parts[n] for n in LATE_MIX]))
    lands_late = dict(zip(LATE_MLP + LATE_MIX, list(lands_mlp) + list(lands_mix), strict=True))

    def gate_bwd(r, k):
        _, vjp = jax.vjp(_gate_fn, r[0], k[0], k[1])
        a, gw, gb = vjp(r[1])
        return [a], [gw, gb]

    d_alr, g["w2"], g["gla_gate_b"] = _rows_call(gate_bwd, [alr, _row(dla)], [w["w2"], w["gate_b"]], [(128, BF16)],
                                                 [(128, 256), (1, 256)], name="gla_gate_bwd")

    dproj = jnp.concatenate([dgq.astype(BF16), dgk.astype(BF16), dgv.astype(BF16), d_og, d_cq, d_ckv, d_kpe, d_alr],
                            axis=1)
    g["in"] = _matmul(xn, dproj, "tn", BF16, "proj_dw")
    dx, g["norm_mix"] = _matmul(dproj, w["in"], "nt", F32, "proj_dx_norm_bwd",
                                epilogue=_norm_bwd_epilogue(x, w["norm_mix"], dh1))
    return loss[0, 0], dx, g, lands_late


def _join_shards(pieces, axis):
    if axis == 0:
        return pieces.reshape(-1, pieces.shape[2])
    return jnp.transpose(pieces, (1, 0, 2)).reshape(pieces.shape[1], -1)


def _split_shards(full, axis):
    r, c = full.shape
    if axis == 0:
        return full.reshape(4, r // 4, c)
    return jnp.transpose(full.reshape(r, 4, c // 4), (1, 0, 2))


def _early_layout(gath, rep):
    w_in = _join_shards(gath["w_in"], 1)
    z = lambda n: jnp.zeros((D_MODEL, n), w_in.dtype)
    seg = lambda lo, n: w_in[:, lo:lo + n]
    ukv = _join_shards(gath["mla_w_ukv"], 1).reshape(MLA_KV_RANK, MLA_HEADS, MLA_NOPE + MLA_V)
    w = {
        "in": jnp.concatenate([seg(N_GQ, 256), seg(N_GK, 256), seg(N_GV, 512), seg(N_OG, 512), seg(N_CQ, 256),
                               seg(N_CKV, 128), z(64), seg(N_KPE, 32), z(32), seg(N_ALR, 16), z(112)], axis=1),
        "uq": jnp.pad(_join_shards(gath["mla_w_uq"], 1).reshape(MLA_Q_RANK, MLA_HEADS, MLA_QK),
                      ((0, 0), (0, 0), (0, LANES - MLA_QK))).reshape(MLA_Q_RANK, MLA_HEADS * LANES),
        "k": jnp.pad(ukv[:, :, :MLA_NOPE], ((0, 0), (0, 0), (0, LANES - MLA_NOPE))).reshape(MLA_KV_RANK, -1),
        "v": ukv[:, :, MLA_NOPE:].reshape(MLA_KV_RANK, MLA_HEADS * MLA_V),
        "w2": jnp.pad(_join_shards(gath["gla_gate_w2"], 1), ((0, LANES - GLA_RANK), (0, 0))),
        "cb": rep["ffn_conv_b"].reshape(4, 1, D_FF // 4),
        "q_norm": jnp.pad(rep["mla_q_norm"], ((0, 0), (0, LANES - MLA_QK))),
        "k_norm": jnp.pad(rep["mla_k_norm"], ((0, 0), (0, LANES - MLA_QK))),
        "q_a_norm": rep["mla_q_a_norm"], "kv_a_norm": rep["mla_kv_a_norm"], "gate_b": rep["gla_gate_b"],
    }
    for n in ("norm_mix", "gla_out_norm", "norm_xa", "norm_mem", "xa_q_norm", "xa_k_norm", "norm_ffn"):
        w[n] = rep[n]
    return w


def _late_layout(gath):
    return {"out": _join_shards(gath["w_out"], 0), "xq": _join_shards(gath["xa_w_q"], 0),
            "xkv": _join_shards(gath["xa_w_kv"], 0), "xo": _join_shards(gath["xa_w_o"], 1),
            "wg": gath["ffn_w_gate"], "wu": gath["ffn_w_up"], "wd": gath["ffn_w_down"], "cw": gath["ffn_conv_w"]}


def _late_grad_shards(g):
    sh = {"w_out": _split_shards(g["w_out"], 0), "xa_w_q": _split_shards(g["xa_w_q"], 0),
          "xa_w_kv": _split_shards(g["xa_w_kv"], 0), "xa_w_o": _split_shards(g["xa_w_o"], 1),
          "ffn_w_gate": g["ffn_w_gate"], "ffn_w_up": g["ffn_w_up"], "ffn_conv_w": g["ffn_conv_w"],
          "ffn_w_down": g["ffn_w_down"]}
    return {n: v.astype(BF16) for n, v in sh.items()}


def _early_grad_shards(g):
    gi = g["in"]
    seg = lambda lo, n: gi[:, lo:lo + n]
    w_in = jnp.concatenate([seg(P_GQ, 256), seg(P_GK, 256), seg(P_GV, 512), seg(P_ALR, 16), seg(P_OG, 512),
                            seg(P_CQ, 256), seg(P_CKV, 128), seg(P_KPE + 64, 32)], axis=1)
    uq = g["uq"].reshape(MLA_Q_RANK, MLA_HEADS, LANES)[:, :, :MLA_QK].reshape(MLA_Q_RANK, -1)
    ukv = jnp.concatenate([g["k"].reshape(MLA_KV_RANK, MLA_HEADS, LANES)[:, :, :MLA_NOPE],
                           g["v"].reshape(MLA_KV_RANK, MLA_HEADS, MLA_V)], axis=2).reshape(MLA_KV_RANK, -1)
    sh = {"w_in": _split_shards(w_in, 1), "gla_gate_w2": _split_shards(g["w2"][:GLA_RANK], 1),
          "mla_w_uq": _split_shards(uq, 1), "mla_w_ukv": _split_shards(ukv, 1)}
    sh = {n: v.astype(BF16) for n, v in sh.items()}
    rep = {n: g[n] for n in REPLICATED if n in g}
    rep["mla_q_norm"] = g["q_norm"][:, :MLA_QK]
    rep["mla_k_norm"] = g["k_norm"][:, :MLA_QK]
    rep["ffn_conv_b"] = g["ffn_conv_b"].reshape(1, D_FF)
    return sh, rep


SMALL_SHAPE = (8, 1024)


def _pack_small(vectors):
    flat = jnp.concatenate(vectors, axis=1)
    return jnp.pad(flat, ((0, 0), (0, SMALL_SHAPE[0] * SMALL_SHAPE[1] - flat.shape[1]))).reshape(SMALL_SHAPE)


def _unpack_small(buf, widths):
    flat = buf.reshape(1, -1)
    out, off = [], 0
    for wd in widths:
        out.append(flat[:, off:off + wd])
        off += wd
    return out


ANY = pl.BlockSpec(memory_space=pl.ANY)


def _place():
    x, y, c = lax.axis_index("x"), lax.axis_index("y"), lax.axis_index("c")
    chips = [(1 - x, y), (x, 1 - y), (1 - x, 1 - y)]
    return x, y, c, chips


class _Comm:
    def __init__(self, ins, out_shape, sems, start, finish, mid=None):
        self.ins, self.out_shape, self.sems = list(ins), list(out_shape), list(sems)
        self.start, self.finish, self.mid = start, finish, mid or (lambda *args: None)


def _run_comm(plan, name):
    ni, no = len(plan.ins), len(plan.out_shape)

    def body(*refs):
        ins, outs, sems = refs[:ni], refs[ni:ni + no], refs[ni + no:]
        place = _place()
        plan.start(place, ins, outs, sems)
        plan.mid(place, ins, outs, sems)
        plan.finish(place, ins, outs, sems)

    return pl.pallas_call(body, in_specs=[ANY] * ni, out_specs=[ANY] * no, out_shape=plan.out_shape,
                          scratch_shapes=plan.sems, name=name)(*plan.ins)


def _gather_plan(shards):
    n = len(shards)
    split = [s.shape[0] % (2 * BF16_ROWS) == 0 for s in shards]

    def rows(ref, t, c):
        if not split[t]:
            return ref
        half = shards[t].shape[0] // 2
        return ref.at[pl.ds(pl.multiple_of(c * half, BF16_ROWS), half)]

    def remote(src, dst, ss, rs, to):
        return pltpu.make_async_remote_copy(src_ref=src, dst_ref=dst, send_sem=ss, recv_sem=rs, device_id=to,
                                            device_id_type=MESH)

    def first_wave(place, ins, outs, sems):
        x, y, c, chips = place
        ici_s, ici_r, _, _, local = sems
        me = 2 * x + y
        own = [pltpu.make_async_copy(ins[t], outs[t].at[me], local.at[t]) for t in range(n)]
        push = [remote(rows(ins[t], t, c), rows(outs[t].at[me], t, c), ici_s.at[3 * t + j], ici_r.at[3 * t + j], (px, py, c))
                for t in range(n) for j, (px, py) in enumerate(chips)]
        return own, push

    def second_wave(place, ins, outs, sems, last):
        x, y, c, chips = place
        ici_s, ici_r, d2d_s, d2d_r, local = sems
        sib = (x, y, 1 - c)
        out = []
        for t in range(n):
            for j, (px, py) in enumerate(chips):
                block = outs[t].at[2 * px + py]
                got = rows(block, t, c)
                if split[t]:
                    hand = remote(got, got, d2d_s.at[3 * t + j], d2d_r.at[3 * t + j], sib)
                    theirs = rows(block, t, 1 - c)
                    other = (remote(theirs, theirs, local.at[0], d2d_r.at[3 * t + j], sib) if last else
                             remote(got, got, local.at[0], ici_r.at[3 * t + j], sib))
                    out.append((other, hand))
                elif last:
                    out.append((remote(got, got, local.at[0], ici_r.at[3 * t + j], sib), None))
        return out

    def start(place, ins, outs, sems):
        own, push = first_wave(place, ins, outs, sems)
        for cp in own + push:
            cp.start()

    def mid(place, ins, outs, sems):
        for arrival, hand in second_wave(place, ins, outs, sems, False):
            arrival.wait_recv()
            hand.start()

    def finish(place, ins, outs, sems):
        own, push = first_wave(place, ins, outs, sems)
        for arrival, hand in second_wave(place, ins, outs, sems, True):
            arrival.wait_recv()
            if hand is not None:
                hand.wait_send()
        for cp in push:
            cp.wait_send()
        for cp in own:
            cp.wait()

    dma = pltpu.SemaphoreType.DMA
    return _Comm(shards, [jax.ShapeDtypeStruct((4,) + s.shape, s.dtype) for s in shards],
                 [dma((3 * n,)), dma((3 * n,)), dma((3 * n,)), dma((3 * n,)), dma((n,))], start, finish, mid)


def _scatter_plan(parts, small=None):
    n = len(parts)
    ns = 0 if small is None else 1

    def unpack(place, ins, outs, sems):
        x, y, c, chips = place
        return x, y, c, chips, 2 * x + y, 4 * x + 2 * y + c, (x, y, 1 - c)

    def remote(src, dst, ss, rs, to):
        return pltpu.make_async_remote_copy(src_ref=src, dst_ref=dst, send_sem=ss, recv_sem=rs, device_id=to,
                                            device_id_type=MESH)

    def first_wave(place, ins, outs, sems):
        x, y, c, chips, me, dev, sib = unpack(place, ins, outs, sems)
        ici_s, ici_r, d2d_s, d2d_r, sm_s, sm_r, local = sems
        own, push = [], []
        if ns:
            own.append(pltpu.make_async_copy(ins[n], outs[n].at[dev], local.at[n]))
            for k in range(1, 8):
                px = (1 - x) if (k >> 2) & 1 else x
                py = (1 - y) if (k >> 1) & 1 else y
                pc = (1 - c) if k & 1 else c
                push.append(remote(ins[n], outs[n].at[dev], sm_s.at[k - 1], sm_r.at[k - 1], (px, py, pc)))
        for t in range(n):
            own.append(pltpu.make_async_copy(ins[t].at[me], outs[t].at[dev], local.at[t]))
            push.append(remote(ins[t].at[me], outs[t].at[dev], d2d_s.at[4 * t], d2d_r.at[4 * t], sib))
            for j, (px, py) in enumerate(chips):
                push.append(remote(ins[t].at[2 * px + py], outs[t].at[dev], ici_s.at[3 * t + j], ici_r.at[3 * t + j],
                                   (px, py, c)))
        return own, push

    def start(place, ins, outs, sems):
        own, push = first_wave(place, ins, outs, sems)
        for cp in own + push:
            cp.start()

    def landed(dst, rs, sems, sib):
        remote(dst, dst, sems[-1].at[0], rs, sib).wait_recv()

    def forwards(place, ins, outs, sems):
        x, y, c, chips, me, dev, sib = unpack(place, ins, outs, sems)
        d2d_s, d2d_r = sems[2], sems[3]
        slots = [(t, j, outs[t].at[4 * px + 2 * py + c]) for t in range(n) for j, (px, py) in enumerate(chips)]
        return [(t, j, slot, remote(slot, slot, d2d_s.at[4 * t + 1 + j], d2d_r.at[4 * t + 1 + j], sib))
                for t, j, slot in slots]

    def mid(place, ins, outs, sems):
        sib = unpack(place, ins, outs, sems)[-1]
        for t, j, slot, cp in forwards(place, ins, outs, sems):
            landed(slot, sems[1].at[3 * t + j], sems, sib)
            cp.start()

    def finish(place, ins, outs, sems):
        x, y, c, chips, me, dev, sib = unpack(place, ins, outs, sems)
        d2d_r, sm_r = sems[3], sems[5]
        own, push = first_wave(place, ins, outs, sems)
        push += [cp for _, _, _, cp in forwards(place, ins, outs, sems)]
        for t in range(n):
            landed(outs[t].at[4 * x + 2 * y + (1 - c)], d2d_r.at[4 * t], sems, sib)
            for j, (px, py) in enumerate(chips):
                landed(outs[t].at[4 * px + 2 * py + (1 - c)], d2d_r.at[4 * t + 1 + j], sems, sib)
        if ns:
            for k in range(1, 8):
                px = (1 - x) if (k >> 2) & 1 else x
                py = (1 - y) if (k >> 1) & 1 else y
                pc = (1 - c) if k & 1 else c
                landed(outs[n].at[4 * px + 2 * py + pc], sm_r.at[k - 1], sems, sib)
        for cp in push:
            cp.wait_send()
        for cp in own:
            cp.wait()

    dma = pltpu.SemaphoreType.DMA
    ins = list(parts) + ([small] if ns else [])
    out_shape = [jax.ShapeDtypeStruct((8,) + p.shape[1:], p.dtype) for p in parts]
    if ns:
        out_shape.append(jax.ShapeDtypeStruct((8,) + small.shape, small.dtype))
    return _Comm(ins, out_shape, [dma((3 * n,)), dma((3 * n,)), dma((4 * n,)), dma((4 * n,)), dma((7,)), dma((7,)),
                                  dma((n + 1,))], start, finish, mid)


ADAM_ROWS = 288


def _row_tile(r, cap):
    if r <= cap:
        return r
    return max(t for t in range(8, cap + 1, 8) if r % t == 0)


def _adamw_update(w, m, v, land):
    g = land[0].astype(F32)
    for i in range(1, 8):
        g = g + land[i].astype(F32)
    m_new = ADAM_B1 * m + (1.0 - ADAM_B1) * g
    v_new = ADAM_B2 * v + (1.0 - ADAM_B2) * (g * g)
    m_hat = m_new / (1.0 - ADAM_B1 ** ADAM_STEP)
    v_hat = v_new / (1.0 - ADAM_B2 ** ADAM_STEP)
    return g, -ADAM_LR * (m_hat / (jnp.sqrt(v_hat) + ADAM_EPS) + ADAM_WD * w), m_new, v_new


def _adamw(tensors, name, comm=None):
    k = len(tensors)
    r, c = tensors[0][0].shape
    t = _row_tile(r, ADAM_ROWS // k)
    n = r // t
    nci, nco, nsem = (len(comm.ins), len(comm.out_shape), len(comm.sems)) if comm else (0, 0, 0)

    def kern(*refs):
        ins, cins, outs, couts, csems = _split_refs(refs, (4 * k, nci, 4 * k, nco, nsem))
        if comm:
            place = _place()

            @pl.when(pl.program_id(0) == 0)
            def _():
                comm.start(place, cins, couts, csems)

        for i in range(k):
            w_ref, m_ref, v_ref, l_ref = ins[4 * i:4 * i + 4]
            res = _adamw_update(w_ref[...], m_ref[...], v_ref[...], l_ref)
            for ref, val in zip(outs[4 * i:4 * i + 4], res, strict=True):
                ref[...] = val
        if comm:
            @pl.when(pl.program_id(0) == n - 1)
            def _():
                comm.mid(place, cins, couts, csems)
                comm.finish(place, cins, couts, csems)

    spec = pl.BlockSpec((t, c), lambda i: (i, 0))
    lspec = pl.BlockSpec((8, t, c), lambda i: (0, i, 0))
    res = pl.pallas_call(
        kern, grid=(n,), in_specs=[spec, spec, spec, lspec] * k + [ANY] * nci, out_specs=[spec] * (4 * k) + [ANY] * nco,
        out_shape=[jax.ShapeDtypeStruct((r, c), F32)] * (4 * k) + (comm.out_shape if comm else []),
        scratch_shapes=comm.sems if comm else [],
        compiler_params=pltpu.CompilerParams(dimension_semantics=("arbitrary" if comm else "parallel",),
                                             vmem_limit_bytes=VMEM_LIMIT),
        name=name)(*[x for tens in tensors for x in tens], *(comm.ins if comm else []))
    return [res[4 * i:4 * i + 4] for i in range(k)], res[4 * k:]


def _step(a):
    def sq(n):
        v = a[n][0] if a[n].ndim == 3 else a[n]
        return v.T if n.removeprefix("m_").removeprefix("v_") in TRANSPOSED else v

    payload = lambda n: sq(n) if n in EXACT_GATHER else sq(n).astype(BF16)

    gathered = _run_comm(_gather_plan([payload(n) for n in EARLY]), "gather_early")
    w = _early_layout(dict(zip(EARLY, gathered, strict=True)), {n: a[n] for n in REPLICATED})

    loss, dx, g, lands_late = _local_step(sq("x"), sq("mem"), a["positions"][0], sq("loss_target"), w,
                                          [payload(n) for n in LATE])

    sh, rep = _early_grad_shards(g)
    small = _pack_small([rep[n] for n in REPLICATED] + [loss.reshape(1, 1)])
    *lands_early, land_small = _run_comm(_scatter_plan([sh[n] for n in EARLY], small), "scatter_last")
    quad = lambda n, land: (sq(n), sq("m_" + n), sq("v_" + n), land)
    lands = dict(zip(EARLY, lands_early, strict=True)) | lands_late

    outs = {}
    kinds = ("grad_", "delta_", "new_m_", "new_v_")
    for n, _ in SHARDED:
        res = _adamw([quad(n, lands[n])], "adamw_" + n)[0][0]
        for kind, val in zip(kinds, res, strict=True):
            outs[kind + n] = (val.T if n in TRANSPOSED else val).reshape(a[n].shape)
    zero = jnp.zeros((1, 1), F32)
    packed = [_pack_small([a[p + n] for n in REPLICATED] + [zero]) for p in ("", "m_", "v_")]
    res = _adamw([(*packed, land_small)], "adamw_replicated")[0][0]
    widths = [a[n].shape[1] for n in REPLICATED] + [1]
    for kind, buf in zip(kinds, res, strict=True):
        *vals, total = _unpack_small(buf, widths)
        for n, val in zip(REPLICATED, vals, strict=True):
            outs[kind + n] = val
        if kind == "grad_":
            loss = total[0, 0]

    ordered = [outs[kind + n] for kind in kinds for n in WEIGHTS]
    return (loss, dx[None], *ordered)


def kernel(x, mem, positions, norm_mix, w_in, gla_gate_w2, gla_gate_b, gla_out_norm, mla_q_a_norm, mla_w_uq, mla_kv_a_norm, mla_w_ukv, mla_q_norm, mla_k_norm, w_out, norm_xa, norm_mem, xa_w_q, xa_w_kv, xa_q_norm, xa_k_norm, xa_w_o, norm_ffn, ffn_w_gate, ffn_w_up, ffn_conv_w, ffn_conv_b, ffn_w_down, loss_target, m_norm_mix, m_w_in, m_gla_gate_w2, m_gla_gate_b, m_gla_out_norm, m_mla_q_a_norm, m_mla_w_uq, m_mla_kv_a_norm, m_mla_w_ukv, m_mla_q_norm, m_mla_k_norm, m_w_out, m_norm_xa, m_norm_mem, m_xa_w_q, m_xa_w_kv, m_xa_q_norm, m_xa_k_norm, m_xa_w_o, m_norm_ffn, m_ffn_w_gate, m_ffn_w_up, m_ffn_conv_w, m_ffn_conv_b, m_ffn_w_down, v_norm_mix, v_w_in, v_gla_gate_w2, v_gla_gate_b, v_gla_out_norm, v_mla_q_a_norm, v_mla_w_uq, v_mla_kv_a_norm, v_mla_w_ukv, v_mla_q_norm, v_mla_k_norm, v_w_out, v_norm_xa, v_norm_mem, v_xa_w_q, v_xa_w_kv, v_xa_q_norm, v_xa_k_norm, v_xa_w_o, v_norm_ffn, v_ffn_w_gate, v_ffn_w_up, v_ffn_conv_w, v_ffn_conv_b, v_ffn_w_down):
    return _step(dict(locals()))
```

```python
import functools

import jax
import jax.numpy as jnp
from jax import lax
from jax.experimental import pallas as pl
from jax.experimental.pallas import tpu as pltpu

F32, BF16 = jnp.float32, jnp.bfloat16
MESH = pl.DeviceIdType.MESH

D_MODEL = 1024
EPS = 1e-6
GLA_HEADS, GLA_DK, GLA_DV, GLA_RANK, GLA_CHUNK = 4, 64, 128, 16, 64
GLA_GATE_NORM = 16.0
MLA_HEADS, MLA_Q_RANK, MLA_KV_RANK, MLA_NOPE, MLA_ROPE, MLA_V = 8, 256, 128, 64, 32, 64
MLA_QK = MLA_NOPE + MLA_ROPE
ROPE_THETA = 10000.0
LOG2E, LN2 = 1.4426950408889634, 0.6931471805599453
XA_HEADS, XA_DIM = 4, 128
D_FF = 2816
ADAM_LR, ADAM_B1, ADAM_B2, ADAM_EPS, ADAM_WD, ADAM_STEP = 0.001, 0.9, 0.999, 1e-08, 0.01, 10

LANES = 128
BF16_ROWS = 16
VMEM_LIMIT = 56 * 1024 * 1024
MATMUL_VMEM = 44 * 1024 * 1024

P_GQ, P_GK, P_GV, P_OG, P_CQ, P_CKV, P_KPE, P_ALR, P_WIDTH = 0, 256, 512, 1024, 1536, 1792, 1920, 2048, 2176
N_GQ, N_GK, N_GV, N_ALR, N_OG, N_CQ, N_CKV, N_KPE, N_WIDTH = 0, 256, 512, 1024, 1040, 1552, 1808, 1936, 1968

SHARDED = (("w_in", 1), ("gla_gate_w2", 1), ("mla_w_uq", 1), ("mla_w_ukv", 1), ("w_out", 0), ("xa_w_q", 0),
           ("xa_w_kv", 0), ("xa_w_o", 1), ("ffn_w_gate", 1), ("ffn_w_up", 1), ("ffn_conv_w", 1), ("ffn_w_down", 0))
REPLICATED = ("norm_mix", "gla_gate_b", "gla_out_norm", "mla_q_a_norm", "mla_kv_a_norm", "mla_q_norm", "mla_k_norm",
              "norm_xa", "norm_mem", "xa_q_norm", "xa_k_norm", "norm_ffn", "ffn_conv_b")
EXACT_GATHER = ("gla_gate_w2", "ffn_conv_w")
TRANSPOSED = ("ffn_w_gate", "ffn_w_up")
EARLY = ("w_in", "gla_gate_w2", "mla_w_uq", "mla_w_ukv")
LATE = tuple(n for n, _ in SHARDED if n not in EARLY)
WEIGHTS = ("norm_mix", "w_in", "gla_gate_w2", "gla_gate_b", "gla_out_norm", "mla_q_a_norm", "mla_w_uq",
           "mla_kv_a_norm", "mla_w_ukv", "mla_q_norm", "mla_k_norm", "w_out", "norm_xa", "norm_mem", "xa_w_q",
           "xa_w_kv", "xa_q_norm", "xa_k_norm", "xa_w_o", "norm_ffn", "ffn_w_gate", "ffn_w_up", "ffn_conv_w",
           "ffn_conv_b", "ffn_w_down")


_NN = ((1,), (0,))
_NT = ((1,), (1,))
_TN = ((0,), (0,))


def _dg(a, b, dims):
    return lax.dot_general(a.astype(BF16), b.astype(BF16), (dims, ((), ())), preferred_element_type=F32)


@jax.custom_vjp
def _dot_nn(a, b):
    return _dg(a, b, _NN)


_dot_nn.defvjp(lambda a, b: (_dg(a, b, _NN), (a, b)),
               lambda r, g: (_dg(g, r[1], _NT).astype(r[0].dtype), _dg(r[0], g, _TN).astype(r[1].dtype)))


@jax.custom_vjp
def _dot_nt(a, b):
    return _dg(a, b, _NT)


_dot_nt.defvjp(lambda a, b: (_dg(a, b, _NT), (a, b)),
               lambda r, g: (_dg(g, r[1], _NN).astype(r[0].dtype), _dg(g, r[0], _TN).astype(r[1].dtype)))


@jax.custom_vjp
def _dot_tn(a, b):
    return _dg(a, b, _TN)


_dot_tn.defvjp(lambda a, b: (_dg(a, b, _TN), (a, b)),
               lambda r, g: (_dg(r[1], g, _NT).astype(r[0].dtype), _dg(r[0], g, _NN).astype(r[1].dtype)))


def _rms(x, w, n=None):
    n = x.shape[-1] if n is None else n
    ms = jnp.sum(x * x, axis=-1, keepdims=True) * (1.0 / n)
    return x * lax.rsqrt(ms + EPS) * w


def _silu(x):
    return x * jax.nn.sigmoid(x)


def _log_sigmoid(x):
    return jnp.minimum(x, 0.0) - jnp.log(1.0 + jnp.exp(-jnp.abs(x)))


@jax.custom_vjp
def _rope(y, c, sa, sb):
    return y * c + pltpu.roll(y, LANES - 16, 1) * sa + pltpu.roll(y, 16, 1) * sb


def _rope_bwd(res, g):
    c, sa, sb = res
    gy = g * c + pltpu.roll(g * sa, 16, 1) + pltpu.roll(g * sb, LANES - 16, 1)
    return gy, jnp.zeros_like(c), jnp.zeros_like(sa), jnp.zeros_like(sb)


_rope.defvjp(lambda y, c, sa, sb: (_rope(y, c, sa, sb), (c, sa, sb)), _rope_bwd)


@jax.custom_vjp
def _cumsum_rows(x):
    n = x.shape[0]
    row = lax.broadcasted_iota(jnp.int32, x.shape, 0)
    k = 1
    while k < n:
        x = x + jnp.where(row >= k, pltpu.roll(x, k, 0), 0.0)
        k *= 2
    return x


def _cumsum_rows_bwd(_, g):
    n = g.shape[0]
    row = lax.broadcasted_iota(jnp.int32, g.shape, 0)
    k = 1
    while k < n:
        g = g + jnp.where(row < n - k, pltpu.roll(g, n - k, 0), 0.0)
        k *= 2
    return (g,)


_cumsum_rows.defvjp(lambda x: (_cumsum_rows(x), None), _cumsum_rows_bwd)


def _lane_mask(lo, hi):
    lane = lax.broadcasted_iota(jnp.int32, (1, LANES), 1)
    return ((lane >= lo) & (lane < hi)).astype(F32)


def _tile(n, t):
    t = min(n, t)
    assert n % t == 0, (n, t)
    return t


class _Epilogue:
    def __init__(self, fn, rows=(), consts=(), outs=(), accs=()):
        self.fn, self.rows, self.consts, self.outs, self.accs = fn, list(rows), list(consts), list(outs), list(accs)


def _matmul(a, b, mode, out_dtype, name, residual=None, a_lead=None, b_lead=None, more=None, epilogue=None):
    (a0, a1), (b0, b1) = a.shape[-2:], b.shape[-2:]
    if mode == "nn":
        m, k, k2, n = a0, a1, b0, b1
    elif mode == "nt":
        m, k, n, k2 = a0, a1, b0, b1
    else:
        k, m, k2, n = a0, a1, b0, b1
    assert k == k2, (a.shape, b.shape, mode)
    npar = 4 if "p" in (a_lead, b_lead) else 1
    nsum = 4 if "k" in (a_lead, b_lead) else 1
    pairs = [(a, b)] + ([more] if more else [])
    a_item, b_item, o_item = a.dtype.itemsize, b.dtype.itemsize, jnp.dtype(out_dtype).itemsize
    ep = epilogue
    row_extra = 4 if residual is not None else 0
    if ep:
        row_extra += (sum(r.dtype.itemsize * wd for r, wd, _ in ep.rows) + sum(jnp.dtype(d).itemsize * wd for wd, d in ep.outs)) / n

    def vmem_need(tm, tn, tk):
        need = 2 * (nsum if a_lead == "k" else 1) * tm * tk * a_item + 2 * (nsum if b_lead == "k" else 1) * tk * tn * b_item
        need *= len(pairs)
        need += (0 if ep else 2 * tm * tn * o_item) + tm * tn * 4 * (2 if tk < k else 1)
        need += tm * tk * 2 * (a_item == 4 or mode == "tn") + tk * tn * 2 * (b_item == 4)
        return need + int(2 * tm * tn * row_extra) + (3 * tm * tn * 4 if ep else 0)

    halvings = (4096, 2048, 1024, 512, 256, 128, 64, 32, 16, 8)
    if mode == "tn":
        tm = m if m <= 1408 else m // 2
        tn = n if tm * n <= 1024 * 2304 else n // 2
        tk = next((r for r in halvings if k % r == 0 and vmem_need(tm, tn, r) <= MATMUL_VMEM), k)
    else:
        tn, tk = n, k
        tm = next((r for r in halvings if m % r == 0 and vmem_need(r, tn, tk) <= MATMUL_VMEM), m)
    assert m % tm == 0 and n % tn == 0 and k % tk == 0
    assert ep is None or (tn == n and tk == k and npar == 1)
    nk = k // tk
    dims = {"nn": _NN, "nt": _NT, "tn": _TN}[mode]
    n_in = 2 * len(pairs) + (residual is not None)
    n_ep_in = len(ep.rows) + len(ep.consts) if ep else 0
    n_out = len(ep.outs) + len(ep.accs) if ep else 1

    def body(*refs):
        ab, rs, ep_in, outs, scratch = _split_refs(refs, (2 * len(pairs), n_in - 2 * len(pairs), n_ep_in, n_out, nk > 1))
        prod = None
        for a_ref, b_ref in zip(ab[0::2], ab[1::2]):
            for sh in range(nsum):
                term = _dg(a_ref[sh] if a_lead == "k" else a_ref[...], b_ref[sh] if b_lead == "k" else b_ref[...], dims)
                prod = term if prod is None else prod + term

        def finish(r):
            if rs:
                r = r + rs[0][...]
            if ep is None:
                outs[0][...] = r.astype(outs[0].dtype)
                return
            vals = [x[...] for x in ep_in]
            ro, ao = ep.fn(r, vals[:len(ep.rows)], vals[len(ep.rows):])
            for ref, val in zip(outs[:len(ep.outs)], ro, strict=True):
                ref[...] = val.astype(ref.dtype)
            if ep.accs:
                @pl.when(pl.program_id(0) == 0)
                def _():
                    for ref in outs[len(ep.outs):]:
                        ref[...] = jnp.zeros_like(ref)

                for ref, val in zip(outs[len(ep.outs):], ao, strict=True):
                    ref[...] += val

        if nk == 1:
            finish(prod)
            return
        acc = scratch[0]
        kk = pl.program_id(3)

        @pl.when(kk == 0)
        def _():
            acc[...] = prod

        @pl.when(kk > 0)
        def _():
            acc[...] += prod

        @pl.when(kk == nk - 1)
        def _():
            finish(acc[...])

    def spec(lead, blk, idx):
        if lead is None:
            return pl.BlockSpec(blk, lambda i, j, p, kk: idx(i, j, kk))
        if lead == "p":
            return pl.BlockSpec((None,) + blk, lambda i, j, p, kk: (p,) + idx(i, j, kk))
        return pl.BlockSpec((nsum,) + blk, lambda i, j, p, kk: (0,) + idx(i, j, kk))

    if mode == "nn":
        pair_specs = [spec(a_lead, (tm, tk), lambda i, j, kk: (i, kk)), spec(b_lead, (tk, tn), lambda i, j, kk: (kk, j))]
    elif mode == "nt":
        pair_specs = [spec(a_lead, (tm, tk), lambda i, j, kk: (i, kk)), spec(b_lead, (tn, tk), lambda i, j, kk: (j, kk))]
    else:
        pair_specs = [spec(a_lead, (tk, tm), lambda i, j, kk: (kk, i)), spec(b_lead, (tk, tn), lambda i, j, kk: (kk, j))]
    tile = spec(None, (tm, tn), lambda i, j, kk: (i, j))
    in_specs = pair_specs * len(pairs)
    args = [x for pair in pairs for x in pair]
    if residual is not None:
        assert npar == 1
        in_specs.append(tile)
        args.append(residual)
    if ep:
        in_specs += [pl.BlockSpec((tm, wd), functools.partial(lambda cb, i, j, p, kk: (i, cb), cb)) for _, wd, cb in ep.rows]
        in_specs += [pl.BlockSpec(c.shape, lambda i, j, p, kk: (0, 0)) for c in ep.consts]
        args += [r for r, _, _ in ep.rows] + ep.consts
        out_specs = [pl.BlockSpec((tm, wd), lambda i, j, p, kk: (i, 0)) for wd, _ in ep.outs]
        out_specs += [pl.BlockSpec(shape, lambda i, j, p, kk: (0, 0)) for shape in ep.accs]
        out_shape = [jax.ShapeDtypeStruct((m, wd), d) for wd, d in ep.outs] + [jax.ShapeDtypeStruct(sh, F32) for sh in ep.accs]
    else:
        out_specs = spec("p" if npar > 1 else None, (tm, tn), lambda i, j, kk: (i, j))
        out_shape = jax.ShapeDtypeStruct(((4,) if npar > 1 else ()) + (m, n), out_dtype)
    outer = "arbitrary" if ep and ep.accs else "parallel"
    return pl.pallas_call(
        body, grid=(m // tm, n // tn, npar, nk), in_specs=in_specs, out_specs=out_specs, out_shape=out_shape,
        scratch_shapes=[pltpu.VMEM((tm, tn), F32)] if nk > 1 else [],
        compiler_params=pltpu.CompilerParams(dimension_semantics=(outer, outer, outer, "arbitrary"),
                                             vmem_limit_bytes=VMEM_LIMIT),
        name=name)(*args)


def _row(a, width=None, col_block=0):
    return (a, a.shape[1] if width is None else width, col_block)


def _rows_call(body, rows, consts, outs, accs=(), *, name, tile=512):
    s = rows[0][0].shape[0]
    t = _tile(s, tile)
    nr, nc, no = len(rows), len(consts), len(outs)

    def kern(*refs):
        r = [x[...] for x in refs[:nr]]
        c = [x[...] for x in refs[nr:nr + nc]]
        o_refs = refs[nr + nc:nr + nc + no]
        a_refs = refs[nr + nc + no:]
        ro, ao = body(r, c)
        for ref, val in zip(o_refs, ro, strict=True):
            ref[...] = val.astype(ref.dtype)
        if a_refs:
            @pl.when(pl.program_id(0) == 0)
            def _():
                for ref in a_refs:
                    ref[...] = jnp.zeros_like(ref)

            for ref, val in zip(a_refs, ao, strict=True):
                ref[...] += val

    in_specs = [pl.BlockSpec((t, w), functools.partial(lambda cb, i: (i, cb), cb)) for (_, w, cb) in rows]
    in_specs += [pl.BlockSpec(c.shape, lambda i: (0, 0)) for c in consts]
    out_specs = [pl.BlockSpec((t, w), lambda i: (i, 0)) for (w, _) in outs]
    out_specs += [pl.BlockSpec(shape, lambda i: (0, 0)) for shape in accs]
    out_shape = [jax.ShapeDtypeStruct((s, w), dt) for (w, dt) in outs]
    out_shape += [jax.ShapeDtypeStruct(shape, F32) for shape in accs]
    return pl.pallas_call(
        kern, grid=(s // t,), in_specs=in_specs, out_specs=out_specs, out_shape=out_shape,
        compiler_params=pltpu.CompilerParams(dimension_semantics=("arbitrary" if accs else "parallel",),
                                             vmem_limit_bytes=VMEM_LIMIT),
        name=name)(*[r[0] for r in rows], *consts)


def _gla_chunk(q, k, la, v0, v1, s0, s1):
    c = q.shape[0]
    r = lax.broadcasted_iota(jnp.int32, (c, c), 0)
    cc = lax.broadcasted_iota(jnp.int32, (c, c), 1)
    tril = cc <= r
    cum = _cumsum_rows(la)
    cl = jnp.sum(la, axis=0, keepdims=True)
    qd = q * (GLA_DK ** -0.5) * jnp.exp(cum)
    ki = k * jnp.exp(-cum)
    ke = k * jnp.exp(cl - cum)
    dec = jnp.exp(cl)
    outs, news = [], []
    for h, (v, s) in enumerate(((v0, s0), (v1, s1))):
        mk = _lane_mask(GLA_DK * h, GLA_DK * (h + 1))
        qh = qd * mk
        att = jnp.where(tril, _dot_nt(qh, ki), 0.0)
        outs.append(_dot_nn(att, v) + _dot_nt(qh, s))
        news.append(s * dec + _dot_tn(v, ke * mk))
    return outs[0], outs[1], news[0], news[1]


def _gla_specs(tb, rev_nb=None):
    blk = (lambda b: b) if rev_nb is None else (lambda b: rev_nb - 1 - b)
    q = pl.BlockSpec((tb, 128), lambda p, b: (blk(b), P_GQ // 128 + p))
    k = pl.BlockSpec((tb, 128), lambda p, b: (blk(b), P_GK // 128 + p))
    la = pl.BlockSpec((tb, 128), lambda p, b: (blk(b), p))
    v = pl.BlockSpec((tb, 256), lambda p, b: (blk(b), P_GV // 256 + p))
    o = pl.BlockSpec((tb, 256), lambda p, b: (blk(b), p))
    st = pl.BlockSpec((tb // GLA_CHUNK, 2, 128, 128), lambda p, b: (blk(b), p, 0, 0))
    return q, k, la, v, o, st


def _gla_fwd(proj, la):
    s = proj.shape[0]
    tb = _tile(s, 512)
    nb, nch = s // tb, tb // GLA_CHUNK

    def kern(q_ref, k_ref, la_ref, v_ref, o_ref, st_ref, s_sc):
        @pl.when(pl.program_id(1) == 0)
        def _():
            s_sc[...] = jnp.zeros_like(s_sc)

        s0, s1 = s_sc[0], s_sc[1]
        for ci in range(nch):
            sl = slice(ci * GLA_CHUNK, (ci + 1) * GLA_CHUNK)
            st_ref[ci, 0] = s0
            st_ref[ci, 1] = s1
            o0, o1, s0, s1 = _gla_chunk(q_ref[sl, :], k_ref[sl, :], la_ref[sl, :], v_ref[sl, 0:128],
                                        v_ref[sl, 128:256], s0, s1)
            o_ref[sl, 0:128] = o0
            o_ref[sl, 128:256] = o1
        s_sc[0] = s0
        s_sc[1] = s1

    q, k, lasp, v, o, st = _gla_specs(tb)
    return pl.pallas_call(
        kern, grid=(2, nb), in_specs=[q, k, lasp, v], out_specs=[o, st],
        out_shape=[jax.ShapeDtypeStruct((s, 512), F32),
                   jax.ShapeDtypeStruct((s // GLA_CHUNK, GLA_HEADS, 128, 128), F32)],
        scratch_shapes=[pltpu.VMEM((2, 128, 128), F32)],
        compiler_params=pltpu.CompilerParams(dimension_semantics=("parallel", "arbitrary"),
                                             vmem_limit_bytes=VMEM_LIMIT),
        name="gla_fwd")(proj, proj, la, proj)


def _gla_bwd(proj, la, states, d_o, comm):
    s = proj.shape[0]
    tb = _tile(s, 512)
    nb, nch = s // tb, tb // GLA_CHUNK
    nci, nco = len(comm.ins), len(comm.out_shape)

    def kern(*refs):
        (q_ref, k_ref, la_ref, v_ref, do_ref, st_ref), cins, (dq_ref, dk_ref, dla_ref, dv_ref), couts, (ds_sc,), csems = \
            _split_refs(refs, (6, nci, 4, nco, 1, len(comm.sems)))
        place = _place()
        pair, blk = pl.program_id(0), pl.program_id(1)

        @pl.when((pair == 0) & (blk == 0))
        def _():
            comm.start(place, cins, couts, csems)

        @pl.when((pair == 1) & (blk == nb // 2))
        def _():
            comm.mid(place, cins, couts, csems)

        @pl.when(blk == 0)
        def _():
            ds_sc[...] = jnp.zeros_like(ds_sc)

        d0, d1 = ds_sc[0], ds_sc[1]
        for ci in reversed(range(nch)):
            sl = slice(ci * GLA_CHUNK, (ci + 1) * GLA_CHUNK)
            _, vjp = jax.vjp(_gla_chunk, q_ref[sl, :], k_ref[sl, :], la_ref[sl, :], v_ref[sl, 0:128],
                             v_ref[sl, 128:256], st_ref[ci, 0], st_ref[ci, 1])
            gq, gk, gla, gv0, gv1, d0, d1 = vjp((do_ref[sl, 0:128], do_ref[sl, 128:256], d0, d1))
            dq_ref[sl, :] = gq
            dk_ref[sl, :] = gk
            dla_ref[sl, :] = gla
            dv_ref[sl, 0:128] = gv0
            dv_ref[sl, 128:256] = gv1
        ds_sc[0] = d0
        ds_sc[1] = d1

        @pl.when((pair == 1) & (blk == nb - 1))
        def _():
            comm.finish(place, cins, couts, csems)

    q, k, lasp, v, o, st = _gla_specs(tb, rev_nb=nb)
    res = pl.pallas_call(
        kern, grid=(2, nb), in_specs=[q, k, lasp, v, o, st] + [ANY] * nci, out_specs=[lasp, lasp, lasp, o] + [ANY] * nco,
        out_shape=[jax.ShapeDtypeStruct((s, 256), F32), jax.ShapeDtypeStruct((s, 256), F32),
                   jax.ShapeDtypeStruct((s, 256), F32), jax.ShapeDtypeStruct((s, 512), F32)] + comm.out_shape,
        scratch_shapes=[pltpu.VMEM((2, 128, 128), F32)] + comm.sems,
        compiler_params=pltpu.CompilerParams(dimension_semantics=("arbitrary", "arbitrary"),
                                             vmem_limit_bytes=VMEM_LIMIT),
        name="gla_bwd")(proj, proj, la, proj, d_o, states, *comm.ins)
    return res[0], res[1], res[2], res[3], res[4:]


def _causal_keep(t, qi, ki):
    row = lax.broadcasted_iota(jnp.int32, (t, t), 0) + qi * t
    col = lax.broadcasted_iota(jnp.int32, (t, t), 1) + ki * t
    return col <= row


def _split_refs(refs, counts):
    out, off = [], 0
    for cnt in counts:
        out.append(refs[off:off + cnt])
        off += cnt
    return out


def _attn_fwd(q, k, v, comm, tile=1024):
    s = q.shape[0]
    t = _tile(s, tile)
    n = s // t
    nci, nco = len(comm.ins), len(comm.out_shape)

    def kern(*refs):
        (q_ref, k_ref, v_ref), cins, (o_ref, lse_ref), couts, (m_sc, l_sc, acc_sc), csems = _split_refs(
            refs, (3, nci, 2, nco, 3, len(comm.sems)))
        qi, ki = pl.program_id(1), pl.program_id(2)
        place = _place()

        @pl.when((pl.program_id(0) == 0) & (qi == 0) & (ki == 0))
        def _():
            comm.start(place, cins, couts, csems)

        @pl.when((pl.program_id(0) == MLA_HEADS // 2 - 1) & (qi == 0) & (ki == 0))
        def _():
            comm.mid(place, cins, couts, csems)

        first = lax.broadcasted_iota(jnp.int32, (t, LANES), 1) < MLA_V

        @pl.when(ki == 0)
        def _():
            m_sc[...] = jnp.full_like(m_sc, -jnp.inf)
            l_sc[...] = jnp.zeros_like(l_sc)
            acc_sc[...] = jnp.zeros_like(acc_sc)

        def update(diagonal):
            keep = _causal_keep(t, 0, 0)
            alphas, pvs = [], []
            for h in range(2):
                sc = _dg(q_ref[:, 128 * h:128 * (h + 1)], k_ref[:, 128 * h:128 * (h + 1)], _NT)
                if diagonal:
                    sc = jnp.where(keep, sc, -jnp.inf)
                m_prev = m_sc[h]
                m_new = jnp.maximum(m_prev, jnp.max(sc, axis=1, keepdims=True))
                alpha = jnp.exp2(m_prev - m_new)
                p = jnp.exp2(sc - m_new[:, 0:1])
                l_sc[h] = alpha * l_sc[h] + jnp.sum(p, axis=1, keepdims=True)
                m_sc[h] = m_new
                alphas.append(alpha)
                pvs.append(_dg(p, v_ref[...], _NN))
            acc_sc[...] = acc_sc[...] * jnp.where(first, alphas[0], alphas[1]) + jnp.where(first, pvs[0], pvs[1])

        @pl.when(ki < qi)
        def _():
            update(False)

        @pl.when(ki == qi)
        def _():
            update(True)

        @pl.when(ki == qi)
        def _():
            l = jnp.where(first, l_sc[0], l_sc[1])
            m = jnp.where(first, m_sc[0], m_sc[1])
            o_ref[...] = acc_sc[...] / l
            lse_ref[...] = m + jnp.log2(l)

        @pl.when((pl.program_id(0) == MLA_HEADS // 2 - 1) & (qi == n - 1) & (ki == n - 1))
        def _():
            comm.finish(place, cins, couts, csems)

    kv_idx = lambda p, qi, ki: (jnp.minimum(ki, qi), p)
    res = pl.pallas_call(
        kern, grid=(MLA_HEADS // 2, n, n),
        in_specs=[pl.BlockSpec((t, 256), lambda p, qi, ki: (qi, p)), pl.BlockSpec((t, 256), kv_idx),
                  pl.BlockSpec((t, 128), kv_idx)] + [ANY] * nci,
        out_specs=[pl.BlockSpec((t, 128), lambda p, qi, ki: (qi, p)), pl.BlockSpec((t, 128), lambda p, qi, ki: (qi, p))]
        + [ANY] * nco,
        out_shape=[jax.ShapeDtypeStruct((s, 512), F32), jax.ShapeDtypeStruct((s, 512), F32)] + comm.out_shape,
        scratch_shapes=[pltpu.VMEM((2, t, LANES), F32), pltpu.VMEM((2, t, LANES), F32), pltpu.VMEM((t, LANES), F32)]
        + comm.sems,
        compiler_params=pltpu.CompilerParams(dimension_semantics=("arbitrary", "arbitrary", "arbitrary"),
                                             vmem_limit_bytes=VMEM_LIMIT),
        name="mla_attn_fwd")(q, k, v, *comm.ins)
    return res[0], res[1], res[2:]


def _attn_bwd(q, k, v, o, lse, d_o, comm, tile=512):
    s = q.shape[0]
    t = _tile(s, tile)
    n = s // t
    nci, nco = len(comm.ins), len(comm.out_shape)

    def kern(*refs):
        (q_ref, k_ref, v_ref, o_ref, lse_ref, do_ref), cins, (dq_ref, dk_ref, dv_ref), couts, (dk_sc, dv_sc), csems = \
            _split_refs(refs, (6, nci, 3, nco, 2, len(comm.sems)))
        ki, qi = pl.program_id(1), pl.program_id(2)
        place = _place()

        @pl.when((pl.program_id(0) == 0) & (qi == 0) & (ki == 0))
        def _():
            comm.start(place, cins, couts, csems)

        @pl.when((pl.program_id(0) == MLA_HEADS // 2 - 1) & (qi == 0) & (ki == 0))
        def _():
            comm.mid(place, cins, couts, csems)

        @pl.when((ki == 0) & (qi == 0))
        def _():
            dq_ref[...] = jnp.zeros_like(dq_ref)

        @pl.when(qi == ki)
        def _():
            dk_sc[...] = jnp.zeros_like(dk_sc)
            dv_sc[...] = jnp.zeros_like(dv_sc)

        def update(diagonal):
            keep = _causal_keep(t, 0, 0)
            d_o = do_ref[...]
            prod = d_o * o_ref[...]
            rows = pl.ds(pl.multiple_of(qi * t, t), t)
            for h in range(2):
                hs = slice(128 * h, 128 * (h + 1))
                mk = _lane_mask(MLA_V * h, MLA_V * (h + 1))
                qh, kh = q_ref[:, hs], k_ref[:, hs]
                sc = _dg(qh, kh, _NT)
                if diagonal:
                    sc = jnp.where(keep, sc, -jnp.inf)
                p = jnp.exp2(sc - lse_ref[:, MLA_V * h:MLA_V * h + 1])
                doh = d_o * mk
                dp = _dg(doh * LN2, v_ref[...], _NT)
                delta = jnp.sum(prod * mk, axis=1, keepdims=True) * LN2
                ds = p * (dp - delta)
                dv_sc[...] += _dg(p, doh, _TN)
                dk_sc[:, hs] += _dg(ds, qh, _TN)
                dq_ref[rows, hs] += _dg(ds, kh, _NN)

        @pl.when(qi > ki)
        def _():
            update(False)

        @pl.when(qi == ki)
        def _():
            update(True)

        @pl.when(qi == n - 1)
        def _():
            dk_ref[...] = dk_sc[...]
            dv_ref[...] = dv_sc[...].astype(dv_ref.dtype)

        @pl.when((pl.program_id(0) == MLA_HEADS // 2 - 1) & (qi == n - 1) & (ki == n - 1))
        def _():
            comm.finish(place, cins, couts, csems)

    q_idx = lambda p, ki, qi: (jnp.maximum(qi, ki), p)
    res = pl.pallas_call(
        kern, grid=(MLA_HEADS // 2, n, n),
        in_specs=[pl.BlockSpec((t, 256), q_idx), pl.BlockSpec((t, 256), lambda p, ki, qi: (ki, p)),
                  pl.BlockSpec((t, 128), lambda p, ki, qi: (ki, p)), pl.BlockSpec((t, 128), q_idx),
                  pl.BlockSpec((t, 128), q_idx),
                  pl.BlockSpec((t, 128), q_idx)] + [ANY] * nci,
        out_specs=[pl.BlockSpec((s, 256), lambda p, ki, qi: (0, p)), pl.BlockSpec((t, 256), lambda p, ki, qi: (ki, p)),
                   pl.BlockSpec((t, 128), lambda p, ki, qi: (ki, p))] + [ANY] * nco,
        out_shape=[jax.ShapeDtypeStruct((s, 1024), F32), jax.ShapeDtypeStruct((s, 1024), F32),
                   jax.ShapeDtypeStruct((s, 512), BF16)] + comm.out_shape,
        scratch_shapes=[pltpu.VMEM((t, 256), F32), pltpu.VMEM((t, 128), F32)] + comm.sems,
        compiler_params=pltpu.CompilerParams(dimension_semantics=("arbitrary", "arbitrary", "arbitrary"),
                                             vmem_limit_bytes=VMEM_LIMIT),
        name="mla_attn_bwd")(q, k, v, o, lse, d_o, *comm.ins)
    return res[0], res[1], res[2], res[3:]


def _gate_fn(alr, w2, b):
    return _log_sigmoid(_dot_nn(alr, w2) + b) * (1.0 / GLA_GATE_NORM)


def _qk_head(qh, kh, kpe, c, sa, sb, qn, kn):
    kfull = kh + kpe * _lane_mask(MLA_NOPE, MLA_QK)
    q_r = _rope(_rms(qh, qn, MLA_QK), c, sa, sb) * (MLA_QK ** -0.5 * LOG2E)
    k_r = _rope(_rms(kfull, kn, MLA_QK), c, sa, sb)
    return q_r, k_r


def _mix_head(o, og, gn):
    return _rms(o, gn) * _silu(og)


def _xa_head(xq, xk, xv, qn, kn):
    sc = _dot_nt(_rms(xq, qn), _rms(xk, kn)) * (XA_DIM ** -0.5)
    e = jnp.exp(sc - lax.stop_gradient(jnp.max(sc, axis=1, keepdims=True)))
    p = e / jnp.sum(e, axis=1, keepdims=True)
    return _dot_nn(p, xv)


def _heads(x, n):
    return [x[:, 128 * h:128 * (h + 1)] for h in range(n)]


def _cat(xs):
    return jnp.concatenate(xs, axis=1)


def _norm_fwd(x, w, name):
    return _rows_call(lambda r, c: ([_rms(r[0], c[0])], []), [_row(x)], [w], [(x.shape[1], BF16)], name=name)[0]


def _norm_fwd_epilogue(w):
    return _Epilogue(lambda h, rows, consts: ([h, _rms(h, consts[0])], []), [], [w], [(D_MODEL, F32), (D_MODEL, BF16)], [])


def _norm_bwd_epilogue(x, w, add):
    def fn(d_out, rows, consts):
        _, vjp = jax.vjp(_rms, rows[0], consts[0])
        dx, dw = vjp(d_out)
        return [dx + rows[1]], [dw]

    return _Epilogue(fn, [_row(x), _row(add)], [w], [(D_MODEL, F32)], [w.shape])


def _norm_fwd_comm(x, w, comm, name):
    s, d = x.shape
    t = _tile(s, 512)
    n = s // t
    nci, nco = len(comm.ins), len(comm.out_shape)

    def kern(*refs):
        (x_ref, w_ref), cins, (o_ref,), couts, csems = _split_refs(refs, (2, nci, 1, nco, len(comm.sems)))
        place = _place()

        @pl.when(pl.program_id(0) == 0)
        def _():
            comm.start(place, cins, couts, csems)

        o_ref[...] = _rms(x_ref[...], w_ref[...]).astype(o_ref.dtype)

        @pl.when(pl.program_id(0) == n - 1)
        def _():
            comm.mid(place, cins, couts, csems)
            comm.finish(place, cins, couts, csems)

    tile = pl.BlockSpec((t, d), lambda i: (i, 0))
    res = pl.pallas_call(
        kern, grid=(n,), in_specs=[tile, pl.BlockSpec(w.shape, lambda i: (0, 0))] + [ANY] * nci,
        out_specs=[tile] + [ANY] * nco, out_shape=[jax.ShapeDtypeStruct((s, d), BF16)] + comm.out_shape,
        scratch_shapes=comm.sems,
        compiler_params=pltpu.CompilerParams(dimension_semantics=("arbitrary",), vmem_limit_bytes=VMEM_LIMIT),
        name=name)(x, w, *comm.ins)
    return res[0], res[1:]


def _norm_bwd(x, w, d_out, add, name):
    def body(r, c):
        _, vjp = jax.vjp(_rms, r[0], c[0])
        dx, dw = vjp(r[1])
        return [dx + r[2]], [dw]

    return _rows_call(body, [_row(x), _row(d_out), _row(add)], [w], [(x.shape[1], F32)], [w.shape], name=name)


CONV_HALO = BF16_ROWS


def _conv_specs(s, f, t):
    n8 = t // CONV_HALO
    cur = pl.BlockSpec((None, t, f), lambda j, i: (j, i, 0))
    prev = pl.BlockSpec((None, CONV_HALO, f), lambda j, i: (j, jnp.maximum(i * n8 - 1, 0), 0))
    nxt = pl.BlockSpec((None, CONV_HALO, f), lambda j, i: (j, jnp.minimum((i + 1) * n8, s // CONV_HALO - 1), 0))
    cw = pl.BlockSpec((None, 3, f), lambda j, i: (j, 0, 0))
    cb = pl.BlockSpec((None, 1, f), lambda j, i: (j, 0, 0))
    return cur, prev, nxt, cw, cb


def _conv_taps(g, prev, first):
    ext = jnp.concatenate([jnp.where(first, 0.0, prev.astype(F32)), g], axis=0)
    return pltpu.roll(ext, 1, 0)[CONV_HALO:], pltpu.roll(ext, 2, 0)[CONV_HALO:]


def _conv_fwd(gg, uu, cw, cb):
    _, s, f = gg.shape
    t = _tile(s, 512)

    def kern(g_ref, gp_ref, u_ref, cw_ref, cb_ref, o_ref):
        g = g_ref[...].astype(F32)
        g1, g2 = _conv_taps(g, gp_ref[...], pl.program_id(1) == 0)
        w = cw_ref[...]
        gc = cb_ref[...] + w[0:1] * g2 + w[1:2] * g1 + w[2:3] * g
        o_ref[...] = (_silu(gc) * u_ref[...].astype(F32)).astype(o_ref.dtype)

    cur, prev, _, cws, cbs = _conv_specs(s, f, t)
    return pl.pallas_call(
        kern, grid=(4, s // t), in_specs=[cur, prev, cur, cws, cbs], out_specs=cur,
        out_shape=jax.ShapeDtypeStruct(gg.shape, BF16),
        compiler_params=pltpu.CompilerParams(dimension_semantics=("parallel", "parallel"), vmem_limit_bytes=VMEM_LIMIT),
        name="ffn_conv_fwd")(gg, gg, uu, cw, cb)


def _conv_bwd(gg, uu, dact, cw, cb):
    _, s, f = gg.shape
    t = _tile(s, 512)
    nt = s // t

    def kern(g_ref, gp_ref, gn_ref, u_ref, un_ref, da_ref, dan_ref, cw_ref, cb_ref, du_ref, dg_ref, dcw_ref, dcb_ref):
        i = pl.program_id(1)
        cat = lambda a_ref, b_ref: jnp.concatenate([a_ref[...].astype(F32), b_ref[...].astype(F32)], axis=0)
        g, u, da = cat(g_ref, gn_ref), cat(u_ref, un_ref), cat(da_ref, dan_ref)
        g1, g2 = _conv_taps(g, gp_ref[...], i == 0)
        w = cw_ref[...]
        gc = cb_ref[...] + w[0:1] * g2 + w[1:2] * g1 + w[2:3] * g
        sg = jax.nn.sigmoid(gc)
        du_ref[...] = (da[:t] * (gc[:t] * sg[:t])).astype(du_ref.dtype)
        row = lax.broadcasted_iota(jnp.int32, (t + CONV_HALO, 1), 0)
        dgc = jnp.where((row < t) | (i < nt - 1), da * u * (sg * (1.0 + gc * (1.0 - sg))), 0.0)
        up1 = pltpu.roll(dgc, t + CONV_HALO - 1, 0)[:t]
        up2 = pltpu.roll(dgc, t + CONV_HALO - 2, 0)[:t]
        dgc = dgc[:t]
        dg_ref[...] = (w[2:3] * dgc + w[1:2] * up1 + w[0:1] * up2).astype(dg_ref.dtype)

        @pl.when(i == 0)
        def _():
            dcw_ref[...] = jnp.zeros_like(dcw_ref)
            dcb_ref[...] = jnp.zeros_like(dcb_ref)

        dcw_ref[0:1, :] += jnp.sum(dgc * g2[:t], axis=0, keepdims=True)
        dcw_ref[1:2, :] += jnp.sum(dgc * g1[:t], axis=0, keepdims=True)
        dcw_ref[2:3, :] += jnp.sum(dgc * g[:t], axis=0, keepdims=True)
        dcb_ref[...] += jnp.sum(dgc, axis=0, keepdims=True)

    cur, prev, nxt, cws, cbs = _conv_specs(s, f, t)
    return pl.pallas_call(
        kern, grid=(4, nt), in_specs=[cur, prev, nxt, cur, nxt, cur, nxt, cws, cbs], out_specs=[cur, cur, cws, cbs],
        out_shape=[jax.ShapeDtypeStruct(gg.shape, BF16), jax.ShapeDtypeStruct(gg.shape, BF16),
                   jax.ShapeDtypeStruct(cw.shape, F32), jax.ShapeDtypeStruct(cb.shape, F32)],
        compiler_params=pltpu.CompilerParams(dimension_semantics=("parallel", "arbitrary"), vmem_limit_bytes=VMEM_LIMIT),
        name="ffn_conv_bwd")(gg, gg, gg, uu, uu, dact, dact, cw, cb)


def _rope_tables(pos):
    half = MLA_ROPE // 2
    inv = ROPE_THETA ** (-jnp.arange(half, dtype=F32) / half)
    ang = pos.astype(F32)[:, None] * inv
    cos, sin = jnp.cos(ang), jnp.sin(ang)
    s = pos.shape[0]
    z = lambda w: jnp.zeros((s, w), F32)
    c = jnp.concatenate([jnp.ones((s, MLA_NOPE), F32), cos, cos, jnp.ones((s, LANES - MLA_QK), F32)], axis=1)
    sa = jnp.concatenate([z(MLA_NOPE), -sin, z(half), z(LANES - MLA_QK)], axis=1)
    sb = jnp.concatenate([z(MLA_NOPE), z(half), sin, z(LANES - MLA_QK)], axis=1)
    return c, sa, sb


def _local_step(x, mem, pos, target, rep, early_shards, late_shards):
    g = {}
    c, sa, sb = _rope_tables(pos)

    xn, gathered = _norm_fwd_comm(x, rep["norm_mix"], _gather_plan(early_shards), "norm_mix_fwd_gather")
    w = _early_layout(dict(zip(EARLY, gathered, strict=True)), rep)

    def proj_fn(r, rows, k):
        la_ = _gate_fn(r[:, P_ALR:P_ALR + 128], k[0], k[1])
        return [r, la_, _rms(r[:, P_CQ:P_CQ + MLA_Q_RANK], k[2]), _rms(r[:, P_CKV:P_CKV + MLA_KV_RANK], k[3])], []

    proj, la, q_lat, kv_lat = _matmul(
        xn, w["in"], "nn", F32, "proj_fwd", epilogue=_Epilogue(
            proj_fn, [], [w["w2"], w["gate_b"], w["q_a_norm"], w["kv_a_norm"]],
            [(P_WIDTH, F32), (256, F32), (MLA_Q_RANK, BF16), (MLA_KV_RANK, BF16)], []))
    alr = _row(proj, 128, P_ALR // 128)
    kpe = _row(proj, 128, P_KPE // 128)
    og = _row(proj, 512, P_OG // 512)
    cq = _row(proj, 256, P_CQ // 256)
    ckv = _row(proj, 128, P_CKV // 128)

    o_gla, states = _gla_fwd(proj, la)

    q_up = _matmul(q_lat, w["uq"], "nn", F32, "mla_q_fwd")
    k_up = _matmul(kv_lat, w["k"], "nn", F32, "mla_k_fwd")
    v_mla = _matmul(kv_lat, w["v"], "nn", BF16, "mla_v_fwd")

    def qk_body(r, k):
        qs, ks = [], []
        for qh, kh in zip(_heads(r[0], MLA_HEADS), _heads(r[1], MLA_HEADS)):
            a, b = _qk_head(qh, kh, r[2], r[3], r[4], r[5], k[0], k[1])
            qs.append(a)
            ks.append(b)
        return [_cat(qs), _cat(ks)], []

    tabs = [_row(c), _row(sa), _row(sb)]
    q_r, k_r = _rows_call(qk_body, [_row(q_up), _row(k_up), kpe] + tabs, [w["q_norm"], w["k_norm"]],
                          [(1024, BF16), (1024, BF16)], name="mla_qk_fwd")
    o_mla, lse, gathered = _attn_fwd(q_r, k_r, v_mla, _gather_plan(late_shards))
    w.update(_late_layout(dict(zip(LATE, gathered, strict=True))))

    def mix_body(r, k):
        ys = [_mix_head(o, g_, k[0]) for o, g_ in zip(_heads(r[0], GLA_HEADS), _heads(r[1], GLA_HEADS))]
        return [_cat(ys + [r[2]])], []

    cat = _rows_call(mix_body, [_row(o_gla), og, _row(o_mla)], [w["gla_out_norm"]], [(1024, BF16)],
                     name="mix_fwd")[0]
    h1, hn = _matmul(cat, w["out"], "nn", F32, "out_fwd_norm", residual=x, epilogue=_norm_fwd_epilogue(w["norm_xa"]))
    mn = _norm_fwd(mem, w["norm_mem"], "norm_mem_fwd")
    xkv = _matmul(mn, w["xkv"], "nn", F32, "xa_kv_fwd")

    def xa_fn(r, rows, k):
        ks, vs = _heads(k[0], 2 * XA_HEADS)[:XA_HEADS], _heads(k[0], 2 * XA_HEADS)[XA_HEADS:]
        return [r, _cat([_xa_head(a, b, v_, k[1], k[2]) for a, b, v_ in zip(_heads(r, XA_HEADS), ks, vs)])], []

    xq, xo = _matmul(hn, w["xq"], "nn", F32, "xa_q_fwd_attn", epilogue=_Epilogue(
        xa_fn, [], [xkv, w["xa_q_norm"], w["xa_k_norm"]], [(512, F32), (512, BF16)], []))
    h2, fn = _matmul(xo, w["xo"], "nn", F32, "xa_o_fwd_norm", residual=h1, epilogue=_norm_fwd_epilogue(w["norm_ffn"]))
    gg = _matmul(fn, w["wg"], "nt", BF16, "ffn_gate_fwd", b_lead="p")
    uu = _matmul(fn, w["wu"], "nt", BF16, "ffn_up_fwd", b_lead="p")
    act = _conv_fwd(gg, uu, w["cw"], w["cb"])
    def loss_fn(y, rows, consts):
        err = y - rows[0]
        part = 0.5 * jnp.sum(jnp.sum(err * err, axis=1, keepdims=True) * (1.0 / D_MODEL), axis=0, keepdims=True)
        return [err * (1.0 / D_MODEL)], [jnp.broadcast_to(part, (1, LANES))]

    dy, loss = _matmul(act, w["wd"], "nn", F32, "ffn_down_fwd_loss", residual=h2, a_lead="k", b_lead="k",
                       epilogue=_Epilogue(loss_fn, [_row(target)], [], [(D_MODEL, F32)], [(1, LANES)]))

    g["ffn_w_down"] = _matmul(act, dy, "tn", BF16, "ffn_down_dw", a_lead="p")
    dact = _matmul(dy, w["wd"], "nt", BF16, "ffn_down_dx", b_lead="p")
    duu, dgg, g["ffn_conv_w"], g["ffn_conv_b"] = _conv_bwd(gg, uu, dact, w["cw"], w["cb"])
    g["ffn_w_gate"] = _matmul(dgg, fn, "tn", BF16, "ffn_gate_dw", a_lead="p")
    g["ffn_w_up"] = _matmul(duu, fn, "tn", BF16, "ffn_up_dw", a_lead="p")
    dh2, g["norm_ffn"] = _matmul(dgg, w["wg"], "nn", F32, "ffn_dx_norm_bwd", a_lead="k", b_lead="k", more=(duu, w["wu"]),
                                 epilogue=_norm_bwd_epilogue(h2, w["norm_ffn"], dy))

    g["xa_w_o"] = _matmul(xo, dh2, "tn", BF16, "xa_o_dw")
    def xa_bwd(dxo_, rows, k):
        kvh = _heads(k[0], 2 * XA_HEADS)
        dq_, dk_, dv_ = [], [], []
        dqn, dkn = 0.0, 0.0
        for h, (a, d_) in enumerate(zip(_heads(rows[0], XA_HEADS), _heads(dxo_, XA_HEADS))):
            _, vjp = jax.vjp(_xa_head, a, kvh[h], kvh[XA_HEADS + h], k[1], k[2])
            ga, gk, gv, gqn, gkn = vjp(d_)
            dq_.append(ga)
            dk_.append(gk)
            dv_.append(gv)
            dqn, dkn = dqn + gqn, dkn + gkn
        return [_cat(dq_)], [_cat(dk_ + dv_), dqn, dkn]

    dxq, dxkv, g["xa_q_norm"], g["xa_k_norm"] = _matmul(dh2, w["xo"], "nt", F32, "xa_o_dx_attn_bwd", epilogue=_Epilogue(
        xa_bwd, [_row(xq)], [xkv, w["xa_q_norm"], w["xa_k_norm"]], [(512, BF16)], [xkv.shape, (1, 128), (1, 128)]))
    g["xa_w_q"] = _matmul(hn, dxq, "tn", BF16, "xa_q_dw")
    dh1, g["norm_xa"] = _matmul(dxq, w["xq"], "nt", F32, "xa_q_dx_norm_bwd",
                                epilogue=_norm_bwd_epilogue(h1, w["norm_xa"], dh2))
    g["xa_w_kv"] = _matmul(mn, dxkv, "tn", BF16, "xa_kv_dw")
    dmn = _matmul(dxkv, w["xkv"], "nt", F32, "xa_kv_dx")
    _, g["norm_mem"] = _norm_bwd(mem, w["norm_mem"], dmn, dmn, "norm_mem_bwd")

    g["w_out"] = _matmul(cat, dh1, "tn", BF16, "out_dw")
    def mix_bwd(dcat_, rows, k):
        do_, dog_ = [], []
        dgn = 0.0
        for o, g_, d_ in zip(_heads(rows[0], GLA_HEADS), _heads(rows[1], GLA_HEADS), _heads(dcat_, GLA_HEADS)):
            _, vjp = jax.vjp(_mix_head, o, g_, k[0])
            a, b, gn_ = vjp(d_)
            do_.append(a)
            dog_.append(b)
            dgn = dgn + gn_
        return [_cat(do_), _cat(dog_), dcat_[:, 512:]], [dgn]

    do_gla, d_og, do_mla, g["gla_out_norm"] = _matmul(dh1, w["out"], "nt", F32, "out_dx_mix_bwd", epilogue=_Epilogue(
        mix_bwd, [_row(o_gla), og], [w["gla_out_norm"]], [(512, F32), (512, BF16), (512, F32)], [(1, 128)]))

    late_parts = _late_grad_shards(g)
    dq_r, dk_r, dv_mla, lands_late = _attn_bwd(q_r, k_r, v_mla, o_mla, lse, do_mla,
                                               _scatter_plan([late_parts[n] for n in LATE]))
    lands_late = dict(zip(LATE, lands_late, strict=True))

    def qk_bwd(r, k):
        dqs, dks = [], []
        dkpe, dqn, dkn = 0.0, 0.0, 0.0
        for qh, kh, dqh, dkh in zip(_heads(r[0], MLA_HEADS), _heads(r[1], MLA_HEADS), _heads(r[6], MLA_HEADS),
                                    _heads(r[7], MLA_HEADS)):
            _, vjp = jax.vjp(lambda a, b, e, f, h_: _qk_head(a, b, e, r[3], r[4], r[5], f, h_), qh, kh, r[2], k[0], k[1])
            ga, gb, ge, gf, gh = vjp((dqh, dkh))
            dqs.append(ga)
            dks.append(gb)
            dkpe, dqn, dkn = dkpe + ge, dqn + gf, dkn + gh
        return [_cat(dqs), _cat(dks), dkpe], [dqn, dkn]

    dq_up, dk_up, d_kpe, g["q_norm"], g["k_norm"] = _rows_call(
        qk_bwd, [_row(q_up), _row(k_up), kpe] + tabs + [_row(dq_r), _row(dk_r)], [w["q_norm"], w["k_norm"]],
        [(1024, BF16), (1024, BF16), (128, BF16)], [(1, 128), (1, 128)], name="mla_qk_bwd")
    g["uq"] = _matmul(q_lat, dq_up, "tn", BF16, "mla_q_dw")
    dq_lat = _matmul(dq_up, w["uq"], "nt", F32, "mla_q_dx")
    g["k"] = _matmul(kv_lat, dk_up, "tn", BF16, "mla_k_dw")
    g["v"] = _matmul(kv_lat, dv_mla, "tn", BF16, "mla_v_dw")
    dkv_lat = _matmul(dk_up, w["k"], "nt", F32, "mla_k_dx")
    dkv_lat = _matmul(dv_mla, w["v"], "nt", F32, "mla_v_dx", residual=dkv_lat)

    dgq, dgk, dla, dgv, _ = _gla_bwd(proj, la, states, do_gla, _Comm([], [], [], lambda *args: None, lambda *args: None))

    def dproj_body(r, k):
        alr_, cq_, ckv_, dla_, dq_lat_, dkv_lat_, dgq_, dgk_, dgv_, d_og_, d_kpe_ = r
        _, gate_vjp = jax.vjp(_gate_fn, alr_, k[0], k[1])
        d_alr, gw2, gb = gate_vjp(dla_)
        _, q_vjp = jax.vjp(_rms, cq_, k[2])
        _, kv_vjp = jax.vjp(_rms, ckv_, k[3])
        d_cq, gqa = q_vjp(dq_lat_)
        d_ckv, gkva = kv_vjp(dkv_lat_)
        pieces = [dgq_, dgk_, dgv_, d_og_, d_cq, d_ckv, d_kpe_, d_alr]
        return [_cat([x_.astype(BF16) for x_ in pieces])], [gw2, gb, gqa, gkva]

    dproj, g["w2"], g["gla_gate_b"], g["mla_q_a_norm"], g["mla_kv_a_norm"] = _rows_call(
        dproj_body, [alr, cq, ckv, _row(dla), _row(dq_lat), _row(dkv_lat), _row(dgq), _row(dgk), _row(dgv), _row(d_og),
                     _row(d_kpe)], [w["w2"], w["gate_b"], w["q_a_norm"], w["kv_a_norm"]], [(P_WIDTH, BF16)],
        [(128, 256), (1, 256), (1, 256), (1, 128)], name="proj_cotangent")
    g["in"] = _matmul(xn, dproj, "tn", BF16, "proj_dw")
    dx, g["norm_mix"] = _matmul(dproj, w["in"], "nt", F32, "proj_dx_norm_bwd",
                                epilogue=_norm_bwd_epilogue(x, w["norm_mix"], dh1))
    return loss[0, 0], dx, g, lands_late


def _join_shards(pieces, axis):
    if axis == 0:
        return pieces.reshape(-1, pieces.shape[2])
    return jnp.transpose(pieces, (1, 0, 2)).reshape(pieces.shape[1], -1)


def _split_shards(full, axis):
    r, c = full.shape
    if axis == 0:
        return full.reshape(4, r // 4, c)
    return jnp.transpose(full.reshape(r, 4, c // 4), (1, 0, 2))


def _early_layout(gath, rep):
    w_in = _join_shards(gath["w_in"], 1)
    z = lambda n: jnp.zeros((D_MODEL, n), w_in.dtype)
    seg = lambda lo, n: w_in[:, lo:lo + n]
    ukv = _join_shards(gath["mla_w_ukv"], 1).reshape(MLA_KV_RANK, MLA_HEADS, MLA_NOPE + MLA_V)
    w = {
        "in": jnp.concatenate([seg(N_GQ, 256), seg(N_GK, 256), seg(N_GV, 512), seg(N_OG, 512), seg(N_CQ, 256),
                               seg(N_CKV, 128), z(64), seg(N_KPE, 32), z(32), seg(N_ALR, 16), z(112)], axis=1),
        "uq": jnp.pad(_join_shards(gath["mla_w_uq"], 1).reshape(MLA_Q_RANK, MLA_HEADS, MLA_QK),
                      ((0, 0), (0, 0), (0, LANES - MLA_QK))).reshape(MLA_Q_RANK, MLA_HEADS * LANES),
        "k": jnp.pad(ukv[:, :, :MLA_NOPE], ((0, 0), (0, 0), (0, LANES - MLA_NOPE))).reshape(MLA_KV_RANK, -1),
        "v": ukv[:, :, MLA_NOPE:].reshape(MLA_KV_RANK, MLA_HEADS * MLA_V),
        "w2": jnp.pad(_join_shards(gath["gla_gate_w2"], 1), ((0, LANES - GLA_RANK), (0, 0))),
        "cb": rep["ffn_conv_b"].reshape(4, 1, D_FF // 4),
        "q_norm": jnp.pad(rep["mla_q_norm"], ((0, 0), (0, LANES - MLA_QK))),
        "k_norm": jnp.pad(rep["mla_k_norm"], ((0, 0), (0, LANES - MLA_QK))),
        "q_a_norm": rep["mla_q_a_norm"], "kv_a_norm": rep["mla_kv_a_norm"], "gate_b": rep["gla_gate_b"],
    }
    for n in ("norm_mix", "gla_out_norm", "norm_xa", "norm_mem", "xa_q_norm", "xa_k_norm", "norm_ffn"):
        w[n] = rep[n]
    return w


def _late_layout(gath):
    return {"out": _join_shards(gath["w_out"], 0), "xq": _join_shards(gath["xa_w_q"], 0),
            "xkv": _join_shards(gath["xa_w_kv"], 0), "xo": _join_shards(gath["xa_w_o"], 1),
            "wg": gath["ffn_w_gate"], "wu": gath["ffn_w_up"], "wd": gath["ffn_w_down"], "cw": gath["ffn_conv_w"]}


def _late_grad_shards(g):
    sh = {"w_out": _split_shards(g["w_out"], 0), "xa_w_q": _split_shards(g["xa_w_q"], 0),
          "xa_w_kv": _split_shards(g["xa_w_kv"], 0), "xa_w_o": _split_shards(g["xa_w_o"], 1),
          "ffn_w_gate": g["ffn_w_gate"], "ffn_w_up": g["ffn_w_up"], "ffn_conv_w": g["ffn_conv_w"],
          "ffn_w_down": g["ffn_w_down"]}
    return {n: v.astype(BF16) for n, v in sh.items()}


def _early_grad_shards(g):
    gi = g["in"]
    seg = lambda lo, n: gi[:, lo:lo + n]
    w_in = jnp.concatenate([seg(P_GQ, 256), seg(P_GK, 256), seg(P_GV, 512), seg(P_ALR, 16), seg(P_OG, 512),
                            seg(P_CQ, 256), seg(P_CKV, 128), seg(P_KPE + 64, 32)], axis=1)
    uq = g["uq"].reshape(MLA_Q_RANK, MLA_HEADS, LANES)[:, :, :MLA_QK].reshape(MLA_Q_RANK, -1)
    ukv = jnp.concatenate([g["k"].reshape(MLA_KV_RANK, MLA_HEADS, LANES)[:, :, :MLA_NOPE],
                           g["v"].reshape(MLA_KV_RANK, MLA_HEADS, MLA_V)], axis=2).reshape(MLA_KV_RANK, -1)
    sh = {"w_in": _split_shards(w_in, 1), "gla_gate_w2": _split_shards(g["w2"][:GLA_RANK], 1),
          "mla_w_uq": _split_shards(uq, 1), "mla_w_ukv": _split_shards(ukv, 1)}
    sh = {n: v.astype(BF16) for n, v in sh.items()}
    rep = {n: g[n] for n in REPLICATED if n in g}
    rep["mla_q_norm"] = g["q_norm"][:, :MLA_QK]
    rep["mla_k_norm"] = g["k_norm"][:, :MLA_QK]
    rep["ffn_conv_b"] = g["ffn_conv_b"].reshape(1, D_FF)
    return sh, rep


SMALL_SHAPE = (8, 1024)


def _pack_small(vectors):
    flat = jnp.concatenate(vectors, axis=1)
    return jnp.pad(flat, ((0, 0), (0, SMALL_SHAPE[0] * SMALL_SHAPE[1] - flat.shape[1]))).reshape(SMALL_SHAPE)


def _unpack_small(buf, widths):
    flat = buf.reshape(1, -1)
    out, off = [], 0
    for wd in widths:
        out.append(flat[:, off:off + wd])
        off += wd
    return out


ANY = pl.BlockSpec(memory_space=pl.ANY)


def _place():
    x, y, c = lax.axis_index("x"), lax.axis_index("y"), lax.axis_index("c")
    chips = [(1 - x, y), (x, 1 - y), (1 - x, 1 - y)]
    return x, y, c, chips


class _Comm:
    def __init__(self, ins, out_shape, sems, start, finish, mid=None):
        self.ins, self.out_shape, self.sems = list(ins), list(out_shape), list(sems)
        self.start, self.finish, self.mid = start, finish, mid or (lambda *args: None)


def _run_comm(plan, name):
    ni, no = len(plan.ins), len(plan.out_shape)

    def body(*refs):
        ins, outs, sems = refs[:ni], refs[ni:ni + no], refs[ni + no:]
        place = _place()
        plan.start(place, ins, outs, sems)
        plan.mid(place, ins, outs, sems)
        plan.finish(place, ins, outs, sems)

    return pl.pallas_call(body, in_specs=[ANY] * ni, out_specs=[ANY] * no, out_shape=plan.out_shape,
                          scratch_shapes=plan.sems, name=name)(*plan.ins)


def _gather_plan(shards):
    n = len(shards)
    split = [s.shape[0] % (2 * BF16_ROWS) == 0 for s in shards]

    def rows(ref, t, c):
        if not split[t]:
            return ref
        half = shards[t].shape[0] // 2
        return ref.at[pl.ds(pl.multiple_of(c * half, BF16_ROWS), half)]

    def remote(src, dst, ss, rs, to):
        return pltpu.make_async_remote_copy(src_ref=src, dst_ref=dst, send_sem=ss, recv_sem=rs, device_id=to,
                                            device_id_type=MESH)

    def first_wave(place, ins, outs, sems):
        x, y, c, chips = place
        ici_s, ici_r, _, _, local = sems
        me = 2 * x + y
        own = [pltpu.make_async_copy(ins[t], outs[t].at[me], local.at[t]) for t in range(n)]
        push = [remote(rows(ins[t], t, c), rows(outs[t].at[me], t, c), ici_s.at[3 * t + j], ici_r.at[3 * t + j], (px, py, c))
                for t in range(n) for j, (px, py) in enumerate(chips)]
        return own, push

    def second_wave(place, ins, outs, sems, last):
        x, y, c, chips = place
        ici_s, ici_r, d2d_s, d2d_r, local = sems
        sib = (x, y, 1 - c)
        out = []
        for t in range(n):
            for j, (px, py) in enumerate(chips):
                block = outs[t].at[2 * px + py]
                got = rows(block, t, c)
                if split[t]:
                    hand = remote(got, got, d2d_s.at[3 * t + j], d2d_r.at[3 * t + j], sib)
                    theirs = rows(block, t, 1 - c)
                    other = (remote(theirs, theirs, local.at[0], d2d_r.at[3 * t + j], sib) if last else
                             remote(got, got, local.at[0], ici_r.at[3 * t + j], sib))
                    out.append((other, hand))
                elif last:
                    out.append((remote(got, got, local.at[0], ici_r.at[3 * t + j], sib), None))
        return out

    def start(place, ins, outs, sems):
        own, push = first_wave(place, ins, outs, sems)
        for cp in own + push:
            cp.start()

    def mid(place, ins, outs, sems):
        for arrival, hand in second_wave(place, ins, outs, sems, False):
            arrival.wait_recv()
            hand.start()

    def finish(place, ins, outs, sems):
        own, push = first_wave(place, ins, outs, sems)
        for arrival, hand in second_wave(place, ins, outs, sems, True):
            arrival.wait_recv()
            if hand is not None:
                hand.wait_send()
        for cp in push:
            cp.wait_send()
        for cp in own:
            cp.wait()

    dma = pltpu.SemaphoreType.DMA
    return _Comm(shards, [jax.ShapeDtypeStruct((4,) + s.shape, s.dtype) for s in shards],
                 [dma((3 * n,)), dma((3 * n,)), dma((3 * n,)), dma((3 * n,)), dma((n,))], start, finish, mid)


def _scatter_plan(parts, small=None):
    n = len(parts)
    ns = 0 if small is None else 1

    def unpack(place, ins, outs, sems):
        x, y, c, chips = place
        return x, y, c, chips, 2 * x + y, 4 * x + 2 * y + c, (x, y, 1 - c)

    def remote(src, dst, ss, rs, to):
        return pltpu.make_async_remote_copy(src_ref=src, dst_ref=dst, send_sem=ss, recv_sem=rs, device_id=to,
                                            device_id_type=MESH)

    def first_wave(place, ins, outs, sems):
        x, y, c, chips, me, dev, sib = unpack(place, ins, outs, sems)
        ici_s, ici_r, d2d_s, d2d_r, sm_s, sm_r, local = sems
        own, push = [], []
        if ns:
            own.append(pltpu.make_async_copy(ins[n], outs[n].at[dev], local.at[n]))
            for k in range(1, 8):
                px = (1 - x) if (k >> 2) & 1 else x
                py = (1 - y) if (k >> 1) & 1 else y
                pc = (1 - c) if k & 1 else c
                push.append(remote(ins[n], outs[n].at[dev], sm_s.at[k - 1], sm_r.at[k - 1], (px, py, pc)))
        for t in range(n):
            own.append(pltpu.make_async_copy(ins[t].at[me], outs[t].at[dev], local.at[t]))
            push.append(remote(ins[t].at[me], outs[t].at[dev], d2d_s.at[4 * t], d2d_r.at[4 * t], sib))
            for j, (px, py) in enumerate(chips):
                push.append(remote(ins[t].at[2 * px + py], outs[t].at[dev], ici_s.at[3 * t + j], ici_r.at[3 * t + j],
                                   (px, py, c)))
        return own, push

    def start(place, ins, outs, sems):
        own, push = first_wave(place, ins, outs, sems)
        for cp in own + push:
            cp.start()

    def landed(dst, rs, sems, sib):
        remote(dst, dst, sems[-1].at[0], rs, sib).wait_recv()

    def forwards(place, ins, outs, sems):
        x, y, c, chips, me, dev, sib = unpack(place, ins, outs, sems)
        d2d_s, d2d_r = sems[2], sems[3]
        slots = [(t, j, outs[t].at[4 * px + 2 * py + c]) for t in range(n) for j, (px, py) in enumerate(chips)]
        return [(t, j, slot, remote(slot, slot, d2d_s.at[4 * t + 1 + j], d2d_r.at[4 * t + 1 + j], sib))
                for t, j, slot in slots]

    def mid(place, ins, outs, sems):
        sib = unpack(place, ins, outs, sems)[-1]
        for t, j, slot, cp in forwards(place, ins, outs, sems):
            landed(slot, sems[1].at[3 * t + j], sems, sib)
            cp.start()

    def finish(place, ins, outs, sems):
        x, y, c, chips, me, dev, sib = unpack(place, ins, outs, sems)
        d2d_r, sm_r = sems[3], sems[5]
        own, push = first_wave(place, ins, outs, sems)
        push += [cp for _, _, _, cp in forwards(place, ins, outs, sems)]
        for t in range(n):
            landed(outs[t].at[4 * x + 2 * y + (1 - c)], d2d_r.at[4 * t], sems, sib)
            for j, (px, py) in enumerate(chips):
                landed(outs[t].at[4 * px + 2 * py + (1 - c)], d2d_r.at[4 * t + 1 + j], sems, sib)
        if ns:
            for k in range(1, 8):
                px = (1 - x) if (k >> 2) & 1 else x
                py = (1 - y) if (k >> 1) & 1 else y
                pc = (1 - c) if k & 1 else c
                landed(outs[n].at[4 * px + 2 * py + pc], sm_r.at[k - 1], sems, sib)
        for cp in push:
            cp.wait_send()
        for cp in own:
            cp.wait()

    dma = pltpu.SemaphoreType.DMA
    ins = list(parts) + ([small] if ns else [])
    out_shape = [jax.ShapeDtypeStruct((8,) + p.shape[1:], p.dtype) for p in parts]
    if ns:
        out_shape.append(jax.ShapeDtypeStruct((8,) + small.shape, small.dtype))
    return _Comm(ins, out_shape, [dma((3 * n,)), dma((3 * n,)), dma((4 * n,)), dma((4 * n,)), dma((7,)), dma((7,)),
                                  dma((n + 1,))], start, finish, mid)


ADAM_ROWS = 288


def _row_tile(r, cap):
    if r <= cap:
        return r
    return max(t for t in range(8, cap + 1, 8) if r % t == 0)


def _adamw_update(w, m, v, land):
    g = land[0].astype(F32)
    for i in range(1, 8):
        g = g + land[i].astype(F32)
    m_new = ADAM_B1 * m + (1.0 - ADAM_B1) * g
    v_new = ADAM_B2 * v + (1.0 - ADAM_B2) * (g * g)
    m_hat = m_new / (1.0 - ADAM_B1 ** ADAM_STEP)
    v_hat = v_new / (1.0 - ADAM_B2 ** ADAM_STEP)
    return g, -ADAM_LR * (m_hat / (jnp.sqrt(v_hat) + ADAM_EPS) + ADAM_WD * w), m_new, v_new


def _adamw(tensors, name, comm=None):
    k = len(tensors)
    r, c = tensors[0][0].shape
    t = _row_tile(r, ADAM_ROWS // k)
    n = r // t
    nci, nco, nsem = (len(comm.ins), len(comm.out_shape), len(comm.sems)) if comm else (0, 0, 0)

    def kern(*refs):
        ins, cins, outs, couts, csems = _split_refs(refs, (4 * k, nci, 4 * k, nco, nsem))
        if comm:
            place = _place()

            @pl.when(pl.program_id(0) == 0)
            def _():
                comm.start(place, cins, couts, csems)

        for i in range(k):
            w_ref, m_ref, v_ref, l_ref = ins[4 * i:4 * i + 4]
            res = _adamw_update(w_ref[...], m_ref[...], v_ref[...], l_ref)
            for ref, val in zip(outs[4 * i:4 * i + 4], res, strict=True):
                ref[...] = val
        if comm:
            @pl.when(pl.program_id(0) == n - 1)
            def _():
                comm.mid(place, cins, couts, csems)
                comm.finish(place, cins, couts, csems)

    spec = pl.BlockSpec((t, c), lambda i: (i, 0))
    lspec = pl.BlockSpec((8, t, c), lambda i: (0, i, 0))
    res = pl.pallas_call(
        kern, grid=(n,), in_specs=[spec, spec, spec, lspec] * k + [ANY] * nci, out_specs=[spec] * (4 * k) + [ANY] * nco,
        out_shape=[jax.ShapeDtypeStruct((r, c), F32)] * (4 * k) + (comm.out_shape if comm else []),
        scratch_shapes=comm.sems if comm else [],
        compiler_params=pltpu.CompilerParams(dimension_semantics=("arbitrary" if comm else "parallel",),
                                             vmem_limit_bytes=VMEM_LIMIT),
        name=name)(*[x for tens in tensors for x in tens], *(comm.ins if comm else []))
    return [res[4 * i:4 * i + 4] for i in range(k)], res[4 * k:]


def _step(a):
    def sq(n):
        v = a[n][0] if a[n].ndim == 3 else a[n]
        return v.T if n.removeprefix("m_").removeprefix("v_") in TRANSPOSED else v

    payload = lambda n: sq(n) if n in EXACT_GATHER else sq(n).astype(BF16)

    loss, dx, g, lands_late = _local_step(sq("x"), sq("mem"), a["positions"][0], sq("loss_target"),
                                          {n: a[n] for n in REPLICATED}, [payload(n) for n in EARLY],
                                          [payload(n) for n in LATE])

    sh, rep = _early_grad_shards(g)
    small = _pack_small([rep[n] for n in REPLICATED] + [loss.reshape(1, 1)])
    *lands_early, land_small = _run_comm(_scatter_plan([sh[n] for n in EARLY], small), "scatter_last")
    quad = lambda n, land: (sq(n), sq("m_" + n), sq("v_" + n), land)
    lands = dict(zip(EARLY, lands_early, strict=True)) | lands_late

    outs = {}
    kinds = ("grad_", "delta_", "new_m_", "new_v_")
    for n, _ in SHARDED:
        res = _adamw([quad(n, lands[n])], "adamw_" + n)[0][0]
        for kind, val in zip(kinds, res, strict=True):
            outs[kind + n] = (val.T if n in TRANSPOSED else val).reshape(a[n].shape)
    zero = jnp.zeros((1, 1), F32)
    packed = [_pack_small([a[p + n] for n in REPLICATED] + [zero]) for p in ("", "m_", "v_")]
    res = _adamw([(*packed, land_small)], "adamw_replicated")[0][0]
    widths = [a[n].shape[1] for n in REPLICATED] + [1]
    for kind, buf in zip(kinds, res, strict=True):
        *vals, total = _unpack_small(buf, widths)
        for n, val in zip(REPLICATED, vals, strict=True):
            outs[kind + n] = val
        if kind == "grad_":
            loss = total[0, 0]

    ordered = [outs[kind + n] for kind in kinds for n in WEIGHTS]
    return (loss, dx[None], *ordered)


def kernel(x, mem, positions, norm_mix, w_in, gla_gate_w2, gla_gate_b, gla_out_norm, mla_q_a_norm, mla_w_uq, mla_kv_a_norm, mla_w_ukv, mla_q_norm, mla_k_norm, w_out, norm_xa, norm_mem, xa_w_q, xa_w_kv, xa_q_norm, xa_k_norm, xa_w_o, norm_ffn, ffn_w_gate, ffn_w_up, ffn_conv_w, ffn_conv_b, ffn_w_down, loss_target, m_norm_mix, m_w_in, m_gla_gate_w2, m_gla_gate_b, m_gla_out_norm, m_mla_q_a_norm, m_mla_w_uq, m_mla_kv_a_norm, m_mla_w_ukv, m_mla_q_norm, m_mla_k_norm, m_w_out, m_norm_xa, m_norm_mem, m_xa_w_q, m_xa_w_kv, m_xa_q_norm, m_xa_k_norm, m_xa_w_o, m_norm_ffn, m_ffn_w_gate, m_ffn_w_up, m_ffn_conv_w, m_ffn_conv_b, m_ffn_w_down, v_norm_mix, v_w_in, v_gla_gate_w2, v_gla_gate_b, v_gla_out_norm, v_mla_q_a_norm, v_mla_w_uq, v_mla_kv_a_norm, v_mla_w_ukv, v_mla_q_norm, v_mla_k_norm, v_w_out, v_norm_xa, v_norm_mem, v_xa_w_q, v_xa_w_kv, v_xa_q_norm, v_xa_k_norm, v_xa_w_o, v_norm_ffn, v_ffn_w_gate, v_ffn_w_up, v_ffn_conv_w, v_ffn_conv_b, v_ffn_w_down):
    return _step(dict(locals()))
```

```python
import functools

import jax
import jax.numpy as jnp
import numpy as np
from jax import lax
from jax.experimental import pallas as pl
from jax.experimental.pallas import tpu as pltpu

F32, BF16 = jnp.float32, jnp.bfloat16
MESH = pl.DeviceIdType.MESH

D_MODEL = 1024
EPS = 1e-6
GLA_HEADS, GLA_DK, GLA_DV, GLA_RANK, GLA_CHUNK = 4, 64, 128, 16, 64
GLA_GATE_NORM = 16.0
MLA_HEADS, MLA_Q_RANK, MLA_KV_RANK, MLA_NOPE, MLA_ROPE, MLA_V = 8, 256, 128, 64, 32, 64
MLA_QK = MLA_NOPE + MLA_ROPE
ROPE_THETA = 10000.0
LOG2E, LN2 = 1.4426950408889634, 0.6931471805599453
XA_HEADS, XA_DIM = 4, 128
D_FF = 2816
ADAM_LR, ADAM_B1, ADAM_B2, ADAM_EPS, ADAM_WD, ADAM_STEP = 0.001, 0.9, 0.999, 1e-08, 0.01, 10

LANES = 128
BF16_ROWS = 16
VMEM_LIMIT = 56 * 1024 * 1024
MATMUL_VMEM = 44 * 1024 * 1024

P_GQ, P_GK, P_GV, P_OG, P_CQ, P_CKV, P_KPE, P_ALR, P_WIDTH = 0, 256, 512, 1024, 1536, 1792, 1920, 2048, 2176
N_GQ, N_GK, N_GV, N_ALR, N_OG, N_CQ, N_CKV, N_KPE, N_WIDTH = 0, 256, 512, 1024, 1040, 1552, 1808, 1936, 1968

SHARDED = (("w_in", 1), ("gla_gate_w2", 1), ("mla_w_uq", 1), ("mla_w_ukv", 1), ("w_out", 0), ("xa_w_q", 0),
           ("xa_w_kv", 0), ("xa_w_o", 1), ("ffn_w_gate", 1), ("ffn_w_up", 1), ("ffn_conv_w", 1), ("ffn_w_down", 0))
REPLICATED = ("norm_mix", "gla_gate_b", "gla_out_norm", "mla_q_a_norm", "mla_kv_a_norm", "mla_q_norm", "mla_k_norm",
              "norm_xa", "norm_mem", "xa_q_norm", "xa_k_norm", "norm_ffn", "ffn_conv_b")
EXACT_GATHER = ("gla_gate_w2", "ffn_conv_w")
TRANSPOSED = ("ffn_w_gate", "ffn_w_up")
EARLY = ("w_in", "gla_gate_w2", "mla_w_uq", "mla_w_ukv")
LATE = tuple(n for n, _ in SHARDED if n not in EARLY)
WEIGHTS = ("norm_mix", "w_in", "gla_gate_w2", "gla_gate_b", "gla_out_norm", "mla_q_a_norm", "mla_w_uq",
           "mla_kv_a_norm", "mla_w_ukv", "mla_q_norm", "mla_k_norm", "w_out", "norm_xa", "norm_mem", "xa_w_q",
           "xa_w_kv", "xa_q_norm", "xa_k_norm", "xa_w_o", "norm_ffn", "ffn_w_gate", "ffn_w_up", "ffn_conv_w",
           "ffn_conv_b", "ffn_w_down")


_NN = ((1,), (0,))
_NT = ((1,), (1,))
_TN = ((0,), (0,))


def _dg(a, b, dims):
    return lax.dot_general(a.astype(BF16), b.astype(BF16), (dims, ((), ())), preferred_element_type=F32)


@jax.custom_vjp
def _dot_nn(a, b):
    return _dg(a, b, _NN)


_dot_nn.defvjp(lambda a, b: (_dg(a, b, _NN), (a, b)),
               lambda r, g: (_dg(g, r[1], _NT).astype(r[0].dtype), _dg(r[0], g, _TN).astype(r[1].dtype)))


@jax.custom_vjp
def _dot_nt(a, b):
    return _dg(a, b, _NT)


_dot_nt.defvjp(lambda a, b: (_dg(a, b, _NT), (a, b)),
               lambda r, g: (_dg(g, r[1], _NN).astype(r[0].dtype), _dg(g, r[0], _TN).astype(r[1].dtype)))


@jax.custom_vjp
def _dot_tn(a, b):
    return _dg(a, b, _TN)


_dot_tn.defvjp(lambda a, b: (_dg(a, b, _TN), (a, b)),
               lambda r, g: (_dg(r[1], g, _NT).astype(r[0].dtype), _dg(r[0], g, _NN).astype(r[1].dtype)))


def _rms(x, w, n=None):
    n = x.shape[-1] if n is None else n
    ms = jnp.sum(x * x, axis=-1, keepdims=True) * (1.0 / n)
    return x * lax.rsqrt(ms + EPS) * w


def _silu(x):
    return x * jax.nn.sigmoid(x)


def _log_sigmoid(x):
    return jnp.minimum(x, 0.0) - jnp.log(1.0 + jnp.exp(-jnp.abs(x)))


@jax.custom_vjp
def _rope(y, c, sa, sb):
    return y * c + pltpu.roll(y, LANES - 16, 1) * sa + pltpu.roll(y, 16, 1) * sb


def _rope_bwd(res, g):
    c, sa, sb = res
    gy = g * c + pltpu.roll(g * sa, 16, 1) + pltpu.roll(g * sb, LANES - 16, 1)
    return gy, jnp.zeros_like(c), jnp.zeros_like(sa), jnp.zeros_like(sb)


_rope.defvjp(lambda y, c, sa, sb: (_rope(y, c, sa, sb), (c, sa, sb)), _rope_bwd)


@jax.custom_vjp
def _cumsum_rows(x):
    n = x.shape[0]
    row = lax.broadcasted_iota(jnp.int32, x.shape, 0)
    k = 1
    while k < n:
        x = x + jnp.where(row >= k, pltpu.roll(x, k, 0), 0.0)
        k *= 2
    return x


def _cumsum_rows_bwd(_, g):
    n = g.shape[0]
    row = lax.broadcasted_iota(jnp.int32, g.shape, 0)
    k = 1
    while k < n:
        g = g + jnp.where(row < n - k, pltpu.roll(g, n - k, 0), 0.0)
        k *= 2
    return (g,)


_cumsum_rows.defvjp(lambda x: (_cumsum_rows(x), None), _cumsum_rows_bwd)


def _lane_mask(lo, hi):
    lane = lax.broadcasted_iota(jnp.int32, (1, LANES), 1)
    return ((lane >= lo) & (lane < hi)).astype(F32)


def _tile(n, t):
    t = min(n, t)
    assert n % t == 0, (n, t)
    return t


class _Epilogue:
    def __init__(self, fn, rows=(), consts=(), outs=(), accs=()):
        self.fn, self.rows, self.consts, self.outs, self.accs = fn, list(rows), list(consts), list(outs), list(accs)


def _matmul(a, b, mode, out_dtype, name, residual=None, a_lead=None, b_lead=None, more=None, epilogue=None):
    (a0, a1), (b0, b1) = a.shape[-2:], b.shape[-2:]
    if mode == "nn":
        m, k, k2, n = a0, a1, b0, b1
    elif mode == "nt":
        m, k, n, k2 = a0, a1, b0, b1
    else:
        k, m, k2, n = a0, a1, b0, b1
    assert k == k2, (a.shape, b.shape, mode)
    npar = 4 if "p" in (a_lead, b_lead) else 1
    nsum = 4 if "k" in (a_lead, b_lead) else 1
    pairs = [(a, b)] + ([more] if more else [])
    a_item, b_item, o_item = a.dtype.itemsize, b.dtype.itemsize, jnp.dtype(out_dtype).itemsize
    ep = epilogue
    row_extra = 4 if residual is not None else 0
    if ep:
        row_extra += (sum(r.dtype.itemsize * wd for r, wd, _ in ep.rows) + sum(jnp.dtype(d).itemsize * wd for wd, d in ep.outs)) / n

    def vmem_need(tm, tn, tk):
        need = 2 * (nsum if a_lead == "k" else 1) * tm * tk * a_item + 2 * (nsum if b_lead == "k" else 1) * tk * tn * b_item
        need *= len(pairs)
        need += (0 if ep else 2 * tm * tn * o_item) + tm * tn * 4 * (2 if tk < k else 1)
        need += tm * tk * 2 * (a_item == 4 or mode == "tn") + tk * tn * 2 * (b_item == 4)
        return need + int(2 * tm * tn * row_extra) + (3 * tm * tn * 4 if ep else 0)

    halvings = (4096, 2048, 1024, 512, 256, 128, 64, 32, 16, 8)
    if mode == "tn":
        tm = m if m <= 1408 else m // 2
        tn = n if tm * n <= 1024 * 2304 else n // 2
        tk = next((r for r in halvings if k % r == 0 and vmem_need(tm, tn, r) <= MATMUL_VMEM), k)
    else:
        tn, tk = n, k
        tm = next((r for r in halvings if m % r == 0 and vmem_need(r, tn, tk) <= MATMUL_VMEM), m)
    assert m % tm == 0 and n % tn == 0 and k % tk == 0
    assert ep is None or (tn == n and tk == k and npar == 1)
    nk = k // tk
    dims = {"nn": _NN, "nt": _NT, "tn": _TN}[mode]
    n_in = 2 * len(pairs) + (residual is not None)
    n_ep_in = len(ep.rows) + len(ep.consts) if ep else 0
    n_out = len(ep.outs) + len(ep.accs) if ep else 1

    def body(*refs):
        ab, rs, ep_in, outs, scratch = _split_refs(refs, (2 * len(pairs), n_in - 2 * len(pairs), n_ep_in, n_out, nk > 1))
        prod = None
        for a_ref, b_ref in zip(ab[0::2], ab[1::2]):
            for sh in range(nsum):
                term = _dg(a_ref[sh] if a_lead == "k" else a_ref[...], b_ref[sh] if b_lead == "k" else b_ref[...], dims)
                prod = term if prod is None else prod + term

        def finish(r):
            if rs:
                r = r + rs[0][...]
            if ep is None:
                outs[0][...] = r.astype(outs[0].dtype)
                return
            vals = [x[...] for x in ep_in]
            ro, ao = ep.fn(r, vals[:len(ep.rows)], vals[len(ep.rows):])
            for ref, val in zip(outs[:len(ep.outs)], ro, strict=True):
                ref[...] = val.astype(ref.dtype)
            if ep.accs:
                @pl.when(pl.program_id(0) == 0)
                def _():
                    for ref in outs[len(ep.outs):]:
                        ref[...] = jnp.zeros_like(ref)

                for ref, val in zip(outs[len(ep.outs):], ao, strict=True):
                    ref[...] += val

        if nk == 1:
            finish(prod)
            return
        acc = scratch[0]
        kk = pl.program_id(3)

        @pl.when(kk == 0)
        def _():
            acc[...] = prod

        @pl.when(kk > 0)
        def _():
            acc[...] += prod

        @pl.when(kk == nk - 1)
        def _():
            finish(acc[...])

    def spec(lead, blk, idx):
        if lead is None:
            return pl.BlockSpec(blk, lambda i, j, p, kk: idx(i, j, kk))
        if lead == "p":
            return pl.BlockSpec((None,) + blk, lambda i, j, p, kk: (p,) + idx(i, j, kk))
        return pl.BlockSpec((nsum,) + blk, lambda i, j, p, kk: (0,) + idx(i, j, kk))

    if mode == "nn":
        pair_specs = [spec(a_lead, (tm, tk), lambda i, j, kk: (i, kk)), spec(b_lead, (tk, tn), lambda i, j, kk: (kk, j))]
    elif mode == "nt":
        pair_specs = [spec(a_lead, (tm, tk), lambda i, j, kk: (i, kk)), spec(b_lead, (tn, tk), lambda i, j, kk: (j, kk))]
    else:
        pair_specs = [spec(a_lead, (tk, tm), lambda i, j, kk: (kk, i)), spec(b_lead, (tk, tn), lambda i, j, kk: (kk, j))]
    tile = spec(None, (tm, tn), lambda i, j, kk: (i, j))
    in_specs = pair_specs * len(pairs)
    args = [x for pair in pairs for x in pair]
    if residual is not None:
        assert npar == 1
        in_specs.append(tile)
        args.append(residual)
    if ep:
        in_specs += [pl.BlockSpec((tm, wd), functools.partial(lambda cb, i, j, p, kk: (i, cb), cb)) for _, wd, cb in ep.rows]
        in_specs += [pl.BlockSpec(c.shape, lambda i, j, p, kk: (0, 0)) for c in ep.consts]
        args += [r for r, _, _ in ep.rows] + ep.consts
        out_specs = [pl.BlockSpec((tm, wd), lambda i, j, p, kk: (i, 0)) for wd, _ in ep.outs]
        out_specs += [pl.BlockSpec(shape, lambda i, j, p, kk: (0, 0)) for shape in ep.accs]
        out_shape = [jax.ShapeDtypeStruct((m, wd), d) for wd, d in ep.outs] + [jax.ShapeDtypeStruct(sh, F32) for sh in ep.accs]
    else:
        out_specs = spec("p" if npar > 1 else None, (tm, tn), lambda i, j, kk: (i, j))
        out_shape = jax.ShapeDtypeStruct(((4,) if npar > 1 else ()) + (m, n), out_dtype)
    outer = "arbitrary" if ep and ep.accs else "parallel"
    return pl.pallas_call(
        body, grid=(m // tm, n // tn, npar, nk), in_specs=in_specs, out_specs=out_specs, out_shape=out_shape,
        scratch_shapes=[pltpu.VMEM((tm, tn), F32)] if nk > 1 else [],
        compiler_params=pltpu.CompilerParams(dimension_semantics=(outer, outer, outer, "arbitrary"),
                                             vmem_limit_bytes=VMEM_LIMIT),
        name=name)(*args)


def _row(a, width=None, col_block=0):
    return (a, a.shape[1] if width is None else width, col_block)


def _rows_call(body, rows, consts, outs, accs=(), *, name, tile=512):
    s = rows[0][0].shape[0]
    t = _tile(s, tile)
    nr, nc, no = len(rows), len(consts), len(outs)

    def kern(*refs):
        r = [x[...] for x in refs[:nr]]
        c = [x[...] for x in refs[nr:nr + nc]]
        o_refs = refs[nr + nc:nr + nc + no]
        a_refs = refs[nr + nc + no:]
        ro, ao = body(r, c)
        for ref, val in zip(o_refs, ro, strict=True):
            ref[...] = val.astype(ref.dtype)
        if a_refs:
            @pl.when(pl.program_id(0) == 0)
            def _():
                for ref in a_refs:
                    ref[...] = jnp.zeros_like(ref)

            for ref, val in zip(a_refs, ao, strict=True):
                ref[...] += val

    in_specs = [pl.BlockSpec((t, w), functools.partial(lambda cb, i: (i, cb), cb)) for (_, w, cb) in rows]
    in_specs += [pl.BlockSpec(c.shape, lambda i: (0, 0)) for c in consts]
    out_specs = [pl.BlockSpec((t, w), lambda i: (i, 0)) for (w, _) in outs]
    out_specs += [pl.BlockSpec(shape, lambda i: (0, 0)) for shape in accs]
    out_shape = [jax.ShapeDtypeStruct((s, w), dt) for (w, dt) in outs]
    out_shape += [jax.ShapeDtypeStruct(shape, F32) for shape in accs]
    return pl.pallas_call(
        kern, grid=(s // t,), in_specs=in_specs, out_specs=out_specs, out_shape=out_shape,
        compiler_params=pltpu.CompilerParams(dimension_semantics=("arbitrary" if accs else "parallel",),
                                             vmem_limit_bytes=VMEM_LIMIT),
        name=name)(*[r[0] for r in rows], *consts)


def _gla_chunk(q, k, la, v0, v1, s0, s1):
    c = q.shape[0]
    r = lax.broadcasted_iota(jnp.int32, (c, c), 0)
    cc = lax.broadcasted_iota(jnp.int32, (c, c), 1)
    tril = cc <= r
    cum = _cumsum_rows(la)
    cl = jnp.sum(la, axis=0, keepdims=True)
    qd = q * (GLA_DK ** -0.5) * jnp.exp(cum)
    ki = k * jnp.exp(-cum)
    ke = k * jnp.exp(cl - cum)
    dec = jnp.exp(cl)
    outs, news = [], []
    for h, (v, s) in enumerate(((v0, s0), (v1, s1))):
        mk = _lane_mask(GLA_DK * h, GLA_DK * (h + 1))
        qh = qd * mk
        att = jnp.where(tril, _dot_nt(qh, ki), 0.0)
        outs.append(_dot_nn(att, v) + _dot_nt(qh, s))
        news.append(s * dec + _dot_tn(v, ke * mk))
    return outs[0], outs[1], news[0], news[1]


def _gla_specs(tb, rev_nb=None):
    blk = (lambda b: b) if rev_nb is None else (lambda b: rev_nb - 1 - b)
    q = pl.BlockSpec((tb, 128), lambda p, b: (blk(b), P_GQ // 128 + p))
    k = pl.BlockSpec((tb, 128), lambda p, b: (blk(b), P_GK // 128 + p))
    la = pl.BlockSpec((tb, 128), lambda p, b: (blk(b), p))
    v = pl.BlockSpec((tb, 256), lambda p, b: (blk(b), P_GV // 256 + p))
    o = pl.BlockSpec((tb, 256), lambda p, b: (blk(b), p))
    st = pl.BlockSpec((tb // GLA_CHUNK, 2, 128, 128), lambda p, b: (blk(b), p, 0, 0))
    return q, k, la, v, o, st


def _gla_fwd(proj, la):
    s = proj.shape[0]
    tb = _tile(s, 512)
    nb, nch = s // tb, tb // GLA_CHUNK

    def kern(q_ref, k_ref, la_ref, v_ref, o_ref, st_ref, s_sc):
        @pl.when(pl.program_id(1) == 0)
        def _():
            s_sc[...] = jnp.zeros_like(s_sc)

        s0, s1 = s_sc[0], s_sc[1]
        for ci in range(nch):
            sl = slice(ci * GLA_CHUNK, (ci + 1) * GLA_CHUNK)
            st_ref[ci, 0] = s0
            st_ref[ci, 1] = s1
            o0, o1, s0, s1 = _gla_chunk(q_ref[sl, :], k_ref[sl, :], la_ref[sl, :], v_ref[sl, 0:128],
                                        v_ref[sl, 128:256], s0, s1)
            o_ref[sl, 0:128] = o0
            o_ref[sl, 128:256] = o1
        s_sc[0] = s0
        s_sc[1] = s1

    q, k, lasp, v, o, st = _gla_specs(tb)
    return pl.pallas_call(
        kern, grid=(2, nb), in_specs=[q, k, lasp, v], out_specs=[o, st],
        out_shape=[jax.ShapeDtypeStruct((s, 512), F32),
                   jax.ShapeDtypeStruct((s // GLA_CHUNK, GLA_HEADS, 128, 128), F32)],
        scratch_shapes=[pltpu.VMEM((2, 128, 128), F32)],
        compiler_params=pltpu.CompilerParams(dimension_semantics=("parallel", "arbitrary"),
                                             vmem_limit_bytes=VMEM_LIMIT),
        name="gla_fwd")(proj, proj, la, proj)


def _gla_bwd(proj, la, states, d_o, comm):
    s = proj.shape[0]
    tb = _tile(s, 512)
    nb, nch = s // tb, tb // GLA_CHUNK
    nci, nco = len(comm.ins), len(comm.out_shape)

    def kern(*refs):
        (q_ref, k_ref, la_ref, v_ref, do_ref, st_ref), cins, (dq_ref, dk_ref, dla_ref, dv_ref), couts, (ds_sc,), csems = \
            _split_refs(refs, (6, nci, 4, nco, 1, len(comm.sems)))
        place = _place()
        pair, blk = pl.program_id(0), pl.program_id(1)

        @pl.when((pair == 0) & (blk == 0))
        def _():
            comm.start(place, cins, couts, csems)

        @pl.when((pair == 1) & (blk == nb // 2))
        def _():
            comm.mid(place, cins, couts, csems)

        @pl.when(blk == 0)
        def _():
            ds_sc[...] = jnp.zeros_like(ds_sc)

        d0, d1 = ds_sc[0], ds_sc[1]
        for ci in reversed(range(nch)):
            sl = slice(ci * GLA_CHUNK, (ci + 1) * GLA_CHUNK)
            _, vjp = jax.vjp(_gla_chunk, q_ref[sl, :], k_ref[sl, :], la_ref[sl, :], v_ref[sl, 0:128],
                             v_ref[sl, 128:256], st_ref[ci, 0], st_ref[ci, 1])
            gq, gk, gla, gv0, gv1, d0, d1 = vjp((do_ref[sl, 0:128], do_ref[sl, 128:256], d0, d1))
            dq_ref[sl, :] = gq
            dk_ref[sl, :] = gk
            dla_ref[sl, :] = gla
            dv_ref[sl, 0:128] = gv0
            dv_ref[sl, 128:256] = gv1
        ds_sc[0] = d0
        ds_sc[1] = d1

        @pl.when((pair == 1) & (blk == nb - 1))
        def _():
            comm.finish(place, cins, couts, csems)

    q, k, lasp, v, o, st = _gla_specs(tb, rev_nb=nb)
    res = pl.pallas_call(
        kern, grid=(2, nb), in_specs=[q, k, lasp, v, o, st] + [ANY] * nci, out_specs=[lasp, lasp, lasp, o] + [ANY] * nco,
        out_shape=[jax.ShapeDtypeStruct((s, 256), F32), jax.ShapeDtypeStruct((s, 256), F32),
                   jax.ShapeDtypeStruct((s, 256), F32), jax.ShapeDtypeStruct((s, 512), F32)] + comm.out_shape,
        scratch_shapes=[pltpu.VMEM((2, 128, 128), F32)] + comm.sems,
        compiler_params=pltpu.CompilerParams(dimension_semantics=("arbitrary", "arbitrary"),
                                             vmem_limit_bytes=VMEM_LIMIT),
        name="gla_bwd")(proj, proj, la, proj, d_o, states, *comm.ins)
    return res[0], res[1], res[2], res[3], res[4:]


def _causal_keep(t, qi, ki):
    row = lax.broadcasted_iota(jnp.int32, (t, t), 0) + qi * t
    col = lax.broadcasted_iota(jnp.int32, (t, t), 1) + ki * t
    return col <= row


def _split_refs(refs, counts):
    out, off = [], 0
    for cnt in counts:
        out.append(refs[off:off + cnt])
        off += cnt
    return out


def _causal_blocks(n, key_major):
    pairs = ([(ki, qi) for ki in range(n) for qi in range(ki, n)] if key_major else
             [(ki, qi) for qi in range(n) for ki in range(qi + 1)])
    return np.array([ki for ki, _ in pairs], np.int32), np.array([qi for _, qi in pairs], np.int32)


def _attn_fwd(q, k, v, comm, tile=1024):
    s = q.shape[0]
    t = _tile(s, tile)
    n = s // t
    nci, nco = len(comm.ins), len(comm.out_shape)

    ki_tab, qi_tab = _causal_blocks(n, key_major=False)
    steps = len(ki_tab)

    def kern(ki_ref, qi_ref, *refs):
        (q_ref, k_ref, v_ref), cins, (o_ref, lse_ref), couts, (m_sc, l_sc, acc_sc), csems = _split_refs(
            refs, (3, nci, 2, nco, 3, len(comm.sems)))
        pair, step = pl.program_id(0), pl.program_id(1)
        qi, ki = qi_ref[step], ki_ref[step]
        place = _place()

        @pl.when((pair == 0) & (step == 0))
        def _():
            comm.start(place, cins, couts, csems)

        @pl.when((pair == MLA_HEADS // 2 - 1) & (step == 0))
        def _():
            comm.mid(place, cins, couts, csems)

        first = lax.broadcasted_iota(jnp.int32, (t, LANES), 1) < MLA_V

        @pl.when(ki == 0)
        def _():
            m_sc[...] = jnp.full_like(m_sc, -jnp.inf)
            l_sc[...] = jnp.zeros_like(l_sc)
            acc_sc[...] = jnp.zeros_like(acc_sc)

        def update(diagonal):
            keep = _causal_keep(t, 0, 0)
            alphas, pvs = [], []
            for h in range(2):
                sc = _dg(q_ref[:, 128 * h:128 * (h + 1)], k_ref[:, 128 * h:128 * (h + 1)], _NT)
                if diagonal:
                    sc = jnp.where(keep, sc, -jnp.inf)
                m_prev = m_sc[h]
                m_new = jnp.maximum(m_prev, jnp.max(sc, axis=1, keepdims=True))
                alpha = jnp.exp2(m_prev - m_new)
                p = jnp.exp2(sc - m_new[:, 0:1])
                l_sc[h] = alpha * l_sc[h] + jnp.sum(p, axis=1, keepdims=True)
                m_sc[h] = m_new
                alphas.append(alpha)
                pvs.append(_dg(p, v_ref[...], _NN))
            acc_sc[...] = acc_sc[...] * jnp.where(first, alphas[0], alphas[1]) + jnp.where(first, pvs[0], pvs[1])

        @pl.when(ki < qi)
        def _():
            update(False)

        @pl.when(ki == qi)
        def _():
            update(True)

        @pl.when(ki == qi)
        def _():
            l = jnp.where(first, l_sc[0], l_sc[1])
            m = jnp.where(first, m_sc[0], m_sc[1])
            o_ref[...] = acc_sc[...] / l
            lse_ref[...] = m + jnp.log2(l)

        @pl.when((pair == MLA_HEADS // 2 - 1) & (step == steps - 1))
        def _():
            comm.finish(place, cins, couts, csems)

    q_idx = lambda p, st, ki_r, qi_r: (qi_r[st], p)
    k_idx = lambda p, st, ki_r, qi_r: (ki_r[st], p)
    res = pl.pallas_call(
        kern, grid_spec=pltpu.PrefetchScalarGridSpec(
            num_scalar_prefetch=2, grid=(MLA_HEADS // 2, steps),
            in_specs=[pl.BlockSpec((t, 256), q_idx), pl.BlockSpec((t, 256), k_idx), pl.BlockSpec((t, 128), k_idx)]
            + [ANY] * nci,
            out_specs=[pl.BlockSpec((t, 128), q_idx), pl.BlockSpec((t, 128), q_idx)] + [ANY] * nco,
            scratch_shapes=[pltpu.VMEM((2, t, LANES), F32), pltpu.VMEM((2, t, LANES), F32),
                            pltpu.VMEM((t, LANES), F32)] + comm.sems),
        out_shape=[jax.ShapeDtypeStruct((s, 512), F32), jax.ShapeDtypeStruct((s, 512), F32)] + comm.out_shape,
        compiler_params=pltpu.CompilerParams(dimension_semantics=("arbitrary", "arbitrary"),
                                             vmem_limit_bytes=VMEM_LIMIT),
        name="mla_attn_fwd")(ki_tab, qi_tab, q, k, v, *comm.ins)
    return res[0], res[1], res[2:]


def _attn_bwd(q, k, v, o, lse, d_o, comm, tile=512):
    s = q.shape[0]
    t = _tile(s, tile)
    n = s // t
    nci, nco = len(comm.ins), len(comm.out_shape)

    ki_tab, qi_tab = _causal_blocks(n, key_major=True)
    steps = len(ki_tab)

    def kern(ki_ref, qi_ref, *refs):
        (q_ref, k_ref, v_ref, o_ref, lse_ref, do_ref), cins, (dq_ref, dk_ref, dv_ref), couts, (dk_sc, dv_sc), csems = \
            _split_refs(refs, (6, nci, 3, nco, 2, len(comm.sems)))
        pair, step = pl.program_id(0), pl.program_id(1)
        ki, qi = ki_ref[step], qi_ref[step]
        place = _place()

        @pl.when((pair == 0) & (step == 0))
        def _():
            comm.start(place, cins, couts, csems)

        @pl.when((pair == MLA_HEADS // 2 - 1) & (step == 0))
        def _():
            comm.mid(place, cins, couts, csems)

        @pl.when((ki == 0) & (qi == 0))
        def _():
            dq_ref[...] = jnp.zeros_like(dq_ref)

        @pl.when(qi == ki)
        def _():
            dk_sc[...] = jnp.zeros_like(dk_sc)
            dv_sc[...] = jnp.zeros_like(dv_sc)

        def update(diagonal):
            keep = _causal_keep(t, 0, 0)
            d_o = do_ref[...]
            prod = d_o * o_ref[...]
            rows = pl.ds(pl.multiple_of(qi * t, t), t)
            for h in range(2):
                hs = slice(128 * h, 128 * (h + 1))
                mk = _lane_mask(MLA_V * h, MLA_V * (h + 1))
                qh, kh = q_ref[:, hs], k_ref[:, hs]
                sc = _dg(qh, kh, _NT)
                if diagonal:
                    sc = jnp.where(keep, sc, -jnp.inf)
                p = jnp.exp2(sc - lse_ref[:, MLA_V * h:MLA_V * h + 1])
                doh = d_o * mk
                dp = _dg(doh * LN2, v_ref[...], _NT)
                delta = jnp.sum(prod * mk, axis=1, keepdims=True) * LN2
                ds = p * (dp - delta)
                dv_sc[...] += _dg(p, doh, _TN)
                dk_sc[:, hs] += _dg(ds, qh, _TN)
                dq_ref[rows, hs] += _dg(ds, kh, _NN)

        @pl.when(qi > ki)
        def _():
            update(False)

        @pl.when(qi == ki)
        def _():
            update(True)

        @pl.when(qi == n - 1)
        def _():
            dk_ref[...] = dk_sc[...]
            dv_ref[...] = dv_sc[...].astype(dv_ref.dtype)

        @pl.when((pair == MLA_HEADS // 2 - 1) & (step == steps - 1))
        def _():
            comm.finish(place, cins, couts, csems)

    q_idx = lambda p, st, ki_r, qi_r: (qi_r[st], p)
    k_idx = lambda p, st, ki_r, qi_r: (ki_r[st], p)
    res = pl.pallas_call(
        kern, grid_spec=pltpu.PrefetchScalarGridSpec(
            num_scalar_prefetch=2, grid=(MLA_HEADS // 2, steps),
            in_specs=[pl.BlockSpec((t, 256), q_idx), pl.BlockSpec((t, 256), k_idx), pl.BlockSpec((t, 128), k_idx),
                      pl.BlockSpec((t, 128), q_idx), pl.BlockSpec((t, 128), q_idx), pl.BlockSpec((t, 128), q_idx)]
            + [ANY] * nci,
            out_specs=[pl.BlockSpec((s, 256), lambda p, st, ki_r, qi_r: (0, p)), pl.BlockSpec((t, 256), k_idx),
                       pl.BlockSpec((t, 128), k_idx)] + [ANY] * nco,
            scratch_shapes=[pltpu.VMEM((t, 256), F32), pltpu.VMEM((t, 128), F32)] + comm.sems),
        out_shape=[jax.ShapeDtypeStruct((s, 1024), F32), jax.ShapeDtypeStruct((s, 1024), F32),
                   jax.ShapeDtypeStruct((s, 512), BF16)] + comm.out_shape,
        compiler_params=pltpu.CompilerParams(dimension_semantics=("arbitrary", "arbitrary"),
                                             vmem_limit_bytes=VMEM_LIMIT),
        name="mla_attn_bwd")(ki_tab, qi_tab, q, k, v, o, lse, d_o, *comm.ins)
    return res[0], res[1], res[2], res[3:]


def _gate_fn(alr, w2, b):
    return _log_sigmoid(_dot_nn(alr, w2) + b) * (1.0 / GLA_GATE_NORM)


def _qk_head(qh, kh, kpe, c, sa, sb, qn, kn):
    kfull = kh + kpe * _lane_mask(MLA_NOPE, MLA_QK)
    q_r = _rope(_rms(qh, qn, MLA_QK), c, sa, sb) * (MLA_QK ** -0.5 * LOG2E)
    k_r = _rope(_rms(kfull, kn, MLA_QK), c, sa, sb)
    return q_r, k_r


def _mix_head(o, og, gn):
    return _rms(o, gn) * _silu(og)


def _xa_head(xq, xk, xv, qn, kn):
    sc = _dot_nt(_rms(xq, qn), _rms(xk, kn)) * (XA_DIM ** -0.5)
    e = jnp.exp(sc - lax.stop_gradient(jnp.max(sc, axis=1, keepdims=True)))
    p = e / jnp.sum(e, axis=1, keepdims=True)
    return _dot_nn(p, xv)


def _heads(x, n):
    return [x[:, 128 * h:128 * (h + 1)] for h in range(n)]


def _cat(xs):
    return jnp.concatenate(xs, axis=1)


def _norm_fwd(x, w, name):
    return _rows_call(lambda r, c: ([_rms(r[0], c[0])], []), [_row(x)], [w], [(x.shape[1], BF16)], name=name)[0]


def _norm_fwd_epilogue(w):
    return _Epilogue(lambda h, rows, consts: ([h, _rms(h, consts[0])], []), [], [w], [(D_MODEL, F32), (D_MODEL, BF16)], [])


def _norm_bwd_epilogue(x, w, add):
    def fn(d_out, rows, consts):
        _, vjp = jax.vjp(_rms, rows[0], consts[0])
        dx, dw = vjp(d_out)
        return [dx + rows[1]], [dw]

    return _Epilogue(fn, [_row(x), _row(add)], [w], [(D_MODEL, F32)], [w.shape])


def _norm_fwd_comm(x, w, comm, name):
    s, d = x.shape
    t = _tile(s, 512)
    n = s // t
    nci, nco = len(comm.ins), len(comm.out_shape)

    def kern(*refs):
        (x_ref, w_ref), cins, (o_ref,), couts, csems = _split_refs(refs, (2, nci, 1, nco, len(comm.sems)))
        place = _place()

        @pl.when(pl.program_id(0) == 0)
        def _():
            comm.start(place, cins, couts, csems)

        o_ref[...] = _rms(x_ref[...], w_ref[...]).astype(o_ref.dtype)

        @pl.when(pl.program_id(0) == n - 1)
        def _():
            comm.mid(place, cins, couts, csems)
            comm.finish(place, cins, couts, csems)

    tile = pl.BlockSpec((t, d), lambda i: (i, 0))
    res = pl.pallas_call(
        kern, grid=(n,), in_specs=[tile, pl.BlockSpec(w.shape, lambda i: (0, 0))] + [ANY] * nci,
        out_specs=[tile] + [ANY] * nco, out_shape=[jax.ShapeDtypeStruct((s, d), BF16)] + comm.out_shape,
        scratch_shapes=comm.sems,
        compiler_params=pltpu.CompilerParams(dimension_semantics=("arbitrary",), vmem_limit_bytes=VMEM_LIMIT),
        name=name)(x, w, *comm.ins)
    return res[0], res[1:]


def _norm_bwd(x, w, d_out, add, name):
    def body(r, c):
        _, vjp = jax.vjp(_rms, r[0], c[0])
        dx, dw = vjp(r[1])
        return [dx + r[2]], [dw]

    return _rows_call(body, [_row(x), _row(d_out), _row(add)], [w], [(x.shape[1], F32)], [w.shape], name=name)


CONV_HALO = BF16_ROWS


def _conv_specs(s, f, t):
    n8 = t // CONV_HALO
    cur = pl.BlockSpec((None, t, f), lambda j, i: (j, i, 0))
    prev = pl.BlockSpec((None, CONV_HALO, f), lambda j, i: (j, jnp.maximum(i * n8 - 1, 0), 0))
    nxt = pl.BlockSpec((None, CONV_HALO, f), lambda j, i: (j, jnp.minimum((i + 1) * n8, s // CONV_HALO - 1), 0))
    cw = pl.BlockSpec((None, 3, f), lambda j, i: (j, 0, 0))
    cb = pl.BlockSpec((None, 1, f), lambda j, i: (j, 0, 0))
    return cur, prev, nxt, cw, cb


def _conv_taps(g, prev, first):
    ext = jnp.concatenate([jnp.where(first, 0.0, prev.astype(F32)), g], axis=0)
    return pltpu.roll(ext, 1, 0)[CONV_HALO:], pltpu.roll(ext, 2, 0)[CONV_HALO:]


def _conv_fwd(gg, uu, cw, cb):
    _, s, f = gg.shape
    t = _tile(s, 512)

    def kern(g_ref, gp_ref, u_ref, cw_ref, cb_ref, o_ref):
        g = g_ref[...].astype(F32)
        g1, g2 = _conv_taps(g, gp_ref[...], pl.program_id(1) == 0)
        w = cw_ref[...]
        gc = cb_ref[...] + w[0:1] * g2 + w[1:2] * g1 + w[2:3] * g
        o_ref[...] = (_silu(gc) * u_ref[...].astype(F32)).astype(o_ref.dtype)

    cur, prev, _, cws, cbs = _conv_specs(s, f, t)
    return pl.pallas_call(
        kern, grid=(4, s // t), in_specs=[cur, prev, cur, cws, cbs], out_specs=cur,
        out_shape=jax.ShapeDtypeStruct(gg.shape, BF16),
        compiler_params=pltpu.CompilerParams(dimension_semantics=("parallel", "parallel"), vmem_limit_bytes=VMEM_LIMIT),
        name="ffn_conv_fwd")(gg, gg, uu, cw, cb)


def _conv_bwd(gg, uu, dact, cw, cb):
    _, s, f = gg.shape
    t = _tile(s, 512)
    nt = s // t

    def kern(g_ref, gp_ref, gn_ref, u_ref, un_ref, da_ref, dan_ref, cw_ref, cb_ref, du_ref, dg_ref, dcw_ref, dcb_ref):
        i = pl.program_id(1)
        cat = lambda a_ref, b_ref: jnp.concatenate([a_ref[...].astype(F32), b_ref[...].astype(F32)], axis=0)
        g, u, da = cat(g_ref, gn_ref), cat(u_ref, un_ref), cat(da_ref, dan_ref)
        g1, g2 = _conv_taps(g, gp_ref[...], i == 0)
        w = cw_ref[...]
        gc = cb_ref[...] + w[0:1] * g2 + w[1:2] * g1 + w[2:3] * g
        sg = jax.nn.sigmoid(gc)
        du_ref[...] = (da[:t] * (gc[:t] * sg[:t])).astype(du_ref.dtype)
        row = lax.broadcasted_iota(jnp.int32, (t + CONV_HALO, 1), 0)
        dgc = jnp.where((row < t) | (i < nt - 1), da * u * (sg * (1.0 + gc * (1.0 - sg))), 0.0)
        up1 = pltpu.roll(dgc, t + CONV_HALO - 1, 0)[:t]
        up2 = pltpu.roll(dgc, t + CONV_HALO - 2, 0)[:t]
        dgc = dgc[:t]
        dg_ref[...] = (w[2:3] * dgc + w[1:2] * up1 + w[0:1] * up2).astype(dg_ref.dtype)

        @pl.when(i == 0)
        def _():
            dcw_ref[...] = jnp.zeros_like(dcw_ref)
            dcb_ref[...] = jnp.zeros_like(dcb_ref)

        dcw_ref[0:1, :] += jnp.sum(dgc * g2[:t], axis=0, keepdims=True)
        dcw_ref[1:2, :] += jnp.sum(dgc * g1[:t], axis=0, keepdims=True)
        dcw_ref[2:3, :] += jnp.sum(dgc * g[:t], axis=0, keepdims=True)
        dcb_ref[...] += jnp.sum(dgc, axis=0, keepdims=True)

    cur, prev, nxt, cws, cbs = _conv_specs(s, f, t)
    return pl.pallas_call(
        kern, grid=(4, nt), in_specs=[cur, prev, nxt, cur, nxt, cur, nxt, cws, cbs], out_specs=[cur, cur, cws, cbs],
        out_shape=[jax.ShapeDtypeStruct(gg.shape, BF16), jax.ShapeDtypeStruct(gg.shape, BF16),
                   jax.ShapeDtypeStruct(cw.shape, F32), jax.ShapeDtypeStruct(cb.shape, F32)],
        compiler_params=pltpu.CompilerParams(dimension_semantics=("parallel", "arbitrary"), vmem_limit_bytes=VMEM_LIMIT),
        name="ffn_conv_bwd")(gg, gg, gg, uu, uu, dact, dact, cw, cb)


def _rope_tables(pos):
    half = MLA_ROPE // 2
    inv = ROPE_THETA ** (-jnp.arange(half, dtype=F32) / half)
    ang = pos.astype(F32)[:, None] * inv
    cos, sin = jnp.cos(ang), jnp.sin(ang)
    s = pos.shape[0]
    z = lambda w: jnp.zeros((s, w), F32)
    c = jnp.concatenate([jnp.ones((s, MLA_NOPE), F32), cos, cos, jnp.ones((s, LANES - MLA_QK), F32)], axis=1)
    sa = jnp.concatenate([z(MLA_NOPE), -sin, z(half), z(LANES - MLA_QK)], axis=1)
    sb = jnp.concatenate([z(MLA_NOPE), z(half), sin, z(LANES - MLA_QK)], axis=1)
    return c, sa, sb


def _local_step(x, mem, pos, target, rep, early_shards, late_shards):
    g = {}
    c, sa, sb = _rope_tables(pos)

    xn, gathered = _norm_fwd_comm(x, rep["norm_mix"], _gather_plan(early_shards), "norm_mix_fwd_gather")
    w = _early_layout(dict(zip(EARLY, gathered, strict=True)), rep)

    def proj_fn(r, rows, k):
        la_ = _gate_fn(r[:, P_ALR:P_ALR + 128], k[0], k[1])
        return [r, la_, _rms(r[:, P_CQ:P_CQ + MLA_Q_RANK], k[2]), _rms(r[:, P_CKV:P_CKV + MLA_KV_RANK], k[3])], []

    proj, la, q_lat, kv_lat = _matmul(
        xn, w["in"], "nn", F32, "proj_fwd", epilogue=_Epilogue(
            proj_fn, [], [w["w2"], w["gate_b"], w["q_a_norm"], w["kv_a_norm"]],
            [(P_WIDTH, F32), (256, F32), (MLA_Q_RANK, BF16), (MLA_KV_RANK, BF16)], []))
    alr = _row(proj, 128, P_ALR // 128)
    kpe = _row(proj, 128, P_KPE // 128)
    og = _row(proj, 512, P_OG // 512)
    cq = _row(proj, 256, P_CQ // 256)
    ckv = _row(proj, 128, P_CKV // 128)

    o_gla, states = _gla_fwd(proj, la)

    q_up = _matmul(q_lat, w["uq"], "nn", F32, "mla_q_fwd")
    k_up = _matmul(kv_lat, w["k"], "nn", F32, "mla_k_fwd")
    v_mla = _matmul(kv_lat, w["v"], "nn", BF16, "mla_v_fwd")

    def qk_body(r, k):
        qs, ks = [], []
        for qh, kh in zip(_heads(r[0], MLA_HEADS), _heads(r[1], MLA_HEADS)):
            a, b = _qk_head(qh, kh, r[2], r[3], r[4], r[5], k[0], k[1])
            qs.append(a)
            ks.append(b)
        return [_cat(qs), _cat(ks)], []

    tabs = [_row(c), _row(sa), _row(sb)]
    q_r, k_r = _rows_call(qk_body, [_row(q_up), _row(k_up), kpe] + tabs, [w["q_norm"], w["k_norm"]],
                          [(1024, BF16), (1024, BF16)], name="mla_qk_fwd")
    o_mla, lse, gathered = _attn_fwd(q_r, k_r, v_mla, _gather_plan(late_shards))
    w.update(_late_layout(dict(zip(LATE, gathered, strict=True))))

    def mix_body(r, k):
        ys = [_mix_head(o, g_, k[0]) for o, g_ in zip(_heads(r[0], GLA_HEADS), _heads(r[1], GLA_HEADS))]
        return [_cat(ys + [r[2]])], []

    cat = _rows_call(mix_body, [_row(o_gla), og, _row(o_mla)], [w["gla_out_norm"]], [(1024, BF16)],
                     name="mix_fwd")[0]
    h1, hn = _matmul(cat, w["out"], "nn", F32, "out_fwd_norm", residual=x, epilogue=_norm_fwd_epilogue(w["norm_xa"]))
    mn = _norm_fwd(mem, w["norm_mem"], "norm_mem_fwd")
    xkv = _matmul(mn, w["xkv"], "nn", F32, "xa_kv_fwd")

    def xa_fn(r, rows, k):
        ks, vs = _heads(k[0], 2 * XA_HEADS)[:XA_HEADS], _heads(k[0], 2 * XA_HEADS)[XA_HEADS:]
        return [r, _cat([_xa_head(a, b, v_, k[1], k[2]) for a, b, v_ in zip(_heads(r, XA_HEADS), ks, vs)])], []

    xq, xo = _matmul(hn, w["xq"], "nn", F32, "xa_q_fwd_attn", epilogue=_Epilogue(
        xa_fn, [], [xkv, w["xa_q_norm"], w["xa_k_norm"]], [(512, F32), (512, BF16)], []))
    h2, fn = _matmul(xo, w["xo"], "nn", F32, "xa_o_fwd_norm", residual=h1, epilogue=_norm_fwd_epilogue(w["norm_ffn"]))
    gg = _matmul(fn, w["wg"], "nt", BF16, "ffn_gate_fwd", b_lead="p")
    uu = _matmul(fn, w["wu"], "nt", BF16, "ffn_up_fwd", b_lead="p")
    act = _conv_fwd(gg, uu, w["cw"], w["cb"])
    def loss_fn(y, rows, consts):
        err = y - rows[0]
        part = 0.5 * jnp.sum(jnp.sum(err * err, axis=1, keepdims=True) * (1.0 / D_MODEL), axis=0, keepdims=True)
        return [err * (1.0 / D_MODEL)], [jnp.broadcast_to(part, (1, LANES))]

    dy, loss = _matmul(act, w["wd"], "nn", F32, "ffn_down_fwd_loss", residual=h2, a_lead="k", b_lead="k",
                       epilogue=_Epilogue(loss_fn, [_row(target)], [], [(D_MODEL, F32)], [(1, LANES)]))

    g["ffn_w_down"] = _matmul(act, dy, "tn", BF16, "ffn_down_dw", a_lead="p")
    dact = _matmul(dy, w["wd"], "nt", BF16, "ffn_down_dx", b_lead="p")
    duu, dgg, g["ffn_conv_w"], g["ffn_conv_b"] = _conv_bwd(gg, uu, dact, w["cw"], w["cb"])
    g["ffn_w_gate"] = _matmul(dgg, fn, "tn", BF16, "ffn_gate_dw", a_lead="p")
    g["ffn_w_up"] = _matmul(duu, fn, "tn", BF16, "ffn_up_dw", a_lead="p")
    dh2, g["norm_ffn"] = _matmul(dgg, w["wg"], "nn", F32, "ffn_dx_norm_bwd", a_lead="k", b_lead="k", more=(duu, w["wu"]),
                                 epilogue=_norm_bwd_epilogue(h2, w["norm_ffn"], dy))

    g["xa_w_o"] = _matmul(xo, dh2, "tn", BF16, "xa_o_dw")
    def xa_bwd(dxo_, rows, k):
        kvh = _heads(k[0], 2 * XA_HEADS)
        dq_, dk_, dv_ = [], [], []
        dqn, dkn = 0.0, 0.0
        for h, (a, d_) in enumerate(zip(_heads(rows[0], XA_HEADS), _heads(dxo_, XA_HEADS))):
            _, vjp = jax.vjp(_xa_head, a, kvh[h], kvh[XA_HEADS + h], k[1], k[2])
            ga, gk, gv, gqn, gkn = vjp(d_)
            dq_.append(ga)
            dk_.append(gk)
            dv_.append(gv)
            dqn, dkn = dqn + gqn, dkn + gkn
        return [_cat(dq_)], [_cat(dk_ + dv_), dqn, dkn]

    dxq, dxkv, g["xa_q_norm"], g["xa_k_norm"] = _matmul(dh2, w["xo"], "nt", F32, "xa_o_dx_attn_bwd", epilogue=_Epilogue(
        xa_bwd, [_row(xq)], [xkv, w["xa_q_norm"], w["xa_k_norm"]], [(512, BF16)], [xkv.shape, (1, 128), (1, 128)]))
    g["xa_w_q"] = _matmul(hn, dxq, "tn", BF16, "xa_q_dw")
    dh1, g["norm_xa"] = _matmul(dxq, w["xq"], "nt", F32, "xa_q_dx_norm_bwd",
                                epilogue=_norm_bwd_epilogue(h1, w["norm_xa"], dh2))
    g["xa_w_kv"] = _matmul(mn, dxkv, "tn", BF16, "xa_kv_dw")
    dmn = _matmul(dxkv, w["xkv"], "nt", F32, "xa_kv_dx")
    _, g["norm_mem"] = _norm_bwd(mem, w["norm_mem"], dmn, dmn, "norm_mem_bwd")

    g["w_out"] = _matmul(cat, dh1, "tn", BF16, "out_dw")
    def mix_bwd(dcat_, rows, k):
        do_, dog_ = [], []
        dgn = 0.0
        for o, g_, d_ in zip(_heads(rows[0], GLA_HEADS), _heads(rows[1], GLA_HEADS), _heads(dcat_, GLA_HEADS)):
            _, vjp = jax.vjp(_mix_head, o, g_, k[0])
            a, b, gn_ = vjp(d_)
            do_.append(a)
            dog_.append(b)
            dgn = dgn + gn_
        return [_cat(do_), _cat(dog_), dcat_[:, 512:]], [dgn]

    do_gla, d_og, do_mla, g["gla_out_norm"] = _matmul(dh1, w["out"], "nt", F32, "out_dx_mix_bwd", epilogue=_Epilogue(
        mix_bwd, [_row(o_gla), og], [w["gla_out_norm"]], [(512, F32), (512, BF16), (512, F32)], [(1, 128)]))

    late_parts = _late_grad_shards(g)
    dq_r, dk_r, dv_mla, lands_late = _attn_bwd(q_r, k_r, v_mla, o_mla, lse, do_mla,
                                               _scatter_plan([late_parts[n] for n in LATE]))
    lands_late = dict(zip(LATE, lands_late, strict=True))

    def qk_bwd(r, k):
        dqs, dks = [], []
        dkpe, dqn, dkn = 0.0, 0.0, 0.0
        for qh, kh, dqh, dkh in zip(_heads(r[0], MLA_HEADS), _heads(r[1], MLA_HEADS), _heads(r[6], MLA_HEADS),
                                    _heads(r[7], MLA_HEADS)):
            _, vjp = jax.vjp(lambda a, b, e, f, h_: _qk_head(a, b, e, r[3], r[4], r[5], f, h_), qh, kh, r[2], k[0], k[1])
            ga, gb, ge, gf, gh = vjp((dqh, dkh))
            dqs.append(ga)
            dks.append(gb)
            dkpe, dqn, dkn = dkpe + ge, dqn + gf, dkn + gh
        return [_cat(dqs), _cat(dks), dkpe], [dqn, dkn]

    dq_up, dk_up, d_kpe, g["q_norm"], g["k_norm"] = _rows_call(
        qk_bwd, [_row(q_up), _row(k_up), kpe] + tabs + [_row(dq_r), _row(dk_r)], [w["q_norm"], w["k_norm"]],
        [(1024, BF16), (1024, BF16), (128, BF16)], [(1, 128), (1, 128)], name="mla_qk_bwd")
    g["uq"] = _matmul(q_lat, dq_up, "tn", BF16, "mla_q_dw")
    dq_lat = _matmul(dq_up, w["uq"], "nt", F32, "mla_q_dx")
    g["k"] = _matmul(kv_lat, dk_up, "tn", BF16, "mla_k_dw")
    g["v"] = _matmul(kv_lat, dv_mla, "tn", BF16, "mla_v_dw")
    dkv_lat = _matmul(dk_up, w["k"], "nt", F32, "mla_k_dx")
    dkv_lat = _matmul(dv_mla, w["v"], "nt", F32, "mla_v_dx", residual=dkv_lat)

    dgq, dgk, dla, dgv, _ = _gla_bwd(proj, la, states, do_gla, _Comm([], [], [], lambda *args: None, lambda *args: None))

    def dproj_body(r, k):
        alr_, cq_, ckv_, dla_, dq_lat_, dkv_lat_, dgq_, dgk_, dgv_, d_og_, d_kpe_ = r
        _, gate_vjp = jax.vjp(_gate_fn, alr_, k[0], k[1])
        d_alr, gw2, gb = gate_vjp(dla_)
        _, q_vjp = jax.vjp(_rms, cq_, k[2])
        _, kv_vjp = jax.vjp(_rms, ckv_, k[3])
        d_cq, gqa = q_vjp(dq_lat_)
        d_ckv, gkva = kv_vjp(dkv_lat_)
        pieces = [dgq_, dgk_, dgv_, d_og_, d_cq, d_ckv, d_kpe_, d_alr]
        return [_cat([x_.astype(BF16) for x_ in pieces])], [gw2, gb, gqa, gkva]

    dproj, g["w2"], g["gla_gate_b"], g["mla_q_a_norm"], g["mla_kv_a_norm"] = _rows_call(
        dproj_body, [alr, cq, ckv, _row(dla), _row(dq_lat), _row(dkv_lat), _row(dgq), _row(dgk), _row(dgv), _row(d_og),
                     _row(d_kpe)], [w["w2"], w["gate_b"], w["q_a_norm"], w["kv_a_norm"]], [(P_WIDTH, BF16)],
        [(128, 256), (1, 256), (1, 256), (1, 128)], name="proj_cotangent")
    g["in"] = _matmul(xn, dproj, "tn", BF16, "proj_dw")
    dx, g["norm_mix"] = _matmul(dproj, w["in"], "nt", F32, "proj_dx_norm_bwd",
                                epilogue=_norm_bwd_epilogue(x, w["norm_mix"], dh1))
    return loss[0, 0], dx, g, lands_late


def _join_shards(pieces, axis):
    if axis == 0:
        return pieces.reshape(-1, pieces.shape[2])
    return jnp.transpose(pieces, (1, 0, 2)).reshape(pieces.shape[1], -1)


def _split_shards(full, axis):
    r, c = full.shape
    if axis == 0:
        return full.reshape(4, r // 4, c)
    return jnp.transpose(full.reshape(r, 4, c // 4), (1, 0, 2))


def _early_layout(gath, rep):
    w_in = _join_shards(gath["w_in"], 1)
    z = lambda n: jnp.zeros((D_MODEL, n), w_in.dtype)
    seg = lambda lo, n: w_in[:, lo:lo + n]
    ukv = _join_shards(gath["mla_w_ukv"], 1).reshape(MLA_KV_RANK, MLA_HEADS, MLA_NOPE + MLA_V)
    w = {
        "in": jnp.concatenate([seg(N_GQ, 256), seg(N_GK, 256), seg(N_GV, 512), seg(N_OG, 512), seg(N_CQ, 256),
                               seg(N_CKV, 128), z(64), seg(N_KPE, 32), z(32), seg(N_ALR, 16), z(112)], axis=1),
        "uq": jnp.pad(_join_shards(gath["mla_w_uq"], 1).reshape(MLA_Q_RANK, MLA_HEADS, MLA_QK),
                      ((0, 0), (0, 0), (0, LANES - MLA_QK))).reshape(MLA_Q_RANK, MLA_HEADS * LANES),
        "k": jnp.pad(ukv[:, :, :MLA_NOPE], ((0, 0), (0, 0), (0, LANES - MLA_NOPE))).reshape(MLA_KV_RANK, -1),
        "v": ukv[:, :, MLA_NOPE:].reshape(MLA_KV_RANK, MLA_HEADS * MLA_V),
        "w2": jnp.pad(_join_shards(gath["gla_gate_w2"], 1), ((0, LANES - GLA_RANK), (0, 0))),
        "cb": rep["ffn_conv_b"].reshape(4, 1, D_FF // 4),
        "q_norm": jnp.pad(rep["mla_q_norm"], ((0, 0), (0, LANES - MLA_QK))),
        "k_norm": jnp.pad(rep["mla_k_norm"], ((0, 0), (0, LANES - MLA_QK))),
        "q_a_norm": rep["mla_q_a_norm"], "kv_a_norm": rep["mla_kv_a_norm"], "gate_b": rep["gla_gate_b"],
    }
    for n in ("norm_mix", "gla_out_norm", "norm_xa", "norm_mem", "xa_q_norm", "xa_k_norm", "norm_ffn"):
        w[n] = rep[n]
    return w


def _late_layout(gath):
    return {"out": _join_shards(gath["w_out"], 0), "xq": _join_shards(gath["xa_w_q"], 0),
            "xkv": _join_shards(gath["xa_w_kv"], 0), "xo": _join_shards(gath["xa_w_o"], 1),
            "wg": gath["ffn_w_gate"], "wu": gath["ffn_w_up"], "wd": gath["ffn_w_down"], "cw": gath["ffn_conv_w"]}


def _late_grad_shards(g):
    sh = {"w_out": _split_shards(g["w_out"], 0), "xa_w_q": _split_shards(g["xa_w_q"], 0),
          "xa_w_kv": _split_shards(g["xa_w_kv"], 0), "xa_w_o": _split_shards(g["xa_w_o"], 1),
          "ffn_w_gate": g["ffn_w_gate"], "ffn_w_up": g["ffn_w_up"], "ffn_conv_w": g["ffn_conv_w"],
          "ffn_w_down": g["ffn_w_down"]}
    return {n: v.astype(BF16) for n, v in sh.items()}


def _early_grad_shards(g):
    gi = g["in"]
    seg = lambda lo, n: gi[:, lo:lo + n]
    w_in = jnp.concatenate([seg(P_GQ, 256), seg(P_GK, 256), seg(P_GV, 512), seg(P_ALR, 16), seg(P_OG, 512),
                            seg(P_CQ, 256), seg(P_CKV, 128), seg(P_KPE + 64, 32)], axis=1)
    uq = g["uq"].reshape(MLA_Q_RANK, MLA_HEADS, LANES)[:, :, :MLA_QK].reshape(MLA_Q_RANK, -1)
    ukv = jnp.concatenate([g["k"].reshape(MLA_KV_RANK, MLA_HEADS, LANES)[:, :, :MLA_NOPE],
                           g["v"].reshape(MLA_KV_RANK, MLA_HEADS, MLA_V)], axis=2).reshape(MLA_KV_RANK, -1)
    sh = {"w_in": _split_shards(w_in, 1), "gla_gate_w2": _split_shards(g["w2"][:GLA_RANK], 1),
          "mla_w_uq": _split_shards(uq, 1), "mla_w_ukv": _split_shards(ukv, 1)}
    sh = {n: v.astype(BF16) for n, v in sh.items()}
    rep = {n: g[n] for n in REPLICATED if n in g}
    rep["mla_q_norm"] = g["q_norm"][:, :MLA_QK]
    rep["mla_k_norm"] = g["k_norm"][:, :MLA_QK]
    rep["ffn_conv_b"] = g["ffn_conv_b"].reshape(1, D_FF)
    return sh, rep


SMALL_SHAPE = (8, 1024)


def _pack_small(vectors):
    flat = jnp.concatenate(vectors, axis=1)
    return jnp.pad(flat, ((0, 0), (0, SMALL_SHAPE[0] * SMALL_SHAPE[1] - flat.shape[1]))).reshape(SMALL_SHAPE)


def _unpack_small(buf, widths):
    flat = buf.reshape(1, -1)
    out, off = [], 0
    for wd in widths:
        out.append(flat[:, off:off + wd])
        off += wd
    return out


ANY = pl.BlockSpec(memory_space=pl.ANY)


def _place():
    x, y, c = lax.axis_index("x"), lax.axis_index("y"), lax.axis_index("c")
    chips = [(1 - x, y), (x, 1 - y), (1 - x, 1 - y)]
    return x, y, c, chips


class _Comm:
    def __init__(self, ins, out_shape, sems, start, finish, mid=None):
        self.ins, self.out_shape, self.sems = list(ins), list(out_shape), list(sems)
        self.start, self.finish, self.mid = start, finish, mid or (lambda *args: None)


def _run_comm(plan, name):
    ni, no = len(plan.ins), len(plan.out_shape)

    def body(*refs):
        ins, outs, sems = refs[:ni], refs[ni:ni + no], refs[ni + no:]
        place = _place()
        plan.start(place, ins, outs, sems)
        plan.mid(place, ins, outs, sems)
        plan.finish(place, ins, outs, sems)

    return pl.pallas_call(body, in_specs=[ANY] * ni, out_specs=[ANY] * no, out_shape=plan.out_shape,
                          scratch_shapes=plan.sems, name=name)(*plan.ins)


def _gather_plan(shards):
    n = len(shards)
    split = [s.shape[0] % (2 * BF16_ROWS) == 0 for s in shards]

    def rows(ref, t, c):
        if not split[t]:
            return ref
        half = shards[t].shape[0] // 2
        return ref.at[pl.ds(pl.multiple_of(c * half, BF16_ROWS), half)]

    def remote(src, dst, ss, rs, to):
        return pltpu.make_async_remote_copy(src_ref=src, dst_ref=dst, send_sem=ss, recv_sem=rs, device_id=to,
                                            device_id_type=MESH)

    def first_wave(place, ins, outs, sems):
        x, y, c, chips = place
        ici_s, ici_r, _, _, local = sems
        me = 2 * x + y
        own = [pltpu.make_async_copy(ins[t], outs[t].at[me], local.at[t]) for t in range(n)]
        push = [remote(rows(ins[t], t, c), rows(outs[t].at[me], t, c), ici_s.at[3 * t + j], ici_r.at[3 * t + j], (px, py, c))
                for t in range(n) for j, (px, py) in enumerate(chips)]
        return own, push

    def second_wave(place, ins, outs, sems, last):
        x, y, c, chips = place
        ici_s, ici_r, d2d_s, d2d_r, local = sems
        sib = (x, y, 1 - c)
        out = []
        for t in range(n):
            for j, (px, py) in enumerate(chips):
                block = outs[t].at[2 * px + py]
                got = rows(block, t, c)
                if split[t]:
                    hand = remote(got, got, d2d_s.at[3 * t + j], d2d_r.at[3 * t + j], sib)
                    theirs = rows(block, t, 1 - c)
                    other = (remote(theirs, theirs, local.at[0], d2d_r.at[3 * t + j], sib) if last else
                             remote(got, got, local.at[0], ici_r.at[3 * t + j], sib))
                    out.append((other, hand))
                elif last:
                    out.append((remote(got, got, local.at[0], ici_r.at[3 * t + j], sib), None))
        return out

    def start(place, ins, outs, sems):
        own, push = first_wave(place, ins, outs, sems)
        for cp in own + push:
            cp.start()

    def mid(place, ins, outs, sems):
        for arrival, hand in second_wave(place, ins, outs, sems, False):
            arrival.wait_recv()
            hand.start()

    def finish(place, ins, outs, sems):
        own, push = first_wave(place, ins, outs, sems)
        for arrival, hand in second_wave(place, ins, outs, sems, True):
            arrival.wait_recv()
            if hand is not None:
                hand.wait_send()
        for cp in push:
            cp.wait_send()
        for cp in own:
            cp.wait()

    dma = pltpu.SemaphoreType.DMA
    return _Comm(shards, [jax.ShapeDtypeStruct((4,) + s.shape, s.dtype) for s in shards],
                 [dma((3 * n,)), dma((3 * n,)), dma((3 * n,)), dma((3 * n,)), dma((n,))], start, finish, mid)


def _scatter_plan(parts, small=None):
    n = len(parts)
    ns = 0 if small is None else 1

    def unpack(place, ins, outs, sems):
        x, y, c, chips = place
        return x, y, c, chips, 2 * x + y, 4 * x + 2 * y + c, (x, y, 1 - c)

    def remote(src, dst, ss, rs, to):
        return pltpu.make_async_remote_copy(src_ref=src, dst_ref=dst, send_sem=ss, recv_sem=rs, device_id=to,
                                            device_id_type=MESH)

    def first_wave(place, ins, outs, sems):
        x, y, c, chips, me, dev, sib = unpack(place, ins, outs, sems)
        ici_s, ici_r, d2d_s, d2d_r, sm_s, sm_r, local = sems
        own, push = [], []
        if ns:
            own.append(pltpu.make_async_copy(ins[n], outs[n].at[dev], local.at[n]))
            for k in range(1, 8):
                px = (1 - x) if (k >> 2) & 1 else x
                py = (1 - y) if (k >> 1) & 1 else y
                pc = (1 - c) if k & 1 else c
                push.append(remote(ins[n], outs[n].at[dev], sm_s.at[k - 1], sm_r.at[k - 1], (px, py, pc)))
        for t in range(n):
            own.append(pltpu.make_async_copy(ins[t].at[me], outs[t].at[dev], local.at[t]))
            push.append(remote(ins[t].at[me], outs[t].at[dev], d2d_s.at[4 * t], d2d_r.at[4 * t], sib))
            for j, (px, py) in enumerate(chips):
                push.append(remote(ins[t].at[2 * px + py], outs[t].at[dev], ici_s.at[3 * t + j], ici_r.at[3 * t + j],
                                   (px, py, c)))
        return own, push

    def start(place, ins, outs, sems):
        own, push = first_wave(place, ins, outs, sems)
        for cp in own + push:
            cp.start()

    def landed(dst, rs, sems, sib):
        remote(dst, dst, sems[-1].at[0], rs, sib).wait_recv()

    def forwards(place, ins, outs, sems):
        x, y, c, chips, me, dev, sib = unpack(place, ins, outs, sems)
        d2d_s, d2d_r = sems[2], sems[3]
        slots = [(t, j, outs[t].at[4 * px + 2 * py + c]) for t in range(n) for j, (px, py) in enumerate(chips)]
        return [(t, j, slot, remote(slot, slot, d2d_s.at[4 * t + 1 + j], d2d_r.at[4 * t + 1 + j], sib))
                for t, j, slot in slots]

    def mid(place, ins, outs, sems):
        sib = unpack(place, ins, outs, sems)[-1]
        for t, j, slot, cp in forwards(place, ins, outs, sems):
            landed(slot, sems[1].at[3 * t + j], sems, sib)
            cp.start()

    def finish(place, ins, outs, sems):
        x, y, c, chips, me, dev, sib = unpack(place, ins, outs, sems)
        d2d_r, sm_r = sems[3], sems[5]
        own, push = first_wave(place, ins, outs, sems)
        push += [cp for _, _, _, cp in forwards(place, ins, outs, sems)]
        for t in range(n):
            landed(outs[t].at[4 * x + 2 * y + (1 - c)], d2d_r.at[4 * t], sems, sib)
            for j, (px, py) in enumerate(chips):
                landed(outs[t].at[4 * px + 2 * py + (1 - c)], d2d_r.at[4 * t + 1 + j], sems, sib)
        if ns:
            for k in range(1, 8):
                px = (1 - x) if (k >> 2) & 1 else x
                py = (1 - y) if (k >> 1) & 1 else y
                pc = (1 - c) if k & 1 else c
                landed(outs[n].at[4 * px + 2 * py + pc], sm_r.at[k - 1], sems, sib)
        for cp in push:
            cp.wait_send()
        for cp in own:
            cp.wait()

    dma = pltpu.SemaphoreType.DMA
    ins = list(parts) + ([small] if ns else [])
    out_shape = [jax.ShapeDtypeStruct((8,) + p.shape[1:], p.dtype) for p in parts]
    if ns:
        out_shape.append(jax.ShapeDtypeStruct((8,) + small.shape, small.dtype))
    return _Comm(ins, out_shape, [dma((3 * n,)), dma((3 * n,)), dma((4 * n,)), dma((4 * n,)), dma((7,)), dma((7,)),
                                  dma((n + 1,))], start, finish, mid)


ADAM_ROWS = 288


def _row_tile(r, cap):
    if r <= cap:
        return r
    return max(t for t in range(8, cap + 1, 8) if r % t == 0)


def _adamw_update(w, m, v, land):
    g = land[0].astype(F32)
    for i in range(1, 8):
        g = g + land[i].astype(F32)
    m_new = ADAM_B1 * m + (1.0 - ADAM_B1) * g
    v_new = ADAM_B2 * v + (1.0 - ADAM_B2) * (g * g)
    m_hat = m_new / (1.0 - ADAM_B1 ** ADAM_STEP)
    v_hat = v_new / (1.0 - ADAM_B2 ** ADAM_STEP)
    return g, -ADAM_LR * (m_hat / (jnp.sqrt(v_hat) + ADAM_EPS) + ADAM_WD * w), m_new, v_new


def _adamw(tensors, name, comm=None):
    k = len(tensors)
    r, c = tensors[0][0].shape
    t = _row_tile(r, ADAM_ROWS // k)
    n = r // t
    nci, nco, nsem = (len(comm.ins), len(comm.out_shape), len(comm.sems)) if comm else (0, 0, 0)

    def kern(*refs):
        ins, cins, outs, couts, csems = _split_refs(refs, (4 * k, nci, 4 * k, nco, nsem))
        if comm:
            place = _place()

            @pl.when(pl.program_id(0) == 0)
            def _():
                comm.start(place, cins, couts, csems)

        for i in range(k):
            w_ref, m_ref, v_ref, l_ref = ins[4 * i:4 * i + 4]
            res = _adamw_update(w_ref[...], m_ref[...], v_ref[...], l_ref)
            for ref, val in zip(outs[4 * i:4 * i + 4], res, strict=True):
                ref[...] = val
        if comm:
            @pl.when(pl.program_id(0) == n - 1)
            def _():
                comm.mid(place, cins, couts, csems)
                comm.finish(place, cins, couts, csems)

    spec = pl.BlockSpec((t, c), lambda i: (i, 0))
    lspec = pl.BlockSpec((8, t, c), lambda i: (0, i, 0))
    res = pl.pallas_call(
        kern, grid=(n,), in_specs=[spec, spec, spec, lspec] * k + [ANY] * nci, out_specs=[spec] * (4 * k) + [ANY] * nco,
        out_shape=[jax.ShapeDtypeStruct((r, c), F32)] * (4 * k) + (comm.out_shape if comm else []),
        scratch_shapes=comm.sems if comm else [],
        compiler_params=pltpu.CompilerParams(dimension_semantics=("arbitrary" if comm else "parallel",),
                                             vmem_limit_bytes=VMEM_LIMIT),
        name=name)(*[x for tens in tensors for x in tens], *(comm.ins if comm else []))
    return [res[4 * i:4 * i + 4] for i in range(k)], res[4 * k:]


def _step(a):
    def sq(n):
        v = a[n][0] if a[n].ndim == 3 else a[n]
        return v.T if n.removeprefix("m_").removeprefix("v_") in TRANSPOSED else v

    payload = lambda n: sq(n) if n in EXACT_GATHER else sq(n).astype(BF16)

    loss, dx, g, lands_late = _local_step(sq("x"), sq("mem"), a["positions"][0], sq("loss_target"),
                                          {n: a[n] for n in REPLICATED}, [payload(n) for n in EARLY],
                                          [payload(n) for n in LATE])

    sh, rep = _early_grad_shards(g)
    small = _pack_small([rep[n] for n in REPLICATED] + [loss.reshape(1, 1)])
    *lands_early, land_small = _run_comm(_scatter_plan([sh[n] for n in EARLY], small), "scatter_last")
    quad = lambda n, land: (sq(n), sq("m_" + n), sq("v_" + n), land)
    lands = dict(zip(EARLY, lands_early, strict=True)) | lands_late

    outs = {}
    kinds = ("grad_", "delta_", "new_m_", "new_v_")
    for n, _ in SHARDED:
        res = _adamw([quad(n, lands[n])], "adamw_" + n)[0][0]
        for kind, val in zip(kinds, res, strict=True):
            outs[kind + n] = (val.T if n in TRANSPOSED else val).reshape(a[n].shape)
    zero = jnp.zeros((1, 1), F32)
    packed = [_pack_small([a[p + n] for n in REPLICATED] + [zero]) for p in ("", "m_", "v_")]
    res = _adamw([(*packed, land_small)], "adamw_replicated")[0][0]
    widths = [a[n].shape[1] for n in REPLICATED] + [1]
    for kind, buf in zip(kinds, res, strict=True):
        *vals, total = _unpack_small(buf, widths)
        for n, val in zip(REPLICATED, vals, strict=True):
            outs[kind + n] = val
        if kind == "grad_":
            loss = total[0, 0]

    ordered = [outs[kind + n] for kind in kinds for n in WEIGHTS]
    return (loss, dx[None], *ordered)


def kernel(x, mem, positions, norm_mix, w_in, gla_gate_w2, gla_gate_b, gla_out_norm, mla_q_a_norm, mla_w_uq, mla_kv_a_norm, mla_w_ukv, mla_q_norm, mla_k_norm, w_out, norm_xa, norm_mem, xa_w_q, xa_w_kv, xa_q_norm, xa_k_norm, xa_w_o, norm_ffn, ffn_w_gate, ffn_w_up, ffn_conv_w, ffn_conv_b, ffn_w_down, loss_target, m_norm_mix, m_w_in, m_gla_gate_w2, m_gla_gate_b, m_gla_out_norm, m_mla_q_a_norm, m_mla_w_uq, m_mla_kv_a_norm, m_mla_w_ukv, m_mla_q_norm, m_mla_k_norm, m_w_out, m_norm_xa, m_norm_mem, m_xa_w_q, m_xa_w_kv, m_xa_q_norm, m_xa_k_norm, m_xa_w_o, m_norm_ffn, m_ffn_w_gate, m_ffn_w_up, m_ffn_conv_w, m_ffn_conv_b, m_ffn_w_down, v_norm_mix, v_w_in, v_gla_gate_w2, v_gla_gate_b, v_gla_out_norm, v_mla_q_a_norm, v_mla_w_uq, v_mla_kv_a_norm, v_mla_w_ukv, v_mla_q_norm, v_mla_k_norm, v_w_out, v_norm_xa, v_norm_mem, v_xa_w_q, v_xa_w_kv, v_xa_q_norm, v_xa_k_norm, v_xa_w_o, v_norm_ffn, v_ffn_w_gate, v_ffn_w_up, v_ffn_conv_w, v_ffn_conv_b, v_ffn_w_down):
    return _step(dict(locals()))
```

```python
import functools

import jax
import jax.numpy as jnp
import numpy as np
from jax import lax
from jax.experimental import pallas as pl
from jax.experimental.pallas import tpu as pltpu

F32, BF16 = jnp.float32, jnp.bfloat16
MESH = pl.DeviceIdType.MESH

D_MODEL = 1024
EPS = 1e-6
GLA_HEADS, GLA_DK, GLA_DV, GLA_RANK, GLA_CHUNK = 4, 64, 128, 16, 64
GLA_GATE_NORM = 16.0
MLA_HEADS, MLA_Q_RANK, MLA_KV_RANK, MLA_NOPE, MLA_ROPE, MLA_V = 8, 256, 128, 64, 32, 64
MLA_QK = MLA_NOPE + MLA_ROPE
ROPE_THETA = 10000.0
LOG2E, LN2 = 1.4426950408889634, 0.6931471805599453
XA_HEADS, XA_DIM = 4, 128
D_FF = 2816
ADAM_LR, ADAM_B1, ADAM_B2, ADAM_EPS, ADAM_WD, ADAM_STEP = 0.001, 0.9, 0.999, 1e-08, 0.01, 10

LANES = 128
BF16_ROWS = 16
VMEM_LIMIT = 56 * 1024 * 1024
MATMUL_VMEM = 44 * 1024 * 1024

P_GQ, P_GK, P_GV, P_OG, P_CQ, P_CKV, P_KPE, P_ALR, P_WIDTH = 0, 256, 512, 1024, 1536, 1792, 1920, 2048, 2176
N_GQ, N_GK, N_GV, N_ALR, N_OG, N_CQ, N_CKV, N_KPE, N_WIDTH = 0, 256, 512, 1024, 1040, 1552, 1808, 1936, 1968

SHARDED = (("w_in", 1), ("gla_gate_w2", 1), ("mla_w_uq", 1), ("mla_w_ukv", 1), ("w_out", 0), ("xa_w_q", 0),
           ("xa_w_kv", 0), ("xa_w_o", 1), ("ffn_w_gate", 1), ("ffn_w_up", 1), ("ffn_conv_w", 1), ("ffn_w_down", 0))
REPLICATED = ("norm_mix", "gla_gate_b", "gla_out_norm", "mla_q_a_norm", "mla_kv_a_norm", "mla_q_norm", "mla_k_norm",
              "norm_xa", "norm_mem", "xa_q_norm", "xa_k_norm", "norm_ffn", "ffn_conv_b")
EXACT_GATHER = ("gla_gate_w2", "ffn_conv_w")
TRANSPOSED = ("w_in", "ffn_w_gate", "ffn_w_up")
EARLY = ("w_in", "gla_gate_w2", "mla_w_uq", "mla_w_ukv")
LATE = tuple(n for n, _ in SHARDED if n not in EARLY)
WEIGHTS = ("norm_mix", "w_in", "gla_gate_w2", "gla_gate_b", "gla_out_norm", "mla_q_a_norm", "mla_w_uq",
           "mla_kv_a_norm", "mla_w_ukv", "mla_q_norm", "mla_k_norm", "w_out", "norm_xa", "norm_mem", "xa_w_q",
           "xa_w_kv", "xa_q_norm", "xa_k_norm", "xa_w_o", "norm_ffn", "ffn_w_gate", "ffn_w_up", "ffn_conv_w",
           "ffn_conv_b", "ffn_w_down")


_NN = ((1,), (0,))
_NT = ((1,), (1,))
_TN = ((0,), (0,))


def _dg(a, b, dims):
    return lax.dot_general(a.astype(BF16), b.astype(BF16), (dims, ((), ())), preferred_element_type=F32)


@jax.custom_vjp
def _dot_nn(a, b):
    return _dg(a, b, _NN)


_dot_nn.defvjp(lambda a, b: (_dg(a, b, _NN), (a, b)),
               lambda r, g: (_dg(g, r[1], _NT).astype(r[0].dtype), _dg(r[0], g, _TN).astype(r[1].dtype)))


@jax.custom_vjp
def _dot_nt(a, b):
    return _dg(a, b, _NT)


_dot_nt.defvjp(lambda a, b: (_dg(a, b, _NT), (a, b)),
               lambda r, g: (_dg(g, r[1], _NN).astype(r[0].dtype), _dg(g, r[0], _TN).astype(r[1].dtype)))


@jax.custom_vjp
def _dot_tn(a, b):
    return _dg(a, b, _TN)


_dot_tn.defvjp(lambda a, b: (_dg(a, b, _TN), (a, b)),
               lambda r, g: (_dg(r[1], g, _NT).astype(r[0].dtype), _dg(r[0], g, _NN).astype(r[1].dtype)))


def _rms(x, w, n=None):
    n = x.shape[-1] if n is None else n
    ms = jnp.sum(x * x, axis=-1, keepdims=True) * (1.0 / n)
    return x * lax.rsqrt(ms + EPS) * w


def _silu(x):
    return x * jax.nn.sigmoid(x)


def _log_sigmoid(x):
    return jnp.minimum(x, 0.0) - jnp.log(1.0 + jnp.exp(-jnp.abs(x)))


@jax.custom_vjp
def _rope(y, c, sa, sb):
    return y * c + pltpu.roll(y, LANES - 16, 1) * sa + pltpu.roll(y, 16, 1) * sb


def _rope_bwd(res, g):
    c, sa, sb = res
    gy = g * c + pltpu.roll(g * sa, 16, 1) + pltpu.roll(g * sb, LANES - 16, 1)
    return gy, jnp.zeros_like(c), jnp.zeros_like(sa), jnp.zeros_like(sb)


_rope.defvjp(lambda y, c, sa, sb: (_rope(y, c, sa, sb), (c, sa, sb)), _rope_bwd)


@jax.custom_vjp
def _cumsum_rows(x):
    n = x.shape[0]
    row = lax.broadcasted_iota(jnp.int32, x.shape, 0)
    k = 1
    while k < n:
        x = x + jnp.where(row >= k, pltpu.roll(x, k, 0), 0.0)
        k *= 2
    return x


def _cumsum_rows_bwd(_, g):
    n = g.shape[0]
    row = lax.broadcasted_iota(jnp.int32, g.shape, 0)
    k = 1
    while k < n:
        g = g + jnp.where(row < n - k, pltpu.roll(g, n - k, 0), 0.0)
        k *= 2
    return (g,)


_cumsum_rows.defvjp(lambda x: (_cumsum_rows(x), None), _cumsum_rows_bwd)


def _lane_mask(lo, hi):
    lane = lax.broadcasted_iota(jnp.int32, (1, LANES), 1)
    return ((lane >= lo) & (lane < hi)).astype(F32)


def _tile(n, t):
    t = min(n, t)
    assert n % t == 0, (n, t)
    return t


class _Epilogue:
    def __init__(self, fn, rows=(), consts=(), outs=(), accs=()):
        self.fn, self.rows, self.consts, self.outs, self.accs = fn, list(rows), list(consts), list(outs), list(accs)


def _matmul(a, b, mode, out_dtype, name, residual=None, a_lead=None, b_lead=None, more=None, epilogue=None):
    (a0, a1), (b0, b1) = a.shape[-2:], b.shape[-2:]
    if mode == "nn":
        m, k, k2, n = a0, a1, b0, b1
    elif mode == "nt":
        m, k, n, k2 = a0, a1, b0, b1
    else:
        k, m, k2, n = a0, a1, b0, b1
    assert k == k2, (a.shape, b.shape, mode)
    npar = 4 if "p" in (a_lead, b_lead) else 1
    nsum = 4 if "k" in (a_lead, b_lead) else 1
    pairs = [(a, b)] + ([more] if more else [])
    a_item, b_item, o_item = a.dtype.itemsize, b.dtype.itemsize, jnp.dtype(out_dtype).itemsize
    ep = epilogue
    row_extra = 4 if residual is not None else 0
    if ep:
        row_extra += (sum(r.dtype.itemsize * wd for r, wd, _ in ep.rows) + sum(jnp.dtype(d).itemsize * wd for wd, d in ep.outs)) / n

    def vmem_need(tm, tn, tk):
        need = 2 * (nsum if a_lead == "k" else 1) * tm * tk * a_item + 2 * (nsum if b_lead == "k" else 1) * tk * tn * b_item
        need *= len(pairs)
        need += (0 if ep else 2 * tm * tn * o_item) + tm * tn * 4 * (2 if tk < k else 1)
        need += tm * tk * 2 * (a_item == 4 or mode == "tn") + tk * tn * 2 * (b_item == 4)
        return need + int(2 * tm * tn * row_extra) + (3 * tm * tn * 4 if ep else 0)

    halvings = (4096, 2048, 1024, 512, 256, 128, 64, 32, 16, 8)
    if mode == "tn":
        tm = m if m <= 2304 else m // 2
        tn = n if tm * n <= 1024 * 2304 else n // 2
        tk = next((r for r in halvings if k % r == 0 and vmem_need(tm, tn, r) <= MATMUL_VMEM), k)
    else:
        tn, tk = n, k
        tm = next((r for r in halvings if m % r == 0 and vmem_need(r, tn, tk) <= MATMUL_VMEM), m)
    assert m % tm == 0 and n % tn == 0 and k % tk == 0
    assert ep is None or (tn == n and tk == k and npar == 1)
    nk = k // tk
    dims = {"nn": _NN, "nt": _NT, "tn": _TN}[mode]
    n_in = 2 * len(pairs) + (residual is not None)
    n_ep_in = len(ep.rows) + len(ep.consts) if ep else 0
    n_out = len(ep.outs) + len(ep.accs) if ep else 1

    def body(*refs):
        ab, rs, ep_in, outs, scratch = _split_refs(refs, (2 * len(pairs), n_in - 2 * len(pairs), n_ep_in, n_out, nk > 1))
        prod = None
        for a_ref, b_ref in zip(ab[0::2], ab[1::2]):
            for sh in range(nsum):
                term = _dg(a_ref[sh] if a_lead == "k" else a_ref[...], b_ref[sh] if b_lead == "k" else b_ref[...], dims)
                prod = term if prod is None else prod + term

        def finish(r):
            if rs:
                r = r + rs[0][...]
            if ep is None:
                outs[0][...] = r.astype(outs[0].dtype)
                return
            vals = [x[...] for x in ep_in]
            ro, ao = ep.fn(r, vals[:len(ep.rows)], vals[len(ep.rows):])
            for ref, val in zip(outs[:len(ep.outs)], ro, strict=True):
                ref[...] = val.astype(ref.dtype)
            if ep.accs:
                @pl.when(pl.program_id(0) == 0)
                def _():
                    for ref in outs[len(ep.outs):]:
                        ref[...] = jnp.zeros_like(ref)

                for ref, val in zip(outs[len(ep.outs):], ao, strict=True):
                    ref[...] += val

        if nk == 1:
            finish(prod)
            return
        acc = scratch[0]
        kk = pl.program_id(3)

        @pl.when(kk == 0)
        def _():
            acc[...] = prod

        @pl.when(kk > 0)
        def _():
            acc[...] += prod

        @pl.when(kk == nk - 1)
        def _():
            finish(acc[...])

    def spec(lead, blk, idx):
        if lead is None:
            return pl.BlockSpec(blk, lambda i, j, p, kk: idx(i, j, kk))
        if lead == "p":
            return pl.BlockSpec((None,) + blk, lambda i, j, p, kk: (p,) + idx(i, j, kk))
        return pl.BlockSpec((nsum,) + blk, lambda i, j, p, kk: (0,) + idx(i, j, kk))

    if mode == "nn":
        pair_specs = [spec(a_lead, (tm, tk), lambda i, j, kk: (i, kk)), spec(b_lead, (tk, tn), lambda i, j, kk: (kk, j))]
    elif mode == "nt":
        pair_specs = [spec(a_lead, (tm, tk), lambda i, j, kk: (i, kk)), spec(b_lead, (tn, tk), lambda i, j, kk: (j, kk))]
    else:
        pair_specs = [spec(a_lead, (tk, tm), lambda i, j, kk: (kk, i)), spec(b_lead, (tk, tn), lambda i, j, kk: (kk, j))]
    tile = spec(None, (tm, tn), lambda i, j, kk: (i, j))
    in_specs = pair_specs * len(pairs)
    args = [x for pair in pairs for x in pair]
    if residual is not None:
        assert npar == 1
        in_specs.append(tile)
        args.append(residual)
    if ep:
        in_specs += [pl.BlockSpec((tm, wd), functools.partial(lambda cb, i, j, p, kk: (i, cb), cb)) for _, wd, cb in ep.rows]
        in_specs += [pl.BlockSpec(c.shape, lambda i, j, p, kk: (0, 0)) for c in ep.consts]
        args += [r for r, _, _ in ep.rows] + ep.consts
        out_specs = [pl.BlockSpec((tm, wd), lambda i, j, p, kk: (i, 0)) for wd, _ in ep.outs]
        out_specs += [pl.BlockSpec(shape, lambda i, j, p, kk: (0, 0)) for shape in ep.accs]
        out_shape = [jax.ShapeDtypeStruct((m, wd), d) for wd, d in ep.outs] + [jax.ShapeDtypeStruct(sh, F32) for sh in ep.accs]
    else:
        out_specs = spec("p" if npar > 1 else None, (tm, tn), lambda i, j, kk: (i, j))
        out_shape = jax.ShapeDtypeStruct(((4,) if npar > 1 else ()) + (m, n), out_dtype)
    outer = "arbitrary" if ep and ep.accs else "parallel"
    return pl.pallas_call(
        body, grid=(m // tm, n // tn, npar, nk), in_specs=in_specs, out_specs=out_specs, out_shape=out_shape,
        scratch_shapes=[pltpu.VMEM((tm, tn), F32)] if nk > 1 else [],
        compiler_params=pltpu.CompilerParams(dimension_semantics=(outer, outer, outer, "arbitrary"),
                                             vmem_limit_bytes=VMEM_LIMIT),
        name=name)(*args)


def _row(a, width=None, col_block=0):
    return (a, a.shape[1] if width is None else width, col_block)


def _rows_call(body, rows, consts, outs, accs=(), *, name, tile=512):
    s = rows[0][0].shape[0]
    t = _tile(s, tile)
    nr, nc, no = len(rows), len(consts), len(outs)

    def kern(*refs):
        r = [x[...] for x in refs[:nr]]
        c = [x[...] for x in refs[nr:nr + nc]]
        o_refs = refs[nr + nc:nr + nc + no]
        a_refs = refs[nr + nc + no:]
        ro, ao = body(r, c)
        for ref, val in zip(o_refs, ro, strict=True):
            ref[...] = val.astype(ref.dtype)
        if a_refs:
            @pl.when(pl.program_id(0) == 0)
            def _():
                for ref in a_refs:
                    ref[...] = jnp.zeros_like(ref)

            for ref, val in zip(a_refs, ao, strict=True):
                ref[...] += val

    in_specs = [pl.BlockSpec((t, w), functools.partial(lambda cb, i: (i, cb), cb)) for (_, w, cb) in rows]
    in_specs += [pl.BlockSpec(c.shape, lambda i: (0, 0)) for c in consts]
    out_specs = [pl.BlockSpec((t, w), lambda i: (i, 0)) for (w, _) in outs]
    out_specs += [pl.BlockSpec(shape, lambda i: (0, 0)) for shape in accs]
    out_shape = [jax.ShapeDtypeStruct((s, w), dt) for (w, dt) in outs]
    out_shape += [jax.ShapeDtypeStruct(shape, F32) for shape in accs]
    return pl.pallas_call(
        kern, grid=(s // t,), in_specs=in_specs, out_specs=out_specs, out_shape=out_shape,
        compiler_params=pltpu.CompilerParams(dimension_semantics=("arbitrary" if accs else "parallel",),
                                             vmem_limit_bytes=VMEM_LIMIT),
        name=name)(*[r[0] for r in rows], *consts)


def _gla_chunk(q, k, la, v0, v1, s0, s1):
    c = q.shape[0]
    r = lax.broadcasted_iota(jnp.int32, (c, c), 0)
    cc = lax.broadcasted_iota(jnp.int32, (c, c), 1)
    tril = cc <= r
    cum = _cumsum_rows(la)
    cl = jnp.sum(la, axis=0, keepdims=True)
    qd = q * (GLA_DK ** -0.5) * jnp.exp(cum)
    ki = k * jnp.exp(-cum)
    ke = k * jnp.exp(cl - cum)
    dec = jnp.exp(cl)
    outs, news = [], []
    for h, (v, s) in enumerate(((v0, s0), (v1, s1))):
        mk = _lane_mask(GLA_DK * h, GLA_DK * (h + 1))
        qh = qd * mk
        att = jnp.where(tril, _dot_nt(qh, ki), 0.0)
        outs.append(_dot_nn(att, v) + _dot_nt(qh, s))
        news.append(s * dec + _dot_tn(v, ke * mk))
    return outs[0], outs[1], news[0], news[1]


def _gla_specs(tb, rev_nb=None):
    blk = (lambda b: b) if rev_nb is None else (lambda b: rev_nb - 1 - b)
    q = pl.BlockSpec((tb, 128), lambda p, b: (blk(b), P_GQ // 128 + p))
    k = pl.BlockSpec((tb, 128), lambda p, b: (blk(b), P_GK // 128 + p))
    la = pl.BlockSpec((tb, 128), lambda p, b: (blk(b), p))
    v = pl.BlockSpec((tb, 256), lambda p, b: (blk(b), P_GV // 256 + p))
    o = pl.BlockSpec((tb, 256), lambda p, b: (blk(b), p))
    st = pl.BlockSpec((tb // GLA_CHUNK, 2, 128, 128), lambda p, b: (blk(b), p, 0, 0))
    return q, k, la, v, o, st


def _gla_fwd(proj, la):
    s = proj.shape[0]
    tb = _tile(s, 512)
    nb, nch = s // tb, tb // GLA_CHUNK

    def kern(q_ref, k_ref, la_ref, v_ref, o_ref, st_ref, s_sc):
        @pl.when(pl.program_id(1) == 0)
        def _():
            s_sc[...] = jnp.zeros_like(s_sc)

        s0, s1 = s_sc[0], s_sc[1]
        for ci in range(nch):
            sl = slice(ci * GLA_CHUNK, (ci + 1) * GLA_CHUNK)
            st_ref[ci, 0] = s0
            st_ref[ci, 1] = s1
            o0, o1, s0, s1 = _gla_chunk(q_ref[sl, :], k_ref[sl, :], la_ref[sl, :], v_ref[sl, 0:128],
                                        v_ref[sl, 128:256], s0, s1)
            o_ref[sl, 0:128] = o0
            o_ref[sl, 128:256] = o1
        s_sc[0] = s0
        s_sc[1] = s1

    q, k, lasp, v, o, st = _gla_specs(tb)
    return pl.pallas_call(
        kern, grid=(2, nb), in_specs=[q, k, lasp, v], out_specs=[o, st],
        out_shape=[jax.ShapeDtypeStruct((s, 512), F32),
                   jax.ShapeDtypeStruct((s // GLA_CHUNK, GLA_HEADS, 128, 128), F32)],
        scratch_shapes=[pltpu.VMEM((2, 128, 128), F32)],
        compiler_params=pltpu.CompilerParams(dimension_semantics=("parallel", "arbitrary"),
                                             vmem_limit_bytes=VMEM_LIMIT),
        name="gla_fwd")(proj, proj, la, proj)


def _gla_bwd(proj, la, states, d_o, comm):
    s = proj.shape[0]
    tb = _tile(s, 512)
    nb, nch = s // tb, tb // GLA_CHUNK
    nci, nco = len(comm.ins), len(comm.out_shape)

    def kern(*refs):
        (q_ref, k_ref, la_ref, v_ref, do_ref, st_ref), cins, (dq_ref, dk_ref, dla_ref, dv_ref), couts, (ds_sc,), csems = \
            _split_refs(refs, (6, nci, 4, nco, 1, len(comm.sems)))
        place = _place()
        pair, blk = pl.program_id(0), pl.program_id(1)

        @pl.when((pair == 0) & (blk == 0))
        def _():
            comm.start(place, cins, couts, csems)

        @pl.when((pair == 1) & (blk == nb // 2))
        def _():
            comm.mid(place, cins, couts, csems)

        @pl.when(blk == 0)
        def _():
            ds_sc[...] = jnp.zeros_like(ds_sc)

        d0, d1 = ds_sc[0], ds_sc[1]
        for ci in reversed(range(nch)):
            sl = slice(ci * GLA_CHUNK, (ci + 1) * GLA_CHUNK)
            _, vjp = jax.vjp(_gla_chunk, q_ref[sl, :], k_ref[sl, :], la_ref[sl, :], v_ref[sl, 0:128],
                             v_ref[sl, 128:256], st_ref[ci, 0], st_ref[ci, 1])
            gq, gk, gla, gv0, gv1, d0, d1 = vjp((do_ref[sl, 0:128], do_ref[sl, 128:256], d0, d1))
            dq_ref[sl, :] = gq
            dk_ref[sl, :] = gk
            dla_ref[sl, :] = gla
            dv_ref[sl, 0:128] = gv0
            dv_ref[sl, 128:256] = gv1
        ds_sc[0] = d0
        ds_sc[1] = d1

        @pl.when((pair == 1) & (blk == nb - 1))
        def _():
            comm.finish(place, cins, couts, csems)

    q, k, lasp, v, o, st = _gla_specs(tb, rev_nb=nb)
    res = pl.pallas_call(
        kern, grid=(2, nb), in_specs=[q, k, lasp, v, o, st] + [ANY] * nci, out_specs=[lasp, lasp, lasp, o] + [ANY] * nco,
        out_shape=[jax.ShapeDtypeStruct((s, 256), F32), jax.ShapeDtypeStruct((s, 256), F32),
                   jax.ShapeDtypeStruct((s, 256), F32), jax.ShapeDtypeStruct((s, 512), F32)] + comm.out_shape,
        scratch_shapes=[pltpu.VMEM((2, 128, 128), F32)] + comm.sems,
        compiler_params=pltpu.CompilerParams(dimension_semantics=("arbitrary", "arbitrary"),
                                             vmem_limit_bytes=VMEM_LIMIT),
        name="gla_bwd")(proj, proj, la, proj, d_o, states, *comm.ins)
    return res[0], res[1], res[2], res[3], res[4:]


def _causal_keep(t, qi, ki):
    row = lax.broadcasted_iota(jnp.int32, (t, t), 0) + qi * t
    col = lax.broadcasted_iota(jnp.int32, (t, t), 1) + ki * t
    return col <= row


def _split_refs(refs, counts):
    out, off = [], 0
    for cnt in counts:
        out.append(refs[off:off + cnt])
        off += cnt
    return out


def _causal_blocks(n, key_major):
    pairs = ([(ki, qi) for ki in range(n) for qi in range(ki, n)] if key_major else
             [(ki, qi) for qi in range(n) for ki in range(qi + 1)])
    return np.array([ki for ki, _ in pairs], np.int32), np.array([qi for _, qi in pairs], np.int32)


def _attn_fwd(q, k, v, comm, tile=1024):
    s = q.shape[0]
    t = _tile(s, tile)
    n = s // t
    nci, nco = len(comm.ins), len(comm.out_shape)

    ki_tab, qi_tab = _causal_blocks(n, key_major=False)
    steps = len(ki_tab)

    def kern(ki_ref, qi_ref, *refs):
        (q_ref, k_ref, v_ref), cins, (o_ref, lse_ref), couts, (m_sc, l_sc, acc_sc), csems = _split_refs(
            refs, (3, nci, 2, nco, 3, len(comm.sems)))
        pair, step = pl.program_id(0), pl.program_id(1)
        qi, ki = qi_ref[step], ki_ref[step]
        place = _place()

        @pl.when((pair == 0) & (step == 0))
        def _():
            comm.start(place, cins, couts, csems)

        @pl.when((pair == MLA_HEADS // 2 - 1) & (step == 0))
        def _():
            comm.mid(place, cins, couts, csems)

        first = lax.broadcasted_iota(jnp.int32, (t, LANES), 1) < MLA_V

        @pl.when(ki == 0)
        def _():
            m_sc[...] = jnp.full_like(m_sc, -jnp.inf)
            l_sc[...] = jnp.zeros_like(l_sc)
            acc_sc[...] = jnp.zeros_like(acc_sc)

        def update(diagonal):
            keep = _causal_keep(t, 0, 0)
            alphas, pvs = [], []
            for h in range(2):
                sc = _dg(q_ref[:, 128 * h:128 * (h + 1)], k_ref[:, 128 * h:128 * (h + 1)], _NT)
                if diagonal:
                    sc = jnp.where(keep, sc, -jnp.inf)
                m_prev = m_sc[h]
                m_new = jnp.maximum(m_prev, jnp.max(sc, axis=1, keepdims=True))
                alpha = jnp.exp2(m_prev - m_new)
                p = jnp.exp2(sc - m_new[:, 0:1])
                l_sc[h] = alpha * l_sc[h] + jnp.sum(p, axis=1, keepdims=True)
                m_sc[h] = m_new
                alphas.append(alpha)
                pvs.append(_dg(p, v_ref[...], _NN))
            acc_sc[...] = acc_sc[...] * jnp.where(first, alphas[0], alphas[1]) + jnp.where(first, pvs[0], pvs[1])

        @pl.when(ki < qi)
        def _():
            update(False)

        @pl.when(ki == qi)
        def _():
            update(True)

        @pl.when(ki == qi)
        def _():
            l = jnp.where(first, l_sc[0], l_sc[1])
            m = jnp.where(first, m_sc[0], m_sc[1])
            o_ref[...] = acc_sc[...] / l
            lse_ref[...] = m + jnp.log2(l)

        @pl.when((pair == MLA_HEADS // 2 - 1) & (step == steps - 1))
        def _():
            comm.finish(place, cins, couts, csems)

    q_idx = lambda p, st, ki_r, qi_r: (qi_r[st], p)
    k_idx = lambda p, st, ki_r, qi_r: (ki_r[st], p)
    res = pl.pallas_call(
        kern, grid_spec=pltpu.PrefetchScalarGridSpec(
            num_scalar_prefetch=2, grid=(MLA_HEADS // 2, steps),
            in_specs=[pl.BlockSpec((t, 256), q_idx), pl.BlockSpec((t, 256), k_idx), pl.BlockSpec((t, 128), k_idx)]
            + [ANY] * nci,
            out_specs=[pl.BlockSpec((t, 128), q_idx), pl.BlockSpec((t, 128), q_idx)] + [ANY] * nco,
            scratch_shapes=[pltpu.VMEM((2, t, LANES), F32), pltpu.VMEM((2, t, LANES), F32),
                            pltpu.VMEM((t, LANES), F32)] + comm.sems),
        out_shape=[jax.ShapeDtypeStruct((s, 512), F32), jax.ShapeDtypeStruct((s, 512), F32)] + comm.out_shape,
        compiler_params=pltpu.CompilerParams(dimension_semantics=("arbitrary", "arbitrary"),
                                             vmem_limit_bytes=VMEM_LIMIT),
        name="mla_attn_fwd")(ki_tab, qi_tab, q, k, v, *comm.ins)
    return res[0], res[1], res[2:]


def _attn_bwd(q, k, v, o, lse, d_o, comm, tile=512):
    s = q.shape[0]
    t = _tile(s, tile)
    n = s // t
    nci, nco = len(comm.ins), len(comm.out_shape)

    ki_tab, qi_tab = _causal_blocks(n, key_major=True)
    steps = len(ki_tab)

    def kern(ki_ref, qi_ref, *refs):
        (q_ref, k_ref, v_ref, o_ref, lse_ref, do_ref), cins, (dq_ref, dk_ref, dv_ref), couts, (dk_sc, dv_sc), csems = \
            _split_refs(refs, (6, nci, 3, nco, 2, len(comm.sems)))
        pair, step = pl.program_id(0), pl.program_id(1)
        ki, qi = ki_ref[step], qi_ref[step]
        place = _place()

        @pl.when((pair == 0) & (step == 0))
        def _():
            comm.start(place, cins, couts, csems)

        @pl.when((pair == MLA_HEADS // 2 - 1) & (step == 0))
        def _():
            comm.mid(place, cins, couts, csems)

        @pl.when((ki == 0) & (qi == 0))
        def _():
            dq_ref[...] = jnp.zeros_like(dq_ref)

        @pl.when(qi == ki)
        def _():
            dk_sc[...] = jnp.zeros_like(dk_sc)
            dv_sc[...] = jnp.zeros_like(dv_sc)

        def update(diagonal):
            keep = _causal_keep(t, 0, 0)
            d_o = do_ref[...]
            prod = d_o * o_ref[...]
            rows = pl.ds(pl.multiple_of(qi * t, t), t)
            for h in range(2):
                hs = slice(128 * h, 128 * (h + 1))
                mk = _lane_mask(MLA_V * h, MLA_V * (h + 1))
                qh, kh = q_ref[:, hs], k_ref[:, hs]
                sc = _dg(qh, kh, _NT)
                if diagonal:
                    sc = jnp.where(keep, sc, -jnp.inf)
                p = jnp.exp2(sc - lse_ref[:, MLA_V * h:MLA_V * h + 1])
                doh = d_o * mk
                dp = _dg(doh * LN2, v_ref[...], _NT)
                delta = jnp.sum(prod * mk, axis=1, keepdims=True) * LN2
                ds = p * (dp - delta)
                dv_sc[...] += _dg(p, doh, _TN)
                dk_sc[:, hs] += _dg(ds, qh, _TN)
                dq_ref[rows, hs] += _dg(ds, kh, _NN)

        @pl.when(qi > ki)
        def _():
            update(False)

        @pl.when(qi == ki)
        def _():
            update(True)

        @pl.when(qi == n - 1)
        def _():
            dk_ref[...] = dk_sc[...]
            dv_ref[...] = dv_sc[...].astype(dv_ref.dtype)

        @pl.when((pair == MLA_HEADS // 2 - 1) & (step == steps - 1))
        def _():
            comm.finish(place, cins, couts, csems)

    q_idx = lambda p, st, ki_r, qi_r: (qi_r[st], p)
    k_idx = lambda p, st, ki_r, qi_r: (ki_r[st], p)
    res = pl.pallas_call(
        kern, grid_spec=pltpu.PrefetchScalarGridSpec(
            num_scalar_prefetch=2, grid=(MLA_HEADS // 2, steps),
            in_specs=[pl.BlockSpec((t, 256), q_idx), pl.BlockSpec((t, 256), k_idx), pl.BlockSpec((t, 128), k_idx),
                      pl.BlockSpec((t, 128), q_idx), pl.BlockSpec((t, 128), q_idx), pl.BlockSpec((t, 128), q_idx)]
            + [ANY] * nci,
            out_specs=[pl.BlockSpec((s, 256), lambda p, st, ki_r, qi_r: (0, p)), pl.BlockSpec((t, 256), k_idx),
                       pl.BlockSpec((t, 128), k_idx)] + [ANY] * nco,
            scratch_shapes=[pltpu.VMEM((t, 256), F32), pltpu.VMEM((t, 128), F32)] + comm.sems),
        out_shape=[jax.ShapeDtypeStruct((s, 1024), F32), jax.ShapeDtypeStruct((s, 1024), F32),
                   jax.ShapeDtypeStruct((s, 512), BF16)] + comm.out_shape,
        compiler_params=pltpu.CompilerParams(dimension_semantics=("arbitrary", "arbitrary"),
                                             vmem_limit_bytes=VMEM_LIMIT),
        name="mla_attn_bwd")(ki_tab, qi_tab, q, k, v, o, lse, d_o, *comm.ins)
    return res[0], res[1], res[2], res[3:]


def _gate_fn(alr, w2, b):
    return _log_sigmoid(_dot_nn(alr, w2) + b) * (1.0 / GLA_GATE_NORM)


def _qk_head(qh, kh, kpe, c, sa, sb, qn, kn):
    kfull = kh + kpe * _lane_mask(MLA_NOPE, MLA_QK)
    q_r = _rope(_rms(qh, qn, MLA_QK), c, sa, sb) * (MLA_QK ** -0.5 * LOG2E)
    k_r = _rope(_rms(kfull, kn, MLA_QK), c, sa, sb)
    return q_r, k_r


def _mix_head(o, og, gn):
    return _rms(o, gn) * _silu(og)


def _xa_head(xq, xk, xv, qn, kn):
    sc = _dot_nt(_rms(xq, qn), _rms(xk, kn)) * (XA_DIM ** -0.5)
    e = jnp.exp(sc - lax.stop_gradient(jnp.max(sc, axis=1, keepdims=True)))
    p = e / jnp.sum(e, axis=1, keepdims=True)
    return _dot_nn(p, xv)


def _heads(x, n):
    return [x[:, 128 * h:128 * (h + 1)] for h in range(n)]


def _cat(xs):
    return jnp.concatenate(xs, axis=1)


def _norm_fwd(x, w, name):
    return _rows_call(lambda r, c: ([_rms(r[0], c[0])], []), [_row(x)], [w], [(x.shape[1], BF16)], name=name)[0]


def _norm_fwd_epilogue(w):
    return _Epilogue(lambda h, rows, consts: ([h, _rms(h, consts[0])], []), [], [w], [(D_MODEL, F32), (D_MODEL, BF16)], [])


def _norm_bwd_epilogue(x, w, add):
    def fn(d_out, rows, consts):
        _, vjp = jax.vjp(_rms, rows[0], consts[0])
        dx, dw = vjp(d_out)
        return [dx + rows[1]], [dw]

    return _Epilogue(fn, [_row(x), _row(add)], [w], [(D_MODEL, F32)], [w.shape])


def _norm_fwd_comm(x, w, comm, name):
    s, d = x.shape
    t = _tile(s, 512)
    n = s // t
    nci, nco = len(comm.ins), len(comm.out_shape)

    def kern(*refs):
        (x_ref, w_ref), cins, (o_ref,), couts, csems = _split_refs(refs, (2, nci, 1, nco, len(comm.sems)))
        place = _place()

        @pl.when(pl.program_id(0) == 0)
        def _():
            comm.start(place, cins, couts, csems)

        o_ref[...] = _rms(x_ref[...], w_ref[...]).astype(o_ref.dtype)

        @pl.when(pl.program_id(0) == n - 1)
        def _():
            comm.mid(place, cins, couts, csems)
            comm.finish(place, cins, couts, csems)

    tile = pl.BlockSpec((t, d), lambda i: (i, 0))
    res = pl.pallas_call(
        kern, grid=(n,), in_specs=[tile, pl.BlockSpec(w.shape, lambda i: (0, 0))] + [ANY] * nci,
        out_specs=[tile] + [ANY] * nco, out_shape=[jax.ShapeDtypeStruct((s, d), BF16)] + comm.out_shape,
        scratch_shapes=comm.sems,
        compiler_params=pltpu.CompilerParams(dimension_semantics=("arbitrary",), vmem_limit_bytes=VMEM_LIMIT),
        name=name)(x, w, *comm.ins)
    return res[0], res[1:]


def _norm_bwd(x, w, d_out, add, name):
    def body(r, c):
        _, vjp = jax.vjp(_rms, r[0], c[0])
        dx, dw = vjp(r[1])
        return [dx + r[2]], [dw]

    return _rows_call(body, [_row(x), _row(d_out), _row(add)], [w], [(x.shape[1], F32)], [w.shape], name=name)


CONV_HALO = BF16_ROWS


def _conv_specs(s, f, t):
    n8 = t // CONV_HALO
    cur = pl.BlockSpec((None, t, f), lambda j, i: (j, i, 0))
    prev = pl.BlockSpec((None, CONV_HALO, f), lambda j, i: (j, jnp.maximum(i * n8 - 1, 0), 0))
    nxt = pl.BlockSpec((None, CONV_HALO, f), lambda j, i: (j, jnp.minimum((i + 1) * n8, s // CONV_HALO - 1), 0))
    cw = pl.BlockSpec((None, 3, f), lambda j, i: (j, 0, 0))
    cb = pl.BlockSpec((None, 1, f), lambda j, i: (j, 0, 0))
    return cur, prev, nxt, cw, cb


def _conv_taps(g, prev, first):
    ext = jnp.concatenate([jnp.where(first, 0.0, prev.astype(F32)), g], axis=0)
    return pltpu.roll(ext, 1, 0)[CONV_HALO:], pltpu.roll(ext, 2, 0)[CONV_HALO:]


def _conv_fwd(gg, uu, cw, cb):
    _, s, f = gg.shape
    t = _tile(s, 512)

    def kern(g_ref, gp_ref, u_ref, cw_ref, cb_ref, o_ref):
        g = g_ref[...].astype(F32)
        g1, g2 = _conv_taps(g, gp_ref[...], pl.program_id(1) == 0)
        w = cw_ref[...]
        gc = cb_ref[...] + w[0:1] * g2 + w[1:2] * g1 + w[2:3] * g
        o_ref[...] = (_silu(gc) * u_ref[...].astype(F32)).astype(o_ref.dtype)

    cur, prev, _, cws, cbs = _conv_specs(s, f, t)
    return pl.pallas_call(
        kern, grid=(4, s // t), in_specs=[cur, prev, cur, cws, cbs], out_specs=cur,
        out_shape=jax.ShapeDtypeStruct(gg.shape, BF16),
        compiler_params=pltpu.CompilerParams(dimension_semantics=("parallel", "parallel"), vmem_limit_bytes=VMEM_LIMIT),
        name="ffn_conv_fwd")(gg, gg, uu, cw, cb)


def _conv_bwd(gg, uu, dact, cw, cb):
    _, s, f = gg.shape
    t = _tile(s, 512)
    nt = s // t

    def kern(g_ref, gp_ref, gn_ref, u_ref, un_ref, da_ref, dan_ref, cw_ref, cb_ref, du_ref, dg_ref, dcw_ref, dcb_ref):
        i = pl.program_id(1)
        cat = lambda a_ref, b_ref: jnp.concatenate([a_ref[...].astype(F32), b_ref[...].astype(F32)], axis=0)
        g, u, da = cat(g_ref, gn_ref), cat(u_ref, un_ref), cat(da_ref, dan_ref)
        g1, g2 = _conv_taps(g, gp_ref[...], i == 0)
        w = cw_ref[...]
        gc = cb_ref[...] + w[0:1] * g2 + w[1:2] * g1 + w[2:3] * g
        sg = jax.nn.sigmoid(gc)
        du_ref[...] = (da[:t] * (gc[:t] * sg[:t])).astype(du_ref.dtype)
        row = lax.broadcasted_iota(jnp.int32, (t + CONV_HALO, 1), 0)
        dgc = jnp.where((row < t) | (i < nt - 1), da * u * (sg * (1.0 + gc * (1.0 - sg))), 0.0)
        up1 = pltpu.roll(dgc, t + CONV_HALO - 1, 0)[:t]
        up2 = pltpu.roll(dgc, t + CONV_HALO - 2, 0)[:t]
        dgc = dgc[:t]
        dg_ref[...] = (w[2:3] * dgc + w[1:2] * up1 + w[0:1] * up2).astype(dg_ref.dtype)

        @pl.when(i == 0)
        def _():
            dcw_ref[...] = jnp.zeros_like(dcw_ref)
            dcb_ref[...] = jnp.zeros_like(dcb_ref)

        dcw_ref[0:1, :] += jnp.sum(dgc * g2[:t], axis=0, keepdims=True)
        dcw_ref[1:2, :] += jnp.sum(dgc * g1[:t], axis=0, keepdims=True)
        dcw_ref[2:3, :] += jnp.sum(dgc * g[:t], axis=0, keepdims=True)
        dcb_ref[...] += jnp.sum(dgc, axis=0, keepdims=True)

    cur, prev, nxt, cws, cbs = _conv_specs(s, f, t)
    return pl.pallas_call(
        kern, grid=(4, nt), in_specs=[cur, prev, nxt, cur, nxt, cur, nxt, cws, cbs], out_specs=[cur, cur, cws, cbs],
        out_shape=[jax.ShapeDtypeStruct(gg.shape, BF16), jax.ShapeDtypeStruct(gg.shape, BF16),
                   jax.ShapeDtypeStruct(cw.shape, F32), jax.ShapeDtypeStruct(cb.shape, F32)],
        compiler_params=pltpu.CompilerParams(dimension_semantics=("parallel", "arbitrary"), vmem_limit_bytes=VMEM_LIMIT),
        name="ffn_conv_bwd")(gg, gg, gg, uu, uu, dact, dact, cw, cb)


def _rope_tables(pos):
    half = MLA_ROPE // 2
    inv = ROPE_THETA ** (-jnp.arange(half, dtype=F32) / half)
    ang = pos.astype(F32)[:, None] * inv
    cos, sin = jnp.cos(ang), jnp.sin(ang)
    s = pos.shape[0]
    z = lambda w: jnp.zeros((s, w), F32)
    c = jnp.concatenate([jnp.ones((s, MLA_NOPE), F32), cos, cos, jnp.ones((s, LANES - MLA_QK), F32)], axis=1)
    sa = jnp.concatenate([z(MLA_NOPE), -sin, z(half), z(LANES - MLA_QK)], axis=1)
    sb = jnp.concatenate([z(MLA_NOPE), z(half), sin, z(LANES - MLA_QK)], axis=1)
    return c, sa, sb


def _local_step(x, mem, pos, target, rep, early_shards, late_shards):
    g = {}
    c, sa, sb = _rope_tables(pos)

    xn, gathered = _norm_fwd_comm(x, rep["norm_mix"], _gather_plan(early_shards), "norm_mix_fwd_gather")
    w = _early_layout(dict(zip(EARLY, gathered, strict=True)), rep)

    def proj_fn(r, rows, k):
        la_ = _gate_fn(r[:, P_ALR:P_ALR + 128], k[0], k[1])
        return [r, la_, _rms(r[:, P_CQ:P_CQ + MLA_Q_RANK], k[2]), _rms(r[:, P_CKV:P_CKV + MLA_KV_RANK], k[3])], []

    proj, la, q_lat, kv_lat = _matmul(
        xn, w["in"], "nt", F32, "proj_fwd", epilogue=_Epilogue(
            proj_fn, [], [w["w2"], w["gate_b"], w["q_a_norm"], w["kv_a_norm"]],
            [(P_WIDTH, F32), (256, F32), (MLA_Q_RANK, BF16), (MLA_KV_RANK, BF16)], []))
    alr = _row(proj, 128, P_ALR // 128)
    kpe = _row(proj, 128, P_KPE // 128)
    og = _row(proj, 512, P_OG // 512)
    cq = _row(proj, 256, P_CQ // 256)
    ckv = _row(proj, 128, P_CKV // 128)

    o_gla, states = _gla_fwd(proj, la)

    q_up = _matmul(q_lat, w["uq"], "nn", F32, "mla_q_fwd")
    k_up = _matmul(kv_lat, w["k"], "nn", F32, "mla_k_fwd")
    v_mla = _matmul(kv_lat, w["v"], "nn", BF16, "mla_v_fwd")

    def qk_body(r, k):
        qs, ks = [], []
        for qh, kh in zip(_heads(r[0], MLA_HEADS), _heads(r[1], MLA_HEADS)):
            a, b = _qk_head(qh, kh, r[2], r[3], r[4], r[5], k[0], k[1])
            qs.append(a)
            ks.append(b)
        return [_cat(qs), _cat(ks)], []

    tabs = [_row(c), _row(sa), _row(sb)]
    q_r, k_r = _rows_call(qk_body, [_row(q_up), _row(k_up), kpe] + tabs, [w["q_norm"], w["k_norm"]],
                          [(1024, BF16), (1024, BF16)], name="mla_qk_fwd")
    o_mla, lse, gathered = _attn_fwd(q_r, k_r, v_mla, _gather_plan(late_shards))
    w.update(_late_layout(dict(zip(LATE, gathered, strict=True))))

    def mix_body(r, k):
        ys = [_mix_head(o, g_, k[0]) for o, g_ in zip(_heads(r[0], GLA_HEADS), _heads(r[1], GLA_HEADS))]
        return [_cat(ys + [r[2]])], []

    cat = _rows_call(mix_body, [_row(o_gla), og, _row(o_mla)], [w["gla_out_norm"]], [(1024, BF16)],
                     name="mix_fwd")[0]
    h1, hn = _matmul(cat, w["out"], "nn", F32, "out_fwd_norm", residual=x, epilogue=_norm_fwd_epilogue(w["norm_xa"]))
    mn = _norm_fwd(mem, w["norm_mem"], "norm_mem_fwd")
    xkv = _matmul(mn, w["xkv"], "nn", F32, "xa_kv_fwd")

    def xa_fn(r, rows, k):
        ks, vs = _heads(k[0], 2 * XA_HEADS)[:XA_HEADS], _heads(k[0], 2 * XA_HEADS)[XA_HEADS:]
        return [r, _cat([_xa_head(a, b, v_, k[1], k[2]) for a, b, v_ in zip(_heads(r, XA_HEADS), ks, vs)])], []

    xq, xo = _matmul(hn, w["xq"], "nn", F32, "xa_q_fwd_attn", epilogue=_Epilogue(
        xa_fn, [], [xkv, w["xa_q_norm"], w["xa_k_norm"]], [(512, F32), (512, BF16)], []))
    h2, fn = _matmul(xo, w["xo"], "nn", F32, "xa_o_fwd_norm", residual=h1, epilogue=_norm_fwd_epilogue(w["norm_ffn"]))
    gg = _matmul(fn, w["wg"], "nt", BF16, "ffn_gate_fwd", b_lead="p")
    uu = _matmul(fn, w["wu"], "nt", BF16, "ffn_up_fwd", b_lead="p")
    act = _conv_fwd(gg, uu, w["cw"], w["cb"])
    def loss_fn(y, rows, consts):
        err = y - rows[0]
        part = 0.5 * jnp.sum(jnp.sum(err * err, axis=1, keepdims=True) * (1.0 / D_MODEL), axis=0, keepdims=True)
        return [err * (1.0 / D_MODEL)], [jnp.broadcast_to(part, (1, LANES))]

    dy, loss = _matmul(act, w["wd"], "nn", F32, "ffn_down_fwd_loss", residual=h2, a_lead="k", b_lead="k",
                       epilogue=_Epilogue(loss_fn, [_row(target)], [], [(D_MODEL, F32)], [(1, LANES)]))

    g["ffn_w_down"] = _matmul(act, dy, "tn", BF16, "ffn_down_dw", a_lead="p")
    dact = _matmul(dy, w["wd"], "nt", BF16, "ffn_down_dx", b_lead="p")
    duu, dgg, g["ffn_conv_w"], g["ffn_conv_b"] = _conv_bwd(gg, uu, dact, w["cw"], w["cb"])
    g["ffn_w_gate"] = _matmul(dgg, fn, "tn", BF16, "ffn_gate_dw", a_lead="p")
    g["ffn_w_up"] = _matmul(duu, fn, "tn", BF16, "ffn_up_dw", a_lead="p")
    dh2, g["norm_ffn"] = _matmul(dgg, w["wg"], "nn", F32, "ffn_dx_norm_bwd", a_lead="k", b_lead="k", more=(duu, w["wu"]),
                                 epilogue=_norm_bwd_epilogue(h2, w["norm_ffn"], dy))

    g["xa_w_o"] = _matmul(xo, dh2, "tn", BF16, "xa_o_dw")
    def xa_bwd(dxo_, rows, k):
        kvh = _heads(k[0], 2 * XA_HEADS)
        dq_, dk_, dv_ = [], [], []
        dqn, dkn = 0.0, 0.0
        for h, (a, d_) in enumerate(zip(_heads(rows[0], XA_HEADS), _heads(dxo_, XA_HEADS))):
            _, vjp = jax.vjp(_xa_head, a, kvh[h], kvh[XA_HEADS + h], k[1], k[2])
            ga, gk, gv, gqn, gkn = vjp(d_)
            dq_.append(ga)
            dk_.append(gk)
            dv_.append(gv)
            dqn, dkn = dqn + gqn, dkn + gkn
        return [_cat(dq_)], [_cat(dk_ + dv_), dqn, dkn]

    dxq, dxkv, g["xa_q_norm"], g["xa_k_norm"] = _matmul(dh2, w["xo"], "nt", F32, "xa_o_dx_attn_bwd", epilogue=_Epilogue(
        xa_bwd, [_row(xq)], [xkv, w["xa_q_norm"], w["xa_k_norm"]], [(512, BF16)], [xkv.shape, (1, 128), (1, 128)]))
    g["xa_w_q"] = _matmul(hn, dxq, "tn", BF16, "xa_q_dw")
    dh1, g["norm_xa"] = _matmul(dxq, w["xq"], "nt", F32, "xa_q_dx_norm_bwd",
                                epilogue=_norm_bwd_epilogue(h1, w["norm_xa"], dh2))
    g["xa_w_kv"] = _matmul(mn, dxkv, "tn", BF16, "xa_kv_dw")
    dmn = _matmul(dxkv, w["xkv"], "nt", F32, "xa_kv_dx")
    _, g["norm_mem"] = _norm_bwd(mem, w["norm_mem"], dmn, dmn, "norm_mem_bwd")

    g["w_out"] = _matmul(cat, dh1, "tn", BF16, "out_dw")
    def mix_bwd(dcat_, rows, k):
        do_, dog_ = [], []
        dgn = 0.0
        for o, g_, d_ in zip(_heads(rows[0], GLA_HEADS), _heads(rows[1], GLA_HEADS), _heads(dcat_, GLA_HEADS)):
            _, vjp = jax.vjp(_mix_head, o, g_, k[0])
            a, b, gn_ = vjp(d_)
            do_.append(a)
            dog_.append(b)
            dgn = dgn + gn_
        return [_cat(do_), _cat(dog_), dcat_[:, 512:]], [dgn]

    do_gla, d_og, do_mla, g["gla_out_norm"] = _matmul(dh1, w["out"], "nt", F32, "out_dx_mix_bwd", epilogue=_Epilogue(
        mix_bwd, [_row(o_gla), og], [w["gla_out_norm"]], [(512, F32), (512, BF16), (512, F32)], [(1, 128)]))

    late_parts = _late_grad_shards(g)
    dq_r, dk_r, dv_mla, lands_late = _attn_bwd(q_r, k_r, v_mla, o_mla, lse, do_mla,
                                               _scatter_plan([late_parts[n] for n in LATE]))
    lands_late = dict(zip(LATE, lands_late, strict=True))

    def qk_bwd(r, k):
        dqs, dks = [], []
        dkpe, dqn, dkn = 0.0, 0.0, 0.0
        for qh, kh, dqh, dkh in zip(_heads(r[0], MLA_HEADS), _heads(r[1], MLA_HEADS), _heads(r[6], MLA_HEADS),
                                    _heads(r[7], MLA_HEADS)):
            _, vjp = jax.vjp(lambda a, b, e, f, h_: _qk_head(a, b, e, r[3], r[4], r[5], f, h_), qh, kh, r[2], k[0], k[1])
            ga, gb, ge, gf, gh = vjp((dqh, dkh))
            dqs.append(ga)
            dks.append(gb)
            dkpe, dqn, dkn = dkpe + ge, dqn + gf, dkn + gh
        return [_cat(dqs), _cat(dks), dkpe], [dqn, dkn]

    dq_up, dk_up, d_kpe, g["q_norm"], g["k_norm"] = _rows_call(
        qk_bwd, [_row(q_up), _row(k_up), kpe] + tabs + [_row(dq_r), _row(dk_r)], [w["q_norm"], w["k_norm"]],
        [(1024, BF16), (1024, BF16), (128, BF16)], [(1, 128), (1, 128)], name="mla_qk_bwd")
    g["uq"] = _matmul(q_lat, dq_up, "tn", BF16, "mla_q_dw")
    dq_lat = _matmul(dq_up, w["uq"], "nt", F32, "mla_q_dx")
    g["k"] = _matmul(kv_lat, dk_up, "tn", BF16, "mla_k_dw")
    g["v"] = _matmul(kv_lat, dv_mla, "tn", BF16, "mla_v_dw")
    dkv_lat = _matmul(dk_up, w["k"], "nt", F32, "mla_k_dx")
    dkv_lat = _matmul(dv_mla, w["v"], "nt", F32, "mla_v_dx", residual=dkv_lat)

    dgq, dgk, dla, dgv, _ = _gla_bwd(proj, la, states, do_gla, _Comm([], [], [], lambda *args: None, lambda *args: None))

    def dproj_body(r, k):
        alr_, cq_, ckv_, dla_, dq_lat_, dkv_lat_, dgq_, dgk_, dgv_, d_og_, d_kpe_ = r
        _, gate_vjp = jax.vjp(_gate_fn, alr_, k[0], k[1])
        d_alr, gw2, gb = gate_vjp(dla_)
        _, q_vjp = jax.vjp(_rms, cq_, k[2])
        _, kv_vjp = jax.vjp(_rms, ckv_, k[3])
        d_cq, gqa = q_vjp(dq_lat_)
        d_ckv, gkva = kv_vjp(dkv_lat_)
        pieces = [dgq_, dgk_, dgv_, d_og_, d_cq, d_ckv, d_kpe_, d_alr]
        return [_cat([x_.astype(BF16) for x_ in pieces])], [gw2, gb, gqa, gkva]

    dproj, g["w2"], g["gla_gate_b"], g["mla_q_a_norm"], g["mla_kv_a_norm"] = _rows_call(
        dproj_body, [alr, cq, ckv, _row(dla), _row(dq_lat), _row(dkv_lat), _row(dgq), _row(dgk), _row(dgv), _row(d_og),
                     _row(d_kpe)], [w["w2"], w["gate_b"], w["q_a_norm"], w["kv_a_norm"]], [(P_WIDTH, BF16)],
        [(128, 256), (1, 256), (1, 256), (1, 128)], name="proj_cotangent")
    g["in"] = _matmul(dproj, xn, "tn", BF16, "proj_dw")
    dx, g["norm_mix"] = _matmul(dproj, w["in"], "nn", F32, "proj_dx_norm_bwd",
                                epilogue=_norm_bwd_epilogue(x, w["norm_mix"], dh1))
    return loss[0, 0], dx, g, lands_late


def _join_shards(pieces, axis):
    if axis == 0:
        return pieces.reshape(-1, pieces.shape[2])
    return jnp.transpose(pieces, (1, 0, 2)).reshape(pieces.shape[1], -1)


def _split_shards(full, axis):
    r, c = full.shape
    if axis == 0:
        return full.reshape(4, r // 4, c)
    return jnp.transpose(full.reshape(r, 4, c // 4), (1, 0, 2))


def _early_layout(gath, rep):
    w_in = gath["w_in"].reshape(N_WIDTH, D_MODEL)
    z = lambda n: jnp.zeros((n, D_MODEL), w_in.dtype)
    seg = lambda lo, n: w_in[lo:lo + n]
    ukv = _join_shards(gath["mla_w_ukv"], 1).reshape(MLA_KV_RANK, MLA_HEADS, MLA_NOPE + MLA_V)
    w = {
        "in": jnp.concatenate([seg(N_GQ, 256), seg(N_GK, 256), seg(N_GV, 512), seg(N_OG, 512), seg(N_CQ, 256),
                               seg(N_CKV, 128), z(64), seg(N_KPE, 32), z(32), seg(N_ALR, 16), z(112)], axis=0),
        "uq": jnp.pad(_join_shards(gath["mla_w_uq"], 1).reshape(MLA_Q_RANK, MLA_HEADS, MLA_QK),
                      ((0, 0), (0, 0), (0, LANES - MLA_QK))).reshape(MLA_Q_RANK, MLA_HEADS * LANES),
        "k": jnp.pad(ukv[:, :, :MLA_NOPE], ((0, 0), (0, 0), (0, LANES - MLA_NOPE))).reshape(MLA_KV_RANK, -1),
        "v": ukv[:, :, MLA_NOPE:].reshape(MLA_KV_RANK, MLA_HEADS * MLA_V),
        "w2": jnp.pad(_join_shards(gath["gla_gate_w2"], 1), ((0, LANES - GLA_RANK), (0, 0))),
        "cb": rep["ffn_conv_b"].reshape(4, 1, D_FF // 4),
        "q_norm": jnp.pad(rep["mla_q_norm"], ((0, 0), (0, LANES - MLA_QK))),
        "k_norm": jnp.pad(rep["mla_k_norm"], ((0, 0), (0, LANES - MLA_QK))),
        "q_a_norm": rep["mla_q_a_norm"], "kv_a_norm": rep["mla_kv_a_norm"], "gate_b": rep["gla_gate_b"],
    }
    for n in ("norm_mix", "gla_out_norm", "norm_xa", "norm_mem", "xa_q_norm", "xa_k_norm", "norm_ffn"):
        w[n] = rep[n]
    return w


def _late_layout(gath):
    return {"out": _join_shards(gath["w_out"], 0), "xq": _join_shards(gath["xa_w_q"], 0),
            "xkv": _join_shards(gath["xa_w_kv"], 0), "xo": _join_shards(gath["xa_w_o"], 1),
            "wg": gath["ffn_w_gate"], "wu": gath["ffn_w_up"], "wd": gath["ffn_w_down"], "cw": gath["ffn_conv_w"]}


def _late_grad_shards(g):
    sh = {"w_out": _split_shards(g["w_out"], 0), "xa_w_q": _split_shards(g["xa_w_q"], 0),
          "xa_w_kv": _split_shards(g["xa_w_kv"], 0), "xa_w_o": _split_shards(g["xa_w_o"], 1),
          "ffn_w_gate": g["ffn_w_gate"], "ffn_w_up": g["ffn_w_up"], "ffn_conv_w": g["ffn_conv_w"],
          "ffn_w_down": g["ffn_w_down"]}
    return {n: v.astype(BF16) for n, v in sh.items()}


def _early_grad_shards(g):
    gi = g["in"]
    seg = lambda lo, n: gi[lo:lo + n]
    w_in = jnp.concatenate([seg(P_GQ, 256), seg(P_GK, 256), seg(P_GV, 512), seg(P_ALR, 16), seg(P_OG, 512),
                            seg(P_CQ, 256), seg(P_CKV, 128), seg(P_KPE + 64, 32)], axis=0)
    uq = g["uq"].reshape(MLA_Q_RANK, MLA_HEADS, LANES)[:, :, :MLA_QK].reshape(MLA_Q_RANK, -1)
    ukv = jnp.concatenate([g["k"].reshape(MLA_KV_RANK, MLA_HEADS, LANES)[:, :, :MLA_NOPE],
                           g["v"].reshape(MLA_KV_RANK, MLA_HEADS, MLA_V)], axis=2).reshape(MLA_KV_RANK, -1)
    sh = {"w_in": w_in.reshape(4, N_WIDTH // 4, D_MODEL), "gla_gate_w2": _split_shards(g["w2"][:GLA_RANK], 1),
          "mla_w_uq": _split_shards(uq, 1), "mla_w_ukv": _split_shards(ukv, 1)}
    sh = {n: v.astype(BF16) for n, v in sh.items()}
    rep = {n: g[n] for n in REPLICATED if n in g}
    rep["mla_q_norm"] = g["q_norm"][:, :MLA_QK]
    rep["mla_k_norm"] = g["k_norm"][:, :MLA_QK]
    rep["ffn_conv_b"] = g["ffn_conv_b"].reshape(1, D_FF)
    return sh, rep


SMALL_SHAPE = (8, 1024)


def _pack_small(vectors):
    flat = jnp.concatenate(vectors, axis=1)
    return jnp.pad(flat, ((0, 0), (0, SMALL_SHAPE[0] * SMALL_SHAPE[1] - flat.shape[1]))).reshape(SMALL_SHAPE)


def _unpack_small(buf, widths):
    flat = buf.reshape(1, -1)
    out, off = [], 0
    for wd in widths:
        out.append(flat[:, off:off + wd])
        off += wd
    return out


ANY = pl.BlockSpec(memory_space=pl.ANY)


def _place():
    x, y, c = lax.axis_index("x"), lax.axis_index("y"), lax.axis_index("c")
    chips = [(1 - x, y), (x, 1 - y), (1 - x, 1 - y)]
    return x, y, c, chips


class _Comm:
    def __init__(self, ins, out_shape, sems, start, finish, mid=None):
        self.ins, self.out_shape, self.sems = list(ins), list(out_shape), list(sems)
        self.start, self.finish, self.mid = start, finish, mid or (lambda *args: None)


def _run_comm(plan, name):
    ni, no = len(plan.ins), len(plan.out_shape)

    def body(*refs):
        ins, outs, sems = refs[:ni], refs[ni:ni + no], refs[ni + no:]
        place = _place()
        plan.start(place, ins, outs, sems)
        plan.mid(place, ins, outs, sems)
        plan.finish(place, ins, outs, sems)

    return pl.pallas_call(body, in_specs=[ANY] * ni, out_specs=[ANY] * no, out_shape=plan.out_shape,
                          scratch_shapes=plan.sems, name=name)(*plan.ins)


def _gather_plan(shards):
    n = len(shards)
    split = [s.shape[0] % (2 * BF16_ROWS) == 0 for s in shards]

    def rows(ref, t, c):
        if not split[t]:
            return ref
        half = shards[t].shape[0] // 2
        return ref.at[pl.ds(pl.multiple_of(c * half, BF16_ROWS), half)]

    def remote(src, dst, ss, rs, to):
        return pltpu.make_async_remote_copy(src_ref=src, dst_ref=dst, send_sem=ss, recv_sem=rs, device_id=to,
                                            device_id_type=MESH)

    def first_wave(place, ins, outs, sems):
        x, y, c, chips = place
        ici_s, ici_r, _, _, local = sems
        me = 2 * x + y
        own = [pltpu.make_async_copy(ins[t], outs[t].at[me], local.at[t]) for t in range(n)]
        push = [remote(rows(ins[t], t, c), rows(outs[t].at[me], t, c), ici_s.at[3 * t + j], ici_r.at[3 * t + j], (px, py, c))
                for t in range(n) for j, (px, py) in enumerate(chips)]
        return own, push

    def second_wave(place, ins, outs, sems, last):
        x, y, c, chips = place
        ici_s, ici_r, d2d_s, d2d_r, local = sems
        sib = (x, y, 1 - c)
        out = []
        for t in range(n):
            for j, (px, py) in enumerate(chips):
                block = outs[t].at[2 * px + py]
                got = rows(block, t, c)
                if split[t]:
                    hand = remote(got, got, d2d_s.at[3 * t + j], d2d_r.at[3 * t + j], sib)
                    theirs = rows(block, t, 1 - c)
                    other = (remote(theirs, theirs, local.at[0], d2d_r.at[3 * t + j], sib) if last else
                             remote(got, got, local.at[0], ici_r.at[3 * t + j], sib))
                    out.append((other, hand))
                elif last:
                    out.append((remote(got, got, local.at[0], ici_r.at[3 * t + j], sib), None))
        return out

    def start(place, ins, outs, sems):
        own, push = first_wave(place, ins, outs, sems)
        for cp in own + push:
            cp.start()

    def mid(place, ins, outs, sems):
        for arrival, hand in second_wave(place, ins, outs, sems, False):
            arrival.wait_recv()
            hand.start()

    def finish(place, ins, outs, sems):
        own, push = first_wave(place, ins, outs, sems)
        for arrival, hand in second_wave(place, ins, outs, sems, True):
            arrival.wait_recv()
            if hand is not None:
                hand.wait_send()
        for cp in push:
            cp.wait_send()
        for cp in own:
            cp.wait()

    dma = pltpu.SemaphoreType.DMA
    return _Comm(shards, [jax.ShapeDtypeStruct((4,) + s.shape, s.dtype) for s in shards],
                 [dma((3 * n,)), dma((3 * n,)), dma((3 * n,)), dma((3 * n,)), dma((n,))], start, finish, mid)


def _scatter_plan(parts, small=None):
    n = len(parts)
    ns = 0 if small is None else 1

    def unpack(place, ins, outs, sems):
        x, y, c, chips = place
        return x, y, c, chips, 2 * x + y, 4 * x + 2 * y + c, (x, y, 1 - c)

    def remote(src, dst, ss, rs, to):
        return pltpu.make_async_remote_copy(src_ref=src, dst_ref=dst, send_sem=ss, recv_sem=rs, device_id=to,
                                            device_id_type=MESH)

    def first_wave(place, ins, outs, sems):
        x, y, c, chips, me, dev, sib = unpack(place, ins, outs, sems)
        ici_s, ici_r, d2d_s, d2d_r, sm_s, sm_r, local = sems
        own, push = [], []
        if ns:
            own.append(pltpu.make_async_copy(ins[n], outs[n].at[dev], local.at[n]))
            for k in range(1, 8):
                px = (1 - x) if (k >> 2) & 1 else x
                py = (1 - y) if (k >> 1) & 1 else y
                pc = (1 - c) if k & 1 else c
                push.append(remote(ins[n], outs[n].at[dev], sm_s.at[k - 1], sm_r.at[k - 1], (px, py, pc)))
        for t in range(n):
            own.append(pltpu.make_async_copy(ins[t].at[me], outs[t].at[dev], local.at[t]))
            push.append(remote(ins[t].at[me], outs[t].at[dev], d2d_s.at[4 * t], d2d_r.at[4 * t], sib))
            for j, (px, py) in enumerate(chips):
                push.append(remote(ins[t].at[2 * px + py], outs[t].at[dev], ici_s.at[3 * t + j], ici_r.at[3 * t + j],
                                   (px, py, c)))
        return own, push

    def start(place, ins, outs, sems):
        own, push = first_wave(place, ins, outs, sems)
        for cp in own + push:
            cp.start()

    def landed(dst, rs, sems, sib):
        remote(dst, dst, sems[-1].at[0], rs, sib).wait_recv()

    def forwards(place, ins, outs, sems):
        x, y, c, chips, me, dev, sib = unpack(place, ins, outs, sems)
        d2d_s, d2d_r = sems[2], sems[3]
        slots = [(t, j, outs[t].at[4 * px + 2 * py + c]) for t in range(n) for j, (px, py) in enumerate(chips)]
        return [(t, j, slot, remote(slot, slot, d2d_s.at[4 * t + 1 + j], d2d_r.at[4 * t + 1 + j], sib))
                for t, j, slot in slots]

    def mid(place, ins, outs, sems):
        sib = unpack(place, ins, outs, sems)[-1]
        for t, j, slot, cp in forwards(place, ins, outs, sems):
            landed(slot, sems[1].at[3 * t + j], sems, sib)
            cp.start()

    def finish(place, ins, outs, sems):
        x, y, c, chips, me, dev, sib = unpack(place, ins, outs, sems)
        d2d_r, sm_r = sems[3], sems[5]
        own, push = first_wave(place, ins, outs, sems)
        push += [cp for _, _, _, cp in forwards(place, ins, outs, sems)]
        for t in range(n):
            landed(outs[t].at[4 * x + 2 * y + (1 - c)], d2d_r.at[4 * t], sems, sib)
            for j, (px, py) in enumerate(chips):
                landed(outs[t].at[4 * px + 2 * py + (1 - c)], d2d_r.at[4 * t + 1 + j], sems, sib)
        if ns:
            for k in range(1, 8):
                px = (1 - x) if (k >> 2) & 1 else x
                py = (1 - y) if (k >> 1) & 1 else y
                pc = (1 - c) if k & 1 else c
                landed(outs[n].at[4 * px + 2 * py + pc], sm_r.at[k - 1], sems, sib)
        for cp in push:
            cp.wait_send()
        for cp in own:
            cp.wait()

    dma = pltpu.SemaphoreType.DMA
    ins = list(parts) + ([small] if ns else [])
    out_shape = [jax.ShapeDtypeStruct((8,) + p.shape[1:], p.dtype) for p in parts]
    if ns:
        out_shape.append(jax.ShapeDtypeStruct((8,) + small.shape, small.dtype))
    return _Comm(ins, out_shape, [dma((3 * n,)), dma((3 * n,)), dma((4 * n,)), dma((4 * n,)), dma((7,)), dma((7,)),
                                  dma((n + 1,))], start, finish, mid)


ADAM_ROWS = 288


def _row_tile(r, cap):
    if r <= cap:
        return r
    return max((t for t in range(8, cap + 1, 8) if r % t == 0), default=r)


def _adamw_update(w, m, v, land):
    g = land[0].astype(F32)
    for i in range(1, 8):
        g = g + land[i].astype(F32)
    m_new = ADAM_B1 * m + (1.0 - ADAM_B1) * g
    v_new = ADAM_B2 * v + (1.0 - ADAM_B2) * (g * g)
    m_hat = m_new / (1.0 - ADAM_B1 ** ADAM_STEP)
    v_hat = v_new / (1.0 - ADAM_B2 ** ADAM_STEP)
    return g, -ADAM_LR * (m_hat / (jnp.sqrt(v_hat) + ADAM_EPS) + ADAM_WD * w), m_new, v_new


def _adamw(tensors, name, comm=None):
    k = len(tensors)
    r, c = tensors[0][0].shape
    t = _row_tile(r, ADAM_ROWS // k)
    tc = c if t < r or r <= ADAM_ROWS else 2 * LANES
    n = (r // t) * (c // tc)
    nci, nco, nsem = (len(comm.ins), len(comm.out_shape), len(comm.sems)) if comm else (0, 0, 0)

    def kern(*refs):
        ins, cins, outs, couts, csems = _split_refs(refs, (4 * k, nci, 4 * k, nco, nsem))
        if comm:
            place = _place()

            @pl.when(pl.program_id(0) == 0)
            def _():
                comm.start(place, cins, couts, csems)

        for i in range(k):
            w_ref, m_ref, v_ref, l_ref = ins[4 * i:4 * i + 4]
            res = _adamw_update(w_ref[...], m_ref[...], v_ref[...], l_ref)
            for ref, val in zip(outs[4 * i:4 * i + 4], res, strict=True):
                ref[...] = val
        if comm:
            @pl.when(pl.program_id(0) == n - 1)
            def _():
                comm.mid(place, cins, couts, csems)
                comm.finish(place, cins, couts, csems)

    where = (lambda i: (i, 0)) if tc == c else (lambda i: (0, i))
    spec = pl.BlockSpec((t, tc), where)
    lspec = pl.BlockSpec((8, t, tc), lambda i: (0,) + where(i))
    res = pl.pallas_call(
        kern, grid=(n,), in_specs=[spec, spec, spec, lspec] * k + [ANY] * nci, out_specs=[spec] * (4 * k) + [ANY] * nco,
        out_shape=[jax.ShapeDtypeStruct((r, c), F32)] * (4 * k) + (comm.out_shape if comm else []),
        scratch_shapes=comm.sems if comm else [],
        compiler_params=pltpu.CompilerParams(dimension_semantics=("arbitrary" if comm else "parallel",),
                                             vmem_limit_bytes=VMEM_LIMIT),
        name=name)(*[x for tens in tensors for x in tens], *(comm.ins if comm else []))
    return [res[4 * i:4 * i + 4] for i in range(k)], res[4 * k:]


def _step(a):
    def sq(n):
        v = a[n][0] if a[n].ndim == 3 else a[n]
        return v.T if n.removeprefix("m_").removeprefix("v_") in TRANSPOSED else v

    payload = lambda n: sq(n) if n in EXACT_GATHER else sq(n).astype(BF16)

    loss, dx, g, lands_late = _local_step(sq("x"), sq("mem"), a["positions"][0], sq("loss_target"),
                                          {n: a[n] for n in REPLICATED}, [payload(n) for n in EARLY],
                                          [payload(n) for n in LATE])

    sh, rep = _early_grad_shards(g)
    small = _pack_small([rep[n] for n in REPLICATED] + [loss.reshape(1, 1)])
    *lands_early, land_small = _run_comm(_scatter_plan([sh[n] for n in EARLY], small), "scatter_last")
    quad = lambda n, land: (sq(n), sq("m_" + n), sq("v_" + n), land)
    lands = dict(zip(EARLY, lands_early, strict=True)) | lands_late

    outs = {}
    kinds = ("grad_", "delta_", "new_m_", "new_v_")
    for n, _ in SHARDED:
        res = _adamw([quad(n, lands[n])], "adamw_" + n)[0][0]
        for kind, val in zip(kinds, res, strict=True):
            outs[kind + n] = (val.T if n in TRANSPOSED else val).reshape(a[n].shape)
    zero = jnp.zeros((1, 1), F32)
    packed = [_pack_small([a[p + n] for n in REPLICATED] + [zero]) for p in ("", "m_", "v_")]
    res = _adamw([(*packed, land_small)], "adamw_replicated")[0][0]
    widths = [a[n].shape[1] for n in REPLICATED] + [1]
    for kind, buf in zip(kinds, res, strict=True):
        *vals, total = _unpack_small(buf, widths)
        for n, val in zip(REPLICATED, vals, strict=True):
            outs[kind + n] = val
        if kind == "grad_":
            loss = total[0, 0]

    ordered = [outs[kind + n] for kind in kinds for n in WEIGHTS]
    return (loss, dx[None], *ordered)


def kernel(x, mem, positions, norm_mix, w_in, gla_gate_w2, gla_gate_b, gla_out_norm, mla_q_a_norm, mla_w_uq, mla_kv_a_norm, mla_w_ukv, mla_q_norm, mla_k_norm, w_out, norm_xa, norm_mem, xa_w_q, xa_w_kv, xa_q_norm, xa_k_norm, xa_w_o, norm_ffn, ffn_w_gate, ffn_w_up, ffn_conv_w, ffn_conv_b, ffn_w_down, loss_target, m_norm_mix, m_w_in, m_gla_gate_w2, m_gla_gate_b, m_gla_out_norm, m_mla_q_a_norm, m_mla_w_uq, m_mla_kv_a_norm, m_mla_w_ukv, m_mla_q_norm, m_mla_k_norm, m_w_out, m_norm_xa, m_norm_mem, m_xa_w_q, m_xa_w_kv, m_xa_q_norm, m_xa_k_norm, m_xa_w_o, m_norm_ffn, m_ffn_w_gate, m_ffn_w_up, m_ffn_conv_w, m_ffn_conv_b, m_ffn_w_down, v_norm_mix, v_w_in, v_gla_gate_w2, v_gla_gate_b, v_gla_out_norm, v_mla_q_a_norm, v_mla_w_uq, v_mla_kv_a_norm, v_mla_w_ukv, v_mla_q_norm, v_mla_k_norm, v_w_out, v_norm_xa, v_norm_mem, v_xa_w_q, v_xa_w_kv, v_xa_q_norm, v_xa_k_norm, v_xa_w_o, v_norm_ffn, v_ffn_w_gate, v_ffn_w_up, v_ffn_conv_w, v_ffn_conv_b, v_ffn_w_down):
    return _step(dict(locals()))
```

```python
import functools

import jax
import jax.numpy as jnp
import numpy as np
from jax import lax
from jax.experimental import pallas as pl
from jax.experimental.pallas import tpu as pltpu

F32, BF16 = jnp.float32, jnp.bfloat16
MESH = pl.DeviceIdType.MESH

D_MODEL = 1024
EPS = 1e-6
GLA_HEADS, GLA_DK, GLA_DV, GLA_RANK, GLA_CHUNK = 4, 64, 128, 16, 64
GLA_GATE_NORM = 16.0
MLA_HEADS, MLA_Q_RANK, MLA_KV_RANK, MLA_NOPE, MLA_ROPE, MLA_V = 8, 256, 128, 64, 32, 64
MLA_QK = MLA_NOPE + MLA_ROPE
ROPE_THETA = 10000.0
LOG2E, LN2 = 1.4426950408889634, 0.6931471805599453
XA_HEADS, XA_DIM = 4, 128
D_FF = 2816
ADAM_LR, ADAM_B1, ADAM_B2, ADAM_EPS, ADAM_WD, ADAM_STEP = 0.001, 0.9, 0.999, 1e-08, 0.01, 10

LANES = 128
BF16_ROWS = 16
VMEM_LIMIT = 56 * 1024 * 1024
MATMUL_VMEM = 44 * 1024 * 1024

P_GQ, P_GK, P_GV, P_OG, P_CQ, P_CKV, P_KPE, P_ALR, P_WIDTH = 0, 256, 512, 1024, 1536, 1792, 1920, 2048, 2176
N_GQ, N_GK, N_GV, N_ALR, N_OG, N_CQ, N_CKV, N_KPE, N_WIDTH = 0, 256, 512, 1024, 1040, 1552, 1808, 1936, 1968

SHARDED = (("w_in", 1), ("gla_gate_w2", 1), ("mla_w_uq", 1), ("mla_w_ukv", 1), ("w_out", 0), ("xa_w_q", 0),
           ("xa_w_kv", 0), ("xa_w_o", 1), ("ffn_w_gate", 1), ("ffn_w_up", 1), ("ffn_conv_w", 1), ("ffn_w_down", 0))
REPLICATED = ("norm_mix", "gla_gate_b", "gla_out_norm", "mla_q_a_norm", "mla_kv_a_norm", "mla_q_norm", "mla_k_norm",
              "norm_xa", "norm_mem", "xa_q_norm", "xa_k_norm", "norm_ffn", "ffn_conv_b")
EXACT_GATHER = ("gla_gate_w2", "ffn_conv_w")
TRANSPOSED = ("w_in", "ffn_w_gate", "ffn_w_up")
EARLY = ("w_in", "gla_gate_w2", "mla_w_uq", "mla_w_ukv")
LATE = tuple(n for n, _ in SHARDED if n not in EARLY)
WEIGHTS = ("norm_mix", "w_in", "gla_gate_w2", "gla_gate_b", "gla_out_norm", "mla_q_a_norm", "mla_w_uq",
           "mla_kv_a_norm", "mla_w_ukv", "mla_q_norm", "mla_k_norm", "w_out", "norm_xa", "norm_mem", "xa_w_q",
           "xa_w_kv", "xa_q_norm", "xa_k_norm", "xa_w_o", "norm_ffn", "ffn_w_gate", "ffn_w_up", "ffn_conv_w",
           "ffn_conv_b", "ffn_w_down")


_NN = ((1,), (0,))
_NT = ((1,), (1,))
_TN = ((0,), (0,))


def _dg(a, b, dims):
    return lax.dot_general(a.astype(BF16), b.astype(BF16), (dims, ((), ())), preferred_element_type=F32)


@jax.custom_vjp
def _dot_nn(a, b):
    return _dg(a, b, _NN)


_dot_nn.defvjp(lambda a, b: (_dg(a, b, _NN), (a, b)),
               lambda r, g: (_dg(g, r[1], _NT).astype(r[0].dtype), _dg(r[0], g, _TN).astype(r[1].dtype)))


@jax.custom_vjp
def _dot_nt(a, b):
    return _dg(a, b, _NT)


_dot_nt.defvjp(lambda a, b: (_dg(a, b, _NT), (a, b)),
               lambda r, g: (_dg(g, r[1], _NN).astype(r[0].dtype), _dg(g, r[0], _TN).astype(r[1].dtype)))


@jax.custom_vjp
def _dot_tn(a, b):
    return _dg(a, b, _TN)


_dot_tn.defvjp(lambda a, b: (_dg(a, b, _TN), (a, b)),
               lambda r, g: (_dg(r[1], g, _NT).astype(r[0].dtype), _dg(r[0], g, _NN).astype(r[1].dtype)))


def _rms(x, w, n=None):
    n = x.shape[-1] if n is None else n
    ms = jnp.sum(x * x, axis=-1, keepdims=True) * (1.0 / n)
    return x * lax.rsqrt(ms + EPS) * w


def _silu(x):
    return x * jax.nn.sigmoid(x)


def _log_sigmoid(x):
    return jnp.minimum(x, 0.0) - jnp.log(1.0 + jnp.exp(-jnp.abs(x)))


@jax.custom_vjp
def _rope(y, c, sa, sb):
    return y * c + pltpu.roll(y, LANES - 16, 1) * sa + pltpu.roll(y, 16, 1) * sb


def _rope_bwd(res, g):
    c, sa, sb = res
    gy = g * c + pltpu.roll(g * sa, 16, 1) + pltpu.roll(g * sb, LANES - 16, 1)
    return gy, jnp.zeros_like(c), jnp.zeros_like(sa), jnp.zeros_like(sb)


_rope.defvjp(lambda y, c, sa, sb: (_rope(y, c, sa, sb), (c, sa, sb)), _rope_bwd)


@jax.custom_vjp
def _cumsum_rows(x):
    n = x.shape[0]
    row = lax.broadcasted_iota(jnp.int32, x.shape, 0)
    k = 1
    while k < n:
        x = x + jnp.where(row >= k, pltpu.roll(x, k, 0), 0.0)
        k *= 2
    return x


def _cumsum_rows_bwd(_, g):
    n = g.shape[0]
    row = lax.broadcasted_iota(jnp.int32, g.shape, 0)
    k = 1
    while k < n:
        g = g + jnp.where(row < n - k, pltpu.roll(g, n - k, 0), 0.0)
        k *= 2
    return (g,)


_cumsum_rows.defvjp(lambda x: (_cumsum_rows(x), None), _cumsum_rows_bwd)


def _lane_mask(lo, hi):
    lane = lax.broadcasted_iota(jnp.int32, (1, LANES), 1)
    return ((lane >= lo) & (lane < hi)).astype(F32)


def _tile(n, t):
    t = min(n, t)
    assert n % t == 0, (n, t)
    return t


class _Epilogue:
    def __init__(self, fn, rows=(), consts=(), outs=(), accs=()):
        self.fn, self.rows, self.consts, self.outs, self.accs = fn, list(rows), list(consts), list(outs), list(accs)


def _matmul(a, b, mode, out_dtype, name, residual=None, a_lead=None, b_lead=None, more=None, epilogue=None):
    (a0, a1), (b0, b1) = a.shape[-2:], b.shape[-2:]
    if mode == "nn":
        m, k, k2, n = a0, a1, b0, b1
    elif mode == "nt":
        m, k, n, k2 = a0, a1, b0, b1
    else:
        k, m, k2, n = a0, a1, b0, b1
    assert k == k2, (a.shape, b.shape, mode)
    npar = 4 if "p" in (a_lead, b_lead) else 1
    nsum = 4 if "k" in (a_lead, b_lead) else 1
    pairs = [(a, b)] + ([more] if more else [])
    a_item, b_item, o_item = a.dtype.itemsize, b.dtype.itemsize, jnp.dtype(out_dtype).itemsize
    ep = epilogue
    row_extra = 4 if residual is not None else 0
    if ep:
        row_extra += (sum(r.dtype.itemsize * wd for r, wd, _ in ep.rows) + sum(jnp.dtype(d).itemsize * wd for wd, d in ep.outs)) / n

    def vmem_need(tm, tn, tk):
        need = 2 * (nsum if a_lead == "k" else 1) * tm * tk * a_item + 2 * (nsum if b_lead == "k" else 1) * tk * tn * b_item
        need *= len(pairs)
        need += (0 if ep else 2 * tm * tn * o_item) + tm * tn * 4 * (2 if tk < k else 1)
        need += tm * tk * 2 * (a_item == 4 or mode == "tn") + tk * tn * 2 * (b_item == 4)
        return need + int(2 * tm * tn * row_extra) + (3 * tm * tn * 4 if ep else 0)

    halvings = (4096, 2048, 1024, 512, 256, 128, 64, 32, 16, 8)
    if mode == "tn":
        tm = m if m <= 2304 else m // 2
        tn = n if tm * n <= 1024 * 2304 else n // 2
        tk = next((r for r in halvings if k % r == 0 and vmem_need(tm, tn, r) <= MATMUL_VMEM), k)
    else:
        tn, tk = n, k
        tm = next((r for r in halvings if m % r == 0 and vmem_need(r, tn, tk) <= MATMUL_VMEM), m)
    assert m % tm == 0 and n % tn == 0 and k % tk == 0
    assert ep is None or (tn == n and tk == k and npar == 1)
    nk = k // tk
    dims = {"nn": _NN, "nt": _NT, "tn": _TN}[mode]
    n_in = 2 * len(pairs) + (residual is not None)
    n_ep_in = len(ep.rows) + len(ep.consts) if ep else 0
    n_out = len(ep.outs) + len(ep.accs) if ep else 1

    def body(*refs):
        ab, rs, ep_in, outs, scratch = _split_refs(refs, (2 * len(pairs), n_in - 2 * len(pairs), n_ep_in, n_out, nk > 1))
        prod = None
        for a_ref, b_ref in zip(ab[0::2], ab[1::2]):
            for sh in range(nsum):
                term = _dg(a_ref[sh] if a_lead == "k" else a_ref[...], b_ref[sh] if b_lead == "k" else b_ref[...], dims)
                prod = term if prod is None else prod + term

        def finish(r):
            if rs:
                r = r + rs[0][...]
            if ep is None:
                outs[0][...] = r.astype(outs[0].dtype)
                return
            vals = [x[...] for x in ep_in]
            ro, ao = ep.fn(r, vals[:len(ep.rows)], vals[len(ep.rows):])
            for ref, val in zip(outs[:len(ep.outs)], ro, strict=True):
                ref[...] = val.astype(ref.dtype)
            if ep.accs:
                @pl.when(pl.program_id(0) == 0)
                def _():
                    for ref in outs[len(ep.outs):]:
                        ref[...] = jnp.zeros_like(ref)

                for ref, val in zip(outs[len(ep.outs):], ao, strict=True):
                    ref[...] += val

        if nk == 1:
            finish(prod)
            return
        acc = scratch[0]
        kk = pl.program_id(3)

        @pl.when(kk == 0)
        def _():
            acc[...] = prod

        @pl.when(kk > 0)
        def _():
            acc[...] += prod

        @pl.when(kk == nk - 1)
        def _():
            finish(acc[...])

    def spec(lead, blk, idx):
        if lead is None:
            return pl.BlockSpec(blk, lambda i, j, p, kk: idx(i, j, kk))
        if lead == "p":
            return pl.BlockSpec((None,) + blk, lambda i, j, p, kk: (p,) + idx(i, j, kk))
        return pl.BlockSpec((nsum,) + blk, lambda i, j, p, kk: (0,) + idx(i, j, kk))

    if mode == "nn":
        pair_specs = [spec(a_lead, (tm, tk), lambda i, j, kk: (i, kk)), spec(b_lead, (tk, tn), lambda i, j, kk: (kk, j))]
    elif mode == "nt":
        pair_specs = [spec(a_lead, (tm, tk), lambda i, j, kk: (i, kk)), spec(b_lead, (tn, tk), lambda i, j, kk: (j, kk))]
    else:
        pair_specs = [spec(a_lead, (tk, tm), lambda i, j, kk: (kk, i)), spec(b_lead, (tk, tn), lambda i, j, kk: (kk, j))]
    tile = spec(None, (tm, tn), lambda i, j, kk: (i, j))
    in_specs = pair_specs * len(pairs)
    args = [x for pair in pairs for x in pair]
    if residual is not None:
        assert npar == 1
        in_specs.append(tile)
        args.append(residual)
    if ep:
        in_specs += [pl.BlockSpec((tm, wd), functools.partial(lambda cb, i, j, p, kk: (i, cb), cb)) for _, wd, cb in ep.rows]
        in_specs += [pl.BlockSpec(c.shape, lambda i, j, p, kk: (0, 0)) for c in ep.consts]
        args += [r for r, _, _ in ep.rows] + ep.consts
        out_specs = [pl.BlockSpec((tm, wd), lambda i, j, p, kk: (i, 0)) for wd, _ in ep.outs]
        out_specs += [pl.BlockSpec(shape, lambda i, j, p, kk: (0, 0)) for shape in ep.accs]
        out_shape = [jax.ShapeDtypeStruct((m, wd), d) for wd, d in ep.outs] + [jax.ShapeDtypeStruct(sh, F32) for sh in ep.accs]
    else:
        out_specs = spec("p" if npar > 1 else None, (tm, tn), lambda i, j, kk: (i, j))
        out_shape = jax.ShapeDtypeStruct(((4,) if npar > 1 else ()) + (m, n), out_dtype)
    outer = "arbitrary" if ep and ep.accs else "parallel"
    return pl.pallas_call(
        body, grid=(m // tm, n // tn, npar, nk), in_specs=in_specs, out_specs=out_specs, out_shape=out_shape,
        scratch_shapes=[pltpu.VMEM((tm, tn), F32)] if nk > 1 else [],
        compiler_params=pltpu.CompilerParams(dimension_semantics=(outer, outer, outer, "arbitrary"),
                                             vmem_limit_bytes=VMEM_LIMIT),
        name=name)(*args)


def _row(a, width=None, col_block=0):
    return (a, a.shape[1] if width is None else width, col_block)


def _rows_call(body, rows, consts, outs, accs=(), *, name, tile=512):
    s = rows[0][0].shape[0]
    t = _tile(s, tile)
    nr, nc, no = len(rows), len(consts), len(outs)

    def kern(*refs):
        r = [x[...] for x in refs[:nr]]
        c = [x[...] for x in refs[nr:nr + nc]]
        o_refs = refs[nr + nc:nr + nc + no]
        a_refs = refs[nr + nc + no:]
        ro, ao = body(r, c)
        for ref, val in zip(o_refs, ro, strict=True):
            ref[...] = val.astype(ref.dtype)
        if a_refs:
            @pl.when(pl.program_id(0) == 0)
            def _():
                for ref in a_refs:
                    ref[...] = jnp.zeros_like(ref)

            for ref, val in zip(a_refs, ao, strict=True):
                ref[...] += val

    in_specs = [pl.BlockSpec((t, w), functools.partial(lambda cb, i: (i, cb), cb)) for (_, w, cb) in rows]
    in_specs += [pl.BlockSpec(c.shape, lambda i: (0, 0)) for c in consts]
    out_specs = [pl.BlockSpec((t, w), lambda i: (i, 0)) for (w, _) in outs]
    out_specs += [pl.BlockSpec(shape, lambda i: (0, 0)) for shape in accs]
    out_shape = [jax.ShapeDtypeStruct((s, w), dt) for (w, dt) in outs]
    out_shape += [jax.ShapeDtypeStruct(shape, F32) for shape in accs]
    return pl.pallas_call(
        kern, grid=(s // t,), in_specs=in_specs, out_specs=out_specs, out_shape=out_shape,
        compiler_params=pltpu.CompilerParams(dimension_semantics=("arbitrary" if accs else "parallel",),
                                             vmem_limit_bytes=VMEM_LIMIT),
        name=name)(*[r[0] for r in rows], *consts)


def _gla_chunk(q, k, la, v0, v1, s0, s1):
    c = q.shape[0]
    r = lax.broadcasted_iota(jnp.int32, (c, c), 0)
    cc = lax.broadcasted_iota(jnp.int32, (c, c), 1)
    tril = cc <= r
    cum = _cumsum_rows(la)
    cl = jnp.sum(la, axis=0, keepdims=True)
    qd = q * (GLA_DK ** -0.5) * jnp.exp(cum)
    ki = k * jnp.exp(-cum)
    ke = k * jnp.exp(cl - cum)
    dec = jnp.exp(cl)
    outs, news = [], []
    for h, (v, s) in enumerate(((v0, s0), (v1, s1))):
        mk = _lane_mask(GLA_DK * h, GLA_DK * (h + 1))
        qh = qd * mk
        att = jnp.where(tril, _dot_nt(qh, ki), 0.0)
        outs.append(_dot_nn(att, v) + _dot_nt(qh, s))
        news.append(s * dec + _dot_tn(v, ke * mk))
    return outs[0], outs[1], news[0], news[1]


def _gla_specs(tb, rev_nb=None):
    blk = (lambda b: b) if rev_nb is None else (lambda b: rev_nb - 1 - b)
    q = pl.BlockSpec((tb, 128), lambda p, b: (blk(b), P_GQ // 128 + p))
    k = pl.BlockSpec((tb, 128), lambda p, b: (blk(b), P_GK // 128 + p))
    la = pl.BlockSpec((tb, 128), lambda p, b: (blk(b), p))
    v = pl.BlockSpec((tb, 256), lambda p, b: (blk(b), P_GV // 256 + p))
    o = pl.BlockSpec((tb, 256), lambda p, b: (blk(b), p))
    st = pl.BlockSpec((tb // GLA_CHUNK, 2, 128, 128), lambda p, b: (blk(b), p, 0, 0))
    return q, k, la, v, o, st


def _gla_fwd(proj, la):
    s = proj.shape[0]
    tb = _tile(s, 512)
    nb, nch = s // tb, tb // GLA_CHUNK

    def kern(q_ref, k_ref, la_ref, v_ref, o_ref, st_ref, s_sc):
        @pl.when(pl.program_id(1) == 0)
        def _():
            s_sc[...] = jnp.zeros_like(s_sc)

        s0, s1 = s_sc[0], s_sc[1]
        for ci in range(nch):
            sl = slice(ci * GLA_CHUNK, (ci + 1) * GLA_CHUNK)
            st_ref[ci, 0] = s0
            st_ref[ci, 1] = s1
            o0, o1, s0, s1 = _gla_chunk(q_ref[sl, :], k_ref[sl, :], la_ref[sl, :], v_ref[sl, 0:128],
                                        v_ref[sl, 128:256], s0, s1)
            o_ref[sl, 0:128] = o0
            o_ref[sl, 128:256] = o1
        s_sc[0] = s0
        s_sc[1] = s1

    q, k, lasp, v, o, st = _gla_specs(tb)
    return pl.pallas_call(
        kern, grid=(2, nb), in_specs=[q, k, lasp, v], out_specs=[o, st],
        out_shape=[jax.ShapeDtypeStruct((s, 512), F32),
                   jax.ShapeDtypeStruct((s // GLA_CHUNK, GLA_HEADS, 128, 128), F32)],
        scratch_shapes=[pltpu.VMEM((2, 128, 128), F32)],
        compiler_params=pltpu.CompilerParams(dimension_semantics=("parallel", "arbitrary"),
                                             vmem_limit_bytes=VMEM_LIMIT),
        name="gla_fwd")(proj, proj, la, proj)


def _gla_bwd(proj, la, states, d_o, comm):
    s = proj.shape[0]
    tb = _tile(s, 512)
    nb, nch = s // tb, tb // GLA_CHUNK
    nci, nco = len(comm.ins), len(comm.out_shape)

    def kern(*refs):
        (q_ref, k_ref, la_ref, v_ref, do_ref, st_ref), cins, (dq_ref, dk_ref, dla_ref, dv_ref), couts, (ds_sc,), csems = \
            _split_refs(refs, (6, nci, 4, nco, 1, len(comm.sems)))
        place = _place()
        pair, blk = pl.program_id(0), pl.program_id(1)

        @pl.when((pair == 0) & (blk == 0))
        def _():
            comm.start(place, cins, couts, csems)

        @pl.when((pair == 1) & (blk == nb // 2))
        def _():
            comm.mid(place, cins, couts, csems)

        @pl.when(blk == 0)
        def _():
            ds_sc[...] = jnp.zeros_like(ds_sc)

        d0, d1 = ds_sc[0], ds_sc[1]
        for ci in reversed(range(nch)):
            sl = slice(ci * GLA_CHUNK, (ci + 1) * GLA_CHUNK)
            _, vjp = jax.vjp(_gla_chunk, q_ref[sl, :], k_ref[sl, :], la_ref[sl, :], v_ref[sl, 0:128],
                             v_ref[sl, 128:256], st_ref[ci, 0], st_ref[ci, 1])
            gq, gk, gla, gv0, gv1, d0, d1 = vjp((do_ref[sl, 0:128], do_ref[sl, 128:256], d0, d1))
            dq_ref[sl, :] = gq
            dk_ref[sl, :] = gk
            dla_ref[sl, :] = gla
            dv_ref[sl, 0:128] = gv0
            dv_ref[sl, 128:256] = gv1
        ds_sc[0] = d0
        ds_sc[1] = d1

        @pl.when((pair == 1) & (blk == nb - 1))
        def _():
            comm.finish(place, cins, couts, csems)

    q, k, lasp, v, o, st = _gla_specs(tb, rev_nb=nb)
    res = pl.pallas_call(
        kern, grid=(2, nb), in_specs=[q, k, lasp, v, o, st] + [ANY] * nci, out_specs=[lasp, lasp, lasp, o] + [ANY] * nco,
        out_shape=[jax.ShapeDtypeStruct((s, 256), F32), jax.ShapeDtypeStruct((s, 256), F32),
                   jax.ShapeDtypeStruct((s, 256), F32), jax.ShapeDtypeStruct((s, 512), F32)] + comm.out_shape,
        scratch_shapes=[pltpu.VMEM((2, 128, 128), F32)] + comm.sems,
        compiler_params=pltpu.CompilerParams(dimension_semantics=("arbitrary", "arbitrary"),
                                             vmem_limit_bytes=VMEM_LIMIT),
        name="gla_bwd")(proj, proj, la, proj, d_o, states, *comm.ins)
    return res[0], res[1], res[2], res[3], res[4:]


def _causal_keep(t, qi, ki):
    row = lax.broadcasted_iota(jnp.int32, (t, t), 0) + qi * t
    col = lax.broadcasted_iota(jnp.int32, (t, t), 1) + ki * t
    return col <= row


def _split_refs(refs, counts):
    out, off = [], 0
    for cnt in counts:
        out.append(refs[off:off + cnt])
        off += cnt
    return out


def _causal_blocks(n, key_major):
    pairs = ([(ki, qi) for ki in range(n) for qi in range(ki, n)] if key_major else
             [(ki, qi) for qi in range(n) for ki in range(qi + 1)])
    return np.array([ki for ki, _ in pairs], np.int32), np.array([qi for _, qi in pairs], np.int32)


def _attn_fwd(q, k, v, comm, tile=1024):
    s = q.shape[0]
    t = _tile(s, tile)
    n = s // t
    nci, nco = len(comm.ins), len(comm.out_shape)

    ki_tab, qi_tab = _causal_blocks(n, key_major=False)
    steps = len(ki_tab)

    def kern(ki_ref, qi_ref, *refs):
        (q_ref, k_ref, v_ref), cins, (o_ref, lse_ref), couts, (m_sc, l_sc, acc_sc), csems = _split_refs(
            refs, (3, nci, 2, nco, 3, len(comm.sems)))
        pair, step = pl.program_id(0), pl.program_id(1)
        qi, ki = qi_ref[step], ki_ref[step]
        place = _place()

        @pl.when((pair == 0) & (step == 0))
        def _():
            comm.start(place, cins, couts, csems)

        @pl.when((pair == MLA_HEADS // 2 - 1) & (step == 0))
        def _():
            comm.mid(place, cins, couts, csems)

        first = lax.broadcasted_iota(jnp.int32, (t, LANES), 1) < MLA_V

        @pl.when(ki == 0)
        def _():
            m_sc[...] = jnp.full_like(m_sc, -jnp.inf)
            l_sc[...] = jnp.zeros_like(l_sc)
            acc_sc[...] = jnp.zeros_like(acc_sc)

        def update(diagonal):
            keep = _causal_keep(t, 0, 0)
            alphas, pvs = [], []
            for h in range(2):
                sc = _dg(q_ref[:, 128 * h:128 * (h + 1)], k_ref[:, 128 * h:128 * (h + 1)], _NT)
                if diagonal:
                    sc = jnp.where(keep, sc, -jnp.inf)
                m_prev = m_sc[h]
                m_new = jnp.maximum(m_prev, jnp.max(sc, axis=1, keepdims=True))
                alpha = jnp.exp2(m_prev - m_new)
                p = jnp.exp2(sc - m_new[:, 0:1])
                l_sc[h] = alpha * l_sc[h] + jnp.sum(p, axis=1, keepdims=True)
                m_sc[h] = m_new
                alphas.append(alpha)
                pvs.append(_dg(p, v_ref[...], _NN))
            acc_sc[...] = acc_sc[...] * jnp.where(first, alphas[0], alphas[1]) + jnp.where(first, pvs[0], pvs[1])

        @pl.when(ki < qi)
        def _():
            update(False)

        @pl.when(ki == qi)
        def _():
            update(True)

        @pl.when(ki == qi)
        def _():
            l = jnp.where(first, l_sc[0], l_sc[1])
            m = jnp.where(first, m_sc[0], m_sc[1])
            o_ref[...] = acc_sc[...] / l
            lse_ref[...] = m + jnp.log2(l)

        @pl.when((pair == MLA_HEADS // 2 - 1) & (step == steps - 1))
        def _():
            comm.finish(place, cins, couts, csems)

    q_idx = lambda p, st, ki_r, qi_r: (qi_r[st], p)
    k_idx = lambda p, st, ki_r, qi_r: (ki_r[st], p)
    res = pl.pallas_call(
        kern, grid_spec=pltpu.PrefetchScalarGridSpec(
            num_scalar_prefetch=2, grid=(MLA_HEADS // 2, steps),
            in_specs=[pl.BlockSpec((t, 256), q_idx), pl.BlockSpec((t, 256), k_idx), pl.BlockSpec((t, 128), k_idx)]
            + [ANY] * nci,
            out_specs=[pl.BlockSpec((t, 128), q_idx), pl.BlockSpec((t, 128), q_idx)] + [ANY] * nco,
            scratch_shapes=[pltpu.VMEM((2, t, LANES), F32), pltpu.VMEM((2, t, LANES), F32),
                            pltpu.VMEM((t, LANES), F32)] + comm.sems),
        out_shape=[jax.ShapeDtypeStruct((s, 512), F32), jax.ShapeDtypeStruct((s, 512), F32)] + comm.out_shape,
        compiler_params=pltpu.CompilerParams(dimension_semantics=("arbitrary", "arbitrary"),
                                             vmem_limit_bytes=VMEM_LIMIT),
        name="mla_attn_fwd")(ki_tab, qi_tab, q, k, v, *comm.ins)
    return res[0], res[1], res[2:]


def _attn_bwd(q, k, v, o, lse, d_o, comm, tile=512):
    s = q.shape[0]
    t = _tile(s, tile)
    n = s // t
    nci, nco = len(comm.ins), len(comm.out_shape)

    ki_tab, qi_tab = _causal_blocks(n, key_major=True)
    steps = len(ki_tab)

    def kern(ki_ref, qi_ref, *refs):
        (q_ref, k_ref, v_ref, o_ref, lse_ref, do_ref), cins, (dq_ref, dk_ref, dv_ref), couts, (dk_sc, dv_sc), csems = \
            _split_refs(refs, (6, nci, 3, nco, 2, len(comm.sems)))
        pair, step = pl.program_id(0), pl.program_id(1)
        ki, qi = ki_ref[step], qi_ref[step]
        place = _place()

        @pl.when((pair == 0) & (step == 0))
        def _():
            comm.start(place, cins, couts, csems)

        @pl.when((pair == MLA_HEADS // 2 - 1) & (step == 0))
        def _():
            comm.mid(place, cins, couts, csems)

        @pl.when((ki == 0) & (qi == 0))
        def _():
            dq_ref[...] = jnp.zeros_like(dq_ref)

        @pl.when(qi == ki)
        def _():
            dk_sc[...] = jnp.zeros_like(dk_sc)
            dv_sc[...] = jnp.zeros_like(dv_sc)

        def update(diagonal):
            keep = _causal_keep(t, 0, 0)
            d_o = do_ref[...]
            prod = d_o * o_ref[...]
            rows = pl.ds(pl.multiple_of(qi * t, t), t)
            for h in range(2):
                hs = slice(128 * h, 128 * (h + 1))
                mk = _lane_mask(MLA_V * h, MLA_V * (h + 1))
                qh, kh = q_ref[:, hs], k_ref[:, hs]
                sc = _dg(qh, kh, _NT)
                if diagonal:
                    sc = jnp.where(keep, sc, -jnp.inf)
                p = jnp.exp2(sc - lse_ref[:, MLA_V * h:MLA_V * h + 1])
                doh = d_o * mk
                dp = _dg(doh * LN2, v_ref[...], _NT)
                delta = jnp.sum(prod * mk, axis=1, keepdims=True) * LN2
                ds = p * (dp - delta)
                dv_sc[...] += _dg(p, doh, _TN)
                dk_sc[:, hs] += _dg(ds, qh, _TN)
                dq_ref[rows, hs] += _dg(ds, kh, _NN)

        @pl.when(qi > ki)
        def _():
            update(False)

        @pl.when(qi == ki)
        def _():
            update(True)

        @pl.when(qi == n - 1)
        def _():
            dk_ref[...] = dk_sc[...]
            dv_ref[...] = dv_sc[...].astype(dv_ref.dtype)

        @pl.when((pair == MLA_HEADS // 2 - 1) & (step == steps - 1))
        def _():
            comm.finish(place, cins, couts, csems)

    q_idx = lambda p, st, ki_r, qi_r: (qi_r[st], p)
    k_idx = lambda p, st, ki_r, qi_r: (ki_r[st], p)
    res = pl.pallas_call(
        kern, grid_spec=pltpu.PrefetchScalarGridSpec(
            num_scalar_prefetch=2, grid=(MLA_HEADS // 2, steps),
            in_specs=[pl.BlockSpec((t, 256), q_idx), pl.BlockSpec((t, 256), k_idx), pl.BlockSpec((t, 128), k_idx),
                      pl.BlockSpec((t, 128), q_idx), pl.BlockSpec((t, 128), q_idx), pl.BlockSpec((t, 128), q_idx)]
            + [ANY] * nci,
            out_specs=[pl.BlockSpec((s, 256), lambda p, st, ki_r, qi_r: (0, p)), pl.BlockSpec((t, 256), k_idx),
                       pl.BlockSpec((t, 128), k_idx)] + [ANY] * nco,
            scratch_shapes=[pltpu.VMEM((t, 256), F32), pltpu.VMEM((t, 128), F32)] + comm.sems),
        out_shape=[jax.ShapeDtypeStruct((s, 1024), F32), jax.ShapeDtypeStruct((s, 1024), F32),
                   jax.ShapeDtypeStruct((s, 512), BF16)] + comm.out_shape,
        compiler_params=pltpu.CompilerParams(dimension_semantics=("arbitrary", "arbitrary"),
                                             vmem_limit_bytes=VMEM_LIMIT),
        name="mla_attn_bwd")(ki_tab, qi_tab, q, k, v, o, lse, d_o, *comm.ins)
    return res[0], res[1], res[2], res[3:]


def _gate_fn(alr, w2, b):
    return _log_sigmoid(_dot_nn(alr, w2) + b) * (1.0 / GLA_GATE_NORM)


def _qk_head(qh, kh, kpe, c, sa, sb, qn, kn):
    kfull = kh + kpe * _lane_mask(MLA_NOPE, MLA_QK)
    q_r = _rope(_rms(qh, qn, MLA_QK), c, sa, sb) * (MLA_QK ** -0.5 * LOG2E)
    k_r = _rope(_rms(kfull, kn, MLA_QK), c, sa, sb)
    return q_r, k_r


def _mix_head(o, og, gn):
    return _rms(o, gn) * _silu(og)


def _xa_head(xq, xk, xv, qn, kn):
    sc = _dot_nt(_rms(xq, qn), _rms(xk, kn)) * (XA_DIM ** -0.5)
    e = jnp.exp(sc - lax.stop_gradient(jnp.max(sc, axis=1, keepdims=True)))
    p = e / jnp.sum(e, axis=1, keepdims=True)
    return _dot_nn(p, xv)


def _heads(x, n):
    return [x[:, 128 * h:128 * (h + 1)] for h in range(n)]


def _cat(xs):
    return jnp.concatenate(xs, axis=1)


def _norm_fwd(x, w, name):
    return _rows_call(lambda r, c: ([_rms(r[0], c[0])], []), [_row(x)], [w], [(x.shape[1], BF16)], name=name)[0]


def _norm_fwd_epilogue(w):
    return _Epilogue(lambda h, rows, consts: ([h, _rms(h, consts[0])], []), [], [w], [(D_MODEL, F32), (D_MODEL, BF16)], [])


def _norm_bwd_epilogue(x, w, add):
    def fn(d_out, rows, consts):
        _, vjp = jax.vjp(_rms, rows[0], consts[0])
        dx, dw = vjp(d_out)
        return [dx + rows[1]], [dw]

    return _Epilogue(fn, [_row(x), _row(add)], [w], [(D_MODEL, F32)], [w.shape])


def _norm_fwd_comm(x, w, comm, name):
    s, d = x.shape
    t = _tile(s, 512)
    n = s // t
    nci, nco = len(comm.ins), len(comm.out_shape)

    def kern(*refs):
        (x_ref, w_ref), cins, (o_ref,), couts, csems = _split_refs(refs, (2, nci, 1, nco, len(comm.sems)))
        place = _place()

        @pl.when(pl.program_id(0) == 0)
        def _():
            comm.start(place, cins, couts, csems)

        o_ref[...] = _rms(x_ref[...], w_ref[...]).astype(o_ref.dtype)

        @pl.when(pl.program_id(0) == n - 1)
        def _():
            comm.mid(place, cins, couts, csems)
            comm.finish(place, cins, couts, csems)

    tile = pl.BlockSpec((t, d), lambda i: (i, 0))
    res = pl.pallas_call(
        kern, grid=(n,), in_specs=[tile, pl.BlockSpec(w.shape, lambda i: (0, 0))] + [ANY] * nci,
        out_specs=[tile] + [ANY] * nco, out_shape=[jax.ShapeDtypeStruct((s, d), BF16)] + comm.out_shape,
        scratch_shapes=comm.sems,
        compiler_params=pltpu.CompilerParams(dimension_semantics=("arbitrary",), vmem_limit_bytes=VMEM_LIMIT),
        name=name)(x, w, *comm.ins)
    return res[0], res[1:]


def _norm_bwd(x, w, d_out, add, name):
    def body(r, c):
        _, vjp = jax.vjp(_rms, r[0], c[0])
        dx, dw = vjp(r[1])
        return [dx + r[2]], [dw]

    return _rows_call(body, [_row(x), _row(d_out), _row(add)], [w], [(x.shape[1], F32)], [w.shape], name=name)


CONV_HALO = BF16_ROWS


def _conv_specs(s, f, t):
    n8 = t // CONV_HALO
    cur = pl.BlockSpec((None, t, f), lambda j, i: (j, i, 0))
    prev = pl.BlockSpec((None, CONV_HALO, f), lambda j, i: (j, jnp.maximum(i * n8 - 1, 0), 0))
    nxt = pl.BlockSpec((None, CONV_HALO, f), lambda j, i: (j, jnp.minimum((i + 1) * n8, s // CONV_HALO - 1), 0))
    cw = pl.BlockSpec((None, 3, f), lambda j, i: (j, 0, 0))
    cb = pl.BlockSpec((None, 1, f), lambda j, i: (j, 0, 0))
    return cur, prev, nxt, cw, cb


def _conv_taps(g, prev, first):
    ext = jnp.concatenate([jnp.where(first, 0.0, prev.astype(F32)), g], axis=0)
    return pltpu.roll(ext, 1, 0)[CONV_HALO:], pltpu.roll(ext, 2, 0)[CONV_HALO:]


def _conv_fwd(gg, uu, cw, cb):
    _, s, f = gg.shape
    t = _tile(s, 512)

    def kern(g_ref, gp_ref, u_ref, cw_ref, cb_ref, o_ref):
        g = g_ref[...].astype(F32)
        g1, g2 = _conv_taps(g, gp_ref[...], pl.program_id(1) == 0)
        w = cw_ref[...]
        gc = cb_ref[...] + w[0:1] * g2 + w[1:2] * g1 + w[2:3] * g
        o_ref[...] = (_silu(gc) * u_ref[...].astype(F32)).astype(o_ref.dtype)

    cur, prev, _, cws, cbs = _conv_specs(s, f, t)
    return pl.pallas_call(
        kern, grid=(4, s // t), in_specs=[cur, prev, cur, cws, cbs], out_specs=cur,
        out_shape=jax.ShapeDtypeStruct(gg.shape, BF16),
        compiler_params=pltpu.CompilerParams(dimension_semantics=("parallel", "parallel"), vmem_limit_bytes=VMEM_LIMIT),
        name="ffn_conv_fwd")(gg, gg, uu, cw, cb)


def _conv_bwd(gg, uu, dact, cw, cb):
    _, s, f = gg.shape
    t = _tile(s, 512)
    nt = s // t

    def kern(g_ref, gp_ref, gn_ref, u_ref, un_ref, da_ref, dan_ref, cw_ref, cb_ref, du_ref, dg_ref, dcw_ref, dcb_ref):
        i = pl.program_id(1)
        cat = lambda a_ref, b_ref: jnp.concatenate([a_ref[...].astype(F32), b_ref[...].astype(F32)], axis=0)
        g, u, da = cat(g_ref, gn_ref), cat(u_ref, un_ref), cat(da_ref, dan_ref)
        g1, g2 = _conv_taps(g, gp_ref[...], i == 0)
        w = cw_ref[...]
        gc = cb_ref[...] + w[0:1] * g2 + w[1:2] * g1 + w[2:3] * g
        sg = jax.nn.sigmoid(gc)
        du_ref[...] = (da[:t] * (gc[:t] * sg[:t])).astype(du_ref.dtype)
        row = lax.broadcasted_iota(jnp.int32, (t + CONV_HALO, 1), 0)
        dgc = jnp.where((row < t) | (i < nt - 1), da * u * (sg * (1.0 + gc * (1.0 - sg))), 0.0)
        up1 = pltpu.roll(dgc, t + CONV_HALO - 1, 0)[:t]
        up2 = pltpu.roll(dgc, t + CONV_HALO - 2, 0)[:t]
        dgc = dgc[:t]
        dg_ref[...] = (w[2:3] * dgc + w[1:2] * up1 + w[0:1] * up2).astype(dg_ref.dtype)

        @pl.when(i == 0)
        def _():
            dcw_ref[...] = jnp.zeros_like(dcw_ref)
            dcb_ref[...] = jnp.zeros_like(dcb_ref)

        dcw_ref[0:1, :] += jnp.sum(dgc * g2[:t], axis=0, keepdims=True)
        dcw_ref[1:2, :] += jnp.sum(dgc * g1[:t], axis=0, keepdims=True)
        dcw_ref[2:3, :] += jnp.sum(dgc * g[:t], axis=0, keepdims=True)
        dcb_ref[...] += jnp.sum(dgc, axis=0, keepdims=True)

    cur, prev, nxt, cws, cbs = _conv_specs(s, f, t)
    return pl.pallas_call(
        kern, grid=(4, nt), in_specs=[cur, prev, nxt, cur, nxt, cur, nxt, cws, cbs], out_specs=[cur, cur, cws, cbs],
        out_shape=[jax.ShapeDtypeStruct(gg.shape, BF16), jax.ShapeDtypeStruct(gg.shape, BF16),
                   jax.ShapeDtypeStruct(cw.shape, F32), jax.ShapeDtypeStruct(cb.shape, F32)],
        compiler_params=pltpu.CompilerParams(dimension_semantics=("parallel", "arbitrary"), vmem_limit_bytes=VMEM_LIMIT),
        name="ffn_conv_bwd")(gg, gg, gg, uu, uu, dact, dact, cw, cb)


def _rope_tables(pos):
    half = MLA_ROPE // 2
    inv = ROPE_THETA ** (-jnp.arange(half, dtype=F32) / half)
    ang = pos.astype(F32)[:, None] * inv
    cos, sin = jnp.cos(ang), jnp.sin(ang)
    s = pos.shape[0]
    z = lambda w: jnp.zeros((s, w), F32)
    c = jnp.concatenate([jnp.ones((s, MLA_NOPE), F32), cos, cos, jnp.ones((s, LANES - MLA_QK), F32)], axis=1)
    sa = jnp.concatenate([z(MLA_NOPE), -sin, z(half), z(LANES - MLA_QK)], axis=1)
    sb = jnp.concatenate([z(MLA_NOPE), z(half), sin, z(LANES - MLA_QK)], axis=1)
    return c, sa, sb


def _local_step(x, mem, pos, target, rep, early_shards, late_shards):
    g = {}
    c, sa, sb = _rope_tables(pos)

    xn, gathered = _norm_fwd_comm(x, rep["norm_mix"], _gather_plan(early_shards), "norm_mix_fwd_gather")
    w = _early_layout(dict(zip(EARLY, gathered, strict=True)), rep)

    def proj_fn(r, rows, k):
        la_ = _gate_fn(r[:, P_ALR:P_ALR + 128], k[0], k[1])
        return [r, la_, _rms(r[:, P_CQ:P_CQ + MLA_Q_RANK], k[2]), _rms(r[:, P_CKV:P_CKV + MLA_KV_RANK], k[3])], []

    proj, la, q_lat, kv_lat = _matmul(
        xn, w["in"], "nt", F32, "proj_fwd", epilogue=_Epilogue(
            proj_fn, [], [w["w2"], w["gate_b"], w["q_a_norm"], w["kv_a_norm"]],
            [(P_WIDTH, F32), (256, F32), (MLA_Q_RANK, BF16), (MLA_KV_RANK, BF16)], []))
    alr = _row(proj, 128, P_ALR // 128)
    kpe = _row(proj, 128, P_KPE // 128)
    og = _row(proj, 512, P_OG // 512)
    cq = _row(proj, 256, P_CQ // 256)
    ckv = _row(proj, 128, P_CKV // 128)

    o_gla, states = _gla_fwd(proj, la)

    q_up = _matmul(q_lat, w["uq"], "nn", F32, "mla_q_fwd")
    k_up = _matmul(kv_lat, w["k"], "nn", F32, "mla_k_fwd")
    v_mla = _matmul(kv_lat, w["v"], "nn", BF16, "mla_v_fwd")

    def qk_body(r, k):
        qs, ks = [], []
        for qh, kh in zip(_heads(r[0], MLA_HEADS), _heads(r[1], MLA_HEADS)):
            a, b = _qk_head(qh, kh, r[2], r[3], r[4], r[5], k[0], k[1])
            qs.append(a)
            ks.append(b)
        return [_cat(qs), _cat(ks)], []

    tabs = [_row(c), _row(sa), _row(sb)]
    q_r, k_r = _rows_call(qk_body, [_row(q_up), _row(k_up), kpe] + tabs, [w["q_norm"], w["k_norm"]],
                          [(1024, BF16), (1024, BF16)], name="mla_qk_fwd")
    o_mla, lse, gathered = _attn_fwd(q_r, k_r, v_mla, _gather_plan(late_shards))
    w.update(_late_layout(dict(zip(LATE, gathered, strict=True))))

    def mix_body(r, k):
        ys = [_mix_head(o, g_, k[0]) for o, g_ in zip(_heads(r[0], GLA_HEADS), _heads(r[1], GLA_HEADS))]
        return [_cat(ys + [r[2]])], []

    cat = _rows_call(mix_body, [_row(o_gla), og, _row(o_mla)], [w["gla_out_norm"]], [(1024, BF16)],
                     name="mix_fwd")[0]
    h1, hn = _matmul(cat, w["out"], "nn", F32, "out_fwd_norm", residual=x, epilogue=_norm_fwd_epilogue(w["norm_xa"]))
    mn = _norm_fwd(mem, w["norm_mem"], "norm_mem_fwd")
    xkv = _matmul(mn, w["xkv"], "nn", F32, "xa_kv_fwd")

    def xa_fn(r, rows, k):
        ks, vs = _heads(k[0], 2 * XA_HEADS)[:XA_HEADS], _heads(k[0], 2 * XA_HEADS)[XA_HEADS:]
        return [r, _cat([_xa_head(a, b, v_, k[1], k[2]) for a, b, v_ in zip(_heads(r, XA_HEADS), ks, vs)])], []

    xq, xo = _matmul(hn, w["xq"], "nn", F32, "xa_q_fwd_attn", epilogue=_Epilogue(
        xa_fn, [], [xkv, w["xa_q_norm"], w["xa_k_norm"]], [(512, F32), (512, BF16)], []))
    h2, fn = _matmul(xo, w["xo"], "nn", F32, "xa_o_fwd_norm", residual=h1, epilogue=_norm_fwd_epilogue(w["norm_ffn"]))
    gg = _matmul(fn, w["wg"], "nt", BF16, "ffn_gate_fwd", b_lead="p")
    uu = _matmul(fn, w["wu"], "nt", BF16, "ffn_up_fwd", b_lead="p")
    act = _conv_fwd(gg, uu, w["cw"], w["cb"])
    def loss_fn(y, rows, consts):
        err = y - rows[0]
        part = 0.5 * jnp.sum(jnp.sum(err * err, axis=1, keepdims=True) * (1.0 / D_MODEL), axis=0, keepdims=True)
        return [err * (1.0 / D_MODEL)], [jnp.broadcast_to(part, (1, LANES))]

    dy, loss = _matmul(act, w["wd"], "nn", F32, "ffn_down_fwd_loss", residual=h2, a_lead="k", b_lead="k",
                       epilogue=_Epilogue(loss_fn, [_row(target)], [], [(D_MODEL, F32)], [(1, LANES)]))

    g["ffn_w_down"] = _matmul(act, dy, "tn", BF16, "ffn_down_dw", a_lead="p")
    dact = _matmul(dy, w["wd"], "nt", BF16, "ffn_down_dx", b_lead="p")
    duu, dgg, g["ffn_conv_w"], g["ffn_conv_b"] = _conv_bwd(gg, uu, dact, w["cw"], w["cb"])
    g["ffn_w_gate"] = _matmul(dgg, fn, "tn", BF16, "ffn_gate_dw", a_lead="p")
    g["ffn_w_up"] = _matmul(duu, fn, "tn", BF16, "ffn_up_dw", a_lead="p")
    dh2, g["norm_ffn"] = _matmul(dgg, w["wg"], "nn", F32, "ffn_dx_norm_bwd", a_lead="k", b_lead="k", more=(duu, w["wu"]),
                                 epilogue=_norm_bwd_epilogue(h2, w["norm_ffn"], dy))

    g["xa_w_o"] = _matmul(xo, dh2, "tn", BF16, "xa_o_dw")
    def xa_bwd(dxo_, rows, k):
        kvh = _heads(k[0], 2 * XA_HEADS)
        dq_, dk_, dv_ = [], [], []
        dqn, dkn = 0.0, 0.0
        for h, (a, d_) in enumerate(zip(_heads(rows[0], XA_HEADS), _heads(dxo_, XA_HEADS))):
            _, vjp = jax.vjp(_xa_head, a, kvh[h], kvh[XA_HEADS + h], k[1], k[2])
            ga, gk, gv, gqn, gkn = vjp(d_)
            dq_.append(ga)
            dk_.append(gk)
            dv_.append(gv)
            dqn, dkn = dqn + gqn, dkn + gkn
        return [_cat(dq_)], [_cat(dk_ + dv_), dqn, dkn]

    dxq, dxkv, g["xa_q_norm"], g["xa_k_norm"] = _matmul(dh2, w["xo"], "nt", F32, "xa_o_dx_attn_bwd", epilogue=_Epilogue(
        xa_bwd, [_row(xq)], [xkv, w["xa_q_norm"], w["xa_k_norm"]], [(512, BF16)], [xkv.shape, (1, 128), (1, 128)]))
    g["xa_w_q"] = _matmul(hn, dxq, "tn", BF16, "xa_q_dw")
    dh1, g["norm_xa"] = _matmul(dxq, w["xq"], "nt", F32, "xa_q_dx_norm_bwd",
                                epilogue=_norm_bwd_epilogue(h1, w["norm_xa"], dh2))
    g["xa_w_kv"] = _matmul(mn, dxkv, "tn", BF16, "xa_kv_dw")
    dmn = _matmul(dxkv, w["xkv"], "nt", F32, "xa_kv_dx")
    _, g["norm_mem"] = _norm_bwd(mem, w["norm_mem"], dmn, dmn, "norm_mem_bwd")

    g["w_out"] = _matmul(cat, dh1, "tn", BF16, "out_dw")
    def mix_bwd(dcat_, rows, k):
        do_, dog_ = [], []
        dgn = 0.0
        for o, g_, d_ in zip(_heads(rows[0], GLA_HEADS), _heads(rows[1], GLA_HEADS), _heads(dcat_, GLA_HEADS)):
            _, vjp = jax.vjp(_mix_head, o, g_, k[0])
            a, b, gn_ = vjp(d_)
            do_.append(a)
            dog_.append(b)
            dgn = dgn + gn_
        return [_cat(do_), _cat(dog_), dcat_[:, 512:]], [dgn]

    do_gla, d_og, do_mla, g["gla_out_norm"] = _matmul(dh1, w["out"], "nt", F32, "out_dx_mix_bwd", epilogue=_Epilogue(
        mix_bwd, [_row(o_gla), og], [w["gla_out_norm"]], [(512, F32), (512, BF16), (512, F32)], [(1, 128)]))

    late_parts = _late_grad_shards(g)
    dq_r, dk_r, dv_mla, lands_late = _attn_bwd(q_r, k_r, v_mla, o_mla, lse, do_mla,
                                               _scatter_plan([late_parts[n] for n in LATE]))
    lands_late = dict(zip(LATE, lands_late, strict=True))

    def qk_bwd(r, k):
        dqs, dks = [], []
        dkpe, dqn, dkn = 0.0, 0.0, 0.0
        for qh, kh, dqh, dkh in zip(_heads(r[0], MLA_HEADS), _heads(r[1], MLA_HEADS), _heads(r[6], MLA_HEADS),
                                    _heads(r[7], MLA_HEADS)):
            _, vjp = jax.vjp(lambda a, b, e, f, h_: _qk_head(a, b, e, r[3], r[4], r[5], f, h_), qh, kh, r[2], k[0], k[1])
            ga, gb, ge, gf, gh = vjp((dqh, dkh))
            dqs.append(ga)
            dks.append(gb)
            dkpe, dqn, dkn = dkpe + ge, dqn + gf, dkn + gh
        return [_cat(dqs), _cat(dks), dkpe], [dqn, dkn]

    dq_up, dk_up, d_kpe, g["q_norm"], g["k_norm"] = _rows_call(
        qk_bwd, [_row(q_up), _row(k_up), kpe] + tabs + [_row(dq_r), _row(dk_r)], [w["q_norm"], w["k_norm"]],
        [(1024, BF16), (1024, BF16), (128, BF16)], [(1, 128), (1, 128)], name="mla_qk_bwd")
    g["uq"] = _matmul(q_lat, dq_up, "tn", BF16, "mla_q_dw")
    dq_lat = _matmul(dq_up, w["uq"], "nt", F32, "mla_q_dx")
    g["k"] = _matmul(kv_lat, dk_up, "tn", BF16, "mla_k_dw")
    g["v"] = _matmul(kv_lat, dv_mla, "tn", BF16, "mla_v_dw")
    dkv_lat = _matmul(dk_up, w["k"], "nt", F32, "mla_k_dx")
    dkv_lat = _matmul(dv_mla, w["v"], "nt", F32, "mla_v_dx", residual=dkv_lat)

    dgq, dgk, dla, dgv, _ = _gla_bwd(proj, la, states, do_gla, _Comm([], [], [], lambda *args: None, lambda *args: None))

    def dproj_body(r, k):
        alr_, cq_, ckv_, dla_, dq_lat_, dkv_lat_, dgq_, dgk_, dgv_, d_og_, d_kpe_ = r
        _, gate_vjp = jax.vjp(_gate_fn, alr_, k[0], k[1])
        d_alr, gw2, gb = gate_vjp(dla_)
        _, q_vjp = jax.vjp(_rms, cq_, k[2])
        _, kv_vjp = jax.vjp(_rms, ckv_, k[3])
        d_cq, gqa = q_vjp(dq_lat_)
        d_ckv, gkva = kv_vjp(dkv_lat_)
        pieces = [dgq_, dgk_, dgv_, d_og_, d_cq, d_ckv, d_kpe_, d_alr]
        return [_cat([x_.astype(BF16) for x_ in pieces])], [gw2, gb, gqa, gkva]

    dproj, g["w2"], g["gla_gate_b"], g["mla_q_a_norm"], g["mla_kv_a_norm"] = _rows_call(
        dproj_body, [alr, cq, ckv, _row(dla), _row(dq_lat), _row(dkv_lat), _row(dgq), _row(dgk), _row(dgv), _row(d_og),
                     _row(d_kpe)], [w["w2"], w["gate_b"], w["q_a_norm"], w["kv_a_norm"]], [(P_WIDTH, BF16)],
        [(128, 256), (1, 256), (1, 256), (1, 128)], name="proj_cotangent")
    g["in"] = _matmul(dproj, xn, "tn", BF16, "proj_dw")
    dx, g["norm_mix"] = _matmul(dproj, w["in"], "nn", F32, "proj_dx_norm_bwd",
                                epilogue=_norm_bwd_epilogue(x, w["norm_mix"], dh1))
    return loss[0, 0], dx, g, lands_late


def _join_shards(pieces, axis):
    if axis == 0:
        return pieces.reshape(-1, pieces.shape[2])
    return jnp.transpose(pieces, (1, 0, 2)).reshape(pieces.shape[1], -1)


def _split_shards(full, axis):
    r, c = full.shape
    if axis == 0:
        return full.reshape(4, r // 4, c)
    return jnp.transpose(full.reshape(r, 4, c // 4), (1, 0, 2))


def _early_layout(gath, rep):
    w_in = gath["w_in"].reshape(N_WIDTH, D_MODEL)
    z = lambda n: jnp.zeros((n, D_MODEL), w_in.dtype)
    seg = lambda lo, n: w_in[lo:lo + n]
    ukv = _join_shards(gath["mla_w_ukv"], 1).reshape(MLA_KV_RANK, MLA_HEADS, MLA_NOPE + MLA_V)
    w = {
        "in": jnp.concatenate([seg(N_GQ, 256), seg(N_GK, 256), seg(N_GV, 512), seg(N_OG, 512), seg(N_CQ, 256),
                               seg(N_CKV, 128), z(64), seg(N_KPE, 32), z(32), seg(N_ALR, 16), z(112)], axis=0),
        "uq": jnp.pad(_join_shards(gath["mla_w_uq"], 1).reshape(MLA_Q_RANK, MLA_HEADS, MLA_QK),
                      ((0, 0), (0, 0), (0, LANES - MLA_QK))).reshape(MLA_Q_RANK, MLA_HEADS * LANES),
        "k": jnp.pad(ukv[:, :, :MLA_NOPE], ((0, 0), (0, 0), (0, LANES - MLA_NOPE))).reshape(MLA_KV_RANK, -1),
        "v": ukv[:, :, MLA_NOPE:].reshape(MLA_KV_RANK, MLA_HEADS * MLA_V),
        "w2": jnp.pad(_join_shards(gath["gla_gate_w2"], 1), ((0, LANES - GLA_RANK), (0, 0))),
        "cb": rep["ffn_conv_b"].reshape(4, 1, D_FF // 4),
        "q_norm": jnp.pad(rep["mla_q_norm"], ((0, 0), (0, LANES - MLA_QK))),
        "k_norm": jnp.pad(rep["mla_k_norm"], ((0, 0), (0, LANES - MLA_QK))),
        "q_a_norm": rep["mla_q_a_norm"], "kv_a_norm": rep["mla_kv_a_norm"], "gate_b": rep["gla_gate_b"],
    }
    for n in ("norm_mix", "gla_out_norm", "norm_xa", "norm_mem", "xa_q_norm", "xa_k_norm", "norm_ffn"):
        w[n] = rep[n]
    return w


def _late_layout(gath):
    return {"out": _join_shards(gath["w_out"], 0), "xq": _join_shards(gath["xa_w_q"], 0),
            "xkv": _join_shards(gath["xa_w_kv"], 0), "xo": _join_shards(gath["xa_w_o"], 1),
            "wg": gath["ffn_w_gate"], "wu": gath["ffn_w_up"], "wd": gath["ffn_w_down"], "cw": gath["ffn_conv_w"]}


def _late_grad_shards(g):
    sh = {"w_out": _split_shards(g["w_out"], 0), "xa_w_q": _split_shards(g["xa_w_q"], 0),
          "xa_w_kv": _split_shards(g["xa_w_kv"], 0), "xa_w_o": _split_shards(g["xa_w_o"], 1),
          "ffn_w_gate": g["ffn_w_gate"], "ffn_w_up": g["ffn_w_up"], "ffn_conv_w": g["ffn_conv_w"],
          "ffn_w_down": g["ffn_w_down"]}
    return {n: v.astype(BF16) for n, v in sh.items()}


def _early_grad_shards(g):
    gi = g["in"]
    seg = lambda lo, n: gi[lo:lo + n]
    w_in = jnp.concatenate([seg(P_GQ, 256), seg(P_GK, 256), seg(P_GV, 512), seg(P_ALR, 16), seg(P_OG, 512),
                            seg(P_CQ, 256), seg(P_CKV, 128), seg(P_KPE + 64, 32)], axis=0)
    uq = g["uq"].reshape(MLA_Q_RANK, MLA_HEADS, LANES)[:, :, :MLA_QK].reshape(MLA_Q_RANK, -1)
    ukv = jnp.concatenate([g["k"].reshape(MLA_KV_RANK, MLA_HEADS, LANES)[:, :, :MLA_NOPE],
                           g["v"].reshape(MLA_KV_RANK, MLA_HEADS, MLA_V)], axis=2).reshape(MLA_KV_RANK, -1)
    sh = {"w_in": w_in.reshape(4, N_WIDTH // 4, D_MODEL), "gla_gate_w2": _split_shards(g["w2"][:GLA_RANK], 1),
          "mla_w_uq": _split_shards(uq, 1), "mla_w_ukv": _split_shards(ukv, 1)}
    sh = {n: v.astype(BF16) for n, v in sh.items()}
    rep = {n: g[n] for n in REPLICATED if n in g}
    rep["mla_q_norm"] = g["q_norm"][:, :MLA_QK]
    rep["mla_k_norm"] = g["k_norm"][:, :MLA_QK]
    rep["ffn_conv_b"] = g["ffn_conv_b"].reshape(1, D_FF)
    return sh, rep


SMALL_SHAPE = (8, 1024)


def _pack_small(vectors):
    flat = jnp.concatenate(vectors, axis=1)
    return jnp.pad(flat, ((0, 0), (0, SMALL_SHAPE[0] * SMALL_SHAPE[1] - flat.shape[1]))).reshape(SMALL_SHAPE)


def _unpack_small(buf, widths):
    flat = buf.reshape(1, -1)
    out, off = [], 0
    for wd in widths:
        out.append(flat[:, off:off + wd])
        off += wd
    return out


ANY = pl.BlockSpec(memory_space=pl.ANY)


def _place():
    x, y, c = lax.axis_index("x"), lax.axis_index("y"), lax.axis_index("c")
    chips = [(1 - x, y), (x, 1 - y), (1 - x, 1 - y)]
    return x, y, c, chips


class _Comm:
    def __init__(self, ins, out_shape, sems, start, finish, mid=None):
        self.ins, self.out_shape, self.sems = list(ins), list(out_shape), list(sems)
        self.start, self.finish, self.mid = start, finish, mid or (lambda *args: None)


def _run_comm(plan, name):
    ni, no = len(plan.ins), len(plan.out_shape)

    def body(*refs):
        ins, outs, sems = refs[:ni], refs[ni:ni + no], refs[ni + no:]
        place = _place()
        plan.start(place, ins, outs, sems)
        plan.mid(place, ins, outs, sems)
        plan.finish(place, ins, outs, sems)

    return pl.pallas_call(body, in_specs=[ANY] * ni, out_specs=[ANY] * no, out_shape=plan.out_shape,
                          scratch_shapes=plan.sems, name=name)(*plan.ins)


def _gather_plan(shards):
    n = len(shards)
    by_rows = [s.shape[0] % (2 * BF16_ROWS) == 0 for s in shards]
    by_cols = [not r and s.shape[1] % (2 * LANES) == 0 for r, s in zip(by_rows, shards)]
    split = [r or c for r, c in zip(by_rows, by_cols)]

    def rows(ref, t, c):
        if by_rows[t]:
            half = shards[t].shape[0] // 2
            return ref.at[pl.ds(pl.multiple_of(c * half, BF16_ROWS), half)]
        if by_cols[t]:
            half = shards[t].shape[1] // 2
            return ref.at[:, pl.ds(pl.multiple_of(c * half, LANES), half)]
        return ref

    def remote(src, dst, ss, rs, to):
        return pltpu.make_async_remote_copy(src_ref=src, dst_ref=dst, send_sem=ss, recv_sem=rs, device_id=to,
                                            device_id_type=MESH)

    def first_wave(place, ins, outs, sems):
        x, y, c, chips = place
        ici_s, ici_r, _, _, local = sems
        me = 2 * x + y
        own = [pltpu.make_async_copy(ins[t], outs[t].at[me], local.at[t]) for t in range(n)]
        push = [remote(rows(ins[t], t, c), rows(outs[t].at[me], t, c), ici_s.at[3 * t + j], ici_r.at[3 * t + j], (px, py, c))
                for t in range(n) for j, (px, py) in enumerate(chips)]
        return own, push

    def second_wave(place, ins, outs, sems, last):
        x, y, c, chips = place
        ici_s, ici_r, d2d_s, d2d_r, local = sems
        sib = (x, y, 1 - c)
        out = []
        for t in range(n):
            for j, (px, py) in enumerate(chips):
                block = outs[t].at[2 * px + py]
                got = rows(block, t, c)
                if split[t]:
                    hand = remote(got, got, d2d_s.at[3 * t + j], d2d_r.at[3 * t + j], sib)
                    theirs = rows(block, t, 1 - c)
                    other = (remote(theirs, theirs, local.at[0], d2d_r.at[3 * t + j], sib) if last else
                             remote(got, got, local.at[0], ici_r.at[3 * t + j], sib))
                    out.append((other, hand))
                elif last:
                    out.append((remote(got, got, local.at[0], ici_r.at[3 * t + j], sib), None))
        return out

    def start(place, ins, outs, sems):
        own, push = first_wave(place, ins, outs, sems)
        for cp in own + push:
            cp.start()

    def mid(place, ins, outs, sems):
        for arrival, hand in second_wave(place, ins, outs, sems, False):
            arrival.wait_recv()
            hand.start()

    def finish(place, ins, outs, sems):
        own, push = first_wave(place, ins, outs, sems)
        for arrival, hand in second_wave(place, ins, outs, sems, True):
            arrival.wait_recv()
            if hand is not None:
                hand.wait_send()
        for cp in push:
            cp.wait_send()
        for cp in own:
            cp.wait()

    dma = pltpu.SemaphoreType.DMA
    return _Comm(shards, [jax.ShapeDtypeStruct((4,) + s.shape, s.dtype) for s in shards],
                 [dma((3 * n,)), dma((3 * n,)), dma((3 * n,)), dma((3 * n,)), dma((n,))], start, finish, mid)


def _scatter_plan(parts, small=None):
    n = len(parts)
    ns = 0 if small is None else 1

    def unpack(place, ins, outs, sems):
        x, y, c, chips = place
        return x, y, c, chips, 2 * x + y, 4 * x + 2 * y + c, (x, y, 1 - c)

    def remote(src, dst, ss, rs, to):
        return pltpu.make_async_remote_copy(src_ref=src, dst_ref=dst, send_sem=ss, recv_sem=rs, device_id=to,
                                            device_id_type=MESH)

    def first_wave(place, ins, outs, sems):
        x, y, c, chips, me, dev, sib = unpack(place, ins, outs, sems)
        ici_s, ici_r, d2d_s, d2d_r, sm_s, sm_r, local = sems
        own, push = [], []
        if ns:
            own.append(pltpu.make_async_copy(ins[n], outs[n].at[dev], local.at[n]))
            for k in range(1, 8):
                px = (1 - x) if (k >> 2) & 1 else x
                py = (1 - y) if (k >> 1) & 1 else y
                pc = (1 - c) if k & 1 else c
                push.append(remote(ins[n], outs[n].at[dev], sm_s.at[k - 1], sm_r.at[k - 1], (px, py, pc)))
        for t in range(n):
            own.append(pltpu.make_async_copy(ins[t].at[me], outs[t].at[dev], local.at[t]))
            push.append(remote(ins[t].at[me], outs[t].at[dev], d2d_s.at[4 * t], d2d_r.at[4 * t], sib))
            for j, (px, py) in enumerate(chips):
                push.append(remote(ins[t].at[2 * px + py], outs[t].at[dev], ici_s.at[3 * t + j], ici_r.at[3 * t + j],
                                   (px, py, c)))
        return own, push

    def start(place, ins, outs, sems):
        own, push = first_wave(place, ins, outs, sems)
        for cp in own + push:
            cp.start()

    def landed(dst, rs, sems, sib):
        remote(dst, dst, sems[-1].at[0], rs, sib).wait_recv()

    def forwards(place, ins, outs, sems):
        x, y, c, chips, me, dev, sib = unpack(place, ins, outs, sems)
        d2d_s, d2d_r = sems[2], sems[3]
        slots = [(t, j, outs[t].at[4 * px + 2 * py + c]) for t in range(n) for j, (px, py) in enumerate(chips)]
        return [(t, j, slot, remote(slot, slot, d2d_s.at[4 * t + 1 + j], d2d_r.at[4 * t + 1 + j], sib))
                for t, j, slot in slots]

    def mid(place, ins, outs, sems):
        sib = unpack(place, ins, outs, sems)[-1]
        for t, j, slot, cp in forwards(place, ins, outs, sems):
            landed(slot, sems[1].at[3 * t + j], sems, sib)
            cp.start()

    def finish(place, ins, outs, sems):
        x, y, c, chips, me, dev, sib = unpack(place, ins, outs, sems)
        d2d_r, sm_r = sems[3], sems[5]
        own, push = first_wave(place, ins, outs, sems)
        push += [cp for _, _, _, cp in forwards(place, ins, outs, sems)]
        for t in range(n):
            landed(outs[t].at[4 * x + 2 * y + (1 - c)], d2d_r.at[4 * t], sems, sib)
            for j, (px, py) in enumerate(chips):
                landed(outs[t].at[4 * px + 2 * py + (1 - c)], d2d_r.at[4 * t + 1 + j], sems, sib)
        if ns:
            for k in range(1, 8):
                px = (1 - x) if (k >> 2) & 1 else x
                py = (1 - y) if (k >> 1) & 1 else y
                pc = (1 - c) if k & 1 else c
                landed(outs[n].at[4 * px + 2 * py + pc], sm_r.at[k - 1], sems, sib)
        for cp in push:
            cp.wait_send()
        for cp in own:
            cp.wait()

    dma = pltpu.SemaphoreType.DMA
    ins = list(parts) + ([small] if ns else [])
    out_shape = [jax.ShapeDtypeStruct((8,) + p.shape[1:], p.dtype) for p in parts]
    if ns:
        out_shape.append(jax.ShapeDtypeStruct((8,) + small.shape, small.dtype))
    return _Comm(ins, out_shape, [dma((3 * n,)), dma((3 * n,)), dma((4 * n,)), dma((4 * n,)), dma((7,)), dma((7,)),
                                  dma((n + 1,))], start, finish, mid)


ADAM_ROWS = 288


def _row_tile(r, cap):
    if r <= cap:
        return r
    return max((t for t in range(8, cap + 1, 8) if r % t == 0), default=r)


def _adamw_update(w, m, v, land):
    g = land[0].astype(F32)
    for i in range(1, 8):
        g = g + land[i].astype(F32)
    m_new = ADAM_B1 * m + (1.0 - ADAM_B1) * g
    v_new = ADAM_B2 * v + (1.0 - ADAM_B2) * (g * g)
    m_hat = m_new / (1.0 - ADAM_B1 ** ADAM_STEP)
    v_hat = v_new / (1.0 - ADAM_B2 ** ADAM_STEP)
    return g, -ADAM_LR * (m_hat / (jnp.sqrt(v_hat) + ADAM_EPS) + ADAM_WD * w), m_new, v_new


def _adamw(tensors, name, comm=None):
    k = len(tensors)
    r, c = tensors[0][0].shape
    t = _row_tile(r, ADAM_ROWS // k)
    tc = c if t < r or r <= ADAM_ROWS else 2 * LANES
    n = (r // t) * (c // tc)
    nci, nco, nsem = (len(comm.ins), len(comm.out_shape), len(comm.sems)) if comm else (0, 0, 0)

    def kern(*refs):
        ins, cins, outs, couts, csems = _split_refs(refs, (4 * k, nci, 4 * k, nco, nsem))
        if comm:
            place = _place()

            @pl.when(pl.program_id(0) == 0)
            def _():
                comm.start(place, cins, couts, csems)

        for i in range(k):
            w_ref, m_ref, v_ref, l_ref = ins[4 * i:4 * i + 4]
            res = _adamw_update(w_ref[...], m_ref[...], v_ref[...], l_ref)
            for ref, val in zip(outs[4 * i:4 * i + 4], res, strict=True):
                ref[...] = val
        if comm:
            @pl.when(pl.program_id(0) == n - 1)
            def _():
                comm.mid(place, cins, couts, csems)
                comm.finish(place, cins, couts, csems)

    where = (lambda i: (i, 0)) if tc == c else (lambda i: (0, i))
    spec = pl.BlockSpec((t, tc), where)
    lspec = pl.BlockSpec((8, t, tc), lambda i: (0,) + where(i))
    res = pl.pallas_call(
        kern, grid=(n,), in_specs=[spec, spec, spec, lspec] * k + [ANY] * nci, out_specs=[spec] * (4 * k) + [ANY] * nco,
        out_shape=[jax.ShapeDtypeStruct((r, c), F32)] * (4 * k) + (comm.out_shape if comm else []),
        scratch_shapes=comm.sems if comm else [],
        compiler_params=pltpu.CompilerParams(dimension_semantics=("arbitrary" if comm else "parallel",),
                                             vmem_limit_bytes=VMEM_LIMIT),
        name=name)(*[x for tens in tensors for x in tens], *(comm.ins if comm else []))
    return [res[4 * i:4 * i + 4] for i in range(k)], res[4 * k:]


def _step(a):
    def sq(n):
        v = a[n][0] if a[n].ndim == 3 else a[n]
        return v.T if n.removeprefix("m_").removeprefix("v_") in TRANSPOSED else v

    payload = lambda n: sq(n) if n in EXACT_GATHER else sq(n).astype(BF16)

    loss, dx, g, lands_late = _local_step(sq("x"), sq("mem"), a["positions"][0], sq("loss_target"),
                                          {n: a[n] for n in REPLICATED}, [payload(n) for n in EARLY],
                                          [payload(n) for n in LATE])

    sh, rep = _early_grad_shards(g)
    small = _pack_small([rep[n] for n in REPLICATED] + [loss.reshape(1, 1)])
    *lands_early, land_small = _run_comm(_scatter_plan([sh[n] for n in EARLY], small), "scatter_last")
    quad = lambda n, land: (sq(n), sq("m_" + n), sq("v_" + n), land)
    lands = dict(zip(EARLY, lands_early, strict=True)) | lands_late

    outs = {}
    kinds = ("grad_", "delta_", "new_m_", "new_v_")
    for n, _ in SHARDED:
        res = _adamw([quad(n, lands[n])], "adamw_" + n)[0][0]
        for kind, val in zip(kinds, res, strict=True):
            outs[kind + n] = (val.T if n in TRANSPOSED else val).reshape(a[n].shape)
    zero = jnp.zeros((1, 1), F32)
    packed = [_pack_small([a[p + n] for n in REPLICATED] + [zero]) for p in ("", "m_", "v_")]
    res = _adamw([(*packed, land_small)], "adamw_replicated")[0][0]
    widths = [a[n].shape[1] for n in REPLICATED] + [1]
    for kind, buf in zip(kinds, res, strict=True):
        *vals, total = _unpack_small(buf, widths)
        for n, val in zip(REPLICATED, vals, strict=True):
            outs[kind + n] = val
        if kind == "grad_":
            loss = total[0, 0]

    ordered = [outs[kind + n] for kind in kinds for n in WEIGHTS]
    return (loss, dx[None], *ordered)


def kernel(x, mem, positions, norm_mix, w_in, gla_gate_w2, gla_gate_b, gla_out_norm, mla_q_a_norm, mla_w_uq, mla_kv_a_norm, mla_w_ukv, mla_q_norm, mla_k_norm, w_out, norm_xa, norm_mem, xa_w_q, xa_w_kv, xa_q_norm, xa_k_norm, xa_w_o, norm_ffn, ffn_w_gate, ffn_w_up, ffn_conv_w, ffn_conv_b, ffn_w_down, loss_target, m_norm_mix, m_w_in, m_gla_gate_w2, m_gla_gate_b, m_gla_out_norm, m_mla_q_a_norm, m_mla_w_uq, m_mla_kv_a_norm, m_mla_w_ukv, m_mla_q_norm, m_mla_k_norm, m_w_out, m_norm_xa, m_norm_mem, m_xa_w_q, m_xa_w_kv, m_xa_q_norm, m_xa_k_norm, m_xa_w_o, m_norm_ffn, m_ffn_w_gate, m_ffn_w_up, m_ffn_conv_w, m_ffn_conv_b, m_ffn_w_down, v_norm_mix, v_w_in, v_gla_gate_w2, v_gla_gate_b, v_gla_out_norm, v_mla_q_a_norm, v_mla_w_uq, v_mla_kv_a_norm, v_mla_w_ukv, v_mla_q_norm, v_mla_k_norm, v_w_out, v_norm_xa, v_norm_mem, v_xa_w_q, v_xa_w_kv, v_xa_q_norm, v_xa_k_norm, v_xa_w_o, v_norm_ffn, v_ffn_w_gate, v_ffn_w_up, v_ffn_conv_w, v_ffn_conv_b, v_ffn_w_down):
    return _step(dict(locals()))
```

```python
import functools

import jax
import jax.numpy as jnp
import numpy as np
from jax import lax
from jax.experimental import pallas as pl
from jax.experimental.pallas import tpu as pltpu

F32, BF16 = jnp.float32, jnp.bfloat16
MESH = pl.DeviceIdType.MESH

D_MODEL = 1024
EPS = 1e-6
GLA_HEADS, GLA_DK, GLA_DV, GLA_RANK, GLA_CHUNK = 4, 64, 128, 16, 64
GLA_GATE_NORM = 16.0
MLA_HEADS, MLA_Q_RANK, MLA_KV_RANK, MLA_NOPE, MLA_ROPE, MLA_V = 8, 256, 128, 64, 32, 64
MLA_QK = MLA_NOPE + MLA_ROPE
ROPE_THETA = 10000.0
LOG2E, LN2 = 1.4426950408889634, 0.6931471805599453
XA_HEADS, XA_DIM = 4, 128
D_FF = 2816
ADAM_LR, ADAM_B1, ADAM_B2, ADAM_EPS, ADAM_WD, ADAM_STEP = 0.001, 0.9, 0.999, 1e-08, 0.01, 10

LANES = 128
BF16_ROWS = 16
VMEM_LIMIT = 56 * 1024 * 1024
MATMUL_VMEM = 44 * 1024 * 1024

P_GQ, P_GK, P_GV, P_OG, P_CQ, P_CKV, P_KPE, P_ALR, P_WIDTH = 0, 256, 512, 1024, 1536, 1792, 1920, 2048, 2176
N_GQ, N_GK, N_GV, N_ALR, N_OG, N_CQ, N_CKV, N_KPE, N_WIDTH = 0, 256, 512, 1024, 1040, 1552, 1808, 1936, 1968

SHARDED = (("w_in", 1), ("gla_gate_w2", 1), ("mla_w_uq", 1), ("mla_w_ukv", 1), ("w_out", 0), ("xa_w_q", 0),
           ("xa_w_kv", 0), ("xa_w_o", 1), ("ffn_w_gate", 1), ("ffn_w_up", 1), ("ffn_conv_w", 1), ("ffn_w_down", 0))
REPLICATED = ("norm_mix", "gla_gate_b", "gla_out_norm", "mla_q_a_norm", "mla_kv_a_norm", "mla_q_norm", "mla_k_norm",
              "norm_xa", "norm_mem", "xa_q_norm", "xa_k_norm", "norm_ffn", "ffn_conv_b")
EXACT_GATHER = ("gla_gate_w2", "ffn_conv_w")
TRANSPOSED = ("w_in", "ffn_w_gate", "ffn_w_up")
EARLY = ("w_in", "gla_gate_w2", "mla_w_uq", "mla_w_ukv")
LATE = tuple(n for n, _ in SHARDED if n not in EARLY)
WEIGHTS = ("norm_mix", "w_in", "gla_gate_w2", "gla_gate_b", "gla_out_norm", "mla_q_a_norm", "mla_w_uq",
           "mla_kv_a_norm", "mla_w_ukv", "mla_q_norm", "mla_k_norm", "w_out", "norm_xa", "norm_mem", "xa_w_q",
           "xa_w_kv", "xa_q_norm", "xa_k_norm", "xa_w_o", "norm_ffn", "ffn_w_gate", "ffn_w_up", "ffn_conv_w",
           "ffn_conv_b", "ffn_w_down")


_NN = ((1,), (0,))
_NT = ((1,), (1,))
_TN = ((0,), (0,))


def _dg(a, b, dims):
    return lax.dot_general(a.astype(BF16), b.astype(BF16), (dims, ((), ())), preferred_element_type=F32)


@jax.custom_vjp
def _dot_nn(a, b):
    return _dg(a, b, _NN)


_dot_nn.defvjp(lambda a, b: (_dg(a, b, _NN), (a, b)),
               lambda r, g: (_dg(g, r[1], _NT).astype(r[0].dtype), _dg(r[0], g, _TN).astype(r[1].dtype)))


@jax.custom_vjp
def _dot_nt(a, b):
    return _dg(a, b, _NT)


_dot_nt.defvjp(lambda a, b: (_dg(a, b, _NT), (a, b)),
               lambda r, g: (_dg(g, r[1], _NN).astype(r[0].dtype), _dg(g, r[0], _TN).astype(r[1].dtype)))


@jax.custom_vjp
def _dot_tn(a, b):
    return _dg(a, b, _TN)


_dot_tn.defvjp(lambda a, b: (_dg(a, b, _TN), (a, b)),
               lambda r, g: (_dg(r[1], g, _NT).astype(r[0].dtype), _dg(r[0], g, _NN).astype(r[1].dtype)))


def _rms(x, w, n=None):
    n = x.shape[-1] if n is None else n
    ms = jnp.sum(x * x, axis=-1, keepdims=True) * (1.0 / n)
    return x * lax.rsqrt(ms + EPS) * w


def _silu(x):
    return x * jax.nn.sigmoid(x)


def _log_sigmoid(x):
    return jnp.minimum(x, 0.0) - jnp.log(1.0 + jnp.exp(-jnp.abs(x)))


@jax.custom_vjp
def _rope(y, c, sa, sb):
    return y * c + pltpu.roll(y, LANES - 16, 1) * sa + pltpu.roll(y, 16, 1) * sb


def _rope_bwd(res, g):
    c, sa, sb = res
    gy = g * c + pltpu.roll(g * sa, 16, 1) + pltpu.roll(g * sb, LANES - 16, 1)
    return gy, jnp.zeros_like(c), jnp.zeros_like(sa), jnp.zeros_like(sb)


_rope.defvjp(lambda y, c, sa, sb: (_rope(y, c, sa, sb), (c, sa, sb)), _rope_bwd)


@jax.custom_vjp
def _cumsum_rows(x):
    n = x.shape[0]
    row = lax.broadcasted_iota(jnp.int32, x.shape, 0)
    k = 1
    while k < n:
        x = x + jnp.where(row >= k, pltpu.roll(x, k, 0), 0.0)
        k *= 2
    return x


def _cumsum_rows_bwd(_, g):
    n = g.shape[0]
    row = lax.broadcasted_iota(jnp.int32, g.shape, 0)
    k = 1
    while k < n:
        g = g + jnp.where(row < n - k, pltpu.roll(g, n - k, 0), 0.0)
        k *= 2
    return (g,)


_cumsum_rows.defvjp(lambda x: (_cumsum_rows(x), None), _cumsum_rows_bwd)


def _lane_mask(lo, hi):
    lane = lax.broadcasted_iota(jnp.int32, (1, LANES), 1)
    return ((lane >= lo) & (lane < hi)).astype(F32)


def _tile(n, t):
    t = min(n, t)
    assert n % t == 0, (n, t)
    return t


class _Epilogue:
    def __init__(self, fn, rows=(), consts=(), outs=(), accs=()):
        self.fn, self.rows, self.consts, self.outs, self.accs = fn, list(rows), list(consts), list(outs), list(accs)


def _matmul(a, b, mode, out_dtype, name, residual=None, a_lead=None, b_lead=None, more=None, epilogue=None):
    (a0, a1), (b0, b1) = a.shape[-2:], b.shape[-2:]
    if mode == "nn":
        m, k, k2, n = a0, a1, b0, b1
    elif mode == "nt":
        m, k, n, k2 = a0, a1, b0, b1
    else:
        k, m, k2, n = a0, a1, b0, b1
    assert k == k2, (a.shape, b.shape, mode)
    npar = 4 if "p" in (a_lead, b_lead) else 1
    nsum = 4 if "k" in (a_lead, b_lead) else 1
    pairs = [(a, b)] + ([more] if more else [])
    a_item, b_item, o_item = a.dtype.itemsize, b.dtype.itemsize, jnp.dtype(out_dtype).itemsize
    ep = epilogue
    row_extra = 4 if residual is not None else 0
    if ep:
        row_extra += (sum(r.dtype.itemsize * wd for r, wd, _ in ep.rows) + sum(jnp.dtype(d).itemsize * wd for wd, d in ep.outs)) / n

    def vmem_need(tm, tn, tk):
        need = 2 * (nsum if a_lead == "k" else 1) * tm * tk * a_item + 2 * (nsum if b_lead == "k" else 1) * tk * tn * b_item
        need *= len(pairs)
        need += (0 if ep else 2 * tm * tn * o_item) + tm * tn * 4 * (2 if tk < k else 1)
        need += tm * tk * 2 * (a_item == 4 or mode == "tn") + tk * tn * 2 * (b_item == 4)
        return need + int(2 * tm * tn * row_extra) + (3 * tm * tn * 4 if ep else 0)

    halvings = (4096, 2048, 1024, 512, 256, 128, 64, 32, 16, 8)
    if mode == "tn":
        tm = m if m <= 2304 else m // 2
        tn = n if tm * n <= 1024 * 2304 else n // 2
        tk = next((r for r in halvings if k % r == 0 and vmem_need(tm, tn, r) <= MATMUL_VMEM), k)
    else:
        tn, tk = n, k
        tm = next((r for r in halvings if m % r == 0 and vmem_need(r, tn, tk) <= MATMUL_VMEM), m)
    assert m % tm == 0 and n % tn == 0 and k % tk == 0
    assert ep is None or (tn == n and tk == k and npar == 1)
    nk = k // tk
    dims = {"nn": _NN, "nt": _NT, "tn": _TN}[mode]
    n_in = 2 * len(pairs) + (residual is not None)
    n_ep_in = len(ep.rows) + len(ep.consts) if ep else 0
    n_out = len(ep.outs) + len(ep.accs) if ep else 1

    def body(*refs):
        ab, rs, ep_in, outs, scratch = _split_refs(refs, (2 * len(pairs), n_in - 2 * len(pairs), n_ep_in, n_out, nk > 1))
        prod = None
        for a_ref, b_ref in zip(ab[0::2], ab[1::2]):
            for sh in range(nsum):
                term = _dg(a_ref[sh] if a_lead == "k" else a_ref[...], b_ref[sh] if b_lead == "k" else b_ref[...], dims)
                prod = term if prod is None else prod + term

        def finish(r):
            if rs:
                r = r + rs[0][...]
            if ep is None:
                outs[0][...] = r.astype(outs[0].dtype)
                return
            vals = [x[...] for x in ep_in]
            ro, ao = ep.fn(r, vals[:len(ep.rows)], vals[len(ep.rows):])
            for ref, val in zip(outs[:len(ep.outs)], ro, strict=True):
                ref[...] = val.astype(ref.dtype)
            if ep.accs:
                @pl.when(pl.program_id(0) == 0)
                def _():
                    for ref in outs[len(ep.outs):]:
                        ref[...] = jnp.zeros_like(ref)

                for ref, val in zip(outs[len(ep.outs):], ao, strict=True):
                    ref[...] += val

        if nk == 1:
            finish(prod)
            return
        acc = scratch[0]
        kk = pl.program_id(3)

        @pl.when(kk == 0)
        def _():
            acc[...] = prod

        @pl.when(kk > 0)
        def _():
            acc[...] += prod

        @pl.when(kk == nk - 1)
        def _():
            finish(acc[...])

    def spec(lead, blk, idx):
        if lead is None:
            return pl.BlockSpec(blk, lambda i, j, p, kk: idx(i, j, kk))
        if lead == "p":
            return pl.BlockSpec((None,) + blk, lambda i, j, p, kk: (p,) + idx(i, j, kk))
        return pl.BlockSpec((nsum,) + blk, lambda i, j, p, kk: (0,) + idx(i, j, kk))

    if mode == "nn":
        pair_specs = [spec(a_lead, (tm, tk), lambda i, j, kk: (i, kk)), spec(b_lead, (tk, tn), lambda i, j, kk: (kk, j))]
    elif mode == "nt":
        pair_specs = [spec(a_lead, (tm, tk), lambda i, j, kk: (i, kk)), spec(b_lead, (tn, tk), lambda i, j, kk: (j, kk))]
    else:
        pair_specs = [spec(a_lead, (tk, tm), lambda i, j, kk: (kk, i)), spec(b_lead, (tk, tn), lambda i, j, kk: (kk, j))]
    tile = spec(None, (tm, tn), lambda i, j, kk: (i, j))
    in_specs = pair_specs * len(pairs)
    args = [x for pair in pairs for x in pair]
    if residual is not None:
        assert npar == 1
        in_specs.append(tile)
        args.append(residual)
    if ep:
        in_specs += [pl.BlockSpec((tm, wd), functools.partial(lambda cb, i, j, p, kk: (i, cb), cb)) for _, wd, cb in ep.rows]
        in_specs += [pl.BlockSpec(c.shape, lambda i, j, p, kk: (0, 0)) for c in ep.consts]
        args += [r for r, _, _ in ep.rows] + ep.consts
        out_specs = [pl.BlockSpec((tm, wd), lambda i, j, p, kk: (i, 0)) for wd, _ in ep.outs]
        out_specs += [pl.BlockSpec(shape, lambda i, j, p, kk: (0, 0)) for shape in ep.accs]
        out_shape = [jax.ShapeDtypeStruct((m, wd), d) for wd, d in ep.outs] + [jax.ShapeDtypeStruct(sh, F32) for sh in ep.accs]
    else:
        out_specs = spec("p" if npar > 1 else None, (tm, tn), lambda i, j, kk: (i, j))
        out_shape = jax.ShapeDtypeStruct(((4,) if npar > 1 else ()) + (m, n), out_dtype)
    outer = "arbitrary" if ep and ep.accs else "parallel"
    return pl.pallas_call(
        body, grid=(m // tm, n // tn, npar, nk), in_specs=in_specs, out_specs=out_specs, out_shape=out_shape,
        scratch_shapes=[pltpu.VMEM((tm, tn), F32)] if nk > 1 else [],
        compiler_params=pltpu.CompilerParams(dimension_semantics=(outer, outer, outer, "arbitrary"),
                                             vmem_limit_bytes=VMEM_LIMIT),
        name=name)(*args)


def _row(a, width=None, col_block=0):
    return (a, a.shape[1] if width is None else width, col_block)


def _rows_call(body, rows, consts, outs, accs=(), *, name, tile=512):
    s = rows[0][0].shape[0]
    t = _tile(s, tile)
    nr, nc, no = len(rows), len(consts), len(outs)

    def kern(*refs):
        r = [x[...] for x in refs[:nr]]
        c = [x[...] for x in refs[nr:nr + nc]]
        o_refs = refs[nr + nc:nr + nc + no]
        a_refs = refs[nr + nc + no:]
        ro, ao = body(r, c)
        for ref, val in zip(o_refs, ro, strict=True):
            ref[...] = val.astype(ref.dtype)
        if a_refs:
            @pl.when(pl.program_id(0) == 0)
            def _():
                for ref in a_refs:
                    ref[...] = jnp.zeros_like(ref)

            for ref, val in zip(a_refs, ao, strict=True):
                ref[...] += val

    in_specs = [pl.BlockSpec((t, w), functools.partial(lambda cb, i: (i, cb), cb)) for (_, w, cb) in rows]
    in_specs += [pl.BlockSpec(c.shape, lambda i: (0, 0)) for c in consts]
    out_specs = [pl.BlockSpec((t, w), lambda i: (i, 0)) for (w, _) in outs]
    out_specs += [pl.BlockSpec(shape, lambda i: (0, 0)) for shape in accs]
    out_shape = [jax.ShapeDtypeStruct((s, w), dt) for (w, dt) in outs]
    out_shape += [jax.ShapeDtypeStruct(shape, F32) for shape in accs]
    return pl.pallas_call(
        kern, grid=(s // t,), in_specs=in_specs, out_specs=out_specs, out_shape=out_shape,
        compiler_params=pltpu.CompilerParams(dimension_semantics=("arbitrary" if accs else "parallel",),
                                             vmem_limit_bytes=VMEM_LIMIT),
        name=name)(*[r[0] for r in rows], *consts)


def _gla_chunk(q, k, la, v0, v1, s0, s1):
    c = q.shape[0]
    r = lax.broadcasted_iota(jnp.int32, (c, c), 0)
    cc = lax.broadcasted_iota(jnp.int32, (c, c), 1)
    tril = cc <= r
    cum = _cumsum_rows(la)
    cl = jnp.sum(la, axis=0, keepdims=True)
    qd = q * (GLA_DK ** -0.5) * jnp.exp(cum)
    ki = k * jnp.exp(-cum)
    ke = k * jnp.exp(cl - cum)
    dec = jnp.exp(cl)
    outs, news = [], []
    for h, (v, s) in enumerate(((v0, s0), (v1, s1))):
        mk = _lane_mask(GLA_DK * h, GLA_DK * (h + 1))
        qh = qd * mk
        att = jnp.where(tril, _dot_nt(qh, ki), 0.0)
        outs.append(_dot_nn(att, v) + _dot_nt(qh, s))
        news.append(s * dec + _dot_tn(v, ke * mk))
    return outs[0], outs[1], news[0], news[1]


def _gla_specs(tb, rev_nb=None):
    blk = (lambda b: b) if rev_nb is None else (lambda b: rev_nb - 1 - b)
    q = pl.BlockSpec((tb, 128), lambda p, b: (blk(b), P_GQ // 128 + p))
    k = pl.BlockSpec((tb, 128), lambda p, b: (blk(b), P_GK // 128 + p))
    la = pl.BlockSpec((tb, 128), lambda p, b: (blk(b), p))
    v = pl.BlockSpec((tb, 256), lambda p, b: (blk(b), P_GV // 256 + p))
    o = pl.BlockSpec((tb, 256), lambda p, b: (blk(b), p))
    st = pl.BlockSpec((tb // GLA_CHUNK, 2, 128, 128), lambda p, b: (blk(b), p, 0, 0))
    return q, k, la, v, o, st


def _gla_fwd(proj, la):
    s = proj.shape[0]
    tb = _tile(s, 512)
    nb, nch = s // tb, tb // GLA_CHUNK

    def kern(q_ref, k_ref, la_ref, v_ref, o_ref, st_ref, s_sc):
        @pl.when(pl.program_id(1) == 0)
        def _():
            s_sc[...] = jnp.zeros_like(s_sc)

        s0, s1 = s_sc[0], s_sc[1]
        for ci in range(nch):
            sl = slice(ci * GLA_CHUNK, (ci + 1) * GLA_CHUNK)
            st_ref[ci, 0] = s0
            st_ref[ci, 1] = s1
            o0, o1, s0, s1 = _gla_chunk(q_ref[sl, :], k_ref[sl, :], la_ref[sl, :], v_ref[sl, 0:128],
                                        v_ref[sl, 128:256], s0, s1)
            o_ref[sl, 0:128] = o0
            o_ref[sl, 128:256] = o1
        s_sc[0] = s0
        s_sc[1] = s1

    q, k, lasp, v, o, st = _gla_specs(tb)
    return pl.pallas_call(
        kern, grid=(2, nb), in_specs=[q, k, lasp, v], out_specs=[o, st],
        out_shape=[jax.ShapeDtypeStruct((s, 512), F32),
                   jax.ShapeDtypeStruct((s // GLA_CHUNK, GLA_HEADS, 128, 128), F32)],
        scratch_shapes=[pltpu.VMEM((2, 128, 128), F32)],
        compiler_params=pltpu.CompilerParams(dimension_semantics=("parallel", "arbitrary"),
                                             vmem_limit_bytes=VMEM_LIMIT),
        name="gla_fwd")(proj, proj, la, proj)


def _gla_bwd(proj, la, states, d_o, comm):
    s = proj.shape[0]
    tb = _tile(s, 512)
    nb, nch = s // tb, tb // GLA_CHUNK
    nci, nco = len(comm.ins), len(comm.out_shape)

    def kern(*refs):
        (q_ref, k_ref, la_ref, v_ref, do_ref, st_ref), cins, (dq_ref, dk_ref, dla_ref, dv_ref), couts, (ds_sc,), csems = \
            _split_refs(refs, (6, nci, 4, nco, 1, len(comm.sems)))
        place = _place()
        pair, blk = pl.program_id(0), pl.program_id(1)

        @pl.when((pair == 0) & (blk == 0))
        def _():
            comm.start(place, cins, couts, csems)

        @pl.when((pair == 1) & (blk == nb // 2))
        def _():
            comm.mid(place, cins, couts, csems)

        @pl.when(blk == 0)
        def _():
            ds_sc[...] = jnp.zeros_like(ds_sc)

        d0, d1 = ds_sc[0], ds_sc[1]
        for ci in reversed(range(nch)):
            sl = slice(ci * GLA_CHUNK, (ci + 1) * GLA_CHUNK)
            _, vjp = jax.vjp(_gla_chunk, q_ref[sl, :], k_ref[sl, :], la_ref[sl, :], v_ref[sl, 0:128],
                             v_ref[sl, 128:256], st_ref[ci, 0], st_ref[ci, 1])
            gq, gk, gla, gv0, gv1, d0, d1 = vjp((do_ref[sl, 0:128], do_ref[sl, 128:256], d0, d1))
            dq_ref[sl, :] = gq
            dk_ref[sl, :] = gk
            dla_ref[sl, :] = gla
            dv_ref[sl, 0:128] = gv0
            dv_ref[sl, 128:256] = gv1
        ds_sc[0] = d0
        ds_sc[1] = d1

        @pl.when((pair == 1) & (blk == nb - 1))
        def _():
            comm.finish(place, cins, couts, csems)

    q, k, lasp, v, o, st = _gla_specs(tb, rev_nb=nb)
    res = pl.pallas_call(
        kern, grid=(2, nb), in_specs=[q, k, lasp, v, o, st] + [ANY] * nci, out_specs=[lasp, lasp, lasp, o] + [ANY] * nco,
        out_shape=[jax.ShapeDtypeStruct((s, 256), F32), jax.ShapeDtypeStruct((s, 256), F32),
                   jax.ShapeDtypeStruct((s, 256), F32), jax.ShapeDtypeStruct((s, 512), F32)] + comm.out_shape,
        scratch_shapes=[pltpu.VMEM((2, 128, 128), F32)] + comm.sems,
        compiler_params=pltpu.CompilerParams(dimension_semantics=("arbitrary", "arbitrary"),
                                             vmem_limit_bytes=VMEM_LIMIT),
        name="gla_bwd")(proj, proj, la, proj, d_o, states, *comm.ins)
    return res[0], res[1], res[2], res[3], res[4:]


def _causal_keep(t, qi, ki):
    row = lax.broadcasted_iota(jnp.int32, (t, t), 0) + qi * t
    col = lax.broadcasted_iota(jnp.int32, (t, t), 1) + ki * t
    return col <= row


def _split_refs(refs, counts):
    out, off = [], 0
    for cnt in counts:
        out.append(refs[off:off + cnt])
        off += cnt
    return out


def _causal_blocks(n, key_major):
    pairs = ([(ki, qi) for ki in range(n) for qi in range(ki, n)] if key_major else
             [(ki, qi) for qi in range(n) for ki in range(qi + 1)])
    return np.array([ki for ki, _ in pairs], np.int32), np.array([qi for _, qi in pairs], np.int32)


def _attn_fwd(q, k, v, comm, tile=1024):
    s = q.shape[0]
    t = _tile(s, tile)
    n = s // t
    nci, nco = len(comm.ins), len(comm.out_shape)

    ki_tab, qi_tab = _causal_blocks(n, key_major=False)
    steps = len(ki_tab)

    def kern(ki_ref, qi_ref, *refs):
        (q_ref, k_ref, v_ref), cins, (o_ref, lse_ref), couts, (m_sc, l_sc, acc_sc), csems = _split_refs(
            refs, (3, nci, 2, nco, 3, len(comm.sems)))
        pair, step = pl.program_id(0), pl.program_id(1)
        qi, ki = qi_ref[step], ki_ref[step]
        place = _place()

        @pl.when((pair == 0) & (step == 0))
        def _():
            comm.start(place, cins, couts, csems)

        @pl.when((pair == MLA_HEADS // 2 - 1) & (step == 0))
        def _():
            comm.mid(place, cins, couts, csems)

        first = lax.broadcasted_iota(jnp.int32, (t, LANES), 1) < MLA_V

        @pl.when(ki == 0)
        def _():
            m_sc[...] = jnp.full_like(m_sc, -jnp.inf)
            l_sc[...] = jnp.zeros_like(l_sc)
            acc_sc[...] = jnp.zeros_like(acc_sc)

        def update(diagonal):
            keep = _causal_keep(t, 0, 0)
            alphas, pvs = [], []
            for h in range(2):
                sc = _dg(q_ref[:, 128 * h:128 * (h + 1)], k_ref[:, 128 * h:128 * (h + 1)], _NT)
                if diagonal:
                    sc = jnp.where(keep, sc, -jnp.inf)
                m_prev = m_sc[h]
                m_new = jnp.maximum(m_prev, jnp.max(sc, axis=1, keepdims=True))
                alpha = jnp.exp2(m_prev - m_new)
                p = jnp.exp2(sc - m_new[:, 0:1])
                l_sc[h] = alpha * l_sc[h] + jnp.sum(p, axis=1, keepdims=True)
                m_sc[h] = m_new
                alphas.append(alpha)
                pvs.append(_dg(p, v_ref[...], _NN))
            acc_sc[...] = acc_sc[...] * jnp.where(first, alphas[0], alphas[1]) + jnp.where(first, pvs[0], pvs[1])

        @pl.when(ki < qi)
        def _():
            update(False)

        @pl.when(ki == qi)
        def _():
            update(True)

        @pl.when(ki == qi)
        def _():
            l = jnp.where(first, l_sc[0], l_sc[1])
            m = jnp.where(first, m_sc[0], m_sc[1])
            o_ref[...] = acc_sc[...] / l
            lse_ref[...] = m + jnp.log2(l)

        @pl.when((pair == MLA_HEADS // 2 - 1) & (step == steps - 1))
        def _():
            comm.finish(place, cins, couts, csems)

    q_idx = lambda p, st, ki_r, qi_r: (qi_r[st], p)
    k_idx = lambda p, st, ki_r, qi_r: (ki_r[st], p)
    res = pl.pallas_call(
        kern, grid_spec=pltpu.PrefetchScalarGridSpec(
            num_scalar_prefetch=2, grid=(MLA_HEADS // 2, steps),
            in_specs=[pl.BlockSpec((t, 256), q_idx), pl.BlockSpec((t, 256), k_idx), pl.BlockSpec((t, 128), k_idx)]
            + [ANY] * nci,
            out_specs=[pl.BlockSpec((t, 128), q_idx), pl.BlockSpec((t, 128), q_idx)] + [ANY] * nco,
            scratch_shapes=[pltpu.VMEM((2, t, LANES), F32), pltpu.VMEM((2, t, LANES), F32),
                            pltpu.VMEM((t, LANES), F32)] + comm.sems),
        out_shape=[jax.ShapeDtypeStruct((s, 512), F32), jax.ShapeDtypeStruct((s, 512), F32)] + comm.out_shape,
        compiler_params=pltpu.CompilerParams(dimension_semantics=("arbitrary", "arbitrary"),
                                             vmem_limit_bytes=VMEM_LIMIT),
        name="mla_attn_fwd")(ki_tab, qi_tab, q, k, v, *comm.ins)
    return res[0], res[1], res[2:]


def _attn_bwd(q, k, v, o, lse, d_o, comm, tile=512):
    s = q.shape[0]
    t = _tile(s, tile)
    n = s // t
    nci, nco = len(comm.ins), len(comm.out_shape)

    ki_tab, qi_tab = _causal_blocks(n, key_major=True)
    steps = len(ki_tab)

    def kern(ki_ref, qi_ref, *refs):
        (q_ref, k_ref, v_ref, o_ref, lse_ref, do_ref), cins, (dq_ref, dk_ref, dv_ref), couts, (dk_sc, dv_sc), csems = \
            _split_refs(refs, (6, nci, 3, nco, 2, len(comm.sems)))
        pair, step = pl.program_id(0), pl.program_id(1)
        ki, qi = ki_ref[step], qi_ref[step]
        place = _place()

        @pl.when((pair == 0) & (step == 0))
        def _():
            comm.start(place, cins, couts, csems)

        @pl.when((pair == MLA_HEADS // 2 - 1) & (step == 0))
        def _():
            comm.mid(place, cins, couts, csems)

        @pl.when((ki == 0) & (qi == 0))
        def _():
            dq_ref[...] = jnp.zeros_like(dq_ref)

        @pl.when(qi == ki)
        def _():
            dk_sc[...] = jnp.zeros_like(dk_sc)
            dv_sc[...] = jnp.zeros_like(dv_sc)

        def update(diagonal):
            keep = _causal_keep(t, 0, 0)
            d_o = do_ref[...]
            prod = d_o * o_ref[...]
            rows = pl.ds(pl.multiple_of(qi * t, t), t)
            for h in range(2):
                hs = slice(128 * h, 128 * (h + 1))
                mk = _lane_mask(MLA_V * h, MLA_V * (h + 1))
                qh, kh = q_ref[:, hs], k_ref[:, hs]
                sc = _dg(qh, kh, _NT)
                if diagonal:
                    sc = jnp.where(keep, sc, -jnp.inf)
                p = jnp.exp2(sc - lse_ref[:, MLA_V * h:MLA_V * h + 1])
                doh = d_o * mk
                dp = _dg(doh * LN2, v_ref[...], _NT)
                delta = jnp.sum(prod * mk, axis=1, keepdims=True) * LN2
                ds = p * (dp - delta)
                dv_sc[...] += _dg(p, doh, _TN)
                dk_sc[:, hs] += _dg(ds, qh, _TN)
                dq_ref[rows, hs] += _dg(ds, kh, _NN)

        @pl.when(qi > ki)
        def _():
            update(False)

        @pl.when(qi == ki)
        def _():
            update(True)

        @pl.when(qi == n - 1)
        def _():
            dk_ref[...] = dk_sc[...]
            dv_ref[...] = dv_sc[...].astype(dv_ref.dtype)

        @pl.when((pair == MLA_HEADS // 2 - 1) & (step == steps - 1))
        def _():
            comm.finish(place, cins, couts, csems)

    q_idx = lambda p, st, ki_r, qi_r: (qi_r[st], p)
    k_idx = lambda p, st, ki_r, qi_r: (ki_r[st], p)
    res = pl.pallas_call(
        kern, grid_spec=pltpu.PrefetchScalarGridSpec(
            num_scalar_prefetch=2, grid=(MLA_HEADS // 2, steps),
            in_specs=[pl.BlockSpec((t, 256), q_idx), pl.BlockSpec((t, 256), k_idx), pl.BlockSpec((t, 128), k_idx),
                      pl.BlockSpec((t, 128), q_idx), pl.BlockSpec((t, 128), q_idx), pl.BlockSpec((t, 128), q_idx)]
            + [ANY] * nci,
            out_specs=[pl.BlockSpec((s, 256), lambda p, st, ki_r, qi_r: (0, p)), pl.BlockSpec((t, 256), k_idx),
                       pl.BlockSpec((t, 128), k_idx)] + [ANY] * nco,
            scratch_shapes=[pltpu.VMEM((t, 256), F32), pltpu.VMEM((t, 128), F32)] + comm.sems),
        out_shape=[jax.ShapeDtypeStruct((s, 1024), F32), jax.ShapeDtypeStruct((s, 1024), F32),
                   jax.ShapeDtypeStruct((s, 512), BF16)] + comm.out_shape,
        compiler_params=pltpu.CompilerParams(dimension_semantics=("arbitrary", "arbitrary"),
                                             vmem_limit_bytes=VMEM_LIMIT),
        name="mla_attn_bwd")(ki_tab, qi_tab, q, k, v, o, lse, d_o, *comm.ins)
    return res[0], res[1], res[2], res[3:]


def _gate_fn(alr, w2, b):
    return _log_sigmoid(_dot_nn(alr, w2) + b) * (1.0 / GLA_GATE_NORM)


def _qk_head(qh, kh, kpe, c, sa, sb, qn, kn):
    kfull = kh + kpe * _lane_mask(MLA_NOPE, MLA_QK)
    q_r = _rope(_rms(qh, qn, MLA_QK), c, sa, sb) * (MLA_QK ** -0.5 * LOG2E)
    k_r = _rope(_rms(kfull, kn, MLA_QK), c, sa, sb)
    return q_r, k_r


def _mix_head(o, og, gn):
    return _rms(o, gn) * _silu(og)


def _xa_head(xq, xk, xv, qn, kn):
    sc = _dot_nt(_rms(xq, qn), _rms(xk, kn)) * (XA_DIM ** -0.5)
    e = jnp.exp(sc - lax.stop_gradient(jnp.max(sc, axis=1, keepdims=True)))
    p = e / jnp.sum(e, axis=1, keepdims=True)
    return _dot_nn(p, xv)


def _heads(x, n):
    return [x[:, 128 * h:128 * (h + 1)] for h in range(n)]


def _cat(xs):
    return jnp.concatenate(xs, axis=1)


def _norm_fwd(x, w, name):
    return _rows_call(lambda r, c: ([_rms(r[0], c[0])], []), [_row(x)], [w], [(x.shape[1], BF16)], name=name)[0]


def _norm_fwd_epilogue(w):
    return _Epilogue(lambda h, rows, consts: ([h, _rms(h, consts[0])], []), [], [w], [(D_MODEL, F32), (D_MODEL, BF16)], [])


def _norm_bwd_epilogue(x, w, add):
    def fn(d_out, rows, consts):
        _, vjp = jax.vjp(_rms, rows[0], consts[0])
        dx, dw = vjp(d_out)
        return [dx + rows[1]], [dw]

    return _Epilogue(fn, [_row(x), _row(add)], [w], [(D_MODEL, F32)], [w.shape])


def _norm_fwd_comm(x, w, comm, name):
    s, d = x.shape
    t = _tile(s, 512)
    n = s // t
    nci, nco = len(comm.ins), len(comm.out_shape)

    def kern(*refs):
        (x_ref, w_ref), cins, (o_ref,), couts, csems = _split_refs(refs, (2, nci, 1, nco, len(comm.sems)))
        place = _place()

        @pl.when(pl.program_id(0) == 0)
        def _():
            comm.start(place, cins, couts, csems)

        o_ref[...] = _rms(x_ref[...], w_ref[...]).astype(o_ref.dtype)

        @pl.when(pl.program_id(0) == n - 1)
        def _():
            comm.mid(place, cins, couts, csems)
            comm.finish(place, cins, couts, csems)

    tile = pl.BlockSpec((t, d), lambda i: (i, 0))
    res = pl.pallas_call(
        kern, grid=(n,), in_specs=[tile, pl.BlockSpec(w.shape, lambda i: (0, 0))] + [ANY] * nci,
        out_specs=[tile] + [ANY] * nco, out_shape=[jax.ShapeDtypeStruct((s, d), BF16)] + comm.out_shape,
        scratch_shapes=comm.sems,
        compiler_params=pltpu.CompilerParams(dimension_semantics=("arbitrary",), vmem_limit_bytes=VMEM_LIMIT),
        name=name)(x, w, *comm.ins)
    return res[0], res[1:]


def _norm_bwd(x, w, d_out, add, name):
    def body(r, c):
        _, vjp = jax.vjp(_rms, r[0], c[0])
        dx, dw = vjp(r[1])
        return [dx + r[2]], [dw]

    return _rows_call(body, [_row(x), _row(d_out), _row(add)], [w], [(x.shape[1], F32)], [w.shape], name=name)


CONV_HALO = BF16_ROWS


def _conv_specs(s, f, t):
    n8 = t // CONV_HALO
    cur = pl.BlockSpec((None, t, f), lambda j, i: (j, i, 0))
    prev = pl.BlockSpec((None, CONV_HALO, f), lambda j, i: (j, jnp.maximum(i * n8 - 1, 0), 0))
    nxt = pl.BlockSpec((None, CONV_HALO, f), lambda j, i: (j, jnp.minimum((i + 1) * n8, s // CONV_HALO - 1), 0))
    cw = pl.BlockSpec((None, 3, f), lambda j, i: (j, 0, 0))
    cb = pl.BlockSpec((None, 1, f), lambda j, i: (j, 0, 0))
    return cur, prev, nxt, cw, cb


def _conv_taps(g, prev, first):
    ext = jnp.concatenate([jnp.where(first, 0.0, prev.astype(F32)), g], axis=0)
    return pltpu.roll(ext, 1, 0)[CONV_HALO:], pltpu.roll(ext, 2, 0)[CONV_HALO:]


def _conv_fwd(gg, uu, cw, cb):
    _, s, f = gg.shape
    t = _tile(s, 512)

    def kern(g_ref, gp_ref, u_ref, cw_ref, cb_ref, o_ref):
        g = g_ref[...].astype(F32)
        g1, g2 = _conv_taps(g, gp_ref[...], pl.program_id(1) == 0)
        w = cw_ref[...]
        gc = cb_ref[...] + w[0:1] * g2 + w[1:2] * g1 + w[2:3] * g
        o_ref[...] = (_silu(gc) * u_ref[...].astype(F32)).astype(o_ref.dtype)

    cur, prev, _, cws, cbs = _conv_specs(s, f, t)
    return pl.pallas_call(
        kern, grid=(4, s // t), in_specs=[cur, prev, cur, cws, cbs], out_specs=cur,
        out_shape=jax.ShapeDtypeStruct(gg.shape, BF16),
        compiler_params=pltpu.CompilerParams(dimension_semantics=("parallel", "parallel"), vmem_limit_bytes=VMEM_LIMIT),
        name="ffn_conv_fwd")(gg, gg, uu, cw, cb)


def _conv_bwd(gg, uu, dact, cw, cb):
    _, s, f = gg.shape
    t = _tile(s, 512)
    nt = s // t

    def kern(g_ref, gp_ref, gn_ref, u_ref, un_ref, da_ref, dan_ref, cw_ref, cb_ref, du_ref, dg_ref, dcw_ref, dcb_ref):
        i = pl.program_id(1)
        cat = lambda a_ref, b_ref: jnp.concatenate([a_ref[...].astype(F32), b_ref[...].astype(F32)], axis=0)
        g, u, da = cat(g_ref, gn_ref), cat(u_ref, un_ref), cat(da_ref, dan_ref)
        g1, g2 = _conv_taps(g, gp_ref[...], i == 0)
        w = cw_ref[...]
        gc = cb_ref[...] + w[0:1] * g2 + w[1:2] * g1 + w[2:3] * g
        sg = jax.nn.sigmoid(gc)
        du_ref[...] = (da[:t] * (gc[:t] * sg[:t])).astype(du_ref.dtype)
        row = lax.broadcasted_iota(jnp.int32, (t + CONV_HALO, 1), 0)
        dgc = jnp.where((row < t) | (i < nt - 1), da * u * (sg * (1.0 + gc * (1.0 - sg))), 0.0)
        up1 = pltpu.roll(dgc, t + CONV_HALO - 1, 0)[:t]
        up2 = pltpu.roll(dgc, t + CONV_HALO - 2, 0)[:t]
        dgc = dgc[:t]
        dg_ref[...] = (w[2:3] * dgc + w[1:2] * up1 + w[0:1] * up2).astype(dg_ref.dtype)

        @pl.when(i == 0)
        def _():
            dcw_ref[...] = jnp.zeros_like(dcw_ref)
            dcb_ref[...] = jnp.zeros_like(dcb_ref)

        dcw_ref[0:1, :] += jnp.sum(dgc * g2[:t], axis=0, keepdims=True)
        dcw_ref[1:2, :] += jnp.sum(dgc * g1[:t], axis=0, keepdims=True)
        dcw_ref[2:3, :] += jnp.sum(dgc * g[:t], axis=0, keepdims=True)
        dcb_ref[...] += jnp.sum(dgc, axis=0, keepdims=True)

    cur, prev, nxt, cws, cbs = _conv_specs(s, f, t)
    return pl.pallas_call(
        kern, grid=(4, nt), in_specs=[cur, prev, nxt, cur, nxt, cur, nxt, cws, cbs], out_specs=[cur, cur, cws, cbs],
        out_shape=[jax.ShapeDtypeStruct(gg.shape, BF16), jax.ShapeDtypeStruct(gg.shape, BF16),
                   jax.ShapeDtypeStruct(cw.shape, F32), jax.ShapeDtypeStruct(cb.shape, F32)],
        compiler_params=pltpu.CompilerParams(dimension_semantics=("parallel", "arbitrary"), vmem_limit_bytes=VMEM_LIMIT),
        name="ffn_conv_bwd")(gg, gg, gg, uu, uu, dact, dact, cw, cb)


def _rope_tables(pos):
    half = MLA_ROPE // 2
    inv = ROPE_THETA ** (-jnp.arange(half, dtype=F32) / half)
    ang = pos.astype(F32)[:, None] * inv
    cos, sin = jnp.cos(ang), jnp.sin(ang)
    s = pos.shape[0]
    z = lambda w: jnp.zeros((s, w), F32)
    c = jnp.concatenate([jnp.ones((s, MLA_NOPE), F32), cos, cos, jnp.ones((s, LANES - MLA_QK), F32)], axis=1)
    sa = jnp.concatenate([z(MLA_NOPE), -sin, z(half), z(LANES - MLA_QK)], axis=1)
    sb = jnp.concatenate([z(MLA_NOPE), z(half), sin, z(LANES - MLA_QK)], axis=1)
    return c, sa, sb


def _local_step(x, mem, pos, target, rep, early_shards, late_shards):
    g = {}
    c, sa, sb = _rope_tables(pos)

    xn, gathered = _norm_fwd_comm(x, rep["norm_mix"], _gather_plan(early_shards), "norm_mix_fwd_gather")
    w = _early_layout(dict(zip(EARLY, gathered, strict=True)), rep)

    def proj_fn(r, rows, k):
        la_ = _gate_fn(r[:, P_ALR:P_ALR + 128], k[0], k[1])
        return [r, la_, _rms(r[:, P_CQ:P_CQ + MLA_Q_RANK], k[2]), _rms(r[:, P_CKV:P_CKV + MLA_KV_RANK], k[3])], []

    proj, la, q_lat, kv_lat = _matmul(
        xn, w["in"], "nt", F32, "proj_fwd", epilogue=_Epilogue(
            proj_fn, [], [w["w2"], w["gate_b"], w["q_a_norm"], w["kv_a_norm"]],
            [(P_WIDTH, F32), (256, F32), (MLA_Q_RANK, BF16), (MLA_KV_RANK, BF16)], []))
    alr = _row(proj, 128, P_ALR // 128)
    kpe = _row(proj, 128, P_KPE // 128)
    og = _row(proj, 512, P_OG // 512)
    cq = _row(proj, 256, P_CQ // 256)
    ckv = _row(proj, 128, P_CKV // 128)

    o_gla, states = _gla_fwd(proj, la)

    def qk_body(r, k):
        q_up, k_up = _dg(r[0], k[0], _NN), _dg(r[1], k[1], _NN)
        qs, ks = [], []
        for qh, kh in zip(_heads(q_up, MLA_HEADS), _heads(k_up, MLA_HEADS)):
            a, b = _qk_head(qh, kh, r[2], r[3], r[4], r[5], k[3], k[4])
            qs.append(a)
            ks.append(b)
        return [_cat(qs), _cat(ks), _dg(r[1], k[2], _NN)], []

    tabs = [_row(c), _row(sa), _row(sb)]
    qk_consts = [w["uq"], w["k"], w["v"], w["q_norm"], w["k_norm"]]
    q_r, k_r, v_mla = _rows_call(qk_body, [_row(q_lat), _row(kv_lat), kpe] + tabs, qk_consts,
                                 [(1024, BF16), (1024, BF16), (512, BF16)], name="mla_qk_fwd")
    o_mla, lse, gathered = _attn_fwd(q_r, k_r, v_mla, _gather_plan(late_shards))
    w.update(_late_layout(dict(zip(LATE, gathered, strict=True))))

    def mix_body(r, k):
        ys = [_mix_head(o, g_, k[0]) for o, g_ in zip(_heads(r[0], GLA_HEADS), _heads(r[1], GLA_HEADS))]
        return [_cat(ys + [r[2]])], []

    cat = _rows_call(mix_body, [_row(o_gla), og, _row(o_mla)], [w["gla_out_norm"]], [(1024, BF16)],
                     name="mix_fwd")[0]
    h1, hn = _matmul(cat, w["out"], "nn", F32, "out_fwd_norm", residual=x, epilogue=_norm_fwd_epilogue(w["norm_xa"]))
    mn = _norm_fwd(mem, w["norm_mem"], "norm_mem_fwd")
    xkv = _matmul(mn, w["xkv"], "nn", F32, "xa_kv_fwd")

    def xa_fn(r, rows, k):
        ks, vs = _heads(k[0], 2 * XA_HEADS)[:XA_HEADS], _heads(k[0], 2 * XA_HEADS)[XA_HEADS:]
        return [r, _cat([_xa_head(a, b, v_, k[1], k[2]) for a, b, v_ in zip(_heads(r, XA_HEADS), ks, vs)])], []

    xq, xo = _matmul(hn, w["xq"], "nn", F32, "xa_q_fwd_attn", epilogue=_Epilogue(
        xa_fn, [], [xkv, w["xa_q_norm"], w["xa_k_norm"]], [(512, F32), (512, BF16)], []))
    h2, fn = _matmul(xo, w["xo"], "nn", F32, "xa_o_fwd_norm", residual=h1, epilogue=_norm_fwd_epilogue(w["norm_ffn"]))
    gg = _matmul(fn, w["wg"], "nt", BF16, "ffn_gate_fwd", b_lead="p")
    uu = _matmul(fn, w["wu"], "nt", BF16, "ffn_up_fwd", b_lead="p")
    act = _conv_fwd(gg, uu, w["cw"], w["cb"])
    def loss_fn(y, rows, consts):
        err = y - rows[0]
        part = 0.5 * jnp.sum(jnp.sum(err * err, axis=1, keepdims=True) * (1.0 / D_MODEL), axis=0, keepdims=True)
        return [err * (1.0 / D_MODEL)], [jnp.broadcast_to(part, (1, LANES))]

    dy, loss = _matmul(act, w["wd"], "nn", F32, "ffn_down_fwd_loss", residual=h2, a_lead="k", b_lead="k",
                       epilogue=_Epilogue(loss_fn, [_row(target)], [], [(D_MODEL, F32)], [(1, LANES)]))

    g["ffn_w_down"] = _matmul(act, dy, "tn", BF16, "ffn_down_dw", a_lead="p")
    dact = _matmul(dy, w["wd"], "nt", BF16, "ffn_down_dx", b_lead="p")
    duu, dgg, g["ffn_conv_w"], g["ffn_conv_b"] = _conv_bwd(gg, uu, dact, w["cw"], w["cb"])
    g["ffn_w_gate"] = _matmul(dgg, fn, "tn", BF16, "ffn_gate_dw", a_lead="p")
    g["ffn_w_up"] = _matmul(duu, fn, "tn", BF16, "ffn_up_dw", a_lead="p")
    dh2, g["norm_ffn"] = _matmul(dgg, w["wg"], "nn", F32, "ffn_dx_norm_bwd", a_lead="k", b_lead="k", more=(duu, w["wu"]),
                                 epilogue=_norm_bwd_epilogue(h2, w["norm_ffn"], dy))

    g["xa_w_o"] = _matmul(xo, dh2, "tn", BF16, "xa_o_dw")
    def xa_bwd(dxo_, rows, k):
        kvh = _heads(k[0], 2 * XA_HEADS)
        dq_, dk_, dv_ = [], [], []
        dqn, dkn = 0.0, 0.0
        for h, (a, d_) in enumerate(zip(_heads(rows[0], XA_HEADS), _heads(dxo_, XA_HEADS))):
            _, vjp = jax.vjp(_xa_head, a, kvh[h], kvh[XA_HEADS + h], k[1], k[2])
            ga, gk, gv, gqn, gkn = vjp(d_)
            dq_.append(ga)
            dk_.append(gk)
            dv_.append(gv)
            dqn, dkn = dqn + gqn, dkn + gkn
        return [_cat(dq_)], [_cat(dk_ + dv_), dqn, dkn]

    dxq, dxkv, g["xa_q_norm"], g["xa_k_norm"] = _matmul(dh2, w["xo"], "nt", F32, "xa_o_dx_attn_bwd", epilogue=_Epilogue(
        xa_bwd, [_row(xq)], [xkv, w["xa_q_norm"], w["xa_k_norm"]], [(512, BF16)], [xkv.shape, (1, 128), (1, 128)]))
    g["xa_w_q"] = _matmul(hn, dxq, "tn", BF16, "xa_q_dw")
    dh1, g["norm_xa"] = _matmul(dxq, w["xq"], "nt", F32, "xa_q_dx_norm_bwd",
                                epilogue=_norm_bwd_epilogue(h1, w["norm_xa"], dh2))
    g["xa_w_kv"] = _matmul(mn, dxkv, "tn", BF16, "xa_kv_dw")
    dmn = _matmul(dxkv, w["xkv"], "nt", F32, "xa_kv_dx")
    _, g["norm_mem"] = _norm_bwd(mem, w["norm_mem"], dmn, dmn, "norm_mem_bwd")

    g["w_out"] = _matmul(cat, dh1, "tn", BF16, "out_dw")
    def mix_bwd(dcat_, rows, k):
        do_, dog_ = [], []
        dgn = 0.0
        for o, g_, d_ in zip(_heads(rows[0], GLA_HEADS), _heads(rows[1], GLA_HEADS), _heads(dcat_, GLA_HEADS)):
            _, vjp = jax.vjp(_mix_head, o, g_, k[0])
            a, b, gn_ = vjp(d_)
            do_.append(a)
            dog_.append(b)
            dgn = dgn + gn_
        return [_cat(do_), _cat(dog_), dcat_[:, 512:]], [dgn]

    do_gla, d_og, do_mla, g["gla_out_norm"] = _matmul(dh1, w["out"], "nt", F32, "out_dx_mix_bwd", epilogue=_Epilogue(
        mix_bwd, [_row(o_gla), og], [w["gla_out_norm"]], [(512, F32), (512, BF16), (512, F32)], [(1, 128)]))

    late_parts = _late_grad_shards(g)
    dq_r, dk_r, dv_mla, lands_late = _attn_bwd(q_r, k_r, v_mla, o_mla, lse, do_mla,
                                               _scatter_plan([late_parts[n] for n in LATE]))
    lands_late = dict(zip(LATE, lands_late, strict=True))

    def qk_bwd(r, k):
        q_up, k_up = _dg(r[0], k[0], _NN), _dg(r[1], k[1], _NN)
        dqs, dks = [], []
        dkpe, dqn, dkn = 0.0, 0.0, 0.0
        for qh, kh, dqh, dkh in zip(_heads(q_up, MLA_HEADS), _heads(k_up, MLA_HEADS), _heads(r[6], MLA_HEADS),
                                    _heads(r[7], MLA_HEADS)):
            _, vjp = jax.vjp(lambda a, b, e, f, h_: _qk_head(a, b, e, r[3], r[4], r[5], f, h_), qh, kh, r[2], k[3], k[4])
            ga, gb, ge, gf, gh = vjp((dqh, dkh))
            dqs.append(ga)
            dks.append(gb)
            dkpe, dqn, dkn = dkpe + ge, dqn + gf, dkn + gh
        dq_up, dk_up, dv = _cat(dqs), _cat(dks), r[8]
        dq_lat_ = _dg(dq_up, k[0], _NT)
        dkv_lat_ = _dg(dk_up, k[1], _NT) + _dg(dv, k[2], _NT)
        return [dq_lat_, dkv_lat_, dkpe], [dqn, dkn, _dg(r[0], dq_up, _TN), _dg(r[1], dk_up, _TN), _dg(r[1], dv, _TN)]

    dq_lat, dkv_lat, d_kpe, g["q_norm"], g["k_norm"], g["uq"], g["k"], g["v"] = _rows_call(
        qk_bwd, [_row(q_lat), _row(kv_lat), kpe] + tabs + [_row(dq_r), _row(dk_r), _row(dv_mla)], qk_consts,
        [(MLA_Q_RANK, F32), (MLA_KV_RANK, F32), (128, BF16)],
        [(1, 128), (1, 128), w["uq"].shape, w["k"].shape, w["v"].shape], name="mla_qk_bwd")

    dgq, dgk, dla, dgv, _ = _gla_bwd(proj, la, states, do_gla, _Comm([], [], [], lambda *args: None, lambda *args: None))

    def dproj_body(r, k):
        alr_, cq_, ckv_, dla_, dq_lat_, dkv_lat_, dgq_, dgk_, dgv_, d_og_, d_kpe_ = r
        _, gate_vjp = jax.vjp(_gate_fn, alr_, k[0], k[1])
        d_alr, gw2, gb = gate_vjp(dla_)
        _, q_vjp = jax.vjp(_rms, cq_, k[2])
        _, kv_vjp = jax.vjp(_rms, ckv_, k[3])
        d_cq, gqa = q_vjp(dq_lat_)
        d_ckv, gkva = kv_vjp(dkv_lat_)
        pieces = [dgq_, dgk_, dgv_, d_og_, d_cq, d_ckv, d_kpe_, d_alr]
        return [_cat([x_.astype(BF16) for x_ in pieces])], [gw2, gb, gqa, gkva]

    dproj, g["w2"], g["gla_gate_b"], g["mla_q_a_norm"], g["mla_kv_a_norm"] = _rows_call(
        dproj_body, [alr, cq, ckv, _row(dla), _row(dq_lat), _row(dkv_lat), _row(dgq), _row(dgk), _row(dgv), _row(d_og),
                     _row(d_kpe)], [w["w2"], w["gate_b"], w["q_a_norm"], w["kv_a_norm"]], [(P_WIDTH, BF16)],
        [(128, 256), (1, 256), (1, 256), (1, 128)], name="proj_cotangent")
    g["in"] = _matmul(dproj, xn, "tn", BF16, "proj_dw")
    dx, g["norm_mix"] = _matmul(dproj, w["in"], "nn", F32, "proj_dx_norm_bwd",
                                epilogue=_norm_bwd_epilogue(x, w["norm_mix"], dh1))
    return loss[0, 0], dx, g, lands_late


def _join_shards(pieces, axis):
    if axis == 0:
        return pieces.reshape(-1, pieces.shape[2])
    return jnp.transpose(pieces, (1, 0, 2)).reshape(pieces.shape[1], -1)


def _split_shards(full, axis):
    r, c = full.shape
    if axis == 0:
        return full.reshape(4, r // 4, c)
    return jnp.transpose(full.reshape(r, 4, c // 4), (1, 0, 2))


def _early_layout(gath, rep):
    w_in = gath["w_in"].reshape(N_WIDTH, D_MODEL)
    z = lambda n: jnp.zeros((n, D_MODEL), w_in.dtype)
    seg = lambda lo, n: w_in[lo:lo + n]
    ukv = _join_shards(gath["mla_w_ukv"], 1).reshape(MLA_KV_RANK, MLA_HEADS, MLA_NOPE + MLA_V)
    w = {
        "in": jnp.concatenate([seg(N_GQ, 256), seg(N_GK, 256), seg(N_GV, 512), seg(N_OG, 512), seg(N_CQ, 256),
                               seg(N_CKV, 128), z(64), seg(N_KPE, 32), z(32), seg(N_ALR, 16), z(112)], axis=0),
        "uq": jnp.pad(_join_shards(gath["mla_w_uq"], 1).reshape(MLA_Q_RANK, MLA_HEADS, MLA_QK),
                      ((0, 0), (0, 0), (0, LANES - MLA_QK))).reshape(MLA_Q_RANK, MLA_HEADS * LANES),
        "k": jnp.pad(ukv[:, :, :MLA_NOPE], ((0, 0), (0, 0), (0, LANES - MLA_NOPE))).reshape(MLA_KV_RANK, -1),
        "v": ukv[:, :, MLA_NOPE:].reshape(MLA_KV_RANK, MLA_HEADS * MLA_V),
        "w2": jnp.pad(_join_shards(gath["gla_gate_w2"], 1), ((0, LANES - GLA_RANK), (0, 0))),
        "cb": rep["ffn_conv_b"].reshape(4, 1, D_FF // 4),
        "q_norm": jnp.pad(rep["mla_q_norm"], ((0, 0), (0, LANES - MLA_QK))),
        "k_norm": jnp.pad(rep["mla_k_norm"], ((0, 0), (0, LANES - MLA_QK))),
        "q_a_norm": rep["mla_q_a_norm"], "kv_a_norm": rep["mla_kv_a_norm"], "gate_b": rep["gla_gate_b"],
    }
    for n in ("norm_mix", "gla_out_norm", "norm_xa", "norm_mem", "xa_q_norm", "xa_k_norm", "norm_ffn"):
        w[n] = rep[n]
    return w


def _late_layout(gath):
    return {"out": _join_shards(gath["w_out"], 0), "xq": _join_shards(gath["xa_w_q"], 0),
            "xkv": _join_shards(gath["xa_w_kv"], 0), "xo": _join_shards(gath["xa_w_o"], 1),
            "wg": gath["ffn_w_gate"], "wu": gath["ffn_w_up"], "wd": gath["ffn_w_down"], "cw": gath["ffn_conv_w"]}


def _late_grad_shards(g):
    sh = {"w_out": _split_shards(g["w_out"], 0), "xa_w_q": _split_shards(g["xa_w_q"], 0),
          "xa_w_kv": _split_shards(g["xa_w_kv"], 0), "xa_w_o": _split_shards(g["xa_w_o"], 1),
          "ffn_w_gate": g["ffn_w_gate"], "ffn_w_up": g["ffn_w_up"], "ffn_conv_w": g["ffn_conv_w"],
          "ffn_w_down": g["ffn_w_down"]}
    return {n: v.astype(BF16) for n, v in sh.items()}


def _early_grad_shards(g):
    gi = g["in"]
    seg = lambda lo, n: gi[lo:lo + n]
    w_in = jnp.concatenate([seg(P_GQ, 256), seg(P_GK, 256), seg(P_GV, 512), seg(P_ALR, 16), seg(P_OG, 512),
                            seg(P_CQ, 256), seg(P_CKV, 128), seg(P_KPE + 64, 32)], axis=0)
    uq = g["uq"].reshape(MLA_Q_RANK, MLA_HEADS, LANES)[:, :, :MLA_QK].reshape(MLA_Q_RANK, -1)
    ukv = jnp.concatenate([g["k"].reshape(MLA_KV_RANK, MLA_HEADS, LANES)[:, :, :MLA_NOPE],
                           g["v"].reshape(MLA_KV_RANK, MLA_HEADS, MLA_V)], axis=2).reshape(MLA_KV_RANK, -1)
    sh = {"w_in": w_in.reshape(4, N_WIDTH // 4, D_MODEL), "gla_gate_w2": _split_shards(g["w2"][:GLA_RANK], 1),
          "mla_w_uq": _split_shards(uq, 1), "mla_w_ukv": _split_shards(ukv, 1)}
    sh = {n: v.astype(BF16) for n, v in sh.items()}
    rep = {n: g[n] for n in REPLICATED if n in g}
    rep["mla_q_norm"] = g["q_norm"][:, :MLA_QK]
    rep["mla_k_norm"] = g["k_norm"][:, :MLA_QK]
    rep["ffn_conv_b"] = g["ffn_conv_b"].reshape(1, D_FF)
    return sh, rep


SMALL_SHAPE = (8, 1024)


def _pack_small(vectors):
    flat = jnp.concatenate(vectors, axis=1)
    return jnp.pad(flat, ((0, 0), (0, SMALL_SHAPE[0] * SMALL_SHAPE[1] - flat.shape[1]))).reshape(SMALL_SHAPE)


def _unpack_small(buf, widths):
    flat = buf.reshape(1, -1)
    out, off = [], 0
    for wd in widths:
        out.append(flat[:, off:off + wd])
        off += wd
    return out


ANY = pl.BlockSpec(memory_space=pl.ANY)


def _place():
    x, y, c = lax.axis_index("x"), lax.axis_index("y"), lax.axis_index("c")
    chips = [(1 - x, y), (x, 1 - y), (1 - x, 1 - y)]
    return x, y, c, chips


class _Comm:
    def __init__(self, ins, out_shape, sems, start, finish, mid=None):
        self.ins, self.out_shape, self.sems = list(ins), list(out_shape), list(sems)
        self.start, self.finish, self.mid = start, finish, mid or (lambda *args: None)


def _run_comm(plan, name):
    ni, no = len(plan.ins), len(plan.out_shape)

    def body(*refs):
        ins, outs, sems = refs[:ni], refs[ni:ni + no], refs[ni + no:]
        place = _place()
        plan.start(place, ins, outs, sems)
        plan.mid(place, ins, outs, sems)
        plan.finish(place, ins, outs, sems)

    return pl.pallas_call(body, in_specs=[ANY] * ni, out_specs=[ANY] * no, out_shape=plan.out_shape,
                          scratch_shapes=plan.sems, name=name)(*plan.ins)


def _gather_plan(shards):
    n = len(shards)
    by_rows = [s.shape[0] % (2 * BF16_ROWS) == 0 for s in shards]
    by_cols = [not r and s.shape[1] % (2 * LANES) == 0 for r, s in zip(by_rows, shards)]
    split = [r or c for r, c in zip(by_rows, by_cols)]

    def rows(ref, t, c):
        if by_rows[t]:
            half = shards[t].shape[0] // 2
            return ref.at[pl.ds(pl.multiple_of(c * half, BF16_ROWS), half)]
        if by_cols[t]:
            half = shards[t].shape[1] // 2
            return ref.at[:, pl.ds(pl.multiple_of(c * half, LANES), half)]
        return ref

    def remote(src, dst, ss, rs, to):
        return pltpu.make_async_remote_copy(src_ref=src, dst_ref=dst, send_sem=ss, recv_sem=rs, device_id=to,
                                            device_id_type=MESH)

    def first_wave(place, ins, outs, sems):
        x, y, c, chips = place
        ici_s, ici_r, _, _, local = sems
        me = 2 * x + y
        own = [pltpu.make_async_copy(ins[t], outs[t].at[me], local.at[t]) for t in range(n)]
        push = [remote(rows(ins[t], t, c), rows(outs[t].at[me], t, c), ici_s.at[3 * t + j], ici_r.at[3 * t + j], (px, py, c))
                for t in range(n) for j, (px, py) in enumerate(chips)]
        return own, push

    def second_wave(place, ins, outs, sems, last):
        x, y, c, chips = place
        ici_s, ici_r, d2d_s, d2d_r, local = sems
        sib = (x, y, 1 - c)
        out = []
        for t in range(n):
            for j, (px, py) in enumerate(chips):
                block = outs[t].at[2 * px + py]
                got = rows(block, t, c)
                if split[t]:
                    hand = remote(got, got, d2d_s.at[3 * t + j], d2d_r.at[3 * t + j], sib)
                    theirs = rows(block, t, 1 - c)
                    other = (remote(theirs, theirs, local.at[0], d2d_r.at[3 * t + j], sib) if last else
                             remote(got, got, local.at[0], ici_r.at[3 * t + j], sib))
                    out.append((other, hand))
                elif last:
                    out.append((remote(got, got, local.at[0], ici_r.at[3 * t + j], sib), None))
        return out

    def start(place, ins, outs, sems):
        own, push = first_wave(place, ins, outs, sems)
        for cp in own + push:
            cp.start()

    def mid(place, ins, outs, sems):
        for arrival, hand in second_wave(place, ins, outs, sems, False):
            arrival.wait_recv()
            hand.start()

    def finish(place, ins, outs, sems):
        own, push = first_wave(place, ins, outs, sems)
        for arrival, hand in second_wave(place, ins, outs, sems, True):
            arrival.wait_recv()
            if hand is not None:
                hand.wait_send()
        for cp in push:
            cp.wait_send()
        for cp in own:
            cp.wait()

    dma = pltpu.SemaphoreType.DMA
    return _Comm(shards, [jax.ShapeDtypeStruct((4,) + s.shape, s.dtype) for s in shards],
                 [dma((3 * n,)), dma((3 * n,)), dma((3 * n,)), dma((3 * n,)), dma((n,))], start, finish, mid)


def _scatter_plan(parts, small=None):
    n = len(parts)
    ns = 0 if small is None else 1

    def unpack(place, ins, outs, sems):
        x, y, c, chips = place
        return x, y, c, chips, 2 * x + y, 4 * x + 2 * y + c, (x, y, 1 - c)

    def remote(src, dst, ss, rs, to):
        return pltpu.make_async_remote_copy(src_ref=src, dst_ref=dst, send_sem=ss, recv_sem=rs, device_id=to,
                                            device_id_type=MESH)

    def first_wave(place, ins, outs, sems):
        x, y, c, chips, me, dev, sib = unpack(place, ins, outs, sems)
        ici_s, ici_r, d2d_s, d2d_r, sm_s, sm_r, local = sems
        own, push = [], []
        if ns:
            own.append(pltpu.make_async_copy(ins[n], outs[n].at[dev], local.at[n]))
            for k in range(1, 8):
                px = (1 - x) if (k >> 2) & 1 else x
                py = (1 - y) if (k >> 1) & 1 else y
                pc = (1 - c) if k & 1 else c
                push.append(remote(ins[n], outs[n].at[dev], sm_s.at[k - 1], sm_r.at[k - 1], (px, py, pc)))
        for t in range(n):
            own.append(pltpu.make_async_copy(ins[t].at[me], outs[t].at[dev], local.at[t]))
            push.append(remote(ins[t].at[me], outs[t].at[dev], d2d_s.at[4 * t], d2d_r.at[4 * t], sib))
            for j, (px, py) in enumerate(chips):
                push.append(remote(ins[t].at[2 * px + py], outs[t].at[dev], ici_s.at[3 * t + j], ici_r.at[3 * t + j],
                                   (px, py, c)))
        return own, push

    def start(place, ins, outs, sems):
        own, push = first_wave(place, ins, outs, sems)
        for cp in own + push:
            cp.start()

    def landed(dst, rs, sems, sib):
        remote(dst, dst, sems[-1].at[0], rs, sib).wait_recv()

    def forwards(place, ins, outs, sems):
        x, y, c, chips, me, dev, sib = unpack(place, ins, outs, sems)
        d2d_s, d2d_r = sems[2], sems[3]
        slots = [(t, j, outs[t].at[4 * px + 2 * py + c]) for t in range(n) for j, (px, py) in enumerate(chips)]
        return [(t, j, slot, remote(slot, slot, d2d_s.at[4 * t + 1 + j], d2d_r.at[4 * t + 1 + j], sib))
                for t, j, slot in slots]

    def mid(place, ins, outs, sems):
        sib = unpack(place, ins, outs, sems)[-1]
        for t, j, slot, cp in forwards(place, ins, outs, sems):
            landed(slot, sems[1].at[3 * t + j], sems, sib)
            cp.start()

    def finish(place, ins, outs, sems):
        x, y, c, chips, me, dev, sib = unpack(place, ins, outs, sems)
        d2d_r, sm_r = sems[3], sems[5]
        own, push = first_wave(place, ins, outs, sems)
        push += [cp for _, _, _, cp in forwards(place, ins, outs, sems)]
        for t in range(n):
            landed(outs[t].at[4 * x + 2 * y + (1 - c)], d2d_r.at[4 * t], sems, sib)
            for j, (px, py) in enumerate(chips):
                landed(outs[t].at[4 * px + 2 * py + (1 - c)], d2d_r.at[4 * t + 1 + j], sems, sib)
        if ns:
            for k in range(1, 8):
                px = (1 - x) if (k >> 2) & 1 else x
                py = (1 - y) if (k >> 1) & 1 else y
                pc = (1 - c) if k & 1 else c
                landed(outs[n].at[4 * px + 2 * py + pc], sm_r.at[k - 1], sems, sib)
        for cp in push:
            cp.wait_send()
        for cp in own:
            cp.wait()

    dma = pltpu.SemaphoreType.DMA
    ins = list(parts) + ([small] if ns else [])
    out_shape = [jax.ShapeDtypeStruct((8,) + p.shape[1:], p.dtype) for p in parts]
    if ns:
        out_shape.append(jax.ShapeDtypeStruct((8,) + small.shape, small.dtype))
    return _Comm(ins, out_shape, [dma((3 * n,)), dma((3 * n,)), dma((4 * n,)), dma((4 * n,)), dma((7,)), dma((7,)),
                                  dma((n + 1,))], start, finish, mid)


ADAM_ROWS = 288


def _row_tile(r, cap):
    if r <= cap:
        return r
    return max((t for t in range(8, cap + 1, 8) if r % t == 0), default=r)


def _adamw_update(w, m, v, land):
    g = land[0].astype(F32)
    for i in range(1, 8):
        g = g + land[i].astype(F32)
    m_new = ADAM_B1 * m + (1.0 - ADAM_B1) * g
    v_new = ADAM_B2 * v + (1.0 - ADAM_B2) * (g * g)
    m_hat = m_new / (1.0 - ADAM_B1 ** ADAM_STEP)
    v_hat = v_new / (1.0 - ADAM_B2 ** ADAM_STEP)
    return g, -ADAM_LR * (m_hat / (jnp.sqrt(v_hat) + ADAM_EPS) + ADAM_WD * w), m_new, v_new


def _adamw(tensors, name, comm=None):
    k = len(tensors)
    r, c = tensors[0][0].shape
    t = _row_tile(r, ADAM_ROWS // k)
    tc = c if t < r or r <= ADAM_ROWS else 2 * LANES
    n = (r // t) * (c // tc)
    nci, nco, nsem = (len(comm.ins), len(comm.out_shape), len(comm.sems)) if comm else (0, 0, 0)

    def kern(*refs):
        ins, cins, outs, couts, csems = _split_refs(refs, (4 * k, nci, 4 * k, nco, nsem))
        if comm:
            place = _place()

            @pl.when(pl.program_id(0) == 0)
            def _():
                comm.start(place, cins, couts, csems)

        for i in range(k):
            w_ref, m_ref, v_ref, l_ref = ins[4 * i:4 * i + 4]
            res = _adamw_update(w_ref[...], m_ref[...], v_ref[...], l_ref)
            for ref, val in zip(outs[4 * i:4 * i + 4], res, strict=True):
                ref[...] = val
        if comm:
            @pl.when(pl.program_id(0) == n - 1)
            def _():
                comm.mid(place, cins, couts, csems)
                comm.finish(place, cins, couts, csems)

    where = (lambda i: (i, 0)) if tc == c else (lambda i: (0, i))
    spec = pl.BlockSpec((t, tc), where)
    lspec = pl.BlockSpec((8, t, tc), lambda i: (0,) + where(i))
    res = pl.pallas_call(
        kern, grid=(n,), in_specs=[spec, spec, spec, lspec] * k + [ANY] * nci, out_specs=[spec] * (4 * k) + [ANY] * nco,
        out_shape=[jax.ShapeDtypeStruct((r, c), F32)] * (4 * k) + (comm.out_shape if comm else []),
        scratch_shapes=comm.sems if comm else [],
        compiler_params=pltpu.CompilerParams(dimension_semantics=("arbitrary" if comm else "parallel",),
                                             vmem_limit_bytes=VMEM_LIMIT),
        name=name)(*[x for tens in tensors for x in tens], *(comm.ins if comm else []))
    return [res[4 * i:4 * i + 4] for i in range(k)], res[4 * k:]


def _step(a):
    def sq(n):
        v = a[n][0] if a[n].ndim == 3 else a[n]
        return v.T if n.removeprefix("m_").removeprefix("v_") in TRANSPOSED else v

    payload = lambda n: sq(n) if n in EXACT_GATHER else sq(n).astype(BF16)

    loss, dx, g, lands_late = _local_step(sq("x"), sq("mem"), a["positions"][0], sq("loss_target"),
                                          {n: a[n] for n in REPLICATED}, [payload(n) for n in EARLY],
                                          [payload(n) for n in LATE])

    sh, rep = _early_grad_shards(g)
    small = _pack_small([rep[n] for n in REPLICATED] + [loss.reshape(1, 1)])
    *lands_early, land_small = _run_comm(_scatter_plan([sh[n] for n in EARLY], small), "scatter_last")
    quad = lambda n, land: (sq(n), sq("m_" + n), sq("v_" + n), land)
    lands = dict(zip(EARLY, lands_early, strict=True)) | lands_late

    outs = {}
    kinds = ("grad_", "delta_", "new_m_", "new_v_")
    for n, _ in SHARDED:
        res = _adamw([quad(n, lands[n])], "adamw_" + n)[0][0]
        for kind, val in zip(kinds, res, strict=True):
            outs[kind + n] = (val.T if n in TRANSPOSED else val).reshape(a[n].shape)
    zero = jnp.zeros((1, 1), F32)
    packed = [_pack_small([a[p + n] for n in REPLICATED] + [zero]) for p in ("", "m_", "v_")]
    res = _adamw([(*packed, land_small)], "adamw_replicated")[0][0]
    widths = [a[n].shape[1] for n in REPLICATED] + [1]
    for kind, buf in zip(kinds, res, strict=True):
        *vals, total = _unpack_small(buf, widths)
        for n, val in zip(REPLICATED, vals, strict=True):
            outs[kind + n] = val
        if kind == "grad_":
            loss = total[0, 0]

    ordered = [outs[kind + n] for kind in kinds for n in WEIGHTS]
    return (loss, dx[None], *ordered)


def kernel(x, mem, positions, norm_mix, w_in, gla_gate_w2, gla_gate_b, gla_out_norm, mla_q_a_norm, mla_w_uq, mla_kv_a_norm, mla_w_ukv, mla_q_norm, mla_k_norm, w_out, norm_xa, norm_mem, xa_w_q, xa_w_kv, xa_q_norm, xa_k_norm, xa_w_o, norm_ffn, ffn_w_gate, ffn_w_up, ffn_conv_w, ffn_conv_b, ffn_w_down, loss_target, m_norm_mix, m_w_in, m_gla_gate_w2, m_gla_gate_b, m_gla_out_norm, m_mla_q_a_norm, m_mla_w_uq, m_mla_kv_a_norm, m_mla_w_ukv, m_mla_q_norm, m_mla_k_norm, m_w_out, m_norm_xa, m_norm_mem, m_xa_w_q, m_xa_w_kv, m_xa_q_norm, m_xa_k_norm, m_xa_w_o, m_norm_ffn, m_ffn_w_gate, m_ffn_w_up, m_ffn_conv_w, m_ffn_conv_b, m_ffn_w_down, v_norm_mix, v_w_in, v_gla_gate_w2, v_gla_gate_b, v_gla_out_norm, v_mla_q_a_norm, v_mla_w_uq, v_mla_kv_a_norm, v_mla_w_ukv, v_mla_q_norm, v_mla_k_norm, v_w_out, v_norm_xa, v_norm_mem, v_xa_w_q, v_xa_w_kv, v_xa_q_norm, v_xa_k_norm, v_xa_w_o, v_norm_ffn, v_ffn_w_gate, v_ffn_w_up, v_ffn_conv_w, v_ffn_conv_b, v_ffn_w_down):
    return _step(dict(locals()))
```

```python
import functools

import jax
import jax.numpy as jnp
import numpy as np
from jax import lax
from jax.experimental import pallas as pl
from jax.experimental.pallas import tpu as pltpu

F32, BF16 = jnp.float32, jnp.bfloat16
MESH = pl.DeviceIdType.MESH

D_MODEL = 1024
EPS = 1e-6
GLA_HEADS, GLA_DK, GLA_DV, GLA_RANK, GLA_CHUNK = 4, 64, 128, 16, 64
GLA_GATE_NORM = 16.0
MLA_HEADS, MLA_Q_RANK, MLA_KV_RANK, MLA_NOPE, MLA_ROPE, MLA_V = 8, 256, 128, 64, 32, 64
MLA_QK = MLA_NOPE + MLA_ROPE
ROPE_THETA = 10000.0
LOG2E, LN2 = 1.4426950408889634, 0.6931471805599453
XA_HEADS, XA_DIM = 4, 128
D_FF = 2816
ADAM_LR, ADAM_B1, ADAM_B2, ADAM_EPS, ADAM_WD, ADAM_STEP = 0.001, 0.9, 0.999, 1e-08, 0.01, 10

LANES = 128
BF16_ROWS = 16
VMEM_LIMIT = 56 * 1024 * 1024
MATMUL_VMEM = 44 * 1024 * 1024

P_GQ, P_GK, P_GV, P_OG, P_CQ, P_CKV, P_KPE, P_ALR, P_WIDTH = 0, 256, 512, 1024, 1536, 1792, 1920, 2048, 2176
N_GQ, N_GK, N_GV, N_ALR, N_OG, N_CQ, N_CKV, N_KPE, N_WIDTH = 0, 256, 512, 1024, 1040, 1552, 1808, 1936, 1968

SHARDED = (("w_in", 1), ("gla_gate_w2", 1), ("mla_w_uq", 1), ("mla_w_ukv", 1), ("w_out", 0), ("xa_w_q", 0),
           ("xa_w_kv", 0), ("xa_w_o", 1), ("ffn_w_gate", 1), ("ffn_w_up", 1), ("ffn_conv_w", 1), ("ffn_w_down", 0))
REPLICATED = ("norm_mix", "gla_gate_b", "gla_out_norm", "mla_q_a_norm", "mla_kv_a_norm", "mla_q_norm", "mla_k_norm",
              "norm_xa", "norm_mem", "xa_q_norm", "xa_k_norm", "norm_ffn", "ffn_conv_b")
EXACT_GATHER = ("gla_gate_w2", "ffn_conv_w")
TRANSPOSED = ("w_in", "ffn_w_gate", "ffn_w_up")
EARLY = ("w_in", "gla_gate_w2", "mla_w_uq", "mla_w_ukv")
LATE = tuple(n for n, _ in SHARDED if n not in EARLY)
WEIGHTS = ("norm_mix", "w_in", "gla_gate_w2", "gla_gate_b", "gla_out_norm", "mla_q_a_norm", "mla_w_uq",
           "mla_kv_a_norm", "mla_w_ukv", "mla_q_norm", "mla_k_norm", "w_out", "norm_xa", "norm_mem", "xa_w_q",
           "xa_w_kv", "xa_q_norm", "xa_k_norm", "xa_w_o", "norm_ffn", "ffn_w_gate", "ffn_w_up", "ffn_conv_w",
           "ffn_conv_b", "ffn_w_down")


_NN = ((1,), (0,))
_NT = ((1,), (1,))
_TN = ((0,), (0,))


def _dg(a, b, dims):
    return lax.dot_general(a.astype(BF16), b.astype(BF16), (dims, ((), ())), preferred_element_type=F32)


@jax.custom_vjp
def _dot_nn(a, b):
    return _dg(a, b, _NN)


_dot_nn.defvjp(lambda a, b: (_dg(a, b, _NN), (a, b)),
               lambda r, g: (_dg(g, r[1], _NT).astype(r[0].dtype), _dg(r[0], g, _TN).astype(r[1].dtype)))


@jax.custom_vjp
def _dot_nt(a, b):
    return _dg(a, b, _NT)


_dot_nt.defvjp(lambda a, b: (_dg(a, b, _NT), (a, b)),
               lambda r, g: (_dg(g, r[1], _NN).astype(r[0].dtype), _dg(g, r[0], _TN).astype(r[1].dtype)))


@jax.custom_vjp
def _dot_tn(a, b):
    return _dg(a, b, _TN)


_dot_tn.defvjp(lambda a, b: (_dg(a, b, _TN), (a, b)),
               lambda r, g: (_dg(r[1], g, _NT).astype(r[0].dtype), _dg(r[0], g, _NN).astype(r[1].dtype)))


def _rms(x, w, n=None):
    n = x.shape[-1] if n is None else n
    ms = jnp.sum(x * x, axis=-1, keepdims=True) * (1.0 / n)
    return x * lax.rsqrt(ms + EPS) * w


def _silu(x):
    return x * jax.nn.sigmoid(x)


def _log_sigmoid(x):
    return jnp.minimum(x, 0.0) - jnp.log(1.0 + jnp.exp(-jnp.abs(x)))


@jax.custom_vjp
def _rope(y, c, sa, sb):
    return y * c + pltpu.roll(y, LANES - 16, 1) * sa + pltpu.roll(y, 16, 1) * sb


def _rope_bwd(res, g):
    c, sa, sb = res
    gy = g * c + pltpu.roll(g * sa, 16, 1) + pltpu.roll(g * sb, LANES - 16, 1)
    return gy, jnp.zeros_like(c), jnp.zeros_like(sa), jnp.zeros_like(sb)


_rope.defvjp(lambda y, c, sa, sb: (_rope(y, c, sa, sb), (c, sa, sb)), _rope_bwd)


@jax.custom_vjp
def _cumsum_rows(x):
    n = x.shape[0]
    row = lax.broadcasted_iota(jnp.int32, x.shape, 0)
    k = 1
    while k < n:
        x = x + jnp.where(row >= k, pltpu.roll(x, k, 0), 0.0)
        k *= 2
    return x


def _cumsum_rows_bwd(_, g):
    n = g.shape[0]
    row = lax.broadcasted_iota(jnp.int32, g.shape, 0)
    k = 1
    while k < n:
        g = g + jnp.where(row < n - k, pltpu.roll(g, n - k, 0), 0.0)
        k *= 2
    return (g,)


_cumsum_rows.defvjp(lambda x: (_cumsum_rows(x), None), _cumsum_rows_bwd)


def _lane_mask(lo, hi):
    lane = lax.broadcasted_iota(jnp.int32, (1, LANES), 1)
    return ((lane >= lo) & (lane < hi)).astype(F32)


def _tile(n, t):
    t = min(n, t)
    assert n % t == 0, (n, t)
    return t


class _Epilogue:
    def __init__(self, fn, rows=(), consts=(), outs=(), accs=()):
        self.fn, self.rows, self.consts, self.outs, self.accs = fn, list(rows), list(consts), list(outs), list(accs)


def _matmul(a, b, mode, out_dtype, name, residual=None, a_lead=None, b_lead=None, more=None, epilogue=None):
    (a0, a1), (b0, b1) = a.shape[-2:], b.shape[-2:]
    if mode == "nn":
        m, k, k2, n = a0, a1, b0, b1
    elif mode == "nt":
        m, k, n, k2 = a0, a1, b0, b1
    else:
        k, m, k2, n = a0, a1, b0, b1
    assert k == k2, (a.shape, b.shape, mode)
    npar = 4 if "p" in (a_lead, b_lead) else 1
    nsum = 4 if "k" in (a_lead, b_lead) else 1
    pairs = [(a, b)] + ([more] if more else [])
    a_item, b_item, o_item = a.dtype.itemsize, b.dtype.itemsize, jnp.dtype(out_dtype).itemsize
    ep = epilogue
    row_extra = 4 if residual is not None else 0
    if ep:
        row_extra += (sum(r.dtype.itemsize * wd for r, wd, _ in ep.rows) + sum(jnp.dtype(d).itemsize * wd for wd, d in ep.outs)) / n

    def resident(lead, tiles):
        return lead != "p" and tiles == 1

    def vmem_need(tm, tn, tk):
        a_bufs = 1 if resident(a_lead, (m // tm) * (k // tk)) else 2
        b_bufs = 1 if resident(b_lead, (n // tn) * (k // tk)) else 2
        need = a_bufs * (nsum if a_lead == "k" else 1) * tm * tk * a_item + b_bufs * (nsum if b_lead == "k" else 1) * tk * tn * b_item
        need *= len(pairs)
        need += (0 if ep else 2 * tm * tn * o_item) + tm * tn * 4 * (2 if tk < k else 1)
        need += tm * tk * 2 * (a_item == 4 or mode == "tn") + tk * tn * 2 * (b_item == 4)
        return need + int(2 * tm * tn * row_extra) + (3 * tm * tn * 4 if ep else 0)

    halvings = (4096, 2048, 1024, 512, 256, 128, 64, 32, 16, 8)
    if mode == "tn":
        tm = m if m <= 2304 else m // 2
        tn = n if tm * n <= 1024 * 2304 else n // 2
        tk = next((r for r in halvings if k % r == 0 and vmem_need(tm, tn, r) <= MATMUL_VMEM), k)
    else:
        tn, tk = n, k
        tm = next((r for r in halvings if m % r == 0 and vmem_need(r, tn, tk) <= MATMUL_VMEM), m)
    assert m % tm == 0 and n % tn == 0 and k % tk == 0
    assert ep is None or (tn == n and tk == k and npar == 1)
    nk = k // tk
    dims = {"nn": _NN, "nt": _NT, "tn": _TN}[mode]
    n_in = 2 * len(pairs) + (residual is not None)
    n_ep_in = len(ep.rows) + len(ep.consts) if ep else 0
    n_out = len(ep.outs) + len(ep.accs) if ep else 1

    def body(*refs):
        ab, rs, ep_in, outs, scratch = _split_refs(refs, (2 * len(pairs), n_in - 2 * len(pairs), n_ep_in, n_out, nk > 1))
        prod = None
        for a_ref, b_ref in zip(ab[0::2], ab[1::2]):
            for sh in range(nsum):
                term = _dg(a_ref[sh] if a_lead == "k" else a_ref[...], b_ref[sh] if b_lead == "k" else b_ref[...], dims)
                prod = term if prod is None else prod + term

        def finish(r):
            if rs:
                r = r + rs[0][...]
            if ep is None:
                outs[0][...] = r.astype(outs[0].dtype)
                return
            vals = [x[...] for x in ep_in]
            ro, ao = ep.fn(r, vals[:len(ep.rows)], vals[len(ep.rows):])
            for ref, val in zip(outs[:len(ep.outs)], ro, strict=True):
                ref[...] = val.astype(ref.dtype)
            if ep.accs:
                @pl.when(pl.program_id(0) == 0)
                def _():
                    for ref in outs[len(ep.outs):]:
                        ref[...] = jnp.zeros_like(ref)

                for ref, val in zip(outs[len(ep.outs):], ao, strict=True):
                    ref[...] += val

        if nk == 1:
            finish(prod)
            return
        acc = scratch[0]
        kk = pl.program_id(3)

        @pl.when(kk == 0)
        def _():
            acc[...] = prod

        @pl.when(kk > 0)
        def _():
            acc[...] += prod

        @pl.when(kk == nk - 1)
        def _():
            finish(acc[...])

    def spec(lead, blk, idx, tiles=0):
        mode = {"pipeline_mode": pl.Buffered(1)} if resident(lead, tiles) else {}
        if lead is None:
            return pl.BlockSpec(blk, lambda i, j, p, kk: idx(i, j, kk), **mode)
        if lead == "p":
            return pl.BlockSpec((None,) + blk, lambda i, j, p, kk: (p,) + idx(i, j, kk))
        return pl.BlockSpec((nsum,) + blk, lambda i, j, p, kk: (0,) + idx(i, j, kk), **mode)

    a_tiles, b_tiles = (m // tm) * nk, (n // tn) * nk
    if mode == "nn":
        pair_specs = [spec(a_lead, (tm, tk), lambda i, j, kk: (i, kk), a_tiles),
                      spec(b_lead, (tk, tn), lambda i, j, kk: (kk, j), b_tiles)]
    elif mode == "nt":
        pair_specs = [spec(a_lead, (tm, tk), lambda i, j, kk: (i, kk), a_tiles),
                      spec(b_lead, (tn, tk), lambda i, j, kk: (j, kk), b_tiles)]
    else:
        pair_specs = [spec(a_lead, (tk, tm), lambda i, j, kk: (kk, i), a_tiles),
                      spec(b_lead, (tk, tn), lambda i, j, kk: (kk, j), b_tiles)]
    tile = spec(None, (tm, tn), lambda i, j, kk: (i, j))
    in_specs = pair_specs * len(pairs)
    args = [x for pair in pairs for x in pair]
    if residual is not None:
        assert npar == 1
        in_specs.append(tile)
        args.append(residual)
    if ep:
        in_specs += [pl.BlockSpec((tm, wd), functools.partial(lambda cb, i, j, p, kk: (i, cb), cb)) for _, wd, cb in ep.rows]
        in_specs += [pl.BlockSpec(c.shape, lambda i, j, p, kk: (0, 0)) for c in ep.consts]
        args += [r for r, _, _ in ep.rows] + ep.consts
        out_specs = [pl.BlockSpec((tm, wd), lambda i, j, p, kk: (i, 0)) for wd, _ in ep.outs]
        out_specs += [pl.BlockSpec(shape, lambda i, j, p, kk: (0, 0)) for shape in ep.accs]
        out_shape = [jax.ShapeDtypeStruct((m, wd), d) for wd, d in ep.outs] + [jax.ShapeDtypeStruct(sh, F32) for sh in ep.accs]
    else:
        out_specs = spec("p" if npar > 1 else None, (tm, tn), lambda i, j, kk: (i, j))
        out_shape = jax.ShapeDtypeStruct(((4,) if npar > 1 else ()) + (m, n), out_dtype)
    outer = "arbitrary" if ep and ep.accs else "parallel"
    return pl.pallas_call(
        body, grid=(m // tm, n // tn, npar, nk), in_specs=in_specs, out_specs=out_specs, out_shape=out_shape,
        scratch_shapes=[pltpu.VMEM((tm, tn), F32)] if nk > 1 else [],
        compiler_params=pltpu.CompilerParams(dimension_semantics=(outer, outer, outer, "arbitrary"),
                                             vmem_limit_bytes=VMEM_LIMIT),
        name=name)(*args)


def _row(a, width=None, col_block=0):
    return (a, a.shape[1] if width is None else width, col_block)


def _rows_call(body, rows, consts, outs, accs=(), *, name, tile=512):
    s = rows[0][0].shape[0]
    t = _tile(s, tile)
    nr, nc, no = len(rows), len(consts), len(outs)

    def kern(*refs):
        r = [x[...] for x in refs[:nr]]
        c = [x[...] for x in refs[nr:nr + nc]]
        o_refs = refs[nr + nc:nr + nc + no]
        a_refs = refs[nr + nc + no:]
        ro, ao = body(r, c)
        for ref, val in zip(o_refs, ro, strict=True):
            ref[...] = val.astype(ref.dtype)
        if a_refs:
            @pl.when(pl.program_id(0) == 0)
            def _():
                for ref in a_refs:
                    ref[...] = jnp.zeros_like(ref)

            for ref, val in zip(a_refs, ao, strict=True):
                ref[...] += val

    in_specs = [pl.BlockSpec((t, w), functools.partial(lambda cb, i: (i, cb), cb)) for (_, w, cb) in rows]
    in_specs += [pl.BlockSpec(c.shape, lambda i: (0, 0)) for c in consts]
    out_specs = [pl.BlockSpec((t, w), lambda i: (i, 0)) for (w, _) in outs]
    out_specs += [pl.BlockSpec(shape, lambda i: (0, 0)) for shape in accs]
    out_shape = [jax.ShapeDtypeStruct((s, w), dt) for (w, dt) in outs]
    out_shape += [jax.ShapeDtypeStruct(shape, F32) for shape in accs]
    return pl.pallas_call(
        kern, grid=(s // t,), in_specs=in_specs, out_specs=out_specs, out_shape=out_shape,
        compiler_params=pltpu.CompilerParams(dimension_semantics=("arbitrary" if accs else "parallel",),
                                             vmem_limit_bytes=VMEM_LIMIT),
        name=name)(*[r[0] for r in rows], *consts)


def _gla_chunk(q, k, la, v0, v1, s0, s1):
    c = q.shape[0]
    r = lax.broadcasted_iota(jnp.int32, (c, c), 0)
    cc = lax.broadcasted_iota(jnp.int32, (c, c), 1)
    tril = cc <= r
    cum = _cumsum_rows(la)
    cl = jnp.sum(la, axis=0, keepdims=True)
    qd = q * (GLA_DK ** -0.5) * jnp.exp(cum)
    ki = k * jnp.exp(-cum)
    ke = k * jnp.exp(cl - cum)
    dec = jnp.exp(cl)
    outs, news = [], []
    for h, (v, s) in enumerate(((v0, s0), (v1, s1))):
        mk = _lane_mask(GLA_DK * h, GLA_DK * (h + 1))
        qh = qd * mk
        att = jnp.where(tril, _dot_nt(qh, ki), 0.0)
        outs.append(_dot_nn(att, v) + _dot_nt(qh, s))
        news.append(s * dec + _dot_tn(v, ke * mk))
    return outs[0], outs[1], news[0], news[1]


def _gla_specs(tb, rev_nb=None):
    blk = (lambda b: b) if rev_nb is None else (lambda b: rev_nb - 1 - b)
    q = pl.BlockSpec((tb, 128), lambda p, b: (blk(b), P_GQ // 128 + p))
    k = pl.BlockSpec((tb, 128), lambda p, b: (blk(b), P_GK // 128 + p))
    la = pl.BlockSpec((tb, 128), lambda p, b: (blk(b), p))
    v = pl.BlockSpec((tb, 256), lambda p, b: (blk(b), P_GV // 256 + p))
    o = pl.BlockSpec((tb, 256), lambda p, b: (blk(b), p))
    st = pl.BlockSpec((tb // GLA_CHUNK, 2, 128, 128), lambda p, b: (blk(b), p, 0, 0))
    return q, k, la, v, o, st


def _gla_fwd(proj, la):
    s = proj.shape[0]
    tb = _tile(s, 512)
    nb, nch = s // tb, tb // GLA_CHUNK

    def kern(q_ref, k_ref, la_ref, v_ref, o_ref, st_ref, s_sc):
        @pl.when(pl.program_id(1) == 0)
        def _():
            s_sc[...] = jnp.zeros_like(s_sc)

        s0, s1 = s_sc[0], s_sc[1]
        for ci in range(nch):
            sl = slice(ci * GLA_CHUNK, (ci + 1) * GLA_CHUNK)
            st_ref[ci, 0] = s0
            st_ref[ci, 1] = s1
            o0, o1, s0, s1 = _gla_chunk(q_ref[sl, :], k_ref[sl, :], la_ref[sl, :], v_ref[sl, 0:128],
                                        v_ref[sl, 128:256], s0, s1)
            o_ref[sl, 0:128] = o0
            o_ref[sl, 128:256] = o1
        s_sc[0] = s0
        s_sc[1] = s1

    q, k, lasp, v, o, st = _gla_specs(tb)
    return pl.pallas_call(
        kern, grid=(2, nb), in_specs=[q, k, lasp, v], out_specs=[o, st],
        out_shape=[jax.ShapeDtypeStruct((s, 512), F32),
                   jax.ShapeDtypeStruct((s // GLA_CHUNK, GLA_HEADS, 128, 128), F32)],
        scratch_shapes=[pltpu.VMEM((2, 128, 128), F32)],
        compiler_params=pltpu.CompilerParams(dimension_semantics=("parallel", "arbitrary"),
                                             vmem_limit_bytes=VMEM_LIMIT),
        name="gla_fwd")(proj, proj, la, proj)


def _gla_bwd(proj, la, states, d_o, comm):
    s = proj.shape[0]
    tb = _tile(s, 512)
    nb, nch = s // tb, tb // GLA_CHUNK
    nci, nco = len(comm.ins), len(comm.out_shape)

    def kern(*refs):
        (q_ref, k_ref, la_ref, v_ref, do_ref, st_ref), cins, (dq_ref, dk_ref, dla_ref, dv_ref), couts, (ds_sc,), csems = \
            _split_refs(refs, (6, nci, 4, nco, 1, len(comm.sems)))
        place = _place()
        pair, blk = pl.program_id(0), pl.program_id(1)

        @pl.when((pair == 0) & (blk == 0))
        def _():
            comm.start(place, cins, couts, csems)

        @pl.when((pair == 1) & (blk == nb // 2))
        def _():
            comm.mid(place, cins, couts, csems)

        @pl.when(blk == 0)
        def _():
            ds_sc[...] = jnp.zeros_like(ds_sc)

        d0, d1 = ds_sc[0], ds_sc[1]
        for ci in reversed(range(nch)):
            sl = slice(ci * GLA_CHUNK, (ci + 1) * GLA_CHUNK)
            _, vjp = jax.vjp(_gla_chunk, q_ref[sl, :], k_ref[sl, :], la_ref[sl, :], v_ref[sl, 0:128],
                             v_ref[sl, 128:256], st_ref[ci, 0], st_ref[ci, 1])
            gq, gk, gla, gv0, gv1, d0, d1 = vjp((do_ref[sl, 0:128], do_ref[sl, 128:256], d0, d1))
            dq_ref[sl, :] = gq
            dk_ref[sl, :] = gk
            dla_ref[sl, :] = gla
            dv_ref[sl, 0:128] = gv0
            dv_ref[sl, 128:256] = gv1
        ds_sc[0] = d0
        ds_sc[1] = d1

        @pl.when((pair == 1) & (blk == nb - 1))
        def _():
            comm.finish(place, cins, couts, csems)

    q, k, lasp, v, o, st = _gla_specs(tb, rev_nb=nb)
    res = pl.pallas_call(
        kern, grid=(2, nb), in_specs=[q, k, lasp, v, o, st] + [ANY] * nci, out_specs=[lasp, lasp, lasp, o] + [ANY] * nco,
        out_shape=[jax.ShapeDtypeStruct((s, 256), F32), jax.ShapeDtypeStruct((s, 256), F32),
                   jax.ShapeDtypeStruct((s, 256), F32), jax.ShapeDtypeStruct((s, 512), F32)] + comm.out_shape,
        scratch_shapes=[pltpu.VMEM((2, 128, 128), F32)] + comm.sems,
        compiler_params=pltpu.CompilerParams(dimension_semantics=("arbitrary", "arbitrary"),
                                             vmem_limit_bytes=VMEM_LIMIT),
        name="gla_bwd")(proj, proj, la, proj, d_o, states, *comm.ins)
    return res[0], res[1], res[2], res[3], res[4:]


def _causal_keep(t, qi, ki):
    row = lax.broadcasted_iota(jnp.int32, (t, t), 0) + qi * t
    col = lax.broadcasted_iota(jnp.int32, (t, t), 1) + ki * t
    return col <= row


def _split_refs(refs, counts):
    out, off = [], 0
    for cnt in counts:
        out.append(refs[off:off + cnt])
        off += cnt
    return out


def _causal_blocks(n, key_major):
    pairs = ([(ki, qi) for ki in range(n) for qi in range(ki, n)] if key_major else
             [(ki, qi) for qi in range(n) for ki in range(qi + 1)])
    return np.array([ki for ki, _ in pairs], np.int32), np.array([qi for _, qi in pairs], np.int32)


def _attn_fwd(q, k, v, comm, tile=1024):
    s = q.shape[0]
    t = _tile(s, tile)
    n = s // t
    nci, nco = len(comm.ins), len(comm.out_shape)

    ki_tab, qi_tab = _causal_blocks(n, key_major=False)
    steps = len(ki_tab)

    def kern(ki_ref, qi_ref, *refs):
        (q_ref, k_ref, v_ref), cins, (o_ref, lse_ref), couts, (m_sc, l_sc, acc_sc), csems = _split_refs(
            refs, (3, nci, 2, nco, 3, len(comm.sems)))
        pair, step = pl.program_id(0), pl.program_id(1)
        qi, ki = qi_ref[step], ki_ref[step]
        place = _place()

        @pl.when((pair == 0) & (step == 0))
        def _():
            comm.start(place, cins, couts, csems)

        @pl.when((pair == MLA_HEADS // 2 - 1) & (step == 0))
        def _():
            comm.mid(place, cins, couts, csems)

        first = lax.broadcasted_iota(jnp.int32, (t, LANES), 1) < MLA_V

        @pl.when(ki == 0)
        def _():
            m_sc[...] = jnp.full_like(m_sc, -jnp.inf)
            l_sc[...] = jnp.zeros_like(l_sc)
            acc_sc[...] = jnp.zeros_like(acc_sc)

        def update(diagonal):
            keep = _causal_keep(t, 0, 0)
            alphas, pvs = [], []
            for h in range(2):
                sc = _dg(q_ref[:, 128 * h:128 * (h + 1)], k_ref[:, 128 * h:128 * (h + 1)], _NT)
                if diagonal:
                    sc = jnp.where(keep, sc, -jnp.inf)
                m_prev = m_sc[h]
                m_new = jnp.maximum(m_prev, jnp.max(sc, axis=1, keepdims=True))
                alpha = jnp.exp2(m_prev - m_new)
                p = jnp.exp2(sc - m_new[:, 0:1])
                l_sc[h] = alpha * l_sc[h] + jnp.sum(p, axis=1, keepdims=True)
                m_sc[h] = m_new
                alphas.append(alpha)
                pvs.append(_dg(p, v_ref[...], _NN))
            acc_sc[...] = acc_sc[...] * jnp.where(first, alphas[0], alphas[1]) + jnp.where(first, pvs[0], pvs[1])

        @pl.when(ki < qi)
        def _():
            update(False)

        @pl.when(ki == qi)
        def _():
            update(True)

        @pl.when(ki == qi)
        def _():
            l = jnp.where(first, l_sc[0], l_sc[1])
            m = jnp.where(first, m_sc[0], m_sc[1])
            o_ref[...] = acc_sc[...] / l
            lse_ref[...] = m + jnp.log2(l)

        @pl.when((pair == MLA_HEADS // 2 - 1) & (step == steps - 1))
        def _():
            comm.finish(place, cins, couts, csems)

    q_idx = lambda p, st, ki_r, qi_r: (qi_r[st], p)
    k_idx = lambda p, st, ki_r, qi_r: (ki_r[st], p)
    res = pl.pallas_call(
        kern, grid_spec=pltpu.PrefetchScalarGridSpec(
            num_scalar_prefetch=2, grid=(MLA_HEADS // 2, steps),
            in_specs=[pl.BlockSpec((t, 256), q_idx), pl.BlockSpec((t, 256), k_idx), pl.BlockSpec((t, 128), k_idx)]
            + [ANY] * nci,
            out_specs=[pl.BlockSpec((t, 128), q_idx), pl.BlockSpec((t, 128), q_idx)] + [ANY] * nco,
            scratch_shapes=[pltpu.VMEM((2, t, LANES), F32), pltpu.VMEM((2, t, LANES), F32),
                            pltpu.VMEM((t, LANES), F32)] + comm.sems),
        out_shape=[jax.ShapeDtypeStruct((s, 512), F32), jax.ShapeDtypeStruct((s, 512), F32)] + comm.out_shape,
        compiler_params=pltpu.CompilerParams(dimension_semantics=("arbitrary", "arbitrary"),
                                             vmem_limit_bytes=VMEM_LIMIT),
        name="mla_attn_fwd")(ki_tab, qi_tab, q, k, v, *comm.ins)
    return res[0], res[1], res[2:]


def _attn_bwd(q, k, v, o, lse, d_o, comm, tile=512):
    s = q.shape[0]
    t = _tile(s, tile)
    n = s // t
    nci, nco = len(comm.ins), len(comm.out_shape)

    ki_tab, qi_tab = _causal_blocks(n, key_major=True)
    steps = len(ki_tab)

    def kern(ki_ref, qi_ref, *refs):
        (q_ref, k_ref, v_ref, o_ref, lse_ref, do_ref), cins, (dq_ref, dk_ref, dv_ref), couts, (dk_sc, dv_sc), csems = \
            _split_refs(refs, (6, nci, 3, nco, 2, len(comm.sems)))
        pair, step = pl.program_id(0), pl.program_id(1)
        ki, qi = ki_ref[step], qi_ref[step]
        place = _place()

        @pl.when((pair == 0) & (step == 0))
        def _():
            comm.start(place, cins, couts, csems)

        @pl.when((pair == MLA_HEADS // 2 - 1) & (step == 0))
        def _():
            comm.mid(place, cins, couts, csems)

        @pl.when((ki == 0) & (qi == 0))
        def _():
            dq_ref[...] = jnp.zeros_like(dq_ref)

        @pl.when(qi == ki)
        def _():
            dk_sc[...] = jnp.zeros_like(dk_sc)
            dv_sc[...] = jnp.zeros_like(dv_sc)

        def update(diagonal):
            keep = _causal_keep(t, 0, 0)
            d_o = do_ref[...]
            prod = d_o * o_ref[...]
            rows = pl.ds(pl.multiple_of(qi * t, t), t)
            for h in range(2):
                hs = slice(128 * h, 128 * (h + 1))
                mk = _lane_mask(MLA_V * h, MLA_V * (h + 1))
                qh, kh = q_ref[:, hs], k_ref[:, hs]
                sc = _dg(qh, kh, _NT)
                if diagonal:
                    sc = jnp.where(keep, sc, -jnp.inf)
                p = jnp.exp2(sc - lse_ref[:, MLA_V * h:MLA_V * h + 1])
                doh = d_o * mk
                dp = _dg(doh * LN2, v_ref[...], _NT)
                delta = jnp.sum(prod * mk, axis=1, keepdims=True) * LN2
                ds = p * (dp - delta)
                dv_sc[...] += _dg(p, doh, _TN)
                dk_sc[:, hs] += _dg(ds, qh, _TN)
                dq_ref[rows, hs] += _dg(ds, kh, _NN)

        @pl.when(qi > ki)
        def _():
            update(False)

        @pl.when(qi == ki)
        def _():
            update(True)

        @pl.when(qi == n - 1)
        def _():
            dk_ref[...] = dk_sc[...]
            dv_ref[...] = dv_sc[...].astype(dv_ref.dtype)

        @pl.when((pair == MLA_HEADS // 2 - 1) & (step == steps - 1))
        def _():
            comm.finish(place, cins, couts, csems)

    q_idx = lambda p, st, ki_r, qi_r: (qi_r[st], p)
    k_idx = lambda p, st, ki_r, qi_r: (ki_r[st], p)
    res = pl.pallas_call(
        kern, grid_spec=pltpu.PrefetchScalarGridSpec(
            num_scalar_prefetch=2, grid=(MLA_HEADS // 2, steps),
            in_specs=[pl.BlockSpec((t, 256), q_idx), pl.BlockSpec((t, 256), k_idx), pl.BlockSpec((t, 128), k_idx),
                      pl.BlockSpec((t, 128), q_idx), pl.BlockSpec((t, 128), q_idx), pl.BlockSpec((t, 128), q_idx)]
            + [ANY] * nci,
            out_specs=[pl.BlockSpec((s, 256), lambda p, st, ki_r, qi_r: (0, p)), pl.BlockSpec((t, 256), k_idx),
                       pl.BlockSpec((t, 128), k_idx)] + [ANY] * nco,
            scratch_shapes=[pltpu.VMEM((t, 256), F32), pltpu.VMEM((t, 128), F32)] + comm.sems),
        out_shape=[jax.ShapeDtypeStruct((s, 1024), F32), jax.ShapeDtypeStruct((s, 1024), F32),
                   jax.ShapeDtypeStruct((s, 512), BF16)] + comm.out_shape,
        compiler_params=pltpu.CompilerParams(dimension_semantics=("arbitrary", "arbitrary"),
                                             vmem_limit_bytes=VMEM_LIMIT),
        name="mla_attn_bwd")(ki_tab, qi_tab, q, k, v, o, lse, d_o, *comm.ins)
    return res[0], res[1], res[2], res[3:]


def _gate_fn(alr, w2, b):
    return _log_sigmoid(_dot_nn(alr, w2) + b) * (1.0 / GLA_GATE_NORM)


def _qk_head(qh, kh, kpe, c, sa, sb, qn, kn):
    kfull = kh + kpe * _lane_mask(MLA_NOPE, MLA_QK)
    q_r = _rope(_rms(qh, qn, MLA_QK), c, sa, sb) * (MLA_QK ** -0.5 * LOG2E)
    k_r = _rope(_rms(kfull, kn, MLA_QK), c, sa, sb)
    return q_r, k_r


def _mix_head(o, og, gn):
    return _rms(o, gn) * _silu(og)


def _xa_head(xq, xk, xv, qn, kn):
    sc = _dot_nt(_rms(xq, qn), _rms(xk, kn)) * (XA_DIM ** -0.5)
    e = jnp.exp(sc - lax.stop_gradient(jnp.max(sc, axis=1, keepdims=True)))
    p = e / jnp.sum(e, axis=1, keepdims=True)
    return _dot_nn(p, xv)


def _heads(x, n):
    return [x[:, 128 * h:128 * (h + 1)] for h in range(n)]


def _cat(xs):
    return jnp.concatenate(xs, axis=1)


def _norm_fwd(x, w, name):
    return _rows_call(lambda r, c: ([_rms(r[0], c[0])], []), [_row(x)], [w], [(x.shape[1], BF16)], name=name)[0]


def _norm_fwd_epilogue(w):
    return _Epilogue(lambda h, rows, consts: ([h, _rms(h, consts[0])], []), [], [w], [(D_MODEL, F32), (D_MODEL, BF16)], [])


def _norm_bwd_epilogue(x, w, add):
    def fn(d_out, rows, consts):
        _, vjp = jax.vjp(_rms, rows[0], consts[0])
        dx, dw = vjp(d_out)
        return [dx + rows[1]], [dw]

    return _Epilogue(fn, [_row(x), _row(add)], [w], [(D_MODEL, F32)], [w.shape])


def _norm_fwd_comm(x, w, comm, name):
    s, d = x.shape
    t = _tile(s, 512)
    n = s // t
    nci, nco = len(comm.ins), len(comm.out_shape)

    def kern(*refs):
        (x_ref, w_ref), cins, (o_ref,), couts, csems = _split_refs(refs, (2, nci, 1, nco, len(comm.sems)))
        place = _place()

        @pl.when(pl.program_id(0) == 0)
        def _():
            comm.start(place, cins, couts, csems)

        o_ref[...] = _rms(x_ref[...], w_ref[...]).astype(o_ref.dtype)

        @pl.when(pl.program_id(0) == n - 1)
        def _():
            comm.mid(place, cins, couts, csems)
            comm.finish(place, cins, couts, csems)

    tile = pl.BlockSpec((t, d), lambda i: (i, 0))
    res = pl.pallas_call(
        kern, grid=(n,), in_specs=[tile, pl.BlockSpec(w.shape, lambda i: (0, 0))] + [ANY] * nci,
        out_specs=[tile] + [ANY] * nco, out_shape=[jax.ShapeDtypeStruct((s, d), BF16)] + comm.out_shape,
        scratch_shapes=comm.sems,
        compiler_params=pltpu.CompilerParams(dimension_semantics=("arbitrary",), vmem_limit_bytes=VMEM_LIMIT),
        name=name)(x, w, *comm.ins)
    return res[0], res[1:]


def _norm_bwd(x, w, d_out, add, name):
    def body(r, c):
        _, vjp = jax.vjp(_rms, r[0], c[0])
        dx, dw = vjp(r[1])
        return [dx + r[2]], [dw]

    return _rows_call(body, [_row(x), _row(d_out), _row(add)], [w], [(x.shape[1], F32)], [w.shape], name=name)


CONV_HALO = BF16_ROWS


def _conv_specs(s, f, t):
    n8 = t // CONV_HALO
    cur = pl.BlockSpec((None, t, f), lambda j, i: (j, i, 0))
    prev = pl.BlockSpec((None, CONV_HALO, f), lambda j, i: (j, jnp.maximum(i * n8 - 1, 0), 0))
    nxt = pl.BlockSpec((None, CONV_HALO, f), lambda j, i: (j, jnp.minimum((i + 1) * n8, s // CONV_HALO - 1), 0))
    cw = pl.BlockSpec((None, 3, f), lambda j, i: (j, 0, 0))
    cb = pl.BlockSpec((None, 1, f), lambda j, i: (j, 0, 0))
    return cur, prev, nxt, cw, cb


def _conv_taps(g, prev, first):
    ext = jnp.concatenate([jnp.where(first, 0.0, prev.astype(F32)), g], axis=0)
    return pltpu.roll(ext, 1, 0)[CONV_HALO:], pltpu.roll(ext, 2, 0)[CONV_HALO:]


def _conv_fwd(gg, uu, cw, cb):
    _, s, f = gg.shape
    t = _tile(s, 512)

    def kern(g_ref, gp_ref, u_ref, cw_ref, cb_ref, o_ref):
        g = g_ref[...].astype(F32)
        g1, g2 = _conv_taps(g, gp_ref[...], pl.program_id(1) == 0)
        w = cw_ref[...]
        gc = cb_ref[...] + w[0:1] * g2 + w[1:2] * g1 + w[2:3] * g
        o_ref[...] = (_silu(gc) * u_ref[...].astype(F32)).astype(o_ref.dtype)

    cur, prev, _, cws, cbs = _conv_specs(s, f, t)
    return pl.pallas_call(
        kern, grid=(4, s // t), in_specs=[cur, prev, cur, cws, cbs], out_specs=cur,
        out_shape=jax.ShapeDtypeStruct(gg.shape, BF16),
        compiler_params=pltpu.CompilerParams(dimension_semantics=("parallel", "parallel"), vmem_limit_bytes=VMEM_LIMIT),
        name="ffn_conv_fwd")(gg, gg, uu, cw, cb)


def _conv_bwd(gg, uu, dact, cw, cb):
    _, s, f = gg.shape
    t = _tile(s, 512)
    nt = s // t

    def kern(g_ref, gp_ref, gn_ref, u_ref, un_ref, da_ref, dan_ref, cw_ref, cb_ref, du_ref, dg_ref, dcw_ref, dcb_ref):
        i = pl.program_id(1)
        cat = lambda a_ref, b_ref: jnp.concatenate([a_ref[...].astype(F32), b_ref[...].astype(F32)], axis=0)
        g, u, da = cat(g_ref, gn_ref), cat(u_ref, un_ref), cat(da_ref, dan_ref)
        g1, g2 = _conv_taps(g, gp_ref[...], i == 0)
        w = cw_ref[...]
        gc = cb_ref[...] + w[0:1] * g2 + w[1:2] * g1 + w[2:3] * g
        sg = jax.nn.sigmoid(gc)
        du_ref[...] = (da[:t] * (gc[:t] * sg[:t])).astype(du_ref.dtype)
        row = lax.broadcasted_iota(jnp.int32, (t + CONV_HALO, 1), 0)
        dgc = jnp.where((row < t) | (i < nt - 1), da * u * (sg * (1.0 + gc * (1.0 - sg))), 0.0)
        up1 = pltpu.roll(dgc, t + CONV_HALO - 1, 0)[:t]
        up2 = pltpu.roll(dgc, t + CONV_HALO - 2, 0)[:t]
        dgc = dgc[:t]
        dg_ref[...] = (w[2:3] * dgc + w[1:2] * up1 + w[0:1] * up2).astype(dg_ref.dtype)

        @pl.when(i == 0)
        def _():
            dcw_ref[...] = jnp.zeros_like(dcw_ref)
            dcb_ref[...] = jnp.zeros_like(dcb_ref)

        dcw_ref[0:1, :] += jnp.sum(dgc * g2[:t], axis=0, keepdims=True)
        dcw_ref[1:2, :] += jnp.sum(dgc * g1[:t], axis=0, keepdims=True)
        dcw_ref[2:3, :] += jnp.sum(dgc * g[:t], axis=0, keepdims=True)
        dcb_ref[...] += jnp.sum(dgc, axis=0, keepdims=True)

    cur, prev, nxt, cws, cbs = _conv_specs(s, f, t)
    return pl.pallas_call(
        kern, grid=(4, nt), in_specs=[cur, prev, nxt, cur, nxt, cur, nxt, cws, cbs], out_specs=[cur, cur, cws, cbs],
        out_shape=[jax.ShapeDtypeStruct(gg.shape, BF16), jax.ShapeDtypeStruct(gg.shape, BF16),
                   jax.ShapeDtypeStruct(cw.shape, F32), jax.ShapeDtypeStruct(cb.shape, F32)],
        compiler_params=pltpu.CompilerParams(dimension_semantics=("parallel", "arbitrary"), vmem_limit_bytes=VMEM_LIMIT),
        name="ffn_conv_bwd")(gg, gg, gg, uu, uu, dact, dact, cw, cb)


def _rope_tables(pos):
    half = MLA_ROPE // 2
    lane = jnp.arange(LANES)
    rotary = (lane >= MLA_NOPE) & (lane < MLA_QK)
    inv = jnp.where(rotary, ROPE_THETA ** (-((lane - MLA_NOPE) % half).astype(F32) / half), 0.0)
    ang = pos.astype(F32)[:, None] * inv
    cos, sin = jnp.cos(ang), jnp.sin(ang)
    first = rotary & (lane < MLA_NOPE + half)
    return cos, jnp.where(first, -sin, 0.0), jnp.where(rotary & ~first, sin, 0.0)


def _local_step(x, mem, pos, target, rep, early_shards, late_shards):
    g = {}
    c, sa, sb = _rope_tables(pos)

    xn, gathered = _norm_fwd_comm(x, rep["norm_mix"], _gather_plan(early_shards), "norm_mix_fwd_gather")
    w = _early_layout(dict(zip(EARLY, gathered, strict=True)), rep)

    def proj_fn(r, rows, k):
        la_ = _gate_fn(r[:, P_ALR:P_ALR + 128], k[0], k[1])
        return [r, la_, _rms(r[:, P_CQ:P_CQ + MLA_Q_RANK], k[2]), _rms(r[:, P_CKV:P_CKV + MLA_KV_RANK], k[3])], []

    proj, la, q_lat, kv_lat = _matmul(
        xn, w["in"], "nt", F32, "proj_fwd", epilogue=_Epilogue(
            proj_fn, [], [w["w2"], w["gate_b"], w["q_a_norm"], w["kv_a_norm"]],
            [(P_WIDTH, F32), (256, F32), (MLA_Q_RANK, BF16), (MLA_KV_RANK, BF16)], []))
    alr = _row(proj, 128, P_ALR // 128)
    kpe = _row(proj, 128, P_KPE // 128)
    og = _row(proj, 512, P_OG // 512)
    cq = _row(proj, 256, P_CQ // 256)
    ckv = _row(proj, 128, P_CKV // 128)

    o_gla, states = _gla_fwd(proj, la)

    def qk_body(r, k):
        q_up, k_up = _dg(r[0], k[0], _NN), _dg(r[1], k[1], _NN)
        qs, ks = [], []
        for qh, kh in zip(_heads(q_up, MLA_HEADS), _heads(k_up, MLA_HEADS)):
            a, b = _qk_head(qh, kh, r[2], r[3], r[4], r[5], k[3], k[4])
            qs.append(a)
            ks.append(b)
        return [_cat(qs), _cat(ks), _dg(r[1], k[2], _NN)], []

    tabs = [_row(c), _row(sa), _row(sb)]
    qk_consts = [w["uq"], w["k"], w["v"], w["q_norm"], w["k_norm"]]
    q_r, k_r, v_mla = _rows_call(qk_body, [_row(q_lat), _row(kv_lat), kpe] + tabs, qk_consts,
                                 [(1024, BF16), (1024, BF16), (512, BF16)], name="mla_qk_fwd")
    o_mla, lse, gathered = _attn_fwd(q_r, k_r, v_mla, _gather_plan(late_shards))
    w.update(_late_layout(dict(zip(LATE, gathered, strict=True))))

    def mix_body(r, k):
        ys = [_mix_head(o, g_, k[0]) for o, g_ in zip(_heads(r[0], GLA_HEADS), _heads(r[1], GLA_HEADS))]
        return [_cat(ys + [r[2]])], []

    cat = _rows_call(mix_body, [_row(o_gla), og, _row(o_mla)], [w["gla_out_norm"]], [(1024, BF16)],
                     name="mix_fwd")[0]
    h1, hn = _matmul(cat, w["out"], "nn", F32, "out_fwd_norm", residual=x, epilogue=_norm_fwd_epilogue(w["norm_xa"]))
    mn = _norm_fwd(mem, w["norm_mem"], "norm_mem_fwd")
    xkv = _matmul(mn, w["xkv"], "nn", F32, "xa_kv_fwd")

    def xa_fn(r, rows, k):
        ks, vs = _heads(k[0], 2 * XA_HEADS)[:XA_HEADS], _heads(k[0], 2 * XA_HEADS)[XA_HEADS:]
        return [r, _cat([_xa_head(a, b, v_, k[1], k[2]) for a, b, v_ in zip(_heads(r, XA_HEADS), ks, vs)])], []

    xq, xo = _matmul(hn, w["xq"], "nn", F32, "xa_q_fwd_attn", epilogue=_Epilogue(
        xa_fn, [], [xkv, w["xa_q_norm"], w["xa_k_norm"]], [(512, F32), (512, BF16)], []))
    h2, fn = _matmul(xo, w["xo"], "nn", F32, "xa_o_fwd_norm", residual=h1, epilogue=_norm_fwd_epilogue(w["norm_ffn"]))
    gg = _matmul(fn, w["wg"], "nt", BF16, "ffn_gate_fwd", b_lead="p")
    uu = _matmul(fn, w["wu"], "nt", BF16, "ffn_up_fwd", b_lead="p")
    act = _conv_fwd(gg, uu, w["cw"], w["cb"])
    def loss_fn(y, rows, consts):
        err = y - rows[0]
        part = 0.5 * jnp.sum(jnp.sum(err * err, axis=1, keepdims=True) * (1.0 / D_MODEL), axis=0, keepdims=True)
        return [err * (1.0 / D_MODEL)], [jnp.broadcast_to(part, (1, LANES))]

    dy, loss = _matmul(act, w["wd"], "nn", F32, "ffn_down_fwd_loss", residual=h2, a_lead="k", b_lead="k",
                       epilogue=_Epilogue(loss_fn, [_row(target)], [], [(D_MODEL, F32)], [(1, LANES)]))

    g["ffn_w_down"] = _matmul(act, dy, "tn", BF16, "ffn_down_dw", a_lead="p")
    dact = _matmul(dy, w["wd"], "nt", BF16, "ffn_down_dx", b_lead="p")
    duu, dgg, g["ffn_conv_w"], g["ffn_conv_b"] = _conv_bwd(gg, uu, dact, w["cw"], w["cb"])
    g["ffn_w_gate"] = _matmul(dgg, fn, "tn", BF16, "ffn_gate_dw", a_lead="p")
    g["ffn_w_up"] = _matmul(duu, fn, "tn", BF16, "ffn_up_dw", a_lead="p")
    dh2, g["norm_ffn"] = _matmul(dgg, w["wg"], "nn", F32, "ffn_dx_norm_bwd", a_lead="k", b_lead="k", more=(duu, w["wu"]),
                                 epilogue=_norm_bwd_epilogue(h2, w["norm_ffn"], dy))

    g["xa_w_o"] = _matmul(xo, dh2, "tn", BF16, "xa_o_dw")
    def xa_bwd(dxo_, rows, k):
        kvh = _heads(k[0], 2 * XA_HEADS)
        dq_, dk_, dv_ = [], [], []
        dqn, dkn = 0.0, 0.0
        for h, (a, d_) in enumerate(zip(_heads(rows[0], XA_HEADS), _heads(dxo_, XA_HEADS))):
            _, vjp = jax.vjp(_xa_head, a, kvh[h], kvh[XA_HEADS + h], k[1], k[2])
            ga, gk, gv, gqn, gkn = vjp(d_)
            dq_.append(ga)
            dk_.append(gk)
            dv_.append(gv)
            dqn, dkn = dqn + gqn, dkn + gkn
        return [_cat(dq_)], [_cat(dk_ + dv_), dqn, dkn]

    dxq, dxkv, g["xa_q_norm"], g["xa_k_norm"] = _matmul(dh2, w["xo"], "nt", F32, "xa_o_dx_attn_bwd", epilogue=_Epilogue(
        xa_bwd, [_row(xq)], [xkv, w["xa_q_norm"], w["xa_k_norm"]], [(512, BF16)], [xkv.shape, (1, 128), (1, 128)]))
    g["xa_w_q"] = _matmul(hn, dxq, "tn", BF16, "xa_q_dw")
    dh1, g["norm_xa"] = _matmul(dxq, w["xq"], "nt", F32, "xa_q_dx_norm_bwd",
                                epilogue=_norm_bwd_epilogue(h1, w["norm_xa"], dh2))
    g["xa_w_kv"] = _matmul(mn, dxkv, "tn", BF16, "xa_kv_dw")
    dmn = _matmul(dxkv, w["xkv"], "nt", F32, "xa_kv_dx")
    _, g["norm_mem"] = _norm_bwd(mem, w["norm_mem"], dmn, dmn, "norm_mem_bwd")

    g["w_out"] = _matmul(cat, dh1, "tn", BF16, "out_dw")
    def mix_bwd(dcat_, rows, k):
        do_, dog_ = [], []
        dgn = 0.0
        for o, g_, d_ in zip(_heads(rows[0], GLA_HEADS), _heads(rows[1], GLA_HEADS), _heads(dcat_, GLA_HEADS)):
            _, vjp = jax.vjp(_mix_head, o, g_, k[0])
            a, b, gn_ = vjp(d_)
            do_.append(a)
            dog_.append(b)
            dgn = dgn + gn_
        return [_cat(do_), _cat(dog_), dcat_[:, 512:]], [dgn]

    do_gla, d_og, do_mla, g["gla_out_norm"] = _matmul(dh1, w["out"], "nt", F32, "out_dx_mix_bwd", epilogue=_Epilogue(
        mix_bwd, [_row(o_gla), og], [w["gla_out_norm"]], [(512, F32), (512, BF16), (512, F32)], [(1, 128)]))

    late_parts = _late_grad_shards(g)
    dq_r, dk_r, dv_mla, lands_late = _attn_bwd(q_r, k_r, v_mla, o_mla, lse, do_mla,
                                               _scatter_plan([late_parts[n] for n in LATE]))
    lands_late = dict(zip(LATE, lands_late, strict=True))

    def qk_bwd(r, k):
        q_up, k_up = _dg(r[0], k[0], _NN), _dg(r[1], k[1], _NN)
        dqs, dks = [], []
        dkpe, dqn, dkn = 0.0, 0.0, 0.0
        for qh, kh, dqh, dkh in zip(_heads(q_up, MLA_HEADS), _heads(k_up, MLA_HEADS), _heads(r[6], MLA_HEADS),
                                    _heads(r[7], MLA_HEADS)):
            _, vjp = jax.vjp(lambda a, b, e, f, h_: _qk_head(a, b, e, r[3], r[4], r[5], f, h_), qh, kh, r[2], k[3], k[4])
            ga, gb, ge, gf, gh = vjp((dqh, dkh))
            dqs.append(ga)
            dks.append(gb)
            dkpe, dqn, dkn = dkpe + ge, dqn + gf, dkn + gh
        dq_up, dk_up, dv = _cat(dqs), _cat(dks), r[8]
        dq_lat_ = _dg(dq_up, k[0], _NT)
        dkv_lat_ = _dg(dk_up, k[1], _NT) + _dg(dv, k[2], _NT)
        return [dq_lat_, dkv_lat_, dkpe], [dqn, dkn, _dg(r[0], dq_up, _TN), _dg(r[1], dk_up, _TN), _dg(r[1], dv, _TN)]

    dq_lat, dkv_lat, d_kpe, g["q_norm"], g["k_norm"], g["uq"], g["k"], g["v"] = _rows_call(
        qk_bwd, [_row(q_lat), _row(kv_lat), kpe] + tabs + [_row(dq_r), _row(dk_r), _row(dv_mla)], qk_consts,
        [(MLA_Q_RANK, F32), (MLA_KV_RANK, F32), (128, BF16)],
        [(1, 128), (1, 128), w["uq"].shape, w["k"].shape, w["v"].shape], name="mla_qk_bwd")

    dgq, dgk, dla, dgv, _ = _gla_bwd(proj, la, states, do_gla, _Comm([], [], [], lambda *args: None, lambda *args: None))

    def dproj_body(r, k):
        alr_, cq_, ckv_, dla_, dq_lat_, dkv_lat_, dgq_, dgk_, dgv_, d_og_, d_kpe_ = r
        _, gate_vjp = jax.vjp(_gate_fn, alr_, k[0], k[1])
        d_alr, gw2, gb = gate_vjp(dla_)
        _, q_vjp = jax.vjp(_rms, cq_, k[2])
        _, kv_vjp = jax.vjp(_rms, ckv_, k[3])
        d_cq, gqa = q_vjp(dq_lat_)
        d_ckv, gkva = kv_vjp(dkv_lat_)
        pieces = [dgq_, dgk_, dgv_, d_og_, d_cq, d_ckv, d_kpe_, d_alr]
        return [_cat([x_.astype(BF16) for x_ in pieces])], [gw2, gb, gqa, gkva]

    dproj, g["w2"], g["gla_gate_b"], g["mla_q_a_norm"], g["mla_kv_a_norm"] = _rows_call(
        dproj_body, [alr, cq, ckv, _row(dla), _row(dq_lat), _row(dkv_lat), _row(dgq), _row(dgk), _row(dgv), _row(d_og),
                     _row(d_kpe)], [w["w2"], w["gate_b"], w["q_a_norm"], w["kv_a_norm"]], [(P_WIDTH, BF16)],
        [(128, 256), (1, 256), (1, 256), (1, 128)], name="proj_cotangent")
    g["in"] = _matmul(dproj, xn, "tn", BF16, "proj_dw")
    dx, g["norm_mix"] = _matmul(dproj, w["in"], "nn", F32, "proj_dx_norm_bwd",
                                epilogue=_norm_bwd_epilogue(x, w["norm_mix"], dh1))
    return loss[0, 0], dx, g, lands_late


def _join_shards(pieces, axis):
    if axis == 0:
        return pieces.reshape(-1, pieces.shape[2])
    return jnp.transpose(pieces, (1, 0, 2)).reshape(pieces.shape[1], -1)


def _split_shards(full, axis):
    r, c = full.shape
    if axis == 0:
        return full.reshape(4, r // 4, c)
    return jnp.transpose(full.reshape(r, 4, c // 4), (1, 0, 2))


def _early_layout(gath, rep):
    w_in = gath["w_in"].reshape(N_WIDTH, D_MODEL)
    z = lambda n: jnp.zeros((n, D_MODEL), w_in.dtype)
    seg = lambda lo, n: w_in[lo:lo + n]
    ukv = _join_shards(gath["mla_w_ukv"], 1).reshape(MLA_KV_RANK, MLA_HEADS, MLA_NOPE + MLA_V)
    w = {
        "in": jnp.concatenate([seg(N_GQ, 256), seg(N_GK, 256), seg(N_GV, 512), seg(N_OG, 512), seg(N_CQ, 256),
                               seg(N_CKV, 128), z(64), seg(N_KPE, 32), z(32), seg(N_ALR, 16), z(112)], axis=0),
        "uq": jnp.pad(_join_shards(gath["mla_w_uq"], 1).reshape(MLA_Q_RANK, MLA_HEADS, MLA_QK),
                      ((0, 0), (0, 0), (0, LANES - MLA_QK))).reshape(MLA_Q_RANK, MLA_HEADS * LANES),
        "k": jnp.pad(ukv[:, :, :MLA_NOPE], ((0, 0), (0, 0), (0, LANES - MLA_NOPE))).reshape(MLA_KV_RANK, -1),
        "v": ukv[:, :, MLA_NOPE:].reshape(MLA_KV_RANK, MLA_HEADS * MLA_V),
        "w2": jnp.pad(_join_shards(gath["gla_gate_w2"], 1), ((0, LANES - GLA_RANK), (0, 0))),
        "cb": rep["ffn_conv_b"].reshape(4, 1, D_FF // 4),
        "q_norm": jnp.pad(rep["mla_q_norm"], ((0, 0), (0, LANES - MLA_QK))),
        "k_norm": jnp.pad(rep["mla_k_norm"], ((0, 0), (0, LANES - MLA_QK))),
        "q_a_norm": rep["mla_q_a_norm"], "kv_a_norm": rep["mla_kv_a_norm"], "gate_b": rep["gla_gate_b"],
    }
    for n in ("norm_mix", "gla_out_norm", "norm_xa", "norm_mem", "xa_q_norm", "xa_k_norm", "norm_ffn"):
        w[n] = rep[n]
    return w


def _late_layout(gath):
    return {"out": _join_shards(gath["w_out"], 0), "xq": _join_shards(gath["xa_w_q"], 0),
            "xkv": _join_shards(gath["xa_w_kv"], 0), "xo": _join_shards(gath["xa_w_o"], 1),
            "wg": gath["ffn_w_gate"], "wu": gath["ffn_w_up"], "wd": gath["ffn_w_down"], "cw": gath["ffn_conv_w"]}


def _late_grad_shards(g):
    sh = {"w_out": _split_shards(g["w_out"], 0), "xa_w_q": _split_shards(g["xa_w_q"], 0),
          "xa_w_kv": _split_shards(g["xa_w_kv"], 0), "xa_w_o": _split_shards(g["xa_w_o"], 1),
          "ffn_w_gate": g["ffn_w_gate"], "ffn_w_up": g["ffn_w_up"], "ffn_conv_w": g["ffn_conv_w"],
          "ffn_w_down": g["ffn_w_down"]}
    return {n: v.astype(BF16) for n, v in sh.items()}


def _early_grad_shards(g):
    gi = g["in"]
    seg = lambda lo, n: gi[lo:lo + n]
    w_in = jnp.concatenate([seg(P_GQ, 256), seg(P_GK, 256), seg(P_GV, 512), seg(P_ALR, 16), seg(P_OG, 512),
                            seg(P_CQ, 256), seg(P_CKV, 128), seg(P_KPE + 64, 32)], axis=0)
    uq = g["uq"].reshape(MLA_Q_RANK, MLA_HEADS, LANES)[:, :, :MLA_QK].reshape(MLA_Q_RANK, -1)
    ukv = jnp.concatenate([g["k"].reshape(MLA_KV_RANK, MLA_HEADS, LANES)[:, :, :MLA_NOPE],
                           g["v"].reshape(MLA_KV_RANK, MLA_HEADS, MLA_V)], axis=2).reshape(MLA_KV_RANK, -1)
    sh = {"w_in": w_in.reshape(4, N_WIDTH // 4, D_MODEL), "gla_gate_w2": _split_shards(g["w2"][:GLA_RANK], 1),
          "mla_w_uq": _split_shards(uq, 1), "mla_w_ukv": _split_shards(ukv, 1)}
    sh = {n: v.astype(BF16) for n, v in sh.items()}
    rep = {n: g[n] for n in REPLICATED if n in g}
    rep["mla_q_norm"] = g["q_norm"][:, :MLA_QK]
    rep["mla_k_norm"] = g["k_norm"][:, :MLA_QK]
    rep["ffn_conv_b"] = g["ffn_conv_b"].reshape(1, D_FF)
    return sh, rep


SMALL_SHAPE = (8, 1024)


def _pack_small(vectors):
    flat = jnp.concatenate(vectors, axis=1)
    return jnp.pad(flat, ((0, 0), (0, SMALL_SHAPE[0] * SMALL_SHAPE[1] - flat.shape[1]))).reshape(SMALL_SHAPE)


def _unpack_small(buf, widths):
    flat = buf.reshape(1, -1)
    out, off = [], 0
    for wd in widths:
        out.append(flat[:, off:off + wd])
        off += wd
    return out


ANY = pl.BlockSpec(memory_space=pl.ANY)


def _place():
    x, y, c = lax.axis_index("x"), lax.axis_index("y"), lax.axis_index("c")
    chips = [(1 - x, y), (x, 1 - y), (1 - x, 1 - y)]
    return x, y, c, chips


class _Comm:
    def __init__(self, ins, out_shape, sems, start, finish, mid=None):
        self.ins, self.out_shape, self.sems = list(ins), list(out_shape), list(sems)
        self.start, self.finish, self.mid = start, finish, mid or (lambda *args: None)


def _run_comm(plan, name):
    ni, no = len(plan.ins), len(plan.out_shape)

    def body(*refs):
        ins, outs, sems = refs[:ni], refs[ni:ni + no], refs[ni + no:]
        place = _place()
        plan.start(place, ins, outs, sems)
        plan.mid(place, ins, outs, sems)
        plan.finish(place, ins, outs, sems)

    return pl.pallas_call(body, in_specs=[ANY] * ni, out_specs=[ANY] * no, out_shape=plan.out_shape,
                          scratch_shapes=plan.sems, name=name)(*plan.ins)


def _gather_plan(shards):
    n = len(shards)
    by_rows = [s.shape[0] % (2 * BF16_ROWS) == 0 for s in shards]
    by_cols = [not r and s.shape[1] % (2 * LANES) == 0 for r, s in zip(by_rows, shards)]
    split = [r or c for r, c in zip(by_rows, by_cols)]

    def rows(ref, t, c):
        if by_rows[t]:
            half = shards[t].shape[0] // 2
            return ref.at[pl.ds(pl.multiple_of(c * half, BF16_ROWS), half)]
        if by_cols[t]:
            half = shards[t].shape[1] // 2
            return ref.at[:, pl.ds(pl.multiple_of(c * half, LANES), half)]
        return ref

    def remote(src, dst, ss, rs, to):
        return pltpu.make_async_remote_copy(src_ref=src, dst_ref=dst, send_sem=ss, recv_sem=rs, device_id=to,
                                            device_id_type=MESH)

    def first_wave(place, ins, outs, sems):
        x, y, c, chips = place
        ici_s, ici_r, _, _, local = sems
        me = 2 * x + y
        own = [pltpu.make_async_copy(ins[t], outs[t].at[me], local.at[t]) for t in range(n)]
        push = [remote(rows(ins[t], t, c), rows(outs[t].at[me], t, c), ici_s.at[3 * t + j], ici_r.at[3 * t + j], (px, py, c))
                for t in range(n) for j, (px, py) in enumerate(chips)]
        return own, push

    def second_wave(place, ins, outs, sems, last):
        x, y, c, chips = place
        ici_s, ici_r, d2d_s, d2d_r, local = sems
        sib = (x, y, 1 - c)
        out = []
        for t in range(n):
            for j, (px, py) in enumerate(chips):
                block = outs[t].at[2 * px + py]
                got = rows(block, t, c)
                if split[t]:
                    hand = remote(got, got, d2d_s.at[3 * t + j], d2d_r.at[3 * t + j], sib)
                    theirs = rows(block, t, 1 - c)
                    other = (remote(theirs, theirs, local.at[0], d2d_r.at[3 * t + j], sib) if last else
                             remote(got, got, local.at[0], ici_r.at[3 * t + j], sib))
                    out.append((other, hand))
                elif last:
                    out.append((remote(got, got, local.at[0], ici_r.at[3 * t + j], sib), None))
        return out

    def start(place, ins, outs, sems):
        own, push = first_wave(place, ins, outs, sems)
        for cp in own + push:
            cp.start()

    def mid(place, ins, outs, sems):
        for arrival, hand in second_wave(place, ins, outs, sems, False):
            arrival.wait_recv()
            hand.start()

    def finish(place, ins, outs, sems):
        own, push = first_wave(place, ins, outs, sems)
        for arrival, hand in second_wave(place, ins, outs, sems, True):
            arrival.wait_recv()
            if hand is not None:
                hand.wait_send()
        for cp in push:
            cp.wait_send()
        for cp in own:
            cp.wait()

    dma = pltpu.SemaphoreType.DMA
    return _Comm(shards, [jax.ShapeDtypeStruct((4,) + s.shape, s.dtype) for s in shards],
                 [dma((3 * n,)), dma((3 * n,)), dma((3 * n,)), dma((3 * n,)), dma((n,))], start, finish, mid)


def _scatter_plan(parts, small=None):
    n = len(parts)
    ns = 0 if small is None else 1

    def unpack(place, ins, outs, sems):
        x, y, c, chips = place
        return x, y, c, chips, 2 * x + y, 4 * x + 2 * y + c, (x, y, 1 - c)

    def remote(src, dst, ss, rs, to):
        return pltpu.make_async_remote_copy(src_ref=src, dst_ref=dst, send_sem=ss, recv_sem=rs, device_id=to,
                                            device_id_type=MESH)

    def first_wave(place, ins, outs, sems):
        x, y, c, chips, me, dev, sib = unpack(place, ins, outs, sems)
        ici_s, ici_r, d2d_s, d2d_r, sm_s, sm_r, local = sems
        own, push = [], []
        if ns:
            own.append(pltpu.make_async_copy(ins[n], outs[n].at[dev], local.at[n]))
            for k in range(1, 8):
                px = (1 - x) if (k >> 2) & 1 else x
                py = (1 - y) if (k >> 1) & 1 else y
                pc = (1 - c) if k & 1 else c
                push.append(remote(ins[n], outs[n].at[dev], sm_s.at[k - 1], sm_r.at[k - 1], (px, py, pc)))
        for t in range(n):
            own.append(pltpu.make_async_copy(ins[t].at[me], outs[t].at[dev], local.at[t]))
            push.append(remote(ins[t].at[me], outs[t].at[dev], d2d_s.at[4 * t], d2d_r.at[4 * t], sib))
            for j, (px, py) in enumerate(chips):
                push.append(remote(ins[t].at[2 * px + py], outs[t].at[dev], ici_s.at[3 * t + j], ici_r.at[3 * t + j],
                                   (px, py, c)))
        return own, push

    def start(place, ins, outs, sems):
        own, push = first_wave(place, ins, outs, sems)
        for cp in own + push:
            cp.start()

    def landed(dst, rs, sems, sib):
        remote(dst, dst, sems[-1].at[0], rs, sib).wait_recv()

    def forwards(place, ins, outs, sems):
        x, y, c, chips, me, dev, sib = unpack(place, ins, outs, sems)
        d2d_s, d2d_r = sems[2], sems[3]
        slots = [(t, j, outs[t].at[4 * px + 2 * py + c]) for t in range(n) for j, (px, py) in enumerate(chips)]
        return [(t, j, slot, remote(slot, slot, d2d_s.at[4 * t + 1 + j], d2d_r.at[4 * t + 1 + j], sib))
                for t, j, slot in slots]

    def mid(place, ins, outs, sems):
        sib = unpack(place, ins, outs, sems)[-1]
        for t, j, slot, cp in forwards(place, ins, outs, sems):
            landed(slot, sems[1].at[3 * t + j], sems, sib)
            cp.start()

    def finish(place, ins, outs, sems):
        x, y, c, chips, me, dev, sib = unpack(place, ins, outs, sems)
        d2d_r, sm_r = sems[3], sems[5]
        own, push = first_wave(place, ins, outs, sems)
        push += [cp for _, _, _, cp in forwards(place, ins, outs, sems)]
        for t in range(n):
            landed(outs[t].at[4 * x + 2 * y + (1 - c)], d2d_r.at[4 * t], sems, sib)
            for j, (px, py) in enumerate(chips):
                landed(outs[t].at[4 * px + 2 * py + (1 - c)], d2d_r.at[4 * t + 1 + j], sems, sib)
        if ns:
            for k in range(1, 8):
                px = (1 - x) if (k >> 2) & 1 else x
                py = (1 - y) if (k >> 1) & 1 else y
                pc = (1 - c) if k & 1 else c
                landed(outs[n].at[4 * px + 2 * py + pc], sm_r.at[k - 1], sems, sib)
        for cp in push:
            cp.wait_send()
        for cp in own:
            cp.wait()

    dma = pltpu.SemaphoreType.DMA
    ins = list(parts) + ([small] if ns else [])
    out_shape = [jax.ShapeDtypeStruct((8,) + p.shape[1:], p.dtype) for p in parts]
    if ns:
        out_shape.append(jax.ShapeDtypeStruct((8,) + small.shape, small.dtype))
    return _Comm(ins, out_shape, [dma((3 * n,)), dma((3 * n,)), dma((4 * n,)), dma((4 * n,)), dma((7,)), dma((7,)),
                                  dma((n + 1,))], start, finish, mid)


ADAM_ROWS = 288


def _row_tile(r, cap):
    if r <= cap:
        return r
    return max((t for t in range(8, cap + 1, 8) if r % t == 0), default=r)


def _adamw_update(w, m, v, land):
    g = land[0].astype(F32)
    for i in range(1, 8):
        g = g + land[i].astype(F32)
    m_new = ADAM_B1 * m + (1.0 - ADAM_B1) * g
    v_new = ADAM_B2 * v + (1.0 - ADAM_B2) * (g * g)
    m_hat = m_new / (1.0 - ADAM_B1 ** ADAM_STEP)
    v_hat = v_new / (1.0 - ADAM_B2 ** ADAM_STEP)
    return g, -ADAM_LR * (m_hat / (jnp.sqrt(v_hat) + ADAM_EPS) + ADAM_WD * w), m_new, v_new


def _adamw(tensors, name, comm=None):
    k = len(tensors)
    r, c = tensors[0][0].shape
    t = _row_tile(r, ADAM_ROWS // k)
    tc = c if t < r or r <= ADAM_ROWS else 2 * LANES
    n = (r // t) * (c // tc)
    nci, nco, nsem = (len(comm.ins), len(comm.out_shape), len(comm.sems)) if comm else (0, 0, 0)

    def kern(*refs):
        ins, cins, outs, couts, csems = _split_refs(refs, (4 * k, nci, 4 * k, nco, nsem))
        if comm:
            place = _place()

            @pl.when(pl.program_id(0) == 0)
            def _():
                comm.start(place, cins, couts, csems)

        for i in range(k):
            w_ref, m_ref, v_ref, l_ref = ins[4 * i:4 * i + 4]
            res = _adamw_update(w_ref[...], m_ref[...], v_ref[...], l_ref)
            for ref, val in zip(outs[4 * i:4 * i + 4], res, strict=True):
                ref[...] = val
        if comm:
            @pl.when(pl.program_id(0) == n - 1)
            def _():
                comm.mid(place, cins, couts, csems)
                comm.finish(place, cins, couts, csems)

    where = (lambda i: (i, 0)) if tc == c else (lambda i: (0, i))
    spec = pl.BlockSpec((t, tc), where)
    lspec = pl.BlockSpec((8, t, tc), lambda i: (0,) + where(i))
    res = pl.pallas_call(
        kern, grid=(n,), in_specs=[spec, spec, spec, lspec] * k + [ANY] * nci, out_specs=[spec] * (4 * k) + [ANY] * nco,
        out_shape=[jax.ShapeDtypeStruct((r, c), F32)] * (4 * k) + (comm.out_shape if comm else []),
        scratch_shapes=comm.sems if comm else [],
        compiler_params=pltpu.CompilerParams(dimension_semantics=("arbitrary" if comm else "parallel",),
                                             vmem_limit_bytes=VMEM_LIMIT),
        name=name)(*[x for tens in tensors for x in tens], *(comm.ins if comm else []))
    return [res[4 * i:4 * i + 4] for i in range(k)], res[4 * k:]


def _step(a):
    def sq(n):
        v = a[n][0] if a[n].ndim == 3 else a[n]
        return v.T if n.removeprefix("m_").removeprefix("v_") in TRANSPOSED else v

    payload = lambda n: sq(n) if n in EXACT_GATHER else sq(n).astype(BF16)

    loss, dx, g, lands_late = _local_step(sq("x"), sq("mem"), a["positions"][0], sq("loss_target"),
                                          {n: a[n] for n in REPLICATED}, [payload(n) for n in EARLY],
                                          [payload(n) for n in LATE])

    sh, rep = _early_grad_shards(g)
    small = _pack_small([rep[n] for n in REPLICATED] + [loss.reshape(1, 1)])
    *lands_early, land_small = _run_comm(_scatter_plan([sh[n] for n in EARLY], small), "scatter_last")
    quad = lambda n, land: (sq(n), sq("m_" + n), sq("v_" + n), land)
    lands = dict(zip(EARLY, lands_early, strict=True)) | lands_late

    outs = {}
    kinds = ("grad_", "delta_", "new_m_", "new_v_")
    for n, _ in SHARDED:
        res = _adamw([quad(n, lands[n])], "adamw_" + n)[0][0]
        for kind, val in zip(kinds, res, strict=True):
            outs[kind + n] = (val.T if n in TRANSPOSED else val).reshape(a[n].shape)
    zero = jnp.zeros((1, 1), F32)
    packed = [_pack_small([a[p + n] for n in REPLICATED] + [zero]) for p in ("", "m_", "v_")]
    res = _adamw([(*packed, land_small)], "adamw_replicated")[0][0]
    widths = [a[n].shape[1] for n in REPLICATED] + [1]
    for kind, buf in zip(kinds, res, strict=True):
        *vals, total = _unpack_small(buf, widths)
        for n, val in zip(REPLICATED, vals, strict=True):
            outs[kind + n] = val
        if kind == "grad_":
            loss = total[0, 0]

    ordered = [outs[kind + n] for kind in kinds for n in WEIGHTS]
    return (loss, dx[None], *ordered)


def kernel(x, mem, positions, norm_mix, w_in, gla_gate_w2, gla_gate_b, gla_out_norm, mla_q_a_norm, mla_w_uq, mla_kv_a_norm, mla_w_ukv, mla_q_norm, mla_k_norm, w_out, norm_xa, norm_mem, xa_w_q, xa_w_kv, xa_q_norm, xa_k_norm, xa_w_o, norm_ffn, ffn_w_gate, ffn_w_up, ffn_conv_w, ffn_conv_b, ffn_w_down, loss_target, m_norm_mix, m_w_in, m_gla_gate_w2, m_gla_gate_b, m_gla_out_norm, m_mla_q_a_norm, m_mla_w_uq, m_mla_kv_a_norm, m_mla_w_ukv, m_mla_q_norm, m_mla_k_norm, m_w_out, m_norm_xa, m_norm_mem, m_xa_w_q, m_xa_w_kv, m_xa_q_norm, m_xa_k_norm, m_xa_w_o, m_norm_ffn, m_ffn_w_gate, m_ffn_w_up, m_ffn_conv_w, m_ffn_conv_b, m_ffn_w_down, v_norm_mix, v_w_in, v_gla_gate_w2, v_gla_gate_b, v_gla_out_norm, v_mla_q_a_norm, v_mla_w_uq, v_mla_kv_a_norm, v_mla_w_ukv, v_mla_q_norm, v_mla_k_norm, v_w_out, v_norm_xa, v_norm_mem, v_xa_w_q, v_xa_w_kv, v_xa_q_norm, v_xa_k_norm, v_xa_w_o, v_norm_ffn, v_ffn_w_gate, v_ffn_w_up, v_ffn_conv_w, v_ffn_conv_b, v_ffn_w_down):
    return _step(dict(locals()))
```

```python
import functools

import jax
import jax.numpy as jnp
import numpy as np
from jax import lax
from jax.experimental import pallas as pl
from jax.experimental.pallas import tpu as pltpu

F32, BF16 = jnp.float32, jnp.bfloat16
MESH = pl.DeviceIdType.MESH

D_MODEL = 1024
EPS = 1e-6
GLA_HEADS, GLA_DK, GLA_DV, GLA_RANK, GLA_CHUNK = 4, 64, 128, 16, 64
GLA_GATE_NORM = 16.0
MLA_HEADS, MLA_Q_RANK, MLA_KV_RANK, MLA_NOPE, MLA_ROPE, MLA_V = 8, 256, 128, 64, 32, 64
MLA_QK = MLA_NOPE + MLA_ROPE
ROPE_THETA = 10000.0
LOG2E, LN2 = 1.4426950408889634, 0.6931471805599453
XA_HEADS, XA_DIM = 4, 128
D_FF = 2816
ADAM_LR, ADAM_B1, ADAM_B2, ADAM_EPS, ADAM_WD, ADAM_STEP = 0.001, 0.9, 0.999, 1e-08, 0.01, 10

LANES = 128
BF16_ROWS = 16
VMEM_LIMIT = 56 * 1024 * 1024
MATMUL_VMEM = 44 * 1024 * 1024
ROW_TILE = 512

P_GQ, P_GK, P_GV, P_OG, P_CQ, P_CKV, P_KPE, P_ALR, P_WIDTH = 0, 256, 512, 1024, 1536, 1792, 1920, 2048, 2176
N_GQ, N_GK, N_GV, N_ALR, N_OG, N_CQ, N_CKV, N_KPE, N_WIDTH = 0, 256, 512, 1024, 1040, 1552, 1808, 1936, 1968

SHARDED = (("w_in", 1), ("gla_gate_w2", 1), ("mla_w_uq", 1), ("mla_w_ukv", 1), ("w_out", 0), ("xa_w_q", 0),
           ("xa_w_kv", 0), ("xa_w_o", 1), ("ffn_w_gate", 1), ("ffn_w_up", 1), ("ffn_conv_w", 1), ("ffn_w_down", 0))
REPLICATED = ("norm_mix", "gla_gate_b", "gla_out_norm", "mla_q_a_norm", "mla_kv_a_norm", "mla_q_norm", "mla_k_norm",
              "norm_xa", "norm_mem", "xa_q_norm", "xa_k_norm", "norm_ffn", "ffn_conv_b")
EXACT_GATHER = ("gla_gate_w2", "ffn_conv_w")
TRANSPOSED = ("w_in", "ffn_w_gate", "ffn_w_up")
EARLY = ("w_in", "gla_gate_w2", "mla_w_uq", "mla_w_ukv")
LATE = tuple(n for n, _ in SHARDED if n not in EARLY)
WEIGHTS = ("norm_mix", "w_in", "gla_gate_w2", "gla_gate_b", "gla_out_norm", "mla_q_a_norm", "mla_w_uq",
           "mla_kv_a_norm", "mla_w_ukv", "mla_q_norm", "mla_k_norm", "w_out", "norm_xa", "norm_mem", "xa_w_q",
           "xa_w_kv", "xa_q_norm", "xa_k_norm", "xa_w_o", "norm_ffn", "ffn_w_gate", "ffn_w_up", "ffn_conv_w",
           "ffn_conv_b", "ffn_w_down")


_NN = ((1,), (0,))
_NT = ((1,), (1,))
_TN = ((0,), (0,))


def _dg(a, b, dims):
    return lax.dot_general(a.astype(BF16), b.astype(BF16), (dims, ((), ())), preferred_element_type=F32)


@jax.custom_vjp
def _dot_nn(a, b):
    return _dg(a, b, _NN)


_dot_nn.defvjp(lambda a, b: (_dg(a, b, _NN), (a, b)),
               lambda r, g: (_dg(g, r[1], _NT).astype(r[0].dtype), _dg(r[0], g, _TN).astype(r[1].dtype)))


@jax.custom_vjp
def _dot_nt(a, b):
    return _dg(a, b, _NT)


_dot_nt.defvjp(lambda a, b: (_dg(a, b, _NT), (a, b)),
               lambda r, g: (_dg(g, r[1], _NN).astype(r[0].dtype), _dg(g, r[0], _TN).astype(r[1].dtype)))


@jax.custom_vjp
def _dot_tn(a, b):
    return _dg(a, b, _TN)


_dot_tn.defvjp(lambda a, b: (_dg(a, b, _TN), (a, b)),
               lambda r, g: (_dg(r[1], g, _NT).astype(r[0].dtype), _dg(r[0], g, _NN).astype(r[1].dtype)))


def _rms(x, w, n=None):
    n = x.shape[-1] if n is None else n
    ms = jnp.sum(x * x, axis=-1, keepdims=True) * (1.0 / n)
    return x * lax.rsqrt(ms + EPS) * w


def _silu(x):
    return x * jax.nn.sigmoid(x)


def _log_sigmoid(x):
    return jnp.minimum(x, 0.0) - jnp.log(1.0 + jnp.exp(-jnp.abs(x)))


@jax.custom_vjp
def _rope(y, c, sa, sb):
    return y * c + pltpu.roll(y, LANES - 16, 1) * sa + pltpu.roll(y, 16, 1) * sb


def _rope_bwd(res, g):
    c, sa, sb = res
    gy = g * c + pltpu.roll(g * sa, 16, 1) + pltpu.roll(g * sb, LANES - 16, 1)
    return gy, jnp.zeros_like(c), jnp.zeros_like(sa), jnp.zeros_like(sb)


_rope.defvjp(lambda y, c, sa, sb: (_rope(y, c, sa, sb), (c, sa, sb)), _rope_bwd)


@jax.custom_vjp
def _cumsum_rows(x):
    n = x.shape[0]
    row = lax.broadcasted_iota(jnp.int32, x.shape, 0)
    k = 1
    while k < n:
        x = x + jnp.where(row >= k, pltpu.roll(x, k, 0), 0.0)
        k *= 2
    return x


def _cumsum_rows_bwd(_, g):
    n = g.shape[0]
    row = lax.broadcasted_iota(jnp.int32, g.shape, 0)
    k = 1
    while k < n:
        g = g + jnp.where(row < n - k, pltpu.roll(g, n - k, 0), 0.0)
        k *= 2
    return (g,)


_cumsum_rows.defvjp(lambda x: (_cumsum_rows(x), None), _cumsum_rows_bwd)


def _lane_mask(lo, hi):
    lane = lax.broadcasted_iota(jnp.int32, (1, LANES), 1)
    return ((lane >= lo) & (lane < hi)).astype(F32)


def _tile(n, t):
    t = min(n, t)
    assert n % t == 0, (n, t)
    return t


class _Epilogue:
    def __init__(self, fn, rows=(), consts=(), outs=(), accs=()):
        self.fn, self.rows, self.consts, self.outs, self.accs = fn, list(rows), list(consts), list(outs), list(accs)


def _matmul(a, b, mode, out_dtype, name, residual=None, a_lead=None, b_lead=None, more=None, epilogue=None):
    (a0, a1), (b0, b1) = a.shape[-2:], b.shape[-2:]
    if mode == "nn":
        m, k, k2, n = a0, a1, b0, b1
    elif mode == "nt":
        m, k, n, k2 = a0, a1, b0, b1
    else:
        k, m, k2, n = a0, a1, b0, b1
    assert k == k2, (a.shape, b.shape, mode)
    npar = 4 if "p" in (a_lead, b_lead) else 1
    nsum = 4 if "k" in (a_lead, b_lead) else 1
    pairs = [(a, b)] + ([more] if more else [])
    a_item, b_item, o_item = a.dtype.itemsize, b.dtype.itemsize, jnp.dtype(out_dtype).itemsize
    ep = epilogue
    row_extra = 4 if residual is not None else 0
    if ep:
        row_extra += (sum(r.dtype.itemsize * wd for r, wd, _ in ep.rows) + sum(jnp.dtype(d).itemsize * wd for wd, d in ep.outs)) / n

    def resident(lead, tiles):
        return lead != "p" and tiles == 1

    def vmem_need(tm, tn, tk):
        a_bufs = 1 if resident(a_lead, (m // tm) * (k // tk)) else 2
        b_bufs = 1 if resident(b_lead, (n // tn) * (k // tk)) else 2
        need = a_bufs * (nsum if a_lead == "k" else 1) * tm * tk * a_item + b_bufs * (nsum if b_lead == "k" else 1) * tk * tn * b_item
        need *= len(pairs)
        need += (0 if ep else 2 * tm * tn * o_item) + tm * tn * 4 * (2 if tk < k else 1)
        need += tm * tk * 2 * (a_item == 4 or mode == "tn") + tk * tn * 2 * (b_item == 4)
        return need + int(2 * tm * tn * row_extra) + (3 * tm * tn * 4 if ep else 0)

    halvings = (4096, 2048, 1024, 512, 256, 128, 64, 32, 16, 8)
    if mode == "tn":
        tm = m if m <= 2304 else m // 2
        tn = n if tm * n <= 1024 * 2304 else n // 2
        tk = next((r for r in halvings if k % r == 0 and vmem_need(tm, tn, r) <= MATMUL_VMEM), k)
    else:
        tn, tk = n, k
        tm = next((r for r in halvings if m % r == 0 and vmem_need(r, tn, tk) <= MATMUL_VMEM), m)
    assert m % tm == 0 and n % tn == 0 and k % tk == 0
    assert ep is None or (tn == n and tk == k and npar == 1)
    nk = k // tk
    dims = {"nn": _NN, "nt": _NT, "tn": _TN}[mode]
    n_in = 2 * len(pairs) + (residual is not None)
    n_ep_in = len(ep.rows) + len(ep.consts) if ep else 0
    n_out = len(ep.outs) + len(ep.accs) if ep else 1

    def body(*refs):
        ab, rs, ep_in, outs, scratch = _split_refs(refs, (2 * len(pairs), n_in - 2 * len(pairs), n_ep_in, n_out, nk > 1))
        prod = None
        for a_ref, b_ref in zip(ab[0::2], ab[1::2]):
            for sh in range(nsum):
                term = _dg(a_ref[sh] if a_lead == "k" else a_ref[...], b_ref[sh] if b_lead == "k" else b_ref[...], dims)
                prod = term if prod is None else prod + term

        def finish(r):
            if rs:
                r = r + rs[0][...]
            if ep is None:
                outs[0][...] = r.astype(outs[0].dtype)
                return
            vals = [x[...] for x in ep_in]
            ro, ao = ep.fn(r, vals[:len(ep.rows)], vals[len(ep.rows):])
            for ref, val in zip(outs[:len(ep.outs)], ro, strict=True):
                ref[...] = val.astype(ref.dtype)
            if ep.accs:
                @pl.when(pl.program_id(0) == 0)
                def _():
                    for ref in outs[len(ep.outs):]:
                        ref[...] = jnp.zeros_like(ref)

                for ref, val in zip(outs[len(ep.outs):], ao, strict=True):
                    ref[...] += val

        if nk == 1:
            finish(prod)
            return
        acc = scratch[0]
        kk = pl.program_id(3)

        @pl.when(kk == 0)
        def _():
            acc[...] = prod

        @pl.when(kk > 0)
        def _():
            acc[...] += prod

        @pl.when(kk == nk - 1)
        def _():
            finish(acc[...])

    def spec(lead, blk, idx, tiles=0):
        mode = {"pipeline_mode": pl.Buffered(1)} if resident(lead, tiles) else {}
        if lead is None:
            return pl.BlockSpec(blk, lambda i, j, p, kk: idx(i, j, kk), **mode)
        if lead == "p":
            return pl.BlockSpec((None,) + blk, lambda i, j, p, kk: (p,) + idx(i, j, kk))
        return pl.BlockSpec((nsum,) + blk, lambda i, j, p, kk: (0,) + idx(i, j, kk), **mode)

    a_tiles, b_tiles = (m // tm) * nk, (n // tn) * nk
    if mode == "nn":
        pair_specs = [spec(a_lead, (tm, tk), lambda i, j, kk: (i, kk), a_tiles),
                      spec(b_lead, (tk, tn), lambda i, j, kk: (kk, j), b_tiles)]
    elif mode == "nt":
        pair_specs = [spec(a_lead, (tm, tk), lambda i, j, kk: (i, kk), a_tiles),
                      spec(b_lead, (tn, tk), lambda i, j, kk: (j, kk), b_tiles)]
    else:
        pair_specs = [spec(a_lead, (tk, tm), lambda i, j, kk: (kk, i), a_tiles),
                      spec(b_lead, (tk, tn), lambda i, j, kk: (kk, j), b_tiles)]
    tile = spec(None, (tm, tn), lambda i, j, kk: (i, j))
    in_specs = pair_specs * len(pairs)
    args = [x for pair in pairs for x in pair]
    if residual is not None:
        assert npar == 1
        in_specs.append(tile)
        args.append(residual)
    if ep:
        in_specs += [pl.BlockSpec((tm, wd), functools.partial(lambda cb, i, j, p, kk: (i, cb), cb)) for _, wd, cb in ep.rows]
        in_specs += [pl.BlockSpec(c.shape, lambda i, j, p, kk: (0, 0)) for c in ep.consts]
        args += [r for r, _, _ in ep.rows] + ep.consts
        out_specs = [pl.BlockSpec((tm, wd), lambda i, j, p, kk: (i, 0)) for wd, _ in ep.outs]
        out_specs += [pl.BlockSpec(shape, lambda i, j, p, kk: (0, 0)) for shape in ep.accs]
        out_shape = [jax.ShapeDtypeStruct((m, wd), d) for wd, d in ep.outs] + [jax.ShapeDtypeStruct(sh, F32) for sh in ep.accs]
    else:
        out_specs = spec("p" if npar > 1 else None, (tm, tn), lambda i, j, kk: (i, j))
        out_shape = jax.ShapeDtypeStruct(((4,) if npar > 1 else ()) + (m, n), out_dtype)
    outer = "arbitrary" if ep and ep.accs else "parallel"
    return pl.pallas_call(
        body, grid=(m // tm, n // tn, npar, nk), in_specs=in_specs, out_specs=out_specs, out_shape=out_shape,
        scratch_shapes=[pltpu.VMEM((tm, tn), F32)] if nk > 1 else [],
        compiler_params=pltpu.CompilerParams(dimension_semantics=(outer, outer, outer, "arbitrary"),
                                             vmem_limit_bytes=VMEM_LIMIT),
        name=name)(*args)


def _row(a, width=None, col_block=0):
    return (a, a.shape[1] if width is None else width, col_block)


def _rows_call(body, rows, consts, outs, accs=(), *, name, tile=ROW_TILE):
    s = rows[0][0].shape[0]
    t = _tile(s, tile)
    nr, nc, no = len(rows), len(consts), len(outs)

    def kern(*refs):
        r = [x[...] for x in refs[:nr]]
        c = [x[...] for x in refs[nr:nr + nc]]
        o_refs = refs[nr + nc:nr + nc + no]
        a_refs = refs[nr + nc + no:]
        ro, ao = body(r, c)
        for ref, val in zip(o_refs, ro, strict=True):
            ref[...] = val.astype(ref.dtype)
        if a_refs:
            @pl.when(pl.program_id(0) == 0)
            def _():
                for ref in a_refs:
                    ref[...] = jnp.zeros_like(ref)

            for ref, val in zip(a_refs, ao, strict=True):
                ref[...] += val

    in_specs = [pl.BlockSpec((t, w), functools.partial(lambda cb, i: (i, cb), cb)) for (_, w, cb) in rows]
    in_specs += [pl.BlockSpec(c.shape, lambda i: (0, 0)) for c in consts]
    out_specs = [pl.BlockSpec((t, w), lambda i: (i, 0)) for (w, _) in outs]
    out_specs += [pl.BlockSpec(shape, lambda i: (0, 0)) for shape in accs]
    out_shape = [jax.ShapeDtypeStruct((s, w), dt) for (w, dt) in outs]
    out_shape += [jax.ShapeDtypeStruct(shape, F32) for shape in accs]
    return pl.pallas_call(
        kern, grid=(s // t,), in_specs=in_specs, out_specs=out_specs, out_shape=out_shape,
        compiler_params=pltpu.CompilerParams(dimension_semantics=("arbitrary" if accs else "parallel",),
                                             vmem_limit_bytes=VMEM_LIMIT),
        name=name)(*[r[0] for r in rows], *consts)


def _gla_chunk(q, k, la, v0, v1, s0, s1):
    c = q.shape[0]
    r = lax.broadcasted_iota(jnp.int32, (c, c), 0)
    cc = lax.broadcasted_iota(jnp.int32, (c, c), 1)
    tril = cc <= r
    cum = _cumsum_rows(la)
    cl = jnp.sum(la, axis=0, keepdims=True)
    qd = q * (GLA_DK ** -0.5) * jnp.exp(cum)
    ki = k * jnp.exp(-cum)
    ke = k * jnp.exp(cl - cum)
    dec = jnp.exp(cl)
    outs, news = [], []
    for h, (v, s) in enumerate(((v0, s0), (v1, s1))):
        mk = _lane_mask(GLA_DK * h, GLA_DK * (h + 1))
        qh = qd * mk
        att = jnp.where(tril, _dot_nt(qh, ki), 0.0)
        outs.append(_dot_nn(att, v) + _dot_nt(qh, s))
        news.append(s * dec + _dot_tn(v, ke * mk))
    return outs[0], outs[1], news[0], news[1]


def _gla_specs(tb, rev_nb=None):
    blk = (lambda b: b) if rev_nb is None else (lambda b: rev_nb - 1 - b)
    q = pl.BlockSpec((tb, 128), lambda p, b: (blk(b), P_GQ // 128 + p))
    k = pl.BlockSpec((tb, 128), lambda p, b: (blk(b), P_GK // 128 + p))
    la = pl.BlockSpec((tb, 128), lambda p, b: (blk(b), p))
    v = pl.BlockSpec((tb, 256), lambda p, b: (blk(b), P_GV // 256 + p))
    o = pl.BlockSpec((tb, 256), lambda p, b: (blk(b), p))
    st = pl.BlockSpec((tb // GLA_CHUNK, 2, 128, 128), lambda p, b: (blk(b), p, 0, 0))
    return q, k, la, v, o, st


def _gla_fwd(proj, la):
    s = proj.shape[0]
    tb = _tile(s, ROW_TILE)
    nb, nch = s // tb, tb // GLA_CHUNK

    def kern(q_ref, k_ref, la_ref, v_ref, o_ref, st_ref, s_sc):
        @pl.when(pl.program_id(1) == 0)
        def _():
            s_sc[...] = jnp.zeros_like(s_sc)

        s0, s1 = s_sc[0], s_sc[1]
        for ci in range(nch):
            sl = slice(ci * GLA_CHUNK, (ci + 1) * GLA_CHUNK)
            st_ref[ci, 0] = s0
            st_ref[ci, 1] = s1
            o0, o1, s0, s1 = _gla_chunk(q_ref[sl, :], k_ref[sl, :], la_ref[sl, :], v_ref[sl, 0:128],
                                        v_ref[sl, 128:256], s0, s1)
            o_ref[sl, 0:128] = o0
            o_ref[sl, 128:256] = o1
        s_sc[0] = s0
        s_sc[1] = s1

    q, k, lasp, v, o, st = _gla_specs(tb)
    return pl.pallas_call(
        kern, grid=(2, nb), in_specs=[q, k, lasp, v], out_specs=[o, st],
        out_shape=[jax.ShapeDtypeStruct((s, 512), F32),
                   jax.ShapeDtypeStruct((s // GLA_CHUNK, GLA_HEADS, 128, 128), F32)],
        scratch_shapes=[pltpu.VMEM((2, 128, 128), F32)],
        compiler_params=pltpu.CompilerParams(dimension_semantics=("parallel", "arbitrary"),
                                             vmem_limit_bytes=VMEM_LIMIT),
        name="gla_fwd")(proj, proj, la, proj)


def _gla_bwd(proj, la, states, d_o):
    s = proj.shape[0]
    tb = _tile(s, ROW_TILE)
    nb, nch = s // tb, tb // GLA_CHUNK

    def kern(q_ref, k_ref, la_ref, v_ref, do_ref, st_ref, dq_ref, dk_ref, dla_ref, dv_ref, ds_sc):
        @pl.when(pl.program_id(1) == 0)
        def _():
            ds_sc[...] = jnp.zeros_like(ds_sc)

        d0, d1 = ds_sc[0], ds_sc[1]
        for ci in reversed(range(nch)):
            sl = slice(ci * GLA_CHUNK, (ci + 1) * GLA_CHUNK)
            _, vjp = jax.vjp(_gla_chunk, q_ref[sl, :], k_ref[sl, :], la_ref[sl, :], v_ref[sl, 0:128],
                             v_ref[sl, 128:256], st_ref[ci, 0], st_ref[ci, 1])
            gq, gk, gla, gv0, gv1, d0, d1 = vjp((do_ref[sl, 0:128], do_ref[sl, 128:256], d0, d1))
            dq_ref[sl, :] = gq
            dk_ref[sl, :] = gk
            dla_ref[sl, :] = gla
            dv_ref[sl, 0:128] = gv0
            dv_ref[sl, 128:256] = gv1
        ds_sc[0] = d0
        ds_sc[1] = d1

    q, k, lasp, v, o, st = _gla_specs(tb, rev_nb=nb)
    return pl.pallas_call(
        kern, grid=(2, nb), in_specs=[q, k, lasp, v, o, st], out_specs=[lasp, lasp, lasp, o],
        out_shape=[jax.ShapeDtypeStruct((s, 256), F32), jax.ShapeDtypeStruct((s, 256), F32),
                   jax.ShapeDtypeStruct((s, 256), F32), jax.ShapeDtypeStruct((s, 512), F32)],
        scratch_shapes=[pltpu.VMEM((2, 128, 128), F32)],
        compiler_params=pltpu.CompilerParams(dimension_semantics=("parallel", "arbitrary"),
                                             vmem_limit_bytes=VMEM_LIMIT),
        name="gla_bwd")(proj, proj, la, proj, d_o, states)


def _causal_keep(t):
    return lax.broadcasted_iota(jnp.int32, (t, t), 1) <= lax.broadcasted_iota(jnp.int32, (t, t), 0)


def _split_refs(refs, counts):
    out, off = [], 0
    for cnt in counts:
        out.append(refs[off:off + cnt])
        off += cnt
    return out


def _causal_blocks(n, key_major):
    pairs = ([(ki, qi) for ki in range(n) for qi in range(ki, n)] if key_major else
             [(ki, qi) for qi in range(n) for ki in range(qi + 1)])
    return np.array([ki for ki, _ in pairs], np.int32), np.array([qi for _, qi in pairs], np.int32)


def _attn_fwd(q, k, v, comm, tile=1024):
    s = q.shape[0]
    t = _tile(s, tile)
    n = s // t
    nci, nco = len(comm.ins), len(comm.out_shape)

    ki_tab, qi_tab = _causal_blocks(n, key_major=False)
    steps = len(ki_tab)

    def kern(ki_ref, qi_ref, *refs):
        (q_ref, k_ref, v_ref), cins, (o_ref, lse_ref), couts, (m_sc, l_sc, acc_sc), csems = _split_refs(
            refs, (3, nci, 2, nco, 3, len(comm.sems)))
        pair, step = pl.program_id(0), pl.program_id(1)
        qi, ki = qi_ref[step], ki_ref[step]
        place = _place()

        @pl.when((pair == 0) & (step == 0))
        def _():
            comm.start(place, cins, couts, csems)

        @pl.when((pair == MLA_HEADS // 2 - 1) & (step == 0))
        def _():
            comm.mid(place, cins, couts, csems)

        first = lax.broadcasted_iota(jnp.int32, (t, LANES), 1) < MLA_V

        @pl.when(ki == 0)
        def _():
            m_sc[...] = jnp.full_like(m_sc, -jnp.inf)
            l_sc[...] = jnp.zeros_like(l_sc)
            acc_sc[...] = jnp.zeros_like(acc_sc)

        def update(diagonal):
            keep = _causal_keep(t)
            alphas, pvs = [], []
            for h in range(2):
                sc = _dg(q_ref[:, 128 * h:128 * (h + 1)], k_ref[:, 128 * h:128 * (h + 1)], _NT)
                if diagonal:
                    sc = jnp.where(keep, sc, -jnp.inf)
                m_prev = m_sc[h]
                m_new = jnp.maximum(m_prev, jnp.max(sc, axis=1, keepdims=True))
                alpha = jnp.exp2(m_prev - m_new)
                p = jnp.exp2(sc - m_new[:, 0:1])
                l_sc[h] = alpha * l_sc[h] + jnp.sum(p, axis=1, keepdims=True)
                m_sc[h] = m_new
                alphas.append(alpha)
                pvs.append(_dg(p, v_ref[...], _NN))
            acc_sc[...] = acc_sc[...] * jnp.where(first, alphas[0], alphas[1]) + jnp.where(first, pvs[0], pvs[1])

        @pl.when(ki < qi)
        def _():
            update(False)

        @pl.when(ki == qi)
        def _():
            update(True)

        @pl.when(ki == qi)
        def _():
            l = jnp.where(first, l_sc[0], l_sc[1])
            m = jnp.where(first, m_sc[0], m_sc[1])
            o_ref[...] = acc_sc[...] / l
            lse_ref[...] = m + jnp.log2(l)

        @pl.when((pair == MLA_HEADS // 2 - 1) & (step == steps - 1))
        def _():
            comm.finish(place, cins, couts, csems)

    q_idx = lambda p, st, ki_r, qi_r: (qi_r[st], p)
    k_idx = lambda p, st, ki_r, qi_r: (ki_r[st], p)
    res = pl.pallas_call(
        kern, grid_spec=pltpu.PrefetchScalarGridSpec(
            num_scalar_prefetch=2, grid=(MLA_HEADS // 2, steps),
            in_specs=[pl.BlockSpec((t, 256), q_idx), pl.BlockSpec((t, 256), k_idx), pl.BlockSpec((t, 128), k_idx)]
            + [ANY] * nci,
            out_specs=[pl.BlockSpec((t, 128), q_idx), pl.BlockSpec((t, 128), q_idx)] + [ANY] * nco,
            scratch_shapes=[pltpu.VMEM((2, t, LANES), F32), pltpu.VMEM((2, t, LANES), F32),
                            pltpu.VMEM((t, LANES), F32)] + comm.sems),
        out_shape=[jax.ShapeDtypeStruct((s, 512), F32), jax.ShapeDtypeStruct((s, 512), F32)] + comm.out_shape,
        compiler_params=pltpu.CompilerParams(dimension_semantics=("arbitrary", "arbitrary"),
                                             vmem_limit_bytes=VMEM_LIMIT),
        name="mla_attn_fwd")(ki_tab, qi_tab, q, k, v, *comm.ins)
    return res[0], res[1], res[2:]


def _attn_bwd(q, k, v, o, lse, d_o, comm, tile=ROW_TILE):
    s = q.shape[0]
    t = _tile(s, tile)
    n = s // t
    nci, nco = len(comm.ins), len(comm.out_shape)

    ki_tab, qi_tab = _causal_blocks(n, key_major=True)
    steps = len(ki_tab)

    def kern(ki_ref, qi_ref, *refs):
        (q_ref, k_ref, v_ref, o_ref, lse_ref, do_ref), cins, (dq_ref, dk_ref, dv_ref), couts, (dk_sc, dv_sc), csems = \
            _split_refs(refs, (6, nci, 3, nco, 2, len(comm.sems)))
        pair, step = pl.program_id(0), pl.program_id(1)
        ki, qi = ki_ref[step], qi_ref[step]
        place = _place()

        @pl.when((pair == 0) & (step == 0))
        def _():
            comm.start(place, cins, couts, csems)

        @pl.when((pair == MLA_HEADS // 2 - 1) & (step == 0))
        def _():
            comm.mid(place, cins, couts, csems)

        @pl.when((ki == 0) & (qi == 0))
        def _():
            dq_ref[...] = jnp.zeros_like(dq_ref)

        @pl.when(qi == ki)
        def _():
            dk_sc[...] = jnp.zeros_like(dk_sc)
            dv_sc[...] = jnp.zeros_like(dv_sc)

        def update(diagonal):
            keep = _causal_keep(t)
            d_o = do_ref[...]
            prod = d_o * o_ref[...]
            rows = pl.ds(pl.multiple_of(qi * t, t), t)
            for h in range(2):
                hs = slice(128 * h, 128 * (h + 1))
                mk = _lane_mask(MLA_V * h, MLA_V * (h + 1))
                qh, kh = q_ref[:, hs], k_ref[:, hs]
                sc = _dg(qh, kh, _NT)
                if diagonal:
                    sc = jnp.where(keep, sc, -jnp.inf)
                p = jnp.exp2(sc - lse_ref[:, MLA_V * h:MLA_V * h + 1])
                doh = d_o * mk
                dp = _dg(doh * LN2, v_ref[...], _NT)
                delta = jnp.sum(prod * mk, axis=1, keepdims=True) * LN2
                ds = p * (dp - delta)
                dv_sc[...] += _dg(p, doh, _TN)
                dk_sc[:, hs] += _dg(ds, qh, _TN)
                dq_ref[rows, hs] += _dg(ds, kh, _NN)

        @pl.when(qi > ki)
        def _():
            update(False)

        @pl.when(qi == ki)
        def _():
            update(True)

        @pl.when(qi == n - 1)
        def _():
            dk_ref[...] = dk_sc[...]
            dv_ref[...] = dv_sc[...].astype(dv_ref.dtype)

        @pl.when((pair == MLA_HEADS // 2 - 1) & (step == steps - 1))
        def _():
            comm.finish(place, cins, couts, csems)

    q_idx = lambda p, st, ki_r, qi_r: (qi_r[st], p)
    k_idx = lambda p, st, ki_r, qi_r: (ki_r[st], p)
    res = pl.pallas_call(
        kern, grid_spec=pltpu.PrefetchScalarGridSpec(
            num_scalar_prefetch=2, grid=(MLA_HEADS // 2, steps),
            in_specs=[pl.BlockSpec((t, 256), q_idx), pl.BlockSpec((t, 256), k_idx), pl.BlockSpec((t, 128), k_idx),
                      pl.BlockSpec((t, 128), q_idx), pl.BlockSpec((t, 128), q_idx), pl.BlockSpec((t, 128), q_idx)]
            + [ANY] * nci,
            out_specs=[pl.BlockSpec((s, 256), lambda p, st, ki_r, qi_r: (0, p)), pl.BlockSpec((t, 256), k_idx),
                       pl.BlockSpec((t, 128), k_idx)] + [ANY] * nco,
            scratch_shapes=[pltpu.VMEM((t, 256), F32), pltpu.VMEM((t, 128), F32)] + comm.sems),
        out_shape=[jax.ShapeDtypeStruct((s, 1024), F32), jax.ShapeDtypeStruct((s, 1024), F32),
                   jax.ShapeDtypeStruct((s, 512), BF16)] + comm.out_shape,
        compiler_params=pltpu.CompilerParams(dimension_semantics=("arbitrary", "arbitrary"),
                                             vmem_limit_bytes=VMEM_LIMIT),
        name="mla_attn_bwd")(ki_tab, qi_tab, q, k, v, o, lse, d_o, *comm.ins)
    return res[0], res[1], res[2], res[3:]


def _gate_fn(alr, w2, b):
    return _log_sigmoid(_dot_nn(alr, w2) + b) * (1.0 / GLA_GATE_NORM)


def _qk_head(qh, kh, kpe, c, sa, sb, qn, kn):
    kfull = kh + kpe * _lane_mask(MLA_NOPE, MLA_QK)
    q_r = _rope(_rms(qh, qn, MLA_QK), c, sa, sb) * (MLA_QK ** -0.5 * LOG2E)
    k_r = _rope(_rms(kfull, kn, MLA_QK), c, sa, sb)
    return q_r, k_r


def _mix_head(o, og, gn):
    return _rms(o, gn) * _silu(og)


def _xa_head(xq, xk, xv, qn, kn):
    sc = _dot_nt(_rms(xq, qn), _rms(xk, kn)) * (XA_DIM ** -0.5)
    e = jnp.exp(sc - lax.stop_gradient(jnp.max(sc, axis=1, keepdims=True)))
    p = e / jnp.sum(e, axis=1, keepdims=True)
    return _dot_nn(p, xv)


def _heads(x, n):
    return [x[:, 128 * h:128 * (h + 1)] for h in range(n)]


def _cat(xs):
    return jnp.concatenate(xs, axis=1)


def _norm_fwd(x, w, name):
    return _rows_call(lambda r, c: ([_rms(r[0], c[0])], []), [_row(x)], [w], [(x.shape[1], BF16)], name=name)[0]


def _norm_fwd_epilogue(w):
    return _Epilogue(lambda h, rows, consts: ([h, _rms(h, consts[0])], []), [], [w], [(D_MODEL, F32), (D_MODEL, BF16)], [])


def _norm_bwd_epilogue(x, w, add):
    def fn(d_out, rows, consts):
        _, vjp = jax.vjp(_rms, rows[0], consts[0])
        dx, dw = vjp(d_out)
        return [dx + rows[1]], [dw]

    return _Epilogue(fn, [_row(x), _row(add)], [w], [(D_MODEL, F32)], [w.shape])


def _norm_fwd_comm(x, w, comm, name):
    s, d = x.shape
    t = _tile(s, ROW_TILE)
    n = s // t
    nci, nco = len(comm.ins), len(comm.out_shape)

    def kern(*refs):
        (x_ref, w_ref), cins, (o_ref,), couts, csems = _split_refs(refs, (2, nci, 1, nco, len(comm.sems)))
        place = _place()

        @pl.when(pl.program_id(0) == 0)
        def _():
            comm.start(place, cins, couts, csems)

        o_ref[...] = _rms(x_ref[...], w_ref[...]).astype(o_ref.dtype)

        @pl.when(pl.program_id(0) == n - 1)
        def _():
            comm.mid(place, cins, couts, csems)
            comm.finish(place, cins, couts, csems)

    tile = pl.BlockSpec((t, d), lambda i: (i, 0))
    res = pl.pallas_call(
        kern, grid=(n,), in_specs=[tile, pl.BlockSpec(w.shape, lambda i: (0, 0))] + [ANY] * nci,
        out_specs=[tile] + [ANY] * nco, out_shape=[jax.ShapeDtypeStruct((s, d), BF16)] + comm.out_shape,
        scratch_shapes=comm.sems,
        compiler_params=pltpu.CompilerParams(dimension_semantics=("arbitrary",), vmem_limit_bytes=VMEM_LIMIT),
        name=name)(x, w, *comm.ins)
    return res[0], res[1:]


def _norm_bwd(x, w, d_out, add, name):
    def body(r, c):
        _, vjp = jax.vjp(_rms, r[0], c[0])
        dx, dw = vjp(r[1])
        return [dx + r[2]], [dw]

    return _rows_call(body, [_row(x), _row(d_out), _row(add)], [w], [(x.shape[1], F32)], [w.shape], name=name)


CONV_HALO = BF16_ROWS


def _conv_specs(s, f, t):
    n8 = t // CONV_HALO
    cur = pl.BlockSpec((None, t, f), lambda j, i: (j, i, 0))
    prev = pl.BlockSpec((None, CONV_HALO, f), lambda j, i: (j, jnp.maximum(i * n8 - 1, 0), 0))
    nxt = pl.BlockSpec((None, CONV_HALO, f), lambda j, i: (j, jnp.minimum((i + 1) * n8, s // CONV_HALO - 1), 0))
    cw = pl.BlockSpec((None, 3, f), lambda j, i: (j, 0, 0))
    cb = pl.BlockSpec((None, 1, f), lambda j, i: (j, 0, 0))
    return cur, prev, nxt, cw, cb


def _conv_taps(g, prev, first):
    ext = jnp.concatenate([jnp.where(first, 0.0, prev.astype(F32)), g], axis=0)
    return pltpu.roll(ext, 1, 0)[CONV_HALO:], pltpu.roll(ext, 2, 0)[CONV_HALO:]


def _conv_fwd(gg, uu, cw, cb):
    _, s, f = gg.shape
    t = _tile(s, ROW_TILE)

    def kern(g_ref, gp_ref, u_ref, cw_ref, cb_ref, o_ref):
        g = g_ref[...].astype(F32)
        g1, g2 = _conv_taps(g, gp_ref[...], pl.program_id(1) == 0)
        w = cw_ref[...]
        gc = cb_ref[...] + w[0:1] * g2 + w[1:2] * g1 + w[2:3] * g
        o_ref[...] = (_silu(gc) * u_ref[...].astype(F32)).astype(o_ref.dtype)

    cur, prev, _, cws, cbs = _conv_specs(s, f, t)
    return pl.pallas_call(
        kern, grid=(4, s // t), in_specs=[cur, prev, cur, cws, cbs], out_specs=cur,
        out_shape=jax.ShapeDtypeStruct(gg.shape, BF16),
        compiler_params=pltpu.CompilerParams(dimension_semantics=("parallel", "parallel"), vmem_limit_bytes=VMEM_LIMIT),
        name="ffn_conv_fwd")(gg, gg, uu, cw, cb)


def _conv_bwd(gg, uu, dact, cw, cb):
    _, s, f = gg.shape
    t = _tile(s, ROW_TILE)
    nt = s // t

    def kern(g_ref, gp_ref, gn_ref, u_ref, un_ref, da_ref, dan_ref, cw_ref, cb_ref, du_ref, dg_ref, dcw_ref, dcb_ref):
        i = pl.program_id(1)
        cat = lambda a_ref, b_ref: jnp.concatenate([a_ref[...].astype(F32), b_ref[...].astype(F32)], axis=0)
        g, u, da = cat(g_ref, gn_ref), cat(u_ref, un_ref), cat(da_ref, dan_ref)
        g1, g2 = _conv_taps(g, gp_ref[...], i == 0)
        w = cw_ref[...]
        gc = cb_ref[...] + w[0:1] * g2 + w[1:2] * g1 + w[2:3] * g
        sg = jax.nn.sigmoid(gc)
        du_ref[...] = (da[:t] * (gc[:t] * sg[:t])).astype(du_ref.dtype)
        row = lax.broadcasted_iota(jnp.int32, (t + CONV_HALO, 1), 0)
        dgc = jnp.where((row < t) | (i < nt - 1), da * u * (sg * (1.0 + gc * (1.0 - sg))), 0.0)
        up1 = pltpu.roll(dgc, t + CONV_HALO - 1, 0)[:t]
        up2 = pltpu.roll(dgc, t + CONV_HALO - 2, 0)[:t]
        dgc = dgc[:t]
        dg_ref[...] = (w[2:3] * dgc + w[1:2] * up1 + w[0:1] * up2).astype(dg_ref.dtype)

        @pl.when(i == 0)
        def _():
            dcw_ref[...] = jnp.zeros_like(dcw_ref)
            dcb_ref[...] = jnp.zeros_like(dcb_ref)

        dcw_ref[0:1, :] += jnp.sum(dgc * g2[:t], axis=0, keepdims=True)
        dcw_ref[1:2, :] += jnp.sum(dgc * g1[:t], axis=0, keepdims=True)
        dcw_ref[2:3, :] += jnp.sum(dgc * g[:t], axis=0, keepdims=True)
        dcb_ref[...] += jnp.sum(dgc, axis=0, keepdims=True)

    cur, prev, nxt, cws, cbs = _conv_specs(s, f, t)
    return pl.pallas_call(
        kern, grid=(4, nt), in_specs=[cur, prev, nxt, cur, nxt, cur, nxt, cws, cbs], out_specs=[cur, cur, cws, cbs],
        out_shape=[jax.ShapeDtypeStruct(gg.shape, BF16), jax.ShapeDtypeStruct(gg.shape, BF16),
                   jax.ShapeDtypeStruct(cw.shape, F32), jax.ShapeDtypeStruct(cb.shape, F32)],
        compiler_params=pltpu.CompilerParams(dimension_semantics=("parallel", "arbitrary"), vmem_limit_bytes=VMEM_LIMIT),
        name="ffn_conv_bwd")(gg, gg, gg, uu, uu, dact, dact, cw, cb)


def _rope_tables(pos):
    half = MLA_ROPE // 2
    lane = jnp.arange(LANES)
    rotary = (lane >= MLA_NOPE) & (lane < MLA_QK)
    inv = jnp.where(rotary, ROPE_THETA ** (-((lane - MLA_NOPE) % half).astype(F32) / half), 0.0)
    ang = pos.astype(F32)[:, None] * inv
    cos, sin = jnp.cos(ang), jnp.sin(ang)
    first = rotary & (lane < MLA_NOPE + half)
    return cos, jnp.where(first, -sin, 0.0), jnp.where(rotary & ~first, sin, 0.0)


def _local_step(x, mem, pos, target, rep, early_shards, late_shards):
    g = {}
    c, sa, sb = _rope_tables(pos)

    xn, gathered = _norm_fwd_comm(x, rep["norm_mix"], _gather_plan(early_shards), "norm_mix_fwd_gather")
    w = _early_layout(dict(zip(EARLY, gathered, strict=True)), rep)

    def proj_fn(r, rows, k):
        la_ = _gate_fn(r[:, P_ALR:P_ALR + 128], k[0], k[1])
        return [r, la_, _rms(r[:, P_CQ:P_CQ + MLA_Q_RANK], k[2]), _rms(r[:, P_CKV:P_CKV + MLA_KV_RANK], k[3])], []

    proj, la, q_lat, kv_lat = _matmul(
        xn, w["in"], "nt", F32, "proj_fwd", epilogue=_Epilogue(
            proj_fn, [], [w["w2"], w["gate_b"], w["q_a_norm"], w["kv_a_norm"]],
            [(P_WIDTH, F32), (256, F32), (MLA_Q_RANK, BF16), (MLA_KV_RANK, BF16)], []))
    alr = _row(proj, 128, P_ALR // 128)
    kpe = _row(proj, 128, P_KPE // 128)
    og = _row(proj, 512, P_OG // 512)
    cq = _row(proj, 256, P_CQ // 256)
    ckv = _row(proj, 128, P_CKV // 128)

    o_gla, states = _gla_fwd(proj, la)

    def qk_body(r, k):
        q_up, k_up = _dg(r[0], k[0], _NN), _dg(r[1], k[1], _NN)
        qs, ks = [], []
        for qh, kh in zip(_heads(q_up, MLA_HEADS), _heads(k_up, MLA_HEADS)):
            a, b = _qk_head(qh, kh, r[2], r[3], r[4], r[5], k[3], k[4])
            qs.append(a)
            ks.append(b)
        return [_cat(qs), _cat(ks), _dg(r[1], k[2], _NN)], []

    tabs = [_row(c), _row(sa), _row(sb)]
    qk_consts = [w["uq"], w["k"], w["v"], w["q_norm"], w["k_norm"]]
    q_r, k_r, v_mla = _rows_call(qk_body, [_row(q_lat), _row(kv_lat), kpe] + tabs, qk_consts,
                                 [(1024, BF16), (1024, BF16), (512, BF16)], name="mla_qk_fwd")
    o_mla, lse, gathered = _attn_fwd(q_r, k_r, v_mla, _gather_plan(late_shards))
    w.update(_late_layout(dict(zip(LATE, gathered, strict=True))))

    def mix_body(r, k):
        ys = [_mix_head(o, g_, k[0]) for o, g_ in zip(_heads(r[0], GLA_HEADS), _heads(r[1], GLA_HEADS))]
        return [_cat(ys + [r[2]])], []

    cat = _rows_call(mix_body, [_row(o_gla), og, _row(o_mla)], [w["gla_out_norm"]], [(1024, BF16)],
                     name="mix_fwd")[0]
    h1, hn = _matmul(cat, w["out"], "nn", F32, "out_fwd_norm", residual=x, epilogue=_norm_fwd_epilogue(w["norm_xa"]))
    mn = _norm_fwd(mem, w["norm_mem"], "norm_mem_fwd")
    xkv = _matmul(mn, w["xkv"], "nn", F32, "xa_kv_fwd")

    def xa_fn(r, rows, k):
        ks, vs = _heads(k[0], 2 * XA_HEADS)[:XA_HEADS], _heads(k[0], 2 * XA_HEADS)[XA_HEADS:]
        return [r, _cat([_xa_head(a, b, v_, k[1], k[2]) for a, b, v_ in zip(_heads(r, XA_HEADS), ks, vs)])], []

    xq, xo = _matmul(hn, w["xq"], "nn", F32, "xa_q_fwd_attn", epilogue=_Epilogue(
        xa_fn, [], [xkv, w["xa_q_norm"], w["xa_k_norm"]], [(512, F32), (512, BF16)], []))
    h2, fn = _matmul(xo, w["xo"], "nn", F32, "xa_o_fwd_norm", residual=h1, epilogue=_norm_fwd_epilogue(w["norm_ffn"]))
    gg = _matmul(fn, w["wg"], "nt", BF16, "ffn_gate_fwd", b_lead="p")
    uu = _matmul(fn, w["wu"], "nt", BF16, "ffn_up_fwd", b_lead="p")
    act = _conv_fwd(gg, uu, w["cw"], w["cb"])
    def loss_fn(y, rows, consts):
        err = y - rows[0]
        part = 0.5 * jnp.sum(jnp.sum(err * err, axis=1, keepdims=True) * (1.0 / D_MODEL), axis=0, keepdims=True)
        return [err * (1.0 / D_MODEL)], [jnp.broadcast_to(part, (1, LANES))]

    dy, loss = _matmul(act, w["wd"], "nn", F32, "ffn_down_fwd_loss", residual=h2, a_lead="k", b_lead="k",
                       epilogue=_Epilogue(loss_fn, [_row(target)], [], [(D_MODEL, F32)], [(1, LANES)]))

    g["ffn_w_down"] = _matmul(act, dy, "tn", BF16, "ffn_down_dw", a_lead="p")
    dact = _matmul(dy, w["wd"], "nt", BF16, "ffn_down_dx", b_lead="p")
    duu, dgg, g["ffn_conv_w"], g["ffn_conv_b"] = _conv_bwd(gg, uu, dact, w["cw"], w["cb"])
    g["ffn_w_gate"] = _matmul(dgg, fn, "tn", BF16, "ffn_gate_dw", a_lead="p")
    g["ffn_w_up"] = _matmul(duu, fn, "tn", BF16, "ffn_up_dw", a_lead="p")
    dh2, g["norm_ffn"] = _matmul(dgg, w["wg"], "nn", F32, "ffn_dx_norm_bwd", a_lead="k", b_lead="k", more=(duu, w["wu"]),
                                 epilogue=_norm_bwd_epilogue(h2, w["norm_ffn"], dy))

    g["xa_w_o"] = _matmul(xo, dh2, "tn", BF16, "xa_o_dw")
    def xa_bwd(dxo_, rows, k):
        kvh = _heads(k[0], 2 * XA_HEADS)
        dq_, dk_, dv_ = [], [], []
        dqn, dkn = 0.0, 0.0
        for h, (a, d_) in enumerate(zip(_heads(rows[0], XA_HEADS), _heads(dxo_, XA_HEADS))):
            _, vjp = jax.vjp(_xa_head, a, kvh[h], kvh[XA_HEADS + h], k[1], k[2])
            ga, gk, gv, gqn, gkn = vjp(d_)
            dq_.append(ga)
            dk_.append(gk)
            dv_.append(gv)
            dqn, dkn = dqn + gqn, dkn + gkn
        return [_cat(dq_)], [_cat(dk_ + dv_), dqn, dkn]

    dxq, dxkv, g["xa_q_norm"], g["xa_k_norm"] = _matmul(dh2, w["xo"], "nt", F32, "xa_o_dx_attn_bwd", epilogue=_Epilogue(
        xa_bwd, [_row(xq)], [xkv, w["xa_q_norm"], w["xa_k_norm"]], [(512, BF16)], [xkv.shape, (1, 128), (1, 128)]))
    g["xa_w_q"] = _matmul(hn, dxq, "tn", BF16, "xa_q_dw")
    dh1, g["norm_xa"] = _matmul(dxq, w["xq"], "nt", F32, "xa_q_dx_norm_bwd",
                                epilogue=_norm_bwd_epilogue(h1, w["norm_xa"], dh2))
    g["xa_w_kv"] = _matmul(mn, dxkv, "tn", BF16, "xa_kv_dw")
    dmn = _matmul(dxkv, w["xkv"], "nt", F32, "xa_kv_dx")
    _, g["norm_mem"] = _norm_bwd(mem, w["norm_mem"], dmn, dmn, "norm_mem_bwd")

    g["w_out"] = _matmul(cat, dh1, "tn", BF16, "out_dw")
    def mix_bwd(dcat_, rows, k):
        do_, dog_ = [], []
        dgn = 0.0
        for o, g_, d_ in zip(_heads(rows[0], GLA_HEADS), _heads(rows[1], GLA_HEADS), _heads(dcat_, GLA_HEADS)):
            _, vjp = jax.vjp(_mix_head, o, g_, k[0])
            a, b, gn_ = vjp(d_)
            do_.append(a)
            dog_.append(b)
            dgn = dgn + gn_
        return [_cat(do_), _cat(dog_), dcat_[:, 512:]], [dgn]

    do_gla, d_og, do_mla, g["gla_out_norm"] = _matmul(dh1, w["out"], "nt", F32, "out_dx_mix_bwd", epilogue=_Epilogue(
        mix_bwd, [_row(o_gla), og], [w["gla_out_norm"]], [(512, F32), (512, BF16), (512, F32)], [(1, 128)]))

    late_parts = _late_grad_shards(g)
    dq_r, dk_r, dv_mla, lands_late = _attn_bwd(q_r, k_r, v_mla, o_mla, lse, do_mla,
                                               _scatter_plan([late_parts[n] for n in LATE]))
    lands_late = dict(zip(LATE, lands_late, strict=True))

    def qk_bwd(r, k):
        q_up, k_up = _dg(r[0], k[0], _NN), _dg(r[1], k[1], _NN)
        dqs, dks = [], []
        dkpe, dqn, dkn = 0.0, 0.0, 0.0
        for qh, kh, dqh, dkh in zip(_heads(q_up, MLA_HEADS), _heads(k_up, MLA_HEADS), _heads(r[6], MLA_HEADS),
                                    _heads(r[7], MLA_HEADS)):
            _, vjp = jax.vjp(lambda a, b, e, f, h_: _qk_head(a, b, e, r[3], r[4], r[5], f, h_), qh, kh, r[2], k[3], k[4])
            ga, gb, ge, gf, gh = vjp((dqh, dkh))
            dqs.append(ga)
            dks.append(gb)
            dkpe, dqn, dkn = dkpe + ge, dqn + gf, dkn + gh
        dq_up, dk_up, dv = _cat(dqs), _cat(dks), r[8]
        dq_lat_ = _dg(dq_up, k[0], _NT)
        dkv_lat_ = _dg(dk_up, k[1], _NT) + _dg(dv, k[2], _NT)
        return [dq_lat_, dkv_lat_, dkpe], [dqn, dkn, _dg(r[0], dq_up, _TN), _dg(r[1], dk_up, _TN), _dg(r[1], dv, _TN)]

    dq_lat, dkv_lat, d_kpe, g["q_norm"], g["k_norm"], g["uq"], g["k"], g["v"] = _rows_call(
        qk_bwd, [_row(q_lat), _row(kv_lat), kpe] + tabs + [_row(dq_r), _row(dk_r), _row(dv_mla)], qk_consts,
        [(MLA_Q_RANK, F32), (MLA_KV_RANK, F32), (128, BF16)],
        [(1, 128), (1, 128), w["uq"].shape, w["k"].shape, w["v"].shape], name="mla_qk_bwd")

    dgq, dgk, dla, dgv = _gla_bwd(proj, la, states, do_gla)

    def dproj_body(r, k):
        alr_, cq_, ckv_, dla_, dq_lat_, dkv_lat_, dgq_, dgk_, dgv_, d_og_, d_kpe_ = r
        _, gate_vjp = jax.vjp(_gate_fn, alr_, k[0], k[1])
        d_alr, gw2, gb = gate_vjp(dla_)
        _, q_vjp = jax.vjp(_rms, cq_, k[2])
        _, kv_vjp = jax.vjp(_rms, ckv_, k[3])
        d_cq, gqa = q_vjp(dq_lat_)
        d_ckv, gkva = kv_vjp(dkv_lat_)
        pieces = [dgq_, dgk_, dgv_, d_og_, d_cq, d_ckv, d_kpe_, d_alr]
        return [_cat([x_.astype(BF16) for x_ in pieces])], [gw2, gb, gqa, gkva]

    dproj, g["w2"], g["gla_gate_b"], g["mla_q_a_norm"], g["mla_kv_a_norm"] = _rows_call(
        dproj_body, [alr, cq, ckv, _row(dla), _row(dq_lat), _row(dkv_lat), _row(dgq), _row(dgk), _row(dgv), _row(d_og),
                     _row(d_kpe)], [w["w2"], w["gate_b"], w["q_a_norm"], w["kv_a_norm"]], [(P_WIDTH, BF16)],
        [(128, 256), (1, 256), (1, 256), (1, 128)], name="proj_cotangent")
    g["in"] = _matmul(dproj, xn, "tn", BF16, "proj_dw")
    dx, g["norm_mix"] = _matmul(dproj, w["in"], "nn", F32, "proj_dx_norm_bwd",
                                epilogue=_norm_bwd_epilogue(x, w["norm_mix"], dh1))
    return loss[0, 0], dx, g, lands_late


def _join_shards(pieces, axis):
    if axis == 0:
        return pieces.reshape(-1, pieces.shape[2])
    return jnp.transpose(pieces, (1, 0, 2)).reshape(pieces.shape[1], -1)


def _split_shards(full, axis):
    r, c = full.shape
    if axis == 0:
        return full.reshape(4, r // 4, c)
    return jnp.transpose(full.reshape(r, 4, c // 4), (1, 0, 2))


def _early_layout(gath, rep):
    w_in = gath["w_in"].reshape(N_WIDTH, D_MODEL)
    z = lambda n: jnp.zeros((n, D_MODEL), w_in.dtype)
    seg = lambda lo, n: w_in[lo:lo + n]
    ukv = _join_shards(gath["mla_w_ukv"], 1).reshape(MLA_KV_RANK, MLA_HEADS, MLA_NOPE + MLA_V)
    w = {
        "in": jnp.concatenate([seg(N_GQ, 256), seg(N_GK, 256), seg(N_GV, 512), seg(N_OG, 512), seg(N_CQ, 256),
                               seg(N_CKV, 128), z(64), seg(N_KPE, 32), z(32), seg(N_ALR, 16), z(112)], axis=0),
        "uq": jnp.pad(_join_shards(gath["mla_w_uq"], 1).reshape(MLA_Q_RANK, MLA_HEADS, MLA_QK),
                      ((0, 0), (0, 0), (0, LANES - MLA_QK))).reshape(MLA_Q_RANK, MLA_HEADS * LANES),
        "k": jnp.pad(ukv[:, :, :MLA_NOPE], ((0, 0), (0, 0), (0, LANES - MLA_NOPE))).reshape(MLA_KV_RANK, -1),
        "v": ukv[:, :, MLA_NOPE:].reshape(MLA_KV_RANK, MLA_HEADS * MLA_V),
        "w2": jnp.pad(_join_shards(gath["gla_gate_w2"], 1), ((0, LANES - GLA_RANK), (0, 0))),
        "cb": rep["ffn_conv_b"].reshape(4, 1, D_FF // 4),
        "q_norm": jnp.pad(rep["mla_q_norm"], ((0, 0), (0, LANES - MLA_QK))),
        "k_norm": jnp.pad(rep["mla_k_norm"], ((0, 0), (0, LANES - MLA_QK))),
        "q_a_norm": rep["mla_q_a_norm"], "kv_a_norm": rep["mla_kv_a_norm"], "gate_b": rep["gla_gate_b"],
    }
    for n in ("norm_mix", "gla_out_norm", "norm_xa", "norm_mem", "xa_q_norm", "xa_k_norm", "norm_ffn"):
        w[n] = rep[n]
    return w


def _late_layout(gath):
    return {"out": _join_shards(gath["w_out"], 0), "xq": _join_shards(gath["xa_w_q"], 0),
            "xkv": _join_shards(gath["xa_w_kv"], 0), "xo": _join_shards(gath["xa_w_o"], 1),
            "wg": gath["ffn_w_gate"], "wu": gath["ffn_w_up"], "wd": gath["ffn_w_down"], "cw": gath["ffn_conv_w"]}


def _late_grad_shards(g):
    sh = {"w_out": _split_shards(g["w_out"], 0), "xa_w_q": _split_shards(g["xa_w_q"], 0),
          "xa_w_kv": _split_shards(g["xa_w_kv"], 0), "xa_w_o": _split_shards(g["xa_w_o"], 1),
          "ffn_w_gate": g["ffn_w_gate"], "ffn_w_up": g["ffn_w_up"], "ffn_conv_w": g["ffn_conv_w"],
          "ffn_w_down": g["ffn_w_down"]}
    return {n: v.astype(BF16) for n, v in sh.items()}


def _early_grad_shards(g):
    gi = g["in"]
    seg = lambda lo, n: gi[lo:lo + n]
    w_in = jnp.concatenate([seg(P_GQ, 256), seg(P_GK, 256), seg(P_GV, 512), seg(P_ALR, 16), seg(P_OG, 512),
                            seg(P_CQ, 256), seg(P_CKV, 128), seg(P_KPE + 64, 32)], axis=0)
    uq = g["uq"].reshape(MLA_Q_RANK, MLA_HEADS, LANES)[:, :, :MLA_QK].reshape(MLA_Q_RANK, -1)
    ukv = jnp.concatenate([g["k"].reshape(MLA_KV_RANK, MLA_HEADS, LANES)[:, :, :MLA_NOPE],
                           g["v"].reshape(MLA_KV_RANK, MLA_HEADS, MLA_V)], axis=2).reshape(MLA_KV_RANK, -1)
    sh = {"w_in": w_in.reshape(4, N_WIDTH // 4, D_MODEL), "gla_gate_w2": _split_shards(g["w2"][:GLA_RANK], 1),
          "mla_w_uq": _split_shards(uq, 1), "mla_w_ukv": _split_shards(ukv, 1)}
    sh = {n: v.astype(BF16) for n, v in sh.items()}
    rep = {n: g[n] for n in REPLICATED if n in g}
    rep["mla_q_norm"] = g["q_norm"][:, :MLA_QK]
    rep["mla_k_norm"] = g["k_norm"][:, :MLA_QK]
    rep["ffn_conv_b"] = g["ffn_conv_b"].reshape(1, D_FF)
    return sh, rep


SMALL_SHAPE = (8, 1024)


def _pack_small(vectors):
    flat = jnp.concatenate(vectors, axis=1)
    return jnp.pad(flat, ((0, 0), (0, SMALL_SHAPE[0] * SMALL_SHAPE[1] - flat.shape[1]))).reshape(SMALL_SHAPE)


def _unpack_small(buf, widths):
    flat = buf.reshape(1, -1)
    out, off = [], 0
    for wd in widths:
        out.append(flat[:, off:off + wd])
        off += wd
    return out


ANY = pl.BlockSpec(memory_space=pl.ANY)


def _place():
    x, y, c = lax.axis_index("x"), lax.axis_index("y"), lax.axis_index("c")
    chips = [(1 - x, y), (x, 1 - y), (1 - x, 1 - y)]
    return x, y, c, chips


class _Comm:
    def __init__(self, ins, out_shape, sems, start, finish, mid=None):
        self.ins, self.out_shape, self.sems = list(ins), list(out_shape), list(sems)
        self.start, self.finish, self.mid = start, finish, mid or (lambda *args: None)


def _run_comm(plan, name):
    ni, no = len(plan.ins), len(plan.out_shape)

    def body(*refs):
        ins, outs, sems = refs[:ni], refs[ni:ni + no], refs[ni + no:]
        place = _place()
        plan.start(place, ins, outs, sems)
        plan.mid(place, ins, outs, sems)
        plan.finish(place, ins, outs, sems)

    return pl.pallas_call(body, in_specs=[ANY] * ni, out_specs=[ANY] * no, out_shape=plan.out_shape,
                          scratch_shapes=plan.sems, name=name)(*plan.ins)


def _gather_plan(shards):
    n = len(shards)
    by_rows = [s.shape[0] % (2 * BF16_ROWS) == 0 for s in shards]
    by_cols = [not r and s.shape[1] % (2 * LANES) == 0 for r, s in zip(by_rows, shards)]
    split = [r or c for r, c in zip(by_rows, by_cols)]

    def rows(ref, t, c):
        if by_rows[t]:
            half = shards[t].shape[0] // 2
            return ref.at[pl.ds(pl.multiple_of(c * half, BF16_ROWS), half)]
        if by_cols[t]:
            half = shards[t].shape[1] // 2
            return ref.at[:, pl.ds(pl.multiple_of(c * half, LANES), half)]
        return ref

    def remote(src, dst, ss, rs, to):
        return pltpu.make_async_remote_copy(src_ref=src, dst_ref=dst, send_sem=ss, recv_sem=rs, device_id=to,
                                            device_id_type=MESH)

    def first_wave(place, ins, outs, sems):
        x, y, c, chips = place
        ici_s, ici_r, _, _, local = sems
        me = 2 * x + y
        own = [pltpu.make_async_copy(ins[t], outs[t].at[me], local.at[t]) for t in range(n)]
        push = [remote(rows(ins[t], t, c), rows(outs[t].at[me], t, c), ici_s.at[3 * t + j], ici_r.at[3 * t + j], (px, py, c))
                for t in range(n) for j, (px, py) in enumerate(chips)]
        return own, push

    def second_wave(place, ins, outs, sems, last):
        x, y, c, chips = place
        ici_s, ici_r, d2d_s, d2d_r, local = sems
        sib = (x, y, 1 - c)
        out = []
        for t in range(n):
            for j, (px, py) in enumerate(chips):
                block = outs[t].at[2 * px + py]
                got = rows(block, t, c)
                if split[t]:
                    hand = remote(got, got, d2d_s.at[3 * t + j], d2d_r.at[3 * t + j], sib)
                    theirs = rows(block, t, 1 - c)
                    other = (remote(theirs, theirs, local.at[0], d2d_r.at[3 * t + j], sib) if last else
                             remote(got, got, local.at[0], ici_r.at[3 * t + j], sib))
                    out.append((other, hand))
                elif last:
                    out.append((remote(got, got, local.at[0], ici_r.at[3 * t + j], sib), None))
        return out

    def start(place, ins, outs, sems):
        own, push = first_wave(place, ins, outs, sems)
        for cp in own + push:
            cp.start()

    def mid(place, ins, outs, sems):
        for arrival, hand in second_wave(place, ins, outs, sems, False):
            arrival.wait_recv()
            hand.start()

    def finish(place, ins, outs, sems):
        own, push = first_wave(place, ins, outs, sems)
        for arrival, hand in second_wave(place, ins, outs, sems, True):
            arrival.wait_recv()
            if hand is not None:
                hand.wait_send()
        for cp in push:
            cp.wait_send()
        for cp in own:
            cp.wait()

    dma = pltpu.SemaphoreType.DMA
    return _Comm(shards, [jax.ShapeDtypeStruct((4,) + s.shape, s.dtype) for s in shards],
                 [dma((3 * n,)), dma((3 * n,)), dma((3 * n,)), dma((3 * n,)), dma((n,))], start, finish, mid)


def _scatter_plan(parts, small=None):
    n = len(parts)
    ns = 0 if small is None else 1

    def unpack(place, ins, outs, sems):
        x, y, c, chips = place
        return x, y, c, chips, 2 * x + y, 4 * x + 2 * y + c, (x, y, 1 - c)

    def remote(src, dst, ss, rs, to):
        return pltpu.make_async_remote_copy(src_ref=src, dst_ref=dst, send_sem=ss, recv_sem=rs, device_id=to,
                                            device_id_type=MESH)

    def first_wave(place, ins, outs, sems):
        x, y, c, chips, me, dev, sib = unpack(place, ins, outs, sems)
        ici_s, ici_r, d2d_s, d2d_r, sm_s, sm_r, local = sems
        own, push = [], []
        if ns:
            own.append(pltpu.make_async_copy(ins[n], outs[n].at[dev], local.at[n]))
            for k in range(1, 8):
                px = (1 - x) if (k >> 2) & 1 else x
                py = (1 - y) if (k >> 1) & 1 else y
                pc = (1 - c) if k & 1 else c
                push.append(remote(ins[n], outs[n].at[dev], sm_s.at[k - 1], sm_r.at[k - 1], (px, py, pc)))
        for t in range(n):
            own.append(pltpu.make_async_copy(ins[t].at[me], outs[t].at[dev], local.at[t]))
            push.append(remote(ins[t].at[me], outs[t].at[dev], d2d_s.at[4 * t], d2d_r.at[4 * t], sib))
            for j, (px, py) in enumerate(chips):
                push.append(remote(ins[t].at[2 * px + py], outs[t].at[dev], ici_s.at[3 * t + j], ici_r.at[3 * t + j],
                                   (px, py, c)))
        return own, push

    def start(place, ins, outs, sems):
        own, push = first_wave(place, ins, outs, sems)
        for cp in own + push:
            cp.start()

    def landed(dst, rs, sems, sib):
        remote(dst, dst, sems[-1].at[0], rs, sib).wait_recv()

    def forwards(place, ins, outs, sems):
        x, y, c, chips, me, dev, sib = unpack(place, ins, outs, sems)
        d2d_s, d2d_r = sems[2], sems[3]
        slots = [(t, j, outs[t].at[4 * px + 2 * py + c]) for t in range(n) for j, (px, py) in enumerate(chips)]
        return [(t, j, slot, remote(slot, slot, d2d_s.at[4 * t + 1 + j], d2d_r.at[4 * t + 1 + j], sib))
                for t, j, slot in slots]

    def mid(place, ins, outs, sems):
        sib = unpack(place, ins, outs, sems)[-1]
        for t, j, slot, cp in forwards(place, ins, outs, sems):
            landed(slot, sems[1].at[3 * t + j], sems, sib)
            cp.start()

    def finish(place, ins, outs, sems):
        x, y, c, chips, me, dev, sib = unpack(place, ins, outs, sems)
        d2d_r, sm_r = sems[3], sems[5]
        own, push = first_wave(place, ins, outs, sems)
        push += [cp for _, _, _, cp in forwards(place, ins, outs, sems)]
        for t in range(n):
            landed(outs[t].at[4 * x + 2 * y + (1 - c)], d2d_r.at[4 * t], sems, sib)
            for j, (px, py) in enumerate(chips):
                landed(outs[t].at[4 * px + 2 * py + (1 - c)], d2d_r.at[4 * t + 1 + j], sems, sib)
        if ns:
            for k in range(1, 8):
                px = (1 - x) if (k >> 2) & 1 else x
                py = (1 - y) if (k >> 1) & 1 else y
                pc = (1 - c) if k & 1 else c
                landed(outs[n].at[4 * px + 2 * py + pc], sm_r.at[k - 1], sems, sib)
        for cp in push:
            cp.wait_send()
        for cp in own:
            cp.wait()

    dma = pltpu.SemaphoreType.DMA
    ins = list(parts) + ([small] if ns else [])
    out_shape = [jax.ShapeDtypeStruct((8,) + p.shape[1:], p.dtype) for p in parts]
    if ns:
        out_shape.append(jax.ShapeDtypeStruct((8,) + small.shape, small.dtype))
    return _Comm(ins, out_shape, [dma((3 * n,)), dma((3 * n,)), dma((4 * n,)), dma((4 * n,)), dma((7,)), dma((7,)),
                                  dma((n + 1,))], start, finish, mid)


ADAM_ROWS = 288


def _row_tile(r, cap):
    if r <= cap:
        return r
    return max((t for t in range(8, cap + 1, 8) if r % t == 0), default=r)


def _adamw_update(w, m, v, land):
    g = land[0].astype(F32)
    for i in range(1, 8):
        g = g + land[i].astype(F32)
    m_new = ADAM_B1 * m + (1.0 - ADAM_B1) * g
    v_new = ADAM_B2 * v + (1.0 - ADAM_B2) * (g * g)
    m_hat = m_new / (1.0 - ADAM_B1 ** ADAM_STEP)
    v_hat = v_new / (1.0 - ADAM_B2 ** ADAM_STEP)
    return g, -ADAM_LR * (m_hat / (jnp.sqrt(v_hat) + ADAM_EPS) + ADAM_WD * w), m_new, v_new


def _adamw(tensors, name, comm=None):
    k = len(tensors)
    r, c = tensors[0][0].shape
    t = _row_tile(r, ADAM_ROWS // k)
    tc = c if t < r or r <= ADAM_ROWS else 2 * LANES
    n = (r // t) * (c // tc)
    nci, nco, nsem = (len(comm.ins), len(comm.out_shape), len(comm.sems)) if comm else (0, 0, 0)

    def kern(*refs):
        ins, cins, outs, couts, csems = _split_refs(refs, (4 * k, nci, 4 * k, nco, nsem))
        if comm:
            place = _place()

            @pl.when(pl.program_id(0) == 0)
            def _():
                comm.start(place, cins, couts, csems)

        for i in range(k):
            w_ref, m_ref, v_ref, l_ref = ins[4 * i:4 * i + 4]
            res = _adamw_update(w_ref[...], m_ref[...], v_ref[...], l_ref)
            for ref, val in zip(outs[4 * i:4 * i + 4], res, strict=True):
                ref[...] = val
        if comm:
            @pl.when(pl.program_id(0) == n - 1)
            def _():
                comm.mid(place, cins, couts, csems)
                comm.finish(place, cins, couts, csems)

    where = (lambda i: (i, 0)) if tc == c else (lambda i: (0, i))
    spec = pl.BlockSpec((t, tc), where)
    lspec = pl.BlockSpec((8, t, tc), lambda i: (0,) + where(i))
    res = pl.pallas_call(
        kern, grid=(n,), in_specs=[spec, spec, spec, lspec] * k + [ANY] * nci, out_specs=[spec] * (4 * k) + [ANY] * nco,
        out_shape=[jax.ShapeDtypeStruct((r, c), F32)] * (4 * k) + (comm.out_shape if comm else []),
        scratch_shapes=comm.sems if comm else [],
        compiler_params=pltpu.CompilerParams(dimension_semantics=("arbitrary" if comm else "parallel",),
                                             vmem_limit_bytes=VMEM_LIMIT),
        name=name)(*[x for tens in tensors for x in tens], *(comm.ins if comm else []))
    return [res[4 * i:4 * i + 4] for i in range(k)], res[4 * k:]


def _step(a):
    def sq(n):
        v = a[n][0] if a[n].ndim == 3 else a[n]
        return v.T if n.removeprefix("m_").removeprefix("v_") in TRANSPOSED else v

    payload = lambda n: sq(n) if n in EXACT_GATHER else sq(n).astype(BF16)

    loss, dx, g, lands_late = _local_step(sq("x"), sq("mem"), a["positions"][0], sq("loss_target"),
                                          {n: a[n] for n in REPLICATED}, [payload(n) for n in EARLY],
                                          [payload(n) for n in LATE])

    sh, rep = _early_grad_shards(g)
    small = _pack_small([rep[n] for n in REPLICATED] + [loss.reshape(1, 1)])
    *lands_early, land_small = _run_comm(_scatter_plan([sh[n] for n in EARLY], small), "scatter_last")
    quad = lambda n, land: (sq(n), sq("m_" + n), sq("v_" + n), land)
    lands = dict(zip(EARLY, lands_early, strict=True)) | lands_late

    outs = {}
    kinds = ("grad_", "delta_", "new_m_", "new_v_")
    for n, _ in SHARDED:
        res = _adamw([quad(n, lands[n])], "adamw_" + n)[0][0]
        for kind, val in zip(kinds, res, strict=True):
            outs[kind + n] = (val.T if n in TRANSPOSED else val).reshape(a[n].shape)
    zero = jnp.zeros((1, 1), F32)
    packed = [_pack_small([a[p + n] for n in REPLICATED] + [zero]) for p in ("", "m_", "v_")]
    res = _adamw([(*packed, land_small)], "adamw_replicated")[0][0]
    widths = [a[n].shape[1] for n in REPLICATED] + [1]
    for kind, buf in zip(kinds, res, strict=True):
        *vals, total = _unpack_small(buf, widths)
        for n, val in zip(REPLICATED, vals, strict=True):
            outs[kind + n] = val
        if kind == "grad_":
            loss = total[0, 0]

    ordered = [outs[kind + n] for kind in kinds for n in WEIGHTS]
    return (loss, dx[None], *ordered)


def kernel(x, mem, positions, norm_mix, w_in, gla_gate_w2, gla_gate_b, gla_out_norm, mla_q_a_norm, mla_w_uq, mla_kv_a_norm, mla_w_ukv, mla_q_norm, mla_k_norm, w_out, norm_xa, norm_mem, xa_w_q, xa_w_kv, xa_q_norm, xa_k_norm, xa_w_o, norm_ffn, ffn_w_gate, ffn_w_up, ffn_conv_w, ffn_conv_b, ffn_w_down, loss_target, m_norm_mix, m_w_in, m_gla_gate_w2, m_gla_gate_b, m_gla_out_norm, m_mla_q_a_norm, m_mla_w_uq, m_mla_kv_a_norm, m_mla_w_ukv, m_mla_q_norm, m_mla_k_norm, m_w_out, m_norm_xa, m_norm_mem, m_xa_w_q, m_xa_w_kv, m_xa_q_norm, m_xa_k_norm, m_xa_w_o, m_norm_ffn, m_ffn_w_gate, m_ffn_w_up, m_ffn_conv_w, m_ffn_conv_b, m_ffn_w_down, v_norm_mix, v_w_in, v_gla_gate_w2, v_gla_gate_b, v_gla_out_norm, v_mla_q_a_norm, v_mla_w_uq, v_mla_kv_a_norm, v_mla_w_ukv, v_mla_q_norm, v_mla_k_norm, v_w_out, v_norm_xa, v_norm_mem, v_xa_w_q, v_xa_w_kv, v_xa_q_norm, v_xa_k_norm, v_xa_w_o, v_norm_ffn, v_ffn_w_gate, v_ffn_w_up, v_ffn_conv_w, v_ffn_conv_b, v_ffn_w_down):
    return _step(dict(locals()))
```

```python
import functools

import jax
import jax.numpy as jnp
import numpy as np
from jax import lax
from jax.experimental import pallas as pl
from jax.experimental.pallas import tpu as pltpu

F32, BF16 = jnp.float32, jnp.bfloat16
MESH = pl.DeviceIdType.MESH

D_MODEL = 1024
EPS = 1e-6
GLA_HEADS, GLA_DK, GLA_DV, GLA_RANK, GLA_CHUNK = 4, 64, 128, 16, 64
GLA_GATE_NORM = 16.0
MLA_HEADS, MLA_Q_RANK, MLA_KV_RANK, MLA_NOPE, MLA_ROPE, MLA_V = 8, 256, 128, 64, 32, 64
MLA_QK = MLA_NOPE + MLA_ROPE
ROPE_THETA = 10000.0
LOG2E, LN2 = 1.4426950408889634, 0.6931471805599453
XA_HEADS, XA_DIM = 4, 128
D_FF = 2816
ADAM_LR, ADAM_B1, ADAM_B2, ADAM_EPS, ADAM_WD, ADAM_STEP = 0.001, 0.9, 0.999, 1e-08, 0.01, 10

LANES = 128
BF16_ROWS = 16
VMEM_LIMIT = 56 * 1024 * 1024
MATMUL_VMEM = 44 * 1024 * 1024
ROW_TILE = 512

P_GQ, P_GK, P_GV, P_OG, P_CQ, P_CKV, P_KPE, P_ALR, P_WIDTH = 0, 256, 512, 1024, 1536, 1792, 1920, 2048, 2176
N_GQ, N_GK, N_GV, N_ALR, N_OG, N_CQ, N_CKV, N_KPE, N_WIDTH = 0, 256, 512, 1024, 1040, 1552, 1808, 1936, 1968

SHARDED = (("w_in", 1), ("gla_gate_w2", 1), ("mla_w_uq", 1), ("mla_w_ukv", 1), ("w_out", 0), ("xa_w_q", 0),
           ("xa_w_kv", 0), ("xa_w_o", 1), ("ffn_w_gate", 1), ("ffn_w_up", 1), ("ffn_conv_w", 1), ("ffn_w_down", 0))
REPLICATED = ("norm_mix", "gla_gate_b", "gla_out_norm", "mla_q_a_norm", "mla_kv_a_norm", "mla_q_norm", "mla_k_norm",
              "norm_xa", "norm_mem", "xa_q_norm", "xa_k_norm", "norm_ffn", "ffn_conv_b")
EXACT_GATHER = ("gla_gate_w2", "ffn_conv_w")
TRANSPOSED = ("w_in", "ffn_w_gate", "ffn_w_up")
EARLY = ("w_in", "gla_gate_w2", "mla_w_uq", "mla_w_ukv")
LATE = tuple(n for n, _ in SHARDED if n not in EARLY)
WEIGHTS = ("norm_mix", "w_in", "gla_gate_w2", "gla_gate_b", "gla_out_norm", "mla_q_a_norm", "mla_w_uq",
           "mla_kv_a_norm", "mla_w_ukv", "mla_q_norm", "mla_k_norm", "w_out", "norm_xa", "norm_mem", "xa_w_q",
           "xa_w_kv", "xa_q_norm", "xa_k_norm", "xa_w_o", "norm_ffn", "ffn_w_gate", "ffn_w_up", "ffn_conv_w",
           "ffn_conv_b", "ffn_w_down")


_NN = ((1,), (0,))
_NT = ((1,), (1,))
_TN = ((0,), (0,))


def _dg(a, b, dims):
    return lax.dot_general(a.astype(BF16), b.astype(BF16), (dims, ((), ())), preferred_element_type=F32)


@jax.custom_vjp
def _dot_nn(a, b):
    return _dg(a, b, _NN)


_dot_nn.defvjp(lambda a, b: (_dg(a, b, _NN), (a, b)),
               lambda r, g: (_dg(g, r[1], _NT).astype(r[0].dtype), _dg(r[0], g, _TN).astype(r[1].dtype)))


@jax.custom_vjp
def _dot_nt(a, b):
    return _dg(a, b, _NT)


_dot_nt.defvjp(lambda a, b: (_dg(a, b, _NT), (a, b)),
               lambda r, g: (_dg(g, r[1], _NN).astype(r[0].dtype), _dg(g, r[0], _TN).astype(r[1].dtype)))


@jax.custom_vjp
def _dot_tn(a, b):
    return _dg(a, b, _TN)


_dot_tn.defvjp(lambda a, b: (_dg(a, b, _TN), (a, b)),
               lambda r, g: (_dg(r[1], g, _NT).astype(r[0].dtype), _dg(r[0], g, _NN).astype(r[1].dtype)))


def _rms(x, w, n=None):
    n = x.shape[-1] if n is None else n
    ms = jnp.sum(x * x, axis=-1, keepdims=True) * (1.0 / n)
    return x * lax.rsqrt(ms + EPS) * w


def _silu(x):
    return x * jax.nn.sigmoid(x)


def _log_sigmoid(x):
    return jnp.minimum(x, 0.0) - jnp.log(1.0 + jnp.exp(-jnp.abs(x)))


@jax.custom_vjp
def _cumsum_rows(x):
    n = x.shape[0]
    row = lax.broadcasted_iota(jnp.int32, x.shape, 0)
    k = 1
    while k < n:
        x = x + jnp.where(row >= k, pltpu.roll(x, k, 0), 0.0)
        k *= 2
    return x


def _cumsum_rows_bwd(_, g):
    n = g.shape[0]
    row = lax.broadcasted_iota(jnp.int32, g.shape, 0)
    k = 1
    while k < n:
        g = g + jnp.where(row < n - k, pltpu.roll(g, n - k, 0), 0.0)
        k *= 2
    return (g,)


_cumsum_rows.defvjp(lambda x: (_cumsum_rows(x), None), _cumsum_rows_bwd)


def _lane_mask(lo, hi):
    lane = lax.broadcasted_iota(jnp.int32, (1, LANES), 1)
    return ((lane >= lo) & (lane < hi)).astype(F32)


def _tile(n, t):
    t = min(n, t)
    assert n % t == 0, (n, t)
    return t


class _Epilogue:
    def __init__(self, fn, rows=(), consts=(), outs=(), accs=()):
        self.fn, self.rows, self.consts, self.outs, self.accs = fn, list(rows), list(consts), list(outs), list(accs)


def _matmul(a, b, mode, out_dtype, name, residual=None, a_lead=None, b_lead=None, more=None, epilogue=None):
    (a0, a1), (b0, b1) = a.shape[-2:], b.shape[-2:]
    if mode == "nn":
        m, k, k2, n = a0, a1, b0, b1
    elif mode == "nt":
        m, k, n, k2 = a0, a1, b0, b1
    else:
        k, m, k2, n = a0, a1, b0, b1
    assert k == k2, (a.shape, b.shape, mode)
    npar = 4 if "p" in (a_lead, b_lead) else 1
    nsum = 4 if "k" in (a_lead, b_lead) else 1
    pairs = [(a, b)] + ([more] if more else [])
    a_item, b_item, o_item = a.dtype.itemsize, b.dtype.itemsize, jnp.dtype(out_dtype).itemsize
    ep = epilogue
    row_extra = 4 if residual is not None else 0
    if ep:
        row_extra += (sum(r.dtype.itemsize * wd for r, wd, _ in ep.rows) + sum(jnp.dtype(d).itemsize * wd for wd, d in ep.outs)) / n

    def resident(lead, tiles):
        return lead != "p" and tiles == 1

    def vmem_need(tm, tn, tk):
        a_bufs = 1 if resident(a_lead, (m // tm) * (k // tk)) else 2
        b_bufs = 1 if resident(b_lead, (n // tn) * (k // tk)) else 2
        need = a_bufs * (nsum if a_lead == "k" else 1) * tm * tk * a_item + b_bufs * (nsum if b_lead == "k" else 1) * tk * tn * b_item
        need *= len(pairs)
        need += (0 if ep else 2 * tm * tn * o_item) + tm * tn * 4 * (2 if tk < k else 1)
        need += tm * tk * 2 * (a_item == 4 or mode == "tn") + tk * tn * 2 * (b_item == 4)
        return need + int(2 * tm * tn * row_extra) + (3 * tm * tn * 4 if ep else 0)

    halvings = (4096, 2048, 1024, 512, 256, 128, 64, 32, 16, 8)
    if mode == "tn":
        tm = m if m <= 2304 else m // 2
        tn = n if tm * n <= 1024 * 2304 else n // 2
        tk = next((r for r in halvings if k % r == 0 and vmem_need(tm, tn, r) <= MATMUL_VMEM), k)
    else:
        tn, tk = n, k
        tm = next((r for r in halvings if m % r == 0 and vmem_need(r, tn, tk) <= MATMUL_VMEM), m)
    assert m % tm == 0 and n % tn == 0 and k % tk == 0
    assert ep is None or (tn == n and tk == k and npar == 1)
    nk = k // tk
    dims = {"nn": _NN, "nt": _NT, "tn": _TN}[mode]
    n_in = 2 * len(pairs) + (residual is not None)
    n_ep_in = len(ep.rows) + len(ep.consts) if ep else 0
    n_out = len(ep.outs) + len(ep.accs) if ep else 1

    def body(*refs):
        ab, rs, ep_in, outs, scratch = _split_refs(refs, (2 * len(pairs), n_in - 2 * len(pairs), n_ep_in, n_out, nk > 1))
        prod = None
        for a_ref, b_ref in zip(ab[0::2], ab[1::2]):
            for sh in range(nsum):
                term = _dg(a_ref[sh] if a_lead == "k" else a_ref[...], b_ref[sh] if b_lead == "k" else b_ref[...], dims)
                prod = term if prod is None else prod + term

        def finish(r):
            if rs:
                r = r + rs[0][...]
            if ep is None:
                outs[0][...] = r.astype(outs[0].dtype)
                return
            vals = [x[...] for x in ep_in]
            ro, ao = ep.fn(r, vals[:len(ep.rows)], vals[len(ep.rows):])
            for ref, val in zip(outs[:len(ep.outs)], ro, strict=True):
                ref[...] = val.astype(ref.dtype)
            if ep.accs:
                @pl.when(pl.program_id(0) == 0)
                def _():
                    for ref in outs[len(ep.outs):]:
                        ref[...] = jnp.zeros_like(ref)

                for ref, val in zip(outs[len(ep.outs):], ao, strict=True):
                    ref[...] += val

        if nk == 1:
            finish(prod)
            return
        acc = scratch[0]
        kk = pl.program_id(3)

        @pl.when(kk == 0)
        def _():
            acc[...] = prod

        @pl.when(kk > 0)
        def _():
            acc[...] += prod

        @pl.when(kk == nk - 1)
        def _():
            finish(acc[...])

    def spec(lead, blk, idx, tiles=0):
        mode = {"pipeline_mode": pl.Buffered(1)} if resident(lead, tiles) else {}
        if lead is None:
            return pl.BlockSpec(blk, lambda i, j, p, kk: idx(i, j, kk), **mode)
        if lead == "p":
            return pl.BlockSpec((None,) + blk, lambda i, j, p, kk: (p,) + idx(i, j, kk))
        return pl.BlockSpec((nsum,) + blk, lambda i, j, p, kk: (0,) + idx(i, j, kk), **mode)

    a_tiles, b_tiles = (m // tm) * nk, (n // tn) * nk
    if mode == "nn":
        pair_specs = [spec(a_lead, (tm, tk), lambda i, j, kk: (i, kk), a_tiles),
                      spec(b_lead, (tk, tn), lambda i, j, kk: (kk, j), b_tiles)]
    elif mode == "nt":
        pair_specs = [spec(a_lead, (tm, tk), lambda i, j, kk: (i, kk), a_tiles),
                      spec(b_lead, (tn, tk), lambda i, j, kk: (j, kk), b_tiles)]
    else:
        pair_specs = [spec(a_lead, (tk, tm), lambda i, j, kk: (kk, i), a_tiles),
                      spec(b_lead, (tk, tn), lambda i, j, kk: (kk, j), b_tiles)]
    tile = spec(None, (tm, tn), lambda i, j, kk: (i, j))
    in_specs = pair_specs * len(pairs)
    args = [x for pair in pairs for x in pair]
    if residual is not None:
        assert npar == 1
        in_specs.append(tile)
        args.append(residual)
    if ep:
        in_specs += [pl.BlockSpec((tm, wd), functools.partial(lambda cb, i, j, p, kk: (i, cb), cb)) for _, wd, cb in ep.rows]
        in_specs += [pl.BlockSpec(c.shape, lambda i, j, p, kk: (0, 0)) for c in ep.consts]
        args += [r for r, _, _ in ep.rows] + ep.consts
        out_specs = [pl.BlockSpec((tm, wd), lambda i, j, p, kk: (i, 0)) for wd, _ in ep.outs]
        out_specs += [pl.BlockSpec(shape, lambda i, j, p, kk: (0, 0)) for shape in ep.accs]
        out_shape = [jax.ShapeDtypeStruct((m, wd), d) for wd, d in ep.outs] + [jax.ShapeDtypeStruct(sh, F32) for sh in ep.accs]
    else:
        out_specs = spec("p" if npar > 1 else None, (tm, tn), lambda i, j, kk: (i, j))
        out_shape = jax.ShapeDtypeStruct(((4,) if npar > 1 else ()) + (m, n), out_dtype)
    outer = "arbitrary" if ep and ep.accs else "parallel"
    return pl.pallas_call(
        body, grid=(m // tm, n // tn, npar, nk), in_specs=in_specs, out_specs=out_specs, out_shape=out_shape,
        scratch_shapes=[pltpu.VMEM((tm, tn), F32)] if nk > 1 else [],
        compiler_params=pltpu.CompilerParams(dimension_semantics=(outer, outer, outer, "arbitrary"),
                                             vmem_limit_bytes=VMEM_LIMIT),
        name=name)(*args)


def _row(a, width=None, col_block=0):
    return (a, a.shape[1] if width is None else width, col_block)


def _rows_call(body, rows, consts, outs, accs=(), *, name, tile=ROW_TILE):
    s = rows[0][0].shape[0]
    t = _tile(s, tile)
    nr, nc, no = len(rows), len(consts), len(outs)

    def kern(*refs):
        r = [x[...] for x in refs[:nr]]
        c = [x[...] for x in refs[nr:nr + nc]]
        o_refs = refs[nr + nc:nr + nc + no]
        a_refs = refs[nr + nc + no:]
        ro, ao = body(r, c)
        for ref, val in zip(o_refs, ro, strict=True):
            ref[...] = val.astype(ref.dtype)
        if a_refs:
            @pl.when(pl.program_id(0) == 0)
            def _():
                for ref in a_refs:
                    ref[...] = jnp.zeros_like(ref)

            for ref, val in zip(a_refs, ao, strict=True):
                ref[...] += val

    in_specs = [pl.BlockSpec((t, w), functools.partial(lambda cb, i: (i, cb), cb)) for (_, w, cb) in rows]
    in_specs += [pl.BlockSpec(c.shape, lambda i: (0, 0)) for c in consts]
    out_specs = [pl.BlockSpec((t, w), lambda i: (i, 0)) for (w, _) in outs]
    out_specs += [pl.BlockSpec(shape, lambda i: (0, 0)) for shape in accs]
    out_shape = [jax.ShapeDtypeStruct((s, w), dt) for (w, dt) in outs]
    out_shape += [jax.ShapeDtypeStruct(shape, F32) for shape in accs]
    return pl.pallas_call(
        kern, grid=(s // t,), in_specs=in_specs, out_specs=out_specs, out_shape=out_shape,
        compiler_params=pltpu.CompilerParams(dimension_semantics=("arbitrary" if accs else "parallel",),
                                             vmem_limit_bytes=VMEM_LIMIT),
        name=name)(*[r[0] for r in rows], *consts)


def _gla_chunk(q, k, la, v0, v1, s0, s1):
    c = q.shape[0]
    r = lax.broadcasted_iota(jnp.int32, (c, c), 0)
    cc = lax.broadcasted_iota(jnp.int32, (c, c), 1)
    tril = cc <= r
    cum = _cumsum_rows(la)
    cl = jnp.sum(la, axis=0, keepdims=True)
    qd = q * (GLA_DK ** -0.5) * jnp.exp(cum)
    ki = k * jnp.exp(-cum)
    ke = k * jnp.exp(cl - cum)
    dec = jnp.exp(cl)
    outs, news = [], []
    for h, (v, s) in enumerate(((v0, s0), (v1, s1))):
        mk = _lane_mask(GLA_DK * h, GLA_DK * (h + 1))
        qh = qd * mk
        att = jnp.where(tril, _dot_nt(qh, ki), 0.0)
        outs.append(_dot_nn(att, v) + _dot_nt(qh, s))
        news.append(s * dec + _dot_tn(v, ke * mk))
    return outs[0], outs[1], news[0], news[1]


def _gla_specs(tb, rev_nb=None):
    blk = (lambda b: b) if rev_nb is None else (lambda b: rev_nb - 1 - b)
    q = pl.BlockSpec((tb, 128), lambda p, b: (blk(b), P_GQ // 128 + p))
    k = pl.BlockSpec((tb, 128), lambda p, b: (blk(b), P_GK // 128 + p))
    la = pl.BlockSpec((tb, 128), lambda p, b: (blk(b), p))
    v = pl.BlockSpec((tb, 256), lambda p, b: (blk(b), P_GV // 256 + p))
    o = pl.BlockSpec((tb, 256), lambda p, b: (blk(b), p))
    st = pl.BlockSpec((tb // GLA_CHUNK, 2, 128, 128), lambda p, b: (blk(b), p, 0, 0))
    return q, k, la, v, o, st


def _gla_fwd(proj, la):
    s = proj.shape[0]
    tb = _tile(s, ROW_TILE)
    nb, nch = s // tb, tb // GLA_CHUNK

    def kern(q_ref, k_ref, la_ref, v_ref, o_ref, st_ref, s_sc):
        @pl.when(pl.program_id(1) == 0)
        def _():
            s_sc[...] = jnp.zeros_like(s_sc)

        s0, s1 = s_sc[0], s_sc[1]
        for ci in range(nch):
            sl = slice(ci * GLA_CHUNK, (ci + 1) * GLA_CHUNK)
            st_ref[ci, 0] = s0
            st_ref[ci, 1] = s1
            o0, o1, s0, s1 = _gla_chunk(q_ref[sl, :], k_ref[sl, :], la_ref[sl, :], v_ref[sl, 0:128],
                                        v_ref[sl, 128:256], s0, s1)
            o_ref[sl, 0:128] = o0
            o_ref[sl, 128:256] = o1
        s_sc[0] = s0
        s_sc[1] = s1

    q, k, lasp, v, o, st = _gla_specs(tb)
    return pl.pallas_call(
        kern, grid=(2, nb), in_specs=[q, k, lasp, v], out_specs=[o, st],
        out_shape=[jax.ShapeDtypeStruct((s, 512), F32),
                   jax.ShapeDtypeStruct((s // GLA_CHUNK, GLA_HEADS, 128, 128), F32)],
        scratch_shapes=[pltpu.VMEM((2, 128, 128), F32)],
        compiler_params=pltpu.CompilerParams(dimension_semantics=("parallel", "arbitrary"),
                                             vmem_limit_bytes=VMEM_LIMIT),
        name="gla_fwd")(proj, proj, la, proj)


def _gla_bwd(proj, la, states, d_o):
    s = proj.shape[0]
    tb = _tile(s, ROW_TILE)
    nb, nch = s // tb, tb // GLA_CHUNK

    def kern(q_ref, k_ref, la_ref, v_ref, do_ref, st_ref, dq_ref, dk_ref, dla_ref, dv_ref, ds_sc):
        @pl.when(pl.program_id(1) == 0)
        def _():
            ds_sc[...] = jnp.zeros_like(ds_sc)

        d0, d1 = ds_sc[0], ds_sc[1]
        for ci in reversed(range(nch)):
            sl = slice(ci * GLA_CHUNK, (ci + 1) * GLA_CHUNK)
            _, vjp = jax.vjp(_gla_chunk, q_ref[sl, :], k_ref[sl, :], la_ref[sl, :], v_ref[sl, 0:128],
                             v_ref[sl, 128:256], st_ref[ci, 0], st_ref[ci, 1])
            gq, gk, gla, gv0, gv1, d0, d1 = vjp((do_ref[sl, 0:128], do_ref[sl, 128:256], d0, d1))
            dq_ref[sl, :] = gq
            dk_ref[sl, :] = gk
            dla_ref[sl, :] = gla
            dv_ref[sl, 0:128] = gv0
            dv_ref[sl, 128:256] = gv1
        ds_sc[0] = d0
        ds_sc[1] = d1

    q, k, lasp, v, o, st = _gla_specs(tb, rev_nb=nb)
    return pl.pallas_call(
        kern, grid=(2, nb), in_specs=[q, k, lasp, v, o, st], out_specs=[lasp, lasp, lasp, o],
        out_shape=[jax.ShapeDtypeStruct((s, 256), F32), jax.ShapeDtypeStruct((s, 256), F32),
                   jax.ShapeDtypeStruct((s, 256), F32), jax.ShapeDtypeStruct((s, 512), F32)],
        scratch_shapes=[pltpu.VMEM((2, 128, 128), F32)],
        compiler_params=pltpu.CompilerParams(dimension_semantics=("parallel", "arbitrary"),
                                             vmem_limit_bytes=VMEM_LIMIT),
        name="gla_bwd")(proj, proj, la, proj, d_o, states)


def _causal_keep(t):
    return lax.broadcasted_iota(jnp.int32, (t, t), 1) <= lax.broadcasted_iota(jnp.int32, (t, t), 0)


def _split_refs(refs, counts):
    out, off = [], 0
    for cnt in counts:
        out.append(refs[off:off + cnt])
        off += cnt
    return out


def _causal_blocks(n, key_major):
    pairs = ([(ki, qi) for ki in range(n) for qi in range(ki, n)] if key_major else
             [(ki, qi) for qi in range(n) for ki in range(qi + 1)])
    return np.array([ki for ki, _ in pairs], np.int32), np.array([qi for _, qi in pairs], np.int32)


def _attn_fwd(q, k, v, comm, tile=1024):
    s = q.shape[0]
    t = _tile(s, tile)
    n = s // t
    nci, nco = len(comm.ins), len(comm.out_shape)

    ki_tab, qi_tab = _causal_blocks(n, key_major=False)
    steps = len(ki_tab)

    def kern(ki_ref, qi_ref, *refs):
        (q_ref, k_ref, v_ref), cins, (o_ref, lse_ref), couts, (m_sc, l_sc, acc_sc), csems = _split_refs(
            refs, (3, nci, 2, nco, 3, len(comm.sems)))
        pair, step = pl.program_id(0), pl.program_id(1)
        qi, ki = qi_ref[step], ki_ref[step]
        place = _place()

        @pl.when((pair == 0) & (step == 0))
        def _():
            comm.start(place, cins, couts, csems)

        @pl.when((pair == MLA_HEADS // 2 - 1) & (step == 0))
        def _():
            comm.mid(place, cins, couts, csems)

        first = lax.broadcasted_iota(jnp.int32, (t, LANES), 1) < MLA_V

        @pl.when(ki == 0)
        def _():
            m_sc[...] = jnp.full_like(m_sc, -jnp.inf)
            l_sc[...] = jnp.zeros_like(l_sc)
            acc_sc[...] = jnp.zeros_like(acc_sc)

        def update(diagonal):
            keep = _causal_keep(t)
            alphas, pvs = [], []
            for h in range(2):
                sc = _dg(q_ref[:, 128 * h:128 * (h + 1)], k_ref[:, 128 * h:128 * (h + 1)], _NT)
                if diagonal:
                    sc = jnp.where(keep, sc, -jnp.inf)
                m_prev = m_sc[h]
                m_new = jnp.maximum(m_prev, jnp.max(sc, axis=1, keepdims=True))
                alpha = jnp.exp2(m_prev - m_new)
                p = jnp.exp2(sc - m_new[:, 0:1])
                l_sc[h] = alpha * l_sc[h] + jnp.sum(p, axis=1, keepdims=True)
                m_sc[h] = m_new
                alphas.append(alpha)
                pvs.append(_dg(p, v_ref[...], _NN))
            acc_sc[...] = acc_sc[...] * jnp.where(first, alphas[0], alphas[1]) + jnp.where(first, pvs[0], pvs[1])

        @pl.when(ki < qi)
        def _():
            update(False)

        @pl.when(ki == qi)
        def _():
            update(True)

        @pl.when(ki == qi)
        def _():
            l = jnp.where(first, l_sc[0], l_sc[1])
            m = jnp.where(first, m_sc[0], m_sc[1])
            o_ref[...] = acc_sc[...] / l
            lse_ref[...] = m + jnp.log2(l)

        @pl.when((pair == MLA_HEADS // 2 - 1) & (step == steps - 1))
        def _():
            comm.finish(place, cins, couts, csems)

    q_idx = lambda p, st, ki_r, qi_r: (qi_r[st], p)
    k_idx = lambda p, st, ki_r, qi_r: (ki_r[st], p)
    res = pl.pallas_call(
        kern, grid_spec=pltpu.PrefetchScalarGridSpec(
            num_scalar_prefetch=2, grid=(MLA_HEADS // 2, steps),
            in_specs=[pl.BlockSpec((t, 256), q_idx), pl.BlockSpec((t, 256), k_idx), pl.BlockSpec((t, 128), k_idx)]
            + [ANY] * nci,
            out_specs=[pl.BlockSpec((t, 128), q_idx), pl.BlockSpec((t, 128), q_idx)] + [ANY] * nco,
            scratch_shapes=[pltpu.VMEM((2, t, LANES), F32), pltpu.VMEM((2, t, LANES), F32),
                            pltpu.VMEM((t, LANES), F32)] + comm.sems),
        out_shape=[jax.ShapeDtypeStruct((s, 512), F32), jax.ShapeDtypeStruct((s, 512), F32)] + comm.out_shape,
        compiler_params=pltpu.CompilerParams(dimension_semantics=("arbitrary", "arbitrary"),
                                             vmem_limit_bytes=VMEM_LIMIT),
        name="mla_attn_fwd")(ki_tab, qi_tab, q, k, v, *comm.ins)
    return res[0], res[1], res[2:]


def _attn_bwd(q, k, v, o, lse, d_o, comm, tile=ROW_TILE):
    s = q.shape[0]
    t = _tile(s, tile)
    n = s // t
    nci, nco = len(comm.ins), len(comm.out_shape)

    ki_tab, qi_tab = _causal_blocks(n, key_major=True)
    steps = len(ki_tab)

    def kern(ki_ref, qi_ref, *refs):
        (q_ref, k_ref, v_ref, o_ref, lse_ref, do_ref), cins, (dq_ref, dk_ref, dv_ref), couts, (dk_sc, dv_sc), csems = \
            _split_refs(refs, (6, nci, 3, nco, 2, len(comm.sems)))
        pair, step = pl.program_id(0), pl.program_id(1)
        ki, qi = ki_ref[step], qi_ref[step]
        place = _place()

        @pl.when((pair == 0) & (step == 0))
        def _():
            comm.start(place, cins, couts, csems)

        @pl.when((pair == MLA_HEADS // 2 - 1) & (step == 0))
        def _():
            comm.mid(place, cins, couts, csems)

        @pl.when((ki == 0) & (qi == 0))
        def _():
            dq_ref[...] = jnp.zeros_like(dq_ref)

        @pl.when(qi == ki)
        def _():
            dk_sc[...] = jnp.zeros_like(dk_sc)
            dv_sc[...] = jnp.zeros_like(dv_sc)

        def update(diagonal):
            keep = _causal_keep(t)
            d_o = do_ref[...]
            prod = d_o * o_ref[...]
            rows = pl.ds(pl.multiple_of(qi * t, t), t)
            for h in range(2):
                hs = slice(128 * h, 128 * (h + 1))
                mk = _lane_mask(MLA_V * h, MLA_V * (h + 1))
                qh, kh = q_ref[:, hs], k_ref[:, hs]
                sc = _dg(qh, kh, _NT)
                if diagonal:
                    sc = jnp.where(keep, sc, -jnp.inf)
                p = jnp.exp2(sc - lse_ref[:, MLA_V * h:MLA_V * h + 1])
                doh = d_o * mk
                dp = _dg(doh * LN2, v_ref[...], _NT)
                delta = jnp.sum(prod * mk, axis=1, keepdims=True) * LN2
                ds = p * (dp - delta)
                dv_sc[...] += _dg(p, doh, _TN)
                dk_sc[:, hs] += _dg(ds, qh, _TN)
                dq_ref[rows, hs] += _dg(ds, kh, _NN)

        @pl.when(qi > ki)
        def _():
            update(False)

        @pl.when(qi == ki)
        def _():
            update(True)

        @pl.when(qi == n - 1)
        def _():
            dk_ref[...] = dk_sc[...]
            dv_ref[...] = dv_sc[...].astype(dv_ref.dtype)

        @pl.when((pair == MLA_HEADS // 2 - 1) & (step == steps - 1))
        def _():
            comm.finish(place, cins, couts, csems)

    q_idx = lambda p, st, ki_r, qi_r: (qi_r[st], p)
    k_idx = lambda p, st, ki_r, qi_r: (ki_r[st], p)
    res = pl.pallas_call(
        kern, grid_spec=pltpu.PrefetchScalarGridSpec(
            num_scalar_prefetch=2, grid=(MLA_HEADS // 2, steps),
            in_specs=[pl.BlockSpec((t, 256), q_idx), pl.BlockSpec((t, 256), k_idx), pl.BlockSpec((t, 128), k_idx),
                      pl.BlockSpec((t, 128), q_idx), pl.BlockSpec((t, 128), q_idx), pl.BlockSpec((t, 128), q_idx)]
            + [ANY] * nci,
            out_specs=[pl.BlockSpec((s, 256), lambda p, st, ki_r, qi_r: (0, p)), pl.BlockSpec((t, 256), k_idx),
                       pl.BlockSpec((t, 128), k_idx)] + [ANY] * nco,
            scratch_shapes=[pltpu.VMEM((t, 256), F32), pltpu.VMEM((t, 128), F32)] + comm.sems),
        out_shape=[jax.ShapeDtypeStruct((s, 1024), F32), jax.ShapeDtypeStruct((s, 1024), F32),
                   jax.ShapeDtypeStruct((s, 512), BF16)] + comm.out_shape,
        compiler_params=pltpu.CompilerParams(dimension_semantics=("arbitrary", "arbitrary"),
                                             vmem_limit_bytes=VMEM_LIMIT),
        name="mla_attn_bwd")(ki_tab, qi_tab, q, k, v, o, lse, d_o, *comm.ins)
    return res[0], res[1], res[2], res[3:]


def _gate_fn(alr, w2, b):
    return _log_sigmoid(_dot_nn(alr, w2) + b) * (1.0 / GLA_GATE_NORM)


def _make_norm_rope(scale):
    def forward(x, w, c, sa, sb):
        r = lax.rsqrt(jnp.sum(x * x, axis=-1, keepdims=True) * (1.0 / MLA_QK) + EPS)
        y = x * r * w
        out = y * c + pltpu.roll(y, LANES - 16, 1) * sa + pltpu.roll(y, 16, 1) * sb
        return (out if scale == 1.0 else out * scale), r

    @jax.custom_vjp
    def norm_rope(x, w, c, sa, sb):
        return forward(x, w, c, sa, sb)[0]

    def fwd(x, w, c, sa, sb):
        out, r = forward(x, w, c, sa, sb)
        return out, (x, w, c, sa, sb, r)

    def bwd(res, g):
        x, w, c, sa, sb, r = res
        if scale != 1.0:
            g = g * scale
        gy = g * c + pltpu.roll(g * sa, 16, 1) + pltpu.roll(g * sb, LANES - 16, 1)
        xr = x * r
        t = gy * w
        m = jnp.sum(t * xr, axis=-1, keepdims=True) * (1.0 / MLA_QK)
        return r * (t - xr * m), jnp.sum(gy * xr, axis=0, keepdims=True), jnp.zeros_like(c), jnp.zeros_like(sa), jnp.zeros_like(sb)

    norm_rope.defvjp(fwd, bwd)
    return norm_rope


_q_norm_rope = _make_norm_rope(MLA_QK ** -0.5 * LOG2E)
_k_norm_rope = _make_norm_rope(1.0)


def _qk_head(qh, kh, kpe, c, sa, sb, qn, kn):
    kfull = kh + kpe * _lane_mask(MLA_NOPE, MLA_QK)
    return _q_norm_rope(qh, qn, c, sa, sb), _k_norm_rope(kfull, kn, c, sa, sb)


def _mix_head(o, og, gn):
    return _rms(o, gn) * _silu(og)


def _xa_head(xq, xk, xv, qn, kn):
    sc = _dot_nt(_rms(xq, qn), _rms(xk, kn)) * (XA_DIM ** -0.5)
    e = jnp.exp(sc - lax.stop_gradient(jnp.max(sc, axis=1, keepdims=True)))
    p = e / jnp.sum(e, axis=1, keepdims=True)
    return _dot_nn(p, xv)


def _heads(x, n):
    return [x[:, 128 * h:128 * (h + 1)] for h in range(n)]


def _cat(xs):
    return jnp.concatenate(xs, axis=1)


def _norm_fwd(x, w, name):
    return _rows_call(lambda r, c: ([_rms(r[0], c[0])], []), [_row(x)], [w], [(x.shape[1], BF16)], name=name)[0]


def _norm_fwd_epilogue(w):
    return _Epilogue(lambda h, rows, consts: ([h, _rms(h, consts[0])], []), [], [w], [(D_MODEL, F32), (D_MODEL, BF16)], [])


def _norm_bwd_epilogue(x, w, add):
    def fn(d_out, rows, consts):
        _, vjp = jax.vjp(_rms, rows[0], consts[0])
        dx, dw = vjp(d_out)
        return [dx + rows[1]], [dw]

    return _Epilogue(fn, [_row(x), _row(add)], [w], [(D_MODEL, F32)], [w.shape])


def _norm_fwd_comm(x, w, comm, name):
    s, d = x.shape
    t = _tile(s, ROW_TILE)
    n = s // t
    nci, nco = len(comm.ins), len(comm.out_shape)

    def kern(*refs):
        (x_ref, w_ref), cins, (o_ref,), couts, csems = _split_refs(refs, (2, nci, 1, nco, len(comm.sems)))
        place = _place()

        @pl.when(pl.program_id(0) == 0)
        def _():
            comm.start(place, cins, couts, csems)

        o_ref[...] = _rms(x_ref[...], w_ref[...]).astype(o_ref.dtype)

        @pl.when(pl.program_id(0) == n - 1)
        def _():
            comm.mid(place, cins, couts, csems)
            comm.finish(place, cins, couts, csems)

    tile = pl.BlockSpec((t, d), lambda i: (i, 0))
    res = pl.pallas_call(
        kern, grid=(n,), in_specs=[tile, pl.BlockSpec(w.shape, lambda i: (0, 0))] + [ANY] * nci,
        out_specs=[tile] + [ANY] * nco, out_shape=[jax.ShapeDtypeStruct((s, d), BF16)] + comm.out_shape,
        scratch_shapes=comm.sems,
        compiler_params=pltpu.CompilerParams(dimension_semantics=("arbitrary",), vmem_limit_bytes=VMEM_LIMIT),
        name=name)(x, w, *comm.ins)
    return res[0], res[1:]


def _norm_bwd(x, w, d_out, add, name):
    def body(r, c):
        _, vjp = jax.vjp(_rms, r[0], c[0])
        dx, dw = vjp(r[1])
        return [dx + r[2]], [dw]

    return _rows_call(body, [_row(x), _row(d_out), _row(add)], [w], [(x.shape[1], F32)], [w.shape], name=name)


CONV_HALO = BF16_ROWS


def _conv_specs(s, f, t):
    n8 = t // CONV_HALO
    cur = pl.BlockSpec((None, t, f), lambda j, i: (j, i, 0))
    prev = pl.BlockSpec((None, CONV_HALO, f), lambda j, i: (j, jnp.maximum(i * n8 - 1, 0), 0))
    nxt = pl.BlockSpec((None, CONV_HALO, f), lambda j, i: (j, jnp.minimum((i + 1) * n8, s // CONV_HALO - 1), 0))
    cw = pl.BlockSpec((None, 3, f), lambda j, i: (j, 0, 0))
    cb = pl.BlockSpec((None, 1, f), lambda j, i: (j, 0, 0))
    return cur, prev, nxt, cw, cb


def _conv_taps(g, prev, first):
    ext = jnp.concatenate([jnp.where(first, 0.0, prev.astype(F32)), g], axis=0)
    return pltpu.roll(ext, 1, 0)[CONV_HALO:], pltpu.roll(ext, 2, 0)[CONV_HALO:]


def _conv_fwd(gg, uu, cw, cb):
    _, s, f = gg.shape
    t = _tile(s, ROW_TILE)

    def kern(g_ref, gp_ref, u_ref, cw_ref, cb_ref, o_ref):
        g = g_ref[...].astype(F32)
        g1, g2 = _conv_taps(g, gp_ref[...], pl.program_id(1) == 0)
        w = cw_ref[...]
        gc = cb_ref[...] + w[0:1] * g2 + w[1:2] * g1 + w[2:3] * g
        o_ref[...] = (_silu(gc) * u_ref[...].astype(F32)).astype(o_ref.dtype)

    cur, prev, _, cws, cbs = _conv_specs(s, f, t)
    return pl.pallas_call(
        kern, grid=(4, s // t), in_specs=[cur, prev, cur, cws, cbs], out_specs=cur,
        out_shape=jax.ShapeDtypeStruct(gg.shape, BF16),
        compiler_params=pltpu.CompilerParams(dimension_semantics=("parallel", "parallel"), vmem_limit_bytes=VMEM_LIMIT),
        name="ffn_conv_fwd")(gg, gg, uu, cw, cb)


def _conv_bwd(gg, uu, dact, cw, cb):
    _, s, f = gg.shape
    t = _tile(s, ROW_TILE)
    nt = s // t

    def kern(g_ref, gp_ref, gn_ref, u_ref, un_ref, da_ref, dan_ref, cw_ref, cb_ref, du_ref, dg_ref, dcw_ref, dcb_ref):
        i = pl.program_id(1)
        cat = lambda a_ref, b_ref: jnp.concatenate([a_ref[...].astype(F32), b_ref[...].astype(F32)], axis=0)
        g, u, da = cat(g_ref, gn_ref), cat(u_ref, un_ref), cat(da_ref, dan_ref)
        g1, g2 = _conv_taps(g, gp_ref[...], i == 0)
        w = cw_ref[...]
        gc = cb_ref[...] + w[0:1] * g2 + w[1:2] * g1 + w[2:3] * g
        sg = jax.nn.sigmoid(gc)
        du_ref[...] = (da[:t] * (gc[:t] * sg[:t])).astype(du_ref.dtype)
        row = lax.broadcasted_iota(jnp.int32, (t + CONV_HALO, 1), 0)
        dgc = jnp.where((row < t) | (i < nt - 1), da * u * (sg * (1.0 + gc * (1.0 - sg))), 0.0)
        up1 = pltpu.roll(dgc, t + CONV_HALO - 1, 0)[:t]
        up2 = pltpu.roll(dgc, t + CONV_HALO - 2, 0)[:t]
        dgc = dgc[:t]
        dg_ref[...] = (w[2:3] * dgc + w[1:2] * up1 + w[0:1] * up2).astype(dg_ref.dtype)

        @pl.when(i == 0)
        def _():
            dcw_ref[...] = jnp.zeros_like(dcw_ref)
            dcb_ref[...] = jnp.zeros_like(dcb_ref)

        ones = jnp.ones((8, t), BF16)
        col_sum = lambda a: _dg(ones, a, _NN)[0:1]
        dcw_ref[0:1, :] += col_sum(dgc * g2[:t])
        dcw_ref[1:2, :] += col_sum(dgc * g1[:t])
        dcw_ref[2:3, :] += col_sum(dgc * g[:t])
        dcb_ref[...] += col_sum(dgc)

    cur, prev, nxt, cws, cbs = _conv_specs(s, f, t)
    return pl.pallas_call(
        kern, grid=(4, nt), in_specs=[cur, prev, nxt, cur, nxt, cur, nxt, cws, cbs], out_specs=[cur, cur, cws, cbs],
        out_shape=[jax.ShapeDtypeStruct(gg.shape, BF16), jax.ShapeDtypeStruct(gg.shape, BF16),
                   jax.ShapeDtypeStruct(cw.shape, F32), jax.ShapeDtypeStruct(cb.shape, F32)],
        compiler_params=pltpu.CompilerParams(dimension_semantics=("parallel", "arbitrary"), vmem_limit_bytes=VMEM_LIMIT),
        name="ffn_conv_bwd")(gg, gg, gg, uu, uu, dact, dact, cw, cb)


def _rope_tables(pos):
    half = MLA_ROPE // 2
    lane = jnp.arange(LANES)
    rotary = (lane >= MLA_NOPE) & (lane < MLA_QK)
    inv = jnp.where(rotary, ROPE_THETA ** (-((lane - MLA_NOPE) % half).astype(F32) / half), 0.0)
    ang = pos.astype(F32)[:, None] * inv
    cos, sin = jnp.cos(ang), jnp.sin(ang)
    first = rotary & (lane < MLA_NOPE + half)
    return cos, jnp.where(first, -sin, 0.0), jnp.where(rotary & ~first, sin, 0.0)


def _local_step(x, mem, pos, target, rep, early_shards, late_shards):
    g = {}
    c, sa, sb = _rope_tables(pos)

    xn, gathered = _norm_fwd_comm(x, rep["norm_mix"], _gather_plan(early_shards), "norm_mix_fwd_gather")
    w = _early_layout(dict(zip(EARLY, gathered, strict=True)), rep)

    def proj_fn(r, rows, k):
        la_ = _gate_fn(r[:, P_ALR:P_ALR + 128], k[0], k[1])
        return [r, la_, _rms(r[:, P_CQ:P_CQ + MLA_Q_RANK], k[2]), _rms(r[:, P_CKV:P_CKV + MLA_KV_RANK], k[3])], []

    proj, la, q_lat, kv_lat = _matmul(
        xn, w["in"], "nt", F32, "proj_fwd", epilogue=_Epilogue(
            proj_fn, [], [w["w2"], w["gate_b"], w["q_a_norm"], w["kv_a_norm"]],
            [(P_WIDTH, F32), (256, F32), (MLA_Q_RANK, BF16), (MLA_KV_RANK, BF16)], []))
    alr = _row(proj, 128, P_ALR // 128)
    kpe = _row(proj, 128, P_KPE // 128)
    og = _row(proj, 512, P_OG // 512)
    cq = _row(proj, 256, P_CQ // 256)
    ckv = _row(proj, 128, P_CKV // 128)

    o_gla, states = _gla_fwd(proj, la)

    def qk_body(r, k):
        q_up, k_up = _dg(r[0], k[0], _NN), _dg(r[1], k[1], _NN)
        qs, ks = [], []
        for qh, kh in zip(_heads(q_up, MLA_HEADS), _heads(k_up, MLA_HEADS)):
            a, b = _qk_head(qh, kh, r[2], r[3], r[4], r[5], k[3], k[4])
            qs.append(a)
            ks.append(b)
        return [_cat(qs), _cat(ks), _dg(r[1], k[2], _NN)], []

    tabs = [_row(c), _row(sa), _row(sb)]
    qk_consts = [w["uq"], w["k"], w["v"], w["q_norm"], w["k_norm"]]
    q_r, k_r, v_mla = _rows_call(qk_body, [_row(q_lat), _row(kv_lat), kpe] + tabs, qk_consts,
                                 [(1024, BF16), (1024, BF16), (512, BF16)], name="mla_qk_fwd")
    o_mla, lse, gathered = _attn_fwd(q_r, k_r, v_mla, _gather_plan(late_shards))
    w.update(_late_layout(dict(zip(LATE, gathered, strict=True))))

    def mix_body(r, k):
        ys = [_mix_head(o, g_, k[0]) for o, g_ in zip(_heads(r[0], GLA_HEADS), _heads(r[1], GLA_HEADS))]
        return [_cat(ys + [r[2]])], []

    cat = _rows_call(mix_body, [_row(o_gla), og, _row(o_mla)], [w["gla_out_norm"]], [(1024, BF16)],
                     name="mix_fwd")[0]
    h1, hn = _matmul(cat, w["out"], "nn", F32, "out_fwd_norm", residual=x, epilogue=_norm_fwd_epilogue(w["norm_xa"]))
    mn = _norm_fwd(mem, w["norm_mem"], "norm_mem_fwd")
    xkv = _matmul(mn, w["xkv"], "nn", F32, "xa_kv_fwd")

    def xa_fn(r, rows, k):
        ks, vs = _heads(k[0], 2 * XA_HEADS)[:XA_HEADS], _heads(k[0], 2 * XA_HEADS)[XA_HEADS:]
        return [r, _cat([_xa_head(a, b, v_, k[1], k[2]) for a, b, v_ in zip(_heads(r, XA_HEADS), ks, vs)])], []

    xq, xo = _matmul(hn, w["xq"], "nn", F32, "xa_q_fwd_attn", epilogue=_Epilogue(
        xa_fn, [], [xkv, w["xa_q_norm"], w["xa_k_norm"]], [(512, F32), (512, BF16)], []))
    h2, fn = _matmul(xo, w["xo"], "nn", F32, "xa_o_fwd_norm", residual=h1, epilogue=_norm_fwd_epilogue(w["norm_ffn"]))
    gg = _matmul(fn, w["wg"], "nt", BF16, "ffn_gate_fwd", b_lead="p")
    uu = _matmul(fn, w["wu"], "nt", BF16, "ffn_up_fwd", b_lead="p")
    act = _conv_fwd(gg, uu, w["cw"], w["cb"])
    def loss_fn(y, rows, consts):
        err = y - rows[0]
        part = 0.5 * jnp.sum(jnp.sum(err * err, axis=1, keepdims=True) * (1.0 / D_MODEL), axis=0, keepdims=True)
        return [err * (1.0 / D_MODEL)], [jnp.broadcast_to(part, (1, LANES))]

    dy, loss = _matmul(act, w["wd"], "nn", F32, "ffn_down_fwd_loss", residual=h2, a_lead="k", b_lead="k",
                       epilogue=_Epilogue(loss_fn, [_row(target)], [], [(D_MODEL, F32)], [(1, LANES)]))

    g["ffn_w_down"] = _matmul(act, dy, "tn", BF16, "ffn_down_dw", a_lead="p")
    dact = _matmul(dy, w["wd"], "nt", BF16, "ffn_down_dx", b_lead="p")
    duu, dgg, g["ffn_conv_w"], g["ffn_conv_b"] = _conv_bwd(gg, uu, dact, w["cw"], w["cb"])
    g["ffn_w_gate"] = _matmul(dgg, fn, "tn", BF16, "ffn_gate_dw", a_lead="p")
    g["ffn_w_up"] = _matmul(duu, fn, "tn", BF16, "ffn_up_dw", a_lead="p")
    dh2, g["norm_ffn"] = _matmul(dgg, w["wg"], "nn", F32, "ffn_dx_norm_bwd", a_lead="k", b_lead="k", more=(duu, w["wu"]),
                                 epilogue=_norm_bwd_epilogue(h2, w["norm_ffn"], dy))

    g["xa_w_o"] = _matmul(xo, dh2, "tn", BF16, "xa_o_dw")
    def xa_bwd(dxo_, rows, k):
        kvh = _heads(k[0], 2 * XA_HEADS)
        dq_, dk_, dv_ = [], [], []
        dqn, dkn = 0.0, 0.0
        for h, (a, d_) in enumerate(zip(_heads(rows[0], XA_HEADS), _heads(dxo_, XA_HEADS))):
            _, vjp = jax.vjp(_xa_head, a, kvh[h], kvh[XA_HEADS + h], k[1], k[2])
            ga, gk, gv, gqn, gkn = vjp(d_)
            dq_.append(ga)
            dk_.append(gk)
            dv_.append(gv)
            dqn, dkn = dqn + gqn, dkn + gkn
        return [_cat(dq_)], [_cat(dk_ + dv_), dqn, dkn]

    dxq, dxkv, g["xa_q_norm"], g["xa_k_norm"] = _matmul(dh2, w["xo"], "nt", F32, "xa_o_dx_attn_bwd", epilogue=_Epilogue(
        xa_bwd, [_row(xq)], [xkv, w["xa_q_norm"], w["xa_k_norm"]], [(512, BF16)], [xkv.shape, (1, 128), (1, 128)]))
    g["xa_w_q"] = _matmul(hn, dxq, "tn", BF16, "xa_q_dw")
    dh1, g["norm_xa"] = _matmul(dxq, w["xq"], "nt", F32, "xa_q_dx_norm_bwd",
                                epilogue=_norm_bwd_epilogue(h1, w["norm_xa"], dh2))
    g["xa_w_kv"] = _matmul(mn, dxkv, "tn", BF16, "xa_kv_dw")
    dmn = _matmul(dxkv, w["xkv"], "nt", F32, "xa_kv_dx")
    _, g["norm_mem"] = _norm_bwd(mem, w["norm_mem"], dmn, dmn, "norm_mem_bwd")

    g["w_out"] = _matmul(cat, dh1, "tn", BF16, "out_dw")
    def mix_bwd(dcat_, rows, k):
        do_, dog_ = [], []
        dgn = 0.0
        for o, g_, d_ in zip(_heads(rows[0], GLA_HEADS), _heads(rows[1], GLA_HEADS), _heads(dcat_, GLA_HEADS)):
            _, vjp = jax.vjp(_mix_head, o, g_, k[0])
            a, b, gn_ = vjp(d_)
            do_.append(a)
            dog_.append(b)
            dgn = dgn + gn_
        return [_cat(do_), _cat(dog_), dcat_[:, 512:]], [dgn]

    do_gla, d_og, do_mla, g["gla_out_norm"] = _matmul(dh1, w["out"], "nt", F32, "out_dx_mix_bwd", epilogue=_Epilogue(
        mix_bwd, [_row(o_gla), og], [w["gla_out_norm"]], [(512, F32), (512, BF16), (512, F32)], [(1, 128)]))

    late_parts = _late_grad_shards(g)
    dq_r, dk_r, dv_mla, lands_late = _attn_bwd(q_r, k_r, v_mla, o_mla, lse, do_mla,
                                               _scatter_plan([late_parts[n] for n in LATE]))
    lands_late = dict(zip(LATE, lands_late, strict=True))

    def qk_bwd(r, k):
        q_up, k_up = _dg(r[0], k[0], _NN), _dg(r[1], k[1], _NN)
        dqs, dks = [], []
        dkpe, dqn, dkn = 0.0, 0.0, 0.0
        for qh, kh, dqh, dkh in zip(_heads(q_up, MLA_HEADS), _heads(k_up, MLA_HEADS), _heads(r[6], MLA_HEADS),
                                    _heads(r[7], MLA_HEADS)):
            _, vjp = jax.vjp(lambda a, b, e, f, h_: _qk_head(a, b, e, r[3], r[4], r[5], f, h_), qh, kh, r[2], k[3], k[4])
            ga, gb, ge, gf, gh = vjp((dqh, dkh))
            dqs.append(ga)
            dks.append(gb)
            dkpe, dqn, dkn = dkpe + ge, dqn + gf, dkn + gh
        dq_up, dk_up, dv = _cat(dqs), _cat(dks), r[8]
        dq_lat_ = _dg(dq_up, k[0], _NT)
        dkv_lat_ = _dg(dk_up, k[1], _NT) + _dg(dv, k[2], _NT)
        return [dq_lat_, dkv_lat_, dkpe], [dqn, dkn, _dg(r[0], dq_up, _TN), _dg(r[1], dk_up, _TN), _dg(r[1], dv, _TN)]

    dq_lat, dkv_lat, d_kpe, g["q_norm"], g["k_norm"], g["uq"], g["k"], g["v"] = _rows_call(
        qk_bwd, [_row(q_lat), _row(kv_lat), kpe] + tabs + [_row(dq_r), _row(dk_r), _row(dv_mla)], qk_consts,
        [(MLA_Q_RANK, F32), (MLA_KV_RANK, F32), (128, BF16)],
        [(1, 128), (1, 128), w["uq"].shape, w["k"].shape, w["v"].shape], name="mla_qk_bwd")

    dgq, dgk, dla, dgv = _gla_bwd(proj, la, states, do_gla)

    def dproj_body(r, k):
        alr_, cq_, ckv_, dla_, dq_lat_, dkv_lat_, dgq_, dgk_, dgv_, d_og_, d_kpe_ = r
        _, gate_vjp = jax.vjp(_gate_fn, alr_, k[0], k[1])
        d_alr, gw2, gb = gate_vjp(dla_)
        _, q_vjp = jax.vjp(_rms, cq_, k[2])
        _, kv_vjp = jax.vjp(_rms, ckv_, k[3])
        d_cq, gqa = q_vjp(dq_lat_)
        d_ckv, gkva = kv_vjp(dkv_lat_)
        pieces = [dgq_, dgk_, dgv_, d_og_, d_cq, d_ckv, d_kpe_, d_alr]
        return [_cat([x_.astype(BF16) for x_ in pieces])], [gw2, gb, gqa, gkva]

    dproj, g["w2"], g["gla_gate_b"], g["mla_q_a_norm"], g["mla_kv_a_norm"] = _rows_call(
        dproj_body, [alr, cq, ckv, _row(dla), _row(dq_lat), _row(dkv_lat), _row(dgq), _row(dgk), _row(dgv), _row(d_og),
                     _row(d_kpe)], [w["w2"], w["gate_b"], w["q_a_norm"], w["kv_a_norm"]], [(P_WIDTH, BF16)],
        [(128, 256), (1, 256), (1, 256), (1, 128)], name="proj_cotangent")
    g["in"] = _matmul(dproj, xn, "tn", BF16, "proj_dw")
    dx, g["norm_mix"] = _matmul(dproj, w["in"], "nn", F32, "proj_dx_norm_bwd",
                                epilogue=_norm_bwd_epilogue(x, w["norm_mix"], dh1))
    return loss[0, 0], dx, g, lands_late


def _join_shards(pieces, axis):
    if axis == 0:
        return pieces.reshape(-1, pieces.shape[2])
    return jnp.transpose(pieces, (1, 0, 2)).reshape(pieces.shape[1], -1)


def _split_shards(full, axis):
    r, c = full.shape
    if axis == 0:
        return full.reshape(4, r // 4, c)
    return jnp.transpose(full.reshape(r, 4, c // 4), (1, 0, 2))


def _early_layout(gath, rep):
    w_in = gath["w_in"].reshape(N_WIDTH, D_MODEL)
    z = lambda n: jnp.zeros((n, D_MODEL), w_in.dtype)
    seg = lambda lo, n: w_in[lo:lo + n]
    ukv = _join_shards(gath["mla_w_ukv"], 1).reshape(MLA_KV_RANK, MLA_HEADS, MLA_NOPE + MLA_V)
    w = {
        "in": jnp.concatenate([seg(N_GQ, 256), seg(N_GK, 256), seg(N_GV, 512), seg(N_OG, 512), seg(N_CQ, 256),
                               seg(N_CKV, 128), z(64), seg(N_KPE, 32), z(32), seg(N_ALR, 16), z(112)], axis=0),
        "uq": jnp.pad(_join_shards(gath["mla_w_uq"], 1).reshape(MLA_Q_RANK, MLA_HEADS, MLA_QK),
                      ((0, 0), (0, 0), (0, LANES - MLA_QK))).reshape(MLA_Q_RANK, MLA_HEADS * LANES),
        "k": jnp.pad(ukv[:, :, :MLA_NOPE], ((0, 0), (0, 0), (0, LANES - MLA_NOPE))).reshape(MLA_KV_RANK, -1),
        "v": ukv[:, :, MLA_NOPE:].reshape(MLA_KV_RANK, MLA_HEADS * MLA_V),
        "w2": jnp.pad(_join_shards(gath["gla_gate_w2"], 1), ((0, LANES - GLA_RANK), (0, 0))),
        "cb": rep["ffn_conv_b"].reshape(4, 1, D_FF // 4),
        "q_norm": jnp.pad(rep["mla_q_norm"], ((0, 0), (0, LANES - MLA_QK))),
        "k_norm": jnp.pad(rep["mla_k_norm"], ((0, 0), (0, LANES - MLA_QK))),
        "q_a_norm": rep["mla_q_a_norm"], "kv_a_norm": rep["mla_kv_a_norm"], "gate_b": rep["gla_gate_b"],
    }
    for n in ("norm_mix", "gla_out_norm", "norm_xa", "norm_mem", "xa_q_norm", "xa_k_norm", "norm_ffn"):
        w[n] = rep[n]
    return w


def _late_layout(gath):
    return {"out": _join_shards(gath["w_out"], 0), "xq": _join_shards(gath["xa_w_q"], 0),
            "xkv": _join_shards(gath["xa_w_kv"], 0), "xo": _join_shards(gath["xa_w_o"], 1),
            "wg": gath["ffn_w_gate"], "wu": gath["ffn_w_up"], "wd": gath["ffn_w_down"], "cw": gath["ffn_conv_w"]}


def _late_grad_shards(g):
    sh = {"w_out": _split_shards(g["w_out"], 0), "xa_w_q": _split_shards(g["xa_w_q"], 0),
          "xa_w_kv": _split_shards(g["xa_w_kv"], 0), "xa_w_o": _split_shards(g["xa_w_o"], 1),
          "ffn_w_gate": g["ffn_w_gate"], "ffn_w_up": g["ffn_w_up"], "ffn_conv_w": g["ffn_conv_w"],
          "ffn_w_down": g["ffn_w_down"]}
    return {n: v.astype(BF16) for n, v in sh.items()}


def _early_grad_shards(g):
    gi = g["in"]
    seg = lambda lo, n: gi[lo:lo + n]
    w_in = jnp.concatenate([seg(P_GQ, 256), seg(P_GK, 256), seg(P_GV, 512), seg(P_ALR, 16), seg(P_OG, 512),
                            seg(P_CQ, 256), seg(P_CKV, 128), seg(P_KPE + 64, 32)], axis=0)
    uq = g["uq"].reshape(MLA_Q_RANK, MLA_HEADS, LANES)[:, :, :MLA_QK].reshape(MLA_Q_RANK, -1)
    ukv = jnp.concatenate([g["k"].reshape(MLA_KV_RANK, MLA_HEADS, LANES)[:, :, :MLA_NOPE],
                           g["v"].reshape(MLA_KV_RANK, MLA_HEADS, MLA_V)], axis=2).reshape(MLA_KV_RANK, -1)
    sh = {"w_in": w_in.reshape(4, N_WIDTH // 4, D_MODEL), "gla_gate_w2": _split_shards(g["w2"][:GLA_RANK], 1),
          "mla_w_uq": _split_shards(uq, 1), "mla_w_ukv": _split_shards(ukv, 1)}
    sh = {n: v.astype(BF16) for n, v in sh.items()}
    rep = {n: g[n] for n in REPLICATED if n in g}
    rep["mla_q_norm"] = g["q_norm"][:, :MLA_QK]
    rep["mla_k_norm"] = g["k_norm"][:, :MLA_QK]
    rep["ffn_conv_b"] = g["ffn_conv_b"].reshape(1, D_FF)
    return sh, rep


SMALL_SHAPE = (8, 1024)


def _pack_small(vectors):
    flat = jnp.concatenate(vectors, axis=1)
    return jnp.pad(flat, ((0, 0), (0, SMALL_SHAPE[0] * SMALL_SHAPE[1] - flat.shape[1]))).reshape(SMALL_SHAPE)


def _unpack_small(buf, widths):
    flat = buf.reshape(1, -1)
    out, off = [], 0
    for wd in widths:
        out.append(flat[:, off:off + wd])
        off += wd
    return out


ANY = pl.BlockSpec(memory_space=pl.ANY)


def _place():
    x, y, c = lax.axis_index("x"), lax.axis_index("y"), lax.axis_index("c")
    chips = [(1 - x, y), (x, 1 - y), (1 - x, 1 - y)]
    return x, y, c, chips


class _Comm:
    def __init__(self, ins, out_shape, sems, start, finish, mid=None):
        self.ins, self.out_shape, self.sems = list(ins), list(out_shape), list(sems)
        self.start, self.finish, self.mid = start, finish, mid or (lambda *args: None)


def _run_comm(plan, name):
    ni, no = len(plan.ins), len(plan.out_shape)

    def body(*refs):
        ins, outs, sems = refs[:ni], refs[ni:ni + no], refs[ni + no:]
        place = _place()
        plan.start(place, ins, outs, sems)
        plan.mid(place, ins, outs, sems)
        plan.finish(place, ins, outs, sems)

    return pl.pallas_call(body, in_specs=[ANY] * ni, out_specs=[ANY] * no, out_shape=plan.out_shape,
                          scratch_shapes=plan.sems, name=name)(*plan.ins)


def _gather_plan(shards):
    n = len(shards)
    by_rows = [s.shape[0] % (2 * BF16_ROWS) == 0 for s in shards]
    by_cols = [not r and s.shape[1] % (2 * LANES) == 0 for r, s in zip(by_rows, shards)]
    split = [r or c for r, c in zip(by_rows, by_cols)]

    def rows(ref, t, c):
        if by_rows[t]:
            half = shards[t].shape[0] // 2
            return ref.at[pl.ds(pl.multiple_of(c * half, BF16_ROWS), half)]
        if by_cols[t]:
            half = shards[t].shape[1] // 2
            return ref.at[:, pl.ds(pl.multiple_of(c * half, LANES), half)]
        return ref

    def remote(src, dst, ss, rs, to):
        return pltpu.make_async_remote_copy(src_ref=src, dst_ref=dst, send_sem=ss, recv_sem=rs, device_id=to,
                                            device_id_type=MESH)

    def first_wave(place, ins, outs, sems):
        x, y, c, chips = place
        ici_s, ici_r, _, _, local = sems
        me = 2 * x + y
        own = [pltpu.make_async_copy(ins[t], outs[t].at[me], local.at[t]) for t in range(n)]
        push = [remote(rows(ins[t], t, c), rows(outs[t].at[me], t, c), ici_s.at[3 * t + j], ici_r.at[3 * t + j], (px, py, c))
                for t in range(n) for j, (px, py) in enumerate(chips)]
        return own, push

    def second_wave(place, ins, outs, sems, last):
        x, y, c, chips = place
        ici_s, ici_r, d2d_s, d2d_r, local = sems
        sib = (x, y, 1 - c)
        out = []
        for t in range(n):
            for j, (px, py) in enumerate(chips):
                block = outs[t].at[2 * px + py]
                got = rows(block, t, c)
                if split[t]:
                    hand = remote(got, got, d2d_s.at[3 * t + j], d2d_r.at[3 * t + j], sib)
                    theirs = rows(block, t, 1 - c)
                    other = (remote(theirs, theirs, local.at[0], d2d_r.at[3 * t + j], sib) if last else
                             remote(got, got, local.at[0], ici_r.at[3 * t + j], sib))
                    out.append((other, hand))
                elif last:
                    out.append((remote(got, got, local.at[0], ici_r.at[3 * t + j], sib), None))
        return out

    def start(place, ins, outs, sems):
        own, push = first_wave(place, ins, outs, sems)
        for cp in own + push:
            cp.start()

    def mid(place, ins, outs, sems):
        for arrival, hand in second_wave(place, ins, outs, sems, False):
            arrival.wait_recv()
            hand.start()

    def finish(place, ins, outs, sems):
        own, push = first_wave(place, ins, outs, sems)
        for arrival, hand in second_wave(place, ins, outs, sems, True):
            arrival.wait_recv()
            if hand is not None:
                hand.wait_send()
        for cp in push:
            cp.wait_send()
        for cp in own:
            cp.wait()

    dma = pltpu.SemaphoreType.DMA
    return _Comm(shards, [jax.ShapeDtypeStruct((4,) + s.shape, s.dtype) for s in shards],
                 [dma((3 * n,)), dma((3 * n,)), dma((3 * n,)), dma((3 * n,)), dma((n,))], start, finish, mid)


def _scatter_plan(parts, small=None):
    n = len(parts)
    ns = 0 if small is None else 1

    def unpack(place, ins, outs, sems):
        x, y, c, chips = place
        return x, y, c, chips, 2 * x + y, 4 * x + 2 * y + c, (x, y, 1 - c)

    def remote(src, dst, ss, rs, to):
        return pltpu.make_async_remote_copy(src_ref=src, dst_ref=dst, send_sem=ss, recv_sem=rs, device_id=to,
                                            device_id_type=MESH)

    def first_wave(place, ins, outs, sems):
        x, y, c, chips, me, dev, sib = unpack(place, ins, outs, sems)
        ici_s, ici_r, d2d_s, d2d_r, sm_s, sm_r, local = sems
        own, push = [], []
        if ns:
            own.append(pltpu.make_async_copy(ins[n], outs[n].at[dev], local.at[n]))
            for k in range(1, 8):
                px = (1 - x) if (k >> 2) & 1 else x
                py = (1 - y) if (k >> 1) & 1 else y
                pc = (1 - c) if k & 1 else c
                push.append(remote(ins[n], outs[n].at[dev], sm_s.at[k - 1], sm_r.at[k - 1], (px, py, pc)))
        for t in range(n):
            own.append(pltpu.make_async_copy(ins[t].at[me], outs[t].at[dev], local.at[t]))
            push.append(remote(ins[t].at[me], outs[t].at[dev], d2d_s.at[4 * t], d2d_r.at[4 * t], sib))
            for j, (px, py) in enumerate(chips):
                push.append(remote(ins[t].at[2 * px + py], outs[t].at[dev], ici_s.at[3 * t + j], ici_r.at[3 * t + j],
                                   (px, py, c)))
        return own, push

    def start(place, ins, outs, sems):
        own, push = first_wave(place, ins, outs, sems)
        for cp in own + push:
            cp.start()

    def landed(dst, rs, sems, sib):
        remote(dst, dst, sems[-1].at[0], rs, sib).wait_recv()

    def forwards(place, ins, outs, sems):
        x, y, c, chips, me, dev, sib = unpack(place, ins, outs, sems)
        d2d_s, d2d_r = sems[2], sems[3]
        slots = [(t, j, outs[t].at[4 * px + 2 * py + c]) for t in range(n) for j, (px, py) in enumerate(chips)]
        return [(t, j, slot, remote(slot, slot, d2d_s.at[4 * t + 1 + j], d2d_r.at[4 * t + 1 + j], sib))
                for t, j, slot in slots]

    def mid(place, ins, outs, sems):
        sib = unpack(place, ins, outs, sems)[-1]
        for t, j, slot, cp in forwards(place, ins, outs, sems):
            landed(slot, sems[1].at[3 * t + j], sems, sib)
            cp.start()

    def finish(place, ins, outs, sems):
        x, y, c, chips, me, dev, sib = unpack(place, ins, outs, sems)
        d2d_r, sm_r = sems[3], sems[5]
        own, push = first_wave(place, ins, outs, sems)
        push += [cp for _, _, _, cp in forwards(place, ins, outs, sems)]
        for t in range(n):
            landed(outs[t].at[4 * x + 2 * y + (1 - c)], d2d_r.at[4 * t], sems, sib)
            for j, (px, py) in enumerate(chips):
                landed(outs[t].at[4 * px + 2 * py + (1 - c)], d2d_r.at[4 * t + 1 + j], sems, sib)
        if ns:
            for k in range(1, 8):
                px = (1 - x) if (k >> 2) & 1 else x
                py = (1 - y) if (k >> 1) & 1 else y
                pc = (1 - c) if k & 1 else c
                landed(outs[n].at[4 * px + 2 * py + pc], sm_r.at[k - 1], sems, sib)
        for cp in push:
            cp.wait_send()
        for cp in own:
            cp.wait()

    dma = pltpu.SemaphoreType.DMA
    ins = list(parts) + ([small] if ns else [])
    out_shape = [jax.ShapeDtypeStruct((8,) + p.shape[1:], p.dtype) for p in parts]
    if ns:
        out_shape.append(jax.ShapeDtypeStruct((8,) + small.shape, small.dtype))
    return _Comm(ins, out_shape, [dma((3 * n,)), dma((3 * n,)), dma((4 * n,)), dma((4 * n,)), dma((7,)), dma((7,)),
                                  dma((n + 1,))], start, finish, mid)


ADAM_ROWS = 288


def _row_tile(r, cap):
    if r <= cap:
        return r
    return max((t for t in range(8, cap + 1, 8) if r % t == 0), default=r)


def _adamw_update(w, m, v, land):
    g = land[0].astype(F32)
    for i in range(1, 8):
        g = g + land[i].astype(F32)
    m_new = ADAM_B1 * m + (1.0 - ADAM_B1) * g
    v_new = ADAM_B2 * v + (1.0 - ADAM_B2) * (g * g)
    m_hat = m_new / (1.0 - ADAM_B1 ** ADAM_STEP)
    v_hat = v_new / (1.0 - ADAM_B2 ** ADAM_STEP)
    return g, -ADAM_LR * (m_hat / (jnp.sqrt(v_hat) + ADAM_EPS) + ADAM_WD * w), m_new, v_new


def _adamw(tensors, name, comm=None):
    k = len(tensors)
    r, c = tensors[0][0].shape
    t = _row_tile(r, ADAM_ROWS // k)
    tc = c if t < r or r <= ADAM_ROWS else 2 * LANES
    n = (r // t) * (c // tc)
    nci, nco, nsem = (len(comm.ins), len(comm.out_shape), len(comm.sems)) if comm else (0, 0, 0)

    def kern(*refs):
        ins, cins, outs, couts, csems = _split_refs(refs, (4 * k, nci, 4 * k, nco, nsem))
        if comm:
            place = _place()

            @pl.when(pl.program_id(0) == 0)
            def _():
                comm.start(place, cins, couts, csems)

        for i in range(k):
            w_ref, m_ref, v_ref, l_ref = ins[4 * i:4 * i + 4]
            res = _adamw_update(w_ref[...], m_ref[...], v_ref[...], l_ref)
            for ref, val in zip(outs[4 * i:4 * i + 4], res, strict=True):
                ref[...] = val
        if comm:
            @pl.when(pl.program_id(0) == n - 1)
            def _():
                comm.mid(place, cins, couts, csems)
                comm.finish(place, cins, couts, csems)

    where = (lambda i: (i, 0)) if tc == c else (lambda i: (0, i))
    spec = pl.BlockSpec((t, tc), where)
    lspec = pl.BlockSpec((8, t, tc), lambda i: (0,) + where(i))
    res = pl.pallas_call(
        kern, grid=(n,), in_specs=[spec, spec, spec, lspec] * k + [ANY] * nci, out_specs=[spec] * (4 * k) + [ANY] * nco,
        out_shape=[jax.ShapeDtypeStruct((r, c), F32)] * (4 * k) + (comm.out_shape if comm else []),
        scratch_shapes=comm.sems if comm else [],
        compiler_params=pltpu.CompilerParams(dimension_semantics=("arbitrary" if comm else "parallel",),
                                             vmem_limit_bytes=VMEM_LIMIT),
        name=name)(*[x for tens in tensors for x in tens], *(comm.ins if comm else []))
    return [res[4 * i:4 * i + 4] for i in range(k)], res[4 * k:]


def _step(a):
    def sq(n):
        v = a[n][0] if a[n].ndim == 3 else a[n]
        return v.T if n.removeprefix("m_").removeprefix("v_") in TRANSPOSED else v

    payload = lambda n: sq(n) if n in EXACT_GATHER else sq(n).astype(BF16)

    loss, dx, g, lands_late = _local_step(sq("x"), sq("mem"), a["positions"][0], sq("loss_target"),
                                          {n: a[n] for n in REPLICATED}, [payload(n) for n in EARLY],
                                          [payload(n) for n in LATE])

    sh, rep = _early_grad_shards(g)
    small = _pack_small([rep[n] for n in REPLICATED] + [loss.reshape(1, 1)])
    *lands_early, land_small = _run_comm(_scatter_plan([sh[n] for n in EARLY], small), "scatter_last")
    quad = lambda n, land: (sq(n), sq("m_" + n), sq("v_" + n), land)
    lands = dict(zip(EARLY, lands_early, strict=True)) | lands_late

    outs = {}
    kinds = ("grad_", "delta_", "new_m_", "new_v_")
    for n, _ in SHARDED:
        res = _adamw([quad(n, lands[n])], "adamw_" + n)[0][0]
        for kind, val in zip(kinds, res, strict=True):
            outs[kind + n] = (val.T if n in TRANSPOSED else val).reshape(a[n].shape)
    zero = jnp.zeros((1, 1), F32)
    packed = [_pack_small([a[p + n] for n in REPLICATED] + [zero]) for p in ("", "m_", "v_")]
    res = _adamw([(*packed, land_small)], "adamw_replicated")[0][0]
    widths = [a[n].shape[1] for n in REPLICATED] + [1]
    for kind, buf in zip(kinds, res, strict=True):
        *vals, total = _unpack_small(buf, widths)
        for n, val in zip(REPLICATED, vals, strict=True):
            outs[kind + n] = val
        if kind == "grad_":
            loss = total[0, 0]

    ordered = [outs[kind + n] for kind in kinds for n in WEIGHTS]
    return (loss, dx[None], *ordered)


def kernel(x, mem, positions, norm_mix, w_in, gla_gate_w2, gla_gate_b, gla_out_norm, mla_q_a_norm, mla_w_uq, mla_kv_a_norm, mla_w_ukv, mla_q_norm, mla_k_norm, w_out, norm_xa, norm_mem, xa_w_q, xa_w_kv, xa_q_norm, xa_k_norm, xa_w_o, norm_ffn, ffn_w_gate, ffn_w_up, ffn_conv_w, ffn_conv_b, ffn_w_down, loss_target, m_norm_mix, m_w_in, m_gla_gate_w2, m_gla_gate_b, m_gla_out_norm, m_mla_q_a_norm, m_mla_w_uq, m_mla_kv_a_norm, m_mla_w_ukv, m_mla_q_norm, m_mla_k_norm, m_w_out, m_norm_xa, m_norm_mem, m_xa_w_q, m_xa_w_kv, m_xa_q_norm, m_xa_k_norm, m_xa_w_o, m_norm_ffn, m_ffn_w_gate, m_ffn_w_up, m_ffn_conv_w, m_ffn_conv_b, m_ffn_w_down, v_norm_mix, v_w_in, v_gla_gate_w2, v_gla_gate_b, v_gla_out_norm, v_mla_q_a_norm, v_mla_w_uq, v_mla_kv_a_norm, v_mla_w_ukv, v_mla_q_norm, v_mla_k_norm, v_w_out, v_norm_xa, v_norm_mem, v_xa_w_q, v_xa_w_kv, v_xa_q_norm, v_xa_k_norm, v_xa_w_o, v_norm_ffn, v_ffn_w_gate, v_ffn_w_up, v_ffn_conv_w, v_ffn_conv_b, v_ffn_w_down):
    return _step(dict(locals()))
```

```python
import functools

import jax
import jax.numpy as jnp
import numpy as np
from jax import lax
from jax.experimental import pallas as pl
from jax.experimental.pallas import tpu as pltpu

F32, BF16 = jnp.float32, jnp.bfloat16
MESH = pl.DeviceIdType.MESH

D_MODEL = 1024
EPS = 1e-6
GLA_HEADS, GLA_DK, GLA_DV, GLA_RANK, GLA_CHUNK = 4, 64, 128, 16, 64
GLA_GATE_NORM = 16.0
MLA_HEADS, MLA_Q_RANK, MLA_KV_RANK, MLA_NOPE, MLA_ROPE, MLA_V = 8, 256, 128, 64, 32, 64
MLA_QK = MLA_NOPE + MLA_ROPE
ROPE_THETA = 10000.0
LOG2E, LN2 = 1.4426950408889634, 0.6931471805599453
XA_HEADS, XA_DIM = 4, 128
D_FF = 2816
ADAM_LR, ADAM_B1, ADAM_B2, ADAM_EPS, ADAM_WD, ADAM_STEP = 0.001, 0.9, 0.999, 1e-08, 0.01, 10

LANES = 128
BF16_ROWS = 16
VMEM_LIMIT = 56 * 1024 * 1024
MATMUL_VMEM = 44 * 1024 * 1024
ROW_TILE = 512

P_GQ, P_GK, P_GV, P_OG, P_CQ, P_CKV, P_KPE, P_ALR, P_WIDTH = 0, 256, 512, 1024, 1536, 1792, 1920, 2048, 2176
N_GQ, N_GK, N_GV, N_ALR, N_OG, N_CQ, N_CKV, N_KPE, N_WIDTH = 0, 256, 512, 1024, 1040, 1552, 1808, 1936, 1968

SHARDED = (("w_in", 1), ("gla_gate_w2", 1), ("mla_w_uq", 1), ("mla_w_ukv", 1), ("w_out", 0), ("xa_w_q", 0),
           ("xa_w_kv", 0), ("xa_w_o", 1), ("ffn_w_gate", 1), ("ffn_w_up", 1), ("ffn_conv_w", 1), ("ffn_w_down", 0))
REPLICATED = ("norm_mix", "gla_gate_b", "gla_out_norm", "mla_q_a_norm", "mla_kv_a_norm", "mla_q_norm", "mla_k_norm",
              "norm_xa", "norm_mem", "xa_q_norm", "xa_k_norm", "norm_ffn", "ffn_conv_b")
EXACT_GATHER = ("gla_gate_w2", "ffn_conv_w")
TRANSPOSED = ("w_in", "ffn_w_gate", "ffn_w_up")
EARLY = ("w_in", "gla_gate_w2", "mla_w_uq", "mla_w_ukv")
LATE = tuple(n for n, _ in SHARDED if n not in EARLY)
WEIGHTS = ("norm_mix", "w_in", "gla_gate_w2", "gla_gate_b", "gla_out_norm", "mla_q_a_norm", "mla_w_uq",
           "mla_kv_a_norm", "mla_w_ukv", "mla_q_norm", "mla_k_norm", "w_out", "norm_xa", "norm_mem", "xa_w_q",
           "xa_w_kv", "xa_q_norm", "xa_k_norm", "xa_w_o", "norm_ffn", "ffn_w_gate", "ffn_w_up", "ffn_conv_w",
           "ffn_conv_b", "ffn_w_down")


_NN = ((1,), (0,))
_NT = ((1,), (1,))
_TN = ((0,), (0,))


def _dg(a, b, dims):
    return lax.dot_general(a.astype(BF16), b.astype(BF16), (dims, ((), ())), preferred_element_type=F32)


@jax.custom_vjp
def _dot_nn(a, b):
    return _dg(a, b, _NN)


_dot_nn.defvjp(lambda a, b: (_dg(a, b, _NN), (a, b)),
               lambda r, g: (_dg(g, r[1], _NT).astype(r[0].dtype), _dg(r[0], g, _TN).astype(r[1].dtype)))


@jax.custom_vjp
def _dot_nt(a, b):
    return _dg(a, b, _NT)


_dot_nt.defvjp(lambda a, b: (_dg(a, b, _NT), (a, b)),
               lambda r, g: (_dg(g, r[1], _NN).astype(r[0].dtype), _dg(g, r[0], _TN).astype(r[1].dtype)))


@jax.custom_vjp
def _dot_tn(a, b):
    return _dg(a, b, _TN)


_dot_tn.defvjp(lambda a, b: (_dg(a, b, _TN), (a, b)),
               lambda r, g: (_dg(r[1], g, _NT).astype(r[0].dtype), _dg(r[0], g, _NN).astype(r[1].dtype)))


def _rms(x, w, n=None):
    n = x.shape[-1] if n is None else n
    ms = jnp.sum(x * x, axis=-1, keepdims=True) * (1.0 / n)
    return x * lax.rsqrt(ms + EPS) * w


def _silu(x):
    return x * jax.nn.sigmoid(x)


def _log_sigmoid(x):
    return jnp.minimum(x, 0.0) - jnp.log(1.0 + jnp.exp(-jnp.abs(x)))


@jax.custom_vjp
def _cumsum_rows(x):
    n = x.shape[0]
    row = lax.broadcasted_iota(jnp.int32, x.shape, 0)
    k = 1
    while k < n:
        x = x + jnp.where(row >= k, pltpu.roll(x, k, 0), 0.0)
        k *= 2
    return x


def _cumsum_rows_bwd(_, g):
    n = g.shape[0]
    row = lax.broadcasted_iota(jnp.int32, g.shape, 0)
    k = 1
    while k < n:
        g = g + jnp.where(row < n - k, pltpu.roll(g, n - k, 0), 0.0)
        k *= 2
    return (g,)


_cumsum_rows.defvjp(lambda x: (_cumsum_rows(x), None), _cumsum_rows_bwd)


def _lane_mask(lo, hi):
    lane = lax.broadcasted_iota(jnp.int32, (1, LANES), 1)
    return ((lane >= lo) & (lane < hi)).astype(F32)


def _tile(n, t):
    t = min(n, t)
    assert n % t == 0, (n, t)
    return t


class _Epilogue:
    def __init__(self, fn, rows=(), consts=(), outs=(), accs=()):
        self.fn, self.rows, self.consts, self.outs, self.accs = fn, list(rows), list(consts), list(outs), list(accs)


def _matmul(a, b, mode, out_dtype, name, residual=None, a_lead=None, b_lead=None, more=None, epilogue=None):
    (a0, a1), (b0, b1) = a.shape[-2:], b.shape[-2:]
    if mode == "nn":
        m, k, k2, n = a0, a1, b0, b1
    elif mode == "nt":
        m, k, n, k2 = a0, a1, b0, b1
    else:
        k, m, k2, n = a0, a1, b0, b1
    assert k == k2, (a.shape, b.shape, mode)
    npar = 4 if "p" in (a_lead, b_lead) else 1
    nsum = 4 if "k" in (a_lead, b_lead) else 1
    pairs = [(a, b)] + ([more] if more else [])
    a_item, b_item, o_item = a.dtype.itemsize, b.dtype.itemsize, jnp.dtype(out_dtype).itemsize
    ep = epilogue
    row_extra = 4 if residual is not None else 0
    if ep:
        row_extra += (sum(r.dtype.itemsize * wd for r, wd, _ in ep.rows) + sum(jnp.dtype(d).itemsize * wd for wd, d in ep.outs)) / n

    def resident(lead, tiles):
        return lead != "p" and tiles == 1

    def vmem_need(tm, tn, tk):
        a_bufs = 1 if resident(a_lead, (m // tm) * (k // tk)) else 2
        b_bufs = 1 if resident(b_lead, (n // tn) * (k // tk)) else 2
        need = a_bufs * (nsum if a_lead == "k" else 1) * tm * tk * a_item + b_bufs * (nsum if b_lead == "k" else 1) * tk * tn * b_item
        need *= len(pairs)
        need += (0 if ep else 2 * tm * tn * o_item) + tm * tn * 4 * (2 if tk < k else 1)
        need += tm * tk * 2 * (a_item == 4 or mode == "tn") + tk * tn * 2 * (b_item == 4)
        return need + int(2 * tm * tn * row_extra) + (3 * tm * tn * 4 if ep else 0)

    halvings = (4096, 2048, 1024, 512, 256, 128, 64, 32, 16, 8)
    if mode == "tn":
        tm = m if m <= 2304 else m // 2
        tn = n if tm * n <= 1024 * 2304 else n // 2
        tk = next((r for r in halvings if k % r == 0 and vmem_need(tm, tn, r) <= MATMUL_VMEM), k)
    else:
        tn, tk = n, k
        tm = next((r for r in halvings if m % r == 0 and vmem_need(r, tn, tk) <= MATMUL_VMEM), m)
    assert m % tm == 0 and n % tn == 0 and k % tk == 0
    assert ep is None or (tn == n and tk == k and npar == 1)
    nk = k // tk
    dims = {"nn": _NN, "nt": _NT, "tn": _TN}[mode]
    n_in = 2 * len(pairs) + (residual is not None)
    n_ep_in = len(ep.rows) + len(ep.consts) if ep else 0
    n_out = len(ep.outs) + len(ep.accs) if ep else 1

    def body(*refs):
        ab, rs, ep_in, outs, scratch = _split_refs(refs, (2 * len(pairs), n_in - 2 * len(pairs), n_ep_in, n_out, nk > 1))
        prod = None
        for a_ref, b_ref in zip(ab[0::2], ab[1::2]):
            for sh in range(nsum):
                term = _dg(a_ref[sh] if a_lead == "k" else a_ref[...], b_ref[sh] if b_lead == "k" else b_ref[...], dims)
                prod = term if prod is None else prod + term

        def finish(r):
            if rs:
                r = r + rs[0][...]
            if ep is None:
                outs[0][...] = r.astype(outs[0].dtype)
                return
            vals = [x[...] for x in ep_in]
            ro, ao = ep.fn(r, vals[:len(ep.rows)], vals[len(ep.rows):])
            for ref, val in zip(outs[:len(ep.outs)], ro, strict=True):
                ref[...] = val.astype(ref.dtype)
            if ep.accs:
                @pl.when(pl.program_id(0) == 0)
                def _():
                    for ref in outs[len(ep.outs):]:
                        ref[...] = jnp.zeros_like(ref)

                for ref, val in zip(outs[len(ep.outs):], ao, strict=True):
                    ref[...] += val

        if nk == 1:
            finish(prod)
            return
        acc = scratch[0]
        kk = pl.program_id(3)

        @pl.when(kk == 0)
        def _():
            acc[...] = prod

        @pl.when(kk > 0)
        def _():
            acc[...] += prod

        @pl.when(kk == nk - 1)
        def _():
            finish(acc[...])

    def spec(lead, blk, idx, tiles=0):
        mode = {"pipeline_mode": pl.Buffered(1)} if resident(lead, tiles) else {}
        if lead is None:
            return pl.BlockSpec(blk, lambda i, j, p, kk: idx(i, j, kk), **mode)
        if lead == "p":
            return pl.BlockSpec((None,) + blk, lambda i, j, p, kk: (p,) + idx(i, j, kk))
        return pl.BlockSpec((nsum,) + blk, lambda i, j, p, kk: (0,) + idx(i, j, kk), **mode)

    a_tiles, b_tiles = (m // tm) * nk, (n // tn) * nk
    if mode == "nn":
        pair_specs = [spec(a_lead, (tm, tk), lambda i, j, kk: (i, kk), a_tiles),
                      spec(b_lead, (tk, tn), lambda i, j, kk: (kk, j), b_tiles)]
    elif mode == "nt":
        pair_specs = [spec(a_lead, (tm, tk), lambda i, j, kk: (i, kk), a_tiles),
                      spec(b_lead, (tn, tk), lambda i, j, kk: (j, kk), b_tiles)]
    else:
        pair_specs = [spec(a_lead, (tk, tm), lambda i, j, kk: (kk, i), a_tiles),
                      spec(b_lead, (tk, tn), lambda i, j, kk: (kk, j), b_tiles)]
    tile = spec(None, (tm, tn), lambda i, j, kk: (i, j))
    in_specs = pair_specs * len(pairs)
    args = [x for pair in pairs for x in pair]
    if residual is not None:
        assert npar == 1
        in_specs.append(tile)
        args.append(residual)
    if ep:
        in_specs += [pl.BlockSpec((tm, wd), functools.partial(lambda cb, i, j, p, kk: (i, cb), cb)) for _, wd, cb in ep.rows]
        in_specs += [pl.BlockSpec(c.shape, lambda i, j, p, kk: (0, 0)) for c in ep.consts]
        args += [r for r, _, _ in ep.rows] + ep.consts
        out_specs = [pl.BlockSpec((tm, wd), lambda i, j, p, kk: (i, 0)) for wd, _ in ep.outs]
        out_specs += [pl.BlockSpec(shape, lambda i, j, p, kk: (0, 0)) for shape in ep.accs]
        out_shape = [jax.ShapeDtypeStruct((m, wd), d) for wd, d in ep.outs] + [jax.ShapeDtypeStruct(sh, F32) for sh in ep.accs]
    else:
        out_specs = spec("p" if npar > 1 else None, (tm, tn), lambda i, j, kk: (i, j))
        out_shape = jax.ShapeDtypeStruct(((4,) if npar > 1 else ()) + (m, n), out_dtype)
    outer = "arbitrary" if ep and ep.accs else "parallel"
    return pl.pallas_call(
        body, grid=(m // tm, n // tn, npar, nk), in_specs=in_specs, out_specs=out_specs, out_shape=out_shape,
        scratch_shapes=[pltpu.VMEM((tm, tn), F32)] if nk > 1 else [],
        compiler_params=pltpu.CompilerParams(dimension_semantics=(outer, outer, outer, "arbitrary"),
                                             vmem_limit_bytes=VMEM_LIMIT),
        name=name)(*args)


def _row(a, width=None, col_block=0):
    return (a, a.shape[1] if width is None else width, col_block)


def _rows_call(body, rows, consts, outs, accs=(), *, name, tile=ROW_TILE):
    s = rows[0][0].shape[0]
    t = _tile(s, tile)
    nr, nc, no = len(rows), len(consts), len(outs)

    def kern(*refs):
        r = [x[...] for x in refs[:nr]]
        c = [x[...] for x in refs[nr:nr + nc]]
        o_refs = refs[nr + nc:nr + nc + no]
        a_refs = refs[nr + nc + no:]
        ro, ao = body(r, c)
        for ref, val in zip(o_refs, ro, strict=True):
            ref[...] = val.astype(ref.dtype)
        if a_refs:
            @pl.when(pl.program_id(0) == 0)
            def _():
                for ref in a_refs:
                    ref[...] = jnp.zeros_like(ref)

            for ref, val in zip(a_refs, ao, strict=True):
                ref[...] += val

    in_specs = [pl.BlockSpec((t, w), functools.partial(lambda cb, i: (i, cb), cb)) for (_, w, cb) in rows]
    in_specs += [pl.BlockSpec(c.shape, lambda i: (0, 0)) for c in consts]
    out_specs = [pl.BlockSpec((t, w), lambda i: (i, 0)) for (w, _) in outs]
    out_specs += [pl.BlockSpec(shape, lambda i: (0, 0)) for shape in accs]
    out_shape = [jax.ShapeDtypeStruct((s, w), dt) for (w, dt) in outs]
    out_shape += [jax.ShapeDtypeStruct(shape, F32) for shape in accs]
    return pl.pallas_call(
        kern, grid=(s // t,), in_specs=in_specs, out_specs=out_specs, out_shape=out_shape,
        compiler_params=pltpu.CompilerParams(dimension_semantics=("arbitrary" if accs else "parallel",),
                                             vmem_limit_bytes=VMEM_LIMIT),
        name=name)(*[r[0] for r in rows], *consts)


def _gla_chunk(q, k, la, v0, v1, s0, s1):
    c = q.shape[0]
    r = lax.broadcasted_iota(jnp.int32, (c, c), 0)
    cc = lax.broadcasted_iota(jnp.int32, (c, c), 1)
    tril = cc <= r
    cum = _cumsum_rows(la)
    cl = jnp.sum(la, axis=0, keepdims=True)
    qd = q * (GLA_DK ** -0.5) * jnp.exp(cum)
    ki = k * jnp.exp(-cum)
    ke = k * jnp.exp(cl - cum)
    dec = jnp.exp(cl)
    outs, news = [], []
    for h, (v, s) in enumerate(((v0, s0), (v1, s1))):
        mk = _lane_mask(GLA_DK * h, GLA_DK * (h + 1))
        qh = qd * mk
        att = jnp.where(tril, _dot_nt(qh, ki), 0.0)
        outs.append(_dot_nn(att, v) + _dot_nt(qh, s))
        news.append(s * dec + _dot_tn(v, ke * mk))
    return outs[0], outs[1], news[0], news[1]


def _gla_specs(tb, rev_nb=None):
    blk = (lambda b: b) if rev_nb is None else (lambda b: rev_nb - 1 - b)
    q = pl.BlockSpec((tb, 128), lambda p, b: (blk(b), P_GQ // 128 + p))
    k = pl.BlockSpec((tb, 128), lambda p, b: (blk(b), P_GK // 128 + p))
    la = pl.BlockSpec((tb, 128), lambda p, b: (blk(b), p))
    v = pl.BlockSpec((tb, 256), lambda p, b: (blk(b), P_GV // 256 + p))
    o = pl.BlockSpec((tb, 256), lambda p, b: (blk(b), p))
    st = pl.BlockSpec((tb // GLA_CHUNK, 2, 128, 128), lambda p, b: (blk(b), p, 0, 0))
    return q, k, la, v, o, st


def _gla_fwd(proj, la):
    s = proj.shape[0]
    tb = _tile(s, ROW_TILE)
    nb, nch = s // tb, tb // GLA_CHUNK

    def kern(q_ref, k_ref, la_ref, v_ref, o_ref, st_ref, s_sc):
        @pl.when(pl.program_id(1) == 0)
        def _():
            s_sc[...] = jnp.zeros_like(s_sc)

        s0, s1 = s_sc[0], s_sc[1]
        for ci in range(nch):
            sl = slice(ci * GLA_CHUNK, (ci + 1) * GLA_CHUNK)
            st_ref[ci, 0] = s0
            st_ref[ci, 1] = s1
            o0, o1, s0, s1 = _gla_chunk(q_ref[sl, :], k_ref[sl, :], la_ref[sl, :], v_ref[sl, 0:128],
                                        v_ref[sl, 128:256], s0, s1)
            o_ref[sl, 0:128] = o0
            o_ref[sl, 128:256] = o1
        s_sc[0] = s0
        s_sc[1] = s1

    q, k, lasp, v, o, st = _gla_specs(tb)
    return pl.pallas_call(
        kern, grid=(2, nb), in_specs=[q, k, lasp, v], out_specs=[o, st],
        out_shape=[jax.ShapeDtypeStruct((s, 512), F32),
                   jax.ShapeDtypeStruct((s // GLA_CHUNK, GLA_HEADS, 128, 128), F32)],
        scratch_shapes=[pltpu.VMEM((2, 128, 128), F32)],
        compiler_params=pltpu.CompilerParams(dimension_semantics=("parallel", "arbitrary"),
                                             vmem_limit_bytes=VMEM_LIMIT),
        name="gla_fwd")(proj, proj, la, proj)


def _gla_bwd(proj, la, states, d_o):
    s = proj.shape[0]
    tb = _tile(s, ROW_TILE)
    nb, nch = s // tb, tb // GLA_CHUNK

    def kern(q_ref, k_ref, la_ref, v_ref, do_ref, st_ref, dq_ref, dk_ref, dla_ref, dv_ref, ds_sc):
        @pl.when(pl.program_id(1) == 0)
        def _():
            ds_sc[...] = jnp.zeros_like(ds_sc)

        d0, d1 = ds_sc[0], ds_sc[1]
        for ci in reversed(range(nch)):
            sl = slice(ci * GLA_CHUNK, (ci + 1) * GLA_CHUNK)
            _, vjp = jax.vjp(_gla_chunk, q_ref[sl, :], k_ref[sl, :], la_ref[sl, :], v_ref[sl, 0:128],
                             v_ref[sl, 128:256], st_ref[ci, 0], st_ref[ci, 1])
            gq, gk, gla, gv0, gv1, d0, d1 = vjp((do_ref[sl, 0:128], do_ref[sl, 128:256], d0, d1))
            dq_ref[sl, :] = gq
            dk_ref[sl, :] = gk
            dla_ref[sl, :] = gla
            dv_ref[sl, 0:128] = gv0
            dv_ref[sl, 128:256] = gv1
        ds_sc[0] = d0
        ds_sc[1] = d1

    q, k, lasp, v, o, st = _gla_specs(tb, rev_nb=nb)
    return pl.pallas_call(
        kern, grid=(2, nb), in_specs=[q, k, lasp, v, o, st], out_specs=[lasp, lasp, lasp, o],
        out_shape=[jax.ShapeDtypeStruct((s, 256), F32), jax.ShapeDtypeStruct((s, 256), F32),
                   jax.ShapeDtypeStruct((s, 256), F32), jax.ShapeDtypeStruct((s, 512), F32)],
        scratch_shapes=[pltpu.VMEM((2, 128, 128), F32)],
        compiler_params=pltpu.CompilerParams(dimension_semantics=("parallel", "arbitrary"),
                                             vmem_limit_bytes=VMEM_LIMIT),
        name="gla_bwd")(proj, proj, la, proj, d_o, states)


def _causal_keep(t):
    return lax.broadcasted_iota(jnp.int32, (t, t), 1) <= lax.broadcasted_iota(jnp.int32, (t, t), 0)


def _split_refs(refs, counts):
    out, off = [], 0
    for cnt in counts:
        out.append(refs[off:off + cnt])
        off += cnt
    return out


def _causal_blocks(n, key_major):
    pairs = ([(ki, qi) for ki in range(n) for qi in range(ki, n)] if key_major else
             [(ki, qi) for qi in range(n) for ki in range(qi + 1)])
    return np.array([ki for ki, _ in pairs], np.int32), np.array([qi for _, qi in pairs], np.int32)


def _attn_fwd(q, k, v, comm, tile=1024):
    s = q.shape[0]
    t = _tile(s, tile)
    n = s // t
    nci, nco = len(comm.ins), len(comm.out_shape)

    ki_tab, qi_tab = _causal_blocks(n, key_major=False)
    steps = len(ki_tab)

    def kern(ki_ref, qi_ref, *refs):
        (q_ref, k_ref, v_ref), cins, (o_ref, lse_ref), couts, (m_sc, l_sc, acc_sc), csems = _split_refs(
            refs, (3, nci, 2, nco, 3, len(comm.sems)))
        pair, step = pl.program_id(0), pl.program_id(1)
        qi, ki = qi_ref[step], ki_ref[step]
        place = _place()

        @pl.when((pair == 0) & (step == 0))
        def _():
            comm.start(place, cins, couts, csems)

        @pl.when((pair == MLA_HEADS // 2 - 1) & (step == 0))
        def _():
            comm.mid(place, cins, couts, csems)

        first = lax.broadcasted_iota(jnp.int32, (t, LANES), 1) < MLA_V

        @pl.when(ki == 0)
        def _():
            m_sc[...] = jnp.full_like(m_sc, -jnp.inf)
            l_sc[...] = jnp.zeros_like(l_sc)
            acc_sc[...] = jnp.zeros_like(acc_sc)

        def update(rows, cols, masked):
            nr = rows.stop - rows.start
            sel = first[:nr]
            alphas, pvs = [], []
            for h in range(2):
                sc = _dg(q_ref[rows, 128 * h:128 * (h + 1)], k_ref[cols, 128 * h:128 * (h + 1)], _NT)
                if masked:
                    sc = jnp.where(_causal_keep(nr), sc, -jnp.inf)
                m_prev = m_sc[h, rows]
                m_new = jnp.maximum(m_prev, jnp.max(sc, axis=1, keepdims=True))
                alpha = jnp.exp2(m_prev - m_new)
                p = jnp.exp2(sc - m_new[:, 0:1])
                l_sc[h, rows] = alpha * l_sc[h, rows] + jnp.sum(p, axis=1, keepdims=True)
                m_sc[h, rows] = m_new
                alphas.append(alpha)
                pvs.append(_dg(p, v_ref[cols, :], _NN))
            acc_sc[rows] = acc_sc[rows] * jnp.where(sel, alphas[0], alphas[1]) + jnp.where(sel, pvs[0], pvs[1])

        halves = [slice(0, t)] if t % 256 else [slice(0, t // 2), slice(t // 2, t)]

        @pl.when(ki < qi)
        def _():
            for rows in halves:
                for cols in halves:
                    update(rows, cols, False)

        @pl.when(ki == qi)
        def _():
            for i, rows in enumerate(halves):
                for j, cols in enumerate(halves[:i + 1]):
                    update(rows, cols, i == j)

        @pl.when(ki == qi)
        def _():
            l = jnp.where(first, l_sc[0], l_sc[1])
            m = jnp.where(first, m_sc[0], m_sc[1])
            o_ref[...] = acc_sc[...] / l
            lse_ref[...] = m + jnp.log2(l)

        @pl.when((pair == MLA_HEADS // 2 - 1) & (step == steps - 1))
        def _():
            comm.finish(place, cins, couts, csems)

    q_idx = lambda p, st, ki_r, qi_r: (qi_r[st], p)
    k_idx = lambda p, st, ki_r, qi_r: (ki_r[st], p)
    res = pl.pallas_call(
        kern, grid_spec=pltpu.PrefetchScalarGridSpec(
            num_scalar_prefetch=2, grid=(MLA_HEADS // 2, steps),
            in_specs=[pl.BlockSpec((t, 256), q_idx), pl.BlockSpec((t, 256), k_idx), pl.BlockSpec((t, 128), k_idx)]
            + [ANY] * nci,
            out_specs=[pl.BlockSpec((t, 128), q_idx), pl.BlockSpec((t, 128), q_idx)] + [ANY] * nco,
            scratch_shapes=[pltpu.VMEM((2, t, LANES), F32), pltpu.VMEM((2, t, LANES), F32),
                            pltpu.VMEM((t, LANES), F32)] + comm.sems),
        out_shape=[jax.ShapeDtypeStruct((s, 512), F32), jax.ShapeDtypeStruct((s, 512), F32)] + comm.out_shape,
        compiler_params=pltpu.CompilerParams(dimension_semantics=("arbitrary", "arbitrary"),
                                             vmem_limit_bytes=VMEM_LIMIT),
        name="mla_attn_fwd")(ki_tab, qi_tab, q, k, v, *comm.ins)
    return res[0], res[1], res[2:]


def _attn_bwd(q, k, v, o, lse, d_o, comm, tile=ROW_TILE):
    s = q.shape[0]
    t = _tile(s, tile)
    n = s // t
    nci, nco = len(comm.ins), len(comm.out_shape)

    ki_tab, qi_tab = _causal_blocks(n, key_major=True)
    steps = len(ki_tab)

    def kern(ki_ref, qi_ref, *refs):
        (q_ref, k_ref, v_ref, o_ref, lse_ref, do_ref), cins, (dq_ref, dk_ref, dv_ref), couts, (dk_sc, dv_sc), csems = \
            _split_refs(refs, (6, nci, 3, nco, 2, len(comm.sems)))
        pair, step = pl.program_id(0), pl.program_id(1)
        ki, qi = ki_ref[step], qi_ref[step]
        place = _place()

        @pl.when((pair == 0) & (step == 0))
        def _():
            comm.start(place, cins, couts, csems)

        @pl.when((pair == MLA_HEADS // 2 - 1) & (step == 0))
        def _():
            comm.mid(place, cins, couts, csems)

        @pl.when((ki == 0) & (qi == 0))
        def _():
            dq_ref[...] = jnp.zeros_like(dq_ref)

        @pl.when(qi == ki)
        def _():
            dk_sc[...] = jnp.zeros_like(dk_sc)
            dv_sc[...] = jnp.zeros_like(dv_sc)

        def update(diagonal):
            keep = _causal_keep(t)
            d_o = do_ref[...]
            prod = d_o * o_ref[...]
            rows = pl.ds(pl.multiple_of(qi * t, t), t)
            for h in range(2):
                hs = slice(128 * h, 128 * (h + 1))
                mk = _lane_mask(MLA_V * h, MLA_V * (h + 1))
                qh, kh = q_ref[:, hs], k_ref[:, hs]
                sc = _dg(qh, kh, _NT)
                if diagonal:
                    sc = jnp.where(keep, sc, -jnp.inf)
                p = jnp.exp2(sc - lse_ref[:, MLA_V * h:MLA_V * h + 1])
                doh = d_o * mk
                dp = _dg(doh * LN2, v_ref[...], _NT)
                delta = jnp.sum(prod * mk, axis=1, keepdims=True) * LN2
                ds = p * (dp - delta)
                dv_sc[...] += _dg(p, doh, _TN)
                dk_sc[:, hs] += _dg(ds, qh, _TN)
                dq_ref[rows, hs] += _dg(ds, kh, _NN)

        @pl.when(qi > ki)
        def _():
            update(False)

        @pl.when(qi == ki)
        def _():
            update(True)

        @pl.when(qi == n - 1)
        def _():
            dk_ref[...] = dk_sc[...]
            dv_ref[...] = dv_sc[...].astype(dv_ref.dtype)

        @pl.when((pair == MLA_HEADS // 2 - 1) & (step == steps - 1))
        def _():
            comm.finish(place, cins, couts, csems)

    q_idx = lambda p, st, ki_r, qi_r: (qi_r[st], p)
    k_idx = lambda p, st, ki_r, qi_r: (ki_r[st], p)
    res = pl.pallas_call(
        kern, grid_spec=pltpu.PrefetchScalarGridSpec(
            num_scalar_prefetch=2, grid=(MLA_HEADS // 2, steps),
            in_specs=[pl.BlockSpec((t, 256), q_idx), pl.BlockSpec((t, 256), k_idx), pl.BlockSpec((t, 128), k_idx),
                      pl.BlockSpec((t, 128), q_idx), pl.BlockSpec((t, 128), q_idx), pl.BlockSpec((t, 128), q_idx)]
            + [ANY] * nci,
            out_specs=[pl.BlockSpec((s, 256), lambda p, st, ki_r, qi_r: (0, p)), pl.BlockSpec((t, 256), k_idx),
                       pl.BlockSpec((t, 128), k_idx)] + [ANY] * nco,
            scratch_shapes=[pltpu.VMEM((t, 256), F32), pltpu.VMEM((t, 128), F32)] + comm.sems),
        out_shape=[jax.ShapeDtypeStruct((s, 1024), F32), jax.ShapeDtypeStruct((s, 1024), F32),
                   jax.ShapeDtypeStruct((s, 512), BF16)] + comm.out_shape,
        compiler_params=pltpu.CompilerParams(dimension_semantics=("arbitrary", "arbitrary"),
                                             vmem_limit_bytes=VMEM_LIMIT),
        name="mla_attn_bwd")(ki_tab, qi_tab, q, k, v, o, lse, d_o, *comm.ins)
    return res[0], res[1], res[2], res[3:]


def _gate_fn(alr, w2, b):
    return _log_sigmoid(_dot_nn(alr, w2) + b) * (1.0 / GLA_GATE_NORM)


def _make_norm_rope(scale):
    def forward(x, w, c, sa, sb):
        r = lax.rsqrt(jnp.sum(x * x, axis=-1, keepdims=True) * (1.0 / MLA_QK) + EPS)
        y = x * r * w
        out = y * c + pltpu.roll(y, LANES - 16, 1) * sa + pltpu.roll(y, 16, 1) * sb
        return (out if scale == 1.0 else out * scale), r

    @jax.custom_vjp
    def norm_rope(x, w, c, sa, sb):
        return forward(x, w, c, sa, sb)[0]

    def fwd(x, w, c, sa, sb):
        out, r = forward(x, w, c, sa, sb)
        return out, (x, w, c, sa, sb, r)

    def bwd(res, g):
        x, w, c, sa, sb, r = res
        if scale != 1.0:
            g = g * scale
        gy = g * c + pltpu.roll(g * sa, 16, 1) + pltpu.roll(g * sb, LANES - 16, 1)
        xr = x * r
        t = gy * w
        m = jnp.sum(t * xr, axis=-1, keepdims=True) * (1.0 / MLA_QK)
        return r * (t - xr * m), jnp.sum(gy * xr, axis=0, keepdims=True), jnp.zeros_like(c), jnp.zeros_like(sa), jnp.zeros_like(sb)

    norm_rope.defvjp(fwd, bwd)
    return norm_rope


_q_norm_rope = _make_norm_rope(MLA_QK ** -0.5 * LOG2E)
_k_norm_rope = _make_norm_rope(1.0)


def _qk_head(qh, kh, kpe, c, sa, sb, qn, kn):
    kfull = kh + kpe * _lane_mask(MLA_NOPE, MLA_QK)
    return _q_norm_rope(qh, qn, c, sa, sb), _k_norm_rope(kfull, kn, c, sa, sb)


def _mix_head(o, og, gn):
    return _rms(o, gn) * _silu(og)


def _xa_head(xq, xk, xv, qn, kn):
    sc = _dot_nt(_rms(xq, qn), _rms(xk, kn)) * (XA_DIM ** -0.5)
    e = jnp.exp(sc - lax.stop_gradient(jnp.max(sc, axis=1, keepdims=True)))
    p = e / jnp.sum(e, axis=1, keepdims=True)
    return _dot_nn(p, xv)


def _heads(x, n):
    return [x[:, 128 * h:128 * (h + 1)] for h in range(n)]


def _cat(xs):
    return jnp.concatenate(xs, axis=1)


def _norm_fwd(x, w, name):
    return _rows_call(lambda r, c: ([_rms(r[0], c[0])], []), [_row(x)], [w], [(x.shape[1], BF16)], name=name)[0]


def _norm_fwd_epilogue(w):
    return _Epilogue(lambda h, rows, consts: ([h, _rms(h, consts[0])], []), [], [w], [(D_MODEL, F32), (D_MODEL, BF16)], [])


def _norm_bwd_epilogue(x, w, add):
    def fn(d_out, rows, consts):
        _, vjp = jax.vjp(_rms, rows[0], consts[0])
        dx, dw = vjp(d_out)
        return [dx + rows[1]], [dw]

    return _Epilogue(fn, [_row(x), _row(add)], [w], [(D_MODEL, F32)], [w.shape])


def _norm_fwd_comm(x, w, comm, name):
    s, d = x.shape
    t = _tile(s, ROW_TILE)
    n = s // t
    nci, nco = len(comm.ins), len(comm.out_shape)

    def kern(*refs):
        (x_ref, w_ref), cins, (o_ref,), couts, csems = _split_refs(refs, (2, nci, 1, nco, len(comm.sems)))
        place = _place()

        @pl.when(pl.program_id(0) == 0)
        def _():
            comm.start(place, cins, couts, csems)

        o_ref[...] = _rms(x_ref[...], w_ref[...]).astype(o_ref.dtype)

        @pl.when(pl.program_id(0) == n - 1)
        def _():
            comm.mid(place, cins, couts, csems)
            comm.finish(place, cins, couts, csems)

    tile = pl.BlockSpec((t, d), lambda i: (i, 0))
    res = pl.pallas_call(
        kern, grid=(n,), in_specs=[tile, pl.BlockSpec(w.shape, lambda i: (0, 0))] + [ANY] * nci,
        out_specs=[tile] + [ANY] * nco, out_shape=[jax.ShapeDtypeStruct((s, d), BF16)] + comm.out_shape,
        scratch_shapes=comm.sems,
        compiler_params=pltpu.CompilerParams(dimension_semantics=("arbitrary",), vmem_limit_bytes=VMEM_LIMIT),
        name=name)(x, w, *comm.ins)
    return res[0], res[1:]


def _norm_bwd(x, w, d_out, add, name):
    def body(r, c):
        _, vjp = jax.vjp(_rms, r[0], c[0])
        dx, dw = vjp(r[1])
        return [dx + r[2]], [dw]

    return _rows_call(body, [_row(x), _row(d_out), _row(add)], [w], [(x.shape[1], F32)], [w.shape], name=name)


CONV_HALO = BF16_ROWS


def _conv_specs(s, f, t):
    n8 = t // CONV_HALO
    cur = pl.BlockSpec((None, t, f), lambda j, i: (j, i, 0))
    prev = pl.BlockSpec((None, CONV_HALO, f), lambda j, i: (j, jnp.maximum(i * n8 - 1, 0), 0))
    nxt = pl.BlockSpec((None, CONV_HALO, f), lambda j, i: (j, jnp.minimum((i + 1) * n8, s // CONV_HALO - 1), 0))
    cw = pl.BlockSpec((None, 3, f), lambda j, i: (j, 0, 0))
    cb = pl.BlockSpec((None, 1, f), lambda j, i: (j, 0, 0))
    return cur, prev, nxt, cw, cb


def _conv_taps(g, prev, first):
    ext = jnp.concatenate([jnp.where(first, 0.0, prev.astype(F32)), g], axis=0)
    return pltpu.roll(ext, 1, 0)[CONV_HALO:], pltpu.roll(ext, 2, 0)[CONV_HALO:]


def _conv_fwd(gg, uu, cw, cb):
    _, s, f = gg.shape
    t = _tile(s, ROW_TILE)

    def kern(g_ref, gp_ref, u_ref, cw_ref, cb_ref, o_ref):
        g = g_ref[...].astype(F32)
        g1, g2 = _conv_taps(g, gp_ref[...], pl.program_id(1) == 0)
        w = cw_ref[...]
        gc = cb_ref[...] + w[0:1] * g2 + w[1:2] * g1 + w[2:3] * g
        o_ref[...] = (_silu(gc) * u_ref[...].astype(F32)).astype(o_ref.dtype)

    cur, prev, _, cws, cbs = _conv_specs(s, f, t)
    return pl.pallas_call(
        kern, grid=(4, s // t), in_specs=[cur, prev, cur, cws, cbs], out_specs=cur,
        out_shape=jax.ShapeDtypeStruct(gg.shape, BF16),
        compiler_params=pltpu.CompilerParams(dimension_semantics=("parallel", "parallel"), vmem_limit_bytes=VMEM_LIMIT),
        name="ffn_conv_fwd")(gg, gg, uu, cw, cb)


def _conv_bwd(gg, uu, dact, cw, cb):
    _, s, f = gg.shape
    t = _tile(s, ROW_TILE)
    nt = s // t

    def kern(g_ref, gp_ref, gn_ref, u_ref, un_ref, da_ref, dan_ref, cw_ref, cb_ref, du_ref, dg_ref, dcw_ref, dcb_ref):
        i = pl.program_id(1)
        cat = lambda a_ref, b_ref: jnp.concatenate([a_ref[...].astype(F32), b_ref[...].astype(F32)], axis=0)
        g, u, da = cat(g_ref, gn_ref), cat(u_ref, un_ref), cat(da_ref, dan_ref)
        g1, g2 = _conv_taps(g, gp_ref[...], i == 0)
        w = cw_ref[...]
        gc = cb_ref[...] + w[0:1] * g2 + w[1:2] * g1 + w[2:3] * g
        sg = jax.nn.sigmoid(gc)
        du_ref[...] = (da[:t] * (gc[:t] * sg[:t])).astype(du_ref.dtype)
        row = lax.broadcasted_iota(jnp.int32, (t + CONV_HALO, 1), 0)
        dgc = jnp.where((row < t) | (i < nt - 1), da * u * (sg * (1.0 + gc * (1.0 - sg))), 0.0)
        up1 = pltpu.roll(dgc, t + CONV_HALO - 1, 0)[:t]
        up2 = pltpu.roll(dgc, t + CONV_HALO - 2, 0)[:t]
        dgc = dgc[:t]
        dg_ref[...] = (w[2:3] * dgc + w[1:2] * up1 + w[0:1] * up2).astype(dg_ref.dtype)

        @pl.when(i == 0)
        def _():
            dcw_ref[...] = jnp.zeros_like(dcw_ref)
            dcb_ref[...] = jnp.zeros_like(dcb_ref)

        ones = jnp.ones((8, t), BF16)
        col_sum = lambda a: _dg(ones, a, _NN)[0:1]
        dcw_ref[0:1, :] += col_sum(dgc * g2[:t])
        dcw_ref[1:2, :] += col_sum(dgc * g1[:t])
        dcw_ref[2:3, :] += col_sum(dgc * g[:t])
        dcb_ref[...] += col_sum(dgc)

    cur, prev, nxt, cws, cbs = _conv_specs(s, f, t)
    return pl.pallas_call(
        kern, grid=(4, nt), in_specs=[cur, prev, nxt, cur, nxt, cur, nxt, cws, cbs], out_specs=[cur, cur, cws, cbs],
        out_shape=[jax.ShapeDtypeStruct(gg.shape, BF16), jax.ShapeDtypeStruct(gg.shape, BF16),
                   jax.ShapeDtypeStruct(cw.shape, F32), jax.ShapeDtypeStruct(cb.shape, F32)],
        compiler_params=pltpu.CompilerParams(dimension_semantics=("parallel", "arbitrary"), vmem_limit_bytes=VMEM_LIMIT),
        name="ffn_conv_bwd")(gg, gg, gg, uu, uu, dact, dact, cw, cb)


def _rope_tables(pos):
    half = MLA_ROPE // 2
    lane = jnp.arange(LANES)
    rotary = (lane >= MLA_NOPE) & (lane < MLA_QK)
    inv = jnp.where(rotary, ROPE_THETA ** (-((lane - MLA_NOPE) % half).astype(F32) / half), 0.0)
    ang = pos.astype(F32)[:, None] * inv
    cos, sin = jnp.cos(ang), jnp.sin(ang)
    first = rotary & (lane < MLA_NOPE + half)
    return cos, jnp.where(first, -sin, 0.0), jnp.where(rotary & ~first, sin, 0.0)


def _local_step(x, mem, pos, target, rep, early_shards, late_shards):
    g = {}
    c, sa, sb = _rope_tables(pos)

    xn, gathered = _norm_fwd_comm(x, rep["norm_mix"], _gather_plan(early_shards), "norm_mix_fwd_gather")
    w = _early_layout(dict(zip(EARLY, gathered, strict=True)), rep)

    def proj_fn(r, rows, k):
        la_ = _gate_fn(r[:, P_ALR:P_ALR + 128], k[0], k[1])
        return [r, la_, _rms(r[:, P_CQ:P_CQ + MLA_Q_RANK], k[2]), _rms(r[:, P_CKV:P_CKV + MLA_KV_RANK], k[3])], []

    proj, la, q_lat, kv_lat = _matmul(
        xn, w["in"], "nt", F32, "proj_fwd", epilogue=_Epilogue(
            proj_fn, [], [w["w2"], w["gate_b"], w["q_a_norm"], w["kv_a_norm"]],
            [(P_WIDTH, F32), (256, F32), (MLA_Q_RANK, BF16), (MLA_KV_RANK, BF16)], []))
    alr = _row(proj, 128, P_ALR // 128)
    kpe = _row(proj, 128, P_KPE // 128)
    og = _row(proj, 512, P_OG // 512)
    cq = _row(proj, 256, P_CQ // 256)
    ckv = _row(proj, 128, P_CKV // 128)

    o_gla, states = _gla_fwd(proj, la)

    def qk_body(r, k):
        q_up, k_up = _dg(r[0], k[0], _NN), _dg(r[1], k[1], _NN)
        qs, ks = [], []
        for qh, kh in zip(_heads(q_up, MLA_HEADS), _heads(k_up, MLA_HEADS)):
            a, b = _qk_head(qh, kh, r[2], r[3], r[4], r[5], k[3], k[4])
            qs.append(a)
            ks.append(b)
        return [_cat(qs), _cat(ks), _dg(r[1], k[2], _NN)], []

    tabs = [_row(c), _row(sa), _row(sb)]
    qk_consts = [w["uq"], w["k"], w["v"], w["q_norm"], w["k_norm"]]
    q_r, k_r, v_mla = _rows_call(qk_body, [_row(q_lat), _row(kv_lat), kpe] + tabs, qk_consts,
                                 [(1024, BF16), (1024, BF16), (512, BF16)], name="mla_qk_fwd")
    o_mla, lse, gathered = _attn_fwd(q_r, k_r, v_mla, _gather_plan(late_shards))
    w.update(_late_layout(dict(zip(LATE, gathered, strict=True))))

    def mix_body(r, k):
        ys = [_mix_head(o, g_, k[0]) for o, g_ in zip(_heads(r[0], GLA_HEADS), _heads(r[1], GLA_HEADS))]
        return [_cat(ys + [r[2]])], []

    cat = _rows_call(mix_body, [_row(o_gla), og, _row(o_mla)], [w["gla_out_norm"]], [(1024, BF16)],
                     name="mix_fwd")[0]
    h1, hn = _matmul(cat, w["out"], "nn", F32, "out_fwd_norm", residual=x, epilogue=_norm_fwd_epilogue(w["norm_xa"]))
    mn = _norm_fwd(mem, w["norm_mem"], "norm_mem_fwd")
    xkv = _matmul(mn, w["xkv"], "nn", F32, "xa_kv_fwd")

    def xa_fn(r, rows, k):
        ks, vs = _heads(k[0], 2 * XA_HEADS)[:XA_HEADS], _heads(k[0], 2 * XA_HEADS)[XA_HEADS:]
        return [r, _cat([_xa_head(a, b, v_, k[1], k[2]) for a, b, v_ in zip(_heads(r, XA_HEADS), ks, vs)])], []

    xq, xo = _matmul(hn, w["xq"], "nn", F32, "xa_q_fwd_attn", epilogue=_Epilogue(
        xa_fn, [], [xkv, w["xa_q_norm"], w["xa_k_norm"]], [(512, F32), (512, BF16)], []))
    h2, fn = _matmul(xo, w["xo"], "nn", F32, "xa_o_fwd_norm", residual=h1, epilogue=_norm_fwd_epilogue(w["norm_ffn"]))
    gg = _matmul(fn, w["wg"], "nt", BF16, "ffn_gate_fwd", b_lead="p")
    uu = _matmul(fn, w["wu"], "nt", BF16, "ffn_up_fwd", b_lead="p")
    act = _conv_fwd(gg, uu, w["cw"], w["cb"])
    def loss_fn(y, rows, consts):
        err = y - rows[0]
        part = 0.5 * jnp.sum(jnp.sum(err * err, axis=1, keepdims=True) * (1.0 / D_MODEL), axis=0, keepdims=True)
        return [err * (1.0 / D_MODEL)], [jnp.broadcast_to(part, (1, LANES))]

    dy, loss = _matmul(act, w["wd"], "nn", F32, "ffn_down_fwd_loss", residual=h2, a_lead="k", b_lead="k",
                       epilogue=_Epilogue(loss_fn, [_row(target)], [], [(D_MODEL, F32)], [(1, LANES)]))

    g["ffn_w_down"] = _matmul(act, dy, "tn", BF16, "ffn_down_dw", a_lead="p")
    dact = _matmul(dy, w["wd"], "nt", BF16, "ffn_down_dx", b_lead="p")
    duu, dgg, g["ffn_conv_w"], g["ffn_conv_b"] = _conv_bwd(gg, uu, dact, w["cw"], w["cb"])
    g["ffn_w_gate"] = _matmul(dgg, fn, "tn", BF16, "ffn_gate_dw", a_lead="p")
    g["ffn_w_up"] = _matmul(duu, fn, "tn", BF16, "ffn_up_dw", a_lead="p")
    dh2, g["norm_ffn"] = _matmul(dgg, w["wg"], "nn", F32, "ffn_dx_norm_bwd", a_lead="k", b_lead="k", more=(duu, w["wu"]),
                                 epilogue=_norm_bwd_epilogue(h2, w["norm_ffn"], dy))

    g["xa_w_o"] = _matmul(xo, dh2, "tn", BF16, "xa_o_dw")
    def xa_bwd(dxo_, rows, k):
        kvh = _heads(k[0], 2 * XA_HEADS)
        dq_, dk_, dv_ = [], [], []
        dqn, dkn = 0.0, 0.0
        for h, (a, d_) in enumerate(zip(_heads(rows[0], XA_HEADS), _heads(dxo_, XA_HEADS))):
            _, vjp = jax.vjp(_xa_head, a, kvh[h], kvh[XA_HEADS + h], k[1], k[2])
            ga, gk, gv, gqn, gkn = vjp(d_)
            dq_.append(ga)
            dk_.append(gk)
            dv_.append(gv)
            dqn, dkn = dqn + gqn, dkn + gkn
        return [_cat(dq_)], [_cat(dk_ + dv_), dqn, dkn]

    dxq, dxkv, g["xa_q_norm"], g["xa_k_norm"] = _matmul(dh2, w["xo"], "nt", F32, "xa_o_dx_attn_bwd", epilogue=_Epilogue(
        xa_bwd, [_row(xq)], [xkv, w["xa_q_norm"], w["xa_k_norm"]], [(512, BF16)], [xkv.shape, (1, 128), (1, 128)]))
    g["xa_w_q"] = _matmul(hn, dxq, "tn", BF16, "xa_q_dw")
    dh1, g["norm_xa"] = _matmul(dxq, w["xq"], "nt", F32, "xa_q_dx_norm_bwd",
                                epilogue=_norm_bwd_epilogue(h1, w["norm_xa"], dh2))
    g["xa_w_kv"] = _matmul(mn, dxkv, "tn", BF16, "xa_kv_dw")
    dmn = _matmul(dxkv, w["xkv"], "nt", F32, "xa_kv_dx")
    _, g["norm_mem"] = _norm_bwd(mem, w["norm_mem"], dmn, dmn, "norm_mem_bwd")

    g["w_out"] = _matmul(cat, dh1, "tn", BF16, "out_dw")
    def mix_bwd(dcat_, rows, k):
        do_, dog_ = [], []
        dgn = 0.0
        for o, g_, d_ in zip(_heads(rows[0], GLA_HEADS), _heads(rows[1], GLA_HEADS), _heads(dcat_, GLA_HEADS)):
            _, vjp = jax.vjp(_mix_head, o, g_, k[0])
            a, b, gn_ = vjp(d_)
            do_.append(a)
            dog_.append(b)
            dgn = dgn + gn_
        return [_cat(do_), _cat(dog_), dcat_[:, 512:]], [dgn]

    do_gla, d_og, do_mla, g["gla_out_norm"] = _matmul(dh1, w["out"], "nt", F32, "out_dx_mix_bwd", epilogue=_Epilogue(
        mix_bwd, [_row(o_gla), og], [w["gla_out_norm"]], [(512, F32), (512, BF16), (512, F32)], [(1, 128)]))

    late_parts = _late_grad_shards(g)
    dq_r, dk_r, dv_mla, lands_late = _attn_bwd(q_r, k_r, v_mla, o_mla, lse, do_mla,
                                               _scatter_plan([late_parts[n] for n in LATE]))
    lands_late = dict(zip(LATE, lands_late, strict=True))

    def qk_bwd(r, k):
        q_up, k_up = _dg(r[0], k[0], _NN), _dg(r[1], k[1], _NN)
        dqs, dks = [], []
        dkpe, dqn, dkn = 0.0, 0.0, 0.0
        for qh, kh, dqh, dkh in zip(_heads(q_up, MLA_HEADS), _heads(k_up, MLA_HEADS), _heads(r[6], MLA_HEADS),
                                    _heads(r[7], MLA_HEADS)):
            _, vjp = jax.vjp(lambda a, b, e, f, h_: _qk_head(a, b, e, r[3], r[4], r[5], f, h_), qh, kh, r[2], k[3], k[4])
            ga, gb, ge, gf, gh = vjp((dqh, dkh))
            dqs.append(ga)
            dks.append(gb)
            dkpe, dqn, dkn = dkpe + ge, dqn + gf, dkn + gh
        dq_up, dk_up, dv = _cat(dqs), _cat(dks), r[8]
        dq_lat_ = _dg(dq_up, k[0], _NT)
        dkv_lat_ = _dg(dk_up, k[1], _NT) + _dg(dv, k[2], _NT)
        return [dq_lat_, dkv_lat_, dkpe], [dqn, dkn, _dg(r[0], dq_up, _TN), _dg(r[1], dk_up, _TN), _dg(r[1], dv, _TN)]

    dq_lat, dkv_lat, d_kpe, g["q_norm"], g["k_norm"], g["uq"], g["k"], g["v"] = _rows_call(
        qk_bwd, [_row(q_lat), _row(kv_lat), kpe] + tabs + [_row(dq_r), _row(dk_r), _row(dv_mla)], qk_consts,
        [(MLA_Q_RANK, F32), (MLA_KV_RANK, F32), (128, BF16)],
        [(1, 128), (1, 128), w["uq"].shape, w["k"].shape, w["v"].shape], name="mla_qk_bwd")

    dgq, dgk, dla, dgv = _gla_bwd(proj, la, states, do_gla)

    def dproj_body(r, k):
        alr_, cq_, ckv_, dla_, dq_lat_, dkv_lat_, dgq_, dgk_, dgv_, d_og_, d_kpe_ = r
        _, gate_vjp = jax.vjp(_gate_fn, alr_, k[0], k[1])
        d_alr, gw2, gb = gate_vjp(dla_)
        _, q_vjp = jax.vjp(_rms, cq_, k[2])
        _, kv_vjp = jax.vjp(_rms, ckv_, k[3])
        d_cq, gqa = q_vjp(dq_lat_)
        d_ckv, gkva = kv_vjp(dkv_lat_)
        pieces = [dgq_, dgk_, dgv_, d_og_, d_cq, d_ckv, d_kpe_, d_alr]
        return [_cat([x_.astype(BF16) for x_ in pieces])], [gw2, gb, gqa, gkva]

    dproj, g["w2"], g["gla_gate_b"], g["mla_q_a_norm"], g["mla_kv_a_norm"] = _rows_call(
        dproj_body, [alr, cq, ckv, _row(dla), _row(dq_lat), _row(dkv_lat), _row(dgq), _row(dgk), _row(dgv), _row(d_og),
                     _row(d_kpe)], [w["w2"], w["gate_b"], w["q_a_norm"], w["kv_a_norm"]], [(P_WIDTH, BF16)],
        [(128, 256), (1, 256), (1, 256), (1, 128)], name="proj_cotangent")
    g["in"] = _matmul(dproj, xn, "tn", BF16, "proj_dw")
    dx, g["norm_mix"] = _matmul(dproj, w["in"], "nn", F32, "proj_dx_norm_bwd",
                                epilogue=_norm_bwd_epilogue(x, w["norm_mix"], dh1))
    return loss[0, 0], dx, g, lands_late


def _join_shards(pieces, axis):
    if axis == 0:
        return pieces.reshape(-1, pieces.shape[2])
    return jnp.transpose(pieces, (1, 0, 2)).reshape(pieces.shape[1], -1)


def _split_shards(full, axis):
    r, c = full.shape
    if axis == 0:
        return full.reshape(4, r // 4, c)
    return jnp.transpose(full.reshape(r, 4, c // 4), (1, 0, 2))


def _early_layout(gath, rep):
    w_in = gath["w_in"].reshape(N_WIDTH, D_MODEL)
    z = lambda n: jnp.zeros((n, D_MODEL), w_in.dtype)
    seg = lambda lo, n: w_in[lo:lo + n]
    ukv = _join_shards(gath["mla_w_ukv"], 1).reshape(MLA_KV_RANK, MLA_HEADS, MLA_NOPE + MLA_V)
    w = {
        "in": jnp.concatenate([seg(N_GQ, 256), seg(N_GK, 256), seg(N_GV, 512), seg(N_OG, 512), seg(N_CQ, 256),
                               seg(N_CKV, 128), z(64), seg(N_KPE, 32), z(32), seg(N_ALR, 16), z(112)], axis=0),
        "uq": jnp.pad(_join_shards(gath["mla_w_uq"], 1).reshape(MLA_Q_RANK, MLA_HEADS, MLA_QK),
                      ((0, 0), (0, 0), (0, LANES - MLA_QK))).reshape(MLA_Q_RANK, MLA_HEADS * LANES),
        "k": jnp.pad(ukv[:, :, :MLA_NOPE], ((0, 0), (0, 0), (0, LANES - MLA_NOPE))).reshape(MLA_KV_RANK, -1),
        "v": ukv[:, :, MLA_NOPE:].reshape(MLA_KV_RANK, MLA_HEADS * MLA_V),
        "w2": jnp.pad(_join_shards(gath["gla_gate_w2"], 1), ((0, LANES - GLA_RANK), (0, 0))),
        "cb": rep["ffn_conv_b"].reshape(4, 1, D_FF // 4),
        "q_norm": jnp.pad(rep["mla_q_norm"], ((0, 0), (0, LANES - MLA_QK))),
        "k_norm": jnp.pad(rep["mla_k_norm"], ((0, 0), (0, LANES - MLA_QK))),
        "q_a_norm": rep["mla_q_a_norm"], "kv_a_norm": rep["mla_kv_a_norm"], "gate_b": rep["gla_gate_b"],
    }
    for n in ("norm_mix", "gla_out_norm", "norm_xa", "norm_mem", "xa_q_norm", "xa_k_norm", "norm_ffn"):
        w[n] = rep[n]
    return w


def _late_layout(gath):
    return {"out": _join_shards(gath["w_out"], 0), "xq": _join_shards(gath["xa_w_q"], 0),
            "xkv": _join_shards(gath["xa_w_kv"], 0), "xo": _join_shards(gath["xa_w_o"], 1),
            "wg": gath["ffn_w_gate"], "wu": gath["ffn_w_up"], "wd": gath["ffn_w_down"], "cw": gath["ffn_conv_w"]}


def _late_grad_shards(g):
    sh = {"w_out": _split_shards(g["w_out"], 0), "xa_w_q": _split_shards(g["xa_w_q"], 0),
          "xa_w_kv": _split_shards(g["xa_w_kv"], 0), "xa_w_o": _split_shards(g["xa_w_o"], 1),
          "ffn_w_gate": g["ffn_w_gate"], "ffn_w_up": g["ffn_w_up"], "ffn_conv_w": g["ffn_conv_w"],
          "ffn_w_down": g["ffn_w_down"]}
    return {n: v.astype(BF16) for n, v in sh.items()}


def _early_grad_shards(g):
    gi = g["in"]
    seg = lambda lo, n: gi[lo:lo + n]
    w_in = jnp.concatenate([seg(P_GQ, 256), seg(P_GK, 256), seg(P_GV, 512), seg(P_ALR, 16), seg(P_OG, 512),
                            seg(P_CQ, 256), seg(P_CKV, 128), seg(P_KPE + 64, 32)], axis=0)
    uq = g["uq"].reshape(MLA_Q_RANK, MLA_HEADS, LANES)[:, :, :MLA_QK].reshape(MLA_Q_RANK, -1)
    ukv = jnp.concatenate([g["k"].reshape(MLA_KV_RANK, MLA_HEADS, LANES)[:, :, :MLA_NOPE],
                           g["v"].reshape(MLA_KV_RANK, MLA_HEADS, MLA_V)], axis=2).reshape(MLA_KV_RANK, -1)
    sh = {"w_in": w_in.reshape(4, N_WIDTH // 4, D_MODEL), "gla_gate_w2": _split_shards(g["w2"][:GLA_RANK], 1),
          "mla_w_uq": _split_shards(uq, 1), "mla_w_ukv": _split_shards(ukv, 1)}
    sh = {n: v.astype(BF16) for n, v in sh.items()}
    rep = {n: g[n] for n in REPLICATED if n in g}
    rep["mla_q_norm"] = g["q_norm"][:, :MLA_QK]
    rep["mla_k_norm"] = g["k_norm"][:, :MLA_QK]
    rep["ffn_conv_b"] = g["ffn_conv_b"].reshape(1, D_FF)
    return sh, rep


SMALL_SHAPE = (8, 1024)


def _pack_small(vectors):
    flat = jnp.concatenate(vectors, axis=1)
    return jnp.pad(flat, ((0, 0), (0, SMALL_SHAPE[0] * SMALL_SHAPE[1] - flat.shape[1]))).reshape(SMALL_SHAPE)


def _unpack_small(buf, widths):
    flat = buf.reshape(1, -1)
    out, off = [], 0
    for wd in widths:
        out.append(flat[:, off:off + wd])
        off += wd
    return out


ANY = pl.BlockSpec(memory_space=pl.ANY)


def _place():
    x, y, c = lax.axis_index("x"), lax.axis_index("y"), lax.axis_index("c")
    chips = [(1 - x, y), (x, 1 - y), (1 - x, 1 - y)]
    return x, y, c, chips


class _Comm:
    def __init__(self, ins, out_shape, sems, start, finish, mid=None):
        self.ins, self.out_shape, self.sems = list(ins), list(out_shape), list(sems)
        self.start, self.finish, self.mid = start, finish, mid or (lambda *args: None)


def _run_comm(plan, name):
    ni, no = len(plan.ins), len(plan.out_shape)

    def body(*refs):
        ins, outs, sems = refs[:ni], refs[ni:ni + no], refs[ni + no:]
        place = _place()
        plan.start(place, ins, outs, sems)
        plan.mid(place, ins, outs, sems)
        plan.finish(place, ins, outs, sems)

    return pl.pallas_call(body, in_specs=[ANY] * ni, out_specs=[ANY] * no, out_shape=plan.out_shape,
                          scratch_shapes=plan.sems, name=name)(*plan.ins)


def _gather_plan(shards):
    n = len(shards)
    by_rows = [s.shape[0] % (2 * BF16_ROWS) == 0 for s in shards]
    by_cols = [not r and s.shape[1] % (2 * LANES) == 0 for r, s in zip(by_rows, shards)]
    split = [r or c for r, c in zip(by_rows, by_cols)]

    def rows(ref, t, c):
        if by_rows[t]:
            half = shards[t].shape[0] // 2
            return ref.at[pl.ds(pl.multiple_of(c * half, BF16_ROWS), half)]
        if by_cols[t]:
            half = shards[t].shape[1] // 2
            return ref.at[:, pl.ds(pl.multiple_of(c * half, LANES), half)]
        return ref

    def remote(src, dst, ss, rs, to):
        return pltpu.make_async_remote_copy(src_ref=src, dst_ref=dst, send_sem=ss, recv_sem=rs, device_id=to,
                                            device_id_type=MESH)

    def first_wave(place, ins, outs, sems):
        x, y, c, chips = place
        ici_s, ici_r, _, _, local = sems
        me = 2 * x + y
        own = [pltpu.make_async_copy(ins[t], outs[t].at[me], local.at[t]) for t in range(n)]
        push = [remote(rows(ins[t], t, c), rows(outs[t].at[me], t, c), ici_s.at[3 * t + j], ici_r.at[3 * t + j], (px, py, c))
                for t in range(n) for j, (px, py) in enumerate(chips)]
        return own, push

    def second_wave(place, ins, outs, sems, last):
        x, y, c, chips = place
        ici_s, ici_r, d2d_s, d2d_r, local = sems
        sib = (x, y, 1 - c)
        out = []
        for t in range(n):
            for j, (px, py) in enumerate(chips):
                block = outs[t].at[2 * px + py]
                got = rows(block, t, c)
                if split[t]:
                    hand = remote(got, got, d2d_s.at[3 * t + j], d2d_r.at[3 * t + j], sib)
                    theirs = rows(block, t, 1 - c)
                    other = (remote(theirs, theirs, local.at[0], d2d_r.at[3 * t + j], sib) if last else
                             remote(got, got, local.at[0], ici_r.at[3 * t + j], sib))
                    out.append((other, hand))
                elif last:
                    out.append((remote(got, got, local.at[0], ici_r.at[3 * t + j], sib), None))
        return out

    def start(place, ins, outs, sems):
        own, push = first_wave(place, ins, outs, sems)
        for cp in own + push:
            cp.start()

    def mid(place, ins, outs, sems):
        for arrival, hand in second_wave(place, ins, outs, sems, False):
            arrival.wait_recv()
            hand.start()

    def finish(place, ins, outs, sems):
        own, push = first_wave(place, ins, outs, sems)
        for arrival, hand in second_wave(place, ins, outs, sems, True):
            arrival.wait_recv()
            if hand is not None:
                hand.wait_send()
        for cp in push:
            cp.wait_send()
        for cp in own:
            cp.wait()

    dma = pltpu.SemaphoreType.DMA
    return _Comm(shards, [jax.ShapeDtypeStruct((4,) + s.shape, s.dtype) for s in shards],
                 [dma((3 * n,)), dma((3 * n,)), dma((3 * n,)), dma((3 * n,)), dma((n,))], start, finish, mid)


def _scatter_plan(parts, small=None):
    n = len(parts)
    ns = 0 if small is None else 1

    def unpack(place, ins, outs, sems):
        x, y, c, chips = place
        return x, y, c, chips, 2 * x + y, 4 * x + 2 * y + c, (x, y, 1 - c)

    def remote(src, dst, ss, rs, to):
        return pltpu.make_async_remote_copy(src_ref=src, dst_ref=dst, send_sem=ss, recv_sem=rs, device_id=to,
                                            device_id_type=MESH)

    def first_wave(place, ins, outs, sems):
        x, y, c, chips, me, dev, sib = unpack(place, ins, outs, sems)
        ici_s, ici_r, d2d_s, d2d_r, sm_s, sm_r, local = sems
        own, push = [], []
        if ns:
            own.append(pltpu.make_async_copy(ins[n], outs[n].at[dev], local.at[n]))
            for k in range(1, 8):
                px = (1 - x) if (k >> 2) & 1 else x
                py = (1 - y) if (k >> 1) & 1 else y
                pc = (1 - c) if k & 1 else c
                push.append(remote(ins[n], outs[n].at[dev], sm_s.at[k - 1], sm_r.at[k - 1], (px, py, pc)))
        for t in range(n):
            own.append(pltpu.make_async_copy(ins[t].at[me], outs[t].at[dev], local.at[t]))
            push.append(remote(ins[t].at[me], outs[t].at[dev], d2d_s.at[4 * t], d2d_r.at[4 * t], sib))
            for j, (px, py) in enumerate(chips):
                push.append(remote(ins[t].at[2 * px + py], outs[t].at[dev], ici_s.at[3 * t + j], ici_r.at[3 * t + j],
                                   (px, py, c)))
        return own, push

    def start(place, ins, outs, sems):
        own, push = first_wave(place, ins, outs, sems)
        for cp in own + push:
            cp.start()

    def landed(dst, rs, sems, sib):
        remote(dst, dst, sems[-1].at[0], rs, sib).wait_recv()

    def forwards(place, ins, outs, sems):
        x, y, c, chips, me, dev, sib = unpack(place, ins, outs, sems)
        d2d_s, d2d_r = sems[2], sems[3]
        slots = [(t, j, outs[t].at[4 * px + 2 * py + c]) for t in range(n) for j, (px, py) in enumerate(chips)]
        return [(t, j, slot, remote(slot, slot, d2d_s.at[4 * t + 1 + j], d2d_r.at[4 * t + 1 + j], sib))
                for t, j, slot in slots]

    def mid(place, ins, outs, sems):
        sib = unpack(place, ins, outs, sems)[-1]
        for t, j, slot, cp in forwards(place, ins, outs, sems):
            landed(slot, sems[1].at[3 * t + j], sems, sib)
            cp.start()

    def finish(place, ins, outs, sems):
        x, y, c, chips, me, dev, sib = unpack(place, ins, outs, sems)
        d2d_r, sm_r = sems[3], sems[5]
        own, push = first_wave(place, ins, outs, sems)
        push += [cp for _, _, _, cp in forwards(place, ins, outs, sems)]
        for t in range(n):
            landed(outs[t].at[4 * x + 2 * y + (1 - c)], d2d_r.at[4 * t], sems, sib)
            for j, (px, py) in enumerate(chips):
                landed(outs[t].at[4 * px + 2 * py + (1 - c)], d2d_r.at[4 * t + 1 + j], sems, sib)
        if ns:
            for k in range(1, 8):
                px = (1 - x) if (k >> 2) & 1 else x
                py = (1 - y) if (k >> 1) & 1 else y
                pc = (1 - c) if k & 1 else c
                landed(outs[n].at[4 * px + 2 * py + pc], sm_r.at[k - 1], sems, sib)
        for cp in push:
            cp.wait_send()
        for cp in own:
            cp.wait()

    dma = pltpu.SemaphoreType.DMA
    ins = list(parts) + ([small] if ns else [])
    out_shape = [jax.ShapeDtypeStruct((8,) + p.shape[1:], p.dtype) for p in parts]
    if ns:
        out_shape.append(jax.ShapeDtypeStruct((8,) + small.shape, small.dtype))
    return _Comm(ins, out_shape, [dma((3 * n,)), dma((3 * n,)), dma((4 * n,)), dma((4 * n,)), dma((7,)), dma((7,)),
                                  dma((n + 1,))], start, finish, mid)


ADAM_ROWS = 288


def _row_tile(r, cap):
    if r <= cap:
        return r
    return max((t for t in range(8, cap + 1, 8) if r % t == 0), default=r)


def _adamw_update(w, m, v, land):
    g = land[0].astype(F32)
    for i in range(1, 8):
        g = g + land[i].astype(F32)
    m_new = ADAM_B1 * m + (1.0 - ADAM_B1) * g
    v_new = ADAM_B2 * v + (1.0 - ADAM_B2) * (g * g)
    m_hat = m_new / (1.0 - ADAM_B1 ** ADAM_STEP)
    v_hat = v_new / (1.0 - ADAM_B2 ** ADAM_STEP)
    return g, -ADAM_LR * (m_hat / (jnp.sqrt(v_hat) + ADAM_EPS) + ADAM_WD * w), m_new, v_new


def _adamw(tensors, name, comm=None):
    k = len(tensors)
    r, c = tensors[0][0].shape
    t = _row_tile(r, ADAM_ROWS // k)
    tc = c if t < r or r <= ADAM_ROWS else 2 * LANES
    n = (r // t) * (c // tc)
    nci, nco, nsem = (len(comm.ins), len(comm.out_shape), len(comm.sems)) if comm else (0, 0, 0)

    def kern(*refs):
        ins, cins, outs, couts, csems = _split_refs(refs, (4 * k, nci, 4 * k, nco, nsem))
        if comm:
            place = _place()

            @pl.when(pl.program_id(0) == 0)
            def _():
                comm.start(place, cins, couts, csems)

        for i in range(k):
            w_ref, m_ref, v_ref, l_ref = ins[4 * i:4 * i + 4]
            res = _adamw_update(w_ref[...], m_ref[...], v_ref[...], l_ref)
            for ref, val in zip(outs[4 * i:4 * i + 4], res, strict=True):
                ref[...] = val
        if comm:
            @pl.when(pl.program_id(0) == n - 1)
            def _():
                comm.mid(place, cins, couts, csems)
                comm.finish(place, cins, couts, csems)

    where = (lambda i: (i, 0)) if tc == c else (lambda i: (0, i))
    spec = pl.BlockSpec((t, tc), where)
    lspec = pl.BlockSpec((8, t, tc), lambda i: (0,) + where(i))
    res = pl.pallas_call(
        kern, grid=(n,), in_specs=[spec, spec, spec, lspec] * k + [ANY] * nci, out_specs=[spec] * (4 * k) + [ANY] * nco,
        out_shape=[jax.ShapeDtypeStruct((r, c), F32)] * (4 * k) + (comm.out_shape if comm else []),
        scratch_shapes=comm.sems if comm else [],
        compiler_params=pltpu.CompilerParams(dimension_semantics=("arbitrary" if comm else "parallel",),
                                             vmem_limit_bytes=VMEM_LIMIT),
        name=name)(*[x for tens in tensors for x in tens], *(comm.ins if comm else []))
    return [res[4 * i:4 * i + 4] for i in range(k)], res[4 * k:]


def _step(a):
    def sq(n):
        v = a[n][0] if a[n].ndim == 3 else a[n]
        return v.T if n.removeprefix("m_").removeprefix("v_") in TRANSPOSED else v

    payload = lambda n: sq(n) if n in EXACT_GATHER else sq(n).astype(BF16)

    loss, dx, g, lands_late = _local_step(sq("x"), sq("mem"), a["positions"][0], sq("loss_target"),
                                          {n: a[n] for n in REPLICATED}, [payload(n) for n in EARLY],
                                          [payload(n) for n in LATE])

    sh, rep = _early_grad_shards(g)
    small = _pack_small([rep[n] for n in REPLICATED] + [loss.reshape(1, 1)])
    *lands_early, land_small = _run_comm(_scatter_plan([sh[n] for n in EARLY], small), "scatter_last")
    quad = lambda n, land: (sq(n), sq("m_" + n), sq("v_" + n), land)
    lands = dict(zip(EARLY, lands_early, strict=True)) | lands_late

    outs = {}
    kinds = ("grad_", "delta_", "new_m_", "new_v_")
    for n, _ in SHARDED:
        res = _adamw([quad(n, lands[n])], "adamw_" + n)[0][0]
        for kind, val in zip(kinds, res, strict=True):
            outs[kind + n] = (val.T if n in TRANSPOSED else val).reshape(a[n].shape)
    zero = jnp.zeros((1, 1), F32)
    packed = [_pack_small([a[p + n] for n in REPLICATED] + [zero]) for p in ("", "m_", "v_")]
    res = _adamw([(*packed, land_small)], "adamw_replicated")[0][0]
    widths = [a[n].shape[1] for n in REPLICATED] + [1]
    for kind, buf in zip(kinds, res, strict=True):
        *vals, total = _unpack_small(buf, widths)
        for n, val in zip(REPLICATED, vals, strict=True):
            outs[kind + n] = val
        if kind == "grad_":
            loss = total[0, 0]

    ordered = [outs[kind + n] for kind in kinds for n in WEIGHTS]
    return (loss, dx[None], *ordered)


def kernel(x, mem, positions, norm_mix, w_in, gla_gate_w2, gla_gate_b, gla_out_norm, mla_q_a_norm, mla_w_uq, mla_kv_a_norm, mla_w_ukv, mla_q_norm, mla_k_norm, w_out, norm_xa, norm_mem, xa_w_q, xa_w_kv, xa_q_norm, xa_k_norm, xa_w_o, norm_ffn, ffn_w_gate, ffn_w_up, ffn_conv_w, ffn_conv_b, ffn_w_down, loss_target, m_norm_mix, m_w_in, m_gla_gate_w2, m_gla_gate_b, m_gla_out_norm, m_mla_q_a_norm, m_mla_w_uq, m_mla_kv_a_norm, m_mla_w_ukv, m_mla_q_norm, m_mla_k_norm, m_w_out, m_norm_xa, m_norm_mem, m_xa_w_q, m_xa_w_kv, m_xa_q_norm, m_xa_k_norm, m_xa_w_o, m_norm_ffn, m_ffn_w_gate, m_ffn_w_up, m_ffn_conv_w, m_ffn_conv_b, m_ffn_w_down, v_norm_mix, v_w_in, v_gla_gate_w2, v_gla_gate_b, v_gla_out_norm, v_mla_q_a_norm, v_mla_w_uq, v_mla_kv_a_norm, v_mla_w_ukv, v_mla_q_norm, v_mla_k_norm, v_w_out, v_norm_xa, v_norm_mem, v_xa_w_q, v_xa_w_kv, v_xa_q_norm, v_xa_k_norm, v_xa_w_o, v_norm_ffn, v_ffn_w_gate, v_ffn_w_up, v_ffn_conv_w, v_ffn_conv_b, v_ffn_w_down):
    return _step(dict(locals()))
```

```python
import functools

import jax
import jax.numpy as jnp
import numpy as np
from jax import lax
from jax.experimental import pallas as pl
from jax.experimental.pallas import tpu as pltpu

F32, BF16 = jnp.float32, jnp.bfloat16
MESH = pl.DeviceIdType.MESH

D_MODEL = 1024
EPS = 1e-6
GLA_HEADS, GLA_DK, GLA_DV, GLA_RANK, GLA_CHUNK = 4, 64, 128, 16, 64
GLA_GATE_NORM = 16.0
MLA_HEADS, MLA_Q_RANK, MLA_KV_RANK, MLA_NOPE, MLA_ROPE, MLA_V = 8, 256, 128, 64, 32, 64
MLA_QK = MLA_NOPE + MLA_ROPE
ROPE_THETA = 10000.0
LOG2E, LN2 = 1.4426950408889634, 0.6931471805599453
XA_HEADS, XA_DIM = 4, 128
D_FF = 2816
ADAM_LR, ADAM_B1, ADAM_B2, ADAM_EPS, ADAM_WD, ADAM_STEP = 0.001, 0.9, 0.999, 1e-08, 0.01, 10

LANES = 128
BF16_ROWS = 16
VMEM_LIMIT = 56 * 1024 * 1024
MATMUL_VMEM = 44 * 1024 * 1024
ROW_TILE = 512

P_GQ, P_GK, P_GV, P_OG, P_CQ, P_CKV, P_KPE, P_ALR, P_WIDTH = 0, 256, 512, 1024, 1536, 1792, 1920, 2048, 2176
N_GQ, N_GK, N_GV, N_ALR, N_OG, N_CQ, N_CKV, N_KPE, N_WIDTH = 0, 256, 512, 1024, 1040, 1552, 1808, 1936, 1968

SHARDED = (("w_in", 1), ("gla_gate_w2", 1), ("mla_w_uq", 1), ("mla_w_ukv", 1), ("w_out", 0), ("xa_w_q", 0),
           ("xa_w_kv", 0), ("xa_w_o", 1), ("ffn_w_gate", 1), ("ffn_w_up", 1), ("ffn_conv_w", 1), ("ffn_w_down", 0))
REPLICATED = ("norm_mix", "gla_gate_b", "gla_out_norm", "mla_q_a_norm", "mla_kv_a_norm", "mla_q_norm", "mla_k_norm",
              "norm_xa", "norm_mem", "xa_q_norm", "xa_k_norm", "norm_ffn", "ffn_conv_b")
EXACT_GATHER = ("gla_gate_w2", "ffn_conv_w")
TRANSPOSED = ("w_in", "ffn_w_gate", "ffn_w_up")
EARLY = ("w_in", "gla_gate_w2", "mla_w_uq", "mla_w_ukv")
LATE = tuple(n for n, _ in SHARDED if n not in EARLY)
LAST = ("ffn_w_down",)
WEIGHTS = ("norm_mix", "w_in", "gla_gate_w2", "gla_gate_b", "gla_out_norm", "mla_q_a_norm", "mla_w_uq",
           "mla_kv_a_norm", "mla_w_ukv", "mla_q_norm", "mla_k_norm", "w_out", "norm_xa", "norm_mem", "xa_w_q",
           "xa_w_kv", "xa_q_norm", "xa_k_norm", "xa_w_o", "norm_ffn", "ffn_w_gate", "ffn_w_up", "ffn_conv_w",
           "ffn_conv_b", "ffn_w_down")


_NN = ((1,), (0,))
_NT = ((1,), (1,))
_TN = ((0,), (0,))


def _dg(a, b, dims):
    return lax.dot_general(a.astype(BF16), b.astype(BF16), (dims, ((), ())), preferred_element_type=F32)


@jax.custom_vjp
def _dot_nn(a, b):
    return _dg(a, b, _NN)


_dot_nn.defvjp(lambda a, b: (_dg(a, b, _NN), (a, b)),
               lambda r, g: (_dg(g, r[1], _NT).astype(r[0].dtype), _dg(r[0], g, _TN).astype(r[1].dtype)))


@jax.custom_vjp
def _dot_nt(a, b):
    return _dg(a, b, _NT)


_dot_nt.defvjp(lambda a, b: (_dg(a, b, _NT), (a, b)),
               lambda r, g: (_dg(g, r[1], _NN).astype(r[0].dtype), _dg(g, r[0], _TN).astype(r[1].dtype)))


@jax.custom_vjp
def _dot_tn(a, b):
    return _dg(a, b, _TN)


_dot_tn.defvjp(lambda a, b: (_dg(a, b, _TN), (a, b)),
               lambda r, g: (_dg(r[1], g, _NT).astype(r[0].dtype), _dg(r[0], g, _NN).astype(r[1].dtype)))


def _rms(x, w, n=None):
    n = x.shape[-1] if n is None else n
    ms = jnp.sum(x * x, axis=-1, keepdims=True) * (1.0 / n)
    return x * lax.rsqrt(ms + EPS) * w


def _silu(x):
    return x * jax.nn.sigmoid(x)


def _log_sigmoid(x):
    return jnp.minimum(x, 0.0) - jnp.log(1.0 + jnp.exp(-jnp.abs(x)))


@jax.custom_vjp
def _cumsum_rows(x):
    n = x.shape[0]
    row = lax.broadcasted_iota(jnp.int32, x.shape, 0)
    k = 1
    while k < n:
        x = x + jnp.where(row >= k, pltpu.roll(x, k, 0), 0.0)
        k *= 2
    return x


def _cumsum_rows_bwd(_, g):
    n = g.shape[0]
    row = lax.broadcasted_iota(jnp.int32, g.shape, 0)
    k = 1
    while k < n:
        g = g + jnp.where(row < n - k, pltpu.roll(g, n - k, 0), 0.0)
        k *= 2
    return (g,)


_cumsum_rows.defvjp(lambda x: (_cumsum_rows(x), None), _cumsum_rows_bwd)


def _lane_mask(lo, hi):
    lane = lax.broadcasted_iota(jnp.int32, (1, LANES), 1)
    return ((lane >= lo) & (lane < hi)).astype(F32)


def _tile(n, t):
    t = min(n, t)
    assert n % t == 0, (n, t)
    return t


class _Epilogue:
    def __init__(self, fn, rows=(), consts=(), outs=(), accs=()):
        self.fn, self.rows, self.consts, self.outs, self.accs = fn, list(rows), list(consts), list(outs), list(accs)


def _matmul(a, b, mode, out_dtype, name, residual=None, a_lead=None, b_lead=None, more=None, epilogue=None):
    (a0, a1), (b0, b1) = a.shape[-2:], b.shape[-2:]
    if mode == "nn":
        m, k, k2, n = a0, a1, b0, b1
    elif mode == "nt":
        m, k, n, k2 = a0, a1, b0, b1
    else:
        k, m, k2, n = a0, a1, b0, b1
    assert k == k2, (a.shape, b.shape, mode)
    npar = 4 if "p" in (a_lead, b_lead) else 1
    nsum = 4 if "k" in (a_lead, b_lead) else 1
    pairs = [(a, b)] + ([more] if more else [])
    a_item, b_item, o_item = a.dtype.itemsize, b.dtype.itemsize, jnp.dtype(out_dtype).itemsize
    ep = epilogue
    row_extra = 4 if residual is not None else 0
    if ep:
        row_extra += (sum(r.dtype.itemsize * wd for r, wd, _ in ep.rows) + sum(jnp.dtype(d).itemsize * wd for wd, d in ep.outs)) / n

    def resident(lead, tiles):
        return lead != "p" and tiles == 1

    def vmem_need(tm, tn, tk):
        a_bufs = 1 if resident(a_lead, (m // tm) * (k // tk)) else 2
        b_bufs = 1 if resident(b_lead, (n // tn) * (k // tk)) else 2
        need = a_bufs * (nsum if a_lead == "k" else 1) * tm * tk * a_item + b_bufs * (nsum if b_lead == "k" else 1) * tk * tn * b_item
        need *= len(pairs)
        need += (0 if ep else 2 * tm * tn * o_item) + tm * tn * 4 * (2 if tk < k else 1)
        need += tm * tk * 2 * (a_item == 4 or mode == "tn") + tk * tn * 2 * (b_item == 4)
        return need + int(2 * tm * tn * row_extra) + (3 * tm * tn * 4 if ep else 0)

    halvings = (4096, 2048, 1024, 512, 256, 128, 64, 32, 16, 8)
    if mode == "tn":
        tm = m if m <= 2304 else m // 2
        tn = n if tm * n <= 1024 * 2304 else n // 2
        tk = next((r for r in halvings if k % r == 0 and vmem_need(tm, tn, r) <= MATMUL_VMEM), k)
    else:
        tn, tk = n, k
        tm = next((r for r in halvings if m % r == 0 and vmem_need(r, tn, tk) <= MATMUL_VMEM), m)
    assert m % tm == 0 and n % tn == 0 and k % tk == 0
    assert ep is None or (tn == n and tk == k and npar == 1)
    nk = k // tk
    dims = {"nn": _NN, "nt": _NT, "tn": _TN}[mode]
    n_in = 2 * len(pairs) + (residual is not None)
    n_ep_in = len(ep.rows) + len(ep.consts) if ep else 0
    n_out = len(ep.outs) + len(ep.accs) if ep else 1

    def body(*refs):
        ab, rs, ep_in, outs, scratch = _split_refs(refs, (2 * len(pairs), n_in - 2 * len(pairs), n_ep_in, n_out, nk > 1))
        prod = None
        for a_ref, b_ref in zip(ab[0::2], ab[1::2]):
            for sh in range(nsum):
                term = _dg(a_ref[sh] if a_lead == "k" else a_ref[...], b_ref[sh] if b_lead == "k" else b_ref[...], dims)
                prod = term if prod is None else prod + term

        def finish(r):
            if rs:
                r = r + rs[0][...]
            if ep is None:
                outs[0][...] = r.astype(outs[0].dtype)
                return
            vals = [x[...] for x in ep_in]
            ro, ao = ep.fn(r, vals[:len(ep.rows)], vals[len(ep.rows):])
            for ref, val in zip(outs[:len(ep.outs)], ro, strict=True):
                ref[...] = val.astype(ref.dtype)
            if ep.accs:
                @pl.when(pl.program_id(0) == 0)
                def _():
                    for ref in outs[len(ep.outs):]:
                        ref[...] = jnp.zeros_like(ref)

                for ref, val in zip(outs[len(ep.outs):], ao, strict=True):
                    ref[...] += val

        if nk == 1:
            finish(prod)
            return
        acc = scratch[0]
        kk = pl.program_id(3)

        @pl.when(kk == 0)
        def _():
            acc[...] = prod

        @pl.when(kk > 0)
        def _():
            acc[...] += prod

        @pl.when(kk == nk - 1)
        def _():
            finish(acc[...])

    def spec(lead, blk, idx, tiles=0):
        mode = {"pipeline_mode": pl.Buffered(1)} if resident(lead, tiles) else {}
        if lead is None:
            return pl.BlockSpec(blk, lambda i, j, p, kk: idx(i, j, kk), **mode)
        if lead == "p":
            return pl.BlockSpec((None,) + blk, lambda i, j, p, kk: (p,) + idx(i, j, kk))
        return pl.BlockSpec((nsum,) + blk, lambda i, j, p, kk: (0,) + idx(i, j, kk), **mode)

    a_tiles, b_tiles = (m // tm) * nk, (n // tn) * nk
    if mode == "nn":
        pair_specs = [spec(a_lead, (tm, tk), lambda i, j, kk: (i, kk), a_tiles),
                      spec(b_lead, (tk, tn), lambda i, j, kk: (kk, j), b_tiles)]
    elif mode == "nt":
        pair_specs = [spec(a_lead, (tm, tk), lambda i, j, kk: (i, kk), a_tiles),
                      spec(b_lead, (tn, tk), lambda i, j, kk: (j, kk), b_tiles)]
    else:
        pair_specs = [spec(a_lead, (tk, tm), lambda i, j, kk: (kk, i), a_tiles),
                      spec(b_lead, (tk, tn), lambda i, j, kk: (kk, j), b_tiles)]
    tile = spec(None, (tm, tn), lambda i, j, kk: (i, j))
    in_specs = pair_specs * len(pairs)
    args = [x for pair in pairs for x in pair]
    if residual is not None:
        assert npar == 1
        in_specs.append(tile)
        args.append(residual)
    if ep:
        in_specs += [pl.BlockSpec((tm, wd), functools.partial(lambda cb, i, j, p, kk: (i, cb), cb)) for _, wd, cb in ep.rows]
        in_specs += [pl.BlockSpec(c.shape, lambda i, j, p, kk: (0, 0)) for c in ep.consts]
        args += [r for r, _, _ in ep.rows] + ep.consts
        out_specs = [pl.BlockSpec((tm, wd), lambda i, j, p, kk: (i, 0)) for wd, _ in ep.outs]
        out_specs += [pl.BlockSpec(shape, lambda i, j, p, kk: (0, 0)) for shape in ep.accs]
        out_shape = [jax.ShapeDtypeStruct((m, wd), d) for wd, d in ep.outs] + [jax.ShapeDtypeStruct(sh, F32) for sh in ep.accs]
    else:
        out_specs = spec("p" if npar > 1 else None, (tm, tn), lambda i, j, kk: (i, j))
        out_shape = jax.ShapeDtypeStruct(((4,) if npar > 1 else ()) + (m, n), out_dtype)
    outer = "arbitrary" if ep and ep.accs else "parallel"
    return pl.pallas_call(
        body, grid=(m // tm, n // tn, npar, nk), in_specs=in_specs, out_specs=out_specs, out_shape=out_shape,
        scratch_shapes=[pltpu.VMEM((tm, tn), F32)] if nk > 1 else [],
        compiler_params=pltpu.CompilerParams(dimension_semantics=(outer, outer, outer, "arbitrary"),
                                             vmem_limit_bytes=VMEM_LIMIT),
        name=name)(*args)


def _row(a, width=None, col_block=0):
    return (a, a.shape[1] if width is None else width, col_block)


def _rows_call(body, rows, consts, outs, accs=(), *, name, tile=ROW_TILE):
    s = rows[0][0].shape[0]
    t = _tile(s, tile)
    nr, nc, no = len(rows), len(consts), len(outs)

    def kern(*refs):
        r = [x[...] for x in refs[:nr]]
        c = [x[...] for x in refs[nr:nr + nc]]
        o_refs = refs[nr + nc:nr + nc + no]
        a_refs = refs[nr + nc + no:]
        ro, ao = body(r, c)
        for ref, val in zip(o_refs, ro, strict=True):
            ref[...] = val.astype(ref.dtype)
        if a_refs:
            @pl.when(pl.program_id(0) == 0)
            def _():
                for ref in a_refs:
                    ref[...] = jnp.zeros_like(ref)

            for ref, val in zip(a_refs, ao, strict=True):
                ref[...] += val

    in_specs = [pl.BlockSpec((t, w), functools.partial(lambda cb, i: (i, cb), cb)) for (_, w, cb) in rows]
    in_specs += [pl.BlockSpec(c.shape, lambda i: (0, 0)) for c in consts]
    out_specs = [pl.BlockSpec((t, w), lambda i: (i, 0)) for (w, _) in outs]
    out_specs += [pl.BlockSpec(shape, lambda i: (0, 0)) for shape in accs]
    out_shape = [jax.ShapeDtypeStruct((s, w), dt) for (w, dt) in outs]
    out_shape += [jax.ShapeDtypeStruct(shape, F32) for shape in accs]
    return pl.pallas_call(
        kern, grid=(s // t,), in_specs=in_specs, out_specs=out_specs, out_shape=out_shape,
        compiler_params=pltpu.CompilerParams(dimension_semantics=("arbitrary" if accs else "parallel",),
                                             vmem_limit_bytes=VMEM_LIMIT),
        name=name)(*[r[0] for r in rows], *consts)


def _gla_chunk(q, k, la, v0, v1, s0, s1):
    c = q.shape[0]
    r = lax.broadcasted_iota(jnp.int32, (c, c), 0)
    cc = lax.broadcasted_iota(jnp.int32, (c, c), 1)
    tril = cc <= r
    cum = _cumsum_rows(la)
    cl = jnp.sum(la, axis=0, keepdims=True)
    qd = q * (GLA_DK ** -0.5) * jnp.exp(cum)
    ki = k * jnp.exp(-cum)
    ke = k * jnp.exp(cl - cum)
    dec = jnp.exp(cl)
    outs, news = [], []
    for h, (v, s) in enumerate(((v0, s0), (v1, s1))):
        mk = _lane_mask(GLA_DK * h, GLA_DK * (h + 1))
        qh = qd * mk
        att = jnp.where(tril, _dot_nt(qh, ki), 0.0)
        outs.append(_dot_nn(att, v) + _dot_nt(qh, s))
        news.append(s * dec + _dot_tn(v, ke * mk))
    return outs[0], outs[1], news[0], news[1]


def _gla_specs(tb, rev_nb=None):
    blk = (lambda b: b) if rev_nb is None else (lambda b: rev_nb - 1 - b)
    q = pl.BlockSpec((tb, 128), lambda p, b: (blk(b), P_GQ // 128 + p))
    k = pl.BlockSpec((tb, 128), lambda p, b: (blk(b), P_GK // 128 + p))
    la = pl.BlockSpec((tb, 128), lambda p, b: (blk(b), p))
    v = pl.BlockSpec((tb, 256), lambda p, b: (blk(b), P_GV // 256 + p))
    o = pl.BlockSpec((tb, 256), lambda p, b: (blk(b), p))
    st = pl.BlockSpec((tb // GLA_CHUNK, 2, 128, 128), lambda p, b: (blk(b), p, 0, 0))
    return q, k, la, v, o, st


def _gla_fwd(proj, la):
    s = proj.shape[0]
    tb = _tile(s, ROW_TILE)
    nb, nch = s // tb, tb // GLA_CHUNK

    def kern(q_ref, k_ref, la_ref, v_ref, o_ref, st_ref, s_sc):
        @pl.when(pl.program_id(1) == 0)
        def _():
            s_sc[...] = jnp.zeros_like(s_sc)

        s0, s1 = s_sc[0], s_sc[1]
        for ci in range(nch):
            sl = slice(ci * GLA_CHUNK, (ci + 1) * GLA_CHUNK)
            st_ref[ci, 0] = s0
            st_ref[ci, 1] = s1
            o0, o1, s0, s1 = _gla_chunk(q_ref[sl, :], k_ref[sl, :], la_ref[sl, :], v_ref[sl, 0:128],
                                        v_ref[sl, 128:256], s0, s1)
            o_ref[sl, 0:128] = o0
            o_ref[sl, 128:256] = o1
        s_sc[0] = s0
        s_sc[1] = s1

    q, k, lasp, v, o, st = _gla_specs(tb)
    return pl.pallas_call(
        kern, grid=(2, nb), in_specs=[q, k, lasp, v], out_specs=[o, st],
        out_shape=[jax.ShapeDtypeStruct((s, 512), F32),
                   jax.ShapeDtypeStruct((s // GLA_CHUNK, GLA_HEADS, 128, 128), F32)],
        scratch_shapes=[pltpu.VMEM((2, 128, 128), F32)],
        compiler_params=pltpu.CompilerParams(dimension_semantics=("parallel", "arbitrary"),
                                             vmem_limit_bytes=VMEM_LIMIT),
        name="gla_fwd")(proj, proj, la, proj)


def _gla_bwd(proj, la, states, d_o):
    s = proj.shape[0]
    tb = _tile(s, ROW_TILE)
    nb, nch = s // tb, tb // GLA_CHUNK

    def kern(q_ref, k_ref, la_ref, v_ref, do_ref, st_ref, dq_ref, dk_ref, dla_ref, dv_ref, ds_sc):
        @pl.when(pl.program_id(1) == 0)
        def _():
            ds_sc[...] = jnp.zeros_like(ds_sc)

        d0, d1 = ds_sc[0], ds_sc[1]
        for ci in reversed(range(nch)):
            sl = slice(ci * GLA_CHUNK, (ci + 1) * GLA_CHUNK)
            _, vjp = jax.vjp(_gla_chunk, q_ref[sl, :], k_ref[sl, :], la_ref[sl, :], v_ref[sl, 0:128],
                             v_ref[sl, 128:256], st_ref[ci, 0], st_ref[ci, 1])
            gq, gk, gla, gv0, gv1, d0, d1 = vjp((do_ref[sl, 0:128], do_ref[sl, 128:256], d0, d1))
            dq_ref[sl, :] = gq
            dk_ref[sl, :] = gk
            dla_ref[sl, :] = gla
            dv_ref[sl, 0:128] = gv0
            dv_ref[sl, 128:256] = gv1
        ds_sc[0] = d0
        ds_sc[1] = d1

    q, k, lasp, v, o, st = _gla_specs(tb, rev_nb=nb)
    return pl.pallas_call(
        kern, grid=(2, nb), in_specs=[q, k, lasp, v, o, st], out_specs=[lasp, lasp, lasp, o],
        out_shape=[jax.ShapeDtypeStruct((s, 256), F32), jax.ShapeDtypeStruct((s, 256), F32),
                   jax.ShapeDtypeStruct((s, 256), F32), jax.ShapeDtypeStruct((s, 512), F32)],
        scratch_shapes=[pltpu.VMEM((2, 128, 128), F32)],
        compiler_params=pltpu.CompilerParams(dimension_semantics=("parallel", "arbitrary"),
                                             vmem_limit_bytes=VMEM_LIMIT),
        name="gla_bwd")(proj, proj, la, proj, d_o, states)


def _causal_keep(t):
    return lax.broadcasted_iota(jnp.int32, (t, t), 1) <= lax.broadcasted_iota(jnp.int32, (t, t), 0)


def _split_refs(refs, counts):
    out, off = [], 0
    for cnt in counts:
        out.append(refs[off:off + cnt])
        off += cnt
    return out


def _causal_blocks(n, key_major):
    pairs = ([(ki, qi) for ki in range(n) for qi in range(ki, n)] if key_major else
             [(ki, qi) for qi in range(n) for ki in range(qi + 1)])
    return np.array([ki for ki, _ in pairs], np.int32), np.array([qi for _, qi in pairs], np.int32)


def _attn_fwd(q, k, v, comm, tile=1024):
    s = q.shape[0]
    t = _tile(s, tile)
    n = s // t
    nci, nco = len(comm.ins), len(comm.out_shape)

    ki_tab, qi_tab = _causal_blocks(n, key_major=False)
    steps = len(ki_tab)

    def kern(ki_ref, qi_ref, *refs):
        (q_ref, k_ref, v_ref), cins, (o_ref, lse_ref), couts, (m_sc, l_sc, acc_sc), csems = _split_refs(
            refs, (3, nci, 2, nco, 3, len(comm.sems)))
        pair, step = pl.program_id(0), pl.program_id(1)
        qi, ki = qi_ref[step], ki_ref[step]
        place = _place()

        @pl.when((pair == 0) & (step == 0))
        def _():
            comm.start(place, cins, couts, csems)

        @pl.when((pair == MLA_HEADS // 2 - 1) & (step == 0))
        def _():
            comm.mid(place, cins, couts, csems)

        first = lax.broadcasted_iota(jnp.int32, (t, LANES), 1) < MLA_V

        @pl.when(ki == 0)
        def _():
            m_sc[...] = jnp.full_like(m_sc, -jnp.inf)
            l_sc[...] = jnp.zeros_like(l_sc)
            acc_sc[...] = jnp.zeros_like(acc_sc)

        def update(rows, cols, masked):
            nr = rows.stop - rows.start
            sel = first[:nr]
            alphas, pvs = [], []
            for h in range(2):
                sc = _dg(q_ref[rows, 128 * h:128 * (h + 1)], k_ref[cols, 128 * h:128 * (h + 1)], _NT)
                if masked:
                    sc = jnp.where(_causal_keep(nr), sc, -jnp.inf)
                m_prev = m_sc[h, rows]
                m_new = jnp.maximum(m_prev, jnp.max(sc, axis=1, keepdims=True))
                alpha = jnp.exp2(m_prev - m_new)
                p = jnp.exp2(sc - m_new[:, 0:1])
                l_sc[h, rows] = alpha * l_sc[h, rows] + jnp.sum(p, axis=1, keepdims=True)
                m_sc[h, rows] = m_new
                alphas.append(alpha)
                pvs.append(_dg(p, v_ref[cols, :], _NN))
            acc_sc[rows] = acc_sc[rows] * jnp.where(sel, alphas[0], alphas[1]) + jnp.where(sel, pvs[0], pvs[1])

        halves = [slice(0, t)] if t % 256 else [slice(0, t // 2), slice(t // 2, t)]

        @pl.when(ki < qi)
        def _():
            for rows in halves:
                for cols in halves:
                    update(rows, cols, False)

        @pl.when(ki == qi)
        def _():
            for i, rows in enumerate(halves):
                for j, cols in enumerate(halves[:i + 1]):
                    update(rows, cols, i == j)

        @pl.when(ki == qi)
        def _():
            l = jnp.where(first, l_sc[0], l_sc[1])
            m = jnp.where(first, m_sc[0], m_sc[1])
            o_ref[...] = acc_sc[...] / l
            lse_ref[...] = m + jnp.log2(l)

        @pl.when((pair == MLA_HEADS // 2 - 1) & (step == steps - 1))
        def _():
            comm.finish(place, cins, couts, csems)

    q_idx = lambda p, st, ki_r, qi_r: (qi_r[st], p)
    k_idx = lambda p, st, ki_r, qi_r: (ki_r[st], p)
    res = pl.pallas_call(
        kern, grid_spec=pltpu.PrefetchScalarGridSpec(
            num_scalar_prefetch=2, grid=(MLA_HEADS // 2, steps),
            in_specs=[pl.BlockSpec((t, 256), q_idx), pl.BlockSpec((t, 256), k_idx), pl.BlockSpec((t, 128), k_idx)]
            + [ANY] * nci,
            out_specs=[pl.BlockSpec((t, 128), q_idx), pl.BlockSpec((t, 128), q_idx)] + [ANY] * nco,
            scratch_shapes=[pltpu.VMEM((2, t, LANES), F32), pltpu.VMEM((2, t, LANES), F32),
                            pltpu.VMEM((t, LANES), F32)] + comm.sems),
        out_shape=[jax.ShapeDtypeStruct((s, 512), F32), jax.ShapeDtypeStruct((s, 512), F32)] + comm.out_shape,
        compiler_params=pltpu.CompilerParams(dimension_semantics=("arbitrary", "arbitrary"),
                                             vmem_limit_bytes=VMEM_LIMIT),
        name="mla_attn_fwd")(ki_tab, qi_tab, q, k, v, *comm.ins)
    return res[0], res[1], res[2:]


def _attn_bwd(q, k, v, o, lse, d_o, comm, tile=ROW_TILE):
    s = q.shape[0]
    t = _tile(s, tile)
    n = s // t
    nci, nco = len(comm.ins), len(comm.out_shape)

    ki_tab, qi_tab = _causal_blocks(n, key_major=True)
    steps = len(ki_tab)

    def kern(ki_ref, qi_ref, *refs):
        (q_ref, k_ref, v_ref, o_ref, lse_ref, do_ref), cins, (dq_ref, dk_ref, dv_ref), couts, (dk_sc, dv_sc), csems = \
            _split_refs(refs, (6, nci, 3, nco, 2, len(comm.sems)))
        pair, step = pl.program_id(0), pl.program_id(1)
        ki, qi = ki_ref[step], qi_ref[step]
        place = _place()

        @pl.when((pair == 0) & (step == 0))
        def _():
            comm.start(place, cins, couts, csems)

        @pl.when((pair == MLA_HEADS // 2 - 1) & (step == 0))
        def _():
            comm.mid(place, cins, couts, csems)

        @pl.when((ki == 0) & (qi == 0))
        def _():
            dq_ref[...] = jnp.zeros_like(dq_ref)

        @pl.when(qi == ki)
        def _():
            dk_sc[...] = jnp.zeros_like(dk_sc)
            dv_sc[...] = jnp.zeros_like(dv_sc)

        def update(diagonal):
            keep = _causal_keep(t)
            d_o = do_ref[...]
            prod = d_o * o_ref[...]
            rows = pl.ds(pl.multiple_of(qi * t, t), t)
            for h in range(2):
                hs = slice(128 * h, 128 * (h + 1))
                mk = _lane_mask(MLA_V * h, MLA_V * (h + 1))
                qh, kh = q_ref[:, hs], k_ref[:, hs]
                sc = _dg(qh, kh, _NT)
                if diagonal:
                    sc = jnp.where(keep, sc, -jnp.inf)
                p = jnp.exp2(sc - lse_ref[:, MLA_V * h:MLA_V * h + 1])
                doh = d_o * mk
                dp = _dg(doh * LN2, v_ref[...], _NT)
                delta = jnp.sum(prod * mk, axis=1, keepdims=True) * LN2
                ds = p * (dp - delta)
                dv_sc[...] += _dg(p, doh, _TN)
                dk_sc[:, hs] += _dg(ds, qh, _TN)
                dq_ref[rows, hs] += _dg(ds, kh, _NN)

        @pl.when(qi > ki)
        def _():
            update(False)

        @pl.when(qi == ki)
        def _():
            update(True)

        @pl.when(qi == n - 1)
        def _():
            dk_ref[...] = dk_sc[...]
            dv_ref[...] = dv_sc[...].astype(dv_ref.dtype)

        @pl.when((pair == MLA_HEADS // 2 - 1) & (step == steps - 1))
        def _():
            comm.finish(place, cins, couts, csems)

    q_idx = lambda p, st, ki_r, qi_r: (qi_r[st], p)
    k_idx = lambda p, st, ki_r, qi_r: (ki_r[st], p)
    res = pl.pallas_call(
        kern, grid_spec=pltpu.PrefetchScalarGridSpec(
            num_scalar_prefetch=2, grid=(MLA_HEADS // 2, steps),
            in_specs=[pl.BlockSpec((t, 256), q_idx), pl.BlockSpec((t, 256), k_idx), pl.BlockSpec((t, 128), k_idx),
                      pl.BlockSpec((t, 128), q_idx), pl.BlockSpec((t, 128), q_idx), pl.BlockSpec((t, 128), q_idx)]
            + [ANY] * nci,
            out_specs=[pl.BlockSpec((s, 256), lambda p, st, ki_r, qi_r: (0, p)), pl.BlockSpec((t, 256), k_idx),
                       pl.BlockSpec((t, 128), k_idx)] + [ANY] * nco,
            scratch_shapes=[pltpu.VMEM((t, 256), F32), pltpu.VMEM((t, 128), F32)] + comm.sems),
        out_shape=[jax.ShapeDtypeStruct((s, 1024), F32), jax.ShapeDtypeStruct((s, 1024), F32),
                   jax.ShapeDtypeStruct((s, 512), BF16)] + comm.out_shape,
        compiler_params=pltpu.CompilerParams(dimension_semantics=("arbitrary", "arbitrary"),
                                             vmem_limit_bytes=VMEM_LIMIT),
        name="mla_attn_bwd")(ki_tab, qi_tab, q, k, v, o, lse, d_o, *comm.ins)
    return res[0], res[1], res[2], res[3:]


def _gate_fn(alr, w2, b):
    return _log_sigmoid(_dot_nn(alr, w2) + b) * (1.0 / GLA_GATE_NORM)


def _make_norm_rope(scale):
    def forward(x, w, c, sa, sb):
        r = lax.rsqrt(jnp.sum(x * x, axis=-1, keepdims=True) * (1.0 / MLA_QK) + EPS)
        y = x * r * w
        out = y * c + pltpu.roll(y, LANES - 16, 1) * sa + pltpu.roll(y, 16, 1) * sb
        return (out if scale == 1.0 else out * scale), r

    @jax.custom_vjp
    def norm_rope(x, w, c, sa, sb):
        return forward(x, w, c, sa, sb)[0]

    def fwd(x, w, c, sa, sb):
        out, r = forward(x, w, c, sa, sb)
        return out, (x, w, c, sa, sb, r)

    def bwd(res, g):
        x, w, c, sa, sb, r = res
        if scale != 1.0:
            g = g * scale
        gy = g * c + pltpu.roll(g * sa, 16, 1) + pltpu.roll(g * sb, LANES - 16, 1)
        xr = x * r
        t = gy * w
        m = jnp.sum(t * xr, axis=-1, keepdims=True) * (1.0 / MLA_QK)
        return r * (t - xr * m), jnp.sum(gy * xr, axis=0, keepdims=True), jnp.zeros_like(c), jnp.zeros_like(sa), jnp.zeros_like(sb)

    norm_rope.defvjp(fwd, bwd)
    return norm_rope


_q_norm_rope = _make_norm_rope(MLA_QK ** -0.5 * LOG2E)
_k_norm_rope = _make_norm_rope(1.0)


def _qk_head(qh, kh, kpe, c, sa, sb, qn, kn):
    kfull = kh + kpe * _lane_mask(MLA_NOPE, MLA_QK)
    return _q_norm_rope(qh, qn, c, sa, sb), _k_norm_rope(kfull, kn, c, sa, sb)


def _mix_head(o, og, gn):
    return _rms(o, gn) * _silu(og)


def _xa_head(xq, xk, xv, qn, kn):
    sc = _dot_nt(_rms(xq, qn), _rms(xk, kn)) * (XA_DIM ** -0.5)
    e = jnp.exp(sc - lax.stop_gradient(jnp.max(sc, axis=1, keepdims=True)))
    p = e / jnp.sum(e, axis=1, keepdims=True)
    return _dot_nn(p, xv)


def _heads(x, n):
    return [x[:, 128 * h:128 * (h + 1)] for h in range(n)]


def _cat(xs):
    return jnp.concatenate(xs, axis=1)


def _norm_fwd(x, w, name):
    return _rows_call(lambda r, c: ([_rms(r[0], c[0])], []), [_row(x)], [w], [(x.shape[1], BF16)], name=name)[0]


def _norm_fwd_epilogue(w):
    return _Epilogue(lambda h, rows, consts: ([h, _rms(h, consts[0])], []), [], [w], [(D_MODEL, F32), (D_MODEL, BF16)], [])


def _norm_bwd_epilogue(x, w, add):
    def fn(d_out, rows, consts):
        _, vjp = jax.vjp(_rms, rows[0], consts[0])
        dx, dw = vjp(d_out)
        return [dx + rows[1]], [dw]

    return _Epilogue(fn, [_row(x), _row(add)], [w], [(D_MODEL, F32)], [w.shape])


def _norm_fwd_comm(x, w, comm, name):
    s, d = x.shape
    t = _tile(s, ROW_TILE)
    n = s // t
    nci, nco = len(comm.ins), len(comm.out_shape)

    def kern(*refs):
        (x_ref, w_ref), cins, (o_ref,), couts, csems = _split_refs(refs, (2, nci, 1, nco, len(comm.sems)))
        place = _place()

        @pl.when(pl.program_id(0) == 0)
        def _():
            comm.start(place, cins, couts, csems)

        o_ref[...] = _rms(x_ref[...], w_ref[...]).astype(o_ref.dtype)

        @pl.when(pl.program_id(0) == n - 1)
        def _():
            comm.mid(place, cins, couts, csems)
            comm.finish(place, cins, couts, csems)

    tile = pl.BlockSpec((t, d), lambda i: (i, 0))
    res = pl.pallas_call(
        kern, grid=(n,), in_specs=[tile, pl.BlockSpec(w.shape, lambda i: (0, 0))] + [ANY] * nci,
        out_specs=[tile] + [ANY] * nco, out_shape=[jax.ShapeDtypeStruct((s, d), BF16)] + comm.out_shape,
        scratch_shapes=comm.sems,
        compiler_params=pltpu.CompilerParams(dimension_semantics=("arbitrary",), vmem_limit_bytes=VMEM_LIMIT),
        name=name)(x, w, *comm.ins)
    return res[0], res[1:]


def _norm_bwd(x, w, d_out, add, name):
    def body(r, c):
        _, vjp = jax.vjp(_rms, r[0], c[0])
        dx, dw = vjp(r[1])
        return [dx + r[2]], [dw]

    return _rows_call(body, [_row(x), _row(d_out), _row(add)], [w], [(x.shape[1], F32)], [w.shape], name=name)


CONV_HALO = BF16_ROWS


def _conv_specs(s, f, t):
    n8 = t // CONV_HALO
    cur = pl.BlockSpec((None, t, f), lambda j, i: (j, i, 0))
    prev = pl.BlockSpec((None, CONV_HALO, f), lambda j, i: (j, jnp.maximum(i * n8 - 1, 0), 0))
    nxt = pl.BlockSpec((None, CONV_HALO, f), lambda j, i: (j, jnp.minimum((i + 1) * n8, s // CONV_HALO - 1), 0))
    cw = pl.BlockSpec((None, 3, f), lambda j, i: (j, 0, 0))
    cb = pl.BlockSpec((None, 1, f), lambda j, i: (j, 0, 0))
    return cur, prev, nxt, cw, cb


def _conv_taps(g, prev, first):
    ext = jnp.concatenate([jnp.where(first, 0.0, prev.astype(F32)), g], axis=0)
    return pltpu.roll(ext, 1, 0)[CONV_HALO:], pltpu.roll(ext, 2, 0)[CONV_HALO:]


def _conv_fwd(gg, uu, cw, cb, comm):
    _, s, f = gg.shape
    t = _tile(s, ROW_TILE)
    nt = s // t
    nci, nco = len(comm.ins), len(comm.out_shape)

    def kern(*refs):
        (g_ref, gp_ref, u_ref, cw_ref, cb_ref), cins, (o_ref,), couts, csems = _split_refs(
            refs, (5, nci, 1, nco, len(comm.sems)))
        shard, i = pl.program_id(0), pl.program_id(1)
        place = _place()

        @pl.when((shard == 0) & (i == 0))
        def _():
            comm.start(place, cins, couts, csems)

        @pl.when((shard == 3) & (i == 0))
        def _():
            comm.mid(place, cins, couts, csems)

        g = g_ref[...].astype(F32)
        g1, g2 = _conv_taps(g, gp_ref[...], i == 0)
        w = cw_ref[...]
        gc = cb_ref[...] + w[0:1] * g2 + w[1:2] * g1 + w[2:3] * g
        o_ref[...] = (_silu(gc) * u_ref[...].astype(F32)).astype(o_ref.dtype)

        @pl.when((shard == 3) & (i == nt - 1))
        def _():
            comm.finish(place, cins, couts, csems)

    cur, prev, _, cws, cbs = _conv_specs(s, f, t)
    res = pl.pallas_call(
        kern, grid=(4, nt), in_specs=[cur, prev, cur, cws, cbs] + [ANY] * nci, out_specs=[cur] + [ANY] * nco,
        out_shape=[jax.ShapeDtypeStruct(gg.shape, BF16)] + comm.out_shape, scratch_shapes=comm.sems,
        compiler_params=pltpu.CompilerParams(dimension_semantics=("arbitrary", "arbitrary"), vmem_limit_bytes=VMEM_LIMIT),
        name="ffn_conv_fwd")(gg, gg, uu, cw, cb, *comm.ins)
    return res[0], res[1:]


def _conv_bwd(gg, uu, dact, cw, cb):
    _, s, f = gg.shape
    t = _tile(s, ROW_TILE)
    nt = s // t

    def kern(g_ref, gp_ref, gn_ref, u_ref, un_ref, da_ref, dan_ref, cw_ref, cb_ref, du_ref, dg_ref, dcw_ref, dcb_ref):
        i = pl.program_id(1)
        cat = lambda a_ref, b_ref: jnp.concatenate([a_ref[...].astype(F32), b_ref[...].astype(F32)], axis=0)
        g, u, da = cat(g_ref, gn_ref), cat(u_ref, un_ref), cat(da_ref, dan_ref)
        g1, g2 = _conv_taps(g, gp_ref[...], i == 0)
        w = cw_ref[...]
        gc = cb_ref[...] + w[0:1] * g2 + w[1:2] * g1 + w[2:3] * g
        sg = jax.nn.sigmoid(gc)
        du_ref[...] = (da[:t] * (gc[:t] * sg[:t])).astype(du_ref.dtype)
        row = lax.broadcasted_iota(jnp.int32, (t + CONV_HALO, 1), 0)
        dgc = jnp.where((row < t) | (i < nt - 1), da * u * (sg * (1.0 + gc * (1.0 - sg))), 0.0)
        up1 = pltpu.roll(dgc, t + CONV_HALO - 1, 0)[:t]
        up2 = pltpu.roll(dgc, t + CONV_HALO - 2, 0)[:t]
        dgc = dgc[:t]
        dg_ref[...] = (w[2:3] * dgc + w[1:2] * up1 + w[0:1] * up2).astype(dg_ref.dtype)

        @pl.when(i == 0)
        def _():
            dcw_ref[...] = jnp.zeros_like(dcw_ref)
            dcb_ref[...] = jnp.zeros_like(dcb_ref)

        ones = jnp.ones((8, t), BF16)
        col_sum = lambda a: _dg(ones, a, _NN)[0:1]
        dcw_ref[0:1, :] += col_sum(dgc * g2[:t])
        dcw_ref[1:2, :] += col_sum(dgc * g1[:t])
        dcw_ref[2:3, :] += col_sum(dgc * g[:t])
        dcb_ref[...] += col_sum(dgc)

    cur, prev, nxt, cws, cbs = _conv_specs(s, f, t)
    return pl.pallas_call(
        kern, grid=(4, nt), in_specs=[cur, prev, nxt, cur, nxt, cur, nxt, cws, cbs], out_specs=[cur, cur, cws, cbs],
        out_shape=[jax.ShapeDtypeStruct(gg.shape, BF16), jax.ShapeDtypeStruct(gg.shape, BF16),
                   jax.ShapeDtypeStruct(cw.shape, F32), jax.ShapeDtypeStruct(cb.shape, F32)],
        compiler_params=pltpu.CompilerParams(dimension_semantics=("parallel", "arbitrary"), vmem_limit_bytes=VMEM_LIMIT),
        name="ffn_conv_bwd")(gg, gg, gg, uu, uu, dact, dact, cw, cb)


def _rope_tables(pos):
    half = MLA_ROPE // 2
    lane = jnp.arange(LANES)
    rotary = (lane >= MLA_NOPE) & (lane < MLA_QK)
    inv = jnp.where(rotary, ROPE_THETA ** (-((lane - MLA_NOPE) % half).astype(F32) / half), 0.0)
    ang = pos.astype(F32)[:, None] * inv
    cos, sin = jnp.cos(ang), jnp.sin(ang)
    first = rotary & (lane < MLA_NOPE + half)
    return cos, jnp.where(first, -sin, 0.0), jnp.where(rotary & ~first, sin, 0.0)


def _local_step(x, mem, pos, target, rep, early_shards, late_shards):
    g = {}
    c, sa, sb = _rope_tables(pos)

    xn, gathered = _norm_fwd_comm(x, rep["norm_mix"], _gather_plan(early_shards), "norm_mix_fwd_gather")
    w = _early_layout(dict(zip(EARLY, gathered, strict=True)), rep)

    def proj_fn(r, rows, k):
        la_ = _gate_fn(r[:, P_ALR:P_ALR + 128], k[0], k[1])
        return [r, la_, _rms(r[:, P_CQ:P_CQ + MLA_Q_RANK], k[2]), _rms(r[:, P_CKV:P_CKV + MLA_KV_RANK], k[3])], []

    proj, la, q_lat, kv_lat = _matmul(
        xn, w["in"], "nt", F32, "proj_fwd", epilogue=_Epilogue(
            proj_fn, [], [w["w2"], w["gate_b"], w["q_a_norm"], w["kv_a_norm"]],
            [(P_WIDTH, F32), (256, F32), (MLA_Q_RANK, BF16), (MLA_KV_RANK, BF16)], []))
    alr = _row(proj, 128, P_ALR // 128)
    kpe = _row(proj, 128, P_KPE // 128)
    og = _row(proj, 512, P_OG // 512)
    cq = _row(proj, 256, P_CQ // 256)
    ckv = _row(proj, 128, P_CKV // 128)

    o_gla, states = _gla_fwd(proj, la)

    def qk_body(r, k):
        q_up, k_up = _dg(r[0], k[0], _NN), _dg(r[1], k[1], _NN)
        qs, ks = [], []
        for qh, kh in zip(_heads(q_up, MLA_HEADS), _heads(k_up, MLA_HEADS)):
            a, b = _qk_head(qh, kh, r[2], r[3], r[4], r[5], k[3], k[4])
            qs.append(a)
            ks.append(b)
        return [_cat(qs), _cat(ks), _dg(r[1], k[2], _NN)], []

    tabs = [_row(c), _row(sa), _row(sb)]
    qk_consts = [w["uq"], w["k"], w["v"], w["q_norm"], w["k_norm"]]
    q_r, k_r, v_mla = _rows_call(qk_body, [_row(q_lat), _row(kv_lat), kpe] + tabs, qk_consts,
                                 [(1024, BF16), (1024, BF16), (512, BF16)], name="mla_qk_fwd")
    with_attn = [n for n in LATE if n not in LAST]
    o_mla, lse, gathered = _attn_fwd(q_r, k_r, v_mla, _gather_plan([late_shards[n] for n in with_attn]))
    w.update(_late_layout(dict(zip(with_attn, gathered, strict=True))))

    def mix_body(r, k):
        ys = [_mix_head(o, g_, k[0]) for o, g_ in zip(_heads(r[0], GLA_HEADS), _heads(r[1], GLA_HEADS))]
        return [_cat(ys + [r[2]])], []

    cat = _rows_call(mix_body, [_row(o_gla), og, _row(o_mla)], [w["gla_out_norm"]], [(1024, BF16)],
                     name="mix_fwd")[0]
    h1, hn = _matmul(cat, w["out"], "nn", F32, "out_fwd_norm", residual=x, epilogue=_norm_fwd_epilogue(w["norm_xa"]))
    mn = _norm_fwd(mem, w["norm_mem"], "norm_mem_fwd")
    xkv = _matmul(mn, w["xkv"], "nn", F32, "xa_kv_fwd")

    def xa_fn(r, rows, k):
        ks, vs = _heads(k[0], 2 * XA_HEADS)[:XA_HEADS], _heads(k[0], 2 * XA_HEADS)[XA_HEADS:]
        return [r, _cat([_xa_head(a, b, v_, k[1], k[2]) for a, b, v_ in zip(_heads(r, XA_HEADS), ks, vs)])], []

    xq, xo = _matmul(hn, w["xq"], "nn", F32, "xa_q_fwd_attn", epilogue=_Epilogue(
        xa_fn, [], [xkv, w["xa_q_norm"], w["xa_k_norm"]], [(512, F32), (512, BF16)], []))
    h2, fn = _matmul(xo, w["xo"], "nn", F32, "xa_o_fwd_norm", residual=h1, epilogue=_norm_fwd_epilogue(w["norm_ffn"]))
    gg = _matmul(fn, w["wg"], "nt", BF16, "ffn_gate_fwd", b_lead="p")
    uu = _matmul(fn, w["wu"], "nt", BF16, "ffn_up_fwd", b_lead="p")
    act, gathered = _conv_fwd(gg, uu, w["cw"], w["cb"], _gather_plan([late_shards[n] for n in LAST]))
    w["wd"] = gathered[0]
    def loss_fn(y, rows, consts):
        err = y - rows[0]
        part = 0.5 * jnp.sum(jnp.sum(err * err, axis=1, keepdims=True) * (1.0 / D_MODEL), axis=0, keepdims=True)
        return [err * (1.0 / D_MODEL)], [jnp.broadcast_to(part, (1, LANES))]

    dy, loss = _matmul(act, w["wd"], "nn", F32, "ffn_down_fwd_loss", residual=h2, a_lead="k", b_lead="k",
                       epilogue=_Epilogue(loss_fn, [_row(target)], [], [(D_MODEL, F32)], [(1, LANES)]))

    g["ffn_w_down"] = _matmul(act, dy, "tn", BF16, "ffn_down_dw", a_lead="p")
    dact = _matmul(dy, w["wd"], "nt", BF16, "ffn_down_dx", b_lead="p")
    duu, dgg, g["ffn_conv_w"], g["ffn_conv_b"] = _conv_bwd(gg, uu, dact, w["cw"], w["cb"])
    g["ffn_w_gate"] = _matmul(dgg, fn, "tn", BF16, "ffn_gate_dw", a_lead="p")
    g["ffn_w_up"] = _matmul(duu, fn, "tn", BF16, "ffn_up_dw", a_lead="p")
    dh2, g["norm_ffn"] = _matmul(dgg, w["wg"], "nn", F32, "ffn_dx_norm_bwd", a_lead="k", b_lead="k", more=(duu, w["wu"]),
                                 epilogue=_norm_bwd_epilogue(h2, w["norm_ffn"], dy))

    g["xa_w_o"] = _matmul(xo, dh2, "tn", BF16, "xa_o_dw")
    def xa_bwd(dxo_, rows, k):
        kvh = _heads(k[0], 2 * XA_HEADS)
        dq_, dk_, dv_ = [], [], []
        dqn, dkn = 0.0, 0.0
        for h, (a, d_) in enumerate(zip(_heads(rows[0], XA_HEADS), _heads(dxo_, XA_HEADS))):
            _, vjp = jax.vjp(_xa_head, a, kvh[h], kvh[XA_HEADS + h], k[1], k[2])
            ga, gk, gv, gqn, gkn = vjp(d_)
            dq_.append(ga)
            dk_.append(gk)
            dv_.append(gv)
            dqn, dkn = dqn + gqn, dkn + gkn
        return [_cat(dq_)], [_cat(dk_ + dv_), dqn, dkn]

    dxq, dxkv, g["xa_q_norm"], g["xa_k_norm"] = _matmul(dh2, w["xo"], "nt", F32, "xa_o_dx_attn_bwd", epilogue=_Epilogue(
        xa_bwd, [_row(xq)], [xkv, w["xa_q_norm"], w["xa_k_norm"]], [(512, BF16)], [xkv.shape, (1, 128), (1, 128)]))
    g["xa_w_q"] = _matmul(hn, dxq, "tn", BF16, "xa_q_dw")
    dh1, g["norm_xa"] = _matmul(dxq, w["xq"], "nt", F32, "xa_q_dx_norm_bwd",
                                epilogue=_norm_bwd_epilogue(h1, w["norm_xa"], dh2))
    g["xa_w_kv"] = _matmul(mn, dxkv, "tn", BF16, "xa_kv_dw")
    dmn = _matmul(dxkv, w["xkv"], "nt", F32, "xa_kv_dx")
    _, g["norm_mem"] = _norm_bwd(mem, w["norm_mem"], dmn, dmn, "norm_mem_bwd")

    g["w_out"] = _matmul(cat, dh1, "tn", BF16, "out_dw")
    def mix_bwd(dcat_, rows, k):
        do_, dog_ = [], []
        dgn = 0.0
        for o, g_, d_ in zip(_heads(rows[0], GLA_HEADS), _heads(rows[1], GLA_HEADS), _heads(dcat_, GLA_HEADS)):
            _, vjp = jax.vjp(_mix_head, o, g_, k[0])
            a, b, gn_ = vjp(d_)
            do_.append(a)
            dog_.append(b)
            dgn = dgn + gn_
        return [_cat(do_), _cat(dog_), dcat_[:, 512:]], [dgn]

    do_gla, d_og, do_mla, g["gla_out_norm"] = _matmul(dh1, w["out"], "nt", F32, "out_dx_mix_bwd", epilogue=_Epilogue(
        mix_bwd, [_row(o_gla), og], [w["gla_out_norm"]], [(512, F32), (512, BF16), (512, F32)], [(1, 128)]))

    late_parts = _late_grad_shards(g)
    dq_r, dk_r, dv_mla, lands_late = _attn_bwd(q_r, k_r, v_mla, o_mla, lse, do_mla,
                                               _scatter_plan([late_parts[n] for n in LATE]))
    lands_late = dict(zip(LATE, lands_late, strict=True))

    def qk_bwd(r, k):
        q_up, k_up = _dg(r[0], k[0], _NN), _dg(r[1], k[1], _NN)
        dqs, dks = [], []
        dkpe, dqn, dkn = 0.0, 0.0, 0.0
        for qh, kh, dqh, dkh in zip(_heads(q_up, MLA_HEADS), _heads(k_up, MLA_HEADS), _heads(r[6], MLA_HEADS),
                                    _heads(r[7], MLA_HEADS)):
            _, vjp = jax.vjp(lambda a, b, e, f, h_: _qk_head(a, b, e, r[3], r[4], r[5], f, h_), qh, kh, r[2], k[3], k[4])
            ga, gb, ge, gf, gh = vjp((dqh, dkh))
            dqs.append(ga)
            dks.append(gb)
            dkpe, dqn, dkn = dkpe + ge, dqn + gf, dkn + gh
        dq_up, dk_up, dv = _cat(dqs), _cat(dks), r[8]
        dq_lat_ = _dg(dq_up, k[0], _NT)
        dkv_lat_ = _dg(dk_up, k[1], _NT) + _dg(dv, k[2], _NT)
        return [dq_lat_, dkv_lat_, dkpe], [dqn, dkn, _dg(r[0], dq_up, _TN), _dg(r[1], dk_up, _TN), _dg(r[1], dv, _TN)]

    dq_lat, dkv_lat, d_kpe, g["q_norm"], g["k_norm"], g["uq"], g["k"], g["v"] = _rows_call(
        qk_bwd, [_row(q_lat), _row(kv_lat), kpe] + tabs + [_row(dq_r), _row(dk_r), _row(dv_mla)], qk_consts,
        [(MLA_Q_RANK, F32), (MLA_KV_RANK, F32), (128, BF16)],
        [(1, 128), (1, 128), w["uq"].shape, w["k"].shape, w["v"].shape], name="mla_qk_bwd")

    dgq, dgk, dla, dgv = _gla_bwd(proj, la, states, do_gla)

    def dproj_body(r, k):
        alr_, cq_, ckv_, dla_, dq_lat_, dkv_lat_, dgq_, dgk_, dgv_, d_og_, d_kpe_ = r
        _, gate_vjp = jax.vjp(_gate_fn, alr_, k[0], k[1])
        d_alr, gw2, gb = gate_vjp(dla_)
        _, q_vjp = jax.vjp(_rms, cq_, k[2])
        _, kv_vjp = jax.vjp(_rms, ckv_, k[3])
        d_cq, gqa = q_vjp(dq_lat_)
        d_ckv, gkva = kv_vjp(dkv_lat_)
        pieces = [dgq_, dgk_, dgv_, d_og_, d_cq, d_ckv, d_kpe_, d_alr]
        return [_cat([x_.astype(BF16) for x_ in pieces])], [gw2, gb, gqa, gkva]

    dproj, g["w2"], g["gla_gate_b"], g["mla_q_a_norm"], g["mla_kv_a_norm"] = _rows_call(
        dproj_body, [alr, cq, ckv, _row(dla), _row(dq_lat), _row(dkv_lat), _row(dgq), _row(dgk), _row(dgv), _row(d_og),
                     _row(d_kpe)], [w["w2"], w["gate_b"], w["q_a_norm"], w["kv_a_norm"]], [(P_WIDTH, BF16)],
        [(128, 256), (1, 256), (1, 256), (1, 128)], name="proj_cotangent")
    g["in"] = _matmul(dproj, xn, "tn", BF16, "proj_dw")
    dx, g["norm_mix"] = _matmul(dproj, w["in"], "nn", F32, "proj_dx_norm_bwd",
                                epilogue=_norm_bwd_epilogue(x, w["norm_mix"], dh1))
    return loss[0, 0], dx, g, lands_late


def _join_shards(pieces, axis):
    if axis == 0:
        return pieces.reshape(-1, pieces.shape[2])
    return jnp.transpose(pieces, (1, 0, 2)).reshape(pieces.shape[1], -1)


def _split_shards(full, axis):
    r, c = full.shape
    if axis == 0:
        return full.reshape(4, r // 4, c)
    return jnp.transpose(full.reshape(r, 4, c // 4), (1, 0, 2))


def _early_layout(gath, rep):
    w_in = gath["w_in"].reshape(N_WIDTH, D_MODEL)
    z = lambda n: jnp.zeros((n, D_MODEL), w_in.dtype)
    seg = lambda lo, n: w_in[lo:lo + n]
    ukv = _join_shards(gath["mla_w_ukv"], 1).reshape(MLA_KV_RANK, MLA_HEADS, MLA_NOPE + MLA_V)
    w = {
        "in": jnp.concatenate([seg(N_GQ, 256), seg(N_GK, 256), seg(N_GV, 512), seg(N_OG, 512), seg(N_CQ, 256),
                               seg(N_CKV, 128), z(64), seg(N_KPE, 32), z(32), seg(N_ALR, 16), z(112)], axis=0),
        "uq": jnp.pad(_join_shards(gath["mla_w_uq"], 1).reshape(MLA_Q_RANK, MLA_HEADS, MLA_QK),
                      ((0, 0), (0, 0), (0, LANES - MLA_QK))).reshape(MLA_Q_RANK, MLA_HEADS * LANES),
        "k": jnp.pad(ukv[:, :, :MLA_NOPE], ((0, 0), (0, 0), (0, LANES - MLA_NOPE))).reshape(MLA_KV_RANK, -1),
        "v": ukv[:, :, MLA_NOPE:].reshape(MLA_KV_RANK, MLA_HEADS * MLA_V),
        "w2": jnp.pad(_join_shards(gath["gla_gate_w2"], 1), ((0, LANES - GLA_RANK), (0, 0))),
        "cb": rep["ffn_conv_b"].reshape(4, 1, D_FF // 4),
        "q_norm": jnp.pad(rep["mla_q_norm"], ((0, 0), (0, LANES - MLA_QK))),
        "k_norm": jnp.pad(rep["mla_k_norm"], ((0, 0), (0, LANES - MLA_QK))),
        "q_a_norm": rep["mla_q_a_norm"], "kv_a_norm": rep["mla_kv_a_norm"], "gate_b": rep["gla_gate_b"],
    }
    for n in ("norm_mix", "gla_out_norm", "norm_xa", "norm_mem", "xa_q_norm", "xa_k_norm", "norm_ffn"):
        w[n] = rep[n]
    return w


def _late_layout(gath):
    return {"out": _join_shards(gath["w_out"], 0), "xq": _join_shards(gath["xa_w_q"], 0),
            "xkv": _join_shards(gath["xa_w_kv"], 0), "xo": _join_shards(gath["xa_w_o"], 1),
            "wg": gath["ffn_w_gate"], "wu": gath["ffn_w_up"], "cw": gath["ffn_conv_w"]}


def _late_grad_shards(g):
    sh = {"w_out": _split_shards(g["w_out"], 0), "xa_w_q": _split_shards(g["xa_w_q"], 0),
          "xa_w_kv": _split_shards(g["xa_w_kv"], 0), "xa_w_o": _split_shards(g["xa_w_o"], 1),
          "ffn_w_gate": g["ffn_w_gate"], "ffn_w_up": g["ffn_w_up"], "ffn_conv_w": g["ffn_conv_w"],
          "ffn_w_down": g["ffn_w_down"]}
    return {n: v.astype(BF16) for n, v in sh.items()}


def _early_grad_shards(g):
    gi = g["in"]
    seg = lambda lo, n: gi[lo:lo + n]
    w_in = jnp.concatenate([seg(P_GQ, 256), seg(P_GK, 256), seg(P_GV, 512), seg(P_ALR, 16), seg(P_OG, 512),
                            seg(P_CQ, 256), seg(P_CKV, 128), seg(P_KPE + 64, 32)], axis=0)
    uq = g["uq"].reshape(MLA_Q_RANK, MLA_HEADS, LANES)[:, :, :MLA_QK].reshape(MLA_Q_RANK, -1)
    ukv = jnp.concatenate([g["k"].reshape(MLA_KV_RANK, MLA_HEADS, LANES)[:, :, :MLA_NOPE],
                           g["v"].reshape(MLA_KV_RANK, MLA_HEADS, MLA_V)], axis=2).reshape(MLA_KV_RANK, -1)
    sh = {"w_in": w_in.reshape(4, N_WIDTH // 4, D_MODEL), "gla_gate_w2": _split_shards(g["w2"][:GLA_RANK], 1),
          "mla_w_uq": _split_shards(uq, 1), "mla_w_ukv": _split_shards(ukv, 1)}
    sh = {n: v.astype(BF16) for n, v in sh.items()}
    rep = {n: g[n] for n in REPLICATED if n in g}
    rep["mla_q_norm"] = g["q_norm"][:, :MLA_QK]
    rep["mla_k_norm"] = g["k_norm"][:, :MLA_QK]
    rep["ffn_conv_b"] = g["ffn_conv_b"].reshape(1, D_FF)
    return sh, rep


SMALL_SHAPE = (8, 1024)


def _pack_small(vectors):
    flat = jnp.concatenate(vectors, axis=1)
    return jnp.pad(flat, ((0, 0), (0, SMALL_SHAPE[0] * SMALL_SHAPE[1] - flat.shape[1]))).reshape(SMALL_SHAPE)


def _unpack_small(buf, widths):
    flat = buf.reshape(1, -1)
    out, off = [], 0
    for wd in widths:
        out.append(flat[:, off:off + wd])
        off += wd
    return out


ANY = pl.BlockSpec(memory_space=pl.ANY)


def _place():
    x, y, c = lax.axis_index("x"), lax.axis_index("y"), lax.axis_index("c")
    chips = [(1 - x, y), (x, 1 - y), (1 - x, 1 - y)]
    return x, y, c, chips


class _Comm:
    def __init__(self, ins, out_shape, sems, start, finish, mid=None):
        self.ins, self.out_shape, self.sems = list(ins), list(out_shape), list(sems)
        self.start, self.finish, self.mid = start, finish, mid or (lambda *args: None)


def _run_comm(plan, name):
    ni, no = len(plan.ins), len(plan.out_shape)

    def body(*refs):
        ins, outs, sems = refs[:ni], refs[ni:ni + no], refs[ni + no:]
        place = _place()
        plan.start(place, ins, outs, sems)
        plan.mid(place, ins, outs, sems)
        plan.finish(place, ins, outs, sems)

    return pl.pallas_call(body, in_specs=[ANY] * ni, out_specs=[ANY] * no, out_shape=plan.out_shape,
                          scratch_shapes=plan.sems, name=name)(*plan.ins)


def _gather_plan(shards):
    n = len(shards)
    by_rows = [s.shape[0] % (2 * BF16_ROWS) == 0 for s in shards]
    by_cols = [not r and s.shape[1] % (2 * LANES) == 0 for r, s in zip(by_rows, shards)]
    split = [r or c for r, c in zip(by_rows, by_cols)]

    def rows(ref, t, c):
        if by_rows[t]:
            half = shards[t].shape[0] // 2
            return ref.at[pl.ds(pl.multiple_of(c * half, BF16_ROWS), half)]
        if by_cols[t]:
            half = shards[t].shape[1] // 2
            return ref.at[:, pl.ds(pl.multiple_of(c * half, LANES), half)]
        return ref

    def remote(src, dst, ss, rs, to):
        return pltpu.make_async_remote_copy(src_ref=src, dst_ref=dst, send_sem=ss, recv_sem=rs, device_id=to,
                                            device_id_type=MESH)

    def first_wave(place, ins, outs, sems):
        x, y, c, chips = place
        ici_s, ici_r, _, _, local = sems
        me = 2 * x + y
        own = [pltpu.make_async_copy(ins[t], outs[t].at[me], local.at[t]) for t in range(n)]
        push = [remote(rows(ins[t], t, c), rows(outs[t].at[me], t, c), ici_s.at[3 * t + j], ici_r.at[3 * t + j], (px, py, c))
                for t in range(n) for j, (px, py) in enumerate(chips)]
        return own, push

    def second_wave(place, ins, outs, sems, last):
        x, y, c, chips = place
        ici_s, ici_r, d2d_s, d2d_r, local = sems
        sib = (x, y, 1 - c)
        out = []
        for t in range(n):
            for j, (px, py) in enumerate(chips):
                block = outs[t].at[2 * px + py]
                got = rows(block, t, c)
                if split[t]:
                    hand = remote(got, got, d2d_s.at[3 * t + j], d2d_r.at[3 * t + j], sib)
                    theirs = rows(block, t, 1 - c)
                    other = (remote(theirs, theirs, local.at[0], d2d_r.at[3 * t + j], sib) if last else
                             remote(got, got, local.at[0], ici_r.at[3 * t + j], sib))
                    out.append((other, hand))
                elif last:
                    out.append((remote(got, got, local.at[0], ici_r.at[3 * t + j], sib), None))
        return out

    def start(place, ins, outs, sems):
        own, push = first_wave(place, ins, outs, sems)
        for cp in own + push:
            cp.start()

    def mid(place, ins, outs, sems):
        for arrival, hand in second_wave(place, ins, outs, sems, False):
            arrival.wait_recv()
            hand.start()

    def finish(place, ins, outs, sems):
        own, push = first_wave(place, ins, outs, sems)
        for arrival, hand in second_wave(place, ins, outs, sems, True):
            arrival.wait_recv()
            if hand is not None:
                hand.wait_send()
        for cp in push:
            cp.wait_send()
        for cp in own:
            cp.wait()

    dma = pltpu.SemaphoreType.DMA
    return _Comm(shards, [jax.ShapeDtypeStruct((4,) + s.shape, s.dtype) for s in shards],
                 [dma((3 * n,)), dma((3 * n,)), dma((3 * n,)), dma((3 * n,)), dma((n,))], start, finish, mid)


def _scatter_plan(parts, small=None):
    n = len(parts)
    ns = 0 if small is None else 1

    def unpack(place, ins, outs, sems):
        x, y, c, chips = place
        return x, y, c, chips, 2 * x + y, 4 * x + 2 * y + c, (x, y, 1 - c)

    def remote(src, dst, ss, rs, to):
        return pltpu.make_async_remote_copy(src_ref=src, dst_ref=dst, send_sem=ss, recv_sem=rs, device_id=to,
                                            device_id_type=MESH)

    def first_wave(place, ins, outs, sems):
        x, y, c, chips, me, dev, sib = unpack(place, ins, outs, sems)
        ici_s, ici_r, d2d_s, d2d_r, sm_s, sm_r, local = sems
        own, push = [], []
        if ns:
            own.append(pltpu.make_async_copy(ins[n], outs[n].at[dev], local.at[n]))
            for k in range(1, 8):
                px = (1 - x) if (k >> 2) & 1 else x
                py = (1 - y) if (k >> 1) & 1 else y
                pc = (1 - c) if k & 1 else c
                push.append(remote(ins[n], outs[n].at[dev], sm_s.at[k - 1], sm_r.at[k - 1], (px, py, pc)))
        for t in range(n):
            own.append(pltpu.make_async_copy(ins[t].at[me], outs[t].at[dev], local.at[t]))
            push.append(remote(ins[t].at[me], outs[t].at[dev], d2d_s.at[4 * t], d2d_r.at[4 * t], sib))
            for j, (px, py) in enumerate(chips):
                push.append(remote(ins[t].at[2 * px + py], outs[t].at[dev], ici_s.at[3 * t + j], ici_r.at[3 * t + j],
                                   (px, py, c)))
        return own, push

    def start(place, ins, outs, sems):
        own, push = first_wave(place, ins, outs, sems)
        for cp in own + push:
            cp.start()

    def landed(dst, rs, sems, sib):
        remote(dst, dst, sems[-1].at[0], rs, sib).wait_recv()

    def forwards(place, ins, outs, sems):
        x, y, c, chips, me, dev, sib = unpack(place, ins, outs, sems)
        d2d_s, d2d_r = sems[2], sems[3]
        slots = [(t, j, outs[t].at[4 * px + 2 * py + c]) for t in range(n) for j, (px, py) in enumerate(chips)]
        return [(t, j, slot, remote(slot, slot, d2d_s.at[4 * t + 1 + j], d2d_r.at[4 * t + 1 + j], sib))
                for t, j, slot in slots]

    def mid(place, ins, outs, sems):
        sib = unpack(place, ins, outs, sems)[-1]
        for t, j, slot, cp in forwards(place, ins, outs, sems):
            landed(slot, sems[1].at[3 * t + j], sems, sib)
            cp.start()

    def finish(place, ins, outs, sems):
        x, y, c, chips, me, dev, sib = unpack(place, ins, outs, sems)
        d2d_r, sm_r = sems[3], sems[5]
        own, push = first_wave(place, ins, outs, sems)
        push += [cp for _, _, _, cp in forwards(place, ins, outs, sems)]
        for t in range(n):
            landed(outs[t].at[4 * x + 2 * y + (1 - c)], d2d_r.at[4 * t], sems, sib)
            for j, (px, py) in enumerate(chips):
                landed(outs[t].at[4 * px + 2 * py + (1 - c)], d2d_r.at[4 * t + 1 + j], sems, sib)
        if ns:
            for k in range(1, 8):
                px = (1 - x) if (k >> 2) & 1 else x
                py = (1 - y) if (k >> 1) & 1 else y
                pc = (1 - c) if k & 1 else c
                landed(outs[n].at[4 * px + 2 * py + pc], sm_r.at[k - 1], sems, sib)
        for cp in push:
            cp.wait_send()
        for cp in own:
            cp.wait()

    dma = pltpu.SemaphoreType.DMA
    ins = list(parts) + ([small] if ns else [])
    out_shape = [jax.ShapeDtypeStruct((8,) + p.shape[1:], p.dtype) for p in parts]
    if ns:
        out_shape.append(jax.ShapeDtypeStruct((8,) + small.shape, small.dtype))
    return _Comm(ins, out_shape, [dma((3 * n,)), dma((3 * n,)), dma((4 * n,)), dma((4 * n,)), dma((7,)), dma((7,)),
                                  dma((n + 1,))], start, finish, mid)


ADAM_ROWS = 288


def _row_tile(r, cap):
    if r <= cap:
        return r
    return max((t for t in range(8, cap + 1, 8) if r % t == 0), default=r)


def _adamw_update(w, m, v, land):
    g = land[0].astype(F32)
    for i in range(1, 8):
        g = g + land[i].astype(F32)
    m_new = ADAM_B1 * m + (1.0 - ADAM_B1) * g
    v_new = ADAM_B2 * v + (1.0 - ADAM_B2) * (g * g)
    m_hat = m_new / (1.0 - ADAM_B1 ** ADAM_STEP)
    v_hat = v_new / (1.0 - ADAM_B2 ** ADAM_STEP)
    return g, -ADAM_LR * (m_hat / (jnp.sqrt(v_hat) + ADAM_EPS) + ADAM_WD * w), m_new, v_new


def _adamw(tensors, name, comm=None):
    k = len(tensors)
    r, c = tensors[0][0].shape
    t = _row_tile(r, ADAM_ROWS // k)
    tc = c if t < r or r <= ADAM_ROWS else 2 * LANES
    n = (r // t) * (c // tc)
    nci, nco, nsem = (len(comm.ins), len(comm.out_shape), len(comm.sems)) if comm else (0, 0, 0)

    def kern(*refs):
        ins, cins, outs, couts, csems = _split_refs(refs, (4 * k, nci, 4 * k, nco, nsem))
        if comm:
            place = _place()

            @pl.when(pl.program_id(0) == 0)
            def _():
                comm.start(place, cins, couts, csems)

        for i in range(k):
            w_ref, m_ref, v_ref, l_ref = ins[4 * i:4 * i + 4]
            res = _adamw_update(w_ref[...], m_ref[...], v_ref[...], l_ref)
            for ref, val in zip(outs[4 * i:4 * i + 4], res, strict=True):
                ref[...] = val
        if comm:
            @pl.when(pl.program_id(0) == n - 1)
            def _():
                comm.mid(place, cins, couts, csems)
                comm.finish(place, cins, couts, csems)

    where = (lambda i: (i, 0)) if tc == c else (lambda i: (0, i))
    spec = pl.BlockSpec((t, tc), where)
    lspec = pl.BlockSpec((8, t, tc), lambda i: (0,) + where(i))
    res = pl.pallas_call(
        kern, grid=(n,), in_specs=[spec, spec, spec, lspec] * k + [ANY] * nci, out_specs=[spec] * (4 * k) + [ANY] * nco,
        out_shape=[jax.ShapeDtypeStruct((r, c), F32)] * (4 * k) + (comm.out_shape if comm else []),
        scratch_shapes=comm.sems if comm else [],
        compiler_params=pltpu.CompilerParams(dimension_semantics=("arbitrary" if comm else "parallel",),
                                             vmem_limit_bytes=VMEM_LIMIT),
        name=name)(*[x for tens in tensors for x in tens], *(comm.ins if comm else []))
    return [res[4 * i:4 * i + 4] for i in range(k)], res[4 * k:]


def _step(a):
    def sq(n):
        v = a[n][0] if a[n].ndim == 3 else a[n]
        return v.T if n.removeprefix("m_").removeprefix("v_") in TRANSPOSED else v

    payload = lambda n: sq(n) if n in EXACT_GATHER else sq(n).astype(BF16)

    loss, dx, g, lands_late = _local_step(sq("x"), sq("mem"), a["positions"][0], sq("loss_target"),
                                          {n: a[n] for n in REPLICATED}, [payload(n) for n in EARLY],
                                          {n: payload(n) for n in LATE})

    sh, rep = _early_grad_shards(g)
    small = _pack_small([rep[n] for n in REPLICATED] + [loss.reshape(1, 1)])
    *lands_early, land_small = _run_comm(_scatter_plan([sh[n] for n in EARLY], small), "scatter_last")
    quad = lambda n, land: (sq(n), sq("m_" + n), sq("v_" + n), land)
    lands = dict(zip(EARLY, lands_early, strict=True)) | lands_late

    outs = {}
    kinds = ("grad_", "delta_", "new_m_", "new_v_")
    for n, _ in SHARDED:
        res = _adamw([quad(n, lands[n])], "adamw_" + n)[0][0]
        for kind, val in zip(kinds, res, strict=True):
            outs[kind + n] = (val.T if n in TRANSPOSED else val).reshape(a[n].shape)
    zero = jnp.zeros((1, 1), F32)
    packed = [_pack_small([a[p + n] for n in REPLICATED] + [zero]) for p in ("", "m_", "v_")]
    res = _adamw([(*packed, land_small)], "adamw_replicated")[0][0]
    widths = [a[n].shape[1] for n in REPLICATED] + [1]
    for kind, buf in zip(kinds, res, strict=True):
        *vals, total = _unpack_small(buf, widths)
        for n, val in zip(REPLICATED, vals, strict=True):
            outs[kind + n] = val
        if kind == "grad_":
            loss = total[0, 0]

    ordered = [outs[kind + n] for kind in kinds for n in WEIGHTS]
    return (loss, dx[None], *ordered)


def kernel(x, mem, positions, norm_mix, w_in, gla_gate_w2, gla_gate_b, gla_out_norm, mla_q_a_norm, mla_w_uq, mla_kv_a_norm, mla_w_ukv, mla_q_norm, mla_k_norm, w_out, norm_xa, norm_mem, xa_w_q, xa_w_kv, xa_q_norm, xa_k_norm, xa_w_o, norm_ffn, ffn_w_gate, ffn_w_up, ffn_conv_w, ffn_conv_b, ffn_w_down, loss_target, m_norm_mix, m_w_in, m_gla_gate_w2, m_gla_gate_b, m_gla_out_norm, m_mla_q_a_norm, m_mla_w_uq, m_mla_kv_a_norm, m_mla_w_ukv, m_mla_q_norm, m_mla_k_norm, m_w_out, m_norm_xa, m_norm_mem, m_xa_w_q, m_xa_w_kv, m_xa_q_norm, m_xa_k_norm, m_xa_w_o, m_norm_ffn, m_ffn_w_gate, m_ffn_w_up, m_ffn_conv_w, m_ffn_conv_b, m_ffn_w_down, v_norm_mix, v_w_in, v_gla_gate_w2, v_gla_gate_b, v_gla_out_norm, v_mla_q_a_norm, v_mla_w_uq, v_mla_kv_a_norm, v_mla_w_ukv, v_mla_q_norm, v_mla_k_norm, v_w_out, v_norm_xa, v_norm_mem, v_xa_w_q, v_xa_w_kv, v_xa_q_norm, v_xa_k_norm, v_xa_w_o, v_norm_ffn, v_ffn_w_gate, v_ffn_w_up, v_ffn_conv_w, v_ffn_conv_b, v_ffn_w_down):
    return _step(dict(locals()))
```

```python
import functools

import jax
import jax.numpy as jnp
import numpy as np
from jax import lax
from jax.experimental import pallas as pl
from jax.experimental.pallas import tpu as pltpu

F32, BF16 = jnp.float32, jnp.bfloat16
MESH = pl.DeviceIdType.MESH

D_MODEL = 1024
EPS = 1e-6
GLA_HEADS, GLA_DK, GLA_DV, GLA_RANK, GLA_CHUNK = 4, 64, 128, 16, 64
GLA_GATE_NORM = 16.0
MLA_HEADS, MLA_Q_RANK, MLA_KV_RANK, MLA_NOPE, MLA_ROPE, MLA_V = 8, 256, 128, 64, 32, 64
MLA_QK = MLA_NOPE + MLA_ROPE
ROPE_THETA = 10000.0
LOG2E, LN2 = 1.4426950408889634, 0.6931471805599453
XA_HEADS, XA_DIM = 4, 128
D_FF = 2816
ADAM_LR, ADAM_B1, ADAM_B2, ADAM_EPS, ADAM_WD, ADAM_STEP = 0.001, 0.9, 0.999, 1e-08, 0.01, 10

LANES = 128
BF16_ROWS = 16
VMEM_LIMIT = 56 * 1024 * 1024
MATMUL_VMEM = 44 * 1024 * 1024
ROW_TILE = 512

P_GQ, P_GK, P_GV, P_OG, P_CQ, P_CKV, P_KPE, P_ALR, P_WIDTH = 0, 256, 512, 1024, 1536, 1792, 1920, 2048, 2176
N_GQ, N_GK, N_GV, N_ALR, N_OG, N_CQ, N_CKV, N_KPE, N_WIDTH = 0, 256, 512, 1024, 1040, 1552, 1808, 1936, 1968

SHARDED = (("w_in", 1), ("gla_gate_w2", 1), ("mla_w_uq", 1), ("mla_w_ukv", 1), ("w_out", 0), ("xa_w_q", 0),
           ("xa_w_kv", 0), ("xa_w_o", 1), ("ffn_w_gate", 1), ("ffn_w_up", 1), ("ffn_conv_w", 1), ("ffn_w_down", 0))
REPLICATED = ("norm_mix", "gla_gate_b", "gla_out_norm", "mla_q_a_norm", "mla_kv_a_norm", "mla_q_norm", "mla_k_norm",
              "norm_xa", "norm_mem", "xa_q_norm", "xa_k_norm", "norm_ffn", "ffn_conv_b")
EXACT_GATHER = ("gla_gate_w2", "ffn_conv_w")
TRANSPOSED = ("w_in", "ffn_w_gate", "ffn_w_up")
EARLY = ("w_in", "gla_gate_w2", "mla_w_uq", "mla_w_ukv")
LATE = tuple(n for n, _ in SHARDED if n not in EARLY)
LAST = ("ffn_w_down",)
AHEAD = ("w_out", "xa_w_q", "xa_w_kv", "xa_w_o")
WEIGHTS = ("norm_mix", "w_in", "gla_gate_w2", "gla_gate_b", "gla_out_norm", "mla_q_a_norm", "mla_w_uq",
           "mla_kv_a_norm", "mla_w_ukv", "mla_q_norm", "mla_k_norm", "w_out", "norm_xa", "norm_mem", "xa_w_q",
           "xa_w_kv", "xa_q_norm", "xa_k_norm", "xa_w_o", "norm_ffn", "ffn_w_gate", "ffn_w_up", "ffn_conv_w",
           "ffn_conv_b", "ffn_w_down")


_NN = ((1,), (0,))
_NT = ((1,), (1,))
_TN = ((0,), (0,))


def _dg(a, b, dims):
    return lax.dot_general(a.astype(BF16), b.astype(BF16), (dims, ((), ())), preferred_element_type=F32)


@jax.custom_vjp
def _dot_nn(a, b):
    return _dg(a, b, _NN)


_dot_nn.defvjp(lambda a, b: (_dg(a, b, _NN), (a, b)),
               lambda r, g: (_dg(g, r[1], _NT).astype(r[0].dtype), _dg(r[0], g, _TN).astype(r[1].dtype)))


@jax.custom_vjp
def _dot_nt(a, b):
    return _dg(a, b, _NT)


_dot_nt.defvjp(lambda a, b: (_dg(a, b, _NT), (a, b)),
               lambda r, g: (_dg(g, r[1], _NN).astype(r[0].dtype), _dg(g, r[0], _TN).astype(r[1].dtype)))


@jax.custom_vjp
def _dot_tn(a, b):
    return _dg(a, b, _TN)


_dot_tn.defvjp(lambda a, b: (_dg(a, b, _TN), (a, b)),
               lambda r, g: (_dg(r[1], g, _NT).astype(r[0].dtype), _dg(r[0], g, _NN).astype(r[1].dtype)))


def _rms(x, w, n=None):
    n = x.shape[-1] if n is None else n
    ms = jnp.sum(x * x, axis=-1, keepdims=True) * (1.0 / n)
    return x * lax.rsqrt(ms + EPS) * w


def _silu(x):
    return x * jax.nn.sigmoid(x)


def _log_sigmoid(x):
    return jnp.minimum(x, 0.0) - jnp.log(1.0 + jnp.exp(-jnp.abs(x)))


@jax.custom_vjp
def _cumsum_rows(x):
    n = x.shape[0]
    row = lax.broadcasted_iota(jnp.int32, x.shape, 0)
    k = 1
    while k < n:
        x = x + jnp.where(row >= k, pltpu.roll(x, k, 0), 0.0)
        k *= 2
    return x


def _cumsum_rows_bwd(_, g):
    n = g.shape[0]
    row = lax.broadcasted_iota(jnp.int32, g.shape, 0)
    k = 1
    while k < n:
        g = g + jnp.where(row < n - k, pltpu.roll(g, n - k, 0), 0.0)
        k *= 2
    return (g,)


_cumsum_rows.defvjp(lambda x: (_cumsum_rows(x), None), _cumsum_rows_bwd)


def _lane_mask(lo, hi):
    lane = lax.broadcasted_iota(jnp.int32, (1, LANES), 1)
    return ((lane >= lo) & (lane < hi)).astype(F32)


def _tile(n, t):
    t = min(n, t)
    assert n % t == 0, (n, t)
    return t


class _Epilogue:
    def __init__(self, fn, rows=(), consts=(), outs=(), accs=()):
        self.fn, self.rows, self.consts, self.outs, self.accs = fn, list(rows), list(consts), list(outs), list(accs)


def _matmul(a, b, mode, out_dtype, name, residual=None, a_lead=None, b_lead=None, more=None, epilogue=None):
    (a0, a1), (b0, b1) = a.shape[-2:], b.shape[-2:]
    if mode == "nn":
        m, k, k2, n = a0, a1, b0, b1
    elif mode == "nt":
        m, k, n, k2 = a0, a1, b0, b1
    else:
        k, m, k2, n = a0, a1, b0, b1
    assert k == k2, (a.shape, b.shape, mode)
    npar = 4 if "p" in (a_lead, b_lead) else 1
    nsum = 4 if "k" in (a_lead, b_lead) else 1
    pairs = [(a, b)] + ([more] if more else [])
    a_item, b_item, o_item = a.dtype.itemsize, b.dtype.itemsize, jnp.dtype(out_dtype).itemsize
    ep = epilogue
    row_extra = 4 if residual is not None else 0
    if ep:
        row_extra += (sum(r.dtype.itemsize * wd for r, wd, _ in ep.rows) + sum(jnp.dtype(d).itemsize * wd for wd, d in ep.outs)) / n

    def resident(lead, tiles):
        return lead != "p" and tiles == 1

    def vmem_need(tm, tn, tk):
        a_bufs = 1 if resident(a_lead, (m // tm) * (k // tk)) else 2
        b_bufs = 1 if resident(b_lead, (n // tn) * (k // tk)) else 2
        need = a_bufs * (nsum if a_lead == "k" else 1) * tm * tk * a_item + b_bufs * (nsum if b_lead == "k" else 1) * tk * tn * b_item
        need *= len(pairs)
        need += (0 if ep else 2 * tm * tn * o_item) + tm * tn * 4 * (2 if tk < k else 1)
        need += tm * tk * 2 * (a_item == 4 or mode == "tn") + tk * tn * 2 * (b_item == 4)
        return need + int(2 * tm * tn * row_extra) + (3 * tm * tn * 4 if ep else 0)

    halvings = (4096, 2048, 1024, 512, 256, 128, 64, 32, 16, 8)
    if mode == "tn":
        tm = m if m <= 2304 else m // 2
        tn = n if tm * n <= 1024 * 2304 else n // 2
        tk = next((r for r in halvings if k % r == 0 and vmem_need(tm, tn, r) <= MATMUL_VMEM), k)
    else:
        tn, tk = n, k
        tm = next((r for r in halvings if m % r == 0 and vmem_need(r, tn, tk) <= MATMUL_VMEM), m)
    assert m % tm == 0 and n % tn == 0 and k % tk == 0
    assert ep is None or (tn == n and tk == k and npar == 1)
    nk = k // tk
    dims = {"nn": _NN, "nt": _NT, "tn": _TN}[mode]
    n_in = 2 * len(pairs) + (residual is not None)
    n_ep_in = len(ep.rows) + len(ep.consts) if ep else 0
    n_out = len(ep.outs) + len(ep.accs) if ep else 1

    def body(*refs):
        ab, rs, ep_in, outs, scratch = _split_refs(refs, (2 * len(pairs), n_in - 2 * len(pairs), n_ep_in, n_out, nk > 1))
        prod = None
        for a_ref, b_ref in zip(ab[0::2], ab[1::2]):
            for sh in range(nsum):
                term = _dg(a_ref[sh] if a_lead == "k" else a_ref[...], b_ref[sh] if b_lead == "k" else b_ref[...], dims)
                prod = term if prod is None else prod + term

        def finish(r):
            if rs:
                r = r + rs[0][...]
            if ep is None:
                outs[0][...] = r.astype(outs[0].dtype)
                return
            vals = [x[...] for x in ep_in]
            ro, ao = ep.fn(r, vals[:len(ep.rows)], vals[len(ep.rows):])
            for ref, val in zip(outs[:len(ep.outs)], ro, strict=True):
                ref[...] = val.astype(ref.dtype)
            if ep.accs:
                @pl.when(pl.program_id(0) == 0)
                def _():
                    for ref in outs[len(ep.outs):]:
                        ref[...] = jnp.zeros_like(ref)

                for ref, val in zip(outs[len(ep.outs):], ao, strict=True):
                    ref[...] += val

        if nk == 1:
            finish(prod)
            return
        acc = scratch[0]
        kk = pl.program_id(3)

        @pl.when(kk == 0)
        def _():
            acc[...] = prod

        @pl.when(kk > 0)
        def _():
            acc[...] += prod

        @pl.when(kk == nk - 1)
        def _():
            finish(acc[...])

    def spec(lead, blk, idx, tiles=0):
        mode = {"pipeline_mode": pl.Buffered(1)} if resident(lead, tiles) else {}
        if lead is None:
            return pl.BlockSpec(blk, lambda i, j, p, kk: idx(i, j, kk), **mode)
        if lead == "p":
            return pl.BlockSpec((None,) + blk, lambda i, j, p, kk: (p,) + idx(i, j, kk))
        return pl.BlockSpec((nsum,) + blk, lambda i, j, p, kk: (0,) + idx(i, j, kk), **mode)

    a_tiles, b_tiles = (m // tm) * nk, (n // tn) * nk
    if mode == "nn":
        pair_specs = [spec(a_lead, (tm, tk), lambda i, j, kk: (i, kk), a_tiles),
                      spec(b_lead, (tk, tn), lambda i, j, kk: (kk, j), b_tiles)]
    elif mode == "nt":
        pair_specs = [spec(a_lead, (tm, tk), lambda i, j, kk: (i, kk), a_tiles),
                      spec(b_lead, (tn, tk), lambda i, j, kk: (j, kk), b_tiles)]
    else:
        pair_specs = [spec(a_lead, (tk, tm), lambda i, j, kk: (kk, i), a_tiles),
                      spec(b_lead, (tk, tn), lambda i, j, kk: (kk, j), b_tiles)]
    tile = spec(None, (tm, tn), lambda i, j, kk: (i, j))
    in_specs = pair_specs * len(pairs)
    args = [x for pair in pairs for x in pair]
    if residual is not None:
        assert npar == 1
        in_specs.append(tile)
        args.append(residual)
    if ep:
        in_specs += [pl.BlockSpec((tm, wd), functools.partial(lambda cb, i, j, p, kk: (i, cb), cb)) for _, wd, cb in ep.rows]
        in_specs += [pl.BlockSpec(c.shape, lambda i, j, p, kk: (0, 0)) for c in ep.consts]
        args += [r for r, _, _ in ep.rows] + ep.consts
        out_specs = [pl.BlockSpec((tm, wd), lambda i, j, p, kk: (i, 0)) for wd, _ in ep.outs]
        out_specs += [pl.BlockSpec(shape, lambda i, j, p, kk: (0, 0)) for shape in ep.accs]
        out_shape = [jax.ShapeDtypeStruct((m, wd), d) for wd, d in ep.outs] + [jax.ShapeDtypeStruct(sh, F32) for sh in ep.accs]
    else:
        out_specs = spec("p" if npar > 1 else None, (tm, tn), lambda i, j, kk: (i, j))
        out_shape = jax.ShapeDtypeStruct(((4,) if npar > 1 else ()) + (m, n), out_dtype)
    outer = "arbitrary" if ep and ep.accs else "parallel"
    return pl.pallas_call(
        body, grid=(m // tm, n // tn, npar, nk), in_specs=in_specs, out_specs=out_specs, out_shape=out_shape,
        scratch_shapes=[pltpu.VMEM((tm, tn), F32)] if nk > 1 else [],
        compiler_params=pltpu.CompilerParams(dimension_semantics=(outer, outer, outer, "arbitrary"),
                                             vmem_limit_bytes=VMEM_LIMIT),
        name=name)(*args)


def _row(a, width=None, col_block=0):
    return (a, a.shape[1] if width is None else width, col_block)


def _rows_call(body, rows, consts, outs, accs=(), *, name, tile=ROW_TILE, comm=None):
    s = rows[0][0].shape[0]
    t = _tile(s, tile)
    n = s // t
    nr, nc, no, na = len(rows), len(consts), len(outs), len(accs)
    nci, nco, nsem = (len(comm.ins), len(comm.out_shape), len(comm.sems)) if comm else (0, 0, 0)

    def kern(*refs):
        r_refs, c_refs, cins, o_refs, a_refs, couts, csems = _split_refs(refs, (nr, nc, nci, no, na, nco, nsem))
        if comm:
            place = _place()

            @pl.when(pl.program_id(0) == 0)
            def _():
                comm.start(place, cins, couts, csems)

        ro, ao = body([x[...] for x in r_refs], [x[...] for x in c_refs])
        for ref, val in zip(o_refs, ro, strict=True):
            ref[...] = val.astype(ref.dtype)
        if a_refs:
            @pl.when(pl.program_id(0) == 0)
            def _():
                for ref in a_refs:
                    ref[...] = jnp.zeros_like(ref)

            for ref, val in zip(a_refs, ao, strict=True):
                ref[...] += val
        if comm:
            @pl.when(pl.program_id(0) == n - 1)
            def _():
                comm.mid(place, cins, couts, csems)
                comm.finish(place, cins, couts, csems)

    in_specs = [pl.BlockSpec((t, w), functools.partial(lambda cb, i: (i, cb), cb)) for (_, w, cb) in rows]
    in_specs += [pl.BlockSpec(c.shape, lambda i: (0, 0)) for c in consts] + [ANY] * nci
    out_specs = [pl.BlockSpec((t, w), lambda i: (i, 0)) for (w, _) in outs]
    out_specs += [pl.BlockSpec(shape, lambda i: (0, 0)) for shape in accs] + [ANY] * nco
    out_shape = [jax.ShapeDtypeStruct((s, w), dt) for (w, dt) in outs]
    out_shape += [jax.ShapeDtypeStruct(shape, F32) for shape in accs] + (comm.out_shape if comm else [])
    res = pl.pallas_call(
        kern, grid=(n,), in_specs=in_specs, out_specs=out_specs, out_shape=out_shape,
        scratch_shapes=comm.sems if comm else [],
        compiler_params=pltpu.CompilerParams(dimension_semantics=("arbitrary" if accs or comm else "parallel",),
                                             vmem_limit_bytes=VMEM_LIMIT),
        name=name)(*[r[0] for r in rows], *consts, *(comm.ins if comm else []))
    return (res[:no + na], res[no + na:]) if comm else res


def _gla_chunk(q, k, la, v0, v1, s0, s1):
    c = q.shape[0]
    r = lax.broadcasted_iota(jnp.int32, (c, c), 0)
    cc = lax.broadcasted_iota(jnp.int32, (c, c), 1)
    tril = cc <= r
    cum = _cumsum_rows(la)
    cl = jnp.sum(la, axis=0, keepdims=True)
    qd = q * (GLA_DK ** -0.5) * jnp.exp(cum)
    ki = k * jnp.exp(-cum)
    ke = k * jnp.exp(cl - cum)
    dec = jnp.exp(cl)
    outs, news = [], []
    for h, (v, s) in enumerate(((v0, s0), (v1, s1))):
        mk = _lane_mask(GLA_DK * h, GLA_DK * (h + 1))
        qh = qd * mk
        att = jnp.where(tril, _dot_nt(qh, ki), 0.0)
        outs.append(_dot_nn(att, v) + _dot_nt(qh, s))
        news.append(s * dec + _dot_tn(v, ke * mk))
    return outs[0], outs[1], news[0], news[1]


def _gla_specs(tb, rev_nb=None):
    blk = (lambda b: b) if rev_nb is None else (lambda b: rev_nb - 1 - b)
    q = pl.BlockSpec((tb, 128), lambda p, b: (blk(b), P_GQ // 128 + p))
    k = pl.BlockSpec((tb, 128), lambda p, b: (blk(b), P_GK // 128 + p))
    la = pl.BlockSpec((tb, 128), lambda p, b: (blk(b), p))
    v = pl.BlockSpec((tb, 256), lambda p, b: (blk(b), P_GV // 256 + p))
    o = pl.BlockSpec((tb, 256), lambda p, b: (blk(b), p))
    st = pl.BlockSpec((tb // GLA_CHUNK, 2, 128, 128), lambda p, b: (blk(b), p, 0, 0))
    return q, k, la, v, o, st


def _gla_fwd(proj, la):
    s = proj.shape[0]
    tb = _tile(s, ROW_TILE)
    nb, nch = s // tb, tb // GLA_CHUNK

    def kern(q_ref, k_ref, la_ref, v_ref, o_ref, st_ref, s_sc):
        @pl.when(pl.program_id(1) == 0)
        def _():
            s_sc[...] = jnp.zeros_like(s_sc)

        s0, s1 = s_sc[0], s_sc[1]
        for ci in range(nch):
            sl = slice(ci * GLA_CHUNK, (ci + 1) * GLA_CHUNK)
            st_ref[ci, 0] = s0
            st_ref[ci, 1] = s1
            o0, o1, s0, s1 = _gla_chunk(q_ref[sl, :], k_ref[sl, :], la_ref[sl, :], v_ref[sl, 0:128],
                                        v_ref[sl, 128:256], s0, s1)
            o_ref[sl, 0:128] = o0
            o_ref[sl, 128:256] = o1
        s_sc[0] = s0
        s_sc[1] = s1

    q, k, lasp, v, o, st = _gla_specs(tb)
    return pl.pallas_call(
        kern, grid=(2, nb), in_specs=[q, k, lasp, v], out_specs=[o, st],
        out_shape=[jax.ShapeDtypeStruct((s, 512), F32),
                   jax.ShapeDtypeStruct((s // GLA_CHUNK, GLA_HEADS, 128, 128), F32)],
        scratch_shapes=[pltpu.VMEM((2, 128, 128), F32)],
        compiler_params=pltpu.CompilerParams(dimension_semantics=("parallel", "arbitrary"),
                                             vmem_limit_bytes=VMEM_LIMIT),
        name="gla_fwd")(proj, proj, la, proj)


def _gla_bwd(proj, la, states, d_o):
    s = proj.shape[0]
    tb = _tile(s, ROW_TILE)
    nb, nch = s // tb, tb // GLA_CHUNK

    def kern(q_ref, k_ref, la_ref, v_ref, do_ref, st_ref, dq_ref, dk_ref, dla_ref, dv_ref, ds_sc):
        @pl.when(pl.program_id(1) == 0)
        def _():
            ds_sc[...] = jnp.zeros_like(ds_sc)

        d0, d1 = ds_sc[0], ds_sc[1]
        for ci in reversed(range(nch)):
            sl = slice(ci * GLA_CHUNK, (ci + 1) * GLA_CHUNK)
            _, vjp = jax.vjp(_gla_chunk, q_ref[sl, :], k_ref[sl, :], la_ref[sl, :], v_ref[sl, 0:128],
                             v_ref[sl, 128:256], st_ref[ci, 0], st_ref[ci, 1])
            gq, gk, gla, gv0, gv1, d0, d1 = vjp((do_ref[sl, 0:128], do_ref[sl, 128:256], d0, d1))
            dq_ref[sl, :] = gq
            dk_ref[sl, :] = gk
            dla_ref[sl, :] = gla
            dv_ref[sl, 0:128] = gv0
            dv_ref[sl, 128:256] = gv1
        ds_sc[0] = d0
        ds_sc[1] = d1

    q, k, lasp, v, o, st = _gla_specs(tb, rev_nb=nb)
    return pl.pallas_call(
        kern, grid=(2, nb), in_specs=[q, k, lasp, v, o, st], out_specs=[lasp, lasp, lasp, o],
        out_shape=[jax.ShapeDtypeStruct((s, 256), F32), jax.ShapeDtypeStruct((s, 256), F32),
                   jax.ShapeDtypeStruct((s, 256), F32), jax.ShapeDtypeStruct((s, 512), F32)],
        scratch_shapes=[pltpu.VMEM((2, 128, 128), F32)],
        compiler_params=pltpu.CompilerParams(dimension_semantics=("parallel", "arbitrary"),
                                             vmem_limit_bytes=VMEM_LIMIT),
        name="gla_bwd")(proj, proj, la, proj, d_o, states)


def _causal_keep(t):
    return lax.broadcasted_iota(jnp.int32, (t, t), 1) <= lax.broadcasted_iota(jnp.int32, (t, t), 0)


def _split_refs(refs, counts):
    out, off = [], 0
    for cnt in counts:
        out.append(refs[off:off + cnt])
        off += cnt
    return out


def _causal_blocks(n, key_major):
    pairs = ([(ki, qi) for ki in range(n) for qi in range(ki, n)] if key_major else
             [(ki, qi) for qi in range(n) for ki in range(qi + 1)])
    return np.array([ki for ki, _ in pairs], np.int32), np.array([qi for _, qi in pairs], np.int32)


def _attn_fwd(q, k, v, comm, tile=1024):
    s = q.shape[0]
    t = _tile(s, tile)
    n = s // t
    nci, nco = len(comm.ins), len(comm.out_shape)

    ki_tab, qi_tab = _causal_blocks(n, key_major=False)
    steps = len(ki_tab)

    def kern(ki_ref, qi_ref, *refs):
        (q_ref, k_ref, v_ref), cins, (o_ref, lse_ref), couts, (m_sc, l_sc, acc_sc), csems = _split_refs(
            refs, (3, nci, 2, nco, 3, len(comm.sems)))
        pair, step = pl.program_id(0), pl.program_id(1)
        qi, ki = qi_ref[step], ki_ref[step]
        place = _place()

        @pl.when((pair == 0) & (step == 0))
        def _():
            comm.start(place, cins, couts, csems)

        @pl.when((pair == MLA_HEADS // 2 - 1) & (step == 0))
        def _():
            comm.mid(place, cins, couts, csems)

        first = lax.broadcasted_iota(jnp.int32, (t, LANES), 1) < MLA_V

        @pl.when(ki == 0)
        def _():
            m_sc[...] = jnp.full_like(m_sc, -jnp.inf)
            l_sc[...] = jnp.zeros_like(l_sc)
            acc_sc[...] = jnp.zeros_like(acc_sc)

        def update(rows, cols, masked):
            nr = rows.stop - rows.start
            sel = first[:nr]
            alphas, pvs = [], []
            for h in range(2):
                sc = _dg(q_ref[rows, 128 * h:128 * (h + 1)], k_ref[cols, 128 * h:128 * (h + 1)], _NT)
                if masked:
                    sc = jnp.where(_causal_keep(nr), sc, -jnp.inf)
                m_prev = m_sc[h, rows]
                m_new = jnp.maximum(m_prev, jnp.max(sc, axis=1, keepdims=True))
                alpha = jnp.exp2(m_prev - m_new)
                p = jnp.exp2(sc - m_new[:, 0:1])
                l_sc[h, rows] = alpha * l_sc[h, rows] + jnp.sum(p, axis=1, keepdims=True)
                m_sc[h, rows] = m_new
                alphas.append(alpha)
                pvs.append(_dg(p, v_ref[cols, :], _NN))
            acc_sc[rows] = acc_sc[rows] * jnp.where(sel, alphas[0], alphas[1]) + jnp.where(sel, pvs[0], pvs[1])

        halves = [slice(0, t)] if t % 256 else [slice(0, t // 2), slice(t // 2, t)]

        @pl.when(ki < qi)
        def _():
            for rows in halves:
                for cols in halves:
                    update(rows, cols, False)

        @pl.when(ki == qi)
        def _():
            for i, rows in enumerate(halves):
                for j, cols in enumerate(halves[:i + 1]):
                    update(rows, cols, i == j)

        @pl.when(ki == qi)
        def _():
            l = jnp.where(first, l_sc[0], l_sc[1])
            m = jnp.where(first, m_sc[0], m_sc[1])
            o_ref[...] = acc_sc[...] / l
            lse_ref[...] = m + jnp.log2(l)

        @pl.when((pair == MLA_HEADS // 2 - 1) & (step == steps - 1))
        def _():
            comm.finish(place, cins, couts, csems)

    q_idx = lambda p, st, ki_r, qi_r: (qi_r[st], p)
    k_idx = lambda p, st, ki_r, qi_r: (ki_r[st], p)
    res = pl.pallas_call(
        kern, grid_spec=pltpu.PrefetchScalarGridSpec(
            num_scalar_prefetch=2, grid=(MLA_HEADS // 2, steps),
            in_specs=[pl.BlockSpec((t, 256), q_idx), pl.BlockSpec((t, 256), k_idx), pl.BlockSpec((t, 128), k_idx)]
            + [ANY] * nci,
            out_specs=[pl.BlockSpec((t, 128), q_idx), pl.BlockSpec((t, 128), q_idx)] + [ANY] * nco,
            scratch_shapes=[pltpu.VMEM((2, t, LANES), F32), pltpu.VMEM((2, t, LANES), F32),
                            pltpu.VMEM((t, LANES), F32)] + comm.sems),
        out_shape=[jax.ShapeDtypeStruct((s, 512), F32), jax.ShapeDtypeStruct((s, 512), F32)] + comm.out_shape,
        compiler_params=pltpu.CompilerParams(dimension_semantics=("arbitrary", "arbitrary"),
                                             vmem_limit_bytes=VMEM_LIMIT),
        name="mla_attn_fwd")(ki_tab, qi_tab, q, k, v, *comm.ins)
    return res[0], res[1], res[2:]


def _attn_bwd(q, k, v, o, lse, d_o, comm, tile=ROW_TILE):
    s = q.shape[0]
    t = _tile(s, tile)
    n = s // t
    nci, nco = len(comm.ins), len(comm.out_shape)

    ki_tab, qi_tab = _causal_blocks(n, key_major=True)
    steps = len(ki_tab)

    def kern(ki_ref, qi_ref, *refs):
        (q_ref, k_ref, v_ref, o_ref, lse_ref, do_ref), cins, (dq_ref, dk_ref, dv_ref), couts, (dk_sc, dv_sc), csems = \
            _split_refs(refs, (6, nci, 3, nco, 2, len(comm.sems)))
        pair, step = pl.program_id(0), pl.program_id(1)
        ki, qi = ki_ref[step], qi_ref[step]
        place = _place()

        @pl.when((pair == 0) & (step == 0))
        def _():
            comm.start(place, cins, couts, csems)

        @pl.when((pair == MLA_HEADS // 2 - 1) & (step == 0))
        def _():
            comm.mid(place, cins, couts, csems)

        @pl.when((ki == 0) & (qi == 0))
        def _():
            dq_ref[...] = jnp.zeros_like(dq_ref)

        @pl.when(qi == ki)
        def _():
            dk_sc[...] = jnp.zeros_like(dk_sc)
            dv_sc[...] = jnp.zeros_like(dv_sc)

        def update(diagonal):
            keep = _causal_keep(t)
            d_o = do_ref[...]
            prod = d_o * o_ref[...]
            rows = pl.ds(pl.multiple_of(qi * t, t), t)
            for h in range(2):
                hs = slice(128 * h, 128 * (h + 1))
                mk = _lane_mask(MLA_V * h, MLA_V * (h + 1))
                qh, kh = q_ref[:, hs], k_ref[:, hs]
                sc = _dg(qh, kh, _NT)
                if diagonal:
                    sc = jnp.where(keep, sc, -jnp.inf)
                p = jnp.exp2(sc - lse_ref[:, MLA_V * h:MLA_V * h + 1])
                doh = d_o * mk
                dp = _dg(doh * LN2, v_ref[...], _NT)
                delta = jnp.sum(prod * mk, axis=1, keepdims=True) * LN2
                ds = p * (dp - delta)
                dv_sc[...] += _dg(p, doh, _TN)
                dk_sc[:, hs] += _dg(ds, qh, _TN)
                dq_ref[rows, hs] += _dg(ds, kh, _NN)

        @pl.when(qi > ki)
        def _():
            update(False)

        @pl.when(qi == ki)
        def _():
            update(True)

        @pl.when(qi == n - 1)
        def _():
            dk_ref[...] = dk_sc[...]
            dv_ref[...] = dv_sc[...].astype(dv_ref.dtype)

        @pl.when((pair == MLA_HEADS // 2 - 1) & (step == steps - 1))
        def _():
            comm.finish(place, cins, couts, csems)

    q_idx = lambda p, st, ki_r, qi_r: (qi_r[st], p)
    k_idx = lambda p, st, ki_r, qi_r: (ki_r[st], p)
    res = pl.pallas_call(
        kern, grid_spec=pltpu.PrefetchScalarGridSpec(
            num_scalar_prefetch=2, grid=(MLA_HEADS // 2, steps),
            in_specs=[pl.BlockSpec((t, 256), q_idx), pl.BlockSpec((t, 256), k_idx), pl.BlockSpec((t, 128), k_idx),
                      pl.BlockSpec((t, 128), q_idx), pl.BlockSpec((t, 128), q_idx), pl.BlockSpec((t, 128), q_idx)]
            + [ANY] * nci,
            out_specs=[pl.BlockSpec((s, 256), lambda p, st, ki_r, qi_r: (0, p)), pl.BlockSpec((t, 256), k_idx),
                       pl.BlockSpec((t, 128), k_idx)] + [ANY] * nco,
            scratch_shapes=[pltpu.VMEM((t, 256), F32), pltpu.VMEM((t, 128), F32)] + comm.sems),
        out_shape=[jax.ShapeDtypeStruct((s, 1024), F32), jax.ShapeDtypeStruct((s, 1024), F32),
                   jax.ShapeDtypeStruct((s, 512), BF16)] + comm.out_shape,
        compiler_params=pltpu.CompilerParams(dimension_semantics=("arbitrary", "arbitrary"),
                                             vmem_limit_bytes=VMEM_LIMIT),
        name="mla_attn_bwd")(ki_tab, qi_tab, q, k, v, o, lse, d_o, *comm.ins)
    return res[0], res[1], res[2], res[3:]


def _gate_fn(alr, w2, b):
    return _log_sigmoid(_dot_nn(alr, w2) + b) * (1.0 / GLA_GATE_NORM)


def _make_norm_rope(scale):
    def forward(x, w, c, sa, sb):
        r = lax.rsqrt(jnp.sum(x * x, axis=-1, keepdims=True) * (1.0 / MLA_QK) + EPS)
        y = x * r * w
        out = y * c + pltpu.roll(y, LANES - 16, 1) * sa + pltpu.roll(y, 16, 1) * sb
        return (out if scale == 1.0 else out * scale), r

    @jax.custom_vjp
    def norm_rope(x, w, c, sa, sb):
        return forward(x, w, c, sa, sb)[0]

    def fwd(x, w, c, sa, sb):
        out, r = forward(x, w, c, sa, sb)
        return out, (x, w, c, sa, sb, r)

    def bwd(res, g):
        x, w, c, sa, sb, r = res
        if scale != 1.0:
            g = g * scale
        gy = g * c + pltpu.roll(g * sa, 16, 1) + pltpu.roll(g * sb, LANES - 16, 1)
        xr = x * r
        t = gy * w
        m = jnp.sum(t * xr, axis=-1, keepdims=True) * (1.0 / MLA_QK)
        return r * (t - xr * m), jnp.sum(gy * xr, axis=0, keepdims=True), jnp.zeros_like(c), jnp.zeros_like(sa), jnp.zeros_like(sb)

    norm_rope.defvjp(fwd, bwd)
    return norm_rope


_q_norm_rope = _make_norm_rope(MLA_QK ** -0.5 * LOG2E)
_k_norm_rope = _make_norm_rope(1.0)


def _qk_head(qh, kh, kpe, c, sa, sb, qn, kn):
    kfull = kh + kpe * _lane_mask(MLA_NOPE, MLA_QK)
    return _q_norm_rope(qh, qn, c, sa, sb), _k_norm_rope(kfull, kn, c, sa, sb)


def _mix_head(o, og, gn):
    return _rms(o, gn) * _silu(og)


def _xa_head(xq, xk, xv, qn, kn):
    sc = _dot_nt(_rms(xq, qn), _rms(xk, kn)) * (XA_DIM ** -0.5)
    e = jnp.exp(sc - lax.stop_gradient(jnp.max(sc, axis=1, keepdims=True)))
    p = e / jnp.sum(e, axis=1, keepdims=True)
    return _dot_nn(p, xv)


def _heads(x, n):
    return [x[:, 128 * h:128 * (h + 1)] for h in range(n)]


def _cat(xs):
    return jnp.concatenate(xs, axis=1)


def _norm_fwd(x, w, name):
    return _rows_call(lambda r, c: ([_rms(r[0], c[0])], []), [_row(x)], [w], [(x.shape[1], BF16)], name=name)[0]


def _norm_fwd_epilogue(w):
    return _Epilogue(lambda h, rows, consts: ([h, _rms(h, consts[0])], []), [], [w], [(D_MODEL, F32), (D_MODEL, BF16)], [])


def _norm_bwd_epilogue(x, w, add):
    def fn(d_out, rows, consts):
        _, vjp = jax.vjp(_rms, rows[0], consts[0])
        dx, dw = vjp(d_out)
        return [dx + rows[1]], [dw]

    return _Epilogue(fn, [_row(x), _row(add)], [w], [(D_MODEL, F32)], [w.shape])


def _norm_fwd_comm(x, w, comm, name):
    s, d = x.shape
    t = _tile(s, ROW_TILE)
    n = s // t
    nci, nco = len(comm.ins), len(comm.out_shape)

    def kern(*refs):
        (x_ref, w_ref), cins, (o_ref,), couts, csems = _split_refs(refs, (2, nci, 1, nco, len(comm.sems)))
        place = _place()

        @pl.when(pl.program_id(0) == 0)
        def _():
            comm.start(place, cins, couts, csems)

        o_ref[...] = _rms(x_ref[...], w_ref[...]).astype(o_ref.dtype)

        @pl.when(pl.program_id(0) == n - 1)
        def _():
            comm.mid(place, cins, couts, csems)
            comm.finish(place, cins, couts, csems)

    tile = pl.BlockSpec((t, d), lambda i: (i, 0))
    res = pl.pallas_call(
        kern, grid=(n,), in_specs=[tile, pl.BlockSpec(w.shape, lambda i: (0, 0))] + [ANY] * nci,
        out_specs=[tile] + [ANY] * nco, out_shape=[jax.ShapeDtypeStruct((s, d), BF16)] + comm.out_shape,
        scratch_shapes=comm.sems,
        compiler_params=pltpu.CompilerParams(dimension_semantics=("arbitrary",), vmem_limit_bytes=VMEM_LIMIT),
        name=name)(x, w, *comm.ins)
    return res[0], res[1:]


def _norm_bwd(x, w, d_out, add, name):
    def body(r, c):
        _, vjp = jax.vjp(_rms, r[0], c[0])
        dx, dw = vjp(r[1])
        return [dx + r[2]], [dw]

    return _rows_call(body, [_row(x), _row(d_out), _row(add)], [w], [(x.shape[1], F32)], [w.shape], name=name)


CONV_HALO = BF16_ROWS


def _conv_specs(s, f, t):
    n8 = t // CONV_HALO
    cur = pl.BlockSpec((None, t, f), lambda j, i: (j, i, 0))
    prev = pl.BlockSpec((None, CONV_HALO, f), lambda j, i: (j, jnp.maximum(i * n8 - 1, 0), 0))
    nxt = pl.BlockSpec((None, CONV_HALO, f), lambda j, i: (j, jnp.minimum((i + 1) * n8, s // CONV_HALO - 1), 0))
    cw = pl.BlockSpec((None, 3, f), lambda j, i: (j, 0, 0))
    cb = pl.BlockSpec((None, 1, f), lambda j, i: (j, 0, 0))
    return cur, prev, nxt, cw, cb


def _conv_taps(g, prev, first):
    ext = jnp.concatenate([jnp.where(first, 0.0, prev.astype(F32)), g], axis=0)
    return pltpu.roll(ext, 1, 0)[CONV_HALO:], pltpu.roll(ext, 2, 0)[CONV_HALO:]


def _conv_fwd(gg, uu, cw, cb, comm):
    _, s, f = gg.shape
    t = _tile(s, ROW_TILE)
    nt = s // t
    nci, nco = len(comm.ins), len(comm.out_shape)

    def kern(*refs):
        (g_ref, gp_ref, u_ref, cw_ref, cb_ref), cins, (o_ref,), couts, csems = _split_refs(
            refs, (5, nci, 1, nco, len(comm.sems)))
        shard, i = pl.program_id(0), pl.program_id(1)
        place = _place()

        @pl.when((shard == 0) & (i == 0))
        def _():
            comm.start(place, cins, couts, csems)

        @pl.when((shard == 3) & (i == 0))
        def _():
            comm.mid(place, cins, couts, csems)

        g = g_ref[...].astype(F32)
        g1, g2 = _conv_taps(g, gp_ref[...], i == 0)
        w = cw_ref[...]
        gc = cb_ref[...] + w[0:1] * g2 + w[1:2] * g1 + w[2:3] * g
        o_ref[...] = (_silu(gc) * u_ref[...].astype(F32)).astype(o_ref.dtype)

        @pl.when((shard == 3) & (i == nt - 1))
        def _():
            comm.finish(place, cins, couts, csems)

    cur, prev, _, cws, cbs = _conv_specs(s, f, t)
    res = pl.pallas_call(
        kern, grid=(4, nt), in_specs=[cur, prev, cur, cws, cbs] + [ANY] * nci, out_specs=[cur] + [ANY] * nco,
        out_shape=[jax.ShapeDtypeStruct(gg.shape, BF16)] + comm.out_shape, scratch_shapes=comm.sems,
        compiler_params=pltpu.CompilerParams(dimension_semantics=("arbitrary", "arbitrary"), vmem_limit_bytes=VMEM_LIMIT),
        name="ffn_conv_fwd")(gg, gg, uu, cw, cb, *comm.ins)
    return res[0], res[1:]


def _conv_bwd(gg, uu, dact, cw, cb):
    _, s, f = gg.shape
    t = _tile(s, ROW_TILE)
    nt = s // t

    def kern(g_ref, gp_ref, gn_ref, u_ref, un_ref, da_ref, dan_ref, cw_ref, cb_ref, du_ref, dg_ref, dcw_ref, dcb_ref):
        i = pl.program_id(1)
        cat = lambda a_ref, b_ref: jnp.concatenate([a_ref[...].astype(F32), b_ref[...].astype(F32)], axis=0)
        g, u, da = cat(g_ref, gn_ref), cat(u_ref, un_ref), cat(da_ref, dan_ref)
        g1, g2 = _conv_taps(g, gp_ref[...], i == 0)
        w = cw_ref[...]
        gc = cb_ref[...] + w[0:1] * g2 + w[1:2] * g1 + w[2:3] * g
        sg = jax.nn.sigmoid(gc)
        du_ref[...] = (da[:t] * (gc[:t] * sg[:t])).astype(du_ref.dtype)
        row = lax.broadcasted_iota(jnp.int32, (t + CONV_HALO, 1), 0)
        dgc = jnp.where((row < t) | (i < nt - 1), da * u * (sg * (1.0 + gc * (1.0 - sg))), 0.0)
        up1 = pltpu.roll(dgc, t + CONV_HALO - 1, 0)[:t]
        up2 = pltpu.roll(dgc, t + CONV_HALO - 2, 0)[:t]
        dgc = dgc[:t]
        dg_ref[...] = (w[2:3] * dgc + w[1:2] * up1 + w[0:1] * up2).astype(dg_ref.dtype)

        @pl.when(i == 0)
        def _():
            dcw_ref[...] = jnp.zeros_like(dcw_ref)
            dcb_ref[...] = jnp.zeros_like(dcb_ref)

        ones = jnp.ones((8, t), BF16)
        col_sum = lambda a: _dg(ones, a, _NN)[0:1]
        dcw_ref[0:1, :] += col_sum(dgc * g2[:t])
        dcw_ref[1:2, :] += col_sum(dgc * g1[:t])
        dcw_ref[2:3, :] += col_sum(dgc * g[:t])
        dcb_ref[...] += col_sum(dgc)

    cur, prev, nxt, cws, cbs = _conv_specs(s, f, t)
    return pl.pallas_call(
        kern, grid=(4, nt), in_specs=[cur, prev, nxt, cur, nxt, cur, nxt, cws, cbs], out_specs=[cur, cur, cws, cbs],
        out_shape=[jax.ShapeDtypeStruct(gg.shape, BF16), jax.ShapeDtypeStruct(gg.shape, BF16),
                   jax.ShapeDtypeStruct(cw.shape, F32), jax.ShapeDtypeStruct(cb.shape, F32)],
        compiler_params=pltpu.CompilerParams(dimension_semantics=("parallel", "arbitrary"), vmem_limit_bytes=VMEM_LIMIT),
        name="ffn_conv_bwd")(gg, gg, gg, uu, uu, dact, dact, cw, cb)


def _rope_tables(pos):
    half = MLA_ROPE // 2
    lane = jnp.arange(LANES)
    rotary = (lane >= MLA_NOPE) & (lane < MLA_QK)
    inv = jnp.where(rotary, ROPE_THETA ** (-((lane - MLA_NOPE) % half).astype(F32) / half), 0.0)
    ang = pos.astype(F32)[:, None] * inv
    cos, sin = jnp.cos(ang), jnp.sin(ang)
    first = rotary & (lane < MLA_NOPE + half)
    return cos, jnp.where(first, -sin, 0.0), jnp.where(rotary & ~first, sin, 0.0)


def _local_step(x, mem, pos, target, rep, early_shards, late_shards):
    g = {}
    c, sa, sb = _rope_tables(pos)

    xn, gathered = _norm_fwd_comm(x, rep["norm_mix"], _gather_plan(early_shards), "norm_mix_fwd_gather")
    w = _early_layout(dict(zip(EARLY, gathered, strict=True)), rep)

    def proj_fn(r, rows, k):
        la_ = _gate_fn(r[:, P_ALR:P_ALR + 128], k[0], k[1])
        return [r, la_, _rms(r[:, P_CQ:P_CQ + MLA_Q_RANK], k[2]), _rms(r[:, P_CKV:P_CKV + MLA_KV_RANK], k[3])], []

    proj, la, q_lat, kv_lat = _matmul(
        xn, w["in"], "nt", F32, "proj_fwd", epilogue=_Epilogue(
            proj_fn, [], [w["w2"], w["gate_b"], w["q_a_norm"], w["kv_a_norm"]],
            [(P_WIDTH, F32), (256, F32), (MLA_Q_RANK, BF16), (MLA_KV_RANK, BF16)], []))
    alr = _row(proj, 128, P_ALR // 128)
    kpe = _row(proj, 128, P_KPE // 128)
    og = _row(proj, 512, P_OG // 512)
    cq = _row(proj, 256, P_CQ // 256)
    ckv = _row(proj, 128, P_CKV // 128)

    o_gla, states = _gla_fwd(proj, la)

    def qk_body(r, k):
        q_up, k_up = _dg(r[0], k[0], _NN), _dg(r[1], k[1], _NN)
        qs, ks = [], []
        for qh, kh in zip(_heads(q_up, MLA_HEADS), _heads(k_up, MLA_HEADS)):
            a, b = _qk_head(qh, kh, r[2], r[3], r[4], r[5], k[3], k[4])
            qs.append(a)
            ks.append(b)
        return [_cat(qs), _cat(ks), _dg(r[1], k[2], _NN)], []

    tabs = [_row(c), _row(sa), _row(sb)]
    qk_consts = [w["uq"], w["k"], w["v"], w["q_norm"], w["k_norm"]]
    (q_r, k_r, v_mla), ahead = _rows_call(qk_body, [_row(q_lat), _row(kv_lat), kpe] + tabs, qk_consts,
                                          [(1024, BF16), (1024, BF16), (512, BF16)], name="mla_qk_fwd",
                                          comm=_gather_plan([late_shards[n] for n in AHEAD]))
    with_attn = [n for n in LATE if n not in LAST + AHEAD]
    o_mla, lse, gathered = _attn_fwd(q_r, k_r, v_mla, _gather_plan([late_shards[n] for n in with_attn]))
    w.update(_late_layout(dict(zip(AHEAD + tuple(with_attn), list(ahead) + list(gathered), strict=True))))

    def mix_body(r, k):
        ys = [_mix_head(o, g_, k[0]) for o, g_ in zip(_heads(r[0], GLA_HEADS), _heads(r[1], GLA_HEADS))]
        return [_cat(ys + [r[2]])], []

    cat = _rows_call(mix_body, [_row(o_gla), og, _row(o_mla)], [w["gla_out_norm"]], [(1024, BF16)],
                     name="mix_fwd")[0]
    h1, hn = _matmul(cat, w["out"], "nn", F32, "out_fwd_norm", residual=x, epilogue=_norm_fwd_epilogue(w["norm_xa"]))
    mn = _norm_fwd(mem, w["norm_mem"], "norm_mem_fwd")
    xkv = _matmul(mn, w["xkv"], "nn", F32, "xa_kv_fwd")

    def xa_fn(r, rows, k):
        ks, vs = _heads(k[0], 2 * XA_HEADS)[:XA_HEADS], _heads(k[0], 2 * XA_HEADS)[XA_HEADS:]
        return [r, _cat([_xa_head(a, b, v_, k[1], k[2]) for a, b, v_ in zip(_heads(r, XA_HEADS), ks, vs)])], []

    xq, xo = _matmul(hn, w["xq"], "nn", F32, "xa_q_fwd_attn", epilogue=_Epilogue(
        xa_fn, [], [xkv, w["xa_q_norm"], w["xa_k_norm"]], [(512, F32), (512, BF16)], []))
    h2, fn = _matmul(xo, w["xo"], "nn", F32, "xa_o_fwd_norm", residual=h1, epilogue=_norm_fwd_epilogue(w["norm_ffn"]))
    gg = _matmul(fn, w["wg"], "nt", BF16, "ffn_gate_fwd", b_lead="p")
    uu = _matmul(fn, w["wu"], "nt", BF16, "ffn_up_fwd", b_lead="p")
    act, gathered = _conv_fwd(gg, uu, w["cw"], w["cb"], _gather_plan([late_shards[n] for n in LAST]))
    w["wd"] = gathered[0]
    def loss_fn(y, rows, consts):
        err = y - rows[0]
        part = 0.5 * jnp.sum(jnp.sum(err * err, axis=1, keepdims=True) * (1.0 / D_MODEL), axis=0, keepdims=True)
        return [err * (1.0 / D_MODEL)], [jnp.broadcast_to(part, (1, LANES))]

    dy, loss = _matmul(act, w["wd"], "nn", F32, "ffn_down_fwd_loss", residual=h2, a_lead="k", b_lead="k",
                       epilogue=_Epilogue(loss_fn, [_row(target)], [], [(D_MODEL, F32)], [(1, LANES)]))

    g["ffn_w_down"] = _matmul(act, dy, "tn", BF16, "ffn_down_dw", a_lead="p")
    dact = _matmul(dy, w["wd"], "nt", BF16, "ffn_down_dx", b_lead="p")
    duu, dgg, g["ffn_conv_w"], g["ffn_conv_b"] = _conv_bwd(gg, uu, dact, w["cw"], w["cb"])
    g["ffn_w_gate"] = _matmul(dgg, fn, "tn", BF16, "ffn_gate_dw", a_lead="p")
    g["ffn_w_up"] = _matmul(duu, fn, "tn", BF16, "ffn_up_dw", a_lead="p")
    dh2, g["norm_ffn"] = _matmul(dgg, w["wg"], "nn", F32, "ffn_dx_norm_bwd", a_lead="k", b_lead="k", more=(duu, w["wu"]),
                                 epilogue=_norm_bwd_epilogue(h2, w["norm_ffn"], dy))

    g["xa_w_o"] = _matmul(xo, dh2, "tn", BF16, "xa_o_dw")
    def xa_bwd(dxo_, rows, k):
        kvh = _heads(k[0], 2 * XA_HEADS)
        dq_, dk_, dv_ = [], [], []
        dqn, dkn = 0.0, 0.0
        for h, (a, d_) in enumerate(zip(_heads(rows[0], XA_HEADS), _heads(dxo_, XA_HEADS))):
            _, vjp = jax.vjp(_xa_head, a, kvh[h], kvh[XA_HEADS + h], k[1], k[2])
            ga, gk, gv, gqn, gkn = vjp(d_)
            dq_.append(ga)
            dk_.append(gk)
            dv_.append(gv)
            dqn, dkn = dqn + gqn, dkn + gkn
        return [_cat(dq_)], [_cat(dk_ + dv_), dqn, dkn]

    dxq, dxkv, g["xa_q_norm"], g["xa_k_norm"] = _matmul(dh2, w["xo"], "nt", F32, "xa_o_dx_attn_bwd", epilogue=_Epilogue(
        xa_bwd, [_row(xq)], [xkv, w["xa_q_norm"], w["xa_k_norm"]], [(512, BF16)], [xkv.shape, (1, 128), (1, 128)]))
    g["xa_w_q"] = _matmul(hn, dxq, "tn", BF16, "xa_q_dw")
    dh1, g["norm_xa"] = _matmul(dxq, w["xq"], "nt", F32, "xa_q_dx_norm_bwd",
                                epilogue=_norm_bwd_epilogue(h1, w["norm_xa"], dh2))
    g["xa_w_kv"] = _matmul(mn, dxkv, "tn", BF16, "xa_kv_dw")
    dmn = _matmul(dxkv, w["xkv"], "nt", F32, "xa_kv_dx")
    _, g["norm_mem"] = _norm_bwd(mem, w["norm_mem"], dmn, dmn, "norm_mem_bwd")

    g["w_out"] = _matmul(cat, dh1, "tn", BF16, "out_dw")
    def mix_bwd(dcat_, rows, k):
        do_, dog_ = [], []
        dgn = 0.0
        for o, g_, d_ in zip(_heads(rows[0], GLA_HEADS), _heads(rows[1], GLA_HEADS), _heads(dcat_, GLA_HEADS)):
            _, vjp = jax.vjp(_mix_head, o, g_, k[0])
            a, b, gn_ = vjp(d_)
            do_.append(a)
            dog_.append(b)
            dgn = dgn + gn_
        return [_cat(do_), _cat(dog_), dcat_[:, 512:]], [dgn]

    do_gla, d_og, do_mla, g["gla_out_norm"] = _matmul(dh1, w["out"], "nt", F32, "out_dx_mix_bwd", epilogue=_Epilogue(
        mix_bwd, [_row(o_gla), og], [w["gla_out_norm"]], [(512, F32), (512, BF16), (512, F32)], [(1, 128)]))

    late_parts = _late_grad_shards(g)
    dq_r, dk_r, dv_mla, lands_late = _attn_bwd(q_r, k_r, v_mla, o_mla, lse, do_mla,
                                               _scatter_plan([late_parts[n] for n in LATE]))
    lands_late = dict(zip(LATE, lands_late, strict=True))

    def qk_bwd(r, k):
        q_up, k_up = _dg(r[0], k[0], _NN), _dg(r[1], k[1], _NN)
        dqs, dks = [], []
        dkpe, dqn, dkn = 0.0, 0.0, 0.0
        for qh, kh, dqh, dkh in zip(_heads(q_up, MLA_HEADS), _heads(k_up, MLA_HEADS), _heads(r[6], MLA_HEADS),
                                    _heads(r[7], MLA_HEADS)):
            _, vjp = jax.vjp(lambda a, b, e, f, h_: _qk_head(a, b, e, r[3], r[4], r[5], f, h_), qh, kh, r[2], k[3], k[4])
            ga, gb, ge, gf, gh = vjp((dqh, dkh))
            dqs.append(ga)
            dks.append(gb)
            dkpe, dqn, dkn = dkpe + ge, dqn + gf, dkn + gh
        dq_up, dk_up, dv = _cat(dqs), _cat(dks), r[8]
        dq_lat_ = _dg(dq_up, k[0], _NT)
        dkv_lat_ = _dg(dk_up, k[1], _NT) + _dg(dv, k[2], _NT)
        return [dq_lat_, dkv_lat_, dkpe], [dqn, dkn, _dg(r[0], dq_up, _TN), _dg(r[1], dk_up, _TN), _dg(r[1], dv, _TN)]

    dq_lat, dkv_lat, d_kpe, g["q_norm"], g["k_norm"], g["uq"], g["k"], g["v"] = _rows_call(
        qk_bwd, [_row(q_lat), _row(kv_lat), kpe] + tabs + [_row(dq_r), _row(dk_r), _row(dv_mla)], qk_consts,
        [(MLA_Q_RANK, F32), (MLA_KV_RANK, F32), (128, BF16)],
        [(1, 128), (1, 128), w["uq"].shape, w["k"].shape, w["v"].shape], name="mla_qk_bwd")

    dgq, dgk, dla, dgv = _gla_bwd(proj, la, states, do_gla)

    def dproj_body(r, k):
        alr_, cq_, ckv_, dla_, dq_lat_, dkv_lat_, dgq_, dgk_, dgv_, d_og_, d_kpe_ = r
        _, gate_vjp = jax.vjp(_gate_fn, alr_, k[0], k[1])
        d_alr, gw2, gb = gate_vjp(dla_)
        _, q_vjp = jax.vjp(_rms, cq_, k[2])
        _, kv_vjp = jax.vjp(_rms, ckv_, k[3])
        d_cq, gqa = q_vjp(dq_lat_)
        d_ckv, gkva = kv_vjp(dkv_lat_)
        pieces = [dgq_, dgk_, dgv_, d_og_, d_cq, d_ckv, d_kpe_, d_alr]
        return [_cat([x_.astype(BF16) for x_ in pieces])], [gw2, gb, gqa, gkva]

    dproj, g["w2"], g["gla_gate_b"], g["mla_q_a_norm"], g["mla_kv_a_norm"] = _rows_call(
        dproj_body, [alr, cq, ckv, _row(dla), _row(dq_lat), _row(dkv_lat), _row(dgq), _row(dgk), _row(dgv), _row(d_og),
                     _row(d_kpe)], [w["w2"], w["gate_b"], w["q_a_norm"], w["kv_a_norm"]], [(P_WIDTH, BF16)],
        [(128, 256), (1, 256), (1, 256), (1, 128)], name="proj_cotangent")
    g["in"] = _matmul(dproj, xn, "tn", BF16, "proj_dw")
    dx, g["norm_mix"] = _matmul(dproj, w["in"], "nn", F32, "proj_dx_norm_bwd",
                                epilogue=_norm_bwd_epilogue(x, w["norm_mix"], dh1))
    return loss[0, 0], dx, g, lands_late


def _join_shards(pieces, axis):
    if axis == 0:
        return pieces.reshape(-1, pieces.shape[2])
    return jnp.transpose(pieces, (1, 0, 2)).reshape(pieces.shape[1], -1)


def _split_shards(full, axis):
    r, c = full.shape
    if axis == 0:
        return full.reshape(4, r // 4, c)
    return jnp.transpose(full.reshape(r, 4, c // 4), (1, 0, 2))


def _early_layout(gath, rep):
    w_in = gath["w_in"].reshape(N_WIDTH, D_MODEL)
    z = lambda n: jnp.zeros((n, D_MODEL), w_in.dtype)
    seg = lambda lo, n: w_in[lo:lo + n]
    ukv = _join_shards(gath["mla_w_ukv"], 1).reshape(MLA_KV_RANK, MLA_HEADS, MLA_NOPE + MLA_V)
    w = {
        "in": jnp.concatenate([seg(N_GQ, 256), seg(N_GK, 256), seg(N_GV, 512), seg(N_OG, 512), seg(N_CQ, 256),
                               seg(N_CKV, 128), z(64), seg(N_KPE, 32), z(32), seg(N_ALR, 16), z(112)], axis=0),
        "uq": jnp.pad(_join_shards(gath["mla_w_uq"], 1).reshape(MLA_Q_RANK, MLA_HEADS, MLA_QK),
                      ((0, 0), (0, 0), (0, LANES - MLA_QK))).reshape(MLA_Q_RANK, MLA_HEADS * LANES),
        "k": jnp.pad(ukv[:, :, :MLA_NOPE], ((0, 0), (0, 0), (0, LANES - MLA_NOPE))).reshape(MLA_KV_RANK, -1),
        "v": ukv[:, :, MLA_NOPE:].reshape(MLA_KV_RANK, MLA_HEADS * MLA_V),
        "w2": jnp.pad(_join_shards(gath["gla_gate_w2"], 1), ((0, LANES - GLA_RANK), (0, 0))),
        "cb": rep["ffn_conv_b"].reshape(4, 1, D_FF // 4),
        "q_norm": jnp.pad(rep["mla_q_norm"], ((0, 0), (0, LANES - MLA_QK))),
        "k_norm": jnp.pad(rep["mla_k_norm"], ((0, 0), (0, LANES - MLA_QK))),
        "q_a_norm": rep["mla_q_a_norm"], "kv_a_norm": rep["mla_kv_a_norm"], "gate_b": rep["gla_gate_b"],
    }
    for n in ("norm_mix", "gla_out_norm", "norm_xa", "norm_mem", "xa_q_norm", "xa_k_norm", "norm_ffn"):
        w[n] = rep[n]
    return w


def _late_layout(gath):
    return {"out": _join_shards(gath["w_out"], 0), "xq": _join_shards(gath["xa_w_q"], 0),
            "xkv": _join_shards(gath["xa_w_kv"], 0), "xo": _join_shards(gath["xa_w_o"], 1),
            "wg": gath["ffn_w_gate"], "wu": gath["ffn_w_up"], "cw": gath["ffn_conv_w"]}


def _late_grad_shards(g):
    sh = {"w_out": _split_shards(g["w_out"], 0), "xa_w_q": _split_shards(g["xa_w_q"], 0),
          "xa_w_kv": _split_shards(g["xa_w_kv"], 0), "xa_w_o": _split_shards(g["xa_w_o"], 1),
          "ffn_w_gate": g["ffn_w_gate"], "ffn_w_up": g["ffn_w_up"], "ffn_conv_w": g["ffn_conv_w"],
          "ffn_w_down": g["ffn_w_down"]}
    return {n: v.astype(BF16) for n, v in sh.items()}


def _early_grad_shards(g):
    gi = g["in"]
    seg = lambda lo, n: gi[lo:lo + n]
    w_in = jnp.concatenate([seg(P_GQ, 256), seg(P_GK, 256), seg(P_GV, 512), seg(P_ALR, 16), seg(P_OG, 512),
                            seg(P_CQ, 256), seg(P_CKV, 128), seg(P_KPE + 64, 32)], axis=0)
    uq = g["uq"].reshape(MLA_Q_RANK, MLA_HEADS, LANES)[:, :, :MLA_QK].reshape(MLA_Q_RANK, -1)
    ukv = jnp.concatenate([g["k"].reshape(MLA_KV_RANK, MLA_HEADS, LANES)[:, :, :MLA_NOPE],
                           g["v"].reshape(MLA_KV_RANK, MLA_HEADS, MLA_V)], axis=2).reshape(MLA_KV_RANK, -1)
    sh = {"w_in": w_in.reshape(4, N_WIDTH // 4, D_MODEL), "gla_gate_w2": _split_shards(g["w2"][:GLA_RANK], 1),
          "mla_w_uq": _split_shards(uq, 1), "mla_w_ukv": _split_shards(ukv, 1)}
    sh = {n: v.astype(BF16) for n, v in sh.items()}
    rep = {n: g[n] for n in REPLICATED if n in g}
    rep["mla_q_norm"] = g["q_norm"][:, :MLA_QK]
    rep["mla_k_norm"] = g["k_norm"][:, :MLA_QK]
    rep["ffn_conv_b"] = g["ffn_conv_b"].reshape(1, D_FF)
    return sh, rep


SMALL_SHAPE = (8, 1024)


def _pack_small(vectors):
    flat = jnp.concatenate(vectors, axis=1)
    return jnp.pad(flat, ((0, 0), (0, SMALL_SHAPE[0] * SMALL_SHAPE[1] - flat.shape[1]))).reshape(SMALL_SHAPE)


def _unpack_small(buf, widths):
    flat = buf.reshape(1, -1)
    out, off = [], 0
    for wd in widths:
        out.append(flat[:, off:off + wd])
        off += wd
    return out


ANY = pl.BlockSpec(memory_space=pl.ANY)


def _place():
    x, y, c = lax.axis_index("x"), lax.axis_index("y"), lax.axis_index("c")
    chips = [(1 - x, y), (x, 1 - y), (1 - x, 1 - y)]
    return x, y, c, chips


class _Comm:
    def __init__(self, ins, out_shape, sems, start, finish, mid=None):
        self.ins, self.out_shape, self.sems = list(ins), list(out_shape), list(sems)
        self.start, self.finish, self.mid = start, finish, mid or (lambda *args: None)


def _run_comm(plan, name):
    ni, no = len(plan.ins), len(plan.out_shape)

    def body(*refs):
        ins, outs, sems = refs[:ni], refs[ni:ni + no], refs[ni + no:]
        place = _place()
        plan.start(place, ins, outs, sems)
        plan.mid(place, ins, outs, sems)
        plan.finish(place, ins, outs, sems)

    return pl.pallas_call(body, in_specs=[ANY] * ni, out_specs=[ANY] * no, out_shape=plan.out_shape,
                          scratch_shapes=plan.sems, name=name)(*plan.ins)


def _gather_plan(shards):
    n = len(shards)
    by_rows = [s.shape[0] % (2 * BF16_ROWS) == 0 for s in shards]
    by_cols = [not r and s.shape[1] % (2 * LANES) == 0 for r, s in zip(by_rows, shards)]
    split = [r or c for r, c in zip(by_rows, by_cols)]

    def rows(ref, t, c):
        if by_rows[t]:
            half = shards[t].shape[0] // 2
            return ref.at[pl.ds(pl.multiple_of(c * half, BF16_ROWS), half)]
        if by_cols[t]:
            half = shards[t].shape[1] // 2
            return ref.at[:, pl.ds(pl.multiple_of(c * half, LANES), half)]
        return ref

    def remote(src, dst, ss, rs, to):
        return pltpu.make_async_remote_copy(src_ref=src, dst_ref=dst, send_sem=ss, recv_sem=rs, device_id=to,
                                            device_id_type=MESH)

    def first_wave(place, ins, outs, sems):
        x, y, c, chips = place
        ici_s, ici_r, _, _, local = sems
        me = 2 * x + y
        own = [pltpu.make_async_copy(ins[t], outs[t].at[me], local.at[t]) for t in range(n)]
        push = [remote(rows(ins[t], t, c), rows(outs[t].at[me], t, c), ici_s.at[3 * t + j], ici_r.at[3 * t + j], (px, py, c))
                for t in range(n) for j, (px, py) in enumerate(chips)]
        return own, push

    def second_wave(place, ins, outs, sems, last):
        x, y, c, chips = place
        ici_s, ici_r, d2d_s, d2d_r, local = sems
        sib = (x, y, 1 - c)
        out = []
        for t in range(n):
            for j, (px, py) in enumerate(chips):
                block = outs[t].at[2 * px + py]
                got = rows(block, t, c)
                if split[t]:
                    hand = remote(got, got, d2d_s.at[3 * t + j], d2d_r.at[3 * t + j], sib)
                    theirs = rows(block, t, 1 - c)
                    other = (remote(theirs, theirs, local.at[0], d2d_r.at[3 * t + j], sib) if last else
                             remote(got, got, local.at[0], ici_r.at[3 * t + j], sib))
                    out.append((other, hand))
                elif last:
                    out.append((remote(got, got, local.at[0], ici_r.at[3 * t + j], sib), None))
        return out

    def start(place, ins, outs, sems):
        own, push = first_wave(place, ins, outs, sems)
        for cp in own + push:
            cp.start()

    def mid(place, ins, outs, sems):
        for arrival, hand in second_wave(place, ins, outs, sems, False):
            arrival.wait_recv()
            hand.start()

    def finish(place, ins, outs, sems):
        own, push = first_wave(place, ins, outs, sems)
        for arrival, hand in second_wave(place, ins, outs, sems, True):
            arrival.wait_recv()
            if hand is not None:
                hand.wait_send()
        for cp in push:
            cp.wait_send()
        for cp in own:
            cp.wait()

    dma = pltpu.SemaphoreType.DMA
    return _Comm(shards, [jax.ShapeDtypeStruct((4,) + s.shape, s.dtype) for s in shards],
                 [dma((3 * n,)), dma((3 * n,)), dma((3 * n,)), dma((3 * n,)), dma((n,))], start, finish, mid)


def _scatter_plan(parts, small=None):
    n = len(parts)
    ns = 0 if small is None else 1

    def unpack(place, ins, outs, sems):
        x, y, c, chips = place
        return x, y, c, chips, 2 * x + y, 4 * x + 2 * y + c, (x, y, 1 - c)

    def remote(src, dst, ss, rs, to):
        return pltpu.make_async_remote_copy(src_ref=src, dst_ref=dst, send_sem=ss, recv_sem=rs, device_id=to,
                                            device_id_type=MESH)

    def first_wave(place, ins, outs, sems):
        x, y, c, chips, me, dev, sib = unpack(place, ins, outs, sems)
        ici_s, ici_r, d2d_s, d2d_r, sm_s, sm_r, local = sems
        own, push = [], []
        if ns:
            own.append(pltpu.make_async_copy(ins[n], outs[n].at[dev], local.at[n]))
            for k in range(1, 8):
                px = (1 - x) if (k >> 2) & 1 else x
                py = (1 - y) if (k >> 1) & 1 else y
                pc = (1 - c) if k & 1 else c
                push.append(remote(ins[n], outs[n].at[dev], sm_s.at[k - 1], sm_r.at[k - 1], (px, py, pc)))
        for t in range(n):
            own.append(pltpu.make_async_copy(ins[t].at[me], outs[t].at[dev], local.at[t]))
            push.append(remote(ins[t].at[me], outs[t].at[dev], d2d_s.at[4 * t], d2d_r.at[4 * t], sib))
            for j, (px, py) in enumerate(chips):
                push.append(remote(ins[t].at[2 * px + py], outs[t].at[dev], ici_s.at[3 * t + j], ici_r.at[3 * t + j],
                                   (px, py, c)))
        return own, push

    def start(place, ins, outs, sems):
        own, push = first_wave(place, ins, outs, sems)
        for cp in own + push:
            cp.start()

    def landed(dst, rs, sems, sib):
        remote(dst, dst, sems[-1].at[0], rs, sib).wait_recv()

    def forwards(place, ins, outs, sems):
        x, y, c, chips, me, dev, sib = unpack(place, ins, outs, sems)
        d2d_s, d2d_r = sems[2], sems[3]
        slots = [(t, j, outs[t].at[4 * px + 2 * py + c]) for t in range(n) for j, (px, py) in enumerate(chips)]
        return [(t, j, slot, remote(slot, slot, d2d_s.at[4 * t + 1 + j], d2d_r.at[4 * t + 1 + j], sib))
                for t, j, slot in slots]

    def mid(place, ins, outs, sems):
        sib = unpack(place, ins, outs, sems)[-1]
        for t, j, slot, cp in forwards(place, ins, outs, sems):
            landed(slot, sems[1].at[3 * t + j], sems, sib)
            cp.start()

    def finish(place, ins, outs, sems):
        x, y, c, chips, me, dev, sib = unpack(place, ins, outs, sems)
        d2d_r, sm_r = sems[3], sems[5]
        own, push = first_wave(place, ins, outs, sems)
        push += [cp for _, _, _, cp in forwards(place, ins, outs, sems)]
        for t in range(n):
            landed(outs[t].at[4 * x + 2 * y + (1 - c)], d2d_r.at[4 * t], sems, sib)
            for j, (px, py) in enumerate(chips):
                landed(outs[t].at[4 * px + 2 * py + (1 - c)], d2d_r.at[4 * t + 1 + j], sems, sib)
        if ns:
            for k in range(1, 8):
                px = (1 - x) if (k >> 2) & 1 else x
                py = (1 - y) if (k >> 1) & 1 else y
                pc = (1 - c) if k & 1 else c
                landed(outs[n].at[4 * px + 2 * py + pc], sm_r.at[k - 1], sems, sib)
        for cp in push:
            cp.wait_send()
        for cp in own:
            cp.wait()

    dma = pltpu.SemaphoreType.DMA
    ins = list(parts) + ([small] if ns else [])
    out_shape = [jax.ShapeDtypeStruct((8,) + p.shape[1:], p.dtype) for p in parts]
    if ns:
        out_shape.append(jax.ShapeDtypeStruct((8,) + small.shape, small.dtype))
    return _Comm(ins, out_shape, [dma((3 * n,)), dma((3 * n,)), dma((4 * n,)), dma((4 * n,)), dma((7,)), dma((7,)),
                                  dma((n + 1,))], start, finish, mid)


ADAM_ROWS = 288


def _row_tile(r, cap):
    if r <= cap:
        return r
    return max((t for t in range(8, cap + 1, 8) if r % t == 0), default=r)


def _adamw_update(w, m, v, land):
    g = land[0].astype(F32)
    for i in range(1, 8):
        g = g + land[i].astype(F32)
    m_new = ADAM_B1 * m + (1.0 - ADAM_B1) * g
    v_new = ADAM_B2 * v + (1.0 - ADAM_B2) * (g * g)
    m_hat = m_new / (1.0 - ADAM_B1 ** ADAM_STEP)
    v_hat = v_new / (1.0 - ADAM_B2 ** ADAM_STEP)
    return g, -ADAM_LR * (m_hat / (jnp.sqrt(v_hat) + ADAM_EPS) + ADAM_WD * w), m_new, v_new


def _adamw(tensors, name, comm=None):
    k = len(tensors)
    r, c = tensors[0][0].shape
    t = _row_tile(r, ADAM_ROWS // k)
    tc = c if t < r or r <= ADAM_ROWS else 2 * LANES
    n = (r // t) * (c // tc)
    nci, nco, nsem = (len(comm.ins), len(comm.out_shape), len(comm.sems)) if comm else (0, 0, 0)

    def kern(*refs):
        ins, cins, outs, couts, csems = _split_refs(refs, (4 * k, nci, 4 * k, nco, nsem))
        if comm:
            place = _place()

            @pl.when(pl.program_id(0) == 0)
            def _():
                comm.start(place, cins, couts, csems)

        for i in range(k):
            w_ref, m_ref, v_ref, l_ref = ins[4 * i:4 * i + 4]
            res = _adamw_update(w_ref[...], m_ref[...], v_ref[...], l_ref)
            for ref, val in zip(outs[4 * i:4 * i + 4], res, strict=True):
                ref[...] = val
        if comm:
            @pl.when(pl.program_id(0) == n - 1)
            def _():
                comm.mid(place, cins, couts, csems)
                comm.finish(place, cins, couts, csems)

    where = (lambda i: (i, 0)) if tc == c else (lambda i: (0, i))
    spec = pl.BlockSpec((t, tc), where)
    lspec = pl.BlockSpec((8, t, tc), lambda i: (0,) + where(i))
    res = pl.pallas_call(
        kern, grid=(n,), in_specs=[spec, spec, spec, lspec] * k + [ANY] * nci, out_specs=[spec] * (4 * k) + [ANY] * nco,
        out_shape=[jax.ShapeDtypeStruct((r, c), F32)] * (4 * k) + (comm.out_shape if comm else []),
        scratch_shapes=comm.sems if comm else [],
        compiler_params=pltpu.CompilerParams(dimension_semantics=("arbitrary" if comm else "parallel",),
                                             vmem_limit_bytes=VMEM_LIMIT),
        name=name)(*[x for tens in tensors for x in tens], *(comm.ins if comm else []))
    return [res[4 * i:4 * i + 4] for i in range(k)], res[4 * k:]


def _step(a):
    def sq(n):
        v = a[n][0] if a[n].ndim == 3 else a[n]
        return v.T if n.removeprefix("m_").removeprefix("v_") in TRANSPOSED else v

    payload = lambda n: sq(n) if n in EXACT_GATHER else sq(n).astype(BF16)

    loss, dx, g, lands_late = _local_step(sq("x"), sq("mem"), a["positions"][0], sq("loss_target"),
                                          {n: a[n] for n in REPLICATED}, [payload(n) for n in EARLY],
                                          {n: payload(n) for n in LATE})

    sh, rep = _early_grad_shards(g)
    small = _pack_small([rep[n] for n in REPLICATED] + [loss.reshape(1, 1)])
    *lands_early, land_small = _run_comm(_scatter_plan([sh[n] for n in EARLY], small), "scatter_last")
    quad = lambda n, land: (sq(n), sq("m_" + n), sq("v_" + n), land)
    lands = dict(zip(EARLY, lands_early, strict=True)) | lands_late

    outs = {}
    kinds = ("grad_", "delta_", "new_m_", "new_v_")
    for n, _ in SHARDED:
        res = _adamw([quad(n, lands[n])], "adamw_" + n)[0][0]
        for kind, val in zip(kinds, res, strict=True):
            outs[kind + n] = (val.T if n in TRANSPOSED else val).reshape(a[n].shape)
    zero = jnp.zeros((1, 1), F32)
    packed = [_pack_small([a[p + n] for n in REPLICATED] + [zero]) for p in ("", "m_", "v_")]
    res = _adamw([(*packed, land_small)], "adamw_replicated")[0][0]
    widths = [a[n].shape[1] for n in REPLICATED] + [1]
    for kind, buf in zip(kinds, res, strict=True):
        *vals, total = _unpack_small(buf, widths)
        for n, val in zip(REPLICATED, vals, strict=True):
            outs[kind + n] = val
        if kind == "grad_":
            loss = total[0, 0]

    ordered = [outs[kind + n] for kind in kinds for n in WEIGHTS]
    return (loss, dx[None], *ordered)


def kernel(x, mem, positions, norm_mix, w_in, gla_gate_w2, gla_gate_b, gla_out_norm, mla_q_a_norm, mla_w_uq, mla_kv_a_norm, mla_w_ukv, mla_q_norm, mla_k_norm, w_out, norm_xa, norm_mem, xa_w_q, xa_w_kv, xa_q_norm, xa_k_norm, xa_w_o, norm_ffn, ffn_w_gate, ffn_w_up, ffn_conv_w, ffn_conv_b, ffn_w_down, loss_target, m_norm_mix, m_w_in, m_gla_gate_w2, m_gla_gate_b, m_gla_out_norm, m_mla_q_a_norm, m_mla_w_uq, m_mla_kv_a_norm, m_mla_w_ukv, m_mla_q_norm, m_mla_k_norm, m_w_out, m_norm_xa, m_norm_mem, m_xa_w_q, m_xa_w_kv, m_xa_q_norm, m_xa_k_norm, m_xa_w_o, m_norm_ffn, m_ffn_w_gate, m_ffn_w_up, m_ffn_conv_w, m_ffn_conv_b, m_ffn_w_down, v_norm_mix, v_w_in, v_gla_gate_w2, v_gla_gate_b, v_gla_out_norm, v_mla_q_a_norm, v_mla_w_uq, v_mla_kv_a_norm, v_mla_w_ukv, v_mla_q_norm, v_mla_k_norm, v_w_out, v_norm_xa, v_norm_mem, v_xa_w_q, v_xa_w_kv, v_xa_q_norm, v_xa_k_norm, v_xa_w_o, v_norm_ffn, v_ffn_w_gate, v_ffn_w_up, v_ffn_conv_w, v_ffn_conv_b, v_ffn_w_down):
    return _step(dict(locals()))
```

```python
import functools

import jax
import jax.numpy as jnp
import numpy as np
from jax import lax
from jax.experimental import pallas as pl
from jax.experimental.pallas import tpu as pltpu

F32, BF16 = jnp.float32, jnp.bfloat16
MESH = pl.DeviceIdType.MESH

D_MODEL = 1024
EPS = 1e-6
GLA_HEADS, GLA_DK, GLA_DV, GLA_RANK, GLA_CHUNK = 4, 64, 128, 16, 64
GLA_GATE_NORM = 16.0
MLA_HEADS, MLA_Q_RANK, MLA_KV_RANK, MLA_NOPE, MLA_ROPE, MLA_V = 8, 256, 128, 64, 32, 64
MLA_QK = MLA_NOPE + MLA_ROPE
ROPE_THETA = 10000.0
LOG2E, LN2 = 1.4426950408889634, 0.6931471805599453
XA_HEADS, XA_DIM = 4, 128
D_FF = 2816
ADAM_LR, ADAM_B1, ADAM_B2, ADAM_EPS, ADAM_WD, ADAM_STEP = 0.001, 0.9, 0.999, 1e-08, 0.01, 10

LANES = 128
BF16_ROWS = 16
VMEM_LIMIT = 56 * 1024 * 1024
MATMUL_VMEM = 44 * 1024 * 1024
ROW_TILE = 512

P_GQ, P_GK, P_GV, P_OG, P_CQ, P_CKV, P_KPE, P_ALR, P_WIDTH = 0, 256, 512, 1024, 1536, 1792, 1920, 2048, 2176
N_GQ, N_GK, N_GV, N_ALR, N_OG, N_CQ, N_CKV, N_KPE, N_WIDTH = 0, 256, 512, 1024, 1040, 1552, 1808, 1936, 1968

SHARDED = (("w_in", 1), ("gla_gate_w2", 1), ("mla_w_uq", 1), ("mla_w_ukv", 1), ("w_out", 0), ("xa_w_q", 0),
           ("xa_w_kv", 0), ("xa_w_o", 1), ("ffn_w_gate", 1), ("ffn_w_up", 1), ("ffn_conv_w", 1), ("ffn_w_down", 0))
REPLICATED = ("norm_mix", "gla_gate_b", "gla_out_norm", "mla_q_a_norm", "mla_kv_a_norm", "mla_q_norm", "mla_k_norm",
              "norm_xa", "norm_mem", "xa_q_norm", "xa_k_norm", "norm_ffn", "ffn_conv_b")
EXACT_GATHER = ("gla_gate_w2", "ffn_conv_w")
TRANSPOSED = ("w_in", "ffn_w_gate", "ffn_w_up")
EARLY = ("w_in", "gla_gate_w2", "mla_w_uq", "mla_w_ukv")
LATE = tuple(n for n, _ in SHARDED if n not in EARLY)
LAST = ("ffn_w_down",)
WEIGHTS = ("norm_mix", "w_in", "gla_gate_w2", "gla_gate_b", "gla_out_norm", "mla_q_a_norm", "mla_w_uq",
           "mla_kv_a_norm", "mla_w_ukv", "mla_q_norm", "mla_k_norm", "w_out", "norm_xa", "norm_mem", "xa_w_q",
           "xa_w_kv", "xa_q_norm", "xa_k_norm", "xa_w_o", "norm_ffn", "ffn_w_gate", "ffn_w_up", "ffn_conv_w",
           "ffn_conv_b", "ffn_w_down")


_NN = ((1,), (0,))
_NT = ((1,), (1,))
_TN = ((0,), (0,))


def _dg(a, b, dims):
    return lax.dot_general(a.astype(BF16), b.astype(BF16), (dims, ((), ())), preferred_element_type=F32)


@jax.custom_vjp
def _dot_nn(a, b):
    return _dg(a, b, _NN)


_dot_nn.defvjp(lambda a, b: (_dg(a, b, _NN), (a, b)),
               lambda r, g: (_dg(g, r[1], _NT).astype(r[0].dtype), _dg(r[0], g, _TN).astype(r[1].dtype)))


@jax.custom_vjp
def _dot_nt(a, b):
    return _dg(a, b, _NT)


_dot_nt.defvjp(lambda a, b: (_dg(a, b, _NT), (a, b)),
               lambda r, g: (_dg(g, r[1], _NN).astype(r[0].dtype), _dg(g, r[0], _TN).astype(r[1].dtype)))


@jax.custom_vjp
def _dot_tn(a, b):
    return _dg(a, b, _TN)


_dot_tn.defvjp(lambda a, b: (_dg(a, b, _TN), (a, b)),
               lambda r, g: (_dg(r[1], g, _NT).astype(r[0].dtype), _dg(r[0], g, _NN).astype(r[1].dtype)))


def _rms(x, w, n=None):
    n = x.shape[-1] if n is None else n
    ms = jnp.sum(x * x, axis=-1, keepdims=True) * (1.0 / n)
    return x * lax.rsqrt(ms + EPS) * w


def _silu(x):
    return x * jax.nn.sigmoid(x)


def _log_sigmoid(x):
    return jnp.minimum(x, 0.0) - jnp.log(1.0 + jnp.exp(-jnp.abs(x)))


@jax.custom_vjp
def _cumsum_rows(x):
    n = x.shape[0]
    row = lax.broadcasted_iota(jnp.int32, x.shape, 0)
    k = 1
    while k < n:
        x = x + jnp.where(row >= k, pltpu.roll(x, k, 0), 0.0)
        k *= 2
    return x


def _cumsum_rows_bwd(_, g):
    n = g.shape[0]
    row = lax.broadcasted_iota(jnp.int32, g.shape, 0)
    k = 1
    while k < n:
        g = g + jnp.where(row < n - k, pltpu.roll(g, n - k, 0), 0.0)
        k *= 2
    return (g,)


_cumsum_rows.defvjp(lambda x: (_cumsum_rows(x), None), _cumsum_rows_bwd)


def _lane_mask(lo, hi):
    lane = lax.broadcasted_iota(jnp.int32, (1, LANES), 1)
    return ((lane >= lo) & (lane < hi)).astype(F32)


def _tile(n, t):
    t = min(n, t)
    assert n % t == 0, (n, t)
    return t


class _Epilogue:
    def __init__(self, fn, rows=(), consts=(), outs=(), accs=()):
        self.fn, self.rows, self.consts, self.outs, self.accs = fn, list(rows), list(consts), list(outs), list(accs)


def _matmul(a, b, mode, out_dtype, name, residual=None, a_lead=None, b_lead=None, more=None, epilogue=None):
    (a0, a1), (b0, b1) = a.shape[-2:], b.shape[-2:]
    if mode == "nn":
        m, k, k2, n = a0, a1, b0, b1
    elif mode == "nt":
        m, k, n, k2 = a0, a1, b0, b1
    else:
        k, m, k2, n = a0, a1, b0, b1
    assert k == k2, (a.shape, b.shape, mode)
    npar = 4 if "p" in (a_lead, b_lead) else 1
    nsum = 4 if "k" in (a_lead, b_lead) else 1
    pairs = [(a, b)] + ([more] if more else [])
    a_item, b_item, o_item = a.dtype.itemsize, b.dtype.itemsize, jnp.dtype(out_dtype).itemsize
    ep = epilogue
    row_extra = 4 if residual is not None else 0
    if ep:
        row_extra += (sum(r.dtype.itemsize * wd for r, wd, _ in ep.rows) + sum(jnp.dtype(d).itemsize * wd for wd, d in ep.outs)) / n

    def resident(lead, tiles):
        return lead != "p" and tiles == 1

    def vmem_need(tm, tn, tk):
        a_bufs = 1 if resident(a_lead, (m // tm) * (k // tk)) else 2
        b_bufs = 1 if resident(b_lead, (n // tn) * (k // tk)) else 2
        need = a_bufs * (nsum if a_lead == "k" else 1) * tm * tk * a_item + b_bufs * (nsum if b_lead == "k" else 1) * tk * tn * b_item
        need *= len(pairs)
        need += (0 if ep else 2 * tm * tn * o_item) + tm * tn * 4 * (2 if tk < k else 1)
        need += tm * tk * 2 * (a_item == 4 or mode == "tn") + tk * tn * 2 * (b_item == 4)
        return need + int(2 * tm * tn * row_extra) + (3 * tm * tn * 4 if ep else 0)

    halvings = (4096, 2048, 1024, 512, 256, 128, 64, 32, 16, 8)
    if mode == "tn":
        tm = m if m <= 2304 else m // 2
        tn = n if tm * n <= 1024 * 2304 else n // 2
        tk = next((r for r in halvings if k % r == 0 and vmem_need(tm, tn, r) <= MATMUL_VMEM), k)
    else:
        tn, tk = n, k
        tm = next((r for r in halvings if m % r == 0 and vmem_need(r, tn, tk) <= MATMUL_VMEM), m)
    assert m % tm == 0 and n % tn == 0 and k % tk == 0
    assert ep is None or (tn == n and tk == k and npar == 1)
    nk = k // tk
    dims = {"nn": _NN, "nt": _NT, "tn": _TN}[mode]
    n_in = 2 * len(pairs) + (residual is not None)
    n_ep_in = len(ep.rows) + len(ep.consts) if ep else 0
    n_out = len(ep.outs) + len(ep.accs) if ep else 1

    def body(*refs):
        ab, rs, ep_in, outs, scratch = _split_refs(refs, (2 * len(pairs), n_in - 2 * len(pairs), n_ep_in, n_out, nk > 1))
        prod = None
        for a_ref, b_ref in zip(ab[0::2], ab[1::2]):
            for sh in range(nsum):
                term = _dg(a_ref[sh] if a_lead == "k" else a_ref[...], b_ref[sh] if b_lead == "k" else b_ref[...], dims)
                prod = term if prod is None else prod + term

        def finish(r):
            if rs:
                r = r + rs[0][...]
            if ep is None:
                outs[0][...] = r.astype(outs[0].dtype)
                return
            vals = [x[...] for x in ep_in]
            ro, ao = ep.fn(r, vals[:len(ep.rows)], vals[len(ep.rows):])
            for ref, val in zip(outs[:len(ep.outs)], ro, strict=True):
                ref[...] = val.astype(ref.dtype)
            if ep.accs:
                @pl.when(pl.program_id(0) == 0)
                def _():
                    for ref in outs[len(ep.outs):]:
                        ref[...] = jnp.zeros_like(ref)

                for ref, val in zip(outs[len(ep.outs):], ao, strict=True):
                    ref[...] += val

        if nk == 1:
            finish(prod)
            return
        acc = scratch[0]
        kk = pl.program_id(3)

        @pl.when(kk == 0)
        def _():
            acc[...] = prod

        @pl.when(kk > 0)
        def _():
            acc[...] += prod

        @pl.when(kk == nk - 1)
        def _():
            finish(acc[...])

    def spec(lead, blk, idx, tiles=0):
        mode = {"pipeline_mode": pl.Buffered(1)} if resident(lead, tiles) else {}
        if lead is None:
            return pl.BlockSpec(blk, lambda i, j, p, kk: idx(i, j, kk), **mode)
        if lead == "p":
            return pl.BlockSpec((None,) + blk, lambda i, j, p, kk: (p,) + idx(i, j, kk))
        return pl.BlockSpec((nsum,) + blk, lambda i, j, p, kk: (0,) + idx(i, j, kk), **mode)

    a_tiles, b_tiles = (m // tm) * nk, (n // tn) * nk
    if mode == "nn":
        pair_specs = [spec(a_lead, (tm, tk), lambda i, j, kk: (i, kk), a_tiles),
                      spec(b_lead, (tk, tn), lambda i, j, kk: (kk, j), b_tiles)]
    elif mode == "nt":
        pair_specs = [spec(a_lead, (tm, tk), lambda i, j, kk: (i, kk), a_tiles),
                      spec(b_lead, (tn, tk), lambda i, j, kk: (j, kk), b_tiles)]
    else:
        pair_specs = [spec(a_lead, (tk, tm), lambda i, j, kk: (kk, i), a_tiles),
                      spec(b_lead, (tk, tn), lambda i, j, kk: (kk, j), b_tiles)]
    tile = spec(None, (tm, tn), lambda i, j, kk: (i, j))
    in_specs = pair_specs * len(pairs)
    args = [x for pair in pairs for x in pair]
    if residual is not None:
        assert npar == 1
        in_specs.append(tile)
        args.append(residual)
    if ep:
        in_specs += [pl.BlockSpec((tm, wd), functools.partial(lambda cb, i, j, p, kk: (i, cb), cb)) for _, wd, cb in ep.rows]
        in_specs += [pl.BlockSpec(c.shape, lambda i, j, p, kk: (0, 0)) for c in ep.consts]
        args += [r for r, _, _ in ep.rows] + ep.consts
        out_specs = [pl.BlockSpec((tm, wd), lambda i, j, p, kk: (i, 0)) for wd, _ in ep.outs]
        out_specs += [pl.BlockSpec(shape, lambda i, j, p, kk: (0, 0)) for shape in ep.accs]
        out_shape = [jax.ShapeDtypeStruct((m, wd), d) for wd, d in ep.outs] + [jax.ShapeDtypeStruct(sh, F32) for sh in ep.accs]
    else:
        out_specs = spec("p" if npar > 1 else None, (tm, tn), lambda i, j, kk: (i, j))
        out_shape = jax.ShapeDtypeStruct(((4,) if npar > 1 else ()) + (m, n), out_dtype)
    outer = "arbitrary" if ep and ep.accs else "parallel"
    return pl.pallas_call(
        body, grid=(m // tm, n // tn, npar, nk), in_specs=in_specs, out_specs=out_specs, out_shape=out_shape,
        scratch_shapes=[pltpu.VMEM((tm, tn), F32)] if nk > 1 else [],
        compiler_params=pltpu.CompilerParams(dimension_semantics=(outer, outer, outer, "arbitrary"),
                                             vmem_limit_bytes=VMEM_LIMIT),
        name=name)(*args)


def _row(a, width=None, col_block=0):
    return (a, a.shape[1] if width is None else width, col_block)


def _rows_call(body, rows, consts, outs, accs=(), *, name, tile=ROW_TILE):
    s = rows[0][0].shape[0]
    t = _tile(s, tile)
    nr, nc, no = len(rows), len(consts), len(outs)

    def kern(*refs):
        r = [x[...] for x in refs[:nr]]
        c = [x[...] for x in refs[nr:nr + nc]]
        o_refs = refs[nr + nc:nr + nc + no]
        a_refs = refs[nr + nc + no:]
        ro, ao = body(r, c)
        for ref, val in zip(o_refs, ro, strict=True):
            ref[...] = val.astype(ref.dtype)
        if a_refs:
            @pl.when(pl.program_id(0) == 0)
            def _():
                for ref in a_refs:
                    ref[...] = jnp.zeros_like(ref)

            for ref, val in zip(a_refs, ao, strict=True):
                ref[...] += val

    in_specs = [pl.BlockSpec((t, w), functools.partial(lambda cb, i: (i, cb), cb)) for (_, w, cb) in rows]
    in_specs += [pl.BlockSpec(c.shape, lambda i: (0, 0)) for c in consts]
    out_specs = [pl.BlockSpec((t, w), lambda i: (i, 0)) for (w, _) in outs]
    out_specs += [pl.BlockSpec(shape, lambda i: (0, 0)) for shape in accs]
    out_shape = [jax.ShapeDtypeStruct((s, w), dt) for (w, dt) in outs]
    out_shape += [jax.ShapeDtypeStruct(shape, F32) for shape in accs]
    return pl.pallas_call(
        kern, grid=(s // t,), in_specs=in_specs, out_specs=out_specs, out_shape=out_shape,
        compiler_params=pltpu.CompilerParams(dimension_semantics=("arbitrary" if accs else "parallel",),
                                             vmem_limit_bytes=VMEM_LIMIT),
        name=name)(*[r[0] for r in rows], *consts)


def _gla_chunk(q, k, la, v0, v1, s0, s1):
    c = q.shape[0]
    r = lax.broadcasted_iota(jnp.int32, (c, c), 0)
    cc = lax.broadcasted_iota(jnp.int32, (c, c), 1)
    tril = cc <= r
    cum = _cumsum_rows(la)
    cl = jnp.sum(la, axis=0, keepdims=True)
    qd = q * (GLA_DK ** -0.5) * jnp.exp(cum)
    ki = k * jnp.exp(-cum)
    ke = k * jnp.exp(cl - cum)
    dec = jnp.exp(cl)
    outs, news = [], []
    for h, (v, s) in enumerate(((v0, s0), (v1, s1))):
        mk = _lane_mask(GLA_DK * h, GLA_DK * (h + 1))
        qh = qd * mk
        att = jnp.where(tril, _dot_nt(qh, ki), 0.0)
        outs.append(_dot_nn(att, v) + _dot_nt(qh, s))
        news.append(s * dec + _dot_tn(v, ke * mk))
    return outs[0], outs[1], news[0], news[1]


def _gla_specs(tb, rev_nb=None):
    blk = (lambda b: b) if rev_nb is None else (lambda b: rev_nb - 1 - b)
    q = pl.BlockSpec((tb, 128), lambda p, b: (blk(b), P_GQ // 128 + p))
    k = pl.BlockSpec((tb, 128), lambda p, b: (blk(b), P_GK // 128 + p))
    la = pl.BlockSpec((tb, 128), lambda p, b: (blk(b), p))
    v = pl.BlockSpec((tb, 256), lambda p, b: (blk(b), P_GV // 256 + p))
    o = pl.BlockSpec((tb, 256), lambda p, b: (blk(b), p))
    st = pl.BlockSpec((tb // GLA_CHUNK, 2, 128, 128), lambda p, b: (blk(b), p, 0, 0))
    return q, k, la, v, o, st


def _gla_fwd(proj, la):
    s = proj.shape[0]
    tb = _tile(s, ROW_TILE)
    nb, nch = s // tb, tb // GLA_CHUNK

    def kern(q_ref, k_ref, la_ref, v_ref, o_ref, st_ref, s_sc):
        @pl.when(pl.program_id(1) == 0)
        def _():
            s_sc[...] = jnp.zeros_like(s_sc)

        s0, s1 = s_sc[0], s_sc[1]
        for ci in range(nch):
            sl = slice(ci * GLA_CHUNK, (ci + 1) * GLA_CHUNK)
            st_ref[ci, 0] = s0
            st_ref[ci, 1] = s1
            o0, o1, s0, s1 = _gla_chunk(q_ref[sl, :], k_ref[sl, :], la_ref[sl, :], v_ref[sl, 0:128],
                                        v_ref[sl, 128:256], s0, s1)
            o_ref[sl, 0:128] = o0
            o_ref[sl, 128:256] = o1
        s_sc[0] = s0
        s_sc[1] = s1

    q, k, lasp, v, o, st = _gla_specs(tb)
    return pl.pallas_call(
        kern, grid=(2, nb), in_specs=[q, k, lasp, v], out_specs=[o, st],
        out_shape=[jax.ShapeDtypeStruct((s, 512), F32),
                   jax.ShapeDtypeStruct((s // GLA_CHUNK, GLA_HEADS, 128, 128), F32)],
        scratch_shapes=[pltpu.VMEM((2, 128, 128), F32)],
        compiler_params=pltpu.CompilerParams(dimension_semantics=("parallel", "arbitrary"),
                                             vmem_limit_bytes=VMEM_LIMIT),
        name="gla_fwd")(proj, proj, la, proj)


def _gla_bwd(proj, la, states, d_o):
    s = proj.shape[0]
    tb = _tile(s, ROW_TILE)
    nb, nch = s // tb, tb // GLA_CHUNK

    def kern(q_ref, k_ref, la_ref, v_ref, do_ref, st_ref, dq_ref, dk_ref, dla_ref, dv_ref, ds_sc):
        @pl.when(pl.program_id(1) == 0)
        def _():
            ds_sc[...] = jnp.zeros_like(ds_sc)

        d0, d1 = ds_sc[0], ds_sc[1]
        for ci in reversed(range(nch)):
            sl = slice(ci * GLA_CHUNK, (ci + 1) * GLA_CHUNK)
            _, vjp = jax.vjp(_gla_chunk, q_ref[sl, :], k_ref[sl, :], la_ref[sl, :], v_ref[sl, 0:128],
                             v_ref[sl, 128:256], st_ref[ci, 0], st_ref[ci, 1])
            gq, gk, gla, gv0, gv1, d0, d1 = vjp((do_ref[sl, 0:128], do_ref[sl, 128:256], d0, d1))
            dq_ref[sl, :] = gq
            dk_ref[sl, :] = gk
            dla_ref[sl, :] = gla
            dv_ref[sl, 0:128] = gv0
            dv_ref[sl, 128:256] = gv1
        ds_sc[0] = d0
        ds_sc[1] = d1

    q, k, lasp, v, o, st = _gla_specs(tb, rev_nb=nb)
    return pl.pallas_call(
        kern, grid=(2, nb), in_specs=[q, k, lasp, v, o, st], out_specs=[lasp, lasp, lasp, o],
        out_shape=[jax.ShapeDtypeStruct((s, 256), F32), jax.ShapeDtypeStruct((s, 256), F32),
                   jax.ShapeDtypeStruct((s, 256), F32), jax.ShapeDtypeStruct((s, 512), F32)],
        scratch_shapes=[pltpu.VMEM((2, 128, 128), F32)],
        compiler_params=pltpu.CompilerParams(dimension_semantics=("parallel", "arbitrary"),
                                             vmem_limit_bytes=VMEM_LIMIT),
        name="gla_bwd")(proj, proj, la, proj, d_o, states)


def _causal_keep(t):
    return lax.broadcasted_iota(jnp.int32, (t, t), 1) <= lax.broadcasted_iota(jnp.int32, (t, t), 0)


def _split_refs(refs, counts):
    out, off = [], 0
    for cnt in counts:
        out.append(refs[off:off + cnt])
        off += cnt
    return out


def _causal_blocks(n, key_major):
    pairs = ([(ki, qi) for ki in range(n) for qi in range(ki, n)] if key_major else
             [(ki, qi) for qi in range(n) for ki in range(qi + 1)])
    return np.array([ki for ki, _ in pairs], np.int32), np.array([qi for _, qi in pairs], np.int32)


def _attn_fwd(q, k, v, comm, tile=1024):
    s = q.shape[0]
    t = _tile(s, tile)
    n = s // t
    nci, nco = len(comm.ins), len(comm.out_shape)

    ki_tab, qi_tab = _causal_blocks(n, key_major=False)
    steps = len(ki_tab)

    def kern(ki_ref, qi_ref, *refs):
        (q_ref, k_ref, v_ref), cins, (o_ref, lse_ref), couts, (m_sc, l_sc, acc_sc), csems = _split_refs(
            refs, (3, nci, 2, nco, 3, len(comm.sems)))
        pair, step = pl.program_id(0), pl.program_id(1)
        qi, ki = qi_ref[step], ki_ref[step]
        place = _place()

        @pl.when((pair == 0) & (step == 0))
        def _():
            comm.start(place, cins, couts, csems)

        @pl.when((pair == MLA_HEADS // 2 - 1) & (step == 0))
        def _():
            comm.mid(place, cins, couts, csems)

        first = lax.broadcasted_iota(jnp.int32, (t, LANES), 1) < MLA_V

        @pl.when(ki == 0)
        def _():
            m_sc[...] = jnp.full_like(m_sc, -jnp.inf)
            l_sc[...] = jnp.zeros_like(l_sc)
            acc_sc[...] = jnp.zeros_like(acc_sc)

        def update(rows, cols, masked):
            nr = rows.stop - rows.start
            sel = first[:nr]
            alphas, pvs = [], []
            for h in range(2):
                sc = _dg(q_ref[rows, 128 * h:128 * (h + 1)], k_ref[cols, 128 * h:128 * (h + 1)], _NT)
                if masked:
                    sc = jnp.where(_causal_keep(nr), sc, -jnp.inf)
                m_prev = m_sc[h, rows]
                m_new = jnp.maximum(m_prev, jnp.max(sc, axis=1, keepdims=True))
                alpha = jnp.exp2(m_prev - m_new)
                p = jnp.exp2(sc - m_new[:, 0:1])
                l_sc[h, rows] = alpha * l_sc[h, rows] + jnp.sum(p, axis=1, keepdims=True)
                m_sc[h, rows] = m_new
                alphas.append(alpha)
                pvs.append(_dg(p, v_ref[cols, :], _NN))
            acc_sc[rows] = acc_sc[rows] * jnp.where(sel, alphas[0], alphas[1]) + jnp.where(sel, pvs[0], pvs[1])

        halves = [slice(0, t)] if t % 256 else [slice(0, t // 2), slice(t // 2, t)]

        @pl.when(ki < qi)
        def _():
            for rows in halves:
                for cols in halves:
                    update(rows, cols, False)

        @pl.when(ki == qi)
        def _():
            for i, rows in enumerate(halves):
                for j, cols in enumerate(halves[:i + 1]):
                    update(rows, cols, i == j)

        @pl.when(ki == qi)
        def _():
            l = jnp.where(first, l_sc[0], l_sc[1])
            m = jnp.where(first, m_sc[0], m_sc[1])
            o_ref[...] = acc_sc[...] / l
            lse_ref[...] = m + jnp.log2(l)

        @pl.when((pair == MLA_HEADS // 2 - 1) & (step == steps - 1))
        def _():
            comm.finish(place, cins, couts, csems)

    q_idx = lambda p, st, ki_r, qi_r: (qi_r[st], p)
    k_idx = lambda p, st, ki_r, qi_r: (ki_r[st], p)
    res = pl.pallas_call(
        kern, grid_spec=pltpu.PrefetchScalarGridSpec(
            num_scalar_prefetch=2, grid=(MLA_HEADS // 2, steps),
            in_specs=[pl.BlockSpec((t, 256), q_idx), pl.BlockSpec((t, 256), k_idx), pl.BlockSpec((t, 128), k_idx)]
            + [ANY] * nci,
            out_specs=[pl.BlockSpec((t, 128), q_idx), pl.BlockSpec((t, 128), q_idx)] + [ANY] * nco,
            scratch_shapes=[pltpu.VMEM((2, t, LANES), F32), pltpu.VMEM((2, t, LANES), F32),
                            pltpu.VMEM((t, LANES), F32)] + comm.sems),
        out_shape=[jax.ShapeDtypeStruct((s, 512), F32), jax.ShapeDtypeStruct((s, 512), F32)] + comm.out_shape,
        compiler_params=pltpu.CompilerParams(dimension_semantics=("arbitrary", "arbitrary"),
                                             vmem_limit_bytes=VMEM_LIMIT),
        name="mla_attn_fwd")(ki_tab, qi_tab, q, k, v, *comm.ins)
    return res[0], res[1], res[2:]


def _attn_bwd(q, k, v, o, lse, d_o, comm, tile=1024):
    s = q.shape[0]
    t = _tile(s, tile)
    n = s // t
    nci, nco = len(comm.ins), len(comm.out_shape)

    ki_tab, qi_tab = _causal_blocks(n, key_major=True)
    steps = len(ki_tab)

    def kern(ki_ref, qi_ref, *refs):
        (q_ref, k_ref, v_ref, o_ref, lse_ref, do_ref), cins, (dq_ref, dk_ref, dv_ref), couts, (dk_sc, dv_sc), csems = \
            _split_refs(refs, (6, nci, 3, nco, 2, len(comm.sems)))
        pair, step = pl.program_id(0), pl.program_id(1)
        ki, qi = ki_ref[step], qi_ref[step]
        place = _place()

        @pl.when((pair == 0) & (step == 0))
        def _():
            comm.start(place, cins, couts, csems)

        @pl.when((pair == MLA_HEADS // 2 - 1) & (step == 0))
        def _():
            comm.mid(place, cins, couts, csems)

        @pl.when((ki == 0) & (qi == 0))
        def _():
            dq_ref[...] = jnp.zeros_like(dq_ref)

        @pl.when(qi == ki)
        def _():
            dk_sc[...] = jnp.zeros_like(dk_sc)
            dv_sc[...] = jnp.zeros_like(dv_sc)

        def update(rows, cols, masked):
            nr = rows.stop - rows.start
            d_o = do_ref[rows, :]
            prod = d_o * o_ref[rows, :]
            dq_rows = pl.ds(pl.multiple_of(qi * t + rows.start, nr), nr)
            for h in range(2):
                hs = slice(128 * h, 128 * (h + 1))
                mk = _lane_mask(MLA_V * h, MLA_V * (h + 1))
                qh, kh = q_ref[rows, hs], k_ref[cols, hs]
                sc = _dg(qh, kh, _NT)
                if masked:
                    sc = jnp.where(_causal_keep(nr), sc, -jnp.inf)
                p = jnp.exp2(sc - lse_ref[rows, MLA_V * h:MLA_V * h + 1])
                doh = d_o * mk
                dp = _dg(doh * LN2, v_ref[cols, :], _NT)
                delta = jnp.sum(prod * mk, axis=1, keepdims=True) * LN2
                ds = p * (dp - delta)
                dv_sc[cols, :] += _dg(p, doh, _TN)
                dk_sc[cols, hs] += _dg(ds, qh, _TN)
                dq_ref[dq_rows, hs] += _dg(ds, kh, _NN)

        halves = [slice(0, t)] if t % 256 else [slice(0, t // 2), slice(t // 2, t)]

        @pl.when(qi > ki)
        def _():
            for rows in halves:
                for cols in halves:
                    update(rows, cols, False)

        @pl.when(qi == ki)
        def _():
            for i, rows in enumerate(halves):
                for j, cols in enumerate(halves[:i + 1]):
                    update(rows, cols, i == j)

        @pl.when(qi == n - 1)
        def _():
            dk_ref[...] = dk_sc[...]
            dv_ref[...] = dv_sc[...].astype(dv_ref.dtype)

        @pl.when((pair == MLA_HEADS // 2 - 1) & (step == steps - 1))
        def _():
            comm.finish(place, cins, couts, csems)

    q_idx = lambda p, st, ki_r, qi_r: (qi_r[st], p)
    k_idx = lambda p, st, ki_r, qi_r: (ki_r[st], p)
    res = pl.pallas_call(
        kern, grid_spec=pltpu.PrefetchScalarGridSpec(
            num_scalar_prefetch=2, grid=(MLA_HEADS // 2, steps),
            in_specs=[pl.BlockSpec((t, 256), q_idx), pl.BlockSpec((t, 256), k_idx), pl.BlockSpec((t, 128), k_idx),
                      pl.BlockSpec((t, 128), q_idx), pl.BlockSpec((t, 128), q_idx), pl.BlockSpec((t, 128), q_idx)]
            + [ANY] * nci,
            out_specs=[pl.BlockSpec((s, 256), lambda p, st, ki_r, qi_r: (0, p)), pl.BlockSpec((t, 256), k_idx),
                       pl.BlockSpec((t, 128), k_idx)] + [ANY] * nco,
            scratch_shapes=[pltpu.VMEM((t, 256), F32), pltpu.VMEM((t, 128), F32)] + comm.sems),
        out_shape=[jax.ShapeDtypeStruct((s, 1024), F32), jax.ShapeDtypeStruct((s, 1024), F32),
                   jax.ShapeDtypeStruct((s, 512), BF16)] + comm.out_shape,
        compiler_params=pltpu.CompilerParams(dimension_semantics=("arbitrary", "arbitrary"),
                                             vmem_limit_bytes=VMEM_LIMIT),
        name="mla_attn_bwd")(ki_tab, qi_tab, q, k, v, o, lse, d_o, *comm.ins)
    return res[0], res[1], res[2], res[3:]


def _gate_fn(alr, w2, b):
    return _log_sigmoid(_dot_nn(alr, w2) + b) * (1.0 / GLA_GATE_NORM)


def _make_norm_rope(scale):
    def forward(x, w, c, sa, sb):
        r = lax.rsqrt(jnp.sum(x * x, axis=-1, keepdims=True) * (1.0 / MLA_QK) + EPS)
        y = x * r * w
        out = y * c + pltpu.roll(y, LANES - 16, 1) * sa + pltpu.roll(y, 16, 1) * sb
        return (out if scale == 1.0 else out * scale), r

    @jax.custom_vjp
    def norm_rope(x, w, c, sa, sb):
        return forward(x, w, c, sa, sb)[0]

    def fwd(x, w, c, sa, sb):
        out, r = forward(x, w, c, sa, sb)
        return out, (x, w, c, sa, sb, r)

    def bwd(res, g):
        x, w, c, sa, sb, r = res
        if scale != 1.0:
            g = g * scale
        gy = g * c + pltpu.roll(g * sa, 16, 1) + pltpu.roll(g * sb, LANES - 16, 1)
        xr = x * r
        t = gy * w
        m = jnp.sum(t * xr, axis=-1, keepdims=True) * (1.0 / MLA_QK)
        return r * (t - xr * m), jnp.sum(gy * xr, axis=0, keepdims=True), jnp.zeros_like(c), jnp.zeros_like(sa), jnp.zeros_like(sb)

    norm_rope.defvjp(fwd, bwd)
    return norm_rope


_q_norm_rope = _make_norm_rope(MLA_QK ** -0.5 * LOG2E)
_k_norm_rope = _make_norm_rope(1.0)


def _qk_head(qh, kh, kpe, c, sa, sb, qn, kn):
    kfull = kh + kpe * _lane_mask(MLA_NOPE, MLA_QK)
    return _q_norm_rope(qh, qn, c, sa, sb), _k_norm_rope(kfull, kn, c, sa, sb)


def _mix_head(o, og, gn):
    return _rms(o, gn) * _silu(og)


def _xa_head(xq, xk, xv, qn, kn):
    sc = _dot_nt(_rms(xq, qn), _rms(xk, kn)) * (XA_DIM ** -0.5)
    e = jnp.exp(sc - lax.stop_gradient(jnp.max(sc, axis=1, keepdims=True)))
    p = e / jnp.sum(e, axis=1, keepdims=True)
    return _dot_nn(p, xv)


def _heads(x, n):
    return [x[:, 128 * h:128 * (h + 1)] for h in range(n)]


def _cat(xs):
    return jnp.concatenate(xs, axis=1)


def _norm_fwd(x, w, name):
    return _rows_call(lambda r, c: ([_rms(r[0], c[0])], []), [_row(x)], [w], [(x.shape[1], BF16)], name=name)[0]


def _norm_fwd_epilogue(w):
    return _Epilogue(lambda h, rows, consts: ([h, _rms(h, consts[0])], []), [], [w], [(D_MODEL, F32), (D_MODEL, BF16)], [])


def _norm_bwd_epilogue(x, w, add):
    def fn(d_out, rows, consts):
        _, vjp = jax.vjp(_rms, rows[0], consts[0])
        dx, dw = vjp(d_out)
        return [dx + rows[1]], [dw]

    return _Epilogue(fn, [_row(x), _row(add)], [w], [(D_MODEL, F32)], [w.shape])


def _norm_fwd_comm(x, w, comm, name):
    s, d = x.shape
    t = _tile(s, ROW_TILE)
    n = s // t
    nci, nco = len(comm.ins), len(comm.out_shape)

    def kern(*refs):
        (x_ref, w_ref), cins, (o_ref,), couts, csems = _split_refs(refs, (2, nci, 1, nco, len(comm.sems)))
        place = _place()

        @pl.when(pl.program_id(0) == 0)
        def _():
            comm.start(place, cins, couts, csems)

        o_ref[...] = _rms(x_ref[...], w_ref[...]).astype(o_ref.dtype)

        @pl.when(pl.program_id(0) == n - 1)
        def _():
            comm.mid(place, cins, couts, csems)
            comm.finish(place, cins, couts, csems)

    tile = pl.BlockSpec((t, d), lambda i: (i, 0))
    res = pl.pallas_call(
        kern, grid=(n,), in_specs=[tile, pl.BlockSpec(w.shape, lambda i: (0, 0))] + [ANY] * nci,
        out_specs=[tile] + [ANY] * nco, out_shape=[jax.ShapeDtypeStruct((s, d), BF16)] + comm.out_shape,
        scratch_shapes=comm.sems,
        compiler_params=pltpu.CompilerParams(dimension_semantics=("arbitrary",), vmem_limit_bytes=VMEM_LIMIT),
        name=name)(x, w, *comm.ins)
    return res[0], res[1:]


def _norm_bwd(x, w, d_out, add, name):
    def body(r, c):
        _, vjp = jax.vjp(_rms, r[0], c[0])
        dx, dw = vjp(r[1])
        return [dx + r[2]], [dw]

    return _rows_call(body, [_row(x), _row(d_out), _row(add)], [w], [(x.shape[1], F32)], [w.shape], name=name)


CONV_HALO = BF16_ROWS


def _conv_specs(s, f, t):
    n8 = t // CONV_HALO
    cur = pl.BlockSpec((None, t, f), lambda j, i: (j, i, 0))
    prev = pl.BlockSpec((None, CONV_HALO, f), lambda j, i: (j, jnp.maximum(i * n8 - 1, 0), 0))
    nxt = pl.BlockSpec((None, CONV_HALO, f), lambda j, i: (j, jnp.minimum((i + 1) * n8, s // CONV_HALO - 1), 0))
    cw = pl.BlockSpec((None, 3, f), lambda j, i: (j, 0, 0))
    cb = pl.BlockSpec((None, 1, f), lambda j, i: (j, 0, 0))
    return cur, prev, nxt, cw, cb


def _conv_taps(g, prev, first):
    ext = jnp.concatenate([jnp.where(first, 0.0, prev.astype(F32)), g], axis=0)
    return pltpu.roll(ext, 1, 0)[CONV_HALO:], pltpu.roll(ext, 2, 0)[CONV_HALO:]


def _conv_fwd(gg, uu, cw, cb, comm):
    _, s, f = gg.shape
    t = _tile(s, ROW_TILE)
    nt = s // t
    nci, nco = len(comm.ins), len(comm.out_shape)

    def kern(*refs):
        (g_ref, gp_ref, u_ref, cw_ref, cb_ref), cins, (o_ref,), couts, csems = _split_refs(
            refs, (5, nci, 1, nco, len(comm.sems)))
        shard, i = pl.program_id(0), pl.program_id(1)
        place = _place()

        @pl.when((shard == 0) & (i == 0))
        def _():
            comm.start(place, cins, couts, csems)

        @pl.when((shard == 3) & (i == 0))
        def _():
            comm.mid(place, cins, couts, csems)

        g = g_ref[...].astype(F32)
        g1, g2 = _conv_taps(g, gp_ref[...], i == 0)
        w = cw_ref[...]
        gc = cb_ref[...] + w[0:1] * g2 + w[1:2] * g1 + w[2:3] * g
        o_ref[...] = (_silu(gc) * u_ref[...].astype(F32)).astype(o_ref.dtype)

        @pl.when((shard == 3) & (i == nt - 1))
        def _():
            comm.finish(place, cins, couts, csems)

    cur, prev, _, cws, cbs = _conv_specs(s, f, t)
    res = pl.pallas_call(
        kern, grid=(4, nt), in_specs=[cur, prev, cur, cws, cbs] + [ANY] * nci, out_specs=[cur] + [ANY] * nco,
        out_shape=[jax.ShapeDtypeStruct(gg.shape, BF16)] + comm.out_shape, scratch_shapes=comm.sems,
        compiler_params=pltpu.CompilerParams(dimension_semantics=("arbitrary", "arbitrary"), vmem_limit_bytes=VMEM_LIMIT),
        name="ffn_conv_fwd")(gg, gg, uu, cw, cb, *comm.ins)
    return res[0], res[1:]


def _conv_bwd(gg, uu, dact, cw, cb):
    _, s, f = gg.shape
    t = _tile(s, ROW_TILE)
    nt = s // t

    def kern(g_ref, gp_ref, gn_ref, u_ref, un_ref, da_ref, dan_ref, cw_ref, cb_ref, du_ref, dg_ref, dcw_ref, dcb_ref):
        i = pl.program_id(1)
        cat = lambda a_ref, b_ref: jnp.concatenate([a_ref[...].astype(F32), b_ref[...].astype(F32)], axis=0)
        g, u, da = cat(g_ref, gn_ref), cat(u_ref, un_ref), cat(da_ref, dan_ref)
        g1, g2 = _conv_taps(g, gp_ref[...], i == 0)
        w = cw_ref[...]
        gc = cb_ref[...] + w[0:1] * g2 + w[1:2] * g1 + w[2:3] * g
        sg = jax.nn.sigmoid(gc)
        du_ref[...] = (da[:t] * (gc[:t] * sg[:t])).astype(du_ref.dtype)
        row = lax.broadcasted_iota(jnp.int32, (t + CONV_HALO, 1), 0)
        dgc = jnp.where((row < t) | (i < nt - 1), da * u * (sg * (1.0 + gc * (1.0 - sg))), 0.0)
        up1 = pltpu.roll(dgc, t + CONV_HALO - 1, 0)[:t]
        up2 = pltpu.roll(dgc, t + CONV_HALO - 2, 0)[:t]
        dgc = dgc[:t]
        dg_ref[...] = (w[2:3] * dgc + w[1:2] * up1 + w[0:1] * up2).astype(dg_ref.dtype)

        @pl.when(i == 0)
        def _():
            dcw_ref[...] = jnp.zeros_like(dcw_ref)
            dcb_ref[...] = jnp.zeros_like(dcb_ref)

        ones = jnp.ones((8, t), BF16)
        col_sum = lambda a: _dg(ones, a, _NN)[0:1]
        dcw_ref[0:1, :] += col_sum(dgc * g2[:t])
        dcw_ref[1:2, :] += col_sum(dgc * g1[:t])
        dcw_ref[2:3, :] += col_sum(dgc * g[:t])
        dcb_ref[...] += col_sum(dgc)

    cur, prev, nxt, cws, cbs = _conv_specs(s, f, t)
    return pl.pallas_call(
        kern, grid=(4, nt), in_specs=[cur, prev, nxt, cur, nxt, cur, nxt, cws, cbs], out_specs=[cur, cur, cws, cbs],
        out_shape=[jax.ShapeDtypeStruct(gg.shape, BF16), jax.ShapeDtypeStruct(gg.shape, BF16),
                   jax.ShapeDtypeStruct(cw.shape, F32), jax.ShapeDtypeStruct(cb.shape, F32)],
        compiler_params=pltpu.CompilerParams(dimension_semantics=("parallel", "arbitrary"), vmem_limit_bytes=VMEM_LIMIT),
        name="ffn_conv_bwd")(gg, gg, gg, uu, uu, dact, dact, cw, cb)


def _rope_tables(pos):
    half = MLA_ROPE // 2
    lane = jnp.arange(LANES)
    rotary = (lane >= MLA_NOPE) & (lane < MLA_QK)
    inv = jnp.where(rotary, ROPE_THETA ** (-((lane - MLA_NOPE) % half).astype(F32) / half), 0.0)
    ang = pos.astype(F32)[:, None] * inv
    cos, sin = jnp.cos(ang), jnp.sin(ang)
    first = rotary & (lane < MLA_NOPE + half)
    return cos, jnp.where(first, -sin, 0.0), jnp.where(rotary & ~first, sin, 0.0)


def _local_step(x, mem, pos, target, rep, early_shards, late_shards):
    g = {}
    c, sa, sb = _rope_tables(pos)

    xn, gathered = _norm_fwd_comm(x, rep["norm_mix"], _gather_plan(early_shards), "norm_mix_fwd_gather")
    w = _early_layout(dict(zip(EARLY, gathered, strict=True)), rep)

    def proj_fn(r, rows, k):
        la_ = _gate_fn(r[:, P_ALR:P_ALR + 128], k[0], k[1])
        return [r, la_, _rms(r[:, P_CQ:P_CQ + MLA_Q_RANK], k[2]), _rms(r[:, P_CKV:P_CKV + MLA_KV_RANK], k[3])], []

    proj, la, q_lat, kv_lat = _matmul(
        xn, w["in"], "nt", F32, "proj_fwd", epilogue=_Epilogue(
            proj_fn, [], [w["w2"], w["gate_b"], w["q_a_norm"], w["kv_a_norm"]],
            [(P_WIDTH, F32), (256, F32), (MLA_Q_RANK, BF16), (MLA_KV_RANK, BF16)], []))
    alr = _row(proj, 128, P_ALR // 128)
    kpe = _row(proj, 128, P_KPE // 128)
    og = _row(proj, 512, P_OG // 512)
    cq = _row(proj, 256, P_CQ // 256)
    ckv = _row(proj, 128, P_CKV // 128)

    o_gla, states = _gla_fwd(proj, la)

    def qk_body(r, k):
        q_up, k_up = _dg(r[0], k[0], _NN), _dg(r[1], k[1], _NN)
        qs, ks = [], []
        for qh, kh in zip(_heads(q_up, MLA_HEADS), _heads(k_up, MLA_HEADS)):
            a, b = _qk_head(qh, kh, r[2], r[3], r[4], r[5], k[3], k[4])
            qs.append(a)
            ks.append(b)
        return [_cat(qs), _cat(ks), _dg(r[1], k[2], _NN)], []

    tabs = [_row(c), _row(sa), _row(sb)]
    qk_consts = [w["uq"], w["k"], w["v"], w["q_norm"], w["k_norm"]]
    q_r, k_r, v_mla = _rows_call(qk_body, [_row(q_lat), _row(kv_lat), kpe] + tabs, qk_consts,
                                 [(1024, BF16), (1024, BF16), (512, BF16)], name="mla_qk_fwd")
    with_attn = [n for n in LATE if n not in LAST]
    o_mla, lse, gathered = _attn_fwd(q_r, k_r, v_mla, _gather_plan([late_shards[n] for n in with_attn]))
    w.update(_late_layout(dict(zip(with_attn, gathered, strict=True))))

    def mix_body(r, k):
        ys = [_mix_head(o, g_, k[0]) for o, g_ in zip(_heads(r[0], GLA_HEADS), _heads(r[1], GLA_HEADS))]
        return [_cat(ys + [r[2]])], []

    cat = _rows_call(mix_body, [_row(o_gla), og, _row(o_mla)], [w["gla_out_norm"]], [(1024, BF16)],
                     name="mix_fwd")[0]
    h1, hn = _matmul(cat, w["out"], "nn", F32, "out_fwd_norm", residual=x, epilogue=_norm_fwd_epilogue(w["norm_xa"]))
    mn = _norm_fwd(mem, w["norm_mem"], "norm_mem_fwd")
    xkv = _matmul(mn, w["xkv"], "nn", F32, "xa_kv_fwd")

    def xa_fn(r, rows, k):
        ks, vs = _heads(k[0], 2 * XA_HEADS)[:XA_HEADS], _heads(k[0], 2 * XA_HEADS)[XA_HEADS:]
        return [r, _cat([_xa_head(a, b, v_, k[1], k[2]) for a, b, v_ in zip(_heads(r, XA_HEADS), ks, vs)])], []

    xq, xo = _matmul(hn, w["xq"], "nn", F32, "xa_q_fwd_attn", epilogue=_Epilogue(
        xa_fn, [], [xkv, w["xa_q_norm"], w["xa_k_norm"]], [(512, F32), (512, BF16)], []))
    h2, fn = _matmul(xo, w["xo"], "nn", F32, "xa_o_fwd_norm", residual=h1, epilogue=_norm_fwd_epilogue(w["norm_ffn"]))
    gg = _matmul(fn, w["wg"], "nt", BF16, "ffn_gate_fwd", b_lead="p")
    uu = _matmul(fn, w["wu"], "nt", BF16, "ffn_up_fwd", b_lead="p")
    act, gathered = _conv_fwd(gg, uu, w["cw"], w["cb"], _gather_plan([late_shards[n] for n in LAST]))
    w["wd"] = gathered[0]
    def loss_fn(y, rows, consts):
        err = y - rows[0]
        part = 0.5 * jnp.sum(jnp.sum(err * err, axis=1, keepdims=True) * (1.0 / D_MODEL), axis=0, keepdims=True)
        return [err * (1.0 / D_MODEL)], [jnp.broadcast_to(part, (1, LANES))]

    dy, loss = _matmul(act, w["wd"], "nn", F32, "ffn_down_fwd_loss", residual=h2, a_lead="k", b_lead="k",
                       epilogue=_Epilogue(loss_fn, [_row(target)], [], [(D_MODEL, F32)], [(1, LANES)]))

    g["ffn_w_down"] = _matmul(act, dy, "tn", BF16, "ffn_down_dw", a_lead="p")
    dact = _matmul(dy, w["wd"], "nt", BF16, "ffn_down_dx", b_lead="p")
    duu, dgg, g["ffn_conv_w"], g["ffn_conv_b"] = _conv_bwd(gg, uu, dact, w["cw"], w["cb"])
    g["ffn_w_gate"] = _matmul(dgg, fn, "tn", BF16, "ffn_gate_dw", a_lead="p")
    g["ffn_w_up"] = _matmul(duu, fn, "tn", BF16, "ffn_up_dw", a_lead="p")
    dh2, g["norm_ffn"] = _matmul(dgg, w["wg"], "nn", F32, "ffn_dx_norm_bwd", a_lead="k", b_lead="k", more=(duu, w["wu"]),
                                 epilogue=_norm_bwd_epilogue(h2, w["norm_ffn"], dy))

    g["xa_w_o"] = _matmul(xo, dh2, "tn", BF16, "xa_o_dw")
    def xa_bwd(dxo_, rows, k):
        kvh = _heads(k[0], 2 * XA_HEADS)
        dq_, dk_, dv_ = [], [], []
        dqn, dkn = 0.0, 0.0
        for h, (a, d_) in enumerate(zip(_heads(rows[0], XA_HEADS), _heads(dxo_, XA_HEADS))):
            _, vjp = jax.vjp(_xa_head, a, kvh[h], kvh[XA_HEADS + h], k[1], k[2])
            ga, gk, gv, gqn, gkn = vjp(d_)
            dq_.append(ga)
            dk_.append(gk)
            dv_.append(gv)
            dqn, dkn = dqn + gqn, dkn + gkn
        return [_cat(dq_)], [_cat(dk_ + dv_), dqn, dkn]

    dxq, dxkv, g["xa_q_norm"], g["xa_k_norm"] = _matmul(dh2, w["xo"], "nt", F32, "xa_o_dx_attn_bwd", epilogue=_Epilogue(
        xa_bwd, [_row(xq)], [xkv, w["xa_q_norm"], w["xa_k_norm"]], [(512, BF16)], [xkv.shape, (1, 128), (1, 128)]))
    g["xa_w_q"] = _matmul(hn, dxq, "tn", BF16, "xa_q_dw")
    dh1, g["norm_xa"] = _matmul(dxq, w["xq"], "nt", F32, "xa_q_dx_norm_bwd",
                                epilogue=_norm_bwd_epilogue(h1, w["norm_xa"], dh2))
    g["xa_w_kv"] = _matmul(mn, dxkv, "tn", BF16, "xa_kv_dw")
    dmn = _matmul(dxkv, w["xkv"], "nt", F32, "xa_kv_dx")
    _, g["norm_mem"] = _norm_bwd(mem, w["norm_mem"], dmn, dmn, "norm_mem_bwd")

    g["w_out"] = _matmul(cat, dh1, "tn", BF16, "out_dw")
    def mix_bwd(dcat_, rows, k):
        do_, dog_ = [], []
        dgn = 0.0
        for o, g_, d_ in zip(_heads(rows[0], GLA_HEADS), _heads(rows[1], GLA_HEADS), _heads(dcat_, GLA_HEADS)):
            _, vjp = jax.vjp(_mix_head, o, g_, k[0])
            a, b, gn_ = vjp(d_)
            do_.append(a)
            dog_.append(b)
            dgn = dgn + gn_
        return [_cat(do_), _cat(dog_), dcat_[:, 512:]], [dgn]

    do_gla, d_og, do_mla, g["gla_out_norm"] = _matmul(dh1, w["out"], "nt", F32, "out_dx_mix_bwd", epilogue=_Epilogue(
        mix_bwd, [_row(o_gla), og], [w["gla_out_norm"]], [(512, F32), (512, BF16), (512, F32)], [(1, 128)]))

    late_parts = _late_grad_shards(g)
    dq_r, dk_r, dv_mla, lands_late = _attn_bwd(q_r, k_r, v_mla, o_mla, lse, do_mla,
                                               _scatter_plan([late_parts[n] for n in LATE]))
    lands_late = dict(zip(LATE, lands_late, strict=True))

    def qk_bwd(r, k):
        q_up, k_up = _dg(r[0], k[0], _NN), _dg(r[1], k[1], _NN)
        dqs, dks = [], []
        dkpe, dqn, dkn = 0.0, 0.0, 0.0
        for qh, kh, dqh, dkh in zip(_heads(q_up, MLA_HEADS), _heads(k_up, MLA_HEADS), _heads(r[6], MLA_HEADS),
                                    _heads(r[7], MLA_HEADS)):
            _, vjp = jax.vjp(lambda a, b, e, f, h_: _qk_head(a, b, e, r[3], r[4], r[5], f, h_), qh, kh, r[2], k[3], k[4])
            ga, gb, ge, gf, gh = vjp((dqh, dkh))
            dqs.append(ga)
            dks.append(gb)
            dkpe, dqn, dkn = dkpe + ge, dqn + gf, dkn + gh
        dq_up, dk_up, dv = _cat(dqs), _cat(dks), r[8]
        dq_lat_ = _dg(dq_up, k[0], _NT)
        dkv_lat_ = _dg(dk_up, k[1], _NT) + _dg(dv, k[2], _NT)
        return [dq_lat_, dkv_lat_, dkpe], [dqn, dkn, _dg(r[0], dq_up, _TN), _dg(r[1], dk_up, _TN), _dg(r[1], dv, _TN)]

    dq_lat, dkv_lat, d_kpe, g["q_norm"], g["k_norm"], g["uq"], g["k"], g["v"] = _rows_call(
        qk_bwd, [_row(q_lat), _row(kv_lat), kpe] + tabs + [_row(dq_r), _row(dk_r), _row(dv_mla)], qk_consts,
        [(MLA_Q_RANK, F32), (MLA_KV_RANK, F32), (128, BF16)],
        [(1, 128), (1, 128), w["uq"].shape, w["k"].shape, w["v"].shape], name="mla_qk_bwd")

    dgq, dgk, dla, dgv = _gla_bwd(proj, la, states, do_gla)

    def dproj_body(r, k):
        alr_, cq_, ckv_, dla_, dq_lat_, dkv_lat_, dgq_, dgk_, dgv_, d_og_, d_kpe_ = r
        _, gate_vjp = jax.vjp(_gate_fn, alr_, k[0], k[1])
        d_alr, gw2, gb = gate_vjp(dla_)
        _, q_vjp = jax.vjp(_rms, cq_, k[2])
        _, kv_vjp = jax.vjp(_rms, ckv_, k[3])
        d_cq, gqa = q_vjp(dq_lat_)
        d_ckv, gkva = kv_vjp(dkv_lat_)
        pieces = [dgq_, dgk_, dgv_, d_og_, d_cq, d_ckv, d_kpe_, d_alr]
        return [_cat([x_.astype(BF16) for x_ in pieces])], [gw2, gb, gqa, gkva]

    dproj, g["w2"], g["gla_gate_b"], g["mla_q_a_norm"], g["mla_kv_a_norm"] = _rows_call(
        dproj_body, [alr, cq, ckv, _row(dla), _row(dq_lat), _row(dkv_lat), _row(dgq), _row(dgk), _row(dgv), _row(d_og),
                     _row(d_kpe)], [w["w2"], w["gate_b"], w["q_a_norm"], w["kv_a_norm"]], [(P_WIDTH, BF16)],
        [(128, 256), (1, 256), (1, 256), (1, 128)], name="proj_cotangent")
    g["in"] = _matmul(dproj, xn, "tn", BF16, "proj_dw")
    dx, g["norm_mix"] = _matmul(dproj, w["in"], "nn", F32, "proj_dx_norm_bwd",
                                epilogue=_norm_bwd_epilogue(x, w["norm_mix"], dh1))
    return loss[0, 0], dx, g, lands_late


def _join_shards(pieces, axis):
    if axis == 0:
        return pieces.reshape(-1, pieces.shape[2])
    return jnp.transpose(pieces, (1, 0, 2)).reshape(pieces.shape[1], -1)


def _split_shards(full, axis):
    r, c = full.shape
    if axis == 0:
        return full.reshape(4, r // 4, c)
    return jnp.transpose(full.reshape(r, 4, c // 4), (1, 0, 2))


def _early_layout(gath, rep):
    w_in = gath["w_in"].reshape(N_WIDTH, D_MODEL)
    z = lambda n: jnp.zeros((n, D_MODEL), w_in.dtype)
    seg = lambda lo, n: w_in[lo:lo + n]
    ukv = _join_shards(gath["mla_w_ukv"], 1).reshape(MLA_KV_RANK, MLA_HEADS, MLA_NOPE + MLA_V)
    w = {
        "in": jnp.concatenate([seg(N_GQ, 256), seg(N_GK, 256), seg(N_GV, 512), seg(N_OG, 512), seg(N_CQ, 256),
                               seg(N_CKV, 128), z(64), seg(N_KPE, 32), z(32), seg(N_ALR, 16), z(112)], axis=0),
        "uq": jnp.pad(_join_shards(gath["mla_w_uq"], 1).reshape(MLA_Q_RANK, MLA_HEADS, MLA_QK),
                      ((0, 0), (0, 0), (0, LANES - MLA_QK))).reshape(MLA_Q_RANK, MLA_HEADS * LANES),
        "k": jnp.pad(ukv[:, :, :MLA_NOPE], ((0, 0), (0, 0), (0, LANES - MLA_NOPE))).reshape(MLA_KV_RANK, -1),
        "v": ukv[:, :, MLA_NOPE:].reshape(MLA_KV_RANK, MLA_HEADS * MLA_V),
        "w2": jnp.pad(_join_shards(gath["gla_gate_w2"], 1), ((0, LANES - GLA_RANK), (0, 0))),
        "cb": rep["ffn_conv_b"].reshape(4, 1, D_FF // 4),
        "q_norm": jnp.pad(rep["mla_q_norm"], ((0, 0), (0, LANES - MLA_QK))),
        "k_norm": jnp.pad(rep["mla_k_norm"], ((0, 0), (0, LANES - MLA_QK))),
        "q_a_norm": rep["mla_q_a_norm"], "kv_a_norm": rep["mla_kv_a_norm"], "gate_b": rep["gla_gate_b"],
    }
    for n in ("norm_mix", "gla_out_norm", "norm_xa", "norm_mem", "xa_q_norm", "xa_k_norm", "norm_ffn"):
        w[n] = rep[n]
    return w


def _late_layout(gath):
    return {"out": _join_shards(gath["w_out"], 0), "xq": _join_shards(gath["xa_w_q"], 0),
            "xkv": _join_shards(gath["xa_w_kv"], 0), "xo": _join_shards(gath["xa_w_o"], 1),
            "wg": gath["ffn_w_gate"], "wu": gath["ffn_w_up"], "cw": gath["ffn_conv_w"]}


def _late_grad_shards(g):
    sh = {"w_out": _split_shards(g["w_out"], 0), "xa_w_q": _split_shards(g["xa_w_q"], 0),
          "xa_w_kv": _split_shards(g["xa_w_kv"], 0), "xa_w_o": _split_shards(g["xa_w_o"], 1),
          "ffn_w_gate": g["ffn_w_gate"], "ffn_w_up": g["ffn_w_up"], "ffn_conv_w": g["ffn_conv_w"],
          "ffn_w_down": g["ffn_w_down"]}
    return {n: v.astype(BF16) for n, v in sh.items()}


def _early_grad_shards(g):
    gi = g["in"]
    seg = lambda lo, n: gi[lo:lo + n]
    w_in = jnp.concatenate([seg(P_GQ, 256), seg(P_GK, 256), seg(P_GV, 512), seg(P_ALR, 16), seg(P_OG, 512),
                            seg(P_CQ, 256), seg(P_CKV, 128), seg(P_KPE + 64, 32)], axis=0)
    uq = g["uq"].reshape(MLA_Q_RANK, MLA_HEADS, LANES)[:, :, :MLA_QK].reshape(MLA_Q_RANK, -1)
    ukv = jnp.concatenate([g["k"].reshape(MLA_KV_RANK, MLA_HEADS, LANES)[:, :, :MLA_NOPE],
                           g["v"].reshape(MLA_KV_RANK, MLA_HEADS, MLA_V)], axis=2).reshape(MLA_KV_RANK, -1)
    sh = {"w_in": w_in.reshape(4, N_WIDTH // 4, D_MODEL), "gla_gate_w2": _split_shards(g["w2"][:GLA_RANK], 1),
          "mla_w_uq": _split_shards(uq, 1), "mla_w_ukv": _split_shards(ukv, 1)}
    sh = {n: v.astype(BF16) for n, v in sh.items()}
    rep = {n: g[n] for n in REPLICATED if n in g}
    rep["mla_q_norm"] = g["q_norm"][:, :MLA_QK]
    rep["mla_k_norm"] = g["k_norm"][:, :MLA_QK]
    rep["ffn_conv_b"] = g["ffn_conv_b"].reshape(1, D_FF)
    return sh, rep


SMALL_SHAPE = (8, 1024)


def _pack_small(vectors):
    flat = jnp.concatenate(vectors, axis=1)
    return jnp.pad(flat, ((0, 0), (0, SMALL_SHAPE[0] * SMALL_SHAPE[1] - flat.shape[1]))).reshape(SMALL_SHAPE)


def _unpack_small(buf, widths):
    flat = buf.reshape(1, -1)
    out, off = [], 0
    for wd in widths:
        out.append(flat[:, off:off + wd])
        off += wd
    return out


ANY = pl.BlockSpec(memory_space=pl.ANY)


def _place():
    x, y, c = lax.axis_index("x"), lax.axis_index("y"), lax.axis_index("c")
    chips = [(1 - x, y), (x, 1 - y), (1 - x, 1 - y)]
    return x, y, c, chips


class _Comm:
    def __init__(self, ins, out_shape, sems, start, finish, mid=None):
        self.ins, self.out_shape, self.sems = list(ins), list(out_shape), list(sems)
        self.start, self.finish, self.mid = start, finish, mid or (lambda *args: None)


def _run_comm(plan, name):
    ni, no = len(plan.ins), len(plan.out_shape)

    def body(*refs):
        ins, outs, sems = refs[:ni], refs[ni:ni + no], refs[ni + no:]
        place = _place()
        plan.start(place, ins, outs, sems)
        plan.mid(place, ins, outs, sems)
        plan.finish(place, ins, outs, sems)

    return pl.pallas_call(body, in_specs=[ANY] * ni, out_specs=[ANY] * no, out_shape=plan.out_shape,
                          scratch_shapes=plan.sems, name=name)(*plan.ins)


def _gather_plan(shards):
    n = len(shards)
    by_rows = [s.shape[0] % (2 * BF16_ROWS) == 0 for s in shards]
    by_cols = [not r and s.shape[1] % (2 * LANES) == 0 for r, s in zip(by_rows, shards)]
    split = [r or c for r, c in zip(by_rows, by_cols)]

    def rows(ref, t, c):
        if by_rows[t]:
            half = shards[t].shape[0] // 2
            return ref.at[pl.ds(pl.multiple_of(c * half, BF16_ROWS), half)]
        if by_cols[t]:
            half = shards[t].shape[1] // 2
            return ref.at[:, pl.ds(pl.multiple_of(c * half, LANES), half)]
        return ref

    def remote(src, dst, ss, rs, to):
        return pltpu.make_async_remote_copy(src_ref=src, dst_ref=dst, send_sem=ss, recv_sem=rs, device_id=to,
                                            device_id_type=MESH)

    def first_wave(place, ins, outs, sems):
        x, y, c, chips = place
        ici_s, ici_r, _, _, local = sems
        me = 2 * x + y
        own = [pltpu.make_async_copy(ins[t], outs[t].at[me], local.at[t]) for t in range(n)]
        push = [remote(rows(ins[t], t, c), rows(outs[t].at[me], t, c), ici_s.at[3 * t + j], ici_r.at[3 * t + j], (px, py, c))
                for t in range(n) for j, (px, py) in enumerate(chips)]
        return own, push

    def second_wave(place, ins, outs, sems, last):
        x, y, c, chips = place
        ici_s, ici_r, d2d_s, d2d_r, local = sems
        sib = (x, y, 1 - c)
        out = []
        for t in range(n):
            for j, (px, py) in enumerate(chips):
                block = outs[t].at[2 * px + py]
                got = rows(block, t, c)
                if split[t]:
                    hand = remote(got, got, d2d_s.at[3 * t + j], d2d_r.at[3 * t + j], sib)
                    theirs = rows(block, t, 1 - c)
                    other = (remote(theirs, theirs, local.at[0], d2d_r.at[3 * t + j], sib) if last else
                             remote(got, got, local.at[0], ici_r.at[3 * t + j], sib))
                    out.append((other, hand))
                elif last:
                    out.append((remote(got, got, local.at[0], ici_r.at[3 * t + j], sib), None))
        return out

    def start(place, ins, outs, sems):
        own, push = first_wave(place, ins, outs, sems)
        for cp in own + push:
            cp.start()

    def mid(place, ins, outs, sems):
        for arrival, hand in second_wave(place, ins, outs, sems, False):
            arrival.wait_recv()
            hand.start()

    def finish(place, ins, outs, sems):
        own, push = first_wave(place, ins, outs, sems)
        for arrival, hand in second_wave(place, ins, outs, sems, True):
            arrival.wait_recv()
            if hand is not None:
                hand.wait_send()
        for cp in push:
            cp.wait_send()
        for cp in own:
            cp.wait()

    dma = pltpu.SemaphoreType.DMA
    return _Comm(shards, [jax.ShapeDtypeStruct((4,) + s.shape, s.dtype) for s in shards],
                 [dma((3 * n,)), dma((3 * n,)), dma((3 * n,)), dma((3 * n,)), dma((n,))], start, finish, mid)


def _scatter_plan(parts, small=None):
    n = len(parts)
    ns = 0 if small is None else 1

    def unpack(place, ins, outs, sems):
        x, y, c, chips = place
        return x, y, c, chips, 2 * x + y, 4 * x + 2 * y + c, (x, y, 1 - c)

    def remote(src, dst, ss, rs, to):
        return pltpu.make_async_remote_copy(src_ref=src, dst_ref=dst, send_sem=ss, recv_sem=rs, device_id=to,
                                            device_id_type=MESH)

    def first_wave(place, ins, outs, sems):
        x, y, c, chips, me, dev, sib = unpack(place, ins, outs, sems)
        ici_s, ici_r, d2d_s, d2d_r, sm_s, sm_r, local = sems
        own, push = [], []
        if ns:
            own.append(pltpu.make_async_copy(ins[n], outs[n].at[dev], local.at[n]))
            for k in range(1, 8):
                px = (1 - x) if (k >> 2) & 1 else x
                py = (1 - y) if (k >> 1) & 1 else y
                pc = (1 - c) if k & 1 else c
                push.append(remote(ins[n], outs[n].at[dev], sm_s.at[k - 1], sm_r.at[k - 1], (px, py, pc)))
        for t in range(n):
            own.append(pltpu.make_async_copy(ins[t].at[me], outs[t].at[dev], local.at[t]))
            push.append(remote(ins[t].at[me], outs[t].at[dev], d2d_s.at[4 * t], d2d_r.at[4 * t], sib))
            for j, (px, py) in enumerate(chips):
                push.append(remote(ins[t].at[2 * px + py], outs[t].at[dev], ici_s.at[3 * t + j], ici_r.at[3 * t + j],
                                   (px, py, c)))
        return own, push

    def start(place, ins, outs, sems):
        own, push = first_wave(place, ins, outs, sems)
        for cp in own + push:
            cp.start()

    def landed(dst, rs, sems, sib):
        remote(dst, dst, sems[-1].at[0], rs, sib).wait_recv()

    def forwards(place, ins, outs, sems):
        x, y, c, chips, me, dev, sib = unpack(place, ins, outs, sems)
        d2d_s, d2d_r = sems[2], sems[3]
        slots = [(t, j, outs[t].at[4 * px + 2 * py + c]) for t in range(n) for j, (px, py) in enumerate(chips)]
        return [(t, j, slot, remote(slot, slot, d2d_s.at[4 * t + 1 + j], d2d_r.at[4 * t + 1 + j], sib))
                for t, j, slot in slots]

    def mid(place, ins, outs, sems):
        sib = unpack(place, ins, outs, sems)[-1]
        for t, j, slot, cp in forwards(place, ins, outs, sems):
            landed(slot, sems[1].at[3 * t + j], sems, sib)
            cp.start()

    def finish(place, ins, outs, sems):
        x, y, c, chips, me, dev, sib = unpack(place, ins, outs, sems)
        d2d_r, sm_r = sems[3], sems[5]
        own, push = first_wave(place, ins, outs, sems)
        push += [cp for _, _, _, cp in forwards(place, ins, outs, sems)]
        for t in range(n):
            landed(outs[t].at[4 * x + 2 * y + (1 - c)], d2d_r.at[4 * t], sems, sib)
            for j, (px, py) in enumerate(chips):
                landed(outs[t].at[4 * px + 2 * py + (1 - c)], d2d_r.at[4 * t + 1 + j], sems, sib)
        if ns:
            for k in range(1, 8):
                px = (1 - x) if (k >> 2) & 1 else x
                py = (1 - y) if (k >> 1) & 1 else y
                pc = (1 - c) if k & 1 else c
                landed(outs[n].at[4 * px + 2 * py + pc], sm_r.at[k - 1], sems, sib)
        for cp in push:
            cp.wait_send()
        for cp in own:
            cp.wait()

    dma = pltpu.SemaphoreType.DMA
    ins = list(parts) + ([small] if ns else [])
    out_shape = [jax.ShapeDtypeStruct((8,) + p.shape[1:], p.dtype) for p in parts]
    if ns:
        out_shape.append(jax.ShapeDtypeStruct((8,) + small.shape, small.dtype))
    return _Comm(ins, out_shape, [dma((3 * n,)), dma((3 * n,)), dma((4 * n,)), dma((4 * n,)), dma((7,)), dma((7,)),
                                  dma((n + 1,))], start, finish, mid)


ADAM_ROWS = 288


def _row_tile(r, cap):
    if r <= cap:
        return r
    return max((t for t in range(8, cap + 1, 8) if r % t == 0), default=r)


def _adamw_update(w, m, v, land):
    g = land[0].astype(F32)
    for i in range(1, 8):
        g = g + land[i].astype(F32)
    m_new = ADAM_B1 * m + (1.0 - ADAM_B1) * g
    v_new = ADAM_B2 * v + (1.0 - ADAM_B2) * (g * g)
    m_hat = m_new / (1.0 - ADAM_B1 ** ADAM_STEP)
    v_hat = v_new / (1.0 - ADAM_B2 ** ADAM_STEP)
    return g, -ADAM_LR * (m_hat / (jnp.sqrt(v_hat) + ADAM_EPS) + ADAM_WD * w), m_new, v_new


def _adamw(tensors, name, comm=None):
    k = len(tensors)
    r, c = tensors[0][0].shape
    t = _row_tile(r, ADAM_ROWS // k)
    tc = c if t < r or r <= ADAM_ROWS else 2 * LANES
    n = (r // t) * (c // tc)
    nci, nco, nsem = (len(comm.ins), len(comm.out_shape), len(comm.sems)) if comm else (0, 0, 0)

    def kern(*refs):
        ins, cins, outs, couts, csems = _split_refs(refs, (4 * k, nci, 4 * k, nco, nsem))
        if comm:
            place = _place()

            @pl.when(pl.program_id(0) == 0)
            def _():
                comm.start(place, cins, couts, csems)

        for i in range(k):
            w_ref, m_ref, v_ref, l_ref = ins[4 * i:4 * i + 4]
            res = _adamw_update(w_ref[...], m_ref[...], v_ref[...], l_ref)
            for ref, val in zip(outs[4 * i:4 * i + 4], res, strict=True):
                ref[...] = val
        if comm:
            @pl.when(pl.program_id(0) == n - 1)
            def _():
                comm.mid(place, cins, couts, csems)
                comm.finish(place, cins, couts, csems)

    where = (lambda i: (i, 0)) if tc == c else (lambda i: (0, i))
    spec = pl.BlockSpec((t, tc), where)
    lspec = pl.BlockSpec((8, t, tc), lambda i: (0,) + where(i))
    res = pl.pallas_call(
        kern, grid=(n,), in_specs=[spec, spec, spec, lspec] * k + [ANY] * nci, out_specs=[spec] * (4 * k) + [ANY] * nco,
        out_shape=[jax.ShapeDtypeStruct((r, c), F32)] * (4 * k) + (comm.out_shape if comm else []),
        scratch_shapes=comm.sems if comm else [],
        compiler_params=pltpu.CompilerParams(dimension_semantics=("arbitrary" if comm else "parallel",),
                                             vmem_limit_bytes=VMEM_LIMIT),
        name=name)(*[x for tens in tensors for x in tens], *(comm.ins if comm else []))
    return [res[4 * i:4 * i + 4] for i in range(k)], res[4 * k:]


def _step(a):
    def sq(n):
        v = a[n][0] if a[n].ndim == 3 else a[n]
        return v.T if n.removeprefix("m_").removeprefix("v_") in TRANSPOSED else v

    payload = lambda n: sq(n) if n in EXACT_GATHER else sq(n).astype(BF16)

    loss, dx, g, lands_late = _local_step(sq("x"), sq("mem"), a["positions"][0], sq("loss_target"),
                                          {n: a[n] for n in REPLICATED}, [payload(n) for n in EARLY],
                                          {n: payload(n) for n in LATE})

    sh, rep = _early_grad_shards(g)
    small = _pack_small([rep[n] for n in REPLICATED] + [loss.reshape(1, 1)])
    *lands_early, land_small = _run_comm(_scatter_plan([sh[n] for n in EARLY], small), "scatter_last")
    quad = lambda n, land: (sq(n), sq("m_" + n), sq("v_" + n), land)
    lands = dict(zip(EARLY, lands_early, strict=True)) | lands_late

    outs = {}
    kinds = ("grad_", "delta_", "new_m_", "new_v_")
    for n, _ in SHARDED:
        res = _adamw([quad(n, lands[n])], "adamw_" + n)[0][0]
        for kind, val in zip(kinds, res, strict=True):
            outs[kind + n] = (val.T if n in TRANSPOSED else val).reshape(a[n].shape)
    zero = jnp.zeros((1, 1), F32)
    packed = [_pack_small([a[p + n] for n in REPLICATED] + [zero]) for p in ("", "m_", "v_")]
    res = _adamw([(*packed, land_small)], "adamw_replicated")[0][0]
    widths = [a[n].shape[1] for n in REPLICATED] + [1]
    for kind, buf in zip(kinds, res, strict=True):
        *vals, total = _unpack_small(buf, widths)
        for n, val in zip(REPLICATED, vals, strict=True):
            outs[kind + n] = val
        if kind == "grad_":
            loss = total[0, 0]

    ordered = [outs[kind + n] for kind in kinds for n in WEIGHTS]
    return (loss, dx[None], *ordered)


def kernel(x, mem, positions, norm_mix, w_in, gla_gate_w2, gla_gate_b, gla_out_norm, mla_q_a_norm, mla_w_uq, mla_kv_a_norm, mla_w_ukv, mla_q_norm, mla_k_norm, w_out, norm_xa, norm_mem, xa_w_q, xa_w_kv, xa_q_norm, xa_k_norm, xa_w_o, norm_ffn, ffn_w_gate, ffn_w_up, ffn_conv_w, ffn_conv_b, ffn_w_down, loss_target, m_norm_mix, m_w_in, m_gla_gate_w2, m_gla_gate_b, m_gla_out_norm, m_mla_q_a_norm, m_mla_w_uq, m_mla_kv_a_norm, m_mla_w_ukv, m_mla_q_norm, m_mla_k_norm, m_w_out, m_norm_xa, m_norm_mem, m_xa_w_q, m_xa_w_kv, m_xa_q_norm, m_xa_k_norm, m_xa_w_o, m_norm_ffn, m_ffn_w_gate, m_ffn_w_up, m_ffn_conv_w, m_ffn_conv_b, m_ffn_w_down, v_norm_mix, v_w_in, v_gla_gate_w2, v_gla_gate_b, v_gla_out_norm, v_mla_q_a_norm, v_mla_w_uq, v_mla_kv_a_norm, v_mla_w_ukv, v_mla_q_norm, v_mla_k_norm, v_w_out, v_norm_xa, v_norm_mem, v_xa_w_q, v_xa_w_kv, v_xa_q_norm, v_xa_k_norm, v_xa_w_o, v_norm_ffn, v_ffn_w_gate, v_ffn_w_up, v_ffn_conv_w, v_ffn_conv_b, v_ffn_w_down):
    return _step(dict(locals()))
```

```python
import functools

import jax
import jax.numpy as jnp
import numpy as np
from jax import lax
from jax.experimental import pallas as pl
from jax.experimental.pallas import tpu as pltpu

F32, BF16 = jnp.float32, jnp.bfloat16
MESH = pl.DeviceIdType.MESH

D_MODEL = 1024
EPS = 1e-6
GLA_HEADS, GLA_DK, GLA_DV, GLA_RANK, GLA_CHUNK = 4, 64, 128, 16, 64
GLA_GATE_NORM = 16.0
MLA_HEADS, MLA_Q_RANK, MLA_KV_RANK, MLA_NOPE, MLA_ROPE, MLA_V = 8, 256, 128, 64, 32, 64
MLA_QK = MLA_NOPE + MLA_ROPE
ROPE_THETA = 10000.0
LOG2E, LN2 = 1.4426950408889634, 0.6931471805599453
XA_HEADS, XA_DIM = 4, 128
D_FF = 2816
ADAM_LR, ADAM_B1, ADAM_B2, ADAM_EPS, ADAM_WD, ADAM_STEP = 0.001, 0.9, 0.999, 1e-08, 0.01, 10

LANES = 128
BF16_ROWS = 16
VMEM_LIMIT = 56 * 1024 * 1024
MATMUL_VMEM = 44 * 1024 * 1024
ROW_TILE = 512

P_GQ, P_GK, P_GV, P_OG, P_CQ, P_CKV, P_KPE, P_ALR, P_WIDTH = 0, 256, 512, 1024, 1536, 1792, 1920, 2048, 2176
N_GQ, N_GK, N_GV, N_ALR, N_OG, N_CQ, N_CKV, N_KPE, N_WIDTH = 0, 256, 512, 1024, 1040, 1552, 1808, 1936, 1968

SHARDED = (("w_in", 1), ("gla_gate_w2", 1), ("mla_w_uq", 1), ("mla_w_ukv", 1), ("w_out", 0), ("xa_w_q", 0),
           ("xa_w_kv", 0), ("xa_w_o", 1), ("ffn_w_gate", 1), ("ffn_w_up", 1), ("ffn_conv_w", 1), ("ffn_w_down", 0))
REPLICATED = ("norm_mix", "gla_gate_b", "gla_out_norm", "mla_q_a_norm", "mla_kv_a_norm", "mla_q_norm", "mla_k_norm",
              "norm_xa", "norm_mem", "xa_q_norm", "xa_k_norm", "norm_ffn", "ffn_conv_b")
EXACT_GATHER = ("gla_gate_w2", "ffn_conv_w")
TRANSPOSED = ("w_in", "ffn_w_gate", "ffn_w_up")
EARLY = ("w_in", "gla_gate_w2", "mla_w_uq", "mla_w_ukv")
LATE = tuple(n for n, _ in SHARDED if n not in EARLY)
LAST = ("ffn_w_down",)
WEIGHTS = ("norm_mix", "w_in", "gla_gate_w2", "gla_gate_b", "gla_out_norm", "mla_q_a_norm", "mla_w_uq",
           "mla_kv_a_norm", "mla_w_ukv", "mla_q_norm", "mla_k_norm", "w_out", "norm_xa", "norm_mem", "xa_w_q",
           "xa_w_kv", "xa_q_norm", "xa_k_norm", "xa_w_o", "norm_ffn", "ffn_w_gate", "ffn_w_up", "ffn_conv_w",
           "ffn_conv_b", "ffn_w_down")


_NN = ((1,), (0,))
_NT = ((1,), (1,))
_TN = ((0,), (0,))


def _dg(a, b, dims):
    return lax.dot_general(a.astype(BF16), b.astype(BF16), (dims, ((), ())), preferred_element_type=F32)


@jax.custom_vjp
def _dot_nn(a, b):
    return _dg(a, b, _NN)


_dot_nn.defvjp(lambda a, b: (_dg(a, b, _NN), (a, b)),
               lambda r, g: (_dg(g, r[1], _NT).astype(r[0].dtype), _dg(r[0], g, _TN).astype(r[1].dtype)))


@jax.custom_vjp
def _dot_nt(a, b):
    return _dg(a, b, _NT)


_dot_nt.defvjp(lambda a, b: (_dg(a, b, _NT), (a, b)),
               lambda r, g: (_dg(g, r[1], _NN).astype(r[0].dtype), _dg(g, r[0], _TN).astype(r[1].dtype)))


@jax.custom_vjp
def _dot_tn(a, b):
    return _dg(a, b, _TN)


_dot_tn.defvjp(lambda a, b: (_dg(a, b, _TN), (a, b)),
               lambda r, g: (_dg(r[1], g, _NT).astype(r[0].dtype), _dg(r[0], g, _NN).astype(r[1].dtype)))


def _rms(x, w, n=None):
    n = x.shape[-1] if n is None else n
    ms = jnp.sum(x * x, axis=-1, keepdims=True) * (1.0 / n)
    return x * lax.rsqrt(ms + EPS) * w


def _silu(x):
    return x * jax.nn.sigmoid(x)


def _log_sigmoid(x):
    return jnp.minimum(x, 0.0) - jnp.log(1.0 + jnp.exp(-jnp.abs(x)))


@jax.custom_vjp
def _cumsum_rows(x):
    n = x.shape[0]
    row = lax.broadcasted_iota(jnp.int32, x.shape, 0)
    k = 1
    while k < n:
        x = x + jnp.where(row >= k, pltpu.roll(x, k, 0), 0.0)
        k *= 2
    return x


def _cumsum_rows_bwd(_, g):
    n = g.shape[0]
    row = lax.broadcasted_iota(jnp.int32, g.shape, 0)
    k = 1
    while k < n:
        g = g + jnp.where(row < n - k, pltpu.roll(g, n - k, 0), 0.0)
        k *= 2
    return (g,)


_cumsum_rows.defvjp(lambda x: (_cumsum_rows(x), None), _cumsum_rows_bwd)


def _lane_mask(lo, hi):
    lane = lax.broadcasted_iota(jnp.int32, (1, LANES), 1)
    return ((lane >= lo) & (lane < hi)).astype(F32)


def _tile(n, t):
    t = min(n, t)
    assert n % t == 0, (n, t)
    return t


class _Epilogue:
    def __init__(self, fn, rows=(), consts=(), outs=(), accs=()):
        self.fn, self.rows, self.consts, self.outs, self.accs = fn, list(rows), list(consts), list(outs), list(accs)


def _matmul(a, b, mode, out_dtype, name, residual=None, a_lead=None, b_lead=None, more=None, epilogue=None):
    (a0, a1), (b0, b1) = a.shape[-2:], b.shape[-2:]
    if mode == "nn":
        m, k, k2, n = a0, a1, b0, b1
    elif mode == "nt":
        m, k, n, k2 = a0, a1, b0, b1
    else:
        k, m, k2, n = a0, a1, b0, b1
    assert k == k2, (a.shape, b.shape, mode)
    npar = 4 if "p" in (a_lead, b_lead) else 1
    nsum = 4 if "k" in (a_lead, b_lead) else 1
    pairs = [(a, b)] + ([more] if more else [])
    a_item, b_item, o_item = a.dtype.itemsize, b.dtype.itemsize, jnp.dtype(out_dtype).itemsize
    ep = epilogue
    row_extra = 4 if residual is not None else 0
    if ep:
        row_extra += (sum(r.dtype.itemsize * wd for r, wd, _ in ep.rows) + sum(jnp.dtype(d).itemsize * wd for wd, d in ep.outs)) / n

    def resident(lead, tiles):
        return lead != "p" and tiles == 1

    def vmem_need(tm, tn, tk):
        a_bufs = 1 if resident(a_lead, (m // tm) * (k // tk)) else 2
        b_bufs = 1 if resident(b_lead, (n // tn) * (k // tk)) else 2
        need = a_bufs * (nsum if a_lead == "k" else 1) * tm * tk * a_item + b_bufs * (nsum if b_lead == "k" else 1) * tk * tn * b_item
        need *= len(pairs)
        need += (0 if ep else 2 * tm * tn * o_item) + tm * tn * 4 * (2 if tk < k else 1)
        need += tm * tk * 2 * (a_item == 4 or mode == "tn") + tk * tn * 2 * (b_item == 4)
        return need + int(2 * tm * tn * row_extra) + (3 * tm * tn * 4 if ep else 0)

    halvings = (4096, 2048, 1024, 512, 256, 128, 64, 32, 16, 8)
    if mode == "tn":
        tm = m if m <= 2304 else m // 2
        tn = n if tm * n <= 1024 * 2304 else n // 2
        tk = next((r for r in halvings if k % r == 0 and vmem_need(tm, tn, r) <= MATMUL_VMEM), k)
    else:
        tn, tk = n, k
        tm = next((r for r in halvings if m % r == 0 and vmem_need(r, tn, tk) <= MATMUL_VMEM), m)
    assert m % tm == 0 and n % tn == 0 and k % tk == 0
    assert ep is None or (tn == n and tk == k and npar == 1)
    nk = k // tk
    dims = {"nn": _NN, "nt": _NT, "tn": _TN}[mode]
    n_in = 2 * len(pairs) + (residual is not None)
    n_ep_in = len(ep.rows) + len(ep.consts) if ep else 0
    n_out = len(ep.outs) + len(ep.accs) if ep else 1

    def body(*refs):
        ab, rs, ep_in, outs, scratch = _split_refs(refs, (2 * len(pairs), n_in - 2 * len(pairs), n_ep_in, n_out, nk > 1))
        prod = None
        for a_ref, b_ref in zip(ab[0::2], ab[1::2]):
            for sh in range(nsum):
                term = _dg(a_ref[sh] if a_lead == "k" else a_ref[...], b_ref[sh] if b_lead == "k" else b_ref[...], dims)
                prod = term if prod is None else prod + term

        def finish(r):
            if rs:
                r = r + rs[0][...]
            if ep is None:
                outs[0][...] = r.astype(outs[0].dtype)
                return
            vals = [x[...] for x in ep_in]
            ro, ao = ep.fn(r, vals[:len(ep.rows)], vals[len(ep.rows):])
            for ref, val in zip(outs[:len(ep.outs)], ro, strict=True):
                ref[...] = val.astype(ref.dtype)
            if ep.accs:
                @pl.when(pl.program_id(0) == 0)
                def _():
                    for ref in outs[len(ep.outs):]:
                        ref[...] = jnp.zeros_like(ref)

                for ref, val in zip(outs[len(ep.outs):], ao, strict=True):
                    ref[...] += val

        if nk == 1:
            finish(prod)
            return
        acc = scratch[0]
        kk = pl.program_id(3)

        @pl.when(kk == 0)
        def _():
            acc[...] = prod

        @pl.when(kk > 0)
        def _():
            acc[...] += prod

        @pl.when(kk == nk - 1)
        def _():
            finish(acc[...])

    def spec(lead, blk, idx, tiles=0):
        mode = {"pipeline_mode": pl.Buffered(1)} if resident(lead, tiles) else {}
        if lead is None:
            return pl.BlockSpec(blk, lambda i, j, p, kk: idx(i, j, kk), **mode)
        if lead == "p":
            return pl.BlockSpec((None,) + blk, lambda i, j, p, kk: (p,) + idx(i, j, kk))
        return pl.BlockSpec((nsum,) + blk, lambda i, j, p, kk: (0,) + idx(i, j, kk), **mode)

    a_tiles, b_tiles = (m // tm) * nk, (n // tn) * nk
    if mode == "nn":
        pair_specs = [spec(a_lead, (tm, tk), lambda i, j, kk: (i, kk), a_tiles),
                      spec(b_lead, (tk, tn), lambda i, j, kk: (kk, j), b_tiles)]
    elif mode == "nt":
        pair_specs = [spec(a_lead, (tm, tk), lambda i, j, kk: (i, kk), a_tiles),
                      spec(b_lead, (tn, tk), lambda i, j, kk: (j, kk), b_tiles)]
    else:
        pair_specs = [spec(a_lead, (tk, tm), lambda i, j, kk: (kk, i), a_tiles),
                      spec(b_lead, (tk, tn), lambda i, j, kk: (kk, j), b_tiles)]
    tile = spec(None, (tm, tn), lambda i, j, kk: (i, j))
    in_specs = pair_specs * len(pairs)
    args = [x for pair in pairs for x in pair]
    if residual is not None:
        assert npar == 1
        in_specs.append(tile)
        args.append(residual)
    if ep:
        in_specs += [pl.BlockSpec((tm, wd), functools.partial(lambda cb, i, j, p, kk: (i, cb), cb)) for _, wd, cb in ep.rows]
        in_specs += [pl.BlockSpec(c.shape, lambda i, j, p, kk: (0, 0)) for c in ep.consts]
        args += [r for r, _, _ in ep.rows] + ep.consts
        out_specs = [pl.BlockSpec((tm, wd), lambda i, j, p, kk: (i, 0)) for wd, _ in ep.outs]
        out_specs += [pl.BlockSpec(shape, lambda i, j, p, kk: (0, 0)) for shape in ep.accs]
        out_shape = [jax.ShapeDtypeStruct((m, wd), d) for wd, d in ep.outs] + [jax.ShapeDtypeStruct(sh, F32) for sh in ep.accs]
    else:
        out_specs = spec("p" if npar > 1 else None, (tm, tn), lambda i, j, kk: (i, j))
        out_shape = jax.ShapeDtypeStruct(((4,) if npar > 1 else ()) + (m, n), out_dtype)
    outer = "arbitrary" if ep and ep.accs else "parallel"
    return pl.pallas_call(
        body, grid=(m // tm, n // tn, npar, nk), in_specs=in_specs, out_specs=out_specs, out_shape=out_shape,
        scratch_shapes=[pltpu.VMEM((tm, tn), F32)] if nk > 1 else [],
        compiler_params=pltpu.CompilerParams(dimension_semantics=(outer, outer, outer, "arbitrary"),
                                             vmem_limit_bytes=VMEM_LIMIT),
        name=name)(*args)


def _row(a, width=None, col_block=0):
    return (a, a.shape[1] if width is None else width, col_block)


def _rows_call(body, rows, consts, outs, accs=(), *, name, tile=ROW_TILE):
    s = rows[0][0].shape[0]
    t = _tile(s, tile)
    nr, nc, no = len(rows), len(consts), len(outs)

    def kern(*refs):
        r = [x[...] for x in refs[:nr]]
        c = [x[...] for x in refs[nr:nr + nc]]
        o_refs = refs[nr + nc:nr + nc + no]
        a_refs = refs[nr + nc + no:]
        ro, ao = body(r, c)
        for ref, val in zip(o_refs, ro, strict=True):
            ref[...] = val.astype(ref.dtype)
        if a_refs:
            @pl.when(pl.program_id(0) == 0)
            def _():
                for ref in a_refs:
                    ref[...] = jnp.zeros_like(ref)

            for ref, val in zip(a_refs, ao, strict=True):
                ref[...] += val

    in_specs = [pl.BlockSpec((t, w), functools.partial(lambda cb, i: (i, cb), cb)) for (_, w, cb) in rows]
    in_specs += [pl.BlockSpec(c.shape, lambda i: (0, 0)) for c in consts]
    out_specs = [pl.BlockSpec((t, w), lambda i: (i, 0)) for (w, _) in outs]
    out_specs += [pl.BlockSpec(shape, lambda i: (0, 0)) for shape in accs]
    out_shape = [jax.ShapeDtypeStruct((s, w), dt) for (w, dt) in outs]
    out_shape += [jax.ShapeDtypeStruct(shape, F32) for shape in accs]
    return pl.pallas_call(
        kern, grid=(s // t,), in_specs=in_specs, out_specs=out_specs, out_shape=out_shape,
        compiler_params=pltpu.CompilerParams(dimension_semantics=("arbitrary" if accs else "parallel",),
                                             vmem_limit_bytes=VMEM_LIMIT),
        name=name)(*[r[0] for r in rows], *consts)


def _gla_chunk(q, k, la, v0, v1, s0, s1):
    c = q.shape[0]
    r = lax.broadcasted_iota(jnp.int32, (c, c), 0)
    cc = lax.broadcasted_iota(jnp.int32, (c, c), 1)
    tril = cc <= r
    cum = _cumsum_rows(la)
    cl = jnp.sum(la, axis=0, keepdims=True)
    qd = q * (GLA_DK ** -0.5) * jnp.exp(cum)
    ki = k * jnp.exp(-cum)
    ke = k * jnp.exp(cl - cum)
    dec = jnp.exp(cl)
    outs, news = [], []
    for h, (v, s) in enumerate(((v0, s0), (v1, s1))):
        mk = _lane_mask(GLA_DK * h, GLA_DK * (h + 1))
        qh = qd * mk
        att = jnp.where(tril, _dot_nt(qh, ki), 0.0)
        outs.append(_dot_nn(att, v) + _dot_nt(qh, s))
        news.append(s * dec + _dot_tn(v, ke * mk))
    return outs[0], outs[1], news[0], news[1]


def _gla_specs(tb, rev_nb=None):
    blk = (lambda b: b) if rev_nb is None else (lambda b: rev_nb - 1 - b)
    q = pl.BlockSpec((tb, 128), lambda p, b: (blk(b), P_GQ // 128 + p))
    k = pl.BlockSpec((tb, 128), lambda p, b: (blk(b), P_GK // 128 + p))
    la = pl.BlockSpec((tb, 128), lambda p, b: (blk(b), p))
    v = pl.BlockSpec((tb, 256), lambda p, b: (blk(b), P_GV // 256 + p))
    o = pl.BlockSpec((tb, 256), lambda p, b: (blk(b), p))
    st = pl.BlockSpec((tb // GLA_CHUNK, 2, 128, 128), lambda p, b: (blk(b), p, 0, 0))
    return q, k, la, v, o, st


def _gla_fwd(proj, la):
    s = proj.shape[0]
    tb = _tile(s, ROW_TILE)
    nb, nch = s // tb, tb // GLA_CHUNK

    def kern(q_ref, k_ref, la_ref, v_ref, o_ref, st_ref, s_sc):
        @pl.when(pl.program_id(1) == 0)
        def _():
            s_sc[...] = jnp.zeros_like(s_sc)

        s0, s1 = s_sc[0], s_sc[1]
        for ci in range(nch):
            sl = slice(ci * GLA_CHUNK, (ci + 1) * GLA_CHUNK)
            st_ref[ci, 0] = s0
            st_ref[ci, 1] = s1
            o0, o1, s0, s1 = _gla_chunk(q_ref[sl, :], k_ref[sl, :], la_ref[sl, :], v_ref[sl, 0:128],
                                        v_ref[sl, 128:256], s0, s1)
            o_ref[sl, 0:128] = o0
            o_ref[sl, 128:256] = o1
        s_sc[0] = s0
        s_sc[1] = s1

    q, k, lasp, v, o, st = _gla_specs(tb)
    return pl.pallas_call(
        kern, grid=(2, nb), in_specs=[q, k, lasp, v], out_specs=[o, st],
        out_shape=[jax.ShapeDtypeStruct((s, 512), F32),
                   jax.ShapeDtypeStruct((s // GLA_CHUNK, GLA_HEADS, 128, 128), F32)],
        scratch_shapes=[pltpu.VMEM((2, 128, 128), F32)],
        compiler_params=pltpu.CompilerParams(dimension_semantics=("parallel", "arbitrary"),
                                             vmem_limit_bytes=VMEM_LIMIT),
        name="gla_fwd")(proj, proj, la, proj)


def _gla_bwd(proj, la, states, d_o):
    s = proj.shape[0]
    tb = _tile(s, ROW_TILE)
    nb, nch = s // tb, tb // GLA_CHUNK

    def kern(q_ref, k_ref, la_ref, v_ref, do_ref, st_ref, dq_ref, dk_ref, dla_ref, dv_ref, ds_sc):
        @pl.when(pl.program_id(1) == 0)
        def _():
            ds_sc[...] = jnp.zeros_like(ds_sc)

        d0, d1 = ds_sc[0], ds_sc[1]
        for ci in reversed(range(nch)):
            sl = slice(ci * GLA_CHUNK, (ci + 1) * GLA_CHUNK)
            _, vjp = jax.vjp(_gla_chunk, q_ref[sl, :], k_ref[sl, :], la_ref[sl, :], v_ref[sl, 0:128],
                             v_ref[sl, 128:256], st_ref[ci, 0], st_ref[ci, 1])
            gq, gk, gla, gv0, gv1, d0, d1 = vjp((do_ref[sl, 0:128], do_ref[sl, 128:256], d0, d1))
            dq_ref[sl, :] = gq
            dk_ref[sl, :] = gk
            dla_ref[sl, :] = gla
            dv_ref[sl, 0:128] = gv0
            dv_ref[sl, 128:256] = gv1
        ds_sc[0] = d0
        ds_sc[1] = d1

    q, k, lasp, v, o, st = _gla_specs(tb, rev_nb=nb)
    return pl.pallas_call(
        kern, grid=(2, nb), in_specs=[q, k, lasp, v, o, st], out_specs=[lasp, lasp, lasp, o],
        out_shape=[jax.ShapeDtypeStruct((s, 256), F32), jax.ShapeDtypeStruct((s, 256), F32),
                   jax.ShapeDtypeStruct((s, 256), F32), jax.ShapeDtypeStruct((s, 512), F32)],
        scratch_shapes=[pltpu.VMEM((2, 128, 128), F32)],
        compiler_params=pltpu.CompilerParams(dimension_semantics=("parallel", "arbitrary"),
                                             vmem_limit_bytes=VMEM_LIMIT),
        name="gla_bwd")(proj, proj, la, proj, d_o, states)


def _causal_keep(t):
    return lax.broadcasted_iota(jnp.int32, (t, t), 1) <= lax.broadcasted_iota(jnp.int32, (t, t), 0)


def _split_refs(refs, counts):
    out, off = [], 0
    for cnt in counts:
        out.append(refs[off:off + cnt])
        off += cnt
    return out


def _causal_blocks(n, key_major):
    pairs = ([(ki, qi) for ki in range(n) for qi in range(ki, n)] if key_major else
             [(ki, qi) for qi in range(n) for ki in range(qi + 1)])
    return np.array([ki for ki, _ in pairs], np.int32), np.array([qi for _, qi in pairs], np.int32)


def _attn_fwd(q, k, v, comm, tile=2048):
    s = q.shape[0]
    t = _tile(s, tile)
    n = s // t
    nci, nco = len(comm.ins), len(comm.out_shape)

    ki_tab, qi_tab = _causal_blocks(n, key_major=False)
    steps = len(ki_tab)

    def kern(ki_ref, qi_ref, *refs):
        (q_ref, k_ref, v_ref), cins, (o_ref, lse_ref), couts, (m_sc, l_sc, acc_sc), csems = _split_refs(
            refs, (3, nci, 2, nco, 3, len(comm.sems)))
        pair, step = pl.program_id(0), pl.program_id(1)
        qi, ki = qi_ref[step], ki_ref[step]
        place = _place()

        @pl.when((pair == 0) & (step == 0))
        def _():
            comm.start(place, cins, couts, csems)

        @pl.when((pair == MLA_HEADS // 2 - 1) & (step == 0))
        def _():
            comm.mid(place, cins, couts, csems)

        first = lax.broadcasted_iota(jnp.int32, (t, LANES), 1) < MLA_V

        @pl.when(ki == 0)
        def _():
            m_sc[...] = jnp.full_like(m_sc, -jnp.inf)
            l_sc[...] = jnp.zeros_like(l_sc)
            acc_sc[...] = jnp.zeros_like(acc_sc)

        def update(rows, cols, masked):
            nr = rows.stop - rows.start
            sel = first[:nr]
            alphas, pvs = [], []
            for h in range(2):
                sc = _dg(q_ref[rows, 128 * h:128 * (h + 1)], k_ref[cols, 128 * h:128 * (h + 1)], _NT)
                if masked:
                    sc = jnp.where(_causal_keep(nr), sc, -jnp.inf)
                m_prev = m_sc[h, rows]
                m_new = jnp.maximum(m_prev, jnp.max(sc, axis=1, keepdims=True))
                alpha = jnp.exp2(m_prev - m_new)
                p = jnp.exp2(sc - m_new[:, 0:1])
                l_sc[h, rows] = alpha * l_sc[h, rows] + jnp.sum(p, axis=1, keepdims=True)
                m_sc[h, rows] = m_new
                alphas.append(alpha)
                pvs.append(_dg(p, v_ref[cols, :], _NN))
            acc_sc[rows] = acc_sc[rows] * jnp.where(sel, alphas[0], alphas[1]) + jnp.where(sel, pvs[0], pvs[1])

        halves = [slice(0, t)] if t % 256 else [slice(0, t // 2), slice(t // 2, t)]

        @pl.when(ki < qi)
        def _():
            for rows in halves:
                for cols in halves:
                    update(rows, cols, False)

        @pl.when(ki == qi)
        def _():
            for i, rows in enumerate(halves):
                for j, cols in enumerate(halves[:i + 1]):
                    update(rows, cols, i == j)

        @pl.when(ki == qi)
        def _():
            l = jnp.where(first, l_sc[0], l_sc[1])
            m = jnp.where(first, m_sc[0], m_sc[1])
            o_ref[...] = acc_sc[...] / l
            lse_ref[...] = m + jnp.log2(l)

        @pl.when((pair == MLA_HEADS // 2 - 1) & (step == steps - 1))
        def _():
            comm.finish(place, cins, couts, csems)

    q_idx = lambda p, st, ki_r, qi_r: (qi_r[st], p)
    k_idx = lambda p, st, ki_r, qi_r: (ki_r[st], p)
    res = pl.pallas_call(
        kern, grid_spec=pltpu.PrefetchScalarGridSpec(
            num_scalar_prefetch=2, grid=(MLA_HEADS // 2, steps),
            in_specs=[pl.BlockSpec((t, 256), q_idx), pl.BlockSpec((t, 256), k_idx), pl.BlockSpec((t, 128), k_idx)]
            + [ANY] * nci,
            out_specs=[pl.BlockSpec((t, 128), q_idx), pl.BlockSpec((t, 128), q_idx)] + [ANY] * nco,
            scratch_shapes=[pltpu.VMEM((2, t, LANES), F32), pltpu.VMEM((2, t, LANES), F32),
                            pltpu.VMEM((t, LANES), F32)] + comm.sems),
        out_shape=[jax.ShapeDtypeStruct((s, 512), F32), jax.ShapeDtypeStruct((s, 512), F32)] + comm.out_shape,
        compiler_params=pltpu.CompilerParams(dimension_semantics=("arbitrary", "arbitrary"),
                                             vmem_limit_bytes=VMEM_LIMIT),
        name="mla_attn_fwd")(ki_tab, qi_tab, q, k, v, *comm.ins)
    return res[0], res[1], res[2:]


def _attn_bwd(q, k, v, o, lse, d_o, comm, tile=1024):
    s = q.shape[0]
    t = _tile(s, tile)
    n = s // t
    nci, nco = len(comm.ins), len(comm.out_shape)

    ki_tab, qi_tab = _causal_blocks(n, key_major=True)
    steps = len(ki_tab)

    def kern(ki_ref, qi_ref, *refs):
        (q_ref, k_ref, v_ref, o_ref, lse_ref, do_ref), cins, (dq_ref, dk_ref, dv_ref), couts, (dk_sc, dv_sc), csems = \
            _split_refs(refs, (6, nci, 3, nco, 2, len(comm.sems)))
        pair, step = pl.program_id(0), pl.program_id(1)
        ki, qi = ki_ref[step], qi_ref[step]
        place = _place()

        @pl.when((pair == 0) & (step == 0))
        def _():
            comm.start(place, cins, couts, csems)

        @pl.when((pair == MLA_HEADS // 2 - 1) & (step == 0))
        def _():
            comm.mid(place, cins, couts, csems)

        @pl.when((ki == 0) & (qi == 0))
        def _():
            dq_ref[...] = jnp.zeros_like(dq_ref)

        @pl.when(qi == ki)
        def _():
            dk_sc[...] = jnp.zeros_like(dk_sc)
            dv_sc[...] = jnp.zeros_like(dv_sc)

        def update(rows, cols, masked):
            nr = rows.stop - rows.start
            d_o = do_ref[rows, :]
            prod = d_o * o_ref[rows, :]
            dq_rows = pl.ds(pl.multiple_of(qi * t + rows.start, nr), nr)
            for h in range(2):
                hs = slice(128 * h, 128 * (h + 1))
                mk = _lane_mask(MLA_V * h, MLA_V * (h + 1))
                qh, kh = q_ref[rows, hs], k_ref[cols, hs]
                sc = _dg(qh, kh, _NT)
                if masked:
                    sc = jnp.where(_causal_keep(nr), sc, -jnp.inf)
                p = jnp.exp2(sc - lse_ref[rows, MLA_V * h:MLA_V * h + 1])
                doh = d_o * mk
                dp = _dg(doh * LN2, v_ref[cols, :], _NT)
                delta = jnp.sum(prod * mk, axis=1, keepdims=True) * LN2
                ds = p * (dp - delta)
                dv_sc[cols, :] += _dg(p, doh, _TN)
                dk_sc[cols, hs] += _dg(ds, qh, _TN)
                dq_ref[dq_rows, hs] += _dg(ds, kh, _NN)

        halves = [slice(0, t)] if t % 256 else [slice(0, t // 2), slice(t // 2, t)]

        @pl.when(qi > ki)
        def _():
            for rows in halves:
                for cols in halves:
                    update(rows, cols, False)

        @pl.when(qi == ki)
        def _():
            for i, rows in enumerate(halves):
                for j, cols in enumerate(halves[:i + 1]):
                    update(rows, cols, i == j)

        @pl.when(qi == n - 1)
        def _():
            dk_ref[...] = dk_sc[...]
            dv_ref[...] = dv_sc[...].astype(dv_ref.dtype)

        @pl.when((pair == MLA_HEADS // 2 - 1) & (step == steps - 1))
        def _():
            comm.finish(place, cins, couts, csems)

    q_idx = lambda p, st, ki_r, qi_r: (qi_r[st], p)
    k_idx = lambda p, st, ki_r, qi_r: (ki_r[st], p)
    res = pl.pallas_call(
        kern, grid_spec=pltpu.PrefetchScalarGridSpec(
            num_scalar_prefetch=2, grid=(MLA_HEADS // 2, steps),
            in_specs=[pl.BlockSpec((t, 256), q_idx), pl.BlockSpec((t, 256), k_idx), pl.BlockSpec((t, 128), k_idx),
                      pl.BlockSpec((t, 128), q_idx), pl.BlockSpec((t, 128), q_idx), pl.BlockSpec((t, 128), q_idx)]
            + [ANY] * nci,
            out_specs=[pl.BlockSpec((s, 256), lambda p, st, ki_r, qi_r: (0, p)), pl.BlockSpec((t, 256), k_idx),
                       pl.BlockSpec((t, 128), k_idx)] + [ANY] * nco,
            scratch_shapes=[pltpu.VMEM((t, 256), F32), pltpu.VMEM((t, 128), F32)] + comm.sems),
        out_shape=[jax.ShapeDtypeStruct((s, 1024), F32), jax.ShapeDtypeStruct((s, 1024), F32),
                   jax.ShapeDtypeStruct((s, 512), BF16)] + comm.out_shape,
        compiler_params=pltpu.CompilerParams(dimension_semantics=("arbitrary", "arbitrary"),
                                             vmem_limit_bytes=VMEM_LIMIT),
        name="mla_attn_bwd")(ki_tab, qi_tab, q, k, v, o, lse, d_o, *comm.ins)
    return res[0], res[1], res[2], res[3:]


def _gate_fn(alr, w2, b):
    return _log_sigmoid(_dot_nn(alr, w2) + b) * (1.0 / GLA_GATE_NORM)


def _make_norm_rope(scale):
    def forward(x, w, c, sa, sb):
        r = lax.rsqrt(jnp.sum(x * x, axis=-1, keepdims=True) * (1.0 / MLA_QK) + EPS)
        y = x * r * w
        out = y * c + pltpu.roll(y, LANES - 16, 1) * sa + pltpu.roll(y, 16, 1) * sb
        return (out if scale == 1.0 else out * scale), r

    @jax.custom_vjp
    def norm_rope(x, w, c, sa, sb):
        return forward(x, w, c, sa, sb)[0]

    def fwd(x, w, c, sa, sb):
        out, r = forward(x, w, c, sa, sb)
        return out, (x, w, c, sa, sb, r)

    def bwd(res, g):
        x, w, c, sa, sb, r = res
        if scale != 1.0:
            g = g * scale
        gy = g * c + pltpu.roll(g * sa, 16, 1) + pltpu.roll(g * sb, LANES - 16, 1)
        xr = x * r
        t = gy * w
        m = jnp.sum(t * xr, axis=-1, keepdims=True) * (1.0 / MLA_QK)
        return r * (t - xr * m), jnp.sum(gy * xr, axis=0, keepdims=True), jnp.zeros_like(c), jnp.zeros_like(sa), jnp.zeros_like(sb)

    norm_rope.defvjp(fwd, bwd)
    return norm_rope


_q_norm_rope = _make_norm_rope(MLA_QK ** -0.5 * LOG2E)
_k_norm_rope = _make_norm_rope(1.0)


def _qk_head(qh, kh, kpe, c, sa, sb, qn, kn):
    kfull = kh + kpe * _lane_mask(MLA_NOPE, MLA_QK)
    return _q_norm_rope(qh, qn, c, sa, sb), _k_norm_rope(kfull, kn, c, sa, sb)


def _mix_head(o, og, gn):
    return _rms(o, gn) * _silu(og)


def _xa_head(xq, xk, xv, qn, kn):
    sc = _dot_nt(_rms(xq, qn), _rms(xk, kn)) * (XA_DIM ** -0.5)
    e = jnp.exp(sc - lax.stop_gradient(jnp.max(sc, axis=1, keepdims=True)))
    p = e / jnp.sum(e, axis=1, keepdims=True)
    return _dot_nn(p, xv)


def _heads(x, n):
    return [x[:, 128 * h:128 * (h + 1)] for h in range(n)]


def _cat(xs):
    return jnp.concatenate(xs, axis=1)


def _norm_fwd(x, w, name):
    return _rows_call(lambda r, c: ([_rms(r[0], c[0])], []), [_row(x)], [w], [(x.shape[1], BF16)], name=name)[0]


def _norm_fwd_epilogue(w):
    return _Epilogue(lambda h, rows, consts: ([h, _rms(h, consts[0])], []), [], [w], [(D_MODEL, F32), (D_MODEL, BF16)], [])


def _norm_bwd_epilogue(x, w, add):
    def fn(d_out, rows, consts):
        _, vjp = jax.vjp(_rms, rows[0], consts[0])
        dx, dw = vjp(d_out)
        return [dx + rows[1]], [dw]

    return _Epilogue(fn, [_row(x), _row(add)], [w], [(D_MODEL, F32)], [w.shape])


def _norm_fwd_comm(x, w, comm, name):
    s, d = x.shape
    t = _tile(s, ROW_TILE)
    n = s // t
    nci, nco = len(comm.ins), len(comm.out_shape)

    def kern(*refs):
        (x_ref, w_ref), cins, (o_ref,), couts, csems = _split_refs(refs, (2, nci, 1, nco, len(comm.sems)))
        place = _place()

        @pl.when(pl.program_id(0) == 0)
        def _():
            comm.start(place, cins, couts, csems)

        o_ref[...] = _rms(x_ref[...], w_ref[...]).astype(o_ref.dtype)

        @pl.when(pl.program_id(0) == n - 1)
        def _():
            comm.mid(place, cins, couts, csems)
            comm.finish(place, cins, couts, csems)

    tile = pl.BlockSpec((t, d), lambda i: (i, 0))
    res = pl.pallas_call(
        kern, grid=(n,), in_specs=[tile, pl.BlockSpec(w.shape, lambda i: (0, 0))] + [ANY] * nci,
        out_specs=[tile] + [ANY] * nco, out_shape=[jax.ShapeDtypeStruct((s, d), BF16)] + comm.out_shape,
        scratch_shapes=comm.sems,
        compiler_params=pltpu.CompilerParams(dimension_semantics=("arbitrary",), vmem_limit_bytes=VMEM_LIMIT),
        name=name)(x, w, *comm.ins)
    return res[0], res[1:]


def _norm_bwd(x, w, d_out, add, name):
    def body(r, c):
        _, vjp = jax.vjp(_rms, r[0], c[0])
        dx, dw = vjp(r[1])
        return [dx + r[2]], [dw]

    return _rows_call(body, [_row(x), _row(d_out), _row(add)], [w], [(x.shape[1], F32)], [w.shape], name=name)


CONV_HALO = BF16_ROWS


def _conv_specs(s, f, t):
    n8 = t // CONV_HALO
    cur = pl.BlockSpec((None, t, f), lambda j, i: (j, i, 0))
    prev = pl.BlockSpec((None, CONV_HALO, f), lambda j, i: (j, jnp.maximum(i * n8 - 1, 0), 0))
    nxt = pl.BlockSpec((None, CONV_HALO, f), lambda j, i: (j, jnp.minimum((i + 1) * n8, s // CONV_HALO - 1), 0))
    cw = pl.BlockSpec((None, 3, f), lambda j, i: (j, 0, 0))
    cb = pl.BlockSpec((None, 1, f), lambda j, i: (j, 0, 0))
    return cur, prev, nxt, cw, cb


def _conv_taps(g, prev, first):
    ext = jnp.concatenate([jnp.where(first, 0.0, prev.astype(F32)), g], axis=0)
    return pltpu.roll(ext, 1, 0)[CONV_HALO:], pltpu.roll(ext, 2, 0)[CONV_HALO:]


def _conv_fwd(gg, uu, cw, cb, comm):
    _, s, f = gg.shape
    t = _tile(s, ROW_TILE)
    nt = s // t
    nci, nco = len(comm.ins), len(comm.out_shape)

    def kern(*refs):
        (g_ref, gp_ref, u_ref, cw_ref, cb_ref), cins, (o_ref,), couts, csems = _split_refs(
            refs, (5, nci, 1, nco, len(comm.sems)))
        shard, i = pl.program_id(0), pl.program_id(1)
        place = _place()

        @pl.when((shard == 0) & (i == 0))
        def _():
            comm.start(place, cins, couts, csems)

        @pl.when((shard == 3) & (i == 0))
        def _():
            comm.mid(place, cins, couts, csems)

        g = g_ref[...].astype(F32)
        g1, g2 = _conv_taps(g, gp_ref[...], i == 0)
        w = cw_ref[...]
        gc = cb_ref[...] + w[0:1] * g2 + w[1:2] * g1 + w[2:3] * g
        o_ref[...] = (_silu(gc) * u_ref[...].astype(F32)).astype(o_ref.dtype)

        @pl.when((shard == 3) & (i == nt - 1))
        def _():
            comm.finish(place, cins, couts, csems)

    cur, prev, _, cws, cbs = _conv_specs(s, f, t)
    res = pl.pallas_call(
        kern, grid=(4, nt), in_specs=[cur, prev, cur, cws, cbs] + [ANY] * nci, out_specs=[cur] + [ANY] * nco,
        out_shape=[jax.ShapeDtypeStruct(gg.shape, BF16)] + comm.out_shape, scratch_shapes=comm.sems,
        compiler_params=pltpu.CompilerParams(dimension_semantics=("arbitrary", "arbitrary"), vmem_limit_bytes=VMEM_LIMIT),
        name="ffn_conv_fwd")(gg, gg, uu, cw, cb, *comm.ins)
    return res[0], res[1:]


def _conv_bwd(gg, uu, dact, cw, cb):
    _, s, f = gg.shape
    t = _tile(s, ROW_TILE)
    nt = s // t

    def kern(g_ref, gp_ref, gn_ref, u_ref, un_ref, da_ref, dan_ref, cw_ref, cb_ref, du_ref, dg_ref, dcw_ref, dcb_ref):
        i = pl.program_id(1)
        cat = lambda a_ref, b_ref: jnp.concatenate([a_ref[...].astype(F32), b_ref[...].astype(F32)], axis=0)
        g, u, da = cat(g_ref, gn_ref), cat(u_ref, un_ref), cat(da_ref, dan_ref)
        g1, g2 = _conv_taps(g, gp_ref[...], i == 0)
        w = cw_ref[...]
        gc = cb_ref[...] + w[0:1] * g2 + w[1:2] * g1 + w[2:3] * g
        sg = jax.nn.sigmoid(gc)
        du_ref[...] = (da[:t] * (gc[:t] * sg[:t])).astype(du_ref.dtype)
        row = lax.broadcasted_iota(jnp.int32, (t + CONV_HALO, 1), 0)
        dgc = jnp.where((row < t) | (i < nt - 1), da * u * (sg * (1.0 + gc * (1.0 - sg))), 0.0)
        up1 = pltpu.roll(dgc, t + CONV_HALO - 1, 0)[:t]
        up2 = pltpu.roll(dgc, t + CONV_HALO - 2, 0)[:t]
        dgc = dgc[:t]
        dg_ref[...] = (w[2:3] * dgc + w[1:2] * up1 + w[0:1] * up2).astype(dg_ref.dtype)

        @pl.when(i == 0)
        def _():
            dcw_ref[...] = jnp.zeros_like(dcw_ref)
            dcb_ref[...] = jnp.zeros_like(dcb_ref)

        ones = jnp.ones((8, t), BF16)
        col_sum = lambda a: _dg(ones, a, _NN)[0:1]
        dcw_ref[0:1, :] += col_sum(dgc * g2[:t])
        dcw_ref[1:2, :] += col_sum(dgc * g1[:t])
        dcw_ref[2:3, :] += col_sum(dgc * g[:t])
        dcb_ref[...] += col_sum(dgc)

    cur, prev, nxt, cws, cbs = _conv_specs(s, f, t)
    return pl.pallas_call(
        kern, grid=(4, nt), in_specs=[cur, prev, nxt, cur, nxt, cur, nxt, cws, cbs], out_specs=[cur, cur, cws, cbs],
        out_shape=[jax.ShapeDtypeStruct(gg.shape, BF16), jax.ShapeDtypeStruct(gg.shape, BF16),
                   jax.ShapeDtypeStruct(cw.shape, F32), jax.ShapeDtypeStruct(cb.shape, F32)],
        compiler_params=pltpu.CompilerParams(dimension_semantics=("parallel", "arbitrary"), vmem_limit_bytes=VMEM_LIMIT),
        name="ffn_conv_bwd")(gg, gg, gg, uu, uu, dact, dact, cw, cb)


def _rope_tables(pos):
    half = MLA_ROPE // 2
    lane = jnp.arange(LANES)
    rotary = (lane >= MLA_NOPE) & (lane < MLA_QK)
    inv = jnp.where(rotary, ROPE_THETA ** (-((lane - MLA_NOPE) % half).astype(F32) / half), 0.0)
    ang = pos.astype(F32)[:, None] * inv
    cos, sin = jnp.cos(ang), jnp.sin(ang)
    first = rotary & (lane < MLA_NOPE + half)
    return cos, jnp.where(first, -sin, 0.0), jnp.where(rotary & ~first, sin, 0.0)


def _local_step(x, mem, pos, target, rep, early_shards, late_shards):
    g = {}
    c, sa, sb = _rope_tables(pos)

    xn, gathered = _norm_fwd_comm(x, rep["norm_mix"], _gather_plan(early_shards), "norm_mix_fwd_gather")
    w = _early_layout(dict(zip(EARLY, gathered, strict=True)), rep)

    def proj_fn(r, rows, k):
        la_ = _gate_fn(r[:, P_ALR:P_ALR + 128], k[0], k[1])
        return [r, la_, _rms(r[:, P_CQ:P_CQ + MLA_Q_RANK], k[2]), _rms(r[:, P_CKV:P_CKV + MLA_KV_RANK], k[3])], []

    proj, la, q_lat, kv_lat = _matmul(
        xn, w["in"], "nt", F32, "proj_fwd", epilogue=_Epilogue(
            proj_fn, [], [w["w2"], w["gate_b"], w["q_a_norm"], w["kv_a_norm"]],
            [(P_WIDTH, F32), (256, F32), (MLA_Q_RANK, BF16), (MLA_KV_RANK, BF16)], []))
    alr = _row(proj, 128, P_ALR // 128)
    kpe = _row(proj, 128, P_KPE // 128)
    og = _row(proj, 512, P_OG // 512)
    cq = _row(proj, 256, P_CQ // 256)
    ckv = _row(proj, 128, P_CKV // 128)

    o_gla, states = _gla_fwd(proj, la)

    def qk_body(r, k):
        q_up, k_up = _dg(r[0], k[0], _NN), _dg(r[1], k[1], _NN)
        qs, ks = [], []
        for qh, kh in zip(_heads(q_up, MLA_HEADS), _heads(k_up, MLA_HEADS)):
            a, b = _qk_head(qh, kh, r[2], r[3], r[4], r[5], k[3], k[4])
            qs.append(a)
            ks.append(b)
        return [_cat(qs), _cat(ks), _dg(r[1], k[2], _NN)], []

    tabs = [_row(c), _row(sa), _row(sb)]
    qk_consts = [w["uq"], w["k"], w["v"], w["q_norm"], w["k_norm"]]
    q_r, k_r, v_mla = _rows_call(qk_body, [_row(q_lat), _row(kv_lat), kpe] + tabs, qk_consts,
                                 [(1024, BF16), (1024, BF16), (512, BF16)], name="mla_qk_fwd")
    with_attn = [n for n in LATE if n not in LAST]
    o_mla, lse, gathered = _attn_fwd(q_r, k_r, v_mla, _gather_plan([late_shards[n] for n in with_attn]))
    w.update(_late_layout(dict(zip(with_attn, gathered, strict=True))))

    def mix_body(r, k):
        ys = [_mix_head(o, g_, k[0]) for o, g_ in zip(_heads(r[0], GLA_HEADS), _heads(r[1], GLA_HEADS))]
        return [_cat(ys + [r[2]])], []

    cat = _rows_call(mix_body, [_row(o_gla), og, _row(o_mla)], [w["gla_out_norm"]], [(1024, BF16)],
                     name="mix_fwd")[0]
    h1, hn = _matmul(cat, w["out"], "nn", F32, "out_fwd_norm", residual=x, epilogue=_norm_fwd_epilogue(w["norm_xa"]))
    mn = _norm_fwd(mem, w["norm_mem"], "norm_mem_fwd")
    xkv = _matmul(mn, w["xkv"], "nn", F32, "xa_kv_fwd")

    def xa_fn(r, rows, k):
        ks, vs = _heads(k[0], 2 * XA_HEADS)[:XA_HEADS], _heads(k[0], 2 * XA_HEADS)[XA_HEADS:]
        return [r, _cat([_xa_head(a, b, v_, k[1], k[2]) for a, b, v_ in zip(_heads(r, XA_HEADS), ks, vs)])], []

    xq, xo = _matmul(hn, w["xq"], "nn", F32, "xa_q_fwd_attn", epilogue=_Epilogue(
        xa_fn, [], [xkv, w["xa_q_norm"], w["xa_k_norm"]], [(512, F32), (512, BF16)], []))
    h2, fn = _matmul(xo, w["xo"], "nn", F32, "xa_o_fwd_norm", residual=h1, epilogue=_norm_fwd_epilogue(w["norm_ffn"]))
    gg = _matmul(fn, w["wg"], "nt", BF16, "ffn_gate_fwd", b_lead="p")
    uu = _matmul(fn, w["wu"], "nt", BF16, "ffn_up_fwd", b_lead="p")
    act, gathered = _conv_fwd(gg, uu, w["cw"], w["cb"], _gather_plan([late_shards[n] for n in LAST]))
    w["wd"] = gathered[0]
    def loss_fn(y, rows, consts):
        err = y - rows[0]
        part = 0.5 * jnp.sum(jnp.sum(err * err, axis=1, keepdims=True) * (1.0 / D_MODEL), axis=0, keepdims=True)
        return [err * (1.0 / D_MODEL)], [jnp.broadcast_to(part, (1, LANES))]

    dy, loss = _matmul(act, w["wd"], "nn", F32, "ffn_down_fwd_loss", residual=h2, a_lead="k", b_lead="k",
                       epilogue=_Epilogue(loss_fn, [_row(target)], [], [(D_MODEL, F32)], [(1, LANES)]))

    g["ffn_w_down"] = _matmul(act, dy, "tn", BF16, "ffn_down_dw", a_lead="p")
    dact = _matmul(dy, w["wd"], "nt", BF16, "ffn_down_dx", b_lead="p")
    duu, dgg, g["ffn_conv_w"], g["ffn_conv_b"] = _conv_bwd(gg, uu, dact, w["cw"], w["cb"])
    g["ffn_w_gate"] = _matmul(dgg, fn, "tn", BF16, "ffn_gate_dw", a_lead="p")
    g["ffn_w_up"] = _matmul(duu, fn, "tn", BF16, "ffn_up_dw", a_lead="p")
    dh2, g["norm_ffn"] = _matmul(dgg, w["wg"], "nn", F32, "ffn_dx_norm_bwd", a_lead="k", b_lead="k", more=(duu, w["wu"]),
                                 epilogue=_norm_bwd_epilogue(h2, w["norm_ffn"], dy))

    g["xa_w_o"] = _matmul(xo, dh2, "tn", BF16, "xa_o_dw")
    def xa_bwd(dxo_, rows, k):
        kvh = _heads(k[0], 2 * XA_HEADS)
        dq_, dk_, dv_ = [], [], []
        dqn, dkn = 0.0, 0.0
        for h, (a, d_) in enumerate(zip(_heads(rows[0], XA_HEADS), _heads(dxo_, XA_HEADS))):
            _, vjp = jax.vjp(_xa_head, a, kvh[h], kvh[XA_HEADS + h], k[1], k[2])
            ga, gk, gv, gqn, gkn = vjp(d_)
            dq_.append(ga)
            dk_.append(gk)
            dv_.append(gv)
            dqn, dkn = dqn + gqn, dkn + gkn
        return [_cat(dq_)], [_cat(dk_ + dv_), dqn, dkn]

    dxq, dxkv, g["xa_q_norm"], g["xa_k_norm"] = _matmul(dh2, w["xo"], "nt", F32, "xa_o_dx_attn_bwd", epilogue=_Epilogue(
        xa_bwd, [_row(xq)], [xkv, w["xa_q_norm"], w["xa_k_norm"]], [(512, BF16)], [xkv.shape, (1, 128), (1, 128)]))
    g["xa_w_q"] = _matmul(hn, dxq, "tn", BF16, "xa_q_dw")
    dh1, g["norm_xa"] = _matmul(dxq, w["xq"], "nt", F32, "xa_q_dx_norm_bwd",
                                epilogue=_norm_bwd_epilogue(h1, w["norm_xa"], dh2))
    g["xa_w_kv"] = _matmul(mn, dxkv, "tn", BF16, "xa_kv_dw")
    dmn = _matmul(dxkv, w["xkv"], "nt", F32, "xa_kv_dx")
    _, g["norm_mem"] = _norm_bwd(mem, w["norm_mem"], dmn, dmn, "norm_mem_bwd")

    g["w_out"] = _matmul(cat, dh1, "tn", BF16, "out_dw")
    def mix_bwd(dcat_, rows, k):
        do_, dog_ = [], []
        dgn = 0.0
        for o, g_, d_ in zip(_heads(rows[0], GLA_HEADS), _heads(rows[1], GLA_HEADS), _heads(dcat_, GLA_HEADS)):
            _, vjp = jax.vjp(_mix_head, o, g_, k[0])
            a, b, gn_ = vjp(d_)
            do_.append(a)
            dog_.append(b)
            dgn = dgn + gn_
        return [_cat(do_), _cat(dog_), dcat_[:, 512:]], [dgn]

    do_gla, d_og, do_mla, g["gla_out_norm"] = _matmul(dh1, w["out"], "nt", F32, "out_dx_mix_bwd", epilogue=_Epilogue(
        mix_bwd, [_row(o_gla), og], [w["gla_out_norm"]], [(512, F32), (512, BF16), (512, F32)], [(1, 128)]))

    late_parts = _late_grad_shards(g)
    dq_r, dk_r, dv_mla, lands_late = _attn_bwd(q_r, k_r, v_mla, o_mla, lse, do_mla,
                                               _scatter_plan([late_parts[n] for n in LATE]))
    lands_late = dict(zip(LATE, lands_late, strict=True))

    def qk_bwd(r, k):
        q_up, k_up = _dg(r[0], k[0], _NN), _dg(r[1], k[1], _NN)
        dqs, dks = [], []
        dkpe, dqn, dkn = 0.0, 0.0, 0.0
        for qh, kh, dqh, dkh in zip(_heads(q_up, MLA_HEADS), _heads(k_up, MLA_HEADS), _heads(r[6], MLA_HEADS),
                                    _heads(r[7], MLA_HEADS)):
            _, vjp = jax.vjp(lambda a, b, e, f, h_: _qk_head(a, b, e, r[3], r[4], r[5], f, h_), qh, kh, r[2], k[3], k[4])
            ga, gb, ge, gf, gh = vjp((dqh, dkh))
            dqs.append(ga)
            dks.append(gb)
            dkpe, dqn, dkn = dkpe + ge, dqn + gf, dkn + gh
        dq_up, dk_up, dv = _cat(dqs), _cat(dks), r[8]
        dq_lat_ = _dg(dq_up, k[0], _NT)
        dkv_lat_ = _dg(dk_up, k[1], _NT) + _dg(dv, k[2], _NT)
        return [dq_lat_, dkv_lat_, dkpe], [dqn, dkn, _dg(r[0], dq_up, _TN), _dg(r[1], dk_up, _TN), _dg(r[1], dv, _TN)]

    dq_lat, dkv_lat, d_kpe, g["q_norm"], g["k_norm"], g["uq"], g["k"], g["v"] = _rows_call(
        qk_bwd, [_row(q_lat), _row(kv_lat), kpe] + tabs + [_row(dq_r), _row(dk_r), _row(dv_mla)], qk_consts,
        [(MLA_Q_RANK, F32), (MLA_KV_RANK, F32), (128, BF16)],
        [(1, 128), (1, 128), w["uq"].shape, w["k"].shape, w["v"].shape], name="mla_qk_bwd")

    dgq, dgk, dla, dgv = _gla_bwd(proj, la, states, do_gla)

    def dproj_body(r, k):
        alr_, cq_, ckv_, dla_, dq_lat_, dkv_lat_, dgq_, dgk_, dgv_, d_og_, d_kpe_ = r
        _, gate_vjp = jax.vjp(_gate_fn, alr_, k[0], k[1])
        d_alr, gw2, gb = gate_vjp(dla_)
        _, q_vjp = jax.vjp(_rms, cq_, k[2])
        _, kv_vjp = jax.vjp(_rms, ckv_, k[3])
        d_cq, gqa = q_vjp(dq_lat_)
        d_ckv, gkva = kv_vjp(dkv_lat_)
        pieces = [dgq_, dgk_, dgv_, d_og_, d_cq, d_ckv, d_kpe_, d_alr]
        return [_cat([x_.astype(BF16) for x_ in pieces])], [gw2, gb, gqa, gkva]

    dproj, g["w2"], g["gla_gate_b"], g["mla_q_a_norm"], g["mla_kv_a_norm"] = _rows_call(
        dproj_body, [alr, cq, ckv, _row(dla), _row(dq_lat), _row(dkv_lat), _row(dgq), _row(dgk), _row(dgv), _row(d_og),
                     _row(d_kpe)], [w["w2"], w["gate_b"], w["q_a_norm"], w["kv_a_norm"]], [(P_WIDTH, BF16)],
        [(128, 256), (1, 256), (1, 256), (1, 128)], name="proj_cotangent")
    g["in"] = _matmul(dproj, xn, "tn", BF16, "proj_dw")
    dx, g["norm_mix"] = _matmul(dproj, w["in"], "nn", F32, "proj_dx_norm_bwd",
                                epilogue=_norm_bwd_epilogue(x, w["norm_mix"], dh1))
    return loss[0, 0], dx, g, lands_late


def _join_shards(pieces, axis):
    if axis == 0:
        return pieces.reshape(-1, pieces.shape[2])
    return jnp.transpose(pieces, (1, 0, 2)).reshape(pieces.shape[1], -1)


def _split_shards(full, axis):
    r, c = full.shape
    if axis == 0:
        return full.reshape(4, r // 4, c)
    return jnp.transpose(full.reshape(r, 4, c // 4), (1, 0, 2))


def _early_layout(gath, rep):
    w_in = gath["w_in"].reshape(N_WIDTH, D_MODEL)
    z = lambda n: jnp.zeros((n, D_MODEL), w_in.dtype)
    seg = lambda lo, n: w_in[lo:lo + n]
    ukv = _join_shards(gath["mla_w_ukv"], 1).reshape(MLA_KV_RANK, MLA_HEADS, MLA_NOPE + MLA_V)
    w = {
        "in": jnp.concatenate([seg(N_GQ, 256), seg(N_GK, 256), seg(N_GV, 512), seg(N_OG, 512), seg(N_CQ, 256),
                               seg(N_CKV, 128), z(64), seg(N_KPE, 32), z(32), seg(N_ALR, 16), z(112)], axis=0),
        "uq": jnp.pad(_join_shards(gath["mla_w_uq"], 1).reshape(MLA_Q_RANK, MLA_HEADS, MLA_QK),
                      ((0, 0), (0, 0), (0, LANES - MLA_QK))).reshape(MLA_Q_RANK, MLA_HEADS * LANES),
        "k": jnp.pad(ukv[:, :, :MLA_NOPE], ((0, 0), (0, 0), (0, LANES - MLA_NOPE))).reshape(MLA_KV_RANK, -1),
        "v": ukv[:, :, MLA_NOPE:].reshape(MLA_KV_RANK, MLA_HEADS * MLA_V),
        "w2": jnp.pad(_join_shards(gath["gla_gate_w2"], 1), ((0, LANES - GLA_RANK), (0, 0))),
        "cb": rep["ffn_conv_b"].reshape(4, 1, D_FF // 4),
        "q_norm": jnp.pad(rep["mla_q_norm"], ((0, 0), (0, LANES - MLA_QK))),
        "k_norm": jnp.pad(rep["mla_k_norm"], ((0, 0), (0, LANES - MLA_QK))),
        "q_a_norm": rep["mla_q_a_norm"], "kv_a_norm": rep["mla_kv_a_norm"], "gate_b": rep["gla_gate_b"],
    }
    for n in ("norm_mix", "gla_out_norm", "norm_xa", "norm_mem", "xa_q_norm", "xa_k_norm", "norm_ffn"):
        w[n] = rep[n]
    return w


def _late_layout(gath):
    return {"out": _join_shards(gath["w_out"], 0), "xq": _join_shards(gath["xa_w_q"], 0),
            "xkv": _join_shards(gath["xa_w_kv"], 0), "xo": _join_shards(gath["xa_w_o"], 1),
            "wg": gath["ffn_w_gate"], "wu": gath["ffn_w_up"], "cw": gath["ffn_conv_w"]}


def _late_grad_shards(g):
    sh = {"w_out": _split_shards(g["w_out"], 0), "xa_w_q": _split_shards(g["xa_w_q"], 0),
          "xa_w_kv": _split_shards(g["xa_w_kv"], 0), "xa_w_o": _split_shards(g["xa_w_o"], 1),
          "ffn_w_gate": g["ffn_w_gate"], "ffn_w_up": g["ffn_w_up"], "ffn_conv_w": g["ffn_conv_w"],
          "ffn_w_down": g["ffn_w_down"]}
    return {n: v.astype(BF16) for n, v in sh.items()}


def _early_grad_shards(g):
    gi = g["in"]
    seg = lambda lo, n: gi[lo:lo + n]
    w_in = jnp.concatenate([seg(P_GQ, 256), seg(P_GK, 256), seg(P_GV, 512), seg(P_ALR, 16), seg(P_OG, 512),
                            seg(P_CQ, 256), seg(P_CKV, 128), seg(P_KPE + 64, 32)], axis=0)
    uq = g["uq"].reshape(MLA_Q_RANK, MLA_HEADS, LANES)[:, :, :MLA_QK].reshape(MLA_Q_RANK, -1)
    ukv = jnp.concatenate([g["k"].reshape(MLA_KV_RANK, MLA_HEADS, LANES)[:, :, :MLA_NOPE],
                           g["v"].reshape(MLA_KV_RANK, MLA_HEADS, MLA_V)], axis=2).reshape(MLA_KV_RANK, -1)
    sh = {"w_in": w_in.reshape(4, N_WIDTH // 4, D_MODEL), "gla_gate_w2": _split_shards(g["w2"][:GLA_RANK], 1),
          "mla_w_uq": _split_shards(uq, 1), "mla_w_ukv": _split_shards(ukv, 1)}
    sh = {n: v.astype(BF16) for n, v in sh.items()}
    rep = {n: g[n] for n in REPLICATED if n in g}
    rep["mla_q_norm"] = g["q_norm"][:, :MLA_QK]
    rep["mla_k_norm"] = g["k_norm"][:, :MLA_QK]
    rep["ffn_conv_b"] = g["ffn_conv_b"].reshape(1, D_FF)
    return sh, rep


SMALL_SHAPE = (8, 1024)


def _pack_small(vectors):
    flat = jnp.concatenate(vectors, axis=1)
    return jnp.pad(flat, ((0, 0), (0, SMALL_SHAPE[0] * SMALL_SHAPE[1] - flat.shape[1]))).reshape(SMALL_SHAPE)


def _unpack_small(buf, widths):
    flat = buf.reshape(1, -1)
    out, off = [], 0
    for wd in widths:
        out.append(flat[:, off:off + wd])
        off += wd
    return out


ANY = pl.BlockSpec(memory_space=pl.ANY)


def _place():
    x, y, c = lax.axis_index("x"), lax.axis_index("y"), lax.axis_index("c")
    chips = [(1 - x, y), (x, 1 - y), (1 - x, 1 - y)]
    return x, y, c, chips


class _Comm:
    def __init__(self, ins, out_shape, sems, start, finish, mid=None):
        self.ins, self.out_shape, self.sems = list(ins), list(out_shape), list(sems)
        self.start, self.finish, self.mid = start, finish, mid or (lambda *args: None)


def _run_comm(plan, name):
    ni, no = len(plan.ins), len(plan.out_shape)

    def body(*refs):
        ins, outs, sems = refs[:ni], refs[ni:ni + no], refs[ni + no:]
        place = _place()
        plan.start(place, ins, outs, sems)
        plan.mid(place, ins, outs, sems)
        plan.finish(place, ins, outs, sems)

    return pl.pallas_call(body, in_specs=[ANY] * ni, out_specs=[ANY] * no, out_shape=plan.out_shape,
                          scratch_shapes=plan.sems, name=name)(*plan.ins)


def _gather_plan(shards):
    n = len(shards)
    by_rows = [s.shape[0] % (2 * BF16_ROWS) == 0 for s in shards]
    by_cols = [not r and s.shape[1] % (2 * LANES) == 0 for r, s in zip(by_rows, shards)]
    split = [r or c for r, c in zip(by_rows, by_cols)]

    def rows(ref, t, c):
        if by_rows[t]:
            half = shards[t].shape[0] // 2
            return ref.at[pl.ds(pl.multiple_of(c * half, BF16_ROWS), half)]
        if by_cols[t]:
            half = shards[t].shape[1] // 2
            return ref.at[:, pl.ds(pl.multiple_of(c * half, LANES), half)]
        return ref

    def remote(src, dst, ss, rs, to):
        return pltpu.make_async_remote_copy(src_ref=src, dst_ref=dst, send_sem=ss, recv_sem=rs, device_id=to,
                                            device_id_type=MESH)

    def first_wave(place, ins, outs, sems):
        x, y, c, chips = place
        ici_s, ici_r, _, _, local = sems
        me = 2 * x + y
        own = [pltpu.make_async_copy(ins[t], outs[t].at[me], local.at[t]) for t in range(n)]
        push = [remote(rows(ins[t], t, c), rows(outs[t].at[me], t, c), ici_s.at[3 * t + j], ici_r.at[3 * t + j], (px, py, c))
                for t in range(n) for j, (px, py) in enumerate(chips)]
        return own, push

    def second_wave(place, ins, outs, sems, last):
        x, y, c, chips = place
        ici_s, ici_r, d2d_s, d2d_r, local = sems
        sib = (x, y, 1 - c)
        out = []
        for t in range(n):
            for j, (px, py) in enumerate(chips):
                block = outs[t].at[2 * px + py]
                got = rows(block, t, c)
                if split[t]:
                    hand = remote(got, got, d2d_s.at[3 * t + j], d2d_r.at[3 * t + j], sib)
                    theirs = rows(block, t, 1 - c)
                    other = (remote(theirs, theirs, local.at[0], d2d_r.at[3 * t + j], sib) if last else
                             remote(got, got, local.at[0], ici_r.at[3 * t + j], sib))
                    out.append((other, hand))
                elif last:
                    out.append((remote(got, got, local.at[0], ici_r.at[3 * t + j], sib), None))
        return out

    def start(place, ins, outs, sems):
        own, push = first_wave(place, ins, outs, sems)
        for cp in own + push:
            cp.start()

    def mid(place, ins, outs, sems):
        for arrival, hand in second_wave(place, ins, outs, sems, False):
            arrival.wait_recv()
            hand.start()

    def finish(place, ins, outs, sems):
        own, push = first_wave(place, ins, outs, sems)
        for arrival, hand in second_wave(place, ins, outs, sems, True):
            arrival.wait_recv()
            if hand is not None:
                hand.wait_send()
        for cp in push:
            cp.wait_send()
        for cp in own:
            cp.wait()

    dma = pltpu.SemaphoreType.DMA
    return _Comm(shards, [jax.ShapeDtypeStruct((4,) + s.shape, s.dtype) for s in shards],
                 [dma((3 * n,)), dma((3 * n,)), dma((3 * n,)), dma((3 * n,)), dma((n,))], start, finish, mid)


def _scatter_plan(parts, small=None):
    n = len(parts)
    ns = 0 if small is None else 1

    def unpack(place, ins, outs, sems):
        x, y, c, chips = place
        return x, y, c, chips, 2 * x + y, 4 * x + 2 * y + c, (x, y, 1 - c)

    def remote(src, dst, ss, rs, to):
        return pltpu.make_async_remote_copy(src_ref=src, dst_ref=dst, send_sem=ss, recv_sem=rs, device_id=to,
                                            device_id_type=MESH)

    def first_wave(place, ins, outs, sems):
        x, y, c, chips, me, dev, sib = unpack(place, ins, outs, sems)
        ici_s, ici_r, d2d_s, d2d_r, sm_s, sm_r, local = sems
        own, push = [], []
        if ns:
            own.append(pltpu.make_async_copy(ins[n], outs[n].at[dev], local.at[n]))
            for k in range(1, 8):
                px = (1 - x) if (k >> 2) & 1 else x
                py = (1 - y) if (k >> 1) & 1 else y
                pc = (1 - c) if k & 1 else c
                push.append(remote(ins[n], outs[n].at[dev], sm_s.at[k - 1], sm_r.at[k - 1], (px, py, pc)))
        for t in range(n):
            own.append(pltpu.make_async_copy(ins[t].at[me], outs[t].at[dev], local.at[t]))
            push.append(remote(ins[t].at[me], outs[t].at[dev], d2d_s.at[4 * t], d2d_r.at[4 * t], sib))
            for j, (px, py) in enumerate(chips):
                push.append(remote(ins[t].at[2 * px + py], outs[t].at[dev], ici_s.at[3 * t + j], ici_r.at[3 * t + j],
                                   (px, py, c)))
        return own, push

    def start(place, ins, outs, sems):
        own, push = first_wave(place, ins, outs, sems)
        for cp in own + push:
            cp.start()

    def landed(dst, rs, sems, sib):
        remote(dst, dst, sems[-1].at[0], rs, sib).wait_recv()

    def forwards(place, ins, outs, sems):
        x, y, c, chips, me, dev, sib = unpack(place, ins, outs, sems)
        d2d_s, d2d_r = sems[2], sems[3]
        slots = [(t, j, outs[t].at[4 * px + 2 * py + c]) for t in range(n) for j, (px, py) in enumerate(chips)]
        return [(t, j, slot, remote(slot, slot, d2d_s.at[4 * t + 1 + j], d2d_r.at[4 * t + 1 + j], sib))
                for t, j, slot in slots]

    def mid(place, ins, outs, sems):
        sib = unpack(place, ins, outs, sems)[-1]
        for t, j, slot, cp in forwards(place, ins, outs, sems):
            landed(slot, sems[1].at[3 * t + j], sems, sib)
            cp.start()

    def finish(place, ins, outs, sems):
        x, y, c, chips, me, dev, sib = unpack(place, ins, outs, sems)
        d2d_r, sm_r = sems[3], sems[5]
        own, push = first_wave(place, ins, outs, sems)
        push += [cp for _, _, _, cp in forwards(place, ins, outs, sems)]
        for t in range(n):
            landed(outs[t].at[4 * x + 2 * y + (1 - c)], d2d_r.at[4 * t], sems, sib)
            for j, (px, py) in enumerate(chips):
                landed(outs[t].at[4 * px + 2 * py + (1 - c)], d2d_r.at[4 * t + 1 + j], sems, sib)
        if ns:
            for k in range(1, 8):
                px = (1 - x) if (k >> 2) & 1 else x
                py = (1 - y) if (k >> 1) & 1 else y
                pc = (1 - c) if k & 1 else c
                landed(outs[n].at[4 * px + 2 * py + pc], sm_r.at[k - 1], sems, sib)
        for cp in push:
            cp.wait_send()
        for cp in own:
            cp.wait()

    dma = pltpu.SemaphoreType.DMA
    ins = list(parts) + ([small] if ns else [])
    out_shape = [jax.ShapeDtypeStruct((8,) + p.shape[1:], p.dtype) for p in parts]
    if ns:
        out_shape.append(jax.ShapeDtypeStruct((8,) + small.shape, small.dtype))
    return _Comm(ins, out_shape, [dma((3 * n,)), dma((3 * n,)), dma((4 * n,)), dma((4 * n,)), dma((7,)), dma((7,)),
                                  dma((n + 1,))], start, finish, mid)


ADAM_ROWS = 288


def _row_tile(r, cap):
    if r <= cap:
        return r
    return max((t for t in range(8, cap + 1, 8) if r % t == 0), default=r)


def _adamw_update(w, m, v, land):
    g = land[0].astype(F32)
    for i in range(1, 8):
        g = g + land[i].astype(F32)
    m_new = ADAM_B1 * m + (1.0 - ADAM_B1) * g
    v_new = ADAM_B2 * v + (1.0 - ADAM_B2) * (g * g)
    m_hat = m_new / (1.0 - ADAM_B1 ** ADAM_STEP)
    v_hat = v_new / (1.0 - ADAM_B2 ** ADAM_STEP)
    return g, -ADAM_LR * (m_hat / (jnp.sqrt(v_hat) + ADAM_EPS) + ADAM_WD * w), m_new, v_new


def _adamw(tensors, name, comm=None):
    k = len(tensors)
    r, c = tensors[0][0].shape
    t = _row_tile(r, ADAM_ROWS // k)
    tc = c if t < r or r <= ADAM_ROWS else 2 * LANES
    n = (r // t) * (c // tc)
    nci, nco, nsem = (len(comm.ins), len(comm.out_shape), len(comm.sems)) if comm else (0, 0, 0)

    def kern(*refs):
        ins, cins, outs, couts, csems = _split_refs(refs, (4 * k, nci, 4 * k, nco, nsem))
        if comm:
            place = _place()

            @pl.when(pl.program_id(0) == 0)
            def _():
                comm.start(place, cins, couts, csems)

        for i in range(k):
            w_ref, m_ref, v_ref, l_ref = ins[4 * i:4 * i + 4]
            res = _adamw_update(w_ref[...], m_ref[...], v_ref[...], l_ref)
            for ref, val in zip(outs[4 * i:4 * i + 4], res, strict=True):
                ref[...] = val
        if comm:
            @pl.when(pl.program_id(0) == n - 1)
            def _():
                comm.mid(place, cins, couts, csems)
                comm.finish(place, cins, couts, csems)

    where = (lambda i: (i, 0)) if tc == c else (lambda i: (0, i))
    spec = pl.BlockSpec((t, tc), where)
    lspec = pl.BlockSpec((8, t, tc), lambda i: (0,) + where(i))
    res = pl.pallas_call(
        kern, grid=(n,), in_specs=[spec, spec, spec, lspec] * k + [ANY] * nci, out_specs=[spec] * (4 * k) + [ANY] * nco,
        out_shape=[jax.ShapeDtypeStruct((r, c), F32)] * (4 * k) + (comm.out_shape if comm else []),
        scratch_shapes=comm.sems if comm else [],
        compiler_params=pltpu.CompilerParams(dimension_semantics=("arbitrary" if comm else "parallel",),
                                             vmem_limit_bytes=VMEM_LIMIT),
        name=name)(*[x for tens in tensors for x in tens], *(comm.ins if comm else []))
    return [res[4 * i:4 * i + 4] for i in range(k)], res[4 * k:]


def _step(a):
    def sq(n):
        v = a[n][0] if a[n].ndim == 3 else a[n]
        return v.T if n.removeprefix("m_").removeprefix("v_") in TRANSPOSED else v

    payload = lambda n: sq(n) if n in EXACT_GATHER else sq(n).astype(BF16)

    loss, dx, g, lands_late = _local_step(sq("x"), sq("mem"), a["positions"][0], sq("loss_target"),
                                          {n: a[n] for n in REPLICATED}, [payload(n) for n in EARLY],
                                          {n: payload(n) for n in LATE})

    sh, rep = _early_grad_shards(g)
    small = _pack_small([rep[n] for n in REPLICATED] + [loss.reshape(1, 1)])
    *lands_early, land_small = _run_comm(_scatter_plan([sh[n] for n in EARLY], small), "scatter_last")
    quad = lambda n, land: (sq(n), sq("m_" + n), sq("v_" + n), land)
    lands = dict(zip(EARLY, lands_early, strict=True)) | lands_late

    outs = {}
    kinds = ("grad_", "delta_", "new_m_", "new_v_")
    for n, _ in SHARDED:
        res = _adamw([quad(n, lands[n])], "adamw_" + n)[0][0]
        for kind, val in zip(kinds, res, strict=True):
            outs[kind + n] = (val.T if n in TRANSPOSED else val).reshape(a[n].shape)
    zero = jnp.zeros((1, 1), F32)
    packed = [_pack_small([a[p + n] for n in REPLICATED] + [zero]) for p in ("", "m_", "v_")]
    res = _adamw([(*packed, land_small)], "adamw_replicated")[0][0]
    widths = [a[n].shape[1] for n in REPLICATED] + [1]
    for kind, buf in zip(kinds, res, strict=True):
        *vals, total = _unpack_small(buf, widths)
        for n, val in zip(REPLICATED, vals, strict=True):
            outs[kind + n] = val
        if kind == "grad_":
            loss = total[0, 0]

    ordered = [outs[kind + n] for kind in kinds for n in WEIGHTS]
    return (loss, dx[None], *ordered)


def kernel(x, mem, positions, norm_mix, w_in, gla_gate_w2, gla_gate_b, gla_out_norm, mla_q_a_norm, mla_w_uq, mla_kv_a_norm, mla_w_ukv, mla_q_norm, mla_k_norm, w_out, norm_xa, norm_mem, xa_w_q, xa_w_kv, xa_q_norm, xa_k_norm, xa_w_o, norm_ffn, ffn_w_gate, ffn_w_up, ffn_conv_w, ffn_conv_b, ffn_w_down, loss_target, m_norm_mix, m_w_in, m_gla_gate_w2, m_gla_gate_b, m_gla_out_norm, m_mla_q_a_norm, m_mla_w_uq, m_mla_kv_a_norm, m_mla_w_ukv, m_mla_q_norm, m_mla_k_norm, m_w_out, m_norm_xa, m_norm_mem, m_xa_w_q, m_xa_w_kv, m_xa_q_norm, m_xa_k_norm, m_xa_w_o, m_norm_ffn, m_ffn_w_gate, m_ffn_w_up, m_ffn_conv_w, m_ffn_conv_b, m_ffn_w_down, v_norm_mix, v_w_in, v_gla_gate_w2, v_gla_gate_b, v_gla_out_norm, v_mla_q_a_norm, v_mla_w_uq, v_mla_kv_a_norm, v_mla_w_ukv, v_mla_q_norm, v_mla_k_norm, v_w_out, v_norm_xa, v_norm_mem, v_xa_w_q, v_xa_w_kv, v_xa_q_norm, v_xa_k_norm, v_xa_w_o, v_norm_ffn, v_ffn_w_gate, v_ffn_w_up, v_ffn_conv_w, v_ffn_conv_b, v_ffn_w_down):
    return _step(dict(locals()))
```

```python
import functools

import jax
import jax.numpy as jnp
import numpy as np
from jax import lax
from jax.experimental import pallas as pl
from jax.experimental.pallas import tpu as pltpu

F32, BF16 = jnp.float32, jnp.bfloat16
MESH = pl.DeviceIdType.MESH

D_MODEL = 1024
EPS = 1e-6
GLA_HEADS, GLA_DK, GLA_DV, GLA_RANK, GLA_CHUNK = 4, 64, 128, 16, 64
GLA_GATE_NORM = 16.0
MLA_HEADS, MLA_Q_RANK, MLA_KV_RANK, MLA_NOPE, MLA_ROPE, MLA_V = 8, 256, 128, 64, 32, 64
MLA_QK = MLA_NOPE + MLA_ROPE
ROPE_THETA = 10000.0
LOG2E, LN2 = 1.4426950408889634, 0.6931471805599453
XA_HEADS, XA_DIM = 4, 128
D_FF = 2816
ADAM_LR, ADAM_B1, ADAM_B2, ADAM_EPS, ADAM_WD, ADAM_STEP = 0.001, 0.9, 0.999, 1e-08, 0.01, 10

LANES = 128
BF16_ROWS = 16
VMEM_LIMIT = 56 * 1024 * 1024
MATMUL_VMEM = 44 * 1024 * 1024
ROW_TILE = 512

P_GQ, P_GK, P_GV, P_OG, P_CQ, P_CKV, P_KPE, P_ALR, P_WIDTH = 0, 256, 512, 1024, 1536, 1792, 1920, 2048, 2176
N_GQ, N_GK, N_GV, N_ALR, N_OG, N_CQ, N_CKV, N_KPE, N_WIDTH = 0, 256, 512, 1024, 1040, 1552, 1808, 1936, 1968

SHARDED = (("w_in", 1), ("gla_gate_w2", 1), ("mla_w_uq", 1), ("mla_w_ukv", 1), ("w_out", 0), ("xa_w_q", 0),
           ("xa_w_kv", 0), ("xa_w_o", 1), ("ffn_w_gate", 1), ("ffn_w_up", 1), ("ffn_conv_w", 1), ("ffn_w_down", 0))
REPLICATED = ("norm_mix", "gla_gate_b", "gla_out_norm", "mla_q_a_norm", "mla_kv_a_norm", "mla_q_norm", "mla_k_norm",
              "norm_xa", "norm_mem", "xa_q_norm", "xa_k_norm", "norm_ffn", "ffn_conv_b")
EXACT_GATHER = ("gla_gate_w2", "ffn_conv_w")
TRANSPOSED = ("w_in", "ffn_w_gate", "ffn_w_up")
EARLY = ("w_in", "gla_gate_w2", "mla_w_uq", "mla_w_ukv")
LATE = tuple(n for n, _ in SHARDED if n not in EARLY)
LAST = ("ffn_w_down",)
WEIGHTS = ("norm_mix", "w_in", "gla_gate_w2", "gla_gate_b", "gla_out_norm", "mla_q_a_norm", "mla_w_uq",
           "mla_kv_a_norm", "mla_w_ukv", "mla_q_norm", "mla_k_norm", "w_out", "norm_xa", "norm_mem", "xa_w_q",
           "xa_w_kv", "xa_q_norm", "xa_k_norm", "xa_w_o", "norm_ffn", "ffn_w_gate", "ffn_w_up", "ffn_conv_w",
           "ffn_conv_b", "ffn_w_down")


_NN = ((1,), (0,))
_NT = ((1,), (1,))
_TN = ((0,), (0,))


def _dg(a, b, dims):
    return lax.dot_general(a.astype(BF16), b.astype(BF16), (dims, ((), ())), preferred_element_type=F32)


@jax.custom_vjp
def _dot_nn(a, b):
    return _dg(a, b, _NN)


_dot_nn.defvjp(lambda a, b: (_dg(a, b, _NN), (a, b)),
               lambda r, g: (_dg(g, r[1], _NT).astype(r[0].dtype), _dg(r[0], g, _TN).astype(r[1].dtype)))


@jax.custom_vjp
def _dot_nt(a, b):
    return _dg(a, b, _NT)


_dot_nt.defvjp(lambda a, b: (_dg(a, b, _NT), (a, b)),
               lambda r, g: (_dg(g, r[1], _NN).astype(r[0].dtype), _dg(g, r[0], _TN).astype(r[1].dtype)))


@jax.custom_vjp
def _dot_tn(a, b):
    return _dg(a, b, _TN)


_dot_tn.defvjp(lambda a, b: (_dg(a, b, _TN), (a, b)),
               lambda r, g: (_dg(r[1], g, _NT).astype(r[0].dtype), _dg(r[0], g, _NN).astype(r[1].dtype)))


def _rms(x, w, n=None):
    n = x.shape[-1] if n is None else n
    ms = jnp.sum(x * x, axis=-1, keepdims=True) * (1.0 / n)
    return x * lax.rsqrt(ms + EPS) * w


def _silu(x):
    return x * jax.nn.sigmoid(x)


def _log_sigmoid(x):
    return jnp.minimum(x, 0.0) - jnp.log(1.0 + jnp.exp(-jnp.abs(x)))


@jax.custom_vjp
def _cumsum_rows(x):
    n = x.shape[0]
    row = lax.broadcasted_iota(jnp.int32, x.shape, 0)
    k = 1
    while k < n:
        x = x + jnp.where(row >= k, pltpu.roll(x, k, 0), 0.0)
        k *= 2
    return x


def _cumsum_rows_bwd(_, g):
    n = g.shape[0]
    row = lax.broadcasted_iota(jnp.int32, g.shape, 0)
    k = 1
    while k < n:
        g = g + jnp.where(row < n - k, pltpu.roll(g, n - k, 0), 0.0)
        k *= 2
    return (g,)


_cumsum_rows.defvjp(lambda x: (_cumsum_rows(x), None), _cumsum_rows_bwd)


def _lane_mask(lo, hi):
    lane = lax.broadcasted_iota(jnp.int32, (1, LANES), 1)
    return ((lane >= lo) & (lane < hi)).astype(F32)


def _tile(n, t):
    t = min(n, t)
    assert n % t == 0, (n, t)
    return t


class _Epilogue:
    def __init__(self, fn, rows=(), consts=(), outs=(), accs=()):
        self.fn, self.rows, self.consts, self.outs, self.accs = fn, list(rows), list(consts), list(outs), list(accs)


def _matmul(a, b, mode, out_dtype, name, residual=None, a_lead=None, b_lead=None, more=None, epilogue=None):
    (a0, a1), (b0, b1) = a.shape[-2:], b.shape[-2:]
    if mode == "nn":
        m, k, k2, n = a0, a1, b0, b1
    elif mode == "nt":
        m, k, n, k2 = a0, a1, b0, b1
    else:
        k, m, k2, n = a0, a1, b0, b1
    assert k == k2, (a.shape, b.shape, mode)
    npar = 4 if "p" in (a_lead, b_lead) else 1
    nsum = 4 if "k" in (a_lead, b_lead) else 1
    pairs = [(a, b)] + ([more] if more else [])
    a_item, b_item, o_item = a.dtype.itemsize, b.dtype.itemsize, jnp.dtype(out_dtype).itemsize
    ep = epilogue
    row_extra = 4 if residual is not None else 0
    if ep:
        row_extra += (sum(r.dtype.itemsize * wd for r, wd, _ in ep.rows) + sum(jnp.dtype(d).itemsize * wd for wd, d in ep.outs)) / n

    def resident(lead, tiles):
        return lead != "p" and tiles == 1

    def vmem_need(tm, tn, tk):
        a_bufs = 1 if resident(a_lead, (m // tm) * (k // tk)) else 2
        b_bufs = 1 if resident(b_lead, (n // tn) * (k // tk)) else 2
        need = a_bufs * (nsum if a_lead == "k" else 1) * tm * tk * a_item + b_bufs * (nsum if b_lead == "k" else 1) * tk * tn * b_item
        need *= len(pairs)
        need += (0 if ep else 2 * tm * tn * o_item) + tm * tn * 4 * (2 if tk < k else 1)
        need += tm * tk * 2 * (a_item == 4 or mode == "tn") + tk * tn * 2 * (b_item == 4)
        return need + int(2 * tm * tn * row_extra) + (3 * tm * tn * 4 if ep else 0)

    halvings = (4096, 2048, 1024, 512, 256, 128, 64, 32, 16, 8)
    if mode == "tn":
        tm = m if m <= 2304 else m // 2
        tn = n if tm * n <= 1024 * 2304 else n // 2
        tk = next((r for r in halvings if k % r == 0 and vmem_need(tm, tn, r) <= MATMUL_VMEM), k)
    else:
        tn, tk = n, k
        tm = next((r for r in halvings if m % r == 0 and vmem_need(r, tn, tk) <= MATMUL_VMEM), m)
    assert m % tm == 0 and n % tn == 0 and k % tk == 0
    assert ep is None or (tn == n and tk == k and npar == 1)
    nk = k // tk
    dims = {"nn": _NN, "nt": _NT, "tn": _TN}[mode]
    n_in = 2 * len(pairs) + (residual is not None)
    n_ep_in = len(ep.rows) + len(ep.consts) if ep else 0
    n_out = len(ep.outs) + len(ep.accs) if ep else 1

    def body(*refs):
        ab, rs, ep_in, outs, scratch = _split_refs(refs, (2 * len(pairs), n_in - 2 * len(pairs), n_ep_in, n_out, nk > 1))
        prod = None
        for a_ref, b_ref in zip(ab[0::2], ab[1::2]):
            for sh in range(nsum):
                term = _dg(a_ref[sh] if a_lead == "k" else a_ref[...], b_ref[sh] if b_lead == "k" else b_ref[...], dims)
                prod = term if prod is None else prod + term

        def finish(r):
            if rs:
                r = r + rs[0][...]
            if ep is None:
                outs[0][...] = r.astype(outs[0].dtype)
                return
            vals = [x[...] for x in ep_in]
            ro, ao = ep.fn(r, vals[:len(ep.rows)], vals[len(ep.rows):])
            for ref, val in zip(outs[:len(ep.outs)], ro, strict=True):
                ref[...] = val.astype(ref.dtype)
            if ep.accs:
                @pl.when(pl.program_id(0) == 0)
                def _():
                    for ref in outs[len(ep.outs):]:
                        ref[...] = jnp.zeros_like(ref)

                for ref, val in zip(outs[len(ep.outs):], ao, strict=True):
                    ref[...] += val

        if nk == 1:
            finish(prod)
            return
        acc = scratch[0]
        kk = pl.program_id(3)

        @pl.when(kk == 0)
        def _():
            acc[...] = prod

        @pl.when(kk > 0)
        def _():
            acc[...] += prod

        @pl.when(kk == nk - 1)
        def _():
            finish(acc[...])

    def spec(lead, blk, idx, tiles=0):
        mode = {"pipeline_mode": pl.Buffered(1)} if resident(lead, tiles) else {}
        if lead is None:
            return pl.BlockSpec(blk, lambda i, j, p, kk: idx(i, j, kk), **mode)
        if lead == "p":
            return pl.BlockSpec((None,) + blk, lambda i, j, p, kk: (p,) + idx(i, j, kk))
        return pl.BlockSpec((nsum,) + blk, lambda i, j, p, kk: (0,) + idx(i, j, kk), **mode)

    a_tiles, b_tiles = (m // tm) * nk, (n // tn) * nk
    if mode == "nn":
        pair_specs = [spec(a_lead, (tm, tk), lambda i, j, kk: (i, kk), a_tiles),
                      spec(b_lead, (tk, tn), lambda i, j, kk: (kk, j), b_tiles)]
    elif mode == "nt":
        pair_specs = [spec(a_lead, (tm, tk), lambda i, j, kk: (i, kk), a_tiles),
                      spec(b_lead, (tn, tk), lambda i, j, kk: (j, kk), b_tiles)]
    else:
        pair_specs = [spec(a_lead, (tk, tm), lambda i, j, kk: (kk, i), a_tiles),
                      spec(b_lead, (tk, tn), lambda i, j, kk: (kk, j), b_tiles)]
    tile = spec(None, (tm, tn), lambda i, j, kk: (i, j))
    in_specs = pair_specs * len(pairs)
    args = [x for pair in pairs for x in pair]
    if residual is not None:
        assert npar == 1
        in_specs.append(tile)
        args.append(residual)
    if ep:
        in_specs += [pl.BlockSpec((tm, wd), functools.partial(lambda cb, i, j, p, kk: (i, cb), cb)) for _, wd, cb in ep.rows]
        in_specs += [pl.BlockSpec(c.shape, lambda i, j, p, kk: (0, 0)) for c in ep.consts]
        args += [r for r, _, _ in ep.rows] + ep.consts
        out_specs = [pl.BlockSpec((tm, wd), lambda i, j, p, kk: (i, 0)) for wd, _ in ep.outs]
        out_specs += [pl.BlockSpec(shape, lambda i, j, p, kk: (0, 0)) for shape in ep.accs]
        out_shape = [jax.ShapeDtypeStruct((m, wd), d) for wd, d in ep.outs] + [jax.ShapeDtypeStruct(sh, F32) for sh in ep.accs]
    else:
        out_specs = spec("p" if npar > 1 else None, (tm, tn), lambda i, j, kk: (i, j))
        out_shape = jax.ShapeDtypeStruct(((4,) if npar > 1 else ()) + (m, n), out_dtype)
    outer = "arbitrary" if ep and ep.accs else "parallel"
    return pl.pallas_call(
        body, grid=(m // tm, n // tn, npar, nk), in_specs=in_specs, out_specs=out_specs, out_shape=out_shape,
        scratch_shapes=[pltpu.VMEM((tm, tn), F32)] if nk > 1 else [],
        compiler_params=pltpu.CompilerParams(dimension_semantics=(outer, outer, outer, "arbitrary"),
                                             vmem_limit_bytes=VMEM_LIMIT),
        name=name)(*args)


def _row(a, width=None, col_block=0):
    return (a, a.shape[1] if width is None else width, col_block)


def _rows_call(body, rows, consts, outs, accs=(), *, name, tile=ROW_TILE):
    s = rows[0][0].shape[0]
    t = _tile(s, tile)
    nr, nc, no = len(rows), len(consts), len(outs)

    def kern(*refs):
        r = [x[...] for x in refs[:nr]]
        c = [x[...] for x in refs[nr:nr + nc]]
        o_refs = refs[nr + nc:nr + nc + no]
        a_refs = refs[nr + nc + no:]
        ro, ao = body(r, c)
        for ref, val in zip(o_refs, ro, strict=True):
            ref[...] = val.astype(ref.dtype)
        if a_refs:
            @pl.when(pl.program_id(0) == 0)
            def _():
                for ref in a_refs:
                    ref[...] = jnp.zeros_like(ref)

            for ref, val in zip(a_refs, ao, strict=True):
                ref[...] += val

    in_specs = [pl.BlockSpec((t, w), functools.partial(lambda cb, i: (i, cb), cb)) for (_, w, cb) in rows]
    in_specs += [pl.BlockSpec(c.shape, lambda i: (0, 0)) for c in consts]
    out_specs = [pl.BlockSpec((t, w), lambda i: (i, 0)) for (w, _) in outs]
    out_specs += [pl.BlockSpec(shape, lambda i: (0, 0)) for shape in accs]
    out_shape = [jax.ShapeDtypeStruct((s, w), dt) for (w, dt) in outs]
    out_shape += [jax.ShapeDtypeStruct(shape, F32) for shape in accs]
    return pl.pallas_call(
        kern, grid=(s // t,), in_specs=in_specs, out_specs=out_specs, out_shape=out_shape,
        compiler_params=pltpu.CompilerParams(dimension_semantics=("arbitrary" if accs else "parallel",),
                                             vmem_limit_bytes=VMEM_LIMIT),
        name=name)(*[r[0] for r in rows], *consts)


def _gla_chunk(q, k, la, v0, v1, s0, s1):
    c = q.shape[0]
    r = lax.broadcasted_iota(jnp.int32, (c, c), 0)
    cc = lax.broadcasted_iota(jnp.int32, (c, c), 1)
    tril = cc <= r
    cum = _cumsum_rows(la)
    cl = jnp.sum(la, axis=0, keepdims=True)
    qd = q * (GLA_DK ** -0.5) * jnp.exp(cum)
    ki = k * jnp.exp(-cum)
    ke = k * jnp.exp(cl - cum)
    dec = jnp.exp(cl)
    outs, news = [], []
    for h, (v, s) in enumerate(((v0, s0), (v1, s1))):
        mk = _lane_mask(GLA_DK * h, GLA_DK * (h + 1))
        qh = qd * mk
        att = jnp.where(tril, _dot_nt(qh, ki), 0.0)
        outs.append(_dot_nn(att, v) + _dot_nt(qh, s))
        news.append(s * dec + _dot_tn(v, ke * mk))
    return outs[0], outs[1], news[0], news[1]


def _gla_specs(tb, rev_nb=None):
    blk = (lambda b: b) if rev_nb is None else (lambda b: rev_nb - 1 - b)
    q = pl.BlockSpec((tb, 128), lambda p, b: (blk(b), P_GQ // 128 + p))
    k = pl.BlockSpec((tb, 128), lambda p, b: (blk(b), P_GK // 128 + p))
    la = pl.BlockSpec((tb, 128), lambda p, b: (blk(b), p))
    v = pl.BlockSpec((tb, 256), lambda p, b: (blk(b), P_GV // 256 + p))
    o = pl.BlockSpec((tb, 256), lambda p, b: (blk(b), p))
    st = pl.BlockSpec((tb // GLA_CHUNK, 2, 128, 128), lambda p, b: (blk(b), p, 0, 0))
    return q, k, la, v, o, st


def _gla_fwd(proj, la):
    s = proj.shape[0]
    tb = _tile(s, ROW_TILE)
    nb, nch = s // tb, tb // GLA_CHUNK

    def kern(q_ref, k_ref, la_ref, v_ref, o_ref, st_ref, s_sc):
        @pl.when(pl.program_id(1) == 0)
        def _():
            s_sc[...] = jnp.zeros_like(s_sc)

        s0, s1 = s_sc[0], s_sc[1]
        for ci in range(nch):
            sl = slice(ci * GLA_CHUNK, (ci + 1) * GLA_CHUNK)
            st_ref[ci, 0] = s0
            st_ref[ci, 1] = s1
            o0, o1, s0, s1 = _gla_chunk(q_ref[sl, :], k_ref[sl, :], la_ref[sl, :], v_ref[sl, 0:128],
                                        v_ref[sl, 128:256], s0, s1)
            o_ref[sl, 0:128] = o0
            o_ref[sl, 128:256] = o1
        s_sc[0] = s0
        s_sc[1] = s1

    q, k, lasp, v, o, st = _gla_specs(tb)
    return pl.pallas_call(
        kern, grid=(2, nb), in_specs=[q, k, lasp, v], out_specs=[o, st],
        out_shape=[jax.ShapeDtypeStruct((s, 512), F32),
                   jax.ShapeDtypeStruct((s // GLA_CHUNK, GLA_HEADS, 128, 128), F32)],
        scratch_shapes=[pltpu.VMEM((2, 128, 128), F32)],
        compiler_params=pltpu.CompilerParams(dimension_semantics=("parallel", "arbitrary"),
                                             vmem_limit_bytes=VMEM_LIMIT),
        name="gla_fwd")(proj, proj, la, proj)


def _gla_bwd(proj, la, states, d_o):
    s = proj.shape[0]
    tb = _tile(s, ROW_TILE)
    nb, nch = s // tb, tb // GLA_CHUNK

    def kern(q_ref, k_ref, la_ref, v_ref, do_ref, st_ref, dq_ref, dk_ref, dla_ref, dv_ref, ds_sc):
        @pl.when(pl.program_id(1) == 0)
        def _():
            ds_sc[...] = jnp.zeros_like(ds_sc)

        d0, d1 = ds_sc[0], ds_sc[1]
        for ci in reversed(range(nch)):
            sl = slice(ci * GLA_CHUNK, (ci + 1) * GLA_CHUNK)
            _, vjp = jax.vjp(_gla_chunk, q_ref[sl, :], k_ref[sl, :], la_ref[sl, :], v_ref[sl, 0:128],
                             v_ref[sl, 128:256], st_ref[ci, 0], st_ref[ci, 1])
            gq, gk, gla, gv0, gv1, d0, d1 = vjp((do_ref[sl, 0:128], do_ref[sl, 128:256], d0, d1))
            dq_ref[sl, :] = gq
            dk_ref[sl, :] = gk
            dla_ref[sl, :] = gla
            dv_ref[sl, 0:128] = gv0
            dv_ref[sl, 128:256] = gv1
        ds_sc[0] = d0
        ds_sc[1] = d1

    q, k, lasp, v, o, st = _gla_specs(tb, rev_nb=nb)
    return pl.pallas_call(
        kern, grid=(2, nb), in_specs=[q, k, lasp, v, o, st], out_specs=[lasp, lasp, lasp, o],
        out_shape=[jax.ShapeDtypeStruct((s, 256), F32), jax.ShapeDtypeStruct((s, 256), F32),
                   jax.ShapeDtypeStruct((s, 256), F32), jax.ShapeDtypeStruct((s, 512), F32)],
        scratch_shapes=[pltpu.VMEM((2, 128, 128), F32)],
        compiler_params=pltpu.CompilerParams(dimension_semantics=("parallel", "arbitrary"),
                                             vmem_limit_bytes=VMEM_LIMIT),
        name="gla_bwd")(proj, proj, la, proj, d_o, states)


def _causal_keep(t):
    return lax.broadcasted_iota(jnp.int32, (t, t), 1) <= lax.broadcasted_iota(jnp.int32, (t, t), 0)


def _split_refs(refs, counts):
    out, off = [], 0
    for cnt in counts:
        out.append(refs[off:off + cnt])
        off += cnt
    return out


def _causal_blocks(n, key_major):
    pairs = ([(ki, qi) for ki in range(n) for qi in range(ki, n)] if key_major else
             [(ki, qi) for qi in range(n) for ki in range(qi + 1)])
    return np.array([ki for ki, _ in pairs], np.int32), np.array([qi for _, qi in pairs], np.int32)


def _attn_fwd(q, k, v, comm, tile=2048):
    s = q.shape[0]
    t = _tile(s, tile)
    n = s // t
    nci, nco = len(comm.ins), len(comm.out_shape)

    ki_tab, qi_tab = _causal_blocks(n, key_major=False)
    steps = len(ki_tab)

    def kern(ki_ref, qi_ref, *refs):
        (q_ref, k_ref, v_ref), cins, (o_ref, lse_ref), couts, (m_sc, l_sc, acc_sc), csems = _split_refs(
            refs, (3, nci, 2, nco, 3, len(comm.sems)))
        pair, step = pl.program_id(0), pl.program_id(1)
        qi, ki = qi_ref[step], ki_ref[step]
        place = _place()

        @pl.when((pair == 0) & (step == 0))
        def _():
            comm.start(place, cins, couts, csems)

        @pl.when((pair == MLA_HEADS // 2 - 1) & (step == 0))
        def _():
            comm.mid(place, cins, couts, csems)

        first = lax.broadcasted_iota(jnp.int32, (t, LANES), 1) < MLA_V

        @pl.when(ki == 0)
        def _():
            m_sc[...] = jnp.full_like(m_sc, -jnp.inf)
            l_sc[...] = jnp.zeros_like(l_sc)
            acc_sc[...] = jnp.zeros_like(acc_sc)

        def update(rows, cols, masked):
            nr = rows.stop - rows.start
            sel = first[:nr]
            alphas, pvs = [], []
            for h in range(2):
                sc = _dg(q_ref[rows, 128 * h:128 * (h + 1)], k_ref[cols, 128 * h:128 * (h + 1)], _NT)
                if masked:
                    sc = jnp.where(_causal_keep(nr), sc, -jnp.inf)
                m_prev = m_sc[h, rows]
                m_new = jnp.maximum(m_prev, jnp.max(sc, axis=1, keepdims=True))
                alpha = jnp.exp2(m_prev - m_new)
                p = jnp.exp2(sc - m_new[:, 0:1])
                l_sc[h, rows] = alpha * l_sc[h, rows] + jnp.sum(p, axis=1, keepdims=True)
                m_sc[h, rows] = m_new
                alphas.append(alpha)
                pvs.append(_dg(p, v_ref[cols, :], _NN))
            acc_sc[rows] = acc_sc[rows] * jnp.where(sel, alphas[0], alphas[1]) + jnp.where(sel, pvs[0], pvs[1])

        halves = [slice(0, t)] if t % 256 else [slice(0, t // 2), slice(t // 2, t)]

        @pl.when(ki < qi)
        def _():
            for rows in halves:
                for cols in halves:
                    update(rows, cols, False)

        @pl.when(ki == qi)
        def _():
            for i, rows in enumerate(halves):
                for j, cols in enumerate(halves[:i + 1]):
                    update(rows, cols, i == j)

        @pl.when(ki == qi)
        def _():
            l = jnp.where(first, l_sc[0], l_sc[1])
            m = jnp.where(first, m_sc[0], m_sc[1])
            o_ref[...] = acc_sc[...] / l
            lse_ref[...] = m + jnp.log2(l)

        @pl.when((pair == MLA_HEADS // 2 - 1) & (step == steps - 1))
        def _():
            comm.finish(place, cins, couts, csems)

    q_idx = lambda p, st, ki_r, qi_r: (qi_r[st], p)
    k_idx = lambda p, st, ki_r, qi_r: (ki_r[st], p)
    res = pl.pallas_call(
        kern, grid_spec=pltpu.PrefetchScalarGridSpec(
            num_scalar_prefetch=2, grid=(MLA_HEADS // 2, steps),
            in_specs=[pl.BlockSpec((t, 256), q_idx), pl.BlockSpec((t, 256), k_idx), pl.BlockSpec((t, 128), k_idx)]
            + [ANY] * nci,
            out_specs=[pl.BlockSpec((t, 128), q_idx), pl.BlockSpec((t, 128), q_idx)] + [ANY] * nco,
            scratch_shapes=[pltpu.VMEM((2, t, LANES), F32), pltpu.VMEM((2, t, LANES), F32),
                            pltpu.VMEM((t, LANES), F32)] + comm.sems),
        out_shape=[jax.ShapeDtypeStruct((s, 512), F32), jax.ShapeDtypeStruct((s, 512), F32)] + comm.out_shape,
        compiler_params=pltpu.CompilerParams(dimension_semantics=("arbitrary", "arbitrary"),
                                             vmem_limit_bytes=VMEM_LIMIT),
        name="mla_attn_fwd")(ki_tab, qi_tab, q, k, v, *comm.ins)
    return res[0], res[1], res[2:]


def _attn_bwd(q, k, v, o, lse, d_o, comm, tile=2048):
    s = q.shape[0]
    t = _tile(s, tile)
    n = s // t
    nci, nco = len(comm.ins), len(comm.out_shape)

    ki_tab, qi_tab = _causal_blocks(n, key_major=True)
    steps = len(ki_tab)

    def kern(ki_ref, qi_ref, *refs):
        (q_ref, k_ref, v_ref, o_ref, lse_ref, do_ref), cins, (dq_ref, dk_ref, dv_ref), couts, (dk_sc, dv_sc), csems = \
            _split_refs(refs, (6, nci, 3, nco, 2, len(comm.sems)))
        pair, step = pl.program_id(0), pl.program_id(1)
        ki, qi = ki_ref[step], qi_ref[step]
        place = _place()

        @pl.when((pair == 0) & (step == 0))
        def _():
            comm.start(place, cins, couts, csems)

        @pl.when((pair == MLA_HEADS // 2 - 1) & (step == 0))
        def _():
            comm.mid(place, cins, couts, csems)

        @pl.when((ki == 0) & (qi == 0))
        def _():
            dq_ref[...] = jnp.zeros_like(dq_ref)

        @pl.when(qi == ki)
        def _():
            dk_sc[...] = jnp.zeros_like(dk_sc)
            dv_sc[...] = jnp.zeros_like(dv_sc)

        def update(rows, cols, masked):
            nr = rows.stop - rows.start
            d_o = do_ref[rows, :]
            prod = d_o * o_ref[rows, :]
            dq_rows = pl.ds(pl.multiple_of(qi * t + rows.start, nr), nr)
            for h in range(2):
                hs = slice(128 * h, 128 * (h + 1))
                mk = _lane_mask(MLA_V * h, MLA_V * (h + 1))
                qh, kh = q_ref[rows, hs], k_ref[cols, hs]
                sc = _dg(qh, kh, _NT)
                if masked:
                    sc = jnp.where(_causal_keep(nr), sc, -jnp.inf)
                p = jnp.exp2(sc - lse_ref[rows, MLA_V * h:MLA_V * h + 1])
                doh = d_o * mk
                dp = _dg(doh * LN2, v_ref[cols, :], _NT)
                delta = jnp.sum(prod * mk, axis=1, keepdims=True) * LN2
                ds = p * (dp - delta)
                dv_sc[cols, :] += _dg(p, doh, _TN)
                dk_sc[cols, hs] += _dg(ds, qh, _TN)
                dq_ref[dq_rows, hs] += _dg(ds, kh, _NN)

        halves = [slice(0, t)] if t % 256 else [slice(0, t // 2), slice(t // 2, t)]

        @pl.when(qi > ki)
        def _():
            for rows in halves:
                for cols in halves:
                    update(rows, cols, False)

        @pl.when(qi == ki)
        def _():
            for i, rows in enumerate(halves):
                for j, cols in enumerate(halves[:i + 1]):
                    update(rows, cols, i == j)

        @pl.when(qi == n - 1)
        def _():
            dk_ref[...] = dk_sc[...]
            dv_ref[...] = dv_sc[...].astype(dv_ref.dtype)

        @pl.when((pair == MLA_HEADS // 2 - 1) & (step == steps - 1))
        def _():
            comm.finish(place, cins, couts, csems)

    q_idx = lambda p, st, ki_r, qi_r: (qi_r[st], p)
    k_idx = lambda p, st, ki_r, qi_r: (ki_r[st], p)
    res = pl.pallas_call(
        kern, grid_spec=pltpu.PrefetchScalarGridSpec(
            num_scalar_prefetch=2, grid=(MLA_HEADS // 2, steps),
            in_specs=[pl.BlockSpec((t, 256), q_idx), pl.BlockSpec((t, 256), k_idx), pl.BlockSpec((t, 128), k_idx),
                      pl.BlockSpec((t, 128), q_idx), pl.BlockSpec((t, 128), q_idx), pl.BlockSpec((t, 128), q_idx)]
            + [ANY] * nci,
            out_specs=[pl.BlockSpec((s, 256), lambda p, st, ki_r, qi_r: (0, p)), pl.BlockSpec((t, 256), k_idx),
                       pl.BlockSpec((t, 128), k_idx)] + [ANY] * nco,
            scratch_shapes=[pltpu.VMEM((t, 256), F32), pltpu.VMEM((t, 128), F32)] + comm.sems),
        out_shape=[jax.ShapeDtypeStruct((s, 1024), F32), jax.ShapeDtypeStruct((s, 1024), F32),
                   jax.ShapeDtypeStruct((s, 512), BF16)] + comm.out_shape,
        compiler_params=pltpu.CompilerParams(dimension_semantics=("arbitrary", "arbitrary"),
                                             vmem_limit_bytes=VMEM_LIMIT),
        name="mla_attn_bwd")(ki_tab, qi_tab, q, k, v, o, lse, d_o, *comm.ins)
    return res[0], res[1], res[2], res[3:]


def _gate_fn(alr, w2, b):
    return _log_sigmoid(_dot_nn(alr, w2) + b) * (1.0 / GLA_GATE_NORM)


def _make_norm_rope(scale):
    def forward(x, w, c, sa, sb):
        r = lax.rsqrt(jnp.sum(x * x, axis=-1, keepdims=True) * (1.0 / MLA_QK) + EPS)
        y = x * r * w
        out = y * c + pltpu.roll(y, LANES - 16, 1) * sa + pltpu.roll(y, 16, 1) * sb
        return (out if scale == 1.0 else out * scale), r

    @jax.custom_vjp
    def norm_rope(x, w, c, sa, sb):
        return forward(x, w, c, sa, sb)[0]

    def fwd(x, w, c, sa, sb):
        out, r = forward(x, w, c, sa, sb)
        return out, (x, w, c, sa, sb, r)

    def bwd(res, g):
        x, w, c, sa, sb, r = res
        if scale != 1.0:
            g = g * scale
        gy = g * c + pltpu.roll(g * sa, 16, 1) + pltpu.roll(g * sb, LANES - 16, 1)
        xr = x * r
        t = gy * w
        m = jnp.sum(t * xr, axis=-1, keepdims=True) * (1.0 / MLA_QK)
        return r * (t - xr * m), jnp.sum(gy * xr, axis=0, keepdims=True), jnp.zeros_like(c), jnp.zeros_like(sa), jnp.zeros_like(sb)

    norm_rope.defvjp(fwd, bwd)
    return norm_rope


_q_norm_rope = _make_norm_rope(MLA_QK ** -0.5 * LOG2E)
_k_norm_rope = _make_norm_rope(1.0)


def _qk_head(qh, kh, kpe, c, sa, sb, qn, kn):
    kfull = kh + kpe * _lane_mask(MLA_NOPE, MLA_QK)
    return _q_norm_rope(qh, qn, c, sa, sb), _k_norm_rope(kfull, kn, c, sa, sb)


def _mix_head(o, og, gn):
    return _rms(o, gn) * _silu(og)


def _xa_head(xq, xk, xv, qn, kn):
    sc = _dot_nt(_rms(xq, qn), _rms(xk, kn)) * (XA_DIM ** -0.5)
    e = jnp.exp(sc - lax.stop_gradient(jnp.max(sc, axis=1, keepdims=True)))
    p = e / jnp.sum(e, axis=1, keepdims=True)
    return _dot_nn(p, xv)


def _heads(x, n):
    return [x[:, 128 * h:128 * (h + 1)] for h in range(n)]


def _cat(xs):
    return jnp.concatenate(xs, axis=1)


def _norm_fwd(x, w, name):
    return _rows_call(lambda r, c: ([_rms(r[0], c[0])], []), [_row(x)], [w], [(x.shape[1], BF16)], name=name)[0]


def _norm_fwd_epilogue(w):
    return _Epilogue(lambda h, rows, consts: ([h, _rms(h, consts[0])], []), [], [w], [(D_MODEL, F32), (D_MODEL, BF16)], [])


def _norm_bwd_epilogue(x, w, add):
    def fn(d_out, rows, consts):
        _, vjp = jax.vjp(_rms, rows[0], consts[0])
        dx, dw = vjp(d_out)
        return [dx + rows[1]], [dw]

    return _Epilogue(fn, [_row(x), _row(add)], [w], [(D_MODEL, F32)], [w.shape])


def _norm_fwd_comm(x, w, comm, name):
    s, d = x.shape
    t = _tile(s, ROW_TILE)
    n = s // t
    nci, nco = len(comm.ins), len(comm.out_shape)

    def kern(*refs):
        (x_ref, w_ref), cins, (o_ref,), couts, csems = _split_refs(refs, (2, nci, 1, nco, len(comm.sems)))
        place = _place()

        @pl.when(pl.program_id(0) == 0)
        def _():
            comm.start(place, cins, couts, csems)

        o_ref[...] = _rms(x_ref[...], w_ref[...]).astype(o_ref.dtype)

        @pl.when(pl.program_id(0) == n - 1)
        def _():
            comm.mid(place, cins, couts, csems)
            comm.finish(place, cins, couts, csems)

    tile = pl.BlockSpec((t, d), lambda i: (i, 0))
    res = pl.pallas_call(
        kern, grid=(n,), in_specs=[tile, pl.BlockSpec(w.shape, lambda i: (0, 0))] + [ANY] * nci,
        out_specs=[tile] + [ANY] * nco, out_shape=[jax.ShapeDtypeStruct((s, d), BF16)] + comm.out_shape,
        scratch_shapes=comm.sems,
        compiler_params=pltpu.CompilerParams(dimension_semantics=("arbitrary",), vmem_limit_bytes=VMEM_LIMIT),
        name=name)(x, w, *comm.ins)
    return res[0], res[1:]


def _norm_bwd(x, w, d_out, add, name):
    def body(r, c):
        _, vjp = jax.vjp(_rms, r[0], c[0])
        dx, dw = vjp(r[1])
        return [dx + r[2]], [dw]

    return _rows_call(body, [_row(x), _row(d_out), _row(add)], [w], [(x.shape[1], F32)], [w.shape], name=name)


CONV_HALO = BF16_ROWS


def _conv_specs(s, f, t):
    n8 = t // CONV_HALO
    cur = pl.BlockSpec((None, t, f), lambda j, i: (j, i, 0))
    prev = pl.BlockSpec((None, CONV_HALO, f), lambda j, i: (j, jnp.maximum(i * n8 - 1, 0), 0))
    nxt = pl.BlockSpec((None, CONV_HALO, f), lambda j, i: (j, jnp.minimum((i + 1) * n8, s // CONV_HALO - 1), 0))
    cw = pl.BlockSpec((None, 3, f), lambda j, i: (j, 0, 0))
    cb = pl.BlockSpec((None, 1, f), lambda j, i: (j, 0, 0))
    return cur, prev, nxt, cw, cb


def _conv_taps(g, prev, first):
    ext = jnp.concatenate([jnp.where(first, 0.0, prev.astype(F32)), g], axis=0)
    return pltpu.roll(ext, 1, 0)[CONV_HALO:], pltpu.roll(ext, 2, 0)[CONV_HALO:]


def _conv_fwd(gg, uu, cw, cb, comm):
    _, s, f = gg.shape
    t = _tile(s, ROW_TILE)
    nt = s // t
    nci, nco = len(comm.ins), len(comm.out_shape)

    def kern(*refs):
        (g_ref, gp_ref, u_ref, cw_ref, cb_ref), cins, (o_ref,), couts, csems = _split_refs(
            refs, (5, nci, 1, nco, len(comm.sems)))
        shard, i = pl.program_id(0), pl.program_id(1)
        place = _place()

        @pl.when((shard == 0) & (i == 0))
        def _():
            comm.start(place, cins, couts, csems)

        @pl.when((shard == 3) & (i == 0))
        def _():
            comm.mid(place, cins, couts, csems)

        g = g_ref[...].astype(F32)
        g1, g2 = _conv_taps(g, gp_ref[...], i == 0)
        w = cw_ref[...]
        gc = cb_ref[...] + w[0:1] * g2 + w[1:2] * g1 + w[2:3] * g
        o_ref[...] = (_silu(gc) * u_ref[...].astype(F32)).astype(o_ref.dtype)

        @pl.when((shard == 3) & (i == nt - 1))
        def _():
            comm.finish(place, cins, couts, csems)

    cur, prev, _, cws, cbs = _conv_specs(s, f, t)
    res = pl.pallas_call(
        kern, grid=(4, nt), in_specs=[cur, prev, cur, cws, cbs] + [ANY] * nci, out_specs=[cur] + [ANY] * nco,
        out_shape=[jax.ShapeDtypeStruct(gg.shape, BF16)] + comm.out_shape, scratch_shapes=comm.sems,
        compiler_params=pltpu.CompilerParams(dimension_semantics=("arbitrary", "arbitrary"), vmem_limit_bytes=VMEM_LIMIT),
        name="ffn_conv_fwd")(gg, gg, uu, cw, cb, *comm.ins)
    return res[0], res[1:]


def _conv_bwd(gg, uu, dact, cw, cb):
    _, s, f = gg.shape
    t = _tile(s, ROW_TILE)
    nt = s // t

    def kern(g_ref, gp_ref, gn_ref, u_ref, un_ref, da_ref, dan_ref, cw_ref, cb_ref, du_ref, dg_ref, dcw_ref, dcb_ref):
        i = pl.program_id(1)
        cat = lambda a_ref, b_ref: jnp.concatenate([a_ref[...].astype(F32), b_ref[...].astype(F32)], axis=0)
        g, u, da = cat(g_ref, gn_ref), cat(u_ref, un_ref), cat(da_ref, dan_ref)
        g1, g2 = _conv_taps(g, gp_ref[...], i == 0)
        w = cw_ref[...]
        gc = cb_ref[...] + w[0:1] * g2 + w[1:2] * g1 + w[2:3] * g
        sg = jax.nn.sigmoid(gc)
        du_ref[...] = (da[:t] * (gc[:t] * sg[:t])).astype(du_ref.dtype)
        row = lax.broadcasted_iota(jnp.int32, (t + CONV_HALO, 1), 0)
        dgc = jnp.where((row < t) | (i < nt - 1), da * u * (sg * (1.0 + gc * (1.0 - sg))), 0.0)
        up1 = pltpu.roll(dgc, t + CONV_HALO - 1, 0)[:t]
        up2 = pltpu.roll(dgc, t + CONV_HALO - 2, 0)[:t]
        dgc = dgc[:t]
        dg_ref[...] = (w[2:3] * dgc + w[1:2] * up1 + w[0:1] * up2).astype(dg_ref.dtype)

        @pl.when(i == 0)
        def _():
            dcw_ref[...] = jnp.zeros_like(dcw_ref)
            dcb_ref[...] = jnp.zeros_like(dcb_ref)

        ones = jnp.ones((8, t), BF16)
        col_sum = lambda a: _dg(ones, a, _NN)[0:1]
        dcw_ref[0:1, :] += col_sum(dgc * g2[:t])
        dcw_ref[1:2, :] += col_sum(dgc * g1[:t])
        dcw_ref[2:3, :] += col_sum(dgc * g[:t])
        dcb_ref[...] += col_sum(dgc)

    cur, prev, nxt, cws, cbs = _conv_specs(s, f, t)
    return pl.pallas_call(
        kern, grid=(4, nt), in_specs=[cur, prev, nxt, cur, nxt, cur, nxt, cws, cbs], out_specs=[cur, cur, cws, cbs],
        out_shape=[jax.ShapeDtypeStruct(gg.shape, BF16), jax.ShapeDtypeStruct(gg.shape, BF16),
                   jax.ShapeDtypeStruct(cw.shape, F32), jax.ShapeDtypeStruct(cb.shape, F32)],
        compiler_params=pltpu.CompilerParams(dimension_semantics=("parallel", "arbitrary"), vmem_limit_bytes=VMEM_LIMIT),
        name="ffn_conv_bwd")(gg, gg, gg, uu, uu, dact, dact, cw, cb)


def _rope_tables(pos):
    half = MLA_ROPE // 2
    lane = jnp.arange(LANES)
    rotary = (lane >= MLA_NOPE) & (lane < MLA_QK)
    inv = jnp.where(rotary, ROPE_THETA ** (-((lane - MLA_NOPE) % half).astype(F32) / half), 0.0)
    ang = pos.astype(F32)[:, None] * inv
    cos, sin = jnp.cos(ang), jnp.sin(ang)
    first = rotary & (lane < MLA_NOPE + half)
    return cos, jnp.where(first, -sin, 0.0), jnp.where(rotary & ~first, sin, 0.0)


def _local_step(x, mem, pos, target, rep, early_shards, late_shards):
    g = {}
    c, sa, sb = _rope_tables(pos)

    xn, gathered = _norm_fwd_comm(x, rep["norm_mix"], _gather_plan(early_shards), "norm_mix_fwd_gather")
    w = _early_layout(dict(zip(EARLY, gathered, strict=True)), rep)

    def proj_fn(r, rows, k):
        la_ = _gate_fn(r[:, P_ALR:P_ALR + 128], k[0], k[1])
        return [r, la_, _rms(r[:, P_CQ:P_CQ + MLA_Q_RANK], k[2]), _rms(r[:, P_CKV:P_CKV + MLA_KV_RANK], k[3])], []

    proj, la, q_lat, kv_lat = _matmul(
        xn, w["in"], "nt", F32, "proj_fwd", epilogue=_Epilogue(
            proj_fn, [], [w["w2"], w["gate_b"], w["q_a_norm"], w["kv_a_norm"]],
            [(P_WIDTH, F32), (256, F32), (MLA_Q_RANK, BF16), (MLA_KV_RANK, BF16)], []))
    alr = _row(proj, 128, P_ALR // 128)
    kpe = _row(proj, 128, P_KPE // 128)
    og = _row(proj, 512, P_OG // 512)
    cq = _row(proj, 256, P_CQ // 256)
    ckv = _row(proj, 128, P_CKV // 128)

    o_gla, states = _gla_fwd(proj, la)

    def qk_body(r, k):
        q_up, k_up = _dg(r[0], k[0], _NN), _dg(r[1], k[1], _NN)
        qs, ks = [], []
        for qh, kh in zip(_heads(q_up, MLA_HEADS), _heads(k_up, MLA_HEADS)):
            a, b = _qk_head(qh, kh, r[2], r[3], r[4], r[5], k[3], k[4])
            qs.append(a)
            ks.append(b)
        return [_cat(qs), _cat(ks), _dg(r[1], k[2], _NN)], []

    tabs = [_row(c), _row(sa), _row(sb)]
    qk_consts = [w["uq"], w["k"], w["v"], w["q_norm"], w["k_norm"]]
    q_r, k_r, v_mla = _rows_call(qk_body, [_row(q_lat), _row(kv_lat), kpe] + tabs, qk_consts,
                                 [(1024, BF16), (1024, BF16), (512, BF16)], name="mla_qk_fwd")
    with_attn = [n for n in LATE if n not in LAST]
    o_mla, lse, gathered = _attn_fwd(q_r, k_r, v_mla, _gather_plan([late_shards[n] for n in with_attn]))
    w.update(_late_layout(dict(zip(with_attn, gathered, strict=True))))

    def mix_body(r, k):
        ys = [_mix_head(o, g_, k[0]) for o, g_ in zip(_heads(r[0], GLA_HEADS), _heads(r[1], GLA_HEADS))]
        return [_cat(ys + [r[2]])], []

    cat = _rows_call(mix_body, [_row(o_gla), og, _row(o_mla)], [w["gla_out_norm"]], [(1024, BF16)],
                     name="mix_fwd")[0]
    h1, hn = _matmul(cat, w["out"], "nn", F32, "out_fwd_norm", residual=x, epilogue=_norm_fwd_epilogue(w["norm_xa"]))
    mn = _norm_fwd(mem, w["norm_mem"], "norm_mem_fwd")
    xkv = _matmul(mn, w["xkv"], "nn", F32, "xa_kv_fwd")

    def xa_fn(r, rows, k):
        ks, vs = _heads(k[0], 2 * XA_HEADS)[:XA_HEADS], _heads(k[0], 2 * XA_HEADS)[XA_HEADS:]
        return [r, _cat([_xa_head(a, b, v_, k[1], k[2]) for a, b, v_ in zip(_heads(r, XA_HEADS), ks, vs)])], []

    xq, xo = _matmul(hn, w["xq"], "nn", F32, "xa_q_fwd_attn", epilogue=_Epilogue(
        xa_fn, [], [xkv, w["xa_q_norm"], w["xa_k_norm"]], [(512, F32), (512, BF16)], []))
    h2, fn = _matmul(xo, w["xo"], "nn", F32, "xa_o_fwd_norm", residual=h1, epilogue=_norm_fwd_epilogue(w["norm_ffn"]))
    gg = _matmul(fn, w["wg"], "nt", BF16, "ffn_gate_fwd", b_lead="p")
    uu = _matmul(fn, w["wu"], "nt", BF16, "ffn_up_fwd", b_lead="p")
    act, gathered = _conv_fwd(gg, uu, w["cw"], w["cb"], _gather_plan([late_shards[n] for n in LAST]))
    w["wd"] = gathered[0]
    def loss_fn(y, rows, consts):
        err = y - rows[0]
        part = 0.5 * jnp.sum(jnp.sum(err * err, axis=1, keepdims=True) * (1.0 / D_MODEL), axis=0, keepdims=True)
        return [err * (1.0 / D_MODEL)], [jnp.broadcast_to(part, (1, LANES))]

    dy, loss = _matmul(act, w["wd"], "nn", F32, "ffn_down_fwd_loss", residual=h2, a_lead="k", b_lead="k",
                       epilogue=_Epilogue(loss_fn, [_row(target)], [], [(D_MODEL, F32)], [(1, LANES)]))

    g["ffn_w_down"] = _matmul(act, dy, "tn", BF16, "ffn_down_dw", a_lead="p")
    dact = _matmul(dy, w["wd"], "nt", BF16, "ffn_down_dx", b_lead="p")
    duu, dgg, g["ffn_conv_w"], g["ffn_conv_b"] = _conv_bwd(gg, uu, dact, w["cw"], w["cb"])
    g["ffn_w_gate"] = _matmul(dgg, fn, "tn", BF16, "ffn_gate_dw", a_lead="p")
    g["ffn_w_up"] = _matmul(duu, fn, "tn", BF16, "ffn_up_dw", a_lead="p")
    dh2, g["norm_ffn"] = _matmul(dgg, w["wg"], "nn", F32, "ffn_dx_norm_bwd", a_lead="k", b_lead="k", more=(duu, w["wu"]),
                                 epilogue=_norm_bwd_epilogue(h2, w["norm_ffn"], dy))

    g["xa_w_o"] = _matmul(xo, dh2, "tn", BF16, "xa_o_dw")
    def xa_bwd(dxo_, rows, k):
        kvh = _heads(k[0], 2 * XA_HEADS)
        dq_, dk_, dv_ = [], [], []
        dqn, dkn = 0.0, 0.0
        for h, (a, d_) in enumerate(zip(_heads(rows[0], XA_HEADS), _heads(dxo_, XA_HEADS))):
            _, vjp = jax.vjp(_xa_head, a, kvh[h], kvh[XA_HEADS + h], k[1], k[2])
            ga, gk, gv, gqn, gkn = vjp(d_)
            dq_.append(ga)
            dk_.append(gk)
            dv_.append(gv)
            dqn, dkn = dqn + gqn, dkn + gkn
        return [_cat(dq_)], [_cat(dk_ + dv_), dqn, dkn]

    dxq, dxkv, g["xa_q_norm"], g["xa_k_norm"] = _matmul(dh2, w["xo"], "nt", F32, "xa_o_dx_attn_bwd", epilogue=_Epilogue(
        xa_bwd, [_row(xq)], [xkv, w["xa_q_norm"], w["xa_k_norm"]], [(512, BF16)], [xkv.shape, (1, 128), (1, 128)]))
    g["xa_w_q"] = _matmul(hn, dxq, "tn", BF16, "xa_q_dw")
    dh1, g["norm_xa"] = _matmul(dxq, w["xq"], "nt", F32, "xa_q_dx_norm_bwd",
                                epilogue=_norm_bwd_epilogue(h1, w["norm_xa"], dh2))
    g["xa_w_kv"] = _matmul(mn, dxkv, "tn", BF16, "xa_kv_dw")
    dmn = _matmul(dxkv, w["xkv"], "nt", F32, "xa_kv_dx")
    _, g["norm_mem"] = _norm_bwd(mem, w["norm_mem"], dmn, dmn, "norm_mem_bwd")

    g["w_out"] = _matmul(cat, dh1, "tn", BF16, "out_dw")
    def mix_bwd(dcat_, rows, k):
        do_, dog_ = [], []
        dgn = 0.0
        for o, g_, d_ in zip(_heads(rows[0], GLA_HEADS), _heads(rows[1], GLA_HEADS), _heads(dcat_, GLA_HEADS)):
            _, vjp = jax.vjp(_mix_head, o, g_, k[0])
            a, b, gn_ = vjp(d_)
            do_.append(a)
            dog_.append(b)
            dgn = dgn + gn_
        return [_cat(do_), _cat(dog_), dcat_[:, 512:]], [dgn]

    do_gla, d_og, do_mla, g["gla_out_norm"] = _matmul(dh1, w["out"], "nt", F32, "out_dx_mix_bwd", epilogue=_Epilogue(
        mix_bwd, [_row(o_gla), og], [w["gla_out_norm"]], [(512, F32), (512, BF16), (512, F32)], [(1, 128)]))

    late_parts = _late_grad_shards(g)
    dq_r, dk_r, dv_mla, lands_late = _attn_bwd(q_r, k_r, v_mla, o_mla, lse, do_mla,
                                               _scatter_plan([late_parts[n] for n in LATE]))
    lands_late = dict(zip(LATE, lands_late, strict=True))

    def qk_bwd(r, k):
        q_up, k_up = _dg(r[0], k[0], _NN), _dg(r[1], k[1], _NN)
        dqs, dks = [], []
        dkpe, dqn, dkn = 0.0, 0.0, 0.0
        for qh, kh, dqh, dkh in zip(_heads(q_up, MLA_HEADS), _heads(k_up, MLA_HEADS), _heads(r[6], MLA_HEADS),
                                    _heads(r[7], MLA_HEADS)):
            _, vjp = jax.vjp(lambda a, b, e, f, h_: _qk_head(a, b, e, r[3], r[4], r[5], f, h_), qh, kh, r[2], k[3], k[4])
            ga, gb, ge, gf, gh = vjp((dqh, dkh))
            dqs.append(ga)
            dks.append(gb)
            dkpe, dqn, dkn = dkpe + ge, dqn + gf, dkn + gh
        dq_up, dk_up, dv = _cat(dqs), _cat(dks), r[8]
        dq_lat_ = _dg(dq_up, k[0], _NT)
        dkv_lat_ = _dg(dk_up, k[1], _NT) + _dg(dv, k[2], _NT)
        return [dq_lat_, dkv_lat_, dkpe], [dqn, dkn, _dg(r[0], dq_up, _TN), _dg(r[1], dk_up, _TN), _dg(r[1], dv, _TN)]

    dq_lat, dkv_lat, d_kpe, g["q_norm"], g["k_norm"], g["uq"], g["k"], g["v"] = _rows_call(
        qk_bwd, [_row(q_lat), _row(kv_lat), kpe] + tabs + [_row(dq_r), _row(dk_r), _row(dv_mla)], qk_consts,
        [(MLA_Q_RANK, F32), (MLA_KV_RANK, F32), (128, BF16)],
        [(1, 128), (1, 128), w["uq"].shape, w["k"].shape, w["v"].shape], name="mla_qk_bwd")

    dgq, dgk, dla, dgv = _gla_bwd(proj, la, states, do_gla)

    def dproj_body(r, k):
        alr_, cq_, ckv_, dla_, dq_lat_, dkv_lat_, dgq_, dgk_, dgv_, d_og_, d_kpe_ = r
        _, gate_vjp = jax.vjp(_gate_fn, alr_, k[0], k[1])
        d_alr, gw2, gb = gate_vjp(dla_)
        _, q_vjp = jax.vjp(_rms, cq_, k[2])
        _, kv_vjp = jax.vjp(_rms, ckv_, k[3])
        d_cq, gqa = q_vjp(dq_lat_)
        d_ckv, gkva = kv_vjp(dkv_lat_)
        pieces = [dgq_, dgk_, dgv_, d_og_, d_cq, d_ckv, d_kpe_, d_alr]
        return [_cat([x_.astype(BF16) for x_ in pieces])], [gw2, gb, gqa, gkva]

    dproj, g["w2"], g["gla_gate_b"], g["mla_q_a_norm"], g["mla_kv_a_norm"] = _rows_call(
        dproj_body, [alr, cq, ckv, _row(dla), _row(dq_lat), _row(dkv_lat), _row(dgq), _row(dgk), _row(dgv), _row(d_og),
                     _row(d_kpe)], [w["w2"], w["gate_b"], w["q_a_norm"], w["kv_a_norm"]], [(P_WIDTH, BF16)],
        [(128, 256), (1, 256), (1, 256), (1, 128)], name="proj_cotangent")
    g["in"] = _matmul(dproj, xn, "tn", BF16, "proj_dw")
    dx, g["norm_mix"] = _matmul(dproj, w["in"], "nn", F32, "proj_dx_norm_bwd",
                                epilogue=_norm_bwd_epilogue(x, w["norm_mix"], dh1))
    return loss[0, 0], dx, g, lands_late


def _join_shards(pieces, axis):
    if axis == 0:
        return pieces.reshape(-1, pieces.shape[2])
    return jnp.transpose(pieces, (1, 0, 2)).reshape(pieces.shape[1], -1)


def _split_shards(full, axis):
    r, c = full.shape
    if axis == 0:
        return full.reshape(4, r // 4, c)
    return jnp.transpose(full.reshape(r, 4, c // 4), (1, 0, 2))


def _early_layout(gath, rep):
    w_in = gath["w_in"].reshape(N_WIDTH, D_MODEL)
    z = lambda n: jnp.zeros((n, D_MODEL), w_in.dtype)
    seg = lambda lo, n: w_in[lo:lo + n]
    ukv = _join_shards(gath["mla_w_ukv"], 1).reshape(MLA_KV_RANK, MLA_HEADS, MLA_NOPE + MLA_V)
    w = {
        "in": jnp.concatenate([seg(N_GQ, 256), seg(N_GK, 256), seg(N_GV, 512), seg(N_OG, 512), seg(N_CQ, 256),
                               seg(N_CKV, 128), z(64), seg(N_KPE, 32), z(32), seg(N_ALR, 16), z(112)], axis=0),
        "uq": jnp.pad(_join_shards(gath["mla_w_uq"], 1).reshape(MLA_Q_RANK, MLA_HEADS, MLA_QK),
                      ((0, 0), (0, 0), (0, LANES - MLA_QK))).reshape(MLA_Q_RANK, MLA_HEADS * LANES),
        "k": jnp.pad(ukv[:, :, :MLA_NOPE], ((0, 0), (0, 0), (0, LANES - MLA_NOPE))).reshape(MLA_KV_RANK, -1),
        "v": ukv[:, :, MLA_NOPE:].reshape(MLA_KV_RANK, MLA_HEADS * MLA_V),
        "w2": jnp.pad(_join_shards(gath["gla_gate_w2"], 1), ((0, LANES - GLA_RANK), (0, 0))),
        "cb": rep["ffn_conv_b"].reshape(4, 1, D_FF // 4),
        "q_norm": jnp.pad(rep["mla_q_norm"], ((0, 0), (0, LANES - MLA_QK))),
        "k_norm": jnp.pad(rep["mla_k_norm"], ((0, 0), (0, LANES - MLA_QK))),
        "q_a_norm": rep["mla_q_a_norm"], "kv_a_norm": rep["mla_kv_a_norm"], "gate_b": rep["gla_gate_b"],
    }
    for n in ("norm_mix", "gla_out_norm", "norm_xa", "norm_mem", "xa_q_norm", "xa_k_norm", "norm_ffn"):
        w[n] = rep[n]
    return w


def _late_layout(gath):
    return {"out": _join_shards(gath["w_out"], 0), "xq": _join_shards(gath["xa_w_q"], 0),
            "xkv": _join_shards(gath["xa_w_kv"], 0), "xo": _join_shards(gath["xa_w_o"], 1),
            "wg": gath["ffn_w_gate"], "wu": gath["ffn_w_up"], "cw": gath["ffn_conv_w"]}


def _late_grad_shards(g):
    sh = {"w_out": _split_shards(g["w_out"], 0), "xa_w_q": _split_shards(g["xa_w_q"], 0),
          "xa_w_kv": _split_shards(g["xa_w_kv"], 0), "xa_w_o": _split_shards(g["xa_w_o"], 1),
          "ffn_w_gate": g["ffn_w_gate"], "ffn_w_up": g["ffn_w_up"], "ffn_conv_w": g["ffn_conv_w"],
          "ffn_w_down": g["ffn_w_down"]}
    return {n: v.astype(BF16) for n, v in sh.items()}


def _early_grad_shards(g):
    gi = g["in"]
    seg = lambda lo, n: gi[lo:lo + n]
    w_in = jnp.concatenate([seg(P_GQ, 256), seg(P_GK, 256), seg(P_GV, 512), seg(P_ALR, 16), seg(P_OG, 512),
                            seg(P_CQ, 256), seg(P_CKV, 128), seg(P_KPE + 64, 32)], axis=0)
    uq = g["uq"].reshape(MLA_Q_RANK, MLA_HEADS, LANES)[:, :, :MLA_QK].reshape(MLA_Q_RANK, -1)
    ukv = jnp.concatenate([g["k"].reshape(MLA_KV_RANK, MLA_HEADS, LANES)[:, :, :MLA_NOPE],
                           g["v"].reshape(MLA_KV_RANK, MLA_HEADS, MLA_V)], axis=2).reshape(MLA_KV_RANK, -1)
    sh = {"w_in": w_in.reshape(4, N_WIDTH // 4, D_MODEL), "gla_gate_w2": _split_shards(g["w2"][:GLA_RANK], 1),
          "mla_w_uq": _split_shards(uq, 1), "mla_w_ukv": _split_shards(ukv, 1)}
    sh = {n: v.astype(BF16) for n, v in sh.items()}
    rep = {n: g[n] for n in REPLICATED if n in g}
    rep["mla_q_norm"] = g["q_norm"][:, :MLA_QK]
    rep["mla_k_norm"] = g["k_norm"][:, :MLA_QK]
    rep["ffn_conv_b"] = g["ffn_conv_b"].reshape(1, D_FF)
    return sh, rep


SMALL_SHAPE = (8, 1024)


def _pack_small(vectors):
    flat = jnp.concatenate(vectors, axis=1)
    return jnp.pad(flat, ((0, 0), (0, SMALL_SHAPE[0] * SMALL_SHAPE[1] - flat.shape[1]))).reshape(SMALL_SHAPE)


def _unpack_small(buf, widths):
    flat = buf.reshape(1, -1)
    out, off = [], 0
    for wd in widths:
        out.append(flat[:, off:off + wd])
        off += wd
    return out


ANY = pl.BlockSpec(memory_space=pl.ANY)


def _place():
    x, y, c = lax.axis_index("x"), lax.axis_index("y"), lax.axis_index("c")
    chips = [(1 - x, y), (x, 1 - y), (1 - x, 1 - y)]
    return x, y, c, chips


class _Comm:
    def __init__(self, ins, out_shape, sems, start, finish, mid=None):
        self.ins, self.out_shape, self.sems = list(ins), list(out_shape), list(sems)
        self.start, self.finish, self.mid = start, finish, mid or (lambda *args: None)


def _run_comm(plan, name):
    ni, no = len(plan.ins), len(plan.out_shape)

    def body(*refs):
        ins, outs, sems = refs[:ni], refs[ni:ni + no], refs[ni + no:]
        place = _place()
        plan.start(place, ins, outs, sems)
        plan.mid(place, ins, outs, sems)
        plan.finish(place, ins, outs, sems)

    return pl.pallas_call(body, in_specs=[ANY] * ni, out_specs=[ANY] * no, out_shape=plan.out_shape,
                          scratch_shapes=plan.sems, name=name)(*plan.ins)


def _gather_plan(shards):
    n = len(shards)
    by_rows = [s.shape[0] % (2 * BF16_ROWS) == 0 for s in shards]
    by_cols = [not r and s.shape[1] % (2 * LANES) == 0 for r, s in zip(by_rows, shards)]
    split = [r or c for r, c in zip(by_rows, by_cols)]

    def rows(ref, t, c):
        if by_rows[t]:
            half = shards[t].shape[0] // 2
            return ref.at[pl.ds(pl.multiple_of(c * half, BF16_ROWS), half)]
        if by_cols[t]:
            half = shards[t].shape[1] // 2
            return ref.at[:, pl.ds(pl.multiple_of(c * half, LANES), half)]
        return ref

    def remote(src, dst, ss, rs, to):
        return pltpu.make_async_remote_copy(src_ref=src, dst_ref=dst, send_sem=ss, recv_sem=rs, device_id=to,
                                            device_id_type=MESH)

    def first_wave(place, ins, outs, sems):
        x, y, c, chips = place
        ici_s, ici_r, _, _, local = sems
        me = 2 * x + y
        own = [pltpu.make_async_copy(ins[t], outs[t].at[me], local.at[t]) for t in range(n)]
        push = [remote(rows(ins[t], t, c), rows(outs[t].at[me], t, c), ici_s.at[3 * t + j], ici_r.at[3 * t + j], (px, py, c))
                for t in range(n) for j, (px, py) in enumerate(chips)]
        return own, push

    def second_wave(place, ins, outs, sems, last):
        x, y, c, chips = place
        ici_s, ici_r, d2d_s, d2d_r, local = sems
        sib = (x, y, 1 - c)
        out = []
        for t in range(n):
            for j, (px, py) in enumerate(chips):
                block = outs[t].at[2 * px + py]
                got = rows(block, t, c)
                if split[t]:
                    hand = remote(got, got, d2d_s.at[3 * t + j], d2d_r.at[3 * t + j], sib)
                    theirs = rows(block, t, 1 - c)
                    other = (remote(theirs, theirs, local.at[0], d2d_r.at[3 * t + j], sib) if last else
                             remote(got, got, local.at[0], ici_r.at[3 * t + j], sib))
                    out.append((other, hand))
                elif last:
                    out.append((remote(got, got, local.at[0], ici_r.at[3 * t + j], sib), None))
        return out

    def start(place, ins, outs, sems):
        own, push = first_wave(place, ins, outs, sems)
        for cp in own + push:
            cp.start()

    def mid(place, ins, outs, sems):
        for arrival, hand in second_wave(place, ins, outs, sems, False):
            arrival.wait_recv()
            hand.start()

    def finish(place, ins, outs, sems):
        own, push = first_wave(place, ins, outs, sems)
        for arrival, hand in second_wave(place, ins, outs, sems, True):
            arrival.wait_recv()
            if hand is not None:
                hand.wait_send()
        for cp in push:
            cp.wait_send()
        for cp in own:
            cp.wait()

    dma = pltpu.SemaphoreType.DMA
    return _Comm(shards, [jax.ShapeDtypeStruct((4,) + s.shape, s.dtype) for s in shards],
                 [dma((3 * n,)), dma((3 * n,)), dma((3 * n,)), dma((3 * n,)), dma((n,))], start, finish, mid)


def _scatter_plan(parts, small=None):
    n = len(parts)
    ns = 0 if small is None else 1

    def unpack(place, ins, outs, sems):
        x, y, c, chips = place
        return x, y, c, chips, 2 * x + y, 4 * x + 2 * y + c, (x, y, 1 - c)

    def remote(src, dst, ss, rs, to):
        return pltpu.make_async_remote_copy(src_ref=src, dst_ref=dst, send_sem=ss, recv_sem=rs, device_id=to,
                                            device_id_type=MESH)

    def first_wave(place, ins, outs, sems):
        x, y, c, chips, me, dev, sib = unpack(place, ins, outs, sems)
        ici_s, ici_r, d2d_s, d2d_r, sm_s, sm_r, local = sems
        own, push = [], []
        if ns:
            own.append(pltpu.make_async_copy(ins[n], outs[n].at[dev], local.at[n]))
            for k in range(1, 8):
                px = (1 - x) if (k >> 2) & 1 else x
                py = (1 - y) if (k >> 1) & 1 else y
                pc = (1 - c) if k & 1 else c
                push.append(remote(ins[n], outs[n].at[dev], sm_s.at[k - 1], sm_r.at[k - 1], (px, py, pc)))
        for t in range(n):
            own.append(pltpu.make_async_copy(ins[t].at[me], outs[t].at[dev], local.at[t]))
            push.append(remote(ins[t].at[me], outs[t].at[dev], d2d_s.at[4 * t], d2d_r.at[4 * t], sib))
            for j, (px, py) in enumerate(chips):
                push.append(remote(ins[t].at[2 * px + py], outs[t].at[dev], ici_s.at[3 * t + j], ici_r.at[3 * t + j],
                                   (px, py, c)))
        return own, push

    def start(place, ins, outs, sems):
        own, push = first_wave(place, ins, outs, sems)
        for cp in own + push:
            cp.start()

    def landed(dst, rs, sems, sib):
        remote(dst, dst, sems[-1].at[0], rs, sib).wait_recv()

    def forwards(place, ins, outs, sems):
        x, y, c, chips, me, dev, sib = unpack(place, ins, outs, sems)
        d2d_s, d2d_r = sems[2], sems[3]
        slots = [(t, j, outs[t].at[4 * px + 2 * py + c]) for t in range(n) for j, (px, py) in enumerate(chips)]
        return [(t, j, slot, remote(slot, slot, d2d_s.at[4 * t + 1 + j], d2d_r.at[4 * t + 1 + j], sib))
                for t, j, slot in slots]

    def mid(place, ins, outs, sems):
        sib = unpack(place, ins, outs, sems)[-1]
        for t, j, slot, cp in forwards(place, ins, outs, sems):
            landed(slot, sems[1].at[3 * t + j], sems, sib)
            cp.start()

    def finish(place, ins, outs, sems):
        x, y, c, chips, me, dev, sib = unpack(place, ins, outs, sems)
        d2d_r, sm_r = sems[3], sems[5]
        own, push = first_wave(place, ins, outs, sems)
        push += [cp for _, _, _, cp in forwards(place, ins, outs, sems)]
        for t in range(n):
            landed(outs[t].at[4 * x + 2 * y + (1 - c)], d2d_r.at[4 * t], sems, sib)
            for j, (px, py) in enumerate(chips):
                landed(outs[t].at[4 * px + 2 * py + (1 - c)], d2d_r.at[4 * t + 1 + j], sems, sib)
        if ns:
            for k in range(1, 8):
                px = (1 - x) if (k >> 2) & 1 else x
                py = (1 - y) if (k >> 1) & 1 else y
                pc = (1 - c) if k & 1 else c
                landed(outs[n].at[4 * px + 2 * py + pc], sm_r.at[k - 1], sems, sib)
        for cp in push:
            cp.wait_send()
        for cp in own:
            cp.wait()

    dma = pltpu.SemaphoreType.DMA
    ins = list(parts) + ([small] if ns else [])
    out_shape = [jax.ShapeDtypeStruct((8,) + p.shape[1:], p.dtype) for p in parts]
    if ns:
        out_shape.append(jax.ShapeDtypeStruct((8,) + small.shape, small.dtype))
    return _Comm(ins, out_shape, [dma((3 * n,)), dma((3 * n,)), dma((4 * n,)), dma((4 * n,)), dma((7,)), dma((7,)),
                                  dma((n + 1,))], start, finish, mid)


ADAM_ROWS = 288


def _row_tile(r, cap):
    if r <= cap:
        return r
    return max((t for t in range(8, cap + 1, 8) if r % t == 0), default=r)


def _adamw_update(w, m, v, land):
    g = land[0].astype(F32)
    for i in range(1, 8):
        g = g + land[i].astype(F32)
    m_new = ADAM_B1 * m + (1.0 - ADAM_B1) * g
    v_new = ADAM_B2 * v + (1.0 - ADAM_B2) * (g * g)
    m_hat = m_new / (1.0 - ADAM_B1 ** ADAM_STEP)
    v_hat = v_new / (1.0 - ADAM_B2 ** ADAM_STEP)
    return g, -ADAM_LR * (m_hat / (jnp.sqrt(v_hat) + ADAM_EPS) + ADAM_WD * w), m_new, v_new


def _adamw(tensors, name, comm=None):
    k = len(tensors)
    r, c = tensors[0][0].shape
    t = _row_tile(r, ADAM_ROWS // k)
    tc = c if t < r or r <= ADAM_ROWS else 2 * LANES
    n = (r // t) * (c // tc)
    nci, nco, nsem = (len(comm.ins), len(comm.out_shape), len(comm.sems)) if comm else (0, 0, 0)

    def kern(*refs):
        ins, cins, outs, couts, csems = _split_refs(refs, (4 * k, nci, 4 * k, nco, nsem))
        if comm:
            place = _place()

            @pl.when(pl.program_id(0) == 0)
            def _():
                comm.start(place, cins, couts, csems)

        for i in range(k):
            w_ref, m_ref, v_ref, l_ref = ins[4 * i:4 * i + 4]
            res = _adamw_update(w_ref[...], m_ref[...], v_ref[...], l_ref)
            for ref, val in zip(outs[4 * i:4 * i + 4], res, strict=True):
                ref[...] = val
        if comm:
            @pl.when(pl.program_id(0) == n - 1)
            def _():
                comm.mid(place, cins, couts, csems)
                comm.finish(place, cins, couts, csems)

    where = (lambda i: (i, 0)) if tc == c else (lambda i: (0, i))
    spec = pl.BlockSpec((t, tc), where)
    lspec = pl.BlockSpec((8, t, tc), lambda i: (0,) + where(i))
    res = pl.pallas_call(
        kern, grid=(n,), in_specs=[spec, spec, spec, lspec] * k + [ANY] * nci, out_specs=[spec] * (4 * k) + [ANY] * nco,
        out_shape=[jax.ShapeDtypeStruct((r, c), F32)] * (4 * k) + (comm.out_shape if comm else []),
        scratch_shapes=comm.sems if comm else [],
        compiler_params=pltpu.CompilerParams(dimension_semantics=("arbitrary" if comm else "parallel",),
                                             vmem_limit_bytes=VMEM_LIMIT),
        name=name)(*[x for tens in tensors for x in tens], *(comm.ins if comm else []))
    return [res[4 * i:4 * i + 4] for i in range(k)], res[4 * k:]


def _step(a):
    def sq(n):
        v = a[n][0] if a[n].ndim == 3 else a[n]
        return v.T if n.removeprefix("m_").removeprefix("v_") in TRANSPOSED else v

    payload = lambda n: sq(n) if n in EXACT_GATHER else sq(n).astype(BF16)

    loss, dx, g, lands_late = _local_step(sq("x"), sq("mem"), a["positions"][0], sq("loss_target"),
                                          {n: a[n] for n in REPLICATED}, [payload(n) for n in EARLY],
                                          {n: payload(n) for n in LATE})

    sh, rep = _early_grad_shards(g)
    small = _pack_small([rep[n] for n in REPLICATED] + [loss.reshape(1, 1)])
    *lands_early, land_small = _run_comm(_scatter_plan([sh[n] for n in EARLY], small), "scatter_last")
    quad = lambda n, land: (sq(n), sq("m_" + n), sq("v_" + n), land)
    lands = dict(zip(EARLY, lands_early, strict=True)) | lands_late

    outs = {}
    kinds = ("grad_", "delta_", "new_m_", "new_v_")
    for n, _ in SHARDED:
        res = _adamw([quad(n, lands[n])], "adamw_" + n)[0][0]
        for kind, val in zip(kinds, res, strict=True):
            outs[kind + n] = (val.T if n in TRANSPOSED else val).reshape(a[n].shape)
    zero = jnp.zeros((1, 1), F32)
    packed = [_pack_small([a[p + n] for n in REPLICATED] + [zero]) for p in ("", "m_", "v_")]
    res = _adamw([(*packed, land_small)], "adamw_replicated")[0][0]
    widths = [a[n].shape[1] for n in REPLICATED] + [1]
    for kind, buf in zip(kinds, res, strict=True):
        *vals, total = _unpack_small(buf, widths)
        for n, val in zip(REPLICATED, vals, strict=True):
            outs[kind + n] = val
        if kind == "grad_":
            loss = total[0, 0]

    ordered = [outs[kind + n] for kind in kinds for n in WEIGHTS]
    return (loss, dx[None], *ordered)


def kernel(x, mem, positions, norm_mix, w_in, gla_gate_w2, gla_gate_b, gla_out_norm, mla_q_a_norm, mla_w_uq, mla_kv_a_norm, mla_w_ukv, mla_q_norm, mla_k_norm, w_out, norm_xa, norm_mem, xa_w_q, xa_w_kv, xa_q_norm, xa_k_norm, xa_w_o, norm_ffn, ffn_w_gate, ffn_w_up, ffn_conv_w, ffn_conv_b, ffn_w_down, loss_target, m_norm_mix, m_w_in, m_gla_gate_w2, m_gla_gate_b, m_gla_out_norm, m_mla_q_a_norm, m_mla_w_uq, m_mla_kv_a_norm, m_mla_w_ukv, m_mla_q_norm, m_mla_k_norm, m_w_out, m_norm_xa, m_norm_mem, m_xa_w_q, m_xa_w_kv, m_xa_q_norm, m_xa_k_norm, m_xa_w_o, m_norm_ffn, m_ffn_w_gate, m_ffn_w_up, m_ffn_conv_w, m_ffn_conv_b, m_ffn_w_down, v_norm_mix, v_w_in, v_gla_gate_w2, v_gla_gate_b, v_gla_out_norm, v_mla_q_a_norm, v_mla_w_uq, v_mla_kv_a_norm, v_mla_w_ukv, v_mla_q_norm, v_mla_k_norm, v_w_out, v_norm_xa, v_norm_mem, v_xa_w_q, v_xa_w_kv, v_xa_q_norm, v_xa_k_norm, v_xa_w_o, v_norm_ffn, v_ffn_w_gate, v_ffn_w_up, v_ffn_conv_w, v_ffn_conv_b, v_ffn_w_down):
    return _step(dict(locals()))
```

```python
import functools

import jax
import jax.numpy as jnp
import numpy as np
from jax import lax
from jax.experimental import pallas as pl
from jax.experimental.pallas import tpu as pltpu

F32, BF16 = jnp.float32, jnp.bfloat16
MESH = pl.DeviceIdType.MESH

D_MODEL = 1024
EPS = 1e-6
GLA_HEADS, GLA_DK, GLA_DV, GLA_RANK, GLA_CHUNK = 4, 64, 128, 16, 64
GLA_GATE_NORM = 16.0
MLA_HEADS, MLA_Q_RANK, MLA_KV_RANK, MLA_NOPE, MLA_ROPE, MLA_V = 8, 256, 128, 64, 32, 64
MLA_QK = MLA_NOPE + MLA_ROPE
ROPE_THETA = 10000.0
LOG2E, LN2 = 1.4426950408889634, 0.6931471805599453
XA_HEADS, XA_DIM = 4, 128
D_FF = 2816
ADAM_LR, ADAM_B1, ADAM_B2, ADAM_EPS, ADAM_WD, ADAM_STEP = 0.001, 0.9, 0.999, 1e-08, 0.01, 10

LANES = 128
BF16_ROWS = 16
VMEM_LIMIT = 56 * 1024 * 1024
MATMUL_VMEM = 44 * 1024 * 1024
ROW_TILE = 512

P_GQ, P_GK, P_GV, P_OG, P_CQ, P_CKV, P_KPE, P_ALR, P_WIDTH = 0, 256, 512, 1024, 1536, 1792, 1920, 2048, 2176
N_GQ, N_GK, N_GV, N_ALR, N_OG, N_CQ, N_CKV, N_KPE, N_WIDTH = 0, 256, 512, 1024, 1040, 1552, 1808, 1936, 1968

SHARDED = (("w_in", 1), ("gla_gate_w2", 1), ("mla_w_uq", 1), ("mla_w_ukv", 1), ("w_out", 0), ("xa_w_q", 0),
           ("xa_w_kv", 0), ("xa_w_o", 1), ("ffn_w_gate", 1), ("ffn_w_up", 1), ("ffn_conv_w", 1), ("ffn_w_down", 0))
REPLICATED = ("norm_mix", "gla_gate_b", "gla_out_norm", "mla_q_a_norm", "mla_kv_a_norm", "mla_q_norm", "mla_k_norm",
              "norm_xa", "norm_mem", "xa_q_norm", "xa_k_norm", "norm_ffn", "ffn_conv_b")
EXACT_GATHER = ("gla_gate_w2", "ffn_conv_w")
TRANSPOSED = ("w_in", "ffn_w_gate", "ffn_w_up")
EARLY = ("w_in", "gla_gate_w2", "mla_w_uq", "mla_w_ukv")
LATE = tuple(n for n, _ in SHARDED if n not in EARLY)
LAST = ("ffn_w_down",)
SUMMED_EARLY = ("ffn_w_gate", "ffn_w_up", "ffn_w_down")
WEIGHTS = ("norm_mix", "w_in", "gla_gate_w2", "gla_gate_b", "gla_out_norm", "mla_q_a_norm", "mla_w_uq",
           "mla_kv_a_norm", "mla_w_ukv", "mla_q_norm", "mla_k_norm", "w_out", "norm_xa", "norm_mem", "xa_w_q",
           "xa_w_kv", "xa_q_norm", "xa_k_norm", "xa_w_o", "norm_ffn", "ffn_w_gate", "ffn_w_up", "ffn_conv_w",
           "ffn_conv_b", "ffn_w_down")


_NN = ((1,), (0,))
_NT = ((1,), (1,))
_TN = ((0,), (0,))


def _dg(a, b, dims):
    return lax.dot_general(a.astype(BF16), b.astype(BF16), (dims, ((), ())), preferred_element_type=F32)


@jax.custom_vjp
def _dot_nn(a, b):
    return _dg(a, b, _NN)


_dot_nn.defvjp(lambda a, b: (_dg(a, b, _NN), (a, b)),
               lambda r, g: (_dg(g, r[1], _NT).astype(r[0].dtype), _dg(r[0], g, _TN).astype(r[1].dtype)))


@jax.custom_vjp
def _dot_nt(a, b):
    return _dg(a, b, _NT)


_dot_nt.defvjp(lambda a, b: (_dg(a, b, _NT), (a, b)),
               lambda r, g: (_dg(g, r[1], _NN).astype(r[0].dtype), _dg(g, r[0], _TN).astype(r[1].dtype)))


@jax.custom_vjp
def _dot_tn(a, b):
    return _dg(a, b, _TN)


_dot_tn.defvjp(lambda a, b: (_dg(a, b, _TN), (a, b)),
               lambda r, g: (_dg(r[1], g, _NT).astype(r[0].dtype), _dg(r[0], g, _NN).astype(r[1].dtype)))


def _rms(x, w, n=None):
    n = x.shape[-1] if n is None else n
    ms = jnp.sum(x * x, axis=-1, keepdims=True) * (1.0 / n)
    return x * lax.rsqrt(ms + EPS) * w


def _silu(x):
    return x * jax.nn.sigmoid(x)


def _log_sigmoid(x):
    return jnp.minimum(x, 0.0) - jnp.log(1.0 + jnp.exp(-jnp.abs(x)))


@jax.custom_vjp
def _cumsum_rows(x):
    n = x.shape[0]
    row = lax.broadcasted_iota(jnp.int32, x.shape, 0)
    k = 1
    while k < n:
        x = x + jnp.where(row >= k, pltpu.roll(x, k, 0), 0.0)
        k *= 2
    return x


def _cumsum_rows_bwd(_, g):
    n = g.shape[0]
    row = lax.broadcasted_iota(jnp.int32, g.shape, 0)
    k = 1
    while k < n:
        g = g + jnp.where(row < n - k, pltpu.roll(g, n - k, 0), 0.0)
        k *= 2
    return (g,)


_cumsum_rows.defvjp(lambda x: (_cumsum_rows(x), None), _cumsum_rows_bwd)


def _lane_mask(lo, hi):
    lane = lax.broadcasted_iota(jnp.int32, (1, LANES), 1)
    return ((lane >= lo) & (lane < hi)).astype(F32)


def _tile(n, t):
    t = min(n, t)
    assert n % t == 0, (n, t)
    return t


class _Epilogue:
    def __init__(self, fn, rows=(), consts=(), outs=(), accs=()):
        self.fn, self.rows, self.consts, self.outs, self.accs = fn, list(rows), list(consts), list(outs), list(accs)


def _matmul(a, b, mode, out_dtype, name, residual=None, a_lead=None, b_lead=None, more=None, epilogue=None):
    (a0, a1), (b0, b1) = a.shape[-2:], b.shape[-2:]
    if mode == "nn":
        m, k, k2, n = a0, a1, b0, b1
    elif mode == "nt":
        m, k, n, k2 = a0, a1, b0, b1
    else:
        k, m, k2, n = a0, a1, b0, b1
    assert k == k2, (a.shape, b.shape, mode)
    npar = 4 if "p" in (a_lead, b_lead) else 1
    nsum = 4 if "k" in (a_lead, b_lead) else 1
    pairs = [(a, b)] + ([more] if more else [])
    a_item, b_item, o_item = a.dtype.itemsize, b.dtype.itemsize, jnp.dtype(out_dtype).itemsize
    ep = epilogue
    row_extra = 4 if residual is not None else 0
    if ep:
        row_extra += (sum(r.dtype.itemsize * wd for r, wd, _ in ep.rows) + sum(jnp.dtype(d).itemsize * wd for wd, d in ep.outs)) / n

    def resident(lead, tiles):
        return lead != "p" and tiles == 1

    def vmem_need(tm, tn, tk):
        a_bufs = 1 if resident(a_lead, (m // tm) * (k // tk)) else 2
        b_bufs = 1 if resident(b_lead, (n // tn) * (k // tk)) else 2
        need = a_bufs * (nsum if a_lead == "k" else 1) * tm * tk * a_item + b_bufs * (nsum if b_lead == "k" else 1) * tk * tn * b_item
        need *= len(pairs)
        need += (0 if ep else 2 * tm * tn * o_item) + tm * tn * 4 * (2 if tk < k else 1)
        need += tm * tk * 2 * (a_item == 4 or mode == "tn") + tk * tn * 2 * (b_item == 4)
        return need + int(2 * tm * tn * row_extra) + (3 * tm * tn * 4 if ep else 0)

    halvings = (4096, 2048, 1024, 512, 256, 128, 64, 32, 16, 8)
    if mode == "tn":
        tm = m if m <= 2304 else m // 2
        tn = n if tm * n <= 1024 * 2304 else n // 2
        tk = next((r for r in halvings if k % r == 0 and vmem_need(tm, tn, r) <= MATMUL_VMEM), k)
    else:
        tn, tk = n, k
        tm = next((r for r in halvings if m % r == 0 and vmem_need(r, tn, tk) <= MATMUL_VMEM), m)
    assert m % tm == 0 and n % tn == 0 and k % tk == 0
    assert ep is None or (tn == n and tk == k and npar == 1)
    nk = k // tk
    dims = {"nn": _NN, "nt": _NT, "tn": _TN}[mode]
    n_in = 2 * len(pairs) + (residual is not None)
    n_ep_in = len(ep.rows) + len(ep.consts) if ep else 0
    n_out = len(ep.outs) + len(ep.accs) if ep else 1

    def body(*refs):
        ab, rs, ep_in, outs, scratch = _split_refs(refs, (2 * len(pairs), n_in - 2 * len(pairs), n_ep_in, n_out, nk > 1))
        prod = None
        for a_ref, b_ref in zip(ab[0::2], ab[1::2]):
            for sh in range(nsum):
                term = _dg(a_ref[sh] if a_lead == "k" else a_ref[...], b_ref[sh] if b_lead == "k" else b_ref[...], dims)
                prod = term if prod is None else prod + term

        def finish(r):
            if rs:
                r = r + rs[0][...]
            if ep is None:
                outs[0][...] = r.astype(outs[0].dtype)
                return
            vals = [x[...] for x in ep_in]
            ro, ao = ep.fn(r, vals[:len(ep.rows)], vals[len(ep.rows):])
            for ref, val in zip(outs[:len(ep.outs)], ro, strict=True):
                ref[...] = val.astype(ref.dtype)
            if ep.accs:
                @pl.when(pl.program_id(0) == 0)
                def _():
                    for ref in outs[len(ep.outs):]:
                        ref[...] = jnp.zeros_like(ref)

                for ref, val in zip(outs[len(ep.outs):], ao, strict=True):
                    ref[...] += val

        if nk == 1:
            finish(prod)
            return
        acc = scratch[0]
        kk = pl.program_id(3)

        @pl.when(kk == 0)
        def _():
            acc[...] = prod

        @pl.when(kk > 0)
        def _():
            acc[...] += prod

        @pl.when(kk == nk - 1)
        def _():
            finish(acc[...])

    def spec(lead, blk, idx, tiles=0):
        mode = {"pipeline_mode": pl.Buffered(1)} if resident(lead, tiles) else {}
        if lead is None:
            return pl.BlockSpec(blk, lambda i, j, p, kk: idx(i, j, kk), **mode)
        if lead == "p":
            return pl.BlockSpec((None,) + blk, lambda i, j, p, kk: (p,) + idx(i, j, kk))
        return pl.BlockSpec((nsum,) + blk, lambda i, j, p, kk: (0,) + idx(i, j, kk), **mode)

    a_tiles, b_tiles = (m // tm) * nk, (n // tn) * nk
    if mode == "nn":
        pair_specs = [spec(a_lead, (tm, tk), lambda i, j, kk: (i, kk), a_tiles),
                      spec(b_lead, (tk, tn), lambda i, j, kk: (kk, j), b_tiles)]
    elif mode == "nt":
        pair_specs = [spec(a_lead, (tm, tk), lambda i, j, kk: (i, kk), a_tiles),
                      spec(b_lead, (tn, tk), lambda i, j, kk: (j, kk), b_tiles)]
    else:
        pair_specs = [spec(a_lead, (tk, tm), lambda i, j, kk: (kk, i), a_tiles),
                      spec(b_lead, (tk, tn), lambda i, j, kk: (kk, j), b_tiles)]
    tile = spec(None, (tm, tn), lambda i, j, kk: (i, j))
    in_specs = pair_specs * len(pairs)
    args = [x for pair in pairs for x in pair]
    if residual is not None:
        assert npar == 1
        in_specs.append(tile)
        args.append(residual)
    if ep:
        in_specs += [pl.BlockSpec((tm, wd), functools.partial(lambda cb, i, j, p, kk: (i, cb), cb)) for _, wd, cb in ep.rows]
        in_specs += [pl.BlockSpec(c.shape, lambda i, j, p, kk: (0, 0)) for c in ep.consts]
        args += [r for r, _, _ in ep.rows] + ep.consts
        out_specs = [pl.BlockSpec((tm, wd), lambda i, j, p, kk: (i, 0)) for wd, _ in ep.outs]
        out_specs += [pl.BlockSpec(shape, lambda i, j, p, kk: (0, 0)) for shape in ep.accs]
        out_shape = [jax.ShapeDtypeStruct((m, wd), d) for wd, d in ep.outs] + [jax.ShapeDtypeStruct(sh, F32) for sh in ep.accs]
    else:
        out_specs = spec("p" if npar > 1 else None, (tm, tn), lambda i, j, kk: (i, j))
        out_shape = jax.ShapeDtypeStruct(((4,) if npar > 1 else ()) + (m, n), out_dtype)
    outer = "arbitrary" if ep and ep.accs else "parallel"
    return pl.pallas_call(
        body, grid=(m // tm, n // tn, npar, nk), in_specs=in_specs, out_specs=out_specs, out_shape=out_shape,
        scratch_shapes=[pltpu.VMEM((tm, tn), F32)] if nk > 1 else [],
        compiler_params=pltpu.CompilerParams(dimension_semantics=(outer, outer, outer, "arbitrary"),
                                             vmem_limit_bytes=VMEM_LIMIT),
        name=name)(*args)


def _row(a, width=None, col_block=0):
    return (a, a.shape[1] if width is None else width, col_block)


def _rows_call(body, rows, consts, outs, accs=(), *, name, tile=ROW_TILE):
    s = rows[0][0].shape[0]
    t = _tile(s, tile)
    nr, nc, no = len(rows), len(consts), len(outs)

    def kern(*refs):
        r = [x[...] for x in refs[:nr]]
        c = [x[...] for x in refs[nr:nr + nc]]
        o_refs = refs[nr + nc:nr + nc + no]
        a_refs = refs[nr + nc + no:]
        ro, ao = body(r, c)
        for ref, val in zip(o_refs, ro, strict=True):
            ref[...] = val.astype(ref.dtype)
        if a_refs:
            @pl.when(pl.program_id(0) == 0)
            def _():
                for ref in a_refs:
                    ref[...] = jnp.zeros_like(ref)

            for ref, val in zip(a_refs, ao, strict=True):
                ref[...] += val

    in_specs = [pl.BlockSpec((t, w), functools.partial(lambda cb, i: (i, cb), cb)) for (_, w, cb) in rows]
    in_specs += [pl.BlockSpec(c.shape, lambda i: (0, 0)) for c in consts]
    out_specs = [pl.BlockSpec((t, w), lambda i: (i, 0)) for (w, _) in outs]
    out_specs += [pl.BlockSpec(shape, lambda i: (0, 0)) for shape in accs]
    out_shape = [jax.ShapeDtypeStruct((s, w), dt) for (w, dt) in outs]
    out_shape += [jax.ShapeDtypeStruct(shape, F32) for shape in accs]
    return pl.pallas_call(
        kern, grid=(s // t,), in_specs=in_specs, out_specs=out_specs, out_shape=out_shape,
        compiler_params=pltpu.CompilerParams(dimension_semantics=("arbitrary" if accs else "parallel",),
                                             vmem_limit_bytes=VMEM_LIMIT),
        name=name)(*[r[0] for r in rows], *consts)


def _gla_chunk(q, k, la, v0, v1, s0, s1):
    c = q.shape[0]
    r = lax.broadcasted_iota(jnp.int32, (c, c), 0)
    cc = lax.broadcasted_iota(jnp.int32, (c, c), 1)
    tril = cc <= r
    cum = _cumsum_rows(la)
    cl = jnp.sum(la, axis=0, keepdims=True)
    qd = q * (GLA_DK ** -0.5) * jnp.exp(cum)
    ki = k * jnp.exp(-cum)
    ke = k * jnp.exp(cl - cum)
    dec = jnp.exp(cl)
    outs, news = [], []
    for h, (v, s) in enumerate(((v0, s0), (v1, s1))):
        mk = _lane_mask(GLA_DK * h, GLA_DK * (h + 1))
        qh = qd * mk
        att = jnp.where(tril, _dot_nt(qh, ki), 0.0)
        outs.append(_dot_nn(att, v) + _dot_nt(qh, s))
        news.append(s * dec + _dot_tn(v, ke * mk))
    return outs[0], outs[1], news[0], news[1]


def _gla_specs(tb, rev_nb=None):
    blk = (lambda b: b) if rev_nb is None else (lambda b: rev_nb - 1 - b)
    q = pl.BlockSpec((tb, 128), lambda p, b: (blk(b), P_GQ // 128 + p))
    k = pl.BlockSpec((tb, 128), lambda p, b: (blk(b), P_GK // 128 + p))
    la = pl.BlockSpec((tb, 128), lambda p, b: (blk(b), p))
    v = pl.BlockSpec((tb, 256), lambda p, b: (blk(b), P_GV // 256 + p))
    o = pl.BlockSpec((tb, 256), lambda p, b: (blk(b), p))
    st = pl.BlockSpec((tb // GLA_CHUNK, 2, 128, 128), lambda p, b: (blk(b), p, 0, 0))
    return q, k, la, v, o, st


def _gla_fwd(proj, la):
    s = proj.shape[0]
    tb = _tile(s, ROW_TILE)
    nb, nch = s // tb, tb // GLA_CHUNK

    def kern(q_ref, k_ref, la_ref, v_ref, o_ref, st_ref, s_sc):
        @pl.when(pl.program_id(1) == 0)
        def _():
            s_sc[...] = jnp.zeros_like(s_sc)

        s0, s1 = s_sc[0], s_sc[1]
        for ci in range(nch):
            sl = slice(ci * GLA_CHUNK, (ci + 1) * GLA_CHUNK)
            st_ref[ci, 0] = s0
            st_ref[ci, 1] = s1
            o0, o1, s0, s1 = _gla_chunk(q_ref[sl, :], k_ref[sl, :], la_ref[sl, :], v_ref[sl, 0:128],
                                        v_ref[sl, 128:256], s0, s1)
            o_ref[sl, 0:128] = o0
            o_ref[sl, 128:256] = o1
        s_sc[0] = s0
        s_sc[1] = s1

    q, k, lasp, v, o, st = _gla_specs(tb)
    return pl.pallas_call(
        kern, grid=(2, nb), in_specs=[q, k, lasp, v], out_specs=[o, st],
        out_shape=[jax.ShapeDtypeStruct((s, 512), F32),
                   jax.ShapeDtypeStruct((s // GLA_CHUNK, GLA_HEADS, 128, 128), F32)],
        scratch_shapes=[pltpu.VMEM((2, 128, 128), F32)],
        compiler_params=pltpu.CompilerParams(dimension_semantics=("parallel", "arbitrary"),
                                             vmem_limit_bytes=VMEM_LIMIT),
        name="gla_fwd")(proj, proj, la, proj)


def _gla_bwd(proj, la, states, d_o):
    s = proj.shape[0]
    tb = _tile(s, ROW_TILE)
    nb, nch = s // tb, tb // GLA_CHUNK

    def kern(q_ref, k_ref, la_ref, v_ref, do_ref, st_ref, dq_ref, dk_ref, dla_ref, dv_ref, ds_sc):
        @pl.when(pl.program_id(1) == 0)
        def _():
            ds_sc[...] = jnp.zeros_like(ds_sc)

        d0, d1 = ds_sc[0], ds_sc[1]
        for ci in reversed(range(nch)):
            sl = slice(ci * GLA_CHUNK, (ci + 1) * GLA_CHUNK)
            _, vjp = jax.vjp(_gla_chunk, q_ref[sl, :], k_ref[sl, :], la_ref[sl, :], v_ref[sl, 0:128],
                             v_ref[sl, 128:256], st_ref[ci, 0], st_ref[ci, 1])
            gq, gk, gla, gv0, gv1, d0, d1 = vjp((do_ref[sl, 0:128], do_ref[sl, 128:256], d0, d1))
            dq_ref[sl, :] = gq
            dk_ref[sl, :] = gk
            dla_ref[sl, :] = gla
            dv_ref[sl, 0:128] = gv0
            dv_ref[sl, 128:256] = gv1
        ds_sc[0] = d0
        ds_sc[1] = d1

    q, k, lasp, v, o, st = _gla_specs(tb, rev_nb=nb)
    return pl.pallas_call(
        kern, grid=(2, nb), in_specs=[q, k, lasp, v, o, st], out_specs=[lasp, lasp, lasp, o],
        out_shape=[jax.ShapeDtypeStruct((s, 256), F32), jax.ShapeDtypeStruct((s, 256), F32),
                   jax.ShapeDtypeStruct((s, 256), F32), jax.ShapeDtypeStruct((s, 512), F32)],
        scratch_shapes=[pltpu.VMEM((2, 128, 128), F32)],
        compiler_params=pltpu.CompilerParams(dimension_semantics=("parallel", "arbitrary"),
                                             vmem_limit_bytes=VMEM_LIMIT),
        name="gla_bwd")(proj, proj, la, proj, d_o, states)


def _causal_keep(t):
    return lax.broadcasted_iota(jnp.int32, (t, t), 1) <= lax.broadcasted_iota(jnp.int32, (t, t), 0)


def _split_refs(refs, counts):
    out, off = [], 0
    for cnt in counts:
        out.append(refs[off:off + cnt])
        off += cnt
    return out


def _causal_blocks(n, key_major):
    pairs = ([(ki, qi) for ki in range(n) for qi in range(ki, n)] if key_major else
             [(ki, qi) for qi in range(n) for ki in range(qi + 1)])
    return np.array([ki for ki, _ in pairs], np.int32), np.array([qi for _, qi in pairs], np.int32)


def _attn_fwd(q, k, v, comm, tile=2048):
    s = q.shape[0]
    t = _tile(s, tile)
    n = s // t
    nci, nco = len(comm.ins), len(comm.out_shape)

    ki_tab, qi_tab = _causal_blocks(n, key_major=False)
    steps = len(ki_tab)

    def kern(ki_ref, qi_ref, *refs):
        (q_ref, k_ref, v_ref), cins, (o_ref, lse_ref), couts, (m_sc, l_sc, acc_sc), csems = _split_refs(
            refs, (3, nci, 2, nco, 3, len(comm.sems)))
        pair, step = pl.program_id(0), pl.program_id(1)
        qi, ki = qi_ref[step], ki_ref[step]
        place = _place()

        @pl.when((pair == 0) & (step == 0))
        def _():
            comm.start(place, cins, couts, csems)

        @pl.when((pair == MLA_HEADS // 2 - 1) & (step == 0))
        def _():
            comm.mid(place, cins, couts, csems)

        first = lax.broadcasted_iota(jnp.int32, (t, LANES), 1) < MLA_V

        @pl.when(ki == 0)
        def _():
            m_sc[...] = jnp.full_like(m_sc, -jnp.inf)
            l_sc[...] = jnp.zeros_like(l_sc)
            acc_sc[...] = jnp.zeros_like(acc_sc)

        def update(rows, cols, masked):
            nr = rows.stop - rows.start
            sel = first[:nr]
            alphas, pvs = [], []
            for h in range(2):
                sc = _dg(q_ref[rows, 128 * h:128 * (h + 1)], k_ref[cols, 128 * h:128 * (h + 1)], _NT)
                if masked:
                    sc = jnp.where(_causal_keep(nr), sc, -jnp.inf)
                m_prev = m_sc[h, rows]
                m_new = jnp.maximum(m_prev, jnp.max(sc, axis=1, keepdims=True))
                alpha = jnp.exp2(m_prev - m_new)
                p = jnp.exp2(sc - m_new[:, 0:1])
                l_sc[h, rows] = alpha * l_sc[h, rows] + jnp.sum(p, axis=1, keepdims=True)
                m_sc[h, rows] = m_new
                alphas.append(alpha)
                pvs.append(_dg(p, v_ref[cols, :], _NN))
            acc_sc[rows] = acc_sc[rows] * jnp.where(sel, alphas[0], alphas[1]) + jnp.where(sel, pvs[0], pvs[1])

        halves = [slice(0, t)] if t % 256 else [slice(0, t // 2), slice(t // 2, t)]

        @pl.when(ki < qi)
        def _():
            for rows in halves:
                for cols in halves:
                    update(rows, cols, False)

        @pl.when(ki == qi)
        def _():
            for i, rows in enumerate(halves):
                for j, cols in enumerate(halves[:i + 1]):
                    update(rows, cols, i == j)

        @pl.when(ki == qi)
        def _():
            l = jnp.where(first, l_sc[0], l_sc[1])
            m = jnp.where(first, m_sc[0], m_sc[1])
            o_ref[...] = acc_sc[...] / l
            lse_ref[...] = m + jnp.log2(l)

        @pl.when((pair == MLA_HEADS // 2 - 1) & (step == steps - 1))
        def _():
            comm.finish(place, cins, couts, csems)

    q_idx = lambda p, st, ki_r, qi_r: (qi_r[st], p)
    k_idx = lambda p, st, ki_r, qi_r: (ki_r[st], p)
    res = pl.pallas_call(
        kern, grid_spec=pltpu.PrefetchScalarGridSpec(
            num_scalar_prefetch=2, grid=(MLA_HEADS // 2, steps),
            in_specs=[pl.BlockSpec((t, 256), q_idx), pl.BlockSpec((t, 256), k_idx), pl.BlockSpec((t, 128), k_idx)]
            + [ANY] * nci,
            out_specs=[pl.BlockSpec((t, 128), q_idx), pl.BlockSpec((t, 128), q_idx)] + [ANY] * nco,
            scratch_shapes=[pltpu.VMEM((2, t, LANES), F32), pltpu.VMEM((2, t, LANES), F32),
                            pltpu.VMEM((t, LANES), F32)] + comm.sems),
        out_shape=[jax.ShapeDtypeStruct((s, 512), F32), jax.ShapeDtypeStruct((s, 512), F32)] + comm.out_shape,
        compiler_params=pltpu.CompilerParams(dimension_semantics=("arbitrary", "arbitrary"),
                                             vmem_limit_bytes=VMEM_LIMIT),
        name="mla_attn_fwd")(ki_tab, qi_tab, q, k, v, *comm.ins)
    return res[0], res[1], res[2:]


def _attn_bwd(q, k, v, o, lse, d_o, comm, tile=1024):
    s = q.shape[0]
    t = _tile(s, tile)
    n = s // t
    nci, nco = len(comm.ins), len(comm.out_shape)

    ki_tab, qi_tab = _causal_blocks(n, key_major=True)
    steps = len(ki_tab)

    def kern(ki_ref, qi_ref, *refs):
        (q_ref, k_ref, v_ref, o_ref, lse_ref, do_ref), cins, (dq_ref, dk_ref, dv_ref), couts, (dk_sc, dv_sc), csems = \
            _split_refs(refs, (6, nci, 3, nco, 2, len(comm.sems)))
        pair, step = pl.program_id(0), pl.program_id(1)
        ki, qi = ki_ref[step], qi_ref[step]
        place = _place()

        @pl.when((pair == 0) & (step == 0))
        def _():
            comm.start(place, cins, couts, csems)

        @pl.when((pair == MLA_HEADS // 2 - 1) & (step == 0))
        def _():
            comm.mid(place, cins, couts, csems)

        @pl.when((ki == 0) & (qi == 0))
        def _():
            dq_ref[...] = jnp.zeros_like(dq_ref)

        @pl.when(qi == ki)
        def _():
            dk_sc[...] = jnp.zeros_like(dk_sc)
            dv_sc[...] = jnp.zeros_like(dv_sc)

        def update(rows, cols, masked):
            nr = rows.stop - rows.start
            d_o = do_ref[rows, :]
            prod = d_o * o_ref[rows, :]
            dq_rows = pl.ds(pl.multiple_of(qi * t + rows.start, nr), nr)
            for h in range(2):
                hs = slice(128 * h, 128 * (h + 1))
                mk = _lane_mask(MLA_V * h, MLA_V * (h + 1))
                qh, kh = q_ref[rows, hs], k_ref[cols, hs]
                sc = _dg(qh, kh, _NT)
                if masked:
                    sc = jnp.where(_causal_keep(nr), sc, -jnp.inf)
                p = jnp.exp2(sc - lse_ref[rows, MLA_V * h:MLA_V * h + 1])
                doh = d_o * mk
                dp = _dg(doh * LN2, v_ref[cols, :], _NT)
                delta = jnp.sum(prod * mk, axis=1, keepdims=True) * LN2
                ds = p * (dp - delta)
                dv_sc[cols, :] += _dg(p, doh, _TN)
                dk_sc[cols, hs] += _dg(ds, qh, _TN)
                dq_ref[dq_rows, hs] += _dg(ds, kh, _NN)

        halves = [slice(0, t)] if t % 256 else [slice(0, t // 2), slice(t // 2, t)]

        @pl.when(qi > ki)
        def _():
            for rows in halves:
                for cols in halves:
                    update(rows, cols, False)

        @pl.when(qi == ki)
        def _():
            for i, rows in enumerate(halves):
                for j, cols in enumerate(halves[:i + 1]):
                    update(rows, cols, i == j)

        @pl.when(qi == n - 1)
        def _():
            dk_ref[...] = dk_sc[...]
            dv_ref[...] = dv_sc[...].astype(dv_ref.dtype)

        @pl.when((pair == MLA_HEADS // 2 - 1) & (step == steps - 1))
        def _():
            comm.finish(place, cins, couts, csems)

    q_idx = lambda p, st, ki_r, qi_r: (qi_r[st], p)
    k_idx = lambda p, st, ki_r, qi_r: (ki_r[st], p)
    res = pl.pallas_call(
        kern, grid_spec=pltpu.PrefetchScalarGridSpec(
            num_scalar_prefetch=2, grid=(MLA_HEADS // 2, steps),
            in_specs=[pl.BlockSpec((t, 256), q_idx), pl.BlockSpec((t, 256), k_idx), pl.BlockSpec((t, 128), k_idx),
                      pl.BlockSpec((t, 128), q_idx), pl.BlockSpec((t, 128), q_idx), pl.BlockSpec((t, 128), q_idx)]
            + [ANY] * nci,
            out_specs=[pl.BlockSpec((s, 256), lambda p, st, ki_r, qi_r: (0, p)), pl.BlockSpec((t, 256), k_idx),
                       pl.BlockSpec((t, 128), k_idx)] + [ANY] * nco,
            scratch_shapes=[pltpu.VMEM((t, 256), F32), pltpu.VMEM((t, 128), F32)] + comm.sems),
        out_shape=[jax.ShapeDtypeStruct((s, 1024), F32), jax.ShapeDtypeStruct((s, 1024), F32),
                   jax.ShapeDtypeStruct((s, 512), BF16)] + comm.out_shape,
        compiler_params=pltpu.CompilerParams(dimension_semantics=("arbitrary", "arbitrary"),
                                             vmem_limit_bytes=VMEM_LIMIT),
        name="mla_attn_bwd")(ki_tab, qi_tab, q, k, v, o, lse, d_o, *comm.ins)
    return res[0], res[1], res[2], res[3:]


def _gate_fn(alr, w2, b):
    return _log_sigmoid(_dot_nn(alr, w2) + b) * (1.0 / GLA_GATE_NORM)


def _make_norm_rope(scale):
    def forward(x, w, c, sa, sb):
        r = lax.rsqrt(jnp.sum(x * x, axis=-1, keepdims=True) * (1.0 / MLA_QK) + EPS)
        y = x * r * w
        out = y * c + pltpu.roll(y, LANES - 16, 1) * sa + pltpu.roll(y, 16, 1) * sb
        return (out if scale == 1.0 else out * scale), r

    @jax.custom_vjp
    def norm_rope(x, w, c, sa, sb):
        return forward(x, w, c, sa, sb)[0]

    def fwd(x, w, c, sa, sb):
        out, r = forward(x, w, c, sa, sb)
        return out, (x, w, c, sa, sb, r)

    def bwd(res, g):
        x, w, c, sa, sb, r = res
        if scale != 1.0:
            g = g * scale
        gy = g * c + pltpu.roll(g * sa, 16, 1) + pltpu.roll(g * sb, LANES - 16, 1)
        xr = x * r
        t = gy * w
        m = jnp.sum(t * xr, axis=-1, keepdims=True) * (1.0 / MLA_QK)
        return r * (t - xr * m), jnp.sum(gy * xr, axis=0, keepdims=True), jnp.zeros_like(c), jnp.zeros_like(sa), jnp.zeros_like(sb)

    norm_rope.defvjp(fwd, bwd)
    return norm_rope


_q_norm_rope = _make_norm_rope(MLA_QK ** -0.5 * LOG2E)
_k_norm_rope = _make_norm_rope(1.0)


def _qk_head(qh, kh, kpe, c, sa, sb, qn, kn):
    kfull = kh + kpe * _lane_mask(MLA_NOPE, MLA_QK)
    return _q_norm_rope(qh, qn, c, sa, sb), _k_norm_rope(kfull, kn, c, sa, sb)


def _mix_head(o, og, gn):
    return _rms(o, gn) * _silu(og)


def _xa_head(xq, xk, xv, qn, kn):
    sc = _dot_nt(_rms(xq, qn), _rms(xk, kn)) * (XA_DIM ** -0.5)
    e = jnp.exp(sc - lax.stop_gradient(jnp.max(sc, axis=1, keepdims=True)))
    p = e / jnp.sum(e, axis=1, keepdims=True)
    return _dot_nn(p, xv)


def _heads(x, n):
    return [x[:, 128 * h:128 * (h + 1)] for h in range(n)]


def _cat(xs):
    return jnp.concatenate(xs, axis=1)


def _norm_fwd(x, w, name):
    return _rows_call(lambda r, c: ([_rms(r[0], c[0])], []), [_row(x)], [w], [(x.shape[1], BF16)], name=name)[0]


def _norm_fwd_epilogue(w):
    return _Epilogue(lambda h, rows, consts: ([h, _rms(h, consts[0])], []), [], [w], [(D_MODEL, F32), (D_MODEL, BF16)], [])


def _norm_bwd_epilogue(x, w, add):
    def fn(d_out, rows, consts):
        _, vjp = jax.vjp(_rms, rows[0], consts[0])
        dx, dw = vjp(d_out)
        return [dx + rows[1]], [dw]

    return _Epilogue(fn, [_row(x), _row(add)], [w], [(D_MODEL, F32)], [w.shape])


def _norm_fwd_comm(x, w, comm, name):
    s, d = x.shape
    t = _tile(s, ROW_TILE)
    n = s // t
    nci, nco = len(comm.ins), len(comm.out_shape)

    def kern(*refs):
        (x_ref, w_ref), cins, (o_ref,), couts, csems = _split_refs(refs, (2, nci, 1, nco, len(comm.sems)))
        place = _place()

        @pl.when(pl.program_id(0) == 0)
        def _():
            comm.start(place, cins, couts, csems)

        o_ref[...] = _rms(x_ref[...], w_ref[...]).astype(o_ref.dtype)

        @pl.when(pl.program_id(0) == n - 1)
        def _():
            comm.mid(place, cins, couts, csems)
            comm.finish(place, cins, couts, csems)

    tile = pl.BlockSpec((t, d), lambda i: (i, 0))
    res = pl.pallas_call(
        kern, grid=(n,), in_specs=[tile, pl.BlockSpec(w.shape, lambda i: (0, 0))] + [ANY] * nci,
        out_specs=[tile] + [ANY] * nco, out_shape=[jax.ShapeDtypeStruct((s, d), BF16)] + comm.out_shape,
        scratch_shapes=comm.sems,
        compiler_params=pltpu.CompilerParams(dimension_semantics=("arbitrary",), vmem_limit_bytes=VMEM_LIMIT),
        name=name)(x, w, *comm.ins)
    return res[0], res[1:]


def _norm_bwd(x, w, d_out, add, name):
    def body(r, c):
        _, vjp = jax.vjp(_rms, r[0], c[0])
        dx, dw = vjp(r[1])
        return [dx + r[2]], [dw]

    return _rows_call(body, [_row(x), _row(d_out), _row(add)], [w], [(x.shape[1], F32)], [w.shape], name=name)


CONV_HALO = BF16_ROWS


def _conv_specs(s, f, t):
    n8 = t // CONV_HALO
    cur = pl.BlockSpec((None, t, f), lambda j, i: (j, i, 0))
    prev = pl.BlockSpec((None, CONV_HALO, f), lambda j, i: (j, jnp.maximum(i * n8 - 1, 0), 0))
    nxt = pl.BlockSpec((None, CONV_HALO, f), lambda j, i: (j, jnp.minimum((i + 1) * n8, s // CONV_HALO - 1), 0))
    cw = pl.BlockSpec((None, 3, f), lambda j, i: (j, 0, 0))
    cb = pl.BlockSpec((None, 1, f), lambda j, i: (j, 0, 0))
    return cur, prev, nxt, cw, cb


def _conv_taps(g, prev, first):
    ext = jnp.concatenate([jnp.where(first, 0.0, prev.astype(F32)), g], axis=0)
    return pltpu.roll(ext, 1, 0)[CONV_HALO:], pltpu.roll(ext, 2, 0)[CONV_HALO:]


def _conv_fwd(gg, uu, cw, cb, comm):
    _, s, f = gg.shape
    t = _tile(s, ROW_TILE)
    nt = s // t
    nci, nco = len(comm.ins), len(comm.out_shape)

    def kern(*refs):
        (g_ref, gp_ref, u_ref, cw_ref, cb_ref), cins, (o_ref,), couts, csems = _split_refs(
            refs, (5, nci, 1, nco, len(comm.sems)))
        shard, i = pl.program_id(0), pl.program_id(1)
        place = _place()

        @pl.when((shard == 0) & (i == 0))
        def _():
            comm.start(place, cins, couts, csems)

        @pl.when((shard == 3) & (i == 0))
        def _():
            comm.mid(place, cins, couts, csems)

        g = g_ref[...].astype(F32)
        g1, g2 = _conv_taps(g, gp_ref[...], i == 0)
        w = cw_ref[...]
        gc = cb_ref[...] + w[0:1] * g2 + w[1:2] * g1 + w[2:3] * g
        o_ref[...] = (_silu(gc) * u_ref[...].astype(F32)).astype(o_ref.dtype)

        @pl.when((shard == 3) & (i == nt - 1))
        def _():
            comm.finish(place, cins, couts, csems)

    cur, prev, _, cws, cbs = _conv_specs(s, f, t)
    res = pl.pallas_call(
        kern, grid=(4, nt), in_specs=[cur, prev, cur, cws, cbs] + [ANY] * nci, out_specs=[cur] + [ANY] * nco,
        out_shape=[jax.ShapeDtypeStruct(gg.shape, BF16)] + comm.out_shape, scratch_shapes=comm.sems,
        compiler_params=pltpu.CompilerParams(dimension_semantics=("arbitrary", "arbitrary"), vmem_limit_bytes=VMEM_LIMIT),
        name="ffn_conv_fwd")(gg, gg, uu, cw, cb, *comm.ins)
    return res[0], res[1:]


def _conv_bwd(gg, uu, dact, cw, cb):
    _, s, f = gg.shape
    t = _tile(s, ROW_TILE)
    nt = s // t

    def kern(g_ref, gp_ref, gn_ref, u_ref, un_ref, da_ref, dan_ref, cw_ref, cb_ref, du_ref, dg_ref, dcw_ref, dcb_ref):
        i = pl.program_id(1)
        cat = lambda a_ref, b_ref: jnp.concatenate([a_ref[...].astype(F32), b_ref[...].astype(F32)], axis=0)
        g, u, da = cat(g_ref, gn_ref), cat(u_ref, un_ref), cat(da_ref, dan_ref)
        g1, g2 = _conv_taps(g, gp_ref[...], i == 0)
        w = cw_ref[...]
        gc = cb_ref[...] + w[0:1] * g2 + w[1:2] * g1 + w[2:3] * g
        sg = jax.nn.sigmoid(gc)
        du_ref[...] = (da[:t] * (gc[:t] * sg[:t])).astype(du_ref.dtype)
        row = lax.broadcasted_iota(jnp.int32, (t + CONV_HALO, 1), 0)
        dgc = jnp.where((row < t) | (i < nt - 1), da * u * (sg * (1.0 + gc * (1.0 - sg))), 0.0)
        up1 = pltpu.roll(dgc, t + CONV_HALO - 1, 0)[:t]
        up2 = pltpu.roll(dgc, t + CONV_HALO - 2, 0)[:t]
        dgc = dgc[:t]
        dg_ref[...] = (w[2:3] * dgc + w[1:2] * up1 + w[0:1] * up2).astype(dg_ref.dtype)

        @pl.when(i == 0)
        def _():
            dcw_ref[...] = jnp.zeros_like(dcw_ref)
            dcb_ref[...] = jnp.zeros_like(dcb_ref)

        ones = jnp.ones((8, t), BF16)
        col_sum = lambda a: _dg(ones, a, _NN)[0:1]
        dcw_ref[0:1, :] += col_sum(dgc * g2[:t])
        dcw_ref[1:2, :] += col_sum(dgc * g1[:t])
        dcw_ref[2:3, :] += col_sum(dgc * g[:t])
        dcb_ref[...] += col_sum(dgc)

    cur, prev, nxt, cws, cbs = _conv_specs(s, f, t)
    return pl.pallas_call(
        kern, grid=(4, nt), in_specs=[cur, prev, nxt, cur, nxt, cur, nxt, cws, cbs], out_specs=[cur, cur, cws, cbs],
        out_shape=[jax.ShapeDtypeStruct(gg.shape, BF16), jax.ShapeDtypeStruct(gg.shape, BF16),
                   jax.ShapeDtypeStruct(cw.shape, F32), jax.ShapeDtypeStruct(cb.shape, F32)],
        compiler_params=pltpu.CompilerParams(dimension_semantics=("parallel", "arbitrary"), vmem_limit_bytes=VMEM_LIMIT),
        name="ffn_conv_bwd")(gg, gg, gg, uu, uu, dact, dact, cw, cb)


def _rope_tables(pos):
    half = MLA_ROPE // 2
    lane = jnp.arange(LANES)
    rotary = (lane >= MLA_NOPE) & (lane < MLA_QK)
    inv = jnp.where(rotary, ROPE_THETA ** (-((lane - MLA_NOPE) % half).astype(F32) / half), 0.0)
    ang = pos.astype(F32)[:, None] * inv
    cos, sin = jnp.cos(ang), jnp.sin(ang)
    first = rotary & (lane < MLA_NOPE + half)
    return cos, jnp.where(first, -sin, 0.0), jnp.where(rotary & ~first, sin, 0.0)


def _local_step(x, mem, pos, target, rep, early_shards, late_shards):
    g = {}
    c, sa, sb = _rope_tables(pos)

    xn, gathered = _norm_fwd_comm(x, rep["norm_mix"], _gather_plan(early_shards), "norm_mix_fwd_gather")
    w = _early_layout(dict(zip(EARLY, gathered, strict=True)), rep)

    def proj_fn(r, rows, k):
        la_ = _gate_fn(r[:, P_ALR:P_ALR + 128], k[0], k[1])
        return [r, la_, _rms(r[:, P_CQ:P_CQ + MLA_Q_RANK], k[2]), _rms(r[:, P_CKV:P_CKV + MLA_KV_RANK], k[3])], []

    proj, la, q_lat, kv_lat = _matmul(
        xn, w["in"], "nt", F32, "proj_fwd", epilogue=_Epilogue(
            proj_fn, [], [w["w2"], w["gate_b"], w["q_a_norm"], w["kv_a_norm"]],
            [(P_WIDTH, F32), (256, F32), (MLA_Q_RANK, BF16), (MLA_KV_RANK, BF16)], []))
    alr = _row(proj, 128, P_ALR // 128)
    kpe = _row(proj, 128, P_KPE // 128)
    og = _row(proj, 512, P_OG // 512)
    cq = _row(proj, 256, P_CQ // 256)
    ckv = _row(proj, 128, P_CKV // 128)

    o_gla, states = _gla_fwd(proj, la)

    def qk_body(r, k):
        q_up, k_up = _dg(r[0], k[0], _NN), _dg(r[1], k[1], _NN)
        qs, ks = [], []
        for qh, kh in zip(_heads(q_up, MLA_HEADS), _heads(k_up, MLA_HEADS)):
            a, b = _qk_head(qh, kh, r[2], r[3], r[4], r[5], k[3], k[4])
            qs.append(a)
            ks.append(b)
        return [_cat(qs), _cat(ks), _dg(r[1], k[2], _NN)], []

    tabs = [_row(c), _row(sa), _row(sb)]
    qk_consts = [w["uq"], w["k"], w["v"], w["q_norm"], w["k_norm"]]
    q_r, k_r, v_mla = _rows_call(qk_body, [_row(q_lat), _row(kv_lat), kpe] + tabs, qk_consts,
                                 [(1024, BF16), (1024, BF16), (512, BF16)], name="mla_qk_fwd")
    with_attn = [n for n in LATE if n not in LAST]
    o_mla, lse, gathered = _attn_fwd(q_r, k_r, v_mla, _gather_plan([late_shards[n] for n in with_attn]))
    w.update(_late_layout(dict(zip(with_attn, gathered, strict=True))))

    def mix_body(r, k):
        ys = [_mix_head(o, g_, k[0]) for o, g_ in zip(_heads(r[0], GLA_HEADS), _heads(r[1], GLA_HEADS))]
        return [_cat(ys + [r[2]])], []

    cat = _rows_call(mix_body, [_row(o_gla), og, _row(o_mla)], [w["gla_out_norm"]], [(1024, BF16)],
                     name="mix_fwd")[0]
    h1, hn = _matmul(cat, w["out"], "nn", F32, "out_fwd_norm", residual=x, epilogue=_norm_fwd_epilogue(w["norm_xa"]))
    mn = _norm_fwd(mem, w["norm_mem"], "norm_mem_fwd")
    xkv = _matmul(mn, w["xkv"], "nn", F32, "xa_kv_fwd")

    def xa_fn(r, rows, k):
        ks, vs = _heads(k[0], 2 * XA_HEADS)[:XA_HEADS], _heads(k[0], 2 * XA_HEADS)[XA_HEADS:]
        return [r, _cat([_xa_head(a, b, v_, k[1], k[2]) for a, b, v_ in zip(_heads(r, XA_HEADS), ks, vs)])], []

    xq, xo = _matmul(hn, w["xq"], "nn", F32, "xa_q_fwd_attn", epilogue=_Epilogue(
        xa_fn, [], [xkv, w["xa_q_norm"], w["xa_k_norm"]], [(512, F32), (512, BF16)], []))
    h2, fn = _matmul(xo, w["xo"], "nn", F32, "xa_o_fwd_norm", residual=h1, epilogue=_norm_fwd_epilogue(w["norm_ffn"]))
    gg = _matmul(fn, w["wg"], "nt", BF16, "ffn_gate_fwd", b_lead="p")
    uu = _matmul(fn, w["wu"], "nt", BF16, "ffn_up_fwd", b_lead="p")
    act, gathered = _conv_fwd(gg, uu, w["cw"], w["cb"], _gather_plan([late_shards[n] for n in LAST]))
    w["wd"] = gathered[0]
    def loss_fn(y, rows, consts):
        err = y - rows[0]
        part = 0.5 * jnp.sum(jnp.sum(err * err, axis=1, keepdims=True) * (1.0 / D_MODEL), axis=0, keepdims=True)
        return [err * (1.0 / D_MODEL)], [jnp.broadcast_to(part, (1, LANES))]

    dy, loss = _matmul(act, w["wd"], "nn", F32, "ffn_down_fwd_loss", residual=h2, a_lead="k", b_lead="k",
                       epilogue=_Epilogue(loss_fn, [_row(target)], [], [(D_MODEL, F32)], [(1, LANES)]))

    g["ffn_w_down"] = _matmul(act, dy, "tn", BF16, "ffn_down_dw", a_lead="p")
    dact = _matmul(dy, w["wd"], "nt", BF16, "ffn_down_dx", b_lead="p")
    duu, dgg, g["ffn_conv_w"], g["ffn_conv_b"] = _conv_bwd(gg, uu, dact, w["cw"], w["cb"])
    g["ffn_w_gate"] = _matmul(dgg, fn, "tn", BF16, "ffn_gate_dw", a_lead="p")
    g["ffn_w_up"] = _matmul(duu, fn, "tn", BF16, "ffn_up_dw", a_lead="p")
    dh2, g["norm_ffn"] = _matmul(dgg, w["wg"], "nn", F32, "ffn_dx_norm_bwd", a_lead="k", b_lead="k", more=(duu, w["wu"]),
                                 epilogue=_norm_bwd_epilogue(h2, w["norm_ffn"], dy))

    g["xa_w_o"] = _matmul(xo, dh2, "tn", BF16, "xa_o_dw")
    def xa_bwd(dxo_, rows, k):
        kvh = _heads(k[0], 2 * XA_HEADS)
        dq_, dk_, dv_ = [], [], []
        dqn, dkn = 0.0, 0.0
        for h, (a, d_) in enumerate(zip(_heads(rows[0], XA_HEADS), _heads(dxo_, XA_HEADS))):
            _, vjp = jax.vjp(_xa_head, a, kvh[h], kvh[XA_HEADS + h], k[1], k[2])
            ga, gk, gv, gqn, gkn = vjp(d_)
            dq_.append(ga)
            dk_.append(gk)
            dv_.append(gv)
            dqn, dkn = dqn + gqn, dkn + gkn
        return [_cat(dq_)], [_cat(dk_ + dv_), dqn, dkn]

    dxq, dxkv, g["xa_q_norm"], g["xa_k_norm"] = _matmul(dh2, w["xo"], "nt", F32, "xa_o_dx_attn_bwd", epilogue=_Epilogue(
        xa_bwd, [_row(xq)], [xkv, w["xa_q_norm"], w["xa_k_norm"]], [(512, BF16)], [xkv.shape, (1, 128), (1, 128)]))
    g["xa_w_q"] = _matmul(hn, dxq, "tn", BF16, "xa_q_dw")
    dh1, g["norm_xa"] = _matmul(dxq, w["xq"], "nt", F32, "xa_q_dx_norm_bwd",
                                epilogue=_norm_bwd_epilogue(h1, w["norm_xa"], dh2))
    g["xa_w_kv"] = _matmul(mn, dxkv, "tn", BF16, "xa_kv_dw")
    dmn = _matmul(dxkv, w["xkv"], "nt", F32, "xa_kv_dx")
    _, g["norm_mem"] = _norm_bwd(mem, w["norm_mem"], dmn, dmn, "norm_mem_bwd")

    g["w_out"] = _matmul(cat, dh1, "tn", BF16, "out_dw")
    def mix_bwd(dcat_, rows, k):
        do_, dog_ = [], []
        dgn = 0.0
        for o, g_, d_ in zip(_heads(rows[0], GLA_HEADS), _heads(rows[1], GLA_HEADS), _heads(dcat_, GLA_HEADS)):
            _, vjp = jax.vjp(_mix_head, o, g_, k[0])
            a, b, gn_ = vjp(d_)
            do_.append(a)
            dog_.append(b)
            dgn = dgn + gn_
        return [_cat(do_), _cat(dog_), dcat_[:, 512:]], [dgn]

    do_gla, d_og, do_mla, g["gla_out_norm"] = _matmul(dh1, w["out"], "nt", F32, "out_dx_mix_bwd", epilogue=_Epilogue(
        mix_bwd, [_row(o_gla), og], [w["gla_out_norm"]], [(512, F32), (512, BF16), (512, F32)], [(1, 128)]))

    late_parts = _late_grad_shards(g)
    dq_r, dk_r, dv_mla, lands_late = _attn_bwd(q_r, k_r, v_mla, o_mla, lse, do_mla,
                                               _scatter_plan([late_parts[n] for n in LATE]))
    lands_late = dict(zip(LATE, lands_late, strict=True))

    def qk_bwd(r, k):
        q_up, k_up = _dg(r[0], k[0], _NN), _dg(r[1], k[1], _NN)
        dqs, dks = [], []
        dkpe, dqn, dkn = 0.0, 0.0, 0.0
        for qh, kh, dqh, dkh in zip(_heads(q_up, MLA_HEADS), _heads(k_up, MLA_HEADS), _heads(r[6], MLA_HEADS),
                                    _heads(r[7], MLA_HEADS)):
            _, vjp = jax.vjp(lambda a, b, e, f, h_: _qk_head(a, b, e, r[3], r[4], r[5], f, h_), qh, kh, r[2], k[3], k[4])
            ga, gb, ge, gf, gh = vjp((dqh, dkh))
            dqs.append(ga)
            dks.append(gb)
            dkpe, dqn, dkn = dkpe + ge, dqn + gf, dkn + gh
        dq_up, dk_up, dv = _cat(dqs), _cat(dks), r[8]
        dq_lat_ = _dg(dq_up, k[0], _NT)
        dkv_lat_ = _dg(dk_up, k[1], _NT) + _dg(dv, k[2], _NT)
        return [dq_lat_, dkv_lat_, dkpe], [dqn, dkn, _dg(r[0], dq_up, _TN), _dg(r[1], dk_up, _TN), _dg(r[1], dv, _TN)]

    dq_lat, dkv_lat, d_kpe, g["q_norm"], g["k_norm"], g["uq"], g["k"], g["v"] = _rows_call(
        qk_bwd, [_row(q_lat), _row(kv_lat), kpe] + tabs + [_row(dq_r), _row(dk_r), _row(dv_mla)], qk_consts,
        [(MLA_Q_RANK, F32), (MLA_KV_RANK, F32), (128, BF16)],
        [(1, 128), (1, 128), w["uq"].shape, w["k"].shape, w["v"].shape], name="mla_qk_bwd")

    dgq, dgk, dla, dgv = _gla_bwd(proj, la, states, do_gla)

    def dproj_body(r, k):
        alr_, cq_, ckv_, dla_, dq_lat_, dkv_lat_, dgq_, dgk_, dgv_, d_og_, d_kpe_ = r
        _, gate_vjp = jax.vjp(_gate_fn, alr_, k[0], k[1])
        d_alr, gw2, gb = gate_vjp(dla_)
        _, q_vjp = jax.vjp(_rms, cq_, k[2])
        _, kv_vjp = jax.vjp(_rms, ckv_, k[3])
        d_cq, gqa = q_vjp(dq_lat_)
        d_ckv, gkva = kv_vjp(dkv_lat_)
        pieces = [dgq_, dgk_, dgv_, d_og_, d_cq, d_ckv, d_kpe_, d_alr]
        return [_cat([x_.astype(BF16) for x_ in pieces])], [gw2, gb, gqa, gkva]

    dproj, g["w2"], g["gla_gate_b"], g["mla_q_a_norm"], g["mla_kv_a_norm"] = _rows_call(
        dproj_body, [alr, cq, ckv, _row(dla), _row(dq_lat), _row(dkv_lat), _row(dgq), _row(dgk), _row(dgv), _row(d_og),
                     _row(d_kpe)], [w["w2"], w["gate_b"], w["q_a_norm"], w["kv_a_norm"]], [(P_WIDTH, BF16)],
        [(128, 256), (1, 256), (1, 256), (1, 128)], name="proj_cotangent")
    g["in"] = _matmul(dproj, xn, "tn", BF16, "proj_dw")
    dx, g["norm_mix"] = _matmul(dproj, w["in"], "nn", F32, "proj_dx_norm_bwd",
                                epilogue=_norm_bwd_epilogue(x, w["norm_mix"], dh1))
    return loss[0, 0], dx, g, lands_late


def _join_shards(pieces, axis):
    if axis == 0:
        return pieces.reshape(-1, pieces.shape[2])
    return jnp.transpose(pieces, (1, 0, 2)).reshape(pieces.shape[1], -1)


def _split_shards(full, axis):
    r, c = full.shape
    if axis == 0:
        return full.reshape(4, r // 4, c)
    return jnp.transpose(full.reshape(r, 4, c // 4), (1, 0, 2))


def _early_layout(gath, rep):
    w_in = gath["w_in"].reshape(N_WIDTH, D_MODEL)
    z = lambda n: jnp.zeros((n, D_MODEL), w_in.dtype)
    seg = lambda lo, n: w_in[lo:lo + n]
    ukv = _join_shards(gath["mla_w_ukv"], 1).reshape(MLA_KV_RANK, MLA_HEADS, MLA_NOPE + MLA_V)
    w = {
        "in": jnp.concatenate([seg(N_GQ, 256), seg(N_GK, 256), seg(N_GV, 512), seg(N_OG, 512), seg(N_CQ, 256),
                               seg(N_CKV, 128), z(64), seg(N_KPE, 32), z(32), seg(N_ALR, 16), z(112)], axis=0),
        "uq": jnp.pad(_join_shards(gath["mla_w_uq"], 1).reshape(MLA_Q_RANK, MLA_HEADS, MLA_QK),
                      ((0, 0), (0, 0), (0, LANES - MLA_QK))).reshape(MLA_Q_RANK, MLA_HEADS * LANES),
        "k": jnp.pad(ukv[:, :, :MLA_NOPE], ((0, 0), (0, 0), (0, LANES - MLA_NOPE))).reshape(MLA_KV_RANK, -1),
        "v": ukv[:, :, MLA_NOPE:].reshape(MLA_KV_RANK, MLA_HEADS * MLA_V),
        "w2": jnp.pad(_join_shards(gath["gla_gate_w2"], 1), ((0, LANES - GLA_RANK), (0, 0))),
        "cb": rep["ffn_conv_b"].reshape(4, 1, D_FF // 4),
        "q_norm": jnp.pad(rep["mla_q_norm"], ((0, 0), (0, LANES - MLA_QK))),
        "k_norm": jnp.pad(rep["mla_k_norm"], ((0, 0), (0, LANES - MLA_QK))),
        "q_a_norm": rep["mla_q_a_norm"], "kv_a_norm": rep["mla_kv_a_norm"], "gate_b": rep["gla_gate_b"],
    }
    for n in ("norm_mix", "gla_out_norm", "norm_xa", "norm_mem", "xa_q_norm", "xa_k_norm", "norm_ffn"):
        w[n] = rep[n]
    return w


def _late_layout(gath):
    return {"out": _join_shards(gath["w_out"], 0), "xq": _join_shards(gath["xa_w_q"], 0),
            "xkv": _join_shards(gath["xa_w_kv"], 0), "xo": _join_shards(gath["xa_w_o"], 1),
            "wg": gath["ffn_w_gate"], "wu": gath["ffn_w_up"], "cw": gath["ffn_conv_w"]}


def _late_grad_shards(g):
    sh = {"w_out": _split_shards(g["w_out"], 0), "xa_w_q": _split_shards(g["xa_w_q"], 0),
          "xa_w_kv": _split_shards(g["xa_w_kv"], 0), "xa_w_o": _split_shards(g["xa_w_o"], 1),
          "ffn_w_gate": g["ffn_w_gate"], "ffn_w_up": g["ffn_w_up"], "ffn_conv_w": g["ffn_conv_w"],
          "ffn_w_down": g["ffn_w_down"]}
    return {n: v.astype(BF16) for n, v in sh.items()}


def _early_grad_shards(g):
    gi = g["in"]
    seg = lambda lo, n: gi[lo:lo + n]
    w_in = jnp.concatenate([seg(P_GQ, 256), seg(P_GK, 256), seg(P_GV, 512), seg(P_ALR, 16), seg(P_OG, 512),
                            seg(P_CQ, 256), seg(P_CKV, 128), seg(P_KPE + 64, 32)], axis=0)
    uq = g["uq"].reshape(MLA_Q_RANK, MLA_HEADS, LANES)[:, :, :MLA_QK].reshape(MLA_Q_RANK, -1)
    ukv = jnp.concatenate([g["k"].reshape(MLA_KV_RANK, MLA_HEADS, LANES)[:, :, :MLA_NOPE],
                           g["v"].reshape(MLA_KV_RANK, MLA_HEADS, MLA_V)], axis=2).reshape(MLA_KV_RANK, -1)
    sh = {"w_in": w_in.reshape(4, N_WIDTH // 4, D_MODEL), "gla_gate_w2": _split_shards(g["w2"][:GLA_RANK], 1),
          "mla_w_uq": _split_shards(uq, 1), "mla_w_ukv": _split_shards(ukv, 1)}
    sh = {n: v.astype(BF16) for n, v in sh.items()}
    rep = {n: g[n] for n in REPLICATED if n in g}
    rep["mla_q_norm"] = g["q_norm"][:, :MLA_QK]
    rep["mla_k_norm"] = g["k_norm"][:, :MLA_QK]
    rep["ffn_conv_b"] = g["ffn_conv_b"].reshape(1, D_FF)
    return sh, rep


SMALL_SHAPE = (8, 1024)


def _pack_small(vectors):
    flat = jnp.concatenate(vectors, axis=1)
    return jnp.pad(flat, ((0, 0), (0, SMALL_SHAPE[0] * SMALL_SHAPE[1] - flat.shape[1]))).reshape(SMALL_SHAPE)


def _unpack_small(buf, widths):
    flat = buf.reshape(1, -1)
    out, off = [], 0
    for wd in widths:
        out.append(flat[:, off:off + wd])
        off += wd
    return out


ANY = pl.BlockSpec(memory_space=pl.ANY)


def _place():
    x, y, c = lax.axis_index("x"), lax.axis_index("y"), lax.axis_index("c")
    chips = [(1 - x, y), (x, 1 - y), (1 - x, 1 - y)]
    return x, y, c, chips


class _Comm:
    def __init__(self, ins, out_shape, sems, start, finish, mid=None):
        self.ins, self.out_shape, self.sems = list(ins), list(out_shape), list(sems)
        self.start, self.finish, self.mid = start, finish, mid or (lambda *args: None)


def _gather_plan(shards):
    n = len(shards)
    by_rows = [s.shape[0] % (2 * BF16_ROWS) == 0 for s in shards]
    by_cols = [not r and s.shape[1] % (2 * LANES) == 0 for r, s in zip(by_rows, shards)]
    split = [r or c for r, c in zip(by_rows, by_cols)]

    def rows(ref, t, c):
        if by_rows[t]:
            half = shards[t].shape[0] // 2
            return ref.at[pl.ds(pl.multiple_of(c * half, BF16_ROWS), half)]
        if by_cols[t]:
            half = shards[t].shape[1] // 2
            return ref.at[:, pl.ds(pl.multiple_of(c * half, LANES), half)]
        return ref

    def remote(src, dst, ss, rs, to):
        return pltpu.make_async_remote_copy(src_ref=src, dst_ref=dst, send_sem=ss, recv_sem=rs, device_id=to,
                                            device_id_type=MESH)

    def first_wave(place, ins, outs, sems):
        x, y, c, chips = place
        ici_s, ici_r, _, _, local = sems
        me = 2 * x + y
        own = [pltpu.make_async_copy(ins[t], outs[t].at[me], local.at[t]) for t in range(n)]
        push = [remote(rows(ins[t], t, c), rows(outs[t].at[me], t, c), ici_s.at[3 * t + j], ici_r.at[3 * t + j], (px, py, c))
                for t in range(n) for j, (px, py) in enumerate(chips)]
        return own, push

    def second_wave(place, ins, outs, sems, last):
        x, y, c, chips = place
        ici_s, ici_r, d2d_s, d2d_r, local = sems
        sib = (x, y, 1 - c)
        out = []
        for t in range(n):
            for j, (px, py) in enumerate(chips):
                block = outs[t].at[2 * px + py]
                got = rows(block, t, c)
                if split[t]:
                    hand = remote(got, got, d2d_s.at[3 * t + j], d2d_r.at[3 * t + j], sib)
                    theirs = rows(block, t, 1 - c)
                    other = (remote(theirs, theirs, local.at[0], d2d_r.at[3 * t + j], sib) if last else
                             remote(got, got, local.at[0], ici_r.at[3 * t + j], sib))
                    out.append((other, hand))
                elif last:
                    out.append((remote(got, got, local.at[0], ici_r.at[3 * t + j], sib), None))
        return out

    def start(place, ins, outs, sems):
        own, push = first_wave(place, ins, outs, sems)
        for cp in own + push:
            cp.start()

    def mid(place, ins, outs, sems):
        for arrival, hand in second_wave(place, ins, outs, sems, False):
            arrival.wait_recv()
            hand.start()

    def finish(place, ins, outs, sems):
        own, push = first_wave(place, ins, outs, sems)
        for arrival, hand in second_wave(place, ins, outs, sems, True):
            arrival.wait_recv()
            if hand is not None:
                hand.wait_send()
        for cp in push:
            cp.wait_send()
        for cp in own:
            cp.wait()

    dma = pltpu.SemaphoreType.DMA
    return _Comm(shards, [jax.ShapeDtypeStruct((4,) + s.shape, s.dtype) for s in shards],
                 [dma((3 * n,)), dma((3 * n,)), dma((3 * n,)), dma((3 * n,)), dma((n,))], start, finish, mid)


def _scatter_plan(parts, small=None):
    n = len(parts)
    ns = 0 if small is None else 1

    def unpack(place, ins, outs, sems):
        x, y, c, chips = place
        return x, y, c, chips, 2 * x + y, 4 * x + 2 * y + c, (x, y, 1 - c)

    def remote(src, dst, ss, rs, to):
        return pltpu.make_async_remote_copy(src_ref=src, dst_ref=dst, send_sem=ss, recv_sem=rs, device_id=to,
                                            device_id_type=MESH)

    def first_wave(place, ins, outs, sems):
        x, y, c, chips, me, dev, sib = unpack(place, ins, outs, sems)
        ici_s, ici_r, d2d_s, d2d_r, sm_s, sm_r, local = sems
        own, push = [], []
        if ns:
            own.append(pltpu.make_async_copy(ins[n], outs[n].at[dev], local.at[n]))
            for k in range(1, 8):
                px = (1 - x) if (k >> 2) & 1 else x
                py = (1 - y) if (k >> 1) & 1 else y
                pc = (1 - c) if k & 1 else c
                push.append(remote(ins[n], outs[n].at[dev], sm_s.at[k - 1], sm_r.at[k - 1], (px, py, pc)))
        for t in range(n):
            own.append(pltpu.make_async_copy(ins[t].at[me], outs[t].at[dev], local.at[t]))
            push.append(remote(ins[t].at[me], outs[t].at[dev], d2d_s.at[4 * t], d2d_r.at[4 * t], sib))
            for j, (px, py) in enumerate(chips):
                push.append(remote(ins[t].at[2 * px + py], outs[t].at[dev], ici_s.at[3 * t + j], ici_r.at[3 * t + j],
                                   (px, py, c)))
        return own, push

    def start(place, ins, outs, sems):
        own, push = first_wave(place, ins, outs, sems)
        for cp in own + push:
            cp.start()

    def landed(dst, rs, sems, sib):
        remote(dst, dst, sems[-1].at[0], rs, sib).wait_recv()

    def forwards(place, ins, outs, sems):
        x, y, c, chips, me, dev, sib = unpack(place, ins, outs, sems)
        d2d_s, d2d_r = sems[2], sems[3]
        slots = [(t, j, outs[t].at[4 * px + 2 * py + c]) for t in range(n) for j, (px, py) in enumerate(chips)]
        return [(t, j, slot, remote(slot, slot, d2d_s.at[4 * t + 1 + j], d2d_r.at[4 * t + 1 + j], sib))
                for t, j, slot in slots]

    def mid(place, ins, outs, sems):
        sib = unpack(place, ins, outs, sems)[-1]
        for t, j, slot, cp in forwards(place, ins, outs, sems):
            landed(slot, sems[1].at[3 * t + j], sems, sib)
            cp.start()

    def finish(place, ins, outs, sems):
        x, y, c, chips, me, dev, sib = unpack(place, ins, outs, sems)
        d2d_r, sm_r = sems[3], sems[5]
        own, push = first_wave(place, ins, outs, sems)
        push += [cp for _, _, _, cp in forwards(place, ins, outs, sems)]
        for t in range(n):
            landed(outs[t].at[4 * x + 2 * y + (1 - c)], d2d_r.at[4 * t], sems, sib)
            for j, (px, py) in enumerate(chips):
                landed(outs[t].at[4 * px + 2 * py + (1 - c)], d2d_r.at[4 * t + 1 + j], sems, sib)
        if ns:
            for k in range(1, 8):
                px = (1 - x) if (k >> 2) & 1 else x
                py = (1 - y) if (k >> 1) & 1 else y
                pc = (1 - c) if k & 1 else c
                landed(outs[n].at[4 * px + 2 * py + pc], sm_r.at[k - 1], sems, sib)
        for cp in push:
            cp.wait_send()
        for cp in own:
            cp.wait()

    dma = pltpu.SemaphoreType.DMA
    ins = list(parts) + ([small] if ns else [])
    out_shape = [jax.ShapeDtypeStruct((8,) + p.shape[1:], p.dtype) for p in parts]
    if ns:
        out_shape.append(jax.ShapeDtypeStruct((8,) + small.shape, small.dtype))
    return _Comm(ins, out_shape, [dma((3 * n,)), dma((3 * n,)), dma((4 * n,)), dma((4 * n,)), dma((7,)), dma((7,)),
                                  dma((n + 1,))], start, finish, mid)


ADAM_ROWS = 288


def _row_tile(r, cap):
    if r <= cap:
        return r
    return max((t for t in range(8, cap + 1, 8) if r % t == 0), default=r)


def _sum_slots(land):
    g = land[0].astype(F32)
    for i in range(1, 8):
        g = g + land[i].astype(F32)
    return g


def _adamw_update(w, m, v, land):
    g = _sum_slots(land) if len(land.shape) == 3 else land[...]
    m_new = ADAM_B1 * m + (1.0 - ADAM_B1) * g
    v_new = ADAM_B2 * v + (1.0 - ADAM_B2) * (g * g)
    m_hat = m_new / (1.0 - ADAM_B1 ** ADAM_STEP)
    v_hat = v_new / (1.0 - ADAM_B2 ** ADAM_STEP)
    return g, -ADAM_LR * (m_hat / (jnp.sqrt(v_hat) + ADAM_EPS) + ADAM_WD * w), m_new, v_new


def _adamw(tensors, name, comm=None):
    k = len(tensors)
    r, c = tensors[0][0].shape
    t = _row_tile(r, ADAM_ROWS // k)
    tc = c if t < r or r <= ADAM_ROWS else 2 * LANES
    n = (r // t) * (c // tc)
    nci, nco, nsem = (len(comm.ins), len(comm.out_shape), len(comm.sems)) if comm else (0, 0, 0)

    def kern(*refs):
        ins, cins, outs, couts, csems = _split_refs(refs, (4 * k, nci, 4 * k, nco, nsem))
        if comm:
            place = _place()

            @pl.when(pl.program_id(0) == 0)
            def _():
                comm.start(place, cins, couts, csems)

        for i in range(k):
            w_ref, m_ref, v_ref, l_ref = ins[4 * i:4 * i + 4]
            res = _adamw_update(w_ref[...], m_ref[...], v_ref[...], l_ref)
            for ref, val in zip(outs[4 * i:4 * i + 4], res, strict=True):
                ref[...] = val
        if comm:
            @pl.when(pl.program_id(0) == n - 1)
            def _():
                comm.mid(place, cins, couts, csems)
                comm.finish(place, cins, couts, csems)

    where = (lambda i: (i, 0)) if tc == c else (lambda i: (0, i))
    spec = pl.BlockSpec((t, tc), where)
    lspec = pl.BlockSpec((8, t, tc), lambda i: (0,) + where(i)) if tensors[0][3].ndim == 3 else spec
    res = pl.pallas_call(
        kern, grid=(n,), in_specs=[spec, spec, spec, lspec] * k + [ANY] * nci, out_specs=[spec] * (4 * k) + [ANY] * nco,
        out_shape=[jax.ShapeDtypeStruct((r, c), F32)] * (4 * k) + (comm.out_shape if comm else []),
        scratch_shapes=comm.sems if comm else [],
        compiler_params=pltpu.CompilerParams(dimension_semantics=("arbitrary" if comm else "parallel",),
                                             vmem_limit_bytes=VMEM_LIMIT),
        name=name)(*[x for tens in tensors for x in tens], *(comm.ins if comm else []))
    return [res[4 * i:4 * i + 4] for i in range(k)], res[4 * k:]


def _sum_partials(lands, name, comm):
    k = len(lands)
    _, r, c = lands[0].shape
    t = _row_tile(r, ADAM_ROWS)
    n = r // t
    nci, nco = len(comm.ins), len(comm.out_shape)

    def kern(*refs):
        ins, cins, outs, couts, csems = _split_refs(refs, (k, nci, k, nco, len(comm.sems)))
        place = _place()

        @pl.when(pl.program_id(0) == 0)
        def _():
            comm.start(place, cins, couts, csems)

        for l_ref, o_ref in zip(ins, outs, strict=True):
            o_ref[...] = _sum_slots(l_ref)

        @pl.when(pl.program_id(0) == n - 1)
        def _():
            comm.mid(place, cins, couts, csems)
            comm.finish(place, cins, couts, csems)

    res = pl.pallas_call(
        kern, grid=(n,), in_specs=[pl.BlockSpec((8, t, c), lambda i: (0, i, 0))] * k + [ANY] * nci,
        out_specs=[pl.BlockSpec((t, c), lambda i: (i, 0))] * k + [ANY] * nco,
        out_shape=[jax.ShapeDtypeStruct((r, c), F32)] * k + comm.out_shape, scratch_shapes=comm.sems,
        compiler_params=pltpu.CompilerParams(dimension_semantics=("arbitrary",), vmem_limit_bytes=VMEM_LIMIT),
        name=name)(*lands, *comm.ins)
    return res[:k], res[k:]


def _step(a):
    def sq(n):
        v = a[n][0] if a[n].ndim == 3 else a[n]
        return v.T if n.removeprefix("m_").removeprefix("v_") in TRANSPOSED else v

    payload = lambda n: sq(n) if n in EXACT_GATHER else sq(n).astype(BF16)

    loss, dx, g, lands_late = _local_step(sq("x"), sq("mem"), a["positions"][0], sq("loss_target"),
                                          {n: a[n] for n in REPLICATED}, [payload(n) for n in EARLY],
                                          {n: payload(n) for n in LATE})

    sh, rep = _early_grad_shards(g)
    small = _pack_small([rep[n] for n in REPLICATED] + [loss.reshape(1, 1)])
    sums, (*lands_early, land_small) = _sum_partials([lands_late[n] for n in SUMMED_EARLY], "sum_scatter_last",
                                                     _scatter_plan([sh[n] for n in EARLY], small))
    quad = lambda n, land: (sq(n), sq("m_" + n), sq("v_" + n), land)
    lands = dict(zip(EARLY, lands_early, strict=True)) | lands_late | dict(zip(SUMMED_EARLY, sums, strict=True))

    outs = {}
    kinds = ("grad_", "delta_", "new_m_", "new_v_")
    for n, _ in SHARDED:
        res = _adamw([quad(n, lands[n])], "adamw_" + n)[0][0]
        for kind, val in zip(kinds, res, strict=True):
            outs[kind + n] = (val.T if n in TRANSPOSED else val).reshape(a[n].shape)
    zero = jnp.zeros((1, 1), F32)
    packed = [_pack_small([a[p + n] for n in REPLICATED] + [zero]) for p in ("", "m_", "v_")]
    res = _adamw([(*packed, land_small)], "adamw_replicated")[0][0]
    widths = [a[n].shape[1] for n in REPLICATED] + [1]
    for kind, buf in zip(kinds, res, strict=True):
        *vals, total = _unpack_small(buf, widths)
        for n, val in zip(REPLICATED, vals, strict=True):
            outs[kind + n] = val
        if kind == "grad_":
            loss = total[0, 0]

    ordered = [outs[kind + n] for kind in kinds for n in WEIGHTS]
    return (loss, dx[None], *ordered)


def kernel(x, mem, positions, norm_mix, w_in, gla_gate_w2, gla_gate_b, gla_out_norm, mla_q_a_norm, mla_w_uq, mla_kv_a_norm, mla_w_ukv, mla_q_norm, mla_k_norm, w_out, norm_xa, norm_mem, xa_w_q, xa_w_kv, xa_q_norm, xa_k_norm, xa_w_o, norm_ffn, ffn_w_gate, ffn_w_up, ffn_conv_w, ffn_conv_b, ffn_w_down, loss_target, m_norm_mix, m_w_in, m_gla_gate_w2, m_gla_gate_b, m_gla_out_norm, m_mla_q_a_norm, m_mla_w_uq, m_mla_kv_a_norm, m_mla_w_ukv, m_mla_q_norm, m_mla_k_norm, m_w_out, m_norm_xa, m_norm_mem, m_xa_w_q, m_xa_w_kv, m_xa_q_norm, m_xa_k_norm, m_xa_w_o, m_norm_ffn, m_ffn_w_gate, m_ffn_w_up, m_ffn_conv_w, m_ffn_conv_b, m_ffn_w_down, v_norm_mix, v_w_in, v_gla_gate_w2, v_gla_gate_b, v_gla_out_norm, v_mla_q_a_norm, v_mla_w_uq, v_mla_kv_a_norm, v_mla_w_ukv, v_mla_q_norm, v_mla_k_norm, v_w_out, v_norm_xa, v_norm_mem, v_xa_w_q, v_xa_w_kv, v_xa_q_norm, v_xa_k_norm, v_xa_w_o, v_norm_ffn, v_ffn_w_gate, v_ffn_w_up, v_ffn_conv_w, v_ffn_conv_b, v_ffn_w_down):
    return _step(dict(locals()))
```

```python
import functools

import jax
import jax.numpy as jnp
import numpy as np
from jax import lax
from jax.experimental import pallas as pl
from jax.experimental.pallas import tpu as pltpu

F32, BF16 = jnp.float32, jnp.bfloat16
MESH = pl.DeviceIdType.MESH

D_MODEL = 1024
EPS = 1e-6
GLA_HEADS, GLA_DK, GLA_DV, GLA_RANK, GLA_CHUNK = 4, 64, 128, 16, 64
GLA_GATE_NORM = 16.0
MLA_HEADS, MLA_Q_RANK, MLA_KV_RANK, MLA_NOPE, MLA_ROPE, MLA_V = 8, 256, 128, 64, 32, 64
MLA_QK = MLA_NOPE + MLA_ROPE
ROPE_THETA = 10000.0
LOG2E, LN2 = 1.4426950408889634, 0.6931471805599453
XA_HEADS, XA_DIM = 4, 128
D_FF = 2816
ADAM_LR, ADAM_B1, ADAM_B2, ADAM_EPS, ADAM_WD, ADAM_STEP = 0.001, 0.9, 0.999, 1e-08, 0.01, 10

LANES = 128
BF16_ROWS = 16
VMEM_LIMIT = 56 * 1024 * 1024
MATMUL_VMEM = 44 * 1024 * 1024
ROW_TILE = 512

P_GQ, P_GK, P_GV, P_OG, P_CQ, P_CKV, P_KPE, P_ALR, P_WIDTH = 0, 256, 512, 1024, 1536, 1792, 1920, 2048, 2176
N_GQ, N_GK, N_GV, N_ALR, N_OG, N_CQ, N_CKV, N_KPE, N_WIDTH = 0, 256, 512, 1024, 1040, 1552, 1808, 1936, 1968

SHARDED = (("w_in", 1), ("gla_gate_w2", 1), ("mla_w_uq", 1), ("mla_w_ukv", 1), ("w_out", 0), ("xa_w_q", 0),
           ("xa_w_kv", 0), ("xa_w_o", 1), ("ffn_w_gate", 1), ("ffn_w_up", 1), ("ffn_conv_w", 1), ("ffn_w_down", 0))
REPLICATED = ("norm_mix", "gla_gate_b", "gla_out_norm", "mla_q_a_norm", "mla_kv_a_norm", "mla_q_norm", "mla_k_norm",
              "norm_xa", "norm_mem", "xa_q_norm", "xa_k_norm", "norm_ffn", "ffn_conv_b")
EXACT_GATHER = ("gla_gate_w2", "ffn_conv_w")
TRANSPOSED = ("w_in", "ffn_w_gate", "ffn_w_up")
EARLY = ("w_in", "gla_gate_w2", "mla_w_uq", "mla_w_ukv")
LATE = tuple(n for n, _ in SHARDED if n not in EARLY)
LAST = ("ffn_w_down",)
WEIGHTS = ("norm_mix", "w_in", "gla_gate_w2", "gla_gate_b", "gla_out_norm", "mla_q_a_norm", "mla_w_uq",
           "mla_kv_a_norm", "mla_w_ukv", "mla_q_norm", "mla_k_norm", "w_out", "norm_xa", "norm_mem", "xa_w_q",
           "xa_w_kv", "xa_q_norm", "xa_k_norm", "xa_w_o", "norm_ffn", "ffn_w_gate", "ffn_w_up", "ffn_conv_w",
           "ffn_conv_b", "ffn_w_down")


_NN = ((1,), (0,))
_NT = ((1,), (1,))
_TN = ((0,), (0,))


def _dg(a, b, dims):
    return lax.dot_general(a.astype(BF16), b.astype(BF16), (dims, ((), ())), preferred_element_type=F32)


@jax.custom_vjp
def _dot_nn(a, b):
    return _dg(a, b, _NN)


_dot_nn.defvjp(lambda a, b: (_dg(a, b, _NN), (a, b)),
               lambda r, g: (_dg(g, r[1], _NT).astype(r[0].dtype), _dg(r[0], g, _TN).astype(r[1].dtype)))


@jax.custom_vjp
def _dot_nt(a, b):
    return _dg(a, b, _NT)


_dot_nt.defvjp(lambda a, b: (_dg(a, b, _NT), (a, b)),
               lambda r, g: (_dg(g, r[1], _NN).astype(r[0].dtype), _dg(g, r[0], _TN).astype(r[1].dtype)))


@jax.custom_vjp
def _dot_tn(a, b):
    return _dg(a, b, _TN)


_dot_tn.defvjp(lambda a, b: (_dg(a, b, _TN), (a, b)),
               lambda r, g: (_dg(r[1], g, _NT).astype(r[0].dtype), _dg(r[0], g, _NN).astype(r[1].dtype)))


def _rms(x, w, n=None):
    n = x.shape[-1] if n is None else n
    ms = jnp.sum(x * x, axis=-1, keepdims=True) * (1.0 / n)
    return x * lax.rsqrt(ms + EPS) * w


def _silu(x):
    return x * jax.nn.sigmoid(x)


def _log_sigmoid(x):
    return jnp.minimum(x, 0.0) - jnp.log(1.0 + jnp.exp(-jnp.abs(x)))


@jax.custom_vjp
def _cumsum_rows(x):
    n = x.shape[0]
    row = lax.broadcasted_iota(jnp.int32, x.shape, 0)
    k = 1
    while k < n:
        x = x + jnp.where(row >= k, pltpu.roll(x, k, 0), 0.0)
        k *= 2
    return x


def _cumsum_rows_bwd(_, g):
    n = g.shape[0]
    row = lax.broadcasted_iota(jnp.int32, g.shape, 0)
    k = 1
    while k < n:
        g = g + jnp.where(row < n - k, pltpu.roll(g, n - k, 0), 0.0)
        k *= 2
    return (g,)


_cumsum_rows.defvjp(lambda x: (_cumsum_rows(x), None), _cumsum_rows_bwd)


def _lane_mask(lo, hi):
    lane = lax.broadcasted_iota(jnp.int32, (1, LANES), 1)
    return ((lane >= lo) & (lane < hi)).astype(F32)


def _tile(n, t):
    t = min(n, t)
    assert n % t == 0, (n, t)
    return t


class _Epilogue:
    def __init__(self, fn, rows=(), consts=(), outs=(), accs=()):
        self.fn, self.rows, self.consts, self.outs, self.accs = fn, list(rows), list(consts), list(outs), list(accs)


def _matmul(a, b, mode, out_dtype, name, residual=None, a_lead=None, b_lead=None, more=None, epilogue=None):
    (a0, a1), (b0, b1) = a.shape[-2:], b.shape[-2:]
    if mode == "nn":
        m, k, k2, n = a0, a1, b0, b1
    elif mode == "nt":
        m, k, n, k2 = a0, a1, b0, b1
    else:
        k, m, k2, n = a0, a1, b0, b1
    assert k == k2, (a.shape, b.shape, mode)
    npar = 4 if "p" in (a_lead, b_lead) else 1
    nsum = 4 if "k" in (a_lead, b_lead) else 1
    pairs = [(a, b)] + ([more] if more else [])
    a_item, b_item, o_item = a.dtype.itemsize, b.dtype.itemsize, jnp.dtype(out_dtype).itemsize
    ep = epilogue
    row_extra = 4 if residual is not None else 0
    if ep:
        row_extra += (sum(r.dtype.itemsize * wd for r, wd, _ in ep.rows) + sum(jnp.dtype(d).itemsize * wd for wd, d in ep.outs)) / n

    def resident(lead, tiles):
        return lead != "p" and tiles == 1

    def vmem_need(tm, tn, tk):
        a_bufs = 1 if resident(a_lead, (m // tm) * (k // tk)) else 2
        b_bufs = 1 if resident(b_lead, (n // tn) * (k // tk)) else 2
        need = a_bufs * (nsum if a_lead == "k" else 1) * tm * tk * a_item + b_bufs * (nsum if b_lead == "k" else 1) * tk * tn * b_item
        need *= len(pairs)
        need += (0 if ep else 2 * tm * tn * o_item) + tm * tn * 4 * (2 if tk < k else 1)
        need += tm * tk * 2 * (a_item == 4 or mode == "tn") + tk * tn * 2 * (b_item == 4)
        return need + int(2 * tm * tn * row_extra) + (3 * tm * tn * 4 if ep else 0)

    halvings = (4096, 2048, 1024, 512, 256, 128, 64, 32, 16, 8)
    if mode == "tn":
        tm = m if m <= 2304 else m // 2
        tn = n if tm * n <= 1024 * 2304 else n // 2
        tk = next((r for r in halvings if k % r == 0 and vmem_need(tm, tn, r) <= MATMUL_VMEM), k)
    else:
        tn, tk = n, k
        tm = next((r for r in halvings if m % r == 0 and vmem_need(r, tn, tk) <= MATMUL_VMEM), m)
    assert m % tm == 0 and n % tn == 0 and k % tk == 0
    assert ep is None or (tn == n and tk == k and npar == 1)
    nk = k // tk
    dims = {"nn": _NN, "nt": _NT, "tn": _TN}[mode]
    n_in = 2 * len(pairs) + (residual is not None)
    n_ep_in = len(ep.rows) + len(ep.consts) if ep else 0
    n_out = len(ep.outs) + len(ep.accs) if ep else 1

    def body(*refs):
        ab, rs, ep_in, outs, scratch = _split_refs(refs, (2 * len(pairs), n_in - 2 * len(pairs), n_ep_in, n_out, nk > 1))
        prod = None
        for a_ref, b_ref in zip(ab[0::2], ab[1::2]):
            for sh in range(nsum):
                term = _dg(a_ref[sh] if a_lead == "k" else a_ref[...], b_ref[sh] if b_lead == "k" else b_ref[...], dims)
                prod = term if prod is None else prod + term

        def finish(r):
            if rs:
                r = r + rs[0][...]
            if ep is None:
                outs[0][...] = r.astype(outs[0].dtype)
                return
            vals = [x[...] for x in ep_in]
            ro, ao = ep.fn(r, vals[:len(ep.rows)], vals[len(ep.rows):])
            for ref, val in zip(outs[:len(ep.outs)], ro, strict=True):
                ref[...] = val.astype(ref.dtype)
            if ep.accs:
                @pl.when(pl.program_id(0) == 0)
                def _():
                    for ref in outs[len(ep.outs):]:
                        ref[...] = jnp.zeros_like(ref)

                for ref, val in zip(outs[len(ep.outs):], ao, strict=True):
                    ref[...] += val

        if nk == 1:
            finish(prod)
            return
        acc = scratch[0]
        kk = pl.program_id(3)

        @pl.when(kk == 0)
        def _():
            acc[...] = prod

        @pl.when(kk > 0)
        def _():
            acc[...] += prod

        @pl.when(kk == nk - 1)
        def _():
            finish(acc[...])

    def spec(lead, blk, idx, tiles=0):
        mode = {"pipeline_mode": pl.Buffered(1)} if resident(lead, tiles) else {}
        if lead is None:
            return pl.BlockSpec(blk, lambda i, j, p, kk: idx(i, j, kk), **mode)
        if lead == "p":
            return pl.BlockSpec((None,) + blk, lambda i, j, p, kk: (p,) + idx(i, j, kk))
        return pl.BlockSpec((nsum,) + blk, lambda i, j, p, kk: (0,) + idx(i, j, kk), **mode)

    a_tiles, b_tiles = (m // tm) * nk, (n // tn) * nk
    if mode == "nn":
        pair_specs = [spec(a_lead, (tm, tk), lambda i, j, kk: (i, kk), a_tiles),
                      spec(b_lead, (tk, tn), lambda i, j, kk: (kk, j), b_tiles)]
    elif mode == "nt":
        pair_specs = [spec(a_lead, (tm, tk), lambda i, j, kk: (i, kk), a_tiles),
                      spec(b_lead, (tn, tk), lambda i, j, kk: (j, kk), b_tiles)]
    else:
        pair_specs = [spec(a_lead, (tk, tm), lambda i, j, kk: (kk, i), a_tiles),
                      spec(b_lead, (tk, tn), lambda i, j, kk: (kk, j), b_tiles)]
    tile = spec(None, (tm, tn), lambda i, j, kk: (i, j))
    in_specs = pair_specs * len(pairs)
    args = [x for pair in pairs for x in pair]
    if residual is not None:
        assert npar == 1
        in_specs.append(tile)
        args.append(residual)
    if ep:
        in_specs += [pl.BlockSpec((tm, wd), functools.partial(lambda cb, i, j, p, kk: (i, cb), cb)) for _, wd, cb in ep.rows]
        in_specs += [pl.BlockSpec(c.shape, lambda i, j, p, kk: (0, 0)) for c in ep.consts]
        args += [r for r, _, _ in ep.rows] + ep.consts
        out_specs = [pl.BlockSpec((tm, wd), lambda i, j, p, kk: (i, 0)) for wd, _ in ep.outs]
        out_specs += [pl.BlockSpec(shape, lambda i, j, p, kk: (0, 0)) for shape in ep.accs]
        out_shape = [jax.ShapeDtypeStruct((m, wd), d) for wd, d in ep.outs] + [jax.ShapeDtypeStruct(sh, F32) for sh in ep.accs]
    else:
        out_specs = spec("p" if npar > 1 else None, (tm, tn), lambda i, j, kk: (i, j))
        out_shape = jax.ShapeDtypeStruct(((4,) if npar > 1 else ()) + (m, n), out_dtype)
    outer = "arbitrary" if ep and ep.accs else "parallel"
    return pl.pallas_call(
        body, grid=(m // tm, n // tn, npar, nk), in_specs=in_specs, out_specs=out_specs, out_shape=out_shape,
        scratch_shapes=[pltpu.VMEM((tm, tn), F32)] if nk > 1 else [],
        compiler_params=pltpu.CompilerParams(dimension_semantics=(outer, outer, outer, "arbitrary"),
                                             vmem_limit_bytes=VMEM_LIMIT),
        name=name)(*args)


def _row(a, width=None, col_block=0):
    return (a, a.shape[1] if width is None else width, col_block)


def _rows_call(body, rows, consts, outs, accs=(), *, name, tile=ROW_TILE):
    s = rows[0][0].shape[0]
    t = _tile(s, tile)
    nr, nc, no = len(rows), len(consts), len(outs)

    def kern(*refs):
        r = [x[...] for x in refs[:nr]]
        c = [x[...] for x in refs[nr:nr + nc]]
        o_refs = refs[nr + nc:nr + nc + no]
        a_refs = refs[nr + nc + no:]
        ro, ao = body(r, c)
        for ref, val in zip(o_refs, ro, strict=True):
            ref[...] = val.astype(ref.dtype)
        if a_refs:
            @pl.when(pl.program_id(0) == 0)
            def _():
                for ref in a_refs:
                    ref[...] = jnp.zeros_like(ref)

            for ref, val in zip(a_refs, ao, strict=True):
                ref[...] += val

    in_specs = [pl.BlockSpec((t, w), functools.partial(lambda cb, i: (i, cb), cb)) for (_, w, cb) in rows]
    in_specs += [pl.BlockSpec(c.shape, lambda i: (0, 0)) for c in consts]
    out_specs = [pl.BlockSpec((t, w), lambda i: (i, 0)) for (w, _) in outs]
    out_specs += [pl.BlockSpec(shape, lambda i: (0, 0)) for shape in accs]
    out_shape = [jax.ShapeDtypeStruct((s, w), dt) for (w, dt) in outs]
    out_shape += [jax.ShapeDtypeStruct(shape, F32) for shape in accs]
    return pl.pallas_call(
        kern, grid=(s // t,), in_specs=in_specs, out_specs=out_specs, out_shape=out_shape,
        compiler_params=pltpu.CompilerParams(dimension_semantics=("arbitrary" if accs else "parallel",),
                                             vmem_limit_bytes=VMEM_LIMIT),
        name=name)(*[r[0] for r in rows], *consts)


def _gla_chunk(q, k, la, v0, v1, s0, s1):
    c = q.shape[0]
    r = lax.broadcasted_iota(jnp.int32, (c, c), 0)
    cc = lax.broadcasted_iota(jnp.int32, (c, c), 1)
    tril = cc <= r
    cum = _cumsum_rows(la)
    cl = jnp.sum(la, axis=0, keepdims=True)
    qd = q * (GLA_DK ** -0.5) * jnp.exp(cum)
    ki = k * jnp.exp(-cum)
    ke = k * jnp.exp(cl - cum)
    dec = jnp.exp(cl)
    outs, news = [], []
    for h, (v, s) in enumerate(((v0, s0), (v1, s1))):
        mk = _lane_mask(GLA_DK * h, GLA_DK * (h + 1))
        qh = qd * mk
        att = jnp.where(tril, _dot_nt(qh, ki), 0.0)
        outs.append(_dot_nn(att, v) + _dot_nt(qh, s))
        news.append(s * dec + _dot_tn(v, ke * mk))
    return outs[0], outs[1], news[0], news[1]


def _gla_specs(tb, rev_nb=None):
    blk = (lambda b: b) if rev_nb is None else (lambda b: rev_nb - 1 - b)
    q = pl.BlockSpec((tb, 128), lambda p, b: (blk(b), P_GQ // 128 + p))
    k = pl.BlockSpec((tb, 128), lambda p, b: (blk(b), P_GK // 128 + p))
    la = pl.BlockSpec((tb, 128), lambda p, b: (blk(b), p))
    v = pl.BlockSpec((tb, 256), lambda p, b: (blk(b), P_GV // 256 + p))
    o = pl.BlockSpec((tb, 256), lambda p, b: (blk(b), p))
    st = pl.BlockSpec((tb // GLA_CHUNK, 2, 128, 128), lambda p, b: (blk(b), p, 0, 0))
    return q, k, la, v, o, st


def _gla_fwd(proj, la):
    s = proj.shape[0]
    tb = _tile(s, ROW_TILE)
    nb, nch = s // tb, tb // GLA_CHUNK

    def kern(q_ref, k_ref, la_ref, v_ref, o_ref, st_ref, s_sc):
        @pl.when(pl.program_id(1) == 0)
        def _():
            s_sc[...] = jnp.zeros_like(s_sc)

        s0, s1 = s_sc[0], s_sc[1]
        for ci in range(nch):
            sl = slice(ci * GLA_CHUNK, (ci + 1) * GLA_CHUNK)
            st_ref[ci, 0] = s0
            st_ref[ci, 1] = s1
            o0, o1, s0, s1 = _gla_chunk(q_ref[sl, :], k_ref[sl, :], la_ref[sl, :], v_ref[sl, 0:128],
                                        v_ref[sl, 128:256], s0, s1)
            o_ref[sl, 0:128] = o0
            o_ref[sl, 128:256] = o1
        s_sc[0] = s0
        s_sc[1] = s1

    q, k, lasp, v, o, st = _gla_specs(tb)
    return pl.pallas_call(
        kern, grid=(2, nb), in_specs=[q, k, lasp, v], out_specs=[o, st],
        out_shape=[jax.ShapeDtypeStruct((s, 512), F32),
                   jax.ShapeDtypeStruct((s // GLA_CHUNK, GLA_HEADS, 128, 128), F32)],
        scratch_shapes=[pltpu.VMEM((2, 128, 128), F32)],
        compiler_params=pltpu.CompilerParams(dimension_semantics=("parallel", "arbitrary"),
                                             vmem_limit_bytes=VMEM_LIMIT),
        name="gla_fwd")(proj, proj, la, proj)


def _gla_bwd(proj, la, states, d_o):
    s = proj.shape[0]
    tb = _tile(s, ROW_TILE)
    nb, nch = s // tb, tb // GLA_CHUNK

    def kern(q_ref, k_ref, la_ref, v_ref, do_ref, st_ref, dq_ref, dk_ref, dla_ref, dv_ref, ds_sc):
        @pl.when(pl.program_id(1) == 0)
        def _():
            ds_sc[...] = jnp.zeros_like(ds_sc)

        d0, d1 = ds_sc[0], ds_sc[1]
        for ci in reversed(range(nch)):
            sl = slice(ci * GLA_CHUNK, (ci + 1) * GLA_CHUNK)
            _, vjp = jax.vjp(_gla_chunk, q_ref[sl, :], k_ref[sl, :], la_ref[sl, :], v_ref[sl, 0:128],
                             v_ref[sl, 128:256], st_ref[ci, 0], st_ref[ci, 1])
            gq, gk, gla, gv0, gv1, d0, d1 = vjp((do_ref[sl, 0:128], do_ref[sl, 128:256], d0, d1))
            dq_ref[sl, :] = gq
            dk_ref[sl, :] = gk
            dla_ref[sl, :] = gla
            dv_ref[sl, 0:128] = gv0
            dv_ref[sl, 128:256] = gv1
        ds_sc[0] = d0
        ds_sc[1] = d1

    q, k, lasp, v, o, st = _gla_specs(tb, rev_nb=nb)
    return pl.pallas_call(
        kern, grid=(2, nb), in_specs=[q, k, lasp, v, o, st], out_specs=[lasp, lasp, lasp, o],
        out_shape=[jax.ShapeDtypeStruct((s, 256), F32), jax.ShapeDtypeStruct((s, 256), F32),
                   jax.ShapeDtypeStruct((s, 256), F32), jax.ShapeDtypeStruct((s, 512), F32)],
        scratch_shapes=[pltpu.VMEM((2, 128, 128), F32)],
        compiler_params=pltpu.CompilerParams(dimension_semantics=("parallel", "arbitrary"),
                                             vmem_limit_bytes=VMEM_LIMIT),
        name="gla_bwd")(proj, proj, la, proj, d_o, states)


def _causal_keep(t):
    return lax.broadcasted_iota(jnp.int32, (t, t), 1) <= lax.broadcasted_iota(jnp.int32, (t, t), 0)


def _split_refs(refs, counts):
    out, off = [], 0
    for cnt in counts:
        out.append(refs[off:off + cnt])
        off += cnt
    return out


def _causal_blocks(n, key_major):
    pairs = ([(ki, qi) for ki in range(n) for qi in range(ki, n)] if key_major else
             [(ki, qi) for qi in range(n) for ki in range(qi + 1)])
    return np.array([ki for ki, _ in pairs], np.int32), np.array([qi for _, qi in pairs], np.int32)


def _attn_fwd(q, k, v, comm, tile=2048):
    s = q.shape[0]
    t = _tile(s, tile)
    n = s // t
    nci, nco = len(comm.ins), len(comm.out_shape)

    ki_tab, qi_tab = _causal_blocks(n, key_major=False)
    steps = len(ki_tab)

    def kern(ki_ref, qi_ref, *refs):
        (q_ref, k_ref, v_ref), cins, (o_ref, lse_ref), couts, (m_sc, l_sc, acc_sc), csems = _split_refs(
            refs, (3, nci, 2, nco, 3, len(comm.sems)))
        pair, step = pl.program_id(0), pl.program_id(1)
        qi, ki = qi_ref[step], ki_ref[step]
        place = _place()

        @pl.when((pair == 0) & (step == 0))
        def _():
            comm.start(place, cins, couts, csems)

        @pl.when((pair == MLA_HEADS // 2 - 1) & (step == 0))
        def _():
            comm.mid(place, cins, couts, csems)

        first = lax.broadcasted_iota(jnp.int32, (t, LANES), 1) < MLA_V

        @pl.when(ki == 0)
        def _():
            m_sc[...] = jnp.full_like(m_sc, -jnp.inf)
            l_sc[...] = jnp.zeros_like(l_sc)
            acc_sc[...] = jnp.zeros_like(acc_sc)

        def update(rows, cols, masked):
            nr = rows.stop - rows.start
            sel = first[:nr]
            alphas, pvs = [], []
            for h in range(2):
                sc = _dg(q_ref[rows, 128 * h:128 * (h + 1)], k_ref[cols, 128 * h:128 * (h + 1)], _NT)
                if masked:
                    sc = jnp.where(_causal_keep(nr), sc, -jnp.inf)
                m_prev = m_sc[h, rows]
                m_new = jnp.maximum(m_prev, jnp.max(sc, axis=1, keepdims=True))
                alpha = jnp.exp2(m_prev - m_new)
                p = jnp.exp2(sc - m_new[:, 0:1])
                l_sc[h, rows] = alpha * l_sc[h, rows] + jnp.sum(p, axis=1, keepdims=True)
                m_sc[h, rows] = m_new
                alphas.append(alpha)
                pvs.append(_dg(p, v_ref[cols, :], _NN))
            acc_sc[rows] = acc_sc[rows] * jnp.where(sel, alphas[0], alphas[1]) + jnp.where(sel, pvs[0], pvs[1])

        halves = [slice(0, t)] if t % 256 else [slice(0, t // 2), slice(t // 2, t)]

        @pl.when(ki < qi)
        def _():
            for rows in halves:
                for cols in halves:
                    update(rows, cols, False)

        @pl.when(ki == qi)
        def _():
            for i, rows in enumerate(halves):
                for j, cols in enumerate(halves[:i + 1]):
                    update(rows, cols, i == j)

        @pl.when(ki == qi)
        def _():
            l = jnp.where(first, l_sc[0], l_sc[1])
            m = jnp.where(first, m_sc[0], m_sc[1])
            o_ref[...] = acc_sc[...] / l
            lse_ref[...] = m + jnp.log2(l)

        @pl.when((pair == MLA_HEADS // 2 - 1) & (step == steps - 1))
        def _():
            comm.finish(place, cins, couts, csems)

    q_idx = lambda p, st, ki_r, qi_r: (qi_r[st], p)
    k_idx = lambda p, st, ki_r, qi_r: (ki_r[st], p)
    res = pl.pallas_call(
        kern, grid_spec=pltpu.PrefetchScalarGridSpec(
            num_scalar_prefetch=2, grid=(MLA_HEADS // 2, steps),
            in_specs=[pl.BlockSpec((t, 256), q_idx), pl.BlockSpec((t, 256), k_idx), pl.BlockSpec((t, 128), k_idx)]
            + [ANY] * nci,
            out_specs=[pl.BlockSpec((t, 128), q_idx), pl.BlockSpec((t, 128), q_idx)] + [ANY] * nco,
            scratch_shapes=[pltpu.VMEM((2, t, LANES), F32), pltpu.VMEM((2, t, LANES), F32),
                            pltpu.VMEM((t, LANES), F32)] + comm.sems),
        out_shape=[jax.ShapeDtypeStruct((s, 512), F32), jax.ShapeDtypeStruct((s, 512), F32)] + comm.out_shape,
        compiler_params=pltpu.CompilerParams(dimension_semantics=("arbitrary", "arbitrary"),
                                             vmem_limit_bytes=VMEM_LIMIT),
        name="mla_attn_fwd")(ki_tab, qi_tab, q, k, v, *comm.ins)
    return res[0], res[1], res[2:]


def _attn_bwd(q, k, v, o, lse, d_o, comm, tile=1024):
    s = q.shape[0]
    t = _tile(s, tile)
    n = s // t
    nci, nco = len(comm.ins), len(comm.out_shape)

    ki_tab, qi_tab = _causal_blocks(n, key_major=True)
    steps = len(ki_tab)

    def kern(ki_ref, qi_ref, *refs):
        (q_ref, k_ref, v_ref, o_ref, lse_ref, do_ref), cins, (dq_ref, dk_ref, dv_ref), couts, (dk_sc, dv_sc), csems = \
            _split_refs(refs, (6, nci, 3, nco, 2, len(comm.sems)))
        pair, step = pl.program_id(0), pl.program_id(1)
        ki, qi = ki_ref[step], qi_ref[step]
        place = _place()

        @pl.when((pair == 0) & (step == 0))
        def _():
            comm.start(place, cins, couts, csems)

        @pl.when((pair == MLA_HEADS // 2 - 1) & (step == 0))
        def _():
            comm.mid(place, cins, couts, csems)

        @pl.when((ki == 0) & (qi == 0))
        def _():
            dq_ref[...] = jnp.zeros_like(dq_ref)

        @pl.when(qi == ki)
        def _():
            dk_sc[...] = jnp.zeros_like(dk_sc)
            dv_sc[...] = jnp.zeros_like(dv_sc)

        def update(rows, cols, masked):
            nr = rows.stop - rows.start
            d_o = do_ref[rows, :]
            prod = d_o * o_ref[rows, :]
            dq_rows = pl.ds(pl.multiple_of(qi * t + rows.start, nr), nr)
            for h in range(2):
                hs = slice(128 * h, 128 * (h + 1))
                mk = _lane_mask(MLA_V * h, MLA_V * (h + 1))
                qh, kh = q_ref[rows, hs], k_ref[cols, hs]
                sc = _dg(qh, kh, _NT)
                if masked:
                    sc = jnp.where(_causal_keep(nr), sc, -jnp.inf)
                p = jnp.exp2(sc - lse_ref[rows, MLA_V * h:MLA_V * h + 1])
                doh = d_o * mk
                dp = _dg(doh * LN2, v_ref[cols, :], _NT)
                delta = jnp.sum(prod * mk, axis=1, keepdims=True) * LN2
                ds = p * (dp - delta)
                dv_sc[cols, :] += _dg(p, doh, _TN)
                dk_sc[cols, hs] += _dg(ds, qh, _TN)
                dq_ref[dq_rows, hs] += _dg(ds, kh, _NN)

        halves = [slice(0, t)] if t % 256 else [slice(0, t // 2), slice(t // 2, t)]

        @pl.when(qi > ki)
        def _():
            for rows in halves:
                for cols in halves:
                    update(rows, cols, False)

        @pl.when(qi == ki)
        def _():
            for i, rows in enumerate(halves):
                for j, cols in enumerate(halves[:i + 1]):
                    update(rows, cols, i == j)

        @pl.when(qi == n - 1)
        def _():
            dk_ref[...] = dk_sc[...]
            dv_ref[...] = dv_sc[...].astype(dv_ref.dtype)

        @pl.when((pair == MLA_HEADS // 2 - 1) & (step == steps - 1))
        def _():
            comm.finish(place, cins, couts, csems)

    q_idx = lambda p, st, ki_r, qi_r: (qi_r[st], p)
    k_idx = lambda p, st, ki_r, qi_r: (ki_r[st], p)
    res = pl.pallas_call(
        kern, grid_spec=pltpu.PrefetchScalarGridSpec(
            num_scalar_prefetch=2, grid=(MLA_HEADS // 2, steps),
            in_specs=[pl.BlockSpec((t, 256), q_idx), pl.BlockSpec((t, 256), k_idx), pl.BlockSpec((t, 128), k_idx),
                      pl.BlockSpec((t, 128), q_idx), pl.BlockSpec((t, 128), q_idx), pl.BlockSpec((t, 128), q_idx)]
            + [ANY] * nci,
            out_specs=[pl.BlockSpec((s, 256), lambda p, st, ki_r, qi_r: (0, p)), pl.BlockSpec((t, 256), k_idx),
                       pl.BlockSpec((t, 128), k_idx)] + [ANY] * nco,
            scratch_shapes=[pltpu.VMEM((t, 256), F32), pltpu.VMEM((t, 128), F32)] + comm.sems),
        out_shape=[jax.ShapeDtypeStruct((s, 1024), F32), jax.ShapeDtypeStruct((s, 1024), F32),
                   jax.ShapeDtypeStruct((s, 512), BF16)] + comm.out_shape,
        compiler_params=pltpu.CompilerParams(dimension_semantics=("arbitrary", "arbitrary"),
                                             vmem_limit_bytes=VMEM_LIMIT),
        name="mla_attn_bwd")(ki_tab, qi_tab, q, k, v, o, lse, d_o, *comm.ins)
    return res[0], res[1], res[2], res[3:]


def _gate_fn(alr, w2, b):
    return _log_sigmoid(_dot_nn(alr, w2) + b) * (1.0 / GLA_GATE_NORM)


def _make_norm_rope(scale):
    def forward(x, w, c, sa, sb):
        r = lax.rsqrt(jnp.sum(x * x, axis=-1, keepdims=True) * (1.0 / MLA_QK) + EPS)
        y = x * r * w
        out = y * c + pltpu.roll(y, LANES - 16, 1) * sa + pltpu.roll(y, 16, 1) * sb
        return (out if scale == 1.0 else out * scale), r

    @jax.custom_vjp
    def norm_rope(x, w, c, sa, sb):
        return forward(x, w, c, sa, sb)[0]

    def fwd(x, w, c, sa, sb):
        out, r = forward(x, w, c, sa, sb)
        return out, (x, w, c, sa, sb, r)

    def bwd(res, g):
        x, w, c, sa, sb, r = res
        if scale != 1.0:
            g = g * scale
        gy = g * c + pltpu.roll(g * sa, 16, 1) + pltpu.roll(g * sb, LANES - 16, 1)
        xr = x * r
        t = gy * w
        m = jnp.sum(t * xr, axis=-1, keepdims=True) * (1.0 / MLA_QK)
        return r * (t - xr * m), jnp.sum(gy * xr, axis=0, keepdims=True), jnp.zeros_like(c), jnp.zeros_like(sa), jnp.zeros_like(sb)

    norm_rope.defvjp(fwd, bwd)
    return norm_rope


_q_norm_rope = _make_norm_rope(MLA_QK ** -0.5 * LOG2E)
_k_norm_rope = _make_norm_rope(1.0)


def _qk_head(qh, kh, kpe, c, sa, sb, qn, kn):
    kfull = kh + kpe * _lane_mask(MLA_NOPE, MLA_QK)
    return _q_norm_rope(qh, qn, c, sa, sb), _k_norm_rope(kfull, kn, c, sa, sb)


def _mix_head(o, og, gn):
    return _rms(o, gn) * _silu(og)


def _xa_head(xq, xk, xv, qn, kn):
    sc = _dot_nt(_rms(xq, qn), _rms(xk, kn)) * (XA_DIM ** -0.5)
    e = jnp.exp(sc - lax.stop_gradient(jnp.max(sc, axis=1, keepdims=True)))
    p = e / jnp.sum(e, axis=1, keepdims=True)
    return _dot_nn(p, xv)


def _heads(x, n):
    return [x[:, 128 * h:128 * (h + 1)] for h in range(n)]


def _cat(xs):
    return jnp.concatenate(xs, axis=1)


def _norm_fwd(x, w, name):
    return _rows_call(lambda r, c: ([_rms(r[0], c[0])], []), [_row(x)], [w], [(x.shape[1], BF16)], name=name)[0]


def _norm_fwd_epilogue(w):
    return _Epilogue(lambda h, rows, consts: ([h, _rms(h, consts[0])], []), [], [w], [(D_MODEL, F32), (D_MODEL, BF16)], [])


def _norm_bwd_epilogue(x, w, add):
    def fn(d_out, rows, consts):
        _, vjp = jax.vjp(_rms, rows[0], consts[0])
        dx, dw = vjp(d_out)
        return [dx + rows[1]], [dw]

    return _Epilogue(fn, [_row(x), _row(add)], [w], [(D_MODEL, F32)], [w.shape])


def _norm_fwd_comm(x, w, comm, name):
    s, d = x.shape
    t = _tile(s, ROW_TILE)
    n = s // t
    nci, nco = len(comm.ins), len(comm.out_shape)

    def kern(*refs):
        (x_ref, w_ref), cins, (o_ref,), couts, csems = _split_refs(refs, (2, nci, 1, nco, len(comm.sems)))
        place = _place()

        @pl.when(pl.program_id(0) == 0)
        def _():
            comm.start(place, cins, couts, csems)

        o_ref[...] = _rms(x_ref[...], w_ref[...]).astype(o_ref.dtype)

        @pl.when(pl.program_id(0) == n - 1)
        def _():
            comm.mid(place, cins, couts, csems)
            comm.finish(place, cins, couts, csems)

    tile = pl.BlockSpec((t, d), lambda i: (i, 0))
    res = pl.pallas_call(
        kern, grid=(n,), in_specs=[tile, pl.BlockSpec(w.shape, lambda i: (0, 0))] + [ANY] * nci,
        out_specs=[tile] + [ANY] * nco, out_shape=[jax.ShapeDtypeStruct((s, d), BF16)] + comm.out_shape,
        scratch_shapes=comm.sems,
        compiler_params=pltpu.CompilerParams(dimension_semantics=("arbitrary",), vmem_limit_bytes=VMEM_LIMIT),
        name=name)(x, w, *comm.ins)
    return res[0], res[1:]


def _norm_bwd(x, w, d_out, add, name):
    def body(r, c):
        _, vjp = jax.vjp(_rms, r[0], c[0])
        dx, dw = vjp(r[1])
        return [dx + r[2]], [dw]

    return _rows_call(body, [_row(x), _row(d_out), _row(add)], [w], [(x.shape[1], F32)], [w.shape], name=name)


CONV_HALO = BF16_ROWS


def _conv_specs(s, f, t):
    n8 = t // CONV_HALO
    cur = pl.BlockSpec((None, t, f), lambda j, i: (j, i, 0))
    prev = pl.BlockSpec((None, CONV_HALO, f), lambda j, i: (j, jnp.maximum(i * n8 - 1, 0), 0))
    nxt = pl.BlockSpec((None, CONV_HALO, f), lambda j, i: (j, jnp.minimum((i + 1) * n8, s // CONV_HALO - 1), 0))
    cw = pl.BlockSpec((None, 3, f), lambda j, i: (j, 0, 0))
    cb = pl.BlockSpec((None, 1, f), lambda j, i: (j, 0, 0))
    return cur, prev, nxt, cw, cb


def _conv_sigmoid(x):
    return 0.5 + 0.5 * jnp.tanh(0.5 * x)


def _conv_taps(g, prev, first):
    ext = jnp.concatenate([jnp.where(first, 0.0, prev.astype(F32)), g], axis=0)
    return pltpu.roll(ext, 1, 0)[CONV_HALO:], pltpu.roll(ext, 2, 0)[CONV_HALO:]


def _conv_fwd(gg, uu, cw, cb, comm):
    _, s, f = gg.shape
    t = _tile(s, ROW_TILE)
    nt = s // t
    nci, nco = len(comm.ins), len(comm.out_shape)

    def kern(*refs):
        (g_ref, gp_ref, u_ref, cw_ref, cb_ref), cins, (o_ref,), couts, csems = _split_refs(
            refs, (5, nci, 1, nco, len(comm.sems)))
        shard, i = pl.program_id(0), pl.program_id(1)
        place = _place()

        @pl.when((shard == 0) & (i == 0))
        def _():
            comm.start(place, cins, couts, csems)

        @pl.when((shard == 3) & (i == 0))
        def _():
            comm.mid(place, cins, couts, csems)

        g = g_ref[...].astype(F32)
        g1, g2 = _conv_taps(g, gp_ref[...], i == 0)
        w = cw_ref[...]
        gc = cb_ref[...] + w[0:1] * g2 + w[1:2] * g1 + w[2:3] * g
        o_ref[...] = (gc * _conv_sigmoid(gc) * u_ref[...].astype(F32)).astype(o_ref.dtype)

        @pl.when((shard == 3) & (i == nt - 1))
        def _():
            comm.finish(place, cins, couts, csems)

    cur, prev, _, cws, cbs = _conv_specs(s, f, t)
    res = pl.pallas_call(
        kern, grid=(4, nt), in_specs=[cur, prev, cur, cws, cbs] + [ANY] * nci, out_specs=[cur] + [ANY] * nco,
        out_shape=[jax.ShapeDtypeStruct(gg.shape, BF16)] + comm.out_shape, scratch_shapes=comm.sems,
        compiler_params=pltpu.CompilerParams(dimension_semantics=("arbitrary", "arbitrary"), vmem_limit_bytes=VMEM_LIMIT),
        name="ffn_conv_fwd")(gg, gg, uu, cw, cb, *comm.ins)
    return res[0], res[1:]


def _conv_bwd(gg, uu, dact, cw, cb):
    _, s, f = gg.shape
    t = _tile(s, ROW_TILE)
    nt = s // t

    def kern(g_ref, gp_ref, gn_ref, u_ref, un_ref, da_ref, dan_ref, cw_ref, cb_ref, du_ref, dg_ref, dcw_ref, dcb_ref):
        i = pl.program_id(1)
        cat = lambda a_ref, b_ref: jnp.concatenate([a_ref[...].astype(F32), b_ref[...].astype(F32)], axis=0)
        g, u, da = cat(g_ref, gn_ref), cat(u_ref, un_ref), cat(da_ref, dan_ref)
        g1, g2 = _conv_taps(g, gp_ref[...], i == 0)
        w = cw_ref[...]
        gc = cb_ref[...] + w[0:1] * g2 + w[1:2] * g1 + w[2:3] * g
        sg = _conv_sigmoid(gc)
        du_ref[...] = (da[:t] * (gc[:t] * sg[:t])).astype(du_ref.dtype)
        row = lax.broadcasted_iota(jnp.int32, (t + CONV_HALO, 1), 0)
        dgc = jnp.where((row < t) | (i < nt - 1), da * u * (sg * (1.0 + gc * (1.0 - sg))), 0.0)
        up1 = pltpu.roll(dgc, t + CONV_HALO - 1, 0)[:t]
        up2 = pltpu.roll(dgc, t + CONV_HALO - 2, 0)[:t]
        dgc = dgc[:t]
        dg_ref[...] = (w[2:3] * dgc + w[1:2] * up1 + w[0:1] * up2).astype(dg_ref.dtype)

        @pl.when(i == 0)
        def _():
            dcw_ref[...] = jnp.zeros_like(dcw_ref)
            dcb_ref[...] = jnp.zeros_like(dcb_ref)

        ones = jnp.ones((8, t), BF16)
        col_sum = lambda a: _dg(ones, a, _NN)[0:1]
        dcw_ref[0:1, :] += col_sum(dgc * g2[:t])
        dcw_ref[1:2, :] += col_sum(dgc * g1[:t])
        dcw_ref[2:3, :] += col_sum(dgc * g[:t])
        dcb_ref[...] += col_sum(dgc)

    cur, prev, nxt, cws, cbs = _conv_specs(s, f, t)
    return pl.pallas_call(
        kern, grid=(4, nt), in_specs=[cur, prev, nxt, cur, nxt, cur, nxt, cws, cbs], out_specs=[cur, cur, cws, cbs],
        out_shape=[jax.ShapeDtypeStruct(gg.shape, BF16), jax.ShapeDtypeStruct(gg.shape, BF16),
                   jax.ShapeDtypeStruct(cw.shape, F32), jax.ShapeDtypeStruct(cb.shape, F32)],
        compiler_params=pltpu.CompilerParams(dimension_semantics=("parallel", "arbitrary"), vmem_limit_bytes=VMEM_LIMIT),
        name="ffn_conv_bwd")(gg, gg, gg, uu, uu, dact, dact, cw, cb)


def _rope_tables(pos):
    half = MLA_ROPE // 2
    lane = jnp.arange(LANES)
    rotary = (lane >= MLA_NOPE) & (lane < MLA_QK)
    inv = jnp.where(rotary, ROPE_THETA ** (-((lane - MLA_NOPE) % half).astype(F32) / half), 0.0)
    ang = pos.astype(F32)[:, None] * inv
    cos, sin = jnp.cos(ang), jnp.sin(ang)
    first = rotary & (lane < MLA_NOPE + half)
    return cos, jnp.where(first, -sin, 0.0), jnp.where(rotary & ~first, sin, 0.0)


def _local_step(x, mem, pos, target, rep, early_shards, late_shards):
    g = {}
    c, sa, sb = _rope_tables(pos)

    xn, gathered = _norm_fwd_comm(x, rep["norm_mix"], _gather_plan(early_shards), "norm_mix_fwd_gather")
    w = _early_layout(dict(zip(EARLY, gathered, strict=True)), rep)

    def proj_fn(r, rows, k):
        la_ = _gate_fn(r[:, P_ALR:P_ALR + 128], k[0], k[1])
        return [r, la_, _rms(r[:, P_CQ:P_CQ + MLA_Q_RANK], k[2]), _rms(r[:, P_CKV:P_CKV + MLA_KV_RANK], k[3])], []

    proj, la, q_lat, kv_lat = _matmul(
        xn, w["in"], "nt", F32, "proj_fwd", epilogue=_Epilogue(
            proj_fn, [], [w["w2"], w["gate_b"], w["q_a_norm"], w["kv_a_norm"]],
            [(P_WIDTH, F32), (256, F32), (MLA_Q_RANK, BF16), (MLA_KV_RANK, BF16)], []))
    alr = _row(proj, 128, P_ALR // 128)
    kpe = _row(proj, 128, P_KPE // 128)
    og = _row(proj, 512, P_OG // 512)
    cq = _row(proj, 256, P_CQ // 256)
    ckv = _row(proj, 128, P_CKV // 128)

    o_gla, states = _gla_fwd(proj, la)

    def qk_body(r, k):
        q_up, k_up = _dg(r[0], k[0], _NN), _dg(r[1], k[1], _NN)
        qs, ks = [], []
        for qh, kh in zip(_heads(q_up, MLA_HEADS), _heads(k_up, MLA_HEADS)):
            a, b = _qk_head(qh, kh, r[2], r[3], r[4], r[5], k[3], k[4])
            qs.append(a)
            ks.append(b)
        return [_cat(qs), _cat(ks), _dg(r[1], k[2], _NN)], []

    tabs = [_row(c), _row(sa), _row(sb)]
    qk_consts = [w["uq"], w["k"], w["v"], w["q_norm"], w["k_norm"]]
    q_r, k_r, v_mla = _rows_call(qk_body, [_row(q_lat), _row(kv_lat), kpe] + tabs, qk_consts,
                                 [(1024, BF16), (1024, BF16), (512, BF16)], name="mla_qk_fwd")
    with_attn = [n for n in LATE if n not in LAST]
    o_mla, lse, gathered = _attn_fwd(q_r, k_r, v_mla, _gather_plan([late_shards[n] for n in with_attn]))
    w.update(_late_layout(dict(zip(with_attn, gathered, strict=True))))

    def mix_body(r, k):
        ys = [_mix_head(o, g_, k[0]) for o, g_ in zip(_heads(r[0], GLA_HEADS), _heads(r[1], GLA_HEADS))]
        return [_cat(ys + [r[2]])], []

    cat = _rows_call(mix_body, [_row(o_gla), og, _row(o_mla)], [w["gla_out_norm"]], [(1024, BF16)],
                     name="mix_fwd")[0]
    h1, hn = _matmul(cat, w["out"], "nn", F32, "out_fwd_norm", residual=x, epilogue=_norm_fwd_epilogue(w["norm_xa"]))
    mn = _norm_fwd(mem, w["norm_mem"], "norm_mem_fwd")
    xkv = _matmul(mn, w["xkv"], "nn", F32, "xa_kv_fwd")

    def xa_fn(r, rows, k):
        ks, vs = _heads(k[0], 2 * XA_HEADS)[:XA_HEADS], _heads(k[0], 2 * XA_HEADS)[XA_HEADS:]
        return [r, _cat([_xa_head(a, b, v_, k[1], k[2]) for a, b, v_ in zip(_heads(r, XA_HEADS), ks, vs)])], []

    xq, xo = _matmul(hn, w["xq"], "nn", F32, "xa_q_fwd_attn", epilogue=_Epilogue(
        xa_fn, [], [xkv, w["xa_q_norm"], w["xa_k_norm"]], [(512, F32), (512, BF16)], []))
    h2, fn = _matmul(xo, w["xo"], "nn", F32, "xa_o_fwd_norm", residual=h1, epilogue=_norm_fwd_epilogue(w["norm_ffn"]))
    gg = _matmul(fn, w["wg"], "nt", BF16, "ffn_gate_fwd", b_lead="p")
    uu = _matmul(fn, w["wu"], "nt", BF16, "ffn_up_fwd", b_lead="p")
    act, gathered = _conv_fwd(gg, uu, w["cw"], w["cb"], _gather_plan([late_shards[n] for n in LAST]))
    w["wd"] = gathered[0]
    def loss_fn(y, rows, consts):
        err = y - rows[0]
        part = 0.5 * jnp.sum(jnp.sum(err * err, axis=1, keepdims=True) * (1.0 / D_MODEL), axis=0, keepdims=True)
        return [err * (1.0 / D_MODEL)], [jnp.broadcast_to(part, (1, LANES))]

    dy, loss = _matmul(act, w["wd"], "nn", F32, "ffn_down_fwd_loss", residual=h2, a_lead="k", b_lead="k",
                       epilogue=_Epilogue(loss_fn, [_row(target)], [], [(D_MODEL, F32)], [(1, LANES)]))

    g["ffn_w_down"] = _matmul(act, dy, "tn", BF16, "ffn_down_dw", a_lead="p")
    dact = _matmul(dy, w["wd"], "nt", BF16, "ffn_down_dx", b_lead="p")
    duu, dgg, g["ffn_conv_w"], g["ffn_conv_b"] = _conv_bwd(gg, uu, dact, w["cw"], w["cb"])
    g["ffn_w_gate"] = _matmul(dgg, fn, "tn", BF16, "ffn_gate_dw", a_lead="p")
    g["ffn_w_up"] = _matmul(duu, fn, "tn", BF16, "ffn_up_dw", a_lead="p")
    dh2, g["norm_ffn"] = _matmul(dgg, w["wg"], "nn", F32, "ffn_dx_norm_bwd", a_lead="k", b_lead="k", more=(duu, w["wu"]),
                                 epilogue=_norm_bwd_epilogue(h2, w["norm_ffn"], dy))

    g["xa_w_o"] = _matmul(xo, dh2, "tn", BF16, "xa_o_dw")
    def xa_bwd(dxo_, rows, k):
        kvh = _heads(k[0], 2 * XA_HEADS)
        dq_, dk_, dv_ = [], [], []
        dqn, dkn = 0.0, 0.0
        for h, (a, d_) in enumerate(zip(_heads(rows[0], XA_HEADS), _heads(dxo_, XA_HEADS))):
            _, vjp = jax.vjp(_xa_head, a, kvh[h], kvh[XA_HEADS + h], k[1], k[2])
            ga, gk, gv, gqn, gkn = vjp(d_)
            dq_.append(ga)
            dk_.append(gk)
            dv_.append(gv)
            dqn, dkn = dqn + gqn, dkn + gkn
        return [_cat(dq_)], [_cat(dk_ + dv_), dqn, dkn]

    dxq, dxkv, g["xa_q_norm"], g["xa_k_norm"] = _matmul(dh2, w["xo"], "nt", F32, "xa_o_dx_attn_bwd", epilogue=_Epilogue(
        xa_bwd, [_row(xq)], [xkv, w["xa_q_norm"], w["xa_k_norm"]], [(512, BF16)], [xkv.shape, (1, 128), (1, 128)]))
    g["xa_w_q"] = _matmul(hn, dxq, "tn", BF16, "xa_q_dw")
    dh1, g["norm_xa"] = _matmul(dxq, w["xq"], "nt", F32, "xa_q_dx_norm_bwd",
                                epilogue=_norm_bwd_epilogue(h1, w["norm_xa"], dh2))
    g["xa_w_kv"] = _matmul(mn, dxkv, "tn", BF16, "xa_kv_dw")
    dmn = _matmul(dxkv, w["xkv"], "nt", F32, "xa_kv_dx")
    _, g["norm_mem"] = _norm_bwd(mem, w["norm_mem"], dmn, dmn, "norm_mem_bwd")

    g["w_out"] = _matmul(cat, dh1, "tn", BF16, "out_dw")
    def mix_bwd(dcat_, rows, k):
        do_, dog_ = [], []
        dgn = 0.0
        for o, g_, d_ in zip(_heads(rows[0], GLA_HEADS), _heads(rows[1], GLA_HEADS), _heads(dcat_, GLA_HEADS)):
            _, vjp = jax.vjp(_mix_head, o, g_, k[0])
            a, b, gn_ = vjp(d_)
            do_.append(a)
            dog_.append(b)
            dgn = dgn + gn_
        return [_cat(do_), _cat(dog_), dcat_[:, 512:]], [dgn]

    do_gla, d_og, do_mla, g["gla_out_norm"] = _matmul(dh1, w["out"], "nt", F32, "out_dx_mix_bwd", epilogue=_Epilogue(
        mix_bwd, [_row(o_gla), og], [w["gla_out_norm"]], [(512, F32), (512, BF16), (512, F32)], [(1, 128)]))

    late_parts = _late_grad_shards(g)
    dq_r, dk_r, dv_mla, lands_late = _attn_bwd(q_r, k_r, v_mla, o_mla, lse, do_mla,
                                               _scatter_plan([late_parts[n] for n in LATE]))
    lands_late = dict(zip(LATE, lands_late, strict=True))

    def qk_bwd(r, k):
        q_up, k_up = _dg(r[0], k[0], _NN), _dg(r[1], k[1], _NN)
        dqs, dks = [], []
        dkpe, dqn, dkn = 0.0, 0.0, 0.0
        for qh, kh, dqh, dkh in zip(_heads(q_up, MLA_HEADS), _heads(k_up, MLA_HEADS), _heads(r[6], MLA_HEADS),
                                    _heads(r[7], MLA_HEADS)):
            _, vjp = jax.vjp(lambda a, b, e, f, h_: _qk_head(a, b, e, r[3], r[4], r[5], f, h_), qh, kh, r[2], k[3], k[4])
            ga, gb, ge, gf, gh = vjp((dqh, dkh))
            dqs.append(ga)
            dks.append(gb)
            dkpe, dqn, dkn = dkpe + ge, dqn + gf, dkn + gh
        dq_up, dk_up, dv = _cat(dqs), _cat(dks), r[8]
        dq_lat_ = _dg(dq_up, k[0], _NT)
        dkv_lat_ = _dg(dk_up, k[1], _NT) + _dg(dv, k[2], _NT)
        return [dq_lat_, dkv_lat_, dkpe], [dqn, dkn, _dg(r[0], dq_up, _TN), _dg(r[1], dk_up, _TN), _dg(r[1], dv, _TN)]

    dq_lat, dkv_lat, d_kpe, g["q_norm"], g["k_norm"], g["uq"], g["k"], g["v"] = _rows_call(
        qk_bwd, [_row(q_lat), _row(kv_lat), kpe] + tabs + [_row(dq_r), _row(dk_r), _row(dv_mla)], qk_consts,
        [(MLA_Q_RANK, F32), (MLA_KV_RANK, F32), (128, BF16)],
        [(1, 128), (1, 128), w["uq"].shape, w["k"].shape, w["v"].shape], name="mla_qk_bwd")

    dgq, dgk, dla, dgv = _gla_bwd(proj, la, states, do_gla)

    def dproj_body(r, k):
        alr_, cq_, ckv_, dla_, dq_lat_, dkv_lat_, dgq_, dgk_, dgv_, d_og_, d_kpe_ = r
        _, gate_vjp = jax.vjp(_gate_fn, alr_, k[0], k[1])
        d_alr, gw2, gb = gate_vjp(dla_)
        _, q_vjp = jax.vjp(_rms, cq_, k[2])
        _, kv_vjp = jax.vjp(_rms, ckv_, k[3])
        d_cq, gqa = q_vjp(dq_lat_)
        d_ckv, gkva = kv_vjp(dkv_lat_)
        pieces = [dgq_, dgk_, dgv_, d_og_, d_cq, d_ckv, d_kpe_, d_alr]
        return [_cat([x_.astype(BF16) for x_ in pieces])], [gw2, gb, gqa, gkva]

    dproj, g["w2"], g["gla_gate_b"], g["mla_q_a_norm"], g["mla_kv_a_norm"] = _rows_call(
        dproj_body, [alr, cq, ckv, _row(dla), _row(dq_lat), _row(dkv_lat), _row(dgq), _row(dgk), _row(dgv), _row(d_og),
                     _row(d_kpe)], [w["w2"], w["gate_b"], w["q_a_norm"], w["kv_a_norm"]], [(P_WIDTH, BF16)],
        [(128, 256), (1, 256), (1, 256), (1, 128)], name="proj_cotangent")
    g["in"] = _matmul(dproj, xn, "tn", BF16, "proj_dw")
    dx, g["norm_mix"] = _matmul(dproj, w["in"], "nn", F32, "proj_dx_norm_bwd",
                                epilogue=_norm_bwd_epilogue(x, w["norm_mix"], dh1))
    return loss[0, 0], dx, g, lands_late


def _join_shards(pieces, axis):
    if axis == 0:
        return pieces.reshape(-1, pieces.shape[2])
    return jnp.transpose(pieces, (1, 0, 2)).reshape(pieces.shape[1], -1)


def _split_shards(full, axis):
    r, c = full.shape
    if axis == 0:
        return full.reshape(4, r // 4, c)
    return jnp.transpose(full.reshape(r, 4, c // 4), (1, 0, 2))


def _early_layout(gath, rep):
    w_in = gath["w_in"].reshape(N_WIDTH, D_MODEL)
    z = lambda n: jnp.zeros((n, D_MODEL), w_in.dtype)
    seg = lambda lo, n: w_in[lo:lo + n]
    ukv = _join_shards(gath["mla_w_ukv"], 1).reshape(MLA_KV_RANK, MLA_HEADS, MLA_NOPE + MLA_V)
    w = {
        "in": jnp.concatenate([seg(N_GQ, 256), seg(N_GK, 256), seg(N_GV, 512), seg(N_OG, 512), seg(N_CQ, 256),
                               seg(N_CKV, 128), z(64), seg(N_KPE, 32), z(32), seg(N_ALR, 16), z(112)], axis=0),
        "uq": jnp.pad(_join_shards(gath["mla_w_uq"], 1).reshape(MLA_Q_RANK, MLA_HEADS, MLA_QK),
                      ((0, 0), (0, 0), (0, LANES - MLA_QK))).reshape(MLA_Q_RANK, MLA_HEADS * LANES),
        "k": jnp.pad(ukv[:, :, :MLA_NOPE], ((0, 0), (0, 0), (0, LANES - MLA_NOPE))).reshape(MLA_KV_RANK, -1),
        "v": ukv[:, :, MLA_NOPE:].reshape(MLA_KV_RANK, MLA_HEADS * MLA_V),
        "w2": jnp.pad(_join_shards(gath["gla_gate_w2"], 1), ((0, LANES - GLA_RANK), (0, 0))),
        "cb": rep["ffn_conv_b"].reshape(4, 1, D_FF // 4),
        "q_norm": jnp.pad(rep["mla_q_norm"], ((0, 0), (0, LANES - MLA_QK))),
        "k_norm": jnp.pad(rep["mla_k_norm"], ((0, 0), (0, LANES - MLA_QK))),
        "q_a_norm": rep["mla_q_a_norm"], "kv_a_norm": rep["mla_kv_a_norm"], "gate_b": rep["gla_gate_b"],
    }
    for n in ("norm_mix", "gla_out_norm", "norm_xa", "norm_mem", "xa_q_norm", "xa_k_norm", "norm_ffn"):
        w[n] = rep[n]
    return w


def _late_layout(gath):
    return {"out": _join_shards(gath["w_out"], 0), "xq": _join_shards(gath["xa_w_q"], 0),
            "xkv": _join_shards(gath["xa_w_kv"], 0), "xo": _join_shards(gath["xa_w_o"], 1),
            "wg": gath["ffn_w_gate"], "wu": gath["ffn_w_up"], "cw": gath["ffn_conv_w"]}


def _late_grad_shards(g):
    sh = {"w_out": _split_shards(g["w_out"], 0), "xa_w_q": _split_shards(g["xa_w_q"], 0),
          "xa_w_kv": _split_shards(g["xa_w_kv"], 0), "xa_w_o": _split_shards(g["xa_w_o"], 1),
          "ffn_w_gate": g["ffn_w_gate"], "ffn_w_up": g["ffn_w_up"], "ffn_conv_w": g["ffn_conv_w"],
          "ffn_w_down": g["ffn_w_down"]}
    return {n: v.astype(BF16) for n, v in sh.items()}


def _early_grad_shards(g):
    gi = g["in"]
    seg = lambda lo, n: gi[lo:lo + n]
    w_in = jnp.concatenate([seg(P_GQ, 256), seg(P_GK, 256), seg(P_GV, 512), seg(P_ALR, 16), seg(P_OG, 512),
                            seg(P_CQ, 256), seg(P_CKV, 128), seg(P_KPE + 64, 32)], axis=0)
    uq = g["uq"].reshape(MLA_Q_RANK, MLA_HEADS, LANES)[:, :, :MLA_QK].reshape(MLA_Q_RANK, -1)
    ukv = jnp.concatenate([g["k"].reshape(MLA_KV_RANK, MLA_HEADS, LANES)[:, :, :MLA_NOPE],
                           g["v"].reshape(MLA_KV_RANK, MLA_HEADS, MLA_V)], axis=2).reshape(MLA_KV_RANK, -1)
    sh = {"w_in": w_in.reshape(4, N_WIDTH // 4, D_MODEL), "gla_gate_w2": _split_shards(g["w2"][:GLA_RANK], 1),
          "mla_w_uq": _split_shards(uq, 1), "mla_w_ukv": _split_shards(ukv, 1)}
    sh = {n: v.astype(BF16) for n, v in sh.items()}
    rep = {n: g[n] for n in REPLICATED if n in g}
    rep["mla_q_norm"] = g["q_norm"][:, :MLA_QK]
    rep["mla_k_norm"] = g["k_norm"][:, :MLA_QK]
    rep["ffn_conv_b"] = g["ffn_conv_b"].reshape(1, D_FF)
    return sh, rep


SMALL_SHAPE = (8, 1024)


def _pack_small(vectors):
    flat = jnp.concatenate(vectors, axis=1)
    return jnp.pad(flat, ((0, 0), (0, SMALL_SHAPE[0] * SMALL_SHAPE[1] - flat.shape[1]))).reshape(SMALL_SHAPE)


def _unpack_small(buf, widths):
    flat = buf.reshape(1, -1)
    out, off = [], 0
    for wd in widths:
        out.append(flat[:, off:off + wd])
        off += wd
    return out


ANY = pl.BlockSpec(memory_space=pl.ANY)


def _place():
    x, y, c = lax.axis_index("x"), lax.axis_index("y"), lax.axis_index("c")
    chips = [(1 - x, y), (x, 1 - y), (1 - x, 1 - y)]
    return x, y, c, chips


class _Comm:
    def __init__(self, ins, out_shape, sems, start, finish, mid=None):
        self.ins, self.out_shape, self.sems = list(ins), list(out_shape), list(sems)
        self.start, self.finish, self.mid = start, finish, mid or (lambda *args: None)


def _run_comm(plan, name):
    ni, no = len(plan.ins), len(plan.out_shape)

    def body(*refs):
        ins, outs, sems = refs[:ni], refs[ni:ni + no], refs[ni + no:]
        place = _place()
        plan.start(place, ins, outs, sems)
        plan.mid(place, ins, outs, sems)
        plan.finish(place, ins, outs, sems)

    return pl.pallas_call(body, in_specs=[ANY] * ni, out_specs=[ANY] * no, out_shape=plan.out_shape,
                          scratch_shapes=plan.sems, name=name)(*plan.ins)


def _gather_plan(shards):
    n = len(shards)
    by_rows = [s.shape[0] % (2 * BF16_ROWS) == 0 for s in shards]
    by_cols = [not r and s.shape[1] % (2 * LANES) == 0 for r, s in zip(by_rows, shards)]
    split = [r or c for r, c in zip(by_rows, by_cols)]

    def rows(ref, t, c):
        if by_rows[t]:
            half = shards[t].shape[0] // 2
            return ref.at[pl.ds(pl.multiple_of(c * half, BF16_ROWS), half)]
        if by_cols[t]:
            half = shards[t].shape[1] // 2
            return ref.at[:, pl.ds(pl.multiple_of(c * half, LANES), half)]
        return ref

    def remote(src, dst, ss, rs, to):
        return pltpu.make_async_remote_copy(src_ref=src, dst_ref=dst, send_sem=ss, recv_sem=rs, device_id=to,
                                            device_id_type=MESH)

    def first_wave(place, ins, outs, sems):
        x, y, c, chips = place
        ici_s, ici_r, _, _, local = sems
        me = 2 * x + y
        own = [pltpu.make_async_copy(ins[t], outs[t].at[me], local.at[t]) for t in range(n)]
        push = [remote(rows(ins[t], t, c), rows(outs[t].at[me], t, c), ici_s.at[3 * t + j], ici_r.at[3 * t + j], (px, py, c))
                for t in range(n) for j, (px, py) in enumerate(chips)]
        return own, push

    def second_wave(place, ins, outs, sems, last):
        x, y, c, chips = place
        ici_s, ici_r, d2d_s, d2d_r, local = sems
        sib = (x, y, 1 - c)
        out = []
        for t in range(n):
            for j, (px, py) in enumerate(chips):
                block = outs[t].at[2 * px + py]
                got = rows(block, t, c)
                if split[t]:
                    hand = remote(got, got, d2d_s.at[3 * t + j], d2d_r.at[3 * t + j], sib)
                    theirs = rows(block, t, 1 - c)
                    other = (remote(theirs, theirs, local.at[0], d2d_r.at[3 * t + j], sib) if last else
                             remote(got, got, local.at[0], ici_r.at[3 * t + j], sib))
                    out.append((other, hand))
                elif last:
                    out.append((remote(got, got, local.at[0], ici_r.at[3 * t + j], sib), None))
        return out

    def start(place, ins, outs, sems):
        own, push = first_wave(place, ins, outs, sems)
        for cp in own + push:
            cp.start()

    def mid(place, ins, outs, sems):
        for arrival, hand in second_wave(place, ins, outs, sems, False):
            arrival.wait_recv()
            hand.start()

    def finish(place, ins, outs, sems):
        own, push = first_wave(place, ins, outs, sems)
        for arrival, hand in second_wave(place, ins, outs, sems, True):
            arrival.wait_recv()
            if hand is not None:
                hand.wait_send()
        for cp in push:
            cp.wait_send()
        for cp in own:
            cp.wait()

    dma = pltpu.SemaphoreType.DMA
    return _Comm(shards, [jax.ShapeDtypeStruct((4,) + s.shape, s.dtype) for s in shards],
                 [dma((3 * n,)), dma((3 * n,)), dma((3 * n,)), dma((3 * n,)), dma((n,))], start, finish, mid)


def _scatter_plan(parts, small=None):
    n = len(parts)
    ns = 0 if small is None else 1

    def unpack(place, ins, outs, sems):
        x, y, c, chips = place
        return x, y, c, chips, 2 * x + y, 4 * x + 2 * y + c, (x, y, 1 - c)

    def remote(src, dst, ss, rs, to):
        return pltpu.make_async_remote_copy(src_ref=src, dst_ref=dst, send_sem=ss, recv_sem=rs, device_id=to,
                                            device_id_type=MESH)

    def first_wave(place, ins, outs, sems):
        x, y, c, chips, me, dev, sib = unpack(place, ins, outs, sems)
        ici_s, ici_r, d2d_s, d2d_r, sm_s, sm_r, local = sems
        own, push = [], []
        if ns:
            own.append(pltpu.make_async_copy(ins[n], outs[n].at[dev], local.at[n]))
            for k in range(1, 8):
                px = (1 - x) if (k >> 2) & 1 else x
                py = (1 - y) if (k >> 1) & 1 else y
                pc = (1 - c) if k & 1 else c
                push.append(remote(ins[n], outs[n].at[dev], sm_s.at[k - 1], sm_r.at[k - 1], (px, py, pc)))
        for t in range(n):
            own.append(pltpu.make_async_copy(ins[t].at[me], outs[t].at[dev], local.at[t]))
            push.append(remote(ins[t].at[me], outs[t].at[dev], d2d_s.at[4 * t], d2d_r.at[4 * t], sib))
            for j, (px, py) in enumerate(chips):
                push.append(remote(ins[t].at[2 * px + py], outs[t].at[dev], ici_s.at[3 * t + j], ici_r.at[3 * t + j],
                                   (px, py, c)))
        return own, push

    def start(place, ins, outs, sems):
        own, push = first_wave(place, ins, outs, sems)
        for cp in own + push:
            cp.start()

    def landed(dst, rs, sems, sib):
        remote(dst, dst, sems[-1].at[0], rs, sib).wait_recv()

    def forwards(place, ins, outs, sems):
        x, y, c, chips, me, dev, sib = unpack(place, ins, outs, sems)
        d2d_s, d2d_r = sems[2], sems[3]
        slots = [(t, j, outs[t].at[4 * px + 2 * py + c]) for t in range(n) for j, (px, py) in enumerate(chips)]
        return [(t, j, slot, remote(slot, slot, d2d_s.at[4 * t + 1 + j], d2d_r.at[4 * t + 1 + j], sib))
                for t, j, slot in slots]

    def mid(place, ins, outs, sems):
        sib = unpack(place, ins, outs, sems)[-1]
        for t, j, slot, cp in forwards(place, ins, outs, sems):
            landed(slot, sems[1].at[3 * t + j], sems, sib)
            cp.start()

    def finish(place, ins, outs, sems):
        x, y, c, chips, me, dev, sib = unpack(place, ins, outs, sems)
        d2d_r, sm_r = sems[3], sems[5]
        own, push = first_wave(place, ins, outs, sems)
        push += [cp for _, _, _, cp in forwards(place, ins, outs, sems)]
        for t in range(n):
            landed(outs[t].at[4 * x + 2 * y + (1 - c)], d2d_r.at[4 * t], sems, sib)
            for j, (px, py) in enumerate(chips):
                landed(outs[t].at[4 * px + 2 * py + (1 - c)], d2d_r.at[4 * t + 1 + j], sems, sib)
        if ns:
            for k in range(1, 8):
                px = (1 - x) if (k >> 2) & 1 else x
                py = (1 - y) if (k >> 1) & 1 else y
                pc = (1 - c) if k & 1 else c
                landed(outs[n].at[4 * px + 2 * py + pc], sm_r.at[k - 1], sems, sib)
        for cp in push:
            cp.wait_send()
        for cp in own:
            cp.wait()

    dma = pltpu.SemaphoreType.DMA
    ins = list(parts) + ([small] if ns else [])
    out_shape = [jax.ShapeDtypeStruct((8,) + p.shape[1:], p.dtype) for p in parts]
    if ns:
        out_shape.append(jax.ShapeDtypeStruct((8,) + small.shape, small.dtype))
    return _Comm(ins, out_shape, [dma((3 * n,)), dma((3 * n,)), dma((4 * n,)), dma((4 * n,)), dma((7,)), dma((7,)),
                                  dma((n + 1,))], start, finish, mid)


ADAM_ROWS = 288


def _row_tile(r, cap):
    if r <= cap:
        return r
    return max((t for t in range(8, cap + 1, 8) if r % t == 0), default=r)


def _adamw_update(w, m, v, land):
    g = land[0].astype(F32)
    for i in range(1, 8):
        g = g + land[i].astype(F32)
    m_new = ADAM_B1 * m + (1.0 - ADAM_B1) * g
    v_new = ADAM_B2 * v + (1.0 - ADAM_B2) * (g * g)
    m_hat = m_new / (1.0 - ADAM_B1 ** ADAM_STEP)
    v_hat = v_new / (1.0 - ADAM_B2 ** ADAM_STEP)
    return g, -ADAM_LR * (m_hat / (jnp.sqrt(v_hat) + ADAM_EPS) + ADAM_WD * w), m_new, v_new


def _adamw(tensors, name, comm=None):
    k = len(tensors)
    r, c = tensors[0][0].shape
    t = _row_tile(r, ADAM_ROWS // k)
    tc = c if t < r or r <= ADAM_ROWS else 2 * LANES
    n = (r // t) * (c // tc)
    nci, nco, nsem = (len(comm.ins), len(comm.out_shape), len(comm.sems)) if comm else (0, 0, 0)

    def kern(*refs):
        ins, cins, outs, couts, csems = _split_refs(refs, (4 * k, nci, 4 * k, nco, nsem))
        if comm:
            place = _place()

            @pl.when(pl.program_id(0) == 0)
            def _():
                comm.start(place, cins, couts, csems)

        for i in range(k):
            w_ref, m_ref, v_ref, l_ref = ins[4 * i:4 * i + 4]
            res = _adamw_update(w_ref[...], m_ref[...], v_ref[...], l_ref)
            for ref, val in zip(outs[4 * i:4 * i + 4], res, strict=True):
                ref[...] = val
        if comm:
            @pl.when(pl.program_id(0) == n - 1)
            def _():
                comm.mid(place, cins, couts, csems)
                comm.finish(place, cins, couts, csems)

    where = (lambda i: (i, 0)) if tc == c else (lambda i: (0, i))
    spec = pl.BlockSpec((t, tc), where)
    lspec = pl.BlockSpec((8, t, tc), lambda i: (0,) + where(i))
    res = pl.pallas_call(
        kern, grid=(n,), in_specs=[spec, spec, spec, lspec] * k + [ANY] * nci, out_specs=[spec] * (4 * k) + [ANY] * nco,
        out_shape=[jax.ShapeDtypeStruct((r, c), F32)] * (4 * k) + (comm.out_shape if comm else []),
        scratch_shapes=comm.sems if comm else [],
        compiler_params=pltpu.CompilerParams(dimension_semantics=("arbitrary" if comm else "parallel",),
                                             vmem_limit_bytes=VMEM_LIMIT),
        name=name)(*[x for tens in tensors for x in tens], *(comm.ins if comm else []))
    return [res[4 * i:4 * i + 4] for i in range(k)], res[4 * k:]


def _step(a):
    def sq(n):
        v = a[n][0] if a[n].ndim == 3 else a[n]
        return v.T if n.removeprefix("m_").removeprefix("v_") in TRANSPOSED else v

    payload = lambda n: sq(n) if n in EXACT_GATHER else sq(n).astype(BF16)

    loss, dx, g, lands_late = _local_step(sq("x"), sq("mem"), a["positions"][0], sq("loss_target"),
                                          {n: a[n] for n in REPLICATED}, [payload(n) for n in EARLY],
                                          {n: payload(n) for n in LATE})

    sh, rep = _early_grad_shards(g)
    small = _pack_small([rep[n] for n in REPLICATED] + [loss.reshape(1, 1)])
    *lands_early, land_small = _run_comm(_scatter_plan([sh[n] for n in EARLY], small), "scatter_last")
    quad = lambda n, land: (sq(n), sq("m_" + n), sq("v_" + n), land)
    lands = dict(zip(EARLY, lands_early, strict=True)) | lands_late

    outs = {}
    kinds = ("grad_", "delta_", "new_m_", "new_v_")
    for n, _ in SHARDED:
        res = _adamw([quad(n, lands[n])], "adamw_" + n)[0][0]
        for kind, val in zip(kinds, res, strict=True):
            outs[kind + n] = (val.T if n in TRANSPOSED else val).reshape(a[n].shape)
    zero = jnp.zeros((1, 1), F32)
    packed = [_pack_small([a[p + n] for n in REPLICATED] + [zero]) for p in ("", "m_", "v_")]
    res = _adamw([(*packed, land_small)], "adamw_replicated")[0][0]
    widths = [a[n].shape[1] for n in REPLICATED] + [1]
    for kind, buf in zip(kinds, res, strict=True):
        *vals, total = _unpack_small(buf, widths)
        for n, val in zip(REPLICATED, vals, strict=True):
            outs[kind + n] = val
        if kind == "grad_":
            loss = total[0, 0]

    ordered = [outs[kind + n] for kind in kinds for n in WEIGHTS]
    return (loss, dx[None], *ordered)


def kernel(x, mem, positions, norm_mix, w_in, gla_gate_w2, gla_gate_b, gla_out_norm, mla_q_a_norm, mla_w_uq, mla_kv_a_norm, mla_w_ukv, mla_q_norm, mla_k_norm, w_out, norm_xa, norm_mem, xa_w_q, xa_w_kv, xa_q_norm, xa_k_norm, xa_w_o, norm_ffn, ffn_w_gate, ffn_w_up, ffn_conv_w, ffn_conv_b, ffn_w_down, loss_target, m_norm_mix, m_w_in, m_gla_gate_w2, m_gla_gate_b, m_gla_out_norm, m_mla_q_a_norm, m_mla_w_uq, m_mla_kv_a_norm, m_mla_w_ukv, m_mla_q_norm, m_mla_k_norm, m_w_out, m_norm_xa, m_norm_mem, m_xa_w_q, m_xa_w_kv, m_xa_q_norm, m_xa_k_norm, m_xa_w_o, m_norm_ffn, m_ffn_w_gate, m_ffn_w_up, m_ffn_conv_w, m_ffn_conv_b, m_ffn_w_down, v_norm_mix, v_w_in, v_gla_gate_w2, v_gla_gate_b, v_gla_out_norm, v_mla_q_a_norm, v_mla_w_uq, v_mla_kv_a_norm, v_mla_w_ukv, v_mla_q_norm, v_mla_k_norm, v_w_out, v_norm_xa, v_norm_mem, v_xa_w_q, v_xa_w_kv, v_xa_q_norm, v_xa_k_norm, v_xa_w_o, v_norm_ffn, v_ffn_w_gate, v_ffn_w_up, v_ffn_conv_w, v_ffn_conv_b, v_ffn_w_down):
    return _step(dict(locals()))
```
